```python
import math
import jax, jax.numpy as jnp
from jax import lax
import numpy as np

D_MODEL = 2048
BATCH = 8
SEQ = 4096
DEPTH = 2

N_A_LAYERS = DEPTH // 2
N_B_LAYERS = DEPTH - N_A_LAYERS
GLA_HEADS = 4
GLA_KEY_DIM = D_MODEL // 2
GLA_VAL_DIM = D_MODEL
GLA_DK = GLA_KEY_DIM // GLA_HEADS
GLA_DV = GLA_VAL_DIM // GLA_HEADS
GATE_RANK = 16
GATE_NORMALIZER = 16.0
GLA_CHUNK = 64
GLA_IN_DIM = 2 * GLA_KEY_DIM + 2 * GLA_VAL_DIM + GATE_RANK
ATT_HEADS = 16
HEAD_DIM = D_MODEL // ATT_HEADS
WINDOWS = (128, 512, 2048)
DILATIONS = (1, 4, 16)
N_BRANCH = 3
ATT_BLOCK = 128
D_FF = 5632
CONV_WIDTH = 3
EPS = 1e-6

kernel_name = "yoco_gla_dilated_swa_convglu"


def rmsnorm(x, g):
    x32 = x.astype(jnp.float32)
    y = x32 * lax.rsqrt(jnp.mean(x32 * x32, axis=-1, keepdims=True) + EPS)
    return (y * g.astype(jnp.float32)).astype(x.dtype)


def alibi_slopes(n):
    def pow2_slopes(m):
        start = 2.0 ** (-8.0 / m)
        return [start ** (i + 1) for i in range(m)]
    if math.log2(n).is_integer():
        s = pow2_slopes(n)
    else:
        c = 2 ** math.floor(math.log2(n))
        s = pow2_slopes(c) + pow2_slopes(2 * c)[0::2][: n - c]
    return jnp.asarray(np.array(s, dtype=np.float32))


def gla_mixer(h, w_in, w_a2, b_a2, head_norm, w_out):
    bsz, s_len, _ = h.shape
    n_chunks = s_len // GLA_CHUNK
    f32 = jnp.float32
    proj = h @ w_in
    q, k, v, r, a = jnp.split(
        proj, [GLA_KEY_DIM, 2 * GLA_KEY_DIM, 2 * GLA_KEY_DIM + GLA_VAL_DIM,
               2 * GLA_KEY_DIM + 2 * GLA_VAL_DIM], axis=-1)
    log_alpha = jax.nn.log_sigmoid((a @ w_a2 + b_a2).astype(f32)) / GATE_NORMALIZER

    def chunks(t, hd):
        return t.astype(f32).reshape(bsz, n_chunks, GLA_CHUNK, GLA_HEADS, hd).transpose(1, 0, 3, 2, 4)

    qc = chunks(q, GLA_DK) * (GLA_DK ** -0.5)
    kc = chunks(k, GLA_DK)
    vc = chunks(v, GLA_DV)
    cum = jnp.cumsum(chunks(log_alpha, GLA_DK), axis=3)
    last = cum[:, :, :, -1:, :]
    q_dec = qc * jnp.exp(cum)
    k_inv = kc * jnp.exp(-cum)
    k_to_end = kc * jnp.exp(last - cum)

    causal = jnp.tril(jnp.ones((GLA_CHUNK, GLA_CHUNK), dtype=bool))
    scores = jnp.where(causal, jnp.einsum('nbhtk,nbhsk->nbhts', q_dec, k_inv), 0.0)
    o_intra = jnp.einsum('nbhts,nbhsv->nbhtv', scores, vc)

    def step(state, xs):
        q_n, k_n, v_n, dec_n = xs
        o_n = jnp.einsum('bhtk,bhkv->bhtv', q_n, state)
        state = state * dec_n[..., None] + jnp.einsum('bhsk,bhsv->bhkv', k_n, v_n)
        return state, o_n

    state0 = jnp.zeros((bsz, GLA_HEADS, GLA_DK, GLA_DV), f32)
    _, o_inter = lax.scan(step, state0, (q_dec, k_to_end, vc, jnp.exp(last[:, :, :, 0, :])))
    o = (o_intra + o_inter).transpose(1, 0, 3, 2, 4).reshape(bsz, s_len, GLA_HEADS, GLA_DV)
    o = rmsnorm(o, head_norm)
    gate = jax.nn.silu(r.astype(f32)).reshape(bsz, s_len, GLA_HEADS, GLA_DV)
    o = (o * gate).reshape(bsz, s_len, GLA_VAL_DIM).astype(h.dtype)
    return o @ w_out


def to_dilated(t, d):
    bsz, s_len, nh, e = t.shape
    return t.reshape(bsz, s_len // d, d, nh, e).transpose(0, 2, 1, 3, 4)


def n_blocks(sub_len):
    return -(-sub_len // ATT_BLOCK)


def to_query_blocks(t, d):
    td = to_dilated(t, d)
    bsz, _, sub_len, nh, e = td.shape
    nb = n_blocks(sub_len)
    td = jnp.pad(td, ((0, 0), (0, 0), (0, nb * ATT_BLOCK - sub_len), (0, 0), (0, 0)))
    return td.reshape(bsz, d, nb, ATT_BLOCK, nh, e)


def to_key_blocks(t, d):
    td = to_dilated(t, d)
    bsz, _, sub_len, nh, e = td.shape
    nb = n_blocks(sub_len)
    td = jnp.pad(td, ((0, 0), (0, 0), (ATT_BLOCK, nb * ATT_BLOCK - sub_len), (0, 0), (0, 0)))
    return td.reshape(bsz, d, nb + 1, ATT_BLOCK, nh, e)


def from_blocks(t, d, s_len):
    bsz, _, nb, _, nh, e = t.shape
    t = t.reshape(bsz, d, nb * ATT_BLOCK, nh, e)[:, :, : s_len // d]
    return t.transpose(0, 2, 1, 3, 4).reshape(bsz, s_len, nh, e)


def shared_kv(h, kv_norm, w_kv):
    bsz, s_len, _ = h.shape
    kv = rmsnorm(h, kv_norm) @ w_kv
    k, v = jnp.split(kv, 2, axis=-1)
    k = k.reshape(bsz, s_len, ATT_HEADS, HEAD_DIM)
    v = v.reshape(bsz, s_len, ATT_HEADS, HEAD_DIM)
    return [(to_key_blocks(k, d), to_key_blocks(v, d)) for d in DILATIONS]


def dilated_branch(qb, kb, vb, d, keys_back, slopes):
    nb = qb.shape[2]
    s_prev = jnp.einsum('brnqhe,brnkhe->brnhqk', qb, kb[:, :, :-1])
    s_cur = jnp.einsum('brnqhe,brnkhe->brnhqk', qb, kb[:, :, 1:])
    s = jnp.concatenate([s_prev, s_cur], axis=-1).astype(jnp.float32) * (HEAD_DIM ** -0.5)
    qa = jnp.arange(ATT_BLOCK)
    kc = jnp.arange(2 * ATT_BLOCK)
    j = qa[:, None] - kc[None, :] + ATT_BLOCK
    key_sub = jnp.arange(nb)[:, None] * ATT_BLOCK - ATT_BLOCK + kc[None, :]
    valid = ((j >= 0) & (j <= keys_back))[None] & (key_sub >= 0)[:, None, :]
    alibi = -slopes[:, None, None] * (j * d).astype(jnp.float32)[None]
    s = jnp.where(valid[None, None, :, None], s + alibi[None, None, None], -jnp.inf)
    m = jnp.max(s, axis=-1, keepdims=True)
    p = jnp.exp(s - m)
    l = jnp.sum(p, axis=-1, keepdims=True)
    o = (jnp.einsum('brnhqk,brnkhe->brnqhe', p[..., :ATT_BLOCK], vb[:, :, :-1])
         + jnp.einsum('brnhqk,brnkhe->brnqhe', p[..., ATT_BLOCK:], vb[:, :, 1:]))
    o = o / l.transpose(0, 1, 2, 4, 3, 5)
    lse = (m + jnp.log(l)).transpose(0, 1, 2, 4, 3, 5)
    return o, lse


def dilated_mixer(h, kv_blocks, w_q, w_out):
    bsz, s_len, _ = h.shape
    q = (h @ w_q).reshape(bsz, s_len, N_BRANCH, ATT_HEADS, HEAD_DIM)
    slopes = alibi_slopes(ATT_HEADS)
    outs, lses = [], []
    for g in range(N_BRANCH):
        d = DILATIONS[g]
        kb, vb = kv_blocks[g]
        o, lse = dilated_branch(to_query_blocks(q[:, :, g], d), kb, vb, d, WINDOWS[g] // d, slopes)
        outs.append(from_blocks(o, d, s_len))
        lses.append(from_blocks(lse, d, s_len))
    w = jax.nn.softmax(jnp.stack(lses, axis=0), axis=0)
    o = jnp.sum(w * jnp.stack(outs, axis=0), axis=0)
    return o.reshape(bsz, s_len, ATT_HEADS * HEAD_DIM).astype(h.dtype) @ w_out


def conv_glu(h, w_up, conv_w, conv_b, w_down):
    u, g = jnp.split(h @ w_up, 2, axis=-1)
    gp = jnp.pad(g, ((0, 0), (CONV_WIDTH - 1, 0), (0, 0)))
    g = conv_w[0] * gp[:, :-2] + conv_w[1] * gp[:, 1:-1] + conv_w[2] * gp[:, 2:] + conv_b
    return (jax.nn.gelu(g, approximate=False) * u) @ w_down


def _fwd_setup_inputs(seed: int = 0) -> dict:
    key = jax.random.key(seed)
    ks = jax.random.split(key, 17)
    f32 = jnp.float32

    def nrm(k, shape, fan_in):
        return jax.random.normal(k, shape, f32) * (fan_in ** -0.5)

    def gain(k, shape):
        return 1.0 + 0.02 * jax.random.normal(k, shape, f32)

    return {
        "x": jax.random.normal(ks[0], (BATCH, SEQ, D_MODEL), f32),
        "attn_norm": gain(ks[1], (DEPTH, D_MODEL)),
        "gla_w_in": nrm(ks[2], (N_A_LAYERS, D_MODEL, GLA_IN_DIM), D_MODEL),
        "gla_w_a2": nrm(ks[3], (N_A_LAYERS, GATE_RANK, GLA_KEY_DIM), GATE_RANK),
        "gla_b_a2": 0.1 * jax.random.normal(ks[4], (N_A_LAYERS, GLA_KEY_DIM), f32),
        "gla_head_norm": gain(ks[5], (N_A_LAYERS, GLA_DV)),
        "gla_w_out": nrm(ks[6], (N_A_LAYERS, GLA_VAL_DIM, D_MODEL), GLA_VAL_DIM),
        "kv_norm": gain(ks[7], (D_MODEL,)),
        "w_kv": nrm(ks[8], (D_MODEL, 2 * ATT_HEADS * HEAD_DIM), D_MODEL),
        "dsa_w_q": nrm(ks[9], (N_B_LAYERS, D_MODEL, N_BRANCH * ATT_HEADS * HEAD_DIM), D_MODEL),
        "dsa_w_out": nrm(ks[10], (N_B_LAYERS, ATT_HEADS * HEAD_DIM, D_MODEL), ATT_HEADS * HEAD_DIM),
        "ffn_norm": gain(ks[11], (DEPTH, D_MODEL)),
        "ffn_w_up": nrm(ks[12], (DEPTH, D_MODEL, 2 * D_FF), D_MODEL),
        "ffn_conv_w": nrm(ks[13], (DEPTH, CONV_WIDTH, D_FF), CONV_WIDTH),
        "ffn_conv_b": 0.02 * jax.random.normal(ks[14], (DEPTH, D_FF), f32),
        "ffn_w_down": nrm(ks[15], (DEPTH, D_FF, D_MODEL), D_FF),
        "final_norm": gain(ks[16], (D_MODEL,)),
    }


def _fwd_reference(x, attn_norm, gla_w_in, gla_w_a2, gla_b_a2, gla_head_norm, gla_w_out,
              kv_norm, w_kv, dsa_w_q, dsa_w_out, ffn_norm, ffn_w_up, ffn_conv_w,
              ffn_conv_b, ffn_w_down, final_norm):
    h = x
    kv_blocks = None
    for i in range(DEPTH):
        if i < N_A_LAYERS:
            h = h + gla_mixer(rmsnorm(h, attn_norm[i]), gla_w_in[i], gla_w_a2[i], gla_b_a2[i],
                              gla_head_norm[i], gla_w_out[i])
        else:
            if i == N_A_LAYERS:
                kv_blocks = shared_kv(h, kv_norm, w_kv)
            j = i - N_A_LAYERS
            h = h + dilated_mixer(rmsnorm(h, attn_norm[i]), kv_blocks, dsa_w_q[j], dsa_w_out[j])
        h = h + conv_glu(rmsnorm(h, ffn_norm[i]), ffn_w_up[i], ffn_conv_w[i], ffn_conv_b[i],
                         ffn_w_down[i])
    return rmsnorm(h, final_norm)


import jax as _jax
import jax.numpy as _jnp

TWIN_FORMAT = 'train_step'
FWD_PARAMS = ['x', 'attn_norm', 'gla_w_in', 'gla_w_a2', 'gla_b_a2', 'gla_head_norm', 'gla_w_out', 'kv_norm', 'w_kv', 'dsa_w_q', 'dsa_w_out', 'ffn_norm', 'ffn_w_up', 'ffn_conv_w', 'ffn_conv_b', 'ffn_w_down', 'final_norm']
TWIN_WEIGHTS = ['attn_norm', 'gla_w_in', 'gla_w_a2', 'gla_b_a2', 'gla_head_norm', 'gla_w_out', 'kv_norm', 'w_kv', 'dsa_w_q', 'dsa_w_out', 'ffn_norm', 'ffn_w_up', 'ffn_conv_w', 'ffn_conv_b', 'ffn_w_down', 'final_norm']
TWIN_DIFF_INPUT = 'x'
TWIN_INPUTS = ['x', 'attn_norm', 'gla_w_in', 'gla_w_a2', 'gla_b_a2', 'gla_head_norm', 'gla_w_out', 'kv_norm', 'w_kv', 'dsa_w_q', 'dsa_w_out', 'ffn_norm', 'ffn_w_up', 'ffn_conv_w', 'ffn_conv_b', 'ffn_w_down', 'final_norm', 'loss_target', 'm_attn_norm', 'm_gla_w_in', 'm_gla_w_a2', 'm_gla_b_a2', 'm_gla_head_norm', 'm_gla_w_out', 'm_kv_norm', 'm_w_kv', 'm_dsa_w_q', 'm_dsa_w_out', 'm_ffn_norm', 'm_ffn_w_up', 'm_ffn_conv_w', 'm_ffn_conv_b', 'm_ffn_w_down', 'm_final_norm', 'v_attn_norm', 'v_gla_w_in', 'v_gla_w_a2', 'v_gla_b_a2', 'v_gla_head_norm', 'v_gla_w_out', 'v_kv_norm', 'v_w_kv', 'v_dsa_w_q', 'v_dsa_w_out', 'v_ffn_norm', 'v_ffn_w_up', 'v_ffn_conv_w', 'v_ffn_conv_b', 'v_ffn_w_down', 'v_final_norm']
TWIN_OUTPUTS = ['loss', 'grad_x', 'grad_attn_norm', 'grad_gla_w_in', 'grad_gla_w_a2', 'grad_gla_b_a2', 'grad_gla_head_norm', 'grad_gla_w_out', 'grad_kv_norm', 'grad_w_kv', 'grad_dsa_w_q', 'grad_dsa_w_out', 'grad_ffn_norm', 'grad_ffn_w_up', 'grad_ffn_conv_w', 'grad_ffn_conv_b', 'grad_ffn_w_down', 'grad_final_norm', 'delta_attn_norm', 'delta_gla_w_in', 'delta_gla_w_a2', 'delta_gla_b_a2', 'delta_gla_head_norm', 'delta_gla_w_out', 'delta_kv_norm', 'delta_w_kv', 'delta_dsa_w_q', 'delta_dsa_w_out', 'delta_ffn_norm', 'delta_ffn_w_up', 'delta_ffn_conv_w', 'delta_ffn_conv_b', 'delta_ffn_w_down', 'delta_final_norm', 'new_m_attn_norm', 'new_m_gla_w_in', 'new_m_gla_w_a2', 'new_m_gla_b_a2', 'new_m_gla_head_norm', 'new_m_gla_w_out', 'new_m_kv_norm', 'new_m_w_kv', 'new_m_dsa_w_q', 'new_m_dsa_w_out', 'new_m_ffn_norm', 'new_m_ffn_w_up', 'new_m_ffn_conv_w', 'new_m_ffn_conv_b', 'new_m_ffn_w_down', 'new_m_final_norm', 'new_v_attn_norm', 'new_v_gla_w_in', 'new_v_gla_w_a2', 'new_v_gla_b_a2', 'new_v_gla_head_norm', 'new_v_gla_w_out', 'new_v_kv_norm', 'new_v_w_kv', 'new_v_dsa_w_q', 'new_v_dsa_w_out', 'new_v_ffn_norm', 'new_v_ffn_w_up', 'new_v_ffn_conv_w', 'new_v_ffn_conv_b', 'new_v_ffn_w_down', 'new_v_final_norm']
TWIN_LEAF_KINDS = {'loss': 'loss', 'grad_x': 'grad_x', 'grad_attn_norm': 'grad_w', 'grad_gla_w_in': 'grad_w', 'grad_gla_w_a2': 'grad_w', 'grad_gla_b_a2': 'grad_w', 'grad_gla_head_norm': 'grad_w', 'grad_gla_w_out': 'grad_w', 'grad_kv_norm': 'grad_w', 'grad_w_kv': 'grad_w', 'grad_dsa_w_q': 'grad_w', 'grad_dsa_w_out': 'grad_w', 'grad_ffn_norm': 'grad_w', 'grad_ffn_w_up': 'grad_w', 'grad_ffn_conv_w': 'grad_w', 'grad_ffn_conv_b': 'grad_w', 'grad_ffn_w_down': 'grad_w', 'grad_final_norm': 'grad_w', 'delta_attn_norm': 'delta_w', 'delta_gla_w_in': 'delta_w', 'delta_gla_w_a2': 'delta_w', 'delta_gla_b_a2': 'delta_w', 'delta_gla_head_norm': 'delta_w', 'delta_gla_w_out': 'delta_w', 'delta_kv_norm': 'delta_w', 'delta_w_kv': 'delta_w', 'delta_dsa_w_q': 'delta_w', 'delta_dsa_w_out': 'delta_w', 'delta_ffn_norm': 'delta_w', 'delta_ffn_w_up': 'delta_w', 'delta_ffn_conv_w': 'delta_w', 'delta_ffn_conv_b': 'delta_w', 'delta_ffn_w_down': 'delta_w', 'delta_final_norm': 'delta_w', 'new_m_attn_norm': 'new_m', 'new_m_gla_w_in': 'new_m', 'new_m_gla_w_a2': 'new_m', 'new_m_gla_b_a2': 'new_m', 'new_m_gla_head_norm': 'new_m', 'new_m_gla_w_out': 'new_m', 'new_m_kv_norm': 'new_m', 'new_m_w_kv': 'new_m', 'new_m_dsa_w_q': 'new_m', 'new_m_dsa_w_out': 'new_m', 'new_m_ffn_norm': 'new_m', 'new_m_ffn_w_up': 'new_m', 'new_m_ffn_conv_w': 'new_m', 'new_m_ffn_conv_b': 'new_m', 'new_m_ffn_w_down': 'new_m', 'new_m_final_norm': 'new_m', 'new_v_attn_norm': 'new_v', 'new_v_gla_w_in': 'new_v', 'new_v_gla_w_a2': 'new_v', 'new_v_gla_b_a2': 'new_v', 'new_v_gla_head_norm': 'new_v', 'new_v_gla_w_out': 'new_v', 'new_v_kv_norm': 'new_v', 'new_v_w_kv': 'new_v', 'new_v_dsa_w_q': 'new_v', 'new_v_dsa_w_out': 'new_v', 'new_v_ffn_norm': 'new_v', 'new_v_ffn_w_up': 'new_v', 'new_v_ffn_conv_w': 'new_v', 'new_v_ffn_conv_b': 'new_v', 'new_v_ffn_w_down': 'new_v', 'new_v_final_norm': 'new_v'}


def _forward(args):
    return _fwd_reference(*[args[k] for k in FWD_PARAMS])


def _output_shape():
    def fwd():
        inp = _fwd_setup_inputs(0)
        return _fwd_reference(*[inp[k] for k in FWD_PARAMS])
    out = _jax.eval_shape(fwd)
    return out.shape, out.dtype

N_MICROBATCH = 1
ADAM_LR = 0.001
ADAM_B1 = 0.9
ADAM_B2 = 0.999
ADAM_EPS = 1e-08
ADAM_WD = 0.01
ADAM_STEP = 10
PER_EXAMPLE_BATCH_AXIS = {'x': 0, 'loss_target': 0}
SHARED_INPUTS = []
_WEIGHT_DTYPES = {'attn_norm': _jnp.float32, 'gla_w_in': _jnp.float32, 'gla_w_a2': _jnp.float32, 'gla_b_a2': _jnp.float32, 'gla_head_norm': _jnp.float32, 'gla_w_out': _jnp.float32, 'kv_norm': _jnp.float32, 'w_kv': _jnp.float32, 'dsa_w_q': _jnp.float32, 'dsa_w_out': _jnp.float32, 'ffn_norm': _jnp.float32, 'ffn_w_up': _jnp.float32, 'ffn_conv_w': _jnp.float32, 'ffn_conv_b': _jnp.float32, 'ffn_w_down': _jnp.float32, 'final_norm': _jnp.float32}
MOMENT_SCALE = {'attn_norm': 8.430758e-02, 'gla_w_in': 6.756085e-02, 'gla_w_a2': 9.503014e-03, 'gla_b_a2': 3.881104e-02, 'gla_head_norm': 1.177283e-01, 'gla_w_out': 5.713720e-02, 'kv_norm': 3.263685e-02, 'w_kv': 2.251578e-02, 'dsa_w_q': 8.743720e-03, 'dsa_w_out': 2.794643e-02, 'ffn_norm': 6.369890e-02, 'ffn_w_up': 2.701196e-02, 'ffn_conv_w': 2.718953e-02, 'ffn_conv_b': 2.621828e-02, 'ffn_w_down': 4.413775e-02, 'final_norm': 1.599446e+01}


def _to_microbatches(a, axis):
    t = _jnp.moveaxis(a, axis, 0)
    t = t.reshape((N_MICROBATCH, t.shape[0] // N_MICROBATCH) + t.shape[1:])
    return _jnp.moveaxis(t, 1, axis + 1)


def setup_inputs(seed: int = 0) -> dict:
    inp = _fwd_setup_inputs(seed)
    key = _jax.random.fold_in(_jax.random.key(seed), 7919)
    shape, _ = _output_shape()
    out = dict(inp)
    out["loss_target"] = _jax.random.normal(_jax.random.fold_in(key, 0), shape, _jnp.float32)
    for i, name in enumerate(TWIN_WEIGHTS):
        w = inp[name].astype(_jnp.float32)
        if MOMENT_SCALE is None:
            s = _jnp.sqrt(_jnp.mean(_jnp.square(w)) + 1e-30)
        else:
            s = MOMENT_SCALE[name]
        km, kv = _jax.random.split(_jax.random.fold_in(key, i + 1))
        out[name] = w
        out["m_" + name] = s * _jax.random.normal(km, w.shape, _jnp.float32)
        out["v_" + name] = (s * s) * _jax.random.uniform(kv, w.shape, _jnp.float32, 0.5, 1.5)
    if N_MICROBATCH > 1:
        for name, axis in PER_EXAMPLE_BATCH_AXIS.items():
            out[name] = _to_microbatches(out[name], axis)
    return {'x': out['x'], 'attn_norm': out['attn_norm'], 'gla_w_in': out['gla_w_in'], 'gla_w_a2': out['gla_w_a2'], 'gla_b_a2': out['gla_b_a2'], 'gla_head_norm': out['gla_head_norm'], 'gla_w_out': out['gla_w_out'], 'kv_norm': out['kv_norm'], 'w_kv': out['w_kv'], 'dsa_w_q': out['dsa_w_q'], 'dsa_w_out': out['dsa_w_out'], 'ffn_norm': out['ffn_norm'], 'ffn_w_up': out['ffn_w_up'], 'ffn_conv_w': out['ffn_conv_w'], 'ffn_conv_b': out['ffn_conv_b'], 'ffn_w_down': out['ffn_w_down'], 'final_norm': out['final_norm'], 'loss_target': out['loss_target'], 'm_attn_norm': out['m_attn_norm'], 'm_gla_w_in': out['m_gla_w_in'], 'm_gla_w_a2': out['m_gla_w_a2'], 'm_gla_b_a2': out['m_gla_b_a2'], 'm_gla_head_norm': out['m_gla_head_norm'], 'm_gla_w_out': out['m_gla_w_out'], 'm_kv_norm': out['m_kv_norm'], 'm_w_kv': out['m_w_kv'], 'm_dsa_w_q': out['m_dsa_w_q'], 'm_dsa_w_out': out['m_dsa_w_out'], 'm_ffn_norm': out['m_ffn_norm'], 'm_ffn_w_up': out['m_ffn_w_up'], 'm_ffn_conv_w': out['m_ffn_conv_w'], 'm_ffn_conv_b': out['m_ffn_conv_b'], 'm_ffn_w_down': out['m_ffn_w_down'], 'm_final_norm': out['m_final_norm'], 'v_attn_norm': out['v_attn_norm'], 'v_gla_w_in': out['v_gla_w_in'], 'v_gla_w_a2': out['v_gla_w_a2'], 'v_gla_b_a2': out['v_gla_b_a2'], 'v_gla_head_norm': out['v_gla_head_norm'], 'v_gla_w_out': out['v_gla_w_out'], 'v_kv_norm': out['v_kv_norm'], 'v_w_kv': out['v_w_kv'], 'v_dsa_w_q': out['v_dsa_w_q'], 'v_dsa_w_out': out['v_dsa_w_out'], 'v_ffn_norm': out['v_ffn_norm'], 'v_ffn_w_up': out['v_ffn_w_up'], 'v_ffn_conv_w': out['v_ffn_conv_w'], 'v_ffn_conv_b': out['v_ffn_conv_b'], 'v_ffn_w_down': out['v_ffn_w_down'], 'v_final_norm': out['v_final_norm']}


def _loss(weights, diff, rest, loss_target):
    with _jax.named_scope("forward"):
        args = {**rest, TWIN_DIFF_INPUT: diff, **{k: w.astype(_WEIGHT_DTYPES[k]) for k, w in weights.items()}}
        y = _forward(args)
    with _jax.named_scope("loss_head"):
        err = _jnp.square(y.astype(_jnp.float32) - loss_target)
        return 0.5 * _jnp.sum(_jnp.mean(err, axis=-1)) if err.ndim else 0.5 * err


def _adamw(w, g, m, v):
    m = ADAM_B1 * m + (1.0 - ADAM_B1) * g
    v = ADAM_B2 * v + (1.0 - ADAM_B2) * _jnp.square(g)
    m_hat = m / (1.0 - ADAM_B1 ** ADAM_STEP)
    v_hat = v / (1.0 - ADAM_B2 ** ADAM_STEP)
    delta = -ADAM_LR * (m_hat / (_jnp.sqrt(v_hat) + ADAM_EPS) + ADAM_WD * w)
    return delta, m, v


def reference(x, attn_norm, gla_w_in, gla_w_a2, gla_b_a2, gla_head_norm, gla_w_out, kv_norm, w_kv, dsa_w_q, dsa_w_out, ffn_norm, ffn_w_up, ffn_conv_w, ffn_conv_b, ffn_w_down, final_norm, loss_target, m_attn_norm, m_gla_w_in, m_gla_w_a2, m_gla_b_a2, m_gla_head_norm, m_gla_w_out, m_kv_norm, m_w_kv, m_dsa_w_q, m_dsa_w_out, m_ffn_norm, m_ffn_w_up, m_ffn_conv_w, m_ffn_conv_b, m_ffn_w_down, m_final_norm, v_attn_norm, v_gla_w_in, v_gla_w_a2, v_gla_b_a2, v_gla_head_norm, v_gla_w_out, v_kv_norm, v_w_kv, v_dsa_w_q, v_dsa_w_out, v_ffn_norm, v_ffn_w_up, v_ffn_conv_w, v_ffn_conv_b, v_ffn_w_down, v_final_norm):
    given = dict(x=x, attn_norm=attn_norm, gla_w_in=gla_w_in, gla_w_a2=gla_w_a2, gla_b_a2=gla_b_a2, gla_head_norm=gla_head_norm, gla_w_out=gla_w_out, kv_norm=kv_norm, w_kv=w_kv, dsa_w_q=dsa_w_q, dsa_w_out=dsa_w_out, ffn_norm=ffn_norm, ffn_w_up=ffn_w_up, ffn_conv_w=ffn_conv_w, ffn_conv_b=ffn_conv_b, ffn_w_down=ffn_w_down, final_norm=final_norm, loss_target=loss_target, m_attn_norm=m_attn_norm, m_gla_w_in=m_gla_w_in, m_gla_w_a2=m_gla_w_a2, m_gla_b_a2=m_gla_b_a2, m_gla_head_norm=m_gla_head_norm, m_gla_w_out=m_gla_w_out, m_kv_norm=m_kv_norm, m_w_kv=m_w_kv, m_dsa_w_q=m_dsa_w_q, m_dsa_w_out=m_dsa_w_out, m_ffn_norm=m_ffn_norm, m_ffn_w_up=m_ffn_w_up, m_ffn_conv_w=m_ffn_conv_w, m_ffn_conv_b=m_ffn_conv_b, m_ffn_w_down=m_ffn_w_down, m_final_norm=m_final_norm, v_attn_norm=v_attn_norm, v_gla_w_in=v_gla_w_in, v_gla_w_a2=v_gla_w_a2, v_gla_b_a2=v_gla_b_a2, v_gla_head_norm=v_gla_head_norm, v_gla_w_out=v_gla_w_out, v_kv_norm=v_kv_norm, v_w_kv=v_w_kv, v_dsa_w_q=v_dsa_w_q, v_dsa_w_out=v_dsa_w_out, v_ffn_norm=v_ffn_norm, v_ffn_w_up=v_ffn_w_up, v_ffn_conv_w=v_ffn_conv_w, v_ffn_conv_b=v_ffn_conv_b, v_ffn_w_down=v_ffn_w_down, v_final_norm=v_final_norm)
    weights = {n: given[n] for n in TWIN_WEIGHTS}
    shared = {n: given[n] for n in SHARED_INPUTS}
    per_example = {n: given[n] for n in ['x']}
    grad_fn = _jax.value_and_grad(_loss, argnums=(0, 1))

    def one_microbatch(ex, loss_target):
        ex = dict(ex)
        diff = ex.pop(TWIN_DIFF_INPUT)
        return grad_fn(weights, diff, {**shared, **ex}, loss_target)

    if N_MICROBATCH == 1:
        loss, (grad_w, grad_x) = one_microbatch(per_example, given["loss_target"])
    else:
        def body(carry, xs):
            loss_sum, grad_sum = carry
            l_k, (gw_k, gx_k) = one_microbatch(xs[0], xs[1])
            with _jax.named_scope("update"):
                return (loss_sum + l_k, _jax.tree.map(_jnp.add, grad_sum, gw_k)), gx_k

        init = (_jnp.zeros((), _jnp.float32), _jax.tree.map(_jnp.zeros_like, weights))
        (loss, grad_w), grad_x = _jax.lax.scan(body, init, (per_example, given["loss_target"]))
    with _jax.named_scope("update"):
        delta_w, new_m, new_v = {}, {}, {}
        for n in TWIN_WEIGHTS:
            delta_w[n], new_m[n], new_v[n] = _adamw(weights[n], grad_w[n], given["m_" + n], given["v_" + n])
    return (loss, grad_x, *[grad_w[n] for n in TWIN_WEIGHTS], *[delta_w[n] for n in TWIN_WEIGHTS],
            *[new_m[n] for n in TWIN_WEIGHTS], *[new_v[n] for n in TWIN_WEIGHTS])
```

```python
import math

import jax
import jax.numpy as jnp
from jax import lax
from jax.experimental import pallas as pl
from jax.experimental.pallas import tpu as pltpu

F32 = jnp.float32
BF16 = jnp.bfloat16

D_MODEL = 2048
SEQ = 4096
GLA_HEADS = 4
GLA_KEY_DIM = D_MODEL // 2
GLA_VAL_DIM = D_MODEL
GATE_RANK = 16
GATE_NORMALIZER = 16.0
GLA_CHUNK = 64
ATT_HEADS = 16
HEAD_DIM = 128
WINDOWS = (128, 512, 2048)
DILATIONS = (1, 4, 16)
ATT_BLOCK = 128
D_FF = 5632
EPS = 1e-6
ADAM_LR = 0.001
ADAM_B1 = 0.9
ADAM_B2 = 0.999
ADAM_EPS = 1e-08
ADAM_WD = 0.01
ADAM_STEP = 10

N_CHIPS = 4
N_DEV = 8
LANE = 128
A_PAD = 128
VMEM_LIMIT = 56 * 1024 * 1024
NEG = -1e30
MESH = pl.DeviceIdType.MESH

NN = (((1,), (0,)), ((), ()))
NT = (((1,), (1,)), ((), ()))
TN = (((0,), (0,)), ((), ()))


def _tile(n, cands):
    for c in cands:
        if c <= n and n % c == 0:
            return c
    return n


def _roundup(n, m):
    return -(-n // m) * m


def _params(n_axes):
    return pltpu.CompilerParams(dimension_semantics=("arbitrary",) * n_axes, vmem_limit_bytes=VMEM_LIMIT)


def _dot(a, b, dims):
    return lax.dot_general(a, b, dims, preferred_element_type=F32)


def _sigmoid(x):
    return 1.0 / (1.0 + jnp.exp(-x))


def _layout():
    d, f = D_MODEL, D_FF
    hd = ATT_HEADS * HEAD_DIM
    gin = 2 * GLA_KEY_DIM + 2 * GLA_VAL_DIM + GATE_RANK
    lay = {}
    up_w = 2 * f // N_CHIPS
    up_t = _tile(up_w, (1408, 1024, 512, 256, 128))
    q_w = 3 * hd // N_CHIPS
    q_t = _tile(q_w, (512, 384, 256, 128))
    kv_w = 2 * hd // N_CHIPS
    kv_t = _tile(kv_w, (1024, 512, 256, 128))
    gin_w = gin // N_CHIPS
    cur = 0
    for name, w, t in (("up0", up_w, up_t), ("up1", up_w, up_t), ("wq", q_w, q_t), ("wkv", kv_w, kv_t),
                       ("gin", gin_w, LANE)):
        off = _roundup(cur, t)
        lay[name] = (off, w, t)
        cur = off + w
    lay["wc_cols"] = _roundup(cur, LANE)
    dn_r = f // N_CHIPS
    dn_t = _tile(dn_r, (1408, 1024, 512, 256, 128))
    go_r = GLA_VAL_DIM // N_CHIPS
    go_t = _tile(go_r, (512, 256, 128))
    do_r = hd // N_CHIPS
    do_t = _tile(do_r, (512, 256, 128))
    cur = 0
    for name, r, t in (("down0", dn_r, dn_t), ("down1", dn_r, dn_t), ("gout", go_r, go_t), ("dout", do_r, do_t)):
        off = _roundup(cur, t)
        lay[name] = (off, r, t)
        cur = off + r
    lay["wr_rows"] = _roundup(cur, 32)
    return lay


def _matmul(name, a, b, dims, grid, a_spec, b_spec, o_spec, out_shape, acc_shape, add=None, add_spec=None):
    nk = grid[2]
    has_add = add is not None

    def body(*refs):
        a_ref, b_ref = refs[0], refs[1]
        pos = 2
        add_ref = None
        if has_add:
            add_ref = refs[pos]
            pos += 1
        o_ref = refs[pos]
        prod = _dot(a_ref[...].astype(BF16), b_ref[...].astype(BF16), dims)

        def finish(val):
            if has_add:
                val = val + add_ref[...].astype(F32)
            o_ref[...] = val.astype(o_ref.dtype)

        if nk == 1:
            finish(prod)
        else:
            acc_ref = refs[pos + 1]
            k = pl.program_id(2)

            @pl.when(k == 0)
            def _():
                acc_ref[...] = prod

            @pl.when(k > 0)
            def _():
                acc_ref[...] += prod

            @pl.when(k == nk - 1)
            def _():
                finish(acc_ref[...])

    in_specs = [a_spec, b_spec]
    args = [a, b]
    if has_add:
        in_specs.append(add_spec)
        args.append(add)
    scratch = [] if nk == 1 else [pltpu.VMEM(acc_shape, F32)]
    return pl.pallas_call(body, name=name, grid=grid, in_specs=in_specs, out_specs=o_spec, out_shape=out_shape,
                          scratch_shapes=scratch, compiler_params=_params(3))(*args)


def _mm_act_wc(name, a, wc, seg, out_dtype):
    off, w, tn = seg
    t_len, d = a.shape
    tm = _tile(t_len, (1024, 512, 256, 128))
    nps = w // tn
    ob = off // tn
    grid = (t_len // tm, N_CHIPS * nps, 1)
    return _matmul(
        name, a, wc, NN, grid,
        pl.BlockSpec((tm, d), lambda i, j, k: (i, 0)),
        pl.BlockSpec((None, d, tn), lambda i, j, k: (j // nps, 0, ob + j % nps)),
        pl.BlockSpec((tm, tn), lambda i, j, k: (i, j)),
        jax.ShapeDtypeStruct((t_len, N_CHIPS * w), out_dtype), (tm, tn))


def _mm_dact_wcT(name, dy, wc, seg, add=None):
    off, w, tk = seg
    t_len = dy.shape[0]
    d = wc.shape[1]
    tm = _tile(t_len, (1024, 512, 256, 128))
    tn = _tile(d, (1024, 512, 256, 128))
    kps = w // tk
    ob = off // tk
    grid = (t_len // tm, d // tn, N_CHIPS * kps)
    return _matmul(
        name, dy, wc, NT, grid,
        pl.BlockSpec((tm, tk), lambda i, j, k: (i, k)),
        pl.BlockSpec((None, tn, tk), lambda i, j, k: (k // kps, j, ob + k % kps)),
        pl.BlockSpec((tm, tn), lambda i, j, k: (i, j)),
        jax.ShapeDtypeStruct((t_len, d), F32), (tm, tn),
        add=add, add_spec=None if add is None else pl.BlockSpec((tm, tn), lambda i, j, k: (i, j)))


def _mm_grad_wc(name, a, dy, seg):
    _, w, tn = seg
    t_len, d = a.shape
    tm = _tile(d, (1024, 512, 256, 128))
    tk = _tile(t_len, (1024, 512, 256, 128))
    nps = w // tn
    grid = (d // tm, N_CHIPS * nps, t_len // tk)
    return _matmul(
        name, a, dy, TN, grid,
        pl.BlockSpec((tk, tm), lambda i, j, k: (k, i)),
        pl.BlockSpec((tk, tn), lambda i, j, k: (k, j)),
        pl.BlockSpec((None, tm, tn), lambda i, j, k: (j // nps, i, j % nps)),
        jax.ShapeDtypeStruct((N_CHIPS, d, w), BF16), (tm, tn))


def _mm_act_wr(name, a, wr, seg, add):
    off, r, tk = seg
    t_len = a.shape[0]
    d = wr.shape[2]
    tm = _tile(t_len, (1024, 512, 256, 128))
    tn = _tile(d, (1024, 512, 256, 128))
    kps = r // tk
    ob = off // tk
    grid = (t_len // tm, d // tn, N_CHIPS * kps)
    return _matmul(
        name, a, wr, NN, grid,
        pl.BlockSpec((tm, tk), lambda i, j, k: (i, k)),
        pl.BlockSpec((None, tk, tn), lambda i, j, k: (k // kps, ob + k % kps, j)),
        pl.BlockSpec((tm, tn), lambda i, j, k: (i, j)),
        jax.ShapeDtypeStruct((t_len, d), F32), (tm, tn),
        add=add, add_spec=pl.BlockSpec((tm, tn), lambda i, j, k: (i, j)))


def _mm_dact_wrT(name, dh, wr, seg):
    off, r, tn = seg
    t_len, d = dh.shape
    tm = _tile(t_len, (1024, 512, 256, 128))
    nps = r // tn
    ob = off // tn
    grid = (t_len // tm, N_CHIPS * nps, 1)
    return _matmul(
        name, dh, wr, NT, grid,
        pl.BlockSpec((tm, d), lambda i, j, k: (i, 0)),
        pl.BlockSpec((None, tn, d), lambda i, j, k: (j // nps, ob + j % nps, 0)),
        pl.BlockSpec((tm, tn), lambda i, j, k: (i, j)),
        jax.ShapeDtypeStruct((t_len, N_CHIPS * r), BF16), (tm, tn))


def _mm_grad_wr(name, a, dh, seg):
    _, r, tm = seg
    t_len, d = dh.shape
    tn = _tile(d, (1024, 512, 256, 128))
    tk = _tile(t_len, (1024, 512, 256, 128))
    mps = r // tm
    grid = (N_CHIPS * mps, d // tn, t_len // tk)
    return _matmul(
        name, a, dh, TN, grid,
        pl.BlockSpec((tk, tm), lambda i, j, k: (k, i)),
        pl.BlockSpec((tk, tn), lambda i, j, k: (k, j)),
        pl.BlockSpec((None, tm, tn), lambda i, j, k: (i // mps, i % mps, j)),
        jax.ShapeDtypeStruct((N_CHIPS, r, d), BF16), (tm, tn))


def _mm_plain(name, a, b, dims, out_dtype, add=None):
    if dims == NN:
        m, kd = a.shape
        n = b.shape[1]
    elif dims == NT:
        m, kd = a.shape
        n = b.shape[0]
    else:
        kd, m = a.shape
        n = b.shape[1]
    tm = _tile(m, (1024, 512, 256, 128))
    tn = _tile(n, (1024, 768, 512, 256, 128))
    tk = _tile(kd, (2048, 1024, 512, 256, 128))
    grid = (m // tm, n // tn, kd // tk)
    if dims == NN:
        a_spec = pl.BlockSpec((tm, tk), lambda i, j, k: (i, k))
        b_spec = pl.BlockSpec((tk, tn), lambda i, j, k: (k, j))
    elif dims == NT:
        a_spec = pl.BlockSpec((tm, tk), lambda i, j, k: (i, k))
        b_spec = pl.BlockSpec((tn, tk), lambda i, j, k: (j, k))
    else:
        a_spec = pl.BlockSpec((tk, tm), lambda i, j, k: (k, i))
        b_spec = pl.BlockSpec((tk, tn), lambda i, j, k: (k, j))
    o_spec = pl.BlockSpec((tm, tn), lambda i, j, k: (i, j))
    return _matmul(name, a, b, dims, grid, a_spec, b_spec, o_spec, jax.ShapeDtypeStruct((m, n), out_dtype), (tm, tn),
                   add=add, add_spec=None if add is None else o_spec)


def _rms_fwd(name, x, g):
    t_len, d = x.shape
    tm = _tile(t_len, (512, 256, 128))

    def body(x_ref, g_ref, o_ref):
        xv = x_ref[...]
        r = lax.rsqrt(jnp.mean(xv * xv, axis=-1, keepdims=True) + EPS)
        o_ref[...] = (xv * r * g_ref[...]).astype(o_ref.dtype)

    return pl.pallas_call(
        body, name=name, grid=(t_len // tm,),
        in_specs=[pl.BlockSpec((tm, d), lambda i: (i, 0)), pl.BlockSpec((1, d), lambda i: (0, 0))],
        out_specs=pl.BlockSpec((tm, d), lambda i: (i, 0)),
        out_shape=jax.ShapeDtypeStruct((t_len, d), BF16), compiler_params=_params(1))(x, g)


def _rms_bwd(name, dy, x, g, dres):
    t_len, d = x.shape
    tm = _tile(t_len, (256, 128))

    def body(dy_ref, x_ref, g_ref, dres_ref, dx_ref, dg_ref):
        xv = x_ref[...]
        r = lax.rsqrt(jnp.mean(xv * xv, axis=-1, keepdims=True) + EPS)
        xhat = xv * r
        dyv = dy_ref[...].astype(F32)
        dxn = dyv * g_ref[...]
        dx = r * (dxn - xhat * jnp.mean(dxn * xhat, axis=-1, keepdims=True))
        dx_ref[...] = dres_ref[...] + dx
        part = jnp.sum(dyv * xhat, axis=0, keepdims=True)

        @pl.when(pl.program_id(0) == 0)
        def _():
            dg_ref[...] = part

        @pl.when(pl.program_id(0) > 0)
        def _():
            dg_ref[...] += part

    row = pl.BlockSpec((tm, d), lambda i: (i, 0))
    vec = pl.BlockSpec((1, d), lambda i: (0, 0))
    return pl.pallas_call(
        body, name=name, grid=(t_len // tm,), in_specs=[row, row, vec, row], out_specs=(row, vec),
        out_shape=(jax.ShapeDtypeStruct((t_len, d), F32), jax.ShapeDtypeStruct((1, d), F32)),
        compiler_params=_params(1))(dy, x, g, dres)


def _loss_head(h, g, target):
    t_len, d = h.shape
    tm = _tile(t_len, (256, 128))

    def body(h_ref, g_ref, t_ref, dh_ref, dg_ref, loss_ref):
        xv = h_ref[...]
        gv = g_ref[...]
        r = lax.rsqrt(jnp.mean(xv * xv, axis=-1, keepdims=True) + EPS)
        xhat = xv * r
        err = xhat * gv - t_ref[...]
        dyv = err * (1.0 / d)
        dxn = dyv * gv
        dh_ref[...] = r * (dxn - xhat * jnp.mean(dxn * xhat, axis=-1, keepdims=True))
        part = jnp.sum(dyv * xhat, axis=0, keepdims=True)
        lpart = jnp.zeros((8, LANE), F32) + (0.5 / d) * jnp.sum(err * err)

        @pl.when(pl.program_id(0) == 0)
        def _():
            dg_ref[...] = part
            loss_ref[...] = lpart

        @pl.when(pl.program_id(0) > 0)
        def _():
            dg_ref[...] += part
            loss_ref[...] += lpart

    row = pl.BlockSpec((tm, d), lambda i: (i, 0))
    vec = pl.BlockSpec((1, d), lambda i: (0, 0))
    return pl.pallas_call(
        body, name="loss_head", grid=(t_len // tm,), in_specs=[row, vec, row],
        out_specs=(row, vec, pl.BlockSpec((8, LANE), lambda i: (0, 0))),
        out_shape=(jax.ShapeDtypeStruct((t_len, d), F32), jax.ShapeDtypeStruct((1, d), F32),
                   jax.ShapeDtypeStruct((8, LANE), F32)),
        compiler_params=_params(1))(h, g, target)


def _chunk_row(shape):
    return lax.broadcasted_iota(jnp.int32, shape, 0) % GLA_CHUNK


def _gla_gate_fwd(a, w_a2p, b_a2):
    t_len = a.shape[0]
    kd = w_a2p.shape[1]
    tm = _tile(t_len, (256, 128, 64))

    def body(a_ref, w_ref, b_ref, ga_ref, cum_ref):
        ga = _dot(a_ref[...], w_ref[...].astype(BF16), NN) + b_ref[...]
        ga_ref[...] = ga
        la = (jnp.minimum(ga, 0.0) - jnp.log(1.0 + jnp.exp(-jnp.abs(ga)))) * (1.0 / GATE_NORMALIZER)
        row = _chunk_row(la.shape)
        s = 1
        while s < GLA_CHUNK:
            la = la + jnp.where(row >= s, pltpu.roll(la, s, 0), 0.0)
            s *= 2
        cum_ref[...] = la

    return pl.pallas_call(
        body, name="gla_gate_fwd", grid=(t_len // tm,),
        in_specs=[pl.BlockSpec((tm, A_PAD), lambda i: (i, 0)), pl.BlockSpec((A_PAD, kd), lambda i: (0, 0)),
                  pl.BlockSpec((1, kd), lambda i: (0, 0))],
        out_specs=(pl.BlockSpec((tm, kd), lambda i: (i, 0)), pl.BlockSpec((tm, kd), lambda i: (i, 0))),
        out_shape=(jax.ShapeDtypeStruct((t_len, kd), F32), jax.ShapeDtypeStruct((t_len, kd), F32)),
        compiler_params=_params(1))(a, w_a2p, b_a2)


def _gla_gate_bwd(dcum, ga, a, w_a2p):
    t_len, kd = dcum.shape
    tm = _tile(t_len, (256, 128, 64))

    def body(dc_ref, ga_ref, a_ref, w_ref, da_ref, dw_ref, db_ref):
        x = dc_ref[...]
        row = _chunk_row(x.shape)
        s = 1
        while s < GLA_CHUNK:
            x = x + jnp.where(row < GLA_CHUNK - s, pltpu.roll(x, tm - s, 0), 0.0)
            s *= 2
        dga = x * (1.0 / GATE_NORMALIZER) * _sigmoid(-ga_ref[...])
        dgab = dga.astype(BF16)
        da_ref[...] = _dot(dgab, w_ref[...].astype(BF16), NT).astype(da_ref.dtype)
        dw = _dot(a_ref[...], dgab, TN)
        db = jnp.sum(dga, axis=0, keepdims=True)

        @pl.when(pl.program_id(0) == 0)
        def _():
            dw_ref[...] = dw
            db_ref[...] = db

        @pl.when(pl.program_id(0) > 0)
        def _():
            dw_ref[...] += dw
            db_ref[...] += db

    wide = pl.BlockSpec((tm, kd), lambda i: (i, 0))
    return pl.pallas_call(
        body, name="gla_gate_bwd", grid=(t_len // tm,),
        in_specs=[wide, wide, pl.BlockSpec((tm, A_PAD), lambda i: (i, 0)), pl.BlockSpec((A_PAD, kd), lambda i: (0, 0))],
        out_specs=(pl.BlockSpec((tm, A_PAD), lambda i: (i, 0)), pl.BlockSpec((A_PAD, kd), lambda i: (0, 0)),
                   pl.BlockSpec((1, kd), lambda i: (0, 0))),
        out_shape=(jax.ShapeDtypeStruct((t_len, A_PAD), BF16), jax.ShapeDtypeStruct((A_PAD, kd), F32),
                   jax.ShapeDtypeStruct((1, kd), F32)),
        compiler_params=_params(1))(dcum, ga, a, w_a2p)


def _gla_dims():
    dk = GLA_KEY_DIM // GLA_HEADS
    dv = GLA_VAL_DIM // GLA_HEADS
    return dk, dv


def _gla_fwd(proj, cum):
    t_len = proj.shape[0]
    dk, dv = _gla_dims()
    nc = t_len // GLA_CHUNK
    c = GLA_CHUNK
    scale = dk ** -0.5
    v0 = 2 * GLA_KEY_DIM // dv

    def body(q_ref, k_ref, v_ref, cum_ref, o_ref, st_ref, s_scr):
        @pl.when(pl.program_id(1) == 0)
        def _():
            s_scr[...] = jnp.zeros_like(s_scr)

        cm = cum_ref[...]
        last = cm[c - 1:c, :]
        q = q_ref[...].astype(F32) * scale
        k = k_ref[...].astype(F32)
        v = v_ref[...].astype(BF16)
        qd = (q * jnp.exp(cm)).astype(BF16)
        ki = (k * jnp.exp(-cm)).astype(BF16)
        ke = (k * jnp.exp(last - cm)).astype(BF16)
        tri = lax.broadcasted_iota(jnp.int32, (c, c), 0) >= lax.broadcasted_iota(jnp.int32, (c, c), 1)
        sc = jnp.where(tri, _dot(qd, ki, NT), 0.0)
        st = s_scr[...]
        st_ref[...] = st
        o_ref[...] = _dot(sc.astype(BF16), v, NN) + _dot(qd, st.astype(BF16), NT)
        s_scr[...] = st * jnp.exp(last) + _dot(v, ke, TN)

    return pl.pallas_call(
        body, name="gla_fwd", grid=(GLA_HEADS, nc),
        in_specs=[pl.BlockSpec((c, dk), lambda h, n: (n, h)),
                  pl.BlockSpec((c, dk), lambda h, n: (n, GLA_HEADS + h)),
                  pl.BlockSpec((c, dv), lambda h, n: (n, v0 + h)),
                  pl.BlockSpec((c, dk), lambda h, n: (n, h))],
        out_specs=(pl.BlockSpec((c, dv), lambda h, n: (n, h)),
                   pl.BlockSpec((None, None, dv, dk), lambda h, n: (h, n, 0, 0))),
        out_shape=(jax.ShapeDtypeStruct((t_len, GLA_VAL_DIM), F32),
                   jax.ShapeDtypeStruct((GLA_HEADS, nc, dv, dk), F32)),
        scratch_shapes=[pltpu.VMEM((dv, dk), F32)], compiler_params=_params(2))(proj, proj, proj, cum)


def _gla_bwd(proj, cum, states, do):
    t_len = proj.shape[0]
    dk, dv = _gla_dims()
    nc = t_len // GLA_CHUNK
    c = GLA_CHUNK
    scale = dk ** -0.5
    v0 = 2 * GLA_KEY_DIM // dv

    def body(q_ref, k_ref, v_ref, cum_ref, st_ref, do_ref, dq_ref, dk_ref, dv_ref, dc_ref, ds_scr):
        @pl.when(pl.program_id(1) == 0)
        def _():
            ds_scr[...] = jnp.zeros_like(ds_scr)

        cm = cum_ref[...]
        last = cm[c - 1:c, :]
        e_c = jnp.exp(cm)
        e_nc = jnp.exp(-cm)
        e_lc = jnp.exp(last - cm)
        e_l = jnp.exp(last)
        q = q_ref[...].astype(F32) * scale
        k = k_ref[...].astype(F32)
        v = v_ref[...].astype(BF16)
        dov = do_ref[...]
        qd32 = q * e_c
        ki32 = k * e_nc
        ke32 = k * e_lc
        qd = qd32.astype(BF16)
        ki = ki32.astype(BF16)
        ke = ke32.astype(BF16)
        st = st_ref[...]
        dst = ds_scr[...]
        dstb = dst.astype(BF16)
        tri = lax.broadcasted_iota(jnp.int32, (c, c), 0) >= lax.broadcasted_iota(jnp.int32, (c, c), 1)
        am = jnp.where(tri, _dot(dov, v, NT), 0.0).astype(BF16)
        pm = jnp.where(tri, _dot(qd, ki, NT), 0.0).astype(BF16)
        dqd = _dot(am, ki, NN) + _dot(dov, st.astype(BF16), NN)
        dki = _dot(am, qd, TN)
        dvv = _dot(pm, dov, TN) + _dot(ke, dstb, NT)
        dke = _dot(v, dstb, NN)
        d_el = jnp.sum(dst * st, axis=0, keepdims=True)
        ds_scr[...] = dst * e_l + _dot(dov, qd, TN)
        dq_ref[...] = (dqd * scale * e_c).astype(dq_ref.dtype)
        dk_ref[...] = (dki * e_nc + dke * e_lc).astype(dk_ref.dtype)
        dv_ref[...] = dvv.astype(dv_ref.dtype)
        dkeke = dke * ke32
        dcum = dqd * qd32 - dki * ki32 - dkeke
        dlast = jnp.sum(dkeke, axis=0, keepdims=True) + d_el * e_l
        row = lax.broadcasted_iota(jnp.int32, dcum.shape, 0)
        dc_ref[...] = jnp.where(row == c - 1, dcum + dlast, dcum)

    rev = nc - 1
    return pl.pallas_call(
        body, name="gla_bwd", grid=(GLA_HEADS, nc),
        in_specs=[pl.BlockSpec((c, dk), lambda h, n: (rev - n, h)),
                  pl.BlockSpec((c, dk), lambda h, n: (rev - n, GLA_HEADS + h)),
                  pl.BlockSpec((c, dv), lambda h, n: (rev - n, v0 + h)),
                  pl.BlockSpec((c, dk), lambda h, n: (rev - n, h)),
                  pl.BlockSpec((None, None, dv, dk), lambda h, n: (h, rev - n, 0, 0)),
                  pl.BlockSpec((c, dv), lambda h, n: (rev - n, h))],
        out_specs=(pl.BlockSpec((c, dk), lambda h, n: (rev - n, h)),
                   pl.BlockSpec((c, dk), lambda h, n: (rev - n, h)),
                   pl.BlockSpec((c, dv), lambda h, n: (rev - n, h)),
                   pl.BlockSpec((c, dk), lambda h, n: (rev - n, h))),
        out_shape=(jax.ShapeDtypeStruct((t_len, GLA_KEY_DIM), BF16), jax.ShapeDtypeStruct((t_len, GLA_KEY_DIM), BF16),
                   jax.ShapeDtypeStruct((t_len, GLA_VAL_DIM), BF16), jax.ShapeDtypeStruct((t_len, GLA_KEY_DIM), F32)),
        scratch_shapes=[pltpu.VMEM((dv, dk), F32)], compiler_params=_params(2))(proj, proj, proj, cum, states, do)


def _gla_out_fwd(o, proj, gn):
    t_len = o.shape[0]
    _, dv = _gla_dims()
    tm = _tile(t_len, (512, 256, 128))
    r0 = (2 * GLA_KEY_DIM + GLA_VAL_DIM) // dv

    def body(o_ref, r_ref, g_ref, y_ref):
        ov = o_ref[...]
        rs = lax.rsqrt(jnp.mean(ov * ov, axis=-1, keepdims=True) + EPS)
        rv = r_ref[...].astype(F32)
        y_ref[...] = (ov * rs * g_ref[...] * (rv * _sigmoid(rv))).astype(y_ref.dtype)

    return pl.pallas_call(
        body, name="gla_out_fwd", grid=(t_len // tm, GLA_HEADS),
        in_specs=[pl.BlockSpec((tm, dv), lambda i, h: (i, h)), pl.BlockSpec((tm, dv), lambda i, h: (i, r0 + h)),
                  pl.BlockSpec((1, dv), lambda i, h: (0, 0))],
        out_specs=pl.BlockSpec((tm, dv), lambda i, h: (i, h)),
        out_shape=jax.ShapeDtypeStruct((t_len, GLA_VAL_DIM), BF16), compiler_params=_params(2))(o, proj, gn)


def _gla_out_bwd(dy, o, proj, gn):
    t_len = o.shape[0]
    _, dv = _gla_dims()
    tm = _tile(t_len, (512, 256, 128))
    r0 = (2 * GLA_KEY_DIM + GLA_VAL_DIM) // dv

    def body(dy_ref, o_ref, r_ref, g_ref, do_ref, dr_ref, dg_ref):
        ov = o_ref[...]
        gv = g_ref[...]
        rs = lax.rsqrt(jnp.mean(ov * ov, axis=-1, keepdims=True) + EPS)
        xhat = ov * rs
        rv = r_ref[...].astype(F32)
        sg = _sigmoid(rv)
        gate = rv * sg
        dyv = dy_ref[...].astype(F32)
        dn = dyv * gate
        dr_ref[...] = (dyv * xhat * gv * (sg * (1.0 + rv * (1.0 - sg)))).astype(dr_ref.dtype)
        dxn = dn * gv
        do_ref[...] = (rs * (dxn - xhat * jnp.mean(dxn * xhat, axis=-1, keepdims=True))).astype(do_ref.dtype)
        part = jnp.sum(dn * xhat, axis=0, keepdims=True)
        first = (pl.program_id(0) == 0) & (pl.program_id(1) == 0)

        @pl.when(first)
        def _():
            dg_ref[...] = part

        @pl.when(jnp.logical_not(first))
        def _():
            dg_ref[...] += part

    blk = pl.BlockSpec((tm, dv), lambda i, h: (i, h))
    return pl.pallas_call(
        body, name="gla_out_bwd", grid=(t_len // tm, GLA_HEADS),
        in_specs=[blk, blk, pl.BlockSpec((tm, dv), lambda i, h: (i, r0 + h)), pl.BlockSpec((1, dv), lambda i, h: (0, 0))],
        out_specs=(blk, blk, pl.BlockSpec((1, dv), lambda i, h: (0, 0))),
        out_shape=(jax.ShapeDtypeStruct((t_len, GLA_VAL_DIM), BF16), jax.ShapeDtypeStruct((t_len, GLA_VAL_DIM), BF16),
                   jax.ShapeDtypeStruct((1, dv), F32)),
        compiler_params=_params(2))(dy, o, proj, gn)


def _alibi_slopes():
    n = ATT_HEADS
    start = 2.0 ** (-8.0 / n)
    return [start ** (i + 1) for i in range(n)]


def _att_masks(d):
    b = ATT_BLOCK
    qa = lax.broadcasted_iota(jnp.int32, (b, b), 0)
    kb = lax.broadcasted_iota(jnp.int32, (b, b), 1)
    dist_c = qa - kb
    dist_p = qa - kb + b
    return dist_c >= 0, dist_p <= b, (dist_c * d).astype(F32), (dist_p * d).astype(F32)


def _att_fwd(q_all, kv, g):
    d = DILATIONS[g]
    assert WINDOWS[g] // d == ATT_BLOCK
    t_len = q_all.shape[0]
    hd = ATT_HEADS * HEAD_DIM
    sub = t_len // d
    nb = sub // ATT_BLOCK
    b = ATT_BLOCK
    e = HEAD_DIM
    scale = e ** -0.5
    slopes = _alibi_slopes()
    qv = q_all.reshape(sub, d * 3 * hd)
    kvv = kv.reshape(sub, d * 2 * hd)

    def body(q_ref, kp_ref, kc_ref, vp_ref, vc_ref, o_ref, l_ref):
        ib = pl.program_id(1)
        valid_c, valid_p0, dist_c, dist_p = _att_masks(d)
        valid_p = valid_p0 & (ib > 0)
        for h in range(ATT_HEADS):
            hs = slice(h * e, (h + 1) * e)
            qh = q_ref[:, hs]
            s_c = jnp.where(valid_c, _dot(qh, kc_ref[:, hs], NT) * scale - slopes[h] * dist_c, NEG)
            s_p = jnp.where(valid_p, _dot(qh, kp_ref[:, hs], NT) * scale - slopes[h] * dist_p, NEG)
            m = jnp.maximum(jnp.max(s_c, axis=1, keepdims=True), jnp.max(s_p, axis=1, keepdims=True))
            p_c = jnp.where(valid_c, jnp.exp(s_c - m), 0.0)
            p_p = jnp.where(valid_p, jnp.exp(s_p - m), 0.0)
            l = jnp.sum(p_c, axis=1, keepdims=True) + jnp.sum(p_p, axis=1, keepdims=True)
            acc = _dot(p_c.astype(BF16), vc_ref[:, hs], NN) + _dot(p_p.astype(BF16), vp_ref[:, hs], NN)
            o_ref[:, hs] = acc / l
            l_ref[:, hs] = jnp.broadcast_to(m + jnp.log(l), (b, e))

    blk = (b, hd)
    o, lse = pl.pallas_call(
        body, name=f"att_fwd{g}", grid=(d, nb),
        in_specs=[pl.BlockSpec(blk, lambda r, i: (i, 3 * r + g)),
                  pl.BlockSpec(blk, lambda r, i: (jnp.maximum(i - 1, 0), 2 * r)),
                  pl.BlockSpec(blk, lambda r, i: (i, 2 * r)),
                  pl.BlockSpec(blk, lambda r, i: (jnp.maximum(i - 1, 0), 2 * r + 1)),
                  pl.BlockSpec(blk, lambda r, i: (i, 2 * r + 1))],
        out_specs=(pl.BlockSpec(blk, lambda r, i: (i, r)), pl.BlockSpec(blk, lambda r, i: (i, r))),
        out_shape=(jax.ShapeDtypeStruct((sub, d * hd), F32), jax.ShapeDtypeStruct((sub, d * hd), F32)),
        compiler_params=_params(2))(qv, kvv, kvv, kvv, kvv)
    return o.reshape(t_len, hd), lse.reshape(t_len, hd)


def _att_merge(os, ls):
    t_len, hd = os[0].shape
    tm = _tile(t_len, (256, 128))

    def body(o0, o1, o2, l0, l1, l2, of_ref, ob_ref, l_ref):
        a0, a1, a2 = l0[...], l1[...], l2[...]
        m = jnp.maximum(jnp.maximum(a0, a1), a2)
        e0, e1, e2 = jnp.exp(a0 - m), jnp.exp(a1 - m), jnp.exp(a2 - m)
        den = e0 + e1 + e2
        o = (e0 * o0[...] + e1 * o1[...] + e2 * o2[...]) / den
        of_ref[...] = o
        ob_ref[...] = o.astype(ob_ref.dtype)
        l_ref[...] = m + jnp.log(den)

    row = pl.BlockSpec((tm, hd), lambda i: (i, 0))
    return pl.pallas_call(
        body, name="att_merge", grid=(t_len // tm,), in_specs=[row] * 6, out_specs=(row, row, row),
        out_shape=(jax.ShapeDtypeStruct((t_len, hd), F32), jax.ShapeDtypeStruct((t_len, hd), BF16),
                   jax.ShapeDtypeStruct((t_len, hd), F32)),
        compiler_params=_params(1))(*os, *ls)


def _att_bwd(q_all, kv, o, lse, do, g):
    d = DILATIONS[g]
    t_len = q_all.shape[0]
    hd = ATT_HEADS * HEAD_DIM
    sub = t_len // d
    nb = sub // ATT_BLOCK
    b = ATT_BLOCK
    e = HEAD_DIM
    scale = e ** -0.5
    slopes = _alibi_slopes()
    qv = q_all.reshape(sub, d * 3 * hd)
    kvv = kv.reshape(sub, d * 2 * hd)
    ov = o.reshape(sub, d * hd)
    lv = lse.reshape(sub, d * hd)
    dov = do.reshape(sub, d * hd)

    def body(qj_ref, qn_ref, kp_ref, kc_ref, vp_ref, vc_ref, doj_ref, don_ref, oj_ref, on_ref, lj_ref, ln_ref,
             dq_ref, dk_ref, dv_ref):
        j = pl.program_id(1)
        valid_c, valid_p0, dist_c, dist_p = _att_masks(d)
        valid_p = valid_p0 & (j > 0)
        valid_n = valid_p0 & (j + 1 < nb)
        for h in range(ATT_HEADS):
            hs = slice(h * e, (h + 1) * e)
            qj, qn = qj_ref[:, hs], qn_ref[:, hs]
            kc, kp = kc_ref[:, hs], kp_ref[:, hs]
            vc, vp = vc_ref[:, hs], vp_ref[:, hs]
            doj, don = doj_ref[:, hs], don_ref[:, hs]
            dlt_j = jnp.sum(doj.astype(F32) * oj_ref[:, hs], axis=1, keepdims=True)
            dlt_n = jnp.sum(don.astype(F32) * on_ref[:, hs], axis=1, keepdims=True)
            lj, ln = lj_ref[:, hs], ln_ref[:, hs]
            s = _dot(qj, kc, NT) * scale - slopes[h] * dist_c
            p = jnp.where(valid_c, jnp.exp(jnp.where(valid_c, s - lj, NEG)), 0.0)
            ds = (p * (_dot(doj, vc, NT) - dlt_j)).astype(BF16)
            dq = _dot(ds, kc, NN)
            dk = _dot(ds, qj, TN)
            dv = _dot(p.astype(BF16), doj, TN)
            s = _dot(qj, kp, NT) * scale - slopes[h] * dist_p
            p = jnp.where(valid_p, jnp.exp(jnp.where(valid_p, s - lj, NEG)), 0.0)
            ds = (p * (_dot(doj, vp, NT) - dlt_j)).astype(BF16)
            dq = dq + _dot(ds, kp, NN)
            s = _dot(qn, kc, NT) * scale - slopes[h] * dist_p
            p = jnp.where(valid_n, jnp.exp(jnp.where(valid_n, s - ln, NEG)), 0.0)
            ds = (p * (_dot(don, vc, NT) - dlt_n)).astype(BF16)
            dk = dk + _dot(ds, qn, TN)
            dv = dv + _dot(p.astype(BF16), don, TN)
            dq_ref[:, hs] = (dq * scale).astype(dq_ref.dtype)
            dk_ref[:, hs] = dk * scale
            dv_ref[:, hs] = dv

    blk = (b, hd)
    nxt = lambda i: jnp.minimum(i + 1, nb - 1)
    prv = lambda i: jnp.maximum(i - 1, 0)
    dq, dk, dv = pl.pallas_call(
        body, name=f"att_bwd{g}", grid=(d, nb),
        in_specs=[pl.BlockSpec(blk, lambda r, i: (i, 3 * r + g)),
                  pl.BlockSpec(blk, lambda r, i: (nxt(i), 3 * r + g)),
                  pl.BlockSpec(blk, lambda r, i: (prv(i), 2 * r)),
                  pl.BlockSpec(blk, lambda r, i: (i, 2 * r)),
                  pl.BlockSpec(blk, lambda r, i: (prv(i), 2 * r + 1)),
                  pl.BlockSpec(blk, lambda r, i: (i, 2 * r + 1)),
                  pl.BlockSpec(blk, lambda r, i: (i, r)),
                  pl.BlockSpec(blk, lambda r, i: (nxt(i), r)),
                  pl.BlockSpec(blk, lambda r, i: (i, r)),
                  pl.BlockSpec(blk, lambda r, i: (nxt(i), r)),
                  pl.BlockSpec(blk, lambda r, i: (i, r)),
                  pl.BlockSpec(blk, lambda r, i: (nxt(i), r))],
        out_specs=(pl.BlockSpec(blk, lambda r, i: (i, r)),) * 3,
        out_shape=(jax.ShapeDtypeStruct((sub, d * hd), BF16), jax.ShapeDtypeStruct((sub, d * hd), F32),
                   jax.ShapeDtypeStruct((sub, d * hd), F32)),
        compiler_params=_params(2))(qv, qv, kvv, kvv, kvv, kvv, dov, dov, ov, ov, lv, lv)
    return dq.reshape(t_len, hd), dk.reshape(t_len, hd), dv.reshape(t_len, hd)


def _kv_grad_sum(dks, dvs):
    t_len, hd = dks[0].shape
    tm = _tile(t_len, (256, 128))

    def body(k0, k1, k2, v0, v1, v2, o_ref):
        o_ref[:, :hd] = (k0[...] + k1[...] + k2[...]).astype(o_ref.dtype)
        o_ref[:, hd:] = (v0[...] + v1[...] + v2[...]).astype(o_ref.dtype)

    row = pl.BlockSpec((tm, hd), lambda i: (i, 0))
    return pl.pallas_call(
        body, name="kv_grad_sum", grid=(t_len // tm,), in_specs=[row] * 6,
        out_specs=pl.BlockSpec((tm, 2 * hd), lambda i: (i, 0)),
        out_shape=jax.ShapeDtypeStruct((t_len, 2 * hd), BF16), compiler_params=_params(1))(*dks, *dvs)


HALO = 16
INV_SQRT2 = 1.0 / math.sqrt(2.0)
INV_SQRT2PI = 1.0 / math.sqrt(2.0 * math.pi)


def _conv_taps(g, halo, cw, cb):
    row = lax.broadcasted_iota(jnp.int32, g.shape, 0)
    h1 = halo[HALO - 1:HALO, :]
    h2 = halo[HALO - 2:HALO - 1, :]
    g1 = jnp.where(row == 0, h1, pltpu.roll(g, 1, 0))
    g2 = jnp.where(row == 0, h2, jnp.where(row == 1, h1, pltpu.roll(g, 2, 0)))
    gc = cw[0:1, :] * g2 + cw[1:2, :] * g1 + cw[2:3, :] * g + cb
    return gc, g1, g2


def _glu_specs(t_len, f, tm, tc):
    nj = f // tc
    hb = tm // HALO
    u = pl.BlockSpec((tm, tc), lambda j, i: (i, j))
    g = pl.BlockSpec((tm, tc), lambda j, i: (i, nj + j))
    gh = pl.BlockSpec((HALO, tc), lambda j, i: (jnp.maximum(i * hb - 1, 0), nj + j))
    cw = pl.BlockSpec((8, tc), lambda j, i: (0, j))
    cb = pl.BlockSpec((1, tc), lambda j, i: (0, j))
    return u, g, gh, cw, cb


def _glu_fwd(name, up, cw, cb):
    t_len = up.shape[0]
    f = up.shape[1] // 2
    tm = _tile(t_len, (512, 256, 128))
    tc = _tile(f, (1408, 1024, 512, 256, 128))
    u_s, g_s, gh_s, cw_s, cb_s = _glu_specs(t_len, f, tm, tc)

    def body(u_ref, g_ref, gh_ref, cw_ref, cb_ref, o_ref):
        first = pl.program_id(1) == 0
        halo = jnp.where(first, 0.0, gh_ref[...].astype(F32))
        gc, _, _ = _conv_taps(g_ref[...].astype(F32), halo, cw_ref[...], cb_ref[...])
        gel = 0.5 * gc * (1.0 + lax.erf(gc * INV_SQRT2))
        o_ref[...] = (gel * u_ref[...].astype(F32)).astype(o_ref.dtype)

    return pl.pallas_call(
        body, name=name, grid=(f // tc, t_len // tm), in_specs=[u_s, g_s, gh_s, cw_s, cb_s],
        out_specs=pl.BlockSpec((tm, tc), lambda j, i: (i, j)),
        out_shape=jax.ShapeDtypeStruct((t_len, f), BF16), compiler_params=_params(2))(up, up, up, cw, cb)


def _glu_bwd_a(name, dact, up, cw, cb):
    t_len = up.shape[0]
    f = up.shape[1] // 2
    tm = _tile(t_len, (256, 128))
    tc = _tile(f, (1408, 1024, 512, 256, 128))
    u_s, g_s, gh_s, cw_s, cb_s = _glu_specs(t_len, f, tm, tc)

    def body(da_ref, u_ref, g_ref, gh_ref, cw_ref, cb_ref, du_ref, dgc_ref, w0_ref, w1_ref, w2_ref, b_ref):
        first = pl.program_id(1) == 0
        halo = jnp.where(first, 0.0, gh_ref[...].astype(F32))
        g = g_ref[...].astype(F32)
        gc, g1, g2 = _conv_taps(g, halo, cw_ref[...], cb_ref[...])
        phi = 0.5 * (1.0 + lax.erf(gc * INV_SQRT2))
        dgel = phi + gc * jnp.exp(-0.5 * gc * gc) * INV_SQRT2PI
        da = da_ref[...].astype(F32)
        du_ref[...] = (da * gc * phi).astype(du_ref.dtype)
        dgc = da * u_ref[...].astype(F32) * dgel
        dgc_ref[...] = dgc.astype(dgc_ref.dtype)
        parts = (jnp.sum(dgc * g2, axis=0, keepdims=True), jnp.sum(dgc * g1, axis=0, keepdims=True),
                 jnp.sum(dgc * g, axis=0, keepdims=True), jnp.sum(dgc, axis=0, keepdims=True))
        refs = (w0_ref, w1_ref, w2_ref, b_ref)

        @pl.when(first)
        def _():
            for r, p in zip(refs, parts):
                r[...] = p

        @pl.when(jnp.logical_not(first))
        def _():
            for r, p in zip(refs, parts):
                r[...] += p

    tile = pl.BlockSpec((tm, tc), lambda j, i: (i, j))
    vec = pl.BlockSpec((1, tc), lambda j, i: (0, j))
    vshape = jax.ShapeDtypeStruct((1, f), F32)
    return pl.pallas_call(
        body, name=name, grid=(f // tc, t_len // tm), in_specs=[tile, u_s, g_s, gh_s, cw_s, cb_s],
        out_specs=(tile, tile, vec, vec, vec, vec),
        out_shape=(jax.ShapeDtypeStruct((t_len, f), BF16), jax.ShapeDtypeStruct((t_len, f), F32),
                   vshape, vshape, vshape, vshape),
        compiler_params=_params(2))(dact, up, up, up, cw, cb)


def _glu_bwd_b(name, du, dgc, cw):
    t_len, f = du.shape
    tm = _tile(t_len, (128, 64))
    hb = tm // HALO
    n_i = t_len // tm
    last_hb = t_len // HALO - 1

    def body(du_ref, d_ref, dh_ref, cw_ref, o_ref):
        last = pl.program_id(0) == n_i - 1
        halo = jnp.where(last, 0.0, dh_ref[...].astype(F32))
        dd = d_ref[...].astype(F32)
        row = lax.broadcasted_iota(jnp.int32, dd.shape, 0)
        h0 = halo[0:1, :]
        h1 = halo[1:2, :]
        d1 = jnp.where(row == tm - 1, h0, pltpu.roll(dd, tm - 1, 0))
        d2 = jnp.where(row == tm - 1, h1, jnp.where(row == tm - 2, h0, pltpu.roll(dd, tm - 2, 0)))
        cwv = cw_ref[...]
        dg = cwv[2:3, :] * dd + cwv[1:2, :] * d1 + cwv[0:1, :] * d2
        o_ref[:, :f] = du_ref[...]
        o_ref[:, f:] = dg.astype(o_ref.dtype)

    row_s = pl.BlockSpec((tm, f), lambda i: (i, 0))
    return pl.pallas_call(
        body, name=name, grid=(n_i,),
        in_specs=[row_s, row_s, pl.BlockSpec((HALO, f), lambda i: (jnp.minimum((i + 1) * hb, last_hb), 0)),
                  pl.BlockSpec((8, f), lambda i: (0, 0))],
        out_specs=pl.BlockSpec((tm, 2 * f), lambda i: (i, 0)),
        out_shape=jax.ShapeDtypeStruct((t_len, 2 * f), BF16), compiler_params=_params(1))(du, dgc, dgc, cw)


def _adamw(name, w, g, m, v):
    rows, cols = w.shape
    tr = _tile(rows, (256, 128, 64, 32, 16, 8))
    c1 = 1.0 / (1.0 - ADAM_B1 ** ADAM_STEP)
    c2 = 1.0 / (1.0 - ADAM_B2 ** ADAM_STEP)

    def body(w_ref, g_ref, m_ref, v_ref, d_ref, nm_ref, nv_ref):
        gv = g_ref[...]
        nm = ADAM_B1 * m_ref[...] + (1.0 - ADAM_B1) * gv
        nv = ADAM_B2 * v_ref[...] + (1.0 - ADAM_B2) * (gv * gv)
        nm_ref[...] = nm
        nv_ref[...] = nv
        d_ref[...] = -ADAM_LR * ((nm * c1) / (jnp.sqrt(nv * c2) + ADAM_EPS) + ADAM_WD * w_ref[...])

    blk = pl.BlockSpec((tr, cols), lambda i: (i, 0))
    shp = jax.ShapeDtypeStruct((rows, cols), F32)
    return pl.pallas_call(body, name=name, grid=(rows // tr,), in_specs=[blk] * 4, out_specs=(blk,) * 3,
                          out_shape=(shp,) * 3, compiler_params=_params(1))(w, g, m, v)


def _local_step(x, target, wc, wr, w_main, w_a, norms, small):
    lay = _layout()
    hd = ATT_HEADS * HEAD_DIM
    f = D_FF

    hn0 = _rms_fwd("rms_attn0", x, norms["attn0"])
    proj = _mm_plain("gla_proj", hn0, w_main, NN, F32)
    a = _mm_plain("gla_proj_a", hn0, w_a, NN, BF16)
    ga, cum = _gla_gate_fwd(a, small["w_a2p"], small["b_a2"])
    o_gla, states = _gla_fwd(proj, cum)
    gated = _gla_out_fwd(o_gla, proj, small["head_norm"])
    h1 = _mm_act_wr("gla_out", gated, wr, lay["gout"], add=x)

    def ffn_fwd(l, h):
        hn = _rms_fwd(f"rms_ffn{l}", h, norms[f"ffn{l}"])
        up = _mm_act_wc(f"ffn_up{l}", hn, wc, lay[f"up{l}"], F32)
        act = _glu_fwd(f"glu_fwd{l}", up, small["conv_w"][l], small["conv_b"][l])
        return hn, up, act, _mm_act_wr(f"ffn_down{l}", act, wr, lay[f"down{l}"], add=h)

    hnf0, up0, act0, h2 = ffn_fwd(0, h1)

    kvn = _rms_fwd("rms_kv", h2, norms["kv"])
    kv = _mm_act_wc("kv_proj", kvn, wc, lay["wkv"], BF16)
    hn1 = _rms_fwd("rms_attn1", h2, norms["attn1"])
    q_all = _mm_act_wc("q_proj", hn1, wc, lay["wq"], BF16)
    branch = [_att_fwd(q_all, kv, g) for g in range(3)]
    o_att, o_att_b, lse = _att_merge([br[0] for br in branch], [br[1] for br in branch])
    h3 = _mm_act_wr("att_out", o_att_b, wr, lay["dout"], add=h2)
    hnf1, up1, act1, h4 = ffn_fwd(1, h3)

    dh4, d_final, loss = _loss_head(h4, norms["final"], target)

    big = {}
    sm = {"final": d_final}

    def ffn_bwd(l, dh, h, hn, up, act):
        dact = _mm_dact_wrT(f"ffn_down_dx{l}", dh, wr, lay[f"down{l}"])
        big[f"down{l}"] = _mm_grad_wr(f"ffn_down_dw{l}", act, dh, lay[f"down{l}"])
        du, dgc, w0, w1, w2, db = _glu_bwd_a(f"glu_bwd_a{l}", dact, up, small["conv_w"][l], small["conv_b"][l])
        sm[f"conv_w{l}"] = (w0, w1, w2)
        sm[f"conv_b{l}"] = db
        dup = _glu_bwd_b(f"glu_bwd_b{l}", du, dgc, small["conv_w"][l])
        dhn = _mm_dact_wcT(f"ffn_up_dx{l}", dup, wc, lay[f"up{l}"])
        big[f"up{l}"] = _mm_grad_wc(f"ffn_up_dw{l}", hn, dup, lay[f"up{l}"])
        dh_in, sm[f"ffn{l}"] = _rms_bwd(f"rms_ffn_bwd{l}", dhn, h, norms[f"ffn{l}"], dh)
        return dh_in

    dh3 = ffn_bwd(1, dh4, h3, hnf1, up1, act1)

    do_att = _mm_dact_wrT("att_out_dx", dh3, wr, lay["dout"])
    big["dout"] = _mm_grad_wr("att_out_dw", o_att_b, dh3, lay["dout"])
    bw = [_att_bwd(q_all, kv, o_att, lse, do_att, g) for g in range(3)]
    dq_all = jnp.concatenate([t[0] for t in bw], axis=1)
    dhn1 = _mm_dact_wcT("q_proj_dx", dq_all, wc, lay["wq"])
    big["wq"] = _mm_grad_wc("q_proj_dw", hn1, dq_all, lay["wq"])
    dh2, sm["attn1"] = _rms_bwd("rms_attn1_bwd", dhn1, h2, norms["attn1"], dh3)
    dkv = _kv_grad_sum([t[1] for t in bw], [t[2] for t in bw])
    dkvn = _mm_dact_wcT("kv_proj_dx", dkv, wc, lay["wkv"])
    big["wkv"] = _mm_grad_wc("kv_proj_dw", kvn, dkv, lay["wkv"])
    dh2, sm["kv"] = _rms_bwd("rms_kv_bwd", dkvn, h2, norms["kv"], dh2)

    dh1 = ffn_bwd(0, dh2, h1, hnf0, up0, act0)

    dgated = _mm_dact_wrT("gla_out_dx", dh1, wr, lay["gout"])
    big["gout"] = _mm_grad_wr("gla_out_dw", gated, dh1, lay["gout"])
    do_gla, dr, sm["head_norm"] = _gla_out_bwd(dgated, o_gla, proj, small["head_norm"])
    dq, dk, dv, dcum = _gla_bwd(proj, cum, states, do_gla)
    da, sm["w_a2p"], sm["b_a2"] = _gla_gate_bwd(dcum, ga, a, small["w_a2p"])
    dproj = jnp.concatenate([dq, dk, dv, dr], axis=1)
    dhn0 = _mm_plain("gla_proj_dx", dproj, w_main, NT, F32)
    dhn0 = _mm_plain("gla_proj_a_dx", da, w_a, NT, F32, add=dhn0)
    big["gin_main"] = _mm_plain("gla_proj_dw", hn0, dproj, TN, BF16)
    big["gin_a"] = _mm_plain("gla_proj_a_dw", hn0, da, TN, BF16)
    grad_x, sm["attn0"] = _rms_bwd("rms_attn0_bwd", dhn0, x, norms["attn0"], dh1)
    return loss, grad_x, big, sm


def _pack_wc(gla_w_in, w_kv, dsa_w_q, ffn_w_up):
    lay = _layout()
    buf = jnp.zeros((D_MODEL, lay["wc_cols"]), BF16)
    parts = (("up0", ffn_w_up[0]), ("up1", ffn_w_up[1]), ("wq", dsa_w_q[0]), ("wkv", w_kv), ("gin", gla_w_in[0]))
    for name, w in parts:
        off = lay[name][0]
        buf = lax.dynamic_update_slice(buf, w.astype(BF16), (0, off))
    return buf


def _pack_wr(gla_w_out, dsa_w_out, ffn_w_down):
    lay = _layout()
    buf = jnp.zeros((lay["wr_rows"], D_MODEL), BF16)
    parts = (("down0", ffn_w_down[0]), ("down1", ffn_w_down[1]), ("gout", gla_w_out[0]), ("dout", dsa_w_out[0]))
    for name, w in parts:
        off = lay[name][0]
        buf = lax.dynamic_update_slice(buf, w.astype(BF16), (off, 0))
    return buf


def _unpack_gin(wc):
    off, w, _ = _layout()["gin"]
    full = jnp.transpose(wc[:, :, off:off + w], (1, 0, 2)).reshape(D_MODEL, N_CHIPS * w)
    n_main = 2 * GLA_KEY_DIM + 2 * GLA_VAL_DIM
    w_a = jnp.pad(full[:, n_main:], ((0, 0), (0, A_PAD - GATE_RANK)))
    return full[:, :n_main], w_a


def _small_params(attn_norm, ffn_norm, kv_norm, final_norm, conv_b, w_a2, b_a2, head_norm, conv_w):
    norms = {"attn0": attn_norm[0:1], "attn1": attn_norm[1:2], "ffn0": ffn_norm[0:1], "ffn1": ffn_norm[1:2],
             "kv": kv_norm[None, :], "final": final_norm[None, :]}
    small = {"w_a2p": jnp.pad(w_a2, ((0, A_PAD - GATE_RANK), (0, 0))), "b_a2": b_a2[None, :],
             "head_norm": head_norm[None, :], "conv_w": jnp.pad(conv_w, ((0, 0), (0, 8 - conv_w.shape[1]), (0, 0))),
             "conv_b": conv_b[:, None, :]}
    return norms, small


ANY = pl.BlockSpec(memory_space=pl.ANY)


def _place():
    return lax.axis_index("x"), lax.axis_index("y"), lax.axis_index("c")


def _other_chips(x, y):
    return [(1 - x, y), (x, 1 - y), (1 - x, 1 - y)]


def _rcopy(src, dst, ssem, rsem, dev):
    return pltpu.make_async_remote_copy(src_ref=src, dst_ref=dst, send_sem=ssem, recv_sem=rsem, device_id=dev,
                                        device_id_type=MESH)


def _gather_weights(arrs):
    n = len(arrs)

    def body(*refs):
        ins, outs = refs[:n], refs[n:2 * n]
        send, recv, fsend, frecv, lsem = refs[2 * n:]
        x, y, c = _place()
        me = 2 * x + y
        chips = _other_chips(x, y)
        sib = (x, y, 1 - c)
        local = [pltpu.make_async_copy(ins[a], outs[a].at[me], lsem.at[a]) for a in range(n)]
        for cp in local:
            cp.start()
        first, fwd = [], []
        for a in range(n):
            h = ins[a].shape[0] // 2
            mine = pl.ds(c * h, h)
            for j, (px, py) in enumerate(chips):
                cp = _rcopy(ins[a].at[mine], outs[a].at[me, mine], send.at[3 * a + j], recv.at[3 * a + j], (px, py, c))
                cp.start()
                first.append(cp)
        for a in range(n):
            h = ins[a].shape[0] // 2
            mine = pl.ds(c * h, h)
            for j, (px, py) in enumerate(chips):
                landed = outs[a].at[2 * px + py, mine]
                _rcopy(landed, landed, send.at[3 * a + j], recv.at[3 * a + j], (px, py, c)).wait_recv()
                cp = _rcopy(landed, landed, fsend.at[3 * a + j], frecv.at[3 * a + j], sib)
                cp.start()
                fwd.append(cp)
        for a in range(n):
            h = ins[a].shape[0] // 2
            theirs = pl.ds((1 - c) * h, h)
            for j, (px, py) in enumerate(chips):
                got = outs[a].at[2 * px + py, theirs]
                _rcopy(got, got, fsend.at[3 * a + j], frecv.at[3 * a + j], sib).wait_recv()
        for cp in first + fwd:
            cp.wait_send()
        for cp in local:
            cp.wait()

    return pl.pallas_call(
        body, name="gather_weights", in_specs=[ANY] * n, out_specs=[ANY] * n,
        out_shape=[jax.ShapeDtypeStruct((N_CHIPS,) + a.shape, a.dtype) for a in arrs],
        scratch_shapes=[pltpu.SemaphoreType.DMA((3 * n,)), pltpu.SemaphoreType.DMA((3 * n,)),
                        pltpu.SemaphoreType.DMA((3 * n,)), pltpu.SemaphoreType.DMA((3 * n,)),
                        pltpu.SemaphoreType.DMA((n,))])(*arrs)


def _swap_halves(arrs):
    n = len(arrs)

    def body(*refs):
        ins, outs = refs[:n], refs[n:2 * n]
        send, recv = refs[2 * n:]
        x, y, c = _place()
        cps = []
        for a in range(n):
            h = ins[a].shape[1] // 2
            cp = _rcopy(ins[a].at[:, pl.ds((1 - c) * h, h)], outs[a], send.at[a], recv.at[a], (x, y, 1 - c))
            cp.start()
            cps.append(cp)
        for cp in cps:
            cp.wait()

    return pl.pallas_call(
        body, name="swap_halves", in_specs=[ANY] * n, out_specs=[ANY] * n,
        out_shape=[jax.ShapeDtypeStruct((a.shape[0], a.shape[1] // 2, a.shape[2]), a.dtype) for a in arrs],
        scratch_shapes=[pltpu.SemaphoreType.DMA((n,)), pltpu.SemaphoreType.DMA((n,))])(*arrs)


def _scatter_to_chips(arrs):
    n = len(arrs)

    def body(*refs):
        ins, outs = refs[:n], refs[n:2 * n]
        send, recv, lsem = refs[2 * n:]
        x, y, c = _place()
        me = 2 * x + y
        chips = _other_chips(x, y)
        local = [pltpu.make_async_copy(ins[a].at[me], outs[a].at[me], lsem.at[a]) for a in range(n)]
        for cp in local:
            cp.start()
        cps = []
        for a in range(n):
            for j, (px, py) in enumerate(chips):
                cp = _rcopy(ins[a].at[2 * px + py], outs[a].at[me], send.at[3 * a + j], recv.at[3 * a + j], (px, py, c))
                cp.start()
                cps.append(cp)
        for a in range(n):
            for j, (px, py) in enumerate(chips):
                got = outs[a].at[2 * px + py]
                _rcopy(got, got, send.at[3 * a + j], recv.at[3 * a + j], (px, py, c)).wait_recv()
        for cp in cps:
            cp.wait_send()
        for cp in local:
            cp.wait()

    return pl.pallas_call(
        body, name="scatter_to_chips", in_specs=[ANY] * n, out_specs=[ANY] * n,
        out_shape=[jax.ShapeDtypeStruct(a.shape, a.dtype) for a in arrs],
        scratch_shapes=[pltpu.SemaphoreType.DMA((3 * n,)), pltpu.SemaphoreType.DMA((3 * n,)),
                        pltpu.SemaphoreType.DMA((n,))])(*arrs)


def _join_halves(arrs):
    n = len(arrs)

    def body(*refs):
        ins, outs = refs[:n], refs[n:2 * n]
        send, recv, lsem = refs[2 * n:]
        x, y, c = _place()
        local, cps = [], []
        for a in range(n):
            h = ins[a].shape[0]
            mine = outs[a].at[pl.ds(c * h, h)]
            cp = pltpu.make_async_copy(ins[a], mine, lsem.at[a])
            cp.start()
            local.append(cp)
            cp = _rcopy(ins[a], mine, send.at[a], recv.at[a], (x, y, 1 - c))
            cp.start()
            cps.append(cp)
        for a in range(n):
            h = ins[a].shape[0]
            got = outs[a].at[pl.ds((1 - c) * h, h)]
            _rcopy(got, got, send.at[a], recv.at[a], (x, y, 1 - c)).wait_recv()
        for cp in cps:
            cp.wait_send()
        for cp in local:
            cp.wait()

    return pl.pallas_call(
        body, name="join_halves", in_specs=[ANY] * n, out_specs=[ANY] * n,
        out_shape=[jax.ShapeDtypeStruct((2 * a.shape[0], a.shape[1]), a.dtype) for a in arrs],
        scratch_shapes=[pltpu.SemaphoreType.DMA((n,)), pltpu.SemaphoreType.DMA((n,)),
                        pltpu.SemaphoreType.DMA((n,))])(*arrs)


def _allgather8(name, xs, reduce):
    m_per, n = xs.shape

    def body(x_ref, out_ref, *rest):
        if reduce:
            sum_ref, send, recv, lsem = rest
        else:
            send, recv, lsem = rest
        x, y, c = _place()
        me, sib = (x, y, c), (x, y, 1 - c)
        chips = _other_chips(x, y)

        def rows(px, py, pc):
            return out_ref.at[pl.ds((4 * px + 2 * py + pc) * m_per, m_per), :]

        def copy(k, block, to, src=None):
            return _rcopy(rows(*block) if src is None else src, rows(*block), send.at[k], recv.at[k], to)

        mine = pltpu.make_async_copy(x_ref, rows(*me), lsem)
        mine.start()
        first = [copy(0, me, sib, src=x_ref)]
        first += [copy(1 + j, me, (*chip, c), src=x_ref) for j, chip in enumerate(chips)]
        for cp in first:
            cp.start()
        passed = [copy(4 + j, (*chip, c), sib) for j, chip in enumerate(chips)]
        for j, chip in enumerate(chips):
            copy(1 + j, (*chip, c), me).wait_recv()
            passed[j].start()
        copy(0, sib, me).wait_recv()
        for j, chip in enumerate(chips):
            copy(4 + j, (*chip, 1 - c), me).wait_recv()
        for cp in first + passed:
            cp.wait_send()
        mine.wait()
        if reduce:
            acc = out_ref[pl.ds(0, m_per), :]
            for dev in range(1, N_DEV):
                acc = acc + out_ref[pl.ds(dev * m_per, m_per), :]
            sum_ref[...] = acc

    vm = pl.BlockSpec(memory_space=pltpu.VMEM)
    out_shape = [jax.ShapeDtypeStruct((N_DEV * m_per, n), xs.dtype)]
    if reduce:
        out_shape.append(jax.ShapeDtypeStruct((m_per, n), xs.dtype))
    return pl.pallas_call(
        body, name=name, in_specs=[vm], out_specs=[vm] * len(out_shape), out_shape=out_shape,
        scratch_shapes=[pltpu.SemaphoreType.DMA((7,)), pltpu.SemaphoreType.DMA((7,)), pltpu.SemaphoreType.DMA],
        compiler_params=pltpu.CompilerParams(vmem_limit_bytes=VMEM_LIMIT))(xs)


def _add_my_half(name, a, rb, c_arr):
    s, h, cols = rb.shape
    tr = _tile(h, (512, 352, 256, 128, 64, 32, 16))
    nt = h // tr

    def body(c_ref, a_ref, b_ref, o_ref):
        o_ref[...] = (a_ref[...].astype(F32) + b_ref[...].astype(F32)).astype(o_ref.dtype)

    return pl.pallas_call(
        body, name=name,
        grid_spec=pltpu.PrefetchScalarGridSpec(
            num_scalar_prefetch=1, grid=(s, nt),
            in_specs=[pl.BlockSpec((None, tr, cols), lambda k, i, c: (k, c[0] * nt + i, 0)),
                      pl.BlockSpec((None, tr, cols), lambda k, i, c: (k, i, 0))],
            out_specs=pl.BlockSpec((None, tr, cols), lambda k, i, c: (k, i, 0))),
        out_shape=jax.ShapeDtypeStruct(rb.shape, BF16), compiler_params=_params(2))(c_arr, a, rb)


def _sum_chips(name, q):
    s, h, cols = q.shape
    tr = _tile(h, (512, 352, 256, 128, 64, 32, 16))

    def body(q_ref, o_ref):
        acc = q_ref[0].astype(F32)
        for j in range(1, s):
            acc = acc + q_ref[j].astype(F32)
        o_ref[...] = acc

    return pl.pallas_call(
        body, name=name, grid=(h // tr,), in_specs=[pl.BlockSpec((s, tr, cols), lambda i: (0, i, 0))],
        out_specs=pl.BlockSpec((tr, cols), lambda i: (i, 0)),
        out_shape=jax.ShapeDtypeStruct((h, cols), F32), compiler_params=_params(1))(q)


def _pack_rows(parts):
    rows = []
    for p in parts:
        flat = p.reshape(-1).astype(F32)
        n = _roundup(flat.shape[0], 8 * LANE)
        rows.append(jnp.pad(flat, (0, n - flat.shape[0])).reshape(-1, LANE))
    return jnp.concatenate(rows, axis=0)


def _unpack_rows(buf, shapes):
    out, r = [], 0
    for shp in shapes:
        size = math.prod(shp)
        nr = _roundup(size, 8 * LANE) // LANE
        out.append(buf[r:r + nr].reshape(-1)[:size].reshape(shp))
        r += nr
    return out


def kernel(x, attn_norm, gla_w_in, gla_w_a2, gla_b_a2, gla_head_norm, gla_w_out, kv_norm, w_kv, dsa_w_q, dsa_w_out, ffn_norm, ffn_w_up, ffn_conv_w, ffn_conv_b, ffn_w_down, final_norm, loss_target, m_attn_norm, m_gla_w_in, m_gla_w_a2, m_gla_b_a2, m_gla_head_norm, m_gla_w_out, m_kv_norm, m_w_kv, m_dsa_w_q, m_dsa_w_out, m_ffn_norm, m_ffn_w_up, m_ffn_conv_w, m_ffn_conv_b, m_ffn_w_down, m_final_norm, v_attn_norm, v_gla_w_in, v_gla_w_a2, v_gla_b_a2, v_gla_head_norm, v_gla_w_out, v_kv_norm, v_w_kv, v_dsa_w_q, v_dsa_w_out, v_ffn_norm, v_ffn_w_up, v_ffn_conv_w, v_ffn_conv_b, v_ffn_w_down, v_final_norm):
    lay = _layout()
    d, f = D_MODEL, D_FF
    cx, cy, cc = _place()
    chip = 2 * cx + cy
    c_arr = jnp.reshape(cc, (1,)).astype(jnp.int32)

    wc, wr = _gather_weights([_pack_wc(gla_w_in, w_kv, dsa_w_q, ffn_w_up), _pack_wr(gla_w_out, dsa_w_out, ffn_w_down)])
    w_main, w_a = _unpack_gin(wc)
    sharded_small = [gla_w_a2[0], gla_b_a2[0], gla_head_norm[0], ffn_conv_w]
    gathered = _allgather8("gather_small", _pack_rows(sharded_small), False)[0]
    per_dev = gathered.reshape(N_DEV, -1, LANE)
    shards = [_unpack_rows(per_dev[2 * s], [p.shape for p in sharded_small]) for s in range(N_CHIPS)]
    w_a2, b_a2, head_norm, conv_w = [jnp.concatenate([shards[s][k] for s in range(N_CHIPS)], axis=-1) for k in range(4)]
    norms, small = _small_params(attn_norm, ffn_norm, kv_norm, final_norm, ffn_conv_b, w_a2, b_a2, head_norm, conv_w)

    loss_blk, grad_x, big, sm = _local_step(x[0], loss_target[0], wc, wr, w_main, w_a, norms, small)

    gin_w = lay["gin"][1]
    gin = jnp.concatenate([big["gin_main"], big["gin_a"][:, :GATE_RANK]], axis=1)
    gin = jnp.transpose(gin.reshape(d, N_CHIPS, gin_w), (1, 0, 2))
    gin = jnp.pad(gin, ((0, 0), (0, 0), (0, _roundup(gin_w, LANE) - gin_w)))
    names = ["up0", "up1", "wq", "wkv", "gin", "down0", "down1", "gout", "dout"]
    parts = [gin if k == "gin" else big[k] for k in names]
    theirs = _swap_halves(parts)
    sums = [_add_my_half(f"add_half_{k}", a, b, c_arr) for k, a, b in zip(names, parts, theirs)]
    landed = _scatter_to_chips(sums)
    halves = [_sum_chips(f"sum_chips_{k}", q) for k, q in zip(names, landed)]
    full = dict(zip(names, _join_halves(halves)))

    small_parts = [loss_blk, jnp.concatenate([sm["attn0"], sm["attn1"]]), jnp.concatenate([sm["ffn0"], sm["ffn1"]]),
                   sm["kv"], sm["final"], jnp.concatenate([sm["conv_b0"], sm["conv_b1"]]),
                   sm["w_a2p"][:GATE_RANK], sm["b_a2"], sm["head_norm"],
                   jnp.stack([jnp.concatenate(sm["conv_w0"]), jnp.concatenate(sm["conv_w1"])])]
    small_shapes = [(8, LANE), (2, d), (2, d), (d,), (d,), (2, f), (GATE_RANK, GLA_KEY_DIM), (GLA_KEY_DIM,),
                    (GLA_VAL_DIM // GLA_HEADS,), (2, 3, f)]
    _, reduced = _allgather8("reduce_small", _pack_rows(small_parts), True)
    loss_r, g_attn, g_ffn, g_kv, g_final, g_cb, g_a2, g_ba2, g_hn, g_cw = _unpack_rows(reduced, small_shapes)
    loss = loss_r[0, 0]

    def mine(g, axis):
        w = g.shape[axis] // N_CHIPS
        return lax.dynamic_slice_in_dim(g, chip * w, w, axis)

    grads = {
        "attn_norm": g_attn, "gla_w_in": full["gin"][None, :, :gin_w], "gla_w_a2": mine(g_a2, 1)[None],
        "gla_b_a2": mine(g_ba2, 0)[None], "gla_head_norm": mine(g_hn, 0)[None], "gla_w_out": full["gout"][None],
        "kv_norm": g_kv, "w_kv": full["wkv"], "dsa_w_q": full["wq"][None], "dsa_w_out": full["dout"][None],
        "ffn_norm": g_ffn, "ffn_w_up": jnp.stack([full["up0"], full["up1"]]), "ffn_conv_w": mine(g_cw, 2),
        "ffn_conv_b": g_cb, "ffn_w_down": jnp.stack([full["down0"], full["down1"]]), "final_norm": g_final,
    }
    weights = {"attn_norm": (attn_norm, m_attn_norm, v_attn_norm), "gla_w_in": (gla_w_in, m_gla_w_in, v_gla_w_in),
               "gla_w_a2": (gla_w_a2, m_gla_w_a2, v_gla_w_a2), "gla_b_a2": (gla_b_a2, m_gla_b_a2, v_gla_b_a2),
               "gla_head_norm": (gla_head_norm, m_gla_head_norm, v_gla_head_norm),
               "gla_w_out": (gla_w_out, m_gla_w_out, v_gla_w_out), "kv_norm": (kv_norm, m_kv_norm, v_kv_norm),
               "w_kv": (w_kv, m_w_kv, v_w_kv), "dsa_w_q": (dsa_w_q, m_dsa_w_q, v_dsa_w_q),
               "dsa_w_out": (dsa_w_out, m_dsa_w_out, v_dsa_w_out), "ffn_norm": (ffn_norm, m_ffn_norm, v_ffn_norm),
               "ffn_w_up": (ffn_w_up, m_ffn_w_up, v_ffn_w_up), "ffn_conv_w": (ffn_conv_w, m_ffn_conv_w, v_ffn_conv_w),
               "ffn_conv_b": (ffn_conv_b, m_ffn_conv_b, v_ffn_conv_b),
               "ffn_w_down": (ffn_w_down, m_ffn_w_down, v_ffn_w_down), "final_norm": (final_norm, m_final_norm, v_final_norm)}
    order = list(weights)
    big_names = ("gla_w_in", "gla_w_out", "w_kv", "dsa_w_q", "dsa_w_out", "ffn_w_up", "ffn_w_down")
    delta, new_m, new_v = {}, {}, {}
    for k in big_names:
        w, m, v = weights[k]
        cols = w.shape[-1]
        res = _adamw(f"adamw_{k}", w.reshape(-1, cols), grads[k].reshape(-1, cols), m.reshape(-1, cols), v.reshape(-1, cols))
        delta[k], new_m[k], new_v[k] = [r.reshape(w.shape) for r in res]
    small_names = [k for k in order if k not in big_names]
    packed = [_pack_rows([src[k] for k in small_names])
              for src in ({k: weights[k][0] for k in small_names}, grads, {k: weights[k][1] for k in small_names},
                          {k: weights[k][2] for k in small_names})]
    res = _adamw("adamw_small", *packed)
    shapes = [weights[k][0].shape for k in small_names]
    for dst, buf in zip((delta, new_m, new_v), res):
        for k, val in zip(small_names, _unpack_rows(buf, shapes)):
            dst[k] = val
    return (loss, grad_x[None], *[grads[k] for k in order], *[delta[k] for k in order], *[new_m[k] for k in order],
            *[new_v[k] for k in order])
```

```python
import math

import jax
import jax.numpy as jnp
from jax import lax
from jax.experimental import pallas as pl
from jax.experimental.pallas import tpu as pltpu

F32 = jnp.float32
BF16 = jnp.bfloat16

D_MODEL = 2048
SEQ = 4096
GLA_HEADS = 4
GLA_KEY_DIM = D_MODEL // 2
GLA_VAL_DIM = D_MODEL
GATE_RANK = 16
GATE_NORMALIZER = 16.0
GLA_CHUNK = 64
ATT_HEADS = 16
HEAD_DIM = 128
WINDOWS = (128, 512, 2048)
DILATIONS = (1, 4, 16)
ATT_BLOCK = 128
D_FF = 5632
EPS = 1e-6
ADAM_LR = 0.001
ADAM_B1 = 0.9
ADAM_B2 = 0.999
ADAM_EPS = 1e-08
ADAM_WD = 0.01
ADAM_STEP = 10

N_CHIPS = 4
N_DEV = 8
LANE = 128
A_PAD = 128
VMEM_LIMIT = 56 * 1024 * 1024
NEG = -1e30
MESH = pl.DeviceIdType.MESH

NN = (((1,), (0,)), ((), ()))
NT = (((1,), (1,)), ((), ()))
TN = (((0,), (0,)), ((), ()))


def _tile(n, cands):
    for c in cands:
        if c <= n and n % c == 0:
            return c
    return n


def _roundup(n, m):
    return -(-n // m) * m


def _params(n_axes):
    return pltpu.CompilerParams(dimension_semantics=("arbitrary",) * n_axes, vmem_limit_bytes=VMEM_LIMIT)


def _dot(a, b, dims):
    return lax.dot_general(a, b, dims, preferred_element_type=F32)


def _sigmoid(x):
    return 1.0 / (1.0 + jnp.exp(-x))


def _layout():
    d, f = D_MODEL, D_FF
    hd = ATT_HEADS * HEAD_DIM
    gin = 2 * GLA_KEY_DIM + 2 * GLA_VAL_DIM + GATE_RANK
    lay = {}
    up_w = 2 * f // N_CHIPS
    up_t = _tile(up_w, (1408, 1024, 512, 256, 128))
    q_w = 3 * hd // N_CHIPS
    q_t = _tile(q_w, (512, 384, 256, 128))
    kv_w = 2 * hd // N_CHIPS
    kv_t = _tile(kv_w, (1024, 512, 256, 128))
    gin_w = gin // N_CHIPS
    cur = 0
    for name, w, t in (("up0", up_w, up_t), ("up1", up_w, up_t), ("wq", q_w, q_t), ("wkv", kv_w, kv_t),
                       ("gin", gin_w, LANE)):
        off = _roundup(cur, t)
        lay[name] = (off, w, t)
        cur = off + w
    lay["wc_cols"] = _roundup(cur, LANE)
    dn_r = f // N_CHIPS
    dn_t = _tile(dn_r, (1408, 1024, 512, 256, 128))
    go_r = GLA_VAL_DIM // N_CHIPS
    go_t = _tile(go_r, (512, 256, 128))
    do_r = hd // N_CHIPS
    do_t = _tile(do_r, (512, 256, 128))
    cur = 0
    for name, r, t in (("down0", dn_r, dn_t), ("down1", dn_r, dn_t), ("gout", go_r, go_t), ("dout", do_r, do_t)):
        off = _roundup(cur, t)
        lay[name] = (off, r, t)
        cur = off + r
    lay["wr_rows"] = _roundup(cur, 32)
    return lay


def _matmul(name, a, b, dims, grid, a_spec, b_spec, o_spec, out_shape, acc_shape, add=None, add_spec=None):
    nk = grid[2]
    has_add = add is not None

    def body(*refs):
        a_ref, b_ref = refs[0], refs[1]
        pos = 2
        add_ref = None
        if has_add:
            add_ref = refs[pos]
            pos += 1
        o_ref = refs[pos]
        prod = _dot(a_ref[...].astype(BF16), b_ref[...].astype(BF16), dims)

        def finish(val):
            if has_add:
                val = val + add_ref[...].astype(F32)
            o_ref[...] = val.astype(o_ref.dtype)

        if nk == 1:
            finish(prod)
        else:
            acc_ref = refs[pos + 1]
            k = pl.program_id(2)

            @pl.when(k == 0)
            def _():
                acc_ref[...] = prod

            @pl.when(k > 0)
            def _():
                acc_ref[...] += prod

            @pl.when(k == nk - 1)
            def _():
                finish(acc_ref[...])

    in_specs = [a_spec, b_spec]
    args = [a, b]
    if has_add:
        in_specs.append(add_spec)
        args.append(add)
    scratch = [] if nk == 1 else [pltpu.VMEM(acc_shape, F32)]
    return pl.pallas_call(body, name=name, grid=grid, in_specs=in_specs, out_specs=o_spec, out_shape=out_shape,
                          scratch_shapes=scratch, compiler_params=_params(3))(*args)


def _mm_act_wc(name, a, wc, seg, out_dtype):
    off, w, tn = seg
    t_len, d = a.shape
    tm = _tile(t_len, (1024, 512, 256, 128))
    nps = w // tn
    ob = off // tn
    grid = (t_len // tm, N_CHIPS * nps, 1)
    return _matmul(
        name, a, wc, NN, grid,
        pl.BlockSpec((tm, d), lambda i, j, k: (i, 0)),
        pl.BlockSpec((None, d, tn), lambda i, j, k: (j // nps, 0, ob + j % nps)),
        pl.BlockSpec((tm, tn), lambda i, j, k: (i, j)),
        jax.ShapeDtypeStruct((t_len, N_CHIPS * w), out_dtype), (tm, tn))


def _mm_dact_wcT(name, dy, wc, seg, add=None):
    off, w, tk = seg
    t_len = dy.shape[0]
    d = wc.shape[1]
    tm = _tile(t_len, (1024, 512, 256, 128))
    tn = _tile(d, (1024, 512, 256, 128))
    kps = w // tk
    ob = off // tk
    grid = (t_len // tm, d // tn, N_CHIPS * kps)
    return _matmul(
        name, dy, wc, NT, grid,
        pl.BlockSpec((tm, tk), lambda i, j, k: (i, k)),
        pl.BlockSpec((None, tn, tk), lambda i, j, k: (k // kps, j, ob + k % kps)),
        pl.BlockSpec((tm, tn), lambda i, j, k: (i, j)),
        jax.ShapeDtypeStruct((t_len, d), F32), (tm, tn),
        add=add, add_spec=None if add is None else pl.BlockSpec((tm, tn), lambda i, j, k: (i, j)))


def _mm_grad_wc(name, a, dy, seg):
    _, w, tn = seg
    t_len, d = a.shape
    tm = _tile(d, (1024, 512, 256, 128))
    tk = _tile(t_len, (1024, 512, 256, 128))
    nps = w // tn
    grid = (d // tm, N_CHIPS * nps, t_len // tk)
    return _matmul(
        name, a, dy, TN, grid,
        pl.BlockSpec((tk, tm), lambda i, j, k: (k, i)),
        pl.BlockSpec((tk, tn), lambda i, j, k: (k, j)),
        pl.BlockSpec((None, tm, tn), lambda i, j, k: (j // nps, i, j % nps)),
        jax.ShapeDtypeStruct((N_CHIPS, d, w), BF16), (tm, tn))


def _mm_act_wr(name, a, wr, seg, add):
    off, r, tk = seg
    t_len = a.shape[0]
    d = wr.shape[2]
    tm = _tile(t_len, (1024, 512, 256, 128))
    tn = _tile(d, (1024, 512, 256, 128))
    kps = r // tk
    ob = off // tk
    grid = (t_len // tm, d // tn, N_CHIPS * kps)
    return _matmul(
        name, a, wr, NN, grid,
        pl.BlockSpec((tm, tk), lambda i, j, k: (i, k)),
        pl.BlockSpec((None, tk, tn), lambda i, j, k: (k // kps, ob + k % kps, j)),
        pl.BlockSpec((tm, tn), lambda i, j, k: (i, j)),
        jax.ShapeDtypeStruct((t_len, d), F32), (tm, tn),
        add=add, add_spec=pl.BlockSpec((tm, tn), lambda i, j, k: (i, j)))


def _mm_dact_wrT(name, dh, wr, seg):
    off, r, tn = seg
    t_len, d = dh.shape
    tm = _tile(t_len, (1024, 512, 256, 128))
    nps = r // tn
    ob = off // tn
    grid = (t_len // tm, N_CHIPS * nps, 1)
    return _matmul(
        name, dh, wr, NT, grid,
        pl.BlockSpec((tm, d), lambda i, j, k: (i, 0)),
        pl.BlockSpec((None, tn, d), lambda i, j, k: (j // nps, ob + j % nps, 0)),
        pl.BlockSpec((tm, tn), lambda i, j, k: (i, j)),
        jax.ShapeDtypeStruct((t_len, N_CHIPS * r), BF16), (tm, tn))


def _mm_grad_wr(name, a, dh, seg):
    _, r, tm = seg
    t_len, d = dh.shape
    tn = _tile(d, (1024, 512, 256, 128))
    tk = _tile(t_len, (1024, 512, 256, 128))
    mps = r // tm
    grid = (N_CHIPS * mps, d // tn, t_len // tk)
    return _matmul(
        name, a, dh, TN, grid,
        pl.BlockSpec((tk, tm), lambda i, j, k: (k, i)),
        pl.BlockSpec((tk, tn), lambda i, j, k: (k, j)),
        pl.BlockSpec((None, tm, tn), lambda i, j, k: (i // mps, i % mps, j)),
        jax.ShapeDtypeStruct((N_CHIPS, r, d), BF16), (tm, tn))


def _mm_plain(name, a, b, dims, out_dtype, add=None):
    if dims == NN:
        m, kd = a.shape
        n = b.shape[1]
    elif dims == NT:
        m, kd = a.shape
        n = b.shape[0]
    else:
        kd, m = a.shape
        n = b.shape[1]
    tm = _tile(m, (1024, 512, 256, 128))
    tn = _tile(n, (1024, 768, 512, 256, 128))
    tk = _tile(kd, (2048, 1024, 512, 256, 128))
    grid = (m // tm, n // tn, kd // tk)
    if dims == NN:
        a_spec = pl.BlockSpec((tm, tk), lambda i, j, k: (i, k))
        b_spec = pl.BlockSpec((tk, tn), lambda i, j, k: (k, j))
    elif dims == NT:
        a_spec = pl.BlockSpec((tm, tk), lambda i, j, k: (i, k))
        b_spec = pl.BlockSpec((tn, tk), lambda i, j, k: (j, k))
    else:
        a_spec = pl.BlockSpec((tk, tm), lambda i, j, k: (k, i))
        b_spec = pl.BlockSpec((tk, tn), lambda i, j, k: (k, j))
    o_spec = pl.BlockSpec((tm, tn), lambda i, j, k: (i, j))
    return _matmul(name, a, b, dims, grid, a_spec, b_spec, o_spec, jax.ShapeDtypeStruct((m, n), out_dtype), (tm, tn),
                   add=add, add_spec=None if add is None else o_spec)


def _rms_fwd(name, x, g):
    t_len, d = x.shape
    tm = _tile(t_len, (512, 256, 128))

    def body(x_ref, g_ref, o_ref):
        xv = x_ref[...]
        r = lax.rsqrt(jnp.mean(xv * xv, axis=-1, keepdims=True) + EPS)
        o_ref[...] = (xv * r * g_ref[...]).astype(o_ref.dtype)

    return pl.pallas_call(
        body, name=name, grid=(t_len // tm,),
        in_specs=[pl.BlockSpec((tm, d), lambda i: (i, 0)), pl.BlockSpec((1, d), lambda i: (0, 0))],
        out_specs=pl.BlockSpec((tm, d), lambda i: (i, 0)),
        out_shape=jax.ShapeDtypeStruct((t_len, d), BF16), compiler_params=_params(1))(x, g)


def _rms_bwd(name, dy, x, g, dres):
    t_len, d = x.shape
    tm = _tile(t_len, (256, 128))

    def body(dy_ref, x_ref, g_ref, dres_ref, dx_ref, dg_ref):
        xv = x_ref[...]
        r = lax.rsqrt(jnp.mean(xv * xv, axis=-1, keepdims=True) + EPS)
        xhat = xv * r
        dyv = dy_ref[...].astype(F32)
        dxn = dyv * g_ref[...]
        dx = r * (dxn - xhat * jnp.mean(dxn * xhat, axis=-1, keepdims=True))
        dx_ref[...] = dres_ref[...] + dx
        part = jnp.sum(dyv * xhat, axis=0, keepdims=True)

        @pl.when(pl.program_id(0) == 0)
        def _():
            dg_ref[...] = part

        @pl.when(pl.program_id(0) > 0)
        def _():
            dg_ref[...] += part

    row = pl.BlockSpec((tm, d), lambda i: (i, 0))
    vec = pl.BlockSpec((1, d), lambda i: (0, 0))
    return pl.pallas_call(
        body, name=name, grid=(t_len // tm,), in_specs=[row, row, vec, row], out_specs=(row, vec),
        out_shape=(jax.ShapeDtypeStruct((t_len, d), F32), jax.ShapeDtypeStruct((1, d), F32)),
        compiler_params=_params(1))(dy, x, g, dres)


def _loss_head(h, g, target):
    t_len, d = h.shape
    tm = _tile(t_len, (256, 128))

    def body(h_ref, g_ref, t_ref, dh_ref, dg_ref, loss_ref):
        xv = h_ref[...]
        gv = g_ref[...]
        r = lax.rsqrt(jnp.mean(xv * xv, axis=-1, keepdims=True) + EPS)
        xhat = xv * r
        err = xhat * gv - t_ref[...]
        dyv = err * (1.0 / d)
        dxn = dyv * gv
        dh_ref[...] = r * (dxn - xhat * jnp.mean(dxn * xhat, axis=-1, keepdims=True))
        part = jnp.sum(dyv * xhat, axis=0, keepdims=True)
        lpart = jnp.zeros((8, LANE), F32) + (0.5 / d) * jnp.sum(err * err)

        @pl.when(pl.program_id(0) == 0)
        def _():
            dg_ref[...] = part
            loss_ref[...] = lpart

        @pl.when(pl.program_id(0) > 0)
        def _():
            dg_ref[...] += part
            loss_ref[...] += lpart

    row = pl.BlockSpec((tm, d), lambda i: (i, 0))
    vec = pl.BlockSpec((1, d), lambda i: (0, 0))
    return pl.pallas_call(
        body, name="loss_head", grid=(t_len // tm,), in_specs=[row, vec, row],
        out_specs=(row, vec, pl.BlockSpec((8, LANE), lambda i: (0, 0))),
        out_shape=(jax.ShapeDtypeStruct((t_len, d), F32), jax.ShapeDtypeStruct((1, d), F32),
                   jax.ShapeDtypeStruct((8, LANE), F32)),
        compiler_params=_params(1))(h, g, target)


def _chunk_row(shape):
    return lax.broadcasted_iota(jnp.int32, shape, 0) % GLA_CHUNK


def _gla_gate_fwd(a, w_a2p, b_a2):
    t_len = a.shape[0]
    kd = w_a2p.shape[1]
    tm = _tile(t_len, (256, 128, 64))

    def body(a_ref, w_ref, b_ref, ga_ref, cum_ref):
        ga = _dot(a_ref[...], w_ref[...].astype(BF16), NN) + b_ref[...]
        ga_ref[...] = ga
        la = (jnp.minimum(ga, 0.0) - jnp.log(1.0 + jnp.exp(-jnp.abs(ga)))) * (1.0 / GATE_NORMALIZER)
        row = _chunk_row(la.shape)
        s = 1
        while s < GLA_CHUNK:
            la = la + jnp.where(row >= s, pltpu.roll(la, s, 0), 0.0)
            s *= 2
        cum_ref[...] = la

    return pl.pallas_call(
        body, name="gla_gate_fwd", grid=(t_len // tm,),
        in_specs=[pl.BlockSpec((tm, A_PAD), lambda i: (i, 0)), pl.BlockSpec((A_PAD, kd), lambda i: (0, 0)),
                  pl.BlockSpec((1, kd), lambda i: (0, 0))],
        out_specs=(pl.BlockSpec((tm, kd), lambda i: (i, 0)), pl.BlockSpec((tm, kd), lambda i: (i, 0))),
        out_shape=(jax.ShapeDtypeStruct((t_len, kd), F32), jax.ShapeDtypeStruct((t_len, kd), F32)),
        compiler_params=_params(1))(a, w_a2p, b_a2)


def _gla_gate_bwd(dcum, ga, a, w_a2p):
    t_len, kd = dcum.shape
    tm = _tile(t_len, (256, 128, 64))

    def body(dc_ref, ga_ref, a_ref, w_ref, da_ref, dw_ref, db_ref):
        x = dc_ref[...]
        row = _chunk_row(x.shape)
        s = 1
        while s < GLA_CHUNK:
            x = x + jnp.where(row < GLA_CHUNK - s, pltpu.roll(x, tm - s, 0), 0.0)
            s *= 2
        dga = x * (1.0 / GATE_NORMALIZER) * _sigmoid(-ga_ref[...])
        dgab = dga.astype(BF16)
        da_ref[...] = _dot(dgab, w_ref[...].astype(BF16), NT).astype(da_ref.dtype)
        dw = _dot(a_ref[...], dgab, TN)
        db = jnp.sum(dga, axis=0, keepdims=True)

        @pl.when(pl.program_id(0) == 0)
        def _():
            dw_ref[...] = dw
            db_ref[...] = db

        @pl.when(pl.program_id(0) > 0)
        def _():
            dw_ref[...] += dw
            db_ref[...] += db

    wide = pl.BlockSpec((tm, kd), lambda i: (i, 0))
    return pl.pallas_call(
        body, name="gla_gate_bwd", grid=(t_len // tm,),
        in_specs=[wide, wide, pl.BlockSpec((tm, A_PAD), lambda i: (i, 0)), pl.BlockSpec((A_PAD, kd), lambda i: (0, 0))],
        out_specs=(pl.BlockSpec((tm, A_PAD), lambda i: (i, 0)), pl.BlockSpec((A_PAD, kd), lambda i: (0, 0)),
                   pl.BlockSpec((1, kd), lambda i: (0, 0))),
        out_shape=(jax.ShapeDtypeStruct((t_len, A_PAD), BF16), jax.ShapeDtypeStruct((A_PAD, kd), F32),
                   jax.ShapeDtypeStruct((1, kd), F32)),
        compiler_params=_params(1))(dcum, ga, a, w_a2p)


def _gla_dims():
    dk = GLA_KEY_DIM // GLA_HEADS
    dv = GLA_VAL_DIM // GLA_HEADS
    return dk, dv


def _gla_fwd(proj, cum):
    t_len = proj.shape[0]
    dk, dv = _gla_dims()
    nc = t_len // GLA_CHUNK
    c = GLA_CHUNK
    scale = dk ** -0.5
    v0 = 2 * GLA_KEY_DIM // dv

    def body(q_ref, k_ref, v_ref, cum_ref, o_ref, st_ref, s_scr):
        @pl.when(pl.program_id(1) == 0)
        def _():
            s_scr[...] = jnp.zeros_like(s_scr)

        cm = cum_ref[...]
        last = cm[c - 1:c, :]
        q = q_ref[...].astype(F32) * scale
        k = k_ref[...].astype(F32)
        v = v_ref[...].astype(BF16)
        qd = (q * jnp.exp(cm)).astype(BF16)
        ki = (k * jnp.exp(-cm)).astype(BF16)
        ke = (k * jnp.exp(last - cm)).astype(BF16)
        tri = lax.broadcasted_iota(jnp.int32, (c, c), 0) >= lax.broadcasted_iota(jnp.int32, (c, c), 1)
        sc = jnp.where(tri, _dot(qd, ki, NT), 0.0)
        st = s_scr[...]
        st_ref[...] = st
        o_ref[...] = _dot(sc.astype(BF16), v, NN) + _dot(qd, st.astype(BF16), NT)
        s_scr[...] = st * jnp.exp(last) + _dot(v, ke, TN)

    return pl.pallas_call(
        body, name="gla_fwd", grid=(GLA_HEADS, nc),
        in_specs=[pl.BlockSpec((c, dk), lambda h, n: (n, h)),
                  pl.BlockSpec((c, dk), lambda h, n: (n, GLA_HEADS + h)),
                  pl.BlockSpec((c, dv), lambda h, n: (n, v0 + h)),
                  pl.BlockSpec((c, dk), lambda h, n: (n, h))],
        out_specs=(pl.BlockSpec((c, dv), lambda h, n: (n, h)),
                   pl.BlockSpec((None, None, dv, dk), lambda h, n: (h, n, 0, 0))),
        out_shape=(jax.ShapeDtypeStruct((t_len, GLA_VAL_DIM), F32),
                   jax.ShapeDtypeStruct((GLA_HEADS, nc, dv, dk), F32)),
        scratch_shapes=[pltpu.VMEM((dv, dk), F32)], compiler_params=_params(2))(proj, proj, proj, cum)


def _gla_bwd(proj, cum, states, do):
    t_len = proj.shape[0]
    dk, dv = _gla_dims()
    nc = t_len // GLA_CHUNK
    c = GLA_CHUNK
    scale = dk ** -0.5
    v0 = 2 * GLA_KEY_DIM // dv

    def body(q_ref, k_ref, v_ref, cum_ref, st_ref, do_ref, dq_ref, dk_ref, dv_ref, dc_ref, ds_scr):
        @pl.when(pl.program_id(1) == 0)
        def _():
            ds_scr[...] = jnp.zeros_like(ds_scr)

        cm = cum_ref[...]
        last = cm[c - 1:c, :]
        e_c = jnp.exp(cm)
        e_nc = jnp.exp(-cm)
        e_lc = jnp.exp(last - cm)
        e_l = jnp.exp(last)
        q = q_ref[...].astype(F32) * scale
        k = k_ref[...].astype(F32)
        v = v_ref[...].astype(BF16)
        dov = do_ref[...]
        qd32 = q * e_c
        ki32 = k * e_nc
        ke32 = k * e_lc
        qd = qd32.astype(BF16)
        ki = ki32.astype(BF16)
        ke = ke32.astype(BF16)
        st = st_ref[...]
        dst = ds_scr[...]
        dstb = dst.astype(BF16)
        tri = lax.broadcasted_iota(jnp.int32, (c, c), 0) >= lax.broadcasted_iota(jnp.int32, (c, c), 1)
        am = jnp.where(tri, _dot(dov, v, NT), 0.0).astype(BF16)
        pm = jnp.where(tri, _dot(qd, ki, NT), 0.0).astype(BF16)
        dqd = _dot(am, ki, NN) + _dot(dov, st.astype(BF16), NN)
        dki = _dot(am, qd, TN)
        dvv = _dot(pm, dov, TN) + _dot(ke, dstb, NT)
        dke = _dot(v, dstb, NN)
        d_el = jnp.sum(dst * st, axis=0, keepdims=True)
        ds_scr[...] = dst * e_l + _dot(dov, qd, TN)
        dq_ref[...] = (dqd * scale * e_c).astype(dq_ref.dtype)
        dk_ref[...] = (dki * e_nc + dke * e_lc).astype(dk_ref.dtype)
        dv_ref[...] = dvv.astype(dv_ref.dtype)
        dkeke = dke * ke32
        dcum = dqd * qd32 - dki * ki32 - dkeke
        dlast = jnp.sum(dkeke, axis=0, keepdims=True) + d_el * e_l
        row = lax.broadcasted_iota(jnp.int32, dcum.shape, 0)
        dc_ref[...] = jnp.where(row == c - 1, dcum + dlast, dcum)

    rev = nc - 1
    return pl.pallas_call(
        body, name="gla_bwd", grid=(GLA_HEADS, nc),
        in_specs=[pl.BlockSpec((c, dk), lambda h, n: (rev - n, h)),
                  pl.BlockSpec((c, dk), lambda h, n: (rev - n, GLA_HEADS + h)),
                  pl.BlockSpec((c, dv), lambda h, n: (rev - n, v0 + h)),
                  pl.BlockSpec((c, dk), lambda h, n: (rev - n, h)),
                  pl.BlockSpec((None, None, dv, dk), lambda h, n: (h, rev - n, 0, 0)),
                  pl.BlockSpec((c, dv), lambda h, n: (rev - n, h))],
        out_specs=(pl.BlockSpec((c, dk), lambda h, n: (rev - n, h)),
                   pl.BlockSpec((c, dk), lambda h, n: (rev - n, h)),
                   pl.BlockSpec((c, dv), lambda h, n: (rev - n, h)),
                   pl.BlockSpec((c, dk), lambda h, n: (rev - n, h))),
        out_shape=(jax.ShapeDtypeStruct((t_len, GLA_KEY_DIM), BF16), jax.ShapeDtypeStruct((t_len, GLA_KEY_DIM), BF16),
                   jax.ShapeDtypeStruct((t_len, GLA_VAL_DIM), BF16), jax.ShapeDtypeStruct((t_len, GLA_KEY_DIM), F32)),
        scratch_shapes=[pltpu.VMEM((dv, dk), F32)], compiler_params=_params(2))(proj, proj, proj, cum, states, do)


def _gla_out_fwd(o, proj, gn):
    t_len = o.shape[0]
    _, dv = _gla_dims()
    tm = _tile(t_len, (512, 256, 128))
    r0 = (2 * GLA_KEY_DIM + GLA_VAL_DIM) // dv

    def body(o_ref, r_ref, g_ref, y_ref):
        ov = o_ref[...]
        rs = lax.rsqrt(jnp.mean(ov * ov, axis=-1, keepdims=True) + EPS)
        rv = r_ref[...].astype(F32)
        y_ref[...] = (ov * rs * g_ref[...] * (rv * _sigmoid(rv))).astype(y_ref.dtype)

    return pl.pallas_call(
        body, name="gla_out_fwd", grid=(t_len // tm, GLA_HEADS),
        in_specs=[pl.BlockSpec((tm, dv), lambda i, h: (i, h)), pl.BlockSpec((tm, dv), lambda i, h: (i, r0 + h)),
                  pl.BlockSpec((1, dv), lambda i, h: (0, 0))],
        out_specs=pl.BlockSpec((tm, dv), lambda i, h: (i, h)),
        out_shape=jax.ShapeDtypeStruct((t_len, GLA_VAL_DIM), BF16), compiler_params=_params(2))(o, proj, gn)


def _gla_out_bwd(dy, o, proj, gn):
    t_len = o.shape[0]
    _, dv = _gla_dims()
    tm = _tile(t_len, (512, 256, 128))
    r0 = (2 * GLA_KEY_DIM + GLA_VAL_DIM) // dv

    def body(dy_ref, o_ref, r_ref, g_ref, do_ref, dr_ref, dg_ref):
        ov = o_ref[...]
        gv = g_ref[...]
        rs = lax.rsqrt(jnp.mean(ov * ov, axis=-1, keepdims=True) + EPS)
        xhat = ov * rs
        rv = r_ref[...].astype(F32)
        sg = _sigmoid(rv)
        gate = rv * sg
        dyv = dy_ref[...].astype(F32)
        dn = dyv * gate
        dr_ref[...] = (dyv * xhat * gv * (sg * (1.0 + rv * (1.0 - sg)))).astype(dr_ref.dtype)
        dxn = dn * gv
        do_ref[...] = (rs * (dxn - xhat * jnp.mean(dxn * xhat, axis=-1, keepdims=True))).astype(do_ref.dtype)
        part = jnp.sum(dn * xhat, axis=0, keepdims=True)
        first = (pl.program_id(0) == 0) & (pl.program_id(1) == 0)

        @pl.when(first)
        def _():
            dg_ref[...] = part

        @pl.when(jnp.logical_not(first))
        def _():
            dg_ref[...] += part

    blk = pl.BlockSpec((tm, dv), lambda i, h: (i, h))
    return pl.pallas_call(
        body, name="gla_out_bwd", grid=(t_len // tm, GLA_HEADS),
        in_specs=[blk, blk, pl.BlockSpec((tm, dv), lambda i, h: (i, r0 + h)), pl.BlockSpec((1, dv), lambda i, h: (0, 0))],
        out_specs=(blk, blk, pl.BlockSpec((1, dv), lambda i, h: (0, 0))),
        out_shape=(jax.ShapeDtypeStruct((t_len, GLA_VAL_DIM), BF16), jax.ShapeDtypeStruct((t_len, GLA_VAL_DIM), BF16),
                   jax.ShapeDtypeStruct((1, dv), F32)),
        compiler_params=_params(2))(dy, o, proj, gn)


def _alibi_slopes():
    n = ATT_HEADS
    start = 2.0 ** (-8.0 / n)
    return [start ** (i + 1) for i in range(n)]


def _att_masks(d):
    b = ATT_BLOCK
    qa = lax.broadcasted_iota(jnp.int32, (b, b), 0)
    kb = lax.broadcasted_iota(jnp.int32, (b, b), 1)
    dist_c = qa - kb
    dist_p = qa - kb + b
    return dist_c >= 0, dist_p <= b, (dist_c * d).astype(F32), (dist_p * d).astype(F32)


def _att_fwd(q_all, kv, g):
    d = DILATIONS[g]
    assert WINDOWS[g] // d == ATT_BLOCK
    t_len = q_all.shape[0]
    hd = ATT_HEADS * HEAD_DIM
    sub = t_len // d
    nb = sub // ATT_BLOCK
    b = ATT_BLOCK
    e = HEAD_DIM
    scale = e ** -0.5
    slopes = _alibi_slopes()
    qv = q_all.reshape(sub, d * 3 * hd)
    kvv = kv.reshape(sub, d * 2 * hd)

    def body(q_ref, kp_ref, kc_ref, vp_ref, vc_ref, o_ref, l_ref):
        ib = pl.program_id(1)
        valid_c, valid_p0, dist_c, dist_p = _att_masks(d)
        valid_p = valid_p0 & (ib > 0)
        for h in range(ATT_HEADS):
            hs = slice(h * e, (h + 1) * e)
            qh = q_ref[:, hs]
            s_c = jnp.where(valid_c, _dot(qh, kc_ref[:, hs], NT) * scale - slopes[h] * dist_c, NEG)
            s_p = jnp.where(valid_p, _dot(qh, kp_ref[:, hs], NT) * scale - slopes[h] * dist_p, NEG)
            m = jnp.maximum(jnp.max(s_c, axis=1, keepdims=True), jnp.max(s_p, axis=1, keepdims=True))
            p_c = jnp.where(valid_c, jnp.exp(s_c - m), 0.0)
            p_p = jnp.where(valid_p, jnp.exp(s_p - m), 0.0)
            l = jnp.sum(p_c, axis=1, keepdims=True) + jnp.sum(p_p, axis=1, keepdims=True)
            acc = _dot(p_c.astype(BF16), vc_ref[:, hs], NN) + _dot(p_p.astype(BF16), vp_ref[:, hs], NN)
            o_ref[:, hs] = acc / l
            l_ref[:, hs] = jnp.broadcast_to(m + jnp.log(l), (b, e))

    blk = (b, hd)
    o, lse = pl.pallas_call(
        body, name=f"att_fwd{g}", grid=(d, nb),
        in_specs=[pl.BlockSpec(blk, lambda r, i: (i, 3 * r + g)),
                  pl.BlockSpec(blk, lambda r, i: (jnp.maximum(i - 1, 0), 2 * r)),
                  pl.BlockSpec(blk, lambda r, i: (i, 2 * r)),
                  pl.BlockSpec(blk, lambda r, i: (jnp.maximum(i - 1, 0), 2 * r + 1)),
                  pl.BlockSpec(blk, lambda r, i: (i, 2 * r + 1))],
        out_specs=(pl.BlockSpec(blk, lambda r, i: (i, r)), pl.BlockSpec(blk, lambda r, i: (i, r))),
        out_shape=(jax.ShapeDtypeStruct((sub, d * hd), F32), jax.ShapeDtypeStruct((sub, d * hd), F32)),
        compiler_params=_params(2))(qv, kvv, kvv, kvv, kvv)
    return o.reshape(t_len, hd), lse.reshape(t_len, hd)


def _att_merge(os, ls):
    t_len, hd = os[0].shape
    tm = _tile(t_len, (256, 128))

    def body(o0, o1, o2, l0, l1, l2, of_ref, ob_ref, l_ref):
        a0, a1, a2 = l0[...], l1[...], l2[...]
        m = jnp.maximum(jnp.maximum(a0, a1), a2)
        e0, e1, e2 = jnp.exp(a0 - m), jnp.exp(a1 - m), jnp.exp(a2 - m)
        den = e0 + e1 + e2
        o = (e0 * o0[...] + e1 * o1[...] + e2 * o2[...]) / den
        of_ref[...] = o
        ob_ref[...] = o.astype(ob_ref.dtype)
        l_ref[...] = m + jnp.log(den)

    row = pl.BlockSpec((tm, hd), lambda i: (i, 0))
    return pl.pallas_call(
        body, name="att_merge", grid=(t_len // tm,), in_specs=[row] * 6, out_specs=(row, row, row),
        out_shape=(jax.ShapeDtypeStruct((t_len, hd), F32), jax.ShapeDtypeStruct((t_len, hd), BF16),
                   jax.ShapeDtypeStruct((t_len, hd), F32)),
        compiler_params=_params(1))(*os, *ls)


def _att_bwd(q_all, kv, o, lse, do, g):
    d = DILATIONS[g]
    t_len = q_all.shape[0]
    hd = ATT_HEADS * HEAD_DIM
    sub = t_len // d
    nb = sub // ATT_BLOCK
    b = ATT_BLOCK
    e = HEAD_DIM
    scale = e ** -0.5
    slopes = _alibi_slopes()
    qv = q_all.reshape(sub, d * 3 * hd)
    kvv = kv.reshape(sub, d * 2 * hd)
    ov = o.reshape(sub, d * hd)
    lv = lse.reshape(sub, d * hd)
    dov = do.reshape(sub, d * hd)

    def body(qj_ref, qn_ref, kp_ref, kc_ref, vp_ref, vc_ref, doj_ref, don_ref, oj_ref, on_ref, lj_ref, ln_ref,
             dq_ref, dk_ref, dv_ref):
        j = pl.program_id(1)
        valid_c, valid_p0, dist_c, dist_p = _att_masks(d)
        valid_p = valid_p0 & (j > 0)
        valid_n = valid_p0 & (j + 1 < nb)
        for h in range(ATT_HEADS):
            hs = slice(h * e, (h + 1) * e)
            qj, qn = qj_ref[:, hs], qn_ref[:, hs]
            kc, kp = kc_ref[:, hs], kp_ref[:, hs]
            vc, vp = vc_ref[:, hs], vp_ref[:, hs]
            doj, don = doj_ref[:, hs], don_ref[:, hs]
            dlt_j = jnp.sum(doj.astype(F32) * oj_ref[:, hs], axis=1, keepdims=True)
            dlt_n = jnp.sum(don.astype(F32) * on_ref[:, hs], axis=1, keepdims=True)
            lj, ln = lj_ref[:, hs], ln_ref[:, hs]
            s = _dot(qj, kc, NT) * scale - slopes[h] * dist_c
            p = jnp.where(valid_c, jnp.exp(jnp.where(valid_c, s - lj, NEG)), 0.0)
            ds = (p * (_dot(doj, vc, NT) - dlt_j)).astype(BF16)
            dq = _dot(ds, kc, NN)
            dk = _dot(ds, qj, TN)
            dv = _dot(p.astype(BF16), doj, TN)
            s = _dot(qj, kp, NT) * scale - slopes[h] * dist_p
            p = jnp.where(valid_p, jnp.exp(jnp.where(valid_p, s - lj, NEG)), 0.0)
            ds = (p * (_dot(doj, vp, NT) - dlt_j)).astype(BF16)
            dq = dq + _dot(ds, kp, NN)
            s = _dot(qn, kc, NT) * scale - slopes[h] * dist_p
            p = jnp.where(valid_n, jnp.exp(jnp.where(valid_n, s - ln, NEG)), 0.0)
            ds = (p * (_dot(don, vc, NT) - dlt_n)).astype(BF16)
            dk = dk + _dot(ds, qn, TN)
            dv = dv + _dot(p.astype(BF16), don, TN)
            dq_ref[:, hs] = (dq * scale).astype(dq_ref.dtype)
            dk_ref[:, hs] = dk * scale
            dv_ref[:, hs] = dv

    blk = (b, hd)
    nxt = lambda i: jnp.minimum(i + 1, nb - 1)
    prv = lambda i: jnp.maximum(i - 1, 0)
    dq, dk, dv = pl.pallas_call(
        body, name=f"att_bwd{g}", grid=(d, nb),
        in_specs=[pl.BlockSpec(blk, lambda r, i: (i, 3 * r + g)),
                  pl.BlockSpec(blk, lambda r, i: (nxt(i), 3 * r + g)),
                  pl.BlockSpec(blk, lambda r, i: (prv(i), 2 * r)),
                  pl.BlockSpec(blk, lambda r, i: (i, 2 * r)),
                  pl.BlockSpec(blk, lambda r, i: (prv(i), 2 * r + 1)),
                  pl.BlockSpec(blk, lambda r, i: (i, 2 * r + 1)),
                  pl.BlockSpec(blk, lambda r, i: (i, r)),
                  pl.BlockSpec(blk, lambda r, i: (nxt(i), r)),
                  pl.BlockSpec(blk, lambda r, i: (i, r)),
                  pl.BlockSpec(blk, lambda r, i: (nxt(i), r)),
                  pl.BlockSpec(blk, lambda r, i: (i, r)),
                  pl.BlockSpec(blk, lambda r, i: (nxt(i), r))],
        out_specs=(pl.BlockSpec(blk, lambda r, i: (i, r)),) * 3,
        out_shape=(jax.ShapeDtypeStruct((sub, d * hd), BF16), jax.ShapeDtypeStruct((sub, d * hd), F32),
                   jax.ShapeDtypeStruct((sub, d * hd), F32)),
        compiler_params=_params(2))(qv, qv, kvv, kvv, kvv, kvv, dov, dov, ov, ov, lv, lv)
    return dq.reshape(t_len, hd), dk.reshape(t_len, hd), dv.reshape(t_len, hd)


def _kv_grad_sum(dks, dvs):
    t_len, hd = dks[0].shape
    tm = _tile(t_len, (256, 128))

    def body(k0, k1, k2, v0, v1, v2, o_ref):
        o_ref[:, :hd] = (k0[...] + k1[...] + k2[...]).astype(o_ref.dtype)
        o_ref[:, hd:] = (v0[...] + v1[...] + v2[...]).astype(o_ref.dtype)

    row = pl.BlockSpec((tm, hd), lambda i: (i, 0))
    return pl.pallas_call(
        body, name="kv_grad_sum", grid=(t_len // tm,), in_specs=[row] * 6,
        out_specs=pl.BlockSpec((tm, 2 * hd), lambda i: (i, 0)),
        out_shape=jax.ShapeDtypeStruct((t_len, 2 * hd), BF16), compiler_params=_params(1))(*dks, *dvs)


HALO = 16
INV_SQRT2 = 1.0 / math.sqrt(2.0)
INV_SQRT2PI = 1.0 / math.sqrt(2.0 * math.pi)


def _conv_taps(g, halo, cw, cb):
    row = lax.broadcasted_iota(jnp.int32, g.shape, 0)
    h1 = halo[HALO - 1:HALO, :]
    h2 = halo[HALO - 2:HALO - 1, :]
    g1 = jnp.where(row == 0, h1, pltpu.roll(g, 1, 0))
    g2 = jnp.where(row == 0, h2, jnp.where(row == 1, h1, pltpu.roll(g, 2, 0)))
    gc = cw[0:1, :] * g2 + cw[1:2, :] * g1 + cw[2:3, :] * g + cb
    return gc, g1, g2


def _glu_specs(t_len, f, tm, tc):
    nj = f // tc
    hb = tm // HALO
    u = pl.BlockSpec((tm, tc), lambda j, i: (i, j))
    g = pl.BlockSpec((tm, tc), lambda j, i: (i, nj + j))
    gh = pl.BlockSpec((HALO, tc), lambda j, i: (jnp.maximum(i * hb - 1, 0), nj + j))
    cw = pl.BlockSpec((8, tc), lambda j, i: (0, j))
    cb = pl.BlockSpec((1, tc), lambda j, i: (0, j))
    return u, g, gh, cw, cb


def _glu_fwd(name, up, cw, cb):
    t_len = up.shape[0]
    f = up.shape[1] // 2
    tm = _tile(t_len, (512, 256, 128))
    tc = _tile(f, (1408, 1024, 512, 256, 128))
    u_s, g_s, gh_s, cw_s, cb_s = _glu_specs(t_len, f, tm, tc)

    def body(u_ref, g_ref, gh_ref, cw_ref, cb_ref, o_ref):
        first = pl.program_id(1) == 0
        halo = jnp.where(first, 0.0, gh_ref[...].astype(F32))
        gc, _, _ = _conv_taps(g_ref[...].astype(F32), halo, cw_ref[...], cb_ref[...])
        gel = 0.5 * gc * (1.0 + lax.erf(gc * INV_SQRT2))
        o_ref[...] = (gel * u_ref[...].astype(F32)).astype(o_ref.dtype)

    return pl.pallas_call(
        body, name=name, grid=(f // tc, t_len // tm), in_specs=[u_s, g_s, gh_s, cw_s, cb_s],
        out_specs=pl.BlockSpec((tm, tc), lambda j, i: (i, j)),
        out_shape=jax.ShapeDtypeStruct((t_len, f), BF16), compiler_params=_params(2))(up, up, up, cw, cb)


def _glu_bwd_a(name, dact, up, cw, cb):
    t_len = up.shape[0]
    f = up.shape[1] // 2
    tm = _tile(t_len, (256, 128))
    tc = _tile(f, (1408, 1024, 512, 256, 128))
    u_s, g_s, gh_s, cw_s, cb_s = _glu_specs(t_len, f, tm, tc)

    def body(da_ref, u_ref, g_ref, gh_ref, cw_ref, cb_ref, du_ref, dgc_ref, w0_ref, w1_ref, w2_ref, b_ref):
        first = pl.program_id(1) == 0
        halo = jnp.where(first, 0.0, gh_ref[...].astype(F32))
        g = g_ref[...].astype(F32)
        gc, g1, g2 = _conv_taps(g, halo, cw_ref[...], cb_ref[...])
        phi = 0.5 * (1.0 + lax.erf(gc * INV_SQRT2))
        dgel = phi + gc * jnp.exp(-0.5 * gc * gc) * INV_SQRT2PI
        da = da_ref[...].astype(F32)
        du_ref[...] = (da * gc * phi).astype(du_ref.dtype)
        dgc = da * u_ref[...].astype(F32) * dgel
        dgc_ref[...] = dgc.astype(dgc_ref.dtype)
        parts = (jnp.sum(dgc * g2, axis=0, keepdims=True), jnp.sum(dgc * g1, axis=0, keepdims=True),
                 jnp.sum(dgc * g, axis=0, keepdims=True), jnp.sum(dgc, axis=0, keepdims=True))
        refs = (w0_ref, w1_ref, w2_ref, b_ref)

        @pl.when(first)
        def _():
            for r, p in zip(refs, parts):
                r[...] = p

        @pl.when(jnp.logical_not(first))
        def _():
            for r, p in zip(refs, parts):
                r[...] += p

    tile = pl.BlockSpec((tm, tc), lambda j, i: (i, j))
    vec = pl.BlockSpec((1, tc), lambda j, i: (0, j))
    vshape = jax.ShapeDtypeStruct((1, f), F32)
    return pl.pallas_call(
        body, name=name, grid=(f // tc, t_len // tm), in_specs=[tile, u_s, g_s, gh_s, cw_s, cb_s],
        out_specs=(tile, tile, vec, vec, vec, vec),
        out_shape=(jax.ShapeDtypeStruct((t_len, f), BF16), jax.ShapeDtypeStruct((t_len, f), F32),
                   vshape, vshape, vshape, vshape),
        compiler_params=_params(2))(dact, up, up, up, cw, cb)


def _glu_bwd_b(name, du, dgc, cw):
    t_len, f = du.shape
    tm = _tile(t_len, (128, 64))
    hb = tm // HALO
    n_i = t_len // tm
    last_hb = t_len // HALO - 1

    def body(du_ref, d_ref, dh_ref, cw_ref, o_ref):
        last = pl.program_id(0) == n_i - 1
        halo = jnp.where(last, 0.0, dh_ref[...].astype(F32))
        dd = d_ref[...].astype(F32)
        row = lax.broadcasted_iota(jnp.int32, dd.shape, 0)
        h0 = halo[0:1, :]
        h1 = halo[1:2, :]
        d1 = jnp.where(row == tm - 1, h0, pltpu.roll(dd, tm - 1, 0))
        d2 = jnp.where(row == tm - 1, h1, jnp.where(row == tm - 2, h0, pltpu.roll(dd, tm - 2, 0)))
        cwv = cw_ref[...]
        dg = cwv[2:3, :] * dd + cwv[1:2, :] * d1 + cwv[0:1, :] * d2
        o_ref[:, :f] = du_ref[...]
        o_ref[:, f:] = dg.astype(o_ref.dtype)

    row_s = pl.BlockSpec((tm, f), lambda i: (i, 0))
    return pl.pallas_call(
        body, name=name, grid=(n_i,),
        in_specs=[row_s, row_s, pl.BlockSpec((HALO, f), lambda i: (jnp.minimum((i + 1) * hb, last_hb), 0)),
                  pl.BlockSpec((8, f), lambda i: (0, 0))],
        out_specs=pl.BlockSpec((tm, 2 * f), lambda i: (i, 0)),
        out_shape=jax.ShapeDtypeStruct((t_len, 2 * f), BF16), compiler_params=_params(1))(du, dgc, dgc, cw)


def _adamw(name, w, g, m, v):
    rows, cols = w.shape
    tr = _tile(rows, (256, 128, 64, 32, 16, 8))
    c1 = 1.0 / (1.0 - ADAM_B1 ** ADAM_STEP)
    c2 = 1.0 / (1.0 - ADAM_B2 ** ADAM_STEP)

    def body(w_ref, g_ref, m_ref, v_ref, d_ref, nm_ref, nv_ref):
        gv = g_ref[...]
        nm = ADAM_B1 * m_ref[...] + (1.0 - ADAM_B1) * gv
        nv = ADAM_B2 * v_ref[...] + (1.0 - ADAM_B2) * (gv * gv)
        nm_ref[...] = nm
        nv_ref[...] = nv
        d_ref[...] = -ADAM_LR * ((nm * c1) / (jnp.sqrt(nv * c2) + ADAM_EPS) + ADAM_WD * w_ref[...])

    blk = pl.BlockSpec((tr, cols), lambda i: (i, 0))
    shp = jax.ShapeDtypeStruct((rows, cols), F32)
    return pl.pallas_call(body, name=name, grid=(rows // tr,), in_specs=[blk] * 4, out_specs=(blk,) * 3,
                          out_shape=(shp,) * 3, compiler_params=_params(1))(w, g, m, v)


def _local_step(x, target, wc, wr, w_main, w_a, norms, small):
    lay = _layout()
    hd = ATT_HEADS * HEAD_DIM
    f = D_FF

    hn0 = _rms_fwd("rms_attn0", x, norms["attn0"])
    proj = _mm_plain("gla_proj", hn0, w_main, NN, F32)
    a = _mm_plain("gla_proj_a", hn0, w_a, NN, BF16)
    ga, cum = _gla_gate_fwd(a, small["w_a2p"], small["b_a2"])
    o_gla, states = _gla_fwd(proj, cum)
    gated = _gla_out_fwd(o_gla, proj, small["head_norm"])
    h1 = _mm_act_wr("gla_out", gated, wr, lay["gout"], add=x)

    def ffn_fwd(l, h):
        hn = _rms_fwd(f"rms_ffn{l}", h, norms[f"ffn{l}"])
        up = _mm_act_wc(f"ffn_up{l}", hn, wc, lay[f"up{l}"], F32)
        act = _glu_fwd(f"glu_fwd{l}", up, small["conv_w"][l], small["conv_b"][l])
        return hn, up, act, _mm_act_wr(f"ffn_down{l}", act, wr, lay[f"down{l}"], add=h)

    hnf0, up0, act0, h2 = ffn_fwd(0, h1)

    kvn = _rms_fwd("rms_kv", h2, norms["kv"])
    kv = _mm_act_wc("kv_proj", kvn, wc, lay["wkv"], BF16)
    hn1 = _rms_fwd("rms_attn1", h2, norms["attn1"])
    q_all = _mm_act_wc("q_proj", hn1, wc, lay["wq"], BF16)
    branch = [_att_fwd(q_all, kv, g) for g in range(3)]
    o_att, o_att_b, lse = _att_merge([br[0] for br in branch], [br[1] for br in branch])
    h3 = _mm_act_wr("att_out", o_att_b, wr, lay["dout"], add=h2)
    hnf1, up1, act1, h4 = ffn_fwd(1, h3)

    dh4, d_final, loss = _loss_head(h4, norms["final"], target)

    big = {}
    sm = {"final": d_final}

    def ffn_bwd(l, dh, h, hn, up, act):
        dact = _mm_dact_wrT(f"ffn_down_dx{l}", dh, wr, lay[f"down{l}"])
        big[f"down{l}"] = _mm_grad_wr(f"ffn_down_dw{l}", act, dh, lay[f"down{l}"])
        du, dgc, w0, w1, w2, db = _glu_bwd_a(f"glu_bwd_a{l}", dact, up, small["conv_w"][l], small["conv_b"][l])
        sm[f"conv_w{l}"] = (w0, w1, w2)
        sm[f"conv_b{l}"] = db
        dup = _glu_bwd_b(f"glu_bwd_b{l}", du, dgc, small["conv_w"][l])
        dhn = _mm_dact_wcT(f"ffn_up_dx{l}", dup, wc, lay[f"up{l}"])
        big[f"up{l}"] = _mm_grad_wc(f"ffn_up_dw{l}", hn, dup, lay[f"up{l}"])
        dh_in, sm[f"ffn{l}"] = _rms_bwd(f"rms_ffn_bwd{l}", dhn, h, norms[f"ffn{l}"], dh)
        return dh_in

    dh3 = ffn_bwd(1, dh4, h3, hnf1, up1, act1)

    do_att = _mm_dact_wrT("att_out_dx", dh3, wr, lay["dout"])
    big["dout"] = _mm_grad_wr("att_out_dw", o_att_b, dh3, lay["dout"])
    bw = [_att_bwd(q_all, kv, o_att, lse, do_att, g) for g in range(3)]
    dq_all = jnp.concatenate([t[0] for t in bw], axis=1)
    dhn1 = _mm_dact_wcT("q_proj_dx", dq_all, wc, lay["wq"])
    big["wq"] = _mm_grad_wc("q_proj_dw", hn1, dq_all, lay["wq"])
    dh2, sm["attn1"] = _rms_bwd("rms_attn1_bwd", dhn1, h2, norms["attn1"], dh3)
    dkv = _kv_grad_sum([t[1] for t in bw], [t[2] for t in bw])
    dkvn = _mm_dact_wcT("kv_proj_dx", dkv, wc, lay["wkv"])
    big["wkv"] = _mm_grad_wc("kv_proj_dw", kvn, dkv, lay["wkv"])
    dh2, sm["kv"] = _rms_bwd("rms_kv_bwd", dkvn, h2, norms["kv"], dh2)

    dh1 = ffn_bwd(0, dh2, h1, hnf0, up0, act0)

    dgated = _mm_dact_wrT("gla_out_dx", dh1, wr, lay["gout"])
    big["gout"] = _mm_grad_wr("gla_out_dw", gated, dh1, lay["gout"])
    do_gla, dr, sm["head_norm"] = _gla_out_bwd(dgated, o_gla, proj, small["head_norm"])
    dq, dk, dv, dcum = _gla_bwd(proj, cum, states, do_gla)
    da, sm["w_a2p"], sm["b_a2"] = _gla_gate_bwd(dcum, ga, a, small["w_a2p"])
    dproj = jnp.concatenate([dq, dk, dv, dr], axis=1)
    dhn0 = _mm_plain("gla_proj_dx", dproj, w_main, NT, F32)
    dhn0 = _mm_plain("gla_proj_a_dx", da, w_a, NT, F32, add=dhn0)
    big["gin_main"] = _mm_plain("gla_proj_dw", hn0, dproj, TN, BF16)
    big["gin_a"] = _mm_plain("gla_proj_a_dw", hn0, da, TN, BF16)
    grad_x, sm["attn0"] = _rms_bwd("rms_attn0_bwd", dhn0, x, norms["attn0"], dh1)
    return loss, grad_x, big, sm


def _pack_wc(chip, gla_w_in, w_kv, dsa_w_q, ffn_w_up):
    lay = _layout()
    buf = jnp.zeros((N_CHIPS, D_MODEL, lay["wc_cols"]), BF16)
    parts = (("up0", ffn_w_up[0]), ("up1", ffn_w_up[1]), ("wq", dsa_w_q[0]), ("wkv", w_kv), ("gin", gla_w_in[0]))
    for name, w in parts:
        off = lay[name][0]
        buf = lax.dynamic_update_slice(buf, w.astype(BF16)[None], (chip, 0, off))
    return buf


def _pack_wr(chip, gla_w_out, dsa_w_out, ffn_w_down):
    lay = _layout()
    buf = jnp.zeros((N_CHIPS, lay["wr_rows"], D_MODEL), BF16)
    parts = (("down0", ffn_w_down[0]), ("down1", ffn_w_down[1]), ("gout", gla_w_out[0]), ("dout", dsa_w_out[0]))
    for name, w in parts:
        off = lay[name][0]
        buf = lax.dynamic_update_slice(buf, w.astype(BF16)[None], (chip, off, 0))
    return buf


def _unpack_gin(wc):
    off, w, _ = _layout()["gin"]
    full = jnp.transpose(wc[:, :, off:off + w], (1, 0, 2)).reshape(D_MODEL, N_CHIPS * w)
    n_main = 2 * GLA_KEY_DIM + 2 * GLA_VAL_DIM
    w_a = jnp.pad(full[:, n_main:], ((0, 0), (0, A_PAD - GATE_RANK)))
    return full[:, :n_main], w_a


def _small_params(attn_norm, ffn_norm, kv_norm, final_norm, conv_b, w_a2, b_a2, head_norm, conv_w):
    norms = {"attn0": attn_norm[0:1], "attn1": attn_norm[1:2], "ffn0": ffn_norm[0:1], "ffn1": ffn_norm[1:2],
             "kv": kv_norm[None, :], "final": final_norm[None, :]}
    small = {"w_a2p": jnp.pad(w_a2, ((0, A_PAD - GATE_RANK), (0, 0))), "b_a2": b_a2[None, :],
             "head_norm": head_norm[None, :], "conv_w": jnp.pad(conv_w, ((0, 0), (0, 8 - conv_w.shape[1]), (0, 0))),
             "conv_b": conv_b[:, None, :]}
    return norms, small


ANY = pl.BlockSpec(memory_space=pl.ANY)


def _place():
    return lax.axis_index("x"), lax.axis_index("y"), lax.axis_index("c")


def _other_chips(x, y):
    return [(1 - x, y), (x, 1 - y), (1 - x, 1 - y)]


def _rcopy(src, dst, ssem, rsem, dev):
    return pltpu.make_async_remote_copy(src_ref=src, dst_ref=dst, send_sem=ssem, recv_sem=rsem, device_id=dev,
                                        device_id_type=MESH)


def _gather_weights(arrs):
    n = len(arrs)

    def body(*refs):
        ins, outs = refs[:n], refs[n:2 * n]
        send, recv, fsend, frecv = refs[2 * n:]
        x, y, c = _place()
        me = 2 * x + y
        chips = _other_chips(x, y)
        sib = (x, y, 1 - c)
        first, fwd = [], []
        for a in range(n):
            h = ins[a].shape[1] // 2
            mine = pl.ds(c * h, h)
            for j, (px, py) in enumerate(chips):
                cp = _rcopy(ins[a].at[me, mine], outs[a].at[me, mine], send.at[3 * a + j], recv.at[3 * a + j], (px, py, c))
                cp.start()
                first.append(cp)
        for a in range(n):
            h = ins[a].shape[1] // 2
            mine = pl.ds(c * h, h)
            for j, (px, py) in enumerate(chips):
                landed = outs[a].at[2 * px + py, mine]
                _rcopy(landed, landed, send.at[3 * a + j], recv.at[3 * a + j], (px, py, c)).wait_recv()
                cp = _rcopy(landed, landed, fsend.at[3 * a + j], frecv.at[3 * a + j], sib)
                cp.start()
                fwd.append(cp)
        for a in range(n):
            h = ins[a].shape[1] // 2
            theirs = pl.ds((1 - c) * h, h)
            for j, (px, py) in enumerate(chips):
                got = outs[a].at[2 * px + py, theirs]
                _rcopy(got, got, fsend.at[3 * a + j], frecv.at[3 * a + j], sib).wait_recv()
        for cp in first + fwd:
            cp.wait_send()

    return pl.pallas_call(
        body, name="gather_weights", in_specs=[ANY] * n, out_specs=[ANY] * n,
        out_shape=[jax.ShapeDtypeStruct(a.shape, a.dtype) for a in arrs],
        input_output_aliases={a: a for a in range(n)},
        scratch_shapes=[pltpu.SemaphoreType.DMA((3 * n,)), pltpu.SemaphoreType.DMA((3 * n,)),
                        pltpu.SemaphoreType.DMA((3 * n,)), pltpu.SemaphoreType.DMA((3 * n,))])(*arrs)


def _swap_halves(arrs):
    n = len(arrs)

    def body(*refs):
        ins, outs = refs[:n], refs[n:2 * n]
        send, recv = refs[2 * n:]
        x, y, c = _place()
        cps = []
        for a in range(n):
            h = ins[a].shape[1] // 2
            cp = _rcopy(ins[a].at[:, pl.ds((1 - c) * h, h)], outs[a], send.at[a], recv.at[a], (x, y, 1 - c))
            cp.start()
            cps.append(cp)
        for cp in cps:
            cp.wait()

    return pl.pallas_call(
        body, name="swap_halves", in_specs=[ANY] * n, out_specs=[ANY] * n,
        out_shape=[jax.ShapeDtypeStruct((a.shape[0], a.shape[1] // 2, a.shape[2]), a.dtype) for a in arrs],
        scratch_shapes=[pltpu.SemaphoreType.DMA((n,)), pltpu.SemaphoreType.DMA((n,))])(*arrs)


def _scatter_to_chips(arrs):
    n = len(arrs)
    landing = [jnp.zeros_like(a) for a in arrs]

    def body(*refs):
        ins, outs = refs[:n], refs[2 * n:3 * n]
        send, recv = refs[3 * n:]
        x, y, c = _place()
        me = 2 * x + y
        chips = _other_chips(x, y)
        cps = []
        for a in range(n):
            for j, (px, py) in enumerate(chips):
                cp = _rcopy(ins[a].at[2 * px + py], outs[a].at[me], send.at[3 * a + j], recv.at[3 * a + j], (px, py, c))
                cp.start()
                cps.append(cp)
        for a in range(n):
            for j, (px, py) in enumerate(chips):
                got = outs[a].at[2 * px + py]
                _rcopy(got, got, send.at[3 * a + j], recv.at[3 * a + j], (px, py, c)).wait_recv()
        for cp in cps:
            cp.wait_send()

    return pl.pallas_call(
        body, name="scatter_to_chips", in_specs=[ANY] * (2 * n), out_specs=[ANY] * n,
        out_shape=[jax.ShapeDtypeStruct(a.shape, a.dtype) for a in arrs],
        input_output_aliases={n + a: a for a in range(n)},
        scratch_shapes=[pltpu.SemaphoreType.DMA((3 * n,)), pltpu.SemaphoreType.DMA((3 * n,))])(*arrs, *landing)


def _join_halves(arrs):
    n = len(arrs)

    def body(*refs):
        ins, outs = refs[:n], refs[n:2 * n]
        send, recv = refs[2 * n:]
        x, y, c = _place()
        cps = []
        for a in range(n):
            h = ins[a].shape[0] // 2
            mine = pl.ds(c * h, h)
            cp = _rcopy(ins[a].at[mine], outs[a].at[mine], send.at[a], recv.at[a], (x, y, 1 - c))
            cp.start()
            cps.append(cp)
        for a in range(n):
            h = ins[a].shape[0] // 2
            got = outs[a].at[pl.ds((1 - c) * h, h)]
            _rcopy(got, got, send.at[a], recv.at[a], (x, y, 1 - c)).wait_recv()
        for cp in cps:
            cp.wait_send()

    return pl.pallas_call(
        body, name="join_halves", in_specs=[ANY] * n, out_specs=[ANY] * n,
        out_shape=[jax.ShapeDtypeStruct(a.shape, a.dtype) for a in arrs],
        input_output_aliases={a: a for a in range(n)},
        scratch_shapes=[pltpu.SemaphoreType.DMA((n,)), pltpu.SemaphoreType.DMA((n,))])(*arrs)


def _allgather8(name, xs, reduce):
    m_per, n = xs.shape

    def body(x_ref, out_ref, *rest):
        if reduce:
            sum_ref, send, recv, lsem = rest
        else:
            send, recv, lsem = rest
        x, y, c = _place()
        me, sib = (x, y, c), (x, y, 1 - c)
        chips = _other_chips(x, y)

        def rows(px, py, pc):
            return out_ref.at[pl.ds((4 * px + 2 * py + pc) * m_per, m_per), :]

        def copy(k, block, to, src=None):
            return _rcopy(rows(*block) if src is None else src, rows(*block), send.at[k], recv.at[k], to)

        mine = pltpu.make_async_copy(x_ref, rows(*me), lsem)
        mine.start()
        first = [copy(0, me, sib, src=x_ref)]
        first += [copy(1 + j, me, (*chip, c), src=x_ref) for j, chip in enumerate(chips)]
        for cp in first:
            cp.start()
        passed = [copy(4 + j, (*chip, c), sib) for j, chip in enumerate(chips)]
        for j, chip in enumerate(chips):
            copy(1 + j, (*chip, c), me).wait_recv()
            passed[j].start()
        copy(0, sib, me).wait_recv()
        for j, chip in enumerate(chips):
            copy(4 + j, (*chip, 1 - c), me).wait_recv()
        for cp in first + passed:
            cp.wait_send()
        mine.wait()
        if reduce:
            acc = out_ref[pl.ds(0, m_per), :]
            for dev in range(1, N_DEV):
                acc = acc + out_ref[pl.ds(dev * m_per, m_per), :]
            sum_ref[...] = acc

    vm = pl.BlockSpec(memory_space=pltpu.VMEM)
    out_shape = [jax.ShapeDtypeStruct((N_DEV * m_per, n), xs.dtype)]
    if reduce:
        out_shape.append(jax.ShapeDtypeStruct((m_per, n), xs.dtype))
    return pl.pallas_call(
        body, name=name, in_specs=[vm], out_specs=[vm] * len(out_shape), out_shape=out_shape,
        scratch_shapes=[pltpu.SemaphoreType.DMA((7,)), pltpu.SemaphoreType.DMA((7,)), pltpu.SemaphoreType.DMA],
        compiler_params=pltpu.CompilerParams(vmem_limit_bytes=VMEM_LIMIT))(xs)


def _add_my_half(name, a, rb, c_arr):
    s, h, cols = rb.shape
    tr = _tile(h, (512, 352, 256, 128, 64, 32, 16))
    nt = h // tr

    def body(c_ref, a_ref, b_ref, o_ref):
        o_ref[...] = (a_ref[...].astype(F32) + b_ref[...].astype(F32)).astype(o_ref.dtype)

    return pl.pallas_call(
        body, name=name,
        grid_spec=pltpu.PrefetchScalarGridSpec(
            num_scalar_prefetch=1, grid=(s, nt),
            in_specs=[pl.BlockSpec((None, tr, cols), lambda k, i, c: (k, c[0] * nt + i, 0)),
                      pl.BlockSpec((None, tr, cols), lambda k, i, c: (k, i, 0))],
            out_specs=pl.BlockSpec((None, tr, cols), lambda k, i, c: (k, i, 0))),
        out_shape=jax.ShapeDtypeStruct(rb.shape, BF16), compiler_params=_params(2))(c_arr, a, rb)


def _sum_chips(name, own, q, place):
    s, h, cols = q.shape
    tr = _tile(h, (512, 352, 256, 128, 64, 32, 16))
    nt = h // tr

    def body(p_ref, own_ref, q_ref, o_ref):
        chip = p_ref[0]
        acc = jnp.where(chip == 0, own_ref[0], q_ref[0]).astype(F32)
        for j in range(1, s):
            acc = acc + jnp.where(chip == j, own_ref[j], q_ref[j]).astype(F32)
        o_ref[...] = acc

    blk = pl.BlockSpec((s, tr, cols), lambda i, p: (0, i, 0))
    return pl.pallas_call(
        body, name=name,
        grid_spec=pltpu.PrefetchScalarGridSpec(
            num_scalar_prefetch=1, grid=(nt,), in_specs=[blk, blk],
            out_specs=pl.BlockSpec((tr, cols), lambda i, p: (p[1] * nt + i, 0))),
        out_shape=jax.ShapeDtypeStruct((2 * h, cols), F32), compiler_params=_params(1))(place, own, q)


def _pack_rows(parts):
    rows = []
    for p in parts:
        flat = p.reshape(-1).astype(F32)
        n = _roundup(flat.shape[0], 8 * LANE)
        rows.append(jnp.pad(flat, (0, n - flat.shape[0])).reshape(-1, LANE))
    return jnp.concatenate(rows, axis=0)


def _unpack_rows(buf, shapes):
    out, r = [], 0
    for shp in shapes:
        size = math.prod(shp)
        nr = _roundup(size, 8 * LANE) // LANE
        out.append(buf[r:r + nr].reshape(-1)[:size].reshape(shp))
        r += nr
    return out


def kernel(x, attn_norm, gla_w_in, gla_w_a2, gla_b_a2, gla_head_norm, gla_w_out, kv_norm, w_kv, dsa_w_q, dsa_w_out, ffn_norm, ffn_w_up, ffn_conv_w, ffn_conv_b, ffn_w_down, final_norm, loss_target, m_attn_norm, m_gla_w_in, m_gla_w_a2, m_gla_b_a2, m_gla_head_norm, m_gla_w_out, m_kv_norm, m_w_kv, m_dsa_w_q, m_dsa_w_out, m_ffn_norm, m_ffn_w_up, m_ffn_conv_w, m_ffn_conv_b, m_ffn_w_down, m_final_norm, v_attn_norm, v_gla_w_in, v_gla_w_a2, v_gla_b_a2, v_gla_head_norm, v_gla_w_out, v_kv_norm, v_w_kv, v_dsa_w_q, v_dsa_w_out, v_ffn_norm, v_ffn_w_up, v_ffn_conv_w, v_ffn_conv_b, v_ffn_w_down, v_final_norm):
    lay = _layout()
    d, f = D_MODEL, D_FF
    cx, cy, cc = _place()
    chip = 2 * cx + cy
    c_arr = jnp.reshape(cc, (1,)).astype(jnp.int32)
    place = jnp.stack([chip, cc]).astype(jnp.int32)

    wc, wr = _gather_weights([_pack_wc(chip, gla_w_in, w_kv, dsa_w_q, ffn_w_up),
                              _pack_wr(chip, gla_w_out, dsa_w_out, ffn_w_down)])
    w_main, w_a = _unpack_gin(wc)
    sharded_small = [gla_w_a2[0], gla_b_a2[0], gla_head_norm[0], ffn_conv_w]
    gathered = _allgather8("gather_small", _pack_rows(sharded_small), False)[0]
    per_dev = gathered.reshape(N_DEV, -1, LANE)
    shards = [_unpack_rows(per_dev[2 * s], [p.shape for p in sharded_small]) for s in range(N_CHIPS)]
    w_a2, b_a2, head_norm, conv_w = [jnp.concatenate([shards[s][k] for s in range(N_CHIPS)], axis=-1) for k in range(4)]
    norms, small = _small_params(attn_norm, ffn_norm, kv_norm, final_norm, ffn_conv_b, w_a2, b_a2, head_norm, conv_w)

    loss_blk, grad_x, big, sm = _local_step(x[0], loss_target[0], wc, wr, w_main, w_a, norms, small)

    gin_w = lay["gin"][1]
    gin = jnp.concatenate([big["gin_main"], big["gin_a"][:, :GATE_RANK]], axis=1)
    gin = jnp.transpose(gin.reshape(d, N_CHIPS, gin_w), (1, 0, 2))
    gin = jnp.pad(gin, ((0, 0), (0, 0), (0, _roundup(gin_w, LANE) - gin_w)))
    names = ["up0", "up1", "wq", "wkv", "gin", "down0", "down1", "gout", "dout"]
    parts = [gin if k == "gin" else big[k] for k in names]
    theirs = _swap_halves(parts)
    sums = [_add_my_half(f"add_half_{k}", a, b, c_arr) for k, a, b in zip(names, parts, theirs)]
    landed = _scatter_to_chips(sums)
    halves = [_sum_chips(f"sum_chips_{k}", s, q, place) for k, s, q in zip(names, sums, landed)]
    full = dict(zip(names, _join_halves(halves)))

    small_parts = [loss_blk, jnp.concatenate([sm["attn0"], sm["attn1"]]), jnp.concatenate([sm["ffn0"], sm["ffn1"]]),
                   sm["kv"], sm["final"], jnp.concatenate([sm["conv_b0"], sm["conv_b1"]]),
                   sm["w_a2p"][:GATE_RANK], sm["b_a2"], sm["head_norm"],
                   jnp.stack([jnp.concatenate(sm["conv_w0"]), jnp.concatenate(sm["conv_w1"])])]
    small_shapes = [(8, LANE), (2, d), (2, d), (d,), (d,), (2, f), (GATE_RANK, GLA_KEY_DIM), (GLA_KEY_DIM,),
                    (GLA_VAL_DIM // GLA_HEADS,), (2, 3, f)]
    _, reduced = _allgather8("reduce_small", _pack_rows(small_parts), True)
    loss_r, g_attn, g_ffn, g_kv, g_final, g_cb, g_a2, g_ba2, g_hn, g_cw = _unpack_rows(reduced, small_shapes)
    loss = loss_r[0, 0]

    def mine(g, axis):
        w = g.shape[axis] // N_CHIPS
        return lax.dynamic_slice_in_dim(g, chip * w, w, axis)

    grads = {
        "attn_norm": g_attn, "gla_w_in": full["gin"][None, :, :gin_w], "gla_w_a2": mine(g_a2, 1)[None],
        "gla_b_a2": mine(g_ba2, 0)[None], "gla_head_norm": mine(g_hn, 0)[None], "gla_w_out": full["gout"][None],
        "kv_norm": g_kv, "w_kv": full["wkv"], "dsa_w_q": full["wq"][None], "dsa_w_out": full["dout"][None],
        "ffn_norm": g_ffn, "ffn_w_up": jnp.stack([full["up0"], full["up1"]]), "ffn_conv_w": mine(g_cw, 2),
        "ffn_conv_b": g_cb, "ffn_w_down": jnp.stack([full["down0"], full["down1"]]), "final_norm": g_final,
    }
    weights = {"attn_norm": (attn_norm, m_attn_norm, v_attn_norm), "gla_w_in": (gla_w_in, m_gla_w_in, v_gla_w_in),
               "gla_w_a2": (gla_w_a2, m_gla_w_a2, v_gla_w_a2), "gla_b_a2": (gla_b_a2, m_gla_b_a2, v_gla_b_a2),
               "gla_head_norm": (gla_head_norm, m_gla_head_norm, v_gla_head_norm),
               "gla_w_out": (gla_w_out, m_gla_w_out, v_gla_w_out), "kv_norm": (kv_norm, m_kv_norm, v_kv_norm),
               "w_kv": (w_kv, m_w_kv, v_w_kv), "dsa_w_q": (dsa_w_q, m_dsa_w_q, v_dsa_w_q),
               "dsa_w_out": (dsa_w_out, m_dsa_w_out, v_dsa_w_out), "ffn_norm": (ffn_norm, m_ffn_norm, v_ffn_norm),
               "ffn_w_up": (ffn_w_up, m_ffn_w_up, v_ffn_w_up), "ffn_conv_w": (ffn_conv_w, m_ffn_conv_w, v_ffn_conv_w),
               "ffn_conv_b": (ffn_conv_b, m_ffn_conv_b, v_ffn_conv_b),
               "ffn_w_down": (ffn_w_down, m_ffn_w_down, v_ffn_w_down), "final_norm": (final_norm, m_final_norm, v_final_norm)}
    order = list(weights)
    big_names = ("gla_w_in", "gla_w_out", "w_kv", "dsa_w_q", "dsa_w_out", "ffn_w_up", "ffn_w_down")
    delta, new_m, new_v = {}, {}, {}
    for k in big_names:
        w, m, v = weights[k]
        cols = w.shape[-1]
        res = _adamw(f"adamw_{k}", w.reshape(-1, cols), grads[k].reshape(-1, cols), m.reshape(-1, cols), v.reshape(-1, cols))
        delta[k], new_m[k], new_v[k] = [r.reshape(w.shape) for r in res]
    small_names = [k for k in order if k not in big_names]
    packed = [_pack_rows([src[k] for k in small_names])
              for src in ({k: weights[k][0] for k in small_names}, grads, {k: weights[k][1] for k in small_names},
                          {k: weights[k][2] for k in small_names})]
    res = _adamw("adamw_small", *packed)
    shapes = [weights[k][0].shape for k in small_names]
    for dst, buf in zip((delta, new_m, new_v), res):
        for k, val in zip(small_names, _unpack_rows(buf, shapes)):
            dst[k] = val
    return (loss, grad_x[None], *[grads[k] for k in order], *[delta[k] for k in order], *[new_m[k] for k in order],
            *[new_v[k] for k in order])
```

```python
import math

import jax
import jax.numpy as jnp
from jax import lax
from jax.experimental import pallas as pl
from jax.experimental.pallas import tpu as pltpu

F32 = jnp.float32
BF16 = jnp.bfloat16

D_MODEL = 2048
SEQ = 4096
GLA_HEADS = 4
GLA_KEY_DIM = D_MODEL // 2
GLA_VAL_DIM = D_MODEL
GATE_RANK = 16
GATE_NORMALIZER = 16.0
GLA_CHUNK = 64
ATT_HEADS = 16
HEAD_DIM = 128
WINDOWS = (128, 512, 2048)
DILATIONS = (1, 4, 16)
ATT_BLOCK = 128
D_FF = 5632
EPS = 1e-6
ADAM_LR = 0.001
ADAM_B1 = 0.9
ADAM_B2 = 0.999
ADAM_EPS = 1e-08
ADAM_WD = 0.01
ADAM_STEP = 10

N_CHIPS = 4
N_DEV = 8
LANE = 128
A_PAD = 128
VMEM_LIMIT = 56 * 1024 * 1024
NEG = -1e30
MESH = pl.DeviceIdType.MESH

NN = (((1,), (0,)), ((), ()))
NT = (((1,), (1,)), ((), ()))
TN = (((0,), (0,)), ((), ()))


def _tile(n, cands):
    for c in cands:
        if c <= n and n % c == 0:
            return c
    return n


def _roundup(n, m):
    return -(-n // m) * m


def _params(n_axes):
    return pltpu.CompilerParams(dimension_semantics=("arbitrary",) * n_axes, vmem_limit_bytes=VMEM_LIMIT)


def _dot(a, b, dims):
    return lax.dot_general(a, b, dims, preferred_element_type=F32)


def _sigmoid(x):
    return 1.0 / (1.0 + jnp.exp(-x))


COL_SHARDED = ("gin", "up0", "up1", "wq", "wkv")
ROW_SHARDED = ("gout", "down0", "down1", "dout")


def _layout():
    f = D_FF
    hd = ATT_HEADS * HEAD_DIM
    gin = 2 * GLA_KEY_DIM + 2 * GLA_VAL_DIM + GATE_RANK
    up_w = 2 * f // N_CHIPS
    q_w = 3 * hd // N_CHIPS
    kv_w = 2 * hd // N_CHIPS
    dn_r = f // N_CHIPS
    go_r = GLA_VAL_DIM // N_CHIPS
    do_r = hd // N_CHIPS
    big = (1408, 1024, 512, 256, 128)
    return {
        "gin": (0, gin // N_CHIPS, LANE),
        "up0": (0, up_w, _tile(up_w, big)), "up1": (0, up_w, _tile(up_w, big)),
        "wq": (0, q_w, _tile(q_w, (512, 384, 256, 128))), "wkv": (0, kv_w, _tile(kv_w, (1024, 512, 256, 128))),
        "down0": (0, dn_r, _tile(dn_r, big)), "down1": (0, dn_r, _tile(dn_r, big)),
        "gout": (0, go_r, _tile(go_r, (512, 256, 128))), "dout": (0, do_r, _tile(do_r, (512, 256, 128))),
    }


def _matmul(name, a, b, dims, grid, a_spec, b_spec, o_spec, out_shape, acc_shape, add=None, add_spec=None):
    nk = grid[2]
    has_add = add is not None

    def body(*refs):
        a_ref, b_ref = refs[0], refs[1]
        pos = 2
        add_ref = None
        if has_add:
            add_ref = refs[pos]
            pos += 1
        o_ref = refs[pos]
        prod = _dot(a_ref[...].astype(BF16), b_ref[...].astype(BF16), dims)

        def finish(val):
            if has_add:
                val = val + add_ref[...].astype(F32)
            o_ref[...] = val.astype(o_ref.dtype)

        if nk == 1:
            finish(prod)
        else:
            acc_ref = refs[pos + 1]
            k = pl.program_id(2)

            @pl.when(k == 0)
            def _():
                acc_ref[...] = prod

            @pl.when(k > 0)
            def _():
                acc_ref[...] += prod

            @pl.when(k == nk - 1)
            def _():
                finish(acc_ref[...])

    in_specs = [a_spec, b_spec]
    args = [a, b]
    if has_add:
        in_specs.append(add_spec)
        args.append(add)
    scratch = [] if nk == 1 else [pltpu.VMEM(acc_shape, F32)]
    return pl.pallas_call(body, name=name, grid=grid, in_specs=in_specs, out_specs=o_spec, out_shape=out_shape,
                          scratch_shapes=scratch, compiler_params=_params(3))(*args)


def _mm_act_wc(name, a, wc, seg, out_dtype):
    off, w, tn = seg
    t_len, d = a.shape
    tm = _tile(t_len, (1024, 512, 256, 128))
    nps = w // tn
    ob = off // tn
    grid = (t_len // tm, N_CHIPS * nps, 1)
    return _matmul(
        name, a, wc, NN, grid,
        pl.BlockSpec((tm, d), lambda i, j, k: (i, 0)),
        pl.BlockSpec((None, d, tn), lambda i, j, k: (j // nps, 0, ob + j % nps)),
        pl.BlockSpec((tm, tn), lambda i, j, k: (i, j)),
        jax.ShapeDtypeStruct((t_len, N_CHIPS * w), out_dtype), (tm, tn))


def _mm_dact_wcT(name, dy, wc, seg, add=None):
    off, w, tk = seg
    t_len = dy.shape[0]
    d = wc.shape[1]
    tm = _tile(t_len, (1024, 512, 256, 128))
    tn = _tile(d, (1024, 512, 256, 128))
    kps = w // tk
    ob = off // tk
    grid = (t_len // tm, d // tn, N_CHIPS * kps)
    return _matmul(
        name, dy, wc, NT, grid,
        pl.BlockSpec((tm, tk), lambda i, j, k: (i, k)),
        pl.BlockSpec((None, tn, tk), lambda i, j, k: (k // kps, j, ob + k % kps)),
        pl.BlockSpec((tm, tn), lambda i, j, k: (i, j)),
        jax.ShapeDtypeStruct((t_len, d), F32), (tm, tn),
        add=add, add_spec=None if add is None else pl.BlockSpec((tm, tn), lambda i, j, k: (i, j)))


def _mm_grad_wc(name, a, dy, seg):
    _, w, tn = seg
    t_len, d = a.shape
    tm = _tile(d, (1024, 512, 256, 128))
    tk = _tile(t_len, (1024, 512, 256, 128))
    nps = w // tn
    grid = (d // tm, N_CHIPS * nps, t_len // tk)
    return _matmul(
        name, a, dy, TN, grid,
        pl.BlockSpec((tk, tm), lambda i, j, k: (k, i)),
        pl.BlockSpec((tk, tn), lambda i, j, k: (k, j)),
        pl.BlockSpec((None, tm, tn), lambda i, j, k: (j // nps, i, j % nps)),
        jax.ShapeDtypeStruct((N_CHIPS, d, w), BF16), (tm, tn))


def _mm_act_wr(name, a, wr, seg, add):
    off, r, tk = seg
    t_len = a.shape[0]
    d = wr.shape[2]
    tm = _tile(t_len, (1024, 512, 256, 128))
    tn = _tile(d, (1024, 512, 256, 128))
    kps = r // tk
    ob = off // tk
    grid = (t_len // tm, d // tn, N_CHIPS * kps)
    return _matmul(
        name, a, wr, NN, grid,
        pl.BlockSpec((tm, tk), lambda i, j, k: (i, k)),
        pl.BlockSpec((None, tk, tn), lambda i, j, k: (k // kps, ob + k % kps, j)),
        pl.BlockSpec((tm, tn), lambda i, j, k: (i, j)),
        jax.ShapeDtypeStruct((t_len, d), F32), (tm, tn),
        add=add, add_spec=pl.BlockSpec((tm, tn), lambda i, j, k: (i, j)))


def _mm_dact_wrT(name, dh, wr, seg):
    off, r, tn = seg
    t_len, d = dh.shape
    tm = _tile(t_len, (1024, 512, 256, 128))
    nps = r // tn
    ob = off // tn
    grid = (t_len // tm, N_CHIPS * nps, 1)
    return _matmul(
        name, dh, wr, NT, grid,
        pl.BlockSpec((tm, d), lambda i, j, k: (i, 0)),
        pl.BlockSpec((None, tn, d), lambda i, j, k: (j // nps, ob + j % nps, 0)),
        pl.BlockSpec((tm, tn), lambda i, j, k: (i, j)),
        jax.ShapeDtypeStruct((t_len, N_CHIPS * r), BF16), (tm, tn))


def _mm_grad_wr(name, a, dh, seg):
    _, r, tm = seg
    t_len, d = dh.shape
    tn = _tile(d, (1024, 512, 256, 128))
    tk = _tile(t_len, (1024, 512, 256, 128))
    mps = r // tm
    grid = (N_CHIPS * mps, d // tn, t_len // tk)
    return _matmul(
        name, a, dh, TN, grid,
        pl.BlockSpec((tk, tm), lambda i, j, k: (k, i)),
        pl.BlockSpec((tk, tn), lambda i, j, k: (k, j)),
        pl.BlockSpec((None, tm, tn), lambda i, j, k: (i // mps, i % mps, j)),
        jax.ShapeDtypeStruct((N_CHIPS, r, d), BF16), (tm, tn))


def _mm_plain(name, a, b, dims, out_dtype, add=None):
    if dims == NN:
        m, kd = a.shape
        n = b.shape[1]
    elif dims == NT:
        m, kd = a.shape
        n = b.shape[0]
    else:
        kd, m = a.shape
        n = b.shape[1]
    tm = _tile(m, (1024, 512, 256, 128))
    tn = _tile(n, (1024, 768, 512, 256, 128))
    tk = _tile(kd, (2048, 1024, 512, 256, 128))
    grid = (m // tm, n // tn, kd // tk)
    if dims == NN:
        a_spec = pl.BlockSpec((tm, tk), lambda i, j, k: (i, k))
        b_spec = pl.BlockSpec((tk, tn), lambda i, j, k: (k, j))
    elif dims == NT:
        a_spec = pl.BlockSpec((tm, tk), lambda i, j, k: (i, k))
        b_spec = pl.BlockSpec((tn, tk), lambda i, j, k: (j, k))
    else:
        a_spec = pl.BlockSpec((tk, tm), lambda i, j, k: (k, i))
        b_spec = pl.BlockSpec((tk, tn), lambda i, j, k: (k, j))
    o_spec = pl.BlockSpec((tm, tn), lambda i, j, k: (i, j))
    return _matmul(name, a, b, dims, grid, a_spec, b_spec, o_spec, jax.ShapeDtypeStruct((m, n), out_dtype), (tm, tn),
                   add=add, add_spec=None if add is None else o_spec)


def _rms_fwd(name, x, g):
    t_len, d = x.shape
    tm = _tile(t_len, (512, 256, 128))

    def body(x_ref, g_ref, o_ref):
        xv = x_ref[...]
        r = lax.rsqrt(jnp.mean(xv * xv, axis=-1, keepdims=True) + EPS)
        o_ref[...] = (xv * r * g_ref[...]).astype(o_ref.dtype)

    return pl.pallas_call(
        body, name=name, grid=(t_len // tm,),
        in_specs=[pl.BlockSpec((tm, d), lambda i: (i, 0)), pl.BlockSpec((1, d), lambda i: (0, 0))],
        out_specs=pl.BlockSpec((tm, d), lambda i: (i, 0)),
        out_shape=jax.ShapeDtypeStruct((t_len, d), BF16), compiler_params=_params(1))(x, g)


def _rms_bwd(name, dy, x, g, dres):
    t_len, d = x.shape
    tm = _tile(t_len, (256, 128))

    def body(dy_ref, x_ref, g_ref, dres_ref, dx_ref, dg_ref):
        xv = x_ref[...]
        r = lax.rsqrt(jnp.mean(xv * xv, axis=-1, keepdims=True) + EPS)
        xhat = xv * r
        dyv = dy_ref[...].astype(F32)
        dxn = dyv * g_ref[...]
        dx = r * (dxn - xhat * jnp.mean(dxn * xhat, axis=-1, keepdims=True))
        dx_ref[...] = dres_ref[...] + dx
        part = jnp.sum(dyv * xhat, axis=0, keepdims=True)

        @pl.when(pl.program_id(0) == 0)
        def _():
            dg_ref[...] = part

        @pl.when(pl.program_id(0) > 0)
        def _():
            dg_ref[...] += part

    row = pl.BlockSpec((tm, d), lambda i: (i, 0))
    vec = pl.BlockSpec((1, d), lambda i: (0, 0))
    return pl.pallas_call(
        body, name=name, grid=(t_len // tm,), in_specs=[row, row, vec, row], out_specs=(row, vec),
        out_shape=(jax.ShapeDtypeStruct((t_len, d), F32), jax.ShapeDtypeStruct((1, d), F32)),
        compiler_params=_params(1))(dy, x, g, dres)


def _loss_head(h, g, target):
    t_len, d = h.shape
    tm = _tile(t_len, (256, 128))

    def body(h_ref, g_ref, t_ref, dh_ref, dg_ref, loss_ref):
        xv = h_ref[...]
        gv = g_ref[...]
        r = lax.rsqrt(jnp.mean(xv * xv, axis=-1, keepdims=True) + EPS)
        xhat = xv * r
        err = xhat * gv - t_ref[...]
        dyv = err * (1.0 / d)
        dxn = dyv * gv
        dh_ref[...] = r * (dxn - xhat * jnp.mean(dxn * xhat, axis=-1, keepdims=True))
        part = jnp.sum(dyv * xhat, axis=0, keepdims=True)
        lpart = jnp.zeros((8, LANE), F32) + (0.5 / d) * jnp.sum(err * err)

        @pl.when(pl.program_id(0) == 0)
        def _():
            dg_ref[...] = part
            loss_ref[...] = lpart

        @pl.when(pl.program_id(0) > 0)
        def _():
            dg_ref[...] += part
            loss_ref[...] += lpart

    row = pl.BlockSpec((tm, d), lambda i: (i, 0))
    vec = pl.BlockSpec((1, d), lambda i: (0, 0))
    return pl.pallas_call(
        body, name="loss_head", grid=(t_len // tm,), in_specs=[row, vec, row],
        out_specs=(row, vec, pl.BlockSpec((8, LANE), lambda i: (0, 0))),
        out_shape=(jax.ShapeDtypeStruct((t_len, d), F32), jax.ShapeDtypeStruct((1, d), F32),
                   jax.ShapeDtypeStruct((8, LANE), F32)),
        compiler_params=_params(1))(h, g, target)


def _chunk_row(shape):
    return lax.broadcasted_iota(jnp.int32, shape, 0) % GLA_CHUNK


def _gla_gate_fwd(a, w_a2p, b_a2):
    t_len = a.shape[0]
    kd = w_a2p.shape[1]
    tm = _tile(t_len, (256, 128, 64))

    def body(a_ref, w_ref, b_ref, ga_ref, cum_ref):
        ga = _dot(a_ref[...], w_ref[...].astype(BF16), NN) + b_ref[...]
        ga_ref[...] = ga
        la = (jnp.minimum(ga, 0.0) - jnp.log(1.0 + jnp.exp(-jnp.abs(ga)))) * (1.0 / GATE_NORMALIZER)
        row = _chunk_row(la.shape)
        s = 1
        while s < GLA_CHUNK:
            la = la + jnp.where(row >= s, pltpu.roll(la, s, 0), 0.0)
            s *= 2
        cum_ref[...] = la

    return pl.pallas_call(
        body, name="gla_gate_fwd", grid=(t_len // tm,),
        in_specs=[pl.BlockSpec((tm, A_PAD), lambda i: (i, 0)), pl.BlockSpec((A_PAD, kd), lambda i: (0, 0)),
                  pl.BlockSpec((1, kd), lambda i: (0, 0))],
        out_specs=(pl.BlockSpec((tm, kd), lambda i: (i, 0)), pl.BlockSpec((tm, kd), lambda i: (i, 0))),
        out_shape=(jax.ShapeDtypeStruct((t_len, kd), F32), jax.ShapeDtypeStruct((t_len, kd), F32)),
        compiler_params=_params(1))(a, w_a2p, b_a2)


def _gla_gate_bwd(dcum, ga, a, w_a2p):
    t_len, kd = dcum.shape
    tm = _tile(t_len, (256, 128, 64))

    def body(dc_ref, ga_ref, a_ref, w_ref, da_ref, dw_ref, db_ref):
        x = dc_ref[...]
        row = _chunk_row(x.shape)
        s = 1
        while s < GLA_CHUNK:
            x = x + jnp.where(row < GLA_CHUNK - s, pltpu.roll(x, tm - s, 0), 0.0)
            s *= 2
        dga = x * (1.0 / GATE_NORMALIZER) * _sigmoid(-ga_ref[...])
        dgab = dga.astype(BF16)
        da_ref[...] = _dot(dgab, w_ref[...].astype(BF16), NT).astype(da_ref.dtype)
        dw = _dot(a_ref[...], dgab, TN)
        db = jnp.sum(dga, axis=0, keepdims=True)

        @pl.when(pl.program_id(0) == 0)
        def _():
            dw_ref[...] = dw
            db_ref[...] = db

        @pl.when(pl.program_id(0) > 0)
        def _():
            dw_ref[...] += dw
            db_ref[...] += db

    wide = pl.BlockSpec((tm, kd), lambda i: (i, 0))
    return pl.pallas_call(
        body, name="gla_gate_bwd", grid=(t_len // tm,),
        in_specs=[wide, wide, pl.BlockSpec((tm, A_PAD), lambda i: (i, 0)), pl.BlockSpec((A_PAD, kd), lambda i: (0, 0))],
        out_specs=(pl.BlockSpec((tm, A_PAD), lambda i: (i, 0)), pl.BlockSpec((A_PAD, kd), lambda i: (0, 0)),
                   pl.BlockSpec((1, kd), lambda i: (0, 0))),
        out_shape=(jax.ShapeDtypeStruct((t_len, A_PAD), BF16), jax.ShapeDtypeStruct((A_PAD, kd), F32),
                   jax.ShapeDtypeStruct((1, kd), F32)),
        compiler_params=_params(1))(dcum, ga, a, w_a2p)


def _gla_dims():
    dk = GLA_KEY_DIM // GLA_HEADS
    dv = GLA_VAL_DIM // GLA_HEADS
    return dk, dv


def _gla_fwd(proj, cum):
    t_len = proj.shape[0]
    dk, dv = _gla_dims()
    nc = t_len // GLA_CHUNK
    c = GLA_CHUNK
    scale = dk ** -0.5
    v0 = 2 * GLA_KEY_DIM // dv

    def body(q_ref, k_ref, v_ref, cum_ref, o_ref, st_ref, s_scr):
        @pl.when(pl.program_id(1) == 0)
        def _():
            s_scr[...] = jnp.zeros_like(s_scr)

        cm = cum_ref[...]
        last = cm[c - 1:c, :]
        q = q_ref[...].astype(F32) * scale
        k = k_ref[...].astype(F32)
        v = v_ref[...].astype(BF16)
        qd = (q * jnp.exp(cm)).astype(BF16)
        ki = (k * jnp.exp(-cm)).astype(BF16)
        ke = (k * jnp.exp(last - cm)).astype(BF16)
        tri = lax.broadcasted_iota(jnp.int32, (c, c), 0) >= lax.broadcasted_iota(jnp.int32, (c, c), 1)
        sc = jnp.where(tri, _dot(qd, ki, NT), 0.0)
        st = s_scr[...]
        st_ref[...] = st
        o_ref[...] = _dot(sc.astype(BF16), v, NN) + _dot(qd, st.astype(BF16), NT)
        s_scr[...] = st * jnp.exp(last) + _dot(v, ke, TN)

    return pl.pallas_call(
        body, name="gla_fwd", grid=(GLA_HEADS, nc),
        in_specs=[pl.BlockSpec((c, dk), lambda h, n: (n, h)),
                  pl.BlockSpec((c, dk), lambda h, n: (n, GLA_HEADS + h)),
                  pl.BlockSpec((c, dv), lambda h, n: (n, v0 + h)),
                  pl.BlockSpec((c, dk), lambda h, n: (n, h))],
        out_specs=(pl.BlockSpec((c, dv), lambda h, n: (n, h)),
                   pl.BlockSpec((None, None, dv, dk), lambda h, n: (h, n, 0, 0))),
        out_shape=(jax.ShapeDtypeStruct((t_len, GLA_VAL_DIM), F32),
                   jax.ShapeDtypeStruct((GLA_HEADS, nc, dv, dk), F32)),
        scratch_shapes=[pltpu.VMEM((dv, dk), F32)], compiler_params=_params(2))(proj, proj, proj, cum)


def _gla_bwd(proj, cum, states, do):
    t_len = proj.shape[0]
    dk, dv = _gla_dims()
    nc = t_len // GLA_CHUNK
    c = GLA_CHUNK
    scale = dk ** -0.5
    v0 = 2 * GLA_KEY_DIM // dv

    def body(q_ref, k_ref, v_ref, cum_ref, st_ref, do_ref, dq_ref, dk_ref, dv_ref, dc_ref, ds_scr):
        @pl.when(pl.program_id(1) == 0)
        def _():
            ds_scr[...] = jnp.zeros_like(ds_scr)

        cm = cum_ref[...]
        last = cm[c - 1:c, :]
        e_c = jnp.exp(cm)
        e_nc = jnp.exp(-cm)
        e_lc = jnp.exp(last - cm)
        e_l = jnp.exp(last)
        q = q_ref[...].astype(F32) * scale
        k = k_ref[...].astype(F32)
        v = v_ref[...].astype(BF16)
        dov = do_ref[...]
        qd32 = q * e_c
        ki32 = k * e_nc
        ke32 = k * e_lc
        qd = qd32.astype(BF16)
        ki = ki32.astype(BF16)
        ke = ke32.astype(BF16)
        st = st_ref[...]
        dst = ds_scr[...]
        dstb = dst.astype(BF16)
        tri = lax.broadcasted_iota(jnp.int32, (c, c), 0) >= lax.broadcasted_iota(jnp.int32, (c, c), 1)
        am = jnp.where(tri, _dot(dov, v, NT), 0.0).astype(BF16)
        pm = jnp.where(tri, _dot(qd, ki, NT), 0.0).astype(BF16)
        dqd = _dot(am, ki, NN) + _dot(dov, st.astype(BF16), NN)
        dki = _dot(am, qd, TN)
        dvv = _dot(pm, dov, TN) + _dot(ke, dstb, NT)
        dke = _dot(v, dstb, NN)
        d_el = jnp.sum(dst * st, axis=0, keepdims=True)
        ds_scr[...] = dst * e_l + _dot(dov, qd, TN)
        dq_ref[...] = (dqd * scale * e_c).astype(dq_ref.dtype)
        dk_ref[...] = (dki * e_nc + dke * e_lc).astype(dk_ref.dtype)
        dv_ref[...] = dvv.astype(dv_ref.dtype)
        dkeke = dke * ke32
        dcum = dqd * qd32 - dki * ki32 - dkeke
        dlast = jnp.sum(dkeke, axis=0, keepdims=True) + d_el * e_l
        row = lax.broadcasted_iota(jnp.int32, dcum.shape, 0)
        dc_ref[...] = jnp.where(row == c - 1, dcum + dlast, dcum)

    rev = nc - 1
    return pl.pallas_call(
        body, name="gla_bwd", grid=(GLA_HEADS, nc),
        in_specs=[pl.BlockSpec((c, dk), lambda h, n: (rev - n, h)),
                  pl.BlockSpec((c, dk), lambda h, n: (rev - n, GLA_HEADS + h)),
                  pl.BlockSpec((c, dv), lambda h, n: (rev - n, v0 + h)),
                  pl.BlockSpec((c, dk), lambda h, n: (rev - n, h)),
                  pl.BlockSpec((None, None, dv, dk), lambda h, n: (h, rev - n, 0, 0)),
                  pl.BlockSpec((c, dv), lambda h, n: (rev - n, h))],
        out_specs=(pl.BlockSpec((c, dk), lambda h, n: (rev - n, h)),
                   pl.BlockSpec((c, dk), lambda h, n: (rev - n, h)),
                   pl.BlockSpec((c, dv), lambda h, n: (rev - n, h)),
                   pl.BlockSpec((c, dk), lambda h, n: (rev - n, h))),
        out_shape=(jax.ShapeDtypeStruct((t_len, GLA_KEY_DIM), BF16), jax.ShapeDtypeStruct((t_len, GLA_KEY_DIM), BF16),
                   jax.ShapeDtypeStruct((t_len, GLA_VAL_DIM), BF16), jax.ShapeDtypeStruct((t_len, GLA_KEY_DIM), F32)),
        scratch_shapes=[pltpu.VMEM((dv, dk), F32)], compiler_params=_params(2))(proj, proj, proj, cum, states, do)


def _gla_out_fwd(o, proj, gn):
    t_len = o.shape[0]
    _, dv = _gla_dims()
    tm = _tile(t_len, (512, 256, 128))
    r0 = (2 * GLA_KEY_DIM + GLA_VAL_DIM) // dv

    def body(o_ref, r_ref, g_ref, y_ref):
        ov = o_ref[...]
        rs = lax.rsqrt(jnp.mean(ov * ov, axis=-1, keepdims=True) + EPS)
        rv = r_ref[...].astype(F32)
        y_ref[...] = (ov * rs * g_ref[...] * (rv * _sigmoid(rv))).astype(y_ref.dtype)

    return pl.pallas_call(
        body, name="gla_out_fwd", grid=(t_len // tm, GLA_HEADS),
        in_specs=[pl.BlockSpec((tm, dv), lambda i, h: (i, h)), pl.BlockSpec((tm, dv), lambda i, h: (i, r0 + h)),
                  pl.BlockSpec((1, dv), lambda i, h: (0, 0))],
        out_specs=pl.BlockSpec((tm, dv), lambda i, h: (i, h)),
        out_shape=jax.ShapeDtypeStruct((t_len, GLA_VAL_DIM), BF16), compiler_params=_params(2))(o, proj, gn)


def _gla_out_bwd(dy, o, proj, gn):
    t_len = o.shape[0]
    _, dv = _gla_dims()
    tm = _tile(t_len, (512, 256, 128))
    r0 = (2 * GLA_KEY_DIM + GLA_VAL_DIM) // dv

    def body(dy_ref, o_ref, r_ref, g_ref, do_ref, dr_ref, dg_ref):
        ov = o_ref[...]
        gv = g_ref[...]
        rs = lax.rsqrt(jnp.mean(ov * ov, axis=-1, keepdims=True) + EPS)
        xhat = ov * rs
        rv = r_ref[...].astype(F32)
        sg = _sigmoid(rv)
        gate = rv * sg
        dyv = dy_ref[...].astype(F32)
        dn = dyv * gate
        dr_ref[...] = (dyv * xhat * gv * (sg * (1.0 + rv * (1.0 - sg)))).astype(dr_ref.dtype)
        dxn = dn * gv
        do_ref[...] = (rs * (dxn - xhat * jnp.mean(dxn * xhat, axis=-1, keepdims=True))).astype(do_ref.dtype)
        part = jnp.sum(dn * xhat, axis=0, keepdims=True)
        first = (pl.program_id(0) == 0) & (pl.program_id(1) == 0)

        @pl.when(first)
        def _():
            dg_ref[...] = part

        @pl.when(jnp.logical_not(first))
        def _():
            dg_ref[...] += part

    blk = pl.BlockSpec((tm, dv), lambda i, h: (i, h))
    return pl.pallas_call(
        body, name="gla_out_bwd", grid=(t_len // tm, GLA_HEADS),
        in_specs=[blk, blk, pl.BlockSpec((tm, dv), lambda i, h: (i, r0 + h)), pl.BlockSpec((1, dv), lambda i, h: (0, 0))],
        out_specs=(blk, blk, pl.BlockSpec((1, dv), lambda i, h: (0, 0))),
        out_shape=(jax.ShapeDtypeStruct((t_len, GLA_VAL_DIM), BF16), jax.ShapeDtypeStruct((t_len, GLA_VAL_DIM), BF16),
                   jax.ShapeDtypeStruct((1, dv), F32)),
        compiler_params=_params(2))(dy, o, proj, gn)


def _alibi_slopes():
    n = ATT_HEADS
    start = 2.0 ** (-8.0 / n)
    return [start ** (i + 1) for i in range(n)]


def _att_masks(d):
    b = ATT_BLOCK
    qa = lax.broadcasted_iota(jnp.int32, (b, b), 0)
    kb = lax.broadcasted_iota(jnp.int32, (b, b), 1)
    dist_c = qa - kb
    dist_p = qa - kb + b
    return dist_c >= 0, dist_p <= b, (dist_c * d).astype(F32), (dist_p * d).astype(F32)


def _att_fwd(q_all, kv, g):
    d = DILATIONS[g]
    assert WINDOWS[g] // d == ATT_BLOCK
    t_len = q_all.shape[0]
    hd = ATT_HEADS * HEAD_DIM
    sub = t_len // d
    nb = sub // ATT_BLOCK
    b = ATT_BLOCK
    e = HEAD_DIM
    scale = e ** -0.5
    slopes = _alibi_slopes()
    qv = q_all.reshape(sub, d * 3 * hd)
    kvv = kv.reshape(sub, d * 2 * hd)

    def body(q_ref, kp_ref, kc_ref, vp_ref, vc_ref, o_ref, l_ref):
        ib = pl.program_id(1)
        valid_c, valid_p0, dist_c, dist_p = _att_masks(d)
        valid_p = valid_p0 & (ib > 0)
        for h in range(ATT_HEADS):
            hs = slice(h * e, (h + 1) * e)
            qh = q_ref[:, hs]
            s_c = jnp.where(valid_c, _dot(qh, kc_ref[:, hs], NT) * scale - slopes[h] * dist_c, NEG)
            s_p = jnp.where(valid_p, _dot(qh, kp_ref[:, hs], NT) * scale - slopes[h] * dist_p, NEG)
            m = jnp.maximum(jnp.max(s_c, axis=1, keepdims=True), jnp.max(s_p, axis=1, keepdims=True))
            p_c = jnp.where(valid_c, jnp.exp(s_c - m), 0.0)
            p_p = jnp.where(valid_p, jnp.exp(s_p - m), 0.0)
            l = jnp.sum(p_c, axis=1, keepdims=True) + jnp.sum(p_p, axis=1, keepdims=True)
            acc = _dot(p_c.astype(BF16), vc_ref[:, hs], NN) + _dot(p_p.astype(BF16), vp_ref[:, hs], NN)
            o_ref[:, hs] = acc / l
            l_ref[:, hs] = jnp.broadcast_to(m + jnp.log(l), (b, e))

    blk = (b, hd)
    o, lse = pl.pallas_call(
        body, name=f"att_fwd{g}", grid=(d, nb),
        in_specs=[pl.BlockSpec(blk, lambda r, i: (i, 3 * r + g)),
                  pl.BlockSpec(blk, lambda r, i: (jnp.maximum(i - 1, 0), 2 * r)),
                  pl.BlockSpec(blk, lambda r, i: (i, 2 * r)),
                  pl.BlockSpec(blk, lambda r, i: (jnp.maximum(i - 1, 0), 2 * r + 1)),
                  pl.BlockSpec(blk, lambda r, i: (i, 2 * r + 1))],
        out_specs=(pl.BlockSpec(blk, lambda r, i: (i, r)), pl.BlockSpec(blk, lambda r, i: (i, r))),
        out_shape=(jax.ShapeDtypeStruct((sub, d * hd), F32), jax.ShapeDtypeStruct((sub, d * hd), F32)),
        compiler_params=_params(2))(qv, kvv, kvv, kvv, kvv)
    return o.reshape(t_len, hd), lse.reshape(t_len, hd)


def _att_merge(os, ls):
    t_len, hd = os[0].shape
    tm = _tile(t_len, (256, 128))

    def body(o0, o1, o2, l0, l1, l2, of_ref, ob_ref, l_ref):
        a0, a1, a2 = l0[...], l1[...], l2[...]
        m = jnp.maximum(jnp.maximum(a0, a1), a2)
        e0, e1, e2 = jnp.exp(a0 - m), jnp.exp(a1 - m), jnp.exp(a2 - m)
        den = e0 + e1 + e2
        o = (e0 * o0[...] + e1 * o1[...] + e2 * o2[...]) / den
        of_ref[...] = o
        ob_ref[...] = o.astype(ob_ref.dtype)
        l_ref[...] = m + jnp.log(den)

    row = pl.BlockSpec((tm, hd), lambda i: (i, 0))
    return pl.pallas_call(
        body, name="att_merge", grid=(t_len // tm,), in_specs=[row] * 6, out_specs=(row, row, row),
        out_shape=(jax.ShapeDtypeStruct((t_len, hd), F32), jax.ShapeDtypeStruct((t_len, hd), BF16),
                   jax.ShapeDtypeStruct((t_len, hd), F32)),
        compiler_params=_params(1))(*os, *ls)


def _att_bwd(q_all, kv, o, lse, do, g):
    d = DILATIONS[g]
    t_len = q_all.shape[0]
    hd = ATT_HEADS * HEAD_DIM
    sub = t_len // d
    nb = sub // ATT_BLOCK
    b = ATT_BLOCK
    e = HEAD_DIM
    scale = e ** -0.5
    slopes = _alibi_slopes()
    qv = q_all.reshape(sub, d * 3 * hd)
    kvv = kv.reshape(sub, d * 2 * hd)
    ov = o.reshape(sub, d * hd)
    lv = lse.reshape(sub, d * hd)
    dov = do.reshape(sub, d * hd)

    def body(qj_ref, qn_ref, kp_ref, kc_ref, vp_ref, vc_ref, doj_ref, don_ref, oj_ref, on_ref, lj_ref, ln_ref,
             dq_ref, dk_ref, dv_ref):
        j = pl.program_id(1)
        valid_c, valid_p0, dist_c, dist_p = _att_masks(d)
        valid_p = valid_p0 & (j > 0)
        valid_n = valid_p0 & (j + 1 < nb)
        for h in range(ATT_HEADS):
            hs = slice(h * e, (h + 1) * e)
            qj, qn = qj_ref[:, hs], qn_ref[:, hs]
            kc, kp = kc_ref[:, hs], kp_ref[:, hs]
            vc, vp = vc_ref[:, hs], vp_ref[:, hs]
            doj, don = doj_ref[:, hs], don_ref[:, hs]
            dlt_j = jnp.sum(doj.astype(F32) * oj_ref[:, hs], axis=1, keepdims=True)
            dlt_n = jnp.sum(don.astype(F32) * on_ref[:, hs], axis=1, keepdims=True)
            lj, ln = lj_ref[:, hs], ln_ref[:, hs]
            s = _dot(qj, kc, NT) * scale - slopes[h] * dist_c
            p = jnp.where(valid_c, jnp.exp(jnp.where(valid_c, s - lj, NEG)), 0.0)
            ds = (p * (_dot(doj, vc, NT) - dlt_j)).astype(BF16)
            dq = _dot(ds, kc, NN)
            dk = _dot(ds, qj, TN)
            dv = _dot(p.astype(BF16), doj, TN)
            s = _dot(qj, kp, NT) * scale - slopes[h] * dist_p
            p = jnp.where(valid_p, jnp.exp(jnp.where(valid_p, s - lj, NEG)), 0.0)
            ds = (p * (_dot(doj, vp, NT) - dlt_j)).astype(BF16)
            dq = dq + _dot(ds, kp, NN)
            s = _dot(qn, kc, NT) * scale - slopes[h] * dist_p
            p = jnp.where(valid_n, jnp.exp(jnp.where(valid_n, s - ln, NEG)), 0.0)
            ds = (p * (_dot(don, vc, NT) - dlt_n)).astype(BF16)
            dk = dk + _dot(ds, qn, TN)
            dv = dv + _dot(p.astype(BF16), don, TN)
            dq_ref[:, hs] = (dq * scale).astype(dq_ref.dtype)
            dk_ref[:, hs] = dk * scale
            dv_ref[:, hs] = dv

    blk = (b, hd)
    nxt = lambda i: jnp.minimum(i + 1, nb - 1)
    prv = lambda i: jnp.maximum(i - 1, 0)
    dq, dk, dv = pl.pallas_call(
        body, name=f"att_bwd{g}", grid=(d, nb),
        in_specs=[pl.BlockSpec(blk, lambda r, i: (i, 3 * r + g)),
                  pl.BlockSpec(blk, lambda r, i: (nxt(i), 3 * r + g)),
                  pl.BlockSpec(blk, lambda r, i: (prv(i), 2 * r)),
                  pl.BlockSpec(blk, lambda r, i: (i, 2 * r)),
                  pl.BlockSpec(blk, lambda r, i: (prv(i), 2 * r + 1)),
                  pl.BlockSpec(blk, lambda r, i: (i, 2 * r + 1)),
                  pl.BlockSpec(blk, lambda r, i: (i, r)),
                  pl.BlockSpec(blk, lambda r, i: (nxt(i), r)),
                  pl.BlockSpec(blk, lambda r, i: (i, r)),
                  pl.BlockSpec(blk, lambda r, i: (nxt(i), r)),
                  pl.BlockSpec(blk, lambda r, i: (i, r)),
                  pl.BlockSpec(blk, lambda r, i: (nxt(i), r))],
        out_specs=(pl.BlockSpec(blk, lambda r, i: (i, r)),) * 3,
        out_shape=(jax.ShapeDtypeStruct((sub, d * hd), BF16), jax.ShapeDtypeStruct((sub, d * hd), F32),
                   jax.ShapeDtypeStruct((sub, d * hd), F32)),
        compiler_params=_params(2))(qv, qv, kvv, kvv, kvv, kvv, dov, dov, ov, ov, lv, lv)
    return dq.reshape(t_len, hd), dk.reshape(t_len, hd), dv.reshape(t_len, hd)


def _kv_grad_sum(dks, dvs):
    t_len, hd = dks[0].shape
    tm = _tile(t_len, (256, 128))

    def body(k0, k1, k2, v0, v1, v2, o_ref):
        o_ref[:, :hd] = (k0[...] + k1[...] + k2[...]).astype(o_ref.dtype)
        o_ref[:, hd:] = (v0[...] + v1[...] + v2[...]).astype(o_ref.dtype)

    row = pl.BlockSpec((tm, hd), lambda i: (i, 0))
    return pl.pallas_call(
        body, name="kv_grad_sum", grid=(t_len // tm,), in_specs=[row] * 6,
        out_specs=pl.BlockSpec((tm, 2 * hd), lambda i: (i, 0)),
        out_shape=jax.ShapeDtypeStruct((t_len, 2 * hd), BF16), compiler_params=_params(1))(*dks, *dvs)


HALO = 16
INV_SQRT2 = 1.0 / math.sqrt(2.0)
INV_SQRT2PI = 1.0 / math.sqrt(2.0 * math.pi)


def _conv_taps(g, halo, cw, cb):
    row = lax.broadcasted_iota(jnp.int32, g.shape, 0)
    h1 = halo[HALO - 1:HALO, :]
    h2 = halo[HALO - 2:HALO - 1, :]
    g1 = jnp.where(row == 0, h1, pltpu.roll(g, 1, 0))
    g2 = jnp.where(row == 0, h2, jnp.where(row == 1, h1, pltpu.roll(g, 2, 0)))
    gc = cw[0:1, :] * g2 + cw[1:2, :] * g1 + cw[2:3, :] * g + cb
    return gc, g1, g2


def _glu_specs(t_len, f, tm, tc):
    nj = f // tc
    hb = tm // HALO
    u = pl.BlockSpec((tm, tc), lambda j, i: (i, j))
    g = pl.BlockSpec((tm, tc), lambda j, i: (i, nj + j))
    gh = pl.BlockSpec((HALO, tc), lambda j, i: (jnp.maximum(i * hb - 1, 0), nj + j))
    cw = pl.BlockSpec((8, tc), lambda j, i: (0, j))
    cb = pl.BlockSpec((1, tc), lambda j, i: (0, j))
    return u, g, gh, cw, cb


def _glu_fwd(name, up, cw, cb):
    t_len = up.shape[0]
    f = up.shape[1] // 2
    tm = _tile(t_len, (512, 256, 128))
    tc = _tile(f, (1408, 1024, 512, 256, 128))
    u_s, g_s, gh_s, cw_s, cb_s = _glu_specs(t_len, f, tm, tc)

    def body(u_ref, g_ref, gh_ref, cw_ref, cb_ref, o_ref):
        first = pl.program_id(1) == 0
        halo = jnp.where(first, 0.0, gh_ref[...].astype(F32))
        gc, _, _ = _conv_taps(g_ref[...].astype(F32), halo, cw_ref[...], cb_ref[...])
        gel = 0.5 * gc * (1.0 + lax.erf(gc * INV_SQRT2))
        o_ref[...] = (gel * u_ref[...].astype(F32)).astype(o_ref.dtype)

    return pl.pallas_call(
        body, name=name, grid=(f // tc, t_len // tm), in_specs=[u_s, g_s, gh_s, cw_s, cb_s],
        out_specs=pl.BlockSpec((tm, tc), lambda j, i: (i, j)),
        out_shape=jax.ShapeDtypeStruct((t_len, f), BF16), compiler_params=_params(2))(up, up, up, cw, cb)


def _glu_bwd_a(name, dact, up, cw, cb):
    t_len = up.shape[0]
    f = up.shape[1] // 2
    tm = _tile(t_len, (256, 128))
    tc = _tile(f, (1408, 1024, 512, 256, 128))
    u_s, g_s, gh_s, cw_s, cb_s = _glu_specs(t_len, f, tm, tc)

    def body(da_ref, u_ref, g_ref, gh_ref, cw_ref, cb_ref, du_ref, dgc_ref, w0_ref, w1_ref, w2_ref, b_ref):
        first = pl.program_id(1) == 0
        halo = jnp.where(first, 0.0, gh_ref[...].astype(F32))
        g = g_ref[...].astype(F32)
        gc, g1, g2 = _conv_taps(g, halo, cw_ref[...], cb_ref[...])
        phi = 0.5 * (1.0 + lax.erf(gc * INV_SQRT2))
        dgel = phi + gc * jnp.exp(-0.5 * gc * gc) * INV_SQRT2PI
        da = da_ref[...].astype(F32)
        du_ref[...] = (da * gc * phi).astype(du_ref.dtype)
        dgc = da * u_ref[...].astype(F32) * dgel
        dgc_ref[...] = dgc.astype(dgc_ref.dtype)
        parts = (jnp.sum(dgc * g2, axis=0, keepdims=True), jnp.sum(dgc * g1, axis=0, keepdims=True),
                 jnp.sum(dgc * g, axis=0, keepdims=True), jnp.sum(dgc, axis=0, keepdims=True))
        refs = (w0_ref, w1_ref, w2_ref, b_ref)

        @pl.when(first)
        def _():
            for r, p in zip(refs, parts):
                r[...] = p

        @pl.when(jnp.logical_not(first))
        def _():
            for r, p in zip(refs, parts):
                r[...] += p

    tile = pl.BlockSpec((tm, tc), lambda j, i: (i, j))
    vec = pl.BlockSpec((1, tc), lambda j, i: (0, j))
    vshape = jax.ShapeDtypeStruct((1, f), F32)
    return pl.pallas_call(
        body, name=name, grid=(f // tc, t_len // tm), in_specs=[tile, u_s, g_s, gh_s, cw_s, cb_s],
        out_specs=(tile, tile, vec, vec, vec, vec),
        out_shape=(jax.ShapeDtypeStruct((t_len, f), BF16), jax.ShapeDtypeStruct((t_len, f), F32),
                   vshape, vshape, vshape, vshape),
        compiler_params=_params(2))(dact, up, up, up, cw, cb)


def _glu_bwd_b(name, du, dgc, cw):
    t_len, f = du.shape
    tm = _tile(t_len, (128, 64))
    hb = tm // HALO
    n_i = t_len // tm
    last_hb = t_len // HALO - 1

    def body(du_ref, d_ref, dh_ref, cw_ref, o_ref):
        last = pl.program_id(0) == n_i - 1
        halo = jnp.where(last, 0.0, dh_ref[...].astype(F32))
        dd = d_ref[...].astype(F32)
        row = lax.broadcasted_iota(jnp.int32, dd.shape, 0)
        h0 = halo[0:1, :]
        h1 = halo[1:2, :]
        d1 = jnp.where(row == tm - 1, h0, pltpu.roll(dd, tm - 1, 0))
        d2 = jnp.where(row == tm - 1, h1, jnp.where(row == tm - 2, h0, pltpu.roll(dd, tm - 2, 0)))
        cwv = cw_ref[...]
        dg = cwv[2:3, :] * dd + cwv[1:2, :] * d1 + cwv[0:1, :] * d2
        o_ref[:, :f] = du_ref[...]
        o_ref[:, f:] = dg.astype(o_ref.dtype)

    row_s = pl.BlockSpec((tm, f), lambda i: (i, 0))
    return pl.pallas_call(
        body, name=name, grid=(n_i,),
        in_specs=[row_s, row_s, pl.BlockSpec((HALO, f), lambda i: (jnp.minimum((i + 1) * hb, last_hb), 0)),
                  pl.BlockSpec((8, f), lambda i: (0, 0))],
        out_specs=pl.BlockSpec((tm, 2 * f), lambda i: (i, 0)),
        out_shape=jax.ShapeDtypeStruct((t_len, 2 * f), BF16), compiler_params=_params(1))(du, dgc, dgc, cw)


def _adamw(name, w, g, m, v):
    rows, cols = w.shape
    tr = _tile(rows, (256, 128, 64, 32, 16, 8))
    c1 = 1.0 / (1.0 - ADAM_B1 ** ADAM_STEP)
    c2 = 1.0 / (1.0 - ADAM_B2 ** ADAM_STEP)

    def body(w_ref, g_ref, m_ref, v_ref, d_ref, nm_ref, nv_ref):
        gv = g_ref[...]
        nm = ADAM_B1 * m_ref[...] + (1.0 - ADAM_B1) * gv
        nv = ADAM_B2 * v_ref[...] + (1.0 - ADAM_B2) * (gv * gv)
        nm_ref[...] = nm
        nv_ref[...] = nv
        d_ref[...] = -ADAM_LR * ((nm * c1) / (jnp.sqrt(nv * c2) + ADAM_EPS) + ADAM_WD * w_ref[...])

    blk = pl.BlockSpec((tr, cols), lambda i: (i, 0))
    shp = jax.ShapeDtypeStruct((rows, cols), F32)
    return pl.pallas_call(body, name=name, grid=(rows // tr,), in_specs=[blk] * 4, out_specs=(blk,) * 3,
                          out_shape=(shp,) * 3, compiler_params=_params(1))(w, g, m, v)


class _NoComm:
    def __init__(self):
        self.grads = {}

    def need(self, group, ws, after):
        return ws

    def reduce(self, group, grads, carry):
        self.grads.update(grads)
        return carry


def _local_step(x, target, ws, norms, small, hooks):
    lay = _layout()

    w_main, w_a = _unpack_gin(ws["gin"])
    hn0 = _rms_fwd("rms_attn0", x, norms["attn0"])
    proj = _mm_plain("gla_proj", hn0, w_main, NN, F32)
    a = _mm_plain("gla_proj_a", hn0, w_a, NN, BF16)
    ga, cum = _gla_gate_fwd(a, small["w_a2p"], small["b_a2"])
    o_gla, states = _gla_fwd(proj, cum)
    gated = _gla_out_fwd(o_gla, proj, small["head_norm"])
    h1 = _mm_act_wr("gla_out", gated, ws["gout"], lay["gout"], add=x)

    def ffn_fwd(l, h):
        hn = _rms_fwd(f"rms_ffn{l}", h, norms[f"ffn{l}"])
        up = _mm_act_wc(f"ffn_up{l}", hn, ws[f"up{l}"], lay[f"up{l}"], F32)
        act = _glu_fwd(f"glu_fwd{l}", up, small["conv_w"][l], small["conv_b"][l])
        return hn, up, act, _mm_act_wr(f"ffn_down{l}", act, ws[f"down{l}"], lay[f"down{l}"], add=h)

    ws = hooks.need("B", ws, h1)
    hnf0, up0, act0, h2 = ffn_fwd(0, h1)

    ws = hooks.need("C", ws, h2)
    kvn = _rms_fwd("rms_kv", h2, norms["kv"])
    kv = _mm_act_wc("kv_proj", kvn, ws["wkv"], lay["wkv"], BF16)
    hn1 = _rms_fwd("rms_attn1", h2, norms["attn1"])
    q_all = _mm_act_wc("q_proj", hn1, ws["wq"], lay["wq"], BF16)
    branch = [_att_fwd(q_all, kv, g) for g in range(3)]
    o_att, o_att_b, lse = _att_merge([br[0] for br in branch], [br[1] for br in branch])
    h3 = _mm_act_wr("att_out", o_att_b, ws["dout"], lay["dout"], add=h2)
    hnf1, up1, act1, h4 = ffn_fwd(1, h3)

    dh4, d_final, loss = _loss_head(h4, norms["final"], target)

    sm = {"final": d_final}

    def ffn_bwd(l, dh, h, hn, up, act):
        big = {}
        dact = _mm_dact_wrT(f"ffn_down_dx{l}", dh, ws[f"down{l}"], lay[f"down{l}"])
        big[f"down{l}"] = _mm_grad_wr(f"ffn_down_dw{l}", act, dh, lay[f"down{l}"])
        du, dgc, w0, w1, w2, db = _glu_bwd_a(f"glu_bwd_a{l}", dact, up, small["conv_w"][l], small["conv_b"][l])
        sm[f"conv_w{l}"] = (w0, w1, w2)
        sm[f"conv_b{l}"] = db
        dup = _glu_bwd_b(f"glu_bwd_b{l}", du, dgc, small["conv_w"][l])
        dhn = _mm_dact_wcT(f"ffn_up_dx{l}", dup, ws[f"up{l}"], lay[f"up{l}"])
        big[f"up{l}"] = _mm_grad_wc(f"ffn_up_dw{l}", hn, dup, lay[f"up{l}"])
        dh_in, sm[f"ffn{l}"] = _rms_bwd(f"rms_ffn_bwd{l}", dhn, h, norms[f"ffn{l}"], dh)
        return hooks.reduce(f"ffn{l}", big, dh_in)

    dh3 = ffn_bwd(1, dh4, h3, hnf1, up1, act1)

    big = {}
    do_att = _mm_dact_wrT("att_out_dx", dh3, ws["dout"], lay["dout"])
    big["dout"] = _mm_grad_wr("att_out_dw", o_att_b, dh3, lay["dout"])
    bw = [_att_bwd(q_all, kv, o_att, lse, do_att, g) for g in range(3)]
    dq_all = jnp.concatenate([t[0] for t in bw], axis=1)
    dhn1 = _mm_dact_wcT("q_proj_dx", dq_all, ws["wq"], lay["wq"])
    big["wq"] = _mm_grad_wc("q_proj_dw", hn1, dq_all, lay["wq"])
    dh2, sm["attn1"] = _rms_bwd("rms_attn1_bwd", dhn1, h2, norms["attn1"], dh3)
    dkv = _kv_grad_sum([t[1] for t in bw], [t[2] for t in bw])
    dkvn = _mm_dact_wcT("kv_proj_dx", dkv, ws["wkv"], lay["wkv"])
    big["wkv"] = _mm_grad_wc("kv_proj_dw", kvn, dkv, lay["wkv"])
    dh2, sm["kv"] = _rms_bwd("rms_kv_bwd", dkvn, h2, norms["kv"], dh2)
    dh2 = hooks.reduce("att", big, dh2)

    dh1 = ffn_bwd(0, dh2, h1, hnf0, up0, act0)

    big = {}
    dgated = _mm_dact_wrT("gla_out_dx", dh1, ws["gout"], lay["gout"])
    big["gout"] = _mm_grad_wr("gla_out_dw", gated, dh1, lay["gout"])
    do_gla, dr, sm["head_norm"] = _gla_out_bwd(dgated, o_gla, proj, small["head_norm"])
    dq, dk, dv, dcum = _gla_bwd(proj, cum, states, do_gla)
    da, sm["w_a2p"], sm["b_a2"] = _gla_gate_bwd(dcum, ga, a, small["w_a2p"])
    dproj = jnp.concatenate([dq, dk, dv, dr], axis=1)
    dhn0 = _mm_plain("gla_proj_dx", dproj, w_main, NT, F32)
    dhn0 = _mm_plain("gla_proj_a_dx", da, w_a, NT, F32, add=dhn0)
    gin_main = _mm_plain("gla_proj_dw", hn0, dproj, TN, BF16)
    gin_a = _mm_plain("gla_proj_a_dw", hn0, da, TN, BF16)
    big["gin"] = _pack_gin_grad(gin_main, gin_a)
    grad_x, sm["attn0"] = _rms_bwd("rms_attn0_bwd", dhn0, x, norms["attn0"], dh1)
    grad_x = hooks.reduce("gla", big, grad_x)
    return loss, grad_x, sm


def _pack_weights(chip, gla_w_in, gla_w_out, w_kv, dsa_w_q, dsa_w_out, ffn_w_up, ffn_w_down):
    gin = gla_w_in[0]
    gin = jnp.pad(gin, ((0, 0), (0, _roundup(gin.shape[1], LANE) - gin.shape[1])))
    shards = {"gin": gin, "gout": gla_w_out[0], "up0": ffn_w_up[0], "up1": ffn_w_up[1], "down0": ffn_w_down[0],
              "down1": ffn_w_down[1], "wq": dsa_w_q[0], "wkv": w_kv, "dout": dsa_w_out[0]}
    out = {}
    for name, w in shards.items():
        buf = jnp.zeros((N_CHIPS,) + w.shape, BF16)
        out[name] = lax.dynamic_update_slice(buf, w.astype(BF16)[None], (chip, 0, 0))
    return out


def _unpack_gin(w_gin):
    w = _layout()["gin"][1]
    full = jnp.transpose(w_gin[:, :, :w], (1, 0, 2)).reshape(D_MODEL, N_CHIPS * w)
    n_main = 2 * GLA_KEY_DIM + 2 * GLA_VAL_DIM
    w_a = jnp.pad(full[:, n_main:], ((0, 0), (0, A_PAD - GATE_RANK)))
    return full[:, :n_main], w_a


def _pack_gin_grad(gin_main, gin_a):
    w = _layout()["gin"][1]
    gin = jnp.concatenate([gin_main, gin_a[:, :GATE_RANK]], axis=1)
    gin = jnp.transpose(gin.reshape(D_MODEL, N_CHIPS, w), (1, 0, 2))
    return jnp.pad(gin, ((0, 0), (0, 0), (0, _roundup(w, LANE) - w)))


def _small_params(attn_norm, ffn_norm, kv_norm, final_norm, conv_b, w_a2, b_a2, head_norm, conv_w):
    norms = {"attn0": attn_norm[0:1], "attn1": attn_norm[1:2], "ffn0": ffn_norm[0:1], "ffn1": ffn_norm[1:2],
             "kv": kv_norm[None, :], "final": final_norm[None, :]}
    small = {"w_a2p": jnp.pad(w_a2, ((0, A_PAD - GATE_RANK), (0, 0))), "b_a2": b_a2[None, :],
             "head_norm": head_norm[None, :], "conv_w": jnp.pad(conv_w, ((0, 0), (0, 8 - conv_w.shape[1]), (0, 0))),
             "conv_b": conv_b[:, None, :]}
    return norms, small


ANY = pl.BlockSpec(memory_space=pl.ANY)


def _place():
    return lax.axis_index("x"), lax.axis_index("y"), lax.axis_index("c")


def _other_chips(x, y):
    return [(1 - x, y), (x, 1 - y), (1 - x, 1 - y)]


def _rcopy(src, dst, ssem, rsem, dev):
    return pltpu.make_async_remote_copy(src_ref=src, dst_ref=dst, send_sem=ssem, recv_sem=rsem, device_id=dev,
                                        device_id_type=MESH)


def _gather_weights(name, arrs):
    n = len(arrs)

    def body(*refs):
        ins, outs = refs[:n], refs[n:2 * n]
        send, recv, fsend, frecv = refs[2 * n:]
        x, y, c = _place()
        me = 2 * x + y
        chips = _other_chips(x, y)
        sib = (x, y, 1 - c)
        first, fwd = [], []
        for a in range(n):
            h = ins[a].shape[1] // 2
            mine = pl.ds(c * h, h)
            for j, (px, py) in enumerate(chips):
                cp = _rcopy(ins[a].at[me, mine], outs[a].at[me, mine], send.at[3 * a + j], recv.at[3 * a + j], (px, py, c))
                cp.start()
                first.append(cp)
        for a in range(n):
            h = ins[a].shape[1] // 2
            mine = pl.ds(c * h, h)
            for j, (px, py) in enumerate(chips):
                landed = outs[a].at[2 * px + py, mine]
                _rcopy(landed, landed, send.at[3 * a + j], recv.at[3 * a + j], (px, py, c)).wait_recv()
                cp = _rcopy(landed, landed, fsend.at[3 * a + j], frecv.at[3 * a + j], sib)
                cp.start()
                fwd.append(cp)
        for a in range(n):
            h = ins[a].shape[1] // 2
            theirs = pl.ds((1 - c) * h, h)
            for j, (px, py) in enumerate(chips):
                got = outs[a].at[2 * px + py, theirs]
                _rcopy(got, got, fsend.at[3 * a + j], frecv.at[3 * a + j], sib).wait_recv()
        for cp in first + fwd:
            cp.wait_send()

    return pl.pallas_call(
        body, name=name, in_specs=[ANY] * n, out_specs=[ANY] * n,
        out_shape=[jax.ShapeDtypeStruct(a.shape, a.dtype) for a in arrs],
        input_output_aliases={a: a for a in range(n)},
        scratch_shapes=[pltpu.SemaphoreType.DMA((3 * n,)), pltpu.SemaphoreType.DMA((3 * n,)),
                        pltpu.SemaphoreType.DMA((3 * n,)), pltpu.SemaphoreType.DMA((3 * n,))])(*arrs)


def _swap_halves(name, arrs):
    n = len(arrs)

    def body(*refs):
        ins, outs = refs[:n], refs[n:2 * n]
        send, recv = refs[2 * n:]
        x, y, c = _place()
        cps = []
        for a in range(n):
            h = ins[a].shape[1] // 2
            cp = _rcopy(ins[a].at[:, pl.ds((1 - c) * h, h)], outs[a], send.at[a], recv.at[a], (x, y, 1 - c))
            cp.start()
            cps.append(cp)
        for cp in cps:
            cp.wait()

    return pl.pallas_call(
        body, name=name, in_specs=[ANY] * n, out_specs=[ANY] * n,
        out_shape=[jax.ShapeDtypeStruct((a.shape[0], a.shape[1] // 2, a.shape[2]), a.dtype) for a in arrs],
        scratch_shapes=[pltpu.SemaphoreType.DMA((n,)), pltpu.SemaphoreType.DMA((n,))])(*arrs)


SEM = pl.BlockSpec(memory_space=pltpu.SEMAPHORE)
EFFECT = pltpu.SideEffectType.DATAFLOW_SIDE_EFFECTING


def _shapes(arrs):
    return [jax.ShapeDtypeStruct(a.shape, a.dtype) for a in arrs]


def _gather_start(name, thru, arrs):
    n, nt = len(arrs), len(thru)

    def body(*refs):
        ins = refs[nt:nt + n]
        send, recv = refs[nt + n], refs[nt + n + 1]
        outs = refs[2 * nt + n + 2:]
        x, y, c = _place()
        me = 2 * x + y
        for a in range(n):
            h = ins[a].shape[1] // 2
            mine = pl.ds(c * h, h)
            for j, (px, py) in enumerate(_other_chips(x, y)):
                _rcopy(ins[a].at[me, mine], outs[a].at[me, mine], send.at[3 * a + j], recv.at[3 * a + j], (px, py, c)).start()

    res = pl.pallas_call(
        body, name=name, in_specs=[ANY] * (nt + n), out_specs=[SEM, SEM] + [ANY] * (nt + n),
        out_shape=[pltpu.SemaphoreType.DMA((3 * n,)), pltpu.SemaphoreType.DMA((3 * n,))] + _shapes(thru) + _shapes(arrs),
        input_output_aliases={i: 2 + i for i in range(nt + n)},
        compiler_params=pltpu.CompilerParams(has_side_effects=EFFECT))(*thru, *arrs)
    return res[0], res[1], res[2:2 + nt], res[2 + nt:]


def _gather_wait(name, send, recv, arrs, after):
    n = len(arrs)

    def body(*refs):
        ins = refs[:n]
        send_ref, recv_ref = refs[n], refs[n + 1]
        x, y, c = _place()
        me = 2 * x + y
        for a in range(n):
            h = ins[a].shape[1] // 2
            mine = pl.ds(c * h, h)
            for j, (px, py) in enumerate(_other_chips(x, y)):
                sent = ins[a].at[me, mine]
                landed = ins[a].at[2 * px + py, mine]
                cp = _rcopy(sent, landed, send_ref.at[3 * a + j], recv_ref.at[3 * a + j], (px, py, c))
                cp.wait_send()
                cp.wait_recv()

    return pl.pallas_call(
        body, name=name, in_specs=[ANY] * n + [SEM, SEM, ANY], out_specs=[ANY] * n, out_shape=_shapes(arrs),
        input_output_aliases={a: a for a in range(n)},
        compiler_params=pltpu.CompilerParams(has_side_effects=EFFECT))(*arrs, send, recv, after)


def _forward_halves(name, arrs):
    n = len(arrs)

    def body(*refs):
        ins, outs = refs[:n], refs[n:2 * n]
        send, recv = refs[2 * n:]
        x, y, c = _place()
        sib = (x, y, 1 - c)
        chips = _other_chips(x, y)
        cps = []
        for a in range(n):
            h = ins[a].shape[1] // 2
            mine = pl.ds(c * h, h)
            for j, (px, py) in enumerate(chips):
                cp = _rcopy(ins[a].at[2 * px + py, mine], outs[a].at[2 * px + py, mine], send.at[3 * a + j],
                            recv.at[3 * a + j], sib)
                cp.start()
                cps.append(cp)
        for a in range(n):
            h = ins[a].shape[1] // 2
            theirs = pl.ds((1 - c) * h, h)
            for j, (px, py) in enumerate(chips):
                got = outs[a].at[2 * px + py, theirs]
                _rcopy(got, got, send.at[3 * a + j], recv.at[3 * a + j], sib).wait_recv()
        for cp in cps:
            cp.wait_send()

    return pl.pallas_call(
        body, name=name, in_specs=[ANY] * n, out_specs=[ANY] * n, out_shape=_shapes(arrs),
        input_output_aliases={a: a for a in range(n)},
        scratch_shapes=[pltpu.SemaphoreType.DMA((3 * n,)), pltpu.SemaphoreType.DMA((3 * n,))])(*arrs)


def _scatter_start(name, thru, arrs):
    n, nt = len(arrs), len(thru)
    landing = [jnp.zeros_like(a) for a in arrs]

    def body(*refs):
        ins = refs[nt:nt + n]
        send, recv = refs[nt + 2 * n], refs[nt + 2 * n + 1]
        outs = refs[2 * nt + 3 * n + 2:]
        x, y, c = _place()
        me = 2 * x + y
        for a in range(n):
            for j, (px, py) in enumerate(_other_chips(x, y)):
                _rcopy(ins[a].at[2 * px + py], outs[a].at[me], send.at[3 * a + j], recv.at[3 * a + j], (px, py, c)).start()

    res = pl.pallas_call(
        body, name=name, in_specs=[ANY] * (nt + 2 * n), out_specs=[SEM, SEM] + [ANY] * (nt + 2 * n),
        out_shape=[pltpu.SemaphoreType.DMA((3 * n,)), pltpu.SemaphoreType.DMA((3 * n,))] + _shapes(thru) + _shapes(arrs)
        + _shapes(landing),
        input_output_aliases={i: 2 + i for i in range(nt + 2 * n)},
        compiler_params=pltpu.CompilerParams(has_side_effects=EFFECT))(*thru, *arrs, *landing)
    return res[0], res[1], res[2:2 + nt], res[2 + nt:2 + nt + n], res[2 + nt + n:]


def _scatter_wait(name, send, recv, arrs, landing, after):
    n = len(arrs)

    def body(*refs):
        ins, land = refs[:n], refs[n:2 * n]
        send_ref, recv_ref = refs[2 * n], refs[2 * n + 1]
        x, y, c = _place()
        for a in range(n):
            for j, (px, py) in enumerate(_other_chips(x, y)):
                cp = _rcopy(ins[a].at[2 * px + py], land[a].at[2 * px + py], send_ref.at[3 * a + j], recv_ref.at[3 * a + j],
                            (px, py, c))
                cp.wait_send()
                cp.wait_recv()

    res = pl.pallas_call(
        body, name=name, in_specs=[ANY] * (2 * n) + [SEM, SEM, ANY], out_specs=[ANY] * (2 * n),
        out_shape=_shapes(arrs) + _shapes(landing), input_output_aliases={i: i for i in range(2 * n)},
        compiler_params=pltpu.CompilerParams(has_side_effects=EFFECT))(*arrs, *landing, send, recv, after)
    return res[:n], res[n:]


def _join_halves(name, arrs):
    n = len(arrs)

    def body(*refs):
        ins, outs = refs[:n], refs[n:2 * n]
        send, recv = refs[2 * n:]
        x, y, c = _place()
        cps = []
        for a in range(n):
            h = ins[a].shape[0] // 2
            mine = pl.ds(c * h, h)
            cp = _rcopy(ins[a].at[mine], outs[a].at[mine], send.at[a], recv.at[a], (x, y, 1 - c))
            cp.start()
            cps.append(cp)
        for a in range(n):
            h = ins[a].shape[0] // 2
            got = outs[a].at[pl.ds((1 - c) * h, h)]
            _rcopy(got, got, send.at[a], recv.at[a], (x, y, 1 - c)).wait_recv()
        for cp in cps:
            cp.wait_send()

    return pl.pallas_call(
        body, name=name, in_specs=[ANY] * n, out_specs=[ANY] * n,
        out_shape=[jax.ShapeDtypeStruct(a.shape, a.dtype) for a in arrs],
        input_output_aliases={a: a for a in range(n)},
        scratch_shapes=[pltpu.SemaphoreType.DMA((n,)), pltpu.SemaphoreType.DMA((n,))])(*arrs)


def _allgather8(name, xs, reduce):
    m_per, n = xs.shape

    def body(x_ref, out_ref, *rest):
        if reduce:
            sum_ref, send, recv, lsem = rest
        else:
            send, recv, lsem = rest
        x, y, c = _place()
        me, sib = (x, y, c), (x, y, 1 - c)
        chips = _other_chips(x, y)

        def rows(px, py, pc):
            return out_ref.at[pl.ds((4 * px + 2 * py + pc) * m_per, m_per), :]

        def copy(k, block, to, src=None):
            return _rcopy(rows(*block) if src is None else src, rows(*block), send.at[k], recv.at[k], to)

        mine = pltpu.make_async_copy(x_ref, rows(*me), lsem)
        mine.start()
        first = [copy(0, me, sib, src=x_ref)]
        first += [copy(1 + j, me, (*chip, c), src=x_ref) for j, chip in enumerate(chips)]
        for cp in first:
            cp.start()
        passed = [copy(4 + j, (*chip, c), sib) for j, chip in enumerate(chips)]
        for j, chip in enumerate(chips):
            copy(1 + j, (*chip, c), me).wait_recv()
            passed[j].start()
        copy(0, sib, me).wait_recv()
        for j, chip in enumerate(chips):
            copy(4 + j, (*chip, 1 - c), me).wait_recv()
        for cp in first + passed:
            cp.wait_send()
        mine.wait()
        if reduce:
            acc = out_ref[pl.ds(0, m_per), :]
            for dev in range(1, N_DEV):
                acc = acc + out_ref[pl.ds(dev * m_per, m_per), :]
            sum_ref[...] = acc

    vm = pl.BlockSpec(memory_space=pltpu.VMEM)
    out_shape = [jax.ShapeDtypeStruct((N_DEV * m_per, n), xs.dtype)]
    if reduce:
        out_shape.append(jax.ShapeDtypeStruct((m_per, n), xs.dtype))
    return pl.pallas_call(
        body, name=name, in_specs=[vm], out_specs=[vm] * len(out_shape), out_shape=out_shape,
        scratch_shapes=[pltpu.SemaphoreType.DMA((7,)), pltpu.SemaphoreType.DMA((7,)), pltpu.SemaphoreType.DMA],
        compiler_params=pltpu.CompilerParams(vmem_limit_bytes=VMEM_LIMIT))(xs)


def _add_my_half(name, a, rb, c_arr):
    s, h, cols = rb.shape
    tr = _tile(h, (512, 352, 256, 128, 64, 32, 16))
    nt = h // tr

    def body(c_ref, a_ref, b_ref, o_ref):
        o_ref[...] = (a_ref[...].astype(F32) + b_ref[...].astype(F32)).astype(o_ref.dtype)

    return pl.pallas_call(
        body, name=name,
        grid_spec=pltpu.PrefetchScalarGridSpec(
            num_scalar_prefetch=1, grid=(s, nt),
            in_specs=[pl.BlockSpec((None, tr, cols), lambda k, i, c: (k, c[0] * nt + i, 0)),
                      pl.BlockSpec((None, tr, cols), lambda k, i, c: (k, i, 0))],
            out_specs=pl.BlockSpec((None, tr, cols), lambda k, i, c: (k, i, 0))),
        out_shape=jax.ShapeDtypeStruct(rb.shape, BF16), compiler_params=_params(2))(c_arr, a, rb)


def _sum_chips(name, own, q, place):
    s, h, cols = q.shape
    tr = _tile(h, (512, 352, 256, 128, 64, 32, 16))
    nt = h // tr

    def body(p_ref, own_ref, q_ref, o_ref):
        chip = p_ref[0]
        acc = jnp.where(chip == 0, own_ref[0], q_ref[0]).astype(F32)
        for j in range(1, s):
            acc = acc + jnp.where(chip == j, own_ref[j], q_ref[j]).astype(F32)
        o_ref[...] = acc

    blk = pl.BlockSpec((s, tr, cols), lambda i, p: (0, i, 0))
    return pl.pallas_call(
        body, name=name,
        grid_spec=pltpu.PrefetchScalarGridSpec(
            num_scalar_prefetch=1, grid=(nt,), in_specs=[blk, blk],
            out_specs=pl.BlockSpec((tr, cols), lambda i, p: (p[1] * nt + i, 0))),
        out_shape=jax.ShapeDtypeStruct((2 * h, cols), F32), compiler_params=_params(1))(place, own, q)


def _pack_rows(parts):
    rows = []
    for p in parts:
        flat = p.reshape(-1).astype(F32)
        n = _roundup(flat.shape[0], 8 * LANE)
        rows.append(jnp.pad(flat, (0, n - flat.shape[0])).reshape(-1, LANE))
    return jnp.concatenate(rows, axis=0)


def _unpack_rows(buf, shapes):
    out, r = [], 0
    for shp in shapes:
        size = math.prod(shp)
        nr = _roundup(size, 8 * LANE) // LANE
        out.append(buf[r:r + nr].reshape(-1)[:size].reshape(shp))
        r += nr
    return out


def kernel(x, attn_norm, gla_w_in, gla_w_a2, gla_b_a2, gla_head_norm, gla_w_out, kv_norm, w_kv, dsa_w_q, dsa_w_out, ffn_norm, ffn_w_up, ffn_conv_w, ffn_conv_b, ffn_w_down, final_norm, loss_target, m_attn_norm, m_gla_w_in, m_gla_w_a2, m_gla_b_a2, m_gla_head_norm, m_gla_w_out, m_kv_norm, m_w_kv, m_dsa_w_q, m_dsa_w_out, m_ffn_norm, m_ffn_w_up, m_ffn_conv_w, m_ffn_conv_b, m_ffn_w_down, m_final_norm, v_attn_norm, v_gla_w_in, v_gla_w_a2, v_gla_b_a2, v_gla_head_norm, v_gla_w_out, v_kv_norm, v_w_kv, v_dsa_w_q, v_dsa_w_out, v_ffn_norm, v_ffn_w_up, v_ffn_conv_w, v_ffn_conv_b, v_ffn_w_down, v_final_norm):
    lay = _layout()
    d, f = D_MODEL, D_FF
    cx, cy, cc = _place()
    chip = 2 * cx + cy
    c_arr = jnp.reshape(cc, (1,)).astype(jnp.int32)
    place = jnp.stack([chip, cc]).astype(jnp.int32)

    groups = {"A": ("gin", "gout"), "B": ("up0", "down0"), "C": ("up1", "down1", "wq", "wkv", "dout")}
    ws = _pack_weights(chip, gla_w_in, gla_w_out, w_kv, dsa_w_q, dsa_w_out, ffn_w_up, ffn_w_down)
    ws.update(zip(groups["A"], _gather_weights("gather_a", [ws[k] for k in groups["A"]])))
    in_flight = {}
    thru = [ws[k] for k in groups["A"]]
    for grp in ("B", "C"):
        send, recv, thru, arrs = _gather_start(f"gather_{grp.lower()}_start", thru, [ws[k] for k in groups[grp]])
        ws.update(zip(groups[grp], arrs))
        in_flight[grp] = (send, recv)
    ws.update(zip(groups["A"], thru))
    pending = []

    class _Comm:
        def need(self, grp, ws, after):
            send, recv = in_flight[grp]
            arrs = _gather_wait(f"gather_{grp.lower()}_wait", send, recv, [ws[k] for k in groups[grp]], after)
            arrs = _forward_halves(f"forward_{grp.lower()}", arrs)
            return {**ws, **dict(zip(groups[grp], arrs))}

        def reduce(self, grp, grads, carry):
            names = list(grads)
            parts = [grads[k] for k in names]
            theirs = _swap_halves(f"swap_{grp}", parts)
            sums = [_add_my_half(f"add_half_{k}", a, b, c_arr) for k, a, b in zip(names, parts, theirs)]
            last = grp == "gla"
            send, recv, thru, sums, landing = _scatter_start(f"scatter_{grp}_start", [] if last else [carry], sums)
            pending.append((grp, names, send, recv, sums, landing))
            return carry if last else thru[0]

    sharded_small = [gla_w_a2[0], gla_b_a2[0], gla_head_norm[0], ffn_conv_w]
    gathered = _allgather8("gather_small", _pack_rows(sharded_small), False)[0]
    per_dev = gathered.reshape(N_DEV, -1, LANE)
    shards = [_unpack_rows(per_dev[2 * s], [p.shape for p in sharded_small]) for s in range(N_CHIPS)]
    w_a2, b_a2, head_norm, conv_w = [jnp.concatenate([shards[s][k] for s in range(N_CHIPS)], axis=-1) for k in range(4)]
    norms, small = _small_params(attn_norm, ffn_norm, kv_norm, final_norm, ffn_conv_b, w_a2, b_a2, head_norm, conv_w)

    loss_blk, grad_x, sm = _local_step(x[0], loss_target[0], ws, norms, small, _Comm())

    gin_w = lay["gin"][1]
    full = {}
    after = pending[-1][4][0]
    for grp, names, send, recv, sums, landing in pending:
        sums, landing = _scatter_wait(f"scatter_{grp}_wait", send, recv, sums, landing, after)
        halves = [_sum_chips(f"sum_chips_{k}", s, q, place) for k, s, q in zip(names, sums, landing)]
        joined = _join_halves(f"join_{grp}", halves)
        full.update(zip(names, joined))
        after = joined[0]

    small_parts = [loss_blk, jnp.concatenate([sm["attn0"], sm["attn1"]]), jnp.concatenate([sm["ffn0"], sm["ffn1"]]),
                   sm["kv"], sm["final"], jnp.concatenate([sm["conv_b0"], sm["conv_b1"]]),
                   sm["w_a2p"][:GATE_RANK], sm["b_a2"], sm["head_norm"],
                   jnp.stack([jnp.concatenate(sm["conv_w0"]), jnp.concatenate(sm["conv_w1"])])]
    small_shapes = [(8, LANE), (2, d), (2, d), (d,), (d,), (2, f), (GATE_RANK, GLA_KEY_DIM), (GLA_KEY_DIM,),
                    (GLA_VAL_DIM // GLA_HEADS,), (2, 3, f)]
    _, reduced = _allgather8("reduce_small", _pack_rows(small_parts), True)
    loss_r, g_attn, g_ffn, g_kv, g_final, g_cb, g_a2, g_ba2, g_hn, g_cw = _unpack_rows(reduced, small_shapes)
    loss = loss_r[0, 0]

    def mine(g, axis):
        w = g.shape[axis] // N_CHIPS
        return lax.dynamic_slice_in_dim(g, chip * w, w, axis)

    grads = {
        "attn_norm": g_attn, "gla_w_in": full["gin"][None, :, :gin_w], "gla_w_a2": mine(g_a2, 1)[None],
        "gla_b_a2": mine(g_ba2, 0)[None], "gla_head_norm": mine(g_hn, 0)[None], "gla_w_out": full["gout"][None],
        "kv_norm": g_kv, "w_kv": full["wkv"], "dsa_w_q": full["wq"][None], "dsa_w_out": full["dout"][None],
        "ffn_norm": g_ffn, "ffn_w_up": jnp.stack([full["up0"], full["up1"]]), "ffn_conv_w": mine(g_cw, 2),
        "ffn_conv_b": g_cb, "ffn_w_down": jnp.stack([full["down0"], full["down1"]]), "final_norm": g_final,
    }
    weights = {"attn_norm": (attn_norm, m_attn_norm, v_attn_norm), "gla_w_in": (gla_w_in, m_gla_w_in, v_gla_w_in),
               "gla_w_a2": (gla_w_a2, m_gla_w_a2, v_gla_w_a2), "gla_b_a2": (gla_b_a2, m_gla_b_a2, v_gla_b_a2),
               "gla_head_norm": (gla_head_norm, m_gla_head_norm, v_gla_head_norm),
               "gla_w_out": (gla_w_out, m_gla_w_out, v_gla_w_out), "kv_norm": (kv_norm, m_kv_norm, v_kv_norm),
               "w_kv": (w_kv, m_w_kv, v_w_kv), "dsa_w_q": (dsa_w_q, m_dsa_w_q, v_dsa_w_q),
               "dsa_w_out": (dsa_w_out, m_dsa_w_out, v_dsa_w_out), "ffn_norm": (ffn_norm, m_ffn_norm, v_ffn_norm),
               "ffn_w_up": (ffn_w_up, m_ffn_w_up, v_ffn_w_up), "ffn_conv_w": (ffn_conv_w, m_ffn_conv_w, v_ffn_conv_w),
               "ffn_conv_b": (ffn_conv_b, m_ffn_conv_b, v_ffn_conv_b),
               "ffn_w_down": (ffn_w_down, m_ffn_w_down, v_ffn_w_down), "final_norm": (final_norm, m_final_norm, v_final_norm)}
    order = list(weights)
    big_names = ("gla_w_in", "gla_w_out", "w_kv", "dsa_w_q", "dsa_w_out", "ffn_w_up", "ffn_w_down")
    delta, new_m, new_v = {}, {}, {}
    for k in big_names:
        w, m, v = weights[k]
        cols = w.shape[-1]
        res = _adamw(f"adamw_{k}", w.reshape(-1, cols), grads[k].reshape(-1, cols), m.reshape(-1, cols), v.reshape(-1, cols))
        delta[k], new_m[k], new_v[k] = [r.reshape(w.shape) for r in res]
    small_names = [k for k in order if k not in big_names]
    packed = [_pack_rows([src[k] for k in small_names])
              for src in ({k: weights[k][0] for k in small_names}, grads, {k: weights[k][1] for k in small_names},
                          {k: weights[k][2] for k in small_names})]
    res = _adamw("adamw_small", *packed)
    shapes = [weights[k][0].shape for k in small_names]
    for dst, buf in zip((delta, new_m, new_v), res):
        for k, val in zip(small_names, _unpack_rows(buf, shapes)):
            dst[k] = val
    return (loss, grad_x[None], *[grads[k] for k in order], *[delta[k] for k in order], *[new_m[k] for k in order],
            *[new_v[k] for k in order])
```

```python
import math

import jax
import jax.numpy as jnp
from jax import lax
from jax.experimental import pallas as pl
from jax.experimental.pallas import tpu as pltpu

F32 = jnp.float32
BF16 = jnp.bfloat16

D_MODEL = 2048
SEQ = 4096
GLA_HEADS = 4
GLA_KEY_DIM = D_MODEL // 2
GLA_VAL_DIM = D_MODEL
GATE_RANK = 16
GATE_NORMALIZER = 16.0
GLA_CHUNK = 64
ATT_HEADS = 16
HEAD_DIM = 128
WINDOWS = (128, 512, 2048)
DILATIONS = (1, 4, 16)
ATT_BLOCK = 128
D_FF = 5632
EPS = 1e-6
ADAM_LR = 0.001
ADAM_B1 = 0.9
ADAM_B2 = 0.999
ADAM_EPS = 1e-08
ADAM_WD = 0.01
ADAM_STEP = 10

N_CHIPS = 4
N_DEV = 8
LANE = 128
A_PAD = 128
VMEM_LIMIT = 56 * 1024 * 1024
MAX_K_TILE = 2816
NEG = -1e30
MESH = pl.DeviceIdType.MESH

NN = (((1,), (0,)), ((), ()))
NT = (((1,), (1,)), ((), ()))
TN = (((0,), (0,)), ((), ()))


def _tile(n, cands):
    for c in cands:
        if c <= n and n % c == 0:
            return c
    return n


def _roundup(n, m):
    return -(-n // m) * m


def _params(n_axes):
    return pltpu.CompilerParams(dimension_semantics=("arbitrary",) * n_axes, vmem_limit_bytes=VMEM_LIMIT)


def _dot(a, b, dims):
    return lax.dot_general(a, b, dims, preferred_element_type=F32)


def _sigmoid(x):
    return 1.0 / (1.0 + jnp.exp(-x))


COL_SHARDED = ("gin", "up0", "up1", "wq", "wkv")
ROW_SHARDED = ("gout", "down0", "down1", "dout")


def _layout():
    f = D_FF
    hd = ATT_HEADS * HEAD_DIM
    gin = 2 * GLA_KEY_DIM + 2 * GLA_VAL_DIM + GATE_RANK
    up_w = 2 * f // N_CHIPS
    q_w = 3 * hd // N_CHIPS
    kv_w = 2 * hd // N_CHIPS
    dn_r = f // N_CHIPS
    go_r = GLA_VAL_DIM // N_CHIPS
    do_r = hd // N_CHIPS
    big = (1408, 1024, 512, 256, 128)
    return {
        "gin": (0, gin // N_CHIPS, LANE),
        "up0": (0, up_w, _tile(up_w, big)), "up1": (0, up_w, _tile(up_w, big)),
        "wq": (0, q_w, _tile(q_w, (512, 384, 256, 128))), "wkv": (0, kv_w, _tile(kv_w, (1024, 512, 256, 128))),
        "down0": (0, dn_r, _tile(dn_r, big)), "down1": (0, dn_r, _tile(dn_r, big)),
        "gout": (0, go_r, _tile(go_r, (512, 256, 128))), "dout": (0, do_r, _tile(do_r, (512, 256, 128))),
    }


def _matmul(name, a, b, dims, grid, a_spec, b_spec, o_spec, out_shape, acc_shape, add=None, add_spec=None):
    nk = grid[2]
    has_add = add is not None

    def body(*refs):
        a_ref, b_ref = refs[0], refs[1]
        pos = 2
        add_ref = None
        if has_add:
            add_ref = refs[pos]
            pos += 1
        o_ref = refs[pos]
        prod = _dot(a_ref[...].astype(BF16), b_ref[...].astype(BF16), dims)

        def finish(val):
            if has_add:
                val = val + add_ref[...].astype(F32)
            o_ref[...] = val.astype(o_ref.dtype)

        if nk == 1:
            finish(prod)
        else:
            acc_ref = refs[pos + 1]
            k = pl.program_id(2)

            @pl.when(k == 0)
            def _():
                acc_ref[...] = prod

            @pl.when(k > 0)
            def _():
                acc_ref[...] += prod

            @pl.when(k == nk - 1)
            def _():
                finish(acc_ref[...])

    in_specs = [a_spec, b_spec]
    args = [a, b]
    if has_add:
        in_specs.append(add_spec)
        args.append(add)
    scratch = [] if nk == 1 else [pltpu.VMEM(acc_shape, F32)]
    return pl.pallas_call(body, name=name, grid=grid, in_specs=in_specs, out_specs=o_spec, out_shape=out_shape,
                          scratch_shapes=scratch, compiler_params=_params(3))(*args)


def _mm_act_wc(name, a, wc, seg, out_dtype):
    off, w, tn = seg
    t_len, d = a.shape
    tm = _tile(t_len, (1024, 512, 256, 128))
    nps = w // tn
    ob = off // tn
    grid = (t_len // tm, N_CHIPS * nps, 1)
    return _matmul(
        name, a, wc, NN, grid,
        pl.BlockSpec((tm, d), lambda i, j, k: (i, 0)),
        pl.BlockSpec((None, d, tn), lambda i, j, k: (j // nps, 0, ob + j % nps)),
        pl.BlockSpec((tm, tn), lambda i, j, k: (i, j)),
        jax.ShapeDtypeStruct((t_len, N_CHIPS * w), out_dtype), (tm, tn))


def _mm_dact_wcT(name, dy, wc, seg, add=None):
    off, w, tk = seg
    if off == 0 and w <= MAX_K_TILE:
        tk = w
    t_len = dy.shape[0]
    d = wc.shape[1]
    tm = _tile(t_len, (1024, 512, 256, 128))
    tn = _tile(d, (1024, 512, 256, 128))
    kps = w // tk
    ob = off // tk
    grid = (t_len // tm, d // tn, N_CHIPS * kps)
    return _matmul(
        name, dy, wc, NT, grid,
        pl.BlockSpec((tm, tk), lambda i, j, k: (i, k)),
        pl.BlockSpec((None, tn, tk), lambda i, j, k: (k // kps, j, ob + k % kps)),
        pl.BlockSpec((tm, tn), lambda i, j, k: (i, j)),
        jax.ShapeDtypeStruct((t_len, d), F32), (tm, tn),
        add=add, add_spec=None if add is None else pl.BlockSpec((tm, tn), lambda i, j, k: (i, j)))


def _mm_grad_wc(name, a, dy, seg):
    _, w, tn = seg
    t_len, d = a.shape
    tm = _tile(d, (1024, 512, 256, 128))
    tk = _tile(t_len, (2048, 1024, 512, 256, 128))
    nps = w // tn
    grid = (d // tm, N_CHIPS * nps, t_len // tk)
    return _matmul(
        name, a, dy, TN, grid,
        pl.BlockSpec((tk, tm), lambda i, j, k: (k, i)),
        pl.BlockSpec((tk, tn), lambda i, j, k: (k, j)),
        pl.BlockSpec((None, tm, tn), lambda i, j, k: (j // nps, i, j % nps)),
        jax.ShapeDtypeStruct((N_CHIPS, d, w), BF16), (tm, tn))


def _mm_act_wr(name, a, wr, seg, add):
    off, r, tk = seg
    t_len = a.shape[0]
    d = wr.shape[2]
    tm = _tile(t_len, (1024, 512, 256, 128))
    tn = _tile(d, (1024, 512, 256, 128))
    kps = r // tk
    ob = off // tk
    grid = (t_len // tm, d // tn, N_CHIPS * kps)
    return _matmul(
        name, a, wr, NN, grid,
        pl.BlockSpec((tm, tk), lambda i, j, k: (i, k)),
        pl.BlockSpec((None, tk, tn), lambda i, j, k: (k // kps, ob + k % kps, j)),
        pl.BlockSpec((tm, tn), lambda i, j, k: (i, j)),
        jax.ShapeDtypeStruct((t_len, d), F32), (tm, tn),
        add=add, add_spec=pl.BlockSpec((tm, tn), lambda i, j, k: (i, j)))


def _mm_dact_wrT(name, dh, wr, seg):
    off, r, tn = seg
    t_len, d = dh.shape
    tm = _tile(t_len, (1024, 512, 256, 128))
    nps = r // tn
    ob = off // tn
    grid = (t_len // tm, N_CHIPS * nps, 1)
    return _matmul(
        name, dh, wr, NT, grid,
        pl.BlockSpec((tm, d), lambda i, j, k: (i, 0)),
        pl.BlockSpec((None, tn, d), lambda i, j, k: (j // nps, ob + j % nps, 0)),
        pl.BlockSpec((tm, tn), lambda i, j, k: (i, j)),
        jax.ShapeDtypeStruct((t_len, N_CHIPS * r), BF16), (tm, tn))


def _mm_grad_wr(name, a, dh, seg):
    _, r, tm = seg
    t_len, d = dh.shape
    tn = _tile(d, (1024, 512, 256, 128))
    tk = _tile(t_len, (2048, 1024, 512, 256, 128))
    mps = r // tm
    grid = (N_CHIPS * mps, d // tn, t_len // tk)
    return _matmul(
        name, a, dh, TN, grid,
        pl.BlockSpec((tk, tm), lambda i, j, k: (k, i)),
        pl.BlockSpec((tk, tn), lambda i, j, k: (k, j)),
        pl.BlockSpec((None, tm, tn), lambda i, j, k: (i // mps, i % mps, j)),
        jax.ShapeDtypeStruct((N_CHIPS, r, d), BF16), (tm, tn))


def _mm_plain(name, a, b, dims, out_dtype, add=None):
    if dims == NN:
        m, kd = a.shape
        n = b.shape[1]
    elif dims == NT:
        m, kd = a.shape
        n = b.shape[0]
    else:
        kd, m = a.shape
        n = b.shape[1]
    tm = _tile(m, (1024, 512, 256, 128))
    tn = _tile(n, (1024, 768, 512, 256, 128))
    tk = _tile(kd, (2048, 1024, 512, 256, 128))
    grid = (m // tm, n // tn, kd // tk)
    if dims == NN:
        a_spec = pl.BlockSpec((tm, tk), lambda i, j, k: (i, k))
        b_spec = pl.BlockSpec((tk, tn), lambda i, j, k: (k, j))
    elif dims == NT:
        a_spec = pl.BlockSpec((tm, tk), lambda i, j, k: (i, k))
        b_spec = pl.BlockSpec((tn, tk), lambda i, j, k: (j, k))
    else:
        a_spec = pl.BlockSpec((tk, tm), lambda i, j, k: (k, i))
        b_spec = pl.BlockSpec((tk, tn), lambda i, j, k: (k, j))
    o_spec = pl.BlockSpec((tm, tn), lambda i, j, k: (i, j))
    return _matmul(name, a, b, dims, grid, a_spec, b_spec, o_spec, jax.ShapeDtypeStruct((m, n), out_dtype), (tm, tn),
                   add=add, add_spec=None if add is None else o_spec)


def _rms_fwd(name, x, g):
    t_len, d = x.shape
    tm = _tile(t_len, (512, 256, 128))

    def body(x_ref, g_ref, o_ref):
        xv = x_ref[...]
        r = lax.rsqrt(jnp.mean(xv * xv, axis=-1, keepdims=True) + EPS)
        o_ref[...] = (xv * r * g_ref[...]).astype(o_ref.dtype)

    return pl.pallas_call(
        body, name=name, grid=(t_len // tm,),
        in_specs=[pl.BlockSpec((tm, d), lambda i: (i, 0)), pl.BlockSpec((1, d), lambda i: (0, 0))],
        out_specs=pl.BlockSpec((tm, d), lambda i: (i, 0)),
        out_shape=jax.ShapeDtypeStruct((t_len, d), BF16), compiler_params=_params(1))(x, g)


def _rms_bwd(name, dy, x, g, dres):
    t_len, d = x.shape
    tm = _tile(t_len, (256, 128))

    def body(dy_ref, x_ref, g_ref, dres_ref, dx_ref, dg_ref):
        xv = x_ref[...]
        r = lax.rsqrt(jnp.mean(xv * xv, axis=-1, keepdims=True) + EPS)
        xhat = xv * r
        dyv = dy_ref[...].astype(F32)
        dxn = dyv * g_ref[...]
        dx = r * (dxn - xhat * jnp.mean(dxn * xhat, axis=-1, keepdims=True))
        dx_ref[...] = dres_ref[...] + dx
        part = jnp.sum(dyv * xhat, axis=0, keepdims=True)

        @pl.when(pl.program_id(0) == 0)
        def _():
            dg_ref[...] = part

        @pl.when(pl.program_id(0) > 0)
        def _():
            dg_ref[...] += part

    row = pl.BlockSpec((tm, d), lambda i: (i, 0))
    vec = pl.BlockSpec((1, d), lambda i: (0, 0))
    return pl.pallas_call(
        body, name=name, grid=(t_len // tm,), in_specs=[row, row, vec, row], out_specs=(row, vec),
        out_shape=(jax.ShapeDtypeStruct((t_len, d), F32), jax.ShapeDtypeStruct((1, d), F32)),
        compiler_params=_params(1))(dy, x, g, dres)


def _loss_head(h, g, target):
    t_len, d = h.shape
    tm = _tile(t_len, (256, 128))

    def body(h_ref, g_ref, t_ref, dh_ref, dg_ref, loss_ref):
        xv = h_ref[...]
        gv = g_ref[...]
        r = lax.rsqrt(jnp.mean(xv * xv, axis=-1, keepdims=True) + EPS)
        xhat = xv * r
        err = xhat * gv - t_ref[...]
        dyv = err * (1.0 / d)
        dxn = dyv * gv
        dh_ref[...] = r * (dxn - xhat * jnp.mean(dxn * xhat, axis=-1, keepdims=True))
        part = jnp.sum(dyv * xhat, axis=0, keepdims=True)
        lpart = jnp.zeros((8, LANE), F32) + (0.5 / d) * jnp.sum(err * err)

        @pl.when(pl.program_id(0) == 0)
        def _():
            dg_ref[...] = part
            loss_ref[...] = lpart

        @pl.when(pl.program_id(0) > 0)
        def _():
            dg_ref[...] += part
            loss_ref[...] += lpart

    row = pl.BlockSpec((tm, d), lambda i: (i, 0))
    vec = pl.BlockSpec((1, d), lambda i: (0, 0))
    return pl.pallas_call(
        body, name="loss_head", grid=(t_len // tm,), in_specs=[row, vec, row],
        out_specs=(row, vec, pl.BlockSpec((8, LANE), lambda i: (0, 0))),
        out_shape=(jax.ShapeDtypeStruct((t_len, d), F32), jax.ShapeDtypeStruct((1, d), F32),
                   jax.ShapeDtypeStruct((8, LANE), F32)),
        compiler_params=_params(1))(h, g, target)


def _chunk_row(shape):
    return lax.broadcasted_iota(jnp.int32, shape, 0) % GLA_CHUNK


def _gla_gate_fwd(a, w_a2p, b_a2):
    t_len = a.shape[0]
    kd = w_a2p.shape[1]
    tm = _tile(t_len, (256, 128, 64))

    def body(a_ref, w_ref, b_ref, ga_ref, cum_ref):
        ga = _dot(a_ref[...], w_ref[...].astype(BF16), NN) + b_ref[...]
        ga_ref[...] = ga
        la = (jnp.minimum(ga, 0.0) - jnp.log(1.0 + jnp.exp(-jnp.abs(ga)))) * (1.0 / GATE_NORMALIZER)
        row = _chunk_row(la.shape)
        s = 1
        while s < GLA_CHUNK:
            la = la + jnp.where(row >= s, pltpu.roll(la, s, 0), 0.0)
            s *= 2
        cum_ref[...] = la

    return pl.pallas_call(
        body, name="gla_gate_fwd", grid=(t_len // tm,),
        in_specs=[pl.BlockSpec((tm, A_PAD), lambda i: (i, 0)), pl.BlockSpec((A_PAD, kd), lambda i: (0, 0)),
                  pl.BlockSpec((1, kd), lambda i: (0, 0))],
        out_specs=(pl.BlockSpec((tm, kd), lambda i: (i, 0)), pl.BlockSpec((tm, kd), lambda i: (i, 0))),
        out_shape=(jax.ShapeDtypeStruct((t_len, kd), F32), jax.ShapeDtypeStruct((t_len, kd), F32)),
        compiler_params=_params(1))(a, w_a2p, b_a2)


def _gla_gate_bwd(dcum, ga, a, w_a2p):
    t_len, kd = dcum.shape
    tm = _tile(t_len, (256, 128, 64))

    def body(dc_ref, ga_ref, a_ref, w_ref, da_ref, dw_ref, db_ref):
        x = dc_ref[...]
        row = _chunk_row(x.shape)
        s = 1
        while s < GLA_CHUNK:
            x = x + jnp.where(row < GLA_CHUNK - s, pltpu.roll(x, tm - s, 0), 0.0)
            s *= 2
        dga = x * (1.0 / GATE_NORMALIZER) * _sigmoid(-ga_ref[...])
        dgab = dga.astype(BF16)
        da_ref[...] = _dot(dgab, w_ref[...].astype(BF16), NT).astype(da_ref.dtype)
        dw = _dot(a_ref[...], dgab, TN)
        db = jnp.sum(dga, axis=0, keepdims=True)

        @pl.when(pl.program_id(0) == 0)
        def _():
            dw_ref[...] = dw
            db_ref[...] = db

        @pl.when(pl.program_id(0) > 0)
        def _():
            dw_ref[...] += dw
            db_ref[...] += db

    wide = pl.BlockSpec((tm, kd), lambda i: (i, 0))
    return pl.pallas_call(
        body, name="gla_gate_bwd", grid=(t_len // tm,),
        in_specs=[wide, wide, pl.BlockSpec((tm, A_PAD), lambda i: (i, 0)), pl.BlockSpec((A_PAD, kd), lambda i: (0, 0))],
        out_specs=(pl.BlockSpec((tm, A_PAD), lambda i: (i, 0)), pl.BlockSpec((A_PAD, kd), lambda i: (0, 0)),
                   pl.BlockSpec((1, kd), lambda i: (0, 0))),
        out_shape=(jax.ShapeDtypeStruct((t_len, A_PAD), BF16), jax.ShapeDtypeStruct((A_PAD, kd), F32),
                   jax.ShapeDtypeStruct((1, kd), F32)),
        compiler_params=_params(1))(dcum, ga, a, w_a2p)


GLA_STEP_CHUNKS = 4


def _gla_dims():
    dk = GLA_KEY_DIM // GLA_HEADS
    dv = GLA_VAL_DIM // GLA_HEADS
    return dk, dv


def _gla_fwd(proj, cum):
    t_len = proj.shape[0]
    dk, dv = _gla_dims()
    nc = t_len // GLA_CHUNK
    c = GLA_CHUNK
    scale = dk ** -0.5
    v0 = 2 * GLA_KEY_DIM // dv

    per = _tile(nc, (GLA_STEP_CHUNKS, 2, 1))
    rows = per * c

    def body(q_ref, k_ref, v_ref, cum_ref, o_ref, st_ref, s_scr):
        @pl.when(pl.program_id(1) == 0)
        def _():
            s_scr[...] = jnp.zeros_like(s_scr)

        tri = lax.broadcasted_iota(jnp.int32, (c, c), 0) >= lax.broadcasted_iota(jnp.int32, (c, c), 1)
        for i in range(per):
            rs = slice(i * c, (i + 1) * c)
            cm = cum_ref[rs, :]
            last = cm[c - 1:c, :]
            q = q_ref[rs, :].astype(F32) * scale
            k = k_ref[rs, :].astype(F32)
            v = v_ref[rs, :].astype(BF16)
            qd = (q * jnp.exp(cm)).astype(BF16)
            ki = (k * jnp.exp(-cm)).astype(BF16)
            ke = (k * jnp.exp(last - cm)).astype(BF16)
            sc = jnp.where(tri, _dot(qd, ki, NT), 0.0)
            st = s_scr[...]
            st_ref[i] = st
            o_ref[rs, :] = _dot(sc.astype(BF16), v, NN) + _dot(qd, st.astype(BF16), NT)
            s_scr[...] = st * jnp.exp(last) + _dot(v, ke, TN)

    return pl.pallas_call(
        body, name="gla_fwd", grid=(GLA_HEADS, nc // per),
        in_specs=[pl.BlockSpec((rows, dk), lambda h, n: (n, h)),
                  pl.BlockSpec((rows, dk), lambda h, n: (n, GLA_HEADS + h)),
                  pl.BlockSpec((rows, dv), lambda h, n: (n, v0 + h)),
                  pl.BlockSpec((rows, dk), lambda h, n: (n, h))],
        out_specs=(pl.BlockSpec((rows, dv), lambda h, n: (n, h)),
                   pl.BlockSpec((None, per, dv, dk), lambda h, n: (h, n, 0, 0))),
        out_shape=(jax.ShapeDtypeStruct((t_len, GLA_VAL_DIM), F32),
                   jax.ShapeDtypeStruct((GLA_HEADS, nc, dv, dk), F32)),
        scratch_shapes=[pltpu.VMEM((dv, dk), F32)], compiler_params=_params(2))(proj, proj, proj, cum)


def _gla_bwd(proj, cum, states, do):
    t_len = proj.shape[0]
    dk, dv = _gla_dims()
    nc = t_len // GLA_CHUNK
    c = GLA_CHUNK
    scale = dk ** -0.5
    v0 = 2 * GLA_KEY_DIM // dv

    per = _tile(nc, (GLA_STEP_CHUNKS, 2, 1))
    rows = per * c

    def body(q_ref, k_ref, v_ref, cum_ref, st_ref, do_ref, dq_ref, dk_ref, dv_ref, dc_ref, ds_scr):
        @pl.when(pl.program_id(1) == 0)
        def _():
            ds_scr[...] = jnp.zeros_like(ds_scr)

        tri = lax.broadcasted_iota(jnp.int32, (c, c), 0) >= lax.broadcasted_iota(jnp.int32, (c, c), 1)
        row = lax.broadcasted_iota(jnp.int32, (c, dk), 0)
        for i in reversed(range(per)):
            rs = slice(i * c, (i + 1) * c)
            cm = cum_ref[rs, :]
            last = cm[c - 1:c, :]
            e_c = jnp.exp(cm)
            e_nc = jnp.exp(-cm)
            e_lc = jnp.exp(last - cm)
            e_l = jnp.exp(last)
            q = q_ref[rs, :].astype(F32) * scale
            k = k_ref[rs, :].astype(F32)
            v = v_ref[rs, :].astype(BF16)
            dov = do_ref[rs, :]
            qd32 = q * e_c
            ki32 = k * e_nc
            ke32 = k * e_lc
            qd = qd32.astype(BF16)
            ki = ki32.astype(BF16)
            ke = ke32.astype(BF16)
            st = st_ref[i]
            dst = ds_scr[...]
            dstb = dst.astype(BF16)
            am = jnp.where(tri, _dot(dov, v, NT), 0.0).astype(BF16)
            pm = jnp.where(tri, _dot(qd, ki, NT), 0.0).astype(BF16)
            dqd = _dot(am, ki, NN) + _dot(dov, st.astype(BF16), NN)
            dki = _dot(am, qd, TN)
            dvv = _dot(pm, dov, TN) + _dot(ke, dstb, NT)
            dke = _dot(v, dstb, NN)
            d_el = jnp.sum(dst * st, axis=0, keepdims=True)
            ds_scr[...] = dst * e_l + _dot(dov, qd, TN)
            dq_ref[rs, :] = (dqd * scale * e_c).astype(dq_ref.dtype)
            dk_ref[rs, :] = (dki * e_nc + dke * e_lc).astype(dk_ref.dtype)
            dv_ref[rs, :] = dvv.astype(dv_ref.dtype)
            dkeke = dke * ke32
            dcum = dqd * qd32 - dki * ki32 - dkeke
            dlast = jnp.sum(dkeke, axis=0, keepdims=True) + d_el * e_l
            dc_ref[rs, :] = jnp.where(row == c - 1, dcum + dlast, dcum)

    rev = nc // per - 1
    return pl.pallas_call(
        body, name="gla_bwd", grid=(GLA_HEADS, nc // per),
        in_specs=[pl.BlockSpec((rows, dk), lambda h, n: (rev - n, h)),
                  pl.BlockSpec((rows, dk), lambda h, n: (rev - n, GLA_HEADS + h)),
                  pl.BlockSpec((rows, dv), lambda h, n: (rev - n, v0 + h)),
                  pl.BlockSpec((rows, dk), lambda h, n: (rev - n, h)),
                  pl.BlockSpec((None, per, dv, dk), lambda h, n: (h, rev - n, 0, 0)),
                  pl.BlockSpec((rows, dv), lambda h, n: (rev - n, h))],
        out_specs=(pl.BlockSpec((rows, dk), lambda h, n: (rev - n, h)),
                   pl.BlockSpec((rows, dk), lambda h, n: (rev - n, h)),
                   pl.BlockSpec((rows, dv), lambda h, n: (rev - n, h)),
                   pl.BlockSpec((rows, dk), lambda h, n: (rev - n, h))),
        out_shape=(jax.ShapeDtypeStruct((t_len, GLA_KEY_DIM), BF16), jax.ShapeDtypeStruct((t_len, GLA_KEY_DIM), BF16),
                   jax.ShapeDtypeStruct((t_len, GLA_VAL_DIM), BF16), jax.ShapeDtypeStruct((t_len, GLA_KEY_DIM), F32)),
        scratch_shapes=[pltpu.VMEM((dv, dk), F32)], compiler_params=_params(2))(proj, proj, proj, cum, states, do)


def _gla_out_fwd(o, proj, gn):
    t_len = o.shape[0]
    _, dv = _gla_dims()
    tm = _tile(t_len, (512, 256, 128))
    r0 = (2 * GLA_KEY_DIM + GLA_VAL_DIM) // dv

    def body(o_ref, r_ref, g_ref, y_ref):
        ov = o_ref[...]
        rs = lax.rsqrt(jnp.mean(ov * ov, axis=-1, keepdims=True) + EPS)
        rv = r_ref[...].astype(F32)
        y_ref[...] = (ov * rs * g_ref[...] * (rv * _sigmoid(rv))).astype(y_ref.dtype)

    return pl.pallas_call(
        body, name="gla_out_fwd", grid=(t_len // tm, GLA_HEADS),
        in_specs=[pl.BlockSpec((tm, dv), lambda i, h: (i, h)), pl.BlockSpec((tm, dv), lambda i, h: (i, r0 + h)),
                  pl.BlockSpec((1, dv), lambda i, h: (0, 0))],
        out_specs=pl.BlockSpec((tm, dv), lambda i, h: (i, h)),
        out_shape=jax.ShapeDtypeStruct((t_len, GLA_VAL_DIM), BF16), compiler_params=_params(2))(o, proj, gn)


def _gla_out_bwd(dy, o, proj, gn):
    t_len = o.shape[0]
    _, dv = _gla_dims()
    tm = _tile(t_len, (512, 256, 128))
    r0 = (2 * GLA_KEY_DIM + GLA_VAL_DIM) // dv

    def body(dy_ref, o_ref, r_ref, g_ref, do_ref, dr_ref, dg_ref):
        ov = o_ref[...]
        gv = g_ref[...]
        rs = lax.rsqrt(jnp.mean(ov * ov, axis=-1, keepdims=True) + EPS)
        xhat = ov * rs
        rv = r_ref[...].astype(F32)
        sg = _sigmoid(rv)
        gate = rv * sg
        dyv = dy_ref[...].astype(F32)
        dn = dyv * gate
        dr_ref[...] = (dyv * xhat * gv * (sg * (1.0 + rv * (1.0 - sg)))).astype(dr_ref.dtype)
        dxn = dn * gv
        do_ref[...] = (rs * (dxn - xhat * jnp.mean(dxn * xhat, axis=-1, keepdims=True))).astype(do_ref.dtype)
        part = jnp.sum(dn * xhat, axis=0, keepdims=True)
        first = (pl.program_id(0) == 0) & (pl.program_id(1) == 0)

        @pl.when(first)
        def _():
            dg_ref[...] = part

        @pl.when(jnp.logical_not(first))
        def _():
            dg_ref[...] += part

    blk = pl.BlockSpec((tm, dv), lambda i, h: (i, h))
    return pl.pallas_call(
        body, name="gla_out_bwd", grid=(t_len // tm, GLA_HEADS),
        in_specs=[blk, blk, pl.BlockSpec((tm, dv), lambda i, h: (i, r0 + h)), pl.BlockSpec((1, dv), lambda i, h: (0, 0))],
        out_specs=(blk, blk, pl.BlockSpec((1, dv), lambda i, h: (0, 0))),
        out_shape=(jax.ShapeDtypeStruct((t_len, GLA_VAL_DIM), BF16), jax.ShapeDtypeStruct((t_len, GLA_VAL_DIM), BF16),
                   jax.ShapeDtypeStruct((1, dv), F32)),
        compiler_params=_params(2))(dy, o, proj, gn)


def _alibi_slopes():
    n = ATT_HEADS
    start = 2.0 ** (-8.0 / n)
    return [start ** (i + 1) for i in range(n)]


def _att_masks(d):
    b = ATT_BLOCK
    qa = lax.broadcasted_iota(jnp.int32, (b, b), 0)
    kb = lax.broadcasted_iota(jnp.int32, (b, b), 1)
    dist_c = qa - kb
    dist_p = qa - kb + b
    return dist_c >= 0, dist_p <= b, (dist_c * d).astype(F32), (dist_p * d).astype(F32)


def _att_fwd(q_all, kv, g):
    d = DILATIONS[g]
    assert WINDOWS[g] // d == ATT_BLOCK
    t_len = q_all.shape[0]
    hd = ATT_HEADS * HEAD_DIM
    sub = t_len // d
    nb = sub // ATT_BLOCK
    b = ATT_BLOCK
    e = HEAD_DIM
    scale = e ** -0.5
    slopes = _alibi_slopes()
    qv = q_all.reshape(sub, d * 3 * hd)
    kvv = kv.reshape(sub, d * 2 * hd)

    def body(q_ref, kp_ref, kc_ref, vp_ref, vc_ref, o_ref, l_ref):
        ib = pl.program_id(1)
        valid_c, valid_p0, dist_c, dist_p = _att_masks(d)
        valid_p = valid_p0 & (ib > 0)
        l_ref[...] = jnp.zeros_like(l_ref)
        for h in range(ATT_HEADS):
            hs = slice(h * e, (h + 1) * e)
            qh = q_ref[:, hs]
            s_c = jnp.where(valid_c, _dot(qh, kc_ref[:, hs], NT) * scale - slopes[h] * dist_c, NEG)
            s_p = jnp.where(valid_p, _dot(qh, kp_ref[:, hs], NT) * scale - slopes[h] * dist_p, NEG)
            m = jnp.maximum(jnp.max(s_c, axis=1, keepdims=True), jnp.max(s_p, axis=1, keepdims=True))
            p_c = jnp.where(valid_c, jnp.exp(s_c - m), 0.0)
            p_p = jnp.where(valid_p, jnp.exp(s_p - m), 0.0)
            l = jnp.sum(p_c, axis=1, keepdims=True) + jnp.sum(p_p, axis=1, keepdims=True)
            acc = _dot(p_c.astype(BF16), vc_ref[:, hs], NN) + _dot(p_p.astype(BF16), vp_ref[:, hs], NN)
            o_ref[:, hs] = acc / l
            l_ref[:, h:h + 1] = m + jnp.log(l)

    blk = (b, hd)
    cblk = (b, LANE)
    o, lse = pl.pallas_call(
        body, name=f"att_fwd{g}", grid=(d, nb),
        in_specs=[pl.BlockSpec(blk, lambda r, i: (i, 3 * r + g)),
                  pl.BlockSpec(blk, lambda r, i: (jnp.maximum(i - 1, 0), 2 * r)),
                  pl.BlockSpec(blk, lambda r, i: (i, 2 * r)),
                  pl.BlockSpec(blk, lambda r, i: (jnp.maximum(i - 1, 0), 2 * r + 1)),
                  pl.BlockSpec(blk, lambda r, i: (i, 2 * r + 1))],
        out_specs=(pl.BlockSpec(blk, lambda r, i: (i, r)), pl.BlockSpec(cblk, lambda r, i: (i, r))),
        out_shape=(jax.ShapeDtypeStruct((sub, d * hd), F32), jax.ShapeDtypeStruct((sub, d * LANE), F32)),
        compiler_params=_params(2))(qv, kvv, kvv, kvv, kvv)
    return o.reshape(t_len, hd), lse.reshape(t_len, LANE)


def _att_merge(os, ls):
    t_len, hd = os[0].shape
    tm = _tile(t_len, (256, 128))
    e = HEAD_DIM

    def body(o0, o1, o2, l0, l1, l2, of_ref, ob_ref, l_ref):
        a0, a1, a2 = l0[...], l1[...], l2[...]
        m = jnp.maximum(jnp.maximum(a0, a1), a2)
        e0, e1, e2 = jnp.exp(a0 - m), jnp.exp(a1 - m), jnp.exp(a2 - m)
        den = e0 + e1 + e2
        w0, w1, w2 = e0 / den, e1 / den, e2 / den
        l_ref[...] = m + jnp.log(den)
        for h in range(ATT_HEADS):
            hs = slice(h * e, (h + 1) * e)
            c = slice(h, h + 1)
            o = w0[:, c] * o0[:, hs] + w1[:, c] * o1[:, hs] + w2[:, c] * o2[:, hs]
            of_ref[:, hs] = o
            ob_ref[:, hs] = o.astype(ob_ref.dtype)

    row = pl.BlockSpec((tm, hd), lambda i: (i, 0))
    crow = pl.BlockSpec((tm, LANE), lambda i: (i, 0))
    return pl.pallas_call(
        body, name="att_merge", grid=(t_len // tm,), in_specs=[row] * 3 + [crow] * 3, out_specs=(row, row, crow),
        out_shape=(jax.ShapeDtypeStruct((t_len, hd), F32), jax.ShapeDtypeStruct((t_len, hd), BF16),
                   jax.ShapeDtypeStruct((t_len, LANE), F32)),
        compiler_params=_params(1))(*os, *ls)


def _att_delta(do, o):
    t_len, hd = o.shape
    tm = _tile(t_len, (256, 128))
    e = HEAD_DIM

    def body(do_ref, o_ref, d_ref):
        d_ref[...] = jnp.zeros_like(d_ref)
        for h in range(ATT_HEADS):
            hs = slice(h * e, (h + 1) * e)
            d_ref[:, h:h + 1] = jnp.sum(do_ref[:, hs].astype(F32) * o_ref[:, hs], axis=1, keepdims=True)

    row = pl.BlockSpec((tm, hd), lambda i: (i, 0))
    return pl.pallas_call(
        body, name="att_delta", grid=(t_len // tm,), in_specs=[row, row],
        out_specs=pl.BlockSpec((tm, LANE), lambda i: (i, 0)),
        out_shape=jax.ShapeDtypeStruct((t_len, LANE), F32), compiler_params=_params(1))(do, o)


def _att_bwd(q_all, kv, delta, lse, do, g):
    d = DILATIONS[g]
    t_len = q_all.shape[0]
    hd = ATT_HEADS * HEAD_DIM
    sub = t_len // d
    nb = sub // ATT_BLOCK
    b = ATT_BLOCK
    e = HEAD_DIM
    scale = e ** -0.5
    slopes = _alibi_slopes()
    qv = q_all.reshape(sub, d * 3 * hd)
    kvv = kv.reshape(sub, d * 2 * hd)
    dlv = delta.reshape(sub, d * LANE)
    lv = lse.reshape(sub, d * LANE)
    dov = do.reshape(sub, d * hd)

    def body(qj_ref, qn_ref, kp_ref, kc_ref, vp_ref, vc_ref, doj_ref, don_ref, dj_ref, dn_ref, lj_ref, ln_ref,
             dq_ref, dk_ref, dv_ref):
        j = pl.program_id(1)
        valid_c, valid_p0, dist_c, dist_p = _att_masks(d)
        valid_p = valid_p0 & (j > 0)
        valid_n = valid_p0 & (j + 1 < nb)
        for h in range(ATT_HEADS):
            hs = slice(h * e, (h + 1) * e)
            c = slice(h, h + 1)
            qj, qn = qj_ref[:, hs], qn_ref[:, hs]
            kc, kp = kc_ref[:, hs], kp_ref[:, hs]
            vc, vp = vc_ref[:, hs], vp_ref[:, hs]
            doj, don = doj_ref[:, hs], don_ref[:, hs]
            dlt_j, dlt_n = dj_ref[:, c], dn_ref[:, c]
            lj, ln = lj_ref[:, c], ln_ref[:, c]
            s = _dot(qj, kc, NT) * scale - slopes[h] * dist_c
            p = jnp.where(valid_c, jnp.exp(jnp.where(valid_c, s - lj, NEG)), 0.0)
            ds = (p * (_dot(doj, vc, NT) - dlt_j)).astype(BF16)
            dq = _dot(ds, kc, NN)
            dk = _dot(ds, qj, TN)
            dv = _dot(p.astype(BF16), doj, TN)
            s = _dot(qj, kp, NT) * scale - slopes[h] * dist_p
            p = jnp.where(valid_p, jnp.exp(jnp.where(valid_p, s - lj, NEG)), 0.0)
            ds = (p * (_dot(doj, vp, NT) - dlt_j)).astype(BF16)
            dq = dq + _dot(ds, kp, NN)
            s = _dot(qn, kc, NT) * scale - slopes[h] * dist_p
            p = jnp.where(valid_n, jnp.exp(jnp.where(valid_n, s - ln, NEG)), 0.0)
            ds = (p * (_dot(don, vc, NT) - dlt_n)).astype(BF16)
            dk = dk + _dot(ds, qn, TN)
            dv = dv + _dot(p.astype(BF16), don, TN)
            dq_ref[:, hs] = (dq * scale).astype(dq_ref.dtype)
            dk_ref[:, hs] = (dk * scale).astype(dk_ref.dtype)
            dv_ref[:, hs] = dv.astype(dv_ref.dtype)

    blk = (b, hd)
    cblk = (b, LANE)
    nxt = lambda i: jnp.minimum(i + 1, nb - 1)
    prv = lambda i: jnp.maximum(i - 1, 0)
    dq, dk, dv = pl.pallas_call(
        body, name=f"att_bwd{g}", grid=(d, nb),
        in_specs=[pl.BlockSpec(blk, lambda r, i: (i, 3 * r + g)),
                  pl.BlockSpec(blk, lambda r, i: (nxt(i), 3 * r + g)),
                  pl.BlockSpec(blk, lambda r, i: (prv(i), 2 * r)),
                  pl.BlockSpec(blk, lambda r, i: (i, 2 * r)),
                  pl.BlockSpec(blk, lambda r, i: (prv(i), 2 * r + 1)),
                  pl.BlockSpec(blk, lambda r, i: (i, 2 * r + 1)),
                  pl.BlockSpec(blk, lambda r, i: (i, r)),
                  pl.BlockSpec(blk, lambda r, i: (nxt(i), r)),
                  pl.BlockSpec(cblk, lambda r, i: (i, r)),
                  pl.BlockSpec(cblk, lambda r, i: (nxt(i), r)),
                  pl.BlockSpec(cblk, lambda r, i: (i, r)),
                  pl.BlockSpec(cblk, lambda r, i: (nxt(i), r))],
        out_specs=(pl.BlockSpec(blk, lambda r, i: (i, r)),) * 3,
        out_shape=(jax.ShapeDtypeStruct((sub, d * hd), BF16),) * 3,
        compiler_params=_params(2))(qv, qv, kvv, kvv, kvv, kvv, dov, dov, dlv, dlv, lv, lv)
    return dq.reshape(t_len, hd), dk.reshape(t_len, hd), dv.reshape(t_len, hd)


def _kv_grad_sum(dks, dvs):
    t_len, hd = dks[0].shape
    tm = _tile(t_len, (256, 128))

    def body(k0, k1, k2, v0, v1, v2, o_ref):
        o_ref[:, :hd] = (k0[...].astype(F32) + k1[...].astype(F32) + k2[...].astype(F32)).astype(o_ref.dtype)
        o_ref[:, hd:] = (v0[...].astype(F32) + v1[...].astype(F32) + v2[...].astype(F32)).astype(o_ref.dtype)

    row = pl.BlockSpec((tm, hd), lambda i: (i, 0))
    return pl.pallas_call(
        body, name="kv_grad_sum", grid=(t_len // tm,), in_specs=[row] * 6,
        out_specs=pl.BlockSpec((tm, 2 * hd), lambda i: (i, 0)),
        out_shape=jax.ShapeDtypeStruct((t_len, 2 * hd), BF16), compiler_params=_params(1))(*dks, *dvs)


HALO = 16
INV_SQRT2 = 1.0 / math.sqrt(2.0)
INV_SQRT2PI = 1.0 / math.sqrt(2.0 * math.pi)


def _conv_taps(g, halo, cw, cb):
    row = lax.broadcasted_iota(jnp.int32, g.shape, 0)
    h1 = halo[HALO - 1:HALO, :]
    h2 = halo[HALO - 2:HALO - 1, :]
    g1 = jnp.where(row == 0, h1, pltpu.roll(g, 1, 0))
    g2 = jnp.where(row == 0, h2, jnp.where(row == 1, h1, pltpu.roll(g, 2, 0)))
    gc = cw[0:1, :] * g2 + cw[1:2, :] * g1 + cw[2:3, :] * g + cb
    return gc, g1, g2


def _glu_specs(t_len, f, tm, tc):
    nj = f // tc
    hb = tm // HALO
    u = pl.BlockSpec((tm, tc), lambda j, i: (i, j))
    g = pl.BlockSpec((tm, tc), lambda j, i: (i, nj + j))
    gh = pl.BlockSpec((HALO, tc), lambda j, i: (jnp.maximum(i * hb - 1, 0), nj + j))
    cw = pl.BlockSpec((8, tc), lambda j, i: (0, j))
    cb = pl.BlockSpec((1, tc), lambda j, i: (0, j))
    return u, g, gh, cw, cb


def _glu_fwd(name, up, cw, cb):
    t_len = up.shape[0]
    f = up.shape[1] // 2
    tm = _tile(t_len, (512, 256, 128))
    tc = _tile(f, (1408, 1024, 512, 256, 128))
    u_s, g_s, gh_s, cw_s, cb_s = _glu_specs(t_len, f, tm, tc)

    def body(u_ref, g_ref, gh_ref, cw_ref, cb_ref, o_ref):
        first = pl.program_id(1) == 0
        halo = jnp.where(first, 0.0, gh_ref[...].astype(F32))
        gc, _, _ = _conv_taps(g_ref[...].astype(F32), halo, cw_ref[...], cb_ref[...])
        gel = 0.5 * gc * (1.0 + lax.erf(gc * INV_SQRT2))
        o_ref[...] = (gel * u_ref[...].astype(F32)).astype(o_ref.dtype)

    return pl.pallas_call(
        body, name=name, grid=(f // tc, t_len // tm), in_specs=[u_s, g_s, gh_s, cw_s, cb_s],
        out_specs=pl.BlockSpec((tm, tc), lambda j, i: (i, j)),
        out_shape=jax.ShapeDtypeStruct((t_len, f), BF16), compiler_params=_params(2))(up, up, up, cw, cb)


def _glu_bwd_a(name, dact, up, cw, cb):
    t_len = up.shape[0]
    f = up.shape[1] // 2
    tm = _tile(t_len, (256, 128))
    tc = _tile(f, (1408, 1024, 512, 256, 128))
    u_s, g_s, gh_s, cw_s, cb_s = _glu_specs(t_len, f, tm, tc)

    def body(da_ref, u_ref, g_ref, gh_ref, cw_ref, cb_ref, du_ref, dgc_ref, w0_ref, w1_ref, w2_ref, b_ref):
        first = pl.program_id(1) == 0
        halo = jnp.where(first, 0.0, gh_ref[...].astype(F32))
        g = g_ref[...].astype(F32)
        gc, g1, g2 = _conv_taps(g, halo, cw_ref[...], cb_ref[...])
        phi = 0.5 * (1.0 + lax.erf(gc * INV_SQRT2))
        dgel = phi + gc * jnp.exp(-0.5 * gc * gc) * INV_SQRT2PI
        da = da_ref[...].astype(F32)
        du_ref[...] = (da * gc * phi).astype(du_ref.dtype)
        dgc = da * u_ref[...].astype(F32) * dgel
        dgc_ref[...] = dgc.astype(dgc_ref.dtype)
        parts = (jnp.sum(dgc * g2, axis=0, keepdims=True), jnp.sum(dgc * g1, axis=0, keepdims=True),
                 jnp.sum(dgc * g, axis=0, keepdims=True), jnp.sum(dgc, axis=0, keepdims=True))
        refs = (w0_ref, w1_ref, w2_ref, b_ref)

        @pl.when(first)
        def _():
            for r, p in zip(refs, parts):
                r[...] = p

        @pl.when(jnp.logical_not(first))
        def _():
            for r, p in zip(refs, parts):
                r[...] += p

    tile = pl.BlockSpec((tm, tc), lambda j, i: (i, j))
    vec = pl.BlockSpec((1, tc), lambda j, i: (0, j))
    vshape = jax.ShapeDtypeStruct((1, f), F32)
    return pl.pallas_call(
        body, name=name, grid=(f // tc, t_len // tm), in_specs=[tile, u_s, g_s, gh_s, cw_s, cb_s],
        out_specs=(tile, tile, vec, vec, vec, vec),
        out_shape=(jax.ShapeDtypeStruct((t_len, f), BF16), jax.ShapeDtypeStruct((t_len, f), F32),
                   vshape, vshape, vshape, vshape),
        compiler_params=_params(2))(dact, up, up, up, cw, cb)


def _glu_bwd_b(name, du, dgc, cw):
    t_len, f = du.shape
    tm = _tile(t_len, (128, 64))
    hb = tm // HALO
    n_i = t_len // tm
    last_hb = t_len // HALO - 1

    def body(du_ref, d_ref, dh_ref, cw_ref, o_ref):
        last = pl.program_id(0) == n_i - 1
        halo = jnp.where(last, 0.0, dh_ref[...].astype(F32))
        dd = d_ref[...].astype(F32)
        row = lax.broadcasted_iota(jnp.int32, dd.shape, 0)
        h0 = halo[0:1, :]
        h1 = halo[1:2, :]
        d1 = jnp.where(row == tm - 1, h0, pltpu.roll(dd, tm - 1, 0))
        d2 = jnp.where(row == tm - 1, h1, jnp.where(row == tm - 2, h0, pltpu.roll(dd, tm - 2, 0)))
        cwv = cw_ref[...]
        dg = cwv[2:3, :] * dd + cwv[1:2, :] * d1 + cwv[0:1, :] * d2
        o_ref[:, :f] = du_ref[...]
        o_ref[:, f:] = dg.astype(o_ref.dtype)

    row_s = pl.BlockSpec((tm, f), lambda i: (i, 0))
    return pl.pallas_call(
        body, name=name, grid=(n_i,),
        in_specs=[row_s, row_s, pl.BlockSpec((HALO, f), lambda i: (jnp.minimum((i + 1) * hb, last_hb), 0)),
                  pl.BlockSpec((8, f), lambda i: (0, 0))],
        out_specs=pl.BlockSpec((tm, 2 * f), lambda i: (i, 0)),
        out_shape=jax.ShapeDtypeStruct((t_len, 2 * f), BF16), compiler_params=_params(1))(du, dgc, dgc, cw)


def _adamw(name, w, g, m, v):
    rows, cols = w.shape
    tr = _tile(rows, (256, 128, 64, 32, 16, 8))
    c1 = 1.0 / (1.0 - ADAM_B1 ** ADAM_STEP)
    c2 = 1.0 / (1.0 - ADAM_B2 ** ADAM_STEP)

    def body(w_ref, g_ref, m_ref, v_ref, d_ref, nm_ref, nv_ref):
        gv = g_ref[...]
        nm = ADAM_B1 * m_ref[...] + (1.0 - ADAM_B1) * gv
        nv = ADAM_B2 * v_ref[...] + (1.0 - ADAM_B2) * (gv * gv)
        nm_ref[...] = nm
        nv_ref[...] = nv
        d_ref[...] = -ADAM_LR * ((nm * c1) / (jnp.sqrt(nv * c2) + ADAM_EPS) + ADAM_WD * w_ref[...])

    blk = pl.BlockSpec((tr, cols), lambda i: (i, 0))
    shp = jax.ShapeDtypeStruct((rows, cols), F32)
    return pl.pallas_call(body, name=name, grid=(rows // tr,), in_specs=[blk] * 4, out_specs=(blk,) * 3,
                          out_shape=(shp,) * 3, compiler_params=_params(1))(w, g, m, v)


class _NoComm:
    def __init__(self):
        self.grads = {}

    def need(self, group, ws, after):
        return ws

    def reduce(self, group, grads, carry):
        self.grads.update(grads)
        return carry


def _local_step(x, target, ws, norms, small, hooks):
    lay = _layout()

    w_main, w_a = _unpack_gin(ws["gin"])
    hn0 = _rms_fwd("rms_attn0", x, norms["attn0"])
    proj = _mm_plain("gla_proj", hn0, w_main, NN, F32)
    a = _mm_plain("gla_proj_a", hn0, w_a, NN, BF16)
    ga, cum = _gla_gate_fwd(a, small["w_a2p"], small["b_a2"])
    o_gla, states = _gla_fwd(proj, cum)
    gated = _gla_out_fwd(o_gla, proj, small["head_norm"])
    h1 = _mm_act_wr("gla_out", gated, ws["gout"], lay["gout"], add=x)

    def ffn_fwd(l, h):
        hn = _rms_fwd(f"rms_ffn{l}", h, norms[f"ffn{l}"])
        up = _mm_act_wc(f"ffn_up{l}", hn, ws[f"up{l}"], lay[f"up{l}"], F32)
        act = _glu_fwd(f"glu_fwd{l}", up, small["conv_w"][l], small["conv_b"][l])
        return hn, up, act, _mm_act_wr(f"ffn_down{l}", act, ws[f"down{l}"], lay[f"down{l}"], add=h)

    ws = hooks.need("B", ws, h1)
    hnf0, up0, act0, h2 = ffn_fwd(0, h1)

    ws = hooks.need("C", ws, h2)
    kvn = _rms_fwd("rms_kv", h2, norms["kv"])
    kv = _mm_act_wc("kv_proj", kvn, ws["wkv"], lay["wkv"], BF16)
    hn1 = _rms_fwd("rms_attn1", h2, norms["attn1"])
    q_all = _mm_act_wc("q_proj", hn1, ws["wq"], lay["wq"], BF16)
    branch = [_att_fwd(q_all, kv, g) for g in range(3)]
    o_att, o_att_b, lse = _att_merge([br[0] for br in branch], [br[1] for br in branch])
    h3 = _mm_act_wr("att_out", o_att_b, ws["dout"], lay["dout"], add=h2)
    hnf1, up1, act1, h4 = ffn_fwd(1, h3)

    dh4, d_final, loss = _loss_head(h4, norms["final"], target)

    sm = {"final": d_final}

    def ffn_bwd(l, dh, h, hn, up, act):
        big = {}
        dact = _mm_dact_wrT(f"ffn_down_dx{l}", dh, ws[f"down{l}"], lay[f"down{l}"])
        big[f"down{l}"] = _mm_grad_wr(f"ffn_down_dw{l}", act, dh, lay[f"down{l}"])
        du, dgc, w0, w1, w2, db = _glu_bwd_a(f"glu_bwd_a{l}", dact, up, small["conv_w"][l], small["conv_b"][l])
        sm[f"conv_w{l}"] = (w0, w1, w2)
        sm[f"conv_b{l}"] = db
        dup = _glu_bwd_b(f"glu_bwd_b{l}", du, dgc, small["conv_w"][l])
        dhn = _mm_dact_wcT(f"ffn_up_dx{l}", dup, ws[f"up{l}"], lay[f"up{l}"])
        big[f"up{l}"] = _mm_grad_wc(f"ffn_up_dw{l}", hn, dup, lay[f"up{l}"])
        dh_in, sm[f"ffn{l}"] = _rms_bwd(f"rms_ffn_bwd{l}", dhn, h, norms[f"ffn{l}"], dh)
        return hooks.reduce(f"ffn{l}", big, dh_in)

    dh3 = ffn_bwd(1, dh4, h3, hnf1, up1, act1)

    big = {}
    do_att = _mm_dact_wrT("att_out_dx", dh3, ws["dout"], lay["dout"])
    big["dout"] = _mm_grad_wr("att_out_dw", o_att_b, dh3, lay["dout"])
    delta = _att_delta(do_att, o_att)
    bw = [_att_bwd(q_all, kv, delta, lse, do_att, g) for g in range(3)]
    dq_all = jnp.concatenate([t[0] for t in bw], axis=1)
    dhn1 = _mm_dact_wcT("q_proj_dx", dq_all, ws["wq"], lay["wq"])
    big["wq"] = _mm_grad_wc("q_proj_dw", hn1, dq_all, lay["wq"])
    dh2, sm["attn1"] = _rms_bwd("rms_attn1_bwd", dhn1, h2, norms["attn1"], dh3)
    dkv = _kv_grad_sum([t[1] for t in bw], [t[2] for t in bw])
    dkvn = _mm_dact_wcT("kv_proj_dx", dkv, ws["wkv"], lay["wkv"])
    big["wkv"] = _mm_grad_wc("kv_proj_dw", kvn, dkv, lay["wkv"])
    dh2, sm["kv"] = _rms_bwd("rms_kv_bwd", dkvn, h2, norms["kv"], dh2)
    dh2 = hooks.reduce("att", big, dh2)

    dh1 = ffn_bwd(0, dh2, h1, hnf0, up0, act0)

    big = {}
    dgated = _mm_dact_wrT("gla_out_dx", dh1, ws["gout"], lay["gout"])
    big["gout"] = _mm_grad_wr("gla_out_dw", gated, dh1, lay["gout"])
    do_gla, dr, sm["head_norm"] = _gla_out_bwd(dgated, o_gla, proj, small["head_norm"])
    dq, dk, dv, dcum = _gla_bwd(proj, cum, states, do_gla)
    da, sm["w_a2p"], sm["b_a2"] = _gla_gate_bwd(dcum, ga, a, small["w_a2p"])
    dproj = jnp.concatenate([dq, dk, dv, dr], axis=1)
    dhn0 = _mm_plain("gla_proj_dx", dproj, w_main, NT, F32)
    dhn0 = _mm_plain("gla_proj_a_dx", da, w_a, NT, F32, add=dhn0)
    gin_main = _mm_plain("gla_proj_dw", hn0, dproj, TN, BF16)
    gin_a = _mm_plain("gla_proj_a_dw", hn0, da, TN, BF16)
    big["gin"] = _pack_gin_grad(gin_main, gin_a)
    grad_x, sm["attn0"] = _rms_bwd("rms_attn0_bwd", dhn0, x, norms["attn0"], dh1)
    return loss, grad_x, sm, big


def _pack_weights(chip, gla_w_in, gla_w_out, w_kv, dsa_w_q, dsa_w_out, ffn_w_up, ffn_w_down):
    gin = gla_w_in[0]
    gin = jnp.pad(gin, ((0, 0), (0, _roundup(gin.shape[1], LANE) - gin.shape[1])))
    shards = {"gin": gin, "gout": gla_w_out[0], "up0": ffn_w_up[0], "up1": ffn_w_up[1], "down0": ffn_w_down[0],
              "down1": ffn_w_down[1], "wq": dsa_w_q[0], "wkv": w_kv, "dout": dsa_w_out[0]}
    out = {}
    for name, w in shards.items():
        buf = jnp.zeros((N_CHIPS,) + w.shape, BF16)
        out[name] = lax.dynamic_update_slice(buf, w.astype(BF16)[None], (chip, 0, 0))
    return out


def _unpack_gin(w_gin):
    w = _layout()["gin"][1]
    full = jnp.transpose(w_gin[:, :, :w], (1, 0, 2)).reshape(D_MODEL, N_CHIPS * w)
    n_main = 2 * GLA_KEY_DIM + 2 * GLA_VAL_DIM
    w_a = jnp.pad(full[:, n_main:], ((0, 0), (0, A_PAD - GATE_RANK)))
    return full[:, :n_main], w_a


def _pack_gin_grad(gin_main, gin_a):
    w = _layout()["gin"][1]
    gin = jnp.concatenate([gin_main, gin_a[:, :GATE_RANK]], axis=1)
    gin = jnp.transpose(gin.reshape(D_MODEL, N_CHIPS, w), (1, 0, 2))
    return jnp.pad(gin, ((0, 0), (0, 0), (0, _roundup(w, LANE) - w)))


def _small_params(attn_norm, ffn_norm, kv_norm, final_norm, conv_b, w_a2, b_a2, head_norm, conv_w):
    norms = {"attn0": attn_norm[0:1], "attn1": attn_norm[1:2], "ffn0": ffn_norm[0:1], "ffn1": ffn_norm[1:2],
             "kv": kv_norm[None, :], "final": final_norm[None, :]}
    small = {"w_a2p": jnp.pad(w_a2, ((0, A_PAD - GATE_RANK), (0, 0))), "b_a2": b_a2[None, :],
             "head_norm": head_norm[None, :], "conv_w": jnp.pad(conv_w, ((0, 0), (0, 8 - conv_w.shape[1]), (0, 0))),
             "conv_b": conv_b[:, None, :]}
    return norms, small


ANY = pl.BlockSpec(memory_space=pl.ANY)


def _place():
    return lax.axis_index("x"), lax.axis_index("y"), lax.axis_index("c")


def _other_chips(x, y):
    return [(1 - x, y), (x, 1 - y), (1 - x, 1 - y)]


def _rcopy(src, dst, ssem, rsem, dev):
    return pltpu.make_async_remote_copy(src_ref=src, dst_ref=dst, send_sem=ssem, recv_sem=rsem, device_id=dev,
                                        device_id_type=MESH)


def _gather_weights(name, arrs):
    n = len(arrs)

    def body(*refs):
        ins, outs = refs[:n], refs[n:2 * n]
        send, recv, fsend, frecv = refs[2 * n:]
        x, y, c = _place()
        me = 2 * x + y
        chips = _other_chips(x, y)
        sib = (x, y, 1 - c)
        first, fwd = [], []
        for a in range(n):
            h = ins[a].shape[1] // 2
            mine = pl.ds(c * h, h)
            for j, (px, py) in enumerate(chips):
                cp = _rcopy(ins[a].at[me, mine], outs[a].at[me, mine], send.at[3 * a + j], recv.at[3 * a + j], (px, py, c))
                cp.start()
                first.append(cp)
        for a in range(n):
            h = ins[a].shape[1] // 2
            mine = pl.ds(c * h, h)
            for j, (px, py) in enumerate(chips):
                landed = outs[a].at[2 * px + py, mine]
                _rcopy(landed, landed, send.at[3 * a + j], recv.at[3 * a + j], (px, py, c)).wait_recv()
                cp = _rcopy(landed, landed, fsend.at[3 * a + j], frecv.at[3 * a + j], sib)
                cp.start()
                fwd.append(cp)
        for a in range(n):
            h = ins[a].shape[1] // 2
            theirs = pl.ds((1 - c) * h, h)
            for j, (px, py) in enumerate(chips):
                got = outs[a].at[2 * px + py, theirs]
                _rcopy(got, got, fsend.at[3 * a + j], frecv.at[3 * a + j], sib).wait_recv()
        for cp in first + fwd:
            cp.wait_send()

    return pl.pallas_call(
        body, name=name, in_specs=[ANY] * n, out_specs=[ANY] * n,
        out_shape=[jax.ShapeDtypeStruct(a.shape, a.dtype) for a in arrs],
        input_output_aliases={a: a for a in range(n)},
        scratch_shapes=[pltpu.SemaphoreType.DMA((3 * n,)), pltpu.SemaphoreType.DMA((3 * n,)),
                        pltpu.SemaphoreType.DMA((3 * n,)), pltpu.SemaphoreType.DMA((3 * n,))])(*arrs)


def _swap_halves(name, arrs):
    n = len(arrs)

    def body(*refs):
        ins, outs = refs[:n], refs[n:2 * n]
        send, recv = refs[2 * n:]
        x, y, c = _place()
        cps = []
        for a in range(n):
            h = ins[a].shape[1] // 2
            cp = _rcopy(ins[a].at[:, pl.ds((1 - c) * h, h)], outs[a], send.at[a], recv.at[a], (x, y, 1 - c))
            cp.start()
            cps.append(cp)
        for cp in cps:
            cp.wait()

    return pl.pallas_call(
        body, name=name, in_specs=[ANY] * n, out_specs=[ANY] * n,
        out_shape=[jax.ShapeDtypeStruct((a.shape[0], a.shape[1] // 2, a.shape[2]), a.dtype) for a in arrs],
        scratch_shapes=[pltpu.SemaphoreType.DMA((n,)), pltpu.SemaphoreType.DMA((n,))])(*arrs)


SEM = pl.BlockSpec(memory_space=pltpu.SEMAPHORE)
EFFECT = pltpu.SideEffectType.DATAFLOW_SIDE_EFFECTING


def _shapes(arrs):
    return [jax.ShapeDtypeStruct(a.shape, a.dtype) for a in arrs]


def _gather_start(name, thru, arrs):
    n, nt = len(arrs), len(thru)

    def body(*refs):
        ins = refs[nt:nt + n]
        send, recv = refs[nt + n], refs[nt + n + 1]
        outs = refs[2 * nt + n + 2:]
        x, y, c = _place()
        me = 2 * x + y
        for a in range(n):
            h = ins[a].shape[1] // 2
            mine = pl.ds(c * h, h)
            for j, (px, py) in enumerate(_other_chips(x, y)):
                _rcopy(ins[a].at[me, mine], outs[a].at[me, mine], send.at[3 * a + j], recv.at[3 * a + j], (px, py, c)).start()

    res = pl.pallas_call(
        body, name=name, in_specs=[ANY] * (nt + n), out_specs=[SEM, SEM] + [ANY] * (nt + n),
        out_shape=[pltpu.SemaphoreType.DMA((3 * n,)), pltpu.SemaphoreType.DMA((3 * n,))] + _shapes(thru) + _shapes(arrs),
        input_output_aliases={i: 2 + i for i in range(nt + n)},
        compiler_params=pltpu.CompilerParams(has_side_effects=EFFECT))(*thru, *arrs)
    return res[0], res[1], res[2:2 + nt], res[2 + nt:]


def _gather_wait(name, send, recv, arrs, after):
    n = len(arrs)

    def body(*refs):
        ins = refs[:n]
        send_ref, recv_ref = refs[n], refs[n + 1]
        x, y, c = _place()
        me = 2 * x + y
        for a in range(n):
            h = ins[a].shape[1] // 2
            mine = pl.ds(c * h, h)
            for j, (px, py) in enumerate(_other_chips(x, y)):
                sent = ins[a].at[me, mine]
                landed = ins[a].at[2 * px + py, mine]
                cp = _rcopy(sent, landed, send_ref.at[3 * a + j], recv_ref.at[3 * a + j], (px, py, c))
                cp.wait_send()
                cp.wait_recv()

    return pl.pallas_call(
        body, name=name, in_specs=[ANY] * n + [SEM, SEM, ANY], out_specs=[ANY] * n, out_shape=_shapes(arrs),
        input_output_aliases={a: a for a in range(n)},
        compiler_params=pltpu.CompilerParams(has_side_effects=EFFECT))(*arrs, send, recv, after)


def _forward_halves(name, arrs):
    n = len(arrs)

    def body(*refs):
        ins, outs = refs[:n], refs[n:2 * n]
        send, recv = refs[2 * n:]
        x, y, c = _place()
        sib = (x, y, 1 - c)
        chips = _other_chips(x, y)
        cps = []
        for a in range(n):
            h = ins[a].shape[1] // 2
            mine = pl.ds(c * h, h)
            for j, (px, py) in enumerate(chips):
                cp = _rcopy(ins[a].at[2 * px + py, mine], outs[a].at[2 * px + py, mine], send.at[3 * a + j],
                            recv.at[3 * a + j], sib)
                cp.start()
                cps.append(cp)
        for a in range(n):
            h = ins[a].shape[1] // 2
            theirs = pl.ds((1 - c) * h, h)
            for j, (px, py) in enumerate(chips):
                got = outs[a].at[2 * px + py, theirs]
                _rcopy(got, got, send.at[3 * a + j], recv.at[3 * a + j], sib).wait_recv()
        for cp in cps:
            cp.wait_send()

    return pl.pallas_call(
        body, name=name, in_specs=[ANY] * n, out_specs=[ANY] * n, out_shape=_shapes(arrs),
        input_output_aliases={a: a for a in range(n)},
        scratch_shapes=[pltpu.SemaphoreType.DMA((3 * n,)), pltpu.SemaphoreType.DMA((3 * n,))])(*arrs)


def _scatter_start(name, thru, arrs):
    n, nt = len(arrs), len(thru)
    landing = [jnp.zeros_like(a) for a in arrs]

    def body(*refs):
        ins = refs[nt:nt + n]
        send, recv = refs[nt + 2 * n], refs[nt + 2 * n + 1]
        outs = refs[2 * nt + 3 * n + 2:]
        x, y, c = _place()
        me = 2 * x + y
        for a in range(n):
            for j, (px, py) in enumerate(_other_chips(x, y)):
                _rcopy(ins[a].at[2 * px + py], outs[a].at[me], send.at[3 * a + j], recv.at[3 * a + j], (px, py, c)).start()

    res = pl.pallas_call(
        body, name=name, in_specs=[ANY] * (nt + 2 * n), out_specs=[SEM, SEM] + [ANY] * (nt + 2 * n),
        out_shape=[pltpu.SemaphoreType.DMA((3 * n,)), pltpu.SemaphoreType.DMA((3 * n,))] + _shapes(thru) + _shapes(arrs)
        + _shapes(landing),
        input_output_aliases={i: 2 + i for i in range(nt + 2 * n)},
        compiler_params=pltpu.CompilerParams(has_side_effects=EFFECT))(*thru, *arrs, *landing)
    return res[0], res[1], res[2:2 + nt], res[2 + nt:2 + nt + n], res[2 + nt + n:]


def _scatter_wait(name, send, recv, arrs, landing, after):
    n = len(arrs)

    def body(*refs):
        ins, land = refs[:n], refs[n:2 * n]
        send_ref, recv_ref = refs[2 * n], refs[2 * n + 1]
        x, y, c = _place()
        for a in range(n):
            for j, (px, py) in enumerate(_other_chips(x, y)):
                cp = _rcopy(ins[a].at[2 * px + py], land[a].at[2 * px + py], send_ref.at[3 * a + j], recv_ref.at[3 * a + j],
                            (px, py, c))
                cp.wait_send()
                cp.wait_recv()

    res = pl.pallas_call(
        body, name=name, in_specs=[ANY] * (2 * n) + [SEM, SEM, ANY], out_specs=[ANY] * (2 * n),
        out_shape=_shapes(arrs) + _shapes(landing), input_output_aliases={i: i for i in range(2 * n)},
        compiler_params=pltpu.CompilerParams(has_side_effects=EFFECT))(*arrs, *landing, send, recv, after)
    return res[:n], res[n:]


def _join_halves(name, arrs):
    n = len(arrs)

    def body(*refs):
        ins, outs = refs[:n], refs[n:2 * n]
        send, recv = refs[2 * n:]
        x, y, c = _place()
        cps = []
        for a in range(n):
            h = ins[a].shape[0] // 2
            mine = pl.ds(c * h, h)
            cp = _rcopy(ins[a].at[mine], outs[a].at[mine], send.at[a], recv.at[a], (x, y, 1 - c))
            cp.start()
            cps.append(cp)
        for a in range(n):
            h = ins[a].shape[0] // 2
            got = outs[a].at[pl.ds((1 - c) * h, h)]
            _rcopy(got, got, send.at[a], recv.at[a], (x, y, 1 - c)).wait_recv()
        for cp in cps:
            cp.wait_send()

    return pl.pallas_call(
        body, name=name, in_specs=[ANY] * n, out_specs=[ANY] * n,
        out_shape=[jax.ShapeDtypeStruct(a.shape, a.dtype) for a in arrs],
        input_output_aliases={a: a for a in range(n)},
        scratch_shapes=[pltpu.SemaphoreType.DMA((n,)), pltpu.SemaphoreType.DMA((n,))])(*arrs)


def _allgather8(name, xs, reduce):
    m_per, n = xs.shape

    def body(x_ref, out_ref, *rest):
        if reduce:
            sum_ref, send, recv, lsem = rest
        else:
            send, recv, lsem = rest
        x, y, c = _place()
        me, sib = (x, y, c), (x, y, 1 - c)
        chips = _other_chips(x, y)

        def rows(px, py, pc):
            return out_ref.at[pl.ds((4 * px + 2 * py + pc) * m_per, m_per), :]

        def copy(k, block, to, src=None):
            return _rcopy(rows(*block) if src is None else src, rows(*block), send.at[k], recv.at[k], to)

        mine = pltpu.make_async_copy(x_ref, rows(*me), lsem)
        mine.start()
        first = [copy(0, me, sib, src=x_ref)]
        first += [copy(1 + j, me, (*chip, c), src=x_ref) for j, chip in enumerate(chips)]
        for cp in first:
            cp.start()
        passed = [copy(4 + j, (*chip, c), sib) for j, chip in enumerate(chips)]
        for j, chip in enumerate(chips):
            copy(1 + j, (*chip, c), me).wait_recv()
            passed[j].start()
        copy(0, sib, me).wait_recv()
        for j, chip in enumerate(chips):
            copy(4 + j, (*chip, 1 - c), me).wait_recv()
        for cp in first + passed:
            cp.wait_send()
        mine.wait()
        if reduce:
            acc = out_ref[pl.ds(0, m_per), :]
            for dev in range(1, N_DEV):
                acc = acc + out_ref[pl.ds(dev * m_per, m_per), :]
            sum_ref[...] = acc

    vm = pl.BlockSpec(memory_space=pltpu.VMEM)
    out_shape = [jax.ShapeDtypeStruct((N_DEV * m_per, n), xs.dtype)]
    if reduce:
        out_shape.append(jax.ShapeDtypeStruct((m_per, n), xs.dtype))
    return pl.pallas_call(
        body, name=name, in_specs=[vm], out_specs=[vm] * len(out_shape), out_shape=out_shape,
        scratch_shapes=[pltpu.SemaphoreType.DMA((7,)), pltpu.SemaphoreType.DMA((7,)), pltpu.SemaphoreType.DMA],
        compiler_params=pltpu.CompilerParams(vmem_limit_bytes=VMEM_LIMIT))(xs)


def _add_my_half(name, a, rb, c_arr):
    s, h, cols = rb.shape
    tr = _tile(h, (512, 352, 256, 128, 64, 32, 16))
    nt = h // tr

    def body(c_ref, a_ref, b_ref, o_ref):
        o_ref[...] = (a_ref[...].astype(F32) + b_ref[...].astype(F32)).astype(o_ref.dtype)

    return pl.pallas_call(
        body, name=name,
        grid_spec=pltpu.PrefetchScalarGridSpec(
            num_scalar_prefetch=1, grid=(s, nt),
            in_specs=[pl.BlockSpec((None, tr, cols), lambda k, i, c: (k, c[0] * nt + i, 0)),
                      pl.BlockSpec((None, tr, cols), lambda k, i, c: (k, i, 0))],
            out_specs=pl.BlockSpec((None, tr, cols), lambda k, i, c: (k, i, 0))),
        out_shape=jax.ShapeDtypeStruct(rb.shape, BF16), compiler_params=_params(2))(c_arr, a, rb)


def _sum_chips(name, own, q, place):
    s, h, cols = q.shape
    tr = _tile(h, (512, 352, 256, 128, 64, 32, 16))
    nt = h // tr

    def body(p_ref, own_ref, q_ref, o_ref):
        chip = p_ref[0]
        acc = jnp.where(chip == 0, own_ref[0], q_ref[0]).astype(F32)
        for j in range(1, s):
            acc = acc + jnp.where(chip == j, own_ref[j], q_ref[j]).astype(F32)
        o_ref[...] = acc

    blk = pl.BlockSpec((s, tr, cols), lambda i, p: (0, i, 0))
    return pl.pallas_call(
        body, name=name,
        grid_spec=pltpu.PrefetchScalarGridSpec(
            num_scalar_prefetch=1, grid=(nt,), in_specs=[blk, blk],
            out_specs=pl.BlockSpec((tr, cols), lambda i, p: (p[1] * nt + i, 0))),
        out_shape=jax.ShapeDtypeStruct((2 * h, cols), F32), compiler_params=_params(1))(place, own, q)


def _pack_rows(parts):
    rows = []
    for p in parts:
        flat = p.reshape(-1).astype(F32)
        n = _roundup(flat.shape[0], 8 * LANE)
        rows.append(jnp.pad(flat, (0, n - flat.shape[0])).reshape(-1, LANE))
    return jnp.concatenate(rows, axis=0)


def _unpack_rows(buf, shapes):
    out, r = [], 0
    for shp in shapes:
        size = math.prod(shp)
        nr = _roundup(size, 8 * LANE) // LANE
        out.append(buf[r:r + nr].reshape(-1)[:size].reshape(shp))
        r += nr
    return out


def kernel(x, attn_norm, gla_w_in, gla_w_a2, gla_b_a2, gla_head_norm, gla_w_out, kv_norm, w_kv, dsa_w_q, dsa_w_out, ffn_norm, ffn_w_up, ffn_conv_w, ffn_conv_b, ffn_w_down, final_norm, loss_target, m_attn_norm, m_gla_w_in, m_gla_w_a2, m_gla_b_a2, m_gla_head_norm, m_gla_w_out, m_kv_norm, m_w_kv, m_dsa_w_q, m_dsa_w_out, m_ffn_norm, m_ffn_w_up, m_ffn_conv_w, m_ffn_conv_b, m_ffn_w_down, m_final_norm, v_attn_norm, v_gla_w_in, v_gla_w_a2, v_gla_b_a2, v_gla_head_norm, v_gla_w_out, v_kv_norm, v_w_kv, v_dsa_w_q, v_dsa_w_out, v_ffn_norm, v_ffn_w_up, v_ffn_conv_w, v_ffn_conv_b, v_ffn_w_down, v_final_norm):
    lay = _layout()
    d, f = D_MODEL, D_FF
    cx, cy, cc = _place()
    chip = 2 * cx + cy
    c_arr = jnp.reshape(cc, (1,)).astype(jnp.int32)
    place = jnp.stack([chip, cc]).astype(jnp.int32)

    groups = {"A": ("gin", "gout", "small"), "B": ("up0", "down0"), "C": ("up1", "down1", "wq", "wkv", "dout")}
    ws = _pack_weights(chip, gla_w_in, gla_w_out, w_kv, dsa_w_q, dsa_w_out, ffn_w_up, ffn_w_down)
    sharded_small = [gla_w_a2[0], gla_b_a2[0], gla_head_norm[0], ffn_conv_w]
    packed = _pack_rows(sharded_small)
    packed = jnp.pad(packed, ((0, _roundup(packed.shape[0], 16) - packed.shape[0]), (0, 0)))
    ws["small"] = lax.dynamic_update_slice(jnp.zeros((N_CHIPS,) + packed.shape, F32), packed[None], (chip, 0, 0))
    ws.update(zip(groups["A"], _gather_weights("gather_a", [ws[k] for k in groups["A"]])))
    in_flight = {}
    thru = [ws[k] for k in groups["A"]]
    for grp in ("B", "C"):
        send, recv, thru, arrs = _gather_start(f"gather_{grp.lower()}_start", thru, [ws[k] for k in groups[grp]])
        ws.update(zip(groups[grp], arrs))
        in_flight[grp] = (send, recv)
    ws.update(zip(groups["A"], thru))
    pending = []

    class _Comm:
        def need(self, grp, ws, after):
            send, recv = in_flight[grp]
            arrs = _gather_wait(f"gather_{grp.lower()}_wait", send, recv, [ws[k] for k in groups[grp]], after)
            arrs = _forward_halves(f"forward_{grp.lower()}", arrs)
            return {**ws, **dict(zip(groups[grp], arrs))}

        def reduce(self, grp, grads, carry):
            names = list(grads)
            parts = [grads[k] for k in names]
            theirs = _swap_halves(f"swap_{grp}", parts)
            sums = [_add_my_half(f"add_half_{k}", a, b, c_arr) for k, a, b in zip(names, parts, theirs)]
            send, recv, thru, sums, landing = _scatter_start(f"scatter_{grp}_start", [carry], sums)
            pending.append((grp, names, send, recv, sums, landing))
            return thru[0]

    shards = [_unpack_rows(ws["small"][s], [p.shape for p in sharded_small]) for s in range(N_CHIPS)]
    w_a2, b_a2, head_norm, conv_w = [jnp.concatenate([shards[s][k] for s in range(N_CHIPS)], axis=-1) for k in range(4)]
    norms, small = _small_params(attn_norm, ffn_norm, kv_norm, final_norm, ffn_conv_b, w_a2, b_a2, head_norm, conv_w)

    comm = _Comm()
    loss_blk, grad_x, sm, last_big = _local_step(x[0], loss_target[0], ws, norms, small, comm)

    small_parts = [loss_blk, jnp.concatenate([sm["attn0"], sm["attn1"]]), jnp.concatenate([sm["ffn0"], sm["ffn1"]]),
                   sm["kv"], sm["final"], jnp.concatenate([sm["conv_b0"], sm["conv_b1"]]),
                   sm["w_a2p"][:GATE_RANK], sm["b_a2"], sm["head_norm"],
                   jnp.stack([jnp.concatenate(sm["conv_w0"]), jnp.concatenate(sm["conv_w1"])])]
    small_shapes = [(8, LANE), (2, d), (2, d), (d,), (d,), (2, f), (GATE_RANK, GLA_KEY_DIM), (GLA_KEY_DIM,),
                    (GLA_VAL_DIM // GLA_HEADS,), (2, 3, f)]
    _, reduced = _allgather8("reduce_small", _pack_rows(small_parts), True)
    reduced = comm.reduce("gla", last_big, reduced)

    gin_w = lay["gin"][1]
    full = {}
    after = reduced
    for grp, names, send, recv, sums, landing in pending:
        sums, landing = _scatter_wait(f"scatter_{grp}_wait", send, recv, sums, landing, after)
        halves = [_sum_chips(f"sum_chips_{k}", s, q, place) for k, s, q in zip(names, sums, landing)]
        joined = _join_halves(f"join_{grp}", halves)
        full.update(zip(names, joined))
        after = joined[0]
    loss_r, g_attn, g_ffn, g_kv, g_final, g_cb, g_a2, g_ba2, g_hn, g_cw = _unpack_rows(reduced, small_shapes)
    loss = loss_r[0, 0]

    def mine(g, axis):
        w = g.shape[axis] // N_CHIPS
        return lax.dynamic_slice_in_dim(g, chip * w, w, axis)

    grads = {
        "attn_norm": g_attn, "gla_w_in": full["gin"][None, :, :gin_w], "gla_w_a2": mine(g_a2, 1)[None],
        "gla_b_a2": mine(g_ba2, 0)[None], "gla_head_norm": mine(g_hn, 0)[None], "gla_w_out": full["gout"][None],
        "kv_norm": g_kv, "w_kv": full["wkv"], "dsa_w_q": full["wq"][None], "dsa_w_out": full["dout"][None],
        "ffn_norm": g_ffn, "ffn_w_up": jnp.stack([full["up0"], full["up1"]]), "ffn_conv_w": mine(g_cw, 2),
        "ffn_conv_b": g_cb, "ffn_w_down": jnp.stack([full["down0"], full["down1"]]), "final_norm": g_final,
    }
    weights = {"attn_norm": (attn_norm, m_attn_norm, v_attn_norm), "gla_w_in": (gla_w_in, m_gla_w_in, v_gla_w_in),
               "gla_w_a2": (gla_w_a2, m_gla_w_a2, v_gla_w_a2), "gla_b_a2": (gla_b_a2, m_gla_b_a2, v_gla_b_a2),
               "gla_head_norm": (gla_head_norm, m_gla_head_norm, v_gla_head_norm),
               "gla_w_out": (gla_w_out, m_gla_w_out, v_gla_w_out), "kv_norm": (kv_norm, m_kv_norm, v_kv_norm),
               "w_kv": (w_kv, m_w_kv, v_w_kv), "dsa_w_q": (dsa_w_q, m_dsa_w_q, v_dsa_w_q),
               "dsa_w_out": (dsa_w_out, m_dsa_w_out, v_dsa_w_out), "ffn_norm": (ffn_norm, m_ffn_norm, v_ffn_norm),
               "ffn_w_up": (ffn_w_up, m_ffn_w_up, v_ffn_w_up), "ffn_conv_w": (ffn_conv_w, m_ffn_conv_w, v_ffn_conv_w),
               "ffn_conv_b": (ffn_conv_b, m_ffn_conv_b, v_ffn_conv_b),
               "ffn_w_down": (ffn_w_down, m_ffn_w_down, v_ffn_w_down), "final_norm": (final_norm, m_final_norm, v_final_norm)}
    order = list(weights)
    big_names = ("gla_w_in", "gla_w_out", "w_kv", "dsa_w_q", "dsa_w_out", "ffn_w_up", "ffn_w_down")
    delta, new_m, new_v = {}, {}, {}
    for k in big_names:
        w, m, v = weights[k]
        cols = w.shape[-1]
        res = _adamw(f"adamw_{k}", w.reshape(-1, cols), grads[k].reshape(-1, cols), m.reshape(-1, cols), v.reshape(-1, cols))
        delta[k], new_m[k], new_v[k] = [r.reshape(w.shape) for r in res]
    small_names = [k for k in order if k not in big_names]
    packed = [_pack_rows([src[k] for k in small_names])
              for src in ({k: weights[k][0] for k in small_names}, grads, {k: weights[k][1] for k in small_names},
                          {k: weights[k][2] for k in small_names})]
    res = _adamw("adamw_small", *packed)
    shapes = [weights[k][0].shape for k in small_names]
    for dst, buf in zip((delta, new_m, new_v), res):
        for k, val in zip(small_names, _unpack_rows(buf, shapes)):
            dst[k] = val
    return (loss, grad_x[None], *[grads[k] for k in order], *[delta[k] for k in order], *[new_m[k] for k in order],
            *[new_v[k] for k in order])
```

```python
import math

import jax
import jax.numpy as jnp
from jax import lax
from jax.experimental import pallas as pl
from jax.experimental.pallas import tpu as pltpu

F32 = jnp.float32
BF16 = jnp.bfloat16

D_MODEL = 2048
SEQ = 4096
GLA_HEADS = 4
GLA_KEY_DIM = D_MODEL // 2
GLA_VAL_DIM = D_MODEL
GATE_RANK = 16
GATE_NORMALIZER = 16.0
GLA_CHUNK = 64
ATT_HEADS = 16
HEAD_DIM = 128
WINDOWS = (128, 512, 2048)
DILATIONS = (1, 4, 16)
ATT_BLOCK = 128
D_FF = 5632
EPS = 1e-6
ADAM_LR = 0.001
ADAM_B1 = 0.9
ADAM_B2 = 0.999
ADAM_EPS = 1e-08
ADAM_WD = 0.01
ADAM_STEP = 10

N_CHIPS = 4
N_DEV = 8
LANE = 128
A_PAD = 128
VMEM_LIMIT = 56 * 1024 * 1024
MAX_K_TILE = 2816
NEG = -1e30
MESH = pl.DeviceIdType.MESH

NN = (((1,), (0,)), ((), ()))
NT = (((1,), (1,)), ((), ()))
TN = (((0,), (0,)), ((), ()))


def _tile(n, cands):
    for c in cands:
        if c <= n and n % c == 0:
            return c
    return n


def _roundup(n, m):
    return -(-n // m) * m


def _params(n_axes):
    return pltpu.CompilerParams(dimension_semantics=("arbitrary",) * n_axes, vmem_limit_bytes=VMEM_LIMIT)


def _dot(a, b, dims):
    return lax.dot_general(a, b, dims, preferred_element_type=F32)


def _sigmoid(x):
    return 1.0 / (1.0 + jnp.exp(-x))


COL_SHARDED = ("gin", "up0", "up1", "wq", "wkv")
ROW_SHARDED = ("gout", "down0", "down1", "dout")


def _layout():
    f = D_FF
    hd = ATT_HEADS * HEAD_DIM
    gin = 2 * GLA_KEY_DIM + 2 * GLA_VAL_DIM + GATE_RANK
    up_w = 2 * f // N_CHIPS
    q_w = 3 * hd // N_CHIPS
    kv_w = 2 * hd // N_CHIPS
    dn_r = f // N_CHIPS
    go_r = GLA_VAL_DIM // N_CHIPS
    do_r = hd // N_CHIPS
    big = (1408, 1024, 512, 256, 128)
    return {
        "gin": (0, gin // N_CHIPS, LANE),
        "up0": (0, up_w, _tile(up_w, big)), "up1": (0, up_w, _tile(up_w, big)),
        "wq": (0, q_w, _tile(q_w, (512, 384, 256, 128))), "wkv": (0, kv_w, _tile(kv_w, (1024, 512, 256, 128))),
        "down0": (0, dn_r, _tile(dn_r, big)), "down1": (0, dn_r, _tile(dn_r, big)),
        "gout": (0, go_r, _tile(go_r, (512, 256, 128))), "dout": (0, do_r, _tile(do_r, (512, 256, 128))),
    }


def _matmul(name, a, b, dims, grid, a_spec, b_spec, o_spec, out_shape, acc_shape, add=None, add_spec=None):
    nk = grid[2]
    has_add = add is not None

    def body(*refs):
        a_ref, b_ref = refs[0], refs[1]
        pos = 2
        add_ref = None
        if has_add:
            add_ref = refs[pos]
            pos += 1
        o_ref = refs[pos]
        prod = _dot(a_ref[...].astype(BF16), b_ref[...].astype(BF16), dims)

        def finish(val):
            if has_add:
                val = val + add_ref[...].astype(F32)
            o_ref[...] = val.astype(o_ref.dtype)

        if nk == 1:
            finish(prod)
        else:
            acc_ref = refs[pos + 1]
            k = pl.program_id(2)

            @pl.when(k == 0)
            def _():
                acc_ref[...] = prod

            @pl.when(k > 0)
            def _():
                acc_ref[...] += prod

            @pl.when(k == nk - 1)
            def _():
                finish(acc_ref[...])

    in_specs = [a_spec, b_spec]
    args = [a, b]
    if has_add:
        in_specs.append(add_spec)
        args.append(add)
    scratch = [] if nk == 1 else [pltpu.VMEM(acc_shape, F32)]
    return pl.pallas_call(body, name=name, grid=grid, in_specs=in_specs, out_specs=o_spec, out_shape=out_shape,
                          scratch_shapes=scratch, compiler_params=_params(3))(*args)


def _mm_act_wc(name, a, wc, seg, out_dtype):
    off, w, tn = seg
    t_len, d = a.shape
    tm = _tile(t_len, (1024, 512, 256, 128))
    nps = w // tn
    ob = off // tn
    grid = (t_len // tm, N_CHIPS * nps, 1)
    return _matmul(
        name, a, wc, NN, grid,
        pl.BlockSpec((tm, d), lambda i, j, k: (i, 0)),
        pl.BlockSpec((None, d, tn), lambda i, j, k: (j // nps, 0, ob + j % nps)),
        pl.BlockSpec((tm, tn), lambda i, j, k: (i, j)),
        jax.ShapeDtypeStruct((t_len, N_CHIPS * w), out_dtype), (tm, tn))


def _mm_dact_wcT(name, dy, wc, seg, add=None):
    off, w, tk = seg
    if off == 0 and w <= MAX_K_TILE:
        tk = w
    t_len = dy.shape[0]
    d = wc.shape[1]
    tm = _tile(t_len, (1024, 512, 256, 128))
    tn = _tile(d, (1024, 512, 256, 128))
    kps = w // tk
    ob = off // tk
    grid = (t_len // tm, d // tn, N_CHIPS * kps)
    return _matmul(
        name, dy, wc, NT, grid,
        pl.BlockSpec((tm, tk), lambda i, j, k: (i, k)),
        pl.BlockSpec((None, tn, tk), lambda i, j, k: (k // kps, j, ob + k % kps)),
        pl.BlockSpec((tm, tn), lambda i, j, k: (i, j)),
        jax.ShapeDtypeStruct((t_len, d), F32), (tm, tn),
        add=add, add_spec=None if add is None else pl.BlockSpec((tm, tn), lambda i, j, k: (i, j)))


def _mm_grad_wc(name, a, dy, seg):
    _, w, tn = seg
    t_len, d = a.shape
    tm = _tile(d, (1024, 512, 256, 128))
    tk = _tile(t_len, (2048, 1024, 512, 256, 128))
    nps = w // tn
    grid = (d // tm, N_CHIPS * nps, t_len // tk)
    return _matmul(
        name, a, dy, TN, grid,
        pl.BlockSpec((tk, tm), lambda i, j, k: (k, i)),
        pl.BlockSpec((tk, tn), lambda i, j, k: (k, j)),
        pl.BlockSpec((None, tm, tn), lambda i, j, k: (j // nps, i, j % nps)),
        jax.ShapeDtypeStruct((N_CHIPS, d, w), BF16), (tm, tn))


def _mm_act_wr(name, a, wr, seg, add):
    off, r, tk = seg
    t_len = a.shape[0]
    d = wr.shape[2]
    tm = _tile(t_len, (1024, 512, 256, 128))
    tn = _tile(d, (1024, 512, 256, 128))
    kps = r // tk
    ob = off // tk
    grid = (t_len // tm, d // tn, N_CHIPS * kps)
    return _matmul(
        name, a, wr, NN, grid,
        pl.BlockSpec((tm, tk), lambda i, j, k: (i, k)),
        pl.BlockSpec((None, tk, tn), lambda i, j, k: (k // kps, ob + k % kps, j)),
        pl.BlockSpec((tm, tn), lambda i, j, k: (i, j)),
        jax.ShapeDtypeStruct((t_len, d), F32), (tm, tn),
        add=add, add_spec=pl.BlockSpec((tm, tn), lambda i, j, k: (i, j)))


def _mm_dact_wrT(name, dh, wr, seg):
    off, r, tn = seg
    t_len, d = dh.shape
    tm = _tile(t_len, (1024, 512, 256, 128))
    nps = r // tn
    ob = off // tn
    grid = (t_len // tm, N_CHIPS * nps, 1)
    return _matmul(
        name, dh, wr, NT, grid,
        pl.BlockSpec((tm, d), lambda i, j, k: (i, 0)),
        pl.BlockSpec((None, tn, d), lambda i, j, k: (j // nps, ob + j % nps, 0)),
        pl.BlockSpec((tm, tn), lambda i, j, k: (i, j)),
        jax.ShapeDtypeStruct((t_len, N_CHIPS * r), BF16), (tm, tn))


def _mm_grad_wr(name, a, dh, seg):
    _, r, tm = seg
    t_len, d = dh.shape
    tn = _tile(d, (1024, 512, 256, 128))
    tk = _tile(t_len, (2048, 1024, 512, 256, 128))
    mps = r // tm
    grid = (N_CHIPS * mps, d // tn, t_len // tk)
    return _matmul(
        name, a, dh, TN, grid,
        pl.BlockSpec((tk, tm), lambda i, j, k: (k, i)),
        pl.BlockSpec((tk, tn), lambda i, j, k: (k, j)),
        pl.BlockSpec((None, tm, tn), lambda i, j, k: (i // mps, i % mps, j)),
        jax.ShapeDtypeStruct((N_CHIPS, r, d), BF16), (tm, tn))


def _mm_plain(name, a, b, dims, out_dtype, add=None):
    if dims == NN:
        m, kd = a.shape
        n = b.shape[1]
    elif dims == NT:
        m, kd = a.shape
        n = b.shape[0]
    else:
        kd, m = a.shape
        n = b.shape[1]
    tm = _tile(m, (1024, 512, 256, 128))
    tn = _tile(n, (1024, 768, 512, 256, 128))
    tk = _tile(kd, (2048, 1024, 512, 256, 128))
    grid = (m // tm, n // tn, kd // tk)
    if dims == NN:
        a_spec = pl.BlockSpec((tm, tk), lambda i, j, k: (i, k))
        b_spec = pl.BlockSpec((tk, tn), lambda i, j, k: (k, j))
    elif dims == NT:
        a_spec = pl.BlockSpec((tm, tk), lambda i, j, k: (i, k))
        b_spec = pl.BlockSpec((tn, tk), lambda i, j, k: (j, k))
    else:
        a_spec = pl.BlockSpec((tk, tm), lambda i, j, k: (k, i))
        b_spec = pl.BlockSpec((tk, tn), lambda i, j, k: (k, j))
    o_spec = pl.BlockSpec((tm, tn), lambda i, j, k: (i, j))
    return _matmul(name, a, b, dims, grid, a_spec, b_spec, o_spec, jax.ShapeDtypeStruct((m, n), out_dtype), (tm, tn),
                   add=add, add_spec=None if add is None else o_spec)


def _rms_fwd(name, x, g):
    t_len, d = x.shape
    tm = _tile(t_len, (512, 256, 128))

    def body(x_ref, g_ref, o_ref):
        xv = x_ref[...]
        r = lax.rsqrt(jnp.mean(xv * xv, axis=-1, keepdims=True) + EPS)
        o_ref[...] = (xv * r * g_ref[...]).astype(o_ref.dtype)

    return pl.pallas_call(
        body, name=name, grid=(t_len // tm,),
        in_specs=[pl.BlockSpec((tm, d), lambda i: (i, 0)), pl.BlockSpec((1, d), lambda i: (0, 0))],
        out_specs=pl.BlockSpec((tm, d), lambda i: (i, 0)),
        out_shape=jax.ShapeDtypeStruct((t_len, d), BF16), compiler_params=_params(1))(x, g)


def _rms_bwd(name, dy, x, g, dres):
    t_len, d = x.shape
    tm = _tile(t_len, (256, 128))

    def body(dy_ref, x_ref, g_ref, dres_ref, dx_ref, dg_ref):
        xv = x_ref[...]
        r = lax.rsqrt(jnp.mean(xv * xv, axis=-1, keepdims=True) + EPS)
        xhat = xv * r
        dyv = dy_ref[...].astype(F32)
        dxn = dyv * g_ref[...]
        dx = r * (dxn - xhat * jnp.mean(dxn * xhat, axis=-1, keepdims=True))
        dx_ref[...] = dres_ref[...] + dx
        part = jnp.sum(dyv * xhat, axis=0, keepdims=True)

        @pl.when(pl.program_id(0) == 0)
        def _():
            dg_ref[...] = part

        @pl.when(pl.program_id(0) > 0)
        def _():
            dg_ref[...] += part

    row = pl.BlockSpec((tm, d), lambda i: (i, 0))
    vec = pl.BlockSpec((1, d), lambda i: (0, 0))
    return pl.pallas_call(
        body, name=name, grid=(t_len // tm,), in_specs=[row, row, vec, row], out_specs=(row, vec),
        out_shape=(jax.ShapeDtypeStruct((t_len, d), F32), jax.ShapeDtypeStruct((1, d), F32)),
        compiler_params=_params(1))(dy, x, g, dres)


def _loss_head(h, g, target):
    t_len, d = h.shape
    tm = _tile(t_len, (256, 128))

    def body(h_ref, g_ref, t_ref, dh_ref, dg_ref, loss_ref):
        xv = h_ref[...]
        gv = g_ref[...]
        r = lax.rsqrt(jnp.mean(xv * xv, axis=-1, keepdims=True) + EPS)
        xhat = xv * r
        err = xhat * gv - t_ref[...]
        dyv = err * (1.0 / d)
        dxn = dyv * gv
        dh_ref[...] = r * (dxn - xhat * jnp.mean(dxn * xhat, axis=-1, keepdims=True))
        part = jnp.sum(dyv * xhat, axis=0, keepdims=True)
        lpart = jnp.zeros((8, LANE), F32) + (0.5 / d) * jnp.sum(err * err)

        @pl.when(pl.program_id(0) == 0)
        def _():
            dg_ref[...] = part
            loss_ref[...] = lpart

        @pl.when(pl.program_id(0) > 0)
        def _():
            dg_ref[...] += part
            loss_ref[...] += lpart

    row = pl.BlockSpec((tm, d), lambda i: (i, 0))
    vec = pl.BlockSpec((1, d), lambda i: (0, 0))
    return pl.pallas_call(
        body, name="loss_head", grid=(t_len // tm,), in_specs=[row, vec, row],
        out_specs=(row, vec, pl.BlockSpec((8, LANE), lambda i: (0, 0))),
        out_shape=(jax.ShapeDtypeStruct((t_len, d), F32), jax.ShapeDtypeStruct((1, d), F32),
                   jax.ShapeDtypeStruct((8, LANE), F32)),
        compiler_params=_params(1))(h, g, target)


def _chunk_row(shape):
    return lax.broadcasted_iota(jnp.int32, shape, 0) % GLA_CHUNK


def _gla_gate_fwd(a, w_a2p, b_a2):
    t_len = a.shape[0]
    kd = w_a2p.shape[1]
    tm = _tile(t_len, (256, 128, 64))

    def body(a_ref, w_ref, b_ref, ga_ref, cum_ref):
        ga = _dot(a_ref[...], w_ref[...].astype(BF16), NN) + b_ref[...]
        ga_ref[...] = ga
        la = (jnp.minimum(ga, 0.0) - jnp.log(1.0 + jnp.exp(-jnp.abs(ga)))) * (1.0 / GATE_NORMALIZER)
        row = _chunk_row(la.shape)
        s = 1
        while s < GLA_CHUNK:
            la = la + jnp.where(row >= s, pltpu.roll(la, s, 0), 0.0)
            s *= 2
        cum_ref[...] = la

    return pl.pallas_call(
        body, name="gla_gate_fwd", grid=(t_len // tm,),
        in_specs=[pl.BlockSpec((tm, A_PAD), lambda i: (i, 0)), pl.BlockSpec((A_PAD, kd), lambda i: (0, 0)),
                  pl.BlockSpec((1, kd), lambda i: (0, 0))],
        out_specs=(pl.BlockSpec((tm, kd), lambda i: (i, 0)), pl.BlockSpec((tm, kd), lambda i: (i, 0))),
        out_shape=(jax.ShapeDtypeStruct((t_len, kd), F32), jax.ShapeDtypeStruct((t_len, kd), F32)),
        compiler_params=_params(1))(a, w_a2p, b_a2)


def _gla_gate_bwd(dcum, ga, a, w_a2p):
    t_len, kd = dcum.shape
    tm = _tile(t_len, (256, 128, 64))

    def body(dc_ref, ga_ref, a_ref, w_ref, da_ref, dw_ref, db_ref):
        x = dc_ref[...]
        row = _chunk_row(x.shape)
        s = 1
        while s < GLA_CHUNK:
            x = x + jnp.where(row < GLA_CHUNK - s, pltpu.roll(x, tm - s, 0), 0.0)
            s *= 2
        dga = x * (1.0 / GATE_NORMALIZER) * _sigmoid(-ga_ref[...])
        dgab = dga.astype(BF16)
        da_ref[...] = _dot(dgab, w_ref[...].astype(BF16), NT).astype(da_ref.dtype)
        dw = _dot(a_ref[...], dgab, TN)
        db = jnp.sum(dga, axis=0, keepdims=True)

        @pl.when(pl.program_id(0) == 0)
        def _():
            dw_ref[...] = dw
            db_ref[...] = db

        @pl.when(pl.program_id(0) > 0)
        def _():
            dw_ref[...] += dw
            db_ref[...] += db

    wide = pl.BlockSpec((tm, kd), lambda i: (i, 0))
    return pl.pallas_call(
        body, name="gla_gate_bwd", grid=(t_len // tm,),
        in_specs=[wide, wide, pl.BlockSpec((tm, A_PAD), lambda i: (i, 0)), pl.BlockSpec((A_PAD, kd), lambda i: (0, 0))],
        out_specs=(pl.BlockSpec((tm, A_PAD), lambda i: (i, 0)), pl.BlockSpec((A_PAD, kd), lambda i: (0, 0)),
                   pl.BlockSpec((1, kd), lambda i: (0, 0))),
        out_shape=(jax.ShapeDtypeStruct((t_len, A_PAD), BF16), jax.ShapeDtypeStruct((A_PAD, kd), F32),
                   jax.ShapeDtypeStruct((1, kd), F32)),
        compiler_params=_params(1))(dcum, ga, a, w_a2p)


GLA_STEP_CHUNKS = 4


def _gla_dims():
    dk = GLA_KEY_DIM // GLA_HEADS
    dv = GLA_VAL_DIM // GLA_HEADS
    return dk, dv


def _gla_fwd(proj, cum):
    t_len = proj.shape[0]
    dk, dv = _gla_dims()
    nc = t_len // GLA_CHUNK
    c = GLA_CHUNK
    scale = dk ** -0.5
    v0 = 2 * GLA_KEY_DIM // dv

    per = _tile(nc, (GLA_STEP_CHUNKS, 2, 1))
    rows = per * c

    def body(q_ref, k_ref, v_ref, cum_ref, o_ref, st_ref, s_scr):
        @pl.when(pl.program_id(1) == 0)
        def _():
            s_scr[...] = jnp.zeros_like(s_scr)

        tri = lax.broadcasted_iota(jnp.int32, (c, c), 0) >= lax.broadcasted_iota(jnp.int32, (c, c), 1)
        for i in range(per):
            rs = slice(i * c, (i + 1) * c)
            cm = cum_ref[rs, :]
            last = cm[c - 1:c, :]
            q = q_ref[rs, :].astype(F32) * scale
            k = k_ref[rs, :].astype(F32)
            v = v_ref[rs, :].astype(BF16)
            qd = (q * jnp.exp(cm)).astype(BF16)
            ki = (k * jnp.exp(-cm)).astype(BF16)
            ke = (k * jnp.exp(last - cm)).astype(BF16)
            sc = jnp.where(tri, _dot(qd, ki, NT), 0.0)
            st = s_scr[...]
            st_ref[i] = st
            o_ref[rs, :] = _dot(sc.astype(BF16), v, NN) + _dot(qd, st.astype(BF16), NT)
            s_scr[...] = st * jnp.exp(last) + _dot(v, ke, TN)

    return pl.pallas_call(
        body, name="gla_fwd", grid=(GLA_HEADS, nc // per),
        in_specs=[pl.BlockSpec((rows, dk), lambda h, n: (n, h)),
                  pl.BlockSpec((rows, dk), lambda h, n: (n, GLA_HEADS + h)),
                  pl.BlockSpec((rows, dv), lambda h, n: (n, v0 + h)),
                  pl.BlockSpec((rows, dk), lambda h, n: (n, h))],
        out_specs=(pl.BlockSpec((rows, dv), lambda h, n: (n, h)),
                   pl.BlockSpec((None, per, dv, dk), lambda h, n: (h, n, 0, 0))),
        out_shape=(jax.ShapeDtypeStruct((t_len, GLA_VAL_DIM), F32),
                   jax.ShapeDtypeStruct((GLA_HEADS, nc, dv, dk), F32)),
        scratch_shapes=[pltpu.VMEM((dv, dk), F32)], compiler_params=_params(2))(proj, proj, proj, cum)


def _gla_bwd(proj, cum, states, do):
    t_len = proj.shape[0]
    dk, dv = _gla_dims()
    nc = t_len // GLA_CHUNK
    c = GLA_CHUNK
    scale = dk ** -0.5
    v0 = 2 * GLA_KEY_DIM // dv

    per = _tile(nc, (GLA_STEP_CHUNKS, 2, 1))
    rows = per * c

    def body(q_ref, k_ref, v_ref, cum_ref, st_ref, do_ref, dq_ref, dk_ref, dv_ref, dc_ref, ds_scr):
        @pl.when(pl.program_id(1) == 0)
        def _():
            ds_scr[...] = jnp.zeros_like(ds_scr)

        tri = lax.broadcasted_iota(jnp.int32, (c, c), 0) >= lax.broadcasted_iota(jnp.int32, (c, c), 1)
        row = lax.broadcasted_iota(jnp.int32, (c, dk), 0)
        for i in reversed(range(per)):
            rs = slice(i * c, (i + 1) * c)
            cm = cum_ref[rs, :]
            last = cm[c - 1:c, :]
            e_c = jnp.exp(cm)
            e_nc = jnp.exp(-cm)
            e_lc = jnp.exp(last - cm)
            e_l = jnp.exp(last)
            q = q_ref[rs, :].astype(F32) * scale
            k = k_ref[rs, :].astype(F32)
            v = v_ref[rs, :].astype(BF16)
            dov = do_ref[rs, :]
            qd32 = q * e_c
            ki32 = k * e_nc
            ke32 = k * e_lc
            qd = qd32.astype(BF16)
            ki = ki32.astype(BF16)
            ke = ke32.astype(BF16)
            st = st_ref[i]
            dst = ds_scr[...]
            dstb = dst.astype(BF16)
            am = jnp.where(tri, _dot(dov, v, NT), 0.0).astype(BF16)
            pm = jnp.where(tri, _dot(qd, ki, NT), 0.0).astype(BF16)
            dqd = _dot(am, ki, NN) + _dot(dov, st.astype(BF16), NN)
            dki = _dot(am, qd, TN)
            dvv = _dot(pm, dov, TN) + _dot(ke, dstb, NT)
            dke = _dot(v, dstb, NN)
            d_el = jnp.sum(dst * st, axis=0, keepdims=True)
            ds_scr[...] = dst * e_l + _dot(dov, qd, TN)
            dq_ref[rs, :] = (dqd * scale * e_c).astype(dq_ref.dtype)
            dk_ref[rs, :] = (dki * e_nc + dke * e_lc).astype(dk_ref.dtype)
            dv_ref[rs, :] = dvv.astype(dv_ref.dtype)
            dkeke = dke * ke32
            dcum = dqd * qd32 - dki * ki32 - dkeke
            dlast = jnp.sum(dkeke, axis=0, keepdims=True) + d_el * e_l
            dc_ref[rs, :] = jnp.where(row == c - 1, dcum + dlast, dcum)

    rev = nc // per - 1
    return pl.pallas_call(
        body, name="gla_bwd", grid=(GLA_HEADS, nc // per),
        in_specs=[pl.BlockSpec((rows, dk), lambda h, n: (rev - n, h)),
                  pl.BlockSpec((rows, dk), lambda h, n: (rev - n, GLA_HEADS + h)),
                  pl.BlockSpec((rows, dv), lambda h, n: (rev - n, v0 + h)),
                  pl.BlockSpec((rows, dk), lambda h, n: (rev - n, h)),
                  pl.BlockSpec((None, per, dv, dk), lambda h, n: (h, rev - n, 0, 0)),
                  pl.BlockSpec((rows, dv), lambda h, n: (rev - n, h))],
        out_specs=(pl.BlockSpec((rows, dk), lambda h, n: (rev - n, h)),
                   pl.BlockSpec((rows, dk), lambda h, n: (rev - n, h)),
                   pl.BlockSpec((rows, dv), lambda h, n: (rev - n, h)),
                   pl.BlockSpec((rows, dk), lambda h, n: (rev - n, h))),
        out_shape=(jax.ShapeDtypeStruct((t_len, GLA_KEY_DIM), BF16), jax.ShapeDtypeStruct((t_len, GLA_KEY_DIM), BF16),
                   jax.ShapeDtypeStruct((t_len, GLA_VAL_DIM), BF16), jax.ShapeDtypeStruct((t_len, GLA_KEY_DIM), F32)),
        scratch_shapes=[pltpu.VMEM((dv, dk), F32)], compiler_params=_params(2))(proj, proj, proj, cum, states, do)


def _gla_out_fwd(o, proj, gn):
    t_len = o.shape[0]
    _, dv = _gla_dims()
    tm = _tile(t_len, (512, 256, 128))
    r0 = (2 * GLA_KEY_DIM + GLA_VAL_DIM) // dv

    def body(o_ref, r_ref, g_ref, y_ref):
        ov = o_ref[...]
        rs = lax.rsqrt(jnp.mean(ov * ov, axis=-1, keepdims=True) + EPS)
        rv = r_ref[...].astype(F32)
        y_ref[...] = (ov * rs * g_ref[...] * (rv * _sigmoid(rv))).astype(y_ref.dtype)

    return pl.pallas_call(
        body, name="gla_out_fwd", grid=(t_len // tm, GLA_HEADS),
        in_specs=[pl.BlockSpec((tm, dv), lambda i, h: (i, h)), pl.BlockSpec((tm, dv), lambda i, h: (i, r0 + h)),
                  pl.BlockSpec((1, dv), lambda i, h: (0, 0))],
        out_specs=pl.BlockSpec((tm, dv), lambda i, h: (i, h)),
        out_shape=jax.ShapeDtypeStruct((t_len, GLA_VAL_DIM), BF16), compiler_params=_params(2))(o, proj, gn)


def _gla_out_bwd(dy, o, proj, gn):
    t_len = o.shape[0]
    _, dv = _gla_dims()
    tm = _tile(t_len, (512, 256, 128))
    r0 = (2 * GLA_KEY_DIM + GLA_VAL_DIM) // dv

    def body(dy_ref, o_ref, r_ref, g_ref, do_ref, dr_ref, dg_ref):
        ov = o_ref[...]
        gv = g_ref[...]
        rs = lax.rsqrt(jnp.mean(ov * ov, axis=-1, keepdims=True) + EPS)
        xhat = ov * rs
        rv = r_ref[...].astype(F32)
        sg = _sigmoid(rv)
        gate = rv * sg
        dyv = dy_ref[...].astype(F32)
        dn = dyv * gate
        dr_ref[...] = (dyv * xhat * gv * (sg * (1.0 + rv * (1.0 - sg)))).astype(dr_ref.dtype)
        dxn = dn * gv
        do_ref[...] = (rs * (dxn - xhat * jnp.mean(dxn * xhat, axis=-1, keepdims=True))).astype(do_ref.dtype)
        part = jnp.sum(dn * xhat, axis=0, keepdims=True)
        first = (pl.program_id(0) == 0) & (pl.program_id(1) == 0)

        @pl.when(first)
        def _():
            dg_ref[...] = part

        @pl.when(jnp.logical_not(first))
        def _():
            dg_ref[...] += part

    blk = pl.BlockSpec((tm, dv), lambda i, h: (i, h))
    return pl.pallas_call(
        body, name="gla_out_bwd", grid=(t_len // tm, GLA_HEADS),
        in_specs=[blk, blk, pl.BlockSpec((tm, dv), lambda i, h: (i, r0 + h)), pl.BlockSpec((1, dv), lambda i, h: (0, 0))],
        out_specs=(blk, blk, pl.BlockSpec((1, dv), lambda i, h: (0, 0))),
        out_shape=(jax.ShapeDtypeStruct((t_len, GLA_VAL_DIM), BF16), jax.ShapeDtypeStruct((t_len, GLA_VAL_DIM), BF16),
                   jax.ShapeDtypeStruct((1, dv), F32)),
        compiler_params=_params(2))(dy, o, proj, gn)


def _alibi_slopes():
    n = ATT_HEADS
    start = 2.0 ** (-8.0 / n)
    return [start ** (i + 1) for i in range(n)]


def _att_masks(d):
    b = ATT_BLOCK
    qa = lax.broadcasted_iota(jnp.int32, (b, b), 0)
    kb = lax.broadcasted_iota(jnp.int32, (b, b), 1)
    dist_c = qa - kb
    dist_p = qa - kb + b
    return dist_c >= 0, dist_p <= b, (dist_c * d).astype(F32), (dist_p * d).astype(F32)


def _att_fwd(q_all, kv, g):
    d = DILATIONS[g]
    assert WINDOWS[g] // d == ATT_BLOCK
    t_len = q_all.shape[0]
    hd = ATT_HEADS * HEAD_DIM
    sub = t_len // d
    nb = sub // ATT_BLOCK
    b = ATT_BLOCK
    e = HEAD_DIM
    scale = e ** -0.5
    slopes = _alibi_slopes()
    qv = q_all.reshape(sub, d * 3 * hd)
    kvv = kv.reshape(sub, d * 2 * hd)

    def body(q_ref, kp_ref, kc_ref, vp_ref, vc_ref, o_ref, l_ref, s_scr, p_scr, li_scr):
        ib = pl.program_id(1)
        valid_c, valid_p0, dist_c, dist_p = _att_masks(d)
        valid_p = valid_p0 & (ib > 0)
        for h in range(ATT_HEADS):
            hs = slice(h * e, (h + 1) * e)
            qh = q_ref[:, hs]
            s_scr[h, 0] = _dot(qh, kc_ref[:, hs], NT)
            s_scr[h, 1] = _dot(qh, kp_ref[:, hs], NT)
        l_ref[...] = jnp.zeros_like(l_ref)
        for h in range(ATT_HEADS):
            s_c = jnp.where(valid_c, s_scr[h, 0] * scale - slopes[h] * dist_c, NEG)
            s_p = jnp.where(valid_p, s_scr[h, 1] * scale - slopes[h] * dist_p, NEG)
            m = jnp.maximum(jnp.max(s_c, axis=1, keepdims=True), jnp.max(s_p, axis=1, keepdims=True))
            p_c = jnp.where(valid_c, jnp.exp(s_c - m), 0.0)
            p_p = jnp.where(valid_p, jnp.exp(s_p - m), 0.0)
            l = jnp.sum(p_c, axis=1, keepdims=True) + jnp.sum(p_p, axis=1, keepdims=True)
            p_scr[h, 0] = p_c.astype(BF16)
            p_scr[h, 1] = p_p.astype(BF16)
            li_scr[:, h:h + 1] = 1.0 / l
            l_ref[:, h:h + 1] = m + jnp.log(l)
        for h in range(ATT_HEADS):
            hs = slice(h * e, (h + 1) * e)
            acc = _dot(p_scr[h, 0], vc_ref[:, hs], NN) + _dot(p_scr[h, 1], vp_ref[:, hs], NN)
            o_ref[:, hs] = acc * li_scr[:, h:h + 1]

    blk = (b, hd)
    cblk = (b, LANE)
    o, lse = pl.pallas_call(
        body, name=f"att_fwd{g}", grid=(d, nb),
        scratch_shapes=[pltpu.VMEM((ATT_HEADS, 2, b, b), F32), pltpu.VMEM((ATT_HEADS, 2, b, b), BF16),
                        pltpu.VMEM((b, LANE), F32)],
        in_specs=[pl.BlockSpec(blk, lambda r, i: (i, 3 * r + g)),
                  pl.BlockSpec(blk, lambda r, i: (jnp.maximum(i - 1, 0), 2 * r)),
                  pl.BlockSpec(blk, lambda r, i: (i, 2 * r)),
                  pl.BlockSpec(blk, lambda r, i: (jnp.maximum(i - 1, 0), 2 * r + 1)),
                  pl.BlockSpec(blk, lambda r, i: (i, 2 * r + 1))],
        out_specs=(pl.BlockSpec(blk, lambda r, i: (i, r)), pl.BlockSpec(cblk, lambda r, i: (i, r))),
        out_shape=(jax.ShapeDtypeStruct((sub, d * hd), F32), jax.ShapeDtypeStruct((sub, d * LANE), F32)),
        compiler_params=_params(2))(qv, kvv, kvv, kvv, kvv)
    return o.reshape(t_len, hd), lse.reshape(t_len, LANE)


def _att_merge(os, ls):
    t_len, hd = os[0].shape
    tm = _tile(t_len, (256, 128))
    e = HEAD_DIM

    def body(o0, o1, o2, l0, l1, l2, of_ref, ob_ref, l_ref):
        a0, a1, a2 = l0[...], l1[...], l2[...]
        m = jnp.maximum(jnp.maximum(a0, a1), a2)
        e0, e1, e2 = jnp.exp(a0 - m), jnp.exp(a1 - m), jnp.exp(a2 - m)
        den = e0 + e1 + e2
        w0, w1, w2 = e0 / den, e1 / den, e2 / den
        l_ref[...] = m + jnp.log(den)
        for h in range(ATT_HEADS):
            hs = slice(h * e, (h + 1) * e)
            c = slice(h, h + 1)
            o = w0[:, c] * o0[:, hs] + w1[:, c] * o1[:, hs] + w2[:, c] * o2[:, hs]
            of_ref[:, hs] = o
            ob_ref[:, hs] = o.astype(ob_ref.dtype)

    row = pl.BlockSpec((tm, hd), lambda i: (i, 0))
    crow = pl.BlockSpec((tm, LANE), lambda i: (i, 0))
    return pl.pallas_call(
        body, name="att_merge", grid=(t_len // tm,), in_specs=[row] * 3 + [crow] * 3, out_specs=(row, row, crow),
        out_shape=(jax.ShapeDtypeStruct((t_len, hd), F32), jax.ShapeDtypeStruct((t_len, hd), BF16),
                   jax.ShapeDtypeStruct((t_len, LANE), F32)),
        compiler_params=_params(1))(*os, *ls)


def _att_delta(do, o):
    t_len, hd = o.shape
    tm = _tile(t_len, (256, 128))
    e = HEAD_DIM

    def body(do_ref, o_ref, d_ref):
        d_ref[...] = jnp.zeros_like(d_ref)
        for h in range(ATT_HEADS):
            hs = slice(h * e, (h + 1) * e)
            d_ref[:, h:h + 1] = jnp.sum(do_ref[:, hs].astype(F32) * o_ref[:, hs], axis=1, keepdims=True)

    row = pl.BlockSpec((tm, hd), lambda i: (i, 0))
    return pl.pallas_call(
        body, name="att_delta", grid=(t_len // tm,), in_specs=[row, row],
        out_specs=pl.BlockSpec((tm, LANE), lambda i: (i, 0)),
        out_shape=jax.ShapeDtypeStruct((t_len, LANE), F32), compiler_params=_params(1))(do, o)


def _att_bwd(q_all, kv, delta, lse, do, g):
    d = DILATIONS[g]
    t_len = q_all.shape[0]
    hd = ATT_HEADS * HEAD_DIM
    sub = t_len // d
    nb = sub // ATT_BLOCK
    b = ATT_BLOCK
    e = HEAD_DIM
    scale = e ** -0.5
    slopes = _alibi_slopes()
    qv = q_all.reshape(sub, d * 3 * hd)
    kvv = kv.reshape(sub, d * 2 * hd)
    dlv = delta.reshape(sub, d * LANE)
    lv = lse.reshape(sub, d * LANE)
    dov = do.reshape(sub, d * hd)

    def body(qj_ref, qn_ref, kp_ref, kc_ref, vp_ref, vc_ref, doj_ref, don_ref, dj_ref, dn_ref, lj_ref, ln_ref,
             dq_ref, dk_ref, dv_ref, s_scr, dp_scr, p_scr, ds_scr):
        j = pl.program_id(1)
        valid_c, valid_p0, dist_c, dist_p = _att_masks(d)
        valid = (valid_c, valid_p0 & (j > 0), valid_p0 & (j + 1 < nb))
        dist = (dist_c, dist_p, dist_p)
        for h in range(ATT_HEADS):
            hs = slice(h * e, (h + 1) * e)
            qj, qn = qj_ref[:, hs], qn_ref[:, hs]
            kc, kp = kc_ref[:, hs], kp_ref[:, hs]
            vc, vp = vc_ref[:, hs], vp_ref[:, hs]
            doj, don = doj_ref[:, hs], don_ref[:, hs]
            s_scr[h, 0] = _dot(qj, kc, NT)
            s_scr[h, 1] = _dot(qj, kp, NT)
            s_scr[h, 2] = _dot(qn, kc, NT)
            dp_scr[h, 0] = _dot(doj, vc, NT)
            dp_scr[h, 1] = _dot(doj, vp, NT)
            dp_scr[h, 2] = _dot(don, vc, NT)
        for h in range(ATT_HEADS):
            c = slice(h, h + 1)
            lse_t = (lj_ref[:, c], lj_ref[:, c], ln_ref[:, c])
            dlt_t = (dj_ref[:, c], dj_ref[:, c], dn_ref[:, c])
            for t in range(3):
                s = s_scr[h, t] * scale - slopes[h] * dist[t]
                p = jnp.where(valid[t], jnp.exp(jnp.where(valid[t], s - lse_t[t], NEG)), 0.0)
                p_scr[h, t] = p.astype(BF16)
                ds_scr[h, t] = (p * (dp_scr[h, t] - dlt_t[t])).astype(BF16)
        for h in range(ATT_HEADS):
            hs = slice(h * e, (h + 1) * e)
            dq = _dot(ds_scr[h, 0], kc_ref[:, hs], NN) + _dot(ds_scr[h, 1], kp_ref[:, hs], NN)
            dk = _dot(ds_scr[h, 0], qj_ref[:, hs], TN) + _dot(ds_scr[h, 2], qn_ref[:, hs], TN)
            dv = _dot(p_scr[h, 0], doj_ref[:, hs], TN) + _dot(p_scr[h, 2], don_ref[:, hs], TN)
            dq_ref[:, hs] = (dq * scale).astype(dq_ref.dtype)
            dk_ref[:, hs] = (dk * scale).astype(dk_ref.dtype)
            dv_ref[:, hs] = dv.astype(dv_ref.dtype)

    blk = (b, hd)
    cblk = (b, LANE)
    nxt = lambda i: jnp.minimum(i + 1, nb - 1)
    prv = lambda i: jnp.maximum(i - 1, 0)
    tiles = (ATT_HEADS, 3, b, b)
    dq, dk, dv = pl.pallas_call(
        body, name=f"att_bwd{g}", grid=(d, nb),
        scratch_shapes=[pltpu.VMEM(tiles, F32), pltpu.VMEM(tiles, F32), pltpu.VMEM(tiles, BF16), pltpu.VMEM(tiles, BF16)],
        in_specs=[pl.BlockSpec(blk, lambda r, i: (i, 3 * r + g)),
                  pl.BlockSpec(blk, lambda r, i: (nxt(i), 3 * r + g)),
                  pl.BlockSpec(blk, lambda r, i: (prv(i), 2 * r)),
                  pl.BlockSpec(blk, lambda r, i: (i, 2 * r)),
                  pl.BlockSpec(blk, lambda r, i: (prv(i), 2 * r + 1)),
                  pl.BlockSpec(blk, lambda r, i: (i, 2 * r + 1)),
                  pl.BlockSpec(blk, lambda r, i: (i, r)),
                  pl.BlockSpec(blk, lambda r, i: (nxt(i), r)),
                  pl.BlockSpec(cblk, lambda r, i: (i, r)),
                  pl.BlockSpec(cblk, lambda r, i: (nxt(i), r)),
                  pl.BlockSpec(cblk, lambda r, i: (i, r)),
                  pl.BlockSpec(cblk, lambda r, i: (nxt(i), r))],
        out_specs=(pl.BlockSpec(blk, lambda r, i: (i, r)),) * 3,
        out_shape=(jax.ShapeDtypeStruct((sub, d * hd), BF16),) * 3,
        compiler_params=_params(2))(qv, qv, kvv, kvv, kvv, kvv, dov, dov, dlv, dlv, lv, lv)
    return dq.reshape(t_len, hd), dk.reshape(t_len, hd), dv.reshape(t_len, hd)


def _kv_grad_sum(dks, dvs):
    t_len, hd = dks[0].shape
    tm = _tile(t_len, (256, 128))

    def body(k0, k1, k2, v0, v1, v2, o_ref):
        o_ref[:, :hd] = (k0[...].astype(F32) + k1[...].astype(F32) + k2[...].astype(F32)).astype(o_ref.dtype)
        o_ref[:, hd:] = (v0[...].astype(F32) + v1[...].astype(F32) + v2[...].astype(F32)).astype(o_ref.dtype)

    row = pl.BlockSpec((tm, hd), lambda i: (i, 0))
    return pl.pallas_call(
        body, name="kv_grad_sum", grid=(t_len // tm,), in_specs=[row] * 6,
        out_specs=pl.BlockSpec((tm, 2 * hd), lambda i: (i, 0)),
        out_shape=jax.ShapeDtypeStruct((t_len, 2 * hd), BF16), compiler_params=_params(1))(*dks, *dvs)


HALO = 16
INV_SQRT2 = 1.0 / math.sqrt(2.0)
INV_SQRT2PI = 1.0 / math.sqrt(2.0 * math.pi)


def _conv_taps(g, halo, cw, cb):
    row = lax.broadcasted_iota(jnp.int32, g.shape, 0)
    h1 = halo[HALO - 1:HALO, :]
    h2 = halo[HALO - 2:HALO - 1, :]
    g1 = jnp.where(row == 0, h1, pltpu.roll(g, 1, 0))
    g2 = jnp.where(row == 0, h2, jnp.where(row == 1, h1, pltpu.roll(g, 2, 0)))
    gc = cw[0:1, :] * g2 + cw[1:2, :] * g1 + cw[2:3, :] * g + cb
    return gc, g1, g2


def _glu_specs(t_len, f, tm, tc):
    nj = f // tc
    hb = tm // HALO
    u = pl.BlockSpec((tm, tc), lambda j, i: (i, j))
    g = pl.BlockSpec((tm, tc), lambda j, i: (i, nj + j))
    gh = pl.BlockSpec((HALO, tc), lambda j, i: (jnp.maximum(i * hb - 1, 0), nj + j))
    cw = pl.BlockSpec((8, tc), lambda j, i: (0, j))
    cb = pl.BlockSpec((1, tc), lambda j, i: (0, j))
    return u, g, gh, cw, cb


def _glu_fwd(name, up, cw, cb):
    t_len = up.shape[0]
    f = up.shape[1] // 2
    tm = _tile(t_len, (512, 256, 128))
    tc = _tile(f, (1408, 1024, 512, 256, 128))
    u_s, g_s, gh_s, cw_s, cb_s = _glu_specs(t_len, f, tm, tc)

    def body(u_ref, g_ref, gh_ref, cw_ref, cb_ref, o_ref):
        first = pl.program_id(1) == 0
        halo = jnp.where(first, 0.0, gh_ref[...].astype(F32))
        gc, _, _ = _conv_taps(g_ref[...].astype(F32), halo, cw_ref[...], cb_ref[...])
        gel = 0.5 * gc * (1.0 + lax.erf(gc * INV_SQRT2))
        o_ref[...] = (gel * u_ref[...].astype(F32)).astype(o_ref.dtype)

    return pl.pallas_call(
        body, name=name, grid=(f // tc, t_len // tm), in_specs=[u_s, g_s, gh_s, cw_s, cb_s],
        out_specs=pl.BlockSpec((tm, tc), lambda j, i: (i, j)),
        out_shape=jax.ShapeDtypeStruct((t_len, f), BF16), compiler_params=_params(2))(up, up, up, cw, cb)


def _glu_bwd_a(name, dact, up, cw, cb):
    t_len = up.shape[0]
    f = up.shape[1] // 2
    tm = _tile(t_len, (256, 128))
    tc = _tile(f, (1408, 1024, 512, 256, 128))
    u_s, g_s, gh_s, cw_s, cb_s = _glu_specs(t_len, f, tm, tc)

    def body(da_ref, u_ref, g_ref, gh_ref, cw_ref, cb_ref, du_ref, dgc_ref, w0_ref, w1_ref, w2_ref, b_ref):
        first = pl.program_id(1) == 0
        halo = jnp.where(first, 0.0, gh_ref[...].astype(F32))
        g = g_ref[...].astype(F32)
        gc, g1, g2 = _conv_taps(g, halo, cw_ref[...], cb_ref[...])
        phi = 0.5 * (1.0 + lax.erf(gc * INV_SQRT2))
        dgel = phi + gc * jnp.exp(-0.5 * gc * gc) * INV_SQRT2PI
        da = da_ref[...].astype(F32)
        du_ref[...] = (da * gc * phi).astype(du_ref.dtype)
        dgc = da * u_ref[...].astype(F32) * dgel
        dgc_ref[...] = dgc.astype(dgc_ref.dtype)
        parts = (jnp.sum(dgc * g2, axis=0, keepdims=True), jnp.sum(dgc * g1, axis=0, keepdims=True),
                 jnp.sum(dgc * g, axis=0, keepdims=True), jnp.sum(dgc, axis=0, keepdims=True))
        refs = (w0_ref, w1_ref, w2_ref, b_ref)

        @pl.when(first)
        def _():
            for r, p in zip(refs, parts):
                r[...] = p

        @pl.when(jnp.logical_not(first))
        def _():
            for r, p in zip(refs, parts):
                r[...] += p

    tile = pl.BlockSpec((tm, tc), lambda j, i: (i, j))
    vec = pl.BlockSpec((1, tc), lambda j, i: (0, j))
    vshape = jax.ShapeDtypeStruct((1, f), F32)
    return pl.pallas_call(
        body, name=name, grid=(f // tc, t_len // tm), in_specs=[tile, u_s, g_s, gh_s, cw_s, cb_s],
        out_specs=(tile, tile, vec, vec, vec, vec),
        out_shape=(jax.ShapeDtypeStruct((t_len, f), BF16), jax.ShapeDtypeStruct((t_len, f), F32),
                   vshape, vshape, vshape, vshape),
        compiler_params=_params(2))(dact, up, up, up, cw, cb)


def _glu_bwd_b(name, du, dgc, cw):
    t_len, f = du.shape
    tm = _tile(t_len, (128, 64))
    hb = tm // HALO
    n_i = t_len // tm
    last_hb = t_len // HALO - 1

    def body(du_ref, d_ref, dh_ref, cw_ref, o_ref):
        last = pl.program_id(0) == n_i - 1
        halo = jnp.where(last, 0.0, dh_ref[...].astype(F32))
        dd = d_ref[...].astype(F32)
        row = lax.broadcasted_iota(jnp.int32, dd.shape, 0)
        h0 = halo[0:1, :]
        h1 = halo[1:2, :]
        d1 = jnp.where(row == tm - 1, h0, pltpu.roll(dd, tm - 1, 0))
        d2 = jnp.where(row == tm - 1, h1, jnp.where(row == tm - 2, h0, pltpu.roll(dd, tm - 2, 0)))
        cwv = cw_ref[...]
        dg = cwv[2:3, :] * dd + cwv[1:2, :] * d1 + cwv[0:1, :] * d2
        o_ref[:, :f] = du_ref[...]
        o_ref[:, f:] = dg.astype(o_ref.dtype)

    row_s = pl.BlockSpec((tm, f), lambda i: (i, 0))
    return pl.pallas_call(
        body, name=name, grid=(n_i,),
        in_specs=[row_s, row_s, pl.BlockSpec((HALO, f), lambda i: (jnp.minimum((i + 1) * hb, last_hb), 0)),
                  pl.BlockSpec((8, f), lambda i: (0, 0))],
        out_specs=pl.BlockSpec((tm, 2 * f), lambda i: (i, 0)),
        out_shape=jax.ShapeDtypeStruct((t_len, 2 * f), BF16), compiler_params=_params(1))(du, dgc, dgc, cw)


def _adamw(name, w, g, m, v):
    rows, cols = w.shape
    tr = _tile(rows, (256, 128, 64, 32, 16, 8))
    c1 = 1.0 / (1.0 - ADAM_B1 ** ADAM_STEP)
    c2 = 1.0 / (1.0 - ADAM_B2 ** ADAM_STEP)

    def body(w_ref, g_ref, m_ref, v_ref, d_ref, nm_ref, nv_ref):
        gv = g_ref[...]
        nm = ADAM_B1 * m_ref[...] + (1.0 - ADAM_B1) * gv
        nv = ADAM_B2 * v_ref[...] + (1.0 - ADAM_B2) * (gv * gv)
        nm_ref[...] = nm
        nv_ref[...] = nv
        d_ref[...] = -ADAM_LR * ((nm * c1) / (jnp.sqrt(nv * c2) + ADAM_EPS) + ADAM_WD * w_ref[...])

    blk = pl.BlockSpec((tr, cols), lambda i: (i, 0))
    shp = jax.ShapeDtypeStruct((rows, cols), F32)
    return pl.pallas_call(body, name=name, grid=(rows // tr,), in_specs=[blk] * 4, out_specs=(blk,) * 3,
                          out_shape=(shp,) * 3, compiler_params=_params(1))(w, g, m, v)


class _NoComm:
    def __init__(self):
        self.grads = {}

    def need(self, group, ws, after):
        return ws

    def reduce(self, group, grads, carry):
        self.grads.update(grads)
        return carry


def _local_step(x, target, ws, norms, small, hooks):
    lay = _layout()

    w_main, w_a = _unpack_gin(ws["gin"])
    hn0 = _rms_fwd("rms_attn0", x, norms["attn0"])
    proj = _mm_plain("gla_proj", hn0, w_main, NN, F32)
    a = _mm_plain("gla_proj_a", hn0, w_a, NN, BF16)
    ga, cum = _gla_gate_fwd(a, small["w_a2p"], small["b_a2"])
    o_gla, states = _gla_fwd(proj, cum)
    gated = _gla_out_fwd(o_gla, proj, small["head_norm"])
    h1 = _mm_act_wr("gla_out", gated, ws["gout"], lay["gout"], add=x)

    def ffn_fwd(l, h):
        hn = _rms_fwd(f"rms_ffn{l}", h, norms[f"ffn{l}"])
        up = _mm_act_wc(f"ffn_up{l}", hn, ws[f"up{l}"], lay[f"up{l}"], F32)
        act = _glu_fwd(f"glu_fwd{l}", up, small["conv_w"][l], small["conv_b"][l])
        return hn, up, act, _mm_act_wr(f"ffn_down{l}", act, ws[f"down{l}"], lay[f"down{l}"], add=h)

    ws = hooks.need("B", ws, h1)
    hnf0, up0, act0, h2 = ffn_fwd(0, h1)

    ws = hooks.need("C", ws, h2)
    kvn = _rms_fwd("rms_kv", h2, norms["kv"])
    kv = _mm_act_wc("kv_proj", kvn, ws["wkv"], lay["wkv"], BF16)
    hn1 = _rms_fwd("rms_attn1", h2, norms["attn1"])
    q_all = _mm_act_wc("q_proj", hn1, ws["wq"], lay["wq"], BF16)
    branch = [_att_fwd(q_all, kv, g) for g in range(3)]
    o_att, o_att_b, lse = _att_merge([br[0] for br in branch], [br[1] for br in branch])
    h3 = _mm_act_wr("att_out", o_att_b, ws["dout"], lay["dout"], add=h2)
    hnf1, up1, act1, h4 = ffn_fwd(1, h3)

    dh4, d_final, loss = _loss_head(h4, norms["final"], target)

    sm = {"final": d_final}

    def ffn_bwd(l, dh, h, hn, up, act):
        big = {}
        dact = _mm_dact_wrT(f"ffn_down_dx{l}", dh, ws[f"down{l}"], lay[f"down{l}"])
        big[f"down{l}"] = _mm_grad_wr(f"ffn_down_dw{l}", act, dh, lay[f"down{l}"])
        du, dgc, w0, w1, w2, db = _glu_bwd_a(f"glu_bwd_a{l}", dact, up, small["conv_w"][l], small["conv_b"][l])
        sm[f"conv_w{l}"] = (w0, w1, w2)
        sm[f"conv_b{l}"] = db
        dup = _glu_bwd_b(f"glu_bwd_b{l}", du, dgc, small["conv_w"][l])
        dhn = _mm_dact_wcT(f"ffn_up_dx{l}", dup, ws[f"up{l}"], lay[f"up{l}"])
        big[f"up{l}"] = _mm_grad_wc(f"ffn_up_dw{l}", hn, dup, lay[f"up{l}"])
        dh_in, sm[f"ffn{l}"] = _rms_bwd(f"rms_ffn_bwd{l}", dhn, h, norms[f"ffn{l}"], dh)
        return hooks.reduce(f"ffn{l}", big, dh_in)

    dh3 = ffn_bwd(1, dh4, h3, hnf1, up1, act1)

    big = {}
    do_att = _mm_dact_wrT("att_out_dx", dh3, ws["dout"], lay["dout"])
    big["dout"] = _mm_grad_wr("att_out_dw", o_att_b, dh3, lay["dout"])
    delta = _att_delta(do_att, o_att)
    bw = [_att_bwd(q_all, kv, delta, lse, do_att, g) for g in range(3)]
    dq_all = jnp.concatenate([t[0] for t in bw], axis=1)
    dhn1 = _mm_dact_wcT("q_proj_dx", dq_all, ws["wq"], lay["wq"])
    big["wq"] = _mm_grad_wc("q_proj_dw", hn1, dq_all, lay["wq"])
    dh2, sm["attn1"] = _rms_bwd("rms_attn1_bwd", dhn1, h2, norms["attn1"], dh3)
    dkv = _kv_grad_sum([t[1] for t in bw], [t[2] for t in bw])
    dkvn = _mm_dact_wcT("kv_proj_dx", dkv, ws["wkv"], lay["wkv"])
    big["wkv"] = _mm_grad_wc("kv_proj_dw", kvn, dkv, lay["wkv"])
    dh2, sm["kv"] = _rms_bwd("rms_kv_bwd", dkvn, h2, norms["kv"], dh2)
    dh2 = hooks.reduce("att", big, dh2)

    dh1 = ffn_bwd(0, dh2, h1, hnf0, up0, act0)

    big = {}
    dgated = _mm_dact_wrT("gla_out_dx", dh1, ws["gout"], lay["gout"])
    big["gout"] = _mm_grad_wr("gla_out_dw", gated, dh1, lay["gout"])
    do_gla, dr, sm["head_norm"] = _gla_out_bwd(dgated, o_gla, proj, small["head_norm"])
    dq, dk, dv, dcum = _gla_bwd(proj, cum, states, do_gla)
    da, sm["w_a2p"], sm["b_a2"] = _gla_gate_bwd(dcum, ga, a, small["w_a2p"])
    dproj = jnp.concatenate([dq, dk, dv, dr], axis=1)
    dhn0 = _mm_plain("gla_proj_dx", dproj, w_main, NT, F32)
    dhn0 = _mm_plain("gla_proj_a_dx", da, w_a, NT, F32, add=dhn0)
    gin_main = _mm_plain("gla_proj_dw", hn0, dproj, TN, BF16)
    gin_a = _mm_plain("gla_proj_a_dw", hn0, da, TN, BF16)
    big["gin"] = _pack_gin_grad(gin_main, gin_a)
    grad_x, sm["attn0"] = _rms_bwd("rms_attn0_bwd", dhn0, x, norms["attn0"], dh1)
    return loss, grad_x, sm, big


def _pack_weights(chip, names, gla_w_in, gla_w_out, w_kv, dsa_w_q, dsa_w_out, ffn_w_up, ffn_w_down):
    gin = gla_w_in[0]
    gin = jnp.pad(gin, ((0, 0), (0, _roundup(gin.shape[1], LANE) - gin.shape[1])))
    shards = {"gin": gin, "gout": gla_w_out[0], "up0": ffn_w_up[0], "up1": ffn_w_up[1], "down0": ffn_w_down[0],
              "down1": ffn_w_down[1], "wq": dsa_w_q[0], "wkv": w_kv, "dout": dsa_w_out[0]}
    out = {}
    for name in names:
        w = shards[name]
        buf = jnp.zeros((N_CHIPS,) + w.shape, BF16)
        out[name] = lax.dynamic_update_slice(buf, w.astype(BF16)[None], (chip, 0, 0))
    return out


def _unpack_gin(w_gin):
    w = _layout()["gin"][1]
    full = jnp.transpose(w_gin[:, :, :w], (1, 0, 2)).reshape(D_MODEL, N_CHIPS * w)
    n_main = 2 * GLA_KEY_DIM + 2 * GLA_VAL_DIM
    w_a = jnp.pad(full[:, n_main:], ((0, 0), (0, A_PAD - GATE_RANK)))
    return full[:, :n_main], w_a


def _pack_gin_grad(gin_main, gin_a):
    w = _layout()["gin"][1]
    gin = jnp.concatenate([gin_main, gin_a[:, :GATE_RANK]], axis=1)
    gin = jnp.transpose(gin.reshape(D_MODEL, N_CHIPS, w), (1, 0, 2))
    return jnp.pad(gin, ((0, 0), (0, 0), (0, _roundup(w, LANE) - w)))


def _small_params(attn_norm, ffn_norm, kv_norm, final_norm, conv_b, w_a2, b_a2, head_norm, conv_w):
    norms = {"attn0": attn_norm[0:1], "attn1": attn_norm[1:2], "ffn0": ffn_norm[0:1], "ffn1": ffn_norm[1:2],
             "kv": kv_norm[None, :], "final": final_norm[None, :]}
    small = {"w_a2p": jnp.pad(w_a2, ((0, A_PAD - GATE_RANK), (0, 0))), "b_a2": b_a2[None, :],
             "head_norm": head_norm[None, :], "conv_w": jnp.pad(conv_w, ((0, 0), (0, 8 - conv_w.shape[1]), (0, 0))),
             "conv_b": conv_b[:, None, :]}
    return norms, small


ANY = pl.BlockSpec(memory_space=pl.ANY)


def _place():
    return lax.axis_index("x"), lax.axis_index("y"), lax.axis_index("c")


def _other_chips(x, y):
    return [(1 - x, y), (x, 1 - y), (1 - x, 1 - y)]


def _rcopy(src, dst, ssem, rsem, dev):
    return pltpu.make_async_remote_copy(src_ref=src, dst_ref=dst, send_sem=ssem, recv_sem=rsem, device_id=dev,
                                        device_id_type=MESH)


def _gather_weights(name, arrs):
    n = len(arrs)

    def body(*refs):
        ins, outs = refs[:n], refs[n:2 * n]
        send, recv, fsend, frecv = refs[2 * n:]
        x, y, c = _place()
        me = 2 * x + y
        chips = _other_chips(x, y)
        sib = (x, y, 1 - c)
        first, fwd = [], []
        for a in range(n):
            h = ins[a].shape[1] // 2
            mine = pl.ds(c * h, h)
            for j, (px, py) in enumerate(chips):
                cp = _rcopy(ins[a].at[me, mine], outs[a].at[me, mine], send.at[3 * a + j], recv.at[3 * a + j], (px, py, c))
                cp.start()
                first.append(cp)
        for a in range(n):
            h = ins[a].shape[1] // 2
            mine = pl.ds(c * h, h)
            for j, (px, py) in enumerate(chips):
                landed = outs[a].at[2 * px + py, mine]
                _rcopy(landed, landed, send.at[3 * a + j], recv.at[3 * a + j], (px, py, c)).wait_recv()
                cp = _rcopy(landed, landed, fsend.at[3 * a + j], frecv.at[3 * a + j], sib)
                cp.start()
                fwd.append(cp)
        for a in range(n):
            h = ins[a].shape[1] // 2
            theirs = pl.ds((1 - c) * h, h)
            for j, (px, py) in enumerate(chips):
                got = outs[a].at[2 * px + py, theirs]
                _rcopy(got, got, fsend.at[3 * a + j], frecv.at[3 * a + j], sib).wait_recv()
        for cp in first + fwd:
            cp.wait_send()

    return pl.pallas_call(
        body, name=name, in_specs=[ANY] * n, out_specs=[ANY] * n,
        out_shape=[jax.ShapeDtypeStruct(a.shape, a.dtype) for a in arrs],
        input_output_aliases={a: a for a in range(n)},
        scratch_shapes=[pltpu.SemaphoreType.DMA((3 * n,)), pltpu.SemaphoreType.DMA((3 * n,)),
                        pltpu.SemaphoreType.DMA((3 * n,)), pltpu.SemaphoreType.DMA((3 * n,))])(*arrs)


def _swap_halves(name, arrs):
    n = len(arrs)

    def body(*refs):
        ins, outs = refs[:n], refs[n:2 * n]
        send, recv = refs[2 * n:]
        x, y, c = _place()
        cps = []
        for a in range(n):
            h = ins[a].shape[1] // 2
            cp = _rcopy(ins[a].at[:, pl.ds((1 - c) * h, h)], outs[a], send.at[a], recv.at[a], (x, y, 1 - c))
            cp.start()
            cps.append(cp)
        for cp in cps:
            cp.wait()

    return pl.pallas_call(
        body, name=name, in_specs=[ANY] * n, out_specs=[ANY] * n,
        out_shape=[jax.ShapeDtypeStruct((a.shape[0], a.shape[1] // 2, a.shape[2]), a.dtype) for a in arrs],
        scratch_shapes=[pltpu.SemaphoreType.DMA((n,)), pltpu.SemaphoreType.DMA((n,))])(*arrs)


SEM = pl.BlockSpec(memory_space=pltpu.SEMAPHORE)
EFFECT = pltpu.SideEffectType.DATAFLOW_SIDE_EFFECTING


def _shapes(arrs):
    return [jax.ShapeDtypeStruct(a.shape, a.dtype) for a in arrs]


def _gather_start(name, thru, arrs):
    n, nt = len(arrs), len(thru)

    def body(*refs):
        ins = refs[nt:nt + n]
        send, recv = refs[nt + n], refs[nt + n + 1]
        outs = refs[2 * nt + n + 2:]
        x, y, c = _place()
        me = 2 * x + y
        for a in range(n):
            h = ins[a].shape[1] // 2
            mine = pl.ds(c * h, h)
            for j, (px, py) in enumerate(_other_chips(x, y)):
                _rcopy(ins[a].at[me, mine], outs[a].at[me, mine], send.at[3 * a + j], recv.at[3 * a + j], (px, py, c)).start()

    res = pl.pallas_call(
        body, name=name, in_specs=[ANY] * (nt + n), out_specs=[SEM, SEM] + [ANY] * (nt + n),
        out_shape=[pltpu.SemaphoreType.DMA((3 * n,)), pltpu.SemaphoreType.DMA((3 * n,))] + _shapes(thru) + _shapes(arrs),
        input_output_aliases={i: 2 + i for i in range(nt + n)},
        compiler_params=pltpu.CompilerParams(has_side_effects=EFFECT))(*thru, *arrs)
    return res[0], res[1], res[2:2 + nt], res[2 + nt:]


def _gather_wait(name, send, recv, arrs, after):
    n = len(arrs)

    def body(*refs):
        ins = refs[:n]
        send_ref, recv_ref = refs[n], refs[n + 1]
        x, y, c = _place()
        me = 2 * x + y
        for a in range(n):
            h = ins[a].shape[1] // 2
            mine = pl.ds(c * h, h)
            for j, (px, py) in enumerate(_other_chips(x, y)):
                sent = ins[a].at[me, mine]
                landed = ins[a].at[2 * px + py, mine]
                cp = _rcopy(sent, landed, send_ref.at[3 * a + j], recv_ref.at[3 * a + j], (px, py, c))
                cp.wait_send()
                cp.wait_recv()

    return pl.pallas_call(
        body, name=name, in_specs=[ANY] * n + [SEM, SEM, ANY], out_specs=[ANY] * n, out_shape=_shapes(arrs),
        input_output_aliases={a: a for a in range(n)},
        compiler_params=pltpu.CompilerParams(has_side_effects=EFFECT))(*arrs, send, recv, after)


def _forward_halves(name, arrs):
    n = len(arrs)

    def body(*refs):
        ins, outs = refs[:n], refs[n:2 * n]
        send, recv = refs[2 * n:]
        x, y, c = _place()
        sib = (x, y, 1 - c)
        chips = _other_chips(x, y)
        cps = []
        for a in range(n):
            h = ins[a].shape[1] // 2
            mine = pl.ds(c * h, h)
            for j, (px, py) in enumerate(chips):
                cp = _rcopy(ins[a].at[2 * px + py, mine], outs[a].at[2 * px + py, mine], send.at[3 * a + j],
                            recv.at[3 * a + j], sib)
                cp.start()
                cps.append(cp)
        for a in range(n):
            h = ins[a].shape[1] // 2
            theirs = pl.ds((1 - c) * h, h)
            for j, (px, py) in enumerate(chips):
                got = outs[a].at[2 * px + py, theirs]
                _rcopy(got, got, send.at[3 * a + j], recv.at[3 * a + j], sib).wait_recv()
        for cp in cps:
            cp.wait_send()

    return pl.pallas_call(
        body, name=name, in_specs=[ANY] * n, out_specs=[ANY] * n, out_shape=_shapes(arrs),
        input_output_aliases={a: a for a in range(n)},
        scratch_shapes=[pltpu.SemaphoreType.DMA((3 * n,)), pltpu.SemaphoreType.DMA((3 * n,))])(*arrs)


def _scatter_start(name, thru, arrs):
    n, nt = len(arrs), len(thru)
    landing = [jnp.zeros_like(a) for a in arrs]

    def body(*refs):
        ins = refs[nt:nt + n]
        send, recv = refs[nt + 2 * n], refs[nt + 2 * n + 1]
        outs = refs[2 * nt + 3 * n + 2:]
        x, y, c = _place()
        me = 2 * x + y
        for a in range(n):
            for j, (px, py) in enumerate(_other_chips(x, y)):
                _rcopy(ins[a].at[2 * px + py], outs[a].at[me], send.at[3 * a + j], recv.at[3 * a + j], (px, py, c)).start()

    res = pl.pallas_call(
        body, name=name, in_specs=[ANY] * (nt + 2 * n), out_specs=[SEM, SEM] + [ANY] * (nt + 2 * n),
        out_shape=[pltpu.SemaphoreType.DMA((3 * n,)), pltpu.SemaphoreType.DMA((3 * n,))] + _shapes(thru) + _shapes(arrs)
        + _shapes(landing),
        input_output_aliases={i: 2 + i for i in range(nt + 2 * n)},
        compiler_params=pltpu.CompilerParams(has_side_effects=EFFECT))(*thru, *arrs, *landing)
    return res[0], res[1], res[2:2 + nt], res[2 + nt:2 + nt + n], res[2 + nt + n:]


def _scatter_wait(name, send, recv, arrs, landing, after):
    n = len(arrs)

    def body(*refs):
        ins, land = refs[:n], refs[n:2 * n]
        send_ref, recv_ref = refs[2 * n], refs[2 * n + 1]
        x, y, c = _place()
        for a in range(n):
            for j, (px, py) in enumerate(_other_chips(x, y)):
                cp = _rcopy(ins[a].at[2 * px + py], land[a].at[2 * px + py], send_ref.at[3 * a + j], recv_ref.at[3 * a + j],
                            (px, py, c))
                cp.wait_send()
                cp.wait_recv()

    res = pl.pallas_call(
        body, name=name, in_specs=[ANY] * (2 * n) + [SEM, SEM, ANY], out_specs=[ANY] * (2 * n),
        out_shape=_shapes(arrs) + _shapes(landing), input_output_aliases={i: i for i in range(2 * n)},
        compiler_params=pltpu.CompilerParams(has_side_effects=EFFECT))(*arrs, *landing, send, recv, after)
    return res[:n], res[n:]


def _join_halves(name, arrs):
    n = len(arrs)

    def body(*refs):
        ins, outs = refs[:n], refs[n:2 * n]
        send, recv = refs[2 * n:]
        x, y, c = _place()
        cps = []
        for a in range(n):
            h = ins[a].shape[0] // 2
            mine = pl.ds(c * h, h)
            cp = _rcopy(ins[a].at[mine], outs[a].at[mine], send.at[a], recv.at[a], (x, y, 1 - c))
            cp.start()
            cps.append(cp)
        for a in range(n):
            h = ins[a].shape[0] // 2
            got = outs[a].at[pl.ds((1 - c) * h, h)]
            _rcopy(got, got, send.at[a], recv.at[a], (x, y, 1 - c)).wait_recv()
        for cp in cps:
            cp.wait_send()

    return pl.pallas_call(
        body, name=name, in_specs=[ANY] * n, out_specs=[ANY] * n,
        out_shape=[jax.ShapeDtypeStruct(a.shape, a.dtype) for a in arrs],
        input_output_aliases={a: a for a in range(n)},
        scratch_shapes=[pltpu.SemaphoreType.DMA((n,)), pltpu.SemaphoreType.DMA((n,))])(*arrs)


def _allgather8(name, xs, reduce):
    m_per, n = xs.shape

    def body(x_ref, out_ref, *rest):
        if reduce:
            sum_ref, send, recv, lsem = rest
        else:
            send, recv, lsem = rest
        x, y, c = _place()
        me, sib = (x, y, c), (x, y, 1 - c)
        chips = _other_chips(x, y)

        def rows(px, py, pc):
            return out_ref.at[pl.ds((4 * px + 2 * py + pc) * m_per, m_per), :]

        def copy(k, block, to, src=None):
            return _rcopy(rows(*block) if src is None else src, rows(*block), send.at[k], recv.at[k], to)

        mine = pltpu.make_async_copy(x_ref, rows(*me), lsem)
        mine.start()
        first = [copy(0, me, sib, src=x_ref)]
        first += [copy(1 + j, me, (*chip, c), src=x_ref) for j, chip in enumerate(chips)]
        for cp in first:
            cp.start()
        passed = [copy(4 + j, (*chip, c), sib) for j, chip in enumerate(chips)]
        for j, chip in enumerate(chips):
            copy(1 + j, (*chip, c), me).wait_recv()
            passed[j].start()
        copy(0, sib, me).wait_recv()
        for j, chip in enumerate(chips):
            copy(4 + j, (*chip, 1 - c), me).wait_recv()
        for cp in first + passed:
            cp.wait_send()
        mine.wait()
        if reduce:
            acc = out_ref[pl.ds(0, m_per), :]
            for dev in range(1, N_DEV):
                acc = acc + out_ref[pl.ds(dev * m_per, m_per), :]
            sum_ref[...] = acc

    vm = pl.BlockSpec(memory_space=pltpu.VMEM)
    out_shape = [jax.ShapeDtypeStruct((N_DEV * m_per, n), xs.dtype)]
    if reduce:
        out_shape.append(jax.ShapeDtypeStruct((m_per, n), xs.dtype))
    return pl.pallas_call(
        body, name=name, in_specs=[vm], out_specs=[vm] * len(out_shape), out_shape=out_shape,
        scratch_shapes=[pltpu.SemaphoreType.DMA((7,)), pltpu.SemaphoreType.DMA((7,)), pltpu.SemaphoreType.DMA],
        compiler_params=pltpu.CompilerParams(vmem_limit_bytes=VMEM_LIMIT))(xs)


def _add_my_half(name, a, rb, c_arr):
    s, h, cols = rb.shape
    tr = _tile(h, (512, 352, 256, 128, 64, 32, 16))
    nt = h // tr

    def body(c_ref, a_ref, b_ref, o_ref):
        o_ref[...] = (a_ref[...].astype(F32) + b_ref[...].astype(F32)).astype(o_ref.dtype)

    return pl.pallas_call(
        body, name=name,
        grid_spec=pltpu.PrefetchScalarGridSpec(
            num_scalar_prefetch=1, grid=(s, nt),
            in_specs=[pl.BlockSpec((None, tr, cols), lambda k, i, c: (k, c[0] * nt + i, 0)),
                      pl.BlockSpec((None, tr, cols), lambda k, i, c: (k, i, 0))],
            out_specs=pl.BlockSpec((None, tr, cols), lambda k, i, c: (k, i, 0))),
        out_shape=jax.ShapeDtypeStruct(rb.shape, BF16), compiler_params=_params(2))(c_arr, a, rb)


def _sum_chips(name, own, q, place):
    s, h, cols = q.shape
    tr = _tile(h, (512, 352, 256, 128, 64, 32, 16))
    nt = h // tr

    def body(p_ref, own_ref, q_ref, o_ref):
        chip = p_ref[0]
        acc = jnp.where(chip == 0, own_ref[0], q_ref[0]).astype(F32)
        for j in range(1, s):
            acc = acc + jnp.where(chip == j, own_ref[j], q_ref[j]).astype(F32)
        o_ref[...] = acc

    blk = pl.BlockSpec((s, tr, cols), lambda i, p: (0, i, 0))
    return pl.pallas_call(
        body, name=name,
        grid_spec=pltpu.PrefetchScalarGridSpec(
            num_scalar_prefetch=1, grid=(nt,), in_specs=[blk, blk],
            out_specs=pl.BlockSpec((tr, cols), lambda i, p: (p[1] * nt + i, 0))),
        out_shape=jax.ShapeDtypeStruct((2 * h, cols), F32), compiler_params=_params(1))(place, own, q)


def _pack_rows(parts):
    rows = []
    for p in parts:
        flat = p.reshape(-1).astype(F32)
        n = _roundup(flat.shape[0], 8 * LANE)
        rows.append(jnp.pad(flat, (0, n - flat.shape[0])).reshape(-1, LANE))
    return jnp.concatenate(rows, axis=0)


def _unpack_rows(buf, shapes):
    out, r = [], 0
    for shp in shapes:
        size = math.prod(shp)
        nr = _roundup(size, 8 * LANE) // LANE
        out.append(buf[r:r + nr].reshape(-1)[:size].reshape(shp))
        r += nr
    return out


def kernel(x, attn_norm, gla_w_in, gla_w_a2, gla_b_a2, gla_head_norm, gla_w_out, kv_norm, w_kv, dsa_w_q, dsa_w_out, ffn_norm, ffn_w_up, ffn_conv_w, ffn_conv_b, ffn_w_down, final_norm, loss_target, m_attn_norm, m_gla_w_in, m_gla_w_a2, m_gla_b_a2, m_gla_head_norm, m_gla_w_out, m_kv_norm, m_w_kv, m_dsa_w_q, m_dsa_w_out, m_ffn_norm, m_ffn_w_up, m_ffn_conv_w, m_ffn_conv_b, m_ffn_w_down, m_final_norm, v_attn_norm, v_gla_w_in, v_gla_w_a2, v_gla_b_a2, v_gla_head_norm, v_gla_w_out, v_kv_norm, v_w_kv, v_dsa_w_q, v_dsa_w_out, v_ffn_norm, v_ffn_w_up, v_ffn_conv_w, v_ffn_conv_b, v_ffn_w_down, v_final_norm):
    lay = _layout()
    d, f = D_MODEL, D_FF
    cx, cy, cc = _place()
    chip = 2 * cx + cy
    c_arr = jnp.reshape(cc, (1,)).astype(jnp.int32)
    place = jnp.stack([chip, cc]).astype(jnp.int32)

    groups = {"A": ("gin", "gout", "small"), "B": ("up0", "down0"), "C": ("up1", "down1", "wq", "wkv", "dout")}
    big_shards = (gla_w_in, gla_w_out, w_kv, dsa_w_q, dsa_w_out, ffn_w_up, ffn_w_down)
    ws = _pack_weights(chip, groups["A"][:2], *big_shards)
    sharded_small = [gla_w_a2[0], gla_b_a2[0], gla_head_norm[0], ffn_conv_w]
    packed = _pack_rows(sharded_small)
    packed = jnp.pad(packed, ((0, _roundup(packed.shape[0], 16) - packed.shape[0]), (0, 0)))
    ws["small"] = lax.dynamic_update_slice(jnp.zeros((N_CHIPS,) + packed.shape, F32), packed[None], (chip, 0, 0))
    send, recv, _, arrs = _gather_start("gather_a_start", [], [ws[k] for k in groups["A"]])
    ws.update(_pack_weights(chip, groups["B"] + groups["C"], *big_shards))
    arrs = _gather_wait("gather_a_wait", send, recv, arrs, ws["up0"])
    ws.update(zip(groups["A"], _forward_halves("forward_a", arrs)))
    in_flight = {}
    thru = [ws[k] for k in groups["A"]]
    for grp in ("B", "C"):
        send, recv, thru, arrs = _gather_start(f"gather_{grp.lower()}_start", thru, [ws[k] for k in groups[grp]])
        ws.update(zip(groups[grp], arrs))
        in_flight[grp] = (send, recv)
    ws.update(zip(groups["A"], thru))
    pending = []

    class _Comm:
        def need(self, grp, ws, after):
            send, recv = in_flight[grp]
            arrs = _gather_wait(f"gather_{grp.lower()}_wait", send, recv, [ws[k] for k in groups[grp]], after)
            arrs = _forward_halves(f"forward_{grp.lower()}", arrs)
            return {**ws, **dict(zip(groups[grp], arrs))}

        def reduce(self, grp, grads, carry):
            names = list(grads)
            parts = [grads[k] for k in names]
            theirs = _swap_halves(f"swap_{grp}", parts)
            sums = [_add_my_half(f"add_half_{k}", a, b, c_arr) for k, a, b in zip(names, parts, theirs)]
            send, recv, thru, sums, landing = _scatter_start(f"scatter_{grp}_start", [carry], sums)
            pending.append((grp, names, send, recv, sums, landing))
            return thru[0]

    shards = [_unpack_rows(ws["small"][s], [p.shape for p in sharded_small]) for s in range(N_CHIPS)]
    w_a2, b_a2, head_norm, conv_w = [jnp.concatenate([shards[s][k] for s in range(N_CHIPS)], axis=-1) for k in range(4)]
    norms, small = _small_params(attn_norm, ffn_norm, kv_norm, final_norm, ffn_conv_b, w_a2, b_a2, head_norm, conv_w)

    comm = _Comm()
    loss_blk, grad_x, sm, last_big = _local_step(x[0], loss_target[0], ws, norms, small, comm)

    small_parts = [loss_blk, jnp.concatenate([sm["attn0"], sm["attn1"]]), jnp.concatenate([sm["ffn0"], sm["ffn1"]]),
                   sm["kv"], sm["final"], jnp.concatenate([sm["conv_b0"], sm["conv_b1"]]),
                   sm["w_a2p"][:GATE_RANK], sm["b_a2"], sm["head_norm"],
                   jnp.stack([jnp.concatenate(sm["conv_w0"]), jnp.concatenate(sm["conv_w1"])])]
    small_shapes = [(8, LANE), (2, d), (2, d), (d,), (d,), (2, f), (GATE_RANK, GLA_KEY_DIM), (GLA_KEY_DIM,),
                    (GLA_VAL_DIM // GLA_HEADS,), (2, 3, f)]
    _, reduced = _allgather8("reduce_small", _pack_rows(small_parts), True)
    reduced = comm.reduce("gla", last_big, reduced)

    gin_w = lay["gin"][1]
    full = {}
    after = reduced
    for grp, names, send, recv, sums, landing in pending:
        sums, landing = _scatter_wait(f"scatter_{grp}_wait", send, recv, sums, landing, after)
        halves = [_sum_chips(f"sum_chips_{k}", s, q, place) for k, s, q in zip(names, sums, landing)]
        joined = _join_halves(f"join_{grp}", halves)
        full.update(zip(names, joined))
        after = joined[0]
    loss_r, g_attn, g_ffn, g_kv, g_final, g_cb, g_a2, g_ba2, g_hn, g_cw = _unpack_rows(reduced, small_shapes)
    loss = loss_r[0, 0]

    def mine(g, axis):
        w = g.shape[axis] // N_CHIPS
        return lax.dynamic_slice_in_dim(g, chip * w, w, axis)

    grads = {
        "attn_norm": g_attn, "gla_w_in": full["gin"][None, :, :gin_w], "gla_w_a2": mine(g_a2, 1)[None],
        "gla_b_a2": mine(g_ba2, 0)[None], "gla_head_norm": mine(g_hn, 0)[None], "gla_w_out": full["gout"][None],
        "kv_norm": g_kv, "w_kv": full["wkv"], "dsa_w_q": full["wq"][None], "dsa_w_out": full["dout"][None],
        "ffn_norm": g_ffn, "ffn_w_up": jnp.stack([full["up0"], full["up1"]]), "ffn_conv_w": mine(g_cw, 2),
        "ffn_conv_b": g_cb, "ffn_w_down": jnp.stack([full["down0"], full["down1"]]), "final_norm": g_final,
    }
    weights = {"attn_norm": (attn_norm, m_attn_norm, v_attn_norm), "gla_w_in": (gla_w_in, m_gla_w_in, v_gla_w_in),
               "gla_w_a2": (gla_w_a2, m_gla_w_a2, v_gla_w_a2), "gla_b_a2": (gla_b_a2, m_gla_b_a2, v_gla_b_a2),
               "gla_head_norm": (gla_head_norm, m_gla_head_norm, v_gla_head_norm),
               "gla_w_out": (gla_w_out, m_gla_w_out, v_gla_w_out), "kv_norm": (kv_norm, m_kv_norm, v_kv_norm),
               "w_kv": (w_kv, m_w_kv, v_w_kv), "dsa_w_q": (dsa_w_q, m_dsa_w_q, v_dsa_w_q),
               "dsa_w_out": (dsa_w_out, m_dsa_w_out, v_dsa_w_out), "ffn_norm": (ffn_norm, m_ffn_norm, v_ffn_norm),
               "ffn_w_up": (ffn_w_up, m_ffn_w_up, v_ffn_w_up), "ffn_conv_w": (ffn_conv_w, m_ffn_conv_w, v_ffn_conv_w),
               "ffn_conv_b": (ffn_conv_b, m_ffn_conv_b, v_ffn_conv_b),
               "ffn_w_down": (ffn_w_down, m_ffn_w_down, v_ffn_w_down), "final_norm": (final_norm, m_final_norm, v_final_norm)}
    order = list(weights)
    big_names = ("gla_w_in", "gla_w_out", "w_kv", "dsa_w_q", "dsa_w_out", "ffn_w_up", "ffn_w_down")
    delta, new_m, new_v = {}, {}, {}
    for k in big_names:
        w, m, v = weights[k]
        cols = w.shape[-1]
        res = _adamw(f"adamw_{k}", w.reshape(-1, cols), grads[k].reshape(-1, cols), m.reshape(-1, cols), v.reshape(-1, cols))
        delta[k], new_m[k], new_v[k] = [r.reshape(w.shape) for r in res]
    small_names = [k for k in order if k not in big_names]
    packed = [_pack_rows([src[k] for k in small_names])
              for src in ({k: weights[k][0] for k in small_names}, grads, {k: weights[k][1] for k in small_names},
                          {k: weights[k][2] for k in small_names})]
    res = _adamw("adamw_small", *packed)
    shapes = [weights[k][0].shape for k in small_names]
    for dst, buf in zip((delta, new_m, new_v), res):
        for k, val in zip(small_names, _unpack_rows(buf, shapes)):
            dst[k] = val
    return (loss, grad_x[None], *[grads[k] for k in order], *[delta[k] for k in order], *[new_m[k] for k in order],
            *[new_v[k] for k in order])
```

```python
import math

import jax
import jax.numpy as jnp
from jax import lax
from jax.experimental import pallas as pl
from jax.experimental.pallas import tpu as pltpu

F32 = jnp.float32
BF16 = jnp.bfloat16

D_MODEL = 2048
SEQ = 4096
GLA_HEADS = 4
GLA_KEY_DIM = D_MODEL // 2
GLA_VAL_DIM = D_MODEL
GATE_RANK = 16
GATE_NORMALIZER = 16.0
GLA_CHUNK = 64
ATT_HEADS = 16
HEAD_DIM = 128
WINDOWS = (128, 512, 2048)
DILATIONS = (1, 4, 16)
ATT_BLOCK = 128
D_FF = 5632
EPS = 1e-6
ADAM_LR = 0.001
ADAM_B1 = 0.9
ADAM_B2 = 0.999
ADAM_EPS = 1e-08
ADAM_WD = 0.01
ADAM_STEP = 10

N_CHIPS = 4
N_DEV = 8
LANE = 128
A_PAD = 128
VMEM_LIMIT = 56 * 1024 * 1024
MAX_K_TILE = 2816
NEG = -1e30
MESH = pl.DeviceIdType.MESH

NN = (((1,), (0,)), ((), ()))
NT = (((1,), (1,)), ((), ()))
TN = (((0,), (0,)), ((), ()))


def _tile(n, cands):
    for c in cands:
        if c <= n and n % c == 0:
            return c
    return n


def _roundup(n, m):
    return -(-n // m) * m


def _params(n_axes):
    return pltpu.CompilerParams(dimension_semantics=("arbitrary",) * n_axes, vmem_limit_bytes=VMEM_LIMIT)


def _dot(a, b, dims):
    return lax.dot_general(a, b, dims, preferred_element_type=F32)


def _sigmoid(x):
    return 1.0 / (1.0 + jnp.exp(-x))


COL_SHARDED = ("gin", "up0", "up1", "wq", "wkv")
ROW_SHARDED = ("gout", "down0", "down1", "dout")


def _layout():
    f = D_FF
    hd = ATT_HEADS * HEAD_DIM
    gin = 2 * GLA_KEY_DIM + 2 * GLA_VAL_DIM + GATE_RANK
    up_w = 2 * f // N_CHIPS
    q_w = 3 * hd // N_CHIPS
    kv_w = 2 * hd // N_CHIPS
    dn_r = f // N_CHIPS
    go_r = GLA_VAL_DIM // N_CHIPS
    do_r = hd // N_CHIPS
    big = (1408, 1024, 512, 256, 128)
    return {
        "gin": (0, gin // N_CHIPS, LANE),
        "up0": (0, up_w, _tile(up_w, big)), "up1": (0, up_w, _tile(up_w, big)),
        "wq": (0, q_w, _tile(q_w, (512, 384, 256, 128))), "wkv": (0, kv_w, _tile(kv_w, (1024, 512, 256, 128))),
        "down0": (0, dn_r, _tile(dn_r, big)), "down1": (0, dn_r, _tile(dn_r, big)),
        "gout": (0, go_r, _tile(go_r, (512, 256, 128))), "dout": (0, do_r, _tile(do_r, (512, 256, 128))),
    }


def _matmul(name, a, b, dims, grid, a_spec, b_spec, o_spec, out_shape, acc_shape, add=None, add_spec=None):
    nk = grid[2]
    has_add = add is not None

    def body(*refs):
        a_ref, b_ref = refs[0], refs[1]
        pos = 2
        add_ref = None
        if has_add:
            add_ref = refs[pos]
            pos += 1
        o_ref = refs[pos]
        prod = _dot(a_ref[...].astype(BF16), b_ref[...].astype(BF16), dims)

        def finish(val):
            if has_add:
                val = val + add_ref[...].astype(F32)
            o_ref[...] = val.astype(o_ref.dtype)

        if nk == 1:
            finish(prod)
        else:
            acc_ref = refs[pos + 1]
            k = pl.program_id(2)

            @pl.when(k == 0)
            def _():
                acc_ref[...] = prod

            @pl.when(k > 0)
            def _():
                acc_ref[...] += prod

            @pl.when(k == nk - 1)
            def _():
                finish(acc_ref[...])

    in_specs = [a_spec, b_spec]
    args = [a, b]
    if has_add:
        in_specs.append(add_spec)
        args.append(add)
    scratch = [] if nk == 1 else [pltpu.VMEM(acc_shape, F32)]
    return pl.pallas_call(body, name=name, grid=grid, in_specs=in_specs, out_specs=o_spec, out_shape=out_shape,
                          scratch_shapes=scratch, compiler_params=_params(3))(*args)


def _mm_act_wc(name, a, wc, seg, out_dtype):
    off, w, tn = seg
    t_len, d = a.shape
    tm = _tile(t_len, (1024, 512, 256, 128))
    nps = w // tn
    ob = off // tn
    grid = (t_len // tm, N_CHIPS * nps, 1)
    return _matmul(
        name, a, wc, NN, grid,
        pl.BlockSpec((tm, d), lambda i, j, k: (i, 0)),
        pl.BlockSpec((None, d, tn), lambda i, j, k: (j // nps, 0, ob + j % nps)),
        pl.BlockSpec((tm, tn), lambda i, j, k: (i, j)),
        jax.ShapeDtypeStruct((t_len, N_CHIPS * w), out_dtype), (tm, tn))


def _mm_dact_wcT(name, dy, wc, seg, add=None):
    off, w, tk = seg
    if off == 0 and w <= MAX_K_TILE:
        tk = w
    t_len = dy.shape[0]
    d = wc.shape[1]
    tm = _tile(t_len, (1024, 512, 256, 128))
    tn = _tile(d, (1024, 512, 256, 128))
    kps = w // tk
    ob = off // tk
    grid = (t_len // tm, d // tn, N_CHIPS * kps)
    return _matmul(
        name, dy, wc, NT, grid,
        pl.BlockSpec((tm, tk), lambda i, j, k: (i, k)),
        pl.BlockSpec((None, tn, tk), lambda i, j, k: (k // kps, j, ob + k % kps)),
        pl.BlockSpec((tm, tn), lambda i, j, k: (i, j)),
        jax.ShapeDtypeStruct((t_len, d), F32), (tm, tn),
        add=add, add_spec=None if add is None else pl.BlockSpec((tm, tn), lambda i, j, k: (i, j)))


def _mm_grad_wc(name, a, dy, seg):
    _, w, tn = seg
    t_len, d = a.shape
    tm = _tile(d, (1024, 512, 256, 128))
    tk = _tile(t_len, (2048, 1024, 512, 256, 128))
    nps = w // tn
    grid = (d // tm, N_CHIPS * nps, t_len // tk)
    return _matmul(
        name, a, dy, TN, grid,
        pl.BlockSpec((tk, tm), lambda i, j, k: (k, i)),
        pl.BlockSpec((tk, tn), lambda i, j, k: (k, j)),
        pl.BlockSpec((None, tm, tn), lambda i, j, k: (j // nps, i, j % nps)),
        jax.ShapeDtypeStruct((N_CHIPS, d, w), BF16), (tm, tn))


def _mm_act_wr(name, a, wr, seg, add):
    off, r, tk = seg
    t_len = a.shape[0]
    d = wr.shape[2]
    tm = _tile(t_len, (1024, 512, 256, 128))
    tn = _tile(d, (1024, 512, 256, 128))
    kps = r // tk
    ob = off // tk
    grid = (t_len // tm, d // tn, N_CHIPS * kps)
    return _matmul(
        name, a, wr, NN, grid,
        pl.BlockSpec((tm, tk), lambda i, j, k: (i, k)),
        pl.BlockSpec((None, tk, tn), lambda i, j, k: (k // kps, ob + k % kps, j)),
        pl.BlockSpec((tm, tn), lambda i, j, k: (i, j)),
        jax.ShapeDtypeStruct((t_len, d), F32), (tm, tn),
        add=add, add_spec=pl.BlockSpec((tm, tn), lambda i, j, k: (i, j)))


def _mm_dact_wrT(name, dh, wr, seg):
    off, r, tn = seg
    t_len, d = dh.shape
    tm = _tile(t_len, (1024, 512, 256, 128))
    nps = r // tn
    ob = off // tn
    grid = (t_len // tm, N_CHIPS * nps, 1)
    return _matmul(
        name, dh, wr, NT, grid,
        pl.BlockSpec((tm, d), lambda i, j, k: (i, 0)),
        pl.BlockSpec((None, tn, d), lambda i, j, k: (j // nps, ob + j % nps, 0)),
        pl.BlockSpec((tm, tn), lambda i, j, k: (i, j)),
        jax.ShapeDtypeStruct((t_len, N_CHIPS * r), BF16), (tm, tn))


def _mm_grad_wr(name, a, dh, seg):
    _, r, tm = seg
    t_len, d = dh.shape
    tn = _tile(d, (1024, 512, 256, 128))
    tk = _tile(t_len, (2048, 1024, 512, 256, 128))
    mps = r // tm
    grid = (N_CHIPS * mps, d // tn, t_len // tk)
    return _matmul(
        name, a, dh, TN, grid,
        pl.BlockSpec((tk, tm), lambda i, j, k: (k, i)),
        pl.BlockSpec((tk, tn), lambda i, j, k: (k, j)),
        pl.BlockSpec((None, tm, tn), lambda i, j, k: (i // mps, i % mps, j)),
        jax.ShapeDtypeStruct((N_CHIPS, r, d), BF16), (tm, tn))


def _mm_plain(name, a, b, dims, out_dtype, add=None):
    if dims == NN:
        m, kd = a.shape
        n = b.shape[1]
    elif dims == NT:
        m, kd = a.shape
        n = b.shape[0]
    else:
        kd, m = a.shape
        n = b.shape[1]
    tm = _tile(m, (1024, 512, 256, 128))
    tn = _tile(n, (1024, 768, 512, 256, 128))
    tk = _tile(kd, (2048, 1024, 512, 256, 128))
    grid = (m // tm, n // tn, kd // tk)
    if dims == NN:
        a_spec = pl.BlockSpec((tm, tk), lambda i, j, k: (i, k))
        b_spec = pl.BlockSpec((tk, tn), lambda i, j, k: (k, j))
    elif dims == NT:
        a_spec = pl.BlockSpec((tm, tk), lambda i, j, k: (i, k))
        b_spec = pl.BlockSpec((tn, tk), lambda i, j, k: (j, k))
    else:
        a_spec = pl.BlockSpec((tk, tm), lambda i, j, k: (k, i))
        b_spec = pl.BlockSpec((tk, tn), lambda i, j, k: (k, j))
    o_spec = pl.BlockSpec((tm, tn), lambda i, j, k: (i, j))
    return _matmul(name, a, b, dims, grid, a_spec, b_spec, o_spec, jax.ShapeDtypeStruct((m, n), out_dtype), (tm, tn),
                   add=add, add_spec=None if add is None else o_spec)


def _rms_fwd(name, x, g):
    t_len, d = x.shape
    tm = _tile(t_len, (512, 256, 128))

    def body(x_ref, g_ref, o_ref):
        xv = x_ref[...]
        r = lax.rsqrt(jnp.mean(xv * xv, axis=-1, keepdims=True) + EPS)
        o_ref[...] = (xv * r * g_ref[...]).astype(o_ref.dtype)

    return pl.pallas_call(
        body, name=name, grid=(t_len // tm,),
        in_specs=[pl.BlockSpec((tm, d), lambda i: (i, 0)), pl.BlockSpec((1, d), lambda i: (0, 0))],
        out_specs=pl.BlockSpec((tm, d), lambda i: (i, 0)),
        out_shape=jax.ShapeDtypeStruct((t_len, d), BF16), compiler_params=_params(1))(x, g)


def _rms_bwd(name, dy, x, g, dres):
    t_len, d = x.shape
    tm = _tile(t_len, (256, 128))

    def body(dy_ref, x_ref, g_ref, dres_ref, dx_ref, dg_ref):
        xv = x_ref[...]
        r = lax.rsqrt(jnp.mean(xv * xv, axis=-1, keepdims=True) + EPS)
        xhat = xv * r
        dyv = dy_ref[...].astype(F32)
        dxn = dyv * g_ref[...]
        dx = r * (dxn - xhat * jnp.mean(dxn * xhat, axis=-1, keepdims=True))
        dx_ref[...] = dres_ref[...] + dx
        part = jnp.sum(dyv * xhat, axis=0, keepdims=True)

        @pl.when(pl.program_id(0) == 0)
        def _():
            dg_ref[...] = part

        @pl.when(pl.program_id(0) > 0)
        def _():
            dg_ref[...] += part

    row = pl.BlockSpec((tm, d), lambda i: (i, 0))
    vec = pl.BlockSpec((1, d), lambda i: (0, 0))
    return pl.pallas_call(
        body, name=name, grid=(t_len // tm,), in_specs=[row, row, vec, row], out_specs=(row, vec),
        out_shape=(jax.ShapeDtypeStruct((t_len, d), F32), jax.ShapeDtypeStruct((1, d), F32)),
        compiler_params=_params(1))(dy, x, g, dres)


def _loss_head(h, g, target):
    t_len, d = h.shape
    tm = _tile(t_len, (256, 128))

    def body(h_ref, g_ref, t_ref, dh_ref, dg_ref, loss_ref):
        xv = h_ref[...]
        gv = g_ref[...]
        r = lax.rsqrt(jnp.mean(xv * xv, axis=-1, keepdims=True) + EPS)
        xhat = xv * r
        err = xhat * gv - t_ref[...]
        dyv = err * (1.0 / d)
        dxn = dyv * gv
        dh_ref[...] = r * (dxn - xhat * jnp.mean(dxn * xhat, axis=-1, keepdims=True))
        part = jnp.sum(dyv * xhat, axis=0, keepdims=True)
        lpart = jnp.zeros((8, LANE), F32) + (0.5 / d) * jnp.sum(err * err)

        @pl.when(pl.program_id(0) == 0)
        def _():
            dg_ref[...] = part
            loss_ref[...] = lpart

        @pl.when(pl.program_id(0) > 0)
        def _():
            dg_ref[...] += part
            loss_ref[...] += lpart

    row = pl.BlockSpec((tm, d), lambda i: (i, 0))
    vec = pl.BlockSpec((1, d), lambda i: (0, 0))
    return pl.pallas_call(
        body, name="loss_head", grid=(t_len // tm,), in_specs=[row, vec, row],
        out_specs=(row, vec, pl.BlockSpec((8, LANE), lambda i: (0, 0))),
        out_shape=(jax.ShapeDtypeStruct((t_len, d), F32), jax.ShapeDtypeStruct((1, d), F32),
                   jax.ShapeDtypeStruct((8, LANE), F32)),
        compiler_params=_params(1))(h, g, target)


def _chunk_row(shape):
    return lax.broadcasted_iota(jnp.int32, shape, 0) % GLA_CHUNK


def _gla_gate_fwd(a, w_a2p, b_a2):
    t_len = a.shape[0]
    kd = w_a2p.shape[1]
    tm = _tile(t_len, (256, 128, 64))

    def body(a_ref, w_ref, b_ref, ga_ref, cum_ref):
        ga = _dot(a_ref[...], w_ref[...].astype(BF16), NN) + b_ref[...]
        ga_ref[...] = ga
        la = (jnp.minimum(ga, 0.0) - jnp.log(1.0 + jnp.exp(-jnp.abs(ga)))) * (1.0 / GATE_NORMALIZER)
        row = _chunk_row(la.shape)
        s = 1
        while s < GLA_CHUNK:
            la = la + jnp.where(row >= s, pltpu.roll(la, s, 0), 0.0)
            s *= 2
        cum_ref[...] = la

    return pl.pallas_call(
        body, name="gla_gate_fwd", grid=(t_len // tm,),
        in_specs=[pl.BlockSpec((tm, A_PAD), lambda i: (i, 0)), pl.BlockSpec((A_PAD, kd), lambda i: (0, 0)),
                  pl.BlockSpec((1, kd), lambda i: (0, 0))],
        out_specs=(pl.BlockSpec((tm, kd), lambda i: (i, 0)), pl.BlockSpec((tm, kd), lambda i: (i, 0))),
        out_shape=(jax.ShapeDtypeStruct((t_len, kd), F32), jax.ShapeDtypeStruct((t_len, kd), F32)),
        compiler_params=_params(1))(a, w_a2p, b_a2)


def _gla_gate_bwd(dcum, ga, a, w_a2p):
    t_len, kd = dcum.shape
    tm = _tile(t_len, (256, 128, 64))

    def body(dc_ref, ga_ref, a_ref, w_ref, da_ref, dw_ref, db_ref):
        x = dc_ref[...]
        row = _chunk_row(x.shape)
        s = 1
        while s < GLA_CHUNK:
            x = x + jnp.where(row < GLA_CHUNK - s, pltpu.roll(x, tm - s, 0), 0.0)
            s *= 2
        dga = x * (1.0 / GATE_NORMALIZER) * _sigmoid(-ga_ref[...])
        dgab = dga.astype(BF16)
        da_ref[...] = _dot(dgab, w_ref[...].astype(BF16), NT).astype(da_ref.dtype)
        dw = _dot(a_ref[...], dgab, TN)
        db = jnp.sum(dga, axis=0, keepdims=True)

        @pl.when(pl.program_id(0) == 0)
        def _():
            dw_ref[...] = dw
            db_ref[...] = db

        @pl.when(pl.program_id(0) > 0)
        def _():
            dw_ref[...] += dw
            db_ref[...] += db

    wide = pl.BlockSpec((tm, kd), lambda i: (i, 0))
    return pl.pallas_call(
        body, name="gla_gate_bwd", grid=(t_len // tm,),
        in_specs=[wide, wide, pl.BlockSpec((tm, A_PAD), lambda i: (i, 0)), pl.BlockSpec((A_PAD, kd), lambda i: (0, 0))],
        out_specs=(pl.BlockSpec((tm, A_PAD), lambda i: (i, 0)), pl.BlockSpec((A_PAD, kd), lambda i: (0, 0)),
                   pl.BlockSpec((1, kd), lambda i: (0, 0))),
        out_shape=(jax.ShapeDtypeStruct((t_len, A_PAD), BF16), jax.ShapeDtypeStruct((A_PAD, kd), F32),
                   jax.ShapeDtypeStruct((1, kd), F32)),
        compiler_params=_params(1))(dcum, ga, a, w_a2p)


GLA_STEP_CHUNKS = 4


def _gla_dims():
    dk = GLA_KEY_DIM // GLA_HEADS
    dv = GLA_VAL_DIM // GLA_HEADS
    return dk, dv


def _gla_fwd(proj, cum):
    t_len = proj.shape[0]
    dk, dv = _gla_dims()
    nc = t_len // GLA_CHUNK
    c = GLA_CHUNK
    scale = dk ** -0.5
    v0 = 2 * GLA_KEY_DIM // dv

    per = _tile(nc, (GLA_STEP_CHUNKS, 2, 1))
    rows = per * c

    def body(q_ref, k_ref, v_ref, cum_ref, o_ref, st_ref, s_scr):
        @pl.when(pl.program_id(1) == 0)
        def _():
            s_scr[...] = jnp.zeros_like(s_scr)

        tri = lax.broadcasted_iota(jnp.int32, (c, c), 0) >= lax.broadcasted_iota(jnp.int32, (c, c), 1)
        for i in range(per):
            rs = slice(i * c, (i + 1) * c)
            cm = cum_ref[rs, :]
            last = cm[c - 1:c, :]
            q = q_ref[rs, :].astype(F32) * scale
            k = k_ref[rs, :].astype(F32)
            v = v_ref[rs, :].astype(BF16)
            qd = (q * jnp.exp(cm)).astype(BF16)
            ki = (k * jnp.exp(-cm)).astype(BF16)
            ke = (k * jnp.exp(last - cm)).astype(BF16)
            sc = jnp.where(tri, _dot(qd, ki, NT), 0.0)
            st = s_scr[...]
            st_ref[i] = st
            o_ref[rs, :] = _dot(sc.astype(BF16), v, NN) + _dot(qd, st.astype(BF16), NT)
            s_scr[...] = st * jnp.exp(last) + _dot(v, ke, TN)

    return pl.pallas_call(
        body, name="gla_fwd", grid=(GLA_HEADS, nc // per),
        in_specs=[pl.BlockSpec((rows, dk), lambda h, n: (n, h)),
                  pl.BlockSpec((rows, dk), lambda h, n: (n, GLA_HEADS + h)),
                  pl.BlockSpec((rows, dv), lambda h, n: (n, v0 + h)),
                  pl.BlockSpec((rows, dk), lambda h, n: (n, h))],
        out_specs=(pl.BlockSpec((rows, dv), lambda h, n: (n, h)),
                   pl.BlockSpec((None, per, dv, dk), lambda h, n: (h, n, 0, 0))),
        out_shape=(jax.ShapeDtypeStruct((t_len, GLA_VAL_DIM), F32),
                   jax.ShapeDtypeStruct((GLA_HEADS, nc, dv, dk), F32)),
        scratch_shapes=[pltpu.VMEM((dv, dk), F32)], compiler_params=_params(2))(proj, proj, proj, cum)


def _gla_bwd(proj, cum, states, do):
    t_len = proj.shape[0]
    dk, dv = _gla_dims()
    nc = t_len // GLA_CHUNK
    c = GLA_CHUNK
    scale = dk ** -0.5
    v0 = 2 * GLA_KEY_DIM // dv

    per = _tile(nc, (GLA_STEP_CHUNKS, 2, 1))
    rows = per * c

    def body(q_ref, k_ref, v_ref, cum_ref, st_ref, do_ref, dq_ref, dk_ref, dv_ref, dc_ref, ds_scr):
        @pl.when(pl.program_id(1) == 0)
        def _():
            ds_scr[...] = jnp.zeros_like(ds_scr)

        tri = lax.broadcasted_iota(jnp.int32, (c, c), 0) >= lax.broadcasted_iota(jnp.int32, (c, c), 1)
        row = lax.broadcasted_iota(jnp.int32, (c, dk), 0)
        for i in reversed(range(per)):
            rs = slice(i * c, (i + 1) * c)
            cm = cum_ref[rs, :]
            last = cm[c - 1:c, :]
            e_c = jnp.exp(cm)
            e_nc = jnp.exp(-cm)
            e_lc = jnp.exp(last - cm)
            e_l = jnp.exp(last)
            q = q_ref[rs, :].astype(F32) * scale
            k = k_ref[rs, :].astype(F32)
            v = v_ref[rs, :].astype(BF16)
            dov = do_ref[rs, :]
            qd32 = q * e_c
            ki32 = k * e_nc
            ke32 = k * e_lc
            qd = qd32.astype(BF16)
            ki = ki32.astype(BF16)
            ke = ke32.astype(BF16)
            st = st_ref[i]
            dst = ds_scr[...]
            dstb = dst.astype(BF16)
            am = jnp.where(tri, _dot(dov, v, NT), 0.0).astype(BF16)
            pm = jnp.where(tri, _dot(qd, ki, NT), 0.0).astype(BF16)
            dqd = _dot(am, ki, NN) + _dot(dov, st.astype(BF16), NN)
            dki = _dot(am, qd, TN)
            dvv = _dot(pm, dov, TN) + _dot(ke, dstb, NT)
            dke = _dot(v, dstb, NN)
            d_el = jnp.sum(dst * st, axis=0, keepdims=True)
            ds_scr[...] = dst * e_l + _dot(dov, qd, TN)
            dq_ref[rs, :] = (dqd * scale * e_c).astype(dq_ref.dtype)
            dk_ref[rs, :] = (dki * e_nc + dke * e_lc).astype(dk_ref.dtype)
            dv_ref[rs, :] = dvv.astype(dv_ref.dtype)
            dkeke = dke * ke32
            dcum = dqd * qd32 - dki * ki32 - dkeke
            dlast = jnp.sum(dkeke, axis=0, keepdims=True) + d_el * e_l
            dc_ref[rs, :] = jnp.where(row == c - 1, dcum + dlast, dcum)

    rev = nc // per - 1
    return pl.pallas_call(
        body, name="gla_bwd", grid=(GLA_HEADS, nc // per),
        in_specs=[pl.BlockSpec((rows, dk), lambda h, n: (rev - n, h)),
                  pl.BlockSpec((rows, dk), lambda h, n: (rev - n, GLA_HEADS + h)),
                  pl.BlockSpec((rows, dv), lambda h, n: (rev - n, v0 + h)),
                  pl.BlockSpec((rows, dk), lambda h, n: (rev - n, h)),
                  pl.BlockSpec((None, per, dv, dk), lambda h, n: (h, rev - n, 0, 0)),
                  pl.BlockSpec((rows, dv), lambda h, n: (rev - n, h))],
        out_specs=(pl.BlockSpec((rows, dk), lambda h, n: (rev - n, h)),
                   pl.BlockSpec((rows, dk), lambda h, n: (rev - n, h)),
                   pl.BlockSpec((rows, dv), lambda h, n: (rev - n, h)),
                   pl.BlockSpec((rows, dk), lambda h, n: (rev - n, h))),
        out_shape=(jax.ShapeDtypeStruct((t_len, GLA_KEY_DIM), BF16), jax.ShapeDtypeStruct((t_len, GLA_KEY_DIM), BF16),
                   jax.ShapeDtypeStruct((t_len, GLA_VAL_DIM), BF16), jax.ShapeDtypeStruct((t_len, GLA_KEY_DIM), F32)),
        scratch_shapes=[pltpu.VMEM((dv, dk), F32)], compiler_params=_params(2))(proj, proj, proj, cum, states, do)


def _gla_out_fwd(o, proj, gn):
    t_len = o.shape[0]
    _, dv = _gla_dims()
    tm = _tile(t_len, (512, 256, 128))
    r0 = (2 * GLA_KEY_DIM + GLA_VAL_DIM) // dv

    def body(o_ref, r_ref, g_ref, y_ref):
        ov = o_ref[...]
        rs = lax.rsqrt(jnp.mean(ov * ov, axis=-1, keepdims=True) + EPS)
        rv = r_ref[...].astype(F32)
        y_ref[...] = (ov * rs * g_ref[...] * (rv * _sigmoid(rv))).astype(y_ref.dtype)

    return pl.pallas_call(
        body, name="gla_out_fwd", grid=(t_len // tm, GLA_HEADS),
        in_specs=[pl.BlockSpec((tm, dv), lambda i, h: (i, h)), pl.BlockSpec((tm, dv), lambda i, h: (i, r0 + h)),
                  pl.BlockSpec((1, dv), lambda i, h: (0, 0))],
        out_specs=pl.BlockSpec((tm, dv), lambda i, h: (i, h)),
        out_shape=jax.ShapeDtypeStruct((t_len, GLA_VAL_DIM), BF16), compiler_params=_params(2))(o, proj, gn)


def _gla_out_bwd(dy, o, proj, gn):
    t_len = o.shape[0]
    _, dv = _gla_dims()
    tm = _tile(t_len, (512, 256, 128))
    r0 = (2 * GLA_KEY_DIM + GLA_VAL_DIM) // dv

    def body(dy_ref, o_ref, r_ref, g_ref, do_ref, dr_ref, dg_ref):
        ov = o_ref[...]
        gv = g_ref[...]
        rs = lax.rsqrt(jnp.mean(ov * ov, axis=-1, keepdims=True) + EPS)
        xhat = ov * rs
        rv = r_ref[...].astype(F32)
        sg = _sigmoid(rv)
        gate = rv * sg
        dyv = dy_ref[...].astype(F32)
        dn = dyv * gate
        dr_ref[...] = (dyv * xhat * gv * (sg * (1.0 + rv * (1.0 - sg)))).astype(dr_ref.dtype)
        dxn = dn * gv
        do_ref[...] = (rs * (dxn - xhat * jnp.mean(dxn * xhat, axis=-1, keepdims=True))).astype(do_ref.dtype)
        part = jnp.sum(dn * xhat, axis=0, keepdims=True)
        first = (pl.program_id(0) == 0) & (pl.program_id(1) == 0)

        @pl.when(first)
        def _():
            dg_ref[...] = part

        @pl.when(jnp.logical_not(first))
        def _():
            dg_ref[...] += part

    blk = pl.BlockSpec((tm, dv), lambda i, h: (i, h))
    return pl.pallas_call(
        body, name="gla_out_bwd", grid=(t_len // tm, GLA_HEADS),
        in_specs=[blk, blk, pl.BlockSpec((tm, dv), lambda i, h: (i, r0 + h)), pl.BlockSpec((1, dv), lambda i, h: (0, 0))],
        out_specs=(blk, blk, pl.BlockSpec((1, dv), lambda i, h: (0, 0))),
        out_shape=(jax.ShapeDtypeStruct((t_len, GLA_VAL_DIM), BF16), jax.ShapeDtypeStruct((t_len, GLA_VAL_DIM), BF16),
                   jax.ShapeDtypeStruct((1, dv), F32)),
        compiler_params=_params(2))(dy, o, proj, gn)


def _alibi_slopes():
    n = ATT_HEADS
    start = 2.0 ** (-8.0 / n)
    return [start ** (i + 1) for i in range(n)]


def _att_masks(d):
    b = ATT_BLOCK
    qa = lax.broadcasted_iota(jnp.int32, (b, b), 0)
    kb = lax.broadcasted_iota(jnp.int32, (b, b), 1)
    dist_c = qa - kb
    dist_p = qa - kb + b
    return dist_c >= 0, dist_p <= b, (dist_c * d).astype(F32), (dist_p * d).astype(F32)


def _att_fwd(q_all, kv, g):
    d = DILATIONS[g]
    assert WINDOWS[g] // d == ATT_BLOCK
    t_len = q_all.shape[0]
    hd = ATT_HEADS * HEAD_DIM
    sub = t_len // d
    nb = sub // ATT_BLOCK
    b = ATT_BLOCK
    e = HEAD_DIM
    scale = e ** -0.5
    slopes = _alibi_slopes()
    qv = q_all.reshape(sub, d * 3 * hd)
    kvv = kv.reshape(sub, d * 2 * hd)

    def body(q_ref, kp_ref, kc_ref, vp_ref, vc_ref, o_ref, l_ref, s_scr, p_scr, li_scr):
        ib = pl.program_id(1)
        valid_c, valid_p0, dist_c, dist_p = _att_masks(d)
        valid_p = valid_p0 & (ib > 0)
        for h in range(ATT_HEADS):
            hs = slice(h * e, (h + 1) * e)
            qh = q_ref[:, hs]
            s_scr[h, 0] = _dot(qh, kc_ref[:, hs], NT)
            s_scr[h, 1] = _dot(qh, kp_ref[:, hs], NT)
        l_ref[...] = jnp.zeros_like(l_ref)
        for h in range(ATT_HEADS):
            s_c = jnp.where(valid_c, s_scr[h, 0] * scale - slopes[h] * dist_c, NEG)
            s_p = jnp.where(valid_p, s_scr[h, 1] * scale - slopes[h] * dist_p, NEG)
            m = jnp.maximum(jnp.max(s_c, axis=1, keepdims=True), jnp.max(s_p, axis=1, keepdims=True))
            p_c = jnp.where(valid_c, jnp.exp(s_c - m), 0.0)
            p_p = jnp.where(valid_p, jnp.exp(s_p - m), 0.0)
            l = jnp.sum(p_c, axis=1, keepdims=True) + jnp.sum(p_p, axis=1, keepdims=True)
            p_scr[h, 0] = p_c.astype(BF16)
            p_scr[h, 1] = p_p.astype(BF16)
            li_scr[:, h:h + 1] = 1.0 / l
            l_ref[:, h:h + 1] = m + jnp.log(l)
        for h in range(ATT_HEADS):
            hs = slice(h * e, (h + 1) * e)
            acc = _dot(p_scr[h, 0], vc_ref[:, hs], NN) + _dot(p_scr[h, 1], vp_ref[:, hs], NN)
            o_ref[:, hs] = acc * li_scr[:, h:h + 1]

    blk = (b, hd)
    cblk = (b, LANE)
    o, lse = pl.pallas_call(
        body, name=f"att_fwd{g}", grid=(d, nb),
        scratch_shapes=[pltpu.VMEM((ATT_HEADS, 2, b, b), F32), pltpu.VMEM((ATT_HEADS, 2, b, b), BF16),
                        pltpu.VMEM((b, LANE), F32)],
        in_specs=[pl.BlockSpec(blk, lambda r, i: (i, 3 * r + g)),
                  pl.BlockSpec(blk, lambda r, i: (jnp.maximum(i - 1, 0), 2 * r)),
                  pl.BlockSpec(blk, lambda r, i: (i, 2 * r)),
                  pl.BlockSpec(blk, lambda r, i: (jnp.maximum(i - 1, 0), 2 * r + 1)),
                  pl.BlockSpec(blk, lambda r, i: (i, 2 * r + 1))],
        out_specs=(pl.BlockSpec(blk, lambda r, i: (i, r)), pl.BlockSpec(cblk, lambda r, i: (i, r))),
        out_shape=(jax.ShapeDtypeStruct((sub, d * hd), F32), jax.ShapeDtypeStruct((sub, d * LANE), F32)),
        compiler_params=_params(2))(qv, kvv, kvv, kvv, kvv)
    return o.reshape(t_len, hd), lse.reshape(t_len, LANE)


def _att_merge(os, ls):
    t_len, hd = os[0].shape
    tm = _tile(t_len, (256, 128))
    e = HEAD_DIM

    def body(o0, o1, o2, l0, l1, l2, of_ref, ob_ref, l_ref):
        a0, a1, a2 = l0[...], l1[...], l2[...]
        m = jnp.maximum(jnp.maximum(a0, a1), a2)
        e0, e1, e2 = jnp.exp(a0 - m), jnp.exp(a1 - m), jnp.exp(a2 - m)
        den = e0 + e1 + e2
        w0, w1, w2 = e0 / den, e1 / den, e2 / den
        l_ref[...] = m + jnp.log(den)
        for h in range(ATT_HEADS):
            hs = slice(h * e, (h + 1) * e)
            c = slice(h, h + 1)
            o = w0[:, c] * o0[:, hs] + w1[:, c] * o1[:, hs] + w2[:, c] * o2[:, hs]
            of_ref[:, hs] = o
            ob_ref[:, hs] = o.astype(ob_ref.dtype)

    row = pl.BlockSpec((tm, hd), lambda i: (i, 0))
    crow = pl.BlockSpec((tm, LANE), lambda i: (i, 0))
    return pl.pallas_call(
        body, name="att_merge", grid=(t_len // tm,), in_specs=[row] * 3 + [crow] * 3, out_specs=(row, row, crow),
        out_shape=(jax.ShapeDtypeStruct((t_len, hd), F32), jax.ShapeDtypeStruct((t_len, hd), BF16),
                   jax.ShapeDtypeStruct((t_len, LANE), F32)),
        compiler_params=_params(1))(*os, *ls)


def _att_delta(do, o):
    t_len, hd = o.shape
    tm = _tile(t_len, (256, 128))
    e = HEAD_DIM

    def body(do_ref, o_ref, d_ref):
        d_ref[...] = jnp.zeros_like(d_ref)
        for h in range(ATT_HEADS):
            hs = slice(h * e, (h + 1) * e)
            d_ref[:, h:h + 1] = jnp.sum(do_ref[:, hs].astype(F32) * o_ref[:, hs], axis=1, keepdims=True)

    row = pl.BlockSpec((tm, hd), lambda i: (i, 0))
    return pl.pallas_call(
        body, name="att_delta", grid=(t_len // tm,), in_specs=[row, row],
        out_specs=pl.BlockSpec((tm, LANE), lambda i: (i, 0)),
        out_shape=jax.ShapeDtypeStruct((t_len, LANE), F32), compiler_params=_params(1))(do, o)


def _att_bwd(q_all, kv, delta, lse, do, g):
    d = DILATIONS[g]
    t_len = q_all.shape[0]
    hd = ATT_HEADS * HEAD_DIM
    sub = t_len // d
    nb = sub // ATT_BLOCK
    b = ATT_BLOCK
    e = HEAD_DIM
    scale = e ** -0.5
    slopes = _alibi_slopes()
    qv = q_all.reshape(sub, d * 3 * hd)
    kvv = kv.reshape(sub, d * 2 * hd)
    dlv = delta.reshape(sub, d * LANE)
    lv = lse.reshape(sub, d * LANE)
    dov = do.reshape(sub, d * hd)

    def body(qj_ref, qn_ref, kp_ref, kc_ref, vp_ref, vc_ref, doj_ref, don_ref, dj_ref, dn_ref, lj_ref, ln_ref,
             dq_ref, dk_ref, dv_ref, s_scr, dp_scr, p_scr, ds_scr):
        j = pl.program_id(1)
        valid_c, valid_p0, dist_c, dist_p = _att_masks(d)
        valid = (valid_c, valid_p0 & (j > 0), valid_p0 & (j + 1 < nb))
        dist = (dist_c, dist_p, dist_p)
        for h in range(ATT_HEADS):
            hs = slice(h * e, (h + 1) * e)
            qj, qn = qj_ref[:, hs], qn_ref[:, hs]
            kc, kp = kc_ref[:, hs], kp_ref[:, hs]
            vc, vp = vc_ref[:, hs], vp_ref[:, hs]
            doj, don = doj_ref[:, hs], don_ref[:, hs]
            s_scr[h, 0] = _dot(qj, kc, NT)
            s_scr[h, 1] = _dot(qj, kp, NT)
            s_scr[h, 2] = _dot(qn, kc, NT)
            dp_scr[h, 0] = _dot(doj, vc, NT)
            dp_scr[h, 1] = _dot(doj, vp, NT)
            dp_scr[h, 2] = _dot(don, vc, NT)
        for h in range(ATT_HEADS):
            c = slice(h, h + 1)
            lse_t = (lj_ref[:, c], lj_ref[:, c], ln_ref[:, c])
            dlt_t = (dj_ref[:, c], dj_ref[:, c], dn_ref[:, c])
            for t in range(3):
                s = s_scr[h, t] * scale - slopes[h] * dist[t]
                p = jnp.where(valid[t], jnp.exp(jnp.where(valid[t], s - lse_t[t], NEG)), 0.0)
                p_scr[h, t] = p.astype(BF16)
                ds_scr[h, t] = (p * (dp_scr[h, t] - dlt_t[t])).astype(BF16)
        for h in range(ATT_HEADS):
            hs = slice(h * e, (h + 1) * e)
            dq = _dot(ds_scr[h, 0], kc_ref[:, hs], NN) + _dot(ds_scr[h, 1], kp_ref[:, hs], NN)
            dk = _dot(ds_scr[h, 0], qj_ref[:, hs], TN) + _dot(ds_scr[h, 2], qn_ref[:, hs], TN)
            dv = _dot(p_scr[h, 0], doj_ref[:, hs], TN) + _dot(p_scr[h, 2], don_ref[:, hs], TN)
            dq_ref[:, hs] = (dq * scale).astype(dq_ref.dtype)
            dk_ref[:, hs] = (dk * scale).astype(dk_ref.dtype)
            dv_ref[:, hs] = dv.astype(dv_ref.dtype)

    blk = (b, hd)
    cblk = (b, LANE)
    nxt = lambda i: jnp.minimum(i + 1, nb - 1)
    prv = lambda i: jnp.maximum(i - 1, 0)
    tiles = (ATT_HEADS, 3, b, b)
    dq, dk, dv = pl.pallas_call(
        body, name=f"att_bwd{g}", grid=(d, nb),
        scratch_shapes=[pltpu.VMEM(tiles, F32), pltpu.VMEM(tiles, F32), pltpu.VMEM(tiles, BF16), pltpu.VMEM(tiles, BF16)],
        in_specs=[pl.BlockSpec(blk, lambda r, i: (i, 3 * r + g)),
                  pl.BlockSpec(blk, lambda r, i: (nxt(i), 3 * r + g)),
                  pl.BlockSpec(blk, lambda r, i: (prv(i), 2 * r)),
                  pl.BlockSpec(blk, lambda r, i: (i, 2 * r)),
                  pl.BlockSpec(blk, lambda r, i: (prv(i), 2 * r + 1)),
                  pl.BlockSpec(blk, lambda r, i: (i, 2 * r + 1)),
                  pl.BlockSpec(blk, lambda r, i: (i, r)),
                  pl.BlockSpec(blk, lambda r, i: (nxt(i), r)),
                  pl.BlockSpec(cblk, lambda r, i: (i, r)),
                  pl.BlockSpec(cblk, lambda r, i: (nxt(i), r)),
                  pl.BlockSpec(cblk, lambda r, i: (i, r)),
                  pl.BlockSpec(cblk, lambda r, i: (nxt(i), r))],
        out_specs=(pl.BlockSpec(blk, lambda r, i: (i, r)),) * 3,
        out_shape=(jax.ShapeDtypeStruct((sub, d * hd), BF16),) * 3,
        compiler_params=_params(2))(qv, qv, kvv, kvv, kvv, kvv, dov, dov, dlv, dlv, lv, lv)
    return dq.reshape(t_len, hd), dk.reshape(t_len, hd), dv.reshape(t_len, hd)


def _kv_grad_sum(dks, dvs):
    t_len, hd = dks[0].shape
    tm = _tile(t_len, (256, 128))

    def body(k0, k1, k2, v0, v1, v2, o_ref):
        o_ref[:, :hd] = (k0[...].astype(F32) + k1[...].astype(F32) + k2[...].astype(F32)).astype(o_ref.dtype)
        o_ref[:, hd:] = (v0[...].astype(F32) + v1[...].astype(F32) + v2[...].astype(F32)).astype(o_ref.dtype)

    row = pl.BlockSpec((tm, hd), lambda i: (i, 0))
    return pl.pallas_call(
        body, name="kv_grad_sum", grid=(t_len // tm,), in_specs=[row] * 6,
        out_specs=pl.BlockSpec((tm, 2 * hd), lambda i: (i, 0)),
        out_shape=jax.ShapeDtypeStruct((t_len, 2 * hd), BF16), compiler_params=_params(1))(*dks, *dvs)


HALO = 16
INV_SQRT2 = 1.0 / math.sqrt(2.0)
INV_SQRT2PI = 1.0 / math.sqrt(2.0 * math.pi)


def _conv_taps(g, halo, cw, cb):
    row = lax.broadcasted_iota(jnp.int32, g.shape, 0)
    h1 = halo[HALO - 1:HALO, :]
    h2 = halo[HALO - 2:HALO - 1, :]
    g1 = jnp.where(row == 0, h1, pltpu.roll(g, 1, 0))
    g2 = jnp.where(row == 0, h2, jnp.where(row == 1, h1, pltpu.roll(g, 2, 0)))
    gc = cw[0:1, :] * g2 + cw[1:2, :] * g1 + cw[2:3, :] * g + cb
    return gc, g1, g2


def _glu_specs(t_len, f, tm, tc):
    nj = f // tc
    hb = tm // HALO
    u = pl.BlockSpec((tm, tc), lambda j, i: (i, j))
    g = pl.BlockSpec((tm, tc), lambda j, i: (i, nj + j))
    gh = pl.BlockSpec((HALO, tc), lambda j, i: (jnp.maximum(i * hb - 1, 0), nj + j))
    cw = pl.BlockSpec((8, tc), lambda j, i: (0, j))
    cb = pl.BlockSpec((1, tc), lambda j, i: (0, j))
    return u, g, gh, cw, cb


def _glu_fwd(name, up, cw, cb):
    t_len = up.shape[0]
    f = up.shape[1] // 2
    tm = _tile(t_len, (512, 256, 128))
    tc = _tile(f, (1408, 1024, 512, 256, 128))
    u_s, g_s, gh_s, cw_s, cb_s = _glu_specs(t_len, f, tm, tc)

    def body(u_ref, g_ref, gh_ref, cw_ref, cb_ref, o_ref):
        first = pl.program_id(1) == 0
        halo = jnp.where(first, 0.0, gh_ref[...].astype(F32))
        gc, _, _ = _conv_taps(g_ref[...].astype(F32), halo, cw_ref[...], cb_ref[...])
        gel = 0.5 * gc * (1.0 + lax.erf(gc * INV_SQRT2))
        o_ref[...] = (gel * u_ref[...].astype(F32)).astype(o_ref.dtype)

    return pl.pallas_call(
        body, name=name, grid=(f // tc, t_len // tm), in_specs=[u_s, g_s, gh_s, cw_s, cb_s],
        out_specs=pl.BlockSpec((tm, tc), lambda j, i: (i, j)),
        out_shape=jax.ShapeDtypeStruct((t_len, f), BF16), compiler_params=_params(2))(up, up, up, cw, cb)


def _glu_bwd_a(name, dact, up, cw, cb):
    t_len = up.shape[0]
    f = up.shape[1] // 2
    tm = _tile(t_len, (256, 128))
    tc = _tile(f, (1408, 1024, 512, 256, 128))
    u_s, g_s, gh_s, cw_s, cb_s = _glu_specs(t_len, f, tm, tc)

    def body(da_ref, u_ref, g_ref, gh_ref, cw_ref, cb_ref, du_ref, dgc_ref, w0_ref, w1_ref, w2_ref, b_ref):
        first = pl.program_id(1) == 0
        halo = jnp.where(first, 0.0, gh_ref[...].astype(F32))
        g = g_ref[...].astype(F32)
        gc, g1, g2 = _conv_taps(g, halo, cw_ref[...], cb_ref[...])
        phi = 0.5 * (1.0 + lax.erf(gc * INV_SQRT2))
        dgel = phi + gc * jnp.exp(-0.5 * gc * gc) * INV_SQRT2PI
        da = da_ref[...].astype(F32)
        du_ref[...] = (da * gc * phi).astype(du_ref.dtype)
        dgc = da * u_ref[...].astype(F32) * dgel
        dgc_ref[...] = dgc.astype(dgc_ref.dtype)
        parts = (jnp.sum(dgc * g2, axis=0, keepdims=True), jnp.sum(dgc * g1, axis=0, keepdims=True),
                 jnp.sum(dgc * g, axis=0, keepdims=True), jnp.sum(dgc, axis=0, keepdims=True))
        refs = (w0_ref, w1_ref, w2_ref, b_ref)

        @pl.when(first)
        def _():
            for r, p in zip(refs, parts):
                r[...] = p

        @pl.when(jnp.logical_not(first))
        def _():
            for r, p in zip(refs, parts):
                r[...] += p

    tile = pl.BlockSpec((tm, tc), lambda j, i: (i, j))
    vec = pl.BlockSpec((1, tc), lambda j, i: (0, j))
    vshape = jax.ShapeDtypeStruct((1, f), F32)
    return pl.pallas_call(
        body, name=name, grid=(f // tc, t_len // tm), in_specs=[tile, u_s, g_s, gh_s, cw_s, cb_s],
        out_specs=(tile, tile, vec, vec, vec, vec),
        out_shape=(jax.ShapeDtypeStruct((t_len, f), BF16), jax.ShapeDtypeStruct((t_len, f), F32),
                   vshape, vshape, vshape, vshape),
        compiler_params=_params(2))(dact, up, up, up, cw, cb)


def _glu_bwd_b(name, du, dgc, cw):
    t_len, f = du.shape
    tm = _tile(t_len, (128, 64))
    hb = tm // HALO
    n_i = t_len // tm
    last_hb = t_len // HALO - 1

    def body(du_ref, d_ref, dh_ref, cw_ref, o_ref):
        last = pl.program_id(0) == n_i - 1
        halo = jnp.where(last, 0.0, dh_ref[...].astype(F32))
        dd = d_ref[...].astype(F32)
        row = lax.broadcasted_iota(jnp.int32, dd.shape, 0)
        h0 = halo[0:1, :]
        h1 = halo[1:2, :]
        d1 = jnp.where(row == tm - 1, h0, pltpu.roll(dd, tm - 1, 0))
        d2 = jnp.where(row == tm - 1, h1, jnp.where(row == tm - 2, h0, pltpu.roll(dd, tm - 2, 0)))
        cwv = cw_ref[...]
        dg = cwv[2:3, :] * dd + cwv[1:2, :] * d1 + cwv[0:1, :] * d2
        o_ref[:, :f] = du_ref[...]
        o_ref[:, f:] = dg.astype(o_ref.dtype)

    row_s = pl.BlockSpec((tm, f), lambda i: (i, 0))
    return pl.pallas_call(
        body, name=name, grid=(n_i,),
        in_specs=[row_s, row_s, pl.BlockSpec((HALO, f), lambda i: (jnp.minimum((i + 1) * hb, last_hb), 0)),
                  pl.BlockSpec((8, f), lambda i: (0, 0))],
        out_specs=pl.BlockSpec((tm, 2 * f), lambda i: (i, 0)),
        out_shape=jax.ShapeDtypeStruct((t_len, 2 * f), BF16), compiler_params=_params(1))(du, dgc, dgc, cw)


def _adamw(name, w, g, m, v):
    rows, cols = w.shape
    tr = _tile(rows, (256, 128, 64, 32, 16, 8))
    c1 = 1.0 / (1.0 - ADAM_B1 ** ADAM_STEP)
    c2 = 1.0 / (1.0 - ADAM_B2 ** ADAM_STEP)

    def body(w_ref, g_ref, m_ref, v_ref, d_ref, nm_ref, nv_ref):
        gv = g_ref[...]
        nm = ADAM_B1 * m_ref[...] + (1.0 - ADAM_B1) * gv
        nv = ADAM_B2 * v_ref[...] + (1.0 - ADAM_B2) * (gv * gv)
        nm_ref[...] = nm
        nv_ref[...] = nv
        d_ref[...] = -ADAM_LR * ((nm * c1) / (jnp.sqrt(nv * c2) + ADAM_EPS) + ADAM_WD * w_ref[...])

    blk = pl.BlockSpec((tr, cols), lambda i: (i, 0))
    shp = jax.ShapeDtypeStruct((rows, cols), F32)
    return pl.pallas_call(body, name=name, grid=(rows // tr,), in_specs=[blk] * 4, out_specs=(blk,) * 3,
                          out_shape=(shp,) * 3, compiler_params=_params(1))(w, g, m, v)


class _NoComm:
    def __init__(self):
        self.grads = {}

    def need(self, group, ws, after):
        return ws

    def reduce(self, group, grads, carry):
        self.grads.update(grads)
        return carry

    def tick(self, carry):
        return carry


def _local_step(x, target, ws, norms, small, hooks):
    lay = _layout()

    w_main, w_a = _unpack_gin(ws["gin"])
    hn0 = _rms_fwd("rms_attn0", x, norms["attn0"])
    proj = _mm_plain("gla_proj", hn0, w_main, NN, F32)
    a = _mm_plain("gla_proj_a", hn0, w_a, NN, BF16)
    ga, cum = _gla_gate_fwd(a, small["w_a2p"], small["b_a2"])
    o_gla, states = _gla_fwd(proj, cum)
    gated = _gla_out_fwd(o_gla, proj, small["head_norm"])
    h1 = _mm_act_wr("gla_out", gated, ws["gout"], lay["gout"], add=x)

    def ffn_fwd(l, h):
        hn = _rms_fwd(f"rms_ffn{l}", h, norms[f"ffn{l}"])
        up = _mm_act_wc(f"ffn_up{l}", hn, ws[f"up{l}"], lay[f"up{l}"], F32)
        act = _glu_fwd(f"glu_fwd{l}", up, small["conv_w"][l], small["conv_b"][l])
        return hn, up, act, _mm_act_wr(f"ffn_down{l}", act, ws[f"down{l}"], lay[f"down{l}"], add=h)

    ws = hooks.need("B", ws, h1)
    hnf0, up0, act0, h2 = ffn_fwd(0, h1)

    ws = hooks.need("C", ws, h2)
    kvn = _rms_fwd("rms_kv", h2, norms["kv"])
    kv = _mm_act_wc("kv_proj", kvn, ws["wkv"], lay["wkv"], BF16)
    hn1 = _rms_fwd("rms_attn1", h2, norms["attn1"])
    q_all = _mm_act_wc("q_proj", hn1, ws["wq"], lay["wq"], BF16)
    branch = [_att_fwd(q_all, kv, g) for g in range(3)]
    o_att, o_att_b, lse = _att_merge([br[0] for br in branch], [br[1] for br in branch])
    h3 = _mm_act_wr("att_out", o_att_b, ws["dout"], lay["dout"], add=h2)
    hnf1, up1, act1, h4 = ffn_fwd(1, h3)

    dh4, d_final, loss = _loss_head(h4, norms["final"], target)

    sm = {"final": d_final}

    def ffn_bwd(l, dh, h, hn, up, act):
        big = {}
        dact = _mm_dact_wrT(f"ffn_down_dx{l}", dh, ws[f"down{l}"], lay[f"down{l}"])
        big[f"down{l}"] = _mm_grad_wr(f"ffn_down_dw{l}", act, dh, lay[f"down{l}"])
        du, dgc, w0, w1, w2, db = _glu_bwd_a(f"glu_bwd_a{l}", dact, up, small["conv_w"][l], small["conv_b"][l])
        sm[f"conv_w{l}"] = (w0, w1, w2)
        sm[f"conv_b{l}"] = db
        dup = hooks.tick(_glu_bwd_b(f"glu_bwd_b{l}", du, dgc, small["conv_w"][l]))
        dhn = _mm_dact_wcT(f"ffn_up_dx{l}", dup, ws[f"up{l}"], lay[f"up{l}"])
        big[f"up{l}"] = _mm_grad_wc(f"ffn_up_dw{l}", hn, dup, lay[f"up{l}"])
        dh_in, sm[f"ffn{l}"] = _rms_bwd(f"rms_ffn_bwd{l}", dhn, h, norms[f"ffn{l}"], dh)
        return hooks.reduce(f"ffn{l}", big, dh_in)

    dh3 = ffn_bwd(1, dh4, h3, hnf1, up1, act1)

    big = {}
    do_att = _mm_dact_wrT("att_out_dx", dh3, ws["dout"], lay["dout"])
    big["dout"] = _mm_grad_wr("att_out_dw", o_att_b, dh3, lay["dout"])
    delta = _att_delta(do_att, o_att)
    bw = [_att_bwd(q_all, kv, delta, lse, do_att, g) for g in range(3)]
    dq_all = jnp.concatenate([t[0] for t in bw], axis=1)
    dhn1 = _mm_dact_wcT("q_proj_dx", dq_all, ws["wq"], lay["wq"])
    big["wq"] = _mm_grad_wc("q_proj_dw", hn1, dq_all, lay["wq"])
    dh2, sm["attn1"] = _rms_bwd("rms_attn1_bwd", dhn1, h2, norms["attn1"], dh3)
    dkv = hooks.tick(_kv_grad_sum([t[1] for t in bw], [t[2] for t in bw]))
    dkvn = _mm_dact_wcT("kv_proj_dx", dkv, ws["wkv"], lay["wkv"])
    big["wkv"] = _mm_grad_wc("kv_proj_dw", kvn, dkv, lay["wkv"])
    dh2, sm["kv"] = _rms_bwd("rms_kv_bwd", dkvn, h2, norms["kv"], dh2)
    dh2 = hooks.reduce("att", big, dh2)

    dh1 = ffn_bwd(0, dh2, h1, hnf0, up0, act0)

    big = {}
    dgated = _mm_dact_wrT("gla_out_dx", dh1, ws["gout"], lay["gout"])
    big["gout"] = _mm_grad_wr("gla_out_dw", gated, dh1, lay["gout"])
    do_gla, dr, sm["head_norm"] = _gla_out_bwd(dgated, o_gla, proj, small["head_norm"])
    dq, dk, dv, dcum = _gla_bwd(proj, cum, states, do_gla)
    dcum = hooks.tick(dcum)
    da, sm["w_a2p"], sm["b_a2"] = _gla_gate_bwd(dcum, ga, a, small["w_a2p"])
    dproj = jnp.concatenate([dq, dk, dv, dr], axis=1)
    dhn0 = _mm_plain("gla_proj_dx", dproj, w_main, NT, F32)
    dhn0 = _mm_plain("gla_proj_a_dx", da, w_a, NT, F32, add=dhn0)
    gin_main = _mm_plain("gla_proj_dw", hn0, dproj, TN, BF16)
    gin_a = _mm_plain("gla_proj_a_dw", hn0, da, TN, BF16)
    big["gin"] = _pack_gin_grad(gin_main, gin_a)
    grad_x, sm["attn0"] = _rms_bwd("rms_attn0_bwd", dhn0, x, norms["attn0"], dh1)
    return loss, grad_x, sm, big


def _pack_weights(chip, names, gla_w_in, gla_w_out, w_kv, dsa_w_q, dsa_w_out, ffn_w_up, ffn_w_down):
    gin = gla_w_in[0]
    gin = jnp.pad(gin, ((0, 0), (0, _roundup(gin.shape[1], LANE) - gin.shape[1])))
    shards = {"gin": gin, "gout": gla_w_out[0], "up0": ffn_w_up[0], "up1": ffn_w_up[1], "down0": ffn_w_down[0],
              "down1": ffn_w_down[1], "wq": dsa_w_q[0], "wkv": w_kv, "dout": dsa_w_out[0]}
    out = {}
    for name in names:
        w = shards[name]
        buf = jnp.zeros((N_CHIPS,) + w.shape, BF16)
        out[name] = lax.dynamic_update_slice(buf, w.astype(BF16)[None], (chip, 0, 0))
    return out


def _unpack_gin(w_gin):
    w = _layout()["gin"][1]
    full = jnp.transpose(w_gin[:, :, :w], (1, 0, 2)).reshape(D_MODEL, N_CHIPS * w)
    n_main = 2 * GLA_KEY_DIM + 2 * GLA_VAL_DIM
    w_a = jnp.pad(full[:, n_main:], ((0, 0), (0, A_PAD - GATE_RANK)))
    return full[:, :n_main], w_a


def _pack_gin_grad(gin_main, gin_a):
    w = _layout()["gin"][1]
    gin = jnp.concatenate([gin_main, gin_a[:, :GATE_RANK]], axis=1)
    gin = jnp.transpose(gin.reshape(D_MODEL, N_CHIPS, w), (1, 0, 2))
    return jnp.pad(gin, ((0, 0), (0, 0), (0, _roundup(w, LANE) - w)))


def _small_params(attn_norm, ffn_norm, kv_norm, final_norm, conv_b, w_a2, b_a2, head_norm, conv_w):
    norms = {"attn0": attn_norm[0:1], "attn1": attn_norm[1:2], "ffn0": ffn_norm[0:1], "ffn1": ffn_norm[1:2],
             "kv": kv_norm[None, :], "final": final_norm[None, :]}
    small = {"w_a2p": jnp.pad(w_a2, ((0, A_PAD - GATE_RANK), (0, 0))), "b_a2": b_a2[None, :],
             "head_norm": head_norm[None, :], "conv_w": jnp.pad(conv_w, ((0, 0), (0, 8 - conv_w.shape[1]), (0, 0))),
             "conv_b": conv_b[:, None, :]}
    return norms, small


ANY = pl.BlockSpec(memory_space=pl.ANY)


def _place():
    return lax.axis_index("x"), lax.axis_index("y"), lax.axis_index("c")


def _other_chips(x, y):
    return [(1 - x, y), (x, 1 - y), (1 - x, 1 - y)]


def _rcopy(src, dst, ssem, rsem, dev):
    return pltpu.make_async_remote_copy(src_ref=src, dst_ref=dst, send_sem=ssem, recv_sem=rsem, device_id=dev,
                                        device_id_type=MESH)


def _pack_shard(name, w, layer, chip_arr, after):
    rows, cols = w.shape[-2:]
    tr = _tile(rows, (512, 352, 256, 128, 64, 32, 16))

    def body(p_ref, w_ref, after_ref, o_ref):
        o_ref[...] = w_ref[...].astype(o_ref.dtype)

    if w.ndim == 3:
        w_spec = pl.BlockSpec((None, tr, cols), lambda i, p: (layer, i, 0))
    else:
        w_spec = pl.BlockSpec((tr, cols), lambda i, p: (i, 0))
    return pl.pallas_call(
        body, name=name,
        grid_spec=pltpu.PrefetchScalarGridSpec(
            num_scalar_prefetch=1, grid=(rows // tr,), in_specs=[w_spec, ANY],
            out_specs=pl.BlockSpec((None, tr, cols), lambda i, p: (p[0], i, 0))),
        out_shape=jax.ShapeDtypeStruct((N_CHIPS, rows, cols), BF16), compiler_params=_params(1))(chip_arr, w, after)


def _swap_halves(name, arrs):
    n = len(arrs)

    def body(*refs):
        ins, outs = refs[:n], refs[n:2 * n]
        send, recv = refs[2 * n:]
        x, y, c = _place()
        cps = []
        for a in range(n):
            h = ins[a].shape[1] // 2
            cp = _rcopy(ins[a].at[:, pl.ds((1 - c) * h, h)], outs[a], send.at[a], recv.at[a], (x, y, 1 - c))
            cp.start()
            cps.append(cp)
        for cp in cps:
            cp.wait()

    return pl.pallas_call(
        body, name=name, in_specs=[ANY] * n, out_specs=[ANY] * n,
        out_shape=[jax.ShapeDtypeStruct((a.shape[0], a.shape[1] // 2, a.shape[2]), a.dtype) for a in arrs],
        scratch_shapes=[pltpu.SemaphoreType.DMA((n,)), pltpu.SemaphoreType.DMA((n,))])(*arrs)


SEM = pl.BlockSpec(memory_space=pltpu.SEMAPHORE)
EFFECT = pltpu.SideEffectType.DATAFLOW_SIDE_EFFECTING


def _shapes(arrs):
    return [jax.ShapeDtypeStruct(a.shape, a.dtype) for a in arrs]


def _gather_start(name, thru, arrs):
    n, nt = len(arrs), len(thru)

    def body(*refs):
        ins = refs[nt:nt + n]
        send, recv = refs[nt + n], refs[nt + n + 1]
        outs = refs[2 * nt + n + 2:]
        x, y, c = _place()
        me = 2 * x + y
        for a in range(n):
            h = ins[a].shape[1] // 2
            mine = pl.ds(c * h, h)
            for j, (px, py) in enumerate(_other_chips(x, y)):
                _rcopy(ins[a].at[me, mine], outs[a].at[me, mine], send.at[3 * a + j], recv.at[3 * a + j], (px, py, c)).start()

    res = pl.pallas_call(
        body, name=name, in_specs=[ANY] * (nt + n), out_specs=[SEM, SEM] + [ANY] * (nt + n),
        out_shape=[pltpu.SemaphoreType.DMA((3 * n,)), pltpu.SemaphoreType.DMA((3 * n,))] + _shapes(thru) + _shapes(arrs),
        input_output_aliases={i: 2 + i for i in range(nt + n)},
        compiler_params=pltpu.CompilerParams(has_side_effects=EFFECT))(*thru, *arrs)
    return res[0], res[1], res[2:2 + nt], res[2 + nt:]


def _gather_wait(name, send, recv, arrs, after):
    n = len(arrs)

    def body(*refs):
        ins = refs[:n]
        send_ref, recv_ref = refs[n], refs[n + 1]
        x, y, c = _place()
        me = 2 * x + y
        for a in range(n):
            h = ins[a].shape[1] // 2
            mine = pl.ds(c * h, h)
            for j, (px, py) in enumerate(_other_chips(x, y)):
                sent = ins[a].at[me, mine]
                landed = ins[a].at[2 * px + py, mine]
                cp = _rcopy(sent, landed, send_ref.at[3 * a + j], recv_ref.at[3 * a + j], (px, py, c))
                cp.wait_send()
                cp.wait_recv()

    return pl.pallas_call(
        body, name=name, in_specs=[ANY] * n + [SEM, SEM, ANY], out_specs=[ANY] * n, out_shape=_shapes(arrs),
        input_output_aliases={a: a for a in range(n)},
        compiler_params=pltpu.CompilerParams(has_side_effects=EFFECT))(*arrs, send, recv, after)


def _forward_halves(name, arrs):
    n = len(arrs)

    def body(*refs):
        ins, outs = refs[:n], refs[n:2 * n]
        send, recv = refs[2 * n:]
        x, y, c = _place()
        sib = (x, y, 1 - c)
        chips = _other_chips(x, y)
        cps = []
        for a in range(n):
            h = ins[a].shape[1] // 2
            mine = pl.ds(c * h, h)
            for j, (px, py) in enumerate(chips):
                cp = _rcopy(ins[a].at[2 * px + py, mine], outs[a].at[2 * px + py, mine], send.at[3 * a + j],
                            recv.at[3 * a + j], sib)
                cp.start()
                cps.append(cp)
        for a in range(n):
            h = ins[a].shape[1] // 2
            theirs = pl.ds((1 - c) * h, h)
            for j, (px, py) in enumerate(chips):
                got = outs[a].at[2 * px + py, theirs]
                _rcopy(got, got, send.at[3 * a + j], recv.at[3 * a + j], sib).wait_recv()
        for cp in cps:
            cp.wait_send()

    return pl.pallas_call(
        body, name=name, in_specs=[ANY] * n, out_specs=[ANY] * n, out_shape=_shapes(arrs),
        input_output_aliases={a: a for a in range(n)},
        scratch_shapes=[pltpu.SemaphoreType.DMA((3 * n,)), pltpu.SemaphoreType.DMA((3 * n,))])(*arrs)


def _scatter_start(name, thru, arrs):
    n, nt = len(arrs), len(thru)
    landing = [jnp.zeros_like(a) for a in arrs]

    def body(*refs):
        ins = refs[nt:nt + n]
        send, recv = refs[nt + 2 * n], refs[nt + 2 * n + 1]
        outs = refs[2 * nt + 3 * n + 2:]
        x, y, c = _place()
        me = 2 * x + y
        for a in range(n):
            for j, (px, py) in enumerate(_other_chips(x, y)):
                _rcopy(ins[a].at[2 * px + py], outs[a].at[me], send.at[3 * a + j], recv.at[3 * a + j], (px, py, c)).start()

    res = pl.pallas_call(
        body, name=name, in_specs=[ANY] * (nt + 2 * n), out_specs=[SEM, SEM] + [ANY] * (nt + 2 * n),
        out_shape=[pltpu.SemaphoreType.DMA((3 * n,)), pltpu.SemaphoreType.DMA((3 * n,))] + _shapes(thru) + _shapes(arrs)
        + _shapes(landing),
        input_output_aliases={i: 2 + i for i in range(nt + 2 * n)},
        compiler_params=pltpu.CompilerParams(has_side_effects=EFFECT))(*thru, *arrs, *landing)
    return res[0], res[1], res[2:2 + nt], res[2 + nt:2 + nt + n], res[2 + nt + n:]


def _scatter_wait(name, send, recv, arrs, landing, after):
    n = len(arrs)

    def body(*refs):
        ins, land = refs[:n], refs[n:2 * n]
        send_ref, recv_ref = refs[2 * n], refs[2 * n + 1]
        x, y, c = _place()
        for a in range(n):
            for j, (px, py) in enumerate(_other_chips(x, y)):
                cp = _rcopy(ins[a].at[2 * px + py], land[a].at[2 * px + py], send_ref.at[3 * a + j], recv_ref.at[3 * a + j],
                            (px, py, c))
                cp.wait_send()
                cp.wait_recv()

    res = pl.pallas_call(
        body, name=name, in_specs=[ANY] * (2 * n) + [SEM, SEM, ANY], out_specs=[ANY] * (2 * n),
        out_shape=_shapes(arrs) + _shapes(landing), input_output_aliases={i: i for i in range(2 * n)},
        compiler_params=pltpu.CompilerParams(has_side_effects=EFFECT))(*arrs, *landing, send, recv, after)
    return res[:n], res[n:]


def _swap_start(name, thru, arrs):
    n, nt = len(arrs), len(thru)
    landing = [lax.empty((a.shape[0], a.shape[1] // 2, a.shape[2]), a.dtype) for a in arrs]

    def body(*refs):
        ins = refs[nt:nt + n]
        send, recv = refs[nt + 2 * n], refs[nt + 2 * n + 1]
        outs = refs[2 * nt + 3 * n + 2:]
        x, y, c = _place()
        for a in range(n):
            h = ins[a].shape[1] // 2
            _rcopy(ins[a].at[:, pl.ds((1 - c) * h, h)], outs[a], send.at[a], recv.at[a], (x, y, 1 - c)).start()

    res = pl.pallas_call(
        body, name=name, in_specs=[ANY] * (nt + 2 * n), out_specs=[SEM, SEM] + [ANY] * (nt + 2 * n),
        out_shape=[pltpu.SemaphoreType.DMA((n,)), pltpu.SemaphoreType.DMA((n,))] + _shapes(thru) + _shapes(arrs)
        + _shapes(landing),
        input_output_aliases={i: 2 + i for i in range(nt + 2 * n)},
        compiler_params=pltpu.CompilerParams(has_side_effects=EFFECT))(*thru, *arrs, *landing)
    return res[0], res[1], res[2:2 + nt], res[2 + nt:2 + nt + n], res[2 + nt + n:]


def _swap_wait(name, send, recv, arrs, landing, after):
    n = len(arrs)

    def body(*refs):
        ins, land = refs[:n], refs[n:2 * n]
        send_ref, recv_ref = refs[2 * n], refs[2 * n + 1]
        x, y, c = _place()
        for a in range(n):
            h = ins[a].shape[1] // 2
            cp = _rcopy(ins[a].at[:, pl.ds((1 - c) * h, h)], land[a], send_ref.at[a], recv_ref.at[a], (x, y, 1 - c))
            cp.wait_send()
            cp.wait_recv()

    res = pl.pallas_call(
        body, name=name, in_specs=[ANY] * (2 * n) + [SEM, SEM, ANY], out_specs=[ANY] * (2 * n),
        out_shape=_shapes(arrs) + _shapes(landing), input_output_aliases={i: i for i in range(2 * n)},
        compiler_params=pltpu.CompilerParams(has_side_effects=EFFECT))(*arrs, *landing, send, recv, after)
    return res[:n], res[n:]


def _join_start(name, arrs):
    n = len(arrs)

    def body(*refs):
        ins = refs[:n]
        send, recv = refs[n], refs[n + 1]
        outs = refs[n + 2:]
        x, y, c = _place()
        for a in range(n):
            h = ins[a].shape[0] // 2
            mine = pl.ds(c * h, h)
            _rcopy(ins[a].at[mine], outs[a].at[mine], send.at[a], recv.at[a], (x, y, 1 - c)).start()

    res = pl.pallas_call(
        body, name=name, in_specs=[ANY] * n, out_specs=[SEM, SEM] + [ANY] * n,
        out_shape=[pltpu.SemaphoreType.DMA((n,)), pltpu.SemaphoreType.DMA((n,))] + _shapes(arrs),
        input_output_aliases={i: 2 + i for i in range(n)},
        compiler_params=pltpu.CompilerParams(has_side_effects=EFFECT))(*arrs)
    return res[0], res[1], res[2:]


def _join_wait(name, send, recv, arrs, after):
    n = len(arrs)

    def body(*refs):
        ins = refs[:n]
        send_ref, recv_ref = refs[n], refs[n + 1]
        x, y, c = _place()
        for a in range(n):
            h = ins[a].shape[0] // 2
            cp = _rcopy(ins[a].at[pl.ds(c * h, h)], ins[a].at[pl.ds((1 - c) * h, h)], send_ref.at[a], recv_ref.at[a],
                        (x, y, 1 - c))
            cp.wait_send()
            cp.wait_recv()

    return pl.pallas_call(
        body, name=name, in_specs=[ANY] * n + [SEM, SEM, ANY], out_specs=[ANY] * n, out_shape=_shapes(arrs),
        input_output_aliases={a: a for a in range(n)},
        compiler_params=pltpu.CompilerParams(has_side_effects=EFFECT))(*arrs, send, recv, after)


def _join_halves(name, arrs):
    n = len(arrs)

    def body(*refs):
        ins, outs = refs[:n], refs[n:2 * n]
        send, recv = refs[2 * n:]
        x, y, c = _place()
        cps = []
        for a in range(n):
            h = ins[a].shape[0] // 2
            mine = pl.ds(c * h, h)
            cp = _rcopy(ins[a].at[mine], outs[a].at[mine], send.at[a], recv.at[a], (x, y, 1 - c))
            cp.start()
            cps.append(cp)
        for a in range(n):
            h = ins[a].shape[0] // 2
            got = outs[a].at[pl.ds((1 - c) * h, h)]
            _rcopy(got, got, send.at[a], recv.at[a], (x, y, 1 - c)).wait_recv()
        for cp in cps:
            cp.wait_send()

    return pl.pallas_call(
        body, name=name, in_specs=[ANY] * n, out_specs=[ANY] * n,
        out_shape=[jax.ShapeDtypeStruct(a.shape, a.dtype) for a in arrs],
        input_output_aliases={a: a for a in range(n)},
        scratch_shapes=[pltpu.SemaphoreType.DMA((n,)), pltpu.SemaphoreType.DMA((n,))])(*arrs)


def _allgather8(name, xs, reduce):
    m_per, n = xs.shape

    def body(x_ref, out_ref, *rest):
        if reduce:
            sum_ref, send, recv, lsem = rest
        else:
            send, recv, lsem = rest
        x, y, c = _place()
        me, sib = (x, y, c), (x, y, 1 - c)
        chips = _other_chips(x, y)

        def rows(px, py, pc):
            return out_ref.at[pl.ds((4 * px + 2 * py + pc) * m_per, m_per), :]

        def copy(k, block, to, src=None):
            return _rcopy(rows(*block) if src is None else src, rows(*block), send.at[k], recv.at[k], to)

        mine = pltpu.make_async_copy(x_ref, rows(*me), lsem)
        mine.start()
        first = [copy(0, me, sib, src=x_ref)]
        first += [copy(1 + j, me, (*chip, c), src=x_ref) for j, chip in enumerate(chips)]
        for cp in first:
            cp.start()
        passed = [copy(4 + j, (*chip, c), sib) for j, chip in enumerate(chips)]
        for j, chip in enumerate(chips):
            copy(1 + j, (*chip, c), me).wait_recv()
            passed[j].start()
        copy(0, sib, me).wait_recv()
        for j, chip in enumerate(chips):
            copy(4 + j, (*chip, 1 - c), me).wait_recv()
        for cp in first + passed:
            cp.wait_send()
        mine.wait()
        if reduce:
            acc = out_ref[pl.ds(0, m_per), :]
            for dev in range(1, N_DEV):
                acc = acc + out_ref[pl.ds(dev * m_per, m_per), :]
            sum_ref[...] = acc

    vm = pl.BlockSpec(memory_space=pltpu.VMEM)
    out_shape = [jax.ShapeDtypeStruct((N_DEV * m_per, n), xs.dtype)]
    if reduce:
        out_shape.append(jax.ShapeDtypeStruct((m_per, n), xs.dtype))
    return pl.pallas_call(
        body, name=name, in_specs=[vm], out_specs=[vm] * len(out_shape), out_shape=out_shape,
        scratch_shapes=[pltpu.SemaphoreType.DMA((7,)), pltpu.SemaphoreType.DMA((7,)), pltpu.SemaphoreType.DMA],
        compiler_params=pltpu.CompilerParams(vmem_limit_bytes=VMEM_LIMIT))(xs)


def _add_my_half(name, a, rb, c_arr):
    s, h, cols = rb.shape
    tr = _tile(h, (512, 352, 256, 128, 64, 32, 16))
    nt = h // tr

    def body(c_ref, a_ref, b_ref, o_ref):
        o_ref[...] = (a_ref[...].astype(F32) + b_ref[...].astype(F32)).astype(o_ref.dtype)

    return pl.pallas_call(
        body, name=name,
        grid_spec=pltpu.PrefetchScalarGridSpec(
            num_scalar_prefetch=1, grid=(s, nt),
            in_specs=[pl.BlockSpec((None, tr, cols), lambda k, i, c: (k, c[0] * nt + i, 0)),
                      pl.BlockSpec((None, tr, cols), lambda k, i, c: (k, i, 0))],
            out_specs=pl.BlockSpec((None, tr, cols), lambda k, i, c: (k, i, 0))),
        out_shape=jax.ShapeDtypeStruct(rb.shape, BF16), compiler_params=_params(2))(c_arr, a, rb)


def _sum_chips(name, own, q, place):
    s, h, cols = q.shape
    tr = _tile(h, (512, 352, 256, 128, 64, 32, 16))
    nt = h // tr

    def body(p_ref, own_ref, q_ref, o_ref):
        chip = p_ref[0]
        acc = jnp.where(chip == 0, own_ref[0], q_ref[0]).astype(F32)
        for j in range(1, s):
            acc = acc + jnp.where(chip == j, own_ref[j], q_ref[j]).astype(F32)
        o_ref[...] = acc

    blk = pl.BlockSpec((s, tr, cols), lambda i, p: (0, i, 0))
    return pl.pallas_call(
        body, name=name,
        grid_spec=pltpu.PrefetchScalarGridSpec(
            num_scalar_prefetch=1, grid=(nt,), in_specs=[blk, blk],
            out_specs=pl.BlockSpec((tr, cols), lambda i, p: (p[1] * nt + i, 0))),
        out_shape=jax.ShapeDtypeStruct((2 * h, cols), F32), compiler_params=_params(1))(place, own, q)


def _pack_rows(parts):
    rows = []
    for p in parts:
        flat = p.reshape(-1).astype(F32)
        n = _roundup(flat.shape[0], 8 * LANE)
        rows.append(jnp.pad(flat, (0, n - flat.shape[0])).reshape(-1, LANE))
    return jnp.concatenate(rows, axis=0)


def _unpack_rows(buf, shapes):
    out, r = [], 0
    for shp in shapes:
        size = math.prod(shp)
        nr = _roundup(size, 8 * LANE) // LANE
        out.append(buf[r:r + nr].reshape(-1)[:size].reshape(shp))
        r += nr
    return out


def kernel(x, attn_norm, gla_w_in, gla_w_a2, gla_b_a2, gla_head_norm, gla_w_out, kv_norm, w_kv, dsa_w_q, dsa_w_out, ffn_norm, ffn_w_up, ffn_conv_w, ffn_conv_b, ffn_w_down, final_norm, loss_target, m_attn_norm, m_gla_w_in, m_gla_w_a2, m_gla_b_a2, m_gla_head_norm, m_gla_w_out, m_kv_norm, m_w_kv, m_dsa_w_q, m_dsa_w_out, m_ffn_norm, m_ffn_w_up, m_ffn_conv_w, m_ffn_conv_b, m_ffn_w_down, m_final_norm, v_attn_norm, v_gla_w_in, v_gla_w_a2, v_gla_b_a2, v_gla_head_norm, v_gla_w_out, v_kv_norm, v_w_kv, v_dsa_w_q, v_dsa_w_out, v_ffn_norm, v_ffn_w_up, v_ffn_conv_w, v_ffn_conv_b, v_ffn_w_down, v_final_norm):
    lay = _layout()
    d, f = D_MODEL, D_FF
    cx, cy, cc = _place()
    chip = 2 * cx + cy
    c_arr = jnp.reshape(cc, (1,)).astype(jnp.int32)
    place = jnp.stack([chip, cc]).astype(jnp.int32)

    groups = {"A": ("gin", "gout", "small"), "B": ("up0", "down0"), "C": ("up1", "down1", "wq", "wkv", "dout")}
    big_shards = (gla_w_in, gla_w_out, w_kv, dsa_w_q, dsa_w_out, ffn_w_up, ffn_w_down)
    ws = _pack_weights(chip, groups["A"][:2], *big_shards)
    sharded_small = [gla_w_a2[0], gla_b_a2[0], gla_head_norm[0], ffn_conv_w]
    packed = _pack_rows(sharded_small)
    packed = jnp.pad(packed, ((0, _roundup(packed.shape[0], 16) - packed.shape[0]), (0, 0)))
    ws["small"] = lax.dynamic_update_slice(jnp.zeros((N_CHIPS,) + packed.shape, F32), packed[None], (chip, 0, 0))
    send, recv, _, arrs = _gather_start("gather_a_start", [], [ws[k] for k in groups["A"]])
    chip_arr = place[:1]
    sources = {"up0": (ffn_w_up, 0), "up1": (ffn_w_up, 1), "down0": (ffn_w_down, 0), "down1": (ffn_w_down, 1),
               "wq": (dsa_w_q, 0), "wkv": (w_kv, 0), "dout": (dsa_w_out, 0)}
    for k in groups["B"] + groups["C"]:
        ws[k] = _pack_shard(f"pack_{k}", *sources[k], chip_arr, arrs[2])
    arrs = _gather_wait("gather_a_wait", send, recv, arrs, ws["dout"])
    ws.update(zip(groups["A"], _forward_halves("forward_a", arrs)))
    in_flight = {}
    thru = [ws[k] for k in groups["A"]]
    for grp in ("B", "C"):
        send, recv, thru, arrs = _gather_start(f"gather_{grp.lower()}_start", thru, [ws[k] for k in groups[grp]])
        ws.update(zip(groups[grp], arrs))
        in_flight[grp] = (send, recv)
    ws.update(zip(groups["A"], thru))
    pending = []

    class _Comm:
        def need(self, grp, ws, after):
            send, recv = in_flight[grp]
            arrs = _gather_wait(f"gather_{grp.lower()}_wait", send, recv, [ws[k] for k in groups[grp]], after)
            arrs = _forward_halves(f"forward_{grp.lower()}", arrs)
            return {**ws, **dict(zip(groups[grp], arrs))}

        swapping = None

        def reduce(self, grp, grads, carry):
            names = list(grads)
            send, recv, thru, parts, theirs = _swap_start(f"swap_{grp}_start", [carry], [grads[k] for k in names])
            self.swapping = (grp, names, send, recv, parts, theirs)
            return thru[0]

        def tick(self, carry):
            if self.swapping is None:
                return carry
            grp, names, send, recv, parts, theirs = self.swapping
            self.swapping = None
            parts, theirs = _swap_wait(f"swap_{grp}_wait", send, recv, parts, theirs, carry)
            return self.scatter(grp, names, parts, theirs, carry)

        def scatter(self, grp, names, parts, theirs, carry):
            sums = [_add_my_half(f"add_half_{k}", a, b, c_arr) for k, a, b in zip(names, parts, theirs)]
            send, recv, thru, sums, landing = _scatter_start(f"scatter_{grp}_start", [carry], sums)
            pending.append((grp, names, send, recv, sums, landing))
            return thru[0]

        def reduce_now(self, grp, grads, carry):
            names = list(grads)
            parts = [grads[k] for k in names]
            return self.scatter(grp, names, parts, _swap_halves(f"swap_{grp}", parts), carry)

    shards = [_unpack_rows(ws["small"][s], [p.shape for p in sharded_small]) for s in range(N_CHIPS)]
    w_a2, b_a2, head_norm, conv_w = [jnp.concatenate([shards[s][k] for s in range(N_CHIPS)], axis=-1) for k in range(4)]
    norms, small = _small_params(attn_norm, ffn_norm, kv_norm, final_norm, ffn_conv_b, w_a2, b_a2, head_norm, conv_w)

    comm = _Comm()
    loss_blk, grad_x, sm, last_big = _local_step(x[0], loss_target[0], ws, norms, small, comm)

    small_parts = [loss_blk, jnp.concatenate([sm["attn0"], sm["attn1"]]), jnp.concatenate([sm["ffn0"], sm["ffn1"]]),
                   sm["kv"], sm["final"], jnp.concatenate([sm["conv_b0"], sm["conv_b1"]]),
                   sm["w_a2p"][:GATE_RANK], sm["b_a2"], sm["head_norm"],
                   jnp.stack([jnp.concatenate(sm["conv_w0"]), jnp.concatenate(sm["conv_w1"])])]
    small_shapes = [(8, LANE), (2, d), (2, d), (d,), (d,), (2, f), (GATE_RANK, GLA_KEY_DIM), (GLA_KEY_DIM,),
                    (GLA_VAL_DIM // GLA_HEADS,), (2, 3, f)]
    _, reduced = _allgather8("reduce_small", _pack_rows(small_parts), True)
    reduced = comm.reduce_now("gla", last_big, reduced)

    gin_w = lay["gin"][1]
    full = {}
    after = reduced
    joining = []
    for grp, names, send, recv, sums, landing in pending:
        sums, landing = _scatter_wait(f"scatter_{grp}_wait", send, recv, sums, landing, after)
        halves = [_sum_chips(f"sum_chips_{k}", s, q, place) for k, s, q in zip(names, sums, landing)]
        send, recv, halves = _join_start(f"join_{grp}_start", halves)
        joining.append((grp, names, send, recv, halves))
        after = halves[0]
    for grp, names, send, recv, halves in joining:
        joined = _join_wait(f"join_{grp}_wait", send, recv, halves, after)
        full.update(zip(names, joined))
        after = joined[0]
    loss_r, g_attn, g_ffn, g_kv, g_final, g_cb, g_a2, g_ba2, g_hn, g_cw = _unpack_rows(reduced, small_shapes)
    loss = loss_r[0, 0]

    def mine(g, axis):
        w = g.shape[axis] // N_CHIPS
        return lax.dynamic_slice_in_dim(g, chip * w, w, axis)

    grads = {
        "attn_norm": g_attn, "gla_w_in": full["gin"][None, :, :gin_w], "gla_w_a2": mine(g_a2, 1)[None],
        "gla_b_a2": mine(g_ba2, 0)[None], "gla_head_norm": mine(g_hn, 0)[None], "gla_w_out": full["gout"][None],
        "kv_norm": g_kv, "w_kv": full["wkv"], "dsa_w_q": full["wq"][None], "dsa_w_out": full["dout"][None],
        "ffn_norm": g_ffn, "ffn_w_up": jnp.stack([full["up0"], full["up1"]]), "ffn_conv_w": mine(g_cw, 2),
        "ffn_conv_b": g_cb, "ffn_w_down": jnp.stack([full["down0"], full["down1"]]), "final_norm": g_final,
    }
    weights = {"attn_norm": (attn_norm, m_attn_norm, v_attn_norm), "gla_w_in": (gla_w_in, m_gla_w_in, v_gla_w_in),
               "gla_w_a2": (gla_w_a2, m_gla_w_a2, v_gla_w_a2), "gla_b_a2": (gla_b_a2, m_gla_b_a2, v_gla_b_a2),
               "gla_head_norm": (gla_head_norm, m_gla_head_norm, v_gla_head_norm),
               "gla_w_out": (gla_w_out, m_gla_w_out, v_gla_w_out), "kv_norm": (kv_norm, m_kv_norm, v_kv_norm),
               "w_kv": (w_kv, m_w_kv, v_w_kv), "dsa_w_q": (dsa_w_q, m_dsa_w_q, v_dsa_w_q),
               "dsa_w_out": (dsa_w_out, m_dsa_w_out, v_dsa_w_out), "ffn_norm": (ffn_norm, m_ffn_norm, v_ffn_norm),
               "ffn_w_up": (ffn_w_up, m_ffn_w_up, v_ffn_w_up), "ffn_conv_w": (ffn_conv_w, m_ffn_conv_w, v_ffn_conv_w),
               "ffn_conv_b": (ffn_conv_b, m_ffn_conv_b, v_ffn_conv_b),
               "ffn_w_down": (ffn_w_down, m_ffn_w_down, v_ffn_w_down), "final_norm": (final_norm, m_final_norm, v_final_norm)}
    order = list(weights)
    big_names = ("gla_w_in", "gla_w_out", "w_kv", "dsa_w_q", "dsa_w_out", "ffn_w_up", "ffn_w_down")
    delta, new_m, new_v = {}, {}, {}
    for k in big_names:
        w, m, v = weights[k]
        cols = w.shape[-1]
        res = _adamw(f"adamw_{k}", w.reshape(-1, cols), grads[k].reshape(-1, cols), m.reshape(-1, cols), v.reshape(-1, cols))
        delta[k], new_m[k], new_v[k] = [r.reshape(w.shape) for r in res]
    small_names = [k for k in order if k not in big_names]
    packed = [_pack_rows([src[k] for k in small_names])
              for src in ({k: weights[k][0] for k in small_names}, grads, {k: weights[k][1] for k in small_names},
                          {k: weights[k][2] for k in small_names})]
    res = _adamw("adamw_small", *packed)
    shapes = [weights[k][0].shape for k in small_names]
    for dst, buf in zip((delta, new_m, new_v), res):
        for k, val in zip(small_names, _unpack_rows(buf, shapes)):
            dst[k] = val
    return (loss, grad_x[None], *[grads[k] for k in order], *[delta[k] for k in order], *[new_m[k] for k in order],
            *[new_v[k] for k in order])
```

```python
import math

import jax
import jax.numpy as jnp
from jax import lax
from jax.experimental import pallas as pl
from jax.experimental.pallas import tpu as pltpu

F32 = jnp.float32
BF16 = jnp.bfloat16

D_MODEL = 2048
SEQ = 4096
GLA_HEADS = 4
GLA_KEY_DIM = D_MODEL // 2
GLA_VAL_DIM = D_MODEL
GATE_RANK = 16
GATE_NORMALIZER = 16.0
GLA_CHUNK = 64
ATT_HEADS = 16
HEAD_DIM = 128
WINDOWS = (128, 512, 2048)
DILATIONS = (1, 4, 16)
ATT_BLOCK = 128
D_FF = 5632
EPS = 1e-6
ADAM_LR = 0.001
ADAM_B1 = 0.9
ADAM_B2 = 0.999
ADAM_EPS = 1e-08
ADAM_WD = 0.01
ADAM_STEP = 10

N_CHIPS = 4
N_DEV = 8
LANE = 128
A_PAD = 128
VMEM_LIMIT = 56 * 1024 * 1024
MAX_K_TILE = 2816
NEG = -1e30
MESH = pl.DeviceIdType.MESH

NN = (((1,), (0,)), ((), ()))
NT = (((1,), (1,)), ((), ()))
TN = (((0,), (0,)), ((), ()))


def _tile(n, cands):
    for c in cands:
        if c <= n and n % c == 0:
            return c
    return n


def _roundup(n, m):
    return -(-n // m) * m


def _params(n_axes):
    return pltpu.CompilerParams(dimension_semantics=("arbitrary",) * n_axes, vmem_limit_bytes=VMEM_LIMIT)


def _dot(a, b, dims):
    return lax.dot_general(a, b, dims, preferred_element_type=F32)


def _sigmoid(x):
    return 1.0 / (1.0 + jnp.exp(-x))


COL_SHARDED = ("gin", "up0", "up1", "wq", "wkv")
ROW_SHARDED = ("gout", "down0", "down1", "dout")


def _layout():
    f = D_FF
    hd = ATT_HEADS * HEAD_DIM
    gin = 2 * GLA_KEY_DIM + 2 * GLA_VAL_DIM + GATE_RANK
    up_w = 2 * f // N_CHIPS
    q_w = 3 * hd // N_CHIPS
    kv_w = 2 * hd // N_CHIPS
    dn_r = f // N_CHIPS
    go_r = GLA_VAL_DIM // N_CHIPS
    do_r = hd // N_CHIPS
    big = (1408, 1024, 512, 256, 128)
    return {
        "gin": (0, gin // N_CHIPS, LANE),
        "up0": (0, up_w, _tile(up_w, big)), "up1": (0, up_w, _tile(up_w, big)),
        "wq": (0, q_w, _tile(q_w, (512, 384, 256, 128))), "wkv": (0, kv_w, _tile(kv_w, (1024, 512, 256, 128))),
        "down0": (0, dn_r, _tile(dn_r, big)), "down1": (0, dn_r, _tile(dn_r, big)),
        "gout": (0, go_r, _tile(go_r, (512, 256, 128))), "dout": (0, do_r, _tile(do_r, (512, 256, 128))),
    }


def _matmul(name, a, b, dims, grid, a_spec, b_spec, o_spec, out_shape, acc_shape, add=None, add_spec=None):
    nk = grid[2]
    has_add = add is not None

    def body(*refs):
        a_ref, b_ref = refs[0], refs[1]
        pos = 2
        add_ref = None
        if has_add:
            add_ref = refs[pos]
            pos += 1
        o_ref = refs[pos]
        prod = _dot(a_ref[...].astype(BF16), b_ref[...].astype(BF16), dims)

        def finish(val):
            if has_add:
                val = val + add_ref[...].astype(F32)
            o_ref[...] = val.astype(o_ref.dtype)

        if nk == 1:
            finish(prod)
        else:
            acc_ref = refs[pos + 1]
            k = pl.program_id(2)

            @pl.when(k == 0)
            def _():
                acc_ref[...] = prod

            @pl.when(k > 0)
            def _():
                acc_ref[...] += prod

            @pl.when(k == nk - 1)
            def _():
                finish(acc_ref[...])

    in_specs = [a_spec, b_spec]
    args = [a, b]
    if has_add:
        in_specs.append(add_spec)
        args.append(add)
    scratch = [] if nk == 1 else [pltpu.VMEM(acc_shape, F32)]
    return pl.pallas_call(body, name=name, grid=grid, in_specs=in_specs, out_specs=o_spec, out_shape=out_shape,
                          scratch_shapes=scratch, compiler_params=_params(3))(*args)


def _mm_act_wc(name, a, wc, seg, out_dtype):
    off, w, tn = seg
    t_len, d = a.shape
    tm = _tile(t_len, (1024, 512, 256, 128))
    nps = w // tn
    ob = off // tn
    grid = (t_len // tm, N_CHIPS * nps, 1)
    return _matmul(
        name, a, wc, NN, grid,
        pl.BlockSpec((tm, d), lambda i, j, k: (i, 0)),
        pl.BlockSpec((None, d, tn), lambda i, j, k: (j // nps, 0, ob + j % nps)),
        pl.BlockSpec((tm, tn), lambda i, j, k: (i, j)),
        jax.ShapeDtypeStruct((t_len, N_CHIPS * w), out_dtype), (tm, tn))


def _mm_dact_wcT(name, dy, wc, seg, add=None):
    off, w, tk = seg
    if off == 0 and w <= MAX_K_TILE:
        tk = w
    t_len = dy.shape[0]
    d = wc.shape[1]
    tm = _tile(t_len, (1024, 512, 256, 128))
    tn = _tile(d, (1024, 512, 256, 128))
    kps = w // tk
    ob = off // tk
    grid = (t_len // tm, d // tn, N_CHIPS * kps)
    return _matmul(
        name, dy, wc, NT, grid,
        pl.BlockSpec((tm, tk), lambda i, j, k: (i, k)),
        pl.BlockSpec((None, tn, tk), lambda i, j, k: (k // kps, j, ob + k % kps)),
        pl.BlockSpec((tm, tn), lambda i, j, k: (i, j)),
        jax.ShapeDtypeStruct((t_len, d), F32), (tm, tn),
        add=add, add_spec=None if add is None else pl.BlockSpec((tm, tn), lambda i, j, k: (i, j)))


def _mm_grad_wc(name, a, dy, seg):
    _, w, tn = seg
    t_len, d = a.shape
    tm = _tile(d, (1024, 512, 256, 128))
    tk = _tile(t_len, (2048, 1024, 512, 256, 128))
    nps = w // tn
    grid = (d // tm, N_CHIPS * nps, t_len // tk)
    return _matmul(
        name, a, dy, TN, grid,
        pl.BlockSpec((tk, tm), lambda i, j, k: (k, i)),
        pl.BlockSpec((tk, tn), lambda i, j, k: (k, j)),
        pl.BlockSpec((None, tm, tn), lambda i, j, k: (j // nps, i, j % nps)),
        jax.ShapeDtypeStruct((N_CHIPS, d, w), BF16), (tm, tn))


def _is_plain(wr, seg):
    return seg[0] == 0 and wr.shape[1] == seg[1] and (N_CHIPS * seg[1]) % 1024 == 0


def _mm_act_wr(name, a, wr, seg, add):
    off, r, tk = seg
    t_len = a.shape[0]
    d = wr.shape[2]
    if _is_plain(wr, seg):
        return _mm_plain(name, a, wr.reshape(N_CHIPS * r, d), NN, F32, add=add)
    tm = _tile(t_len, (1024, 512, 256, 128))
    tn = _tile(d, (1024, 512, 256, 128))
    kps = r // tk
    ob = off // tk
    grid = (t_len // tm, d // tn, N_CHIPS * kps)
    return _matmul(
        name, a, wr, NN, grid,
        pl.BlockSpec((tm, tk), lambda i, j, k: (i, k)),
        pl.BlockSpec((None, tk, tn), lambda i, j, k: (k // kps, ob + k % kps, j)),
        pl.BlockSpec((tm, tn), lambda i, j, k: (i, j)),
        jax.ShapeDtypeStruct((t_len, d), F32), (tm, tn),
        add=add, add_spec=pl.BlockSpec((tm, tn), lambda i, j, k: (i, j)))


def _mm_dact_wrT(name, dh, wr, seg):
    off, r, tn = seg
    t_len, d = dh.shape
    if _is_plain(wr, seg):
        return _mm_plain(name, dh, wr.reshape(N_CHIPS * r, d), NT, BF16)
    tm = _tile(t_len, (1024, 512, 256, 128))
    nps = r // tn
    ob = off // tn
    grid = (t_len // tm, N_CHIPS * nps, 1)
    return _matmul(
        name, dh, wr, NT, grid,
        pl.BlockSpec((tm, d), lambda i, j, k: (i, 0)),
        pl.BlockSpec((None, tn, d), lambda i, j, k: (j // nps, ob + j % nps, 0)),
        pl.BlockSpec((tm, tn), lambda i, j, k: (i, j)),
        jax.ShapeDtypeStruct((t_len, N_CHIPS * r), BF16), (tm, tn))


def _mm_grad_wr(name, a, dh, seg):
    _, r, tm = seg
    t_len, d = dh.shape
    if (N_CHIPS * r) % 1024 == 0:
        return _mm_plain(name, a, dh, TN, BF16).reshape(N_CHIPS, r, d)
    tn = _tile(d, (1024, 512, 256, 128))
    tk = _tile(t_len, (2048, 1024, 512, 256, 128))
    mps = r // tm
    grid = (N_CHIPS * mps, d // tn, t_len // tk)
    return _matmul(
        name, a, dh, TN, grid,
        pl.BlockSpec((tk, tm), lambda i, j, k: (k, i)),
        pl.BlockSpec((tk, tn), lambda i, j, k: (k, j)),
        pl.BlockSpec((None, tm, tn), lambda i, j, k: (i // mps, i % mps, j)),
        jax.ShapeDtypeStruct((N_CHIPS, r, d), BF16), (tm, tn))


def _mm_plain(name, a, b, dims, out_dtype, add=None):
    if dims == NN:
        m, kd = a.shape
        n = b.shape[1]
    elif dims == NT:
        m, kd = a.shape
        n = b.shape[0]
    else:
        kd, m = a.shape
        n = b.shape[1]
    tm = _tile(m, (1024, 512, 256, 128))
    tn = _tile(n, (1024, 768, 512, 256, 128))
    tk = _tile(kd, (2048, 1024, 512, 256, 128))
    grid = (m // tm, n // tn, kd // tk)
    if dims == NN:
        a_spec = pl.BlockSpec((tm, tk), lambda i, j, k: (i, k))
        b_spec = pl.BlockSpec((tk, tn), lambda i, j, k: (k, j))
    elif dims == NT:
        a_spec = pl.BlockSpec((tm, tk), lambda i, j, k: (i, k))
        b_spec = pl.BlockSpec((tn, tk), lambda i, j, k: (j, k))
    else:
        a_spec = pl.BlockSpec((tk, tm), lambda i, j, k: (k, i))
        b_spec = pl.BlockSpec((tk, tn), lambda i, j, k: (k, j))
    o_spec = pl.BlockSpec((tm, tn), lambda i, j, k: (i, j))
    return _matmul(name, a, b, dims, grid, a_spec, b_spec, o_spec, jax.ShapeDtypeStruct((m, n), out_dtype), (tm, tn),
                   add=add, add_spec=None if add is None else o_spec)


def _rms_fwd(name, x, g):
    t_len, d = x.shape
    tm = _tile(t_len, (512, 256, 128))

    def body(x_ref, g_ref, o_ref):
        xv = x_ref[...]
        r = lax.rsqrt(jnp.mean(xv * xv, axis=-1, keepdims=True) + EPS)
        o_ref[...] = (xv * r * g_ref[...]).astype(o_ref.dtype)

    return pl.pallas_call(
        body, name=name, grid=(t_len // tm,),
        in_specs=[pl.BlockSpec((tm, d), lambda i: (i, 0)), pl.BlockSpec((1, d), lambda i: (0, 0))],
        out_specs=pl.BlockSpec((tm, d), lambda i: (i, 0)),
        out_shape=jax.ShapeDtypeStruct((t_len, d), BF16), compiler_params=_params(1))(x, g)


def _rms_bwd(name, dy, x, g, dres):
    t_len, d = x.shape
    tm = _tile(t_len, (256, 128))

    def body(dy_ref, x_ref, g_ref, dres_ref, dx_ref, dg_ref):
        xv = x_ref[...]
        r = lax.rsqrt(jnp.mean(xv * xv, axis=-1, keepdims=True) + EPS)
        xhat = xv * r
        dyv = dy_ref[...].astype(F32)
        dxn = dyv * g_ref[...]
        dx = r * (dxn - xhat * jnp.mean(dxn * xhat, axis=-1, keepdims=True))
        dx_ref[...] = dres_ref[...] + dx
        part = jnp.sum(dyv * xhat, axis=0, keepdims=True)

        @pl.when(pl.program_id(0) == 0)
        def _():
            dg_ref[...] = part

        @pl.when(pl.program_id(0) > 0)
        def _():
            dg_ref[...] += part

    row = pl.BlockSpec((tm, d), lambda i: (i, 0))
    vec = pl.BlockSpec((1, d), lambda i: (0, 0))
    return pl.pallas_call(
        body, name=name, grid=(t_len // tm,), in_specs=[row, row, vec, row], out_specs=(row, vec),
        out_shape=(jax.ShapeDtypeStruct((t_len, d), F32), jax.ShapeDtypeStruct((1, d), F32)),
        compiler_params=_params(1))(dy, x, g, dres)


def _loss_head(h, g, target):
    t_len, d = h.shape
    tm = _tile(t_len, (256, 128))

    def body(h_ref, g_ref, t_ref, dh_ref, dg_ref, loss_ref):
        xv = h_ref[...]
        gv = g_ref[...]
        r = lax.rsqrt(jnp.mean(xv * xv, axis=-1, keepdims=True) + EPS)
        xhat = xv * r
        err = xhat * gv - t_ref[...]
        dyv = err * (1.0 / d)
        dxn = dyv * gv
        dh_ref[...] = r * (dxn - xhat * jnp.mean(dxn * xhat, axis=-1, keepdims=True))
        part = jnp.sum(dyv * xhat, axis=0, keepdims=True)
        lpart = jnp.zeros((8, LANE), F32) + (0.5 / d) * jnp.sum(err * err)

        @pl.when(pl.program_id(0) == 0)
        def _():
            dg_ref[...] = part
            loss_ref[...] = lpart

        @pl.when(pl.program_id(0) > 0)
        def _():
            dg_ref[...] += part
            loss_ref[...] += lpart

    row = pl.BlockSpec((tm, d), lambda i: (i, 0))
    vec = pl.BlockSpec((1, d), lambda i: (0, 0))
    return pl.pallas_call(
        body, name="loss_head", grid=(t_len // tm,), in_specs=[row, vec, row],
        out_specs=(row, vec, pl.BlockSpec((8, LANE), lambda i: (0, 0))),
        out_shape=(jax.ShapeDtypeStruct((t_len, d), F32), jax.ShapeDtypeStruct((1, d), F32),
                   jax.ShapeDtypeStruct((8, LANE), F32)),
        compiler_params=_params(1))(h, g, target)


def _chunk_row(shape):
    return lax.broadcasted_iota(jnp.int32, shape, 0) % GLA_CHUNK


def _gla_gate_fwd(a, w_a2p, b_a2):
    t_len = a.shape[0]
    kd = w_a2p.shape[1]
    tm = _tile(t_len, (256, 128, 64))

    def body(a_ref, w_ref, b_ref, ga_ref, cum_ref):
        ga = _dot(a_ref[...], w_ref[...].astype(BF16), NN) + b_ref[...]
        ga_ref[...] = ga
        la = (jnp.minimum(ga, 0.0) - jnp.log(1.0 + jnp.exp(-jnp.abs(ga)))) * (1.0 / GATE_NORMALIZER)
        row = _chunk_row(la.shape)
        s = 1
        while s < GLA_CHUNK:
            la = la + jnp.where(row >= s, pltpu.roll(la, s, 0), 0.0)
            s *= 2
        cum_ref[...] = la

    return pl.pallas_call(
        body, name="gla_gate_fwd", grid=(t_len // tm,),
        in_specs=[pl.BlockSpec((tm, A_PAD), lambda i: (i, 0)), pl.BlockSpec((A_PAD, kd), lambda i: (0, 0)),
                  pl.BlockSpec((1, kd), lambda i: (0, 0))],
        out_specs=(pl.BlockSpec((tm, kd), lambda i: (i, 0)), pl.BlockSpec((tm, kd), lambda i: (i, 0))),
        out_shape=(jax.ShapeDtypeStruct((t_len, kd), F32), jax.ShapeDtypeStruct((t_len, kd), F32)),
        compiler_params=_params(1))(a, w_a2p, b_a2)


def _gla_gate_bwd(dcum, ga, a, w_a2p):
    t_len, kd = dcum.shape
    tm = _tile(t_len, (256, 128, 64))

    def body(dc_ref, ga_ref, a_ref, w_ref, da_ref, dw_ref, db_ref):
        x = dc_ref[...]
        row = _chunk_row(x.shape)
        s = 1
        while s < GLA_CHUNK:
            x = x + jnp.where(row < GLA_CHUNK - s, pltpu.roll(x, tm - s, 0), 0.0)
            s *= 2
        dga = x * (1.0 / GATE_NORMALIZER) * _sigmoid(-ga_ref[...])
        dgab = dga.astype(BF16)
        da_ref[...] = _dot(dgab, w_ref[...].astype(BF16), NT).astype(da_ref.dtype)
        dw = _dot(a_ref[...], dgab, TN)
        db = jnp.sum(dga, axis=0, keepdims=True)

        @pl.when(pl.program_id(0) == 0)
        def _():
            dw_ref[...] = dw
            db_ref[...] = db

        @pl.when(pl.program_id(0) > 0)
        def _():
            dw_ref[...] += dw
            db_ref[...] += db

    wide = pl.BlockSpec((tm, kd), lambda i: (i, 0))
    return pl.pallas_call(
        body, name="gla_gate_bwd", grid=(t_len // tm,),
        in_specs=[wide, wide, pl.BlockSpec((tm, A_PAD), lambda i: (i, 0)), pl.BlockSpec((A_PAD, kd), lambda i: (0, 0))],
        out_specs=(pl.BlockSpec((tm, A_PAD), lambda i: (i, 0)), pl.BlockSpec((A_PAD, kd), lambda i: (0, 0)),
                   pl.BlockSpec((1, kd), lambda i: (0, 0))),
        out_shape=(jax.ShapeDtypeStruct((t_len, A_PAD), BF16), jax.ShapeDtypeStruct((A_PAD, kd), F32),
                   jax.ShapeDtypeStruct((1, kd), F32)),
        compiler_params=_params(1))(dcum, ga, a, w_a2p)


GLA_STEP_CHUNKS = 4


def _gla_dims():
    dk = GLA_KEY_DIM // GLA_HEADS
    dv = GLA_VAL_DIM // GLA_HEADS
    return dk, dv


def _gla_fwd(proj, cum):
    t_len = proj.shape[0]
    dk, dv = _gla_dims()
    nc = t_len // GLA_CHUNK
    c = GLA_CHUNK
    scale = dk ** -0.5
    v0 = 2 * GLA_KEY_DIM // dv

    per = _tile(nc, (GLA_STEP_CHUNKS, 2, 1))
    rows = per * c

    def body(q_ref, k_ref, v_ref, cum_ref, o_ref, st_ref, s_scr):
        @pl.when(pl.program_id(1) == 0)
        def _():
            s_scr[...] = jnp.zeros_like(s_scr)

        tri = lax.broadcasted_iota(jnp.int32, (c, c), 0) >= lax.broadcasted_iota(jnp.int32, (c, c), 1)
        for i in range(per):
            rs = slice(i * c, (i + 1) * c)
            cm = cum_ref[rs, :]
            last = cm[c - 1:c, :]
            q = q_ref[rs, :].astype(F32) * scale
            k = k_ref[rs, :].astype(F32)
            v = v_ref[rs, :].astype(BF16)
            qd = (q * jnp.exp(cm)).astype(BF16)
            ki = (k * jnp.exp(-cm)).astype(BF16)
            ke = (k * jnp.exp(last - cm)).astype(BF16)
            sc = jnp.where(tri, _dot(qd, ki, NT), 0.0)
            st = s_scr[...]
            st_ref[i] = st
            o_ref[rs, :] = _dot(sc.astype(BF16), v, NN) + _dot(qd, st.astype(BF16), NT)
            s_scr[...] = st * jnp.exp(last) + _dot(v, ke, TN)

    return pl.pallas_call(
        body, name="gla_fwd", grid=(GLA_HEADS, nc // per),
        in_specs=[pl.BlockSpec((rows, dk), lambda h, n: (n, h)),
                  pl.BlockSpec((rows, dk), lambda h, n: (n, GLA_HEADS + h)),
                  pl.BlockSpec((rows, dv), lambda h, n: (n, v0 + h)),
                  pl.BlockSpec((rows, dk), lambda h, n: (n, h))],
        out_specs=(pl.BlockSpec((rows, dv), lambda h, n: (n, h)),
                   pl.BlockSpec((None, per, dv, dk), lambda h, n: (h, n, 0, 0))),
        out_shape=(jax.ShapeDtypeStruct((t_len, GLA_VAL_DIM), F32),
                   jax.ShapeDtypeStruct((GLA_HEADS, nc, dv, dk), F32)),
        scratch_shapes=[pltpu.VMEM((dv, dk), F32)], compiler_params=_params(2))(proj, proj, proj, cum)


def _gla_bwd(proj, cum, states, do):
    t_len = proj.shape[0]
    dk, dv = _gla_dims()
    nc = t_len // GLA_CHUNK
    c = GLA_CHUNK
    scale = dk ** -0.5
    v0 = 2 * GLA_KEY_DIM // dv

    per = _tile(nc, (GLA_STEP_CHUNKS, 2, 1))
    rows = per * c

    def body(q_ref, k_ref, v_ref, cum_ref, st_ref, do_ref, dq_ref, dk_ref, dv_ref, dc_ref, ds_scr):
        @pl.when(pl.program_id(1) == 0)
        def _():
            ds_scr[...] = jnp.zeros_like(ds_scr)

        tri = lax.broadcasted_iota(jnp.int32, (c, c), 0) >= lax.broadcasted_iota(jnp.int32, (c, c), 1)
        row = lax.broadcasted_iota(jnp.int32, (c, dk), 0)
        for i in reversed(range(per)):
            rs = slice(i * c, (i + 1) * c)
            cm = cum_ref[rs, :]
            last = cm[c - 1:c, :]
            e_c = jnp.exp(cm)
            e_nc = jnp.exp(-cm)
            e_lc = jnp.exp(last - cm)
            e_l = jnp.exp(last)
            q = q_ref[rs, :].astype(F32) * scale
            k = k_ref[rs, :].astype(F32)
            v = v_ref[rs, :].astype(BF16)
            dov = do_ref[rs, :]
            qd32 = q * e_c
            ki32 = k * e_nc
            ke32 = k * e_lc
            qd = qd32.astype(BF16)
            ki = ki32.astype(BF16)
            ke = ke32.astype(BF16)
            st = st_ref[i]
            dst = ds_scr[...]
            dstb = dst.astype(BF16)
            am = jnp.where(tri, _dot(dov, v, NT), 0.0).astype(BF16)
            pm = jnp.where(tri, _dot(qd, ki, NT), 0.0).astype(BF16)
            dqd = _dot(am, ki, NN) + _dot(dov, st.astype(BF16), NN)
            dki = _dot(am, qd, TN)
            dvv = _dot(pm, dov, TN) + _dot(ke, dstb, NT)
            dke = _dot(v, dstb, NN)
            d_el = jnp.sum(dst * st, axis=0, keepdims=True)
            ds_scr[...] = dst * e_l + _dot(dov, qd, TN)
            dq_ref[rs, :] = (dqd * scale * e_c).astype(dq_ref.dtype)
            dk_ref[rs, :] = (dki * e_nc + dke * e_lc).astype(dk_ref.dtype)
            dv_ref[rs, :] = dvv.astype(dv_ref.dtype)
            dkeke = dke * ke32
            dcum = dqd * qd32 - dki * ki32 - dkeke
            dlast = jnp.sum(dkeke, axis=0, keepdims=True) + d_el * e_l
            dc_ref[rs, :] = jnp.where(row == c - 1, dcum + dlast, dcum)

    rev = nc // per - 1
    return pl.pallas_call(
        body, name="gla_bwd", grid=(GLA_HEADS, nc // per),
        in_specs=[pl.BlockSpec((rows, dk), lambda h, n: (rev - n, h)),
                  pl.BlockSpec((rows, dk), lambda h, n: (rev - n, GLA_HEADS + h)),
                  pl.BlockSpec((rows, dv), lambda h, n: (rev - n, v0 + h)),
                  pl.BlockSpec((rows, dk), lambda h, n: (rev - n, h)),
                  pl.BlockSpec((None, per, dv, dk), lambda h, n: (h, rev - n, 0, 0)),
                  pl.BlockSpec((rows, dv), lambda h, n: (rev - n, h))],
        out_specs=(pl.BlockSpec((rows, dk), lambda h, n: (rev - n, h)),
                   pl.BlockSpec((rows, dk), lambda h, n: (rev - n, h)),
                   pl.BlockSpec((rows, dv), lambda h, n: (rev - n, h)),
                   pl.BlockSpec((rows, dk), lambda h, n: (rev - n, h))),
        out_shape=(jax.ShapeDtypeStruct((t_len, GLA_KEY_DIM), BF16), jax.ShapeDtypeStruct((t_len, GLA_KEY_DIM), BF16),
                   jax.ShapeDtypeStruct((t_len, GLA_VAL_DIM), BF16), jax.ShapeDtypeStruct((t_len, GLA_KEY_DIM), F32)),
        scratch_shapes=[pltpu.VMEM((dv, dk), F32)], compiler_params=_params(2))(proj, proj, proj, cum, states, do)


def _gla_out_fwd(o, proj, gn):
    t_len = o.shape[0]
    _, dv = _gla_dims()
    tm = _tile(t_len, (512, 256, 128))
    r0 = (2 * GLA_KEY_DIM + GLA_VAL_DIM) // dv

    def body(o_ref, r_ref, g_ref, y_ref):
        ov = o_ref[...]
        rs = lax.rsqrt(jnp.mean(ov * ov, axis=-1, keepdims=True) + EPS)
        rv = r_ref[...].astype(F32)
        y_ref[...] = (ov * rs * g_ref[...] * (rv * _sigmoid(rv))).astype(y_ref.dtype)

    return pl.pallas_call(
        body, name="gla_out_fwd", grid=(t_len // tm, GLA_HEADS),
        in_specs=[pl.BlockSpec((tm, dv), lambda i, h: (i, h)), pl.BlockSpec((tm, dv), lambda i, h: (i, r0 + h)),
                  pl.BlockSpec((1, dv), lambda i, h: (0, 0))],
        out_specs=pl.BlockSpec((tm, dv), lambda i, h: (i, h)),
        out_shape=jax.ShapeDtypeStruct((t_len, GLA_VAL_DIM), BF16), compiler_params=_params(2))(o, proj, gn)


def _gla_out_bwd(dy, o, proj, gn):
    t_len = o.shape[0]
    _, dv = _gla_dims()
    tm = _tile(t_len, (512, 256, 128))
    r0 = (2 * GLA_KEY_DIM + GLA_VAL_DIM) // dv

    def body(dy_ref, o_ref, r_ref, g_ref, do_ref, dr_ref, dg_ref):
        ov = o_ref[...]
        gv = g_ref[...]
        rs = lax.rsqrt(jnp.mean(ov * ov, axis=-1, keepdims=True) + EPS)
        xhat = ov * rs
        rv = r_ref[...].astype(F32)
        sg = _sigmoid(rv)
        gate = rv * sg
        dyv = dy_ref[...].astype(F32)
        dn = dyv * gate
        dr_ref[...] = (dyv * xhat * gv * (sg * (1.0 + rv * (1.0 - sg)))).astype(dr_ref.dtype)
        dxn = dn * gv
        do_ref[...] = (rs * (dxn - xhat * jnp.mean(dxn * xhat, axis=-1, keepdims=True))).astype(do_ref.dtype)
        part = jnp.sum(dn * xhat, axis=0, keepdims=True)
        first = (pl.program_id(0) == 0) & (pl.program_id(1) == 0)

        @pl.when(first)
        def _():
            dg_ref[...] = part

        @pl.when(jnp.logical_not(first))
        def _():
            dg_ref[...] += part

    blk = pl.BlockSpec((tm, dv), lambda i, h: (i, h))
    return pl.pallas_call(
        body, name="gla_out_bwd", grid=(t_len // tm, GLA_HEADS),
        in_specs=[blk, blk, pl.BlockSpec((tm, dv), lambda i, h: (i, r0 + h)), pl.BlockSpec((1, dv), lambda i, h: (0, 0))],
        out_specs=(blk, blk, pl.BlockSpec((1, dv), lambda i, h: (0, 0))),
        out_shape=(jax.ShapeDtypeStruct((t_len, GLA_VAL_DIM), BF16), jax.ShapeDtypeStruct((t_len, GLA_VAL_DIM), BF16),
                   jax.ShapeDtypeStruct((1, dv), F32)),
        compiler_params=_params(2))(dy, o, proj, gn)


def _alibi_slopes():
    n = ATT_HEADS
    start = 2.0 ** (-8.0 / n)
    return [start ** (i + 1) for i in range(n)]


def _att_masks(d):
    b = ATT_BLOCK
    qa = lax.broadcasted_iota(jnp.int32, (b, b), 0)
    kb = lax.broadcasted_iota(jnp.int32, (b, b), 1)
    dist_c = qa - kb
    dist_p = qa - kb + b
    return dist_c >= 0, dist_p <= b, (dist_c * d).astype(F32), (dist_p * d).astype(F32)


def _att_fwd(q_all, kv, g):
    d = DILATIONS[g]
    assert WINDOWS[g] // d == ATT_BLOCK
    t_len = q_all.shape[0]
    hd = ATT_HEADS * HEAD_DIM
    sub = t_len // d
    nb = sub // ATT_BLOCK
    b = ATT_BLOCK
    e = HEAD_DIM
    scale = e ** -0.5
    slopes = _alibi_slopes()
    qv = q_all.reshape(sub, d * 3 * hd)
    kvv = kv.reshape(sub, d * 2 * hd)

    def body(q_ref, kp_ref, kc_ref, vp_ref, vc_ref, o_ref, l_ref, s_scr, p_scr, li_scr):
        ib = pl.program_id(1)
        valid_c, valid_p0, dist_c, dist_p = _att_masks(d)
        valid_p = valid_p0 & (ib > 0)
        for h in range(ATT_HEADS):
            hs = slice(h * e, (h + 1) * e)
            qh = q_ref[:, hs]
            s_scr[h, 0] = _dot(qh, kc_ref[:, hs], NT)
            s_scr[h, 1] = _dot(qh, kp_ref[:, hs], NT)
        l_ref[...] = jnp.zeros_like(l_ref)
        for h in range(ATT_HEADS):
            s_c = jnp.where(valid_c, s_scr[h, 0] * scale - slopes[h] * dist_c, NEG)
            s_p = jnp.where(valid_p, s_scr[h, 1] * scale - slopes[h] * dist_p, NEG)
            m = jnp.maximum(jnp.max(s_c, axis=1, keepdims=True), jnp.max(s_p, axis=1, keepdims=True))
            p_c = jnp.where(valid_c, jnp.exp(s_c - m), 0.0)
            p_p = jnp.where(valid_p, jnp.exp(s_p - m), 0.0)
            l = jnp.sum(p_c, axis=1, keepdims=True) + jnp.sum(p_p, axis=1, keepdims=True)
            p_scr[h, 0] = p_c.astype(BF16)
            p_scr[h, 1] = p_p.astype(BF16)
            li_scr[:, h:h + 1] = 1.0 / l
            l_ref[:, h:h + 1] = m + jnp.log(l)
        for h in range(ATT_HEADS):
            hs = slice(h * e, (h + 1) * e)
            acc = _dot(p_scr[h, 0], vc_ref[:, hs], NN) + _dot(p_scr[h, 1], vp_ref[:, hs], NN)
            o_ref[:, hs] = acc * li_scr[:, h:h + 1]

    blk = (b, hd)
    cblk = (b, LANE)
    o, lse = pl.pallas_call(
        body, name=f"att_fwd{g}", grid=(d, nb),
        scratch_shapes=[pltpu.VMEM((ATT_HEADS, 2, b, b), F32), pltpu.VMEM((ATT_HEADS, 2, b, b), BF16),
                        pltpu.VMEM((b, LANE), F32)],
        in_specs=[pl.BlockSpec(blk, lambda r, i: (i, 3 * r + g)),
                  pl.BlockSpec(blk, lambda r, i: (jnp.maximum(i - 1, 0), 2 * r)),
                  pl.BlockSpec(blk, lambda r, i: (i, 2 * r)),
                  pl.BlockSpec(blk, lambda r, i: (jnp.maximum(i - 1, 0), 2 * r + 1)),
                  pl.BlockSpec(blk, lambda r, i: (i, 2 * r + 1))],
        out_specs=(pl.BlockSpec(blk, lambda r, i: (i, r)), pl.BlockSpec(cblk, lambda r, i: (i, r))),
        out_shape=(jax.ShapeDtypeStruct((sub, d * hd), F32), jax.ShapeDtypeStruct((sub, d * LANE), F32)),
        compiler_params=_params(2))(qv, kvv, kvv, kvv, kvv)
    return o.reshape(t_len, hd), lse.reshape(t_len, LANE)


def _att_merge(os, ls):
    t_len, hd = os[0].shape
    tm = _tile(t_len, (256, 128))
    e = HEAD_DIM

    def body(o0, o1, o2, l0, l1, l2, of_ref, ob_ref, l_ref):
        a0, a1, a2 = l0[...], l1[...], l2[...]
        m = jnp.maximum(jnp.maximum(a0, a1), a2)
        e0, e1, e2 = jnp.exp(a0 - m), jnp.exp(a1 - m), jnp.exp(a2 - m)
        den = e0 + e1 + e2
        w0, w1, w2 = e0 / den, e1 / den, e2 / den
        l_ref[...] = m + jnp.log(den)
        for h in range(ATT_HEADS):
            hs = slice(h * e, (h + 1) * e)
            c = slice(h, h + 1)
            o = w0[:, c] * o0[:, hs] + w1[:, c] * o1[:, hs] + w2[:, c] * o2[:, hs]
            of_ref[:, hs] = o
            ob_ref[:, hs] = o.astype(ob_ref.dtype)

    row = pl.BlockSpec((tm, hd), lambda i: (i, 0))
    crow = pl.BlockSpec((tm, LANE), lambda i: (i, 0))
    return pl.pallas_call(
        body, name="att_merge", grid=(t_len // tm,), in_specs=[row] * 3 + [crow] * 3, out_specs=(row, row, crow),
        out_shape=(jax.ShapeDtypeStruct((t_len, hd), F32), jax.ShapeDtypeStruct((t_len, hd), BF16),
                   jax.ShapeDtypeStruct((t_len, LANE), F32)),
        compiler_params=_params(1))(*os, *ls)


def _att_delta(do, o):
    t_len, hd = o.shape
    tm = _tile(t_len, (256, 128))
    e = HEAD_DIM

    def body(do_ref, o_ref, d_ref):
        d_ref[...] = jnp.zeros_like(d_ref)
        for h in range(ATT_HEADS):
            hs = slice(h * e, (h + 1) * e)
            d_ref[:, h:h + 1] = jnp.sum(do_ref[:, hs].astype(F32) * o_ref[:, hs], axis=1, keepdims=True)

    row = pl.BlockSpec((tm, hd), lambda i: (i, 0))
    return pl.pallas_call(
        body, name="att_delta", grid=(t_len // tm,), in_specs=[row, row],
        out_specs=pl.BlockSpec((tm, LANE), lambda i: (i, 0)),
        out_shape=jax.ShapeDtypeStruct((t_len, LANE), F32), compiler_params=_params(1))(do, o)


def _att_bwd(q_all, kv, delta, lse, do, g):
    d = DILATIONS[g]
    t_len = q_all.shape[0]
    hd = ATT_HEADS * HEAD_DIM
    sub = t_len // d
    nb = sub // ATT_BLOCK
    b = ATT_BLOCK
    e = HEAD_DIM
    scale = e ** -0.5
    slopes = _alibi_slopes()
    qv = q_all.reshape(sub, d * 3 * hd)
    kvv = kv.reshape(sub, d * 2 * hd)
    dlv = delta.reshape(sub, d * LANE)
    lv = lse.reshape(sub, d * LANE)
    dov = do.reshape(sub, d * hd)

    def body(qj_ref, qn_ref, kp_ref, kc_ref, vp_ref, vc_ref, doj_ref, don_ref, dj_ref, dn_ref, lj_ref, ln_ref,
             dq_ref, dk_ref, dv_ref, s_scr, dp_scr, p_scr, ds_scr):
        j = pl.program_id(1)
        valid_c, valid_p0, dist_c, dist_p = _att_masks(d)
        valid = (valid_c, valid_p0 & (j > 0), valid_p0 & (j + 1 < nb))
        dist = (dist_c, dist_p, dist_p)
        for h in range(ATT_HEADS):
            hs = slice(h * e, (h + 1) * e)
            qj, qn = qj_ref[:, hs], qn_ref[:, hs]
            kc, kp = kc_ref[:, hs], kp_ref[:, hs]
            vc, vp = vc_ref[:, hs], vp_ref[:, hs]
            doj, don = doj_ref[:, hs], don_ref[:, hs]
            s_scr[h, 0] = _dot(qj, kc, NT)
            s_scr[h, 1] = _dot(qj, kp, NT)
            s_scr[h, 2] = _dot(qn, kc, NT)
            dp_scr[h, 0] = _dot(doj, vc, NT)
            dp_scr[h, 1] = _dot(doj, vp, NT)
            dp_scr[h, 2] = _dot(don, vc, NT)
        for h in range(ATT_HEADS):
            c = slice(h, h + 1)
            lse_t = (lj_ref[:, c], lj_ref[:, c], ln_ref[:, c])
            dlt_t = (dj_ref[:, c], dj_ref[:, c], dn_ref[:, c])
            for t in range(3):
                s = s_scr[h, t] * scale - slopes[h] * dist[t]
                p = jnp.where(valid[t], jnp.exp(jnp.where(valid[t], s - lse_t[t], NEG)), 0.0)
                p_scr[h, t] = p.astype(BF16)
                ds_scr[h, t] = (p * (dp_scr[h, t] - dlt_t[t])).astype(BF16)
        for h in range(ATT_HEADS):
            hs = slice(h * e, (h + 1) * e)
            dq = _dot(ds_scr[h, 0], kc_ref[:, hs], NN) + _dot(ds_scr[h, 1], kp_ref[:, hs], NN)
            dk = _dot(ds_scr[h, 0], qj_ref[:, hs], TN) + _dot(ds_scr[h, 2], qn_ref[:, hs], TN)
            dv = _dot(p_scr[h, 0], doj_ref[:, hs], TN) + _dot(p_scr[h, 2], don_ref[:, hs], TN)
            dq_ref[:, hs] = (dq * scale).astype(dq_ref.dtype)
            dk_ref[:, hs] = (dk * scale).astype(dk_ref.dtype)
            dv_ref[:, hs] = dv.astype(dv_ref.dtype)

    blk = (b, hd)
    cblk = (b, LANE)
    nxt = lambda i: jnp.minimum(i + 1, nb - 1)
    prv = lambda i: jnp.maximum(i - 1, 0)
    tiles = (ATT_HEADS, 3, b, b)
    dq, dk, dv = pl.pallas_call(
        body, name=f"att_bwd{g}", grid=(d, nb),
        scratch_shapes=[pltpu.VMEM(tiles, F32), pltpu.VMEM(tiles, F32), pltpu.VMEM(tiles, BF16), pltpu.VMEM(tiles, BF16)],
        in_specs=[pl.BlockSpec(blk, lambda r, i: (i, 3 * r + g)),
                  pl.BlockSpec(blk, lambda r, i: (nxt(i), 3 * r + g)),
                  pl.BlockSpec(blk, lambda r, i: (prv(i), 2 * r)),
                  pl.BlockSpec(blk, lambda r, i: (i, 2 * r)),
                  pl.BlockSpec(blk, lambda r, i: (prv(i), 2 * r + 1)),
                  pl.BlockSpec(blk, lambda r, i: (i, 2 * r + 1)),
                  pl.BlockSpec(blk, lambda r, i: (i, r)),
                  pl.BlockSpec(blk, lambda r, i: (nxt(i), r)),
                  pl.BlockSpec(cblk, lambda r, i: (i, r)),
                  pl.BlockSpec(cblk, lambda r, i: (nxt(i), r)),
                  pl.BlockSpec(cblk, lambda r, i: (i, r)),
                  pl.BlockSpec(cblk, lambda r, i: (nxt(i), r))],
        out_specs=(pl.BlockSpec(blk, lambda r, i: (i, r)),) * 3,
        out_shape=(jax.ShapeDtypeStruct((sub, d * hd), BF16),) * 3,
        compiler_params=_params(2))(qv, qv, kvv, kvv, kvv, kvv, dov, dov, dlv, dlv, lv, lv)
    return dq.reshape(t_len, hd), dk.reshape(t_len, hd), dv.reshape(t_len, hd)


def _kv_grad_sum(dks, dvs):
    t_len, hd = dks[0].shape
    tm = _tile(t_len, (256, 128))

    def body(k0, k1, k2, v0, v1, v2, o_ref):
        o_ref[:, :hd] = (k0[...].astype(F32) + k1[...].astype(F32) + k2[...].astype(F32)).astype(o_ref.dtype)
        o_ref[:, hd:] = (v0[...].astype(F32) + v1[...].astype(F32) + v2[...].astype(F32)).astype(o_ref.dtype)

    row = pl.BlockSpec((tm, hd), lambda i: (i, 0))
    return pl.pallas_call(
        body, name="kv_grad_sum", grid=(t_len // tm,), in_specs=[row] * 6,
        out_specs=pl.BlockSpec((tm, 2 * hd), lambda i: (i, 0)),
        out_shape=jax.ShapeDtypeStruct((t_len, 2 * hd), BF16), compiler_params=_params(1))(*dks, *dvs)


HALO = 16
INV_SQRT2 = 1.0 / math.sqrt(2.0)
INV_SQRT2PI = 1.0 / math.sqrt(2.0 * math.pi)


def _conv_taps(g, halo, cw, cb):
    row = lax.broadcasted_iota(jnp.int32, g.shape, 0)
    h1 = halo[HALO - 1:HALO, :]
    h2 = halo[HALO - 2:HALO - 1, :]
    g1 = jnp.where(row == 0, h1, pltpu.roll(g, 1, 0))
    g2 = jnp.where(row == 0, h2, jnp.where(row == 1, h1, pltpu.roll(g, 2, 0)))
    gc = cw[0:1, :] * g2 + cw[1:2, :] * g1 + cw[2:3, :] * g + cb
    return gc, g1, g2


def _glu_specs(t_len, f, tm, tc):
    nj = f // tc
    hb = tm // HALO
    u = pl.BlockSpec((tm, tc), lambda j, i: (i, j))
    g = pl.BlockSpec((tm, tc), lambda j, i: (i, nj + j))
    gh = pl.BlockSpec((HALO, tc), lambda j, i: (jnp.maximum(i * hb - 1, 0), nj + j))
    cw = pl.BlockSpec((8, tc), lambda j, i: (0, j))
    cb = pl.BlockSpec((1, tc), lambda j, i: (0, j))
    return u, g, gh, cw, cb


def _glu_fwd(name, up, cw, cb):
    t_len = up.shape[0]
    f = up.shape[1] // 2
    tm = _tile(t_len, (512, 256, 128))
    tc = _tile(f, (1408, 1024, 512, 256, 128))
    u_s, g_s, gh_s, cw_s, cb_s = _glu_specs(t_len, f, tm, tc)

    def body(u_ref, g_ref, gh_ref, cw_ref, cb_ref, o_ref, phi_ref):
        first = pl.program_id(1) == 0
        halo = jnp.where(first, 0.0, gh_ref[...].astype(F32))
        gc, _, _ = _conv_taps(g_ref[...].astype(F32), halo, cw_ref[...], cb_ref[...])
        phi = 0.5 * (1.0 + lax.erf(gc * INV_SQRT2))
        phi_ref[...] = phi
        o_ref[...] = (gc * phi * u_ref[...].astype(F32)).astype(o_ref.dtype)

    tile = pl.BlockSpec((tm, tc), lambda j, i: (i, j))
    return pl.pallas_call(
        body, name=name, grid=(f // tc, t_len // tm), in_specs=[u_s, g_s, gh_s, cw_s, cb_s], out_specs=(tile, tile),
        out_shape=(jax.ShapeDtypeStruct((t_len, f), BF16), jax.ShapeDtypeStruct((t_len, f), F32)),
        compiler_params=_params(2))(up, up, up, cw, cb)


def _glu_bwd_a(name, dact, up, phi, cw, cb):
    t_len = up.shape[0]
    f = up.shape[1] // 2
    tm = _tile(t_len, (256, 128))
    tc = _tile(f, (1408, 1024, 512, 256, 128))
    u_s, g_s, gh_s, cw_s, cb_s = _glu_specs(t_len, f, tm, tc)

    def body(da_ref, phi_ref, u_ref, g_ref, gh_ref, cw_ref, cb_ref, du_ref, dgc_ref, w0_ref, w1_ref, w2_ref, b_ref):
        first = pl.program_id(1) == 0
        halo = jnp.where(first, 0.0, gh_ref[...].astype(F32))
        g = g_ref[...].astype(F32)
        gc, g1, g2 = _conv_taps(g, halo, cw_ref[...], cb_ref[...])
        phi = phi_ref[...]
        dgel = phi + gc * jnp.exp(-0.5 * gc * gc) * INV_SQRT2PI
        da = da_ref[...].astype(F32)
        du_ref[...] = (da * gc * phi).astype(du_ref.dtype)
        dgc = da * u_ref[...].astype(F32) * dgel
        dgc_ref[...] = dgc.astype(dgc_ref.dtype)
        parts = (jnp.sum(dgc * g2, axis=0, keepdims=True), jnp.sum(dgc * g1, axis=0, keepdims=True),
                 jnp.sum(dgc * g, axis=0, keepdims=True), jnp.sum(dgc, axis=0, keepdims=True))
        refs = (w0_ref, w1_ref, w2_ref, b_ref)

        @pl.when(first)
        def _():
            for r, p in zip(refs, parts):
                r[...] = p

        @pl.when(jnp.logical_not(first))
        def _():
            for r, p in zip(refs, parts):
                r[...] += p

    tile = pl.BlockSpec((tm, tc), lambda j, i: (i, j))
    vec = pl.BlockSpec((1, tc), lambda j, i: (0, j))
    vshape = jax.ShapeDtypeStruct((1, f), F32)
    return pl.pallas_call(
        body, name=name, grid=(f // tc, t_len // tm), in_specs=[tile, tile, u_s, g_s, gh_s, cw_s, cb_s],
        out_specs=(tile, tile, vec, vec, vec, vec),
        out_shape=(jax.ShapeDtypeStruct((t_len, f), BF16), jax.ShapeDtypeStruct((t_len, f), F32),
                   vshape, vshape, vshape, vshape),
        compiler_params=_params(2))(dact, phi, up, up, up, cw, cb)


def _glu_bwd_b(name, du, dgc, cw):
    t_len, f = du.shape
    tm = _tile(t_len, (128, 64))
    hb = tm // HALO
    n_i = t_len // tm
    last_hb = t_len // HALO - 1

    def body(du_ref, d_ref, dh_ref, cw_ref, o_ref):
        last = pl.program_id(0) == n_i - 1
        halo = jnp.where(last, 0.0, dh_ref[...].astype(F32))
        dd = d_ref[...].astype(F32)
        row = lax.broadcasted_iota(jnp.int32, dd.shape, 0)
        h0 = halo[0:1, :]
        h1 = halo[1:2, :]
        d1 = jnp.where(row == tm - 1, h0, pltpu.roll(dd, tm - 1, 0))
        d2 = jnp.where(row == tm - 1, h1, jnp.where(row == tm - 2, h0, pltpu.roll(dd, tm - 2, 0)))
        cwv = cw_ref[...]
        dg = cwv[2:3, :] * dd + cwv[1:2, :] * d1 + cwv[0:1, :] * d2
        o_ref[:, :f] = du_ref[...]
        o_ref[:, f:] = dg.astype(o_ref.dtype)

    row_s = pl.BlockSpec((tm, f), lambda i: (i, 0))
    return pl.pallas_call(
        body, name=name, grid=(n_i,),
        in_specs=[row_s, row_s, pl.BlockSpec((HALO, f), lambda i: (jnp.minimum((i + 1) * hb, last_hb), 0)),
                  pl.BlockSpec((8, f), lambda i: (0, 0))],
        out_specs=pl.BlockSpec((tm, 2 * f), lambda i: (i, 0)),
        out_shape=jax.ShapeDtypeStruct((t_len, 2 * f), BF16), compiler_params=_params(1))(du, dgc, dgc, cw)


def _adamw(name, w, g, m, v):
    rows, cols = w.shape
    tr = _tile(rows, (256, 128, 64, 32, 16, 8))
    c1 = 1.0 / (1.0 - ADAM_B1 ** ADAM_STEP)
    c2 = 1.0 / (1.0 - ADAM_B2 ** ADAM_STEP)

    def body(w_ref, g_ref, m_ref, v_ref, d_ref, nm_ref, nv_ref):
        gv = g_ref[...]
        nm = ADAM_B1 * m_ref[...] + (1.0 - ADAM_B1) * gv
        nv = ADAM_B2 * v_ref[...] + (1.0 - ADAM_B2) * (gv * gv)
        nm_ref[...] = nm
        nv_ref[...] = nv
        d_ref[...] = -ADAM_LR * ((nm * c1) / (jnp.sqrt(nv * c2) + ADAM_EPS) + ADAM_WD * w_ref[...])

    blk = pl.BlockSpec((tr, cols), lambda i: (i, 0))
    shp = jax.ShapeDtypeStruct((rows, cols), F32)
    return pl.pallas_call(body, name=name, grid=(rows // tr,), in_specs=[blk] * 4, out_specs=(blk,) * 3,
                          out_shape=(shp,) * 3, compiler_params=_params(1))(w, g, m, v)


class _NoComm:
    def __init__(self):
        self.grads = {}

    def need(self, group, ws, after):
        return ws

    def reduce(self, group, grads, carry):
        self.grads.update(grads)
        return carry

    def tick(self, carry):
        return carry


def _local_step(x, target, ws, norms, small, hooks):
    lay = _layout()

    w_main, w_a = _unpack_gin(ws["gin"])
    hn0 = _rms_fwd("rms_attn0", x, norms["attn0"])
    proj = _mm_plain("gla_proj", hn0, w_main, NN, F32)
    a = _mm_plain("gla_proj_a", hn0, w_a, NN, BF16)
    ga, cum = _gla_gate_fwd(a, small["w_a2p"], small["b_a2"])
    o_gla, states = _gla_fwd(proj, cum)
    gated = _gla_out_fwd(o_gla, proj, small["head_norm"])
    h1 = _mm_act_wr("gla_out", gated, ws["gout"], lay["gout"], add=x)

    def ffn_fwd(l, h):
        hn = _rms_fwd(f"rms_ffn{l}", h, norms[f"ffn{l}"])
        up = _mm_act_wc(f"ffn_up{l}", hn, ws[f"up{l}"], lay[f"up{l}"], F32)
        act, phi = _glu_fwd(f"glu_fwd{l}", up, small["conv_w"][l], small["conv_b"][l])
        return hn, (up, phi), act, _mm_act_wr(f"ffn_down{l}", act, ws[f"down{l}"], lay[f"down{l}"], add=h)

    ws = hooks.need("B", ws, h1)
    hnf0, up0, act0, h2 = ffn_fwd(0, h1)

    ws = hooks.need("C", ws, h2)
    kvn = _rms_fwd("rms_kv", h2, norms["kv"])
    kv = _mm_act_wc("kv_proj", kvn, ws["wkv"], lay["wkv"], BF16)
    hn1 = _rms_fwd("rms_attn1", h2, norms["attn1"])
    q_all = _mm_act_wc("q_proj", hn1, ws["wq"], lay["wq"], BF16)
    branch = [_att_fwd(q_all, kv, g) for g in range(3)]
    o_att, o_att_b, lse = _att_merge([br[0] for br in branch], [br[1] for br in branch])
    h3 = _mm_act_wr("att_out", o_att_b, ws["dout"], lay["dout"], add=h2)
    hnf1, up1, act1, h4 = ffn_fwd(1, h3)

    dh4, d_final, loss = _loss_head(h4, norms["final"], target)

    sm = {"final": d_final}

    def ffn_bwd(l, dh, h, hn, up, act):
        big = {}
        dact = _mm_dact_wrT(f"ffn_down_dx{l}", dh, ws[f"down{l}"], lay[f"down{l}"])
        big[f"down{l}"] = _mm_grad_wr(f"ffn_down_dw{l}", act, dh, lay[f"down{l}"])
        up, phi = up
        du, dgc, w0, w1, w2, db = _glu_bwd_a(f"glu_bwd_a{l}", dact, up, phi, small["conv_w"][l], small["conv_b"][l])
        sm[f"conv_w{l}"] = (w0, w1, w2)
        sm[f"conv_b{l}"] = db
        dup = hooks.tick(_glu_bwd_b(f"glu_bwd_b{l}", du, dgc, small["conv_w"][l]))
        dhn = _mm_dact_wcT(f"ffn_up_dx{l}", dup, ws[f"up{l}"], lay[f"up{l}"])
        big[f"up{l}"] = _mm_grad_wc(f"ffn_up_dw{l}", hn, dup, lay[f"up{l}"])
        dh_in, sm[f"ffn{l}"] = _rms_bwd(f"rms_ffn_bwd{l}", dhn, h, norms[f"ffn{l}"], dh)
        return hooks.reduce(f"ffn{l}", big, dh_in)

    dh3 = ffn_bwd(1, dh4, h3, hnf1, up1, act1)

    big = {}
    do_att = _mm_dact_wrT("att_out_dx", dh3, ws["dout"], lay["dout"])
    big["dout"] = _mm_grad_wr("att_out_dw", o_att_b, dh3, lay["dout"])
    delta = _att_delta(do_att, o_att)
    bw = [_att_bwd(q_all, kv, delta, lse, do_att, g) for g in range(3)]
    dq_all = jnp.concatenate([t[0] for t in bw], axis=1)
    dhn1 = _mm_dact_wcT("q_proj_dx", dq_all, ws["wq"], lay["wq"])
    big["wq"] = _mm_grad_wc("q_proj_dw", hn1, dq_all, lay["wq"])
    dh2, sm["attn1"] = _rms_bwd("rms_attn1_bwd", dhn1, h2, norms["attn1"], dh3)
    dkv = hooks.tick(_kv_grad_sum([t[1] for t in bw], [t[2] for t in bw]))
    dkvn = _mm_dact_wcT("kv_proj_dx", dkv, ws["wkv"], lay["wkv"])
    big["wkv"] = _mm_grad_wc("kv_proj_dw", kvn, dkv, lay["wkv"])
    dh2, sm["kv"] = _rms_bwd("rms_kv_bwd", dkvn, h2, norms["kv"], dh2)
    dh2 = hooks.reduce("att", big, dh2)

    dh1 = ffn_bwd(0, dh2, h1, hnf0, up0, act0)

    big = {}
    dgated = _mm_dact_wrT("gla_out_dx", dh1, ws["gout"], lay["gout"])
    big["gout"] = _mm_grad_wr("gla_out_dw", gated, dh1, lay["gout"])
    do_gla, dr, sm["head_norm"] = _gla_out_bwd(dgated, o_gla, proj, small["head_norm"])
    dq, dk, dv, dcum = _gla_bwd(proj, cum, states, hooks.tick(do_gla))
    da, sm["w_a2p"], sm["b_a2"] = _gla_gate_bwd(dcum, ga, a, small["w_a2p"])
    dproj = jnp.concatenate([dq, dk, dv, dr], axis=1)
    dhn0 = _mm_plain("gla_proj_dx", dproj, w_main, NT, F32)
    dhn0 = _mm_plain("gla_proj_a_dx", da, w_a, NT, F32, add=dhn0)
    gin_main = _mm_plain("gla_proj_dw", hn0, dproj, TN, BF16)
    gin_a = _mm_plain("gla_proj_a_dw", hn0, da, TN, BF16)
    big["gin"] = _pack_gin_grad(gin_main, gin_a)
    grad_x, sm["attn0"] = _rms_bwd("rms_attn0_bwd", dhn0, x, norms["attn0"], dh1)
    return loss, grad_x, sm, big


def _pack_weights(chip, names, gla_w_in, gla_w_out, w_kv, dsa_w_q, dsa_w_out, ffn_w_up, ffn_w_down):
    gin = gla_w_in[0]
    gin = jnp.pad(gin, ((0, 0), (0, _roundup(gin.shape[1], LANE) - gin.shape[1])))
    shards = {"gin": gin, "gout": gla_w_out[0], "up0": ffn_w_up[0], "up1": ffn_w_up[1], "down0": ffn_w_down[0],
              "down1": ffn_w_down[1], "wq": dsa_w_q[0], "wkv": w_kv, "dout": dsa_w_out[0]}
    out = {}
    for name in names:
        w = shards[name]
        buf = jnp.zeros((N_CHIPS,) + w.shape, BF16)
        out[name] = lax.dynamic_update_slice(buf, w.astype(BF16)[None], (chip, 0, 0))
    return out


def _unpack_gin(w_gin):
    w = _layout()["gin"][1]
    full = jnp.transpose(w_gin[:, :, :w], (1, 0, 2)).reshape(D_MODEL, N_CHIPS * w)
    n_main = 2 * GLA_KEY_DIM + 2 * GLA_VAL_DIM
    w_a = jnp.pad(full[:, n_main:], ((0, 0), (0, A_PAD - GATE_RANK)))
    return full[:, :n_main], w_a


def _pack_gin_grad(gin_main, gin_a):
    w = _layout()["gin"][1]
    gin = jnp.concatenate([gin_main, gin_a[:, :GATE_RANK]], axis=1)
    gin = jnp.transpose(gin.reshape(D_MODEL, N_CHIPS, w), (1, 0, 2))
    return jnp.pad(gin, ((0, 0), (0, 0), (0, _roundup(w, LANE) - w)))


def _small_params(attn_norm, ffn_norm, kv_norm, final_norm, conv_b, w_a2, b_a2, head_norm, conv_w):
    norms = {"attn0": attn_norm[0:1], "attn1": attn_norm[1:2], "ffn0": ffn_norm[0:1], "ffn1": ffn_norm[1:2],
             "kv": kv_norm[None, :], "final": final_norm[None, :]}
    small = {"w_a2p": jnp.pad(w_a2, ((0, A_PAD - GATE_RANK), (0, 0))), "b_a2": b_a2[None, :],
             "head_norm": head_norm[None, :], "conv_w": jnp.pad(conv_w, ((0, 0), (0, 8 - conv_w.shape[1]), (0, 0))),
             "conv_b": conv_b[:, None, :]}
    return norms, small


ANY = pl.BlockSpec(memory_space=pl.ANY)


def _place():
    return lax.axis_index("x"), lax.axis_index("y"), lax.axis_index("c")


def _other_chips(x, y):
    return [(1 - x, y), (x, 1 - y), (1 - x, 1 - y)]


def _rcopy(src, dst, ssem, rsem, dev):
    return pltpu.make_async_remote_copy(src_ref=src, dst_ref=dst, send_sem=ssem, recv_sem=rsem, device_id=dev,
                                        device_id_type=MESH)


def _pack_shard(name, w, layer, chip_arr, after):
    rows, cols = w.shape[-2:]
    tr = _tile(rows, (512, 352, 256, 128, 64, 32, 16))

    def body(p_ref, w_ref, after_ref, o_ref):
        o_ref[...] = w_ref[...].astype(o_ref.dtype)

    if w.ndim == 3:
        w_spec = pl.BlockSpec((None, tr, cols), lambda i, p: (layer, i, 0))
    else:
        w_spec = pl.BlockSpec((tr, cols), lambda i, p: (i, 0))
    return pl.pallas_call(
        body, name=name,
        grid_spec=pltpu.PrefetchScalarGridSpec(
            num_scalar_prefetch=1, grid=(rows // tr,), in_specs=[w_spec, ANY],
            out_specs=pl.BlockSpec((None, tr, cols), lambda i, p: (p[0], i, 0))),
        out_shape=jax.ShapeDtypeStruct((N_CHIPS, rows, cols), BF16), compiler_params=_params(1))(chip_arr, w, after)


def _swap_halves(name, arrs):
    n = len(arrs)

    def body(*refs):
        ins, outs = refs[:n], refs[n:2 * n]
        send, recv = refs[2 * n:]
        x, y, c = _place()
        cps = []
        for a in range(n):
            h = ins[a].shape[1] // 2
            cp = _rcopy(ins[a].at[:, pl.ds((1 - c) * h, h)], outs[a], send.at[a], recv.at[a], (x, y, 1 - c))
            cp.start()
            cps.append(cp)
        for cp in cps:
            cp.wait()

    return pl.pallas_call(
        body, name=name, in_specs=[ANY] * n, out_specs=[ANY] * n,
        out_shape=[jax.ShapeDtypeStruct((a.shape[0], a.shape[1] // 2, a.shape[2]), a.dtype) for a in arrs],
        scratch_shapes=[pltpu.SemaphoreType.DMA((n,)), pltpu.SemaphoreType.DMA((n,))])(*arrs)


SEM = pl.BlockSpec(memory_space=pltpu.SEMAPHORE)
EFFECT = pltpu.SideEffectType.DATAFLOW_SIDE_EFFECTING


def _shapes(arrs):
    return [jax.ShapeDtypeStruct(a.shape, a.dtype) for a in arrs]


def _gather_start(name, thru, arrs):
    n, nt = len(arrs), len(thru)

    def body(*refs):
        ins = refs[nt:nt + n]
        send, recv = refs[nt + n], refs[nt + n + 1]
        outs = refs[2 * nt + n + 2:]
        x, y, c = _place()
        me = 2 * x + y
        for a in range(n):
            h = ins[a].shape[1] // 2
            mine = pl.ds(c * h, h)
            for j, (px, py) in enumerate(_other_chips(x, y)):
                _rcopy(ins[a].at[me, mine], outs[a].at[me, mine], send.at[3 * a + j], recv.at[3 * a + j], (px, py, c)).start()

    res = pl.pallas_call(
        body, name=name, in_specs=[ANY] * (nt + n), out_specs=[SEM, SEM] + [ANY] * (nt + n),
        out_shape=[pltpu.SemaphoreType.DMA((3 * n,)), pltpu.SemaphoreType.DMA((3 * n,))] + _shapes(thru) + _shapes(arrs),
        input_output_aliases={i: 2 + i for i in range(nt + n)},
        compiler_params=pltpu.CompilerParams(has_side_effects=EFFECT))(*thru, *arrs)
    return res[0], res[1], res[2:2 + nt], res[2 + nt:]


def _gather_wait(name, send, recv, arrs, after):
    n = len(arrs)

    def body(*refs):
        ins = refs[:n]
        send_ref, recv_ref = refs[n], refs[n + 1]
        x, y, c = _place()
        me = 2 * x + y
        for a in range(n):
            h = ins[a].shape[1] // 2
            mine = pl.ds(c * h, h)
            for j, (px, py) in enumerate(_other_chips(x, y)):
                sent = ins[a].at[me, mine]
                landed = ins[a].at[2 * px + py, mine]
                cp = _rcopy(sent, landed, send_ref.at[3 * a + j], recv_ref.at[3 * a + j], (px, py, c))
                cp.wait_send()
                cp.wait_recv()

    return pl.pallas_call(
        body, name=name, in_specs=[ANY] * n + [SEM, SEM, ANY], out_specs=[ANY] * n, out_shape=_shapes(arrs),
        input_output_aliases={a: a for a in range(n)},
        compiler_params=pltpu.CompilerParams(has_side_effects=EFFECT))(*arrs, send, recv, after)


def _forward_halves(name, arrs):
    n = len(arrs)

    def body(*refs):
        ins, outs = refs[:n], refs[n:2 * n]
        send, recv = refs[2 * n:]
        x, y, c = _place()
        sib = (x, y, 1 - c)
        chips = _other_chips(x, y)
        cps = []
        for a in range(n):
            h = ins[a].shape[1] // 2
            mine = pl.ds(c * h, h)
            for j, (px, py) in enumerate(chips):
                cp = _rcopy(ins[a].at[2 * px + py, mine], outs[a].at[2 * px + py, mine], send.at[3 * a + j],
                            recv.at[3 * a + j], sib)
                cp.start()
                cps.append(cp)
        for a in range(n):
            h = ins[a].shape[1] // 2
            theirs = pl.ds((1 - c) * h, h)
            for j, (px, py) in enumerate(chips):
                got = outs[a].at[2 * px + py, theirs]
                _rcopy(got, got, send.at[3 * a + j], recv.at[3 * a + j], sib).wait_recv()
        for cp in cps:
            cp.wait_send()

    return pl.pallas_call(
        body, name=name, in_specs=[ANY] * n, out_specs=[ANY] * n, out_shape=_shapes(arrs),
        input_output_aliases={a: a for a in range(n)},
        scratch_shapes=[pltpu.SemaphoreType.DMA((3 * n,)), pltpu.SemaphoreType.DMA((3 * n,))])(*arrs)


def _scatter_start(name, thru, arrs):
    n, nt = len(arrs), len(thru)
    landing = [jnp.zeros_like(a) for a in arrs]

    def body(*refs):
        ins = refs[nt:nt + n]
        send, recv = refs[nt + 2 * n], refs[nt + 2 * n + 1]
        outs = refs[2 * nt + 3 * n + 2:]
        x, y, c = _place()
        me = 2 * x + y
        for a in range(n):
            for j, (px, py) in enumerate(_other_chips(x, y)):
                _rcopy(ins[a].at[2 * px + py], outs[a].at[me], send.at[3 * a + j], recv.at[3 * a + j], (px, py, c)).start()

    res = pl.pallas_call(
        body, name=name, in_specs=[ANY] * (nt + 2 * n), out_specs=[SEM, SEM] + [ANY] * (nt + 2 * n),
        out_shape=[pltpu.SemaphoreType.DMA((3 * n,)), pltpu.SemaphoreType.DMA((3 * n,))] + _shapes(thru) + _shapes(arrs)
        + _shapes(landing),
        input_output_aliases={i: 2 + i for i in range(nt + 2 * n)},
        compiler_params=pltpu.CompilerParams(has_side_effects=EFFECT))(*thru, *arrs, *landing)
    return res[0], res[1], res[2:2 + nt], res[2 + nt:2 + nt + n], res[2 + nt + n:]


def _scatter_wait(name, send, recv, arrs, landing, after):
    n = len(arrs)

    def body(*refs):
        ins, land = refs[:n], refs[n:2 * n]
        send_ref, recv_ref = refs[2 * n], refs[2 * n + 1]
        x, y, c = _place()
        for a in range(n):
            for j, (px, py) in enumerate(_other_chips(x, y)):
                cp = _rcopy(ins[a].at[2 * px + py], land[a].at[2 * px + py], send_ref.at[3 * a + j], recv_ref.at[3 * a + j],
                            (px, py, c))
                cp.wait_send()
                cp.wait_recv()

    res = pl.pallas_call(
        body, name=name, in_specs=[ANY] * (2 * n) + [SEM, SEM, ANY], out_specs=[ANY] * (2 * n),
        out_shape=_shapes(arrs) + _shapes(landing), input_output_aliases={i: i for i in range(2 * n)},
        compiler_params=pltpu.CompilerParams(has_side_effects=EFFECT))(*arrs, *landing, send, recv, after)
    return res[:n], res[n:]


def _swap_start(name, thru, arrs):
    n, nt = len(arrs), len(thru)
    landing = [lax.empty((a.shape[0], a.shape[1] // 2, a.shape[2]), a.dtype) for a in arrs]

    def body(*refs):
        ins = refs[nt:nt + n]
        send, recv = refs[nt + 2 * n], refs[nt + 2 * n + 1]
        outs = refs[2 * nt + 3 * n + 2:]
        x, y, c = _place()
        for a in range(n):
            h = ins[a].shape[1] // 2
            _rcopy(ins[a].at[:, pl.ds((1 - c) * h, h)], outs[a], send.at[a], recv.at[a], (x, y, 1 - c)).start()

    res = pl.pallas_call(
        body, name=name, in_specs=[ANY] * (nt + 2 * n), out_specs=[SEM, SEM] + [ANY] * (nt + 2 * n),
        out_shape=[pltpu.SemaphoreType.DMA((n,)), pltpu.SemaphoreType.DMA((n,))] + _shapes(thru) + _shapes(arrs)
        + _shapes(landing),
        input_output_aliases={i: 2 + i for i in range(nt + 2 * n)},
        compiler_params=pltpu.CompilerParams(has_side_effects=EFFECT))(*thru, *arrs, *landing)
    return res[0], res[1], res[2:2 + nt], res[2 + nt:2 + nt + n], res[2 + nt + n:]


def _swap_wait(name, send, recv, arrs, landing, after):
    n = len(arrs)

    def body(*refs):
        ins, land = refs[:n], refs[n:2 * n]
        send_ref, recv_ref = refs[2 * n], refs[2 * n + 1]
        x, y, c = _place()
        for a in range(n):
            h = ins[a].shape[1] // 2
            cp = _rcopy(ins[a].at[:, pl.ds((1 - c) * h, h)], land[a], send_ref.at[a], recv_ref.at[a], (x, y, 1 - c))
            cp.wait_send()
            cp.wait_recv()

    res = pl.pallas_call(
        body, name=name, in_specs=[ANY] * (2 * n) + [SEM, SEM, ANY], out_specs=[ANY] * (2 * n),
        out_shape=_shapes(arrs) + _shapes(landing), input_output_aliases={i: i for i in range(2 * n)},
        compiler_params=pltpu.CompilerParams(has_side_effects=EFFECT))(*arrs, *landing, send, recv, after)
    return res[:n], res[n:]


def _join_start(name, arrs):
    n = len(arrs)

    def body(*refs):
        ins = refs[:n]
        send, recv = refs[n], refs[n + 1]
        outs = refs[n + 2:]
        x, y, c = _place()
        for a in range(n):
            h = ins[a].shape[0] // 2
            mine = pl.ds(c * h, h)
            _rcopy(ins[a].at[mine], outs[a].at[mine], send.at[a], recv.at[a], (x, y, 1 - c)).start()

    res = pl.pallas_call(
        body, name=name, in_specs=[ANY] * n, out_specs=[SEM, SEM] + [ANY] * n,
        out_shape=[pltpu.SemaphoreType.DMA((n,)), pltpu.SemaphoreType.DMA((n,))] + _shapes(arrs),
        input_output_aliases={i: 2 + i for i in range(n)},
        compiler_params=pltpu.CompilerParams(has_side_effects=EFFECT))(*arrs)
    return res[0], res[1], res[2:]


def _join_wait(name, send, recv, arrs, after):
    n = len(arrs)

    def body(*refs):
        ins = refs[:n]
        send_ref, recv_ref = refs[n], refs[n + 1]
        x, y, c = _place()
        for a in range(n):
            h = ins[a].shape[0] // 2
            cp = _rcopy(ins[a].at[pl.ds(c * h, h)], ins[a].at[pl.ds((1 - c) * h, h)], send_ref.at[a], recv_ref.at[a],
                        (x, y, 1 - c))
            cp.wait_send()
            cp.wait_recv()

    return pl.pallas_call(
        body, name=name, in_specs=[ANY] * n + [SEM, SEM, ANY], out_specs=[ANY] * n, out_shape=_shapes(arrs),
        input_output_aliases={a: a for a in range(n)},
        compiler_params=pltpu.CompilerParams(has_side_effects=EFFECT))(*arrs, send, recv, after)


def _join_halves(name, arrs):
    n = len(arrs)

    def body(*refs):
        ins, outs = refs[:n], refs[n:2 * n]
        send, recv = refs[2 * n:]
        x, y, c = _place()
        cps = []
        for a in range(n):
            h = ins[a].shape[0] // 2
            mine = pl.ds(c * h, h)
            cp = _rcopy(ins[a].at[mine], outs[a].at[mine], send.at[a], recv.at[a], (x, y, 1 - c))
            cp.start()
            cps.append(cp)
        for a in range(n):
            h = ins[a].shape[0] // 2
            got = outs[a].at[pl.ds((1 - c) * h, h)]
            _rcopy(got, got, send.at[a], recv.at[a], (x, y, 1 - c)).wait_recv()
        for cp in cps:
            cp.wait_send()

    return pl.pallas_call(
        body, name=name, in_specs=[ANY] * n, out_specs=[ANY] * n,
        out_shape=[jax.ShapeDtypeStruct(a.shape, a.dtype) for a in arrs],
        input_output_aliases={a: a for a in range(n)},
        scratch_shapes=[pltpu.SemaphoreType.DMA((n,)), pltpu.SemaphoreType.DMA((n,))])(*arrs)


def _allgather8(name, xs, reduce):
    m_per, n = xs.shape

    def body(x_ref, out_ref, *rest):
        if reduce:
            sum_ref, send, recv, lsem = rest
        else:
            send, recv, lsem = rest
        x, y, c = _place()
        me, sib = (x, y, c), (x, y, 1 - c)
        chips = _other_chips(x, y)

        def rows(px, py, pc):
            return out_ref.at[pl.ds((4 * px + 2 * py + pc) * m_per, m_per), :]

        def copy(k, block, to, src=None):
            return _rcopy(rows(*block) if src is None else src, rows(*block), send.at[k], recv.at[k], to)

        mine = pltpu.make_async_copy(x_ref, rows(*me), lsem)
        mine.start()
        first = [copy(0, me, sib, src=x_ref)]
        first += [copy(1 + j, me, (*chip, c), src=x_ref) for j, chip in enumerate(chips)]
        for cp in first:
            cp.start()
        passed = [copy(4 + j, (*chip, c), sib) for j, chip in enumerate(chips)]
        for j, chip in enumerate(chips):
            copy(1 + j, (*chip, c), me).wait_recv()
            passed[j].start()
        copy(0, sib, me).wait_recv()
        for j, chip in enumerate(chips):
            copy(4 + j, (*chip, 1 - c), me).wait_recv()
        for cp in first + passed:
            cp.wait_send()
        mine.wait()
        if reduce:
            acc = out_ref[pl.ds(0, m_per), :]
            for dev in range(1, N_DEV):
                acc = acc + out_ref[pl.ds(dev * m_per, m_per), :]
            sum_ref[...] = acc

    vm = pl.BlockSpec(memory_space=pltpu.VMEM)
    out_shape = [jax.ShapeDtypeStruct((N_DEV * m_per, n), xs.dtype)]
    if reduce:
        out_shape.append(jax.ShapeDtypeStruct((m_per, n), xs.dtype))
    return pl.pallas_call(
        body, name=name, in_specs=[vm], out_specs=[vm] * len(out_shape), out_shape=out_shape,
        scratch_shapes=[pltpu.SemaphoreType.DMA((7,)), pltpu.SemaphoreType.DMA((7,)), pltpu.SemaphoreType.DMA],
        compiler_params=pltpu.CompilerParams(vmem_limit_bytes=VMEM_LIMIT))(xs)


def _add_my_half(name, a, rb, c_arr):
    s, h, cols = rb.shape
    tr = _tile(h, (512, 352, 256, 128, 64, 32, 16))
    nt = h // tr

    def body(c_ref, a_ref, b_ref, o_ref):
        o_ref[...] = (a_ref[...].astype(F32) + b_ref[...].astype(F32)).astype(o_ref.dtype)

    return pl.pallas_call(
        body, name=name,
        grid_spec=pltpu.PrefetchScalarGridSpec(
            num_scalar_prefetch=1, grid=(s, nt),
            in_specs=[pl.BlockSpec((None, tr, cols), lambda k, i, c: (k, c[0] * nt + i, 0)),
                      pl.BlockSpec((None, tr, cols), lambda k, i, c: (k, i, 0))],
            out_specs=pl.BlockSpec((None, tr, cols), lambda k, i, c: (k, i, 0))),
        out_shape=jax.ShapeDtypeStruct(rb.shape, BF16), compiler_params=_params(2))(c_arr, a, rb)


def _sum_chips(name, own, q, place):
    s, h, cols = q.shape
    tr = _tile(h, (512, 352, 256, 128, 64, 32, 16))
    nt = h // tr

    def body(p_ref, own_ref, q_ref, o_ref):
        chip = p_ref[0]
        acc = jnp.where(chip == 0, own_ref[0], q_ref[0]).astype(F32)
        for j in range(1, s):
            acc = acc + jnp.where(chip == j, own_ref[j], q_ref[j]).astype(F32)
        o_ref[...] = acc

    blk = pl.BlockSpec((s, tr, cols), lambda i, p: (0, i, 0))
    return pl.pallas_call(
        body, name=name,
        grid_spec=pltpu.PrefetchScalarGridSpec(
            num_scalar_prefetch=1, grid=(nt,), in_specs=[blk, blk],
            out_specs=pl.BlockSpec((tr, cols), lambda i, p: (p[1] * nt + i, 0))),
        out_shape=jax.ShapeDtypeStruct((2 * h, cols), F32), compiler_params=_params(1))(place, own, q)


def _pack_rows(parts):
    rows = []
    for p in parts:
        flat = p.reshape(-1).astype(F32)
        n = _roundup(flat.shape[0], 8 * LANE)
        rows.append(jnp.pad(flat, (0, n - flat.shape[0])).reshape(-1, LANE))
    return jnp.concatenate(rows, axis=0)


def _unpack_rows(buf, shapes):
    out, r = [], 0
    for shp in shapes:
        size = math.prod(shp)
        nr = _roundup(size, 8 * LANE) // LANE
        out.append(buf[r:r + nr].reshape(-1)[:size].reshape(shp))
        r += nr
    return out


def kernel(x, attn_norm, gla_w_in, gla_w_a2, gla_b_a2, gla_head_norm, gla_w_out, kv_norm, w_kv, dsa_w_q, dsa_w_out, ffn_norm, ffn_w_up, ffn_conv_w, ffn_conv_b, ffn_w_down, final_norm, loss_target, m_attn_norm, m_gla_w_in, m_gla_w_a2, m_gla_b_a2, m_gla_head_norm, m_gla_w_out, m_kv_norm, m_w_kv, m_dsa_w_q, m_dsa_w_out, m_ffn_norm, m_ffn_w_up, m_ffn_conv_w, m_ffn_conv_b, m_ffn_w_down, m_final_norm, v_attn_norm, v_gla_w_in, v_gla_w_a2, v_gla_b_a2, v_gla_head_norm, v_gla_w_out, v_kv_norm, v_w_kv, v_dsa_w_q, v_dsa_w_out, v_ffn_norm, v_ffn_w_up, v_ffn_conv_w, v_ffn_conv_b, v_ffn_w_down, v_final_norm):
    lay = _layout()
    d, f = D_MODEL, D_FF
    cx, cy, cc = _place()
    chip = 2 * cx + cy
    c_arr = jnp.reshape(cc, (1,)).astype(jnp.int32)
    place = jnp.stack([chip, cc]).astype(jnp.int32)

    groups = {"A": ("gin", "gout", "small"), "B": ("up0", "down0"), "C": ("up1", "down1", "wq", "wkv", "dout")}
    big_shards = (gla_w_in, gla_w_out, w_kv, dsa_w_q, dsa_w_out, ffn_w_up, ffn_w_down)
    ws = _pack_weights(chip, groups["A"][:2], *big_shards)
    sharded_small = [gla_w_a2[0], gla_b_a2[0], gla_head_norm[0], ffn_conv_w]
    packed = _pack_rows(sharded_small)
    packed = jnp.pad(packed, ((0, _roundup(packed.shape[0], 16) - packed.shape[0]), (0, 0)))
    ws["small"] = lax.dynamic_update_slice(jnp.zeros((N_CHIPS,) + packed.shape, F32), packed[None], (chip, 0, 0))
    send, recv, _, arrs = _gather_start("gather_a_start", [], [ws[k] for k in groups["A"]])
    chip_arr = place[:1]
    sources = {"up0": (ffn_w_up, 0), "up1": (ffn_w_up, 1), "down0": (ffn_w_down, 0), "down1": (ffn_w_down, 1),
               "wq": (dsa_w_q, 0), "wkv": (w_kv, 0), "dout": (dsa_w_out, 0)}
    for k in groups["B"] + groups["C"]:
        ws[k] = _pack_shard(f"pack_{k}", *sources[k], chip_arr, arrs[2])
    arrs = _gather_wait("gather_a_wait", send, recv, arrs, ws["dout"])
    ws.update(zip(groups["A"], _forward_halves("forward_a", arrs)))
    in_flight = {}
    thru = [ws[k] for k in groups["A"]]
    for grp in ("B", "C"):
        send, recv, thru, arrs = _gather_start(f"gather_{grp.lower()}_start", thru, [ws[k] for k in groups[grp]])
        ws.update(zip(groups[grp], arrs))
        in_flight[grp] = (send, recv)
    ws.update(zip(groups["A"], thru))
    pending = []

    class _Comm:
        def need(self, grp, ws, after):
            send, recv = in_flight[grp]
            arrs = _gather_wait(f"gather_{grp.lower()}_wait", send, recv, [ws[k] for k in groups[grp]], after)
            arrs = _forward_halves(f"forward_{grp.lower()}", arrs)
            return {**ws, **dict(zip(groups[grp], arrs))}

        swapping = None

        def reduce(self, grp, grads, carry):
            names = list(grads)
            send, recv, thru, parts, theirs = _swap_start(f"swap_{grp}_start", [carry], [grads[k] for k in names])
            self.swapping = (grp, names, send, recv, parts, theirs)
            return thru[0]

        def tick(self, carry):
            if self.swapping is None:
                return carry
            grp, names, send, recv, parts, theirs = self.swapping
            self.swapping = None
            parts, theirs = _swap_wait(f"swap_{grp}_wait", send, recv, parts, theirs, carry)
            return self.scatter(grp, names, parts, theirs, carry)

        def scatter(self, grp, names, parts, theirs, carry):
            sums = [_add_my_half(f"add_half_{k}", a, b, c_arr) for k, a, b in zip(names, parts, theirs)]
            send, recv, thru, sums, landing = _scatter_start(f"scatter_{grp}_start", [carry], sums)
            pending.append((grp, names, send, recv, sums, landing))
            return thru[0]

        def reduce_now(self, grp, grads, carry):
            names = list(grads)
            parts = [grads[k] for k in names]
            return self.scatter(grp, names, parts, _swap_halves(f"swap_{grp}", parts), carry)

    shards = [_unpack_rows(ws["small"][s], [p.shape for p in sharded_small]) for s in range(N_CHIPS)]
    w_a2, b_a2, head_norm, conv_w = [jnp.concatenate([shards[s][k] for s in range(N_CHIPS)], axis=-1) for k in range(4)]
    norms, small = _small_params(attn_norm, ffn_norm, kv_norm, final_norm, ffn_conv_b, w_a2, b_a2, head_norm, conv_w)

    comm = _Comm()
    loss_blk, grad_x, sm, last_big = _local_step(x[0], loss_target[0], ws, norms, small, comm)

    small_parts = [loss_blk, jnp.concatenate([sm["attn0"], sm["attn1"]]), jnp.concatenate([sm["ffn0"], sm["ffn1"]]),
                   sm["kv"], sm["final"], jnp.concatenate([sm["conv_b0"], sm["conv_b1"]]),
                   sm["w_a2p"][:GATE_RANK], sm["b_a2"], sm["head_norm"],
                   jnp.stack([jnp.concatenate(sm["conv_w0"]), jnp.concatenate(sm["conv_w1"])])]
    small_shapes = [(8, LANE), (2, d), (2, d), (d,), (d,), (2, f), (GATE_RANK, GLA_KEY_DIM), (GLA_KEY_DIM,),
                    (GLA_VAL_DIM // GLA_HEADS,), (2, 3, f)]
    _, reduced = _allgather8("reduce_small", _pack_rows(small_parts), True)
    reduced = comm.reduce_now("gla", last_big, reduced)

    loss_r, g_attn, g_ffn, g_kv, g_final, g_cb, g_a2, g_ba2, g_hn, g_cw = _unpack_rows(reduced, small_shapes)
    loss = loss_r[0, 0]

    def mine(g, axis):
        w = g.shape[axis] // N_CHIPS
        return lax.dynamic_slice_in_dim(g, chip * w, w, axis)

    grads = {
        "attn_norm": g_attn, "gla_w_a2": mine(g_a2, 1)[None], "gla_b_a2": mine(g_ba2, 0)[None],
        "gla_head_norm": mine(g_hn, 0)[None], "kv_norm": g_kv, "ffn_norm": g_ffn, "ffn_conv_w": mine(g_cw, 2),
        "ffn_conv_b": g_cb, "final_norm": g_final,
    }
    weights = {"attn_norm": (attn_norm, m_attn_norm, v_attn_norm), "gla_w_in": (gla_w_in, m_gla_w_in, v_gla_w_in),
               "gla_w_a2": (gla_w_a2, m_gla_w_a2, v_gla_w_a2), "gla_b_a2": (gla_b_a2, m_gla_b_a2, v_gla_b_a2),
               "gla_head_norm": (gla_head_norm, m_gla_head_norm, v_gla_head_norm),
               "gla_w_out": (gla_w_out, m_gla_w_out, v_gla_w_out), "kv_norm": (kv_norm, m_kv_norm, v_kv_norm),
               "w_kv": (w_kv, m_w_kv, v_w_kv), "dsa_w_q": (dsa_w_q, m_dsa_w_q, v_dsa_w_q),
               "dsa_w_out": (dsa_w_out, m_dsa_w_out, v_dsa_w_out), "ffn_norm": (ffn_norm, m_ffn_norm, v_ffn_norm),
               "ffn_w_up": (ffn_w_up, m_ffn_w_up, v_ffn_w_up), "ffn_conv_w": (ffn_conv_w, m_ffn_conv_w, v_ffn_conv_w),
               "ffn_conv_b": (ffn_conv_b, m_ffn_conv_b, v_ffn_conv_b),
               "ffn_w_down": (ffn_w_down, m_ffn_w_down, v_ffn_w_down), "final_norm": (final_norm, m_final_norm, v_final_norm)}
    order = list(weights)
    big_names = ("gla_w_in", "gla_w_out", "w_kv", "dsa_w_q", "dsa_w_out", "ffn_w_up", "ffn_w_down")
    delta, new_m, new_v = {}, {}, {}

    def adam_big(k, g):
        w, m, v = weights[k]
        cols = w.shape[-1]
        grads[k] = g
        res = _adamw(f"adamw_{k}", w.reshape(-1, cols), g.reshape(-1, cols), m.reshape(-1, cols), v.reshape(-1, cols))
        delta[k], new_m[k], new_v[k] = [r.reshape(w.shape) for r in res]
        return res[0]

    full = {}
    after = reduced
    joining = []
    for grp, names, send, recv, sums, landing in pending[:-1]:
        sums, landing = _scatter_wait(f"scatter_{grp}_wait", send, recv, sums, landing, after)
        halves = [_sum_chips(f"sum_chips_{k}", s, q, place) for k, s, q in zip(names, sums, landing)]
        send, recv, halves = _join_start(f"join_{grp}_start", halves)
        joining.append((grp, names, send, recv, halves))
        after = halves[0]
    for grp, names, send, recv, halves in joining:
        joined = _join_wait(f"join_{grp}_wait", send, recv, halves, after)
        full.update(zip(names, joined))
        after = joined[0]
    after = adam_big("w_kv", full["wkv"])
    after = adam_big("dsa_w_q", full["wq"][None])
    after = adam_big("dsa_w_out", full["dout"][None])
    after = adam_big("ffn_w_up", jnp.stack([full["up0"], full["up1"]]))
    after = adam_big("ffn_w_down", jnp.stack([full["down0"], full["down1"]]))
    grp, names, send, recv, sums, landing = pending[-1]
    sums, landing = _scatter_wait(f"scatter_{grp}_wait", send, recv, sums, landing, after)
    halves = [_sum_chips(f"sum_chips_{k}", s, q, place) for k, s, q in zip(names, sums, landing)]
    full.update(zip(names, _join_halves(f"join_{grp}", halves)))
    adam_big("gla_w_in", full["gin"][None, :, :lay["gin"][1]])
    adam_big("gla_w_out", full["gout"][None])
    small_names = [k for k in order if k not in big_names]
    packed = [_pack_rows([src[k] for k in small_names])
              for src in ({k: weights[k][0] for k in small_names}, grads, {k: weights[k][1] for k in small_names},
                          {k: weights[k][2] for k in small_names})]
    res = _adamw("adamw_small", *packed)
    shapes = [weights[k][0].shape for k in small_names]
    for dst, buf in zip((delta, new_m, new_v), res):
        for k, val in zip(small_names, _unpack_rows(buf, shapes)):
            dst[k] = val
    return (loss, grad_x[None], *[grads[k] for k in order], *[delta[k] for k in order], *[new_m[k] for k in order],
            *[new_v[k] for k in order])
```

```python
import math

import jax
import jax.numpy as jnp
from jax import lax
from jax.experimental import pallas as pl
from jax.experimental.pallas import tpu as pltpu

F32 = jnp.float32
BF16 = jnp.bfloat16

D_MODEL = 2048
SEQ = 4096
GLA_HEADS = 4
GLA_KEY_DIM = D_MODEL // 2
GLA_VAL_DIM = D_MODEL
GATE_RANK = 16
GATE_NORMALIZER = 16.0
GLA_CHUNK = 64
ATT_HEADS = 16
HEAD_DIM = 128
WINDOWS = (128, 512, 2048)
DILATIONS = (1, 4, 16)
ATT_BLOCK = 128
D_FF = 5632
EPS = 1e-6
ADAM_LR = 0.001
ADAM_B1 = 0.9
ADAM_B2 = 0.999
ADAM_EPS = 1e-08
ADAM_WD = 0.01
ADAM_STEP = 10

N_CHIPS = 4
N_DEV = 8
LANE = 128
A_PAD = 128
VMEM_LIMIT = 56 * 1024 * 1024
MAX_K_TILE = 2816
NEG = -1e30
MESH = pl.DeviceIdType.MESH

NN = (((1,), (0,)), ((), ()))
NT = (((1,), (1,)), ((), ()))
TN = (((0,), (0,)), ((), ()))


def _tile(n, cands):
    for c in cands:
        if c <= n and n % c == 0:
            return c
    return n


def _roundup(n, m):
    return -(-n // m) * m


def _params(n_axes):
    return pltpu.CompilerParams(dimension_semantics=("arbitrary",) * n_axes, vmem_limit_bytes=VMEM_LIMIT)


def _dot(a, b, dims):
    return lax.dot_general(a, b, dims, preferred_element_type=F32)


def _sigmoid(x):
    return 1.0 / (1.0 + jnp.exp(-x))


COL_SHARDED = ("gin", "up0", "up1", "wq", "wkv")
ROW_SHARDED = ("gout", "down0", "down1", "dout")


def _layout():
    f = D_FF
    hd = ATT_HEADS * HEAD_DIM
    gin = 2 * GLA_KEY_DIM + 2 * GLA_VAL_DIM + GATE_RANK
    up_w = 2 * f // N_CHIPS
    q_w = 3 * hd // N_CHIPS
    kv_w = 2 * hd // N_CHIPS
    dn_r = f // N_CHIPS
    go_r = GLA_VAL_DIM // N_CHIPS
    do_r = hd // N_CHIPS
    big = (1408, 1024, 512, 256, 128)
    return {
        "gin": (0, gin // N_CHIPS, LANE),
        "up0": (0, up_w, _tile(up_w, big)), "up1": (0, up_w, _tile(up_w, big)),
        "wq": (0, q_w, _tile(q_w, (512, 384, 256, 128))), "wkv": (0, kv_w, _tile(kv_w, (1024, 512, 256, 128))),
        "down0": (0, dn_r, _tile(dn_r, big)), "down1": (0, dn_r, _tile(dn_r, big)),
        "gout": (0, go_r, _tile(go_r, (512, 256, 128))), "dout": (0, do_r, _tile(do_r, (512, 256, 128))),
    }


def _matmul(name, a, b, dims, grid, a_spec, b_spec, o_spec, out_shape, acc_shape, add=None, add_spec=None):
    nk = grid[2]
    has_add = add is not None

    def body(*refs):
        a_ref, b_ref = refs[0], refs[1]
        pos = 2
        add_ref = None
        if has_add:
            add_ref = refs[pos]
            pos += 1
        o_ref = refs[pos]
        prod = _dot(a_ref[...].astype(BF16), b_ref[...].astype(BF16), dims)

        def finish(val):
            if has_add:
                val = val + add_ref[...].astype(F32)
            o_ref[...] = val.astype(o_ref.dtype)

        if nk == 1:
            finish(prod)
        else:
            acc_ref = refs[pos + 1]
            k = pl.program_id(2)

            @pl.when(k == 0)
            def _():
                acc_ref[...] = prod

            @pl.when(k > 0)
            def _():
                acc_ref[...] += prod

            @pl.when(k == nk - 1)
            def _():
                finish(acc_ref[...])

    in_specs = [a_spec, b_spec]
    args = [a, b]
    if has_add:
        in_specs.append(add_spec)
        args.append(add)
    scratch = [] if nk == 1 else [pltpu.VMEM(acc_shape, F32)]
    return pl.pallas_call(body, name=name, grid=grid, in_specs=in_specs, out_specs=o_spec, out_shape=out_shape,
                          scratch_shapes=scratch, compiler_params=_params(3))(*args)


def _mm_act_wc(name, a, wc, seg, out_dtype):
    off, w, tn = seg
    t_len, d = a.shape
    tm = _tile(t_len, (1024, 512, 256, 128))
    nps = w // tn
    ob = off // tn
    grid = (t_len // tm, N_CHIPS * nps, 1)
    return _matmul(
        name, a, wc, NN, grid,
        pl.BlockSpec((tm, d), lambda i, j, k: (i, 0)),
        pl.BlockSpec((None, d, tn), lambda i, j, k: (j // nps, 0, ob + j % nps)),
        pl.BlockSpec((tm, tn), lambda i, j, k: (i, j)),
        jax.ShapeDtypeStruct((t_len, N_CHIPS * w), out_dtype), (tm, tn))


def _mm_dact_wcT(name, dy, wc, seg, add=None):
    off, w, tk = seg
    if off == 0 and w <= MAX_K_TILE:
        tk = w
    t_len = dy.shape[0]
    d = wc.shape[1]
    tm = _tile(t_len, (1024, 512, 256, 128))
    tn = _tile(d, (1024, 512, 256, 128))
    kps = w // tk
    ob = off // tk
    grid = (t_len // tm, d // tn, N_CHIPS * kps)
    return _matmul(
        name, dy, wc, NT, grid,
        pl.BlockSpec((tm, tk), lambda i, j, k: (i, k)),
        pl.BlockSpec((None, tn, tk), lambda i, j, k: (k // kps, j, ob + k % kps)),
        pl.BlockSpec((tm, tn), lambda i, j, k: (i, j)),
        jax.ShapeDtypeStruct((t_len, d), F32), (tm, tn),
        add=add, add_spec=None if add is None else pl.BlockSpec((tm, tn), lambda i, j, k: (i, j)))


def _mm_grad_wc(name, a, dy, seg):
    _, w, tn = seg
    t_len, d = a.shape
    tm = _tile(d, (1024, 512, 256, 128))
    tk = _tile(t_len, (2048, 1024, 512, 256, 128))
    nps = w // tn
    grid = (d // tm, N_CHIPS * nps, t_len // tk)
    return _matmul(
        name, a, dy, TN, grid,
        pl.BlockSpec((tk, tm), lambda i, j, k: (k, i)),
        pl.BlockSpec((tk, tn), lambda i, j, k: (k, j)),
        pl.BlockSpec((None, tm, tn), lambda i, j, k: (j // nps, i, j % nps)),
        jax.ShapeDtypeStruct((N_CHIPS, d, w), BF16), (tm, tn))


def _is_plain(wr, seg):
    return seg[0] == 0 and wr.shape[1] == seg[1] and (N_CHIPS * seg[1]) % 1024 == 0


def _mm_act_wr(name, a, wr, seg, add):
    off, r, tk = seg
    t_len = a.shape[0]
    d = wr.shape[2]
    if seg[0] == 0 and wr.shape[1] == r:
        return _mm_plain(name, a, wr.reshape(N_CHIPS * r, d), NN, F32, add=add)
    tm = _tile(t_len, (1024, 512, 256, 128))
    tn = _tile(d, (1024, 512, 256, 128))
    kps = r // tk
    ob = off // tk
    grid = (t_len // tm, d // tn, N_CHIPS * kps)
    return _matmul(
        name, a, wr, NN, grid,
        pl.BlockSpec((tm, tk), lambda i, j, k: (i, k)),
        pl.BlockSpec((None, tk, tn), lambda i, j, k: (k // kps, ob + k % kps, j)),
        pl.BlockSpec((tm, tn), lambda i, j, k: (i, j)),
        jax.ShapeDtypeStruct((t_len, d), F32), (tm, tn),
        add=add, add_spec=pl.BlockSpec((tm, tn), lambda i, j, k: (i, j)))


def _mm_dact_wrT(name, dh, wr, seg):
    off, r, tn = seg
    t_len, d = dh.shape
    if _is_plain(wr, seg):
        return _mm_plain(name, dh, wr.reshape(N_CHIPS * r, d), NT, BF16)
    tm = _tile(t_len, (1024, 512, 256, 128))
    nps = r // tn
    ob = off // tn
    grid = (t_len // tm, N_CHIPS * nps, 1)
    return _matmul(
        name, dh, wr, NT, grid,
        pl.BlockSpec((tm, d), lambda i, j, k: (i, 0)),
        pl.BlockSpec((None, tn, d), lambda i, j, k: (j // nps, ob + j % nps, 0)),
        pl.BlockSpec((tm, tn), lambda i, j, k: (i, j)),
        jax.ShapeDtypeStruct((t_len, N_CHIPS * r), BF16), (tm, tn))


def _mm_grad_wr(name, a, dh, seg):
    _, r, tm = seg
    t_len, d = dh.shape
    if (N_CHIPS * r) % 1024 == 0:
        return _mm_plain(name, a, dh, TN, BF16).reshape(N_CHIPS, r, d)
    tn = _tile(d, (1024, 512, 256, 128))
    tk = _tile(t_len, (2048, 1024, 512, 256, 128))
    mps = r // tm
    grid = (N_CHIPS * mps, d // tn, t_len // tk)
    return _matmul(
        name, a, dh, TN, grid,
        pl.BlockSpec((tk, tm), lambda i, j, k: (k, i)),
        pl.BlockSpec((tk, tn), lambda i, j, k: (k, j)),
        pl.BlockSpec((None, tm, tn), lambda i, j, k: (i // mps, i % mps, j)),
        jax.ShapeDtypeStruct((N_CHIPS, r, d), BF16), (tm, tn))


def _mm_plain(name, a, b, dims, out_dtype, add=None):
    if dims == NN:
        m, kd = a.shape
        n = b.shape[1]
    elif dims == NT:
        m, kd = a.shape
        n = b.shape[0]
    else:
        kd, m = a.shape
        n = b.shape[1]
    tm = _tile(m, (1024, 512, 256, 128))
    tn = _tile(n, (1024, 768, 512, 256, 128))
    tk = _tile(kd, (MAX_K_TILE, 2048, 1408, 1024, 512, 256, 128))
    grid = (m // tm, n // tn, kd // tk)
    if dims == NN:
        a_spec = pl.BlockSpec((tm, tk), lambda i, j, k: (i, k))
        b_spec = pl.BlockSpec((tk, tn), lambda i, j, k: (k, j))
    elif dims == NT:
        a_spec = pl.BlockSpec((tm, tk), lambda i, j, k: (i, k))
        b_spec = pl.BlockSpec((tn, tk), lambda i, j, k: (j, k))
    else:
        a_spec = pl.BlockSpec((tk, tm), lambda i, j, k: (k, i))
        b_spec = pl.BlockSpec((tk, tn), lambda i, j, k: (k, j))
    o_spec = pl.BlockSpec((tm, tn), lambda i, j, k: (i, j))
    return _matmul(name, a, b, dims, grid, a_spec, b_spec, o_spec, jax.ShapeDtypeStruct((m, n), out_dtype), (tm, tn),
                   add=add, add_spec=None if add is None else o_spec)


def _rms_fwd(name, x, g):
    t_len, d = x.shape
    tm = _tile(t_len, (512, 256, 128))

    def body(x_ref, g_ref, o_ref):
        xv = x_ref[...]
        r = lax.rsqrt(jnp.mean(xv * xv, axis=-1, keepdims=True) + EPS)
        o_ref[...] = (xv * r * g_ref[...]).astype(o_ref.dtype)

    return pl.pallas_call(
        body, name=name, grid=(t_len // tm,),
        in_specs=[pl.BlockSpec((tm, d), lambda i: (i, 0)), pl.BlockSpec((1, d), lambda i: (0, 0))],
        out_specs=pl.BlockSpec((tm, d), lambda i: (i, 0)),
        out_shape=jax.ShapeDtypeStruct((t_len, d), BF16), compiler_params=_params(1))(x, g)


def _rms_bwd(name, dy, x, g, dres):
    t_len, d = x.shape
    tm = _tile(t_len, (256, 128))

    def body(dy_ref, x_ref, g_ref, dres_ref, dx_ref, dg_ref):
        xv = x_ref[...]
        r = lax.rsqrt(jnp.mean(xv * xv, axis=-1, keepdims=True) + EPS)
        xhat = xv * r
        dyv = dy_ref[...].astype(F32)
        dxn = dyv * g_ref[...]
        dx = r * (dxn - xhat * jnp.mean(dxn * xhat, axis=-1, keepdims=True))
        dx_ref[...] = dres_ref[...] + dx
        part = jnp.sum(dyv * xhat, axis=0, keepdims=True)

        @pl.when(pl.program_id(0) == 0)
        def _():
            dg_ref[...] = part

        @pl.when(pl.program_id(0) > 0)
        def _():
            dg_ref[...] += part

    row = pl.BlockSpec((tm, d), lambda i: (i, 0))
    vec = pl.BlockSpec((1, d), lambda i: (0, 0))
    return pl.pallas_call(
        body, name=name, grid=(t_len // tm,), in_specs=[row, row, vec, row], out_specs=(row, vec),
        out_shape=(jax.ShapeDtypeStruct((t_len, d), F32), jax.ShapeDtypeStruct((1, d), F32)),
        compiler_params=_params(1))(dy, x, g, dres)


def _loss_head(h, g, target):
    t_len, d = h.shape
    tm = _tile(t_len, (256, 128))

    def body(h_ref, g_ref, t_ref, dh_ref, dg_ref, loss_ref):
        xv = h_ref[...]
        gv = g_ref[...]
        r = lax.rsqrt(jnp.mean(xv * xv, axis=-1, keepdims=True) + EPS)
        xhat = xv * r
        err = xhat * gv - t_ref[...]
        dyv = err * (1.0 / d)
        dxn = dyv * gv
        dh_ref[...] = r * (dxn - xhat * jnp.mean(dxn * xhat, axis=-1, keepdims=True))
        part = jnp.sum(dyv * xhat, axis=0, keepdims=True)
        lpart = jnp.zeros((8, LANE), F32) + (0.5 / d) * jnp.sum(err * err)

        @pl.when(pl.program_id(0) == 0)
        def _():
            dg_ref[...] = part
            loss_ref[...] = lpart

        @pl.when(pl.program_id(0) > 0)
        def _():
            dg_ref[...] += part
            loss_ref[...] += lpart

    row = pl.BlockSpec((tm, d), lambda i: (i, 0))
    vec = pl.BlockSpec((1, d), lambda i: (0, 0))
    return pl.pallas_call(
        body, name="loss_head", grid=(t_len // tm,), in_specs=[row, vec, row],
        out_specs=(row, vec, pl.BlockSpec((8, LANE), lambda i: (0, 0))),
        out_shape=(jax.ShapeDtypeStruct((t_len, d), F32), jax.ShapeDtypeStruct((1, d), F32),
                   jax.ShapeDtypeStruct((8, LANE), F32)),
        compiler_params=_params(1))(h, g, target)


def _chunk_row(shape):
    return lax.broadcasted_iota(jnp.int32, shape, 0) % GLA_CHUNK


def _gla_gate_fwd(a, w_a2p, b_a2):
    t_len = a.shape[0]
    kd = w_a2p.shape[1]
    tm = _tile(t_len, (256, 128, 64))

    def body(a_ref, w_ref, b_ref, ga_ref, cum_ref):
        ga = _dot(a_ref[...], w_ref[...].astype(BF16), NN) + b_ref[...]
        ga_ref[...] = ga
        la = (jnp.minimum(ga, 0.0) - jnp.log(1.0 + jnp.exp(-jnp.abs(ga)))) * (1.0 / GATE_NORMALIZER)
        row = _chunk_row(la.shape)
        s = 1
        while s < GLA_CHUNK:
            la = la + jnp.where(row >= s, pltpu.roll(la, s, 0), 0.0)
            s *= 2
        cum_ref[...] = la

    return pl.pallas_call(
        body, name="gla_gate_fwd", grid=(t_len // tm,),
        in_specs=[pl.BlockSpec((tm, A_PAD), lambda i: (i, 0)), pl.BlockSpec((A_PAD, kd), lambda i: (0, 0)),
                  pl.BlockSpec((1, kd), lambda i: (0, 0))],
        out_specs=(pl.BlockSpec((tm, kd), lambda i: (i, 0)), pl.BlockSpec((tm, kd), lambda i: (i, 0))),
        out_shape=(jax.ShapeDtypeStruct((t_len, kd), F32), jax.ShapeDtypeStruct((t_len, kd), F32)),
        compiler_params=_params(1))(a, w_a2p, b_a2)


def _gla_gate_bwd(dcum, ga, a, w_a2p):
    t_len, kd = dcum.shape
    tm = _tile(t_len, (256, 128, 64))

    def body(dc_ref, ga_ref, a_ref, w_ref, da_ref, dw_ref, db_ref):
        x = dc_ref[...]
        row = _chunk_row(x.shape)
        s = 1
        while s < GLA_CHUNK:
            x = x + jnp.where(row < GLA_CHUNK - s, pltpu.roll(x, tm - s, 0), 0.0)
            s *= 2
        dga = x * (1.0 / GATE_NORMALIZER) * _sigmoid(-ga_ref[...])
        dgab = dga.astype(BF16)
        da_ref[...] = _dot(dgab, w_ref[...].astype(BF16), NT).astype(da_ref.dtype)
        dw = _dot(a_ref[...], dgab, TN)
        db = jnp.sum(dga, axis=0, keepdims=True)

        @pl.when(pl.program_id(0) == 0)
        def _():
            dw_ref[...] = dw
            db_ref[...] = db

        @pl.when(pl.program_id(0) > 0)
        def _():
            dw_ref[...] += dw
            db_ref[...] += db

    wide = pl.BlockSpec((tm, kd), lambda i: (i, 0))
    return pl.pallas_call(
        body, name="gla_gate_bwd", grid=(t_len // tm,),
        in_specs=[wide, wide, pl.BlockSpec((tm, A_PAD), lambda i: (i, 0)), pl.BlockSpec((A_PAD, kd), lambda i: (0, 0))],
        out_specs=(pl.BlockSpec((tm, A_PAD), lambda i: (i, 0)), pl.BlockSpec((A_PAD, kd), lambda i: (0, 0)),
                   pl.BlockSpec((1, kd), lambda i: (0, 0))),
        out_shape=(jax.ShapeDtypeStruct((t_len, A_PAD), BF16), jax.ShapeDtypeStruct((A_PAD, kd), F32),
                   jax.ShapeDtypeStruct((1, kd), F32)),
        compiler_params=_params(1))(dcum, ga, a, w_a2p)


GLA_STEP_CHUNKS = 4


def _gla_dims():
    dk = GLA_KEY_DIM // GLA_HEADS
    dv = GLA_VAL_DIM // GLA_HEADS
    return dk, dv


def _gla_fwd(proj, cum):
    t_len = proj.shape[0]
    dk, dv = _gla_dims()
    nc = t_len // GLA_CHUNK
    c = GLA_CHUNK
    scale = dk ** -0.5
    v0 = 2 * GLA_KEY_DIM // dv

    per = _tile(nc, (GLA_STEP_CHUNKS, 2, 1))
    rows = per * c

    def body(q_ref, k_ref, v_ref, cum_ref, o_ref, st_ref, s_scr):
        @pl.when(pl.program_id(1) == 0)
        def _():
            s_scr[...] = jnp.zeros_like(s_scr)

        tri = lax.broadcasted_iota(jnp.int32, (c, c), 0) >= lax.broadcasted_iota(jnp.int32, (c, c), 1)
        for i in range(per):
            rs = slice(i * c, (i + 1) * c)
            cm = cum_ref[rs, :]
            last = cm[c - 1:c, :]
            q = q_ref[rs, :].astype(F32) * scale
            k = k_ref[rs, :].astype(F32)
            v = v_ref[rs, :].astype(BF16)
            qd = (q * jnp.exp(cm)).astype(BF16)
            ki = (k * jnp.exp(-cm)).astype(BF16)
            ke = (k * jnp.exp(last - cm)).astype(BF16)
            sc = jnp.where(tri, _dot(qd, ki, NT), 0.0)
            st = s_scr[...]
            st_ref[i] = st
            o_ref[rs, :] = _dot(sc.astype(BF16), v, NN) + _dot(qd, st.astype(BF16), NT)
            s_scr[...] = st * jnp.exp(last) + _dot(v, ke, TN)

    return pl.pallas_call(
        body, name="gla_fwd", grid=(GLA_HEADS, nc // per),
        in_specs=[pl.BlockSpec((rows, dk), lambda h, n: (n, h)),
                  pl.BlockSpec((rows, dk), lambda h, n: (n, GLA_HEADS + h)),
                  pl.BlockSpec((rows, dv), lambda h, n: (n, v0 + h)),
                  pl.BlockSpec((rows, dk), lambda h, n: (n, h))],
        out_specs=(pl.BlockSpec((rows, dv), lambda h, n: (n, h)),
                   pl.BlockSpec((None, per, dv, dk), lambda h, n: (h, n, 0, 0))),
        out_shape=(jax.ShapeDtypeStruct((t_len, GLA_VAL_DIM), F32),
                   jax.ShapeDtypeStruct((GLA_HEADS, nc, dv, dk), F32)),
        scratch_shapes=[pltpu.VMEM((dv, dk), F32)], compiler_params=_params(2))(proj, proj, proj, cum)


def _gla_bwd(proj, cum, states, do):
    t_len = proj.shape[0]
    dk, dv = _gla_dims()
    nc = t_len // GLA_CHUNK
    c = GLA_CHUNK
    scale = dk ** -0.5
    v0 = 2 * GLA_KEY_DIM // dv

    per = _tile(nc, (GLA_STEP_CHUNKS, 2, 1))
    rows = per * c

    def body(q_ref, k_ref, v_ref, cum_ref, st_ref, do_ref, dq_ref, dk_ref, dv_ref, dc_ref, ds_scr):
        @pl.when(pl.program_id(1) == 0)
        def _():
            ds_scr[...] = jnp.zeros_like(ds_scr)

        tri = lax.broadcasted_iota(jnp.int32, (c, c), 0) >= lax.broadcasted_iota(jnp.int32, (c, c), 1)
        row = lax.broadcasted_iota(jnp.int32, (c, dk), 0)
        for i in reversed(range(per)):
            rs = slice(i * c, (i + 1) * c)
            cm = cum_ref[rs, :]
            last = cm[c - 1:c, :]
            e_c = jnp.exp(cm)
            e_nc = jnp.exp(-cm)
            e_lc = jnp.exp(last - cm)
            e_l = jnp.exp(last)
            q = q_ref[rs, :].astype(F32) * scale
            k = k_ref[rs, :].astype(F32)
            v = v_ref[rs, :].astype(BF16)
            dov = do_ref[rs, :]
            qd32 = q * e_c
            ki32 = k * e_nc
            ke32 = k * e_lc
            qd = qd32.astype(BF16)
            ki = ki32.astype(BF16)
            ke = ke32.astype(BF16)
            st = st_ref[i]
            dst = ds_scr[...]
            dstb = dst.astype(BF16)
            am = jnp.where(tri, _dot(dov, v, NT), 0.0).astype(BF16)
            pm = jnp.where(tri, _dot(qd, ki, NT), 0.0).astype(BF16)
            dqd = _dot(am, ki, NN) + _dot(dov, st.astype(BF16), NN)
            dki = _dot(am, qd, TN)
            dvv = _dot(pm, dov, TN) + _dot(ke, dstb, NT)
            dke = _dot(v, dstb, NN)
            d_el = jnp.sum(dst * st, axis=0, keepdims=True)
            ds_scr[...] = dst * e_l + _dot(dov, qd, TN)
            dq_ref[rs, :] = (dqd * scale * e_c).astype(dq_ref.dtype)
            dk_ref[rs, :] = (dki * e_nc + dke * e_lc).astype(dk_ref.dtype)
            dv_ref[rs, :] = dvv.astype(dv_ref.dtype)
            dkeke = dke * ke32
            dcum = dqd * qd32 - dki * ki32 - dkeke
            dlast = jnp.sum(dkeke, axis=0, keepdims=True) + d_el * e_l
            dc_ref[rs, :] = jnp.where(row == c - 1, dcum + dlast, dcum)

    rev = nc // per - 1
    return pl.pallas_call(
        body, name="gla_bwd", grid=(GLA_HEADS, nc // per),
        in_specs=[pl.BlockSpec((rows, dk), lambda h, n: (rev - n, h)),
                  pl.BlockSpec((rows, dk), lambda h, n: (rev - n, GLA_HEADS + h)),
                  pl.BlockSpec((rows, dv), lambda h, n: (rev - n, v0 + h)),
                  pl.BlockSpec((rows, dk), lambda h, n: (rev - n, h)),
                  pl.BlockSpec((None, per, dv, dk), lambda h, n: (h, rev - n, 0, 0)),
                  pl.BlockSpec((rows, dv), lambda h, n: (rev - n, h))],
        out_specs=(pl.BlockSpec((rows, dk), lambda h, n: (rev - n, h)),
                   pl.BlockSpec((rows, dk), lambda h, n: (rev - n, h)),
                   pl.BlockSpec((rows, dv), lambda h, n: (rev - n, h)),
                   pl.BlockSpec((rows, dk), lambda h, n: (rev - n, h))),
        out_shape=(jax.ShapeDtypeStruct((t_len, GLA_KEY_DIM), BF16), jax.ShapeDtypeStruct((t_len, GLA_KEY_DIM), BF16),
                   jax.ShapeDtypeStruct((t_len, GLA_VAL_DIM), BF16), jax.ShapeDtypeStruct((t_len, GLA_KEY_DIM), F32)),
        scratch_shapes=[pltpu.VMEM((dv, dk), F32)], compiler_params=_params(2))(proj, proj, proj, cum, states, do)


def _gla_out_fwd(o, proj, gn):
    t_len = o.shape[0]
    _, dv = _gla_dims()
    tm = _tile(t_len, (512, 256, 128))
    r0 = (2 * GLA_KEY_DIM + GLA_VAL_DIM) // dv

    def body(o_ref, r_ref, g_ref, y_ref):
        ov = o_ref[...]
        rs = lax.rsqrt(jnp.mean(ov * ov, axis=-1, keepdims=True) + EPS)
        rv = r_ref[...].astype(F32)
        y_ref[...] = (ov * rs * g_ref[...] * (rv * _sigmoid(rv))).astype(y_ref.dtype)

    return pl.pallas_call(
        body, name="gla_out_fwd", grid=(t_len // tm, GLA_HEADS),
        in_specs=[pl.BlockSpec((tm, dv), lambda i, h: (i, h)), pl.BlockSpec((tm, dv), lambda i, h: (i, r0 + h)),
                  pl.BlockSpec((1, dv), lambda i, h: (0, 0))],
        out_specs=pl.BlockSpec((tm, dv), lambda i, h: (i, h)),
        out_shape=jax.ShapeDtypeStruct((t_len, GLA_VAL_DIM), BF16), compiler_params=_params(2))(o, proj, gn)


def _gla_out_bwd(dy, o, proj, gn):
    t_len = o.shape[0]
    _, dv = _gla_dims()
    tm = _tile(t_len, (512, 256, 128))
    r0 = (2 * GLA_KEY_DIM + GLA_VAL_DIM) // dv

    def body(dy_ref, o_ref, r_ref, g_ref, do_ref, dr_ref, dg_ref):
        ov = o_ref[...]
        gv = g_ref[...]
        rs = lax.rsqrt(jnp.mean(ov * ov, axis=-1, keepdims=True) + EPS)
        xhat = ov * rs
        rv = r_ref[...].astype(F32)
        sg = _sigmoid(rv)
        gate = rv * sg
        dyv = dy_ref[...].astype(F32)
        dn = dyv * gate
        dr_ref[...] = (dyv * xhat * gv * (sg * (1.0 + rv * (1.0 - sg)))).astype(dr_ref.dtype)
        dxn = dn * gv
        do_ref[...] = (rs * (dxn - xhat * jnp.mean(dxn * xhat, axis=-1, keepdims=True))).astype(do_ref.dtype)
        part = jnp.sum(dn * xhat, axis=0, keepdims=True)
        first = (pl.program_id(0) == 0) & (pl.program_id(1) == 0)

        @pl.when(first)
        def _():
            dg_ref[...] = part

        @pl.when(jnp.logical_not(first))
        def _():
            dg_ref[...] += part

    blk = pl.BlockSpec((tm, dv), lambda i, h: (i, h))
    return pl.pallas_call(
        body, name="gla_out_bwd", grid=(t_len // tm, GLA_HEADS),
        in_specs=[blk, blk, pl.BlockSpec((tm, dv), lambda i, h: (i, r0 + h)), pl.BlockSpec((1, dv), lambda i, h: (0, 0))],
        out_specs=(blk, blk, pl.BlockSpec((1, dv), lambda i, h: (0, 0))),
        out_shape=(jax.ShapeDtypeStruct((t_len, GLA_VAL_DIM), BF16), jax.ShapeDtypeStruct((t_len, GLA_VAL_DIM), BF16),
                   jax.ShapeDtypeStruct((1, dv), F32)),
        compiler_params=_params(2))(dy, o, proj, gn)


def _alibi_slopes():
    n = ATT_HEADS
    start = 2.0 ** (-8.0 / n)
    return [start ** (i + 1) for i in range(n)]


def _att_masks(d):
    b = ATT_BLOCK
    qa = lax.broadcasted_iota(jnp.int32, (b, b), 0)
    kb = lax.broadcasted_iota(jnp.int32, (b, b), 1)
    dist_c = qa - kb
    dist_p = qa - kb + b
    return dist_c >= 0, dist_p <= b, (dist_c * d).astype(F32), (dist_p * d).astype(F32)


def _att_fwd(q_all, kv, g):
    d = DILATIONS[g]
    assert WINDOWS[g] // d == ATT_BLOCK
    t_len = q_all.shape[0]
    hd = ATT_HEADS * HEAD_DIM
    sub = t_len // d
    nb = sub // ATT_BLOCK
    b = ATT_BLOCK
    e = HEAD_DIM
    scale = e ** -0.5
    slopes = _alibi_slopes()
    qv = q_all.reshape(sub, d * 3 * hd)
    kvv = kv.reshape(sub, d * 2 * hd)

    def body(q_ref, kp_ref, kc_ref, vp_ref, vc_ref, o_ref, l_ref, s_scr, p_scr, li_scr):
        ib = pl.program_id(1)
        valid_c, valid_p0, dist_c, dist_p = _att_masks(d)
        valid_p = valid_p0 & (ib > 0)
        for h in range(ATT_HEADS):
            hs = slice(h * e, (h + 1) * e)
            qh = q_ref[:, hs]
            s_scr[h, 0] = _dot(qh, kc_ref[:, hs], NT)
            s_scr[h, 1] = _dot(qh, kp_ref[:, hs], NT)
        l_ref[...] = jnp.zeros_like(l_ref)
        for h in range(ATT_HEADS):
            s_c = jnp.where(valid_c, s_scr[h, 0] * scale - slopes[h] * dist_c, NEG)
            s_p = jnp.where(valid_p, s_scr[h, 1] * scale - slopes[h] * dist_p, NEG)
            m = jnp.maximum(jnp.max(s_c, axis=1, keepdims=True), jnp.max(s_p, axis=1, keepdims=True))
            p_c = jnp.where(valid_c, jnp.exp(s_c - m), 0.0)
            p_p = jnp.where(valid_p, jnp.exp(s_p - m), 0.0)
            l = jnp.sum(p_c, axis=1, keepdims=True) + jnp.sum(p_p, axis=1, keepdims=True)
            p_scr[h, 0] = p_c.astype(BF16)
            p_scr[h, 1] = p_p.astype(BF16)
            li_scr[:, h:h + 1] = 1.0 / l
            l_ref[:, h:h + 1] = m + jnp.log(l)
        for h in range(ATT_HEADS):
            hs = slice(h * e, (h + 1) * e)
            acc = _dot(p_scr[h, 0], vc_ref[:, hs], NN) + _dot(p_scr[h, 1], vp_ref[:, hs], NN)
            o_ref[:, hs] = acc * li_scr[:, h:h + 1]

    blk = (b, hd)
    cblk = (b, LANE)
    o, lse = pl.pallas_call(
        body, name=f"att_fwd{g}", grid=(d, nb),
        scratch_shapes=[pltpu.VMEM((ATT_HEADS, 2, b, b), F32), pltpu.VMEM((ATT_HEADS, 2, b, b), BF16),
                        pltpu.VMEM((b, LANE), F32)],
        in_specs=[pl.BlockSpec(blk, lambda r, i: (i, 3 * r + g)),
                  pl.BlockSpec(blk, lambda r, i: (jnp.maximum(i - 1, 0), 2 * r)),
                  pl.BlockSpec(blk, lambda r, i: (i, 2 * r)),
                  pl.BlockSpec(blk, lambda r, i: (jnp.maximum(i - 1, 0), 2 * r + 1)),
                  pl.BlockSpec(blk, lambda r, i: (i, 2 * r + 1))],
        out_specs=(pl.BlockSpec(blk, lambda r, i: (i, r)), pl.BlockSpec(cblk, lambda r, i: (i, r))),
        out_shape=(jax.ShapeDtypeStruct((sub, d * hd), F32), jax.ShapeDtypeStruct((sub, d * LANE), F32)),
        compiler_params=_params(2))(qv, kvv, kvv, kvv, kvv)
    return o.reshape(t_len, hd), lse.reshape(t_len, LANE)


def _att_merge(os, ls):
    t_len, hd = os[0].shape
    tm = _tile(t_len, (256, 128))
    e = HEAD_DIM

    def body(o0, o1, o2, l0, l1, l2, of_ref, ob_ref, l_ref):
        a0, a1, a2 = l0[...], l1[...], l2[...]
        m = jnp.maximum(jnp.maximum(a0, a1), a2)
        e0, e1, e2 = jnp.exp(a0 - m), jnp.exp(a1 - m), jnp.exp(a2 - m)
        den = e0 + e1 + e2
        w0, w1, w2 = e0 / den, e1 / den, e2 / den
        l_ref[...] = m + jnp.log(den)
        for h in range(ATT_HEADS):
            hs = slice(h * e, (h + 1) * e)
            c = slice(h, h + 1)
            o = w0[:, c] * o0[:, hs] + w1[:, c] * o1[:, hs] + w2[:, c] * o2[:, hs]
            of_ref[:, hs] = o
            ob_ref[:, hs] = o.astype(ob_ref.dtype)

    row = pl.BlockSpec((tm, hd), lambda i: (i, 0))
    crow = pl.BlockSpec((tm, LANE), lambda i: (i, 0))
    return pl.pallas_call(
        body, name="att_merge", grid=(t_len // tm,), in_specs=[row] * 3 + [crow] * 3, out_specs=(row, row, crow),
        out_shape=(jax.ShapeDtypeStruct((t_len, hd), F32), jax.ShapeDtypeStruct((t_len, hd), BF16),
                   jax.ShapeDtypeStruct((t_len, LANE), F32)),
        compiler_params=_params(1))(*os, *ls)


def _att_delta(do, o):
    t_len, hd = o.shape
    tm = _tile(t_len, (256, 128))
    e = HEAD_DIM

    def body(do_ref, o_ref, d_ref):
        d_ref[...] = jnp.zeros_like(d_ref)
        for h in range(ATT_HEADS):
            hs = slice(h * e, (h + 1) * e)
            d_ref[:, h:h + 1] = jnp.sum(do_ref[:, hs].astype(F32) * o_ref[:, hs], axis=1, keepdims=True)

    row = pl.BlockSpec((tm, hd), lambda i: (i, 0))
    return pl.pallas_call(
        body, name="att_delta", grid=(t_len // tm,), in_specs=[row, row],
        out_specs=pl.BlockSpec((tm, LANE), lambda i: (i, 0)),
        out_shape=jax.ShapeDtypeStruct((t_len, LANE), F32), compiler_params=_params(1))(do, o)


def _att_bwd(q_all, kv, delta, lse, do, g):
    d = DILATIONS[g]
    t_len = q_all.shape[0]
    hd = ATT_HEADS * HEAD_DIM
    sub = t_len // d
    nb = sub // ATT_BLOCK
    b = ATT_BLOCK
    e = HEAD_DIM
    scale = e ** -0.5
    slopes = _alibi_slopes()
    qv = q_all.reshape(sub, d * 3 * hd)
    kvv = kv.reshape(sub, d * 2 * hd)
    dlv = delta.reshape(sub, d * LANE)
    lv = lse.reshape(sub, d * LANE)
    dov = do.reshape(sub, d * hd)

    def body(qj_ref, qn_ref, kp_ref, kc_ref, vp_ref, vc_ref, doj_ref, don_ref, dj_ref, dn_ref, lj_ref, ln_ref,
             dq_ref, dk_ref, dv_ref, s_scr, dp_scr, p_scr, ds_scr):
        j = pl.program_id(1)
        valid_c, valid_p0, dist_c, dist_p = _att_masks(d)
        valid = (valid_c, valid_p0 & (j > 0), valid_p0 & (j + 1 < nb))
        dist = (dist_c, dist_p, dist_p)
        for h in range(ATT_HEADS):
            hs = slice(h * e, (h + 1) * e)
            qj, qn = qj_ref[:, hs], qn_ref[:, hs]
            kc, kp = kc_ref[:, hs], kp_ref[:, hs]
            vc, vp = vc_ref[:, hs], vp_ref[:, hs]
            doj, don = doj_ref[:, hs], don_ref[:, hs]
            s_scr[h, 0] = _dot(qj, kc, NT)
            s_scr[h, 1] = _dot(qj, kp, NT)
            s_scr[h, 2] = _dot(qn, kc, NT)
            dp_scr[h, 0] = _dot(doj, vc, NT)
            dp_scr[h, 1] = _dot(doj, vp, NT)
            dp_scr[h, 2] = _dot(don, vc, NT)
        for h in range(ATT_HEADS):
            c = slice(h, h + 1)
            lse_t = (lj_ref[:, c], lj_ref[:, c], ln_ref[:, c])
            dlt_t = (dj_ref[:, c], dj_ref[:, c], dn_ref[:, c])
            for t in range(3):
                s = s_scr[h, t] * scale - slopes[h] * dist[t]
                p = jnp.where(valid[t], jnp.exp(jnp.where(valid[t], s - lse_t[t], NEG)), 0.0)
                p_scr[h, t] = p.astype(BF16)
                ds_scr[h, t] = (p * (dp_scr[h, t] - dlt_t[t])).astype(BF16)
        for h in range(ATT_HEADS):
            hs = slice(h * e, (h + 1) * e)
            dq = _dot(ds_scr[h, 0], kc_ref[:, hs], NN) + _dot(ds_scr[h, 1], kp_ref[:, hs], NN)
            dk = _dot(ds_scr[h, 0], qj_ref[:, hs], TN) + _dot(ds_scr[h, 2], qn_ref[:, hs], TN)
            dv = _dot(p_scr[h, 0], doj_ref[:, hs], TN) + _dot(p_scr[h, 2], don_ref[:, hs], TN)
            dq_ref[:, hs] = (dq * scale).astype(dq_ref.dtype)
            dk_ref[:, hs] = (dk * scale).astype(dk_ref.dtype)
            dv_ref[:, hs] = dv.astype(dv_ref.dtype)

    blk = (b, hd)
    cblk = (b, LANE)
    nxt = lambda i: jnp.minimum(i + 1, nb - 1)
    prv = lambda i: jnp.maximum(i - 1, 0)
    tiles = (ATT_HEADS, 3, b, b)
    dq, dk, dv = pl.pallas_call(
        body, name=f"att_bwd{g}", grid=(d, nb),
        scratch_shapes=[pltpu.VMEM(tiles, F32), pltpu.VMEM(tiles, F32), pltpu.VMEM(tiles, BF16), pltpu.VMEM(tiles, BF16)],
        in_specs=[pl.BlockSpec(blk, lambda r, i: (i, 3 * r + g)),
                  pl.BlockSpec(blk, lambda r, i: (nxt(i), 3 * r + g)),
                  pl.BlockSpec(blk, lambda r, i: (prv(i), 2 * r)),
                  pl.BlockSpec(blk, lambda r, i: (i, 2 * r)),
                  pl.BlockSpec(blk, lambda r, i: (prv(i), 2 * r + 1)),
                  pl.BlockSpec(blk, lambda r, i: (i, 2 * r + 1)),
                  pl.BlockSpec(blk, lambda r, i: (i, r)),
                  pl.BlockSpec(blk, lambda r, i: (nxt(i), r)),
                  pl.BlockSpec(cblk, lambda r, i: (i, r)),
                  pl.BlockSpec(cblk, lambda r, i: (nxt(i), r)),
                  pl.BlockSpec(cblk, lambda r, i: (i, r)),
                  pl.BlockSpec(cblk, lambda r, i: (nxt(i), r))],
        out_specs=(pl.BlockSpec(blk, lambda r, i: (i, r)),) * 3,
        out_shape=(jax.ShapeDtypeStruct((sub, d * hd), BF16),) * 3,
        compiler_params=_params(2))(qv, qv, kvv, kvv, kvv, kvv, dov, dov, dlv, dlv, lv, lv)
    return dq.reshape(t_len, hd), dk.reshape(t_len, hd), dv.reshape(t_len, hd)


def _kv_grad_sum(dks, dvs):
    t_len, hd = dks[0].shape
    tm = _tile(t_len, (256, 128))

    def body(k0, k1, k2, v0, v1, v2, o_ref):
        o_ref[:, :hd] = (k0[...].astype(F32) + k1[...].astype(F32) + k2[...].astype(F32)).astype(o_ref.dtype)
        o_ref[:, hd:] = (v0[...].astype(F32) + v1[...].astype(F32) + v2[...].astype(F32)).astype(o_ref.dtype)

    row = pl.BlockSpec((tm, hd), lambda i: (i, 0))
    return pl.pallas_call(
        body, name="kv_grad_sum", grid=(t_len // tm,), in_specs=[row] * 6,
        out_specs=pl.BlockSpec((tm, 2 * hd), lambda i: (i, 0)),
        out_shape=jax.ShapeDtypeStruct((t_len, 2 * hd), BF16), compiler_params=_params(1))(*dks, *dvs)


HALO = 16
INV_SQRT2 = 1.0 / math.sqrt(2.0)
INV_SQRT2PI = 1.0 / math.sqrt(2.0 * math.pi)


def _conv_taps(g, halo, cw, cb):
    row = lax.broadcasted_iota(jnp.int32, g.shape, 0)
    h1 = halo[HALO - 1:HALO, :]
    h2 = halo[HALO - 2:HALO - 1, :]
    g1 = jnp.where(row == 0, h1, pltpu.roll(g, 1, 0))
    g2 = jnp.where(row == 0, h2, jnp.where(row == 1, h1, pltpu.roll(g, 2, 0)))
    gc = cw[0:1, :] * g2 + cw[1:2, :] * g1 + cw[2:3, :] * g + cb
    return gc, g1, g2


def _glu_specs(t_len, f, tm, tc):
    nj = f // tc
    hb = tm // HALO
    u = pl.BlockSpec((tm, tc), lambda j, i: (i, j))
    g = pl.BlockSpec((tm, tc), lambda j, i: (i, nj + j))
    gh = pl.BlockSpec((HALO, tc), lambda j, i: (jnp.maximum(i * hb - 1, 0), nj + j))
    cw = pl.BlockSpec((8, tc), lambda j, i: (0, j))
    cb = pl.BlockSpec((1, tc), lambda j, i: (0, j))
    return u, g, gh, cw, cb


def _glu_fwd(name, up, cw, cb):
    t_len = up.shape[0]
    f = up.shape[1] // 2
    tm = _tile(t_len, (512, 256, 128))
    tc = _tile(f, (1408, 1024, 512, 256, 128))
    u_s, g_s, gh_s, cw_s, cb_s = _glu_specs(t_len, f, tm, tc)

    def body(u_ref, g_ref, gh_ref, cw_ref, cb_ref, o_ref):
        first = pl.program_id(1) == 0
        halo = jnp.where(first, 0.0, gh_ref[...].astype(F32))
        gc, _, _ = _conv_taps(g_ref[...].astype(F32), halo, cw_ref[...], cb_ref[...])
        gel = 0.5 * gc * (1.0 + lax.erf(gc * INV_SQRT2))
        o_ref[...] = (gel * u_ref[...].astype(F32)).astype(o_ref.dtype)

    return pl.pallas_call(
        body, name=name, grid=(f // tc, t_len // tm), in_specs=[u_s, g_s, gh_s, cw_s, cb_s],
        out_specs=pl.BlockSpec((tm, tc), lambda j, i: (i, j)),
        out_shape=jax.ShapeDtypeStruct((t_len, f), BF16), compiler_params=_params(2))(up, up, up, cw, cb)


def _glu_bwd_a(name, dact, up, cw, cb):
    t_len = up.shape[0]
    f = up.shape[1] // 2
    tm = _tile(t_len, (256, 128))
    tc = _tile(f, (1408, 1024, 512, 256, 128))
    u_s, g_s, gh_s, cw_s, cb_s = _glu_specs(t_len, f, tm, tc)

    def body(da_ref, u_ref, g_ref, gh_ref, cw_ref, cb_ref, du_ref, dgc_ref, w0_ref, w1_ref, w2_ref, b_ref):
        first = pl.program_id(1) == 0
        halo = jnp.where(first, 0.0, gh_ref[...].astype(F32))
        g = g_ref[...].astype(F32)
        gc, g1, g2 = _conv_taps(g, halo, cw_ref[...], cb_ref[...])
        phi = 0.5 * (1.0 + lax.erf(gc * INV_SQRT2))
        dgel = phi + gc * jnp.exp(-0.5 * gc * gc) * INV_SQRT2PI
        da = da_ref[...].astype(F32)
        du_ref[...] = (da * gc * phi).astype(du_ref.dtype)
        dgc = da * u_ref[...].astype(F32) * dgel
        dgc_ref[...] = dgc.astype(dgc_ref.dtype)
        parts = (jnp.sum(dgc * g2, axis=0, keepdims=True), jnp.sum(dgc * g1, axis=0, keepdims=True),
                 jnp.sum(dgc * g, axis=0, keepdims=True), jnp.sum(dgc, axis=0, keepdims=True))
        refs = (w0_ref, w1_ref, w2_ref, b_ref)

        @pl.when(first)
        def _():
            for r, p in zip(refs, parts):
                r[...] = p

        @pl.when(jnp.logical_not(first))
        def _():
            for r, p in zip(refs, parts):
                r[...] += p

    tile = pl.BlockSpec((tm, tc), lambda j, i: (i, j))
    vec = pl.BlockSpec((1, tc), lambda j, i: (0, j))
    vshape = jax.ShapeDtypeStruct((1, f), F32)
    return pl.pallas_call(
        body, name=name, grid=(f // tc, t_len // tm), in_specs=[tile, u_s, g_s, gh_s, cw_s, cb_s],
        out_specs=(tile, tile, vec, vec, vec, vec),
        out_shape=(jax.ShapeDtypeStruct((t_len, f), BF16), jax.ShapeDtypeStruct((t_len, f), BF16),
                   vshape, vshape, vshape, vshape),
        compiler_params=_params(2))(dact, up, up, up, cw, cb)


def _glu_bwd_b(name, du, dgc, cw):
    t_len, f = du.shape
    tm = _tile(t_len, (128, 64))
    hb = tm // HALO
    n_i = t_len // tm
    last_hb = t_len // HALO - 1

    def body(du_ref, d_ref, dh_ref, cw_ref, o_ref):
        last = pl.program_id(0) == n_i - 1
        halo = jnp.where(last, 0.0, dh_ref[...].astype(F32))
        dd = d_ref[...].astype(F32)
        row = lax.broadcasted_iota(jnp.int32, dd.shape, 0)
        h0 = halo[0:1, :]
        h1 = halo[1:2, :]
        d1 = jnp.where(row == tm - 1, h0, pltpu.roll(dd, tm - 1, 0))
        d2 = jnp.where(row == tm - 1, h1, jnp.where(row == tm - 2, h0, pltpu.roll(dd, tm - 2, 0)))
        cwv = cw_ref[...]
        dg = cwv[2:3, :] * dd + cwv[1:2, :] * d1 + cwv[0:1, :] * d2
        o_ref[:, :f] = du_ref[...]
        o_ref[:, f:] = dg.astype(o_ref.dtype)

    row_s = pl.BlockSpec((tm, f), lambda i: (i, 0))
    return pl.pallas_call(
        body, name=name, grid=(n_i,),
        in_specs=[row_s, row_s, pl.BlockSpec((HALO, f), lambda i: (jnp.minimum((i + 1) * hb, last_hb), 0)),
                  pl.BlockSpec((8, f), lambda i: (0, 0))],
        out_specs=pl.BlockSpec((tm, 2 * f), lambda i: (i, 0)),
        out_shape=jax.ShapeDtypeStruct((t_len, 2 * f), BF16), compiler_params=_params(1))(du, dgc, dgc, cw)


def _adamw(name, w, g, m, v):
    rows, cols = w.shape
    tr = _tile(rows, (256, 128, 64, 32, 16, 8))
    c1 = 1.0 / (1.0 - ADAM_B1 ** ADAM_STEP)
    c2 = 1.0 / (1.0 - ADAM_B2 ** ADAM_STEP)

    def body(w_ref, g_ref, m_ref, v_ref, d_ref, nm_ref, nv_ref):
        gv = g_ref[...]
        nm = ADAM_B1 * m_ref[...] + (1.0 - ADAM_B1) * gv
        nv = ADAM_B2 * v_ref[...] + (1.0 - ADAM_B2) * (gv * gv)
        nm_ref[...] = nm
        nv_ref[...] = nv
        d_ref[...] = -ADAM_LR * ((nm * c1) / (jnp.sqrt(nv * c2) + ADAM_EPS) + ADAM_WD * w_ref[...])

    blk = pl.BlockSpec((tr, cols), lambda i: (i, 0))
    shp = jax.ShapeDtypeStruct((rows, cols), F32)
    return pl.pallas_call(body, name=name, grid=(rows // tr,), in_specs=[blk] * 4, out_specs=(blk,) * 3,
                          out_shape=(shp,) * 3, compiler_params=_params(1))(w, g, m, v)


class _NoComm:
    def __init__(self):
        self.grads = {}

    def need(self, group, ws, after):
        return ws

    def reduce(self, group, grads, carry):
        self.grads.update(grads)
        return carry

    def tick(self, carry):
        return carry


def _local_step(x, target, ws, norms, small, hooks):
    lay = _layout()

    w_main, w_a = _unpack_gin(ws["gin"])
    hn0 = _rms_fwd("rms_attn0", x, norms["attn0"])
    proj = _mm_plain("gla_proj", hn0, w_main, NN, F32)
    a = _mm_plain("gla_proj_a", hn0, w_a, NN, BF16)
    ga, cum = _gla_gate_fwd(a, small["w_a2p"], small["b_a2"])
    o_gla, states = _gla_fwd(proj, cum)
    gated = _gla_out_fwd(o_gla, proj, small["head_norm"])
    h1 = _mm_act_wr("gla_out", gated, ws["gout"], lay["gout"], add=x)

    def ffn_fwd(l, h):
        hn = _rms_fwd(f"rms_ffn{l}", h, norms[f"ffn{l}"])
        up = _mm_act_wc(f"ffn_up{l}", hn, ws[f"up{l}"], lay[f"up{l}"], BF16)
        act = _glu_fwd(f"glu_fwd{l}", up, small["conv_w"][l], small["conv_b"][l])
        return hn, up, act, _mm_act_wr(f"ffn_down{l}", act, ws[f"down{l}"], lay[f"down{l}"], add=h)

    ws = hooks.need("B", ws, h1)
    hnf0, up0, act0, h2 = ffn_fwd(0, h1)

    ws = hooks.need("C", ws, h2)
    kvn = _rms_fwd("rms_kv", h2, norms["kv"])
    kv = _mm_act_wc("kv_proj", kvn, ws["wkv"], lay["wkv"], BF16)
    hn1 = _rms_fwd("rms_attn1", h2, norms["attn1"])
    q_all = _mm_act_wc("q_proj", hn1, ws["wq"], lay["wq"], BF16)
    branch = [_att_fwd(q_all, kv, g) for g in range(3)]
    o_att, o_att_b, lse = _att_merge([br[0] for br in branch], [br[1] for br in branch])
    h3 = _mm_act_wr("att_out", o_att_b, ws["dout"], lay["dout"], add=h2)
    hnf1, up1, act1, h4 = ffn_fwd(1, h3)

    dh4, d_final, loss = _loss_head(h4, norms["final"], target)

    sm = {"final": d_final}

    def ffn_bwd(l, dh, h, hn, up, act):
        big = {}
        dact = _mm_dact_wrT(f"ffn_down_dx{l}", dh, ws[f"down{l}"], lay[f"down{l}"])
        big[f"down{l}"] = _mm_grad_wr(f"ffn_down_dw{l}", act, dh, lay[f"down{l}"])
        du, dgc, w0, w1, w2, db = _glu_bwd_a(f"glu_bwd_a{l}", dact, up, small["conv_w"][l], small["conv_b"][l])
        sm[f"conv_w{l}"] = (w0, w1, w2)
        sm[f"conv_b{l}"] = db
        dup = hooks.tick(_glu_bwd_b(f"glu_bwd_b{l}", du, dgc, small["conv_w"][l]))
        dhn = _mm_dact_wcT(f"ffn_up_dx{l}", dup, ws[f"up{l}"], lay[f"up{l}"])
        big[f"up{l}"] = _mm_grad_wc(f"ffn_up_dw{l}", hn, dup, lay[f"up{l}"])
        dh_in, sm[f"ffn{l}"] = _rms_bwd(f"rms_ffn_bwd{l}", dhn, h, norms[f"ffn{l}"], dh)
        return hooks.reduce(f"ffn{l}", big, dh_in)

    dh3 = ffn_bwd(1, dh4, h3, hnf1, up1, act1)

    big = {}
    do_att = _mm_dact_wrT("att_out_dx", dh3, ws["dout"], lay["dout"])
    big["dout"] = _mm_grad_wr("att_out_dw", o_att_b, dh3, lay["dout"])
    delta = _att_delta(do_att, o_att)
    bw = [_att_bwd(q_all, kv, delta, lse, do_att, g) for g in range(3)]
    dq_all = jnp.concatenate([t[0] for t in bw], axis=1)
    dhn1 = _mm_dact_wcT("q_proj_dx", dq_all, ws["wq"], lay["wq"])
    big["wq"] = _mm_grad_wc("q_proj_dw", hn1, dq_all, lay["wq"])
    dh2, sm["attn1"] = _rms_bwd("rms_attn1_bwd", dhn1, h2, norms["attn1"], dh3)
    dkv = hooks.tick(_kv_grad_sum([t[1] for t in bw], [t[2] for t in bw]))
    dkvn = _mm_dact_wcT("kv_proj_dx", dkv, ws["wkv"], lay["wkv"])
    big["wkv"] = _mm_grad_wc("kv_proj_dw", kvn, dkv, lay["wkv"])
    dh2, sm["kv"] = _rms_bwd("rms_kv_bwd", dkvn, h2, norms["kv"], dh2)
    dh2 = hooks.reduce("att", big, dh2)

    dh1 = ffn_bwd(0, dh2, h1, hnf0, up0, act0)

    big = {}
    dgated = _mm_dact_wrT("gla_out_dx", dh1, ws["gout"], lay["gout"])
    big["gout"] = _mm_grad_wr("gla_out_dw", gated, dh1, lay["gout"])
    do_gla, dr, sm["head_norm"] = _gla_out_bwd(dgated, o_gla, proj, small["head_norm"])
    dq, dk, dv, dcum = _gla_bwd(proj, cum, states, hooks.tick(do_gla))
    da, sm["w_a2p"], sm["b_a2"] = _gla_gate_bwd(dcum, ga, a, small["w_a2p"])
    dproj = jnp.concatenate([dq, dk, dv, dr], axis=1)
    dhn0 = _mm_plain("gla_proj_dx", dproj, w_main, NT, F32)
    dhn0 = _mm_plain("gla_proj_a_dx", da, w_a, NT, F32, add=dhn0)
    gin_main = _mm_plain("gla_proj_dw", hn0, dproj, TN, BF16)
    gin_a = _mm_plain("gla_proj_a_dw", hn0, da, TN, BF16)
    big["gin"] = _pack_gin_grad(gin_main, gin_a)
    grad_x, sm["attn0"] = _rms_bwd("rms_attn0_bwd", dhn0, x, norms["attn0"], dh1)
    return loss, grad_x, sm, big


def _pack_weights(chip, names, gla_w_in, gla_w_out, w_kv, dsa_w_q, dsa_w_out, ffn_w_up, ffn_w_down):
    gin = gla_w_in[0]
    gin = jnp.pad(gin, ((0, 0), (0, _roundup(gin.shape[1], LANE) - gin.shape[1])))
    shards = {"gin": gin, "gout": gla_w_out[0], "up0": ffn_w_up[0], "up1": ffn_w_up[1], "down0": ffn_w_down[0],
              "down1": ffn_w_down[1], "wq": dsa_w_q[0], "wkv": w_kv, "dout": dsa_w_out[0]}
    out = {}
    for name in names:
        w = shards[name]
        buf = jnp.zeros((N_CHIPS,) + w.shape, BF16)
        out[name] = lax.dynamic_update_slice(buf, w.astype(BF16)[None], (chip, 0, 0))
    return out


def _unpack_gin(w_gin):
    w = _layout()["gin"][1]
    full = jnp.transpose(w_gin[:, :, :w], (1, 0, 2)).reshape(D_MODEL, N_CHIPS * w)
    n_main = 2 * GLA_KEY_DIM + 2 * GLA_VAL_DIM
    w_a = jnp.pad(full[:, n_main:], ((0, 0), (0, A_PAD - GATE_RANK)))
    return full[:, :n_main], w_a


def _pack_gin_grad(gin_main, gin_a):
    w = _layout()["gin"][1]
    gin = jnp.concatenate([gin_main, gin_a[:, :GATE_RANK]], axis=1)
    gin = jnp.transpose(gin.reshape(D_MODEL, N_CHIPS, w), (1, 0, 2))
    return jnp.pad(gin, ((0, 0), (0, 0), (0, _roundup(w, LANE) - w)))


def _small_params(attn_norm, ffn_norm, kv_norm, final_norm, conv_b, w_a2, b_a2, head_norm, conv_w):
    norms = {"attn0": attn_norm[0:1], "attn1": attn_norm[1:2], "ffn0": ffn_norm[0:1], "ffn1": ffn_norm[1:2],
             "kv": kv_norm[None, :], "final": final_norm[None, :]}
    small = {"w_a2p": jnp.pad(w_a2, ((0, A_PAD - GATE_RANK), (0, 0))), "b_a2": b_a2[None, :],
             "head_norm": head_norm[None, :], "conv_w": jnp.pad(conv_w, ((0, 0), (0, 8 - conv_w.shape[1]), (0, 0))),
             "conv_b": conv_b[:, None, :]}
    return norms, small


ANY = pl.BlockSpec(memory_space=pl.ANY)


def _place():
    return lax.axis_index("x"), lax.axis_index("y"), lax.axis_index("c")


def _other_chips(x, y):
    return [(1 - x, y), (x, 1 - y), (1 - x, 1 - y)]


def _rcopy(src, dst, ssem, rsem, dev):
    return pltpu.make_async_remote_copy(src_ref=src, dst_ref=dst, send_sem=ssem, recv_sem=rsem, device_id=dev,
                                        device_id_type=MESH)


def _pack_shard(name, w, layer, chip_arr, after):
    rows, cols = w.shape[-2:]
    tr = _tile(rows, (512, 352, 256, 128, 64, 32, 16))

    def body(p_ref, w_ref, after_ref, o_ref):
        o_ref[...] = w_ref[...].astype(o_ref.dtype)

    if w.ndim == 3:
        w_spec = pl.BlockSpec((None, tr, cols), lambda i, p: (layer, i, 0))
    else:
        w_spec = pl.BlockSpec((tr, cols), lambda i, p: (i, 0))
    return pl.pallas_call(
        body, name=name,
        grid_spec=pltpu.PrefetchScalarGridSpec(
            num_scalar_prefetch=1, grid=(rows // tr,), in_specs=[w_spec, ANY],
            out_specs=pl.BlockSpec((None, tr, cols), lambda i, p: (p[0], i, 0))),
        out_shape=jax.ShapeDtypeStruct((N_CHIPS, rows, cols), BF16), compiler_params=_params(1))(chip_arr, w, after)


def _swap_halves(name, arrs):
    n = len(arrs)

    def body(*refs):
        ins, outs = refs[:n], refs[n:2 * n]
        send, recv = refs[2 * n:]
        x, y, c = _place()
        cps = []
        for a in range(n):
            h = ins[a].shape[1] // 2
            cp = _rcopy(ins[a].at[:, pl.ds((1 - c) * h, h)], outs[a], send.at[a], recv.at[a], (x, y, 1 - c))
            cp.start()
            cps.append(cp)
        for cp in cps:
            cp.wait()

    return pl.pallas_call(
        body, name=name, in_specs=[ANY] * n, out_specs=[ANY] * n,
        out_shape=[jax.ShapeDtypeStruct((a.shape[0], a.shape[1] // 2, a.shape[2]), a.dtype) for a in arrs],
        scratch_shapes=[pltpu.SemaphoreType.DMA((n,)), pltpu.SemaphoreType.DMA((n,))])(*arrs)


SEM = pl.BlockSpec(memory_space=pltpu.SEMAPHORE)
EFFECT = pltpu.SideEffectType.DATAFLOW_SIDE_EFFECTING


def _shapes(arrs):
    return [jax.ShapeDtypeStruct(a.shape, a.dtype) for a in arrs]


def _gather_start(name, thru, arrs):
    n, nt = len(arrs), len(thru)

    def body(*refs):
        ins = refs[nt:nt + n]
        send, recv = refs[nt + n], refs[nt + n + 1]
        outs = refs[2 * nt + n + 2:]
        x, y, c = _place()
        me = 2 * x + y
        for a in range(n):
            h = ins[a].shape[1] // 2
            mine = pl.ds(c * h, h)
            for j, (px, py) in enumerate(_other_chips(x, y)):
                _rcopy(ins[a].at[me, mine], outs[a].at[me, mine], send.at[3 * a + j], recv.at[3 * a + j], (px, py, c)).start()

    res = pl.pallas_call(
        body, name=name, in_specs=[ANY] * (nt + n), out_specs=[SEM, SEM] + [ANY] * (nt + n),
        out_shape=[pltpu.SemaphoreType.DMA((3 * n,)), pltpu.SemaphoreType.DMA((3 * n,))] + _shapes(thru) + _shapes(arrs),
        input_output_aliases={i: 2 + i for i in range(nt + n)},
        compiler_params=pltpu.CompilerParams(has_side_effects=EFFECT))(*thru, *arrs)
    return res[0], res[1], res[2:2 + nt], res[2 + nt:]


def _gather_wait(name, send, recv, arrs, after):
    n = len(arrs)

    def body(*refs):
        ins = refs[:n]
        send_ref, recv_ref = refs[n], refs[n + 1]
        x, y, c = _place()
        me = 2 * x + y
        for a in range(n):
            h = ins[a].shape[1] // 2
            mine = pl.ds(c * h, h)
            for j, (px, py) in enumerate(_other_chips(x, y)):
                sent = ins[a].at[me, mine]
                landed = ins[a].at[2 * px + py, mine]
                cp = _rcopy(sent, landed, send_ref.at[3 * a + j], recv_ref.at[3 * a + j], (px, py, c))
                cp.wait_send()
                cp.wait_recv()

    return pl.pallas_call(
        body, name=name, in_specs=[ANY] * n + [SEM, SEM, ANY], out_specs=[ANY] * n, out_shape=_shapes(arrs),
        input_output_aliases={a: a for a in range(n)},
        compiler_params=pltpu.CompilerParams(has_side_effects=EFFECT))(*arrs, send, recv, after)


def _forward_halves(name, arrs):
    n = len(arrs)

    def body(*refs):
        ins, outs = refs[:n], refs[n:2 * n]
        send, recv = refs[2 * n:]
        x, y, c = _place()
        sib = (x, y, 1 - c)
        chips = _other_chips(x, y)
        cps = []
        for a in range(n):
            h = ins[a].shape[1] // 2
            mine = pl.ds(c * h, h)
            for j, (px, py) in enumerate(chips):
                cp = _rcopy(ins[a].at[2 * px + py, mine], outs[a].at[2 * px + py, mine], send.at[3 * a + j],
                            recv.at[3 * a + j], sib)
                cp.start()
                cps.append(cp)
        for a in range(n):
            h = ins[a].shape[1] // 2
            theirs = pl.ds((1 - c) * h, h)
            for j, (px, py) in enumerate(chips):
                got = outs[a].at[2 * px + py, theirs]
                _rcopy(got, got, send.at[3 * a + j], recv.at[3 * a + j], sib).wait_recv()
        for cp in cps:
            cp.wait_send()

    return pl.pallas_call(
        body, name=name, in_specs=[ANY] * n, out_specs=[ANY] * n, out_shape=_shapes(arrs),
        input_output_aliases={a: a for a in range(n)},
        scratch_shapes=[pltpu.SemaphoreType.DMA((3 * n,)), pltpu.SemaphoreType.DMA((3 * n,))])(*arrs)


def _scatter_start(name, thru, arrs):
    n, nt = len(arrs), len(thru)
    landing = [jnp.zeros_like(a) for a in arrs]

    def body(*refs):
        ins = refs[nt:nt + n]
        send, recv = refs[nt + 2 * n], refs[nt + 2 * n + 1]
        outs = refs[2 * nt + 3 * n + 2:]
        x, y, c = _place()
        me = 2 * x + y
        for a in range(n):
            for j, (px, py) in enumerate(_other_chips(x, y)):
                _rcopy(ins[a].at[2 * px + py], outs[a].at[me], send.at[3 * a + j], recv.at[3 * a + j], (px, py, c)).start()

    res = pl.pallas_call(
        body, name=name, in_specs=[ANY] * (nt + 2 * n), out_specs=[SEM, SEM] + [ANY] * (nt + 2 * n),
        out_shape=[pltpu.SemaphoreType.DMA((3 * n,)), pltpu.SemaphoreType.DMA((3 * n,))] + _shapes(thru) + _shapes(arrs)
        + _shapes(landing),
        input_output_aliases={i: 2 + i for i in range(nt + 2 * n)},
        compiler_params=pltpu.CompilerParams(has_side_effects=EFFECT))(*thru, *arrs, *landing)
    return res[0], res[1], res[2:2 + nt], res[2 + nt:2 + nt + n], res[2 + nt + n:]


def _scatter_wait(name, send, recv, arrs, landing, after):
    n = len(arrs)

    def body(*refs):
        ins, land = refs[:n], refs[n:2 * n]
        send_ref, recv_ref = refs[2 * n], refs[2 * n + 1]
        x, y, c = _place()
        for a in range(n):
            for j, (px, py) in enumerate(_other_chips(x, y)):
                cp = _rcopy(ins[a].at[2 * px + py], land[a].at[2 * px + py], send_ref.at[3 * a + j], recv_ref.at[3 * a + j],
                            (px, py, c))
                cp.wait_send()
                cp.wait_recv()

    res = pl.pallas_call(
        body, name=name, in_specs=[ANY] * (2 * n) + [SEM, SEM, ANY], out_specs=[ANY] * (2 * n),
        out_shape=_shapes(arrs) + _shapes(landing), input_output_aliases={i: i for i in range(2 * n)},
        compiler_params=pltpu.CompilerParams(has_side_effects=EFFECT))(*arrs, *landing, send, recv, after)
    return res[:n], res[n:]


def _swap_start(name, thru, arrs):
    n, nt = len(arrs), len(thru)
    landing = [lax.empty((a.shape[0], a.shape[1] // 2, a.shape[2]), a.dtype) for a in arrs]

    def body(*refs):
        ins = refs[nt:nt + n]
        send, recv = refs[nt + 2 * n], refs[nt + 2 * n + 1]
        outs = refs[2 * nt + 3 * n + 2:]
        x, y, c = _place()
        for a in range(n):
            h = ins[a].shape[1] // 2
            _rcopy(ins[a].at[:, pl.ds((1 - c) * h, h)], outs[a], send.at[a], recv.at[a], (x, y, 1 - c)).start()

    res = pl.pallas_call(
        body, name=name, in_specs=[ANY] * (nt + 2 * n), out_specs=[SEM, SEM] + [ANY] * (nt + 2 * n),
        out_shape=[pltpu.SemaphoreType.DMA((n,)), pltpu.SemaphoreType.DMA((n,))] + _shapes(thru) + _shapes(arrs)
        + _shapes(landing),
        input_output_aliases={i: 2 + i for i in range(nt + 2 * n)},
        compiler_params=pltpu.CompilerParams(has_side_effects=EFFECT))(*thru, *arrs, *landing)
    return res[0], res[1], res[2:2 + nt], res[2 + nt:2 + nt + n], res[2 + nt + n:]


def _swap_wait(name, send, recv, arrs, landing, after):
    n = len(arrs)

    def body(*refs):
        ins, land = refs[:n], refs[n:2 * n]
        send_ref, recv_ref = refs[2 * n], refs[2 * n + 1]
        x, y, c = _place()
        for a in range(n):
            h = ins[a].shape[1] // 2
            cp = _rcopy(ins[a].at[:, pl.ds((1 - c) * h, h)], land[a], send_ref.at[a], recv_ref.at[a], (x, y, 1 - c))
            cp.wait_send()
            cp.wait_recv()

    res = pl.pallas_call(
        body, name=name, in_specs=[ANY] * (2 * n) + [SEM, SEM, ANY], out_specs=[ANY] * (2 * n),
        out_shape=_shapes(arrs) + _shapes(landing), input_output_aliases={i: i for i in range(2 * n)},
        compiler_params=pltpu.CompilerParams(has_side_effects=EFFECT))(*arrs, *landing, send, recv, after)
    return res[:n], res[n:]


def _join_start(name, arrs):
    n = len(arrs)

    def body(*refs):
        ins = refs[:n]
        send, recv = refs[n], refs[n + 1]
        outs = refs[n + 2:]
        x, y, c = _place()
        for a in range(n):
            h = ins[a].shape[0] // 2
            mine = pl.ds(c * h, h)
            _rcopy(ins[a].at[mine], outs[a].at[mine], send.at[a], recv.at[a], (x, y, 1 - c)).start()

    res = pl.pallas_call(
        body, name=name, in_specs=[ANY] * n, out_specs=[SEM, SEM] + [ANY] * n,
        out_shape=[pltpu.SemaphoreType.DMA((n,)), pltpu.SemaphoreType.DMA((n,))] + _shapes(arrs),
        input_output_aliases={i: 2 + i for i in range(n)},
        compiler_params=pltpu.CompilerParams(has_side_effects=EFFECT))(*arrs)
    return res[0], res[1], res[2:]


def _join_wait(name, send, recv, arrs, after):
    n = len(arrs)

    def body(*refs):
        ins = refs[:n]
        send_ref, recv_ref = refs[n], refs[n + 1]
        x, y, c = _place()
        for a in range(n):
            h = ins[a].shape[0] // 2
            cp = _rcopy(ins[a].at[pl.ds(c * h, h)], ins[a].at[pl.ds((1 - c) * h, h)], send_ref.at[a], recv_ref.at[a],
                        (x, y, 1 - c))
            cp.wait_send()
            cp.wait_recv()

    return pl.pallas_call(
        body, name=name, in_specs=[ANY] * n + [SEM, SEM, ANY], out_specs=[ANY] * n, out_shape=_shapes(arrs),
        input_output_aliases={a: a for a in range(n)},
        compiler_params=pltpu.CompilerParams(has_side_effects=EFFECT))(*arrs, send, recv, after)


def _join_halves(name, arrs):
    n = len(arrs)

    def body(*refs):
        ins, outs = refs[:n], refs[n:2 * n]
        send, recv = refs[2 * n:]
        x, y, c = _place()
        cps = []
        for a in range(n):
            h = ins[a].shape[0] // 2
            mine = pl.ds(c * h, h)
            cp = _rcopy(ins[a].at[mine], outs[a].at[mine], send.at[a], recv.at[a], (x, y, 1 - c))
            cp.start()
            cps.append(cp)
        for a in range(n):
            h = ins[a].shape[0] // 2
            got = outs[a].at[pl.ds((1 - c) * h, h)]
            _rcopy(got, got, send.at[a], recv.at[a], (x, y, 1 - c)).wait_recv()
        for cp in cps:
            cp.wait_send()

    return pl.pallas_call(
        body, name=name, in_specs=[ANY] * n, out_specs=[ANY] * n,
        out_shape=[jax.ShapeDtypeStruct(a.shape, a.dtype) for a in arrs],
        input_output_aliases={a: a for a in range(n)},
        scratch_shapes=[pltpu.SemaphoreType.DMA((n,)), pltpu.SemaphoreType.DMA((n,))])(*arrs)


def _allgather8(name, xs, reduce):
    m_per, n = xs.shape

    def body(x_ref, out_ref, *rest):
        if reduce:
            sum_ref, send, recv, lsem = rest
        else:
            send, recv, lsem = rest
        x, y, c = _place()
        me, sib = (x, y, c), (x, y, 1 - c)
        chips = _other_chips(x, y)

        def rows(px, py, pc):
            return out_ref.at[pl.ds((4 * px + 2 * py + pc) * m_per, m_per), :]

        def copy(k, block, to, src=None):
            return _rcopy(rows(*block) if src is None else src, rows(*block), send.at[k], recv.at[k], to)

        mine = pltpu.make_async_copy(x_ref, rows(*me), lsem)
        mine.start()
        first = [copy(0, me, sib, src=x_ref)]
        first += [copy(1 + j, me, (*chip, c), src=x_ref) for j, chip in enumerate(chips)]
        for cp in first:
            cp.start()
        passed = [copy(4 + j, (*chip, c), sib) for j, chip in enumerate(chips)]
        for j, chip in enumerate(chips):
            copy(1 + j, (*chip, c), me).wait_recv()
            passed[j].start()
        copy(0, sib, me).wait_recv()
        for j, chip in enumerate(chips):
            copy(4 + j, (*chip, 1 - c), me).wait_recv()
        for cp in first + passed:
            cp.wait_send()
        mine.wait()
        if reduce:
            acc = out_ref[pl.ds(0, m_per), :]
            for dev in range(1, N_DEV):
                acc = acc + out_ref[pl.ds(dev * m_per, m_per), :]
            sum_ref[...] = acc

    vm = pl.BlockSpec(memory_space=pltpu.VMEM)
    out_shape = [jax.ShapeDtypeStruct((N_DEV * m_per, n), xs.dtype)]
    if reduce:
        out_shape.append(jax.ShapeDtypeStruct((m_per, n), xs.dtype))
    return pl.pallas_call(
        body, name=name, in_specs=[vm], out_specs=[vm] * len(out_shape), out_shape=out_shape,
        scratch_shapes=[pltpu.SemaphoreType.DMA((7,)), pltpu.SemaphoreType.DMA((7,)), pltpu.SemaphoreType.DMA],
        compiler_params=pltpu.CompilerParams(vmem_limit_bytes=VMEM_LIMIT))(xs)


def _add_my_half(name, a, rb, c_arr):
    s, h, cols = rb.shape
    tr = _tile(h, (512, 352, 256, 128, 64, 32, 16))
    nt = h // tr

    def body(c_ref, a_ref, b_ref, o_ref):
        o_ref[...] = (a_ref[...].astype(F32) + b_ref[...].astype(F32)).astype(o_ref.dtype)

    return pl.pallas_call(
        body, name=name,
        grid_spec=pltpu.PrefetchScalarGridSpec(
            num_scalar_prefetch=1, grid=(s, nt),
            in_specs=[pl.BlockSpec((None, tr, cols), lambda k, i, c: (k, c[0] * nt + i, 0)),
                      pl.BlockSpec((None, tr, cols), lambda k, i, c: (k, i, 0))],
            out_specs=pl.BlockSpec((None, tr, cols), lambda k, i, c: (k, i, 0))),
        out_shape=jax.ShapeDtypeStruct(rb.shape, BF16), compiler_params=_params(2))(c_arr, a, rb)


def _sum_chips(name, own, q, place):
    s, h, cols = q.shape
    tr = _tile(h, (512, 352, 256, 128, 64, 32, 16))
    nt = h // tr

    def body(p_ref, own_ref, q_ref, o_ref):
        chip = p_ref[0]
        acc = jnp.where(chip == 0, own_ref[0], q_ref[0]).astype(F32)
        for j in range(1, s):
            acc = acc + jnp.where(chip == j, own_ref[j], q_ref[j]).astype(F32)
        o_ref[...] = acc

    blk = pl.BlockSpec((s, tr, cols), lambda i, p: (0, i, 0))
    return pl.pallas_call(
        body, name=name,
        grid_spec=pltpu.PrefetchScalarGridSpec(
            num_scalar_prefetch=1, grid=(nt,), in_specs=[blk, blk],
            out_specs=pl.BlockSpec((tr, cols), lambda i, p: (p[1] * nt + i, 0))),
        out_shape=jax.ShapeDtypeStruct((2 * h, cols), F32), compiler_params=_params(1))(place, own, q)


def _pack_rows(parts):
    rows = []
    for p in parts:
        flat = p.reshape(-1).astype(F32)
        n = _roundup(flat.shape[0], 8 * LANE)
        rows.append(jnp.pad(flat, (0, n - flat.shape[0])).reshape(-1, LANE))
    return jnp.concatenate(rows, axis=0)


def _unpack_rows(buf, shapes):
    out, r = [], 0
    for shp in shapes:
        size = math.prod(shp)
        nr = _roundup(size, 8 * LANE) // LANE
        out.append(buf[r:r + nr].reshape(-1)[:size].reshape(shp))
        r += nr
    return out


def kernel(x, attn_norm, gla_w_in, gla_w_a2, gla_b_a2, gla_head_norm, gla_w_out, kv_norm, w_kv, dsa_w_q, dsa_w_out, ffn_norm, ffn_w_up, ffn_conv_w, ffn_conv_b, ffn_w_down, final_norm, loss_target, m_attn_norm, m_gla_w_in, m_gla_w_a2, m_gla_b_a2, m_gla_head_norm, m_gla_w_out, m_kv_norm, m_w_kv, m_dsa_w_q, m_dsa_w_out, m_ffn_norm, m_ffn_w_up, m_ffn_conv_w, m_ffn_conv_b, m_ffn_w_down, m_final_norm, v_attn_norm, v_gla_w_in, v_gla_w_a2, v_gla_b_a2, v_gla_head_norm, v_gla_w_out, v_kv_norm, v_w_kv, v_dsa_w_q, v_dsa_w_out, v_ffn_norm, v_ffn_w_up, v_ffn_conv_w, v_ffn_conv_b, v_ffn_w_down, v_final_norm):
    lay = _layout()
    d, f = D_MODEL, D_FF
    cx, cy, cc = _place()
    chip = 2 * cx + cy
    c_arr = jnp.reshape(cc, (1,)).astype(jnp.int32)
    place = jnp.stack([chip, cc]).astype(jnp.int32)

    groups = {"A": ("gin", "gout", "small"), "B": ("up0", "down0"), "C": ("up1", "down1", "wq", "wkv", "dout")}
    big_shards = (gla_w_in, gla_w_out, w_kv, dsa_w_q, dsa_w_out, ffn_w_up, ffn_w_down)
    ws = _pack_weights(chip, groups["A"][:2], *big_shards)
    sharded_small = [gla_w_a2[0], gla_b_a2[0], gla_head_norm[0], ffn_conv_w]
    packed = _pack_rows(sharded_small)
    packed = jnp.pad(packed, ((0, _roundup(packed.shape[0], 16) - packed.shape[0]), (0, 0)))
    ws["small"] = lax.dynamic_update_slice(jnp.zeros((N_CHIPS,) + packed.shape, F32), packed[None], (chip, 0, 0))
    send, recv, _, arrs = _gather_start("gather_a_start", [], [ws[k] for k in groups["A"]])
    chip_arr = place[:1]
    sources = {"up0": (ffn_w_up, 0), "up1": (ffn_w_up, 1), "down0": (ffn_w_down, 0), "down1": (ffn_w_down, 1),
               "wq": (dsa_w_q, 0), "wkv": (w_kv, 0), "dout": (dsa_w_out, 0)}
    for k in groups["B"] + groups["C"]:
        ws[k] = _pack_shard(f"pack_{k}", *sources[k], chip_arr, arrs[2])
    arrs = _gather_wait("gather_a_wait", send, recv, arrs, ws["dout"])
    ws.update(zip(groups["A"], _forward_halves("forward_a", arrs)))
    in_flight = {}
    thru = [ws[k] for k in groups["A"]]
    for grp in ("B", "C"):
        send, recv, thru, arrs = _gather_start(f"gather_{grp.lower()}_start", thru, [ws[k] for k in groups[grp]])
        ws.update(zip(groups[grp], arrs))
        in_flight[grp] = (send, recv)
    ws.update(zip(groups["A"], thru))
    pending = []

    class _Comm:
        def need(self, grp, ws, after):
            send, recv = in_flight[grp]
            arrs = _gather_wait(f"gather_{grp.lower()}_wait", send, recv, [ws[k] for k in groups[grp]], after)
            arrs = _forward_halves(f"forward_{grp.lower()}", arrs)
            return {**ws, **dict(zip(groups[grp], arrs))}

        swapping = None

        def reduce(self, grp, grads, carry):
            names = list(grads)
            send, recv, thru, parts, theirs = _swap_start(f"swap_{grp}_start", [carry], [grads[k] for k in names])
            self.swapping = (grp, names, send, recv, parts, theirs)
            return thru[0]

        def tick(self, carry):
            if self.swapping is None:
                return carry
            grp, names, send, recv, parts, theirs = self.swapping
            self.swapping = None
            parts, theirs = _swap_wait(f"swap_{grp}_wait", send, recv, parts, theirs, carry)
            return self.scatter(grp, names, parts, theirs, carry)

        def scatter(self, grp, names, parts, theirs, carry):
            sums = [_add_my_half(f"add_half_{k}", a, b, c_arr) for k, a, b in zip(names, parts, theirs)]
            send, recv, thru, sums, landing = _scatter_start(f"scatter_{grp}_start", [carry], sums)
            pending.append((grp, names, send, recv, sums, landing))
            return thru[0]

        def reduce_now(self, grp, grads, carry):
            names = list(grads)
            parts = [grads[k] for k in names]
            return self.scatter(grp, names, parts, _swap_halves(f"swap_{grp}", parts), carry)

    shards = [_unpack_rows(ws["small"][s], [p.shape for p in sharded_small]) for s in range(N_CHIPS)]
    w_a2, b_a2, head_norm, conv_w = [jnp.concatenate([shards[s][k] for s in range(N_CHIPS)], axis=-1) for k in range(4)]
    norms, small = _small_params(attn_norm, ffn_norm, kv_norm, final_norm, ffn_conv_b, w_a2, b_a2, head_norm, conv_w)

    comm = _Comm()
    loss_blk, grad_x, sm, last_big = _local_step(x[0], loss_target[0], ws, norms, small, comm)

    small_parts = [loss_blk, jnp.concatenate([sm["attn0"], sm["attn1"]]), jnp.concatenate([sm["ffn0"], sm["ffn1"]]),
                   sm["kv"], sm["final"], jnp.concatenate([sm["conv_b0"], sm["conv_b1"]]),
                   sm["w_a2p"][:GATE_RANK], sm["b_a2"], sm["head_norm"],
                   jnp.stack([jnp.concatenate(sm["conv_w0"]), jnp.concatenate(sm["conv_w1"])])]
    small_shapes = [(8, LANE), (2, d), (2, d), (d,), (d,), (2, f), (GATE_RANK, GLA_KEY_DIM), (GLA_KEY_DIM,),
                    (GLA_VAL_DIM // GLA_HEADS,), (2, 3, f)]
    _, reduced = _allgather8("reduce_small", _pack_rows(small_parts), True)
    reduced = comm.reduce_now("gla", last_big, reduced)

    loss_r, g_attn, g_ffn, g_kv, g_final, g_cb, g_a2, g_ba2, g_hn, g_cw = _unpack_rows(reduced, small_shapes)
    loss = loss_r[0, 0]

    def mine(g, axis):
        w = g.shape[axis] // N_CHIPS
        return lax.dynamic_slice_in_dim(g, chip * w, w, axis)

    grads = {
        "attn_norm": g_attn, "gla_w_a2": mine(g_a2, 1)[None], "gla_b_a2": mine(g_ba2, 0)[None],
        "gla_head_norm": mine(g_hn, 0)[None], "kv_norm": g_kv, "ffn_norm": g_ffn, "ffn_conv_w": mine(g_cw, 2),
        "ffn_conv_b": g_cb, "final_norm": g_final,
    }
    weights = {"attn_norm": (attn_norm, m_attn_norm, v_attn_norm), "gla_w_in": (gla_w_in, m_gla_w_in, v_gla_w_in),
               "gla_w_a2": (gla_w_a2, m_gla_w_a2, v_gla_w_a2), "gla_b_a2": (gla_b_a2, m_gla_b_a2, v_gla_b_a2),
               "gla_head_norm": (gla_head_norm, m_gla_head_norm, v_gla_head_norm),
               "gla_w_out": (gla_w_out, m_gla_w_out, v_gla_w_out), "kv_norm": (kv_norm, m_kv_norm, v_kv_norm),
               "w_kv": (w_kv, m_w_kv, v_w_kv), "dsa_w_q": (dsa_w_q, m_dsa_w_q, v_dsa_w_q),
               "dsa_w_out": (dsa_w_out, m_dsa_w_out, v_dsa_w_out), "ffn_norm": (ffn_norm, m_ffn_norm, v_ffn_norm),
               "ffn_w_up": (ffn_w_up, m_ffn_w_up, v_ffn_w_up), "ffn_conv_w": (ffn_conv_w, m_ffn_conv_w, v_ffn_conv_w),
               "ffn_conv_b": (ffn_conv_b, m_ffn_conv_b, v_ffn_conv_b),
               "ffn_w_down": (ffn_w_down, m_ffn_w_down, v_ffn_w_down), "final_norm": (final_norm, m_final_norm, v_final_norm)}
    order = list(weights)
    big_names = ("gla_w_in", "gla_w_out", "w_kv", "dsa_w_q", "dsa_w_out", "ffn_w_up", "ffn_w_down")
    delta, new_m, new_v = {}, {}, {}

    def adam_big(k, g):
        w, m, v = weights[k]
        cols = w.shape[-1]
        grads[k] = g
        res = _adamw(f"adamw_{k}", w.reshape(-1, cols), g.reshape(-1, cols), m.reshape(-1, cols), v.reshape(-1, cols))
        delta[k], new_m[k], new_v[k] = [r.reshape(w.shape) for r in res]
        return res[0]

    full = {}
    after = reduced
    joining = []
    for grp, names, send, recv, sums, landing in pending[:-1]:
        sums, landing = _scatter_wait(f"scatter_{grp}_wait", send, recv, sums, landing, after)
        halves = [_sum_chips(f"sum_chips_{k}", s, q, place) for k, s, q in zip(names, sums, landing)]
        send, recv, halves = _join_start(f"join_{grp}_start", halves)
        joining.append((grp, names, send, recv, halves))
        after = halves[0]
    for grp, names, send, recv, halves in joining:
        joined = _join_wait(f"join_{grp}_wait", send, recv, halves, after)
        full.update(zip(names, joined))
        after = joined[0]
    after = adam_big("w_kv", full["wkv"])
    after = adam_big("dsa_w_q", full["wq"][None])
    after = adam_big("dsa_w_out", full["dout"][None])
    after = adam_big("ffn_w_up", jnp.stack([full["up0"], full["up1"]]))
    after = adam_big("ffn_w_down", jnp.stack([full["down0"], full["down1"]]))
    grp, names, send, recv, sums, landing = pending[-1]
    sums, landing = _scatter_wait(f"scatter_{grp}_wait", send, recv, sums, landing, after)
    halves = [_sum_chips(f"sum_chips_{k}", s, q, place) for k, s, q in zip(names, sums, landing)]
    full.update(zip(names, _join_halves(f"join_{grp}", halves)))
    adam_big("gla_w_in", full["gin"][None, :, :lay["gin"][1]])
    adam_big("gla_w_out", full["gout"][None])
    small_names = [k for k in order if k not in big_names]
    packed = [_pack_rows([src[k] for k in small_names])
              for src in ({k: weights[k][0] for k in small_names}, grads, {k: weights[k][1] for k in small_names},
                          {k: weights[k][2] for k in small_names})]
    res = _adamw("adamw_small", *packed)
    shapes = [weights[k][0].shape for k in small_names]
    for dst, buf in zip((delta, new_m, new_v), res):
        for k, val in zip(small_names, _unpack_rows(buf, shapes)):
            dst[k] = val
    return (loss, grad_x[None], *[grads[k] for k in order], *[delta[k] for k in order], *[new_m[k] for k in order],
            *[new_v[k] for k in order])
```

```python
import math

import jax
import jax.numpy as jnp
from jax import lax
from jax.experimental import pallas as pl
from jax.experimental.pallas import tpu as pltpu

F32 = jnp.float32
BF16 = jnp.bfloat16

D_MODEL = 2048
SEQ = 4096
GLA_HEADS = 4
GLA_KEY_DIM = D_MODEL // 2
GLA_VAL_DIM = D_MODEL
GATE_RANK = 16
GATE_NORMALIZER = 16.0
GLA_CHUNK = 64
ATT_HEADS = 16
HEAD_DIM = 128
WINDOWS = (128, 512, 2048)
DILATIONS = (1, 4, 16)
ATT_BLOCK = 128
D_FF = 5632
EPS = 1e-6
ADAM_LR = 0.001
ADAM_B1 = 0.9
ADAM_B2 = 0.999
ADAM_EPS = 1e-08
ADAM_WD = 0.01
ADAM_STEP = 10

N_CHIPS = 4
N_DEV = 8
LANE = 128
A_PAD = 128
VMEM_LIMIT = 56 * 1024 * 1024
MAX_K_TILE = 2816
NEG = -1e30
MESH = pl.DeviceIdType.MESH

NN = (((1,), (0,)), ((), ()))
NT = (((1,), (1,)), ((), ()))
TN = (((0,), (0,)), ((), ()))


def _tile(n, cands):
    for c in cands:
        if c <= n and n % c == 0:
            return c
    return n


def _roundup(n, m):
    return -(-n // m) * m


def _params(n_axes):
    return pltpu.CompilerParams(dimension_semantics=("arbitrary",) * n_axes, vmem_limit_bytes=VMEM_LIMIT)


def _dot(a, b, dims):
    return lax.dot_general(a, b, dims, preferred_element_type=F32)


def _sigmoid(x):
    return 1.0 / (1.0 + jnp.exp(-x))


COL_SHARDED = ("gin", "up0", "up1", "wq", "wkv")
ROW_SHARDED = ("gout", "down0", "down1", "dout")


def _layout():
    f = D_FF
    hd = ATT_HEADS * HEAD_DIM
    gin = 2 * GLA_KEY_DIM + 2 * GLA_VAL_DIM + GATE_RANK
    up_w = 2 * f // N_CHIPS
    q_w = 3 * hd // N_CHIPS
    kv_w = 2 * hd // N_CHIPS
    dn_r = f // N_CHIPS
    go_r = GLA_VAL_DIM // N_CHIPS
    do_r = hd // N_CHIPS
    big = (1408, 1024, 512, 256, 128)
    return {
        "gin": (0, gin // N_CHIPS, LANE),
        "up0": (0, up_w, _tile(up_w, big)), "up1": (0, up_w, _tile(up_w, big)),
        "wq": (0, q_w, _tile(q_w, (512, 384, 256, 128))), "wkv": (0, kv_w, _tile(kv_w, (1024, 512, 256, 128))),
        "down0": (0, dn_r, _tile(dn_r, big)), "down1": (0, dn_r, _tile(dn_r, big)),
        "gout": (0, go_r, _tile(go_r, (512, 256, 128))), "dout": (0, do_r, _tile(do_r, (512, 256, 128))),
    }


def _matmul(name, a, b, dims, grid, a_spec, b_spec, o_spec, out_shape, acc_shape, add=None, add_spec=None):
    nk = grid[2]
    has_add = add is not None

    def body(*refs):
        a_ref, b_ref = refs[0], refs[1]
        pos = 2
        add_ref = None
        if has_add:
            add_ref = refs[pos]
            pos += 1
        o_ref = refs[pos]
        prod = _dot(a_ref[...].astype(BF16), b_ref[...].astype(BF16), dims)

        def finish(val):
            if has_add:
                val = val + add_ref[...].astype(F32)
            o_ref[...] = val.astype(o_ref.dtype)

        if nk == 1:
            finish(prod)
        else:
            acc_ref = refs[pos + 1]
            k = pl.program_id(2)

            @pl.when(k == 0)
            def _():
                acc_ref[...] = prod

            @pl.when(k > 0)
            def _():
                acc_ref[...] += prod

            @pl.when(k == nk - 1)
            def _():
                finish(acc_ref[...])

    in_specs = [a_spec, b_spec]
    args = [a, b]
    if has_add:
        in_specs.append(add_spec)
        args.append(add)
    scratch = [] if nk == 1 else [pltpu.VMEM(acc_shape, F32)]
    return pl.pallas_call(body, name=name, grid=grid, in_specs=in_specs, out_specs=o_spec, out_shape=out_shape,
                          scratch_shapes=scratch, compiler_params=_params(3))(*args)


def _mm_act_wc(name, a, wc, seg, out_dtype):
    off, w, tn = seg
    t_len, d = a.shape
    tm = _tile(t_len, (1024, 512, 256, 128))
    nps = w // tn
    ob = off // tn
    grid = (t_len // tm, N_CHIPS * nps, 1)
    return _matmul(
        name, a, wc, NN, grid,
        pl.BlockSpec((tm, d), lambda i, j, k: (i, 0)),
        pl.BlockSpec((None, d, tn), lambda i, j, k: (j // nps, 0, ob + j % nps)),
        pl.BlockSpec((tm, tn), lambda i, j, k: (i, j)),
        jax.ShapeDtypeStruct((t_len, N_CHIPS * w), out_dtype), (tm, tn))


def _mm_dact_wcT(name, dy, wc, seg, add=None):
    off, w, tk = seg
    if off == 0 and w <= MAX_K_TILE:
        tk = w
    t_len = dy.shape[0]
    d = wc.shape[1]
    tm = _tile(t_len, (1024, 512, 256, 128))
    tn = _tile(d, (1024, 512, 256, 128))
    kps = w // tk
    ob = off // tk
    grid = (t_len // tm, d // tn, N_CHIPS * kps)
    return _matmul(
        name, dy, wc, NT, grid,
        pl.BlockSpec((tm, tk), lambda i, j, k: (i, k)),
        pl.BlockSpec((None, tn, tk), lambda i, j, k: (k // kps, j, ob + k % kps)),
        pl.BlockSpec((tm, tn), lambda i, j, k: (i, j)),
        jax.ShapeDtypeStruct((t_len, d), F32), (tm, tn),
        add=add, add_spec=None if add is None else pl.BlockSpec((tm, tn), lambda i, j, k: (i, j)))


def _mm_grad_wc(name, a, dy, seg):
    _, w, tn = seg
    t_len, d = a.shape
    tm = _tile(d, (1024, 512, 256, 128))
    tk = _tile(t_len, (2048, 1024, 512, 256, 128))
    nps = w // tn
    grid = (d // tm, N_CHIPS * nps, t_len // tk)
    return _matmul(
        name, a, dy, TN, grid,
        pl.BlockSpec((tk, tm), lambda i, j, k: (k, i)),
        pl.BlockSpec((tk, tn), lambda i, j, k: (k, j)),
        pl.BlockSpec((None, tm, tn), lambda i, j, k: (j // nps, i, j % nps)),
        jax.ShapeDtypeStruct((N_CHIPS, d, w), BF16), (tm, tn))


def _is_plain(wr, seg):
    return seg[0] == 0 and wr.shape[1] == seg[1] and (N_CHIPS * seg[1]) % 1024 == 0


def _mm_act_wr(name, a, wr, seg, add):
    off, r, tk = seg
    t_len = a.shape[0]
    d = wr.shape[2]
    if seg[0] == 0 and wr.shape[1] == r:
        return _mm_plain(name, a, wr.reshape(N_CHIPS * r, d), NN, F32, add=add)
    tm = _tile(t_len, (1024, 512, 256, 128))
    tn = _tile(d, (1024, 512, 256, 128))
    kps = r // tk
    ob = off // tk
    grid = (t_len // tm, d // tn, N_CHIPS * kps)
    return _matmul(
        name, a, wr, NN, grid,
        pl.BlockSpec((tm, tk), lambda i, j, k: (i, k)),
        pl.BlockSpec((None, tk, tn), lambda i, j, k: (k // kps, ob + k % kps, j)),
        pl.BlockSpec((tm, tn), lambda i, j, k: (i, j)),
        jax.ShapeDtypeStruct((t_len, d), F32), (tm, tn),
        add=add, add_spec=pl.BlockSpec((tm, tn), lambda i, j, k: (i, j)))


def _mm_dact_wrT(name, dh, wr, seg):
    off, r, tn = seg
    t_len, d = dh.shape
    if _is_plain(wr, seg):
        return _mm_plain(name, dh, wr.reshape(N_CHIPS * r, d), NT, BF16)
    tm = _tile(t_len, (1024, 512, 256, 128))
    nps = r // tn
    ob = off // tn
    grid = (t_len // tm, N_CHIPS * nps, 1)
    return _matmul(
        name, dh, wr, NT, grid,
        pl.BlockSpec((tm, d), lambda i, j, k: (i, 0)),
        pl.BlockSpec((None, tn, d), lambda i, j, k: (j // nps, ob + j % nps, 0)),
        pl.BlockSpec((tm, tn), lambda i, j, k: (i, j)),
        jax.ShapeDtypeStruct((t_len, N_CHIPS * r), BF16), (tm, tn))


def _mm_grad_wr(name, a, dh, seg):
    _, r, tm = seg
    t_len, d = dh.shape
    if (N_CHIPS * r) % 1024 == 0:
        return _mm_plain(name, a, dh, TN, BF16).reshape(N_CHIPS, r, d)
    tn = _tile(d, (1024, 512, 256, 128))
    tk = _tile(t_len, (2048, 1024, 512, 256, 128))
    mps = r // tm
    grid = (N_CHIPS * mps, d // tn, t_len // tk)
    return _matmul(
        name, a, dh, TN, grid,
        pl.BlockSpec((tk, tm), lambda i, j, k: (k, i)),
        pl.BlockSpec((tk, tn), lambda i, j, k: (k, j)),
        pl.BlockSpec((None, tm, tn), lambda i, j, k: (i // mps, i % mps, j)),
        jax.ShapeDtypeStruct((N_CHIPS, r, d), BF16), (tm, tn))


def _mm_plain(name, a, b, dims, out_dtype, add=None):
    if dims == NN:
        m, kd = a.shape
        n = b.shape[1]
    elif dims == NT:
        m, kd = a.shape
        n = b.shape[0]
    else:
        kd, m = a.shape
        n = b.shape[1]
    tm = _tile(m, (1024, 512, 256, 128))
    tn = _tile(n, (1024, 768, 512, 256, 128))
    tk = _tile(kd, (MAX_K_TILE, 2048, 1408, 1024, 512, 256, 128))
    grid = (m // tm, n // tn, kd // tk)
    if dims == NN:
        a_spec = pl.BlockSpec((tm, tk), lambda i, j, k: (i, k))
        b_spec = pl.BlockSpec((tk, tn), lambda i, j, k: (k, j))
    elif dims == NT:
        a_spec = pl.BlockSpec((tm, tk), lambda i, j, k: (i, k))
        b_spec = pl.BlockSpec((tn, tk), lambda i, j, k: (j, k))
    else:
        a_spec = pl.BlockSpec((tk, tm), lambda i, j, k: (k, i))
        b_spec = pl.BlockSpec((tk, tn), lambda i, j, k: (k, j))
    o_spec = pl.BlockSpec((tm, tn), lambda i, j, k: (i, j))
    return _matmul(name, a, b, dims, grid, a_spec, b_spec, o_spec, jax.ShapeDtypeStruct((m, n), out_dtype), (tm, tn),
                   add=add, add_spec=None if add is None else o_spec)


def _rms_fwd(name, x, g):
    t_len, d = x.shape
    tm = _tile(t_len, (512, 256, 128))

    def body(x_ref, g_ref, o_ref):
        xv = x_ref[...]
        r = lax.rsqrt(jnp.mean(xv * xv, axis=-1, keepdims=True) + EPS)
        o_ref[...] = (xv * r * g_ref[...]).astype(o_ref.dtype)

    return pl.pallas_call(
        body, name=name, grid=(t_len // tm,),
        in_specs=[pl.BlockSpec((tm, d), lambda i: (i, 0)), pl.BlockSpec((1, d), lambda i: (0, 0))],
        out_specs=pl.BlockSpec((tm, d), lambda i: (i, 0)),
        out_shape=jax.ShapeDtypeStruct((t_len, d), BF16), compiler_params=_params(1))(x, g)


def _rms_bwd(name, dy, x, g, dres):
    t_len, d = x.shape
    tm = _tile(t_len, (256, 128))

    def body(dy_ref, x_ref, g_ref, dres_ref, dx_ref, dg_ref):
        xv = x_ref[...]
        r = lax.rsqrt(jnp.mean(xv * xv, axis=-1, keepdims=True) + EPS)
        xhat = xv * r
        dyv = dy_ref[...].astype(F32)
        dxn = dyv * g_ref[...]
        dx = r * (dxn - xhat * jnp.mean(dxn * xhat, axis=-1, keepdims=True))
        dx_ref[...] = dres_ref[...] + dx
        part = jnp.sum(dyv * xhat, axis=0, keepdims=True)

        @pl.when(pl.program_id(0) == 0)
        def _():
            dg_ref[...] = part

        @pl.when(pl.program_id(0) > 0)
        def _():
            dg_ref[...] += part

    row = pl.BlockSpec((tm, d), lambda i: (i, 0))
    vec = pl.BlockSpec((1, d), lambda i: (0, 0))
    return pl.pallas_call(
        body, name=name, grid=(t_len // tm,), in_specs=[row, row, vec, row], out_specs=(row, vec),
        out_shape=(jax.ShapeDtypeStruct((t_len, d), F32), jax.ShapeDtypeStruct((1, d), F32)),
        compiler_params=_params(1))(dy, x, g, dres)


def _loss_head(h, g, target):
    t_len, d = h.shape
    tm = _tile(t_len, (256, 128))

    def body(h_ref, g_ref, t_ref, dh_ref, dg_ref, loss_ref):
        xv = h_ref[...]
        gv = g_ref[...]
        r = lax.rsqrt(jnp.mean(xv * xv, axis=-1, keepdims=True) + EPS)
        xhat = xv * r
        err = xhat * gv - t_ref[...]
        dyv = err * (1.0 / d)
        dxn = dyv * gv
        dh_ref[...] = r * (dxn - xhat * jnp.mean(dxn * xhat, axis=-1, keepdims=True))
        part = jnp.sum(dyv * xhat, axis=0, keepdims=True)
        lpart = jnp.zeros((8, LANE), F32) + (0.5 / d) * jnp.sum(err * err)

        @pl.when(pl.program_id(0) == 0)
        def _():
            dg_ref[...] = part
            loss_ref[...] = lpart

        @pl.when(pl.program_id(0) > 0)
        def _():
            dg_ref[...] += part
            loss_ref[...] += lpart

    row = pl.BlockSpec((tm, d), lambda i: (i, 0))
    vec = pl.BlockSpec((1, d), lambda i: (0, 0))
    return pl.pallas_call(
        body, name="loss_head", grid=(t_len // tm,), in_specs=[row, vec, row],
        out_specs=(row, vec, pl.BlockSpec((8, LANE), lambda i: (0, 0))),
        out_shape=(jax.ShapeDtypeStruct((t_len, d), F32), jax.ShapeDtypeStruct((1, d), F32),
                   jax.ShapeDtypeStruct((8, LANE), F32)),
        compiler_params=_params(1))(h, g, target)


def _chunk_row(shape):
    return lax.broadcasted_iota(jnp.int32, shape, 0) % GLA_CHUNK


def _gla_gate_fwd(a, w_a2p, b_a2):
    t_len = a.shape[0]
    kd = w_a2p.shape[1]
    tm = _tile(t_len, (256, 128, 64))

    def body(a_ref, w_ref, b_ref, ga_ref, cum_ref):
        ga = _dot(a_ref[...], w_ref[...].astype(BF16), NN) + b_ref[...]
        ga_ref[...] = ga
        la = (jnp.minimum(ga, 0.0) - jnp.log(1.0 + jnp.exp(-jnp.abs(ga)))) * (1.0 / GATE_NORMALIZER)
        row = _chunk_row(la.shape)
        s = 1
        while s < GLA_CHUNK:
            la = la + jnp.where(row >= s, pltpu.roll(la, s, 0), 0.0)
            s *= 2
        cum_ref[...] = la

    return pl.pallas_call(
        body, name="gla_gate_fwd", grid=(t_len // tm,),
        in_specs=[pl.BlockSpec((tm, A_PAD), lambda i: (i, 0)), pl.BlockSpec((A_PAD, kd), lambda i: (0, 0)),
                  pl.BlockSpec((1, kd), lambda i: (0, 0))],
        out_specs=(pl.BlockSpec((tm, kd), lambda i: (i, 0)), pl.BlockSpec((tm, kd), lambda i: (i, 0))),
        out_shape=(jax.ShapeDtypeStruct((t_len, kd), F32), jax.ShapeDtypeStruct((t_len, kd), F32)),
        compiler_params=_params(1))(a, w_a2p, b_a2)


def _gla_gate_bwd(dcum, ga, a, w_a2p):
    t_len, kd = dcum.shape
    tm = _tile(t_len, (256, 128, 64))

    def body(dc_ref, ga_ref, a_ref, w_ref, da_ref, dw_ref, db_ref):
        x = dc_ref[...]
        row = _chunk_row(x.shape)
        s = 1
        while s < GLA_CHUNK:
            x = x + jnp.where(row < GLA_CHUNK - s, pltpu.roll(x, tm - s, 0), 0.0)
            s *= 2
        dga = x * (1.0 / GATE_NORMALIZER) * _sigmoid(-ga_ref[...])
        dgab = dga.astype(BF16)
        da_ref[...] = _dot(dgab, w_ref[...].astype(BF16), NT).astype(da_ref.dtype)
        dw = _dot(a_ref[...], dgab, TN)
        db = jnp.sum(dga, axis=0, keepdims=True)

        @pl.when(pl.program_id(0) == 0)
        def _():
            dw_ref[...] = dw
            db_ref[...] = db

        @pl.when(pl.program_id(0) > 0)
        def _():
            dw_ref[...] += dw
            db_ref[...] += db

    wide = pl.BlockSpec((tm, kd), lambda i: (i, 0))
    return pl.pallas_call(
        body, name="gla_gate_bwd", grid=(t_len // tm,),
        in_specs=[wide, wide, pl.BlockSpec((tm, A_PAD), lambda i: (i, 0)), pl.BlockSpec((A_PAD, kd), lambda i: (0, 0))],
        out_specs=(pl.BlockSpec((tm, A_PAD), lambda i: (i, 0)), pl.BlockSpec((A_PAD, kd), lambda i: (0, 0)),
                   pl.BlockSpec((1, kd), lambda i: (0, 0))),
        out_shape=(jax.ShapeDtypeStruct((t_len, A_PAD), BF16), jax.ShapeDtypeStruct((A_PAD, kd), F32),
                   jax.ShapeDtypeStruct((1, kd), F32)),
        compiler_params=_params(1))(dcum, ga, a, w_a2p)


GLA_STEP_CHUNKS = 4


def _gla_dims():
    dk = GLA_KEY_DIM // GLA_HEADS
    dv = GLA_VAL_DIM // GLA_HEADS
    return dk, dv


def _gla_fwd(proj, cum):
    t_len = proj.shape[0]
    dk, dv = _gla_dims()
    nc = t_len // GLA_CHUNK
    c = GLA_CHUNK
    scale = dk ** -0.5
    v0 = 2 * GLA_KEY_DIM // dv

    per = _tile(nc, (GLA_STEP_CHUNKS, 2, 1))
    rows = per * c

    def body(q_ref, k_ref, v_ref, cum_ref, o_ref, st_ref, s_scr):
        @pl.when(pl.program_id(1) == 0)
        def _():
            s_scr[...] = jnp.zeros_like(s_scr)

        tri = lax.broadcasted_iota(jnp.int32, (c, c), 0) >= lax.broadcasted_iota(jnp.int32, (c, c), 1)
        for i in range(per):
            rs = slice(i * c, (i + 1) * c)
            cm = cum_ref[rs, :]
            last = cm[c - 1:c, :]
            q = q_ref[rs, :].astype(F32) * scale
            k = k_ref[rs, :].astype(F32)
            v = v_ref[rs, :].astype(BF16)
            qd = (q * jnp.exp(cm)).astype(BF16)
            ki = (k * jnp.exp(-cm)).astype(BF16)
            ke = (k * jnp.exp(last - cm)).astype(BF16)
            sc = jnp.where(tri, _dot(qd, ki, NT), 0.0)
            st = s_scr[...]
            st_ref[i] = st
            o_ref[rs, :] = _dot(sc.astype(BF16), v, NN) + _dot(qd, st.astype(BF16), NT)
            s_scr[...] = st * jnp.exp(last) + _dot(v, ke, TN)

    return pl.pallas_call(
        body, name="gla_fwd", grid=(GLA_HEADS, nc // per),
        in_specs=[pl.BlockSpec((rows, dk), lambda h, n: (n, h)),
                  pl.BlockSpec((rows, dk), lambda h, n: (n, GLA_HEADS + h)),
                  pl.BlockSpec((rows, dv), lambda h, n: (n, v0 + h)),
                  pl.BlockSpec((rows, dk), lambda h, n: (n, h))],
        out_specs=(pl.BlockSpec((rows, dv), lambda h, n: (n, h)),
                   pl.BlockSpec((None, per, dv, dk), lambda h, n: (h, n, 0, 0))),
        out_shape=(jax.ShapeDtypeStruct((t_len, GLA_VAL_DIM), F32),
                   jax.ShapeDtypeStruct((GLA_HEADS, nc, dv, dk), F32)),
        scratch_shapes=[pltpu.VMEM((dv, dk), F32)], compiler_params=_params(2))(proj, proj, proj, cum)


def _gla_bwd(proj, cum, states, do):
    t_len = proj.shape[0]
    dk, dv = _gla_dims()
    nc = t_len // GLA_CHUNK
    c = GLA_CHUNK
    scale = dk ** -0.5
    v0 = 2 * GLA_KEY_DIM // dv

    per = _tile(nc, (GLA_STEP_CHUNKS, 2, 1))
    rows = per * c

    def body(q_ref, k_ref, v_ref, cum_ref, st_ref, do_ref, dq_ref, dk_ref, dv_ref, dc_ref, ds_scr):
        @pl.when(pl.program_id(1) == 0)
        def _():
            ds_scr[...] = jnp.zeros_like(ds_scr)

        tri = lax.broadcasted_iota(jnp.int32, (c, c), 0) >= lax.broadcasted_iota(jnp.int32, (c, c), 1)
        row = lax.broadcasted_iota(jnp.int32, (c, dk), 0)
        for i in reversed(range(per)):
            rs = slice(i * c, (i + 1) * c)
            cm = cum_ref[rs, :]
            last = cm[c - 1:c, :]
            e_c = jnp.exp(cm)
            e_nc = jnp.exp(-cm)
            e_lc = jnp.exp(last - cm)
            e_l = jnp.exp(last)
            q = q_ref[rs, :].astype(F32) * scale
            k = k_ref[rs, :].astype(F32)
            v = v_ref[rs, :].astype(BF16)
            dov = do_ref[rs, :]
            qd32 = q * e_c
            ki32 = k * e_nc
            ke32 = k * e_lc
            qd = qd32.astype(BF16)
            ki = ki32.astype(BF16)
            ke = ke32.astype(BF16)
            st = st_ref[i]
            dst = ds_scr[...]
            dstb = dst.astype(BF16)
            am = jnp.where(tri, _dot(dov, v, NT), 0.0).astype(BF16)
            pm = jnp.where(tri, _dot(qd, ki, NT), 0.0).astype(BF16)
            dqd = _dot(am, ki, NN) + _dot(dov, st.astype(BF16), NN)
            dki = _dot(am, qd, TN)
            dvv = _dot(pm, dov, TN) + _dot(ke, dstb, NT)
            dke = _dot(v, dstb, NN)
            d_el = jnp.sum(dst * st, axis=0, keepdims=True)
            ds_scr[...] = dst * e_l + _dot(dov, qd, TN)
            dq_ref[rs, :] = (dqd * scale * e_c).astype(dq_ref.dtype)
            dk_ref[rs, :] = (dki * e_nc + dke * e_lc).astype(dk_ref.dtype)
            dv_ref[rs, :] = dvv.astype(dv_ref.dtype)
            dkeke = dke * ke32
            dcum = dqd * qd32 - dki * ki32 - dkeke
            dlast = jnp.sum(dkeke, axis=0, keepdims=True) + d_el * e_l
            dc_ref[rs, :] = jnp.where(row == c - 1, dcum + dlast, dcum)

    rev = nc // per - 1
    return pl.pallas_call(
        body, name="gla_bwd", grid=(GLA_HEADS, nc // per),
        in_specs=[pl.BlockSpec((rows, dk), lambda h, n: (rev - n, h)),
                  pl.BlockSpec((rows, dk), lambda h, n: (rev - n, GLA_HEADS + h)),
                  pl.BlockSpec((rows, dv), lambda h, n: (rev - n, v0 + h)),
                  pl.BlockSpec((rows, dk), lambda h, n: (rev - n, h)),
                  pl.BlockSpec((None, per, dv, dk), lambda h, n: (h, rev - n, 0, 0)),
                  pl.BlockSpec((rows, dv), lambda h, n: (rev - n, h))],
        out_specs=(pl.BlockSpec((rows, dk), lambda h, n: (rev - n, h)),
                   pl.BlockSpec((rows, dk), lambda h, n: (rev - n, h)),
                   pl.BlockSpec((rows, dv), lambda h, n: (rev - n, h)),
                   pl.BlockSpec((rows, dk), lambda h, n: (rev - n, h))),
        out_shape=(jax.ShapeDtypeStruct((t_len, GLA_KEY_DIM), BF16), jax.ShapeDtypeStruct((t_len, GLA_KEY_DIM), BF16),
                   jax.ShapeDtypeStruct((t_len, GLA_VAL_DIM), BF16), jax.ShapeDtypeStruct((t_len, GLA_KEY_DIM), F32)),
        scratch_shapes=[pltpu.VMEM((dv, dk), F32)], compiler_params=_params(2))(proj, proj, proj, cum, states, do)


def _gla_out_fwd(o, proj, gn):
    t_len = o.shape[0]
    _, dv = _gla_dims()
    tm = _tile(t_len, (512, 256, 128))
    r0 = (2 * GLA_KEY_DIM + GLA_VAL_DIM) // dv

    def body(o_ref, r_ref, g_ref, y_ref):
        ov = o_ref[...]
        rs = lax.rsqrt(jnp.mean(ov * ov, axis=-1, keepdims=True) + EPS)
        rv = r_ref[...].astype(F32)
        y_ref[...] = (ov * rs * g_ref[...] * (rv * _sigmoid(rv))).astype(y_ref.dtype)

    return pl.pallas_call(
        body, name="gla_out_fwd", grid=(t_len // tm, GLA_HEADS),
        in_specs=[pl.BlockSpec((tm, dv), lambda i, h: (i, h)), pl.BlockSpec((tm, dv), lambda i, h: (i, r0 + h)),
                  pl.BlockSpec((1, dv), lambda i, h: (0, 0))],
        out_specs=pl.BlockSpec((tm, dv), lambda i, h: (i, h)),
        out_shape=jax.ShapeDtypeStruct((t_len, GLA_VAL_DIM), BF16), compiler_params=_params(2))(o, proj, gn)


def _gla_out_bwd(dy, o, proj, gn):
    t_len = o.shape[0]
    _, dv = _gla_dims()
    tm = _tile(t_len, (512, 256, 128))
    r0 = (2 * GLA_KEY_DIM + GLA_VAL_DIM) // dv

    def body(dy_ref, o_ref, r_ref, g_ref, do_ref, dr_ref, dg_ref):
        ov = o_ref[...]
        gv = g_ref[...]
        rs = lax.rsqrt(jnp.mean(ov * ov, axis=-1, keepdims=True) + EPS)
        xhat = ov * rs
        rv = r_ref[...].astype(F32)
        sg = _sigmoid(rv)
        gate = rv * sg
        dyv = dy_ref[...].astype(F32)
        dn = dyv * gate
        dr_ref[...] = (dyv * xhat * gv * (sg * (1.0 + rv * (1.0 - sg)))).astype(dr_ref.dtype)
        dxn = dn * gv
        do_ref[...] = (rs * (dxn - xhat * jnp.mean(dxn * xhat, axis=-1, keepdims=True))).astype(do_ref.dtype)
        part = jnp.sum(dn * xhat, axis=0, keepdims=True)
        first = (pl.program_id(0) == 0) & (pl.program_id(1) == 0)

        @pl.when(first)
        def _():
            dg_ref[...] = part

        @pl.when(jnp.logical_not(first))
        def _():
            dg_ref[...] += part

    blk = pl.BlockSpec((tm, dv), lambda i, h: (i, h))
    return pl.pallas_call(
        body, name="gla_out_bwd", grid=(t_len // tm, GLA_HEADS),
        in_specs=[blk, blk, pl.BlockSpec((tm, dv), lambda i, h: (i, r0 + h)), pl.BlockSpec((1, dv), lambda i, h: (0, 0))],
        out_specs=(blk, blk, pl.BlockSpec((1, dv), lambda i, h: (0, 0))),
        out_shape=(jax.ShapeDtypeStruct((t_len, GLA_VAL_DIM), BF16), jax.ShapeDtypeStruct((t_len, GLA_VAL_DIM), BF16),
                   jax.ShapeDtypeStruct((1, dv), F32)),
        compiler_params=_params(2))(dy, o, proj, gn)


def _alibi_slopes():
    n = ATT_HEADS
    start = 2.0 ** (-8.0 / n)
    return [start ** (i + 1) for i in range(n)]


def _att_masks(d):
    b = ATT_BLOCK
    qa = lax.broadcasted_iota(jnp.int32, (b, b), 0)
    kb = lax.broadcasted_iota(jnp.int32, (b, b), 1)
    dist_c = qa - kb
    dist_p = qa - kb + b
    return dist_c >= 0, dist_p <= b, (dist_c * d).astype(F32), (dist_p * d).astype(F32)


def _att_fwd(q_all, kv, g):
    d = DILATIONS[g]
    assert WINDOWS[g] // d == ATT_BLOCK
    t_len = q_all.shape[0]
    hd = ATT_HEADS * HEAD_DIM
    sub = t_len // d
    nb = sub // ATT_BLOCK
    b = ATT_BLOCK
    e = HEAD_DIM
    scale = e ** -0.5
    slopes = _alibi_slopes()
    qv = q_all.reshape(sub, d * 3 * hd)
    kvv = kv.reshape(sub, d * 2 * hd)

    def body(q_ref, kp_ref, kc_ref, vp_ref, vc_ref, o_ref, l_ref, s_scr, p_scr, li_scr):
        ib = pl.program_id(1)
        valid_c, valid_p0, dist_c, dist_p = _att_masks(d)
        valid_p = valid_p0 & (ib > 0)
        for h in range(ATT_HEADS):
            hs = slice(h * e, (h + 1) * e)
            qh = q_ref[:, hs]
            s_scr[h, 0] = _dot(qh, kc_ref[:, hs], NT)
            s_scr[h, 1] = _dot(qh, kp_ref[:, hs], NT)
        l_ref[...] = jnp.zeros_like(l_ref)
        for h in range(ATT_HEADS):
            s_c = jnp.where(valid_c, s_scr[h, 0] * scale - slopes[h] * dist_c, NEG)
            s_p = jnp.where(valid_p, s_scr[h, 1] * scale - slopes[h] * dist_p, NEG)
            m = jnp.maximum(jnp.max(s_c, axis=1, keepdims=True), jnp.max(s_p, axis=1, keepdims=True))
            p_c = jnp.where(valid_c, jnp.exp(s_c - m), 0.0)
            p_p = jnp.where(valid_p, jnp.exp(s_p - m), 0.0)
            l = jnp.sum(p_c, axis=1, keepdims=True) + jnp.sum(p_p, axis=1, keepdims=True)
            p_scr[h, 0] = p_c.astype(BF16)
            p_scr[h, 1] = p_p.astype(BF16)
            li_scr[:, h:h + 1] = 1.0 / l
            l_ref[:, h:h + 1] = m + jnp.log(l)
        for h in range(ATT_HEADS):
            hs = slice(h * e, (h + 1) * e)
            acc = _dot(p_scr[h, 0], vc_ref[:, hs], NN) + _dot(p_scr[h, 1], vp_ref[:, hs], NN)
            o_ref[:, hs] = acc * li_scr[:, h:h + 1]

    blk = (b, hd)
    cblk = (b, LANE)
    o, lse = pl.pallas_call(
        body, name=f"att_fwd{g}", grid=(d, nb),
        scratch_shapes=[pltpu.VMEM((ATT_HEADS, 2, b, b), F32), pltpu.VMEM((ATT_HEADS, 2, b, b), BF16),
                        pltpu.VMEM((b, LANE), F32)],
        in_specs=[pl.BlockSpec(blk, lambda r, i: (i, 3 * r + g)),
                  pl.BlockSpec(blk, lambda r, i: (jnp.maximum(i - 1, 0), 2 * r)),
                  pl.BlockSpec(blk, lambda r, i: (i, 2 * r)),
                  pl.BlockSpec(blk, lambda r, i: (jnp.maximum(i - 1, 0), 2 * r + 1)),
                  pl.BlockSpec(blk, lambda r, i: (i, 2 * r + 1))],
        out_specs=(pl.BlockSpec(blk, lambda r, i: (i, r)), pl.BlockSpec(cblk, lambda r, i: (i, r))),
        out_shape=(jax.ShapeDtypeStruct((sub, d * hd), F32), jax.ShapeDtypeStruct((sub, d * LANE), F32)),
        compiler_params=_params(2))(qv, kvv, kvv, kvv, kvv)
    return o.reshape(t_len, hd), lse.reshape(t_len, LANE)


def _att_merge(os, ls):
    t_len, hd = os[0].shape
    tm = _tile(t_len, (256, 128))
    e = HEAD_DIM

    def body(o0, o1, o2, l0, l1, l2, of_ref, ob_ref, l_ref):
        a0, a1, a2 = l0[...], l1[...], l2[...]
        m = jnp.maximum(jnp.maximum(a0, a1), a2)
        e0, e1, e2 = jnp.exp(a0 - m), jnp.exp(a1 - m), jnp.exp(a2 - m)
        den = e0 + e1 + e2
        w0, w1, w2 = e0 / den, e1 / den, e2 / den
        l_ref[...] = m + jnp.log(den)
        for h in range(ATT_HEADS):
            hs = slice(h * e, (h + 1) * e)
            c = slice(h, h + 1)
            o = w0[:, c] * o0[:, hs] + w1[:, c] * o1[:, hs] + w2[:, c] * o2[:, hs]
            of_ref[:, hs] = o
            ob_ref[:, hs] = o.astype(ob_ref.dtype)

    row = pl.BlockSpec((tm, hd), lambda i: (i, 0))
    crow = pl.BlockSpec((tm, LANE), lambda i: (i, 0))
    return pl.pallas_call(
        body, name="att_merge", grid=(t_len // tm,), in_specs=[row] * 3 + [crow] * 3, out_specs=(row, row, crow),
        out_shape=(jax.ShapeDtypeStruct((t_len, hd), F32), jax.ShapeDtypeStruct((t_len, hd), BF16),
                   jax.ShapeDtypeStruct((t_len, LANE), F32)),
        compiler_params=_params(1))(*os, *ls)


def _att_delta(do, o):
    t_len, hd = o.shape
    tm = _tile(t_len, (256, 128))
    e = HEAD_DIM

    def body(do_ref, o_ref, d_ref):
        d_ref[...] = jnp.zeros_like(d_ref)
        for h in range(ATT_HEADS):
            hs = slice(h * e, (h + 1) * e)
            d_ref[:, h:h + 1] = jnp.sum(do_ref[:, hs].astype(F32) * o_ref[:, hs], axis=1, keepdims=True)

    row = pl.BlockSpec((tm, hd), lambda i: (i, 0))
    return pl.pallas_call(
        body, name="att_delta", grid=(t_len // tm,), in_specs=[row, row],
        out_specs=pl.BlockSpec((tm, LANE), lambda i: (i, 0)),
        out_shape=jax.ShapeDtypeStruct((t_len, LANE), F32), compiler_params=_params(1))(do, o)


def _att_bwd(q_all, kv, delta, lse, do, g):
    d = DILATIONS[g]
    t_len = q_all.shape[0]
    hd = ATT_HEADS * HEAD_DIM
    sub = t_len // d
    nb = sub // ATT_BLOCK
    b = ATT_BLOCK
    e = HEAD_DIM
    scale = e ** -0.5
    slopes = _alibi_slopes()
    qv = q_all.reshape(sub, d * 3 * hd)
    kvv = kv.reshape(sub, d * 2 * hd)
    dlv = delta.reshape(sub, d * LANE)
    lv = lse.reshape(sub, d * LANE)
    dov = do.reshape(sub, d * hd)

    def body(qj_ref, qn_ref, kp_ref, kc_ref, vp_ref, vc_ref, doj_ref, don_ref, dj_ref, dn_ref, lj_ref, ln_ref,
             dq_ref, dk_ref, dv_ref, s_scr, dp_scr, p_scr, ds_scr):
        j = pl.program_id(1)
        valid_c, valid_p0, dist_c, dist_p = _att_masks(d)
        valid = (valid_c, valid_p0 & (j > 0), valid_p0 & (j + 1 < nb))
        dist = (dist_c, dist_p, dist_p)
        for h in range(ATT_HEADS):
            hs = slice(h * e, (h + 1) * e)
            qj, qn = qj_ref[:, hs], qn_ref[:, hs]
            kc, kp = kc_ref[:, hs], kp_ref[:, hs]
            vc, vp = vc_ref[:, hs], vp_ref[:, hs]
            doj, don = doj_ref[:, hs], don_ref[:, hs]
            s_scr[h, 0] = _dot(qj, kc, NT)
            s_scr[h, 1] = _dot(qj, kp, NT)
            s_scr[h, 2] = _dot(qn, kc, NT)
            dp_scr[h, 0] = _dot(doj, vc, NT)
            dp_scr[h, 1] = _dot(doj, vp, NT)
            dp_scr[h, 2] = _dot(don, vc, NT)
        for h in range(ATT_HEADS):
            c = slice(h, h + 1)
            lse_t = (lj_ref[:, c], lj_ref[:, c], ln_ref[:, c])
            dlt_t = (dj_ref[:, c], dj_ref[:, c], dn_ref[:, c])
            for t in range(3):
                s = s_scr[h, t] * scale - slopes[h] * dist[t]
                p = jnp.where(valid[t], jnp.exp(jnp.where(valid[t], s - lse_t[t], NEG)), 0.0)
                p_scr[h, t] = p.astype(BF16)
                ds_scr[h, t] = (p * (dp_scr[h, t] - dlt_t[t])).astype(BF16)
        for h in range(ATT_HEADS):
            hs = slice(h * e, (h + 1) * e)
            dq = _dot(ds_scr[h, 0], kc_ref[:, hs], NN) + _dot(ds_scr[h, 1], kp_ref[:, hs], NN)
            dk = _dot(ds_scr[h, 0], qj_ref[:, hs], TN) + _dot(ds_scr[h, 2], qn_ref[:, hs], TN)
            dv = _dot(p_scr[h, 0], doj_ref[:, hs], TN) + _dot(p_scr[h, 2], don_ref[:, hs], TN)
            dq_ref[:, hs] = (dq * scale).astype(dq_ref.dtype)
            dk_ref[:, hs] = (dk * scale).astype(dk_ref.dtype)
            dv_ref[:, hs] = dv.astype(dv_ref.dtype)

    blk = (b, hd)
    cblk = (b, LANE)
    nxt = lambda i: jnp.minimum(i + 1, nb - 1)
    prv = lambda i: jnp.maximum(i - 1, 0)
    tiles = (ATT_HEADS, 3, b, b)
    dq, dk, dv = pl.pallas_call(
        body, name=f"att_bwd{g}", grid=(d, nb),
        scratch_shapes=[pltpu.VMEM(tiles, F32), pltpu.VMEM(tiles, F32), pltpu.VMEM(tiles, BF16), pltpu.VMEM(tiles, BF16)],
        in_specs=[pl.BlockSpec(blk, lambda r, i: (i, 3 * r + g)),
                  pl.BlockSpec(blk, lambda r, i: (nxt(i), 3 * r + g)),
                  pl.BlockSpec(blk, lambda r, i: (prv(i), 2 * r)),
                  pl.BlockSpec(blk, lambda r, i: (i, 2 * r)),
                  pl.BlockSpec(blk, lambda r, i: (prv(i), 2 * r + 1)),
                  pl.BlockSpec(blk, lambda r, i: (i, 2 * r + 1)),
                  pl.BlockSpec(blk, lambda r, i: (i, r)),
                  pl.BlockSpec(blk, lambda r, i: (nxt(i), r)),
                  pl.BlockSpec(cblk, lambda r, i: (i, r)),
                  pl.BlockSpec(cblk, lambda r, i: (nxt(i), r)),
                  pl.BlockSpec(cblk, lambda r, i: (i, r)),
                  pl.BlockSpec(cblk, lambda r, i: (nxt(i), r))],
        out_specs=(pl.BlockSpec(blk, lambda r, i: (i, r)),) * 3,
        out_shape=(jax.ShapeDtypeStruct((sub, d * hd), BF16),) * 3,
        compiler_params=_params(2))(qv, qv, kvv, kvv, kvv, kvv, dov, dov, dlv, dlv, lv, lv)
    return dq.reshape(t_len, hd), dk.reshape(t_len, hd), dv.reshape(t_len, hd)


def _kv_grad_sum(dks, dvs):
    t_len, hd = dks[0].shape
    tm = _tile(t_len, (256, 128))

    def body(k0, k1, k2, v0, v1, v2, o_ref):
        o_ref[:, :hd] = (k0[...].astype(F32) + k1[...].astype(F32) + k2[...].astype(F32)).astype(o_ref.dtype)
        o_ref[:, hd:] = (v0[...].astype(F32) + v1[...].astype(F32) + v2[...].astype(F32)).astype(o_ref.dtype)

    row = pl.BlockSpec((tm, hd), lambda i: (i, 0))
    return pl.pallas_call(
        body, name="kv_grad_sum", grid=(t_len // tm,), in_specs=[row] * 6,
        out_specs=pl.BlockSpec((tm, 2 * hd), lambda i: (i, 0)),
        out_shape=jax.ShapeDtypeStruct((t_len, 2 * hd), BF16), compiler_params=_params(1))(*dks, *dvs)


HALO = 16
INV_SQRT2 = 1.0 / math.sqrt(2.0)
INV_SQRT2PI = 1.0 / math.sqrt(2.0 * math.pi)


def _conv_taps(g, halo, cw, cb):
    row = lax.broadcasted_iota(jnp.int32, g.shape, 0)
    h1 = halo[HALO - 1:HALO, :]
    h2 = halo[HALO - 2:HALO - 1, :]
    g1 = jnp.where(row == 0, h1, pltpu.roll(g, 1, 0))
    g2 = jnp.where(row == 0, h2, jnp.where(row == 1, h1, pltpu.roll(g, 2, 0)))
    gc = cw[0:1, :] * g2 + cw[1:2, :] * g1 + cw[2:3, :] * g + cb
    return gc, g1, g2


def _glu_specs(t_len, f, tm, tc):
    nj = f // tc
    hb = tm // HALO
    u = pl.BlockSpec((tm, tc), lambda j, i: (i, j))
    g = pl.BlockSpec((tm, tc), lambda j, i: (i, nj + j))
    gh = pl.BlockSpec((HALO, tc), lambda j, i: (jnp.maximum(i * hb - 1, 0), nj + j))
    cw = pl.BlockSpec((8, tc), lambda j, i: (0, j))
    cb = pl.BlockSpec((1, tc), lambda j, i: (0, j))
    return u, g, gh, cw, cb


def _glu_fwd(name, up, cw, cb):
    t_len = up.shape[0]
    f = up.shape[1] // 2
    tm = _tile(t_len, (512, 256, 128))
    tc = _tile(f, (1408, 1024, 512, 256, 128))
    u_s, g_s, gh_s, cw_s, cb_s = _glu_specs(t_len, f, tm, tc)

    def body(u_ref, g_ref, gh_ref, cw_ref, cb_ref, o_ref):
        first = pl.program_id(1) == 0
        halo = jnp.where(first, 0.0, gh_ref[...].astype(F32))
        gc, _, _ = _conv_taps(g_ref[...].astype(F32), halo, cw_ref[...], cb_ref[...])
        gel = 0.5 * gc * (1.0 + lax.erf(gc * INV_SQRT2))
        o_ref[...] = (gel * u_ref[...].astype(F32)).astype(o_ref.dtype)

    return pl.pallas_call(
        body, name=name, grid=(f // tc, t_len // tm), in_specs=[u_s, g_s, gh_s, cw_s, cb_s],
        out_specs=pl.BlockSpec((tm, tc), lambda j, i: (i, j)),
        out_shape=jax.ShapeDtypeStruct((t_len, f), BF16), compiler_params=_params(2))(up, up, up, cw, cb)


def _glu_bwd_a(name, dact, up, cw, cb):
    t_len = up.shape[0]
    f = up.shape[1] // 2
    tm = _tile(t_len, (256, 128))
    tc = _tile(f, (1408, 1024, 512, 256, 128))
    u_s, g_s, gh_s, cw_s, cb_s = _glu_specs(t_len, f, tm, tc)

    def body(da_ref, u_ref, g_ref, gh_ref, cw_ref, cb_ref, du_ref, dgc_ref, w0_ref, w1_ref, w2_ref, b_ref):
        first = pl.program_id(1) == 0
        halo = jnp.where(first, 0.0, gh_ref[...].astype(F32))
        g = g_ref[...].astype(F32)
        gc, g1, g2 = _conv_taps(g, halo, cw_ref[...], cb_ref[...])
        phi = 0.5 * (1.0 + lax.erf(gc * INV_SQRT2))
        dgel = phi + gc * jnp.exp(-0.5 * gc * gc) * INV_SQRT2PI
        da = da_ref[...].astype(F32)
        du_ref[...] = (da * gc * phi).astype(du_ref.dtype)
        dgc = da * u_ref[...].astype(F32) * dgel
        dgc_ref[...] = dgc.astype(dgc_ref.dtype)
        parts = (jnp.sum(dgc * g2, axis=0, keepdims=True), jnp.sum(dgc * g1, axis=0, keepdims=True),
                 jnp.sum(dgc * g, axis=0, keepdims=True), jnp.sum(dgc, axis=0, keepdims=True))
        refs = (w0_ref, w1_ref, w2_ref, b_ref)

        @pl.when(first)
        def _():
            for r, p in zip(refs, parts):
                r[...] = p

        @pl.when(jnp.logical_not(first))
        def _():
            for r, p in zip(refs, parts):
                r[...] += p

    tile = pl.BlockSpec((tm, tc), lambda j, i: (i, j))
    vec = pl.BlockSpec((1, tc), lambda j, i: (0, j))
    vshape = jax.ShapeDtypeStruct((1, f), F32)
    return pl.pallas_call(
        body, name=name, grid=(f // tc, t_len // tm), in_specs=[tile, u_s, g_s, gh_s, cw_s, cb_s],
        out_specs=(tile, tile, vec, vec, vec, vec),
        out_shape=(jax.ShapeDtypeStruct((t_len, f), BF16), jax.ShapeDtypeStruct((t_len, f), BF16),
                   vshape, vshape, vshape, vshape),
        compiler_params=_params(2))(dact, up, up, up, cw, cb)


def _glu_bwd_b(name, du, dgc, cw):
    t_len, f = du.shape
    tm = _tile(t_len, (128, 64))
    hb = tm // HALO
    n_i = t_len // tm
    last_hb = t_len // HALO - 1

    def body(du_ref, d_ref, dh_ref, cw_ref, o_ref):
        last = pl.program_id(0) == n_i - 1
        halo = jnp.where(last, 0.0, dh_ref[...].astype(F32))
        dd = d_ref[...].astype(F32)
        row = lax.broadcasted_iota(jnp.int32, dd.shape, 0)
        h0 = halo[0:1, :]
        h1 = halo[1:2, :]
        d1 = jnp.where(row == tm - 1, h0, pltpu.roll(dd, tm - 1, 0))
        d2 = jnp.where(row == tm - 1, h1, jnp.where(row == tm - 2, h0, pltpu.roll(dd, tm - 2, 0)))
        cwv = cw_ref[...]
        dg = cwv[2:3, :] * dd + cwv[1:2, :] * d1 + cwv[0:1, :] * d2
        o_ref[:, :f] = du_ref[...]
        o_ref[:, f:] = dg.astype(o_ref.dtype)

    row_s = pl.BlockSpec((tm, f), lambda i: (i, 0))
    return pl.pallas_call(
        body, name=name, grid=(n_i,),
        in_specs=[row_s, row_s, pl.BlockSpec((HALO, f), lambda i: (jnp.minimum((i + 1) * hb, last_hb), 0)),
                  pl.BlockSpec((8, f), lambda i: (0, 0))],
        out_specs=pl.BlockSpec((tm, 2 * f), lambda i: (i, 0)),
        out_shape=jax.ShapeDtypeStruct((t_len, 2 * f), BF16), compiler_params=_params(1))(du, dgc, dgc, cw)


def _adamw(name, w, g, m, v):
    rows, cols = w.shape
    gcols = g.shape[1]
    n_out = 3 if gcols == cols else 4
    tr = _tile(rows, (256, 128, 64, 32, 16, 8))
    c1 = 1.0 / (1.0 - ADAM_B1 ** ADAM_STEP)
    c2 = 1.0 / (1.0 - ADAM_B2 ** ADAM_STEP)

    def body(w_ref, g_ref, m_ref, v_ref, d_ref, nm_ref, nv_ref, *g_out):
        gv = g_ref[...][:, :cols]
        nm = ADAM_B1 * m_ref[...] + (1.0 - ADAM_B1) * gv
        nv = ADAM_B2 * v_ref[...] + (1.0 - ADAM_B2) * (gv * gv)
        nm_ref[...] = nm
        nv_ref[...] = nv
        d_ref[...] = -ADAM_LR * ((nm * c1) / (jnp.sqrt(nv * c2) + ADAM_EPS) + ADAM_WD * w_ref[...])
        for ref in g_out:
            ref[...] = gv

    blk = pl.BlockSpec((tr, cols), lambda i: (i, 0))
    gblk = pl.BlockSpec((tr, gcols), lambda i: (i, 0))
    shp = jax.ShapeDtypeStruct((rows, cols), F32)
    return pl.pallas_call(body, name=name, grid=(rows // tr,), in_specs=[blk, gblk, blk, blk], out_specs=(blk,) * n_out,
                          out_shape=(shp,) * n_out, compiler_params=_params(1))(w, g, m, v)


class _NoComm:
    def __init__(self):
        self.grads = {}

    def need(self, group, ws, after):
        return ws

    def reduce(self, group, grads, carry):
        self.grads.update(grads)
        return carry

    def tick(self, carry):
        return carry


def _local_step(x, target, ws, norms, small, hooks):
    lay = _layout()

    w_main, w_a = _unpack_gin(ws["gin"])
    hn0 = _rms_fwd("rms_attn0", x, norms["attn0"])
    proj = _mm_plain("gla_proj", hn0, w_main, NN, F32)
    a = _mm_plain("gla_proj_a", hn0, w_a, NN, BF16)
    ga, cum = _gla_gate_fwd(a, small["w_a2p"], small["b_a2"])
    o_gla, states = _gla_fwd(proj, cum)
    gated = _gla_out_fwd(o_gla, proj, small["head_norm"])
    ws = hooks.need("B", ws, gated)
    h1 = _mm_act_wr("gla_out", gated, ws["gout"], lay["gout"], add=x)

    def ffn_fwd(l, h):
        hn = _rms_fwd(f"rms_ffn{l}", h, norms[f"ffn{l}"])
        up = _mm_act_wc(f"ffn_up{l}", hn, ws[f"up{l}"], lay[f"up{l}"], BF16)
        act = _glu_fwd(f"glu_fwd{l}", up, small["conv_w"][l], small["conv_b"][l])
        return hn, up, act, _mm_act_wr(f"ffn_down{l}", act, ws[f"down{l}"], lay[f"down{l}"], add=h)

    hnf0, up0, act0, h2 = ffn_fwd(0, h1)

    ws = hooks.need("C", ws, h2)
    kvn = _rms_fwd("rms_kv", h2, norms["kv"])
    kv = _mm_act_wc("kv_proj", kvn, ws["wkv"], lay["wkv"], BF16)
    hn1 = _rms_fwd("rms_attn1", h2, norms["attn1"])
    q_all = _mm_act_wc("q_proj", hn1, ws["wq"], lay["wq"], BF16)
    branch = [_att_fwd(q_all, kv, g) for g in range(3)]
    o_att, o_att_b, lse = _att_merge([br[0] for br in branch], [br[1] for br in branch])
    h3 = _mm_act_wr("att_out", o_att_b, ws["dout"], lay["dout"], add=h2)
    hnf1, up1, act1, h4 = ffn_fwd(1, h3)

    dh4, d_final, loss = _loss_head(h4, norms["final"], target)

    sm = {"final": d_final}

    def ffn_bwd(l, dh, h, hn, up, act):
        big = {}
        dact = _mm_dact_wrT(f"ffn_down_dx{l}", dh, ws[f"down{l}"], lay[f"down{l}"])
        big[f"down{l}"] = _mm_grad_wr(f"ffn_down_dw{l}", act, dh, lay[f"down{l}"])
        du, dgc, w0, w1, w2, db = _glu_bwd_a(f"glu_bwd_a{l}", dact, up, small["conv_w"][l], small["conv_b"][l])
        sm[f"conv_w{l}"] = (w0, w1, w2)
        sm[f"conv_b{l}"] = db
        dup = hooks.tick(_glu_bwd_b(f"glu_bwd_b{l}", du, dgc, small["conv_w"][l]))
        dhn = _mm_dact_wcT(f"ffn_up_dx{l}", dup, ws[f"up{l}"], lay[f"up{l}"])
        big[f"up{l}"] = _mm_grad_wc(f"ffn_up_dw{l}", hn, dup, lay[f"up{l}"])
        dh_in, sm[f"ffn{l}"] = _rms_bwd(f"rms_ffn_bwd{l}", dhn, h, norms[f"ffn{l}"], dh)
        return hooks.reduce(f"ffn{l}", big, dh_in)

    dh3 = ffn_bwd(1, dh4, h3, hnf1, up1, act1)

    big = {}
    do_att = _mm_dact_wrT("att_out_dx", dh3, ws["dout"], lay["dout"])
    big["dout"] = _mm_grad_wr("att_out_dw", o_att_b, dh3, lay["dout"])
    delta = _att_delta(do_att, o_att)
    bw = [_att_bwd(q_all, kv, delta, lse, do_att, g) for g in range(3)]
    dq_all = jnp.concatenate([t[0] for t in bw], axis=1)
    dhn1 = _mm_dact_wcT("q_proj_dx", dq_all, ws["wq"], lay["wq"])
    big["wq"] = _mm_grad_wc("q_proj_dw", hn1, dq_all, lay["wq"])
    dh2, sm["attn1"] = _rms_bwd("rms_attn1_bwd", dhn1, h2, norms["attn1"], dh3)
    dkv = hooks.tick(_kv_grad_sum([t[1] for t in bw], [t[2] for t in bw]))
    dkvn = _mm_dact_wcT("kv_proj_dx", dkv, ws["wkv"], lay["wkv"])
    big["wkv"] = _mm_grad_wc("kv_proj_dw", kvn, dkv, lay["wkv"])
    dh2, sm["kv"] = _rms_bwd("rms_kv_bwd", dkvn, h2, norms["kv"], dh2)
    dh2 = hooks.reduce("att", big, dh2)

    dh1 = ffn_bwd(0, dh2, h1, hnf0, up0, act0)

    big = {}
    dgated = _mm_dact_wrT("gla_out_dx", dh1, ws["gout"], lay["gout"])
    big["gout"] = _mm_grad_wr("gla_out_dw", gated, dh1, lay["gout"])
    do_gla, dr, sm["head_norm"] = _gla_out_bwd(dgated, o_gla, proj, small["head_norm"])
    dq, dk, dv, dcum = _gla_bwd(proj, cum, states, hooks.tick(do_gla))
    da, sm["w_a2p"], sm["b_a2"] = _gla_gate_bwd(dcum, ga, a, small["w_a2p"])
    dproj = jnp.concatenate([dq, dk, dv, dr], axis=1)
    dhn0 = _mm_plain("gla_proj_dx", dproj, w_main, NT, F32)
    dhn0 = _mm_plain("gla_proj_a_dx", da, w_a, NT, F32, add=dhn0)
    gin_main = _mm_plain("gla_proj_dw", hn0, dproj, TN, BF16)
    gin_a = _mm_plain("gla_proj_a_dw", hn0, da, TN, BF16)
    big["gin"] = _pack_gin_grad(gin_main, gin_a)
    grad_x, sm["attn0"] = _rms_bwd("rms_attn0_bwd", dhn0, x, norms["attn0"], dh1)
    return loss, grad_x, sm, big


def _pack_weights(chip, names, gla_w_in, gla_w_out, w_kv, dsa_w_q, dsa_w_out, ffn_w_up, ffn_w_down):
    gin = gla_w_in[0]
    gin = jnp.pad(gin, ((0, 0), (0, _roundup(gin.shape[1], LANE) - gin.shape[1])))
    shards = {"gin": gin, "gout": gla_w_out[0], "up0": ffn_w_up[0], "up1": ffn_w_up[1], "down0": ffn_w_down[0],
              "down1": ffn_w_down[1], "wq": dsa_w_q[0], "wkv": w_kv, "dout": dsa_w_out[0]}
    out = {}
    for name in names:
        w = shards[name]
        buf = jnp.zeros((N_CHIPS,) + w.shape, BF16)
        out[name] = lax.dynamic_update_slice(buf, w.astype(BF16)[None], (chip, 0, 0))
    return out


def _unpack_gin(w_gin):
    w = _layout()["gin"][1]
    d = w_gin.shape[1]
    wp = w_gin.shape[2]
    n_main = 2 * GLA_KEY_DIM + 2 * GLA_VAL_DIM
    tm = _tile(d, (256, 128, 64, 32, 16))

    def body(s_ref, main_ref, a_ref):
        full = jnp.concatenate([s_ref[s][:, :w] for s in range(N_CHIPS)], axis=1)
        main_ref[...] = full[:, :n_main]
        a_ref[...] = jnp.concatenate([full[:, n_main:], jnp.zeros((tm, A_PAD - GATE_RANK), full.dtype)], axis=1)

    return pl.pallas_call(
        body, name="unpack_gin", grid=(d // tm,), in_specs=[pl.BlockSpec((N_CHIPS, tm, wp), lambda i: (0, i, 0))],
        out_specs=(pl.BlockSpec((tm, n_main), lambda i: (i, 0)), pl.BlockSpec((tm, A_PAD), lambda i: (i, 0))),
        out_shape=(jax.ShapeDtypeStruct((d, n_main), w_gin.dtype), jax.ShapeDtypeStruct((d, A_PAD), w_gin.dtype)),
        compiler_params=_params(1))(w_gin)


def _pack_gin_grad(gin_main, gin_a):
    w = _layout()["gin"][1]
    wp = _roundup(w, LANE)
    d, n_main = gin_main.shape
    tm = _tile(d, (256, 128, 64, 32, 16))

    def body(main_ref, a_ref, o_ref):
        full = jnp.concatenate([main_ref[...], a_ref[:, :GATE_RANK]], axis=1)
        fill = jnp.zeros((tm, wp - w), full.dtype)
        for s in range(N_CHIPS):
            o_ref[s] = jnp.concatenate([full[:, s * w:(s + 1) * w], fill], axis=1)

    return pl.pallas_call(
        body, name="pack_gin_grad", grid=(d // tm,),
        in_specs=[pl.BlockSpec((tm, n_main), lambda i: (i, 0)), pl.BlockSpec((tm, A_PAD), lambda i: (i, 0))],
        out_specs=pl.BlockSpec((N_CHIPS, tm, wp), lambda i: (0, i, 0)),
        out_shape=jax.ShapeDtypeStruct((N_CHIPS, d, wp), gin_main.dtype), compiler_params=_params(1))(gin_main, gin_a)


def _small_params(attn_norm, ffn_norm, kv_norm, final_norm, conv_b, w_a2, b_a2, head_norm, conv_w):
    norms = {"attn0": attn_norm[0:1], "attn1": attn_norm[1:2], "ffn0": ffn_norm[0:1], "ffn1": ffn_norm[1:2],
             "kv": kv_norm[None, :], "final": final_norm[None, :]}
    small = {"w_a2p": jnp.pad(w_a2, ((0, A_PAD - GATE_RANK), (0, 0))), "b_a2": b_a2[None, :],
             "head_norm": head_norm[None, :], "conv_w": jnp.pad(conv_w, ((0, 0), (0, 8 - conv_w.shape[1]), (0, 0))),
             "conv_b": conv_b[:, None, :]}
    return norms, small


ANY = pl.BlockSpec(memory_space=pl.ANY)


def _place():
    return lax.axis_index("x"), lax.axis_index("y"), lax.axis_index("c")


def _other_chips(x, y):
    return [(1 - x, y), (x, 1 - y), (1 - x, 1 - y)]


def _rcopy(src, dst, ssem, rsem, dev):
    return pltpu.make_async_remote_copy(src_ref=src, dst_ref=dst, send_sem=ssem, recv_sem=rsem, device_id=dev,
                                        device_id_type=MESH)


def _pack_shard(name, w, layer, chip_arr, after):
    rows, cols = w.shape[-2:]
    tr = _tile(rows, (512, 352, 256, 128, 64, 32, 16))

    def body(p_ref, w_ref, after_ref, o_ref):
        o_ref[...] = w_ref[...].astype(o_ref.dtype)

    if w.ndim == 3:
        w_spec = pl.BlockSpec((None, tr, cols), lambda i, p: (layer, i, 0))
    else:
        w_spec = pl.BlockSpec((tr, cols), lambda i, p: (i, 0))
    return pl.pallas_call(
        body, name=name,
        grid_spec=pltpu.PrefetchScalarGridSpec(
            num_scalar_prefetch=1, grid=(rows // tr,), in_specs=[w_spec, ANY],
            out_specs=pl.BlockSpec((None, tr, cols), lambda i, p: (p[0], i, 0))),
        out_shape=jax.ShapeDtypeStruct((N_CHIPS, rows, cols), BF16), compiler_params=_params(1))(chip_arr, w, after)


def _swap_halves(name, arrs):
    n = len(arrs)

    def body(*refs):
        ins, outs = refs[:n], refs[n:2 * n]
        send, recv = refs[2 * n:]
        x, y, c = _place()
        cps = []
        for a in range(n):
            h = ins[a].shape[1] // 2
            cp = _rcopy(ins[a].at[:, pl.ds((1 - c) * h, h)], outs[a], send.at[a], recv.at[a], (x, y, 1 - c))
            cp.start()
            cps.append(cp)
        for cp in cps:
            cp.wait()

    return pl.pallas_call(
        body, name=name, in_specs=[ANY] * n, out_specs=[ANY] * n,
        out_shape=[jax.ShapeDtypeStruct((a.shape[0], a.shape[1] // 2, a.shape[2]), a.dtype) for a in arrs],
        scratch_shapes=[pltpu.SemaphoreType.DMA((n,)), pltpu.SemaphoreType.DMA((n,))])(*arrs)


SEM = pl.BlockSpec(memory_space=pltpu.SEMAPHORE)
EFFECT = pltpu.SideEffectType.DATAFLOW_SIDE_EFFECTING


def _shapes(arrs):
    return [jax.ShapeDtypeStruct(a.shape, a.dtype) for a in arrs]


def _gather_start(name, thru, arrs):
    n, nt = len(arrs), len(thru)

    def body(*refs):
        ins = refs[nt:nt + n]
        send, recv = refs[nt + n], refs[nt + n + 1]
        outs = refs[2 * nt + n + 2:]
        x, y, c = _place()
        me = 2 * x + y
        for a in range(n):
            h = ins[a].shape[1] // 2
            mine = pl.ds(c * h, h)
            for j, (px, py) in enumerate(_other_chips(x, y)):
                _rcopy(ins[a].at[me, mine], outs[a].at[me, mine], send.at[3 * a + j], recv.at[3 * a + j], (px, py, c)).start()

    res = pl.pallas_call(
        body, name=name, in_specs=[ANY] * (nt + n), out_specs=[SEM, SEM] + [ANY] * (nt + n),
        out_shape=[pltpu.SemaphoreType.DMA((3 * n,)), pltpu.SemaphoreType.DMA((3 * n,))] + _shapes(thru) + _shapes(arrs),
        input_output_aliases={i: 2 + i for i in range(nt + n)},
        compiler_params=pltpu.CompilerParams(has_side_effects=EFFECT))(*thru, *arrs)
    return res[0], res[1], res[2:2 + nt], res[2 + nt:]


def _gather_wait(name, send, recv, arrs, after):
    n = len(arrs)

    def body(*refs):
        ins = refs[:n]
        send_ref, recv_ref = refs[n], refs[n + 1]
        x, y, c = _place()
        me = 2 * x + y
        for a in range(n):
            h = ins[a].shape[1] // 2
            mine = pl.ds(c * h, h)
            for j, (px, py) in enumerate(_other_chips(x, y)):
                sent = ins[a].at[me, mine]
                landed = ins[a].at[2 * px + py, mine]
                cp = _rcopy(sent, landed, send_ref.at[3 * a + j], recv_ref.at[3 * a + j], (px, py, c))
                cp.wait_send()
                cp.wait_recv()

    return pl.pallas_call(
        body, name=name, in_specs=[ANY] * n + [SEM, SEM, ANY], out_specs=[ANY] * n, out_shape=_shapes(arrs),
        input_output_aliases={a: a for a in range(n)},
        compiler_params=pltpu.CompilerParams(has_side_effects=EFFECT))(*arrs, send, recv, after)


def _forward_halves(name, arrs):
    n = len(arrs)

    def body(*refs):
        ins, outs = refs[:n], refs[n:2 * n]
        send, recv = refs[2 * n:]
        x, y, c = _place()
        sib = (x, y, 1 - c)
        chips = _other_chips(x, y)
        cps = []
        for a in range(n):
            h = ins[a].shape[1] // 2
            mine = pl.ds(c * h, h)
            for j, (px, py) in enumerate(chips):
                cp = _rcopy(ins[a].at[2 * px + py, mine], outs[a].at[2 * px + py, mine], send.at[3 * a + j],
                            recv.at[3 * a + j], sib)
                cp.start()
                cps.append(cp)
        for a in range(n):
            h = ins[a].shape[1] // 2
            theirs = pl.ds((1 - c) * h, h)
            for j, (px, py) in enumerate(chips):
                got = outs[a].at[2 * px + py, theirs]
                _rcopy(got, got, send.at[3 * a + j], recv.at[3 * a + j], sib).wait_recv()
        for cp in cps:
            cp.wait_send()

    return pl.pallas_call(
        body, name=name, in_specs=[ANY] * n, out_specs=[ANY] * n, out_shape=_shapes(arrs),
        input_output_aliases={a: a for a in range(n)},
        scratch_shapes=[pltpu.SemaphoreType.DMA((3 * n,)), pltpu.SemaphoreType.DMA((3 * n,))])(*arrs)


def _scatter_start(name, thru, arrs):
    n, nt = len(arrs), len(thru)
    landing = [jnp.zeros_like(a) for a in arrs]

    def body(*refs):
        ins = refs[nt:nt + n]
        send, recv = refs[nt + 2 * n], refs[nt + 2 * n + 1]
        outs = refs[2 * nt + 3 * n + 2:]
        x, y, c = _place()
        me = 2 * x + y
        for a in range(n):
            for j, (px, py) in enumerate(_other_chips(x, y)):
                _rcopy(ins[a].at[2 * px + py], outs[a].at[me], send.at[3 * a + j], recv.at[3 * a + j], (px, py, c)).start()

    res = pl.pallas_call(
        body, name=name, in_specs=[ANY] * (nt + 2 * n), out_specs=[SEM, SEM] + [ANY] * (nt + 2 * n),
        out_shape=[pltpu.SemaphoreType.DMA((3 * n,)), pltpu.SemaphoreType.DMA((3 * n,))] + _shapes(thru) + _shapes(arrs)
        + _shapes(landing),
        input_output_aliases={i: 2 + i for i in range(nt + 2 * n)},
        compiler_params=pltpu.CompilerParams(has_side_effects=EFFECT))(*thru, *arrs, *landing)
    return res[0], res[1], res[2:2 + nt], res[2 + nt:2 + nt + n], res[2 + nt + n:]


def _scatter_wait(name, send, recv, arrs, landing, after):
    n = len(arrs)

    def body(*refs):
        ins, land = refs[:n], refs[n:2 * n]
        send_ref, recv_ref = refs[2 * n], refs[2 * n + 1]
        x, y, c = _place()
        for a in range(n):
            for j, (px, py) in enumerate(_other_chips(x, y)):
                cp = _rcopy(ins[a].at[2 * px + py], land[a].at[2 * px + py], send_ref.at[3 * a + j], recv_ref.at[3 * a + j],
                            (px, py, c))
                cp.wait_send()
                cp.wait_recv()

    res = pl.pallas_call(
        body, name=name, in_specs=[ANY] * (2 * n) + [SEM, SEM, ANY], out_specs=[ANY] * (2 * n),
        out_shape=_shapes(arrs) + _shapes(landing), input_output_aliases={i: i for i in range(2 * n)},
        compiler_params=pltpu.CompilerParams(has_side_effects=EFFECT))(*arrs, *landing, send, recv, after)
    return res[:n], res[n:]


def _swap_start(name, thru, arrs):
    n, nt = len(arrs), len(thru)
    landing = [lax.empty((a.shape[0], a.shape[1] // 2, a.shape[2]), a.dtype) for a in arrs]

    def body(*refs):
        ins = refs[nt:nt + n]
        send, recv = refs[nt + 2 * n], refs[nt + 2 * n + 1]
        outs = refs[2 * nt + 3 * n + 2:]
        x, y, c = _place()
        for a in range(n):
            h = ins[a].shape[1] // 2
            _rcopy(ins[a].at[:, pl.ds((1 - c) * h, h)], outs[a], send.at[a], recv.at[a], (x, y, 1 - c)).start()

    res = pl.pallas_call(
        body, name=name, in_specs=[ANY] * (nt + 2 * n), out_specs=[SEM, SEM] + [ANY] * (nt + 2 * n),
        out_shape=[pltpu.SemaphoreType.DMA((n,)), pltpu.SemaphoreType.DMA((n,))] + _shapes(thru) + _shapes(arrs)
        + _shapes(landing),
        input_output_aliases={i: 2 + i for i in range(nt + 2 * n)},
        compiler_params=pltpu.CompilerParams(has_side_effects=EFFECT))(*thru, *arrs, *landing)
    return res[0], res[1], res[2:2 + nt], res[2 + nt:2 + nt + n], res[2 + nt + n:]


def _swap_wait(name, send, recv, arrs, landing, after):
    n = len(arrs)

    def body(*refs):
        ins, land = refs[:n], refs[n:2 * n]
        send_ref, recv_ref = refs[2 * n], refs[2 * n + 1]
        x, y, c = _place()
        for a in range(n):
            h = ins[a].shape[1] // 2
            cp = _rcopy(ins[a].at[:, pl.ds((1 - c) * h, h)], land[a], send_ref.at[a], recv_ref.at[a], (x, y, 1 - c))
            cp.wait_send()
            cp.wait_recv()

    res = pl.pallas_call(
        body, name=name, in_specs=[ANY] * (2 * n) + [SEM, SEM, ANY], out_specs=[ANY] * (2 * n),
        out_shape=_shapes(arrs) + _shapes(landing), input_output_aliases={i: i for i in range(2 * n)},
        compiler_params=pltpu.CompilerParams(has_side_effects=EFFECT))(*arrs, *landing, send, recv, after)
    return res[:n], res[n:]


def _join_start(name, arrs):
    n = len(arrs)

    def body(*refs):
        ins = refs[:n]
        send, recv = refs[n], refs[n + 1]
        outs = refs[n + 2:]
        x, y, c = _place()
        for a in range(n):
            h = ins[a].shape[0] // 2
            mine = pl.ds(c * h, h)
            _rcopy(ins[a].at[mine], outs[a].at[mine], send.at[a], recv.at[a], (x, y, 1 - c)).start()

    res = pl.pallas_call(
        body, name=name, in_specs=[ANY] * n, out_specs=[SEM, SEM] + [ANY] * n,
        out_shape=[pltpu.SemaphoreType.DMA((n,)), pltpu.SemaphoreType.DMA((n,))] + _shapes(arrs),
        input_output_aliases={i: 2 + i for i in range(n)},
        compiler_params=pltpu.CompilerParams(has_side_effects=EFFECT))(*arrs)
    return res[0], res[1], res[2:]


def _join_wait(name, send, recv, arrs, after):
    n = len(arrs)

    def body(*refs):
        ins = refs[:n]
        send_ref, recv_ref = refs[n], refs[n + 1]
        x, y, c = _place()
        for a in range(n):
            h = ins[a].shape[0] // 2
            cp = _rcopy(ins[a].at[pl.ds(c * h, h)], ins[a].at[pl.ds((1 - c) * h, h)], send_ref.at[a], recv_ref.at[a],
                        (x, y, 1 - c))
            cp.wait_send()
            cp.wait_recv()

    return pl.pallas_call(
        body, name=name, in_specs=[ANY] * n + [SEM, SEM, ANY], out_specs=[ANY] * n, out_shape=_shapes(arrs),
        input_output_aliases={a: a for a in range(n)},
        compiler_params=pltpu.CompilerParams(has_side_effects=EFFECT))(*arrs, send, recv, after)


def _join_halves(name, arrs):
    n = len(arrs)

    def body(*refs):
        ins, outs = refs[:n], refs[n:2 * n]
        send, recv = refs[2 * n:]
        x, y, c = _place()
        cps = []
        for a in range(n):
            h = ins[a].shape[0] // 2
            mine = pl.ds(c * h, h)
            cp = _rcopy(ins[a].at[mine], outs[a].at[mine], send.at[a], recv.at[a], (x, y, 1 - c))
            cp.start()
            cps.append(cp)
        for a in range(n):
            h = ins[a].shape[0] // 2
            got = outs[a].at[pl.ds((1 - c) * h, h)]
            _rcopy(got, got, send.at[a], recv.at[a], (x, y, 1 - c)).wait_recv()
        for cp in cps:
            cp.wait_send()

    return pl.pallas_call(
        body, name=name, in_specs=[ANY] * n, out_specs=[ANY] * n,
        out_shape=[jax.ShapeDtypeStruct(a.shape, a.dtype) for a in arrs],
        input_output_aliases={a: a for a in range(n)},
        scratch_shapes=[pltpu.SemaphoreType.DMA((n,)), pltpu.SemaphoreType.DMA((n,))])(*arrs)


def _allgather8(name, xs, reduce):
    m_per, n = xs.shape

    def body(x_ref, out_ref, *rest):
        if reduce:
            sum_ref, send, recv, lsem = rest
        else:
            send, recv, lsem = rest
        x, y, c = _place()
        me, sib = (x, y, c), (x, y, 1 - c)
        chips = _other_chips(x, y)

        def rows(px, py, pc):
            return out_ref.at[pl.ds((4 * px + 2 * py + pc) * m_per, m_per), :]

        def copy(k, block, to, src=None):
            return _rcopy(rows(*block) if src is None else src, rows(*block), send.at[k], recv.at[k], to)

        mine = pltpu.make_async_copy(x_ref, rows(*me), lsem)
        mine.start()
        first = [copy(0, me, sib, src=x_ref)]
        first += [copy(1 + j, me, (*chip, c), src=x_ref) for j, chip in enumerate(chips)]
        for cp in first:
            cp.start()
        passed = [copy(4 + j, (*chip, c), sib) for j, chip in enumerate(chips)]
        for j, chip in enumerate(chips):
            copy(1 + j, (*chip, c), me).wait_recv()
            passed[j].start()
        copy(0, sib, me).wait_recv()
        for j, chip in enumerate(chips):
            copy(4 + j, (*chip, 1 - c), me).wait_recv()
        for cp in first + passed:
            cp.wait_send()
        mine.wait()
        if reduce:
            acc = out_ref[pl.ds(0, m_per), :]
            for dev in range(1, N_DEV):
                acc = acc + out_ref[pl.ds(dev * m_per, m_per), :]
            sum_ref[...] = acc

    vm = pl.BlockSpec(memory_space=pltpu.VMEM)
    out_shape = [jax.ShapeDtypeStruct((N_DEV * m_per, n), xs.dtype)]
    if reduce:
        out_shape.append(jax.ShapeDtypeStruct((m_per, n), xs.dtype))
    return pl.pallas_call(
        body, name=name, in_specs=[vm], out_specs=[vm] * len(out_shape), out_shape=out_shape,
        scratch_shapes=[pltpu.SemaphoreType.DMA((7,)), pltpu.SemaphoreType.DMA((7,)), pltpu.SemaphoreType.DMA],
        compiler_params=pltpu.CompilerParams(vmem_limit_bytes=VMEM_LIMIT))(xs)


def _add_my_half(name, a, rb, c_arr):
    s, h, cols = rb.shape
    tr = _tile(h, (512, 352, 256, 128, 64, 32, 16))
    nt = h // tr

    def body(c_ref, a_ref, b_ref, o_ref):
        o_ref[...] = (a_ref[...].astype(F32) + b_ref[...].astype(F32)).astype(o_ref.dtype)

    return pl.pallas_call(
        body, name=name,
        grid_spec=pltpu.PrefetchScalarGridSpec(
            num_scalar_prefetch=1, grid=(s, nt),
            in_specs=[pl.BlockSpec((None, tr, cols), lambda k, i, c: (k, c[0] * nt + i, 0)),
                      pl.BlockSpec((None, tr, cols), lambda k, i, c: (k, i, 0))],
            out_specs=pl.BlockSpec((None, tr, cols), lambda k, i, c: (k, i, 0))),
        out_shape=jax.ShapeDtypeStruct(rb.shape, BF16), compiler_params=_params(2))(c_arr, a, rb)


def _sum_chips(name, own, q, place):
    s, h, cols = q.shape
    tr = _tile(h, (512, 352, 256, 128, 64, 32, 16))
    nt = h // tr

    def body(p_ref, own_ref, q_ref, o_ref):
        chip = p_ref[0]
        acc = jnp.where(chip == 0, own_ref[0], q_ref[0]).astype(F32)
        for j in range(1, s):
            acc = acc + jnp.where(chip == j, own_ref[j], q_ref[j]).astype(F32)
        o_ref[...] = acc

    blk = pl.BlockSpec((s, tr, cols), lambda i, p: (0, i, 0))
    return pl.pallas_call(
        body, name=name,
        grid_spec=pltpu.PrefetchScalarGridSpec(
            num_scalar_prefetch=1, grid=(nt,), in_specs=[blk, blk],
            out_specs=pl.BlockSpec((tr, cols), lambda i, p: (p[1] * nt + i, 0))),
        out_shape=jax.ShapeDtypeStruct((2 * h, cols), F32), compiler_params=_params(1))(place, own, q)


def _pack_rows(parts):
    rows = []
    for p in parts:
        flat = p.reshape(-1).astype(F32)
        n = _roundup(flat.shape[0], 8 * LANE)
        rows.append(jnp.pad(flat, (0, n - flat.shape[0])).reshape(-1, LANE))
    return jnp.concatenate(rows, axis=0)


def _unpack_rows(buf, shapes):
    out, r = [], 0
    for shp in shapes:
        size = math.prod(shp)
        nr = _roundup(size, 8 * LANE) // LANE
        out.append(buf[r:r + nr].reshape(-1)[:size].reshape(shp))
        r += nr
    return out


def kernel(x, attn_norm, gla_w_in, gla_w_a2, gla_b_a2, gla_head_norm, gla_w_out, kv_norm, w_kv, dsa_w_q, dsa_w_out, ffn_norm, ffn_w_up, ffn_conv_w, ffn_conv_b, ffn_w_down, final_norm, loss_target, m_attn_norm, m_gla_w_in, m_gla_w_a2, m_gla_b_a2, m_gla_head_norm, m_gla_w_out, m_kv_norm, m_w_kv, m_dsa_w_q, m_dsa_w_out, m_ffn_norm, m_ffn_w_up, m_ffn_conv_w, m_ffn_conv_b, m_ffn_w_down, m_final_norm, v_attn_norm, v_gla_w_in, v_gla_w_a2, v_gla_b_a2, v_gla_head_norm, v_gla_w_out, v_kv_norm, v_w_kv, v_dsa_w_q, v_dsa_w_out, v_ffn_norm, v_ffn_w_up, v_ffn_conv_w, v_ffn_conv_b, v_ffn_w_down, v_final_norm):
    lay = _layout()
    d, f = D_MODEL, D_FF
    cx, cy, cc = _place()
    chip = 2 * cx + cy
    c_arr = jnp.reshape(cc, (1,)).astype(jnp.int32)
    place = jnp.stack([chip, cc]).astype(jnp.int32)

    groups = {"A": ("gin", "small"), "B": ("gout", "up0", "down0"), "C": ("up1", "down1", "wq", "wkv", "dout")}
    big_shards = (gla_w_in, gla_w_out, w_kv, dsa_w_q, dsa_w_out, ffn_w_up, ffn_w_down)
    ws = _pack_weights(chip, groups["A"][:1], *big_shards)
    sharded_small = [gla_w_a2[0], gla_b_a2[0], gla_head_norm[0], ffn_conv_w]
    packed = _pack_rows(sharded_small)
    packed = jnp.pad(packed, ((0, _roundup(packed.shape[0], 16) - packed.shape[0]), (0, 0)))
    ws["small"] = lax.dynamic_update_slice(jnp.zeros((N_CHIPS,) + packed.shape, F32), packed[None], (chip, 0, 0))
    send, recv, _, arrs = _gather_start("gather_a_start", [], [ws[k] for k in groups["A"]])
    chip_arr = place[:1]
    sources = {"up0": (ffn_w_up, 0), "up1": (ffn_w_up, 1), "down0": (ffn_w_down, 0), "down1": (ffn_w_down, 1),
               "wq": (dsa_w_q, 0), "wkv": (w_kv, 0), "dout": (dsa_w_out, 0), "gout": (gla_w_out, 0)}
    for k in groups["B"] + groups["C"]:
        ws[k] = _pack_shard(f"pack_{k}", *sources[k], chip_arr, arrs[1])
    arrs = _gather_wait("gather_a_wait", send, recv, arrs, ws["dout"])
    ws.update(zip(groups["A"], _forward_halves("forward_a", arrs)))
    in_flight = {}
    thru = [ws[k] for k in groups["A"]]
    for grp in ("B", "C"):
        send, recv, thru, arrs = _gather_start(f"gather_{grp.lower()}_start", thru, [ws[k] for k in groups[grp]])
        ws.update(zip(groups[grp], arrs))
        in_flight[grp] = (send, recv)
    ws.update(zip(groups["A"], thru))
    pending = []

    class _Comm:
        def need(self, grp, ws, after):
            send, recv = in_flight[grp]
            arrs = _gather_wait(f"gather_{grp.lower()}_wait", send, recv, [ws[k] for k in groups[grp]], after)
            arrs = _forward_halves(f"forward_{grp.lower()}", arrs)
            return {**ws, **dict(zip(groups[grp], arrs))}

        swapping = None

        def reduce(self, grp, grads, carry):
            names = list(grads)
            send, recv, thru, parts, theirs = _swap_start(f"swap_{grp}_start", [carry], [grads[k] for k in names])
            self.swapping = (grp, names, send, recv, parts, theirs)
            return thru[0]

        def tick(self, carry):
            if self.swapping is None:
                return carry
            grp, names, send, recv, parts, theirs = self.swapping
            self.swapping = None
            parts, theirs = _swap_wait(f"swap_{grp}_wait", send, recv, parts, theirs, carry)
            return self.scatter(grp, names, parts, theirs, carry)

        def scatter(self, grp, names, parts, theirs, carry):
            sums = [_add_my_half(f"add_half_{k}", a, b, c_arr) for k, a, b in zip(names, parts, theirs)]
            send, recv, thru, sums, landing = _scatter_start(f"scatter_{grp}_start", [carry], sums)
            pending.append((grp, names, send, recv, sums, landing))
            return thru[0]

        def reduce_now(self, grp, grads, carry):
            names = list(grads)
            parts = [grads[k] for k in names]
            return self.scatter(grp, names, parts, _swap_halves(f"swap_{grp}", parts), carry)

    shards = [_unpack_rows(ws["small"][s], [p.shape for p in sharded_small]) for s in range(N_CHIPS)]
    w_a2, b_a2, head_norm, conv_w = [jnp.concatenate([shards[s][k] for s in range(N_CHIPS)], axis=-1) for k in range(4)]
    norms, small = _small_params(attn_norm, ffn_norm, kv_norm, final_norm, ffn_conv_b, w_a2, b_a2, head_norm, conv_w)

    comm = _Comm()
    loss_blk, grad_x, sm, last_big = _local_step(x[0], loss_target[0], ws, norms, small, comm)

    small_parts = [loss_blk, jnp.concatenate([sm["attn0"], sm["attn1"]]), jnp.concatenate([sm["ffn0"], sm["ffn1"]]),
                   sm["kv"], sm["final"], jnp.concatenate([sm["conv_b0"], sm["conv_b1"]]),
                   sm["w_a2p"][:GATE_RANK], sm["b_a2"], sm["head_norm"],
                   jnp.stack([jnp.concatenate(sm["conv_w0"]), jnp.concatenate(sm["conv_w1"])])]
    small_shapes = [(8, LANE), (2, d), (2, d), (d,), (d,), (2, f), (GATE_RANK, GLA_KEY_DIM), (GLA_KEY_DIM,),
                    (GLA_VAL_DIM // GLA_HEADS,), (2, 3, f)]
    _, reduced = _allgather8("reduce_small", _pack_rows(small_parts), True)
    reduced = comm.reduce_now("gla", last_big, reduced)

    loss_r, g_attn, g_ffn, g_kv, g_final, g_cb, g_a2, g_ba2, g_hn, g_cw = _unpack_rows(reduced, small_shapes)
    loss = loss_r[0, 0]

    def mine(g, axis):
        w = g.shape[axis] // N_CHIPS
        return lax.dynamic_slice_in_dim(g, chip * w, w, axis)

    grads = {
        "attn_norm": g_attn, "gla_w_a2": mine(g_a2, 1)[None], "gla_b_a2": mine(g_ba2, 0)[None],
        "gla_head_norm": mine(g_hn, 0)[None], "kv_norm": g_kv, "ffn_norm": g_ffn, "ffn_conv_w": mine(g_cw, 2),
        "ffn_conv_b": g_cb, "final_norm": g_final,
    }
    weights = {"attn_norm": (attn_norm, m_attn_norm, v_attn_norm), "gla_w_in": (gla_w_in, m_gla_w_in, v_gla_w_in),
               "gla_w_a2": (gla_w_a2, m_gla_w_a2, v_gla_w_a2), "gla_b_a2": (gla_b_a2, m_gla_b_a2, v_gla_b_a2),
               "gla_head_norm": (gla_head_norm, m_gla_head_norm, v_gla_head_norm),
               "gla_w_out": (gla_w_out, m_gla_w_out, v_gla_w_out), "kv_norm": (kv_norm, m_kv_norm, v_kv_norm),
               "w_kv": (w_kv, m_w_kv, v_w_kv), "dsa_w_q": (dsa_w_q, m_dsa_w_q, v_dsa_w_q),
               "dsa_w_out": (dsa_w_out, m_dsa_w_out, v_dsa_w_out), "ffn_norm": (ffn_norm, m_ffn_norm, v_ffn_norm),
               "ffn_w_up": (ffn_w_up, m_ffn_w_up, v_ffn_w_up), "ffn_conv_w": (ffn_conv_w, m_ffn_conv_w, v_ffn_conv_w),
               "ffn_conv_b": (ffn_conv_b, m_ffn_conv_b, v_ffn_conv_b),
               "ffn_w_down": (ffn_w_down, m_ffn_w_down, v_ffn_w_down), "final_norm": (final_norm, m_final_norm, v_final_norm)}
    order = list(weights)
    big_names = ("gla_w_in", "gla_w_out", "w_kv", "dsa_w_q", "dsa_w_out", "ffn_w_up", "ffn_w_down")
    delta, new_m, new_v = {}, {}, {}

    def adam_big(k, g):
        w, m, v = weights[k]
        cols = w.shape[-1]
        res = _adamw(f"adamw_{k}", w.reshape(-1, cols), g.reshape(-1, g.shape[-1]), m.reshape(-1, cols), v.reshape(-1, cols))
        delta[k], new_m[k], new_v[k] = [r.reshape(w.shape) for r in res[:3]]
        grads[k] = res[3].reshape(w.shape) if len(res) == 4 else g
        return res[0]

    full = {}
    after = reduced
    joining = []
    for grp, names, send, recv, sums, landing in pending[:-1]:
        sums, landing = _scatter_wait(f"scatter_{grp}_wait", send, recv, sums, landing, after)
        halves = [_sum_chips(f"sum_chips_{k}", s, q, place) for k, s, q in zip(names, sums, landing)]
        send, recv, halves = _join_start(f"join_{grp}_start", halves)
        joining.append((grp, names, send, recv, halves))
        after = halves[0]
    for grp, names, send, recv, halves in joining:
        joined = _join_wait(f"join_{grp}_wait", send, recv, halves, after)
        full.update(zip(names, joined))
        after = joined[0]
    after = adam_big("w_kv", full["wkv"])
    after = adam_big("dsa_w_q", full["wq"][None])
    after = adam_big("dsa_w_out", full["dout"][None])
    after = adam_big("ffn_w_up", jnp.stack([full["up0"], full["up1"]]))
    after = adam_big("ffn_w_down", jnp.stack([full["down0"], full["down1"]]))
    grp, names, send, recv, sums, landing = pending[-1]
    sums, landing = _scatter_wait(f"scatter_{grp}_wait", send, recv, sums, landing, after)
    halves = [_sum_chips(f"sum_chips_{k}", s, q, place) for k, s, q in zip(names, sums, landing)]
    full.update(zip(names, _join_halves(f"join_{grp}", halves)))
    adam_big("gla_w_in", full["gin"])
    adam_big("gla_w_out", full["gout"][None])
    small_names = [k for k in order if k not in big_names]
    packed = [_pack_rows([src[k] for k in small_names])
              for src in ({k: weights[k][0] for k in small_names}, grads, {k: weights[k][1] for k in small_names},
                          {k: weights[k][2] for k in small_names})]
    res = _adamw("adamw_small", *packed)
    shapes = [weights[k][0].shape for k in small_names]
    for dst, buf in zip((delta, new_m, new_v), res):
        for k, val in zip(small_names, _unpack_rows(buf, shapes)):
            dst[k] = val
    return (loss, grad_x[None], *[grads[k] for k in order], *[delta[k] for k in order], *[new_m[k] for k in order],
            *[new_v[k] for k in order])
```

```python
import math

import jax
import jax.numpy as jnp
from jax import lax
from jax.experimental import pallas as pl
from jax.experimental.pallas import tpu as pltpu

F32 = jnp.float32
BF16 = jnp.bfloat16

D_MODEL = 2048
SEQ = 4096
GLA_HEADS = 4
GLA_KEY_DIM = D_MODEL // 2
GLA_VAL_DIM = D_MODEL
GATE_RANK = 16
GATE_NORMALIZER = 16.0
GLA_CHUNK = 64
ATT_HEADS = 16
HEAD_DIM = 128
WINDOWS = (128, 512, 2048)
DILATIONS = (1, 4, 16)
ATT_BLOCK = 128
D_FF = 5632
EPS = 1e-6
ADAM_LR = 0.001
ADAM_B1 = 0.9
ADAM_B2 = 0.999
ADAM_EPS = 1e-08
ADAM_WD = 0.01
ADAM_STEP = 10

N_CHIPS = 4
N_DEV = 8
LANE = 128
A_PAD = 128
VMEM_LIMIT = 56 * 1024 * 1024
MAX_K_TILE = 2816
NEG = -1e30
MESH = pl.DeviceIdType.MESH

NN = (((1,), (0,)), ((), ()))
NT = (((1,), (1,)), ((), ()))
TN = (((0,), (0,)), ((), ()))


def _tile(n, cands):
    for c in cands:
        if c <= n and n % c == 0:
            return c
    return n


def _roundup(n, m):
    return -(-n // m) * m


def _params(n_axes):
    return pltpu.CompilerParams(dimension_semantics=("arbitrary",) * n_axes, vmem_limit_bytes=VMEM_LIMIT)


def _dot(a, b, dims):
    return lax.dot_general(a, b, dims, preferred_element_type=F32)


def _sigmoid(x):
    return 1.0 / (1.0 + jnp.exp(-x))


COL_SHARDED = ("gin", "up0", "up1", "wq", "wkv")
ROW_SHARDED = ("gout", "down0", "down1", "dout")


def _layout():
    f = D_FF
    hd = ATT_HEADS * HEAD_DIM
    gin = 2 * GLA_KEY_DIM + 2 * GLA_VAL_DIM + GATE_RANK
    up_w = 2 * f // N_CHIPS
    q_w = 3 * hd // N_CHIPS
    kv_w = 2 * hd // N_CHIPS
    dn_r = f // N_CHIPS
    go_r = GLA_VAL_DIM // N_CHIPS
    do_r = hd // N_CHIPS
    big = (1408, 1024, 512, 256, 128)
    return {
        "gin": (0, gin // N_CHIPS, LANE),
        "up0": (0, up_w, _tile(up_w, big)), "up1": (0, up_w, _tile(up_w, big)),
        "wq": (0, q_w, _tile(q_w, (512, 384, 256, 128))), "wkv": (0, kv_w, _tile(kv_w, (1024, 512, 256, 128))),
        "down0": (0, dn_r, _tile(dn_r, big)), "down1": (0, dn_r, _tile(dn_r, big)),
        "gout": (0, go_r, _tile(go_r, (512, 256, 128))), "dout": (0, do_r, _tile(do_r, (512, 256, 128))),
    }


def _matmul(name, a, b, dims, grid, a_spec, b_spec, o_spec, out_shape, acc_shape, add=None, add_spec=None):
    nk = grid[2]
    has_add = add is not None

    def body(*refs):
        a_ref, b_ref = refs[0], refs[1]
        pos = 2
        add_ref = None
        if has_add:
            add_ref = refs[pos]
            pos += 1
        o_ref = refs[pos]
        prod = _dot(a_ref[...].astype(BF16), b_ref[...].astype(BF16), dims)

        def finish(val):
            if has_add:
                val = val + add_ref[...].astype(F32)
            o_ref[...] = val.astype(o_ref.dtype)

        if nk == 1:
            finish(prod)
        else:
            acc_ref = refs[pos + 1]
            k = pl.program_id(2)

            @pl.when(k == 0)
            def _():
                acc_ref[...] = prod

            @pl.when(k > 0)
            def _():
                acc_ref[...] += prod

            @pl.when(k == nk - 1)
            def _():
                finish(acc_ref[...])

    in_specs = [a_spec, b_spec]
    args = [a, b]
    if has_add:
        in_specs.append(add_spec)
        args.append(add)
    scratch = [] if nk == 1 else [pltpu.VMEM(acc_shape, F32)]
    return pl.pallas_call(body, name=name, grid=grid, in_specs=in_specs, out_specs=o_spec, out_shape=out_shape,
                          scratch_shapes=scratch, compiler_params=_params(3))(*args)


def _mm_act_wc(name, a, wc, seg, out_dtype):
    off, w, tn = seg
    t_len, d = a.shape
    tm = _tile(t_len, (1024, 512, 256, 128))
    nps = w // tn
    ob = off // tn
    grid = (t_len // tm, N_CHIPS * nps, 1)
    return _matmul(
        name, a, wc, NN, grid,
        pl.BlockSpec((tm, d), lambda i, j, k: (i, 0)),
        pl.BlockSpec((None, d, tn), lambda i, j, k: (j // nps, 0, ob + j % nps)),
        pl.BlockSpec((tm, tn), lambda i, j, k: (i, j)),
        jax.ShapeDtypeStruct((t_len, N_CHIPS * w), out_dtype), (tm, tn))


def _mm_dact_wcT(name, dy, wc, seg, add=None):
    off, w, tk = seg
    if off == 0 and w <= MAX_K_TILE:
        tk = w
    t_len = dy.shape[0]
    d = wc.shape[1]
    tm = _tile(t_len, (1024, 512, 256, 128))
    tn = _tile(d, (1024, 512, 256, 128))
    kps = w // tk
    ob = off // tk
    grid = (t_len // tm, d // tn, N_CHIPS * kps)
    return _matmul(
        name, dy, wc, NT, grid,
        pl.BlockSpec((tm, tk), lambda i, j, k: (i, k)),
        pl.BlockSpec((None, tn, tk), lambda i, j, k: (k // kps, j, ob + k % kps)),
        pl.BlockSpec((tm, tn), lambda i, j, k: (i, j)),
        jax.ShapeDtypeStruct((t_len, d), F32), (tm, tn),
        add=add, add_spec=None if add is None else pl.BlockSpec((tm, tn), lambda i, j, k: (i, j)))


def _mm_grad_wc(name, a, dy, seg):
    _, w, tn = seg
    t_len, d = a.shape
    tm = _tile(d, (1024, 512, 256, 128))
    tk = _tile(t_len, (2048, 1024, 512, 256, 128))
    nps = w // tn
    grid = (d // tm, N_CHIPS * nps, t_len // tk)
    return _matmul(
        name, a, dy, TN, grid,
        pl.BlockSpec((tk, tm), lambda i, j, k: (k, i)),
        pl.BlockSpec((tk, tn), lambda i, j, k: (k, j)),
        pl.BlockSpec((None, tm, tn), lambda i, j, k: (j // nps, i, j % nps)),
        jax.ShapeDtypeStruct((N_CHIPS, d, w), BF16), (tm, tn))


def _is_plain(wr, seg):
    return seg[0] == 0 and wr.shape[1] == seg[1] and (N_CHIPS * seg[1]) % 1024 == 0


def _mm_act_wr(name, a, wr, seg, add):
    off, r, tk = seg
    t_len = a.shape[0]
    d = wr.shape[2]
    if seg[0] == 0 and wr.shape[1] == r:
        return _mm_plain(name, a, wr.reshape(N_CHIPS * r, d), NN, F32, add=add)
    tm = _tile(t_len, (1024, 512, 256, 128))
    tn = _tile(d, (1024, 512, 256, 128))
    kps = r // tk
    ob = off // tk
    grid = (t_len // tm, d // tn, N_CHIPS * kps)
    return _matmul(
        name, a, wr, NN, grid,
        pl.BlockSpec((tm, tk), lambda i, j, k: (i, k)),
        pl.BlockSpec((None, tk, tn), lambda i, j, k: (k // kps, ob + k % kps, j)),
        pl.BlockSpec((tm, tn), lambda i, j, k: (i, j)),
        jax.ShapeDtypeStruct((t_len, d), F32), (tm, tn),
        add=add, add_spec=pl.BlockSpec((tm, tn), lambda i, j, k: (i, j)))


def _mm_dact_wrT(name, dh, wr, seg):
    off, r, tn = seg
    t_len, d = dh.shape
    if _is_plain(wr, seg):
        return _mm_plain(name, dh, wr.reshape(N_CHIPS * r, d), NT, BF16)
    tm = _tile(t_len, (1024, 512, 256, 128))
    nps = r // tn
    ob = off // tn
    grid = (t_len // tm, N_CHIPS * nps, 1)
    return _matmul(
        name, dh, wr, NT, grid,
        pl.BlockSpec((tm, d), lambda i, j, k: (i, 0)),
        pl.BlockSpec((None, tn, d), lambda i, j, k: (j // nps, ob + j % nps, 0)),
        pl.BlockSpec((tm, tn), lambda i, j, k: (i, j)),
        jax.ShapeDtypeStruct((t_len, N_CHIPS * r), BF16), (tm, tn))


def _mm_grad_wr(name, a, dh, seg):
    _, r, tm = seg
    t_len, d = dh.shape
    if (N_CHIPS * r) % 1024 == 0:
        return _mm_plain(name, a, dh, TN, BF16).reshape(N_CHIPS, r, d)
    tn = _tile(d, (1024, 512, 256, 128))
    tk = _tile(t_len, (2048, 1024, 512, 256, 128))
    mps = r // tm
    grid = (N_CHIPS * mps, d // tn, t_len // tk)
    return _matmul(
        name, a, dh, TN, grid,
        pl.BlockSpec((tk, tm), lambda i, j, k: (k, i)),
        pl.BlockSpec((tk, tn), lambda i, j, k: (k, j)),
        pl.BlockSpec((None, tm, tn), lambda i, j, k: (i // mps, i % mps, j)),
        jax.ShapeDtypeStruct((N_CHIPS, r, d), BF16), (tm, tn))


def _mm_plain(name, a, b, dims, out_dtype, add=None):
    if dims == NN:
        m, kd = a.shape
        n = b.shape[1]
    elif dims == NT:
        m, kd = a.shape
        n = b.shape[0]
    else:
        kd, m = a.shape
        n = b.shape[1]
    tm = _tile(m, (1024, 512, 256, 128))
    tn = _tile(n, (1024, 768, 512, 256, 128))
    tk = _tile(kd, (MAX_K_TILE, 2048, 1408, 1024, 512, 256, 128))
    grid = (m // tm, n // tn, kd // tk)
    if dims == NN:
        a_spec = pl.BlockSpec((tm, tk), lambda i, j, k: (i, k))
        b_spec = pl.BlockSpec((tk, tn), lambda i, j, k: (k, j))
    elif dims == NT:
        a_spec = pl.BlockSpec((tm, tk), lambda i, j, k: (i, k))
        b_spec = pl.BlockSpec((tn, tk), lambda i, j, k: (j, k))
    else:
        a_spec = pl.BlockSpec((tk, tm), lambda i, j, k: (k, i))
        b_spec = pl.BlockSpec((tk, tn), lambda i, j, k: (k, j))
    o_spec = pl.BlockSpec((tm, tn), lambda i, j, k: (i, j))
    return _matmul(name, a, b, dims, grid, a_spec, b_spec, o_spec, jax.ShapeDtypeStruct((m, n), out_dtype), (tm, tn),
                   add=add, add_spec=None if add is None else o_spec)


def _rms_fwd(name, x, g):
    t_len, d = x.shape
    tm = _tile(t_len, (512, 256, 128))

    def body(x_ref, g_ref, o_ref):
        xv = x_ref[...]
        r = lax.rsqrt(jnp.mean(xv * xv, axis=-1, keepdims=True) + EPS)
        o_ref[...] = (xv * r * g_ref[...]).astype(o_ref.dtype)

    return pl.pallas_call(
        body, name=name, grid=(t_len // tm,),
        in_specs=[pl.BlockSpec((tm, d), lambda i: (i, 0)), pl.BlockSpec((1, d), lambda i: (0, 0))],
        out_specs=pl.BlockSpec((tm, d), lambda i: (i, 0)),
        out_shape=jax.ShapeDtypeStruct((t_len, d), BF16), compiler_params=_params(1))(x, g)


def _rms_bwd(name, dy, x, g, dres):
    t_len, d = x.shape
    tm = _tile(t_len, (256, 128))

    def body(dy_ref, x_ref, g_ref, dres_ref, dx_ref, dg_ref):
        xv = x_ref[...]
        r = lax.rsqrt(jnp.mean(xv * xv, axis=-1, keepdims=True) + EPS)
        xhat = xv * r
        dyv = dy_ref[...].astype(F32)
        dxn = dyv * g_ref[...]
        dx = r * (dxn - xhat * jnp.mean(dxn * xhat, axis=-1, keepdims=True))
        dx_ref[...] = dres_ref[...] + dx
        part = jnp.sum(dyv * xhat, axis=0, keepdims=True)

        @pl.when(pl.program_id(0) == 0)
        def _():
            dg_ref[...] = part

        @pl.when(pl.program_id(0) > 0)
        def _():
            dg_ref[...] += part

    row = pl.BlockSpec((tm, d), lambda i: (i, 0))
    vec = pl.BlockSpec((1, d), lambda i: (0, 0))
    return pl.pallas_call(
        body, name=name, grid=(t_len // tm,), in_specs=[row, row, vec, row], out_specs=(row, vec),
        out_shape=(jax.ShapeDtypeStruct((t_len, d), F32), jax.ShapeDtypeStruct((1, d), F32)),
        compiler_params=_params(1))(dy, x, g, dres)


def _loss_head(h, g, target):
    t_len, d = h.shape
    tm = _tile(t_len, (256, 128))

    def body(h_ref, g_ref, t_ref, dh_ref, dg_ref, loss_ref):
        xv = h_ref[...]
        gv = g_ref[...]
        r = lax.rsqrt(jnp.mean(xv * xv, axis=-1, keepdims=True) + EPS)
        xhat = xv * r
        err = xhat * gv - t_ref[...]
        dyv = err * (1.0 / d)
        dxn = dyv * gv
        dh_ref[...] = r * (dxn - xhat * jnp.mean(dxn * xhat, axis=-1, keepdims=True))
        part = jnp.sum(dyv * xhat, axis=0, keepdims=True)
        lpart = jnp.zeros((8, LANE), F32) + (0.5 / d) * jnp.sum(err * err)

        @pl.when(pl.program_id(0) == 0)
        def _():
            dg_ref[...] = part
            loss_ref[...] = lpart

        @pl.when(pl.program_id(0) > 0)
        def _():
            dg_ref[...] += part
            loss_ref[...] += lpart

    row = pl.BlockSpec((tm, d), lambda i: (i, 0))
    vec = pl.BlockSpec((1, d), lambda i: (0, 0))
    return pl.pallas_call(
        body, name="loss_head", grid=(t_len // tm,), in_specs=[row, vec, row],
        out_specs=(row, vec, pl.BlockSpec((8, LANE), lambda i: (0, 0))),
        out_shape=(jax.ShapeDtypeStruct((t_len, d), F32), jax.ShapeDtypeStruct((1, d), F32),
                   jax.ShapeDtypeStruct((8, LANE), F32)),
        compiler_params=_params(1))(h, g, target)


def _chunk_row(shape):
    return lax.broadcasted_iota(jnp.int32, shape, 0) % GLA_CHUNK


def _gla_gate_fwd(a, w_a2p, b_a2):
    t_len = a.shape[0]
    kd = w_a2p.shape[1]
    tm = _tile(t_len, (256, 128, 64))

    def body(a_ref, w_ref, b_ref, ga_ref, cum_ref):
        ga = _dot(a_ref[...], w_ref[...].astype(BF16), NN) + b_ref[...]
        ga_ref[...] = ga
        la = (jnp.minimum(ga, 0.0) - jnp.log(1.0 + jnp.exp(-jnp.abs(ga)))) * (1.0 / GATE_NORMALIZER)
        row = _chunk_row(la.shape)
        s = 1
        while s < GLA_CHUNK:
            la = la + jnp.where(row >= s, pltpu.roll(la, s, 0), 0.0)
            s *= 2
        cum_ref[...] = la

    return pl.pallas_call(
        body, name="gla_gate_fwd", grid=(t_len // tm,),
        in_specs=[pl.BlockSpec((tm, A_PAD), lambda i: (i, 0)), pl.BlockSpec((A_PAD, kd), lambda i: (0, 0)),
                  pl.BlockSpec((1, kd), lambda i: (0, 0))],
        out_specs=(pl.BlockSpec((tm, kd), lambda i: (i, 0)), pl.BlockSpec((tm, kd), lambda i: (i, 0))),
        out_shape=(jax.ShapeDtypeStruct((t_len, kd), F32), jax.ShapeDtypeStruct((t_len, kd), F32)),
        compiler_params=_params(1))(a, w_a2p, b_a2)


def _gla_gate_bwd(dcum, ga, a, w_a2p):
    t_len, kd = dcum.shape
    tm = _tile(t_len, (256, 128, 64))

    def body(dc_ref, ga_ref, a_ref, w_ref, da_ref, dw_ref, db_ref):
        x = dc_ref[...]
        row = _chunk_row(x.shape)
        s = 1
        while s < GLA_CHUNK:
            x = x + jnp.where(row < GLA_CHUNK - s, pltpu.roll(x, tm - s, 0), 0.0)
            s *= 2
        dga = x * (1.0 / GATE_NORMALIZER) * _sigmoid(-ga_ref[...])
        dgab = dga.astype(BF16)
        da_ref[...] = _dot(dgab, w_ref[...].astype(BF16), NT).astype(da_ref.dtype)
        dw = _dot(a_ref[...], dgab, TN)
        db = jnp.sum(dga, axis=0, keepdims=True)

        @pl.when(pl.program_id(0) == 0)
        def _():
            dw_ref[...] = dw
            db_ref[...] = db

        @pl.when(pl.program_id(0) > 0)
        def _():
            dw_ref[...] += dw
            db_ref[...] += db

    wide = pl.BlockSpec((tm, kd), lambda i: (i, 0))
    return pl.pallas_call(
        body, name="gla_gate_bwd", grid=(t_len // tm,),
        in_specs=[wide, wide, pl.BlockSpec((tm, A_PAD), lambda i: (i, 0)), pl.BlockSpec((A_PAD, kd), lambda i: (0, 0))],
        out_specs=(pl.BlockSpec((tm, A_PAD), lambda i: (i, 0)), pl.BlockSpec((A_PAD, kd), lambda i: (0, 0)),
                   pl.BlockSpec((1, kd), lambda i: (0, 0))),
        out_shape=(jax.ShapeDtypeStruct((t_len, A_PAD), BF16), jax.ShapeDtypeStruct((A_PAD, kd), F32),
                   jax.ShapeDtypeStruct((1, kd), F32)),
        compiler_params=_params(1))(dcum, ga, a, w_a2p)


GLA_STEP_CHUNKS = 4


def _gla_dims():
    dk = GLA_KEY_DIM // GLA_HEADS
    dv = GLA_VAL_DIM // GLA_HEADS
    return dk, dv


def _gla_fwd(proj, cum):
    t_len = proj.shape[0]
    dk, dv = _gla_dims()
    nc = t_len // GLA_CHUNK
    c = GLA_CHUNK
    scale = dk ** -0.5
    v0 = 2 * GLA_KEY_DIM // dv

    per = _tile(nc, (GLA_STEP_CHUNKS, 2, 1))
    rows = per * c

    def body(q_ref, k_ref, v_ref, cum_ref, o_ref, st_ref, s_scr):
        @pl.when(pl.program_id(1) == 0)
        def _():
            s_scr[...] = jnp.zeros_like(s_scr)

        tri = lax.broadcasted_iota(jnp.int32, (c, c), 0) >= lax.broadcasted_iota(jnp.int32, (c, c), 1)
        for i in range(per):
            rs = slice(i * c, (i + 1) * c)
            cm = cum_ref[rs, :]
            last = cm[c - 1:c, :]
            q = q_ref[rs, :].astype(F32) * scale
            k = k_ref[rs, :].astype(F32)
            v = v_ref[rs, :].astype(BF16)
            qd = (q * jnp.exp(cm)).astype(BF16)
            ki = (k * jnp.exp(-cm)).astype(BF16)
            ke = (k * jnp.exp(last - cm)).astype(BF16)
            sc = jnp.where(tri, _dot(qd, ki, NT), 0.0)
            st = s_scr[...]
            st_ref[i] = st
            o_ref[rs, :] = _dot(sc.astype(BF16), v, NN) + _dot(qd, st.astype(BF16), NT)
            s_scr[...] = st * jnp.exp(last) + _dot(v, ke, TN)

    return pl.pallas_call(
        body, name="gla_fwd", grid=(GLA_HEADS, nc // per),
        in_specs=[pl.BlockSpec((rows, dk), lambda h, n: (n, h)),
                  pl.BlockSpec((rows, dk), lambda h, n: (n, GLA_HEADS + h)),
                  pl.BlockSpec((rows, dv), lambda h, n: (n, v0 + h)),
                  pl.BlockSpec((rows, dk), lambda h, n: (n, h))],
        out_specs=(pl.BlockSpec((rows, dv), lambda h, n: (n, h)),
                   pl.BlockSpec((None, per, dv, dk), lambda h, n: (h, n, 0, 0))),
        out_shape=(jax.ShapeDtypeStruct((t_len, GLA_VAL_DIM), F32),
                   jax.ShapeDtypeStruct((GLA_HEADS, nc, dv, dk), F32)),
        scratch_shapes=[pltpu.VMEM((dv, dk), F32)], compiler_params=_params(2))(proj, proj, proj, cum)


def _gla_bwd(proj, cum, states, do):
    t_len = proj.shape[0]
    dk, dv = _gla_dims()
    nc = t_len // GLA_CHUNK
    c = GLA_CHUNK
    scale = dk ** -0.5
    v0 = 2 * GLA_KEY_DIM // dv

    per = _tile(nc, (GLA_STEP_CHUNKS, 2, 1))
    rows = per * c

    def body(q_ref, k_ref, v_ref, cum_ref, st_ref, do_ref, dq_ref, dk_ref, dv_ref, dc_ref, ds_scr):
        @pl.when(pl.program_id(1) == 0)
        def _():
            ds_scr[...] = jnp.zeros_like(ds_scr)

        tri = lax.broadcasted_iota(jnp.int32, (c, c), 0) >= lax.broadcasted_iota(jnp.int32, (c, c), 1)
        row = lax.broadcasted_iota(jnp.int32, (c, dk), 0)
        for i in reversed(range(per)):
            rs = slice(i * c, (i + 1) * c)
            cm = cum_ref[rs, :]
            last = cm[c - 1:c, :]
            e_c = jnp.exp(cm)
            e_nc = jnp.exp(-cm)
            e_lc = jnp.exp(last - cm)
            e_l = jnp.exp(last)
            q = q_ref[rs, :].astype(F32) * scale
            k = k_ref[rs, :].astype(F32)
            v = v_ref[rs, :].astype(BF16)
            dov = do_ref[rs, :]
            qd32 = q * e_c
            ki32 = k * e_nc
            ke32 = k * e_lc
            qd = qd32.astype(BF16)
            ki = ki32.astype(BF16)
            ke = ke32.astype(BF16)
            st = st_ref[i]
            dst = ds_scr[...]
            dstb = dst.astype(BF16)
            am = jnp.where(tri, _dot(dov, v, NT), 0.0).astype(BF16)
            pm = jnp.where(tri, _dot(qd, ki, NT), 0.0).astype(BF16)
            dqd = _dot(am, ki, NN) + _dot(dov, st.astype(BF16), NN)
            dki = _dot(am, qd, TN)
            dvv = _dot(pm, dov, TN) + _dot(ke, dstb, NT)
            dke = _dot(v, dstb, NN)
            d_el = jnp.sum(dst * st, axis=0, keepdims=True)
            ds_scr[...] = dst * e_l + _dot(dov, qd, TN)
            dq_ref[rs, :] = (dqd * scale * e_c).astype(dq_ref.dtype)
            dk_ref[rs, :] = (dki * e_nc + dke * e_lc).astype(dk_ref.dtype)
            dv_ref[rs, :] = dvv.astype(dv_ref.dtype)
            dkeke = dke * ke32
            dcum = dqd * qd32 - dki * ki32 - dkeke
            dlast = jnp.sum(dkeke, axis=0, keepdims=True) + d_el * e_l
            dc_ref[rs, :] = jnp.where(row == c - 1, dcum + dlast, dcum)

    rev = nc // per - 1
    return pl.pallas_call(
        body, name="gla_bwd", grid=(GLA_HEADS, nc // per),
        in_specs=[pl.BlockSpec((rows, dk), lambda h, n: (rev - n, h)),
                  pl.BlockSpec((rows, dk), lambda h, n: (rev - n, GLA_HEADS + h)),
                  pl.BlockSpec((rows, dv), lambda h, n: (rev - n, v0 + h)),
                  pl.BlockSpec((rows, dk), lambda h, n: (rev - n, h)),
                  pl.BlockSpec((None, per, dv, dk), lambda h, n: (h, rev - n, 0, 0)),
                  pl.BlockSpec((rows, dv), lambda h, n: (rev - n, h))],
        out_specs=(pl.BlockSpec((rows, dk), lambda h, n: (rev - n, h)),
                   pl.BlockSpec((rows, dk), lambda h, n: (rev - n, h)),
                   pl.BlockSpec((rows, dv), lambda h, n: (rev - n, h)),
                   pl.BlockSpec((rows, dk), lambda h, n: (rev - n, h))),
        out_shape=(jax.ShapeDtypeStruct((t_len, GLA_KEY_DIM), BF16), jax.ShapeDtypeStruct((t_len, GLA_KEY_DIM), BF16),
                   jax.ShapeDtypeStruct((t_len, GLA_VAL_DIM), BF16), jax.ShapeDtypeStruct((t_len, GLA_KEY_DIM), F32)),
        scratch_shapes=[pltpu.VMEM((dv, dk), F32)], compiler_params=_params(2))(proj, proj, proj, cum, states, do)


def _gla_out_fwd(o, proj, gn):
    t_len = o.shape[0]
    _, dv = _gla_dims()
    tm = _tile(t_len, (512, 256, 128))
    r0 = (2 * GLA_KEY_DIM + GLA_VAL_DIM) // dv

    def body(o_ref, r_ref, g_ref, y_ref):
        ov = o_ref[...]
        rs = lax.rsqrt(jnp.mean(ov * ov, axis=-1, keepdims=True) + EPS)
        rv = r_ref[...].astype(F32)
        y_ref[...] = (ov * rs * g_ref[...] * (rv * _sigmoid(rv))).astype(y_ref.dtype)

    return pl.pallas_call(
        body, name="gla_out_fwd", grid=(t_len // tm, GLA_HEADS),
        in_specs=[pl.BlockSpec((tm, dv), lambda i, h: (i, h)), pl.BlockSpec((tm, dv), lambda i, h: (i, r0 + h)),
                  pl.BlockSpec((1, dv), lambda i, h: (0, 0))],
        out_specs=pl.BlockSpec((tm, dv), lambda i, h: (i, h)),
        out_shape=jax.ShapeDtypeStruct((t_len, GLA_VAL_DIM), BF16), compiler_params=_params(2))(o, proj, gn)


def _gla_out_bwd(dy, o, proj, gn):
    t_len = o.shape[0]
    _, dv = _gla_dims()
    tm = _tile(t_len, (512, 256, 128))
    r0 = (2 * GLA_KEY_DIM + GLA_VAL_DIM) // dv

    def body(dy_ref, o_ref, r_ref, g_ref, do_ref, dr_ref, dg_ref):
        ov = o_ref[...]
        gv = g_ref[...]
        rs = lax.rsqrt(jnp.mean(ov * ov, axis=-1, keepdims=True) + EPS)
        xhat = ov * rs
        rv = r_ref[...].astype(F32)
        sg = _sigmoid(rv)
        gate = rv * sg
        dyv = dy_ref[...].astype(F32)
        dn = dyv * gate
        dr_ref[...] = (dyv * xhat * gv * (sg * (1.0 + rv * (1.0 - sg)))).astype(dr_ref.dtype)
        dxn = dn * gv
        do_ref[...] = (rs * (dxn - xhat * jnp.mean(dxn * xhat, axis=-1, keepdims=True))).astype(do_ref.dtype)
        part = jnp.sum(dn * xhat, axis=0, keepdims=True)
        first = (pl.program_id(0) == 0) & (pl.program_id(1) == 0)

        @pl.when(first)
        def _():
            dg_ref[...] = part

        @pl.when(jnp.logical_not(first))
        def _():
            dg_ref[...] += part

    blk = pl.BlockSpec((tm, dv), lambda i, h: (i, h))
    return pl.pallas_call(
        body, name="gla_out_bwd", grid=(t_len // tm, GLA_HEADS),
        in_specs=[blk, blk, pl.BlockSpec((tm, dv), lambda i, h: (i, r0 + h)), pl.BlockSpec((1, dv), lambda i, h: (0, 0))],
        out_specs=(blk, blk, pl.BlockSpec((1, dv), lambda i, h: (0, 0))),
        out_shape=(jax.ShapeDtypeStruct((t_len, GLA_VAL_DIM), BF16), jax.ShapeDtypeStruct((t_len, GLA_VAL_DIM), BF16),
                   jax.ShapeDtypeStruct((1, dv), F32)),
        compiler_params=_params(2))(dy, o, proj, gn)


def _alibi_slopes():
    n = ATT_HEADS
    start = 2.0 ** (-8.0 / n)
    return [start ** (i + 1) for i in range(n)]


def _att_masks(d):
    b = ATT_BLOCK
    qa = lax.broadcasted_iota(jnp.int32, (b, b), 0)
    kb = lax.broadcasted_iota(jnp.int32, (b, b), 1)
    dist_c = qa - kb
    dist_p = qa - kb + b
    return dist_c >= 0, dist_p <= b, (dist_c * d).astype(F32), (dist_p * d).astype(F32)


def _to_dilated(name, x, d, c0=0, w=None):
    part = x if w is None else x[:, c0:c0 + w]
    return part.reshape(x.shape[0] // d, -1)


def _from_dilated(name, y, d):
    return y.reshape(y.shape[0] * d, y.shape[1] // d)


def _att_views(q_all, kv, g):
    d = DILATIONS[g]
    hd = ATT_HEADS * HEAD_DIM
    if d == 1:
        return q_all, kv
    return _to_dilated(f"q_dilated{g}", q_all, d, g * hd, hd), _to_dilated(f"kv_dilated{g}", kv, d)


def _att_fwd(views, g):
    d = DILATIONS[g]
    assert WINDOWS[g] // d == ATT_BLOCK
    qv, kvv = views
    hd = ATT_HEADS * HEAD_DIM
    sub = kvv.shape[0]
    t_len = sub * d
    nb = sub // ATT_BLOCK
    b = ATT_BLOCK
    e = HEAD_DIM
    scale = e ** -0.5
    slopes = _alibi_slopes()
    qc = (lambda r: 3 * r + g) if d == 1 else (lambda r: r)

    def body(q_ref, kp_ref, kc_ref, vp_ref, vc_ref, o_ref, l_ref, s_scr, p_scr, li_scr):
        ib = pl.program_id(1)
        valid_c, valid_p0, dist_c, dist_p = _att_masks(d)
        valid_p = valid_p0 & (ib > 0)
        for h in range(ATT_HEADS):
            hs = slice(h * e, (h + 1) * e)
            qh = q_ref[:, hs]
            s_scr[h, 0] = _dot(qh, kc_ref[:, hs], NT)
            s_scr[h, 1] = _dot(qh, kp_ref[:, hs], NT)
        l_ref[...] = jnp.zeros_like(l_ref)
        for h in range(ATT_HEADS):
            s_c = jnp.where(valid_c, s_scr[h, 0] * scale - slopes[h] * dist_c, NEG)
            s_p = jnp.where(valid_p, s_scr[h, 1] * scale - slopes[h] * dist_p, NEG)
            m = jnp.maximum(jnp.max(s_c, axis=1, keepdims=True), jnp.max(s_p, axis=1, keepdims=True))
            p_c = jnp.where(valid_c, jnp.exp(s_c - m), 0.0)
            p_p = jnp.where(valid_p, jnp.exp(s_p - m), 0.0)
            l = jnp.sum(p_c, axis=1, keepdims=True) + jnp.sum(p_p, axis=1, keepdims=True)
            p_scr[h, 0] = p_c.astype(BF16)
            p_scr[h, 1] = p_p.astype(BF16)
            li_scr[:, h:h + 1] = 1.0 / l
            l_ref[:, h:h + 1] = m + jnp.log(l)
        for h in range(ATT_HEADS):
            hs = slice(h * e, (h + 1) * e)
            acc = _dot(p_scr[h, 0], vc_ref[:, hs], NN) + _dot(p_scr[h, 1], vp_ref[:, hs], NN)
            o_ref[:, hs] = acc * li_scr[:, h:h + 1]

    blk = (b, hd)
    cblk = (b, LANE)
    o, lse = pl.pallas_call(
        body, name=f"att_fwd{g}", grid=(d, nb),
        scratch_shapes=[pltpu.VMEM((ATT_HEADS, 2, b, b), F32), pltpu.VMEM((ATT_HEADS, 2, b, b), BF16),
                        pltpu.VMEM((b, LANE), F32)],
        in_specs=[pl.BlockSpec(blk, lambda r, i: (i, qc(r))),
                  pl.BlockSpec(blk, lambda r, i: (jnp.maximum(i - 1, 0), 2 * r)),
                  pl.BlockSpec(blk, lambda r, i: (i, 2 * r)),
                  pl.BlockSpec(blk, lambda r, i: (jnp.maximum(i - 1, 0), 2 * r + 1)),
                  pl.BlockSpec(blk, lambda r, i: (i, 2 * r + 1))],
        out_specs=(pl.BlockSpec(blk, lambda r, i: (i, r)), pl.BlockSpec(cblk, lambda r, i: (i, r))),
        out_shape=(jax.ShapeDtypeStruct((sub, d * hd), F32), jax.ShapeDtypeStruct((sub, d * LANE), F32)),
        compiler_params=_params(2))(qv, kvv, kvv, kvv, kvv)
    return _from_dilated(f"o_natural{g}", o, d), lse.reshape(t_len, LANE)


def _att_merge(os, ls):
    t_len, hd = os[0].shape
    tm = _tile(t_len, (256, 128))
    e = HEAD_DIM

    def body(o0, o1, o2, l0, l1, l2, of_ref, ob_ref, l_ref):
        a0, a1, a2 = l0[...], l1[...], l2[...]
        m = jnp.maximum(jnp.maximum(a0, a1), a2)
        e0, e1, e2 = jnp.exp(a0 - m), jnp.exp(a1 - m), jnp.exp(a2 - m)
        den = e0 + e1 + e2
        w0, w1, w2 = e0 / den, e1 / den, e2 / den
        l_ref[...] = m + jnp.log(den)
        for h in range(ATT_HEADS):
            hs = slice(h * e, (h + 1) * e)
            c = slice(h, h + 1)
            o = w0[:, c] * o0[:, hs] + w1[:, c] * o1[:, hs] + w2[:, c] * o2[:, hs]
            of_ref[:, hs] = o
            ob_ref[:, hs] = o.astype(ob_ref.dtype)

    row = pl.BlockSpec((tm, hd), lambda i: (i, 0))
    crow = pl.BlockSpec((tm, LANE), lambda i: (i, 0))
    return pl.pallas_call(
        body, name="att_merge", grid=(t_len // tm,), in_specs=[row] * 3 + [crow] * 3, out_specs=(row, row, crow),
        out_shape=(jax.ShapeDtypeStruct((t_len, hd), F32), jax.ShapeDtypeStruct((t_len, hd), BF16),
                   jax.ShapeDtypeStruct((t_len, LANE), F32)),
        compiler_params=_params(1))(*os, *ls)


def _att_delta(do, o):
    t_len, hd = o.shape
    tm = _tile(t_len, (256, 128))
    e = HEAD_DIM

    def body(do_ref, o_ref, d_ref):
        d_ref[...] = jnp.zeros_like(d_ref)
        for h in range(ATT_HEADS):
            hs = slice(h * e, (h + 1) * e)
            d_ref[:, h:h + 1] = jnp.sum(do_ref[:, hs].astype(F32) * o_ref[:, hs], axis=1, keepdims=True)

    row = pl.BlockSpec((tm, hd), lambda i: (i, 0))
    return pl.pallas_call(
        body, name="att_delta", grid=(t_len // tm,), in_specs=[row, row],
        out_specs=pl.BlockSpec((tm, LANE), lambda i: (i, 0)),
        out_shape=jax.ShapeDtypeStruct((t_len, LANE), F32), compiler_params=_params(1))(do, o)


def _att_bwd(views, delta, lse, do, g):
    d = DILATIONS[g]
    qv, kvv = views
    hd = ATT_HEADS * HEAD_DIM
    sub = kvv.shape[0]
    t_len = sub * d
    nb = sub // ATT_BLOCK
    b = ATT_BLOCK
    e = HEAD_DIM
    scale = e ** -0.5
    slopes = _alibi_slopes()
    qc = (lambda r: 3 * r + g) if d == 1 else (lambda r: r)
    dlv = delta.reshape(sub, d * LANE)
    lv = lse.reshape(sub, d * LANE)
    dov = do if d == 1 else _to_dilated(f"do_dilated{g}", do, d)

    def body(qj_ref, qn_ref, kp_ref, kc_ref, vp_ref, vc_ref, doj_ref, don_ref, dj_ref, dn_ref, lj_ref, ln_ref,
             dq_ref, dk_ref, dv_ref, s_scr, dp_scr, p_scr, ds_scr):
        j = pl.program_id(1)
        valid_c, valid_p0, dist_c, dist_p = _att_masks(d)
        valid = (valid_c, valid_p0 & (j > 0), valid_p0 & (j + 1 < nb))
        dist = (dist_c, dist_p, dist_p)
        for h in range(ATT_HEADS):
            hs = slice(h * e, (h + 1) * e)
            qj, qn = qj_ref[:, hs], qn_ref[:, hs]
            kc, kp = kc_ref[:, hs], kp_ref[:, hs]
            vc, vp = vc_ref[:, hs], vp_ref[:, hs]
            doj, don = doj_ref[:, hs], don_ref[:, hs]
            s_scr[h, 0] = _dot(qj, kc, NT)
            s_scr[h, 1] = _dot(qj, kp, NT)
            s_scr[h, 2] = _dot(qn, kc, NT)
            dp_scr[h, 0] = _dot(doj, vc, NT)
            dp_scr[h, 1] = _dot(doj, vp, NT)
            dp_scr[h, 2] = _dot(don, vc, NT)
        for h in range(ATT_HEADS):
            c = slice(h, h + 1)
            lse_t = (lj_ref[:, c], lj_ref[:, c], ln_ref[:, c])
            dlt_t = (dj_ref[:, c], dj_ref[:, c], dn_ref[:, c])
            for t in range(3):
                s = s_scr[h, t] * scale - slopes[h] * dist[t]
                p = jnp.where(valid[t], jnp.exp(jnp.where(valid[t], s - lse_t[t], NEG)), 0.0)
                p_scr[h, t] = p.astype(BF16)
                ds_scr[h, t] = (p * (dp_scr[h, t] - dlt_t[t])).astype(BF16)
        for h in range(ATT_HEADS):
            hs = slice(h * e, (h + 1) * e)
            dq = _dot(ds_scr[h, 0], kc_ref[:, hs], NN) + _dot(ds_scr[h, 1], kp_ref[:, hs], NN)
            dk = _dot(ds_scr[h, 0], qj_ref[:, hs], TN) + _dot(ds_scr[h, 2], qn_ref[:, hs], TN)
            dv = _dot(p_scr[h, 0], doj_ref[:, hs], TN) + _dot(p_scr[h, 2], don_ref[:, hs], TN)
            dq_ref[:, hs] = (dq * scale).astype(dq_ref.dtype)
            dk_ref[:, hs] = (dk * scale).astype(dk_ref.dtype)
            dv_ref[:, hs] = dv.astype(dv_ref.dtype)

    blk = (b, hd)
    cblk = (b, LANE)
    nxt = lambda i: jnp.minimum(i + 1, nb - 1)
    prv = lambda i: jnp.maximum(i - 1, 0)
    tiles = (ATT_HEADS, 3, b, b)
    dq, dk, dv = pl.pallas_call(
        body, name=f"att_bwd{g}", grid=(d, nb),
        scratch_shapes=[pltpu.VMEM(tiles, F32), pltpu.VMEM(tiles, F32), pltpu.VMEM(tiles, BF16), pltpu.VMEM(tiles, BF16)],
        in_specs=[pl.BlockSpec(blk, lambda r, i: (i, qc(r))),
                  pl.BlockSpec(blk, lambda r, i: (nxt(i), qc(r))),
                  pl.BlockSpec(blk, lambda r, i: (prv(i), 2 * r)),
                  pl.BlockSpec(blk, lambda r, i: (i, 2 * r)),
                  pl.BlockSpec(blk, lambda r, i: (prv(i), 2 * r + 1)),
                  pl.BlockSpec(blk, lambda r, i: (i, 2 * r + 1)),
                  pl.BlockSpec(blk, lambda r, i: (i, r)),
                  pl.BlockSpec(blk, lambda r, i: (nxt(i), r)),
                  pl.BlockSpec(cblk, lambda r, i: (i, r)),
                  pl.BlockSpec(cblk, lambda r, i: (nxt(i), r)),
                  pl.BlockSpec(cblk, lambda r, i: (i, r)),
                  pl.BlockSpec(cblk, lambda r, i: (nxt(i), r))],
        out_specs=(pl.BlockSpec(blk, lambda r, i: (i, r)),) * 3,
        out_shape=(jax.ShapeDtypeStruct((sub, d * hd), BF16),) * 3,
        compiler_params=_params(2))(qv, qv, kvv, kvv, kvv, kvv, dov, dov, dlv, dlv, lv, lv)
    return tuple(_from_dilated(f"{n}_natural{g}", t, d) for n, t in (("dq", dq), ("dk", dk), ("dv", dv)))


def _kv_grad_sum(dks, dvs):
    t_len, hd = dks[0].shape
    tm = _tile(t_len, (256, 128))

    def body(k0, k1, k2, v0, v1, v2, o_ref):
        o_ref[:, :hd] = (k0[...].astype(F32) + k1[...].astype(F32) + k2[...].astype(F32)).astype(o_ref.dtype)
        o_ref[:, hd:] = (v0[...].astype(F32) + v1[...].astype(F32) + v2[...].astype(F32)).astype(o_ref.dtype)

    row = pl.BlockSpec((tm, hd), lambda i: (i, 0))
    return pl.pallas_call(
        body, name="kv_grad_sum", grid=(t_len // tm,), in_specs=[row] * 6,
        out_specs=pl.BlockSpec((tm, 2 * hd), lambda i: (i, 0)),
        out_shape=jax.ShapeDtypeStruct((t_len, 2 * hd), BF16), compiler_params=_params(1))(*dks, *dvs)


HALO = 16
INV_SQRT2 = 1.0 / math.sqrt(2.0)
INV_SQRT2PI = 1.0 / math.sqrt(2.0 * math.pi)


def _conv_taps(g, halo, cw, cb):
    row = lax.broadcasted_iota(jnp.int32, g.shape, 0)
    h1 = halo[HALO - 1:HALO, :]
    h2 = halo[HALO - 2:HALO - 1, :]
    g1 = jnp.where(row == 0, h1, pltpu.roll(g, 1, 0))
    g2 = jnp.where(row == 0, h2, jnp.where(row == 1, h1, pltpu.roll(g, 2, 0)))
    gc = cw[0:1, :] * g2 + cw[1:2, :] * g1 + cw[2:3, :] * g + cb
    return gc, g1, g2


def _glu_specs(t_len, f, tm, tc):
    nj = f // tc
    hb = tm // HALO
    u = pl.BlockSpec((tm, tc), lambda j, i: (i, j))
    g = pl.BlockSpec((tm, tc), lambda j, i: (i, nj + j))
    gh = pl.BlockSpec((HALO, tc), lambda j, i: (jnp.maximum(i * hb - 1, 0), nj + j))
    cw = pl.BlockSpec((8, tc), lambda j, i: (0, j))
    cb = pl.BlockSpec((1, tc), lambda j, i: (0, j))
    return u, g, gh, cw, cb


def _glu_fwd(name, up, cw, cb):
    t_len = up.shape[0]
    f = up.shape[1] // 2
    tm = _tile(t_len, (512, 256, 128))
    tc = _tile(f, (1408, 1024, 512, 256, 128))
    u_s, g_s, gh_s, cw_s, cb_s = _glu_specs(t_len, f, tm, tc)

    def body(u_ref, g_ref, gh_ref, cw_ref, cb_ref, o_ref):
        first = pl.program_id(1) == 0
        halo = jnp.where(first, 0.0, gh_ref[...].astype(F32))
        gc, _, _ = _conv_taps(g_ref[...].astype(F32), halo, cw_ref[...], cb_ref[...])
        gel = 0.5 * gc * (1.0 + lax.erf(gc * INV_SQRT2))
        o_ref[...] = (gel * u_ref[...].astype(F32)).astype(o_ref.dtype)

    return pl.pallas_call(
        body, name=name, grid=(f // tc, t_len // tm), in_specs=[u_s, g_s, gh_s, cw_s, cb_s],
        out_specs=pl.BlockSpec((tm, tc), lambda j, i: (i, j)),
        out_shape=jax.ShapeDtypeStruct((t_len, f), BF16), compiler_params=_params(2))(up, up, up, cw, cb)


def _glu_bwd_a(name, dact, up, cw, cb):
    t_len = up.shape[0]
    f = up.shape[1] // 2
    tm = _tile(t_len, (256, 128))
    tc = _tile(f, (1408, 1024, 512, 256, 128))
    u_s, g_s, gh_s, cw_s, cb_s = _glu_specs(t_len, f, tm, tc)

    def body(da_ref, u_ref, g_ref, gh_ref, cw_ref, cb_ref, du_ref, dgc_ref, w0_ref, w1_ref, w2_ref, b_ref):
        first = pl.program_id(1) == 0
        halo = jnp.where(first, 0.0, gh_ref[...].astype(F32))
        g = g_ref[...].astype(F32)
        gc, g1, g2 = _conv_taps(g, halo, cw_ref[...], cb_ref[...])
        phi = 0.5 * (1.0 + lax.erf(gc * INV_SQRT2))
        dgel = phi + gc * jnp.exp(-0.5 * gc * gc) * INV_SQRT2PI
        da = da_ref[...].astype(F32)
        du_ref[...] = (da * gc * phi).astype(du_ref.dtype)
        dgc = da * u_ref[...].astype(F32) * dgel
        dgc_ref[...] = dgc.astype(dgc_ref.dtype)
        parts = (jnp.sum(dgc * g2, axis=0, keepdims=True), jnp.sum(dgc * g1, axis=0, keepdims=True),
                 jnp.sum(dgc * g, axis=0, keepdims=True), jnp.sum(dgc, axis=0, keepdims=True))
        refs = (w0_ref, w1_ref, w2_ref, b_ref)

        @pl.when(first)
        def _():
            for r, p in zip(refs, parts):
                r[...] = p

        @pl.when(jnp.logical_not(first))
        def _():
            for r, p in zip(refs, parts):
                r[...] += p

    tile = pl.BlockSpec((tm, tc), lambda j, i: (i, j))
    vec = pl.BlockSpec((1, tc), lambda j, i: (0, j))
    vshape = jax.ShapeDtypeStruct((1, f), F32)
    return pl.pallas_call(
        body, name=name, grid=(f // tc, t_len // tm), in_specs=[tile, u_s, g_s, gh_s, cw_s, cb_s],
        out_specs=(tile, tile, vec, vec, vec, vec),
        out_shape=(jax.ShapeDtypeStruct((t_len, f), BF16), jax.ShapeDtypeStruct((t_len, f), BF16),
                   vshape, vshape, vshape, vshape),
        compiler_params=_params(2))(dact, up, up, up, cw, cb)


def _glu_bwd_b(name, du, dgc, cw):
    t_len, f = du.shape
    tm = _tile(t_len, (128, 64))
    hb = tm // HALO
    n_i = t_len // tm
    last_hb = t_len // HALO - 1

    def body(du_ref, d_ref, dh_ref, cw_ref, o_ref):
        last = pl.program_id(0) == n_i - 1
        halo = jnp.where(last, 0.0, dh_ref[...].astype(F32))
        dd = d_ref[...].astype(F32)
        row = lax.broadcasted_iota(jnp.int32, dd.shape, 0)
        h0 = halo[0:1, :]
        h1 = halo[1:2, :]
        d1 = jnp.where(row == tm - 1, h0, pltpu.roll(dd, tm - 1, 0))
        d2 = jnp.where(row == tm - 1, h1, jnp.where(row == tm - 2, h0, pltpu.roll(dd, tm - 2, 0)))
        cwv = cw_ref[...]
        dg = cwv[2:3, :] * dd + cwv[1:2, :] * d1 + cwv[0:1, :] * d2
        o_ref[:, :f] = du_ref[...]
        o_ref[:, f:] = dg.astype(o_ref.dtype)

    row_s = pl.BlockSpec((tm, f), lambda i: (i, 0))
    return pl.pallas_call(
        body, name=name, grid=(n_i,),
        in_specs=[row_s, row_s, pl.BlockSpec((HALO, f), lambda i: (jnp.minimum((i + 1) * hb, last_hb), 0)),
                  pl.BlockSpec((8, f), lambda i: (0, 0))],
        out_specs=pl.BlockSpec((tm, 2 * f), lambda i: (i, 0)),
        out_shape=jax.ShapeDtypeStruct((t_len, 2 * f), BF16), compiler_params=_params(1))(du, dgc, dgc, cw)


def _adamw(name, w, g, m, v):
    rows, cols = w.shape
    gcols = g.shape[1]
    n_out = 3 if gcols == cols else 4
    tr = _tile(rows, (256, 128, 64, 32, 16, 8))
    c1 = 1.0 / (1.0 - ADAM_B1 ** ADAM_STEP)
    c2 = 1.0 / (1.0 - ADAM_B2 ** ADAM_STEP)

    def body(w_ref, g_ref, m_ref, v_ref, d_ref, nm_ref, nv_ref, *g_out):
        gv = g_ref[...][:, :cols]
        nm = ADAM_B1 * m_ref[...] + (1.0 - ADAM_B1) * gv
        nv = ADAM_B2 * v_ref[...] + (1.0 - ADAM_B2) * (gv * gv)
        nm_ref[...] = nm
        nv_ref[...] = nv
        d_ref[...] = -ADAM_LR * ((nm * c1) / (jnp.sqrt(nv * c2) + ADAM_EPS) + ADAM_WD * w_ref[...])
        for ref in g_out:
            ref[...] = gv

    blk = pl.BlockSpec((tr, cols), lambda i: (i, 0))
    gblk = pl.BlockSpec((tr, gcols), lambda i: (i, 0))
    shp = jax.ShapeDtypeStruct((rows, cols), F32)
    return pl.pallas_call(body, name=name, grid=(rows // tr,), in_specs=[blk, gblk, blk, blk], out_specs=(blk,) * n_out,
                          out_shape=(shp,) * n_out, compiler_params=_params(1))(w, g, m, v)


class _NoComm:
    def __init__(self):
        self.grads = {}

    def prefetch(self, group, ws, after):
        return ws

    def need(self, group, ws, after):
        return ws

    def reduce(self, group, grads, carry):
        self.grads.update(grads)
        return carry

    def tick(self, carry):
        return carry


def _local_step(x, target, ws, norms, small, hooks):
    lay = _layout()

    w_main, w_a = _unpack_gin(ws["gin"])
    hn0 = _rms_fwd("rms_attn0", x, norms["attn0"])
    proj = _mm_plain("gla_proj", hn0, w_main, NN, F32)
    a = _mm_plain("gla_proj_a", hn0, w_a, NN, BF16)
    ga, cum = _gla_gate_fwd(a, small["w_a2p"], small["b_a2"])
    o_gla, states = _gla_fwd(proj, cum)
    ws = hooks.prefetch("B", ws, o_gla)
    gated = _gla_out_fwd(o_gla, proj, small["head_norm"])
    ws = hooks.need("B", ws, gated)
    h1 = _mm_act_wr("gla_out", gated, ws["gout"], lay["gout"], add=x)

    def ffn_fwd(l, h, prefetch=None):
        nonlocal ws
        hn = _rms_fwd(f"rms_ffn{l}", h, norms[f"ffn{l}"])
        up = _mm_act_wc(f"ffn_up{l}", hn, ws[f"up{l}"], lay[f"up{l}"], BF16)
        act = _glu_fwd(f"glu_fwd{l}", up, small["conv_w"][l], small["conv_b"][l])
        if prefetch is not None:
            ws = hooks.prefetch(prefetch, ws, act)
        return hn, up, act, _mm_act_wr(f"ffn_down{l}", act, ws[f"down{l}"], lay[f"down{l}"], add=h)

    hnf0, up0, act0, h2 = ffn_fwd(0, h1, prefetch="C1")

    ws = hooks.need("C1", ws, h2)
    kvn = _rms_fwd("rms_kv", h2, norms["kv"])
    kv = _mm_act_wc("kv_proj", kvn, ws["wkv"], lay["wkv"], BF16)
    hn1 = _rms_fwd("rms_attn1", h2, norms["attn1"])
    q_all = _mm_act_wc("q_proj", hn1, ws["wq"], lay["wq"], BF16)
    views = [_att_views(q_all, kv, g) for g in range(3)]
    branch = [_att_fwd(views[g], g) for g in range(3)]
    ws = hooks.prefetch("C2", ws, branch[-1][1])
    o_att, o_att_b, lse = _att_merge([br[0] for br in branch], [br[1] for br in branch])
    h3 = _mm_act_wr("att_out", o_att_b, ws["dout"], lay["dout"], add=h2)
    ws = hooks.need("C2", ws, h3)
    hnf1, up1, act1, h4 = ffn_fwd(1, h3)

    dh4, d_final, loss = _loss_head(h4, norms["final"], target)

    sm = {"final": d_final}

    def ffn_bwd(l, dh, h, hn, up, act):
        big = {}
        dact = _mm_dact_wrT(f"ffn_down_dx{l}", dh, ws[f"down{l}"], lay[f"down{l}"])
        big[f"down{l}"] = _mm_grad_wr(f"ffn_down_dw{l}", act, dh, lay[f"down{l}"])
        du, dgc, w0, w1, w2, db = _glu_bwd_a(f"glu_bwd_a{l}", dact, up, small["conv_w"][l], small["conv_b"][l])
        sm[f"conv_w{l}"] = (w0, w1, w2)
        sm[f"conv_b{l}"] = db
        dup = hooks.tick(_glu_bwd_b(f"glu_bwd_b{l}", du, dgc, small["conv_w"][l]))
        dhn = _mm_dact_wcT(f"ffn_up_dx{l}", dup, ws[f"up{l}"], lay[f"up{l}"])
        big[f"up{l}"] = _mm_grad_wc(f"ffn_up_dw{l}", hn, dup, lay[f"up{l}"])
        dh_in, sm[f"ffn{l}"] = _rms_bwd(f"rms_ffn_bwd{l}", dhn, h, norms[f"ffn{l}"], dh)
        return hooks.reduce(f"ffn{l}", big, dh_in)

    dh3 = ffn_bwd(1, dh4, h3, hnf1, up1, act1)

    big = {}
    do_att = _mm_dact_wrT("att_out_dx", dh3, ws["dout"], lay["dout"])
    big["dout"] = _mm_grad_wr("att_out_dw", o_att_b, dh3, lay["dout"])
    delta = _att_delta(do_att, o_att)
    bw = [_att_bwd(views[g], delta, lse, do_att, g) for g in range(3)]
    dq_all = jnp.concatenate([t[0] for t in bw], axis=1)
    dhn1 = _mm_dact_wcT("q_proj_dx", dq_all, ws["wq"], lay["wq"])
    big["wq"] = _mm_grad_wc("q_proj_dw", hn1, dq_all, lay["wq"])
    dh2, sm["attn1"] = _rms_bwd("rms_attn1_bwd", dhn1, h2, norms["attn1"], dh3)
    dkv = hooks.tick(_kv_grad_sum([t[1] for t in bw], [t[2] for t in bw]))
    dkvn = _mm_dact_wcT("kv_proj_dx", dkv, ws["wkv"], lay["wkv"])
    big["wkv"] = _mm_grad_wc("kv_proj_dw", kvn, dkv, lay["wkv"])
    dh2, sm["kv"] = _rms_bwd("rms_kv_bwd", dkvn, h2, norms["kv"], dh2)
    dh2 = hooks.reduce("att", big, dh2)

    dh1 = ffn_bwd(0, dh2, h1, hnf0, up0, act0)

    big = {}
    dgated = _mm_dact_wrT("gla_out_dx", dh1, ws["gout"], lay["gout"])
    big["gout"] = _mm_grad_wr("gla_out_dw", gated, dh1, lay["gout"])
    do_gla, dr, sm["head_norm"] = _gla_out_bwd(dgated, o_gla, proj, small["head_norm"])
    dq, dk, dv, dcum = _gla_bwd(proj, cum, states, hooks.tick(do_gla))
    da, sm["w_a2p"], sm["b_a2"] = _gla_gate_bwd(dcum, ga, a, small["w_a2p"])
    dproj = jnp.concatenate([dq, dk, dv, dr], axis=1)
    dhn0 = _mm_plain("gla_proj_dx", dproj, w_main, NT, F32)
    dhn0 = _mm_plain("gla_proj_a_dx", da, w_a, NT, F32, add=dhn0)
    gin_main = _mm_plain("gla_proj_dw", hn0, dproj, TN, BF16)
    gin_a = _mm_plain("gla_proj_a_dw", hn0, da, TN, BF16)
    big["gin"] = _pack_gin_grad(gin_main, gin_a)
    grad_x, sm["attn0"] = _rms_bwd("rms_attn0_bwd", dhn0, x, norms["attn0"], dh1)
    return loss, grad_x, sm, big


def _pack_weights(chip, names, gla_w_in, gla_w_out, w_kv, dsa_w_q, dsa_w_out, ffn_w_up, ffn_w_down):
    gin = gla_w_in[0]
    gin = jnp.pad(gin, ((0, 0), (0, _roundup(gin.shape[1], LANE) - gin.shape[1])))
    shards = {"gin": gin, "gout": gla_w_out[0], "up0": ffn_w_up[0], "up1": ffn_w_up[1], "down0": ffn_w_down[0],
              "down1": ffn_w_down[1], "wq": dsa_w_q[0], "wkv": w_kv, "dout": dsa_w_out[0]}
    out = {}
    for name in names:
        w = shards[name]
        buf = jnp.zeros((N_CHIPS,) + w.shape, BF16)
        out[name] = lax.dynamic_update_slice(buf, w.astype(BF16)[None], (chip, 0, 0))
    return out


def _unpack_gin(w_gin):
    w = _layout()["gin"][1]
    d = w_gin.shape[1]
    wp = w_gin.shape[2]
    n_main = 2 * GLA_KEY_DIM + 2 * GLA_VAL_DIM
    tm = _tile(d, (256, 128, 64, 32, 16))

    def body(s_ref, main_ref, a_ref):
        full = jnp.concatenate([s_ref[s][:, :w] for s in range(N_CHIPS)], axis=1)
        main_ref[...] = full[:, :n_main]
        a_ref[...] = jnp.concatenate([full[:, n_main:], jnp.zeros((tm, A_PAD - GATE_RANK), full.dtype)], axis=1)

    return pl.pallas_call(
        body, name="unpack_gin", grid=(d // tm,), in_specs=[pl.BlockSpec((N_CHIPS, tm, wp), lambda i: (0, i, 0))],
        out_specs=(pl.BlockSpec((tm, n_main), lambda i: (i, 0)), pl.BlockSpec((tm, A_PAD), lambda i: (i, 0))),
        out_shape=(jax.ShapeDtypeStruct((d, n_main), w_gin.dtype), jax.ShapeDtypeStruct((d, A_PAD), w_gin.dtype)),
        compiler_params=_params(1))(w_gin)


def _pack_gin_grad(gin_main, gin_a):
    w = _layout()["gin"][1]
    wp = _roundup(w, LANE)
    d, n_main = gin_main.shape
    tm = _tile(d, (256, 128, 64, 32, 16))

    def body(main_ref, a_ref, o_ref):
        full = jnp.concatenate([main_ref[...], a_ref[:, :GATE_RANK]], axis=1)
        fill = jnp.zeros((tm, wp - w), full.dtype)
        for s in range(N_CHIPS):
            o_ref[s] = jnp.concatenate([full[:, s * w:(s + 1) * w], fill], axis=1)

    return pl.pallas_call(
        body, name="pack_gin_grad", grid=(d // tm,),
        in_specs=[pl.BlockSpec((tm, n_main), lambda i: (i, 0)), pl.BlockSpec((tm, A_PAD), lambda i: (i, 0))],
        out_specs=pl.BlockSpec((N_CHIPS, tm, wp), lambda i: (0, i, 0)),
        out_shape=jax.ShapeDtypeStruct((N_CHIPS, d, wp), gin_main.dtype), compiler_params=_params(1))(gin_main, gin_a)


def _small_params(attn_norm, ffn_norm, kv_norm, final_norm, conv_b, w_a2, b_a2, head_norm, conv_w):
    norms = {"attn0": attn_norm[0:1], "attn1": attn_norm[1:2], "ffn0": ffn_norm[0:1], "ffn1": ffn_norm[1:2],
             "kv": kv_norm[None, :], "final": final_norm[None, :]}
    small = {"w_a2p": jnp.pad(w_a2, ((0, A_PAD - GATE_RANK), (0, 0))), "b_a2": b_a2[None, :],
             "head_norm": head_norm[None, :], "conv_w": jnp.pad(conv_w, ((0, 0), (0, 8 - conv_w.shape[1]), (0, 0))),
             "conv_b": conv_b[:, None, :]}
    return norms, small


ANY = pl.BlockSpec(memory_space=pl.ANY)


def _place():
    return lax.axis_index("x"), lax.axis_index("y"), lax.axis_index("c")


def _other_chips(x, y):
    return [(1 - x, y), (x, 1 - y), (1 - x, 1 - y)]


def _rcopy(src, dst, ssem, rsem, dev):
    return pltpu.make_async_remote_copy(src_ref=src, dst_ref=dst, send_sem=ssem, recv_sem=rsem, device_id=dev,
                                        device_id_type=MESH)


def _pack_shard(name, w, layer, chip_arr, after):
    rows, cols = w.shape[-2:]
    tr = _tile(rows, (512, 352, 256, 128, 64, 32, 16))

    def body(p_ref, w_ref, after_ref, o_ref):
        o_ref[...] = w_ref[...].astype(o_ref.dtype)

    if w.ndim == 3:
        w_spec = pl.BlockSpec((None, tr, cols), lambda i, p: (layer, i, 0))
    else:
        w_spec = pl.BlockSpec((tr, cols), lambda i, p: (i, 0))
    return pl.pallas_call(
        body, name=name,
        grid_spec=pltpu.PrefetchScalarGridSpec(
            num_scalar_prefetch=1, grid=(rows // tr,), in_specs=[w_spec, ANY],
            out_specs=pl.BlockSpec((None, tr, cols), lambda i, p: (p[0], i, 0))),
        out_shape=jax.ShapeDtypeStruct((N_CHIPS, rows, cols), BF16), compiler_params=_params(1))(chip_arr, w, after)


def _swap_halves(name, arrs):
    n = len(arrs)

    def body(*refs):
        ins, outs = refs[:n], refs[n:2 * n]
        send, recv = refs[2 * n:]
        x, y, c = _place()
        cps = []
        for a in range(n):
            h = ins[a].shape[1] // 2
            cp = _rcopy(ins[a].at[:, pl.ds((1 - c) * h, h)], outs[a], send.at[a], recv.at[a], (x, y, 1 - c))
            cp.start()
            cps.append(cp)
        for cp in cps:
            cp.wait()

    return pl.pallas_call(
        body, name=name, in_specs=[ANY] * n, out_specs=[ANY] * n,
        out_shape=[jax.ShapeDtypeStruct((a.shape[0], a.shape[1] // 2, a.shape[2]), a.dtype) for a in arrs],
        scratch_shapes=[pltpu.SemaphoreType.DMA((n,)), pltpu.SemaphoreType.DMA((n,))])(*arrs)


SEM = pl.BlockSpec(memory_space=pltpu.SEMAPHORE)
EFFECT = pltpu.SideEffectType.DATAFLOW_SIDE_EFFECTING


def _shapes(arrs):
    return [jax.ShapeDtypeStruct(a.shape, a.dtype) for a in arrs]


def _gather_start(name, thru, arrs):
    n, nt = len(arrs), len(thru)

    def body(*refs):
        ins = refs[nt:nt + n]
        send, recv = refs[nt + n], refs[nt + n + 1]
        outs = refs[2 * nt + n + 2:]
        x, y, c = _place()
        me = 2 * x + y
        for a in range(n):
            h = ins[a].shape[1] // 2
            mine = pl.ds(c * h, h)
            for j, (px, py) in enumerate(_other_chips(x, y)):
                _rcopy(ins[a].at[me, mine], outs[a].at[me, mine], send.at[3 * a + j], recv.at[3 * a + j], (px, py, c)).start()

    res = pl.pallas_call(
        body, name=name, in_specs=[ANY] * (nt + n), out_specs=[SEM, SEM] + [ANY] * (nt + n),
        out_shape=[pltpu.SemaphoreType.DMA((3 * n,)), pltpu.SemaphoreType.DMA((3 * n,))] + _shapes(thru) + _shapes(arrs),
        input_output_aliases={i: 2 + i for i in range(nt + n)},
        compiler_params=pltpu.CompilerParams(has_side_effects=EFFECT))(*thru, *arrs)
    return res[0], res[1], res[2:2 + nt], res[2 + nt:]


def _gather_wait(name, send, recv, arrs, after):
    n = len(arrs)

    def body(*refs):
        ins = refs[:n]
        send_ref, recv_ref = refs[n], refs[n + 1]
        x, y, c = _place()
        me = 2 * x + y
        for a in range(n):
            h = ins[a].shape[1] // 2
            mine = pl.ds(c * h, h)
            for j, (px, py) in enumerate(_other_chips(x, y)):
                sent = ins[a].at[me, mine]
                landed = ins[a].at[2 * px + py, mine]
                cp = _rcopy(sent, landed, send_ref.at[3 * a + j], recv_ref.at[3 * a + j], (px, py, c))
                cp.wait_send()
                cp.wait_recv()

    return pl.pallas_call(
        body, name=name, in_specs=[ANY] * n + [SEM, SEM, ANY], out_specs=[ANY] * n, out_shape=_shapes(arrs),
        input_output_aliases={a: a for a in range(n)},
        compiler_params=pltpu.CompilerParams(has_side_effects=EFFECT))(*arrs, send, recv, after)


def _forward_halves(name, arrs):
    n = len(arrs)

    def body(*refs):
        ins, outs = refs[:n], refs[n:2 * n]
        send, recv = refs[2 * n:]
        x, y, c = _place()
        sib = (x, y, 1 - c)
        chips = _other_chips(x, y)
        cps = []
        for a in range(n):
            h = ins[a].shape[1] // 2
            mine = pl.ds(c * h, h)
            for j, (px, py) in enumerate(chips):
                cp = _rcopy(ins[a].at[2 * px + py, mine], outs[a].at[2 * px + py, mine], send.at[3 * a + j],
                            recv.at[3 * a + j], sib)
                cp.start()
                cps.append(cp)
        for a in range(n):
            h = ins[a].shape[1] // 2
            theirs = pl.ds((1 - c) * h, h)
            for j, (px, py) in enumerate(chips):
                got = outs[a].at[2 * px + py, theirs]
                _rcopy(got, got, send.at[3 * a + j], recv.at[3 * a + j], sib).wait_recv()
        for cp in cps:
            cp.wait_send()

    return pl.pallas_call(
        body, name=name, in_specs=[ANY] * n, out_specs=[ANY] * n, out_shape=_shapes(arrs),
        input_output_aliases={a: a for a in range(n)},
        scratch_shapes=[pltpu.SemaphoreType.DMA((3 * n,)), pltpu.SemaphoreType.DMA((3 * n,))])(*arrs)


def _forward_start(name, arrs):
    n = len(arrs)

    def body(*refs):
        ins = refs[:n]
        send, recv = refs[n], refs[n + 1]
        outs = refs[n + 2:]
        x, y, c = _place()
        for a in range(n):
            h = ins[a].shape[1] // 2
            mine = pl.ds(c * h, h)
            for j, (px, py) in enumerate(_other_chips(x, y)):
                _rcopy(ins[a].at[2 * px + py, mine], outs[a].at[2 * px + py, mine], send.at[3 * a + j], recv.at[3 * a + j],
                       (x, y, 1 - c)).start()

    res = pl.pallas_call(
        body, name=name, in_specs=[ANY] * n, out_specs=[SEM, SEM] + [ANY] * n,
        out_shape=[pltpu.SemaphoreType.DMA((3 * n,)), pltpu.SemaphoreType.DMA((3 * n,))] + _shapes(arrs),
        input_output_aliases={i: 2 + i for i in range(n)},
        compiler_params=pltpu.CompilerParams(has_side_effects=EFFECT))(*arrs)
    return res[0], res[1], res[2:]


def _forward_wait(name, send, recv, arrs, after):
    n = len(arrs)

    def body(*refs):
        ins = refs[:n]
        send_ref, recv_ref = refs[n], refs[n + 1]
        x, y, c = _place()
        for a in range(n):
            h = ins[a].shape[1] // 2
            for j, (px, py) in enumerate(_other_chips(x, y)):
                sent = ins[a].at[2 * px + py, pl.ds(c * h, h)]
                got = ins[a].at[2 * px + py, pl.ds((1 - c) * h, h)]
                cp = _rcopy(sent, got, send_ref.at[3 * a + j], recv_ref.at[3 * a + j], (x, y, 1 - c))
                cp.wait_send()
                cp.wait_recv()

    return pl.pallas_call(
        body, name=name, in_specs=[ANY] * n + [SEM, SEM, ANY], out_specs=[ANY] * n, out_shape=_shapes(arrs),
        input_output_aliases={a: a for a in range(n)},
        compiler_params=pltpu.CompilerParams(has_side_effects=EFFECT))(*arrs, send, recv, after)


def _scatter_start(name, thru, arrs):
    n, nt = len(arrs), len(thru)
    landing = [jnp.zeros_like(a) for a in arrs]

    def body(*refs):
        ins = refs[nt:nt + n]
        send, recv = refs[nt + 2 * n], refs[nt + 2 * n + 1]
        outs = refs[2 * nt + 3 * n + 2:]
        x, y, c = _place()
        me = 2 * x + y
        for a in range(n):
            for j, (px, py) in enumerate(_other_chips(x, y)):
                _rcopy(ins[a].at[2 * px + py], outs[a].at[me], send.at[3 * a + j], recv.at[3 * a + j], (px, py, c)).start()

    res = pl.pallas_call(
        body, name=name, in_specs=[ANY] * (nt + 2 * n), out_specs=[SEM, SEM] + [ANY] * (nt + 2 * n),
        out_shape=[pltpu.SemaphoreType.DMA((3 * n,)), pltpu.SemaphoreType.DMA((3 * n,))] + _shapes(thru) + _shapes(arrs)
        + _shapes(landing),
        input_output_aliases={i: 2 + i for i in range(nt + 2 * n)},
        compiler_params=pltpu.CompilerParams(has_side_effects=EFFECT))(*thru, *arrs, *landing)
    return res[0], res[1], res[2:2 + nt], res[2 + nt:2 + nt + n], res[2 + nt + n:]


def _scatter_wait(name, send, recv, arrs, landing, after):
    n = len(arrs)

    def body(*refs):
        ins, land = refs[:n], refs[n:2 * n]
        send_ref, recv_ref = refs[2 * n], refs[2 * n + 1]
        x, y, c = _place()
        for a in range(n):
            for j, (px, py) in enumerate(_other_chips(x, y)):
                cp = _rcopy(ins[a].at[2 * px + py], land[a].at[2 * px + py], send_ref.at[3 * a + j], recv_ref.at[3 * a + j],
                            (px, py, c))
                cp.wait_send()
                cp.wait_recv()

    res = pl.pallas_call(
        body, name=name, in_specs=[ANY] * (2 * n) + [SEM, SEM, ANY], out_specs=[ANY] * (2 * n),
        out_shape=_shapes(arrs) + _shapes(landing), input_output_aliases={i: i for i in range(2 * n)},
        compiler_params=pltpu.CompilerParams(has_side_effects=EFFECT))(*arrs, *landing, send, recv, after)
    return res[:n], res[n:]


def _swap_start(name, thru, arrs):
    n, nt = len(arrs), len(thru)
    landing = [lax.empty((a.shape[0], a.shape[1] // 2, a.shape[2]), a.dtype) for a in arrs]

    def body(*refs):
        ins = refs[nt:nt + n]
        send, recv = refs[nt + 2 * n], refs[nt + 2 * n + 1]
        outs = refs[2 * nt + 3 * n + 2:]
        x, y, c = _place()
        for a in range(n):
            h = ins[a].shape[1] // 2
            _rcopy(ins[a].at[:, pl.ds((1 - c) * h, h)], outs[a], send.at[a], recv.at[a], (x, y, 1 - c)).start()

    res = pl.pallas_call(
        body, name=name, in_specs=[ANY] * (nt + 2 * n), out_specs=[SEM, SEM] + [ANY] * (nt + 2 * n),
        out_shape=[pltpu.SemaphoreType.DMA((n,)), pltpu.SemaphoreType.DMA((n,))] + _shapes(thru) + _shapes(arrs)
        + _shapes(landing),
        input_output_aliases={i: 2 + i for i in range(nt + 2 * n)},
        compiler_params=pltpu.CompilerParams(has_side_effects=EFFECT))(*thru, *arrs, *landing)
    return res[0], res[1], res[2:2 + nt], res[2 + nt:2 + nt + n], res[2 + nt + n:]


def _swap_wait(name, send, recv, arrs, landing, after):
    n = len(arrs)

    def body(*refs):
        ins, land = refs[:n], refs[n:2 * n]
        send_ref, recv_ref = refs[2 * n], refs[2 * n + 1]
        x, y, c = _place()
        for a in range(n):
            h = ins[a].shape[1] // 2
            cp = _rcopy(ins[a].at[:, pl.ds((1 - c) * h, h)], land[a], send_ref.at[a], recv_ref.at[a], (x, y, 1 - c))
            cp.wait_send()
            cp.wait_recv()

    res = pl.pallas_call(
        body, name=name, in_specs=[ANY] * (2 * n) + [SEM, SEM, ANY], out_specs=[ANY] * (2 * n),
        out_shape=_shapes(arrs) + _shapes(landing), input_output_aliases={i: i for i in range(2 * n)},
        compiler_params=pltpu.CompilerParams(has_side_effects=EFFECT))(*arrs, *landing, send, recv, after)
    return res[:n], res[n:]


def _join_start(name, arrs):
    n = len(arrs)

    def body(*refs):
        ins = refs[:n]
        send, recv = refs[n], refs[n + 1]
        outs = refs[n + 2:]
        x, y, c = _place()
        for a in range(n):
            h = ins[a].shape[0] // 2
            mine = pl.ds(c * h, h)
            _rcopy(ins[a].at[mine], outs[a].at[mine], send.at[a], recv.at[a], (x, y, 1 - c)).start()

    res = pl.pallas_call(
        body, name=name, in_specs=[ANY] * n, out_specs=[SEM, SEM] + [ANY] * n,
        out_shape=[pltpu.SemaphoreType.DMA((n,)), pltpu.SemaphoreType.DMA((n,))] + _shapes(arrs),
        input_output_aliases={i: 2 + i for i in range(n)},
        compiler_params=pltpu.CompilerParams(has_side_effects=EFFECT))(*arrs)
    return res[0], res[1], res[2:]


def _join_wait(name, send, recv, arrs, after):
    n = len(arrs)

    def body(*refs):
        ins = refs[:n]
        send_ref, recv_ref = refs[n], refs[n + 1]
        x, y, c = _place()
        for a in range(n):
            h = ins[a].shape[0] // 2
            cp = _rcopy(ins[a].at[pl.ds(c * h, h)], ins[a].at[pl.ds((1 - c) * h, h)], send_ref.at[a], recv_ref.at[a],
                        (x, y, 1 - c))
            cp.wait_send()
            cp.wait_recv()

    return pl.pallas_call(
        body, name=name, in_specs=[ANY] * n + [SEM, SEM, ANY], out_specs=[ANY] * n, out_shape=_shapes(arrs),
        input_output_aliases={a: a for a in range(n)},
        compiler_params=pltpu.CompilerParams(has_side_effects=EFFECT))(*arrs, send, recv, after)


def _join_halves(name, arrs):
    n = len(arrs)

    def body(*refs):
        ins, outs = refs[:n], refs[n:2 * n]
        send, recv = refs[2 * n:]
        x, y, c = _place()
        cps = []
        for a in range(n):
            h = ins[a].shape[0] // 2
            mine = pl.ds(c * h, h)
            cp = _rcopy(ins[a].at[mine], outs[a].at[mine], send.at[a], recv.at[a], (x, y, 1 - c))
            cp.start()
            cps.append(cp)
        for a in range(n):
            h = ins[a].shape[0] // 2
            got = outs[a].at[pl.ds((1 - c) * h, h)]
            _rcopy(got, got, send.at[a], recv.at[a], (x, y, 1 - c)).wait_recv()
        for cp in cps:
            cp.wait_send()

    return pl.pallas_call(
        body, name=name, in_specs=[ANY] * n, out_specs=[ANY] * n,
        out_shape=[jax.ShapeDtypeStruct(a.shape, a.dtype) for a in arrs],
        input_output_aliases={a: a for a in range(n)},
        scratch_shapes=[pltpu.SemaphoreType.DMA((n,)), pltpu.SemaphoreType.DMA((n,))])(*arrs)


def _allgather8(name, xs, reduce):
    m_per, n = xs.shape

    def body(x_ref, out_ref, *rest):
        if reduce:
            sum_ref, send, recv, lsem = rest
        else:
            send, recv, lsem = rest
        x, y, c = _place()
        me, sib = (x, y, c), (x, y, 1 - c)
        chips = _other_chips(x, y)

        def rows(px, py, pc):
            return out_ref.at[pl.ds((4 * px + 2 * py + pc) * m_per, m_per), :]

        def copy(k, block, to, src=None):
            return _rcopy(rows(*block) if src is None else src, rows(*block), send.at[k], recv.at[k], to)

        mine = pltpu.make_async_copy(x_ref, rows(*me), lsem)
        mine.start()
        first = [copy(0, me, sib, src=x_ref)]
        first += [copy(1 + j, me, (*chip, c), src=x_ref) for j, chip in enumerate(chips)]
        for cp in first:
            cp.start()
        passed = [copy(4 + j, (*chip, c), sib) for j, chip in enumerate(chips)]
        for j, chip in enumerate(chips):
            copy(1 + j, (*chip, c), me).wait_recv()
            passed[j].start()
        copy(0, sib, me).wait_recv()
        for j, chip in enumerate(chips):
            copy(4 + j, (*chip, 1 - c), me).wait_recv()
        for cp in first + passed:
            cp.wait_send()
        mine.wait()
        if reduce:
            acc = out_ref[pl.ds(0, m_per), :]
            for dev in range(1, N_DEV):
                acc = acc + out_ref[pl.ds(dev * m_per, m_per), :]
            sum_ref[...] = acc

    vm = pl.BlockSpec(memory_space=pltpu.VMEM)
    out_shape = [jax.ShapeDtypeStruct((N_DEV * m_per, n), xs.dtype)]
    if reduce:
        out_shape.append(jax.ShapeDtypeStruct((m_per, n), xs.dtype))
    return pl.pallas_call(
        body, name=name, in_specs=[vm], out_specs=[vm] * len(out_shape), out_shape=out_shape,
        scratch_shapes=[pltpu.SemaphoreType.DMA((7,)), pltpu.SemaphoreType.DMA((7,)), pltpu.SemaphoreType.DMA],
        compiler_params=pltpu.CompilerParams(vmem_limit_bytes=VMEM_LIMIT))(xs)


def _add_my_half(name, a, rb, c_arr):
    s, h, cols = rb.shape
    tr = _tile(h, (512, 352, 256, 128, 64, 32, 16))
    nt = h // tr

    def body(c_ref, a_ref, b_ref, o_ref):
        o_ref[...] = (a_ref[...].astype(F32) + b_ref[...].astype(F32)).astype(o_ref.dtype)

    return pl.pallas_call(
        body, name=name,
        grid_spec=pltpu.PrefetchScalarGridSpec(
            num_scalar_prefetch=1, grid=(s, nt),
            in_specs=[pl.BlockSpec((None, tr, cols), lambda k, i, c: (k, c[0] * nt + i, 0)),
                      pl.BlockSpec((None, tr, cols), lambda k, i, c: (k, i, 0))],
            out_specs=pl.BlockSpec((None, tr, cols), lambda k, i, c: (k, i, 0))),
        out_shape=jax.ShapeDtypeStruct(rb.shape, BF16), compiler_params=_params(2))(c_arr, a, rb)


def _sum_chips(name, own, q, place):
    s, h, cols = q.shape
    tr = _tile(h, (512, 352, 256, 128, 64, 32, 16))
    nt = h // tr

    def body(p_ref, own_ref, q_ref, o_ref):
        chip = p_ref[0]
        acc = jnp.where(chip == 0, own_ref[0], q_ref[0]).astype(F32)
        for j in range(1, s):
            acc = acc + jnp.where(chip == j, own_ref[j], q_ref[j]).astype(F32)
        o_ref[...] = acc

    blk = pl.BlockSpec((s, tr, cols), lambda i, p: (0, i, 0))
    return pl.pallas_call(
        body, name=name,
        grid_spec=pltpu.PrefetchScalarGridSpec(
            num_scalar_prefetch=1, grid=(nt,), in_specs=[blk, blk],
            out_specs=pl.BlockSpec((tr, cols), lambda i, p: (p[1] * nt + i, 0))),
        out_shape=jax.ShapeDtypeStruct((2 * h, cols), F32), compiler_params=_params(1))(place, own, q)


def _pack_rows(parts):
    rows = []
    for p in parts:
        flat = p.reshape(-1).astype(F32)
        n = _roundup(flat.shape[0], 8 * LANE)
        rows.append(jnp.pad(flat, (0, n - flat.shape[0])).reshape(-1, LANE))
    return jnp.concatenate(rows, axis=0)


def _unpack_rows(buf, shapes):
    out, r = [], 0
    for shp in shapes:
        size = math.prod(shp)
        nr = _roundup(size, 8 * LANE) // LANE
        out.append(buf[r:r + nr].reshape(-1)[:size].reshape(shp))
        r += nr
    return out


def kernel(x, attn_norm, gla_w_in, gla_w_a2, gla_b_a2, gla_head_norm, gla_w_out, kv_norm, w_kv, dsa_w_q, dsa_w_out, ffn_norm, ffn_w_up, ffn_conv_w, ffn_conv_b, ffn_w_down, final_norm, loss_target, m_attn_norm, m_gla_w_in, m_gla_w_a2, m_gla_b_a2, m_gla_head_norm, m_gla_w_out, m_kv_norm, m_w_kv, m_dsa_w_q, m_dsa_w_out, m_ffn_norm, m_ffn_w_up, m_ffn_conv_w, m_ffn_conv_b, m_ffn_w_down, m_final_norm, v_attn_norm, v_gla_w_in, v_gla_w_a2, v_gla_b_a2, v_gla_head_norm, v_gla_w_out, v_kv_norm, v_w_kv, v_dsa_w_q, v_dsa_w_out, v_ffn_norm, v_ffn_w_up, v_ffn_conv_w, v_ffn_conv_b, v_ffn_w_down, v_final_norm):
    lay = _layout()
    d, f = D_MODEL, D_FF
    cx, cy, cc = _place()
    chip = 2 * cx + cy
    c_arr = jnp.reshape(cc, (1,)).astype(jnp.int32)
    place = jnp.stack([chip, cc]).astype(jnp.int32)

    groups = {"A": ("gin", "small"), "B": ("gout", "up0", "down0"), "C1": ("wkv", "wq", "dout"), "C2": ("up1", "down1")}
    big_shards = (gla_w_in, gla_w_out, w_kv, dsa_w_q, dsa_w_out, ffn_w_up, ffn_w_down)
    ws = _pack_weights(chip, groups["A"][:1], *big_shards)
    sharded_small = [gla_w_a2[0], gla_b_a2[0], gla_head_norm[0], ffn_conv_w]
    packed = _pack_rows(sharded_small)
    packed = jnp.pad(packed, ((0, _roundup(packed.shape[0], 16) - packed.shape[0]), (0, 0)))
    ws["small"] = lax.dynamic_update_slice(jnp.zeros((N_CHIPS,) + packed.shape, F32), packed[None], (chip, 0, 0))
    send, recv, _, arrs = _gather_start("gather_a_start", [], [ws[k] for k in groups["A"]])
    chip_arr = place[:1]
    sources = {"up0": (ffn_w_up, 0), "up1": (ffn_w_up, 1), "down0": (ffn_w_down, 0), "down1": (ffn_w_down, 1),
               "wq": (dsa_w_q, 0), "wkv": (w_kv, 0), "dout": (dsa_w_out, 0), "gout": (gla_w_out, 0)}
    for k in groups["B"] + groups["C1"] + groups["C2"]:
        ws[k] = _pack_shard(f"pack_{k}", *sources[k], chip_arr, arrs[1])
    arrs = _gather_wait("gather_a_wait", send, recv, arrs, ws["dout"])
    ws.update(zip(groups["A"], _forward_halves("forward_a", arrs)))
    in_flight = {}
    thru = [ws[k] for k in groups["A"]]
    for grp in ("B", "C1", "C2"):
        send, recv, thru, arrs = _gather_start(f"gather_{grp.lower()}_start", thru, [ws[k] for k in groups[grp]])
        ws.update(zip(groups[grp], arrs))
        in_flight[grp] = (send, recv)
    ws.update(zip(groups["A"], thru))
    pending = []

    class _Comm:
        def prefetch(self, grp, ws, after):
            send, recv = in_flight[grp]
            arrs = _gather_wait(f"gather_{grp.lower()}_wait", send, recv, [ws[k] for k in groups[grp]], after)
            send, recv, arrs = _forward_start(f"forward_{grp.lower()}_start", arrs)
            in_flight[grp] = (send, recv)
            return {**ws, **dict(zip(groups[grp], arrs))}

        def need(self, grp, ws, after):
            send, recv = in_flight[grp]
            arrs = _forward_wait(f"forward_{grp.lower()}_wait", send, recv, [ws[k] for k in groups[grp]], after)
            return {**ws, **dict(zip(groups[grp], arrs))}

        swapping = None

        def reduce(self, grp, grads, carry):
            names = list(grads)
            send, recv, thru, parts, theirs = _swap_start(f"swap_{grp}_start", [carry], [grads[k] for k in names])
            self.swapping = (grp, names, send, recv, parts, theirs)
            return thru[0]

        def tick(self, carry):
            if self.swapping is None:
                return carry
            grp, names, send, recv, parts, theirs = self.swapping
            self.swapping = None
            parts, theirs = _swap_wait(f"swap_{grp}_wait", send, recv, parts, theirs, carry)
            return self.scatter(grp, names, parts, theirs, carry)

        def scatter(self, grp, names, parts, theirs, carry):
            sums = [_add_my_half(f"add_half_{k}", a, b, c_arr) for k, a, b in zip(names, parts, theirs)]
            send, recv, thru, sums, landing = _scatter_start(f"scatter_{grp}_start", [carry], sums)
            pending.append((grp, names, send, recv, sums, landing))
            return thru[0]

        def reduce_now(self, grp, grads, carry):
            names = list(grads)
            parts = [grads[k] for k in names]
            return self.scatter(grp, names, parts, _swap_halves(f"swap_{grp}", parts), carry)

    shards = [_unpack_rows(ws["small"][s], [p.shape for p in sharded_small]) for s in range(N_CHIPS)]
    w_a2, b_a2, head_norm, conv_w = [jnp.concatenate([shards[s][k] for s in range(N_CHIPS)], axis=-1) for k in range(4)]
    norms, small = _small_params(attn_norm, ffn_norm, kv_norm, final_norm, ffn_conv_b, w_a2, b_a2, head_norm, conv_w)

    comm = _Comm()
    loss_blk, grad_x, sm, last_big = _local_step(x[0], loss_target[0], ws, norms, small, comm)

    small_parts = [loss_blk, jnp.concatenate([sm["attn0"], sm["attn1"]]), jnp.concatenate([sm["ffn0"], sm["ffn1"]]),
                   sm["kv"], sm["final"], jnp.concatenate([sm["conv_b0"], sm["conv_b1"]]),
                   sm["w_a2p"][:GATE_RANK], sm["b_a2"], sm["head_norm"],
                   jnp.stack([jnp.concatenate(sm["conv_w0"]), jnp.concatenate(sm["conv_w1"])])]
    small_shapes = [(8, LANE), (2, d), (2, d), (d,), (d,), (2, f), (GATE_RANK, GLA_KEY_DIM), (GLA_KEY_DIM,),
                    (GLA_VAL_DIM // GLA_HEADS,), (2, 3, f)]
    _, reduced = _allgather8("reduce_small", _pack_rows(small_parts), True)
    reduced = comm.reduce_now("gla", last_big, reduced)

    loss_r, g_attn, g_ffn, g_kv, g_final, g_cb, g_a2, g_ba2, g_hn, g_cw = _unpack_rows(reduced, small_shapes)
    loss = loss_r[0, 0]

    def mine(g, axis):
        w = g.shape[axis] // N_CHIPS
        return lax.dynamic_slice_in_dim(g, chip * w, w, axis)

    grads = {
        "attn_norm": g_attn, "gla_w_a2": mine(g_a2, 1)[None], "gla_b_a2": mine(g_ba2, 0)[None],
        "gla_head_norm": mine(g_hn, 0)[None], "kv_norm": g_kv, "ffn_norm": g_ffn, "ffn_conv_w": mine(g_cw, 2),
        "ffn_conv_b": g_cb, "final_norm": g_final,
    }
    weights = {"attn_norm": (attn_norm, m_attn_norm, v_attn_norm), "gla_w_in": (gla_w_in, m_gla_w_in, v_gla_w_in),
               "gla_w_a2": (gla_w_a2, m_gla_w_a2, v_gla_w_a2), "gla_b_a2": (gla_b_a2, m_gla_b_a2, v_gla_b_a2),
               "gla_head_norm": (gla_head_norm, m_gla_head_norm, v_gla_head_norm),
               "gla_w_out": (gla_w_out, m_gla_w_out, v_gla_w_out), "kv_norm": (kv_norm, m_kv_norm, v_kv_norm),
               "w_kv": (w_kv, m_w_kv, v_w_kv), "dsa_w_q": (dsa_w_q, m_dsa_w_q, v_dsa_w_q),
               "dsa_w_out": (dsa_w_out, m_dsa_w_out, v_dsa_w_out), "ffn_norm": (ffn_norm, m_ffn_norm, v_ffn_norm),
               "ffn_w_up": (ffn_w_up, m_ffn_w_up, v_ffn_w_up), "ffn_conv_w": (ffn_conv_w, m_ffn_conv_w, v_ffn_conv_w),
               "ffn_conv_b": (ffn_conv_b, m_ffn_conv_b, v_ffn_conv_b),
               "ffn_w_down": (ffn_w_down, m_ffn_w_down, v_ffn_w_down), "final_norm": (final_norm, m_final_norm, v_final_norm)}
    order = list(weights)
    big_names = ("gla_w_in", "gla_w_out", "w_kv", "dsa_w_q", "dsa_w_out", "ffn_w_up", "ffn_w_down")
    delta, new_m, new_v = {}, {}, {}

    def adam_big(k, g):
        w, m, v = weights[k]
        cols = w.shape[-1]
        res = _adamw(f"adamw_{k}", w.reshape(-1, cols), g.reshape(-1, g.shape[-1]), m.reshape(-1, cols), v.reshape(-1, cols))
        delta[k], new_m[k], new_v[k] = [r.reshape(w.shape) for r in res[:3]]
        grads[k] = res[3].reshape(w.shape) if len(res) == 4 else g
        return res[0]

    full = {}
    after = reduced
    joining = []
    for grp, names, send, recv, sums, landing in pending[:-1]:
        sums, landing = _scatter_wait(f"scatter_{grp}_wait", send, recv, sums, landing, after)
        halves = [_sum_chips(f"sum_chips_{k}", s, q, place) for k, s, q in zip(names, sums, landing)]
        send, recv, halves = _join_start(f"join_{grp}_start", halves)
        joining.append((grp, names, send, recv, halves))
        after = halves[0]
    for grp, names, send, recv, halves in joining:
        joined = _join_wait(f"join_{grp}_wait", send, recv, halves, after)
        full.update(zip(names, joined))
        after = joined[0]
    after = adam_big("w_kv", full["wkv"])
    after = adam_big("dsa_w_q", full["wq"][None])
    after = adam_big("dsa_w_out", full["dout"][None])
    after = adam_big("ffn_w_up", jnp.stack([full["up0"], full["up1"]]))
    after = adam_big("ffn_w_down", jnp.stack([full["down0"], full["down1"]]))
    grp, names, send, recv, sums, landing = pending[-1]
    sums, landing = _scatter_wait(f"scatter_{grp}_wait", send, recv, sums, landing, after)
    halves = [_sum_chips(f"sum_chips_{k}", s, q, place) for k, s, q in zip(names, sums, landing)]
    full.update(zip(names, _join_halves(f"join_{grp}", halves)))
    adam_big("gla_w_in", full["gin"])
    adam_big("gla_w_out", full["gout"][None])
    small_names = [k for k in order if k not in big_names]
    packed = [_pack_rows([src[k] for k in small_names])
              for src in ({k: weights[k][0] for k in small_names}, grads, {k: weights[k][1] for k in small_names},
                          {k: weights[k][2] for k in small_names})]
    res = _adamw("adamw_small", *packed)
    shapes = [weights[k][0].shape for k in small_names]
    for dst, buf in zip((delta, new_m, new_v), res):
        for k, val in zip(small_names, _unpack_rows(buf, shapes)):
            dst[k] = val
    return (loss, grad_x[None], *[grads[k] for k in order], *[delta[k] for k in order], *[new_m[k] for k in order],
            *[new_v[k] for k in order])
```

```python
import math

import jax
import jax.numpy as jnp
from jax import lax
from jax.experimental import pallas as pl
from jax.experimental.pallas import tpu as pltpu

F32 = jnp.float32
BF16 = jnp.bfloat16

D_MODEL = 2048
SEQ = 4096
GLA_HEADS = 4
GLA_KEY_DIM = D_MODEL // 2
GLA_VAL_DIM = D_MODEL
GATE_RANK = 16
GATE_NORMALIZER = 16.0
GLA_CHUNK = 64
ATT_HEADS = 16
HEAD_DIM = 128
WINDOWS = (128, 512, 2048)
DILATIONS = (1, 4, 16)
ATT_BLOCK = 128
D_FF = 5632
EPS = 1e-6
ADAM_LR = 0.001
ADAM_B1 = 0.9
ADAM_B2 = 0.999
ADAM_EPS = 1e-08
ADAM_WD = 0.01
ADAM_STEP = 10

N_CHIPS = 4
N_DEV = 8
LANE = 128
A_PAD = 128
VMEM_LIMIT = 56 * 1024 * 1024
MAX_K_TILE = 2816
NEG = -1e30
MESH = pl.DeviceIdType.MESH

NN = (((1,), (0,)), ((), ()))
NT = (((1,), (1,)), ((), ()))
TN = (((0,), (0,)), ((), ()))


def _tile(n, cands):
    for c in cands:
        if c <= n and n % c == 0:
            return c
    return n


def _roundup(n, m):
    return -(-n // m) * m


def _params(n_axes):
    return pltpu.CompilerParams(dimension_semantics=("arbitrary",) * n_axes, vmem_limit_bytes=VMEM_LIMIT)


def _dot(a, b, dims):
    return lax.dot_general(a, b, dims, preferred_element_type=F32)


def _sigmoid(x):
    return 1.0 / (1.0 + jnp.exp(-x))


COL_SHARDED = ("gin", "up0", "up1", "wq", "wkv")
ROW_SHARDED = ("gout", "down0", "down1", "dout")


def _layout():
    f = D_FF
    hd = ATT_HEADS * HEAD_DIM
    gin = 2 * GLA_KEY_DIM + 2 * GLA_VAL_DIM + GATE_RANK
    up_w = 2 * f // N_CHIPS
    q_w = 3 * hd // N_CHIPS
    kv_w = 2 * hd // N_CHIPS
    dn_r = f // N_CHIPS
    go_r = GLA_VAL_DIM // N_CHIPS
    do_r = hd // N_CHIPS
    big = (1408, 1024, 512, 256, 128)
    return {
        "gin": (0, gin // N_CHIPS, LANE),
        "up0": (0, up_w, _tile(up_w, big)), "up1": (0, up_w, _tile(up_w, big)),
        "wq": (0, q_w, _tile(q_w, (512, 384, 256, 128))), "wkv": (0, kv_w, _tile(kv_w, (1024, 512, 256, 128))),
        "down0": (0, dn_r, _tile(dn_r, big)), "down1": (0, dn_r, _tile(dn_r, big)),
        "gout": (0, go_r, _tile(go_r, (512, 256, 128))), "dout": (0, do_r, _tile(do_r, (512, 256, 128))),
    }


def _matmul(name, a, b, dims, grid, a_spec, b_spec, o_spec, out_shape, acc_shape, add=None, add_spec=None):
    nk = grid[2]
    has_add = add is not None

    def body(*refs):
        a_ref, b_ref = refs[0], refs[1]
        pos = 2
        add_ref = None
        if has_add:
            add_ref = refs[pos]
            pos += 1
        o_ref = refs[pos]
        prod = _dot(a_ref[...].astype(BF16), b_ref[...].astype(BF16), dims)

        def finish(val):
            if has_add:
                val = val + add_ref[...].astype(F32)
            o_ref[...] = val.astype(o_ref.dtype)

        if nk == 1:
            finish(prod)
        else:
            acc_ref = refs[pos + 1]
            k = pl.program_id(2)

            @pl.when(k == 0)
            def _():
                acc_ref[...] = prod

            @pl.when(k > 0)
            def _():
                acc_ref[...] += prod

            @pl.when(k == nk - 1)
            def _():
                finish(acc_ref[...])

    in_specs = [a_spec, b_spec]
    args = [a, b]
    if has_add:
        in_specs.append(add_spec)
        args.append(add)
    scratch = [] if nk == 1 else [pltpu.VMEM(acc_shape, F32)]
    return pl.pallas_call(body, name=name, grid=grid, in_specs=in_specs, out_specs=o_spec, out_shape=out_shape,
                          scratch_shapes=scratch, compiler_params=_params(3))(*args)


def _mm_act_wc(name, a, wc, seg, out_dtype):
    off, w, tn = seg
    t_len, d = a.shape
    tm = _tile(t_len, (1024, 512, 256, 128))
    nps = w // tn
    ob = off // tn
    grid = (t_len // tm, N_CHIPS * nps, 1)
    return _matmul(
        name, a, wc, NN, grid,
        pl.BlockSpec((tm, d), lambda i, j, k: (i, 0)),
        pl.BlockSpec((None, d, tn), lambda i, j, k: (j // nps, 0, ob + j % nps)),
        pl.BlockSpec((tm, tn), lambda i, j, k: (i, j)),
        jax.ShapeDtypeStruct((t_len, N_CHIPS * w), out_dtype), (tm, tn))


def _mm_dact_wcT(name, dy, wc, seg, add=None):
    off, w, tk = seg
    if off == 0 and w <= MAX_K_TILE:
        tk = w
    t_len = dy.shape[0]
    d = wc.shape[1]
    tm = _tile(t_len, (1024, 512, 256, 128))
    tn = _tile(d, (1024, 512, 256, 128))
    kps = w // tk
    ob = off // tk
    grid = (t_len // tm, d // tn, N_CHIPS * kps)
    return _matmul(
        name, dy, wc, NT, grid,
        pl.BlockSpec((tm, tk), lambda i, j, k: (i, k)),
        pl.BlockSpec((None, tn, tk), lambda i, j, k: (k // kps, j, ob + k % kps)),
        pl.BlockSpec((tm, tn), lambda i, j, k: (i, j)),
        jax.ShapeDtypeStruct((t_len, d), F32), (tm, tn),
        add=add, add_spec=None if add is None else pl.BlockSpec((tm, tn), lambda i, j, k: (i, j)))


def _mm_grad_wc(name, a, dy, seg):
    _, w, tn = seg
    t_len, d = a.shape
    tm = _tile(d, (1024, 512, 256, 128))
    tk = _tile(t_len, (2048, 1024, 512, 256, 128))
    nps = w // tn
    grid = (d // tm, N_CHIPS * nps, t_len // tk)
    return _matmul(
        name, a, dy, TN, grid,
        pl.BlockSpec((tk, tm), lambda i, j, k: (k, i)),
        pl.BlockSpec((tk, tn), lambda i, j, k: (k, j)),
        pl.BlockSpec((None, tm, tn), lambda i, j, k: (j // nps, i, j % nps)),
        jax.ShapeDtypeStruct((N_CHIPS, d, w), BF16), (tm, tn))


def _is_plain(wr, seg):
    return seg[0] == 0 and wr.shape[1] == seg[1] and (N_CHIPS * seg[1]) % 1024 == 0


def _mm_act_wr(name, a, wr, seg, add):
    off, r, tk = seg
    t_len = a.shape[0]
    d = wr.shape[2]
    if seg[0] == 0 and wr.shape[1] == r:
        return _mm_plain(name, a, wr.reshape(N_CHIPS * r, d), NN, F32, add=add)
    tm = _tile(t_len, (1024, 512, 256, 128))
    tn = _tile(d, (1024, 512, 256, 128))
    kps = r // tk
    ob = off // tk
    grid = (t_len // tm, d // tn, N_CHIPS * kps)
    return _matmul(
        name, a, wr, NN, grid,
        pl.BlockSpec((tm, tk), lambda i, j, k: (i, k)),
        pl.BlockSpec((None, tk, tn), lambda i, j, k: (k // kps, ob + k % kps, j)),
        pl.BlockSpec((tm, tn), lambda i, j, k: (i, j)),
        jax.ShapeDtypeStruct((t_len, d), F32), (tm, tn),
        add=add, add_spec=pl.BlockSpec((tm, tn), lambda i, j, k: (i, j)))


def _mm_dact_wrT(name, dh, wr, seg):
    off, r, tn = seg
    t_len, d = dh.shape
    if _is_plain(wr, seg):
        return _mm_plain(name, dh, wr.reshape(N_CHIPS * r, d), NT, BF16)
    tm = _tile(t_len, (1024, 512, 256, 128))
    nps = r // tn
    ob = off // tn
    grid = (t_len // tm, N_CHIPS * nps, 1)
    return _matmul(
        name, dh, wr, NT, grid,
        pl.BlockSpec((tm, d), lambda i, j, k: (i, 0)),
        pl.BlockSpec((None, tn, d), lambda i, j, k: (j // nps, ob + j % nps, 0)),
        pl.BlockSpec((tm, tn), lambda i, j, k: (i, j)),
        jax.ShapeDtypeStruct((t_len, N_CHIPS * r), BF16), (tm, tn))


def _mm_grad_wr(name, a, dh, seg):
    _, r, tm = seg
    t_len, d = dh.shape
    if (N_CHIPS * r) % 1024 == 0:
        return _mm_plain(name, a, dh, TN, BF16).reshape(N_CHIPS, r, d)
    tn = _tile(d, (1024, 512, 256, 128))
    tk = _tile(t_len, (2048, 1024, 512, 256, 128))
    mps = r // tm
    grid = (N_CHIPS * mps, d // tn, t_len // tk)
    return _matmul(
        name, a, dh, TN, grid,
        pl.BlockSpec((tk, tm), lambda i, j, k: (k, i)),
        pl.BlockSpec((tk, tn), lambda i, j, k: (k, j)),
        pl.BlockSpec((None, tm, tn), lambda i, j, k: (i // mps, i % mps, j)),
        jax.ShapeDtypeStruct((N_CHIPS, r, d), BF16), (tm, tn))


def _mm_plain(name, a, b, dims, out_dtype, add=None):
    if dims == NN:
        m, kd = a.shape
        n = b.shape[1]
    elif dims == NT:
        m, kd = a.shape
        n = b.shape[0]
    else:
        kd, m = a.shape
        n = b.shape[1]
    tm = _tile(m, (1024, 512, 256, 128))
    tn = _tile(n, (1024, 768, 512, 256, 128))
    tk = _tile(kd, (MAX_K_TILE, 2048, 1408, 1024, 512, 256, 128))
    grid = (m // tm, n // tn, kd // tk)
    if dims == NN:
        a_spec = pl.BlockSpec((tm, tk), lambda i, j, k: (i, k))
        b_spec = pl.BlockSpec((tk, tn), lambda i, j, k: (k, j))
    elif dims == NT:
        a_spec = pl.BlockSpec((tm, tk), lambda i, j, k: (i, k))
        b_spec = pl.BlockSpec((tn, tk), lambda i, j, k: (j, k))
    else:
        a_spec = pl.BlockSpec((tk, tm), lambda i, j, k: (k, i))
        b_spec = pl.BlockSpec((tk, tn), lambda i, j, k: (k, j))
    o_spec = pl.BlockSpec((tm, tn), lambda i, j, k: (i, j))
    return _matmul(name, a, b, dims, grid, a_spec, b_spec, o_spec, jax.ShapeDtypeStruct((m, n), out_dtype), (tm, tn),
                   add=add, add_spec=None if add is None else o_spec)


def _rms_fwd(name, x, g):
    t_len, d = x.shape
    tm = _tile(t_len, (512, 256, 128))

    def body(x_ref, g_ref, o_ref):
        xv = x_ref[...]
        r = lax.rsqrt(jnp.mean(xv * xv, axis=-1, keepdims=True) + EPS)
        o_ref[...] = (xv * r * g_ref[...]).astype(o_ref.dtype)

    return pl.pallas_call(
        body, name=name, grid=(t_len // tm,),
        in_specs=[pl.BlockSpec((tm, d), lambda i: (i, 0)), pl.BlockSpec((1, d), lambda i: (0, 0))],
        out_specs=pl.BlockSpec((tm, d), lambda i: (i, 0)),
        out_shape=jax.ShapeDtypeStruct((t_len, d), BF16), compiler_params=_params(1))(x, g)


def _rms_bwd(name, dy, x, g, dres):
    t_len, d = x.shape
    tm = _tile(t_len, (256, 128))

    def body(dy_ref, x_ref, g_ref, dres_ref, dx_ref, dg_ref):
        xv = x_ref[...]
        r = lax.rsqrt(jnp.mean(xv * xv, axis=-1, keepdims=True) + EPS)
        xhat = xv * r
        dyv = dy_ref[...].astype(F32)
        dxn = dyv * g_ref[...]
        dx = r * (dxn - xhat * jnp.mean(dxn * xhat, axis=-1, keepdims=True))
        dx_ref[...] = dres_ref[...] + dx
        part = jnp.sum(dyv * xhat, axis=0, keepdims=True)

        @pl.when(pl.program_id(0) == 0)
        def _():
            dg_ref[...] = part

        @pl.when(pl.program_id(0) > 0)
        def _():
            dg_ref[...] += part

    row = pl.BlockSpec((tm, d), lambda i: (i, 0))
    vec = pl.BlockSpec((1, d), lambda i: (0, 0))
    return pl.pallas_call(
        body, name=name, grid=(t_len // tm,), in_specs=[row, row, vec, row], out_specs=(row, vec),
        out_shape=(jax.ShapeDtypeStruct((t_len, d), F32), jax.ShapeDtypeStruct((1, d), F32)),
        compiler_params=_params(1))(dy, x, g, dres)


def _loss_head(h, g, target):
    t_len, d = h.shape
    tm = _tile(t_len, (256, 128))

    def body(h_ref, g_ref, t_ref, dh_ref, dg_ref, loss_ref):
        xv = h_ref[...]
        gv = g_ref[...]
        r = lax.rsqrt(jnp.mean(xv * xv, axis=-1, keepdims=True) + EPS)
        xhat = xv * r
        err = xhat * gv - t_ref[...]
        dyv = err * (1.0 / d)
        dxn = dyv * gv
        dh_ref[...] = r * (dxn - xhat * jnp.mean(dxn * xhat, axis=-1, keepdims=True))
        part = jnp.sum(dyv * xhat, axis=0, keepdims=True)
        lpart = jnp.zeros((8, LANE), F32) + (0.5 / d) * jnp.sum(err * err)

        @pl.when(pl.program_id(0) == 0)
        def _():
            dg_ref[...] = part
            loss_ref[...] = lpart

        @pl.when(pl.program_id(0) > 0)
        def _():
            dg_ref[...] += part
            loss_ref[...] += lpart

    row = pl.BlockSpec((tm, d), lambda i: (i, 0))
    vec = pl.BlockSpec((1, d), lambda i: (0, 0))
    return pl.pallas_call(
        body, name="loss_head", grid=(t_len // tm,), in_specs=[row, vec, row],
        out_specs=(row, vec, pl.BlockSpec((8, LANE), lambda i: (0, 0))),
        out_shape=(jax.ShapeDtypeStruct((t_len, d), F32), jax.ShapeDtypeStruct((1, d), F32),
                   jax.ShapeDtypeStruct((8, LANE), F32)),
        compiler_params=_params(1))(h, g, target)


def _chunk_row(shape):
    return lax.broadcasted_iota(jnp.int32, shape, 0) % GLA_CHUNK


def _gla_gate_fwd(a, w_a2p, b_a2):
    t_len = a.shape[0]
    kd = w_a2p.shape[1]
    tm = _tile(t_len, (256, 128, 64))

    def body(a_ref, w_ref, b_ref, ga_ref, cum_ref):
        ga = _dot(a_ref[...], w_ref[...].astype(BF16), NN) + b_ref[...]
        ga_ref[...] = ga
        la = (jnp.minimum(ga, 0.0) - jnp.log(1.0 + jnp.exp(-jnp.abs(ga)))) * (1.0 / GATE_NORMALIZER)
        row = _chunk_row(la.shape)
        s = 1
        while s < GLA_CHUNK:
            la = la + jnp.where(row >= s, pltpu.roll(la, s, 0), 0.0)
            s *= 2
        cum_ref[...] = la

    return pl.pallas_call(
        body, name="gla_gate_fwd", grid=(t_len // tm,),
        in_specs=[pl.BlockSpec((tm, A_PAD), lambda i: (i, 0)), pl.BlockSpec((A_PAD, kd), lambda i: (0, 0)),
                  pl.BlockSpec((1, kd), lambda i: (0, 0))],
        out_specs=(pl.BlockSpec((tm, kd), lambda i: (i, 0)), pl.BlockSpec((tm, kd), lambda i: (i, 0))),
        out_shape=(jax.ShapeDtypeStruct((t_len, kd), F32), jax.ShapeDtypeStruct((t_len, kd), F32)),
        compiler_params=_params(1))(a, w_a2p, b_a2)


def _gla_gate_bwd(dcum, ga, a, w_a2p):
    t_len, kd = dcum.shape
    tm = _tile(t_len, (256, 128, 64))

    def body(dc_ref, ga_ref, a_ref, w_ref, da_ref, dw_ref, db_ref):
        x = dc_ref[...]
        row = _chunk_row(x.shape)
        s = 1
        while s < GLA_CHUNK:
            x = x + jnp.where(row < GLA_CHUNK - s, pltpu.roll(x, tm - s, 0), 0.0)
            s *= 2
        dga = x * (1.0 / GATE_NORMALIZER) * _sigmoid(-ga_ref[...])
        dgab = dga.astype(BF16)
        da_ref[...] = _dot(dgab, w_ref[...].astype(BF16), NT).astype(da_ref.dtype)
        dw = _dot(a_ref[...], dgab, TN)
        db = jnp.sum(dga, axis=0, keepdims=True)

        @pl.when(pl.program_id(0) == 0)
        def _():
            dw_ref[...] = dw
            db_ref[...] = db

        @pl.when(pl.program_id(0) > 0)
        def _():
            dw_ref[...] += dw
            db_ref[...] += db

    wide = pl.BlockSpec((tm, kd), lambda i: (i, 0))
    return pl.pallas_call(
        body, name="gla_gate_bwd", grid=(t_len // tm,),
        in_specs=[wide, wide, pl.BlockSpec((tm, A_PAD), lambda i: (i, 0)), pl.BlockSpec((A_PAD, kd), lambda i: (0, 0))],
        out_specs=(pl.BlockSpec((tm, A_PAD), lambda i: (i, 0)), pl.BlockSpec((A_PAD, kd), lambda i: (0, 0)),
                   pl.BlockSpec((1, kd), lambda i: (0, 0))),
        out_shape=(jax.ShapeDtypeStruct((t_len, A_PAD), BF16), jax.ShapeDtypeStruct((A_PAD, kd), F32),
                   jax.ShapeDtypeStruct((1, kd), F32)),
        compiler_params=_params(1))(dcum, ga, a, w_a2p)


GLA_STEP_CHUNKS = 4


def _gla_dims():
    dk = GLA_KEY_DIM // GLA_HEADS
    dv = GLA_VAL_DIM // GLA_HEADS
    return dk, dv


def _gla_fwd(proj, cum):
    t_len = proj.shape[0]
    dk, dv = _gla_dims()
    nc = t_len // GLA_CHUNK
    c = GLA_CHUNK
    scale = dk ** -0.5
    v0 = 2 * GLA_KEY_DIM // dv

    per = _tile(nc, (GLA_STEP_CHUNKS, 2, 1))
    rows = per * c

    def body(q_ref, k_ref, v_ref, cum_ref, o_ref, st_ref, s_scr):
        @pl.when(pl.program_id(1) == 0)
        def _():
            s_scr[...] = jnp.zeros_like(s_scr)

        tri = lax.broadcasted_iota(jnp.int32, (c, c), 0) >= lax.broadcasted_iota(jnp.int32, (c, c), 1)
        for i in range(per):
            rs = slice(i * c, (i + 1) * c)
            cm = cum_ref[rs, :]
            last = cm[c - 1:c, :]
            q = q_ref[rs, :].astype(F32) * scale
            k = k_ref[rs, :].astype(F32)
            v = v_ref[rs, :].astype(BF16)
            qd = (q * jnp.exp(cm)).astype(BF16)
            ki = (k * jnp.exp(-cm)).astype(BF16)
            ke = (k * jnp.exp(last - cm)).astype(BF16)
            sc = jnp.where(tri, _dot(qd, ki, NT), 0.0)
            st = s_scr[...]
            st_ref[i] = st
            o_ref[rs, :] = _dot(sc.astype(BF16), v, NN) + _dot(qd, st.astype(BF16), NT)
            s_scr[...] = st * jnp.exp(last) + _dot(v, ke, TN)

    return pl.pallas_call(
        body, name="gla_fwd", grid=(GLA_HEADS, nc // per),
        in_specs=[pl.BlockSpec((rows, dk), lambda h, n: (n, h)),
                  pl.BlockSpec((rows, dk), lambda h, n: (n, GLA_HEADS + h)),
                  pl.BlockSpec((rows, dv), lambda h, n: (n, v0 + h)),
                  pl.BlockSpec((rows, dk), lambda h, n: (n, h))],
        out_specs=(pl.BlockSpec((rows, dv), lambda h, n: (n, h)),
                   pl.BlockSpec((None, per, dv, dk), lambda h, n: (h, n, 0, 0))),
        out_shape=(jax.ShapeDtypeStruct((t_len, GLA_VAL_DIM), F32),
                   jax.ShapeDtypeStruct((GLA_HEADS, nc, dv, dk), F32)),
        scratch_shapes=[pltpu.VMEM((dv, dk), F32)], compiler_params=_params(2))(proj, proj, proj, cum)


def _gla_bwd(proj, cum, states, do):
    t_len = proj.shape[0]
    dk, dv = _gla_dims()
    nc = t_len // GLA_CHUNK
    c = GLA_CHUNK
    scale = dk ** -0.5
    v0 = 2 * GLA_KEY_DIM // dv

    per = _tile(nc, (GLA_STEP_CHUNKS, 2, 1))
    rows = per * c

    def body(q_ref, k_ref, v_ref, cum_ref, st_ref, do_ref, dq_ref, dk_ref, dv_ref, dc_ref, ds_scr):
        @pl.when(pl.program_id(1) == 0)
        def _():
            ds_scr[...] = jnp.zeros_like(ds_scr)

        tri = lax.broadcasted_iota(jnp.int32, (c, c), 0) >= lax.broadcasted_iota(jnp.int32, (c, c), 1)
        row = lax.broadcasted_iota(jnp.int32, (c, dk), 0)
        for i in reversed(range(per)):
            rs = slice(i * c, (i + 1) * c)
            cm = cum_ref[rs, :]
            last = cm[c - 1:c, :]
            e_c = jnp.exp(cm)
            e_nc = jnp.exp(-cm)
            e_lc = jnp.exp(last - cm)
            e_l = jnp.exp(last)
            q = q_ref[rs, :].astype(F32) * scale
            k = k_ref[rs, :].astype(F32)
            v = v_ref[rs, :].astype(BF16)
            dov = do_ref[rs, :]
            qd32 = q * e_c
            ki32 = k * e_nc
            ke32 = k * e_lc
            qd = qd32.astype(BF16)
            ki = ki32.astype(BF16)
            ke = ke32.astype(BF16)
            st = st_ref[i]
            dst = ds_scr[...]
            dstb = dst.astype(BF16)
            am = jnp.where(tri, _dot(dov, v, NT), 0.0).astype(BF16)
            pm = jnp.where(tri, _dot(qd, ki, NT), 0.0).astype(BF16)
            dqd = _dot(am, ki, NN) + _dot(dov, st.astype(BF16), NN)
            dki = _dot(am, qd, TN)
            dvv = _dot(pm, dov, TN) + _dot(ke, dstb, NT)
            dke = _dot(v, dstb, NN)
            d_el = jnp.sum(dst * st, axis=0, keepdims=True)
            ds_scr[...] = dst * e_l + _dot(dov, qd, TN)
            dq_ref[rs, :] = (dqd * scale * e_c).astype(dq_ref.dtype)
            dk_ref[rs, :] = (dki * e_nc + dke * e_lc).astype(dk_ref.dtype)
            dv_ref[rs, :] = dvv.astype(dv_ref.dtype)
            dkeke = dke * ke32
            dcum = dqd * qd32 - dki * ki32 - dkeke
            dlast = jnp.sum(dkeke, axis=0, keepdims=True) + d_el * e_l
            dc_ref[rs, :] = jnp.where(row == c - 1, dcum + dlast, dcum)

    rev = nc // per - 1
    return pl.pallas_call(
        body, name="gla_bwd", grid=(GLA_HEADS, nc // per),
        in_specs=[pl.BlockSpec((rows, dk), lambda h, n: (rev - n, h)),
                  pl.BlockSpec((rows, dk), lambda h, n: (rev - n, GLA_HEADS + h)),
                  pl.BlockSpec((rows, dv), lambda h, n: (rev - n, v0 + h)),
                  pl.BlockSpec((rows, dk), lambda h, n: (rev - n, h)),
                  pl.BlockSpec((None, per, dv, dk), lambda h, n: (h, rev - n, 0, 0)),
                  pl.BlockSpec((rows, dv), lambda h, n: (rev - n, h))],
        out_specs=(pl.BlockSpec((rows, dk), lambda h, n: (rev - n, h)),
                   pl.BlockSpec((rows, dk), lambda h, n: (rev - n, h)),
                   pl.BlockSpec((rows, dv), lambda h, n: (rev - n, h)),
                   pl.BlockSpec((rows, dk), lambda h, n: (rev - n, h))),
        out_shape=(jax.ShapeDtypeStruct((t_len, GLA_KEY_DIM), BF16), jax.ShapeDtypeStruct((t_len, GLA_KEY_DIM), BF16),
                   jax.ShapeDtypeStruct((t_len, GLA_VAL_DIM), BF16), jax.ShapeDtypeStruct((t_len, GLA_KEY_DIM), F32)),
        scratch_shapes=[pltpu.VMEM((dv, dk), F32)], compiler_params=_params(2))(proj, proj, proj, cum, states, do)


def _gla_out_fwd(o, proj, gn):
    t_len = o.shape[0]
    _, dv = _gla_dims()
    tm = _tile(t_len, (512, 256, 128))
    r0 = (2 * GLA_KEY_DIM + GLA_VAL_DIM) // dv

    def body(o_ref, r_ref, g_ref, y_ref):
        ov = o_ref[...]
        rs = lax.rsqrt(jnp.mean(ov * ov, axis=-1, keepdims=True) + EPS)
        rv = r_ref[...].astype(F32)
        y_ref[...] = (ov * rs * g_ref[...] * (rv * _sigmoid(rv))).astype(y_ref.dtype)

    return pl.pallas_call(
        body, name="gla_out_fwd", grid=(t_len // tm, GLA_HEADS),
        in_specs=[pl.BlockSpec((tm, dv), lambda i, h: (i, h)), pl.BlockSpec((tm, dv), lambda i, h: (i, r0 + h)),
                  pl.BlockSpec((1, dv), lambda i, h: (0, 0))],
        out_specs=pl.BlockSpec((tm, dv), lambda i, h: (i, h)),
        out_shape=jax.ShapeDtypeStruct((t_len, GLA_VAL_DIM), BF16), compiler_params=_params(2))(o, proj, gn)


def _gla_out_bwd(dy, o, proj, gn):
    t_len = o.shape[0]
    _, dv = _gla_dims()
    tm = _tile(t_len, (512, 256, 128))
    r0 = (2 * GLA_KEY_DIM + GLA_VAL_DIM) // dv

    def body(dy_ref, o_ref, r_ref, g_ref, do_ref, dr_ref, dg_ref):
        ov = o_ref[...]
        gv = g_ref[...]
        rs = lax.rsqrt(jnp.mean(ov * ov, axis=-1, keepdims=True) + EPS)
        xhat = ov * rs
        rv = r_ref[...].astype(F32)
        sg = _sigmoid(rv)
        gate = rv * sg
        dyv = dy_ref[...].astype(F32)
        dn = dyv * gate
        dr_ref[...] = (dyv * xhat * gv * (sg * (1.0 + rv * (1.0 - sg)))).astype(dr_ref.dtype)
        dxn = dn * gv
        do_ref[...] = (rs * (dxn - xhat * jnp.mean(dxn * xhat, axis=-1, keepdims=True))).astype(do_ref.dtype)
        part = jnp.sum(dn * xhat, axis=0, keepdims=True)
        first = (pl.program_id(0) == 0) & (pl.program_id(1) == 0)

        @pl.when(first)
        def _():
            dg_ref[...] = part

        @pl.when(jnp.logical_not(first))
        def _():
            dg_ref[...] += part

    blk = pl.BlockSpec((tm, dv), lambda i, h: (i, h))
    return pl.pallas_call(
        body, name="gla_out_bwd", grid=(t_len // tm, GLA_HEADS),
        in_specs=[blk, blk, pl.BlockSpec((tm, dv), lambda i, h: (i, r0 + h)), pl.BlockSpec((1, dv), lambda i, h: (0, 0))],
        out_specs=(blk, blk, pl.BlockSpec((1, dv), lambda i, h: (0, 0))),
        out_shape=(jax.ShapeDtypeStruct((t_len, GLA_VAL_DIM), BF16), jax.ShapeDtypeStruct((t_len, GLA_VAL_DIM), BF16),
                   jax.ShapeDtypeStruct((1, dv), F32)),
        compiler_params=_params(2))(dy, o, proj, gn)


def _alibi_slopes():
    n = ATT_HEADS
    start = 2.0 ** (-8.0 / n)
    return [start ** (i + 1) for i in range(n)]


def _att_masks(d):
    b = ATT_BLOCK
    qa = lax.broadcasted_iota(jnp.int32, (b, b), 0)
    kb = lax.broadcasted_iota(jnp.int32, (b, b), 1)
    dist_c = qa - kb
    dist_p = qa - kb + b
    return dist_c >= 0, dist_p <= b, (dist_c * d).astype(F32), (dist_p * d).astype(F32)


def _to_dilated(name, x, d, c0=0, w=None):
    part = x if w is None else x[:, c0:c0 + w]
    return part.reshape(x.shape[0] // d, -1)


def _from_dilated(name, y, d):
    return y.reshape(y.shape[0] * d, y.shape[1] // d)


def _att_views(q_all, kv, g):
    d = DILATIONS[g]
    hd = ATT_HEADS * HEAD_DIM
    if d == 1:
        return q_all, kv
    return _to_dilated(f"q_dilated{g}", q_all, d, g * hd, hd), _to_dilated(f"kv_dilated{g}", kv, d)


def _att_fwd(views, g):
    d = DILATIONS[g]
    assert WINDOWS[g] // d == ATT_BLOCK
    qv, kvv = views
    hd = ATT_HEADS * HEAD_DIM
    sub = kvv.shape[0]
    t_len = sub * d
    nb = sub // ATT_BLOCK
    b = ATT_BLOCK
    e = HEAD_DIM
    scale = e ** -0.5
    slopes = _alibi_slopes()
    qc = (lambda r: 3 * r + g) if d == 1 else (lambda r: r)

    def body(q_ref, kp_ref, kc_ref, vp_ref, vc_ref, o_ref, l_ref, s_scr, p_scr, li_scr):
        ib = pl.program_id(1)
        valid_c, valid_p0, dist_c, dist_p = _att_masks(d)
        valid_p = valid_p0 & (ib > 0)
        for h in range(ATT_HEADS):
            hs = slice(h * e, (h + 1) * e)
            qh = q_ref[:, hs]
            s_scr[h, 0] = _dot(qh, kc_ref[:, hs], NT)
            s_scr[h, 1] = _dot(qh, kp_ref[:, hs], NT)
        l_ref[...] = jnp.zeros_like(l_ref)
        for h in range(ATT_HEADS):
            s_c = jnp.where(valid_c, s_scr[h, 0] * scale - slopes[h] * dist_c, NEG)
            s_p = jnp.where(valid_p, s_scr[h, 1] * scale - slopes[h] * dist_p, NEG)
            m = jnp.maximum(jnp.max(s_c, axis=1, keepdims=True), jnp.max(s_p, axis=1, keepdims=True))
            p_c = jnp.where(valid_c, jnp.exp(s_c - m), 0.0)
            p_p = jnp.where(valid_p, jnp.exp(s_p - m), 0.0)
            l = jnp.sum(p_c, axis=1, keepdims=True) + jnp.sum(p_p, axis=1, keepdims=True)
            p_scr[h, 0] = p_c.astype(BF16)
            p_scr[h, 1] = p_p.astype(BF16)
            li_scr[:, h:h + 1] = 1.0 / l
            l_ref[:, h:h + 1] = m + jnp.log(l)
        for h in range(ATT_HEADS):
            hs = slice(h * e, (h + 1) * e)
            acc = _dot(p_scr[h, 0], vc_ref[:, hs], NN) + _dot(p_scr[h, 1], vp_ref[:, hs], NN)
            o_ref[:, hs] = acc * li_scr[:, h:h + 1]

    blk = (b, hd)
    cblk = (b, LANE)
    o, lse = pl.pallas_call(
        body, name=f"att_fwd{g}", grid=(d, nb),
        scratch_shapes=[pltpu.VMEM((ATT_HEADS, 2, b, b), F32), pltpu.VMEM((ATT_HEADS, 2, b, b), BF16),
                        pltpu.VMEM((b, LANE), F32)],
        in_specs=[pl.BlockSpec(blk, lambda r, i: (i, qc(r))),
                  pl.BlockSpec(blk, lambda r, i: (jnp.maximum(i - 1, 0), 2 * r)),
                  pl.BlockSpec(blk, lambda r, i: (i, 2 * r)),
                  pl.BlockSpec(blk, lambda r, i: (jnp.maximum(i - 1, 0), 2 * r + 1)),
                  pl.BlockSpec(blk, lambda r, i: (i, 2 * r + 1))],
        out_specs=(pl.BlockSpec(blk, lambda r, i: (i, r)), pl.BlockSpec(cblk, lambda r, i: (i, r))),
        out_shape=(jax.ShapeDtypeStruct((sub, d * hd), F32), jax.ShapeDtypeStruct((sub, d * LANE), F32)),
        compiler_params=_params(2))(qv, kvv, kvv, kvv, kvv)
    return _from_dilated(f"o_natural{g}", o, d), lse.reshape(t_len, LANE)


def _att_merge(os, ls):
    t_len, hd = os[0].shape
    tm = _tile(t_len, (256, 128))
    e = HEAD_DIM

    def body(o0, o1, o2, l0, l1, l2, of_ref, ob_ref, l_ref):
        a0, a1, a2 = l0[...], l1[...], l2[...]
        m = jnp.maximum(jnp.maximum(a0, a1), a2)
        e0, e1, e2 = jnp.exp(a0 - m), jnp.exp(a1 - m), jnp.exp(a2 - m)
        den = e0 + e1 + e2
        w0, w1, w2 = e0 / den, e1 / den, e2 / den
        l_ref[...] = m + jnp.log(den)
        for h in range(ATT_HEADS):
            hs = slice(h * e, (h + 1) * e)
            c = slice(h, h + 1)
            o = w0[:, c] * o0[:, hs] + w1[:, c] * o1[:, hs] + w2[:, c] * o2[:, hs]
            of_ref[:, hs] = o
            ob_ref[:, hs] = o.astype(ob_ref.dtype)

    row = pl.BlockSpec((tm, hd), lambda i: (i, 0))
    crow = pl.BlockSpec((tm, LANE), lambda i: (i, 0))
    return pl.pallas_call(
        body, name="att_merge", grid=(t_len // tm,), in_specs=[row] * 3 + [crow] * 3, out_specs=(row, row, crow),
        out_shape=(jax.ShapeDtypeStruct((t_len, hd), F32), jax.ShapeDtypeStruct((t_len, hd), BF16),
                   jax.ShapeDtypeStruct((t_len, LANE), F32)),
        compiler_params=_params(1))(*os, *ls)


def _att_delta(do, o):
    t_len, hd = o.shape
    tm = _tile(t_len, (256, 128))
    e = HEAD_DIM

    def body(do_ref, o_ref, d_ref):
        d_ref[...] = jnp.zeros_like(d_ref)
        for h in range(ATT_HEADS):
            hs = slice(h * e, (h + 1) * e)
            d_ref[:, h:h + 1] = jnp.sum(do_ref[:, hs].astype(F32) * o_ref[:, hs], axis=1, keepdims=True)

    row = pl.BlockSpec((tm, hd), lambda i: (i, 0))
    return pl.pallas_call(
        body, name="att_delta", grid=(t_len // tm,), in_specs=[row, row],
        out_specs=pl.BlockSpec((tm, LANE), lambda i: (i, 0)),
        out_shape=jax.ShapeDtypeStruct((t_len, LANE), F32), compiler_params=_params(1))(do, o)


def _att_bwd(views, delta, lse, do, g):
    d = DILATIONS[g]
    qv, kvv = views
    hd = ATT_HEADS * HEAD_DIM
    sub = kvv.shape[0]
    t_len = sub * d
    nb = sub // ATT_BLOCK
    b = ATT_BLOCK
    e = HEAD_DIM
    scale = e ** -0.5
    slopes = _alibi_slopes()
    qc = (lambda r: 3 * r + g) if d == 1 else (lambda r: r)
    dlv = delta.reshape(sub, d * LANE)
    lv = lse.reshape(sub, d * LANE)
    dov = do if d == 1 else _to_dilated(f"do_dilated{g}", do, d)

    def body(qj_ref, qn_ref, kp_ref, kc_ref, vp_ref, vc_ref, doj_ref, don_ref, dj_ref, dn_ref, lj_ref, ln_ref,
             dq_ref, dk_ref, dv_ref, s_scr, dp_scr, p_scr, ds_scr):
        j = pl.program_id(1)
        valid_c, valid_p0, dist_c, dist_p = _att_masks(d)
        valid = (valid_c, valid_p0 & (j > 0), valid_p0 & (j + 1 < nb))
        dist = (dist_c, dist_p, dist_p)
        for h in range(ATT_HEADS):
            hs = slice(h * e, (h + 1) * e)
            qj, qn = qj_ref[:, hs], qn_ref[:, hs]
            kc, kp = kc_ref[:, hs], kp_ref[:, hs]
            vc, vp = vc_ref[:, hs], vp_ref[:, hs]
            doj, don = doj_ref[:, hs], don_ref[:, hs]
            s_scr[h, 0] = _dot(qj, kc, NT)
            s_scr[h, 1] = _dot(qj, kp, NT)
            s_scr[h, 2] = _dot(qn, kc, NT)
            dp_scr[h, 0] = _dot(doj, vc, NT)
            dp_scr[h, 1] = _dot(doj, vp, NT)
            dp_scr[h, 2] = _dot(don, vc, NT)
        for h in range(ATT_HEADS):
            c = slice(h, h + 1)
            lse_t = (lj_ref[:, c], lj_ref[:, c], ln_ref[:, c])
            dlt_t = (dj_ref[:, c], dj_ref[:, c], dn_ref[:, c])
            for t in range(3):
                s = s_scr[h, t] * scale - slopes[h] * dist[t]
                p = jnp.where(valid[t], jnp.exp(jnp.where(valid[t], s - lse_t[t], NEG)), 0.0)
                p_scr[h, t] = p.astype(BF16)
                ds_scr[h, t] = (p * (dp_scr[h, t] - dlt_t[t])).astype(BF16)
        for h in range(ATT_HEADS):
            hs = slice(h * e, (h + 1) * e)
            dq = _dot(ds_scr[h, 0], kc_ref[:, hs], NN) + _dot(ds_scr[h, 1], kp_ref[:, hs], NN)
            dk = _dot(ds_scr[h, 0], qj_ref[:, hs], TN) + _dot(ds_scr[h, 2], qn_ref[:, hs], TN)
            dv = _dot(p_scr[h, 0], doj_ref[:, hs], TN) + _dot(p_scr[h, 2], don_ref[:, hs], TN)
            dq_ref[:, hs] = (dq * scale).astype(dq_ref.dtype)
            dk_ref[:, hs] = (dk * scale).astype(dk_ref.dtype)
            dv_ref[:, hs] = dv.astype(dv_ref.dtype)

    blk = (b, hd)
    cblk = (b, LANE)
    nxt = lambda i: jnp.minimum(i + 1, nb - 1)
    prv = lambda i: jnp.maximum(i - 1, 0)
    tiles = (ATT_HEADS, 3, b, b)
    dq, dk, dv = pl.pallas_call(
        body, name=f"att_bwd{g}", grid=(d, nb),
        scratch_shapes=[pltpu.VMEM(tiles, F32), pltpu.VMEM(tiles, F32), pltpu.VMEM(tiles, BF16), pltpu.VMEM(tiles, BF16)],
        in_specs=[pl.BlockSpec(blk, lambda r, i: (i, qc(r))),
                  pl.BlockSpec(blk, lambda r, i: (nxt(i), qc(r))),
                  pl.BlockSpec(blk, lambda r, i: (prv(i), 2 * r)),
                  pl.BlockSpec(blk, lambda r, i: (i, 2 * r)),
                  pl.BlockSpec(blk, lambda r, i: (prv(i), 2 * r + 1)),
                  pl.BlockSpec(blk, lambda r, i: (i, 2 * r + 1)),
                  pl.BlockSpec(blk, lambda r, i: (i, r)),
                  pl.BlockSpec(blk, lambda r, i: (nxt(i), r)),
                  pl.BlockSpec(cblk, lambda r, i: (i, r)),
                  pl.BlockSpec(cblk, lambda r, i: (nxt(i), r)),
                  pl.BlockSpec(cblk, lambda r, i: (i, r)),
                  pl.BlockSpec(cblk, lambda r, i: (nxt(i), r))],
        out_specs=(pl.BlockSpec(blk, lambda r, i: (i, r)),) * 3,
        out_shape=(jax.ShapeDtypeStruct((sub, d * hd), BF16),) * 3,
        compiler_params=_params(2))(qv, qv, kvv, kvv, kvv, kvv, dov, dov, dlv, dlv, lv, lv)
    return tuple(_from_dilated(f"{n}_natural{g}", t, d) for n, t in (("dq", dq), ("dk", dk), ("dv", dv)))


def _kv_grad_sum(dks, dvs):
    t_len, hd = dks[0].shape
    tm = _tile(t_len, (256, 128))

    def body(k0, k1, k2, v0, v1, v2, o_ref):
        o_ref[:, :hd] = (k0[...].astype(F32) + k1[...].astype(F32) + k2[...].astype(F32)).astype(o_ref.dtype)
        o_ref[:, hd:] = (v0[...].astype(F32) + v1[...].astype(F32) + v2[...].astype(F32)).astype(o_ref.dtype)

    row = pl.BlockSpec((tm, hd), lambda i: (i, 0))
    return pl.pallas_call(
        body, name="kv_grad_sum", grid=(t_len // tm,), in_specs=[row] * 6,
        out_specs=pl.BlockSpec((tm, 2 * hd), lambda i: (i, 0)),
        out_shape=jax.ShapeDtypeStruct((t_len, 2 * hd), BF16), compiler_params=_params(1))(*dks, *dvs)


HALO = 16
INV_SQRT2 = 1.0 / math.sqrt(2.0)
INV_SQRT2PI = 1.0 / math.sqrt(2.0 * math.pi)


def _conv_taps(g, halo, cw, cb):
    row = lax.broadcasted_iota(jnp.int32, g.shape, 0)
    h1 = halo[HALO - 1:HALO, :]
    h2 = halo[HALO - 2:HALO - 1, :]
    g1 = jnp.where(row == 0, h1, pltpu.roll(g, 1, 0))
    g2 = jnp.where(row == 0, h2, jnp.where(row == 1, h1, pltpu.roll(g, 2, 0)))
    gc = cw[0:1, :] * g2 + cw[1:2, :] * g1 + cw[2:3, :] * g + cb
    return gc, g1, g2


def _glu_specs(t_len, f, tm, tc):
    nj = f // tc
    hb = tm // HALO
    u = pl.BlockSpec((tm, tc), lambda j, i: (i, j))
    g = pl.BlockSpec((tm, tc), lambda j, i: (i, nj + j))
    gh = pl.BlockSpec((HALO, tc), lambda j, i: (jnp.maximum(i * hb - 1, 0), nj + j))
    cw = pl.BlockSpec((8, tc), lambda j, i: (0, j))
    cb = pl.BlockSpec((1, tc), lambda j, i: (0, j))
    return u, g, gh, cw, cb


def _glu_fwd(name, up, cw, cb):
    t_len = up.shape[0]
    f = up.shape[1] // 2
    tm = _tile(t_len, (512, 256, 128))
    tc = _tile(f, (1408, 1024, 512, 256, 128))
    u_s, g_s, gh_s, cw_s, cb_s = _glu_specs(t_len, f, tm, tc)

    def body(u_ref, g_ref, gh_ref, cw_ref, cb_ref, o_ref):
        first = pl.program_id(1) == 0
        halo = jnp.where(first, 0.0, gh_ref[...].astype(F32))
        gc, _, _ = _conv_taps(g_ref[...].astype(F32), halo, cw_ref[...], cb_ref[...])
        gel = 0.5 * gc * (1.0 + lax.erf(gc * INV_SQRT2))
        o_ref[...] = (gel * u_ref[...].astype(F32)).astype(o_ref.dtype)

    return pl.pallas_call(
        body, name=name, grid=(f // tc, t_len // tm), in_specs=[u_s, g_s, gh_s, cw_s, cb_s],
        out_specs=pl.BlockSpec((tm, tc), lambda j, i: (i, j)),
        out_shape=jax.ShapeDtypeStruct((t_len, f), BF16), compiler_params=_params(2))(up, up, up, cw, cb)


def _glu_bwd_a(name, dact, up, cw, cb):
    t_len = up.shape[0]
    f = up.shape[1] // 2
    tm = _tile(t_len, (256, 128))
    tc = _tile(f, (1408, 1024, 512, 256, 128))
    u_s, g_s, gh_s, cw_s, cb_s = _glu_specs(t_len, f, tm, tc)

    def body(da_ref, u_ref, g_ref, gh_ref, cw_ref, cb_ref, du_ref, dgc_ref, w0_ref, w1_ref, w2_ref, b_ref):
        first = pl.program_id(1) == 0
        halo = jnp.where(first, 0.0, gh_ref[...].astype(F32))
        g = g_ref[...].astype(F32)
        gc, g1, g2 = _conv_taps(g, halo, cw_ref[...], cb_ref[...])
        phi = 0.5 * (1.0 + lax.erf(gc * INV_SQRT2))
        dgel = phi + gc * jnp.exp(-0.5 * gc * gc) * INV_SQRT2PI
        da = da_ref[...].astype(F32)
        du_ref[...] = (da * gc * phi).astype(du_ref.dtype)
        dgc = da * u_ref[...].astype(F32) * dgel
        dgc_ref[...] = dgc.astype(dgc_ref.dtype)
        parts = (jnp.sum(dgc * g2, axis=0, keepdims=True), jnp.sum(dgc * g1, axis=0, keepdims=True),
                 jnp.sum(dgc * g, axis=0, keepdims=True), jnp.sum(dgc, axis=0, keepdims=True))
        refs = (w0_ref, w1_ref, w2_ref, b_ref)

        @pl.when(first)
        def _():
            for r, p in zip(refs, parts):
                r[...] = p

        @pl.when(jnp.logical_not(first))
        def _():
            for r, p in zip(refs, parts):
                r[...] += p

    tile = pl.BlockSpec((tm, tc), lambda j, i: (i, j))
    vec = pl.BlockSpec((1, tc), lambda j, i: (0, j))
    vshape = jax.ShapeDtypeStruct((1, f), F32)
    return pl.pallas_call(
        body, name=name, grid=(f // tc, t_len // tm), in_specs=[tile, u_s, g_s, gh_s, cw_s, cb_s],
        out_specs=(tile, tile, vec, vec, vec, vec),
        out_shape=(jax.ShapeDtypeStruct((t_len, f), BF16), jax.ShapeDtypeStruct((t_len, f), BF16),
                   vshape, vshape, vshape, vshape),
        compiler_params=_params(2))(dact, up, up, up, cw, cb)


def _glu_bwd_b(name, du, dgc, cw):
    t_len, f = du.shape
    tm = _tile(t_len, (128, 64))
    hb = tm // HALO
    n_i = t_len // tm
    last_hb = t_len // HALO - 1

    def body(du_ref, d_ref, dh_ref, cw_ref, o_ref):
        last = pl.program_id(0) == n_i - 1
        halo = jnp.where(last, 0.0, dh_ref[...].astype(F32))
        dd = d_ref[...].astype(F32)
        row = lax.broadcasted_iota(jnp.int32, dd.shape, 0)
        h0 = halo[0:1, :]
        h1 = halo[1:2, :]
        d1 = jnp.where(row == tm - 1, h0, pltpu.roll(dd, tm - 1, 0))
        d2 = jnp.where(row == tm - 1, h1, jnp.where(row == tm - 2, h0, pltpu.roll(dd, tm - 2, 0)))
        cwv = cw_ref[...]
        dg = cwv[2:3, :] * dd + cwv[1:2, :] * d1 + cwv[0:1, :] * d2
        o_ref[:, :f] = du_ref[...]
        o_ref[:, f:] = dg.astype(o_ref.dtype)

    row_s = pl.BlockSpec((tm, f), lambda i: (i, 0))
    return pl.pallas_call(
        body, name=name, grid=(n_i,),
        in_specs=[row_s, row_s, pl.BlockSpec((HALO, f), lambda i: (jnp.minimum((i + 1) * hb, last_hb), 0)),
                  pl.BlockSpec((8, f), lambda i: (0, 0))],
        out_specs=pl.BlockSpec((tm, 2 * f), lambda i: (i, 0)),
        out_shape=jax.ShapeDtypeStruct((t_len, 2 * f), BF16), compiler_params=_params(1))(du, dgc, dgc, cw)


def _adamw(name, w, g, m, v):
    rows, cols = w.shape
    gcols = g.shape[1]
    n_out = 3 if gcols == cols else 4
    tr = _tile(rows, (256, 128, 64, 32, 16, 8))
    c1 = 1.0 / (1.0 - ADAM_B1 ** ADAM_STEP)
    c2 = 1.0 / (1.0 - ADAM_B2 ** ADAM_STEP)

    def body(w_ref, g_ref, m_ref, v_ref, d_ref, nm_ref, nv_ref, *g_out):
        gv = g_ref[...][:, :cols]
        nm = ADAM_B1 * m_ref[...] + (1.0 - ADAM_B1) * gv
        nv = ADAM_B2 * v_ref[...] + (1.0 - ADAM_B2) * (gv * gv)
        nm_ref[...] = nm
        nv_ref[...] = nv
        d_ref[...] = -ADAM_LR * ((nm * c1) / (jnp.sqrt(nv * c2) + ADAM_EPS) + ADAM_WD * w_ref[...])
        for ref in g_out:
            ref[...] = gv

    blk = pl.BlockSpec((tr, cols), lambda i: (i, 0))
    gblk = pl.BlockSpec((tr, gcols), lambda i: (i, 0))
    shp = jax.ShapeDtypeStruct((rows, cols), F32)
    return pl.pallas_call(body, name=name, grid=(rows // tr,), in_specs=[blk, gblk, blk, blk], out_specs=(blk,) * n_out,
                          out_shape=(shp,) * n_out, compiler_params=_params(1))(w, g, m, v)


class _NoComm:
    def __init__(self):
        self.grads = {}

    def prefetch(self, group, ws, carry):
        return ws, carry

    def need(self, group, ws, after):
        return ws

    def reduce(self, group, grads, carry):
        self.grads.update(grads)
        return carry

    def tick(self, carry):
        return carry


def _local_step(x, target, ws, norms, small, hooks):
    lay = _layout()

    w_main, w_a = _unpack_gin(ws["gin"])
    hn0 = _rms_fwd("rms_attn0", x, norms["attn0"])
    proj = _mm_plain("gla_proj", hn0, w_main, NN, F32)
    a = _mm_plain("gla_proj_a", hn0, w_a, NN, BF16)
    ga, cum = _gla_gate_fwd(a, small["w_a2p"], small["b_a2"])
    ws, cum = hooks.prefetch("B0", ws, cum)
    o_gla, states = _gla_fwd(proj, cum)
    gated = _gla_out_fwd(o_gla, proj, small["head_norm"])
    ws = hooks.need("B0", ws, gated)
    ws, gated = hooks.prefetch("B", ws, gated)
    h1 = _mm_act_wr("gla_out", gated, ws["gout"], lay["gout"], add=x)
    ws = hooks.need("B", ws, h1)

    def ffn_fwd(l, h, prefetch=None):
        nonlocal ws
        hn = _rms_fwd(f"rms_ffn{l}", h, norms[f"ffn{l}"])
        up = _mm_act_wc(f"ffn_up{l}", hn, ws[f"up{l}"], lay[f"up{l}"], BF16)
        act = _glu_fwd(f"glu_fwd{l}", up, small["conv_w"][l], small["conv_b"][l])
        if prefetch is not None:
            ws, act = hooks.prefetch(prefetch, ws, act)
        return hn, up, act, _mm_act_wr(f"ffn_down{l}", act, ws[f"down{l}"], lay[f"down{l}"], add=h)

    hnf0, up0, act0, h2 = ffn_fwd(0, h1, prefetch="C1")

    ws = hooks.need("C1", ws, h2)
    kvn = _rms_fwd("rms_kv", h2, norms["kv"])
    kv = _mm_act_wc("kv_proj", kvn, ws["wkv"], lay["wkv"], BF16)
    hn1 = _rms_fwd("rms_attn1", h2, norms["attn1"])
    q_all = _mm_act_wc("q_proj", hn1, ws["wq"], lay["wq"], BF16)
    views = [_att_views(q_all, kv, g) for g in range(3)]
    branch = [_att_fwd(views[g], g) for g in range(3)]
    ws, lse2 = hooks.prefetch("C2", ws, branch[-1][1])
    o_att, o_att_b, lse = _att_merge([br[0] for br in branch], [br[1] for br in branch[:-1]] + [lse2])
    h3 = _mm_act_wr("att_out", o_att_b, ws["dout"], lay["dout"], add=h2)
    ws = hooks.need("C2", ws, h3)
    hnf1, up1, act1, h4 = ffn_fwd(1, h3)

    dh4, d_final, loss = _loss_head(h4, norms["final"], target)

    sm = {"final": d_final}

    def ffn_bwd(l, dh, h, hn, up, act):
        big = {}
        dact = _mm_dact_wrT(f"ffn_down_dx{l}", dh, ws[f"down{l}"], lay[f"down{l}"])
        big[f"down{l}"] = _mm_grad_wr(f"ffn_down_dw{l}", act, dh, lay[f"down{l}"])
        du, dgc, w0, w1, w2, db = _glu_bwd_a(f"glu_bwd_a{l}", dact, up, small["conv_w"][l], small["conv_b"][l])
        sm[f"conv_w{l}"] = (w0, w1, w2)
        sm[f"conv_b{l}"] = db
        dup = hooks.tick(_glu_bwd_b(f"glu_bwd_b{l}", du, dgc, small["conv_w"][l]))
        dhn = _mm_dact_wcT(f"ffn_up_dx{l}", dup, ws[f"up{l}"], lay[f"up{l}"])
        big[f"up{l}"] = _mm_grad_wc(f"ffn_up_dw{l}", hn, dup, lay[f"up{l}"])
        dh_in, sm[f"ffn{l}"] = _rms_bwd(f"rms_ffn_bwd{l}", dhn, h, norms[f"ffn{l}"], dh)
        return hooks.reduce(f"ffn{l}", big, dh_in)

    dh3 = ffn_bwd(1, dh4, h3, hnf1, up1, act1)

    big = {}
    do_att = _mm_dact_wrT("att_out_dx", dh3, ws["dout"], lay["dout"])
    big["dout"] = _mm_grad_wr("att_out_dw", o_att_b, dh3, lay["dout"])
    delta = _att_delta(do_att, o_att)
    bw = [_att_bwd(views[g], delta, lse, do_att, g) for g in range(3)]
    dq_all = jnp.concatenate([t[0] for t in bw], axis=1)
    dhn1 = _mm_dact_wcT("q_proj_dx", dq_all, ws["wq"], lay["wq"])
    big["wq"] = _mm_grad_wc("q_proj_dw", hn1, dq_all, lay["wq"])
    dh2, sm["attn1"] = _rms_bwd("rms_attn1_bwd", dhn1, h2, norms["attn1"], dh3)
    dkv = hooks.tick(_kv_grad_sum([t[1] for t in bw], [t[2] for t in bw]))
    dkvn = _mm_dact_wcT("kv_proj_dx", dkv, ws["wkv"], lay["wkv"])
    big["wkv"] = _mm_grad_wc("kv_proj_dw", kvn, dkv, lay["wkv"])
    dh2, sm["kv"] = _rms_bwd("rms_kv_bwd", dkvn, h2, norms["kv"], dh2)
    dh2 = hooks.reduce("att", big, dh2)

    dh1 = ffn_bwd(0, dh2, h1, hnf0, up0, act0)

    big = {}
    dgated = _mm_dact_wrT("gla_out_dx", dh1, ws["gout"], lay["gout"])
    big["gout"] = _mm_grad_wr("gla_out_dw", gated, dh1, lay["gout"])
    do_gla, dr, sm["head_norm"] = _gla_out_bwd(dgated, o_gla, proj, small["head_norm"])
    dq, dk, dv, dcum = _gla_bwd(proj, cum, states, hooks.tick(do_gla))
    da, sm["w_a2p"], sm["b_a2"] = _gla_gate_bwd(dcum, ga, a, small["w_a2p"])
    dproj = jnp.concatenate([dq, dk, dv, dr], axis=1)
    dhn0 = _mm_plain("gla_proj_dx", dproj, w_main, NT, F32)
    dhn0 = _mm_plain("gla_proj_a_dx", da, w_a, NT, F32, add=dhn0)
    gin_main = _mm_plain("gla_proj_dw", hn0, dproj, TN, BF16)
    gin_a = _mm_plain("gla_proj_a_dw", hn0, da, TN, BF16)
    big["gin"] = _pack_gin_grad(gin_main, gin_a)
    grad_x, sm["attn0"] = _rms_bwd("rms_attn0_bwd", dhn0, x, norms["attn0"], dh1)
    return loss, grad_x, sm, big


def _pack_weights(chip, names, gla_w_in, gla_w_out, w_kv, dsa_w_q, dsa_w_out, ffn_w_up, ffn_w_down):
    gin = gla_w_in[0]
    gin = jnp.pad(gin, ((0, 0), (0, _roundup(gin.shape[1], LANE) - gin.shape[1])))
    shards = {"gin": gin, "gout": gla_w_out[0], "up0": ffn_w_up[0], "up1": ffn_w_up[1], "down0": ffn_w_down[0],
              "down1": ffn_w_down[1], "wq": dsa_w_q[0], "wkv": w_kv, "dout": dsa_w_out[0]}
    out = {}
    for name in names:
        w = shards[name]
        buf = jnp.zeros((N_CHIPS,) + w.shape, BF16)
        out[name] = lax.dynamic_update_slice(buf, w.astype(BF16)[None], (chip, 0, 0))
    return out


def _unpack_gin(w_gin):
    w = _layout()["gin"][1]
    d = w_gin.shape[1]
    wp = w_gin.shape[2]
    n_main = 2 * GLA_KEY_DIM + 2 * GLA_VAL_DIM
    tm = _tile(d, (256, 128, 64, 32, 16))

    def body(s_ref, main_ref, a_ref):
        full = jnp.concatenate([s_ref[s][:, :w] for s in range(N_CHIPS)], axis=1)
        main_ref[...] = full[:, :n_main]
        a_ref[...] = jnp.concatenate([full[:, n_main:], jnp.zeros((tm, A_PAD - GATE_RANK), full.dtype)], axis=1)

    return pl.pallas_call(
        body, name="unpack_gin", grid=(d // tm,), in_specs=[pl.BlockSpec((N_CHIPS, tm, wp), lambda i: (0, i, 0))],
        out_specs=(pl.BlockSpec((tm, n_main), lambda i: (i, 0)), pl.BlockSpec((tm, A_PAD), lambda i: (i, 0))),
        out_shape=(jax.ShapeDtypeStruct((d, n_main), w_gin.dtype), jax.ShapeDtypeStruct((d, A_PAD), w_gin.dtype)),
        compiler_params=_params(1))(w_gin)


def _pack_gin_grad(gin_main, gin_a):
    w = _layout()["gin"][1]
    wp = _roundup(w, LANE)
    d, n_main = gin_main.shape
    tm = _tile(d, (256, 128, 64, 32, 16))

    def body(main_ref, a_ref, o_ref):
        full = jnp.concatenate([main_ref[...], a_ref[:, :GATE_RANK]], axis=1)
        fill = jnp.zeros((tm, wp - w), full.dtype)
        for s in range(N_CHIPS):
            o_ref[s] = jnp.concatenate([full[:, s * w:(s + 1) * w], fill], axis=1)

    return pl.pallas_call(
        body, name="pack_gin_grad", grid=(d // tm,),
        in_specs=[pl.BlockSpec((tm, n_main), lambda i: (i, 0)), pl.BlockSpec((tm, A_PAD), lambda i: (i, 0))],
        out_specs=pl.BlockSpec((N_CHIPS, tm, wp), lambda i: (0, i, 0)),
        out_shape=jax.ShapeDtypeStruct((N_CHIPS, d, wp), gin_main.dtype), compiler_params=_params(1))(gin_main, gin_a)


def _small_params(attn_norm, ffn_norm, kv_norm, final_norm, conv_b, w_a2, b_a2, head_norm, conv_w):
    norms = {"attn0": attn_norm[0:1], "attn1": attn_norm[1:2], "ffn0": ffn_norm[0:1], "ffn1": ffn_norm[1:2],
             "kv": kv_norm[None, :], "final": final_norm[None, :]}
    small = {"w_a2p": jnp.pad(w_a2, ((0, A_PAD - GATE_RANK), (0, 0))), "b_a2": b_a2[None, :],
             "head_norm": head_norm[None, :], "conv_w": jnp.pad(conv_w, ((0, 0), (0, 8 - conv_w.shape[1]), (0, 0))),
             "conv_b": conv_b[:, None, :]}
    return norms, small


ANY = pl.BlockSpec(memory_space=pl.ANY)


def _place():
    return lax.axis_index("x"), lax.axis_index("y"), lax.axis_index("c")


def _other_chips(x, y):
    return [(1 - x, y), (x, 1 - y), (1 - x, 1 - y)]


def _rcopy(src, dst, ssem, rsem, dev):
    return pltpu.make_async_remote_copy(src_ref=src, dst_ref=dst, send_sem=ssem, recv_sem=rsem, device_id=dev,
                                        device_id_type=MESH)


def _pack_shard(name, w, layer, chip_arr, after):
    rows, cols = w.shape[-2:]
    tr = _tile(rows, (512, 352, 256, 128, 64, 32, 16))

    def body(p_ref, w_ref, after_ref, o_ref):
        o_ref[...] = w_ref[...].astype(o_ref.dtype)

    if w.ndim == 3:
        w_spec = pl.BlockSpec((None, tr, cols), lambda i, p: (layer, i, 0))
    else:
        w_spec = pl.BlockSpec((tr, cols), lambda i, p: (i, 0))
    return pl.pallas_call(
        body, name=name,
        grid_spec=pltpu.PrefetchScalarGridSpec(
            num_scalar_prefetch=1, grid=(rows // tr,), in_specs=[w_spec, ANY],
            out_specs=pl.BlockSpec((None, tr, cols), lambda i, p: (p[0], i, 0))),
        out_shape=jax.ShapeDtypeStruct((N_CHIPS, rows, cols), BF16), compiler_params=_params(1))(chip_arr, w, after)


def _swap_halves(name, arrs):
    n = len(arrs)

    def body(*refs):
        ins, outs = refs[:n], refs[n:2 * n]
        send, recv = refs[2 * n:]
        x, y, c = _place()
        cps = []
        for a in range(n):
            h = ins[a].shape[1] // 2
            cp = _rcopy(ins[a].at[:, pl.ds((1 - c) * h, h)], outs[a], send.at[a], recv.at[a], (x, y, 1 - c))
            cp.start()
            cps.append(cp)
        for cp in cps:
            cp.wait()

    return pl.pallas_call(
        body, name=name, in_specs=[ANY] * n, out_specs=[ANY] * n,
        out_shape=[jax.ShapeDtypeStruct((a.shape[0], a.shape[1] // 2, a.shape[2]), a.dtype) for a in arrs],
        scratch_shapes=[pltpu.SemaphoreType.DMA((n,)), pltpu.SemaphoreType.DMA((n,))])(*arrs)


SEM = pl.BlockSpec(memory_space=pltpu.SEMAPHORE)
EFFECT = pltpu.SideEffectType.DATAFLOW_SIDE_EFFECTING


def _shapes(arrs):
    return [jax.ShapeDtypeStruct(a.shape, a.dtype) for a in arrs]


def _gather_start(name, thru, arrs):
    n, nt = len(arrs), len(thru)

    def body(*refs):
        ins = refs[nt:nt + n]
        send, recv = refs[nt + n], refs[nt + n + 1]
        outs = refs[2 * nt + n + 2:]
        x, y, c = _place()
        me = 2 * x + y
        for a in range(n):
            h = ins[a].shape[1] // 2
            mine = pl.ds(c * h, h)
            for j, (px, py) in enumerate(_other_chips(x, y)):
                _rcopy(ins[a].at[me, mine], outs[a].at[me, mine], send.at[3 * a + j], recv.at[3 * a + j], (px, py, c)).start()

    res = pl.pallas_call(
        body, name=name, in_specs=[ANY] * (nt + n), out_specs=[SEM, SEM] + [ANY] * (nt + n),
        out_shape=[pltpu.SemaphoreType.DMA((3 * n,)), pltpu.SemaphoreType.DMA((3 * n,))] + _shapes(thru) + _shapes(arrs),
        input_output_aliases={i: 2 + i for i in range(nt + n)},
        compiler_params=pltpu.CompilerParams(has_side_effects=EFFECT))(*thru, *arrs)
    return res[0], res[1], res[2:2 + nt], res[2 + nt:]


def _gather_wait(name, send, recv, arrs, after):
    n = len(arrs)

    def body(*refs):
        ins = refs[:n]
        send_ref, recv_ref = refs[n], refs[n + 1]
        x, y, c = _place()
        me = 2 * x + y
        for a in range(n):
            h = ins[a].shape[1] // 2
            mine = pl.ds(c * h, h)
            for j, (px, py) in enumerate(_other_chips(x, y)):
                sent = ins[a].at[me, mine]
                landed = ins[a].at[2 * px + py, mine]
                cp = _rcopy(sent, landed, send_ref.at[3 * a + j], recv_ref.at[3 * a + j], (px, py, c))
                cp.wait_send()
                cp.wait_recv()

    return pl.pallas_call(
        body, name=name, in_specs=[ANY] * n + [SEM, SEM, ANY], out_specs=[ANY] * n, out_shape=_shapes(arrs),
        input_output_aliases={a: a for a in range(n)},
        compiler_params=pltpu.CompilerParams(has_side_effects=EFFECT))(*arrs, send, recv, after)


def _forward_halves(name, arrs):
    n = len(arrs)

    def body(*refs):
        ins, outs = refs[:n], refs[n:2 * n]
        send, recv = refs[2 * n:]
        x, y, c = _place()
        sib = (x, y, 1 - c)
        chips = _other_chips(x, y)
        cps = []
        for a in range(n):
            h = ins[a].shape[1] // 2
            mine = pl.ds(c * h, h)
            for j, (px, py) in enumerate(chips):
                cp = _rcopy(ins[a].at[2 * px + py, mine], outs[a].at[2 * px + py, mine], send.at[3 * a + j],
                            recv.at[3 * a + j], sib)
                cp.start()
                cps.append(cp)
        for a in range(n):
            h = ins[a].shape[1] // 2
            theirs = pl.ds((1 - c) * h, h)
            for j, (px, py) in enumerate(chips):
                got = outs[a].at[2 * px + py, theirs]
                _rcopy(got, got, send.at[3 * a + j], recv.at[3 * a + j], sib).wait_recv()
        for cp in cps:
            cp.wait_send()

    return pl.pallas_call(
        body, name=name, in_specs=[ANY] * n, out_specs=[ANY] * n, out_shape=_shapes(arrs),
        input_output_aliases={a: a for a in range(n)},
        scratch_shapes=[pltpu.SemaphoreType.DMA((3 * n,)), pltpu.SemaphoreType.DMA((3 * n,))])(*arrs)


def _forward_start(name, thru, arrs):
    n, nt = len(arrs), len(thru)

    def body(*refs):
        ins = refs[nt:nt + n]
        send, recv = refs[nt + n], refs[nt + n + 1]
        outs = refs[2 * nt + n + 2:]
        x, y, c = _place()
        for a in range(n):
            h = ins[a].shape[1] // 2
            mine = pl.ds(c * h, h)
            for j, (px, py) in enumerate(_other_chips(x, y)):
                _rcopy(ins[a].at[2 * px + py, mine], outs[a].at[2 * px + py, mine], send.at[3 * a + j], recv.at[3 * a + j],
                       (x, y, 1 - c)).start()

    res = pl.pallas_call(
        body, name=name, in_specs=[ANY] * (nt + n), out_specs=[SEM, SEM] + [ANY] * (nt + n),
        out_shape=[pltpu.SemaphoreType.DMA((3 * n,)), pltpu.SemaphoreType.DMA((3 * n,))] + _shapes(thru) + _shapes(arrs),
        input_output_aliases={i: 2 + i for i in range(nt + n)},
        compiler_params=pltpu.CompilerParams(has_side_effects=EFFECT))(*thru, *arrs)
    return res[0], res[1], res[2:2 + nt], res[2 + nt:]


def _forward_wait(name, send, recv, arrs, after):
    n = len(arrs)

    def body(*refs):
        ins = refs[:n]
        send_ref, recv_ref = refs[n], refs[n + 1]
        x, y, c = _place()
        for a in range(n):
            h = ins[a].shape[1] // 2
            for j, (px, py) in enumerate(_other_chips(x, y)):
                sent = ins[a].at[2 * px + py, pl.ds(c * h, h)]
                got = ins[a].at[2 * px + py, pl.ds((1 - c) * h, h)]
                cp = _rcopy(sent, got, send_ref.at[3 * a + j], recv_ref.at[3 * a + j], (x, y, 1 - c))
                cp.wait_send()
                cp.wait_recv()

    return pl.pallas_call(
        body, name=name, in_specs=[ANY] * n + [SEM, SEM, ANY], out_specs=[ANY] * n, out_shape=_shapes(arrs),
        input_output_aliases={a: a for a in range(n)},
        compiler_params=pltpu.CompilerParams(has_side_effects=EFFECT))(*arrs, send, recv, after)


def _scatter_start(name, thru, arrs):
    n, nt = len(arrs), len(thru)
    landing = [jnp.zeros_like(a) for a in arrs]

    def body(*refs):
        ins = refs[nt:nt + n]
        send, recv = refs[nt + 2 * n], refs[nt + 2 * n + 1]
        outs = refs[2 * nt + 3 * n + 2:]
        x, y, c = _place()
        me = 2 * x + y
        for a in range(n):
            for j, (px, py) in enumerate(_other_chips(x, y)):
                _rcopy(ins[a].at[2 * px + py], outs[a].at[me], send.at[3 * a + j], recv.at[3 * a + j], (px, py, c)).start()

    res = pl.pallas_call(
        body, name=name, in_specs=[ANY] * (nt + 2 * n), out_specs=[SEM, SEM] + [ANY] * (nt + 2 * n),
        out_shape=[pltpu.SemaphoreType.DMA((3 * n,)), pltpu.SemaphoreType.DMA((3 * n,))] + _shapes(thru) + _shapes(arrs)
        + _shapes(landing),
        input_output_aliases={i: 2 + i for i in range(nt + 2 * n)},
        compiler_params=pltpu.CompilerParams(has_side_effects=EFFECT))(*thru, *arrs, *landing)
    return res[0], res[1], res[2:2 + nt], res[2 + nt:2 + nt + n], res[2 + nt + n:]


def _scatter_wait(name, send, recv, arrs, landing, after):
    n = len(arrs)

    def body(*refs):
        ins, land = refs[:n], refs[n:2 * n]
        send_ref, recv_ref = refs[2 * n], refs[2 * n + 1]
        x, y, c = _place()
        for a in range(n):
            for j, (px, py) in enumerate(_other_chips(x, y)):
                cp = _rcopy(ins[a].at[2 * px + py], land[a].at[2 * px + py], send_ref.at[3 * a + j], recv_ref.at[3 * a + j],
                            (px, py, c))
                cp.wait_send()
                cp.wait_recv()

    res = pl.pallas_call(
        body, name=name, in_specs=[ANY] * (2 * n) + [SEM, SEM, ANY], out_specs=[ANY] * (2 * n),
        out_shape=_shapes(arrs) + _shapes(landing), input_output_aliases={i: i for i in range(2 * n)},
        compiler_params=pltpu.CompilerParams(has_side_effects=EFFECT))(*arrs, *landing, send, recv, after)
    return res[:n], res[n:]


def _swap_start(name, thru, arrs):
    n, nt = len(arrs), len(thru)
    landing = [lax.empty((a.shape[0], a.shape[1] // 2, a.shape[2]), a.dtype) for a in arrs]

    def body(*refs):
        ins = refs[nt:nt + n]
        send, recv = refs[nt + 2 * n], refs[nt + 2 * n + 1]
        outs = refs[2 * nt + 3 * n + 2:]
        x, y, c = _place()
        for a in range(n):
            h = ins[a].shape[1] // 2
            _rcopy(ins[a].at[:, pl.ds((1 - c) * h, h)], outs[a], send.at[a], recv.at[a], (x, y, 1 - c)).start()

    res = pl.pallas_call(
        body, name=name, in_specs=[ANY] * (nt + 2 * n), out_specs=[SEM, SEM] + [ANY] * (nt + 2 * n),
        out_shape=[pltpu.SemaphoreType.DMA((n,)), pltpu.SemaphoreType.DMA((n,))] + _shapes(thru) + _shapes(arrs)
        + _shapes(landing),
        input_output_aliases={i: 2 + i for i in range(nt + 2 * n)},
        compiler_params=pltpu.CompilerParams(has_side_effects=EFFECT))(*thru, *arrs, *landing)
    return res[0], res[1], res[2:2 + nt], res[2 + nt:2 + nt + n], res[2 + nt + n:]


def _swap_wait(name, send, recv, arrs, landing, after):
    n = len(arrs)

    def body(*refs):
        ins, land = refs[:n], refs[n:2 * n]
        send_ref, recv_ref = refs[2 * n], refs[2 * n + 1]
        x, y, c = _place()
        for a in range(n):
            h = ins[a].shape[1] // 2
            cp = _rcopy(ins[a].at[:, pl.ds((1 - c) * h, h)], land[a], send_ref.at[a], recv_ref.at[a], (x, y, 1 - c))
            cp.wait_send()
            cp.wait_recv()

    res = pl.pallas_call(
        body, name=name, in_specs=[ANY] * (2 * n) + [SEM, SEM, ANY], out_specs=[ANY] * (2 * n),
        out_shape=_shapes(arrs) + _shapes(landing), input_output_aliases={i: i for i in range(2 * n)},
        compiler_params=pltpu.CompilerParams(has_side_effects=EFFECT))(*arrs, *landing, send, recv, after)
    return res[:n], res[n:]


def _join_start(name, arrs):
    n = len(arrs)

    def body(*refs):
        ins = refs[:n]
        send, recv = refs[n], refs[n + 1]
        outs = refs[n + 2:]
        x, y, c = _place()
        for a in range(n):
            h = ins[a].shape[0] // 2
            mine = pl.ds(c * h, h)
            _rcopy(ins[a].at[mine], outs[a].at[mine], send.at[a], recv.at[a], (x, y, 1 - c)).start()

    res = pl.pallas_call(
        body, name=name, in_specs=[ANY] * n, out_specs=[SEM, SEM] + [ANY] * n,
        out_shape=[pltpu.SemaphoreType.DMA((n,)), pltpu.SemaphoreType.DMA((n,))] + _shapes(arrs),
        input_output_aliases={i: 2 + i for i in range(n)},
        compiler_params=pltpu.CompilerParams(has_side_effects=EFFECT))(*arrs)
    return res[0], res[1], res[2:]


def _join_wait(name, send, recv, arrs, after):
    n = len(arrs)

    def body(*refs):
        ins = refs[:n]
        send_ref, recv_ref = refs[n], refs[n + 1]
        x, y, c = _place()
        for a in range(n):
            h = ins[a].shape[0] // 2
            cp = _rcopy(ins[a].at[pl.ds(c * h, h)], ins[a].at[pl.ds((1 - c) * h, h)], send_ref.at[a], recv_ref.at[a],
                        (x, y, 1 - c))
            cp.wait_send()
            cp.wait_recv()

    return pl.pallas_call(
        body, name=name, in_specs=[ANY] * n + [SEM, SEM, ANY], out_specs=[ANY] * n, out_shape=_shapes(arrs),
        input_output_aliases={a: a for a in range(n)},
        compiler_params=pltpu.CompilerParams(has_side_effects=EFFECT))(*arrs, send, recv, after)


def _join_halves(name, arrs):
    n = len(arrs)

    def body(*refs):
        ins, outs = refs[:n], refs[n:2 * n]
        send, recv = refs[2 * n:]
        x, y, c = _place()
        cps = []
        for a in range(n):
            h = ins[a].shape[0] // 2
            mine = pl.ds(c * h, h)
            cp = _rcopy(ins[a].at[mine], outs[a].at[mine], send.at[a], recv.at[a], (x, y, 1 - c))
            cp.start()
            cps.append(cp)
        for a in range(n):
            h = ins[a].shape[0] // 2
            got = outs[a].at[pl.ds((1 - c) * h, h)]
            _rcopy(got, got, send.at[a], recv.at[a], (x, y, 1 - c)).wait_recv()
        for cp in cps:
            cp.wait_send()

    return pl.pallas_call(
        body, name=name, in_specs=[ANY] * n, out_specs=[ANY] * n,
        out_shape=[jax.ShapeDtypeStruct(a.shape, a.dtype) for a in arrs],
        input_output_aliases={a: a for a in range(n)},
        scratch_shapes=[pltpu.SemaphoreType.DMA((n,)), pltpu.SemaphoreType.DMA((n,))])(*arrs)


def _allgather8(name, xs, reduce):
    m_per, n = xs.shape

    def body(x_ref, out_ref, *rest):
        if reduce:
            sum_ref, send, recv, lsem = rest
        else:
            send, recv, lsem = rest
        x, y, c = _place()
        me, sib = (x, y, c), (x, y, 1 - c)
        chips = _other_chips(x, y)

        def rows(px, py, pc):
            return out_ref.at[pl.ds((4 * px + 2 * py + pc) * m_per, m_per), :]

        def copy(k, block, to, src=None):
            return _rcopy(rows(*block) if src is None else src, rows(*block), send.at[k], recv.at[k], to)

        mine = pltpu.make_async_copy(x_ref, rows(*me), lsem)
        mine.start()
        first = [copy(0, me, sib, src=x_ref)]
        first += [copy(1 + j, me, (*chip, c), src=x_ref) for j, chip in enumerate(chips)]
        for cp in first:
            cp.start()
        passed = [copy(4 + j, (*chip, c), sib) for j, chip in enumerate(chips)]
        for j, chip in enumerate(chips):
            copy(1 + j, (*chip, c), me).wait_recv()
            passed[j].start()
        copy(0, sib, me).wait_recv()
        for j, chip in enumerate(chips):
            copy(4 + j, (*chip, 1 - c), me).wait_recv()
        for cp in first + passed:
            cp.wait_send()
        mine.wait()
        if reduce:
            acc = out_ref[pl.ds(0, m_per), :]
            for dev in range(1, N_DEV):
                acc = acc + out_ref[pl.ds(dev * m_per, m_per), :]
            sum_ref[...] = acc

    vm = pl.BlockSpec(memory_space=pltpu.VMEM)
    out_shape = [jax.ShapeDtypeStruct((N_DEV * m_per, n), xs.dtype)]
    if reduce:
        out_shape.append(jax.ShapeDtypeStruct((m_per, n), xs.dtype))
    return pl.pallas_call(
        body, name=name, in_specs=[vm], out_specs=[vm] * len(out_shape), out_shape=out_shape,
        scratch_shapes=[pltpu.SemaphoreType.DMA((7,)), pltpu.SemaphoreType.DMA((7,)), pltpu.SemaphoreType.DMA],
        compiler_params=pltpu.CompilerParams(vmem_limit_bytes=VMEM_LIMIT))(xs)


def _add_my_half(name, a, rb, c_arr):
    s, h, cols = rb.shape
    tr = _tile(h, (512, 352, 256, 128, 64, 32, 16))
    nt = h // tr

    def body(c_ref, a_ref, b_ref, o_ref):
        o_ref[...] = (a_ref[...].astype(F32) + b_ref[...].astype(F32)).astype(o_ref.dtype)

    return pl.pallas_call(
        body, name=name,
        grid_spec=pltpu.PrefetchScalarGridSpec(
            num_scalar_prefetch=1, grid=(s, nt),
            in_specs=[pl.BlockSpec((None, tr, cols), lambda k, i, c: (k, c[0] * nt + i, 0)),
                      pl.BlockSpec((None, tr, cols), lambda k, i, c: (k, i, 0))],
            out_specs=pl.BlockSpec((None, tr, cols), lambda k, i, c: (k, i, 0))),
        out_shape=jax.ShapeDtypeStruct(rb.shape, BF16), compiler_params=_params(2))(c_arr, a, rb)


def _sum_chips(name, own, q, place):
    s, h, cols = q.shape
    tr = _tile(h, (512, 352, 256, 128, 64, 32, 16))
    nt = h // tr

    def body(p_ref, own_ref, q_ref, o_ref):
        chip = p_ref[0]
        acc = jnp.where(chip == 0, own_ref[0], q_ref[0]).astype(F32)
        for j in range(1, s):
            acc = acc + jnp.where(chip == j, own_ref[j], q_ref[j]).astype(F32)
        o_ref[...] = acc

    blk = pl.BlockSpec((s, tr, cols), lambda i, p: (0, i, 0))
    return pl.pallas_call(
        body, name=name,
        grid_spec=pltpu.PrefetchScalarGridSpec(
            num_scalar_prefetch=1, grid=(nt,), in_specs=[blk, blk],
            out_specs=pl.BlockSpec((tr, cols), lambda i, p: (p[1] * nt + i, 0))),
        out_shape=jax.ShapeDtypeStruct((2 * h, cols), F32), compiler_params=_params(1))(place, own, q)


def _pack_rows(parts):
    rows = []
    for p in parts:
        flat = p.reshape(-1).astype(F32)
        n = _roundup(flat.shape[0], 8 * LANE)
        rows.append(jnp.pad(flat, (0, n - flat.shape[0])).reshape(-1, LANE))
    return jnp.concatenate(rows, axis=0)


def _unpack_rows(buf, shapes):
    out, r = [], 0
    for shp in shapes:
        size = math.prod(shp)
        nr = _roundup(size, 8 * LANE) // LANE
        out.append(buf[r:r + nr].reshape(-1)[:size].reshape(shp))
        r += nr
    return out


def kernel(x, attn_norm, gla_w_in, gla_w_a2, gla_b_a2, gla_head_norm, gla_w_out, kv_norm, w_kv, dsa_w_q, dsa_w_out, ffn_norm, ffn_w_up, ffn_conv_w, ffn_conv_b, ffn_w_down, final_norm, loss_target, m_attn_norm, m_gla_w_in, m_gla_w_a2, m_gla_b_a2, m_gla_head_norm, m_gla_w_out, m_kv_norm, m_w_kv, m_dsa_w_q, m_dsa_w_out, m_ffn_norm, m_ffn_w_up, m_ffn_conv_w, m_ffn_conv_b, m_ffn_w_down, m_final_norm, v_attn_norm, v_gla_w_in, v_gla_w_a2, v_gla_b_a2, v_gla_head_norm, v_gla_w_out, v_kv_norm, v_w_kv, v_dsa_w_q, v_dsa_w_out, v_ffn_norm, v_ffn_w_up, v_ffn_conv_w, v_ffn_conv_b, v_ffn_w_down, v_final_norm):
    lay = _layout()
    d, f = D_MODEL, D_FF
    cx, cy, cc = _place()
    chip = 2 * cx + cy
    c_arr = jnp.reshape(cc, (1,)).astype(jnp.int32)
    place = jnp.stack([chip, cc]).astype(jnp.int32)

    groups = {"A": ("gin", "small"), "B0": ("gout",), "B": ("up0", "down0"), "C1": ("wkv", "wq", "dout"),
              "C2": ("up1", "down1")}
    big_shards = (gla_w_in, gla_w_out, w_kv, dsa_w_q, dsa_w_out, ffn_w_up, ffn_w_down)
    ws = _pack_weights(chip, groups["A"][:1], *big_shards)
    sharded_small = [gla_w_a2[0], gla_b_a2[0], gla_head_norm[0], ffn_conv_w]
    packed = _pack_rows(sharded_small)
    packed = jnp.pad(packed, ((0, _roundup(packed.shape[0], 16) - packed.shape[0]), (0, 0)))
    ws["small"] = lax.dynamic_update_slice(jnp.zeros((N_CHIPS,) + packed.shape, F32), packed[None], (chip, 0, 0))
    send, recv, _, arrs = _gather_start("gather_a_start", [], [ws[k] for k in groups["A"]])
    chip_arr = place[:1]
    sources = {"up0": (ffn_w_up, 0), "up1": (ffn_w_up, 1), "down0": (ffn_w_down, 0), "down1": (ffn_w_down, 1),
               "wq": (dsa_w_q, 0), "wkv": (w_kv, 0), "dout": (dsa_w_out, 0), "gout": (gla_w_out, 0)}
    later = ("B0", "B", "C1", "C2")
    for k in sum((groups[grp] for grp in later), ()):
        ws[k] = _pack_shard(f"pack_{k}", *sources[k], chip_arr, arrs[1])
    arrs = _gather_wait("gather_a_wait", send, recv, arrs, ws["dout"])
    ws.update(zip(groups["A"], _forward_halves("forward_a", arrs)))
    in_flight = {}
    thru = [ws[k] for k in groups["A"]]
    for grp in later:
        send, recv, thru, arrs = _gather_start(f"gather_{grp.lower()}_start", thru, [ws[k] for k in groups[grp]])
        ws.update(zip(groups[grp], arrs))
        in_flight[grp] = (send, recv)
    ws.update(zip(groups["A"], thru))
    pending = []

    class _Comm:
        def prefetch(self, grp, ws, carry):
            send, recv = in_flight[grp]
            arrs = _gather_wait(f"gather_{grp.lower()}_wait", send, recv, [ws[k] for k in groups[grp]], carry)
            send, recv, thru, arrs = _forward_start(f"forward_{grp.lower()}_start", [carry], arrs)
            in_flight[grp] = (send, recv)
            return {**ws, **dict(zip(groups[grp], arrs))}, thru[0]

        def need(self, grp, ws, after):
            send, recv = in_flight[grp]
            arrs = _forward_wait(f"forward_{grp.lower()}_wait", send, recv, [ws[k] for k in groups[grp]], after)
            return {**ws, **dict(zip(groups[grp], arrs))}

        swapping = None

        def reduce(self, grp, grads, carry):
            names = list(grads)
            send, recv, thru, parts, theirs = _swap_start(f"swap_{grp}_start", [carry], [grads[k] for k in names])
            self.swapping = (grp, names, send, recv, parts, theirs)
            return thru[0]

        def tick(self, carry):
            if self.swapping is None:
                return carry
            grp, names, send, recv, parts, theirs = self.swapping
            self.swapping = None
            parts, theirs = _swap_wait(f"swap_{grp}_wait", send, recv, parts, theirs, carry)
            return self.scatter(grp, names, parts, theirs, carry)

        def scatter(self, grp, names, parts, theirs, carry):
            sums = [_add_my_half(f"add_half_{k}", a, b, c_arr) for k, a, b in zip(names, parts, theirs)]
            send, recv, thru, sums, landing = _scatter_start(f"scatter_{grp}_start", [carry], sums)
            pending.append((grp, names, send, recv, sums, landing))
            return thru[0]

        def reduce_now(self, grp, grads, carry):
            names = list(grads)
            parts = [grads[k] for k in names]
            return self.scatter(grp, names, parts, _swap_halves(f"swap_{grp}", parts), carry)

    shards = [_unpack_rows(ws["small"][s], [p.shape for p in sharded_small]) for s in range(N_CHIPS)]
    w_a2, b_a2, head_norm, conv_w = [jnp.concatenate([shards[s][k] for s in range(N_CHIPS)], axis=-1) for k in range(4)]
    norms, small = _small_params(attn_norm, ffn_norm, kv_norm, final_norm, ffn_conv_b, w_a2, b_a2, head_norm, conv_w)

    comm = _Comm()
    loss_blk, grad_x, sm, last_big = _local_step(x[0], loss_target[0], ws, norms, small, comm)

    small_parts = [loss_blk, jnp.concatenate([sm["attn0"], sm["attn1"]]), jnp.concatenate([sm["ffn0"], sm["ffn1"]]),
                   sm["kv"], sm["final"], jnp.concatenate([sm["conv_b0"], sm["conv_b1"]]),
                   sm["w_a2p"][:GATE_RANK], sm["b_a2"], sm["head_norm"],
                   jnp.stack([jnp.concatenate(sm["conv_w0"]), jnp.concatenate(sm["conv_w1"])])]
    small_shapes = [(8, LANE), (2, d), (2, d), (d,), (d,), (2, f), (GATE_RANK, GLA_KEY_DIM), (GLA_KEY_DIM,),
                    (GLA_VAL_DIM // GLA_HEADS,), (2, 3, f)]
    _, reduced = _allgather8("reduce_small", _pack_rows(small_parts), True)
    reduced = comm.reduce_now("gla", last_big, reduced)

    loss_r, g_attn, g_ffn, g_kv, g_final, g_cb, g_a2, g_ba2, g_hn, g_cw = _unpack_rows(reduced, small_shapes)
    loss = loss_r[0, 0]

    def mine(g, axis):
        w = g.shape[axis] // N_CHIPS
        return lax.dynamic_slice_in_dim(g, chip * w, w, axis)

    grads = {
        "attn_norm": g_attn, "gla_w_a2": mine(g_a2, 1)[None], "gla_b_a2": mine(g_ba2, 0)[None],
        "gla_head_norm": mine(g_hn, 0)[None], "kv_norm": g_kv, "ffn_norm": g_ffn, "ffn_conv_w": mine(g_cw, 2),
        "ffn_conv_b": g_cb, "final_norm": g_final,
    }
    weights = {"attn_norm": (attn_norm, m_attn_norm, v_attn_norm), "gla_w_in": (gla_w_in, m_gla_w_in, v_gla_w_in),
               "gla_w_a2": (gla_w_a2, m_gla_w_a2, v_gla_w_a2), "gla_b_a2": (gla_b_a2, m_gla_b_a2, v_gla_b_a2),
               "gla_head_norm": (gla_head_norm, m_gla_head_norm, v_gla_head_norm),
               "gla_w_out": (gla_w_out, m_gla_w_out, v_gla_w_out), "kv_norm": (kv_norm, m_kv_norm, v_kv_norm),
               "w_kv": (w_kv, m_w_kv, v_w_kv), "dsa_w_q": (dsa_w_q, m_dsa_w_q, v_dsa_w_q),
               "dsa_w_out": (dsa_w_out, m_dsa_w_out, v_dsa_w_out), "ffn_norm": (ffn_norm, m_ffn_norm, v_ffn_norm),
               "ffn_w_up": (ffn_w_up, m_ffn_w_up, v_ffn_w_up), "ffn_conv_w": (ffn_conv_w, m_ffn_conv_w, v_ffn_conv_w),
               "ffn_conv_b": (ffn_conv_b, m_ffn_conv_b, v_ffn_conv_b),
               "ffn_w_down": (ffn_w_down, m_ffn_w_down, v_ffn_w_down), "final_norm": (final_norm, m_final_norm, v_final_norm)}
    order = list(weights)
    big_names = ("gla_w_in", "gla_w_out", "w_kv", "dsa_w_q", "dsa_w_out", "ffn_w_up", "ffn_w_down")
    delta, new_m, new_v = {}, {}, {}

    def adam_big(k, g):
        w, m, v = weights[k]
        cols = w.shape[-1]
        res = _adamw(f"adamw_{k}", w.reshape(-1, cols), g.reshape(-1, g.shape[-1]), m.reshape(-1, cols), v.reshape(-1, cols))
        delta[k], new_m[k], new_v[k] = [r.reshape(w.shape) for r in res[:3]]
        grads[k] = res[3].reshape(w.shape) if len(res) == 4 else g
        return res[0]

    full = {}
    after = reduced
    joining = []
    for grp, names, send, recv, sums, landing in pending[:-1]:
        sums, landing = _scatter_wait(f"scatter_{grp}_wait", send, recv, sums, landing, after)
        halves = [_sum_chips(f"sum_chips_{k}", s, q, place) for k, s, q in zip(names, sums, landing)]
        send, recv, halves = _join_start(f"join_{grp}_start", halves)
        joining.append((grp, names, send, recv, halves))
        after = halves[0]
    for grp, names, send, recv, halves in joining:
        joined = _join_wait(f"join_{grp}_wait", send, recv, halves, after)
        full.update(zip(names, joined))
        after = joined[0]
    after = adam_big("w_kv", full["wkv"])
    after = adam_big("dsa_w_q", full["wq"][None])
    after = adam_big("dsa_w_out", full["dout"][None])
    after = adam_big("ffn_w_up", jnp.stack([full["up0"], full["up1"]]))
    after = adam_big("ffn_w_down", jnp.stack([full["down0"], full["down1"]]))
    grp, names, send, recv, sums, landing = pending[-1]
    sums, landing = _scatter_wait(f"scatter_{grp}_wait", send, recv, sums, landing, after)
    halves = [_sum_chips(f"sum_chips_{k}", s, q, place) for k, s, q in zip(names, sums, landing)]
    full.update(zip(names, _join_halves(f"join_{grp}", halves)))
    adam_big("gla_w_in", full["gin"])
    adam_big("gla_w_out", full["gout"][None])
    small_names = [k for k in order if k not in big_names]
    packed = [_pack_rows([src[k] for k in small_names])
              for src in ({k: weights[k][0] for k in small_names}, grads, {k: weights[k][1] for k in small_names},
                          {k: weights[k][2] for k in small_names})]
    res = _adamw("adamw_small", *packed)
    shapes = [weights[k][0].shape for k in small_names]
    for dst, buf in zip((delta, new_m, new_v), res):
        for k, val in zip(small_names, _unpack_rows(buf, shapes)):
            dst[k] = val
    return (loss, grad_x[None], *[grads[k] for k in order], *[delta[k] for k in order], *[new_m[k] for k in order],
            *[new_v[k] for k in order])
```

```python
import math

import jax
import jax.numpy as jnp
from jax import lax
from jax.experimental import pallas as pl
from jax.experimental.pallas import tpu as pltpu

F32 = jnp.float32
BF16 = jnp.bfloat16

D_MODEL = 2048
SEQ = 4096
GLA_HEADS = 4
GLA_KEY_DIM = D_MODEL // 2
GLA_VAL_DIM = D_MODEL
GATE_RANK = 16
GATE_NORMALIZER = 16.0
GLA_CHUNK = 64
ATT_HEADS = 16
HEAD_DIM = 128
WINDOWS = (128, 512, 2048)
DILATIONS = (1, 4, 16)
ATT_BLOCK = 128
D_FF = 5632
EPS = 1e-6
ADAM_LR = 0.001
ADAM_B1 = 0.9
ADAM_B2 = 0.999
ADAM_EPS = 1e-08
ADAM_WD = 0.01
ADAM_STEP = 10

N_CHIPS = 4
N_DEV = 8
LANE = 128
A_PAD = 128
VMEM_LIMIT = 56 * 1024 * 1024
MAX_K_TILE = 2816
NEG = -1e30
MESH = pl.DeviceIdType.MESH

NN = (((1,), (0,)), ((), ()))
NT = (((1,), (1,)), ((), ()))
TN = (((0,), (0,)), ((), ()))


def _tile(n, cands):
    for c in cands:
        if c <= n and n % c == 0:
            return c
    return n


def _roundup(n, m):
    return -(-n // m) * m


def _params(n_axes):
    return pltpu.CompilerParams(dimension_semantics=("arbitrary",) * n_axes, vmem_limit_bytes=VMEM_LIMIT)


def _dot(a, b, dims):
    return lax.dot_general(a, b, dims, preferred_element_type=F32)


def _sigmoid(x):
    return 1.0 / (1.0 + jnp.exp(-x))


COL_SHARDED = ("gin", "up0", "up1", "wq", "wkv")
ROW_SHARDED = ("gout", "down0", "down1", "dout")


def _layout():
    f = D_FF
    hd = ATT_HEADS * HEAD_DIM
    gin = 2 * GLA_KEY_DIM + 2 * GLA_VAL_DIM + GATE_RANK
    up_w = 2 * f // N_CHIPS
    q_w = 3 * hd // N_CHIPS
    kv_w = 2 * hd // N_CHIPS
    dn_r = f // N_CHIPS
    go_r = GLA_VAL_DIM // N_CHIPS
    do_r = hd // N_CHIPS
    big = (1408, 1024, 512, 256, 128)
    return {
        "gin": (0, gin // N_CHIPS, LANE),
        "up0": (0, up_w, _tile(up_w, big)), "up1": (0, up_w, _tile(up_w, big)),
        "wq": (0, q_w, _tile(q_w, (512, 384, 256, 128))), "wkv": (0, kv_w, _tile(kv_w, (1024, 512, 256, 128))),
        "down0": (0, dn_r, _tile(dn_r, big)), "down1": (0, dn_r, _tile(dn_r, big)),
        "gout": (0, go_r, _tile(go_r, (512, 256, 128))), "dout": (0, do_r, _tile(do_r, (512, 256, 128))),
    }


def _matmul(name, a, b, dims, grid, a_spec, b_spec, o_spec, out_shape, acc_shape, add=None, add_spec=None):
    nk = grid[2]
    has_add = add is not None

    def body(*refs):
        a_ref, b_ref = refs[0], refs[1]
        pos = 2
        add_ref = None
        if has_add:
            add_ref = refs[pos]
            pos += 1
        o_ref = refs[pos]
        prod = _dot(a_ref[...].astype(BF16), b_ref[...].astype(BF16), dims)

        def finish(val):
            if has_add:
                val = val + add_ref[...].astype(F32)
            o_ref[...] = val.astype(o_ref.dtype)

        if nk == 1:
            finish(prod)
        else:
            acc_ref = refs[pos + 1]
            k = pl.program_id(2)

            @pl.when(k == 0)
            def _():
                acc_ref[...] = prod

            @pl.when(k > 0)
            def _():
                acc_ref[...] += prod

            @pl.when(k == nk - 1)
            def _():
                finish(acc_ref[...])

    in_specs = [a_spec, b_spec]
    args = [a, b]
    if has_add:
        in_specs.append(add_spec)
        args.append(add)
    scratch = [] if nk == 1 else [pltpu.VMEM(acc_shape, F32)]
    return pl.pallas_call(body, name=name, grid=grid, in_specs=in_specs, out_specs=o_spec, out_shape=out_shape,
                          scratch_shapes=scratch, compiler_params=_params(3))(*args)


def _mm_act_wc(name, a, wc, seg, out_dtype):
    off, w, tn = seg
    t_len, d = a.shape
    tm = _tile(t_len, (1024, 512, 256, 128))
    nps = w // tn
    ob = off // tn
    grid = (t_len // tm, N_CHIPS * nps, 1)
    return _matmul(
        name, a, wc, NN, grid,
        pl.BlockSpec((tm, d), lambda i, j, k: (i, 0)),
        pl.BlockSpec((None, d, tn), lambda i, j, k: (j // nps, 0, ob + j % nps)),
        pl.BlockSpec((tm, tn), lambda i, j, k: (i, j)),
        jax.ShapeDtypeStruct((t_len, N_CHIPS * w), out_dtype), (tm, tn))


def _mm_dact_wcT(name, dy, wc, seg, add=None):
    off, w, tk = seg
    if off == 0 and w <= MAX_K_TILE:
        tk = w
    t_len = dy.shape[0]
    d = wc.shape[1]
    tm = _tile(t_len, (1024, 512, 256, 128))
    tn = _tile(d, (1024, 512, 256, 128))
    kps = w // tk
    ob = off // tk
    grid = (t_len // tm, d // tn, N_CHIPS * kps)
    return _matmul(
        name, dy, wc, NT, grid,
        pl.BlockSpec((tm, tk), lambda i, j, k: (i, k)),
        pl.BlockSpec((None, tn, tk), lambda i, j, k: (k // kps, j, ob + k % kps)),
        pl.BlockSpec((tm, tn), lambda i, j, k: (i, j)),
        jax.ShapeDtypeStruct((t_len, d), F32), (tm, tn),
        add=add, add_spec=None if add is None else pl.BlockSpec((tm, tn), lambda i, j, k: (i, j)))


def _mm_grad_wc(name, a, dy, seg):
    _, w, tn = seg
    t_len, d = a.shape
    tm = _tile(d, (1024, 512, 256, 128))
    tk = _tile(t_len, (2048, 1024, 512, 256, 128))
    nps = w // tn
    grid = (d // tm, N_CHIPS * nps, t_len // tk)
    return _matmul(
        name, a, dy, TN, grid,
        pl.BlockSpec((tk, tm), lambda i, j, k: (k, i)),
        pl.BlockSpec((tk, tn), lambda i, j, k: (k, j)),
        pl.BlockSpec((None, tm, tn), lambda i, j, k: (j // nps, i, j % nps)),
        jax.ShapeDtypeStruct((N_CHIPS, d, w), BF16), (tm, tn))


def _is_plain(wr, seg):
    return seg[0] == 0 and wr.shape[1] == seg[1] and (N_CHIPS * seg[1]) % 1024 == 0


def _mm_act_wr(name, a, wr, seg, add):
    off, r, tk = seg
    t_len = a.shape[0]
    d = wr.shape[2]
    if seg[0] == 0 and wr.shape[1] == r:
        return _mm_plain(name, a, wr.reshape(N_CHIPS * r, d), NN, F32, add=add)
    tm = _tile(t_len, (1024, 512, 256, 128))
    tn = _tile(d, (1024, 512, 256, 128))
    kps = r // tk
    ob = off // tk
    grid = (t_len // tm, d // tn, N_CHIPS * kps)
    return _matmul(
        name, a, wr, NN, grid,
        pl.BlockSpec((tm, tk), lambda i, j, k: (i, k)),
        pl.BlockSpec((None, tk, tn), lambda i, j, k: (k // kps, ob + k % kps, j)),
        pl.BlockSpec((tm, tn), lambda i, j, k: (i, j)),
        jax.ShapeDtypeStruct((t_len, d), F32), (tm, tn),
        add=add, add_spec=pl.BlockSpec((tm, tn), lambda i, j, k: (i, j)))


def _mm_dact_wrT(name, dh, wr, seg):
    off, r, tn = seg
    t_len, d = dh.shape
    if _is_plain(wr, seg):
        return _mm_plain(name, dh, wr.reshape(N_CHIPS * r, d), NT, BF16)
    tm = _tile(t_len, (1024, 512, 256, 128))
    nps = r // tn
    ob = off // tn
    grid = (t_len // tm, N_CHIPS * nps, 1)
    return _matmul(
        name, dh, wr, NT, grid,
        pl.BlockSpec((tm, d), lambda i, j, k: (i, 0)),
        pl.BlockSpec((None, tn, d), lambda i, j, k: (j // nps, ob + j % nps, 0)),
        pl.BlockSpec((tm, tn), lambda i, j, k: (i, j)),
        jax.ShapeDtypeStruct((t_len, N_CHIPS * r), BF16), (tm, tn))


def _mm_grad_wr(name, a, dh, seg):
    _, r, tm = seg
    t_len, d = dh.shape
    if (N_CHIPS * r) % 1024 == 0:
        return _mm_plain(name, a, dh, TN, BF16).reshape(N_CHIPS, r, d)
    tn = _tile(d, (1024, 512, 256, 128))
    tk = _tile(t_len, (2048, 1024, 512, 256, 128))
    mps = r // tm
    grid = (N_CHIPS * mps, d // tn, t_len // tk)
    return _matmul(
        name, a, dh, TN, grid,
        pl.BlockSpec((tk, tm), lambda i, j, k: (k, i)),
        pl.BlockSpec((tk, tn), lambda i, j, k: (k, j)),
        pl.BlockSpec((None, tm, tn), lambda i, j, k: (i // mps, i % mps, j)),
        jax.ShapeDtypeStruct((N_CHIPS, r, d), BF16), (tm, tn))


def _mm_plain(name, a, b, dims, out_dtype, add=None):
    if dims == NN:
        m, kd = a.shape
        n = b.shape[1]
    elif dims == NT:
        m, kd = a.shape
        n = b.shape[0]
    else:
        kd, m = a.shape
        n = b.shape[1]
    tm = _tile(m, (1024, 512, 256, 128))
    tn = _tile(n, (1024, 768, 512, 256, 128))
    tk = _tile(kd, (MAX_K_TILE, 2048, 1408, 1024, 512, 256, 128))
    grid = (m // tm, n // tn, kd // tk)
    if dims == NN:
        a_spec = pl.BlockSpec((tm, tk), lambda i, j, k: (i, k))
        b_spec = pl.BlockSpec((tk, tn), lambda i, j, k: (k, j))
    elif dims == NT:
        a_spec = pl.BlockSpec((tm, tk), lambda i, j, k: (i, k))
        b_spec = pl.BlockSpec((tn, tk), lambda i, j, k: (j, k))
    else:
        a_spec = pl.BlockSpec((tk, tm), lambda i, j, k: (k, i))
        b_spec = pl.BlockSpec((tk, tn), lambda i, j, k: (k, j))
    o_spec = pl.BlockSpec((tm, tn), lambda i, j, k: (i, j))
    return _matmul(name, a, b, dims, grid, a_spec, b_spec, o_spec, jax.ShapeDtypeStruct((m, n), out_dtype), (tm, tn),
                   add=add, add_spec=None if add is None else o_spec)


def _rms_fwd(name, x, g):
    t_len, d = x.shape
    tm = _tile(t_len, (512, 256, 128))

    def body(x_ref, g_ref, o_ref):
        xv = x_ref[...]
        r = lax.rsqrt(jnp.mean(xv * xv, axis=-1, keepdims=True) + EPS)
        o_ref[...] = (xv * r * g_ref[...]).astype(o_ref.dtype)

    return pl.pallas_call(
        body, name=name, grid=(t_len // tm,),
        in_specs=[pl.BlockSpec((tm, d), lambda i: (i, 0)), pl.BlockSpec((1, d), lambda i: (0, 0))],
        out_specs=pl.BlockSpec((tm, d), lambda i: (i, 0)),
        out_shape=jax.ShapeDtypeStruct((t_len, d), BF16), compiler_params=_params(1))(x, g)


def _rms_bwd(name, dy, x, g, dres):
    t_len, d = x.shape
    tm = _tile(t_len, (256, 128))

    def body(dy_ref, x_ref, g_ref, dres_ref, dx_ref, dg_ref):
        xv = x_ref[...]
        r = lax.rsqrt(jnp.mean(xv * xv, axis=-1, keepdims=True) + EPS)
        xhat = xv * r
        dyv = dy_ref[...].astype(F32)
        dxn = dyv * g_ref[...]
        dx = r * (dxn - xhat * jnp.mean(dxn * xhat, axis=-1, keepdims=True))
        dx_ref[...] = dres_ref[...] + dx
        part = jnp.sum(dyv * xhat, axis=0, keepdims=True)

        @pl.when(pl.program_id(0) == 0)
        def _():
            dg_ref[...] = part

        @pl.when(pl.program_id(0) > 0)
        def _():
            dg_ref[...] += part

    row = pl.BlockSpec((tm, d), lambda i: (i, 0))
    vec = pl.BlockSpec((1, d), lambda i: (0, 0))
    return pl.pallas_call(
        body, name=name, grid=(t_len // tm,), in_specs=[row, row, vec, row], out_specs=(row, vec),
        out_shape=(jax.ShapeDtypeStruct((t_len, d), F32), jax.ShapeDtypeStruct((1, d), F32)),
        compiler_params=_params(1))(dy, x, g, dres)


def _loss_head(h, g, target):
    t_len, d = h.shape
    tm = _tile(t_len, (256, 128))

    def body(h_ref, g_ref, t_ref, dh_ref, dg_ref, loss_ref):
        xv = h_ref[...]
        gv = g_ref[...]
        r = lax.rsqrt(jnp.mean(xv * xv, axis=-1, keepdims=True) + EPS)
        xhat = xv * r
        err = xhat * gv - t_ref[...]
        dyv = err * (1.0 / d)
        dxn = dyv * gv
        dh_ref[...] = r * (dxn - xhat * jnp.mean(dxn * xhat, axis=-1, keepdims=True))
        part = jnp.sum(dyv * xhat, axis=0, keepdims=True)
        lpart = jnp.zeros((8, LANE), F32) + (0.5 / d) * jnp.sum(err * err)

        @pl.when(pl.program_id(0) == 0)
        def _():
            dg_ref[...] = part
            loss_ref[...] = lpart

        @pl.when(pl.program_id(0) > 0)
        def _():
            dg_ref[...] += part
            loss_ref[...] += lpart

    row = pl.BlockSpec((tm, d), lambda i: (i, 0))
    vec = pl.BlockSpec((1, d), lambda i: (0, 0))
    return pl.pallas_call(
        body, name="loss_head", grid=(t_len // tm,), in_specs=[row, vec, row],
        out_specs=(row, vec, pl.BlockSpec((8, LANE), lambda i: (0, 0))),
        out_shape=(jax.ShapeDtypeStruct((t_len, d), F32), jax.ShapeDtypeStruct((1, d), F32),
                   jax.ShapeDtypeStruct((8, LANE), F32)),
        compiler_params=_params(1))(h, g, target)


def _chunk_row(shape):
    return lax.broadcasted_iota(jnp.int32, shape, 0) % GLA_CHUNK


def _gla_gate_fwd(a, w_a2p, b_a2):
    t_len = a.shape[0]
    kd = w_a2p.shape[1]
    tm = _tile(t_len, (256, 128, 64))

    def body(a_ref, w_ref, b_ref, ga_ref, cum_ref):
        ga = _dot(a_ref[...], w_ref[...].astype(BF16), NN) + b_ref[...]
        ga_ref[...] = ga
        la = (jnp.minimum(ga, 0.0) - jnp.log(1.0 + jnp.exp(-jnp.abs(ga)))) * (1.0 / GATE_NORMALIZER)
        row = _chunk_row(la.shape)
        s = 1
        while s < GLA_CHUNK:
            la = la + jnp.where(row >= s, pltpu.roll(la, s, 0), 0.0)
            s *= 2
        cum_ref[...] = la

    return pl.pallas_call(
        body, name="gla_gate_fwd", grid=(t_len // tm,),
        in_specs=[pl.BlockSpec((tm, A_PAD), lambda i: (i, 0)), pl.BlockSpec((A_PAD, kd), lambda i: (0, 0)),
                  pl.BlockSpec((1, kd), lambda i: (0, 0))],
        out_specs=(pl.BlockSpec((tm, kd), lambda i: (i, 0)), pl.BlockSpec((tm, kd), lambda i: (i, 0))),
        out_shape=(jax.ShapeDtypeStruct((t_len, kd), F32), jax.ShapeDtypeStruct((t_len, kd), F32)),
        compiler_params=_params(1))(a, w_a2p, b_a2)


def _gla_gate_bwd(dcum, ga, a, w_a2p):
    t_len, kd = dcum.shape
    tm = _tile(t_len, (256, 128, 64))

    def body(dc_ref, ga_ref, a_ref, w_ref, da_ref, dw_ref, db_ref):
        x = dc_ref[...]
        row = _chunk_row(x.shape)
        s = 1
        while s < GLA_CHUNK:
            x = x + jnp.where(row < GLA_CHUNK - s, pltpu.roll(x, tm - s, 0), 0.0)
            s *= 2
        dga = x * (1.0 / GATE_NORMALIZER) * _sigmoid(-ga_ref[...])
        dgab = dga.astype(BF16)
        da_ref[...] = _dot(dgab, w_ref[...].astype(BF16), NT).astype(da_ref.dtype)
        dw = _dot(a_ref[...], dgab, TN)
        db = jnp.sum(dga, axis=0, keepdims=True)

        @pl.when(pl.program_id(0) == 0)
        def _():
            dw_ref[...] = dw
            db_ref[...] = db

        @pl.when(pl.program_id(0) > 0)
        def _():
            dw_ref[...] += dw
            db_ref[...] += db

    wide = pl.BlockSpec((tm, kd), lambda i: (i, 0))
    return pl.pallas_call(
        body, name="gla_gate_bwd", grid=(t_len // tm,),
        in_specs=[wide, wide, pl.BlockSpec((tm, A_PAD), lambda i: (i, 0)), pl.BlockSpec((A_PAD, kd), lambda i: (0, 0))],
        out_specs=(pl.BlockSpec((tm, A_PAD), lambda i: (i, 0)), pl.BlockSpec((A_PAD, kd), lambda i: (0, 0)),
                   pl.BlockSpec((1, kd), lambda i: (0, 0))),
        out_shape=(jax.ShapeDtypeStruct((t_len, A_PAD), BF16), jax.ShapeDtypeStruct((A_PAD, kd), F32),
                   jax.ShapeDtypeStruct((1, kd), F32)),
        compiler_params=_params(1))(dcum, ga, a, w_a2p)


GLA_STEP_CHUNKS = 4


def _gla_dims():
    dk = GLA_KEY_DIM // GLA_HEADS
    dv = GLA_VAL_DIM // GLA_HEADS
    return dk, dv


def _gla_fwd(proj, cum):
    t_len = proj.shape[0]
    dk, dv = _gla_dims()
    nc = t_len // GLA_CHUNK
    c = GLA_CHUNK
    scale = dk ** -0.5
    v0 = 2 * GLA_KEY_DIM // dv

    per = _tile(nc, (GLA_STEP_CHUNKS, 2, 1))
    rows = per * c

    def body(q_ref, k_ref, v_ref, cum_ref, o_ref, st_ref, s_scr):
        @pl.when(pl.program_id(1) == 0)
        def _():
            s_scr[...] = jnp.zeros_like(s_scr)

        tri = lax.broadcasted_iota(jnp.int32, (c, c), 0) >= lax.broadcasted_iota(jnp.int32, (c, c), 1)
        for i in range(per):
            rs = slice(i * c, (i + 1) * c)
            cm = cum_ref[rs, :]
            last = cm[c - 1:c, :]
            q = q_ref[rs, :].astype(F32) * scale
            k = k_ref[rs, :].astype(F32)
            v = v_ref[rs, :].astype(BF16)
            qd = (q * jnp.exp(cm)).astype(BF16)
            ki = (k * jnp.exp(-cm)).astype(BF16)
            ke = (k * jnp.exp(last - cm)).astype(BF16)
            sc = jnp.where(tri, _dot(qd, ki, NT), 0.0)
            st = s_scr[...]
            st_ref[i] = st
            o_ref[rs, :] = _dot(sc.astype(BF16), v, NN) + _dot(qd, st.astype(BF16), NT)
            s_scr[...] = st * jnp.exp(last) + _dot(v, ke, TN)

    return pl.pallas_call(
        body, name="gla_fwd", grid=(GLA_HEADS, nc // per),
        in_specs=[pl.BlockSpec((rows, dk), lambda h, n: (n, h)),
                  pl.BlockSpec((rows, dk), lambda h, n: (n, GLA_HEADS + h)),
                  pl.BlockSpec((rows, dv), lambda h, n: (n, v0 + h)),
                  pl.BlockSpec((rows, dk), lambda h, n: (n, h))],
        out_specs=(pl.BlockSpec((rows, dv), lambda h, n: (n, h)),
                   pl.BlockSpec((None, per, dv, dk), lambda h, n: (h, n, 0, 0))),
        out_shape=(jax.ShapeDtypeStruct((t_len, GLA_VAL_DIM), F32),
                   jax.ShapeDtypeStruct((GLA_HEADS, nc, dv, dk), F32)),
        scratch_shapes=[pltpu.VMEM((dv, dk), F32)], compiler_params=_params(2))(proj, proj, proj, cum)


def _gla_bwd(proj, cum, states, do):
    t_len = proj.shape[0]
    dk, dv = _gla_dims()
    nc = t_len // GLA_CHUNK
    c = GLA_CHUNK
    scale = dk ** -0.5
    v0 = 2 * GLA_KEY_DIM // dv

    per = _tile(nc, (GLA_STEP_CHUNKS, 2, 1))
    rows = per * c

    def body(q_ref, k_ref, v_ref, cum_ref, st_ref, do_ref, dq_ref, dk_ref, dv_ref, dc_ref, ds_scr):
        @pl.when(pl.program_id(1) == 0)
        def _():
            ds_scr[...] = jnp.zeros_like(ds_scr)

        tri = lax.broadcasted_iota(jnp.int32, (c, c), 0) >= lax.broadcasted_iota(jnp.int32, (c, c), 1)
        row = lax.broadcasted_iota(jnp.int32, (c, dk), 0)
        for i in reversed(range(per)):
            rs = slice(i * c, (i + 1) * c)
            cm = cum_ref[rs, :]
            last = cm[c - 1:c, :]
            e_c = jnp.exp(cm)
            e_nc = jnp.exp(-cm)
            e_lc = jnp.exp(last - cm)
            e_l = jnp.exp(last)
            q = q_ref[rs, :].astype(F32) * scale
            k = k_ref[rs, :].astype(F32)
            v = v_ref[rs, :].astype(BF16)
            dov = do_ref[rs, :]
            qd32 = q * e_c
            ki32 = k * e_nc
            ke32 = k * e_lc
            qd = qd32.astype(BF16)
            ki = ki32.astype(BF16)
            ke = ke32.astype(BF16)
            st = st_ref[i]
            dst = ds_scr[...]
            dstb = dst.astype(BF16)
            am = jnp.where(tri, _dot(dov, v, NT), 0.0).astype(BF16)
            pm = jnp.where(tri, _dot(qd, ki, NT), 0.0).astype(BF16)
            dqd = _dot(am, ki, NN) + _dot(dov, st.astype(BF16), NN)
            dki = _dot(am, qd, TN)
            dvv = _dot(pm, dov, TN) + _dot(ke, dstb, NT)
            dke = _dot(v, dstb, NN)
            d_el = jnp.sum(dst * st, axis=0, keepdims=True)
            ds_scr[...] = dst * e_l + _dot(dov, qd, TN)
            dq_ref[rs, :] = (dqd * scale * e_c).astype(dq_ref.dtype)
            dk_ref[rs, :] = (dki * e_nc + dke * e_lc).astype(dk_ref.dtype)
            dv_ref[rs, :] = dvv.astype(dv_ref.dtype)
            dkeke = dke * ke32
            dcum = dqd * qd32 - dki * ki32 - dkeke
            dlast = jnp.sum(dkeke, axis=0, keepdims=True) + d_el * e_l
            dc_ref[rs, :] = jnp.where(row == c - 1, dcum + dlast, dcum)

    rev = nc // per - 1
    return pl.pallas_call(
        body, name="gla_bwd", grid=(GLA_HEADS, nc // per),
        in_specs=[pl.BlockSpec((rows, dk), lambda h, n: (rev - n, h)),
                  pl.BlockSpec((rows, dk), lambda h, n: (rev - n, GLA_HEADS + h)),
                  pl.BlockSpec((rows, dv), lambda h, n: (rev - n, v0 + h)),
                  pl.BlockSpec((rows, dk), lambda h, n: (rev - n, h)),
                  pl.BlockSpec((None, per, dv, dk), lambda h, n: (h, rev - n, 0, 0)),
                  pl.BlockSpec((rows, dv), lambda h, n: (rev - n, h))],
        out_specs=(pl.BlockSpec((rows, dk), lambda h, n: (rev - n, h)),
                   pl.BlockSpec((rows, dk), lambda h, n: (rev - n, h)),
                   pl.BlockSpec((rows, dv), lambda h, n: (rev - n, h)),
                   pl.BlockSpec((rows, dk), lambda h, n: (rev - n, h))),
        out_shape=(jax.ShapeDtypeStruct((t_len, GLA_KEY_DIM), BF16), jax.ShapeDtypeStruct((t_len, GLA_KEY_DIM), BF16),
                   jax.ShapeDtypeStruct((t_len, GLA_VAL_DIM), BF16), jax.ShapeDtypeStruct((t_len, GLA_KEY_DIM), F32)),
        scratch_shapes=[pltpu.VMEM((dv, dk), F32)], compiler_params=_params(2))(proj, proj, proj, cum, states, do)


def _gla_out_fwd(o, proj, gn):
    t_len = o.shape[0]
    _, dv = _gla_dims()
    tm = _tile(t_len, (512, 256, 128))
    r0 = (2 * GLA_KEY_DIM + GLA_VAL_DIM) // dv

    def body(o_ref, r_ref, g_ref, y_ref):
        ov = o_ref[...]
        rs = lax.rsqrt(jnp.mean(ov * ov, axis=-1, keepdims=True) + EPS)
        rv = r_ref[...].astype(F32)
        y_ref[...] = (ov * rs * g_ref[...] * (rv * _sigmoid(rv))).astype(y_ref.dtype)

    return pl.pallas_call(
        body, name="gla_out_fwd", grid=(t_len // tm, GLA_HEADS),
        in_specs=[pl.BlockSpec((tm, dv), lambda i, h: (i, h)), pl.BlockSpec((tm, dv), lambda i, h: (i, r0 + h)),
                  pl.BlockSpec((1, dv), lambda i, h: (0, 0))],
        out_specs=pl.BlockSpec((tm, dv), lambda i, h: (i, h)),
        out_shape=jax.ShapeDtypeStruct((t_len, GLA_VAL_DIM), BF16), compiler_params=_params(2))(o, proj, gn)


def _gla_out_bwd(dy, o, proj, gn):
    t_len = o.shape[0]
    _, dv = _gla_dims()
    tm = _tile(t_len, (512, 256, 128))
    r0 = (2 * GLA_KEY_DIM + GLA_VAL_DIM) // dv

    def body(dy_ref, o_ref, r_ref, g_ref, do_ref, dr_ref, dg_ref):
        ov = o_ref[...]
        gv = g_ref[...]
        rs = lax.rsqrt(jnp.mean(ov * ov, axis=-1, keepdims=True) + EPS)
        xhat = ov * rs
        rv = r_ref[...].astype(F32)
        sg = _sigmoid(rv)
        gate = rv * sg
        dyv = dy_ref[...].astype(F32)
        dn = dyv * gate
        dr_ref[...] = (dyv * xhat * gv * (sg * (1.0 + rv * (1.0 - sg)))).astype(dr_ref.dtype)
        dxn = dn * gv
        do_ref[...] = (rs * (dxn - xhat * jnp.mean(dxn * xhat, axis=-1, keepdims=True))).astype(do_ref.dtype)
        part = jnp.sum(dn * xhat, axis=0, keepdims=True)
        first = (pl.program_id(0) == 0) & (pl.program_id(1) == 0)

        @pl.when(first)
        def _():
            dg_ref[...] = part

        @pl.when(jnp.logical_not(first))
        def _():
            dg_ref[...] += part

    blk = pl.BlockSpec((tm, dv), lambda i, h: (i, h))
    return pl.pallas_call(
        body, name="gla_out_bwd", grid=(t_len // tm, GLA_HEADS),
        in_specs=[blk, blk, pl.BlockSpec((tm, dv), lambda i, h: (i, r0 + h)), pl.BlockSpec((1, dv), lambda i, h: (0, 0))],
        out_specs=(blk, blk, pl.BlockSpec((1, dv), lambda i, h: (0, 0))),
        out_shape=(jax.ShapeDtypeStruct((t_len, GLA_VAL_DIM), BF16), jax.ShapeDtypeStruct((t_len, GLA_VAL_DIM), BF16),
                   jax.ShapeDtypeStruct((1, dv), F32)),
        compiler_params=_params(2))(dy, o, proj, gn)


def _alibi_slopes():
    n = ATT_HEADS
    start = 2.0 ** (-8.0 / n)
    return [start ** (i + 1) for i in range(n)]


def _att_masks(d):
    b = ATT_BLOCK
    qa = lax.broadcasted_iota(jnp.int32, (b, b), 0)
    kb = lax.broadcasted_iota(jnp.int32, (b, b), 1)
    dist_c = qa - kb
    dist_p = qa - kb + b
    return dist_c >= 0, dist_p <= b, (dist_c * d).astype(F32), (dist_p * d).astype(F32)


def _to_dilated(name, x, d, c0=0, w=None):
    part = x if w is None else x[:, c0:c0 + w]
    return part.reshape(x.shape[0] // d, -1)


def _from_dilated(name, y, d):
    return y.reshape(y.shape[0] * d, y.shape[1] // d)


def _att_views(q_all, kv, g):
    d = DILATIONS[g]
    hd = ATT_HEADS * HEAD_DIM
    if d == 1:
        return q_all, kv
    return _to_dilated(f"q_dilated{g}", q_all, d, g * hd, hd), _to_dilated(f"kv_dilated{g}", kv, d)


def _att_fwd(views, g):
    d = DILATIONS[g]
    assert WINDOWS[g] // d == ATT_BLOCK
    qv, kvv = views
    hd = ATT_HEADS * HEAD_DIM
    sub = kvv.shape[0]
    t_len = sub * d
    nb = sub // ATT_BLOCK
    b = ATT_BLOCK
    e = HEAD_DIM
    scale = e ** -0.5
    slopes = _alibi_slopes()
    qc = (lambda r: 3 * r + g) if d == 1 else (lambda r: r)

    def body(q_ref, kp_ref, kc_ref, vp_ref, vc_ref, o_ref, l_ref, s_scr, p_scr, li_scr):
        ib = pl.program_id(1)
        valid_c, valid_p0, dist_c, dist_p = _att_masks(d)
        valid_p = valid_p0 & (ib > 0)
        for h in range(ATT_HEADS):
            hs = slice(h * e, (h + 1) * e)
            qh = q_ref[:, hs]
            s_scr[h, 0] = _dot(qh, kc_ref[:, hs], NT)
            s_scr[h, 1] = _dot(qh, kp_ref[:, hs], NT)
        l_ref[...] = jnp.zeros_like(l_ref)
        for h in range(ATT_HEADS):
            s_c = jnp.where(valid_c, s_scr[h, 0] * scale - slopes[h] * dist_c, NEG)
            s_p = jnp.where(valid_p, s_scr[h, 1] * scale - slopes[h] * dist_p, NEG)
            m = jnp.maximum(jnp.max(s_c, axis=1, keepdims=True), jnp.max(s_p, axis=1, keepdims=True))
            p_c = jnp.where(valid_c, jnp.exp(s_c - m), 0.0)
            p_p = jnp.where(valid_p, jnp.exp(s_p - m), 0.0)
            l = jnp.sum(p_c, axis=1, keepdims=True) + jnp.sum(p_p, axis=1, keepdims=True)
            p_scr[h, 0] = p_c.astype(BF16)
            p_scr[h, 1] = p_p.astype(BF16)
            li_scr[:, h:h + 1] = 1.0 / l
            l_ref[:, h:h + 1] = m + jnp.log(l)
        for h in range(ATT_HEADS):
            hs = slice(h * e, (h + 1) * e)
            acc = _dot(p_scr[h, 0], vc_ref[:, hs], NN) + _dot(p_scr[h, 1], vp_ref[:, hs], NN)
            o_ref[:, hs] = acc * li_scr[:, h:h + 1]

    blk = (b, hd)
    cblk = (b, LANE)
    o, lse = pl.pallas_call(
        body, name=f"att_fwd{g}", grid=(d, nb),
        scratch_shapes=[pltpu.VMEM((ATT_HEADS, 2, b, b), F32), pltpu.VMEM((ATT_HEADS, 2, b, b), BF16),
                        pltpu.VMEM((b, LANE), F32)],
        in_specs=[pl.BlockSpec(blk, lambda r, i: (i, qc(r))),
                  pl.BlockSpec(blk, lambda r, i: (jnp.maximum(i - 1, 0), 2 * r)),
                  pl.BlockSpec(blk, lambda r, i: (i, 2 * r)),
                  pl.BlockSpec(blk, lambda r, i: (jnp.maximum(i - 1, 0), 2 * r + 1)),
                  pl.BlockSpec(blk, lambda r, i: (i, 2 * r + 1))],
        out_specs=(pl.BlockSpec(blk, lambda r, i: (i, r)), pl.BlockSpec(cblk, lambda r, i: (i, r))),
        out_shape=(jax.ShapeDtypeStruct((sub, d * hd), F32), jax.ShapeDtypeStruct((sub, d * LANE), F32)),
        compiler_params=_params(2))(qv, kvv, kvv, kvv, kvv)
    return _from_dilated(f"o_natural{g}", o, d), lse.reshape(t_len, LANE)


def _att_merge(os, ls):
    t_len, hd = os[0].shape
    tm = _tile(t_len, (256, 128))
    e = HEAD_DIM

    def body(o0, o1, o2, l0, l1, l2, of_ref, ob_ref, l_ref):
        a0, a1, a2 = l0[...], l1[...], l2[...]
        m = jnp.maximum(jnp.maximum(a0, a1), a2)
        e0, e1, e2 = jnp.exp(a0 - m), jnp.exp(a1 - m), jnp.exp(a2 - m)
        den = e0 + e1 + e2
        w0, w1, w2 = e0 / den, e1 / den, e2 / den
        l_ref[...] = m + jnp.log(den)
        for h in range(ATT_HEADS):
            hs = slice(h * e, (h + 1) * e)
            c = slice(h, h + 1)
            o = w0[:, c] * o0[:, hs] + w1[:, c] * o1[:, hs] + w2[:, c] * o2[:, hs]
            of_ref[:, hs] = o
            ob_ref[:, hs] = o.astype(ob_ref.dtype)

    row = pl.BlockSpec((tm, hd), lambda i: (i, 0))
    crow = pl.BlockSpec((tm, LANE), lambda i: (i, 0))
    return pl.pallas_call(
        body, name="att_merge", grid=(t_len // tm,), in_specs=[row] * 3 + [crow] * 3, out_specs=(row, row, crow),
        out_shape=(jax.ShapeDtypeStruct((t_len, hd), F32), jax.ShapeDtypeStruct((t_len, hd), BF16),
                   jax.ShapeDtypeStruct((t_len, LANE), F32)),
        compiler_params=_params(1))(*os, *ls)


def _att_delta(do, o):
    t_len, hd = o.shape
    tm = _tile(t_len, (256, 128))
    e = HEAD_DIM

    def body(do_ref, o_ref, d_ref):
        d_ref[...] = jnp.zeros_like(d_ref)
        for h in range(ATT_HEADS):
            hs = slice(h * e, (h + 1) * e)
            d_ref[:, h:h + 1] = jnp.sum(do_ref[:, hs].astype(F32) * o_ref[:, hs], axis=1, keepdims=True)

    row = pl.BlockSpec((tm, hd), lambda i: (i, 0))
    return pl.pallas_call(
        body, name="att_delta", grid=(t_len // tm,), in_specs=[row, row],
        out_specs=pl.BlockSpec((tm, LANE), lambda i: (i, 0)),
        out_shape=jax.ShapeDtypeStruct((t_len, LANE), F32), compiler_params=_params(1))(do, o)


def _att_bwd(views, delta, lse, do, g):
    d = DILATIONS[g]
    qv, kvv = views
    hd = ATT_HEADS * HEAD_DIM
    sub = kvv.shape[0]
    t_len = sub * d
    nb = sub // ATT_BLOCK
    b = ATT_BLOCK
    e = HEAD_DIM
    scale = e ** -0.5
    slopes = _alibi_slopes()
    qc = (lambda r: 3 * r + g) if d == 1 else (lambda r: r)
    dlv = delta.reshape(sub, d * LANE)
    lv = lse.reshape(sub, d * LANE)
    dov = do if d == 1 else _to_dilated(f"do_dilated{g}", do, d)

    def body(qj_ref, qn_ref, kp_ref, kc_ref, vp_ref, vc_ref, doj_ref, don_ref, dj_ref, dn_ref, lj_ref, ln_ref,
             dq_ref, dk_ref, dv_ref, s_scr, dp_scr, p_scr, ds_scr):
        j = pl.program_id(1)
        valid_c, valid_p0, dist_c, dist_p = _att_masks(d)
        valid = (valid_c, valid_p0 & (j > 0), valid_p0 & (j + 1 < nb))
        dist = (dist_c, dist_p, dist_p)
        for h in range(ATT_HEADS):
            hs = slice(h * e, (h + 1) * e)
            qj, qn = qj_ref[:, hs], qn_ref[:, hs]
            kc, kp = kc_ref[:, hs], kp_ref[:, hs]
            vc, vp = vc_ref[:, hs], vp_ref[:, hs]
            doj, don = doj_ref[:, hs], don_ref[:, hs]
            s_scr[h, 0] = _dot(qj, kc, NT)
            s_scr[h, 1] = _dot(qj, kp, NT)
            s_scr[h, 2] = _dot(qn, kc, NT)
            dp_scr[h, 0] = _dot(doj, vc, NT)
            dp_scr[h, 1] = _dot(doj, vp, NT)
            dp_scr[h, 2] = _dot(don, vc, NT)
        for h in range(ATT_HEADS):
            c = slice(h, h + 1)
            lse_t = (lj_ref[:, c], lj_ref[:, c], ln_ref[:, c])
            dlt_t = (dj_ref[:, c], dj_ref[:, c], dn_ref[:, c])
            for t in range(3):
                s = s_scr[h, t] * scale - slopes[h] * dist[t]
                p = jnp.where(valid[t], jnp.exp(jnp.where(valid[t], s - lse_t[t], NEG)), 0.0)
                p_scr[h, t] = p.astype(BF16)
                ds_scr[h, t] = (p * (dp_scr[h, t] - dlt_t[t])).astype(BF16)
        for h in range(ATT_HEADS):
            hs = slice(h * e, (h + 1) * e)
            dq = _dot(ds_scr[h, 0], kc_ref[:, hs], NN) + _dot(ds_scr[h, 1], kp_ref[:, hs], NN)
            dk = _dot(ds_scr[h, 0], qj_ref[:, hs], TN) + _dot(ds_scr[h, 2], qn_ref[:, hs], TN)
            dv = _dot(p_scr[h, 0], doj_ref[:, hs], TN) + _dot(p_scr[h, 2], don_ref[:, hs], TN)
            dq_ref[:, hs] = (dq * scale).astype(dq_ref.dtype)
            dk_ref[:, hs] = (dk * scale).astype(dk_ref.dtype)
            dv_ref[:, hs] = dv.astype(dv_ref.dtype)

    blk = (b, hd)
    cblk = (b, LANE)
    nxt = lambda i: jnp.minimum(i + 1, nb - 1)
    prv = lambda i: jnp.maximum(i - 1, 0)
    tiles = (ATT_HEADS, 3, b, b)
    dq, dk, dv = pl.pallas_call(
        body, name=f"att_bwd{g}", grid=(d, nb),
        scratch_shapes=[pltpu.VMEM(tiles, F32), pltpu.VMEM(tiles, F32), pltpu.VMEM(tiles, BF16), pltpu.VMEM(tiles, BF16)],
        in_specs=[pl.BlockSpec(blk, lambda r, i: (i, qc(r))),
                  pl.BlockSpec(blk, lambda r, i: (nxt(i), qc(r))),
                  pl.BlockSpec(blk, lambda r, i: (prv(i), 2 * r)),
                  pl.BlockSpec(blk, lambda r, i: (i, 2 * r)),
                  pl.BlockSpec(blk, lambda r, i: (prv(i), 2 * r + 1)),
                  pl.BlockSpec(blk, lambda r, i: (i, 2 * r + 1)),
                  pl.BlockSpec(blk, lambda r, i: (i, r)),
                  pl.BlockSpec(blk, lambda r, i: (nxt(i), r)),
                  pl.BlockSpec(cblk, lambda r, i: (i, r)),
                  pl.BlockSpec(cblk, lambda r, i: (nxt(i), r)),
                  pl.BlockSpec(cblk, lambda r, i: (i, r)),
                  pl.BlockSpec(cblk, lambda r, i: (nxt(i), r))],
        out_specs=(pl.BlockSpec(blk, lambda r, i: (i, r)),) * 3,
        out_shape=(jax.ShapeDtypeStruct((sub, d * hd), BF16),) * 3,
        compiler_params=_params(2))(qv, qv, kvv, kvv, kvv, kvv, dov, dov, dlv, dlv, lv, lv)
    return tuple(_from_dilated(f"{n}_natural{g}", t, d) for n, t in (("dq", dq), ("dk", dk), ("dv", dv)))


def _kv_grad_sum(dks, dvs):
    t_len, hd = dks[0].shape
    tm = _tile(t_len, (256, 128))

    def body(k0, k1, k2, v0, v1, v2, o_ref):
        o_ref[:, :hd] = (k0[...].astype(F32) + k1[...].astype(F32) + k2[...].astype(F32)).astype(o_ref.dtype)
        o_ref[:, hd:] = (v0[...].astype(F32) + v1[...].astype(F32) + v2[...].astype(F32)).astype(o_ref.dtype)

    row = pl.BlockSpec((tm, hd), lambda i: (i, 0))
    return pl.pallas_call(
        body, name="kv_grad_sum", grid=(t_len // tm,), in_specs=[row] * 6,
        out_specs=pl.BlockSpec((tm, 2 * hd), lambda i: (i, 0)),
        out_shape=jax.ShapeDtypeStruct((t_len, 2 * hd), BF16), compiler_params=_params(1))(*dks, *dvs)


HALO = 16
INV_SQRT2 = 1.0 / math.sqrt(2.0)
INV_SQRT2PI = 1.0 / math.sqrt(2.0 * math.pi)


def _conv_taps(g, halo, cw, cb):
    row = lax.broadcasted_iota(jnp.int32, g.shape, 0)
    h1 = halo[HALO - 1:HALO, :]
    h2 = halo[HALO - 2:HALO - 1, :]
    g1 = jnp.where(row == 0, h1, pltpu.roll(g, 1, 0))
    g2 = jnp.where(row == 0, h2, jnp.where(row == 1, h1, pltpu.roll(g, 2, 0)))
    gc = cw[0:1, :] * g2 + cw[1:2, :] * g1 + cw[2:3, :] * g + cb
    return gc, g1, g2


def _glu_specs(t_len, f, tm, tc):
    nj = f // tc
    hb = tm // HALO
    u = pl.BlockSpec((tm, tc), lambda j, i: (i, j))
    g = pl.BlockSpec((tm, tc), lambda j, i: (i, nj + j))
    gh = pl.BlockSpec((HALO, tc), lambda j, i: (jnp.maximum(i * hb - 1, 0), nj + j))
    cw = pl.BlockSpec((8, tc), lambda j, i: (0, j))
    cb = pl.BlockSpec((1, tc), lambda j, i: (0, j))
    return u, g, gh, cw, cb


def _glu_fwd(name, up, cw, cb):
    t_len = up.shape[0]
    f = up.shape[1] // 2
    tm = _tile(t_len, (512, 256, 128))
    tc = _tile(f, (1408, 1024, 512, 256, 128))
    u_s, g_s, gh_s, cw_s, cb_s = _glu_specs(t_len, f, tm, tc)

    def body(u_ref, g_ref, gh_ref, cw_ref, cb_ref, o_ref):
        first = pl.program_id(1) == 0
        halo = jnp.where(first, 0.0, gh_ref[...].astype(F32))
        gc, _, _ = _conv_taps(g_ref[...].astype(F32), halo, cw_ref[...], cb_ref[...])
        gel = 0.5 * gc * (1.0 + lax.erf(gc * INV_SQRT2))
        o_ref[...] = (gel * u_ref[...].astype(F32)).astype(o_ref.dtype)

    return pl.pallas_call(
        body, name=name, grid=(f // tc, t_len // tm), in_specs=[u_s, g_s, gh_s, cw_s, cb_s],
        out_specs=pl.BlockSpec((tm, tc), lambda j, i: (i, j)),
        out_shape=jax.ShapeDtypeStruct((t_len, f), BF16), compiler_params=_params(2))(up, up, up, cw, cb)


def _glu_bwd_a(name, dact, up, cw, cb):
    t_len = up.shape[0]
    f = up.shape[1] // 2
    tm = _tile(t_len, (256, 128))
    tc = _tile(f, (1408, 1024, 512, 256, 128))
    u_s, g_s, gh_s, cw_s, cb_s = _glu_specs(t_len, f, tm, tc)

    def body(da_ref, u_ref, g_ref, gh_ref, cw_ref, cb_ref, du_ref, dgc_ref, w0_ref, w1_ref, w2_ref, b_ref):
        first = pl.program_id(1) == 0
        halo = jnp.where(first, 0.0, gh_ref[...].astype(F32))
        g = g_ref[...].astype(F32)
        gc, g1, g2 = _conv_taps(g, halo, cw_ref[...], cb_ref[...])
        phi = 0.5 * (1.0 + lax.erf(gc * INV_SQRT2))
        dgel = phi + gc * jnp.exp(-0.5 * gc * gc) * INV_SQRT2PI
        da = da_ref[...].astype(F32)
        du_ref[...] = (da * gc * phi).astype(du_ref.dtype)
        dgc = da * u_ref[...].astype(F32) * dgel
        dgc_ref[...] = dgc.astype(dgc_ref.dtype)
        parts = (jnp.sum(dgc * g2, axis=0, keepdims=True), jnp.sum(dgc * g1, axis=0, keepdims=True),
                 jnp.sum(dgc * g, axis=0, keepdims=True), jnp.sum(dgc, axis=0, keepdims=True))
        refs = (w0_ref, w1_ref, w2_ref, b_ref)

        @pl.when(first)
        def _():
            for r, p in zip(refs, parts):
                r[...] = p

        @pl.when(jnp.logical_not(first))
        def _():
            for r, p in zip(refs, parts):
                r[...] += p

    tile = pl.BlockSpec((tm, tc), lambda j, i: (i, j))
    vec = pl.BlockSpec((1, tc), lambda j, i: (0, j))
    vshape = jax.ShapeDtypeStruct((1, f), F32)
    return pl.pallas_call(
        body, name=name, grid=(f // tc, t_len // tm), in_specs=[tile, u_s, g_s, gh_s, cw_s, cb_s],
        out_specs=(tile, tile, vec, vec, vec, vec),
        out_shape=(jax.ShapeDtypeStruct((t_len, f), BF16), jax.ShapeDtypeStruct((t_len, f), BF16),
                   vshape, vshape, vshape, vshape),
        compiler_params=_params(2))(dact, up, up, up, cw, cb)


def _glu_bwd_b(name, du, dgc, cw):
    t_len, f = du.shape
    tm = _tile(t_len, (128, 64))
    hb = tm // HALO
    n_i = t_len // tm
    last_hb = t_len // HALO - 1

    def body(du_ref, d_ref, dh_ref, cw_ref, o_ref):
        last = pl.program_id(0) == n_i - 1
        halo = jnp.where(last, 0.0, dh_ref[...].astype(F32))
        dd = d_ref[...].astype(F32)
        row = lax.broadcasted_iota(jnp.int32, dd.shape, 0)
        h0 = halo[0:1, :]
        h1 = halo[1:2, :]
        d1 = jnp.where(row == tm - 1, h0, pltpu.roll(dd, tm - 1, 0))
        d2 = jnp.where(row == tm - 1, h1, jnp.where(row == tm - 2, h0, pltpu.roll(dd, tm - 2, 0)))
        cwv = cw_ref[...]
        dg = cwv[2:3, :] * dd + cwv[1:2, :] * d1 + cwv[0:1, :] * d2
        o_ref[:, :f] = du_ref[...]
        o_ref[:, f:] = dg.astype(o_ref.dtype)

    row_s = pl.BlockSpec((tm, f), lambda i: (i, 0))
    return pl.pallas_call(
        body, name=name, grid=(n_i,),
        in_specs=[row_s, row_s, pl.BlockSpec((HALO, f), lambda i: (jnp.minimum((i + 1) * hb, last_hb), 0)),
                  pl.BlockSpec((8, f), lambda i: (0, 0))],
        out_specs=pl.BlockSpec((tm, 2 * f), lambda i: (i, 0)),
        out_shape=jax.ShapeDtypeStruct((t_len, 2 * f), BF16), compiler_params=_params(1))(du, dgc, dgc, cw)


def _adamw(name, w, g, m, v):
    rows, cols = w.shape
    gcols = g.shape[1]
    n_out = 3 if gcols == cols else 4
    tr = _tile(rows, (256, 128, 64, 32, 16, 8))
    c1 = 1.0 / (1.0 - ADAM_B1 ** ADAM_STEP)
    c2 = 1.0 / (1.0 - ADAM_B2 ** ADAM_STEP)

    def body(w_ref, g_ref, m_ref, v_ref, d_ref, nm_ref, nv_ref, *g_out):
        gv = g_ref[...][:, :cols]
        nm = ADAM_B1 * m_ref[...] + (1.0 - ADAM_B1) * gv
        nv = ADAM_B2 * v_ref[...] + (1.0 - ADAM_B2) * (gv * gv)
        nm_ref[...] = nm
        nv_ref[...] = nv
        d_ref[...] = -ADAM_LR * ((nm * c1) / (jnp.sqrt(nv * c2) + ADAM_EPS) + ADAM_WD * w_ref[...])
        for ref in g_out:
            ref[...] = gv

    blk = pl.BlockSpec((tr, cols), lambda i: (i, 0))
    gblk = pl.BlockSpec((tr, gcols), lambda i: (i, 0))
    shp = jax.ShapeDtypeStruct((rows, cols), F32)
    return pl.pallas_call(body, name=name, grid=(rows // tr,), in_specs=[blk, gblk, blk, blk], out_specs=(blk,) * n_out,
                          out_shape=(shp,) * n_out, compiler_params=_params(1))(w, g, m, v)


def _adamw_layers(name, w, gs, m, v):
    _, rows, cols = w.shape
    tr = _tile(rows, (128, 64, 32, 16, 8))
    nr = rows // tr
    c1 = 1.0 / (1.0 - ADAM_B1 ** ADAM_STEP)
    c2 = 1.0 / (1.0 - ADAM_B2 ** ADAM_STEP)

    def body(w_ref, g0_ref, g1_ref, m_ref, v_ref, d_ref, nm_ref, nv_ref, g_ref):
        gv = jnp.where(pl.program_id(0) == 0, g0_ref[...], g1_ref[...])
        nm = ADAM_B1 * m_ref[...] + (1.0 - ADAM_B1) * gv
        nv = ADAM_B2 * v_ref[...] + (1.0 - ADAM_B2) * (gv * gv)
        nm_ref[...] = nm
        nv_ref[...] = nv
        d_ref[...] = -ADAM_LR * ((nm * c1) / (jnp.sqrt(nv * c2) + ADAM_EPS) + ADAM_WD * w_ref[...])
        g_ref[...] = gv

    blk = pl.BlockSpec((None, tr, cols), lambda l, i: (l, i, 0))
    g0_blk = pl.BlockSpec((tr, cols), lambda l, i: (jnp.where(l == 0, i, nr - 1), 0))
    g1_blk = pl.BlockSpec((tr, cols), lambda l, i: (jnp.where(l == 0, 0, i), 0))
    shp = jax.ShapeDtypeStruct(w.shape, F32)
    return pl.pallas_call(body, name=name, grid=(2, nr), in_specs=[blk, g0_blk, g1_blk, blk, blk], out_specs=(blk,) * 4,
                          out_shape=(shp,) * 4, compiler_params=_params(2))(w, gs[0], gs[1], m, v)


class _NoComm:
    def __init__(self):
        self.grads = {}

    def prefetch(self, group, ws, carry):
        return ws, carry

    def need(self, group, ws, after):
        return ws

    def reduce(self, group, grads, carry):
        self.grads.update(grads)
        return carry

    def tick(self, carry):
        return carry


def _local_step(x, target, ws, norms, small, hooks):
    lay = _layout()

    w_main, w_a = _unpack_gin(ws["gin"])
    hn0 = _rms_fwd("rms_attn0", x, norms["attn0"])
    proj = _mm_plain("gla_proj", hn0, w_main, NN, F32)
    a = _mm_plain("gla_proj_a", hn0, w_a, NN, BF16)
    ga, cum = _gla_gate_fwd(a, small["w_a2p"], small["b_a2"])
    ws, cum = hooks.prefetch("B0", ws, cum)
    o_gla, states = _gla_fwd(proj, cum)
    gated = _gla_out_fwd(o_gla, proj, small["head_norm"])
    ws = hooks.need("B0", ws, gated)
    ws, gated = hooks.prefetch("B1", ws, gated)
    h1 = _mm_act_wr("gla_out", gated, ws["gout"], lay["gout"], add=x)
    ws = hooks.need("B1", ws, h1)

    def ffn_fwd(l, h, own=None, prefetch=None):
        nonlocal ws
        hn = _rms_fwd(f"rms_ffn{l}", h, norms[f"ffn{l}"])
        if own is not None:
            ws, hn = hooks.prefetch(own, ws, hn)
        up = _mm_act_wc(f"ffn_up{l}", hn, ws[f"up{l}"], lay[f"up{l}"], BF16)
        act = _glu_fwd(f"glu_fwd{l}", up, small["conv_w"][l], small["conv_b"][l])
        if own is not None:
            ws = hooks.need(own, ws, act)
        if prefetch is not None:
            ws, act = hooks.prefetch(prefetch, ws, act)
        return hn, up, act, _mm_act_wr(f"ffn_down{l}", act, ws[f"down{l}"], lay[f"down{l}"], add=h)

    hnf0, up0, act0, h2 = ffn_fwd(0, h1, own="B2", prefetch="C1")

    ws = hooks.need("C1", ws, h2)
    kvn = _rms_fwd("rms_kv", h2, norms["kv"])
    kv = _mm_act_wc("kv_proj", kvn, ws["wkv"], lay["wkv"], BF16)
    hn1 = _rms_fwd("rms_attn1", h2, norms["attn1"])
    q_all = _mm_act_wc("q_proj", hn1, ws["wq"], lay["wq"], BF16)
    views = [_att_views(q_all, kv, g) for g in range(3)]
    branch = [_att_fwd(views[g], g) for g in range(3)]
    ws, lse2 = hooks.prefetch("C2", ws, branch[-1][1])
    o_att, o_att_b, lse = _att_merge([br[0] for br in branch], [br[1] for br in branch[:-1]] + [lse2])
    h3 = _mm_act_wr("att_out", o_att_b, ws["dout"], lay["dout"], add=h2)
    ws = hooks.need("C2", ws, h3)
    hnf1, up1, act1, h4 = ffn_fwd(1, h3)

    dh4, d_final, loss = _loss_head(h4, norms["final"], target)

    sm = {"final": d_final}

    def ffn_bwd(l, dh, h, hn, up, act):
        big = {}
        dact = _mm_dact_wrT(f"ffn_down_dx{l}", dh, ws[f"down{l}"], lay[f"down{l}"])
        big[f"down{l}"] = _mm_grad_wr(f"ffn_down_dw{l}", act, dh, lay[f"down{l}"])
        du, dgc, w0, w1, w2, db = _glu_bwd_a(f"glu_bwd_a{l}", dact, up, small["conv_w"][l], small["conv_b"][l])
        sm[f"conv_w{l}"] = (w0, w1, w2)
        sm[f"conv_b{l}"] = db
        dup = hooks.tick(_glu_bwd_b(f"glu_bwd_b{l}", du, dgc, small["conv_w"][l]))
        dhn = _mm_dact_wcT(f"ffn_up_dx{l}", dup, ws[f"up{l}"], lay[f"up{l}"])
        big[f"up{l}"] = _mm_grad_wc(f"ffn_up_dw{l}", hn, dup, lay[f"up{l}"])
        dh_in, sm[f"ffn{l}"] = _rms_bwd(f"rms_ffn_bwd{l}", dhn, h, norms[f"ffn{l}"], dh)
        return hooks.reduce(f"ffn{l}", big, dh_in)

    dh3 = ffn_bwd(1, dh4, h3, hnf1, up1, act1)

    big = {}
    do_att = _mm_dact_wrT("att_out_dx", dh3, ws["dout"], lay["dout"])
    big["dout"] = _mm_grad_wr("att_out_dw", o_att_b, dh3, lay["dout"])
    delta = _att_delta(do_att, o_att)
    bw = [_att_bwd(views[g], delta, lse, do_att, g) for g in range(3)]
    dq_all = jnp.concatenate([t[0] for t in bw], axis=1)
    dhn1 = _mm_dact_wcT("q_proj_dx", dq_all, ws["wq"], lay["wq"])
    big["wq"] = _mm_grad_wc("q_proj_dw", hn1, dq_all, lay["wq"])
    dh2, sm["attn1"] = _rms_bwd("rms_attn1_bwd", dhn1, h2, norms["attn1"], dh3)
    dkv = hooks.tick(_kv_grad_sum([t[1] for t in bw], [t[2] for t in bw]))
    dkvn = _mm_dact_wcT("kv_proj_dx", dkv, ws["wkv"], lay["wkv"])
    big["wkv"] = _mm_grad_wc("kv_proj_dw", kvn, dkv, lay["wkv"])
    dh2, sm["kv"] = _rms_bwd("rms_kv_bwd", dkvn, h2, norms["kv"], dh2)
    dh2 = hooks.reduce("att", big, dh2)

    dh1 = ffn_bwd(0, dh2, h1, hnf0, up0, act0)

    big = {}
    dgated = _mm_dact_wrT("gla_out_dx", dh1, ws["gout"], lay["gout"])
    big["gout"] = _mm_grad_wr("gla_out_dw", gated, dh1, lay["gout"])
    do_gla, dr, sm["head_norm"] = _gla_out_bwd(dgated, o_gla, proj, small["head_norm"])
    dq, dk, dv, dcum = _gla_bwd(proj, cum, states, hooks.tick(do_gla))
    da, sm["w_a2p"], sm["b_a2"] = _gla_gate_bwd(dcum, ga, a, small["w_a2p"])
    dproj = jnp.concatenate([dq, dk, dv, dr], axis=1)
    dhn0 = _mm_plain("gla_proj_dx", dproj, w_main, NT, F32)
    dhn0 = _mm_plain("gla_proj_a_dx", da, w_a, NT, F32, add=dhn0)
    gin_main = _mm_plain("gla_proj_dw", hn0, dproj, TN, BF16)
    gin_a = _mm_plain("gla_proj_a_dw", hn0, da, TN, BF16)
    big["gin"] = _pack_gin_grad(gin_main, gin_a)
    grad_x, sm["attn0"] = _rms_bwd("rms_attn0_bwd", dhn0, x, norms["attn0"], dh1)
    return loss, grad_x, sm, big


def _pack_weights(chip, names, gla_w_in, gla_w_out, w_kv, dsa_w_q, dsa_w_out, ffn_w_up, ffn_w_down):
    gin = gla_w_in[0]
    gin = jnp.pad(gin, ((0, 0), (0, _roundup(gin.shape[1], LANE) - gin.shape[1])))
    shards = {"gin": gin, "gout": gla_w_out[0], "up0": ffn_w_up[0], "up1": ffn_w_up[1], "down0": ffn_w_down[0],
              "down1": ffn_w_down[1], "wq": dsa_w_q[0], "wkv": w_kv, "dout": dsa_w_out[0]}
    out = {}
    for name in names:
        w = shards[name]
        buf = jnp.zeros((N_CHIPS,) + w.shape, BF16)
        out[name] = lax.dynamic_update_slice(buf, w.astype(BF16)[None], (chip, 0, 0))
    return out


def _unpack_gin(w_gin):
    w = _layout()["gin"][1]
    d = w_gin.shape[1]
    wp = w_gin.shape[2]
    n_main = 2 * GLA_KEY_DIM + 2 * GLA_VAL_DIM
    tm = _tile(d, (256, 128, 64, 32, 16))

    def body(s_ref, main_ref, a_ref):
        full = jnp.concatenate([s_ref[s][:, :w] for s in range(N_CHIPS)], axis=1)
        main_ref[...] = full[:, :n_main]
        a_ref[...] = jnp.concatenate([full[:, n_main:], jnp.zeros((tm, A_PAD - GATE_RANK), full.dtype)], axis=1)

    return pl.pallas_call(
        body, name="unpack_gin", grid=(d // tm,), in_specs=[pl.BlockSpec((N_CHIPS, tm, wp), lambda i: (0, i, 0))],
        out_specs=(pl.BlockSpec((tm, n_main), lambda i: (i, 0)), pl.BlockSpec((tm, A_PAD), lambda i: (i, 0))),
        out_shape=(jax.ShapeDtypeStruct((d, n_main), w_gin.dtype), jax.ShapeDtypeStruct((d, A_PAD), w_gin.dtype)),
        compiler_params=_params(1))(w_gin)


def _pack_gin_grad(gin_main, gin_a):
    w = _layout()["gin"][1]
    wp = _roundup(w, LANE)
    d, n_main = gin_main.shape
    tm = _tile(d, (256, 128, 64, 32, 16))

    def body(main_ref, a_ref, o_ref):
        full = jnp.concatenate([main_ref[...], a_ref[:, :GATE_RANK]], axis=1)
        fill = jnp.zeros((tm, wp - w), full.dtype)
        for s in range(N_CHIPS):
            o_ref[s] = jnp.concatenate([full[:, s * w:(s + 1) * w], fill], axis=1)

    return pl.pallas_call(
        body, name="pack_gin_grad", grid=(d // tm,),
        in_specs=[pl.BlockSpec((tm, n_main), lambda i: (i, 0)), pl.BlockSpec((tm, A_PAD), lambda i: (i, 0))],
        out_specs=pl.BlockSpec((N_CHIPS, tm, wp), lambda i: (0, i, 0)),
        out_shape=jax.ShapeDtypeStruct((N_CHIPS, d, wp), gin_main.dtype), compiler_params=_params(1))(gin_main, gin_a)


def _small_params(attn_norm, ffn_norm, kv_norm, final_norm, conv_b, w_a2, b_a2, head_norm, conv_w):
    norms = {"attn0": attn_norm[0:1], "attn1": attn_norm[1:2], "ffn0": ffn_norm[0:1], "ffn1": ffn_norm[1:2],
             "kv": kv_norm[None, :], "final": final_norm[None, :]}
    small = {"w_a2p": jnp.pad(w_a2, ((0, A_PAD - GATE_RANK), (0, 0))), "b_a2": b_a2[None, :],
             "head_norm": head_norm[None, :], "conv_w": jnp.pad(conv_w, ((0, 0), (0, 8 - conv_w.shape[1]), (0, 0))),
             "conv_b": conv_b[:, None, :]}
    return norms, small


ANY = pl.BlockSpec(memory_space=pl.ANY)


def _place():
    return lax.axis_index("x"), lax.axis_index("y"), lax.axis_index("c")


def _other_chips(x, y):
    return [(1 - x, y), (x, 1 - y), (1 - x, 1 - y)]


def _rcopy(src, dst, ssem, rsem, dev):
    return pltpu.make_async_remote_copy(src_ref=src, dst_ref=dst, send_sem=ssem, recv_sem=rsem, device_id=dev,
                                        device_id_type=MESH)


def _pack_shard(name, w, layer, chip_arr, after):
    rows, cols = w.shape[-2:]
    tr = _tile(rows, (512, 352, 256, 128, 64, 32, 16))

    def body(p_ref, w_ref, after_ref, o_ref):
        o_ref[...] = w_ref[...].astype(o_ref.dtype)

    if w.ndim == 3:
        w_spec = pl.BlockSpec((None, tr, cols), lambda i, p: (layer, i, 0))
    else:
        w_spec = pl.BlockSpec((tr, cols), lambda i, p: (i, 0))
    return pl.pallas_call(
        body, name=name,
        grid_spec=pltpu.PrefetchScalarGridSpec(
            num_scalar_prefetch=1, grid=(rows // tr,), in_specs=[w_spec, ANY],
            out_specs=pl.BlockSpec((None, tr, cols), lambda i, p: (p[0], i, 0))),
        out_shape=jax.ShapeDtypeStruct((N_CHIPS, rows, cols), BF16), compiler_params=_params(1))(chip_arr, w, after)


def _swap_halves(name, arrs):
    n = len(arrs)

    def body(*refs):
        ins, outs = refs[:n], refs[n:2 * n]
        send, recv = refs[2 * n:]
        x, y, c = _place()
        cps = []
        for a in range(n):
            h = ins[a].shape[1] // 2
            cp = _rcopy(ins[a].at[:, pl.ds((1 - c) * h, h)], outs[a], send.at[a], recv.at[a], (x, y, 1 - c))
            cp.start()
            cps.append(cp)
        for cp in cps:
            cp.wait()

    return pl.pallas_call(
        body, name=name, in_specs=[ANY] * n, out_specs=[ANY] * n,
        out_shape=[jax.ShapeDtypeStruct((a.shape[0], a.shape[1] // 2, a.shape[2]), a.dtype) for a in arrs],
        scratch_shapes=[pltpu.SemaphoreType.DMA((n,)), pltpu.SemaphoreType.DMA((n,))])(*arrs)


SEM = pl.BlockSpec(memory_space=pltpu.SEMAPHORE)
EFFECT = pltpu.SideEffectType.DATAFLOW_SIDE_EFFECTING


def _shapes(arrs):
    return [jax.ShapeDtypeStruct(a.shape, a.dtype) for a in arrs]


def _gather_start(name, thru, arrs):
    n, nt = len(arrs), len(thru)

    def body(*refs):
        ins = refs[nt:nt + n]
        send, recv = refs[nt + n], refs[nt + n + 1]
        outs = refs[2 * nt + n + 2:]
        x, y, c = _place()
        me = 2 * x + y
        for a in range(n):
            h = ins[a].shape[1] // 2
            mine = pl.ds(c * h, h)
            for j, (px, py) in enumerate(_other_chips(x, y)):
                _rcopy(ins[a].at[me, mine], outs[a].at[me, mine], send.at[3 * a + j], recv.at[3 * a + j], (px, py, c)).start()

    res = pl.pallas_call(
        body, name=name, in_specs=[ANY] * (nt + n), out_specs=[SEM, SEM] + [ANY] * (nt + n),
        out_shape=[pltpu.SemaphoreType.DMA((3 * n,)), pltpu.SemaphoreType.DMA((3 * n,))] + _shapes(thru) + _shapes(arrs),
        input_output_aliases={i: 2 + i for i in range(nt + n)},
        compiler_params=pltpu.CompilerParams(has_side_effects=EFFECT))(*thru, *arrs)
    return res[0], res[1], res[2:2 + nt], res[2 + nt:]


def _gather_wait(name, send, recv, arrs, after):
    n = len(arrs)

    def body(*refs):
        ins = refs[:n]
        send_ref, recv_ref = refs[n], refs[n + 1]
        x, y, c = _place()
        me = 2 * x + y
        for a in range(n):
            h = ins[a].shape[1] // 2
            mine = pl.ds(c * h, h)
            for j, (px, py) in enumerate(_other_chips(x, y)):
                sent = ins[a].at[me, mine]
                landed = ins[a].at[2 * px + py, mine]
                cp = _rcopy(sent, landed, send_ref.at[3 * a + j], recv_ref.at[3 * a + j], (px, py, c))
                cp.wait_send()
                cp.wait_recv()

    return pl.pallas_call(
        body, name=name, in_specs=[ANY] * n + [SEM, SEM, ANY], out_specs=[ANY] * n, out_shape=_shapes(arrs),
        input_output_aliases={a: a for a in range(n)},
        compiler_params=pltpu.CompilerParams(has_side_effects=EFFECT))(*arrs, send, recv, after)


def _forward_halves(name, arrs):
    n = len(arrs)

    def body(*refs):
        ins, outs = refs[:n], refs[n:2 * n]
        send, recv = refs[2 * n:]
        x, y, c = _place()
        sib = (x, y, 1 - c)
        chips = _other_chips(x, y)
        cps = []
        for a in range(n):
            h = ins[a].shape[1] // 2
            mine = pl.ds(c * h, h)
            for j, (px, py) in enumerate(chips):
                cp = _rcopy(ins[a].at[2 * px + py, mine], outs[a].at[2 * px + py, mine], send.at[3 * a + j],
                            recv.at[3 * a + j], sib)
                cp.start()
                cps.append(cp)
        for a in range(n):
            h = ins[a].shape[1] // 2
            theirs = pl.ds((1 - c) * h, h)
            for j, (px, py) in enumerate(chips):
                got = outs[a].at[2 * px + py, theirs]
                _rcopy(got, got, send.at[3 * a + j], recv.at[3 * a + j], sib).wait_recv()
        for cp in cps:
            cp.wait_send()

    return pl.pallas_call(
        body, name=name, in_specs=[ANY] * n, out_specs=[ANY] * n, out_shape=_shapes(arrs),
        input_output_aliases={a: a for a in range(n)},
        scratch_shapes=[pltpu.SemaphoreType.DMA((3 * n,)), pltpu.SemaphoreType.DMA((3 * n,))])(*arrs)


def _forward_start(name, thru, arrs):
    n, nt = len(arrs), len(thru)

    def body(*refs):
        ins = refs[nt:nt + n]
        send, recv = refs[nt + n], refs[nt + n + 1]
        outs = refs[2 * nt + n + 2:]
        x, y, c = _place()
        for a in range(n):
            h = ins[a].shape[1] // 2
            mine = pl.ds(c * h, h)
            for j, (px, py) in enumerate(_other_chips(x, y)):
                _rcopy(ins[a].at[2 * px + py, mine], outs[a].at[2 * px + py, mine], send.at[3 * a + j], recv.at[3 * a + j],
                       (x, y, 1 - c)).start()

    res = pl.pallas_call(
        body, name=name, in_specs=[ANY] * (nt + n), out_specs=[SEM, SEM] + [ANY] * (nt + n),
        out_shape=[pltpu.SemaphoreType.DMA((3 * n,)), pltpu.SemaphoreType.DMA((3 * n,))] + _shapes(thru) + _shapes(arrs),
        input_output_aliases={i: 2 + i for i in range(nt + n)},
        compiler_params=pltpu.CompilerParams(has_side_effects=EFFECT))(*thru, *arrs)
    return res[0], res[1], res[2:2 + nt], res[2 + nt:]


def _forward_wait(name, send, recv, arrs, after):
    n = len(arrs)

    def body(*refs):
        ins = refs[:n]
        send_ref, recv_ref = refs[n], refs[n + 1]
        x, y, c = _place()
        for a in range(n):
            h = ins[a].shape[1] // 2
            for j, (px, py) in enumerate(_other_chips(x, y)):
                sent = ins[a].at[2 * px + py, pl.ds(c * h, h)]
                got = ins[a].at[2 * px + py, pl.ds((1 - c) * h, h)]
                cp = _rcopy(sent, got, send_ref.at[3 * a + j], recv_ref.at[3 * a + j], (x, y, 1 - c))
                cp.wait_send()
                cp.wait_recv()

    return pl.pallas_call(
        body, name=name, in_specs=[ANY] * n + [SEM, SEM, ANY], out_specs=[ANY] * n, out_shape=_shapes(arrs),
        input_output_aliases={a: a for a in range(n)},
        compiler_params=pltpu.CompilerParams(has_side_effects=EFFECT))(*arrs, send, recv, after)


def _scatter_start(name, thru, arrs):
    n, nt = len(arrs), len(thru)
    landing = [jnp.zeros_like(a) for a in arrs]

    def body(*refs):
        ins = refs[nt:nt + n]
        send, recv = refs[nt + 2 * n], refs[nt + 2 * n + 1]
        outs = refs[2 * nt + 3 * n + 2:]
        x, y, c = _place()
        me = 2 * x + y
        for a in range(n):
            for j, (px, py) in enumerate(_other_chips(x, y)):
                _rcopy(ins[a].at[2 * px + py], outs[a].at[me], send.at[3 * a + j], recv.at[3 * a + j], (px, py, c)).start()

    res = pl.pallas_call(
        body, name=name, in_specs=[ANY] * (nt + 2 * n), out_specs=[SEM, SEM] + [ANY] * (nt + 2 * n),
        out_shape=[pltpu.SemaphoreType.DMA((3 * n,)), pltpu.SemaphoreType.DMA((3 * n,))] + _shapes(thru) + _shapes(arrs)
        + _shapes(landing),
        input_output_aliases={i: 2 + i for i in range(nt + 2 * n)},
        compiler_params=pltpu.CompilerParams(has_side_effects=EFFECT))(*thru, *arrs, *landing)
    return res[0], res[1], res[2:2 + nt], res[2 + nt:2 + nt + n], res[2 + nt + n:]


def _scatter_wait(name, send, recv, arrs, landing, after):
    n = len(arrs)

    def body(*refs):
        ins, land = refs[:n], refs[n:2 * n]
        send_ref, recv_ref = refs[2 * n], refs[2 * n + 1]
        x, y, c = _place()
        for a in range(n):
            for j, (px, py) in enumerate(_other_chips(x, y)):
                cp = _rcopy(ins[a].at[2 * px + py], land[a].at[2 * px + py], send_ref.at[3 * a + j], recv_ref.at[3 * a + j],
                            (px, py, c))
                cp.wait_send()
                cp.wait_recv()

    res = pl.pallas_call(
        body, name=name, in_specs=[ANY] * (2 * n) + [SEM, SEM, ANY], out_specs=[ANY] * (2 * n),
        out_shape=_shapes(arrs) + _shapes(landing), input_output_aliases={i: i for i in range(2 * n)},
        compiler_params=pltpu.CompilerParams(has_side_effects=EFFECT))(*arrs, *landing, send, recv, after)
    return res[:n], res[n:]


def _swap_start(name, thru, arrs):
    n, nt = len(arrs), len(thru)
    landing = [lax.empty((a.shape[0], a.shape[1] // 2, a.shape[2]), a.dtype) for a in arrs]

    def body(*refs):
        ins = refs[nt:nt + n]
        send, recv = refs[nt + 2 * n], refs[nt + 2 * n + 1]
        outs = refs[2 * nt + 3 * n + 2:]
        x, y, c = _place()
        for a in range(n):
            h = ins[a].shape[1] // 2
            _rcopy(ins[a].at[:, pl.ds((1 - c) * h, h)], outs[a], send.at[a], recv.at[a], (x, y, 1 - c)).start()

    res = pl.pallas_call(
        body, name=name, in_specs=[ANY] * (nt + 2 * n), out_specs=[SEM, SEM] + [ANY] * (nt + 2 * n),
        out_shape=[pltpu.SemaphoreType.DMA((n,)), pltpu.SemaphoreType.DMA((n,))] + _shapes(thru) + _shapes(arrs)
        + _shapes(landing),
        input_output_aliases={i: 2 + i for i in range(nt + 2 * n)},
        compiler_params=pltpu.CompilerParams(has_side_effects=EFFECT))(*thru, *arrs, *landing)
    return res[0], res[1], res[2:2 + nt], res[2 + nt:2 + nt + n], res[2 + nt + n:]


def _swap_wait(name, send, recv, arrs, landing, after):
    n = len(arrs)

    def body(*refs):
        ins, land = refs[:n], refs[n:2 * n]
        send_ref, recv_ref = refs[2 * n], refs[2 * n + 1]
        x, y, c = _place()
        for a in range(n):
            h = ins[a].shape[1] // 2
            cp = _rcopy(ins[a].at[:, pl.ds((1 - c) * h, h)], land[a], send_ref.at[a], recv_ref.at[a], (x, y, 1 - c))
            cp.wait_send()
            cp.wait_recv()

    res = pl.pallas_call(
        body, name=name, in_specs=[ANY] * (2 * n) + [SEM, SEM, ANY], out_specs=[ANY] * (2 * n),
        out_shape=_shapes(arrs) + _shapes(landing), input_output_aliases={i: i for i in range(2 * n)},
        compiler_params=pltpu.CompilerParams(has_side_effects=EFFECT))(*arrs, *landing, send, recv, after)
    return res[:n], res[n:]


def _join_start(name, arrs):
    n = len(arrs)

    def body(*refs):
        ins = refs[:n]
        send, recv = refs[n], refs[n + 1]
        outs = refs[n + 2:]
        x, y, c = _place()
        for a in range(n):
            h = ins[a].shape[0] // 2
            mine = pl.ds(c * h, h)
            _rcopy(ins[a].at[mine], outs[a].at[mine], send.at[a], recv.at[a], (x, y, 1 - c)).start()

    res = pl.pallas_call(
        body, name=name, in_specs=[ANY] * n, out_specs=[SEM, SEM] + [ANY] * n,
        out_shape=[pltpu.SemaphoreType.DMA((n,)), pltpu.SemaphoreType.DMA((n,))] + _shapes(arrs),
        input_output_aliases={i: 2 + i for i in range(n)},
        compiler_params=pltpu.CompilerParams(has_side_effects=EFFECT))(*arrs)
    return res[0], res[1], res[2:]


def _join_wait(name, send, recv, arrs, after):
    n = len(arrs)

    def body(*refs):
        ins = refs[:n]
        send_ref, recv_ref = refs[n], refs[n + 1]
        x, y, c = _place()
        for a in range(n):
            h = ins[a].shape[0] // 2
            cp = _rcopy(ins[a].at[pl.ds(c * h, h)], ins[a].at[pl.ds((1 - c) * h, h)], send_ref.at[a], recv_ref.at[a],
                        (x, y, 1 - c))
            cp.wait_send()
            cp.wait_recv()

    return pl.pallas_call(
        body, name=name, in_specs=[ANY] * n + [SEM, SEM, ANY], out_specs=[ANY] * n, out_shape=_shapes(arrs),
        input_output_aliases={a: a for a in range(n)},
        compiler_params=pltpu.CompilerParams(has_side_effects=EFFECT))(*arrs, send, recv, after)


def _join_halves(name, arrs):
    n = len(arrs)

    def body(*refs):
        ins, outs = refs[:n], refs[n:2 * n]
        send, recv = refs[2 * n:]
        x, y, c = _place()
        cps = []
        for a in range(n):
            h = ins[a].shape[0] // 2
            mine = pl.ds(c * h, h)
            cp = _rcopy(ins[a].at[mine], outs[a].at[mine], send.at[a], recv.at[a], (x, y, 1 - c))
            cp.start()
            cps.append(cp)
        for a in range(n):
            h = ins[a].shape[0] // 2
            got = outs[a].at[pl.ds((1 - c) * h, h)]
            _rcopy(got, got, send.at[a], recv.at[a], (x, y, 1 - c)).wait_recv()
        for cp in cps:
            cp.wait_send()

    return pl.pallas_call(
        body, name=name, in_specs=[ANY] * n, out_specs=[ANY] * n,
        out_shape=[jax.ShapeDtypeStruct(a.shape, a.dtype) for a in arrs],
        input_output_aliases={a: a for a in range(n)},
        scratch_shapes=[pltpu.SemaphoreType.DMA((n,)), pltpu.SemaphoreType.DMA((n,))])(*arrs)


def _allgather8(name, xs, reduce):
    m_per, n = xs.shape

    def body(x_ref, out_ref, *rest):
        if reduce:
            sum_ref, send, recv, lsem = rest
        else:
            send, recv, lsem = rest
        x, y, c = _place()
        me, sib = (x, y, c), (x, y, 1 - c)
        chips = _other_chips(x, y)

        def rows(px, py, pc):
            return out_ref.at[pl.ds((4 * px + 2 * py + pc) * m_per, m_per), :]

        def copy(k, block, to, src=None):
            return _rcopy(rows(*block) if src is None else src, rows(*block), send.at[k], recv.at[k], to)

        mine = pltpu.make_async_copy(x_ref, rows(*me), lsem)
        mine.start()
        first = [copy(0, me, sib, src=x_ref)]
        first += [copy(1 + j, me, (*chip, c), src=x_ref) for j, chip in enumerate(chips)]
        for cp in first:
            cp.start()
        passed = [copy(4 + j, (*chip, c), sib) for j, chip in enumerate(chips)]
        for j, chip in enumerate(chips):
            copy(1 + j, (*chip, c), me).wait_recv()
            passed[j].start()
        copy(0, sib, me).wait_recv()
        for j, chip in enumerate(chips):
            copy(4 + j, (*chip, 1 - c), me).wait_recv()
        for cp in first + passed:
            cp.wait_send()
        mine.wait()
        if reduce:
            acc = out_ref[pl.ds(0, m_per), :]
            for dev in range(1, N_DEV):
                acc = acc + out_ref[pl.ds(dev * m_per, m_per), :]
            sum_ref[...] = acc

    vm = pl.BlockSpec(memory_space=pltpu.VMEM)
    out_shape = [jax.ShapeDtypeStruct((N_DEV * m_per, n), xs.dtype)]
    if reduce:
        out_shape.append(jax.ShapeDtypeStruct((m_per, n), xs.dtype))
    return pl.pallas_call(
        body, name=name, in_specs=[vm], out_specs=[vm] * len(out_shape), out_shape=out_shape,
        scratch_shapes=[pltpu.SemaphoreType.DMA((7,)), pltpu.SemaphoreType.DMA((7,)), pltpu.SemaphoreType.DMA],
        compiler_params=pltpu.CompilerParams(vmem_limit_bytes=VMEM_LIMIT))(xs)


def _add_my_half(name, a, rb, c_arr):
    s, h, cols = rb.shape
    tr = _tile(h, (512, 352, 256, 128, 64, 32, 16))
    nt = h // tr

    def body(c_ref, a_ref, b_ref, o_ref):
        o_ref[...] = (a_ref[...].astype(F32) + b_ref[...].astype(F32)).astype(o_ref.dtype)

    return pl.pallas_call(
        body, name=name,
        grid_spec=pltpu.PrefetchScalarGridSpec(
            num_scalar_prefetch=1, grid=(s, nt),
            in_specs=[pl.BlockSpec((None, tr, cols), lambda k, i, c: (k, c[0] * nt + i, 0)),
                      pl.BlockSpec((None, tr, cols), lambda k, i, c: (k, i, 0))],
            out_specs=pl.BlockSpec((None, tr, cols), lambda k, i, c: (k, i, 0))),
        out_shape=jax.ShapeDtypeStruct(rb.shape, BF16), compiler_params=_params(2))(c_arr, a, rb)


def _sum_chips(name, own, q, place):
    s, h, cols = q.shape
    tr = _tile(h, (512, 352, 256, 128, 64, 32, 16))
    nt = h // tr

    def body(p_ref, own_ref, q_ref, o_ref):
        chip = p_ref[0]
        acc = jnp.where(chip == 0, own_ref[0], q_ref[0]).astype(F32)
        for j in range(1, s):
            acc = acc + jnp.where(chip == j, own_ref[j], q_ref[j]).astype(F32)
        o_ref[...] = acc

    blk = pl.BlockSpec((s, tr, cols), lambda i, p: (0, i, 0))
    return pl.pallas_call(
        body, name=name,
        grid_spec=pltpu.PrefetchScalarGridSpec(
            num_scalar_prefetch=1, grid=(nt,), in_specs=[blk, blk],
            out_specs=pl.BlockSpec((tr, cols), lambda i, p: (p[1] * nt + i, 0))),
        out_shape=jax.ShapeDtypeStruct((2 * h, cols), F32), compiler_params=_params(1))(place, own, q)


def _pack_rows(parts):
    rows = []
    for p in parts:
        flat = p.reshape(-1).astype(F32)
        n = _roundup(flat.shape[0], 8 * LANE)
        rows.append(jnp.pad(flat, (0, n - flat.shape[0])).reshape(-1, LANE))
    return jnp.concatenate(rows, axis=0)


def _unpack_rows(buf, shapes):
    out, r = [], 0
    for shp in shapes:
        size = math.prod(shp)
        nr = _roundup(size, 8 * LANE) // LANE
        out.append(buf[r:r + nr].reshape(-1)[:size].reshape(shp))
        r += nr
    return out


def kernel(x, attn_norm, gla_w_in, gla_w_a2, gla_b_a2, gla_head_norm, gla_w_out, kv_norm, w_kv, dsa_w_q, dsa_w_out, ffn_norm, ffn_w_up, ffn_conv_w, ffn_conv_b, ffn_w_down, final_norm, loss_target, m_attn_norm, m_gla_w_in, m_gla_w_a2, m_gla_b_a2, m_gla_head_norm, m_gla_w_out, m_kv_norm, m_w_kv, m_dsa_w_q, m_dsa_w_out, m_ffn_norm, m_ffn_w_up, m_ffn_conv_w, m_ffn_conv_b, m_ffn_w_down, m_final_norm, v_attn_norm, v_gla_w_in, v_gla_w_a2, v_gla_b_a2, v_gla_head_norm, v_gla_w_out, v_kv_norm, v_w_kv, v_dsa_w_q, v_dsa_w_out, v_ffn_norm, v_ffn_w_up, v_ffn_conv_w, v_ffn_conv_b, v_ffn_w_down, v_final_norm):
    lay = _layout()
    d, f = D_MODEL, D_FF
    cx, cy, cc = _place()
    chip = 2 * cx + cy
    c_arr = jnp.reshape(cc, (1,)).astype(jnp.int32)
    place = jnp.stack([chip, cc]).astype(jnp.int32)

    groups = {"A": ("gin", "small"), "B0": ("gout",), "B1": ("up0",), "B2": ("down0",), "C1": ("wkv", "wq", "dout"),
              "C2": ("up1", "down1")}
    big_shards = (gla_w_in, gla_w_out, w_kv, dsa_w_q, dsa_w_out, ffn_w_up, ffn_w_down)
    ws = _pack_weights(chip, groups["A"][:1], *big_shards)
    sharded_small = [gla_w_a2[0], gla_b_a2[0], gla_head_norm[0], ffn_conv_w]
    packed = _pack_rows(sharded_small)
    packed = jnp.pad(packed, ((0, _roundup(packed.shape[0], 16) - packed.shape[0]), (0, 0)))
    ws["small"] = lax.dynamic_update_slice(jnp.zeros((N_CHIPS,) + packed.shape, F32), packed[None], (chip, 0, 0))
    send, recv, _, arrs = _gather_start("gather_a_start", [], [ws[k] for k in groups["A"]])
    chip_arr = place[:1]
    sources = {"up0": (ffn_w_up, 0), "up1": (ffn_w_up, 1), "down0": (ffn_w_down, 0), "down1": (ffn_w_down, 1),
               "wq": (dsa_w_q, 0), "wkv": (w_kv, 0), "dout": (dsa_w_out, 0), "gout": (gla_w_out, 0)}
    later = ("B0", "B1", "B2", "C1", "C2")
    for k in sum((groups[grp] for grp in later), ()):
        ws[k] = _pack_shard(f"pack_{k}", *sources[k], chip_arr, arrs[1])
    arrs = _gather_wait("gather_a_wait", send, recv, arrs, ws["dout"])
    ws.update(zip(groups["A"], _forward_halves("forward_a", arrs)))
    in_flight = {}
    thru = [ws[k] for k in groups["A"]]
    for grp in later:
        send, recv, thru, arrs = _gather_start(f"gather_{grp.lower()}_start", thru, [ws[k] for k in groups[grp]])
        ws.update(zip(groups[grp], arrs))
        in_flight[grp] = (send, recv)
    ws.update(zip(groups["A"], thru))
    pending = []

    class _Comm:
        def prefetch(self, grp, ws, carry):
            send, recv = in_flight[grp]
            arrs = _gather_wait(f"gather_{grp.lower()}_wait", send, recv, [ws[k] for k in groups[grp]], carry)
            send, recv, thru, arrs = _forward_start(f"forward_{grp.lower()}_start", [carry], arrs)
            in_flight[grp] = (send, recv)
            return {**ws, **dict(zip(groups[grp], arrs))}, thru[0]

        def need(self, grp, ws, after):
            send, recv = in_flight[grp]
            arrs = _forward_wait(f"forward_{grp.lower()}_wait", send, recv, [ws[k] for k in groups[grp]], after)
            return {**ws, **dict(zip(groups[grp], arrs))}

        swapping = None

        def reduce(self, grp, grads, carry):
            names = list(grads)
            send, recv, thru, parts, theirs = _swap_start(f"swap_{grp}_start", [carry], [grads[k] for k in names])
            self.swapping = (grp, names, send, recv, parts, theirs)
            return thru[0]

        def tick(self, carry):
            if self.swapping is None:
                return carry
            grp, names, send, recv, parts, theirs = self.swapping
            self.swapping = None
            parts, theirs = _swap_wait(f"swap_{grp}_wait", send, recv, parts, theirs, carry)
            return self.scatter(grp, names, parts, theirs, carry)

        def scatter(self, grp, names, parts, theirs, carry):
            sums = [_add_my_half(f"add_half_{k}", a, b, c_arr) for k, a, b in zip(names, parts, theirs)]
            send, recv, thru, sums, landing = _scatter_start(f"scatter_{grp}_start", [carry], sums)
            pending.append((grp, names, send, recv, sums, landing))
            return thru[0]

        def reduce_now(self, grp, grads, carry):
            names = list(grads)
            parts = [grads[k] for k in names]
            return self.scatter(grp, names, parts, _swap_halves(f"swap_{grp}", parts), carry)

    shards = [_unpack_rows(ws["small"][s], [p.shape for p in sharded_small]) for s in range(N_CHIPS)]
    w_a2, b_a2, head_norm, conv_w = [jnp.concatenate([shards[s][k] for s in range(N_CHIPS)], axis=-1) for k in range(4)]
    norms, small = _small_params(attn_norm, ffn_norm, kv_norm, final_norm, ffn_conv_b, w_a2, b_a2, head_norm, conv_w)

    comm = _Comm()
    loss_blk, grad_x, sm, last_big = _local_step(x[0], loss_target[0], ws, norms, small, comm)

    small_parts = [loss_blk, jnp.concatenate([sm["attn0"], sm["attn1"]]), jnp.concatenate([sm["ffn0"], sm["ffn1"]]),
                   sm["kv"], sm["final"], jnp.concatenate([sm["conv_b0"], sm["conv_b1"]]),
                   sm["w_a2p"][:GATE_RANK], sm["b_a2"], sm["head_norm"],
                   jnp.stack([jnp.concatenate(sm["conv_w0"]), jnp.concatenate(sm["conv_w1"])])]
    small_shapes = [(8, LANE), (2, d), (2, d), (d,), (d,), (2, f), (GATE_RANK, GLA_KEY_DIM), (GLA_KEY_DIM,),
                    (GLA_VAL_DIM // GLA_HEADS,), (2, 3, f)]
    _, reduced = _allgather8("reduce_small", _pack_rows(small_parts), True)
    reduced = comm.reduce_now("gla", last_big, reduced)

    loss_r, g_attn, g_ffn, g_kv, g_final, g_cb, g_a2, g_ba2, g_hn, g_cw = _unpack_rows(reduced, small_shapes)
    loss = loss_r[0, 0]

    def mine(g, axis):
        w = g.shape[axis] // N_CHIPS
        return lax.dynamic_slice_in_dim(g, chip * w, w, axis)

    grads = {
        "attn_norm": g_attn, "gla_w_a2": mine(g_a2, 1)[None], "gla_b_a2": mine(g_ba2, 0)[None],
        "gla_head_norm": mine(g_hn, 0)[None], "kv_norm": g_kv, "ffn_norm": g_ffn, "ffn_conv_w": mine(g_cw, 2),
        "ffn_conv_b": g_cb, "final_norm": g_final,
    }
    weights = {"attn_norm": (attn_norm, m_attn_norm, v_attn_norm), "gla_w_in": (gla_w_in, m_gla_w_in, v_gla_w_in),
               "gla_w_a2": (gla_w_a2, m_gla_w_a2, v_gla_w_a2), "gla_b_a2": (gla_b_a2, m_gla_b_a2, v_gla_b_a2),
               "gla_head_norm": (gla_head_norm, m_gla_head_norm, v_gla_head_norm),
               "gla_w_out": (gla_w_out, m_gla_w_out, v_gla_w_out), "kv_norm": (kv_norm, m_kv_norm, v_kv_norm),
               "w_kv": (w_kv, m_w_kv, v_w_kv), "dsa_w_q": (dsa_w_q, m_dsa_w_q, v_dsa_w_q),
               "dsa_w_out": (dsa_w_out, m_dsa_w_out, v_dsa_w_out), "ffn_norm": (ffn_norm, m_ffn_norm, v_ffn_norm),
               "ffn_w_up": (ffn_w_up, m_ffn_w_up, v_ffn_w_up), "ffn_conv_w": (ffn_conv_w, m_ffn_conv_w, v_ffn_conv_w),
               "ffn_conv_b": (ffn_conv_b, m_ffn_conv_b, v_ffn_conv_b),
               "ffn_w_down": (ffn_w_down, m_ffn_w_down, v_ffn_w_down), "final_norm": (final_norm, m_final_norm, v_final_norm)}
    order = list(weights)
    big_names = ("gla_w_in", "gla_w_out", "w_kv", "dsa_w_q", "dsa_w_out", "ffn_w_up", "ffn_w_down")
    delta, new_m, new_v = {}, {}, {}

    def adam_big(k, g):
        w, m, v = weights[k]
        cols = w.shape[-1]
        res = _adamw(f"adamw_{k}", w.reshape(-1, cols), g.reshape(-1, g.shape[-1]), m.reshape(-1, cols), v.reshape(-1, cols))
        delta[k], new_m[k], new_v[k] = [r.reshape(w.shape) for r in res[:3]]
        grads[k] = res[3].reshape(w.shape) if len(res) == 4 else g
        return res[0]

    full = {}
    after = reduced
    joining = []
    for grp, names, send, recv, sums, landing in pending[:-1]:
        sums, landing = _scatter_wait(f"scatter_{grp}_wait", send, recv, sums, landing, after)
        halves = [_sum_chips(f"sum_chips_{k}", s, q, place) for k, s, q in zip(names, sums, landing)]
        send, recv, halves = _join_start(f"join_{grp}_start", halves)
        joining.append((grp, names, send, recv, halves))
        after = halves[0]
    for grp, names, send, recv, halves in joining:
        joined = _join_wait(f"join_{grp}_wait", send, recv, halves, after)
        full.update(zip(names, joined))
        after = joined[0]
    after = adam_big("w_kv", full["wkv"])
    after = adam_big("dsa_w_q", full["wq"][None])
    after = adam_big("dsa_w_out", full["dout"][None])
    for k, g0, g1 in (("ffn_w_up", "up0", "up1"), ("ffn_w_down", "down0", "down1")):
        delta[k], new_m[k], new_v[k], grads[k] = _adamw_layers(f"adamw_{k}", weights[k][0], (full[g0], full[g1]),
                                                               weights[k][1], weights[k][2])
        after = delta[k]
    grp, names, send, recv, sums, landing = pending[-1]
    sums, landing = _scatter_wait(f"scatter_{grp}_wait", send, recv, sums, landing, after)
    halves = [_sum_chips(f"sum_chips_{k}", s, q, place) for k, s, q in zip(names, sums, landing)]
    full.update(zip(names, _join_halves(f"join_{grp}", halves)))
    adam_big("gla_w_in", full["gin"])
    adam_big("gla_w_out", full["gout"][None])
    small_names = [k for k in order if k not in big_names]
    packed = [_pack_rows([src[k] for k in small_names])
              for src in ({k: weights[k][0] for k in small_names}, grads, {k: weights[k][1] for k in small_names},
                          {k: weights[k][2] for k in small_names})]
    res = _adamw("adamw_small", *packed)
    shapes = [weights[k][0].shape for k in small_names]
    for dst, buf in zip((delta, new_m, new_v), res):
        for k, val in zip(small_names, _unpack_rows(buf, shapes)):
            dst[k] = val
    return (loss, grad_x[None], *[grads[k] for k in order], *[delta[k] for k in order], *[new_m[k] for k in order],
            *[new_v[k] for k in order])
```

```python
import math

import jax
import jax.numpy as jnp
from jax import lax
from jax.experimental import pallas as pl
from jax.experimental.pallas import tpu as pltpu

F32 = jnp.float32
BF16 = jnp.bfloat16

D_MODEL = 2048
SEQ = 4096
GLA_HEADS = 4
GLA_KEY_DIM = D_MODEL // 2
GLA_VAL_DIM = D_MODEL
GATE_RANK = 16
GATE_NORMALIZER = 16.0
GLA_CHUNK = 64
ATT_HEADS = 16
HEAD_DIM = 128
WINDOWS = (128, 512, 2048)
DILATIONS = (1, 4, 16)
ATT_BLOCK = 128
D_FF = 5632
EPS = 1e-6
ADAM_LR = 0.001
ADAM_B1 = 0.9
ADAM_B2 = 0.999
ADAM_EPS = 1e-08
ADAM_WD = 0.01
ADAM_STEP = 10

N_CHIPS = 4
N_DEV = 8
LANE = 128
A_PAD = 128
VMEM_LIMIT = 56 * 1024 * 1024
MAX_K_TILE = 2816
NEG = -1e30
MESH = pl.DeviceIdType.MESH

NN = (((1,), (0,)), ((), ()))
NT = (((1,), (1,)), ((), ()))
TN = (((0,), (0,)), ((), ()))


def _tile(n, cands):
    for c in cands:
        if c <= n and n % c == 0:
            return c
    return n


def _roundup(n, m):
    return -(-n // m) * m


def _params(n_axes):
    return pltpu.CompilerParams(dimension_semantics=("arbitrary",) * n_axes, vmem_limit_bytes=VMEM_LIMIT)


def _dot(a, b, dims):
    return lax.dot_general(a, b, dims, preferred_element_type=F32)


def _sigmoid(x):
    return 1.0 / (1.0 + jnp.exp(-x))


COL_SHARDED = ("gin", "up0", "up1", "wq", "wkv")
ROW_SHARDED = ("gout", "down0", "down1", "dout")


def _layout():
    f = D_FF
    hd = ATT_HEADS * HEAD_DIM
    gin = 2 * GLA_KEY_DIM + 2 * GLA_VAL_DIM + GATE_RANK
    up_w = 2 * f // N_CHIPS
    q_w = 3 * hd // N_CHIPS
    kv_w = 2 * hd // N_CHIPS
    dn_r = f // N_CHIPS
    go_r = GLA_VAL_DIM // N_CHIPS
    do_r = hd // N_CHIPS
    big = (1408, 1024, 512, 256, 128)
    return {
        "gin": (0, gin // N_CHIPS, LANE),
        "up0": (0, up_w, _tile(up_w, big)), "up1": (0, up_w, _tile(up_w, big)),
        "wq": (0, q_w, _tile(q_w, (512, 384, 256, 128))), "wkv": (0, kv_w, _tile(kv_w, (1024, 512, 256, 128))),
        "down0": (0, dn_r, _tile(dn_r, big)), "down1": (0, dn_r, _tile(dn_r, big)),
        "gout": (0, go_r, _tile(go_r, (512, 256, 128))), "dout": (0, do_r, _tile(do_r, (512, 256, 128))),
    }


def _matmul(name, a, b, dims, grid, a_spec, b_spec, o_spec, out_shape, acc_shape, add=None, add_spec=None):
    nk = grid[2]
    has_add = add is not None

    def body(*refs):
        a_ref, b_ref = refs[0], refs[1]
        pos = 2
        add_ref = None
        if has_add:
            add_ref = refs[pos]
            pos += 1
        o_ref = refs[pos]
        prod = _dot(a_ref[...].astype(BF16), b_ref[...].astype(BF16), dims)

        def finish(val):
            if has_add:
                val = val + add_ref[...].astype(F32)
            o_ref[...] = val.astype(o_ref.dtype)

        if nk == 1:
            finish(prod)
        else:
            acc_ref = refs[pos + 1]
            k = pl.program_id(2)

            @pl.when(k == 0)
            def _():
                acc_ref[...] = prod

            @pl.when(k > 0)
            def _():
                acc_ref[...] += prod

            @pl.when(k == nk - 1)
            def _():
                finish(acc_ref[...])

    in_specs = [a_spec, b_spec]
    args = [a, b]
    if has_add:
        in_specs.append(add_spec)
        args.append(add)
    scratch = [] if nk == 1 else [pltpu.VMEM(acc_shape, F32)]
    return pl.pallas_call(body, name=name, grid=grid, in_specs=in_specs, out_specs=o_spec, out_shape=out_shape,
                          scratch_shapes=scratch, compiler_params=_params(3))(*args)


def _mm_act_wc(name, a, wc, seg, out_dtype):
    off, w, tn = seg
    t_len, d = a.shape
    tm = _tile(t_len, (1024, 512, 256, 128))
    nps = w // tn
    ob = off // tn
    grid = (t_len // tm, N_CHIPS * nps, 1)
    return _matmul(
        name, a, wc, NN, grid,
        pl.BlockSpec((tm, d), lambda i, j, k: (i, 0)),
        pl.BlockSpec((None, d, tn), lambda i, j, k: (j // nps, 0, ob + j % nps)),
        pl.BlockSpec((tm, tn), lambda i, j, k: (i, j)),
        jax.ShapeDtypeStruct((t_len, N_CHIPS * w), out_dtype), (tm, tn))


def _mm_dact_wcT(name, dy, wc, seg, add=None):
    off, w, tk = seg
    if off == 0 and w <= MAX_K_TILE:
        tk = w
    t_len = dy.shape[0]
    d = wc.shape[1]
    tm = _tile(t_len, (1024, 512, 256, 128))
    tn = _tile(d, (1024, 512, 256, 128))
    kps = w // tk
    ob = off // tk
    grid = (t_len // tm, d // tn, N_CHIPS * kps)
    return _matmul(
        name, dy, wc, NT, grid,
        pl.BlockSpec((tm, tk), lambda i, j, k: (i, k)),
        pl.BlockSpec((None, tn, tk), lambda i, j, k: (k // kps, j, ob + k % kps)),
        pl.BlockSpec((tm, tn), lambda i, j, k: (i, j)),
        jax.ShapeDtypeStruct((t_len, d), F32), (tm, tn),
        add=add, add_spec=None if add is None else pl.BlockSpec((tm, tn), lambda i, j, k: (i, j)))


def _mm_grad_wc(name, a, dy, seg):
    _, w, tn = seg
    t_len, d = a.shape
    tm = _tile(d, (1024, 512, 256, 128))
    tk = _tile(t_len, (2048, 1024, 512, 256, 128))
    nps = w // tn
    grid = (d // tm, N_CHIPS * nps, t_len // tk)
    return _matmul(
        name, a, dy, TN, grid,
        pl.BlockSpec((tk, tm), lambda i, j, k: (k, i)),
        pl.BlockSpec((tk, tn), lambda i, j, k: (k, j)),
        pl.BlockSpec((None, tm, tn), lambda i, j, k: (j // nps, i, j % nps)),
        jax.ShapeDtypeStruct((N_CHIPS, d, w), BF16), (tm, tn))


def _is_plain(wr, seg):
    return seg[0] == 0 and wr.shape[1] == seg[1] and (N_CHIPS * seg[1]) % 1024 == 0


def _mm_act_wr(name, a, wr, seg, add):
    off, r, tk = seg
    t_len = a.shape[0]
    d = wr.shape[2]
    if seg[0] == 0 and wr.shape[1] == r:
        return _mm_plain(name, a, wr.reshape(N_CHIPS * r, d), NN, F32, add=add)
    tm = _tile(t_len, (1024, 512, 256, 128))
    tn = _tile(d, (1024, 512, 256, 128))
    kps = r // tk
    ob = off // tk
    grid = (t_len // tm, d // tn, N_CHIPS * kps)
    return _matmul(
        name, a, wr, NN, grid,
        pl.BlockSpec((tm, tk), lambda i, j, k: (i, k)),
        pl.BlockSpec((None, tk, tn), lambda i, j, k: (k // kps, ob + k % kps, j)),
        pl.BlockSpec((tm, tn), lambda i, j, k: (i, j)),
        jax.ShapeDtypeStruct((t_len, d), F32), (tm, tn),
        add=add, add_spec=pl.BlockSpec((tm, tn), lambda i, j, k: (i, j)))


def _mm_dact_wrT(name, dh, wr, seg):
    off, r, tn = seg
    t_len, d = dh.shape
    if _is_plain(wr, seg):
        return _mm_plain(name, dh, wr.reshape(N_CHIPS * r, d), NT, BF16)
    tm = _tile(t_len, (1024, 512, 256, 128))
    nps = r // tn
    ob = off // tn
    grid = (t_len // tm, N_CHIPS * nps, 1)
    return _matmul(
        name, dh, wr, NT, grid,
        pl.BlockSpec((tm, d), lambda i, j, k: (i, 0)),
        pl.BlockSpec((None, tn, d), lambda i, j, k: (j // nps, ob + j % nps, 0)),
        pl.BlockSpec((tm, tn), lambda i, j, k: (i, j)),
        jax.ShapeDtypeStruct((t_len, N_CHIPS * r), BF16), (tm, tn))


def _mm_grad_wr(name, a, dh, seg):
    _, r, tm = seg
    t_len, d = dh.shape
    if (N_CHIPS * r) % 1024 == 0:
        return _mm_plain(name, a, dh, TN, BF16).reshape(N_CHIPS, r, d)
    tn = _tile(d, (1024, 512, 256, 128))
    tk = _tile(t_len, (2048, 1024, 512, 256, 128))
    mps = r // tm
    grid = (N_CHIPS * mps, d // tn, t_len // tk)
    return _matmul(
        name, a, dh, TN, grid,
        pl.BlockSpec((tk, tm), lambda i, j, k: (k, i)),
        pl.BlockSpec((tk, tn), lambda i, j, k: (k, j)),
        pl.BlockSpec((None, tm, tn), lambda i, j, k: (i // mps, i % mps, j)),
        jax.ShapeDtypeStruct((N_CHIPS, r, d), BF16), (tm, tn))


def _mm_plain(name, a, b, dims, out_dtype, add=None):
    if dims == NN:
        m, kd = a.shape
        n = b.shape[1]
    elif dims == NT:
        m, kd = a.shape
        n = b.shape[0]
    else:
        kd, m = a.shape
        n = b.shape[1]
    tm = _tile(m, (1024, 512, 256, 128))
    tn = _tile(n, (1024, 768, 512, 256, 128))
    tk = _tile(kd, (MAX_K_TILE, 2048, 1408, 1024, 512, 256, 128))
    grid = (m // tm, n // tn, kd // tk)
    if dims == NN:
        a_spec = pl.BlockSpec((tm, tk), lambda i, j, k: (i, k))
        b_spec = pl.BlockSpec((tk, tn), lambda i, j, k: (k, j))
    elif dims == NT:
        a_spec = pl.BlockSpec((tm, tk), lambda i, j, k: (i, k))
        b_spec = pl.BlockSpec((tn, tk), lambda i, j, k: (j, k))
    else:
        a_spec = pl.BlockSpec((tk, tm), lambda i, j, k: (k, i))
        b_spec = pl.BlockSpec((tk, tn), lambda i, j, k: (k, j))
    o_spec = pl.BlockSpec((tm, tn), lambda i, j, k: (i, j))
    return _matmul(name, a, b, dims, grid, a_spec, b_spec, o_spec, jax.ShapeDtypeStruct((m, n), out_dtype), (tm, tn),
                   add=add, add_spec=None if add is None else o_spec)


def _rms_fwd(name, x, g):
    t_len, d = x.shape
    tm = _tile(t_len, (512, 256, 128))

    def body(x_ref, g_ref, o_ref):
        xv = x_ref[...]
        r = lax.rsqrt(jnp.mean(xv * xv, axis=-1, keepdims=True) + EPS)
        o_ref[...] = (xv * r * g_ref[...]).astype(o_ref.dtype)

    return pl.pallas_call(
        body, name=name, grid=(t_len // tm,),
        in_specs=[pl.BlockSpec((tm, d), lambda i: (i, 0)), pl.BlockSpec((1, d), lambda i: (0, 0))],
        out_specs=pl.BlockSpec((tm, d), lambda i: (i, 0)),
        out_shape=jax.ShapeDtypeStruct((t_len, d), BF16), compiler_params=_params(1))(x, g)


def _rms_bwd(name, dy, x, g, dres):
    t_len, d = x.shape
    tm = _tile(t_len, (256, 128))

    def body(dy_ref, x_ref, g_ref, dres_ref, dx_ref, dg_ref):
        xv = x_ref[...]
        r = lax.rsqrt(jnp.mean(xv * xv, axis=-1, keepdims=True) + EPS)
        xhat = xv * r
        dyv = dy_ref[...].astype(F32)
        dxn = dyv * g_ref[...]
        dx = r * (dxn - xhat * jnp.mean(dxn * xhat, axis=-1, keepdims=True))
        dx_ref[...] = dres_ref[...] + dx
        part = jnp.sum(dyv * xhat, axis=0, keepdims=True)

        @pl.when(pl.program_id(0) == 0)
        def _():
            dg_ref[...] = part

        @pl.when(pl.program_id(0) > 0)
        def _():
            dg_ref[...] += part

    row = pl.BlockSpec((tm, d), lambda i: (i, 0))
    vec = pl.BlockSpec((1, d), lambda i: (0, 0))
    return pl.pallas_call(
        body, name=name, grid=(t_len // tm,), in_specs=[row, row, vec, row], out_specs=(row, vec),
        out_shape=(jax.ShapeDtypeStruct((t_len, d), F32), jax.ShapeDtypeStruct((1, d), F32)),
        compiler_params=_params(1))(dy, x, g, dres)


def _loss_head(h, g, target):
    t_len, d = h.shape
    tm = _tile(t_len, (256, 128))

    def body(h_ref, g_ref, t_ref, dh_ref, dg_ref, loss_ref):
        xv = h_ref[...]
        gv = g_ref[...]
        r = lax.rsqrt(jnp.mean(xv * xv, axis=-1, keepdims=True) + EPS)
        xhat = xv * r
        err = xhat * gv - t_ref[...]
        dyv = err * (1.0 / d)
        dxn = dyv * gv
        dh_ref[...] = r * (dxn - xhat * jnp.mean(dxn * xhat, axis=-1, keepdims=True))
        part = jnp.sum(dyv * xhat, axis=0, keepdims=True)
        lpart = jnp.zeros((8, LANE), F32) + (0.5 / d) * jnp.sum(err * err)

        @pl.when(pl.program_id(0) == 0)
        def _():
            dg_ref[...] = part
            loss_ref[...] = lpart

        @pl.when(pl.program_id(0) > 0)
        def _():
            dg_ref[...] += part
            loss_ref[...] += lpart

    row = pl.BlockSpec((tm, d), lambda i: (i, 0))
    vec = pl.BlockSpec((1, d), lambda i: (0, 0))
    return pl.pallas_call(
        body, name="loss_head", grid=(t_len // tm,), in_specs=[row, vec, row],
        out_specs=(row, vec, pl.BlockSpec((8, LANE), lambda i: (0, 0))),
        out_shape=(jax.ShapeDtypeStruct((t_len, d), F32), jax.ShapeDtypeStruct((1, d), F32),
                   jax.ShapeDtypeStruct((8, LANE), F32)),
        compiler_params=_params(1))(h, g, target)


def _chunk_row(shape):
    return lax.broadcasted_iota(jnp.int32, shape, 0) % GLA_CHUNK


def _gla_gate_fwd(a, w_a2p, b_a2):
    t_len = a.shape[0]
    kd = w_a2p.shape[1]
    tm = _tile(t_len, (256, 128, 64))

    def body(a_ref, w_ref, b_ref, ga_ref, cum_ref):
        ga = _dot(a_ref[...], w_ref[...].astype(BF16), NN) + b_ref[...]
        ga_ref[...] = ga
        la = (jnp.minimum(ga, 0.0) - jnp.log(1.0 + jnp.exp(-jnp.abs(ga)))) * (1.0 / GATE_NORMALIZER)
        row = _chunk_row(la.shape)
        s = 1
        while s < GLA_CHUNK:
            la = la + jnp.where(row >= s, pltpu.roll(la, s, 0), 0.0)
            s *= 2
        cum_ref[...] = la

    return pl.pallas_call(
        body, name="gla_gate_fwd", grid=(t_len // tm,),
        in_specs=[pl.BlockSpec((tm, A_PAD), lambda i: (i, 0)), pl.BlockSpec((A_PAD, kd), lambda i: (0, 0)),
                  pl.BlockSpec((1, kd), lambda i: (0, 0))],
        out_specs=(pl.BlockSpec((tm, kd), lambda i: (i, 0)), pl.BlockSpec((tm, kd), lambda i: (i, 0))),
        out_shape=(jax.ShapeDtypeStruct((t_len, kd), F32), jax.ShapeDtypeStruct((t_len, kd), F32)),
        compiler_params=_params(1))(a, w_a2p, b_a2)


def _gla_gate_bwd(dcum, ga, a, w_a2p):
    t_len, kd = dcum.shape
    tm = _tile(t_len, (256, 128, 64))

    def body(dc_ref, ga_ref, a_ref, w_ref, da_ref, dw_ref, db_ref):
        x = dc_ref[...]
        row = _chunk_row(x.shape)
        s = 1
        while s < GLA_CHUNK:
            x = x + jnp.where(row < GLA_CHUNK - s, pltpu.roll(x, tm - s, 0), 0.0)
            s *= 2
        dga = x * (1.0 / GATE_NORMALIZER) * _sigmoid(-ga_ref[...])
        dgab = dga.astype(BF16)
        da_ref[...] = _dot(dgab, w_ref[...].astype(BF16), NT).astype(da_ref.dtype)
        dw = _dot(a_ref[...], dgab, TN)
        db = jnp.sum(dga, axis=0, keepdims=True)

        @pl.when(pl.program_id(0) == 0)
        def _():
            dw_ref[...] = dw
            db_ref[...] = db

        @pl.when(pl.program_id(0) > 0)
        def _():
            dw_ref[...] += dw
            db_ref[...] += db

    wide = pl.BlockSpec((tm, kd), lambda i: (i, 0))
    return pl.pallas_call(
        body, name="gla_gate_bwd", grid=(t_len // tm,),
        in_specs=[wide, wide, pl.BlockSpec((tm, A_PAD), lambda i: (i, 0)), pl.BlockSpec((A_PAD, kd), lambda i: (0, 0))],
        out_specs=(pl.BlockSpec((tm, A_PAD), lambda i: (i, 0)), pl.BlockSpec((A_PAD, kd), lambda i: (0, 0)),
                   pl.BlockSpec((1, kd), lambda i: (0, 0))),
        out_shape=(jax.ShapeDtypeStruct((t_len, A_PAD), BF16), jax.ShapeDtypeStruct((A_PAD, kd), F32),
                   jax.ShapeDtypeStruct((1, kd), F32)),
        compiler_params=_params(1))(dcum, ga, a, w_a2p)


GLA_STEP_CHUNKS = 4


def _gla_dims():
    dk = GLA_KEY_DIM // GLA_HEADS
    dv = GLA_VAL_DIM // GLA_HEADS
    return dk, dv


def _gla_fwd(proj, cum):
    t_len = proj.shape[0]
    dk, dv = _gla_dims()
    nc = t_len // GLA_CHUNK
    c = GLA_CHUNK
    scale = dk ** -0.5
    v0 = 2 * GLA_KEY_DIM // dv

    per = _tile(nc, (GLA_STEP_CHUNKS, 2, 1))
    rows = per * c

    def body(q_ref, k_ref, v_ref, cum_ref, o_ref, st_ref, s_scr):
        @pl.when(pl.program_id(1) == 0)
        def _():
            s_scr[...] = jnp.zeros_like(s_scr)

        tri = lax.broadcasted_iota(jnp.int32, (c, c), 0) >= lax.broadcasted_iota(jnp.int32, (c, c), 1)
        for i in range(per):
            rs = slice(i * c, (i + 1) * c)
            cm = cum_ref[rs, :]
            last = cm[c - 1:c, :]
            q = q_ref[rs, :].astype(F32) * scale
            k = k_ref[rs, :].astype(F32)
            v = v_ref[rs, :].astype(BF16)
            qd = (q * jnp.exp(cm)).astype(BF16)
            ki = (k * jnp.exp(-cm)).astype(BF16)
            ke = (k * jnp.exp(last - cm)).astype(BF16)
            sc = jnp.where(tri, _dot(qd, ki, NT), 0.0)
            st = s_scr[...]
            st_ref[i] = st
            o_ref[rs, :] = _dot(sc.astype(BF16), v, NN) + _dot(qd, st.astype(BF16), NT)
            s_scr[...] = st * jnp.exp(last) + _dot(v, ke, TN)

    return pl.pallas_call(
        body, name="gla_fwd", grid=(GLA_HEADS, nc // per),
        in_specs=[pl.BlockSpec((rows, dk), lambda h, n: (n, h)),
                  pl.BlockSpec((rows, dk), lambda h, n: (n, GLA_HEADS + h)),
                  pl.BlockSpec((rows, dv), lambda h, n: (n, v0 + h)),
                  pl.BlockSpec((rows, dk), lambda h, n: (n, h))],
        out_specs=(pl.BlockSpec((rows, dv), lambda h, n: (n, h)),
                   pl.BlockSpec((None, per, dv, dk), lambda h, n: (h, n, 0, 0))),
        out_shape=(jax.ShapeDtypeStruct((t_len, GLA_VAL_DIM), F32),
                   jax.ShapeDtypeStruct((GLA_HEADS, nc, dv, dk), F32)),
        scratch_shapes=[pltpu.VMEM((dv, dk), F32)], compiler_params=_params(2))(proj, proj, proj, cum)


def _gla_bwd(proj, cum, states, do):
    t_len = proj.shape[0]
    dk, dv = _gla_dims()
    nc = t_len // GLA_CHUNK
    c = GLA_CHUNK
    scale = dk ** -0.5
    v0 = 2 * GLA_KEY_DIM // dv

    per = _tile(nc, (GLA_STEP_CHUNKS, 2, 1))
    rows = per * c

    def body(q_ref, k_ref, v_ref, cum_ref, st_ref, do_ref, dq_ref, dk_ref, dv_ref, dc_ref, ds_scr):
        @pl.when(pl.program_id(1) == 0)
        def _():
            ds_scr[...] = jnp.zeros_like(ds_scr)

        tri = lax.broadcasted_iota(jnp.int32, (c, c), 0) >= lax.broadcasted_iota(jnp.int32, (c, c), 1)
        row = lax.broadcasted_iota(jnp.int32, (c, dk), 0)
        for i in reversed(range(per)):
            rs = slice(i * c, (i + 1) * c)
            cm = cum_ref[rs, :]
            last = cm[c - 1:c, :]
            e_c = jnp.exp(cm)
            e_nc = jnp.exp(-cm)
            e_lc = jnp.exp(last - cm)
            e_l = jnp.exp(last)
            q = q_ref[rs, :].astype(F32) * scale
            k = k_ref[rs, :].astype(F32)
            v = v_ref[rs, :].astype(BF16)
            dov = do_ref[rs, :]
            qd32 = q * e_c
            ki32 = k * e_nc
            ke32 = k * e_lc
            qd = qd32.astype(BF16)
            ki = ki32.astype(BF16)
            ke = ke32.astype(BF16)
            st = st_ref[i]
            dst = ds_scr[...]
            dstb = dst.astype(BF16)
            am = jnp.where(tri, _dot(dov, v, NT), 0.0).astype(BF16)
            pm = jnp.where(tri, _dot(qd, ki, NT), 0.0).astype(BF16)
            dqd = _dot(am, ki, NN) + _dot(dov, st.astype(BF16), NN)
            dki = _dot(am, qd, TN)
            dvv = _dot(pm, dov, TN) + _dot(ke, dstb, NT)
            dke = _dot(v, dstb, NN)
            d_el = jnp.sum(dst * st, axis=0, keepdims=True)
            ds_scr[...] = dst * e_l + _dot(dov, qd, TN)
            dq_ref[rs, :] = (dqd * scale * e_c).astype(dq_ref.dtype)
            dk_ref[rs, :] = (dki * e_nc + dke * e_lc).astype(dk_ref.dtype)
            dv_ref[rs, :] = dvv.astype(dv_ref.dtype)
            dkeke = dke * ke32
            dcum = dqd * qd32 - dki * ki32 - dkeke
            dlast = jnp.sum(dkeke, axis=0, keepdims=True) + d_el * e_l
            dc_ref[rs, :] = jnp.where(row == c - 1, dcum + dlast, dcum)

    rev = nc // per - 1
    return pl.pallas_call(
        body, name="gla_bwd", grid=(GLA_HEADS, nc // per),
        in_specs=[pl.BlockSpec((rows, dk), lambda h, n: (rev - n, h)),
                  pl.BlockSpec((rows, dk), lambda h, n: (rev - n, GLA_HEADS + h)),
                  pl.BlockSpec((rows, dv), lambda h, n: (rev - n, v0 + h)),
                  pl.BlockSpec((rows, dk), lambda h, n: (rev - n, h)),
                  pl.BlockSpec((None, per, dv, dk), lambda h, n: (h, rev - n, 0, 0)),
                  pl.BlockSpec((rows, dv), lambda h, n: (rev - n, h))],
        out_specs=(pl.BlockSpec((rows, dk), lambda h, n: (rev - n, h)),
                   pl.BlockSpec((rows, dk), lambda h, n: (rev - n, h)),
                   pl.BlockSpec((rows, dv), lambda h, n: (rev - n, h)),
                   pl.BlockSpec((rows, dk), lambda h, n: (rev - n, h))),
        out_shape=(jax.ShapeDtypeStruct((t_len, GLA_KEY_DIM), BF16), jax.ShapeDtypeStruct((t_len, GLA_KEY_DIM), BF16),
                   jax.ShapeDtypeStruct((t_len, GLA_VAL_DIM), BF16), jax.ShapeDtypeStruct((t_len, GLA_KEY_DIM), F32)),
        scratch_shapes=[pltpu.VMEM((dv, dk), F32)], compiler_params=_params(2))(proj, proj, proj, cum, states, do)


def _gla_out_fwd(o, proj, gn):
    t_len = o.shape[0]
    _, dv = _gla_dims()
    tm = _tile(t_len, (512, 256, 128))
    r0 = (2 * GLA_KEY_DIM + GLA_VAL_DIM) // dv

    def body(o_ref, r_ref, g_ref, y_ref):
        ov = o_ref[...]
        rs = lax.rsqrt(jnp.mean(ov * ov, axis=-1, keepdims=True) + EPS)
        rv = r_ref[...].astype(F32)
        y_ref[...] = (ov * rs * g_ref[...] * (rv * _sigmoid(rv))).astype(y_ref.dtype)

    return pl.pallas_call(
        body, name="gla_out_fwd", grid=(t_len // tm, GLA_HEADS),
        in_specs=[pl.BlockSpec((tm, dv), lambda i, h: (i, h)), pl.BlockSpec((tm, dv), lambda i, h: (i, r0 + h)),
                  pl.BlockSpec((1, dv), lambda i, h: (0, 0))],
        out_specs=pl.BlockSpec((tm, dv), lambda i, h: (i, h)),
        out_shape=jax.ShapeDtypeStruct((t_len, GLA_VAL_DIM), BF16), compiler_params=_params(2))(o, proj, gn)


def _gla_out_bwd(dy, o, proj, gn):
    t_len = o.shape[0]
    _, dv = _gla_dims()
    tm = _tile(t_len, (512, 256, 128))
    r0 = (2 * GLA_KEY_DIM + GLA_VAL_DIM) // dv

    def body(dy_ref, o_ref, r_ref, g_ref, do_ref, dr_ref, dg_ref):
        ov = o_ref[...]
        gv = g_ref[...]
        rs = lax.rsqrt(jnp.mean(ov * ov, axis=-1, keepdims=True) + EPS)
        xhat = ov * rs
        rv = r_ref[...].astype(F32)
        sg = _sigmoid(rv)
        gate = rv * sg
        dyv = dy_ref[...].astype(F32)
        dn = dyv * gate
        dr_ref[...] = (dyv * xhat * gv * (sg * (1.0 + rv * (1.0 - sg)))).astype(dr_ref.dtype)
        dxn = dn * gv
        do_ref[...] = (rs * (dxn - xhat * jnp.mean(dxn * xhat, axis=-1, keepdims=True))).astype(do_ref.dtype)
        part = jnp.sum(dn * xhat, axis=0, keepdims=True)
        first = (pl.program_id(0) == 0) & (pl.program_id(1) == 0)

        @pl.when(first)
        def _():
            dg_ref[...] = part

        @pl.when(jnp.logical_not(first))
        def _():
            dg_ref[...] += part

    blk = pl.BlockSpec((tm, dv), lambda i, h: (i, h))
    return pl.pallas_call(
        body, name="gla_out_bwd", grid=(t_len // tm, GLA_HEADS),
        in_specs=[blk, blk, pl.BlockSpec((tm, dv), lambda i, h: (i, r0 + h)), pl.BlockSpec((1, dv), lambda i, h: (0, 0))],
        out_specs=(blk, blk, pl.BlockSpec((1, dv), lambda i, h: (0, 0))),
        out_shape=(jax.ShapeDtypeStruct((t_len, GLA_VAL_DIM), BF16), jax.ShapeDtypeStruct((t_len, GLA_VAL_DIM), BF16),
                   jax.ShapeDtypeStruct((1, dv), F32)),
        compiler_params=_params(2))(dy, o, proj, gn)


def _alibi_slopes():
    n = ATT_HEADS
    start = 2.0 ** (-8.0 / n)
    return [start ** (i + 1) for i in range(n)]


def _att_masks(d):
    b = ATT_BLOCK
    qa = lax.broadcasted_iota(jnp.int32, (b, b), 0)
    kb = lax.broadcasted_iota(jnp.int32, (b, b), 1)
    dist_c = qa - kb
    dist_p = qa - kb + b
    return dist_c >= 0, dist_p <= b, (dist_c * d).astype(F32), (dist_p * d).astype(F32)


def _to_dilated(name, x, d, c0=0, w=None):
    part = x if w is None else x[:, c0:c0 + w]
    return part.reshape(x.shape[0] // d, -1)


def _from_dilated(name, y, d):
    return y.reshape(y.shape[0] * d, y.shape[1] // d)


def _att_views(q_all, kv, g):
    d = DILATIONS[g]
    hd = ATT_HEADS * HEAD_DIM
    if d == 1:
        return q_all, kv
    return _to_dilated(f"q_dilated{g}", q_all, d, g * hd, hd), _to_dilated(f"kv_dilated{g}", kv, d)


def _att_fwd(views, g):
    d = DILATIONS[g]
    assert WINDOWS[g] // d == ATT_BLOCK
    qv, kvv = views
    hd = ATT_HEADS * HEAD_DIM
    sub = kvv.shape[0]
    t_len = sub * d
    nb = sub // ATT_BLOCK
    b = ATT_BLOCK
    e = HEAD_DIM
    scale = e ** -0.5
    slopes = _alibi_slopes()
    qc = (lambda r: 3 * r + g) if d == 1 else (lambda r: r)

    def body(q_ref, kp_ref, kc_ref, vp_ref, vc_ref, o_ref, l_ref, s_scr, p_scr, li_scr):
        ib = pl.program_id(1)
        valid_c, valid_p0, dist_c, dist_p = _att_masks(d)
        valid_p = valid_p0 & (ib > 0)
        for h in range(ATT_HEADS):
            hs = slice(h * e, (h + 1) * e)
            qh = q_ref[:, hs]
            s_scr[h, 0] = _dot(qh, kc_ref[:, hs], NT)
            s_scr[h, 1] = _dot(qh, kp_ref[:, hs], NT)
        l_ref[...] = jnp.zeros_like(l_ref)
        for h in range(ATT_HEADS):
            s_c = jnp.where(valid_c, s_scr[h, 0] * scale - slopes[h] * dist_c, NEG)
            s_p = jnp.where(valid_p, s_scr[h, 1] * scale - slopes[h] * dist_p, NEG)
            m = jnp.maximum(jnp.max(s_c, axis=1, keepdims=True), jnp.max(s_p, axis=1, keepdims=True))
            p_c = jnp.where(valid_c, jnp.exp(s_c - m), 0.0)
            p_p = jnp.where(valid_p, jnp.exp(s_p - m), 0.0)
            l = jnp.sum(p_c, axis=1, keepdims=True) + jnp.sum(p_p, axis=1, keepdims=True)
            p_scr[h, 0] = p_c.astype(BF16)
            p_scr[h, 1] = p_p.astype(BF16)
            li_scr[:, h:h + 1] = 1.0 / l
            l_ref[:, h:h + 1] = m + jnp.log(l)
        for h in range(ATT_HEADS):
            hs = slice(h * e, (h + 1) * e)
            acc = _dot(p_scr[h, 0], vc_ref[:, hs], NN) + _dot(p_scr[h, 1], vp_ref[:, hs], NN)
            o_ref[:, hs] = acc * li_scr[:, h:h + 1]

    blk = (b, hd)
    cblk = (b, LANE)
    o, lse = pl.pallas_call(
        body, name=f"att_fwd{g}", grid=(d, nb),
        scratch_shapes=[pltpu.VMEM((ATT_HEADS, 2, b, b), F32), pltpu.VMEM((ATT_HEADS, 2, b, b), BF16),
                        pltpu.VMEM((b, LANE), F32)],
        in_specs=[pl.BlockSpec(blk, lambda r, i: (i, qc(r))),
                  pl.BlockSpec(blk, lambda r, i: (jnp.maximum(i - 1, 0), 2 * r)),
                  pl.BlockSpec(blk, lambda r, i: (i, 2 * r)),
                  pl.BlockSpec(blk, lambda r, i: (jnp.maximum(i - 1, 0), 2 * r + 1)),
                  pl.BlockSpec(blk, lambda r, i: (i, 2 * r + 1))],
        out_specs=(pl.BlockSpec(blk, lambda r, i: (i, r)), pl.BlockSpec(cblk, lambda r, i: (i, r))),
        out_shape=(jax.ShapeDtypeStruct((sub, d * hd), F32), jax.ShapeDtypeStruct((sub, d * LANE), F32)),
        compiler_params=_params(2))(qv, kvv, kvv, kvv, kvv)
    return _from_dilated(f"o_natural{g}", o, d), lse.reshape(t_len, LANE)


def _att_merge(os, ls):
    t_len, hd = os[0].shape
    tm = _tile(t_len, (256, 128))
    e = HEAD_DIM

    def body(o0, o1, o2, l0, l1, l2, of_ref, ob_ref, l_ref):
        a0, a1, a2 = l0[...], l1[...], l2[...]
        m = jnp.maximum(jnp.maximum(a0, a1), a2)
        e0, e1, e2 = jnp.exp(a0 - m), jnp.exp(a1 - m), jnp.exp(a2 - m)
        den = e0 + e1 + e2
        w0, w1, w2 = e0 / den, e1 / den, e2 / den
        l_ref[...] = m + jnp.log(den)
        for h in range(ATT_HEADS):
            hs = slice(h * e, (h + 1) * e)
            c = slice(h, h + 1)
            o = w0[:, c] * o0[:, hs] + w1[:, c] * o1[:, hs] + w2[:, c] * o2[:, hs]
            of_ref[:, hs] = o
            ob_ref[:, hs] = o.astype(ob_ref.dtype)

    row = pl.BlockSpec((tm, hd), lambda i: (i, 0))
    crow = pl.BlockSpec((tm, LANE), lambda i: (i, 0))
    return pl.pallas_call(
        body, name="att_merge", grid=(t_len // tm,), in_specs=[row] * 3 + [crow] * 3, out_specs=(row, row, crow),
        out_shape=(jax.ShapeDtypeStruct((t_len, hd), F32), jax.ShapeDtypeStruct((t_len, hd), BF16),
                   jax.ShapeDtypeStruct((t_len, LANE), F32)),
        compiler_params=_params(1))(*os, *ls)


def _att_delta(do, o):
    t_len, hd = o.shape
    tm = _tile(t_len, (256, 128))
    e = HEAD_DIM

    def body(do_ref, o_ref, d_ref):
        d_ref[...] = jnp.zeros_like(d_ref)
        for h in range(ATT_HEADS):
            hs = slice(h * e, (h + 1) * e)
            d_ref[:, h:h + 1] = jnp.sum(do_ref[:, hs].astype(F32) * o_ref[:, hs], axis=1, keepdims=True)

    row = pl.BlockSpec((tm, hd), lambda i: (i, 0))
    return pl.pallas_call(
        body, name="att_delta", grid=(t_len // tm,), in_specs=[row, row],
        out_specs=pl.BlockSpec((tm, LANE), lambda i: (i, 0)),
        out_shape=jax.ShapeDtypeStruct((t_len, LANE), F32), compiler_params=_params(1))(do, o)


def _att_bwd(views, delta, lse, do, g):
    d = DILATIONS[g]
    qv, kvv = views
    hd = ATT_HEADS * HEAD_DIM
    sub = kvv.shape[0]
    t_len = sub * d
    nb = sub // ATT_BLOCK
    b = ATT_BLOCK
    e = HEAD_DIM
    scale = e ** -0.5
    slopes = _alibi_slopes()
    qc = (lambda r: 3 * r + g) if d == 1 else (lambda r: r)
    dlv = delta.reshape(sub, d * LANE)
    lv = lse.reshape(sub, d * LANE)
    dov = do if d == 1 else _to_dilated(f"do_dilated{g}", do, d)

    def body(qj_ref, qn_ref, kp_ref, kc_ref, vp_ref, vc_ref, doj_ref, don_ref, dj_ref, dn_ref, lj_ref, ln_ref,
             dq_ref, dk_ref, dv_ref, s_scr, dp_scr, p_scr, ds_scr):
        j = pl.program_id(1)
        valid_c, valid_p0, dist_c, dist_p = _att_masks(d)
        valid = (valid_c, valid_p0 & (j > 0), valid_p0 & (j + 1 < nb))
        dist = (dist_c, dist_p, dist_p)
        for h in range(ATT_HEADS):
            hs = slice(h * e, (h + 1) * e)
            qj, qn = qj_ref[:, hs], qn_ref[:, hs]
            kc, kp = kc_ref[:, hs], kp_ref[:, hs]
            vc, vp = vc_ref[:, hs], vp_ref[:, hs]
            doj, don = doj_ref[:, hs], don_ref[:, hs]
            s_scr[h, 0] = _dot(qj, kc, NT)
            s_scr[h, 1] = _dot(qj, kp, NT)
            s_scr[h, 2] = _dot(qn, kc, NT)
            dp_scr[h, 0] = _dot(doj, vc, NT)
            dp_scr[h, 1] = _dot(doj, vp, NT)
            dp_scr[h, 2] = _dot(don, vc, NT)
        for h in range(ATT_HEADS):
            c = slice(h, h + 1)
            lse_t = (lj_ref[:, c], lj_ref[:, c], ln_ref[:, c])
            dlt_t = (dj_ref[:, c], dj_ref[:, c], dn_ref[:, c])
            for t in range(3):
                s = s_scr[h, t] * scale - slopes[h] * dist[t]
                p = jnp.where(valid[t], jnp.exp(jnp.where(valid[t], s - lse_t[t], NEG)), 0.0)
                p_scr[h, t] = p.astype(BF16)
                ds_scr[h, t] = (p * (dp_scr[h, t] - dlt_t[t])).astype(BF16)
        for h in range(ATT_HEADS):
            hs = slice(h * e, (h + 1) * e)
            dq = _dot(ds_scr[h, 0], kc_ref[:, hs], NN) + _dot(ds_scr[h, 1], kp_ref[:, hs], NN)
            dk = _dot(ds_scr[h, 0], qj_ref[:, hs], TN) + _dot(ds_scr[h, 2], qn_ref[:, hs], TN)
            dv = _dot(p_scr[h, 0], doj_ref[:, hs], TN) + _dot(p_scr[h, 2], don_ref[:, hs], TN)
            dq_ref[:, hs] = (dq * scale).astype(dq_ref.dtype)
            dk_ref[:, hs] = (dk * scale).astype(dk_ref.dtype)
            dv_ref[:, hs] = dv.astype(dv_ref.dtype)

    blk = (b, hd)
    cblk = (b, LANE)
    nxt = lambda i: jnp.minimum(i + 1, nb - 1)
    prv = lambda i: jnp.maximum(i - 1, 0)
    tiles = (ATT_HEADS, 3, b, b)
    dq, dk, dv = pl.pallas_call(
        body, name=f"att_bwd{g}", grid=(d, nb),
        scratch_shapes=[pltpu.VMEM(tiles, F32), pltpu.VMEM(tiles, F32), pltpu.VMEM(tiles, BF16), pltpu.VMEM(tiles, BF16)],
        in_specs=[pl.BlockSpec(blk, lambda r, i: (i, qc(r))),
                  pl.BlockSpec(blk, lambda r, i: (nxt(i), qc(r))),
                  pl.BlockSpec(blk, lambda r, i: (prv(i), 2 * r)),
                  pl.BlockSpec(blk, lambda r, i: (i, 2 * r)),
                  pl.BlockSpec(blk, lambda r, i: (prv(i), 2 * r + 1)),
                  pl.BlockSpec(blk, lambda r, i: (i, 2 * r + 1)),
                  pl.BlockSpec(blk, lambda r, i: (i, r)),
                  pl.BlockSpec(blk, lambda r, i: (nxt(i), r)),
                  pl.BlockSpec(cblk, lambda r, i: (i, r)),
                  pl.BlockSpec(cblk, lambda r, i: (nxt(i), r)),
                  pl.BlockSpec(cblk, lambda r, i: (i, r)),
                  pl.BlockSpec(cblk, lambda r, i: (nxt(i), r))],
        out_specs=(pl.BlockSpec(blk, lambda r, i: (i, r)),) * 3,
        out_shape=(jax.ShapeDtypeStruct((sub, d * hd), BF16),) * 3,
        compiler_params=_params(2))(qv, qv, kvv, kvv, kvv, kvv, dov, dov, dlv, dlv, lv, lv)
    return tuple(_from_dilated(f"{n}_natural{g}", t, d) for n, t in (("dq", dq), ("dk", dk), ("dv", dv)))


def _kv_grad_sum(dks, dvs):
    t_len, hd = dks[0].shape
    tm = _tile(t_len, (256, 128))

    def body(k0, k1, k2, v0, v1, v2, o_ref):
        o_ref[:, :hd] = (k0[...].astype(F32) + k1[...].astype(F32) + k2[...].astype(F32)).astype(o_ref.dtype)
        o_ref[:, hd:] = (v0[...].astype(F32) + v1[...].astype(F32) + v2[...].astype(F32)).astype(o_ref.dtype)

    row = pl.BlockSpec((tm, hd), lambda i: (i, 0))
    return pl.pallas_call(
        body, name="kv_grad_sum", grid=(t_len // tm,), in_specs=[row] * 6,
        out_specs=pl.BlockSpec((tm, 2 * hd), lambda i: (i, 0)),
        out_shape=jax.ShapeDtypeStruct((t_len, 2 * hd), BF16), compiler_params=_params(1))(*dks, *dvs)


HALO = 16
INV_SQRT2 = 1.0 / math.sqrt(2.0)
INV_SQRT2PI = 1.0 / math.sqrt(2.0 * math.pi)


def _conv_taps(g, halo, cw, cb):
    row = lax.broadcasted_iota(jnp.int32, g.shape, 0)
    h1 = halo[HALO - 1:HALO, :]
    h2 = halo[HALO - 2:HALO - 1, :]
    g1 = jnp.where(row == 0, h1, pltpu.roll(g, 1, 0))
    g2 = jnp.where(row == 0, h2, jnp.where(row == 1, h1, pltpu.roll(g, 2, 0)))
    gc = cw[0:1, :] * g2 + cw[1:2, :] * g1 + cw[2:3, :] * g + cb
    return gc, g1, g2


def _glu_specs(t_len, f, tm, tc):
    nj = f // tc
    hb = tm // HALO
    u = pl.BlockSpec((tm, tc), lambda j, i: (i, j))
    g = pl.BlockSpec((tm, tc), lambda j, i: (i, nj + j))
    gh = pl.BlockSpec((HALO, tc), lambda j, i: (jnp.maximum(i * hb - 1, 0), nj + j))
    cw = pl.BlockSpec((8, tc), lambda j, i: (0, j))
    cb = pl.BlockSpec((1, tc), lambda j, i: (0, j))
    return u, g, gh, cw, cb


def _glu_fwd(name, up, cw, cb):
    t_len = up.shape[0]
    f = up.shape[1] // 2
    tm = _tile(t_len, (512, 256, 128))
    tc = _tile(f, (1408, 1024, 512, 256, 128))
    u_s, g_s, gh_s, cw_s, cb_s = _glu_specs(t_len, f, tm, tc)

    def body(u_ref, g_ref, gh_ref, cw_ref, cb_ref, o_ref):
        first = pl.program_id(1) == 0
        halo = jnp.where(first, 0.0, gh_ref[...].astype(F32))
        gc, _, _ = _conv_taps(g_ref[...].astype(F32), halo, cw_ref[...], cb_ref[...])
        gel = 0.5 * gc * (1.0 + lax.erf(gc * INV_SQRT2))
        o_ref[...] = (gel * u_ref[...].astype(F32)).astype(o_ref.dtype)

    return pl.pallas_call(
        body, name=name, grid=(f // tc, t_len // tm), in_specs=[u_s, g_s, gh_s, cw_s, cb_s],
        out_specs=pl.BlockSpec((tm, tc), lambda j, i: (i, j)),
        out_shape=jax.ShapeDtypeStruct((t_len, f), BF16), compiler_params=_params(2))(up, up, up, cw, cb)


def _glu_bwd_a(name, dact, up, cw, cb):
    t_len = up.shape[0]
    f = up.shape[1] // 2
    tm = _tile(t_len, (256, 128))
    tc = _tile(f, (1408, 1024, 512, 256, 128))
    u_s, g_s, gh_s, cw_s, cb_s = _glu_specs(t_len, f, tm, tc)

    def body(da_ref, u_ref, g_ref, gh_ref, cw_ref, cb_ref, du_ref, dgc_ref, w0_ref, w1_ref, w2_ref, b_ref):
        first = pl.program_id(1) == 0
        halo = jnp.where(first, 0.0, gh_ref[...].astype(F32))
        g = g_ref[...].astype(F32)
        gc, g1, g2 = _conv_taps(g, halo, cw_ref[...], cb_ref[...])
        phi = 0.5 * (1.0 + lax.erf(gc * INV_SQRT2))
        dgel = phi + gc * jnp.exp(-0.5 * gc * gc) * INV_SQRT2PI
        da = da_ref[...].astype(F32)
        du_ref[...] = (da * gc * phi).astype(du_ref.dtype)
        dgc = da * u_ref[...].astype(F32) * dgel
        dgc_ref[...] = dgc.astype(dgc_ref.dtype)
        parts = (jnp.sum(dgc * g2, axis=0, keepdims=True), jnp.sum(dgc * g1, axis=0, keepdims=True),
                 jnp.sum(dgc * g, axis=0, keepdims=True), jnp.sum(dgc, axis=0, keepdims=True))
        refs = (w0_ref, w1_ref, w2_ref, b_ref)

        @pl.when(first)
        def _():
            for r, p in zip(refs, parts):
                r[...] = p

        @pl.when(jnp.logical_not(first))
        def _():
            for r, p in zip(refs, parts):
                r[...] += p

    tile = pl.BlockSpec((tm, tc), lambda j, i: (i, j))
    vec = pl.BlockSpec((1, tc), lambda j, i: (0, j))
    vshape = jax.ShapeDtypeStruct((1, f), F32)
    return pl.pallas_call(
        body, name=name, grid=(f // tc, t_len // tm), in_specs=[tile, u_s, g_s, gh_s, cw_s, cb_s],
        out_specs=(tile, tile, vec, vec, vec, vec),
        out_shape=(jax.ShapeDtypeStruct((t_len, f), BF16), jax.ShapeDtypeStruct((t_len, f), BF16),
                   vshape, vshape, vshape, vshape),
        compiler_params=_params(2))(dact, up, up, up, cw, cb)


def _glu_bwd_b(name, du, dgc, cw):
    t_len, f = du.shape
    tm = _tile(t_len, (128, 64))
    hb = tm // HALO
    n_i = t_len // tm
    last_hb = t_len // HALO - 1

    def body(du_ref, d_ref, dh_ref, cw_ref, o_ref):
        last = pl.program_id(0) == n_i - 1
        halo = jnp.where(last, 0.0, dh_ref[...].astype(F32))
        dd = d_ref[...].astype(F32)
        row = lax.broadcasted_iota(jnp.int32, dd.shape, 0)
        h0 = halo[0:1, :]
        h1 = halo[1:2, :]
        d1 = jnp.where(row == tm - 1, h0, pltpu.roll(dd, tm - 1, 0))
        d2 = jnp.where(row == tm - 1, h1, jnp.where(row == tm - 2, h0, pltpu.roll(dd, tm - 2, 0)))
        cwv = cw_ref[...]
        dg = cwv[2:3, :] * dd + cwv[1:2, :] * d1 + cwv[0:1, :] * d2
        o_ref[:, :f] = du_ref[...]
        o_ref[:, f:] = dg.astype(o_ref.dtype)

    row_s = pl.BlockSpec((tm, f), lambda i: (i, 0))
    return pl.pallas_call(
        body, name=name, grid=(n_i,),
        in_specs=[row_s, row_s, pl.BlockSpec((HALO, f), lambda i: (jnp.minimum((i + 1) * hb, last_hb), 0)),
                  pl.BlockSpec((8, f), lambda i: (0, 0))],
        out_specs=pl.BlockSpec((tm, 2 * f), lambda i: (i, 0)),
        out_shape=jax.ShapeDtypeStruct((t_len, 2 * f), BF16), compiler_params=_params(1))(du, dgc, dgc, cw)


def _adamw(name, w, g, m, v):
    rows, cols = w.shape
    gcols = g.shape[1]
    n_out = 3 if gcols == cols else 4
    tr = _tile(rows, (256, 128, 64, 32, 16, 8))
    c1 = 1.0 / (1.0 - ADAM_B1 ** ADAM_STEP)
    c2 = 1.0 / (1.0 - ADAM_B2 ** ADAM_STEP)

    def body(w_ref, g_ref, m_ref, v_ref, d_ref, nm_ref, nv_ref, *g_out):
        gv = g_ref[...][:, :cols]
        nm = ADAM_B1 * m_ref[...] + (1.0 - ADAM_B1) * gv
        nv = ADAM_B2 * v_ref[...] + (1.0 - ADAM_B2) * (gv * gv)
        nm_ref[...] = nm
        nv_ref[...] = nv
        d_ref[...] = -ADAM_LR * ((nm * c1) / (jnp.sqrt(nv * c2) + ADAM_EPS) + ADAM_WD * w_ref[...])
        for ref in g_out:
            ref[...] = gv

    blk = pl.BlockSpec((tr, cols), lambda i: (i, 0))
    gblk = pl.BlockSpec((tr, gcols), lambda i: (i, 0))
    shp = jax.ShapeDtypeStruct((rows, cols), F32)
    return pl.pallas_call(body, name=name, grid=(rows // tr,), in_specs=[blk, gblk, blk, blk], out_specs=(blk,) * n_out,
                          out_shape=(shp,) * n_out, compiler_params=_params(1))(w, g, m, v)


def _adamw_layers(name, w, gs, m, v):
    _, rows, cols = w.shape
    tr = _tile(rows, (128, 64, 32, 16, 8))
    nr = rows // tr
    c1 = 1.0 / (1.0 - ADAM_B1 ** ADAM_STEP)
    c2 = 1.0 / (1.0 - ADAM_B2 ** ADAM_STEP)

    def body(w_ref, g0_ref, g1_ref, m_ref, v_ref, d_ref, nm_ref, nv_ref, g_ref):
        gv = jnp.where(pl.program_id(0) == 0, g0_ref[...], g1_ref[...])
        nm = ADAM_B1 * m_ref[...] + (1.0 - ADAM_B1) * gv
        nv = ADAM_B2 * v_ref[...] + (1.0 - ADAM_B2) * (gv * gv)
        nm_ref[...] = nm
        nv_ref[...] = nv
        d_ref[...] = -ADAM_LR * ((nm * c1) / (jnp.sqrt(nv * c2) + ADAM_EPS) + ADAM_WD * w_ref[...])
        g_ref[...] = gv

    blk = pl.BlockSpec((None, tr, cols), lambda l, i: (l, i, 0))
    g0_blk = pl.BlockSpec((tr, cols), lambda l, i: (jnp.where(l == 0, i, nr - 1), 0))
    g1_blk = pl.BlockSpec((tr, cols), lambda l, i: (jnp.where(l == 0, 0, i), 0))
    shp = jax.ShapeDtypeStruct(w.shape, F32)
    return pl.pallas_call(body, name=name, grid=(2, nr), in_specs=[blk, g0_blk, g1_blk, blk, blk], out_specs=(blk,) * 4,
                          out_shape=(shp,) * 4, compiler_params=_params(2))(w, gs[0], gs[1], m, v)


class _NoComm:
    def __init__(self):
        self.grads = {}

    def prefetch(self, group, ws, carry):
        return ws, carry

    def need(self, group, ws, after):
        return ws

    def reduce(self, group, grads, carry):
        self.grads.update(grads)
        return carry

    def tick(self, carry):
        return carry


def _local_step(x, target, ws, norms, small, hooks):
    lay = _layout()

    w_main, w_a = _unpack_gin(ws["gin"])
    hn0 = _rms_fwd("rms_attn0", x, norms["attn0"])
    proj = _mm_plain("gla_proj", hn0, w_main, NN, F32)
    a = _mm_plain("gla_proj_a", hn0, w_a, NN, BF16)
    ga, cum = _gla_gate_fwd(a, small["w_a2p"], small["b_a2"])
    ws, cum = hooks.prefetch("B0", ws, cum)
    o_gla, states = _gla_fwd(proj, cum)
    gated = _gla_out_fwd(o_gla, proj, small["head_norm"])
    ws = hooks.need("B0", ws, gated)
    ws, gated = hooks.prefetch("B1", ws, gated)
    h1 = _mm_act_wr("gla_out", gated, ws["gout"], lay["gout"], add=x)
    ws = hooks.need("B1", ws, h1)

    def ffn_fwd(l, h, own=None, prefetch=None):
        nonlocal ws
        hn = _rms_fwd(f"rms_ffn{l}", h, norms[f"ffn{l}"])
        if own is not None:
            ws, hn = hooks.prefetch(own, ws, hn)
        up = _mm_act_wc(f"ffn_up{l}", hn, ws[f"up{l}"], lay[f"up{l}"], BF16)
        act = _glu_fwd(f"glu_fwd{l}", up, small["conv_w"][l], small["conv_b"][l])
        if own is not None:
            ws = hooks.need(own, ws, act)
        if prefetch is not None:
            ws, act = hooks.prefetch(prefetch, ws, act)
        return hn, up, act, _mm_act_wr(f"ffn_down{l}", act, ws[f"down{l}"], lay[f"down{l}"], add=h)

    hnf0, up0, act0, h2 = ffn_fwd(0, h1, own="B2", prefetch="C1")

    ws = hooks.need("C1", ws, h2)
    kvn = _rms_fwd("rms_kv", h2, norms["kv"])
    kv = _mm_act_wc("kv_proj", kvn, ws["wkv"], lay["wkv"], BF16)
    hn1 = _rms_fwd("rms_attn1", h2, norms["attn1"])
    q_all = _mm_act_wc("q_proj", hn1, ws["wq"], lay["wq"], BF16)
    views = [_att_views(q_all, kv, g) for g in range(3)]
    branch = [_att_fwd(views[g], g) for g in range(3)]
    ws, lse2 = hooks.prefetch("C2", ws, branch[-1][1])
    o_att, o_att_b, lse = _att_merge([br[0] for br in branch], [br[1] for br in branch[:-1]] + [lse2])
    h3 = _mm_act_wr("att_out", o_att_b, ws["dout"], lay["dout"], add=h2)
    ws = hooks.need("C2", ws, h3)
    hnf1, up1, act1, h4 = ffn_fwd(1, h3)

    dh4, d_final, loss = _loss_head(h4, norms["final"], target)

    sm = {"final": d_final}

    def ffn_bwd(l, dh, h, hn, up, act):
        big = {}
        dact = _mm_dact_wrT(f"ffn_down_dx{l}", dh, ws[f"down{l}"], lay[f"down{l}"])
        big[f"down{l}"] = _mm_grad_wr(f"ffn_down_dw{l}", act, dh, lay[f"down{l}"])
        du, dgc, w0, w1, w2, db = _glu_bwd_a(f"glu_bwd_a{l}", dact, up, small["conv_w"][l], small["conv_b"][l])
        sm[f"conv_w{l}"] = (w0, w1, w2)
        sm[f"conv_b{l}"] = db
        dup = hooks.tick(_glu_bwd_b(f"glu_bwd_b{l}", du, dgc, small["conv_w"][l]))
        dhn = _mm_dact_wcT(f"ffn_up_dx{l}", dup, ws[f"up{l}"], lay[f"up{l}"])
        big[f"up{l}"] = _mm_grad_wc(f"ffn_up_dw{l}", hn, dup, lay[f"up{l}"])
        dh_in, sm[f"ffn{l}"] = _rms_bwd(f"rms_ffn_bwd{l}", dhn, h, norms[f"ffn{l}"], dh)
        return hooks.reduce(f"ffn{l}", big, dh_in)

    dh3 = ffn_bwd(1, dh4, h3, hnf1, up1, act1)

    big = {}
    do_att = _mm_dact_wrT("att_out_dx", dh3, ws["dout"], lay["dout"])
    big["dout"] = _mm_grad_wr("att_out_dw", o_att_b, dh3, lay["dout"])
    delta = _att_delta(do_att, o_att)
    bw = [_att_bwd(views[g], delta, lse, do_att, g) for g in range(3)]
    dq_all = jnp.concatenate([t[0] for t in bw], axis=1)
    dhn1 = _mm_dact_wcT("q_proj_dx", dq_all, ws["wq"], lay["wq"])
    big["wq"] = _mm_grad_wc("q_proj_dw", hn1, dq_all, lay["wq"])
    dh2, sm["attn1"] = _rms_bwd("rms_attn1_bwd", dhn1, h2, norms["attn1"], dh3)
    dkv = hooks.tick(_kv_grad_sum([t[1] for t in bw], [t[2] for t in bw]))
    dkvn = _mm_dact_wcT("kv_proj_dx", dkv, ws["wkv"], lay["wkv"])
    big["wkv"] = _mm_grad_wc("kv_proj_dw", kvn, dkv, lay["wkv"])
    dh2, sm["kv"] = _rms_bwd("rms_kv_bwd", dkvn, h2, norms["kv"], dh2)
    dh2 = hooks.reduce("att", big, dh2)

    dh1 = ffn_bwd(0, dh2, h1, hnf0, up0, act0)

    big = {}
    dgated = _mm_dact_wrT("gla_out_dx", dh1, ws["gout"], lay["gout"])
    big["gout"] = _mm_grad_wr("gla_out_dw", gated, dh1, lay["gout"])
    do_gla, dr, sm["head_norm"] = _gla_out_bwd(dgated, o_gla, proj, small["head_norm"])
    dq, dk, dv, dcum = _gla_bwd(proj, cum, states, hooks.tick(do_gla))
    da, sm["w_a2p"], sm["b_a2"] = _gla_gate_bwd(dcum, ga, a, small["w_a2p"])
    dproj = jnp.concatenate([dq, dk, dv, dr], axis=1)
    dhn0 = _mm_plain("gla_proj_dx", dproj, w_main, NT, F32)
    dhn0 = _mm_plain("gla_proj_a_dx", da, w_a, NT, F32, add=dhn0)
    gin_main = _mm_plain("gla_proj_dw", hn0, dproj, TN, BF16)
    gin_a = _mm_plain("gla_proj_a_dw", hn0, da, TN, BF16)
    big["gin"] = _pack_gin_grad(gin_main, gin_a)
    grad_x, sm["attn0"] = _rms_bwd("rms_attn0_bwd", dhn0, x, norms["attn0"], dh1)
    return loss, grad_x, sm, big


def _pack_weights(chip, names, gla_w_in, gla_w_out, w_kv, dsa_w_q, dsa_w_out, ffn_w_up, ffn_w_down):
    gin = gla_w_in[0]
    gin = jnp.pad(gin, ((0, 0), (0, _roundup(gin.shape[1], LANE) - gin.shape[1])))
    shards = {"gin": gin, "gout": gla_w_out[0], "up0": ffn_w_up[0], "up1": ffn_w_up[1], "down0": ffn_w_down[0],
              "down1": ffn_w_down[1], "wq": dsa_w_q[0], "wkv": w_kv, "dout": dsa_w_out[0]}
    out = {}
    for name in names:
        w = shards[name]
        buf = jnp.zeros((N_CHIPS,) + w.shape, BF16)
        out[name] = lax.dynamic_update_slice(buf, w.astype(BF16)[None], (chip, 0, 0))
    return out


def _unpack_gin(w_gin):
    w = _layout()["gin"][1]
    d = w_gin.shape[1]
    wp = w_gin.shape[2]
    n_main = 2 * GLA_KEY_DIM + 2 * GLA_VAL_DIM
    tm = _tile(d, (256, 128, 64, 32, 16))

    def body(s_ref, main_ref, a_ref):
        full = jnp.concatenate([s_ref[s][:, :w] for s in range(N_CHIPS)], axis=1)
        main_ref[...] = full[:, :n_main]
        a_ref[...] = jnp.concatenate([full[:, n_main:], jnp.zeros((tm, A_PAD - GATE_RANK), full.dtype)], axis=1)

    return pl.pallas_call(
        body, name="unpack_gin", grid=(d // tm,), in_specs=[pl.BlockSpec((N_CHIPS, tm, wp), lambda i: (0, i, 0))],
        out_specs=(pl.BlockSpec((tm, n_main), lambda i: (i, 0)), pl.BlockSpec((tm, A_PAD), lambda i: (i, 0))),
        out_shape=(jax.ShapeDtypeStruct((d, n_main), w_gin.dtype), jax.ShapeDtypeStruct((d, A_PAD), w_gin.dtype)),
        compiler_params=_params(1))(w_gin)


def _pack_gin_grad(gin_main, gin_a):
    w = _layout()["gin"][1]
    wp = _roundup(w, LANE)
    d, n_main = gin_main.shape
    tm = _tile(d, (256, 128, 64, 32, 16))

    def body(main_ref, a_ref, o_ref):
        full = jnp.concatenate([main_ref[...], a_ref[:, :GATE_RANK]], axis=1)
        fill = jnp.zeros((tm, wp - w), full.dtype)
        for s in range(N_CHIPS):
            o_ref[s] = jnp.concatenate([full[:, s * w:(s + 1) * w], fill], axis=1)

    return pl.pallas_call(
        body, name="pack_gin_grad", grid=(d // tm,),
        in_specs=[pl.BlockSpec((tm, n_main), lambda i: (i, 0)), pl.BlockSpec((tm, A_PAD), lambda i: (i, 0))],
        out_specs=pl.BlockSpec((N_CHIPS, tm, wp), lambda i: (0, i, 0)),
        out_shape=jax.ShapeDtypeStruct((N_CHIPS, d, wp), gin_main.dtype), compiler_params=_params(1))(gin_main, gin_a)


def _small_params(attn_norm, ffn_norm, kv_norm, final_norm, conv_b, w_a2, b_a2, head_norm, conv_w):
    norms = {"attn0": attn_norm[0:1], "attn1": attn_norm[1:2], "ffn0": ffn_norm[0:1], "ffn1": ffn_norm[1:2],
             "kv": kv_norm[None, :], "final": final_norm[None, :]}
    small = {"w_a2p": jnp.pad(w_a2, ((0, A_PAD - GATE_RANK), (0, 0))), "b_a2": b_a2[None, :],
             "head_norm": head_norm[None, :], "conv_w": jnp.pad(conv_w, ((0, 0), (0, 8 - conv_w.shape[1]), (0, 0))),
             "conv_b": conv_b[:, None, :]}
    return norms, small


ANY = pl.BlockSpec(memory_space=pl.ANY)


def _place():
    return lax.axis_index("x"), lax.axis_index("y"), lax.axis_index("c")


def _other_chips(x, y):
    return [(1 - x, y), (x, 1 - y), (1 - x, 1 - y)]


def _rcopy(src, dst, ssem, rsem, dev):
    return pltpu.make_async_remote_copy(src_ref=src, dst_ref=dst, send_sem=ssem, recv_sem=rsem, device_id=dev,
                                        device_id_type=MESH)


def _pack_shard(name, w, layer, chip_arr, after):
    rows, cols = w.shape[-2:]
    tr = _tile(rows, (512, 352, 256, 128, 64, 32, 16))

    def body(p_ref, w_ref, after_ref, o_ref):
        o_ref[...] = w_ref[...].astype(o_ref.dtype)

    if w.ndim == 3:
        w_spec = pl.BlockSpec((None, tr, cols), lambda i, p: (layer, i, 0))
    else:
        w_spec = pl.BlockSpec((tr, cols), lambda i, p: (i, 0))
    return pl.pallas_call(
        body, name=name,
        grid_spec=pltpu.PrefetchScalarGridSpec(
            num_scalar_prefetch=1, grid=(rows // tr,), in_specs=[w_spec, ANY],
            out_specs=pl.BlockSpec((None, tr, cols), lambda i, p: (p[0], i, 0))),
        out_shape=jax.ShapeDtypeStruct((N_CHIPS, rows, cols), BF16), compiler_params=_params(1))(chip_arr, w, after)


def _swap_halves(name, arrs):
    n = len(arrs)

    def body(*refs):
        ins, outs = refs[:n], refs[n:2 * n]
        send, recv = refs[2 * n:]
        x, y, c = _place()
        cps = []
        for a in range(n):
            h = ins[a].shape[1] // 2
            cp = _rcopy(ins[a].at[:, pl.ds((1 - c) * h, h)], outs[a], send.at[a], recv.at[a], (x, y, 1 - c))
            cp.start()
            cps.append(cp)
        for cp in cps:
            cp.wait()

    return pl.pallas_call(
        body, name=name, in_specs=[ANY] * n, out_specs=[ANY] * n,
        out_shape=[jax.ShapeDtypeStruct((a.shape[0], a.shape[1] // 2, a.shape[2]), a.dtype) for a in arrs],
        scratch_shapes=[pltpu.SemaphoreType.DMA((n,)), pltpu.SemaphoreType.DMA((n,))])(*arrs)


SEM = pl.BlockSpec(memory_space=pltpu.SEMAPHORE)
EFFECT = pltpu.SideEffectType.DATAFLOW_SIDE_EFFECTING


def _shapes(arrs):
    return [jax.ShapeDtypeStruct(a.shape, a.dtype) for a in arrs]


def _gather_start(name, thru, arrs):
    n, nt = len(arrs), len(thru)

    def body(*refs):
        ins = refs[nt:nt + n]
        send, recv = refs[nt + n], refs[nt + n + 1]
        outs = refs[2 * nt + n + 2:]
        x, y, c = _place()
        me = 2 * x + y
        for a in range(n):
            h = ins[a].shape[1] // 2
            mine = pl.ds(c * h, h)
            for j, (px, py) in enumerate(_other_chips(x, y)):
                _rcopy(ins[a].at[me, mine], outs[a].at[me, mine], send.at[3 * a + j], recv.at[3 * a + j], (px, py, c)).start()

    res = pl.pallas_call(
        body, name=name, in_specs=[ANY] * (nt + n), out_specs=[SEM, SEM] + [ANY] * (nt + n),
        out_shape=[pltpu.SemaphoreType.DMA((3 * n,)), pltpu.SemaphoreType.DMA((3 * n,))] + _shapes(thru) + _shapes(arrs),
        input_output_aliases={i: 2 + i for i in range(nt + n)},
        compiler_params=pltpu.CompilerParams(has_side_effects=EFFECT))(*thru, *arrs)
    return res[0], res[1], res[2:2 + nt], res[2 + nt:]


def _gather_wait(name, send, recv, arrs, after):
    n = len(arrs)

    def body(*refs):
        ins = refs[:n]
        send_ref, recv_ref = refs[n], refs[n + 1]
        x, y, c = _place()
        me = 2 * x + y
        for a in range(n):
            h = ins[a].shape[1] // 2
            mine = pl.ds(c * h, h)
            for j, (px, py) in enumerate(_other_chips(x, y)):
                sent = ins[a].at[me, mine]
                landed = ins[a].at[2 * px + py, mine]
                cp = _rcopy(sent, landed, send_ref.at[3 * a + j], recv_ref.at[3 * a + j], (px, py, c))
                cp.wait_send()
                cp.wait_recv()

    after = list(after) if isinstance(after, (list, tuple)) else [after]
    return pl.pallas_call(
        body, name=name, in_specs=[ANY] * n + [SEM, SEM] + [ANY] * len(after), out_specs=[ANY] * n,
        out_shape=_shapes(arrs), input_output_aliases={a: a for a in range(n)},
        compiler_params=pltpu.CompilerParams(has_side_effects=EFFECT))(*arrs, send, recv, *after)


def _forward_halves(name, arrs):
    n = len(arrs)

    def body(*refs):
        ins, outs = refs[:n], refs[n:2 * n]
        send, recv = refs[2 * n:]
        x, y, c = _place()
        sib = (x, y, 1 - c)
        chips = _other_chips(x, y)
        cps = []
        for a in range(n):
            h = ins[a].shape[1] // 2
            mine = pl.ds(c * h, h)
            for j, (px, py) in enumerate(chips):
                cp = _rcopy(ins[a].at[2 * px + py, mine], outs[a].at[2 * px + py, mine], send.at[3 * a + j],
                            recv.at[3 * a + j], sib)
                cp.start()
                cps.append(cp)
        for a in range(n):
            h = ins[a].shape[1] // 2
            theirs = pl.ds((1 - c) * h, h)
            for j, (px, py) in enumerate(chips):
                got = outs[a].at[2 * px + py, theirs]
                _rcopy(got, got, send.at[3 * a + j], recv.at[3 * a + j], sib).wait_recv()
        for cp in cps:
            cp.wait_send()

    return pl.pallas_call(
        body, name=name, in_specs=[ANY] * n, out_specs=[ANY] * n, out_shape=_shapes(arrs),
        input_output_aliases={a: a for a in range(n)},
        scratch_shapes=[pltpu.SemaphoreType.DMA((3 * n,)), pltpu.SemaphoreType.DMA((3 * n,))])(*arrs)


def _forward_start(name, thru, arrs):
    n, nt = len(arrs), len(thru)

    def body(*refs):
        ins = refs[nt:nt + n]
        send, recv = refs[nt + n], refs[nt + n + 1]
        outs = refs[2 * nt + n + 2:]
        x, y, c = _place()
        for a in range(n):
            h = ins[a].shape[1] // 2
            mine = pl.ds(c * h, h)
            for j, (px, py) in enumerate(_other_chips(x, y)):
                _rcopy(ins[a].at[2 * px + py, mine], outs[a].at[2 * px + py, mine], send.at[3 * a + j], recv.at[3 * a + j],
                       (x, y, 1 - c)).start()

    res = pl.pallas_call(
        body, name=name, in_specs=[ANY] * (nt + n), out_specs=[SEM, SEM] + [ANY] * (nt + n),
        out_shape=[pltpu.SemaphoreType.DMA((3 * n,)), pltpu.SemaphoreType.DMA((3 * n,))] + _shapes(thru) + _shapes(arrs),
        input_output_aliases={i: 2 + i for i in range(nt + n)},
        compiler_params=pltpu.CompilerParams(has_side_effects=EFFECT))(*thru, *arrs)
    return res[0], res[1], res[2:2 + nt], res[2 + nt:]


def _forward_wait(name, send, recv, arrs, after):
    n = len(arrs)

    def body(*refs):
        ins = refs[:n]
        send_ref, recv_ref = refs[n], refs[n + 1]
        x, y, c = _place()
        for a in range(n):
            h = ins[a].shape[1] // 2
            for j, (px, py) in enumerate(_other_chips(x, y)):
                sent = ins[a].at[2 * px + py, pl.ds(c * h, h)]
                got = ins[a].at[2 * px + py, pl.ds((1 - c) * h, h)]
                cp = _rcopy(sent, got, send_ref.at[3 * a + j], recv_ref.at[3 * a + j], (x, y, 1 - c))
                cp.wait_send()
                cp.wait_recv()

    return pl.pallas_call(
        body, name=name, in_specs=[ANY] * n + [SEM, SEM, ANY], out_specs=[ANY] * n, out_shape=_shapes(arrs),
        input_output_aliases={a: a for a in range(n)},
        compiler_params=pltpu.CompilerParams(has_side_effects=EFFECT))(*arrs, send, recv, after)


def _scatter_start(name, thru, arrs):
    n, nt = len(arrs), len(thru)
    landing = [lax.empty(a.shape, a.dtype) for a in arrs]

    def body(*refs):
        ins = refs[nt:nt + n]
        send, recv = refs[nt + 2 * n], refs[nt + 2 * n + 1]
        outs = refs[2 * nt + 3 * n + 2:]
        x, y, c = _place()
        me = 2 * x + y
        for a in range(n):
            for j, (px, py) in enumerate(_other_chips(x, y)):
                _rcopy(ins[a].at[2 * px + py], outs[a].at[me], send.at[3 * a + j], recv.at[3 * a + j], (px, py, c)).start()

    res = pl.pallas_call(
        body, name=name, in_specs=[ANY] * (nt + 2 * n), out_specs=[SEM, SEM] + [ANY] * (nt + 2 * n),
        out_shape=[pltpu.SemaphoreType.DMA((3 * n,)), pltpu.SemaphoreType.DMA((3 * n,))] + _shapes(thru) + _shapes(arrs)
        + _shapes(landing),
        input_output_aliases={i: 2 + i for i in range(nt + 2 * n)},
        compiler_params=pltpu.CompilerParams(has_side_effects=EFFECT))(*thru, *arrs, *landing)
    return res[0], res[1], res[2:2 + nt], res[2 + nt:2 + nt + n], res[2 + nt + n:]


def _scatter_wait(name, send, recv, arrs, landing, after):
    n = len(arrs)

    def body(*refs):
        ins, land = refs[:n], refs[n:2 * n]
        send_ref, recv_ref = refs[2 * n], refs[2 * n + 1]
        x, y, c = _place()
        for a in range(n):
            for j, (px, py) in enumerate(_other_chips(x, y)):
                cp = _rcopy(ins[a].at[2 * px + py], land[a].at[2 * px + py], send_ref.at[3 * a + j], recv_ref.at[3 * a + j],
                            (px, py, c))
                cp.wait_send()
                cp.wait_recv()

    res = pl.pallas_call(
        body, name=name, in_specs=[ANY] * (2 * n) + [SEM, SEM, ANY], out_specs=[ANY] * (2 * n),
        out_shape=_shapes(arrs) + _shapes(landing), input_output_aliases={i: i for i in range(2 * n)},
        compiler_params=pltpu.CompilerParams(has_side_effects=EFFECT))(*arrs, *landing, send, recv, after)
    return res[:n], res[n:]


def _swap_start(name, thru, arrs):
    n, nt = len(arrs), len(thru)
    landing = [lax.empty((a.shape[0], a.shape[1] // 2, a.shape[2]), a.dtype) for a in arrs]

    def body(*refs):
        ins = refs[nt:nt + n]
        send, recv = refs[nt + 2 * n], refs[nt + 2 * n + 1]
        outs = refs[2 * nt + 3 * n + 2:]
        x, y, c = _place()
        for a in range(n):
            h = ins[a].shape[1] // 2
            _rcopy(ins[a].at[:, pl.ds((1 - c) * h, h)], outs[a], send.at[a], recv.at[a], (x, y, 1 - c)).start()

    res = pl.pallas_call(
        body, name=name, in_specs=[ANY] * (nt + 2 * n), out_specs=[SEM, SEM] + [ANY] * (nt + 2 * n),
        out_shape=[pltpu.SemaphoreType.DMA((n,)), pltpu.SemaphoreType.DMA((n,))] + _shapes(thru) + _shapes(arrs)
        + _shapes(landing),
        input_output_aliases={i: 2 + i for i in range(nt + 2 * n)},
        compiler_params=pltpu.CompilerParams(has_side_effects=EFFECT))(*thru, *arrs, *landing)
    return res[0], res[1], res[2:2 + nt], res[2 + nt:2 + nt + n], res[2 + nt + n:]


def _swap_wait(name, send, recv, arrs, landing, after):
    n = len(arrs)

    def body(*refs):
        ins, land = refs[:n], refs[n:2 * n]
        send_ref, recv_ref = refs[2 * n], refs[2 * n + 1]
        x, y, c = _place()
        for a in range(n):
            h = ins[a].shape[1] // 2
            cp = _rcopy(ins[a].at[:, pl.ds((1 - c) * h, h)], land[a], send_ref.at[a], recv_ref.at[a], (x, y, 1 - c))
            cp.wait_send()
            cp.wait_recv()

    res = pl.pallas_call(
        body, name=name, in_specs=[ANY] * (2 * n) + [SEM, SEM, ANY], out_specs=[ANY] * (2 * n),
        out_shape=_shapes(arrs) + _shapes(landing), input_output_aliases={i: i for i in range(2 * n)},
        compiler_params=pltpu.CompilerParams(has_side_effects=EFFECT))(*arrs, *landing, send, recv, after)
    return res[:n], res[n:]


def _join_start(name, arrs):
    n = len(arrs)

    def body(*refs):
        ins = refs[:n]
        send, recv = refs[n], refs[n + 1]
        outs = refs[n + 2:]
        x, y, c = _place()
        for a in range(n):
            h = ins[a].shape[0] // 2
            mine = pl.ds(c * h, h)
            _rcopy(ins[a].at[mine], outs[a].at[mine], send.at[a], recv.at[a], (x, y, 1 - c)).start()

    res = pl.pallas_call(
        body, name=name, in_specs=[ANY] * n, out_specs=[SEM, SEM] + [ANY] * n,
        out_shape=[pltpu.SemaphoreType.DMA((n,)), pltpu.SemaphoreType.DMA((n,))] + _shapes(arrs),
        input_output_aliases={i: 2 + i for i in range(n)},
        compiler_params=pltpu.CompilerParams(has_side_effects=EFFECT))(*arrs)
    return res[0], res[1], res[2:]


def _join_wait(name, send, recv, arrs, after):
    n = len(arrs)

    def body(*refs):
        ins = refs[:n]
        send_ref, recv_ref = refs[n], refs[n + 1]
        x, y, c = _place()
        for a in range(n):
            h = ins[a].shape[0] // 2
            cp = _rcopy(ins[a].at[pl.ds(c * h, h)], ins[a].at[pl.ds((1 - c) * h, h)], send_ref.at[a], recv_ref.at[a],
                        (x, y, 1 - c))
            cp.wait_send()
            cp.wait_recv()

    return pl.pallas_call(
        body, name=name, in_specs=[ANY] * n + [SEM, SEM, ANY], out_specs=[ANY] * n, out_shape=_shapes(arrs),
        input_output_aliases={a: a for a in range(n)},
        compiler_params=pltpu.CompilerParams(has_side_effects=EFFECT))(*arrs, send, recv, after)


def _join_halves(name, arrs):
    n = len(arrs)

    def body(*refs):
        ins, outs = refs[:n], refs[n:2 * n]
        send, recv = refs[2 * n:]
        x, y, c = _place()
        cps = []
        for a in range(n):
            h = ins[a].shape[0] // 2
            mine = pl.ds(c * h, h)
            cp = _rcopy(ins[a].at[mine], outs[a].at[mine], send.at[a], recv.at[a], (x, y, 1 - c))
            cp.start()
            cps.append(cp)
        for a in range(n):
            h = ins[a].shape[0] // 2
            got = outs[a].at[pl.ds((1 - c) * h, h)]
            _rcopy(got, got, send.at[a], recv.at[a], (x, y, 1 - c)).wait_recv()
        for cp in cps:
            cp.wait_send()

    return pl.pallas_call(
        body, name=name, in_specs=[ANY] * n, out_specs=[ANY] * n,
        out_shape=[jax.ShapeDtypeStruct(a.shape, a.dtype) for a in arrs],
        input_output_aliases={a: a for a in range(n)},
        scratch_shapes=[pltpu.SemaphoreType.DMA((n,)), pltpu.SemaphoreType.DMA((n,))])(*arrs)


def _allgather8(name, xs, reduce):
    m_per, n = xs.shape

    def body(x_ref, out_ref, *rest):
        if reduce:
            sum_ref, send, recv, lsem = rest
        else:
            send, recv, lsem = rest
        x, y, c = _place()
        me, sib = (x, y, c), (x, y, 1 - c)
        chips = _other_chips(x, y)

        def rows(px, py, pc):
            return out_ref.at[pl.ds((4 * px + 2 * py + pc) * m_per, m_per), :]

        def copy(k, block, to, src=None):
            return _rcopy(rows(*block) if src is None else src, rows(*block), send.at[k], recv.at[k], to)

        mine = pltpu.make_async_copy(x_ref, rows(*me), lsem)
        mine.start()
        first = [copy(0, me, sib, src=x_ref)]
        first += [copy(1 + j, me, (*chip, c), src=x_ref) for j, chip in enumerate(chips)]
        for cp in first:
            cp.start()
        passed = [copy(4 + j, (*chip, c), sib) for j, chip in enumerate(chips)]
        for j, chip in enumerate(chips):
            copy(1 + j, (*chip, c), me).wait_recv()
            passed[j].start()
        copy(0, sib, me).wait_recv()
        for j, chip in enumerate(chips):
            copy(4 + j, (*chip, 1 - c), me).wait_recv()
        for cp in first + passed:
            cp.wait_send()
        mine.wait()
        if reduce:
            acc = out_ref[pl.ds(0, m_per), :]
            for dev in range(1, N_DEV):
                acc = acc + out_ref[pl.ds(dev * m_per, m_per), :]
            sum_ref[...] = acc

    vm = pl.BlockSpec(memory_space=pltpu.VMEM)
    out_shape = [jax.ShapeDtypeStruct((N_DEV * m_per, n), xs.dtype)]
    if reduce:
        out_shape.append(jax.ShapeDtypeStruct((m_per, n), xs.dtype))
    return pl.pallas_call(
        body, name=name, in_specs=[vm], out_specs=[vm] * len(out_shape), out_shape=out_shape,
        scratch_shapes=[pltpu.SemaphoreType.DMA((7,)), pltpu.SemaphoreType.DMA((7,)), pltpu.SemaphoreType.DMA],
        compiler_params=pltpu.CompilerParams(vmem_limit_bytes=VMEM_LIMIT))(xs)


def _add_my_half(name, a, rb, c_arr):
    s, h, cols = rb.shape
    tr = _tile(h, (512, 352, 256, 128, 64, 32, 16))
    nt = h // tr

    def body(c_ref, a_ref, b_ref, o_ref):
        o_ref[...] = (a_ref[...].astype(F32) + b_ref[...].astype(F32)).astype(o_ref.dtype)

    return pl.pallas_call(
        body, name=name,
        grid_spec=pltpu.PrefetchScalarGridSpec(
            num_scalar_prefetch=1, grid=(s, nt),
            in_specs=[pl.BlockSpec((None, tr, cols), lambda k, i, c: (k, c[0] * nt + i, 0)),
                      pl.BlockSpec((None, tr, cols), lambda k, i, c: (k, i, 0))],
            out_specs=pl.BlockSpec((None, tr, cols), lambda k, i, c: (k, i, 0))),
        out_shape=jax.ShapeDtypeStruct(rb.shape, BF16), compiler_params=_params(2))(c_arr, a, rb)


def _sum_chips(name, own, q, place):
    s, h, cols = q.shape
    tr = _tile(h, (512, 352, 256, 128, 64, 32, 16))
    nt = h // tr

    def body(p_ref, own_ref, q1_ref, q2_ref, q3_ref, o_ref):
        acc = own_ref[...].astype(F32) + q1_ref[...].astype(F32)
        o_ref[...] = acc + q2_ref[...].astype(F32) + q3_ref[...].astype(F32)

    def slab(t):
        return pl.BlockSpec((None, tr, cols), lambda i, p: ((p[0] + t) % s, i, 0))

    return pl.pallas_call(
        body, name=name,
        grid_spec=pltpu.PrefetchScalarGridSpec(
            num_scalar_prefetch=1, grid=(nt,), in_specs=[slab(0), slab(1), slab(2), slab(3)],
            out_specs=pl.BlockSpec((tr, cols), lambda i, p: (p[1] * nt + i, 0))),
        out_shape=jax.ShapeDtypeStruct((2 * h, cols), F32), compiler_params=_params(1))(place, own, q, q, q)


def _pack_rows(parts):
    rows = []
    for p in parts:
        flat = p.reshape(-1).astype(F32)
        n = _roundup(flat.shape[0], 8 * LANE)
        rows.append(jnp.pad(flat, (0, n - flat.shape[0])).reshape(-1, LANE))
    return jnp.concatenate(rows, axis=0)


def _unpack_rows(buf, shapes):
    out, r = [], 0
    for shp in shapes:
        size = math.prod(shp)
        nr = _roundup(size, 8 * LANE) // LANE
        out.append(buf[r:r + nr].reshape(-1)[:size].reshape(shp))
        r += nr
    return out


def kernel(x, attn_norm, gla_w_in, gla_w_a2, gla_b_a2, gla_head_norm, gla_w_out, kv_norm, w_kv, dsa_w_q, dsa_w_out, ffn_norm, ffn_w_up, ffn_conv_w, ffn_conv_b, ffn_w_down, final_norm, loss_target, m_attn_norm, m_gla_w_in, m_gla_w_a2, m_gla_b_a2, m_gla_head_norm, m_gla_w_out, m_kv_norm, m_w_kv, m_dsa_w_q, m_dsa_w_out, m_ffn_norm, m_ffn_w_up, m_ffn_conv_w, m_ffn_conv_b, m_ffn_w_down, m_final_norm, v_attn_norm, v_gla_w_in, v_gla_w_a2, v_gla_b_a2, v_gla_head_norm, v_gla_w_out, v_kv_norm, v_w_kv, v_dsa_w_q, v_dsa_w_out, v_ffn_norm, v_ffn_w_up, v_ffn_conv_w, v_ffn_conv_b, v_ffn_w_down, v_final_norm):
    lay = _layout()
    d, f = D_MODEL, D_FF
    cx, cy, cc = _place()
    chip = 2 * cx + cy
    c_arr = jnp.reshape(cc, (1,)).astype(jnp.int32)
    place = jnp.stack([chip, cc]).astype(jnp.int32)

    groups = {"A": ("gin", "small"), "B0": ("gout",), "B1": ("up0",), "B2": ("down0",), "C1": ("wkv", "wq", "dout"),
              "C2": ("up1", "down1")}
    big_shards = (gla_w_in, gla_w_out, w_kv, dsa_w_q, dsa_w_out, ffn_w_up, ffn_w_down)
    ws = _pack_weights(chip, groups["A"][:1], *big_shards)
    sharded_small = [gla_w_a2[0], gla_b_a2[0], gla_head_norm[0], ffn_conv_w]
    packed = _pack_rows(sharded_small)
    packed = jnp.pad(packed, ((0, _roundup(packed.shape[0], 16) - packed.shape[0]), (0, 0)))
    ws["small"] = lax.dynamic_update_slice(jnp.zeros((N_CHIPS,) + packed.shape, F32), packed[None], (chip, 0, 0))
    send, recv, _, arrs = _gather_start("gather_a_start", [], [ws[k] for k in groups["A"]])
    chip_arr = place[:1]
    sources = {"up0": (ffn_w_up, 0), "up1": (ffn_w_up, 1), "down0": (ffn_w_down, 0), "down1": (ffn_w_down, 1),
               "wq": (dsa_w_q, 0), "wkv": (w_kv, 0), "dout": (dsa_w_out, 0), "gout": (gla_w_out, 0)}
    later = ("B0", "B1", "B2", "C1", "C2")
    for k in sum((groups[grp] for grp in later), ()):
        ws[k] = _pack_shard(f"pack_{k}", *sources[k], chip_arr, arrs[1])
    moments = [t.reshape(-1, t.shape[-1]) for t in (m_gla_w_in, v_gla_w_in)]
    arrs = _gather_wait("gather_a_wait", send, recv, arrs, [ws["dout"]] + moments)
    ws.update(zip(groups["A"], _forward_halves("forward_a", arrs)))
    in_flight = {}
    thru = [ws[k] for k in groups["A"]]
    for grp in later:
        send, recv, thru, arrs = _gather_start(f"gather_{grp.lower()}_start", thru, [ws[k] for k in groups[grp]])
        ws.update(zip(groups[grp], arrs))
        in_flight[grp] = (send, recv)
    ws.update(zip(groups["A"], thru))
    pending = []

    class _Comm:
        def prefetch(self, grp, ws, carry):
            send, recv = in_flight[grp]
            arrs = _gather_wait(f"gather_{grp.lower()}_wait", send, recv, [ws[k] for k in groups[grp]], carry)
            send, recv, thru, arrs = _forward_start(f"forward_{grp.lower()}_start", [carry], arrs)
            in_flight[grp] = (send, recv)
            return {**ws, **dict(zip(groups[grp], arrs))}, thru[0]

        def need(self, grp, ws, after):
            send, recv = in_flight[grp]
            arrs = _forward_wait(f"forward_{grp.lower()}_wait", send, recv, [ws[k] for k in groups[grp]], after)
            return {**ws, **dict(zip(groups[grp], arrs))}

        swapping = None

        def reduce(self, grp, grads, carry):
            names = list(grads)
            send, recv, thru, parts, theirs = _swap_start(f"swap_{grp}_start", [carry], [grads[k] for k in names])
            self.swapping = (grp, names, send, recv, parts, theirs)
            return thru[0]

        def tick(self, carry):
            if self.swapping is None:
                return carry
            grp, names, send, recv, parts, theirs = self.swapping
            self.swapping = None
            parts, theirs = _swap_wait(f"swap_{grp}_wait", send, recv, parts, theirs, carry)
            return self.scatter(grp, names, parts, theirs, carry)

        def scatter(self, grp, names, parts, theirs, carry):
            sums = [_add_my_half(f"add_half_{k}", a, b, c_arr) for k, a, b in zip(names, parts, theirs)]
            send, recv, thru, sums, landing = _scatter_start(f"scatter_{grp}_start", [carry], sums)
            pending.append((grp, names, send, recv, sums, landing))
            return thru[0]

        def reduce_now(self, grp, grads, carry):
            names = list(grads)
            parts = [grads[k] for k in names]
            return self.scatter(grp, names, parts, _swap_halves(f"swap_{grp}", parts), carry)

    shards = [_unpack_rows(ws["small"][s], [p.shape for p in sharded_small]) for s in range(N_CHIPS)]
    w_a2, b_a2, head_norm, conv_w = [jnp.concatenate([shards[s][k] for s in range(N_CHIPS)], axis=-1) for k in range(4)]
    norms, small = _small_params(attn_norm, ffn_norm, kv_norm, final_norm, ffn_conv_b, w_a2, b_a2, head_norm, conv_w)

    comm = _Comm()
    loss_blk, grad_x, sm, last_big = _local_step(x[0], loss_target[0], ws, norms, small, comm)

    small_parts = [loss_blk, jnp.concatenate([sm["attn0"], sm["attn1"]]), jnp.concatenate([sm["ffn0"], sm["ffn1"]]),
                   sm["kv"], sm["final"], jnp.concatenate([sm["conv_b0"], sm["conv_b1"]]),
                   sm["w_a2p"][:GATE_RANK], sm["b_a2"], sm["head_norm"],
                   jnp.stack([jnp.concatenate(sm["conv_w0"]), jnp.concatenate(sm["conv_w1"])])]
    small_shapes = [(8, LANE), (2, d), (2, d), (d,), (d,), (2, f), (GATE_RANK, GLA_KEY_DIM), (GLA_KEY_DIM,),
                    (GLA_VAL_DIM // GLA_HEADS,), (2, 3, f)]
    _, reduced = _allgather8("reduce_small", _pack_rows(small_parts), True)
    reduced = comm.reduce_now("gla", last_big, reduced)

    loss_r, g_attn, g_ffn, g_kv, g_final, g_cb, g_a2, g_ba2, g_hn, g_cw = _unpack_rows(reduced, small_shapes)
    loss = loss_r[0, 0]

    def mine(g, axis):
        w = g.shape[axis] // N_CHIPS
        return lax.dynamic_slice_in_dim(g, chip * w, w, axis)

    grads = {
        "attn_norm": g_attn, "gla_w_a2": mine(g_a2, 1)[None], "gla_b_a2": mine(g_ba2, 0)[None],
        "gla_head_norm": mine(g_hn, 0)[None], "kv_norm": g_kv, "ffn_norm": g_ffn, "ffn_conv_w": mine(g_cw, 2),
        "ffn_conv_b": g_cb, "final_norm": g_final,
    }
    weights = {"attn_norm": (attn_norm, m_attn_norm, v_attn_norm), "gla_w_in": (gla_w_in, m_gla_w_in, v_gla_w_in),
               "gla_w_a2": (gla_w_a2, m_gla_w_a2, v_gla_w_a2), "gla_b_a2": (gla_b_a2, m_gla_b_a2, v_gla_b_a2),
               "gla_head_norm": (gla_head_norm, m_gla_head_norm, v_gla_head_norm),
               "gla_w_out": (gla_w_out, m_gla_w_out, v_gla_w_out), "kv_norm": (kv_norm, m_kv_norm, v_kv_norm),
               "w_kv": (w_kv, m_w_kv, v_w_kv), "dsa_w_q": (dsa_w_q, m_dsa_w_q, v_dsa_w_q),
               "dsa_w_out": (dsa_w_out, m_dsa_w_out, v_dsa_w_out), "ffn_norm": (ffn_norm, m_ffn_norm, v_ffn_norm),
               "ffn_w_up": (ffn_w_up, m_ffn_w_up, v_ffn_w_up), "ffn_conv_w": (ffn_conv_w, m_ffn_conv_w, v_ffn_conv_w),
               "ffn_conv_b": (ffn_conv_b, m_ffn_conv_b, v_ffn_conv_b),
               "ffn_w_down": (ffn_w_down, m_ffn_w_down, v_ffn_w_down), "final_norm": (final_norm, m_final_norm, v_final_norm)}
    order = list(weights)
    big_names = ("gla_w_in", "gla_w_out", "w_kv", "dsa_w_q", "dsa_w_out", "ffn_w_up", "ffn_w_down")
    delta, new_m, new_v = {}, {}, {}

    def adam_big(k, g):
        w, m, v = weights[k]
        cols = w.shape[-1]
        res = _adamw(f"adamw_{k}", w.reshape(-1, cols), g.reshape(-1, g.shape[-1]), m.reshape(-1, cols), v.reshape(-1, cols))
        delta[k], new_m[k], new_v[k] = [r.reshape(w.shape) for r in res[:3]]
        grads[k] = res[3].reshape(w.shape) if len(res) == 4 else g
        return res[0]

    full = {}
    after = reduced
    joining = []
    for grp, names, send, recv, sums, landing in pending[:-1]:
        sums, landing = _scatter_wait(f"scatter_{grp}_wait", send, recv, sums, landing, after)
        halves = [_sum_chips(f"sum_chips_{k}", s, q, place) for k, s, q in zip(names, sums, landing)]
        send, recv, halves = _join_start(f"join_{grp}_start", halves)
        joining.append((grp, names, send, recv, halves))
        after = halves[0]
    for grp, names, send, recv, halves in joining:
        joined = _join_wait(f"join_{grp}_wait", send, recv, halves, after)
        full.update(zip(names, joined))
        after = joined[0]
    after = adam_big("w_kv", full["wkv"])
    after = adam_big("dsa_w_q", full["wq"][None])
    after = adam_big("dsa_w_out", full["dout"][None])
    for k, g0, g1 in (("ffn_w_up", "up0", "up1"), ("ffn_w_down", "down0", "down1")):
        delta[k], new_m[k], new_v[k], grads[k] = _adamw_layers(f"adamw_{k}", weights[k][0], (full[g0], full[g1]),
                                                               weights[k][1], weights[k][2])
        after = delta[k]
    grp, names, send, recv, sums, landing = pending[-1]
    sums, landing = _scatter_wait(f"scatter_{grp}_wait", send, recv, sums, landing, after)
    halves = [_sum_chips(f"sum_chips_{k}", s, q, place) for k, s, q in zip(names, sums, landing)]
    full.update(zip(names, _join_halves(f"join_{grp}", halves)))
    adam_big("gla_w_in", full["gin"])
    adam_big("gla_w_out", full["gout"][None])
    small_names = [k for k in order if k not in big_names]
    packed = [_pack_rows([src[k] for k in small_names])
              for src in ({k: weights[k][0] for k in small_names}, grads, {k: weights[k][1] for k in small_names},
                          {k: weights[k][2] for k in small_names})]
    res = _adamw("adamw_small", *packed)
    shapes = [weights[k][0].shape for k in small_names]
    for dst, buf in zip((delta, new_m, new_v), res):
        for k, val in zip(small_names, _unpack_rows(buf, shapes)):
            dst[k] = val
    return (loss, grad_x[None], *[grads[k] for k in order], *[delta[k] for k in order], *[new_m[k] for k in order],
            *[new_v[k] for k in order])
```

```python
import math

import jax
import jax.numpy as jnp
from jax import lax
from jax.experimental import pallas as pl
from jax.experimental.pallas import tpu as pltpu

F32 = jnp.float32
BF16 = jnp.bfloat16

D_MODEL = 2048
SEQ = 4096
GLA_HEADS = 4
GLA_KEY_DIM = D_MODEL // 2
GLA_VAL_DIM = D_MODEL
GATE_RANK = 16
GATE_NORMALIZER = 16.0
GLA_CHUNK = 64
ATT_HEADS = 16
HEAD_DIM = 128
WINDOWS = (128, 512, 2048)
DILATIONS = (1, 4, 16)
ATT_BLOCK = 128
D_FF = 5632
EPS = 1e-6
ADAM_LR = 0.001
ADAM_B1 = 0.9
ADAM_B2 = 0.999
ADAM_EPS = 1e-08
ADAM_WD = 0.01
ADAM_STEP = 10

N_CHIPS = 4
N_DEV = 8
LANE = 128
A_PAD = 128
VMEM_LIMIT = 56 * 1024 * 1024
MAX_K_TILE = 2816
NEG = -1e30
MESH = pl.DeviceIdType.MESH

NN = (((1,), (0,)), ((), ()))
NT = (((1,), (1,)), ((), ()))
TN = (((0,), (0,)), ((), ()))


def _tile(n, cands):
    for c in cands:
        if c <= n and n % c == 0:
            return c
    return n


def _roundup(n, m):
    return -(-n // m) * m


def _params(n_axes):
    return pltpu.CompilerParams(dimension_semantics=("arbitrary",) * n_axes, vmem_limit_bytes=VMEM_LIMIT)


def _dot(a, b, dims):
    return lax.dot_general(a, b, dims, preferred_element_type=F32)


def _sigmoid(x):
    return 1.0 / (1.0 + jnp.exp(-x))


COL_SHARDED = ("gin", "up0", "up1", "wq", "wkv")
ROW_SHARDED = ("gout", "down0", "down1", "dout")


def _layout():
    f = D_FF
    hd = ATT_HEADS * HEAD_DIM
    gin = 2 * GLA_KEY_DIM + 2 * GLA_VAL_DIM + GATE_RANK
    up_w = 2 * f // N_CHIPS
    q_w = 3 * hd // N_CHIPS
    kv_w = 2 * hd // N_CHIPS
    dn_r = f // N_CHIPS
    go_r = GLA_VAL_DIM // N_CHIPS
    do_r = hd // N_CHIPS
    big = (1408, 1024, 512, 256, 128)
    return {
        "gin": (0, gin // N_CHIPS, LANE),
        "up0": (0, up_w, _tile(up_w, big)), "up1": (0, up_w, _tile(up_w, big)),
        "wq": (0, q_w, _tile(q_w, (512, 384, 256, 128))), "wkv": (0, kv_w, _tile(kv_w, (1024, 512, 256, 128))),
        "down0": (0, dn_r, _tile(dn_r, big)), "down1": (0, dn_r, _tile(dn_r, big)),
        "gout": (0, go_r, _tile(go_r, (512, 256, 128))), "dout": (0, do_r, _tile(do_r, (512, 256, 128))),
    }


def _matmul(name, a, b, dims, grid, a_spec, b_spec, o_spec, out_shape, acc_shape, add=None, add_spec=None):
    nk = grid[2]
    has_add = add is not None

    def body(*refs):
        a_ref, b_ref = refs[0], refs[1]
        pos = 2
        add_ref = None
        if has_add:
            add_ref = refs[pos]
            pos += 1
        o_ref = refs[pos]
        prod = _dot(a_ref[...].astype(BF16), b_ref[...].astype(BF16), dims)

        def finish(val):
            if has_add:
                val = val + add_ref[...].astype(F32)
            o_ref[...] = val.astype(o_ref.dtype)

        if nk == 1:
            finish(prod)
        else:
            acc_ref = refs[pos + 1]
            k = pl.program_id(2)

            @pl.when(k == 0)
            def _():
                acc_ref[...] = prod

            @pl.when(k > 0)
            def _():
                acc_ref[...] += prod

            @pl.when(k == nk - 1)
            def _():
                finish(acc_ref[...])

    in_specs = [a_spec, b_spec]
    args = [a, b]
    if has_add:
        in_specs.append(add_spec)
        args.append(add)
    scratch = [] if nk == 1 else [pltpu.VMEM(acc_shape, F32)]
    return pl.pallas_call(body, name=name, grid=grid, in_specs=in_specs, out_specs=o_spec, out_shape=out_shape,
                          scratch_shapes=scratch, compiler_params=_params(3))(*args)


def _mm_act_wc(name, a, wc, seg, out_dtype):
    off, w, tn = seg
    t_len, d = a.shape
    tm = _tile(t_len, (1024, 512, 256, 128))
    nps = w // tn
    ob = off // tn
    grid = (t_len // tm, N_CHIPS * nps, 1)
    return _matmul(
        name, a, wc, NN, grid,
        pl.BlockSpec((tm, d), lambda i, j, k: (i, 0)),
        pl.BlockSpec((None, d, tn), lambda i, j, k: (j // nps, 0, ob + j % nps)),
        pl.BlockSpec((tm, tn), lambda i, j, k: (i, j)),
        jax.ShapeDtypeStruct((t_len, N_CHIPS * w), out_dtype), (tm, tn))


def _mm_dact_wcT(name, dy, wc, seg, add=None):
    off, w, tk = seg
    if off == 0 and w <= MAX_K_TILE:
        tk = w
    t_len = dy.shape[0]
    d = wc.shape[1]
    tm = _tile(t_len, (1024, 512, 256, 128))
    tn = _tile(d, (1024, 512, 256, 128))
    kps = w // tk
    ob = off // tk
    grid = (t_len // tm, d // tn, N_CHIPS * kps)
    return _matmul(
        name, dy, wc, NT, grid,
        pl.BlockSpec((tm, tk), lambda i, j, k: (i, k)),
        pl.BlockSpec((None, tn, tk), lambda i, j, k: (k // kps, j, ob + k % kps)),
        pl.BlockSpec((tm, tn), lambda i, j, k: (i, j)),
        jax.ShapeDtypeStruct((t_len, d), F32), (tm, tn),
        add=add, add_spec=None if add is None else pl.BlockSpec((tm, tn), lambda i, j, k: (i, j)))


def _mm_grad_wc(name, a, dy, seg):
    _, w, tn = seg
    t_len, d = a.shape
    tm = _tile(d, (1024, 512, 256, 128))
    tk = _tile(t_len, (2048, 1024, 512, 256, 128))
    nps = w // tn
    grid = (d // tm, N_CHIPS * nps, t_len // tk)
    return _matmul(
        name, a, dy, TN, grid,
        pl.BlockSpec((tk, tm), lambda i, j, k: (k, i)),
        pl.BlockSpec((tk, tn), lambda i, j, k: (k, j)),
        pl.BlockSpec((None, tm, tn), lambda i, j, k: (j // nps, i, j % nps)),
        jax.ShapeDtypeStruct((N_CHIPS, d, w), BF16), (tm, tn))


def _is_plain(wr, seg):
    return seg[0] == 0 and wr.shape[1] == seg[1] and (N_CHIPS * seg[1]) % 1024 == 0


def _mm_act_wr(name, a, wr, seg, add):
    off, r, tk = seg
    t_len = a.shape[0]
    d = wr.shape[2]
    if seg[0] == 0 and wr.shape[1] == r:
        return _mm_plain(name, a, wr.reshape(N_CHIPS * r, d), NN, F32, add=add)
    tm = _tile(t_len, (1024, 512, 256, 128))
    tn = _tile(d, (1024, 512, 256, 128))
    kps = r // tk
    ob = off // tk
    grid = (t_len // tm, d // tn, N_CHIPS * kps)
    return _matmul(
        name, a, wr, NN, grid,
        pl.BlockSpec((tm, tk), lambda i, j, k: (i, k)),
        pl.BlockSpec((None, tk, tn), lambda i, j, k: (k // kps, ob + k % kps, j)),
        pl.BlockSpec((tm, tn), lambda i, j, k: (i, j)),
        jax.ShapeDtypeStruct((t_len, d), F32), (tm, tn),
        add=add, add_spec=pl.BlockSpec((tm, tn), lambda i, j, k: (i, j)))


def _mm_dact_wrT(name, dh, wr, seg):
    off, r, tn = seg
    t_len, d = dh.shape
    if _is_plain(wr, seg):
        return _mm_plain(name, dh, wr.reshape(N_CHIPS * r, d), NT, BF16)
    tm = _tile(t_len, (1024, 512, 256, 128))
    nps = r // tn
    ob = off // tn
    grid = (t_len // tm, N_CHIPS * nps, 1)
    return _matmul(
        name, dh, wr, NT, grid,
        pl.BlockSpec((tm, d), lambda i, j, k: (i, 0)),
        pl.BlockSpec((None, tn, d), lambda i, j, k: (j // nps, ob + j % nps, 0)),
        pl.BlockSpec((tm, tn), lambda i, j, k: (i, j)),
        jax.ShapeDtypeStruct((t_len, N_CHIPS * r), BF16), (tm, tn))


def _mm_grad_wr(name, a, dh, seg):
    _, r, tm = seg
    t_len, d = dh.shape
    if (N_CHIPS * r) % 1024 == 0:
        return _mm_plain(name, a, dh, TN, BF16).reshape(N_CHIPS, r, d)
    tn = _tile(d, (1024, 512, 256, 128))
    tk = _tile(t_len, (2048, 1024, 512, 256, 128))
    mps = r // tm
    grid = (N_CHIPS * mps, d // tn, t_len // tk)
    return _matmul(
        name, a, dh, TN, grid,
        pl.BlockSpec((tk, tm), lambda i, j, k: (k, i)),
        pl.BlockSpec((tk, tn), lambda i, j, k: (k, j)),
        pl.BlockSpec((None, tm, tn), lambda i, j, k: (i // mps, i % mps, j)),
        jax.ShapeDtypeStruct((N_CHIPS, r, d), BF16), (tm, tn))


def _mm_plain(name, a, b, dims, out_dtype, add=None):
    if dims == NN:
        m, kd = a.shape
        n = b.shape[1]
    elif dims == NT:
        m, kd = a.shape
        n = b.shape[0]
    else:
        kd, m = a.shape
        n = b.shape[1]
    tm = _tile(m, (1024, 512, 256, 128))
    tn = _tile(n, (1024, 768, 512, 256, 128))
    tk = _tile(kd, (MAX_K_TILE, 2048, 1408, 1024, 512, 256, 128))
    grid = (m // tm, n // tn, kd // tk)
    if dims == NN:
        a_spec = pl.BlockSpec((tm, tk), lambda i, j, k: (i, k))
        b_spec = pl.BlockSpec((tk, tn), lambda i, j, k: (k, j))
    elif dims == NT:
        a_spec = pl.BlockSpec((tm, tk), lambda i, j, k: (i, k))
        b_spec = pl.BlockSpec((tn, tk), lambda i, j, k: (j, k))
    else:
        a_spec = pl.BlockSpec((tk, tm), lambda i, j, k: (k, i))
        b_spec = pl.BlockSpec((tk, tn), lambda i, j, k: (k, j))
    o_spec = pl.BlockSpec((tm, tn), lambda i, j, k: (i, j))
    return _matmul(name, a, b, dims, grid, a_spec, b_spec, o_spec, jax.ShapeDtypeStruct((m, n), out_dtype), (tm, tn),
                   add=add, add_spec=None if add is None else o_spec)


def _rms_fwd(name, x, g):
    t_len, d = x.shape
    tm = _tile(t_len, (512, 256, 128))

    def body(x_ref, g_ref, o_ref):
        xv = x_ref[...]
        r = lax.rsqrt(jnp.mean(xv * xv, axis=-1, keepdims=True) + EPS)
        o_ref[...] = (xv * r * g_ref[...]).astype(o_ref.dtype)

    return pl.pallas_call(
        body, name=name, grid=(t_len // tm,),
        in_specs=[pl.BlockSpec((tm, d), lambda i: (i, 0)), pl.BlockSpec((1, d), lambda i: (0, 0))],
        out_specs=pl.BlockSpec((tm, d), lambda i: (i, 0)),
        out_shape=jax.ShapeDtypeStruct((t_len, d), BF16), compiler_params=_params(1))(x, g)


def _rms_bwd(name, dys, x, gs, dres):
    t_len, d = x.shape
    n = len(dys)
    tm = _tile(t_len, (256, 128))

    def body(*refs):
        dy_refs, x_ref, g_refs = refs[:n], refs[n], refs[n + 1:2 * n + 1]
        dres_ref, dx_ref, dg_refs = refs[2 * n + 1], refs[2 * n + 2], refs[2 * n + 3:]
        xv = x_ref[...]
        r = lax.rsqrt(jnp.mean(xv * xv, axis=-1, keepdims=True) + EPS)
        xhat = xv * r
        dyv = [ref[...].astype(F32) for ref in dy_refs]
        dxn = dyv[0] * g_refs[0][...]
        for k in range(1, n):
            dxn = dxn + dyv[k] * g_refs[k][...]
        dx = r * (dxn - xhat * jnp.mean(dxn * xhat, axis=-1, keepdims=True))
        dx_ref[...] = dres_ref[...] + dx
        parts = [jnp.sum(v * xhat, axis=0, keepdims=True) for v in dyv]

        @pl.when(pl.program_id(0) == 0)
        def _():
            for ref, p in zip(dg_refs, parts):
                ref[...] = p

        @pl.when(pl.program_id(0) > 0)
        def _():
            for ref, p in zip(dg_refs, parts):
                ref[...] += p

    row = pl.BlockSpec((tm, d), lambda i: (i, 0))
    vec = pl.BlockSpec((1, d), lambda i: (0, 0))
    res = pl.pallas_call(
        body, name=name, grid=(t_len // tm,), in_specs=[row] * (n + 1) + [vec] * n + [row], out_specs=(row,) + (vec,) * n,
        out_shape=(jax.ShapeDtypeStruct((t_len, d), F32),) + (jax.ShapeDtypeStruct((1, d), F32),) * n,
        compiler_params=_params(1))(*dys, x, *gs, dres)
    return res[0], res[1:]


def _loss_head(h, g, target):
    t_len, d = h.shape
    tm = _tile(t_len, (256, 128))

    def body(h_ref, g_ref, t_ref, dh_ref, dg_ref, loss_ref):
        xv = h_ref[...]
        gv = g_ref[...]
        r = lax.rsqrt(jnp.mean(xv * xv, axis=-1, keepdims=True) + EPS)
        xhat = xv * r
        err = xhat * gv - t_ref[...]
        dyv = err * (1.0 / d)
        dxn = dyv * gv
        dh_ref[...] = r * (dxn - xhat * jnp.mean(dxn * xhat, axis=-1, keepdims=True))
        part = jnp.sum(dyv * xhat, axis=0, keepdims=True)
        lpart = jnp.zeros((8, LANE), F32) + (0.5 / d) * jnp.sum(err * err)

        @pl.when(pl.program_id(0) == 0)
        def _():
            dg_ref[...] = part
            loss_ref[...] = lpart

        @pl.when(pl.program_id(0) > 0)
        def _():
            dg_ref[...] += part
            loss_ref[...] += lpart

    row = pl.BlockSpec((tm, d), lambda i: (i, 0))
    vec = pl.BlockSpec((1, d), lambda i: (0, 0))
    return pl.pallas_call(
        body, name="loss_head", grid=(t_len // tm,), in_specs=[row, vec, row],
        out_specs=(row, vec, pl.BlockSpec((8, LANE), lambda i: (0, 0))),
        out_shape=(jax.ShapeDtypeStruct((t_len, d), F32), jax.ShapeDtypeStruct((1, d), F32),
                   jax.ShapeDtypeStruct((8, LANE), F32)),
        compiler_params=_params(1))(h, g, target)


def _chunk_row(shape):
    return lax.broadcasted_iota(jnp.int32, shape, 0) % GLA_CHUNK


def _gla_gate_fwd(a, w_a2p, b_a2):
    t_len = a.shape[0]
    kd = w_a2p.shape[1]
    tm = _tile(t_len, (256, 128, 64))

    def body(a_ref, w_ref, b_ref, ga_ref, cum_ref):
        ga = _dot(a_ref[...], w_ref[...].astype(BF16), NN) + b_ref[...]
        ga_ref[...] = ga
        la = (jnp.minimum(ga, 0.0) - jnp.log(1.0 + jnp.exp(-jnp.abs(ga)))) * (1.0 / GATE_NORMALIZER)
        row = _chunk_row(la.shape)
        s = 1
        while s < GLA_CHUNK:
            la = la + jnp.where(row >= s, pltpu.roll(la, s, 0), 0.0)
            s *= 2
        cum_ref[...] = la

    return pl.pallas_call(
        body, name="gla_gate_fwd", grid=(t_len // tm,),
        in_specs=[pl.BlockSpec((tm, A_PAD), lambda i: (i, 0)), pl.BlockSpec((A_PAD, kd), lambda i: (0, 0)),
                  pl.BlockSpec((1, kd), lambda i: (0, 0))],
        out_specs=(pl.BlockSpec((tm, kd), lambda i: (i, 0)), pl.BlockSpec((tm, kd), lambda i: (i, 0))),
        out_shape=(jax.ShapeDtypeStruct((t_len, kd), F32), jax.ShapeDtypeStruct((t_len, kd), F32)),
        compiler_params=_params(1))(a, w_a2p, b_a2)


def _gla_gate_bwd(dcum, ga, a, w_a2p):
    t_len, kd = dcum.shape
    tm = _tile(t_len, (256, 128, 64))

    def body(dc_ref, ga_ref, a_ref, w_ref, da_ref, dw_ref, db_ref):
        x = dc_ref[...]
        row = _chunk_row(x.shape)
        s = 1
        while s < GLA_CHUNK:
            x = x + jnp.where(row < GLA_CHUNK - s, pltpu.roll(x, tm - s, 0), 0.0)
            s *= 2
        dga = x * (1.0 / GATE_NORMALIZER) * _sigmoid(-ga_ref[...])
        dgab = dga.astype(BF16)
        da_ref[...] = _dot(dgab, w_ref[...].astype(BF16), NT).astype(da_ref.dtype)
        dw = _dot(a_ref[...], dgab, TN)
        db = jnp.sum(dga, axis=0, keepdims=True)

        @pl.when(pl.program_id(0) == 0)
        def _():
            dw_ref[...] = dw
            db_ref[...] = db

        @pl.when(pl.program_id(0) > 0)
        def _():
            dw_ref[...] += dw
            db_ref[...] += db

    wide = pl.BlockSpec((tm, kd), lambda i: (i, 0))
    return pl.pallas_call(
        body, name="gla_gate_bwd", grid=(t_len // tm,),
        in_specs=[wide, wide, pl.BlockSpec((tm, A_PAD), lambda i: (i, 0)), pl.BlockSpec((A_PAD, kd), lambda i: (0, 0))],
        out_specs=(pl.BlockSpec((tm, A_PAD), lambda i: (i, 0)), pl.BlockSpec((A_PAD, kd), lambda i: (0, 0)),
                   pl.BlockSpec((1, kd), lambda i: (0, 0))),
        out_shape=(jax.ShapeDtypeStruct((t_len, A_PAD), BF16), jax.ShapeDtypeStruct((A_PAD, kd), F32),
                   jax.ShapeDtypeStruct((1, kd), F32)),
        compiler_params=_params(1))(dcum, ga, a, w_a2p)


GLA_STEP_CHUNKS = 4


def _gla_dims():
    dk = GLA_KEY_DIM // GLA_HEADS
    dv = GLA_VAL_DIM // GLA_HEADS
    return dk, dv


def _gla_fwd(proj, cum):
    t_len = proj.shape[0]
    dk, dv = _gla_dims()
    nc = t_len // GLA_CHUNK
    c = GLA_CHUNK
    scale = dk ** -0.5
    v0 = 2 * GLA_KEY_DIM // dv

    per = _tile(nc, (GLA_STEP_CHUNKS, 2, 1))
    rows = per * c

    def body(q_ref, k_ref, v_ref, cum_ref, o_ref, st_ref, s_scr):
        @pl.when(pl.program_id(1) == 0)
        def _():
            s_scr[...] = jnp.zeros_like(s_scr)

        tri = lax.broadcasted_iota(jnp.int32, (c, c), 0) >= lax.broadcasted_iota(jnp.int32, (c, c), 1)
        for i in range(per):
            rs = slice(i * c, (i + 1) * c)
            cm = cum_ref[rs, :]
            last = cm[c - 1:c, :]
            q = q_ref[rs, :].astype(F32) * scale
            k = k_ref[rs, :].astype(F32)
            v = v_ref[rs, :].astype(BF16)
            qd = (q * jnp.exp(cm)).astype(BF16)
            ki = (k * jnp.exp(-cm)).astype(BF16)
            ke = (k * jnp.exp(last - cm)).astype(BF16)
            sc = jnp.where(tri, _dot(qd, ki, NT), 0.0)
            st = s_scr[...]
            st_ref[i] = st
            o_ref[rs, :] = _dot(sc.astype(BF16), v, NN) + _dot(qd, st.astype(BF16), NT)
            s_scr[...] = st * jnp.exp(last) + _dot(v, ke, TN)

    return pl.pallas_call(
        body, name="gla_fwd", grid=(GLA_HEADS, nc // per),
        in_specs=[pl.BlockSpec((rows, dk), lambda h, n: (n, h)),
                  pl.BlockSpec((rows, dk), lambda h, n: (n, GLA_HEADS + h)),
                  pl.BlockSpec((rows, dv), lambda h, n: (n, v0 + h)),
                  pl.BlockSpec((rows, dk), lambda h, n: (n, h))],
        out_specs=(pl.BlockSpec((rows, dv), lambda h, n: (n, h)),
                   pl.BlockSpec((None, per, dv, dk), lambda h, n: (h, n, 0, 0))),
        out_shape=(jax.ShapeDtypeStruct((t_len, GLA_VAL_DIM), F32),
                   jax.ShapeDtypeStruct((GLA_HEADS, nc, dv, dk), F32)),
        scratch_shapes=[pltpu.VMEM((dv, dk), F32)], compiler_params=_params(2))(proj, proj, proj, cum)


def _gla_bwd(proj, cum, states, do):
    t_len = proj.shape[0]
    dk, dv = _gla_dims()
    nc = t_len // GLA_CHUNK
    c = GLA_CHUNK
    scale = dk ** -0.5
    v0 = 2 * GLA_KEY_DIM // dv

    per = _tile(nc, (GLA_STEP_CHUNKS, 2, 1))
    rows = per * c

    def body(q_ref, k_ref, v_ref, cum_ref, st_ref, do_ref, dq_ref, dk_ref, dv_ref, dc_ref, ds_scr):
        @pl.when(pl.program_id(1) == 0)
        def _():
            ds_scr[...] = jnp.zeros_like(ds_scr)

        tri = lax.broadcasted_iota(jnp.int32, (c, c), 0) >= lax.broadcasted_iota(jnp.int32, (c, c), 1)
        row = lax.broadcasted_iota(jnp.int32, (c, dk), 0)
        for i in reversed(range(per)):
            rs = slice(i * c, (i + 1) * c)
            cm = cum_ref[rs, :]
            last = cm[c - 1:c, :]
            e_c = jnp.exp(cm)
            e_nc = jnp.exp(-cm)
            e_lc = jnp.exp(last - cm)
            e_l = jnp.exp(last)
            q = q_ref[rs, :].astype(F32) * scale
            k = k_ref[rs, :].astype(F32)
            v = v_ref[rs, :].astype(BF16)
            dov = do_ref[rs, :]
            qd32 = q * e_c
            ki32 = k * e_nc
            ke32 = k * e_lc
            qd = qd32.astype(BF16)
            ki = ki32.astype(BF16)
            ke = ke32.astype(BF16)
            st = st_ref[i]
            dst = ds_scr[...]
            dstb = dst.astype(BF16)
            am = jnp.where(tri, _dot(dov, v, NT), 0.0).astype(BF16)
            pm = jnp.where(tri, _dot(qd, ki, NT), 0.0).astype(BF16)
            dqd = _dot(am, ki, NN) + _dot(dov, st.astype(BF16), NN)
            dki = _dot(am, qd, TN)
            dvv = _dot(pm, dov, TN) + _dot(ke, dstb, NT)
            dke = _dot(v, dstb, NN)
            d_el = jnp.sum(dst * st, axis=0, keepdims=True)
            ds_scr[...] = dst * e_l + _dot(dov, qd, TN)
            dq_ref[rs, :] = (dqd * scale * e_c).astype(dq_ref.dtype)
            dk_ref[rs, :] = (dki * e_nc + dke * e_lc).astype(dk_ref.dtype)
            dv_ref[rs, :] = dvv.astype(dv_ref.dtype)
            dkeke = dke * ke32
            dcum = dqd * qd32 - dki * ki32 - dkeke
            dlast = jnp.sum(dkeke, axis=0, keepdims=True) + d_el * e_l
            dc_ref[rs, :] = jnp.where(row == c - 1, dcum + dlast, dcum)

    rev = nc // per - 1
    return pl.pallas_call(
        body, name="gla_bwd", grid=(GLA_HEADS, nc // per),
        in_specs=[pl.BlockSpec((rows, dk), lambda h, n: (rev - n, h)),
                  pl.BlockSpec((rows, dk), lambda h, n: (rev - n, GLA_HEADS + h)),
                  pl.BlockSpec((rows, dv), lambda h, n: (rev - n, v0 + h)),
                  pl.BlockSpec((rows, dk), lambda h, n: (rev - n, h)),
                  pl.BlockSpec((None, per, dv, dk), lambda h, n: (h, rev - n, 0, 0)),
                  pl.BlockSpec((rows, dv), lambda h, n: (rev - n, h))],
        out_specs=(pl.BlockSpec((rows, dk), lambda h, n: (rev - n, h)),
                   pl.BlockSpec((rows, dk), lambda h, n: (rev - n, h)),
                   pl.BlockSpec((rows, dv), lambda h, n: (rev - n, h)),
                   pl.BlockSpec((rows, dk), lambda h, n: (rev - n, h))),
        out_shape=(jax.ShapeDtypeStruct((t_len, GLA_KEY_DIM), BF16), jax.ShapeDtypeStruct((t_len, GLA_KEY_DIM), BF16),
                   jax.ShapeDtypeStruct((t_len, GLA_VAL_DIM), BF16), jax.ShapeDtypeStruct((t_len, GLA_KEY_DIM), F32)),
        scratch_shapes=[pltpu.VMEM((dv, dk), F32)], compiler_params=_params(2))(proj, proj, proj, cum, states, do)


def _gla_out_fwd(o, proj, gn):
    t_len = o.shape[0]
    _, dv = _gla_dims()
    tm = _tile(t_len, (512, 256, 128))
    r0 = (2 * GLA_KEY_DIM + GLA_VAL_DIM) // dv

    def body(o_ref, r_ref, g_ref, y_ref):
        ov = o_ref[...]
        rs = lax.rsqrt(jnp.mean(ov * ov, axis=-1, keepdims=True) + EPS)
        rv = r_ref[...].astype(F32)
        y_ref[...] = (ov * rs * g_ref[...] * (rv * _sigmoid(rv))).astype(y_ref.dtype)

    return pl.pallas_call(
        body, name="gla_out_fwd", grid=(t_len // tm, GLA_HEADS),
        in_specs=[pl.BlockSpec((tm, dv), lambda i, h: (i, h)), pl.BlockSpec((tm, dv), lambda i, h: (i, r0 + h)),
                  pl.BlockSpec((1, dv), lambda i, h: (0, 0))],
        out_specs=pl.BlockSpec((tm, dv), lambda i, h: (i, h)),
        out_shape=jax.ShapeDtypeStruct((t_len, GLA_VAL_DIM), BF16), compiler_params=_params(2))(o, proj, gn)


def _gla_out_bwd(dy, o, proj, gn):
    t_len = o.shape[0]
    _, dv = _gla_dims()
    tm = _tile(t_len, (512, 256, 128))
    r0 = (2 * GLA_KEY_DIM + GLA_VAL_DIM) // dv

    def body(dy_ref, o_ref, r_ref, g_ref, do_ref, dr_ref, dg_ref):
        ov = o_ref[...]
        gv = g_ref[...]
        rs = lax.rsqrt(jnp.mean(ov * ov, axis=-1, keepdims=True) + EPS)
        xhat = ov * rs
        rv = r_ref[...].astype(F32)
        sg = _sigmoid(rv)
        gate = rv * sg
        dyv = dy_ref[...].astype(F32)
        dn = dyv * gate
        dr_ref[...] = (dyv * xhat * gv * (sg * (1.0 + rv * (1.0 - sg)))).astype(dr_ref.dtype)
        dxn = dn * gv
        do_ref[...] = (rs * (dxn - xhat * jnp.mean(dxn * xhat, axis=-1, keepdims=True))).astype(do_ref.dtype)
        part = jnp.sum(dn * xhat, axis=0, keepdims=True)
        first = (pl.program_id(0) == 0) & (pl.program_id(1) == 0)

        @pl.when(first)
        def _():
            dg_ref[...] = part

        @pl.when(jnp.logical_not(first))
        def _():
            dg_ref[...] += part

    blk = pl.BlockSpec((tm, dv), lambda i, h: (i, h))
    return pl.pallas_call(
        body, name="gla_out_bwd", grid=(t_len // tm, GLA_HEADS),
        in_specs=[blk, blk, pl.BlockSpec((tm, dv), lambda i, h: (i, r0 + h)), pl.BlockSpec((1, dv), lambda i, h: (0, 0))],
        out_specs=(blk, blk, pl.BlockSpec((1, dv), lambda i, h: (0, 0))),
        out_shape=(jax.ShapeDtypeStruct((t_len, GLA_VAL_DIM), BF16), jax.ShapeDtypeStruct((t_len, GLA_VAL_DIM), BF16),
                   jax.ShapeDtypeStruct((1, dv), F32)),
        compiler_params=_params(2))(dy, o, proj, gn)


def _alibi_slopes():
    n = ATT_HEADS
    start = 2.0 ** (-8.0 / n)
    return [start ** (i + 1) for i in range(n)]


def _att_masks(d):
    b = ATT_BLOCK
    qa = lax.broadcasted_iota(jnp.int32, (b, b), 0)
    kb = lax.broadcasted_iota(jnp.int32, (b, b), 1)
    dist_c = qa - kb
    dist_p = qa - kb + b
    return dist_c >= 0, dist_p <= b, (dist_c * d).astype(F32), (dist_p * d).astype(F32)


def _to_dilated(name, x, d, c0=0, w=None):
    part = x if w is None else x[:, c0:c0 + w]
    return part.reshape(x.shape[0] // d, -1)


def _from_dilated(name, y, d):
    return y.reshape(y.shape[0] * d, y.shape[1] // d)


def _att_views(q_all, kv, g):
    d = DILATIONS[g]
    hd = ATT_HEADS * HEAD_DIM
    if d == 1:
        return q_all, kv
    return _to_dilated(f"q_dilated{g}", q_all, d, g * hd, hd), _to_dilated(f"kv_dilated{g}", kv, d)


def _att_fwd(views, g):
    d = DILATIONS[g]
    assert WINDOWS[g] // d == ATT_BLOCK
    qv, kvv = views
    hd = ATT_HEADS * HEAD_DIM
    sub = kvv.shape[0]
    t_len = sub * d
    nb = sub // ATT_BLOCK
    b = ATT_BLOCK
    e = HEAD_DIM
    scale = e ** -0.5
    slopes = _alibi_slopes()
    qc = (lambda r: 3 * r + g) if d == 1 else (lambda r: r)

    def body(q_ref, kp_ref, kc_ref, vp_ref, vc_ref, o_ref, l_ref, s_scr, p_scr, li_scr):
        ib = pl.program_id(1)
        valid_c, valid_p0, dist_c, dist_p = _att_masks(d)
        valid_p = valid_p0 & (ib > 0)
        for h in range(ATT_HEADS):
            hs = slice(h * e, (h + 1) * e)
            qh = q_ref[:, hs]
            s_scr[h, 0] = _dot(qh, kc_ref[:, hs], NT)
            s_scr[h, 1] = _dot(qh, kp_ref[:, hs], NT)
        l_ref[...] = jnp.zeros_like(l_ref)
        for h in range(ATT_HEADS):
            s_c = jnp.where(valid_c, s_scr[h, 0] * scale - slopes[h] * dist_c, NEG)
            s_p = jnp.where(valid_p, s_scr[h, 1] * scale - slopes[h] * dist_p, NEG)
            m = jnp.maximum(jnp.max(s_c, axis=1, keepdims=True), jnp.max(s_p, axis=1, keepdims=True))
            p_c = jnp.where(valid_c, jnp.exp(s_c - m), 0.0)
            p_p = jnp.where(valid_p, jnp.exp(s_p - m), 0.0)
            l = jnp.sum(p_c, axis=1, keepdims=True) + jnp.sum(p_p, axis=1, keepdims=True)
            p_scr[h, 0] = p_c.astype(BF16)
            p_scr[h, 1] = p_p.astype(BF16)
            li_scr[:, h:h + 1] = 1.0 / l
            l_ref[:, h:h + 1] = m + jnp.log(l)
        for h in range(ATT_HEADS):
            hs = slice(h * e, (h + 1) * e)
            acc = _dot(p_scr[h, 0], vc_ref[:, hs], NN) + _dot(p_scr[h, 1], vp_ref[:, hs], NN)
            o_ref[:, hs] = acc * li_scr[:, h:h + 1]

    blk = (b, hd)
    cblk = (b, LANE)
    o, lse = pl.pallas_call(
        body, name=f"att_fwd{g}", grid=(d, nb),
        scratch_shapes=[pltpu.VMEM((ATT_HEADS, 2, b, b), F32), pltpu.VMEM((ATT_HEADS, 2, b, b), BF16),
                        pltpu.VMEM((b, LANE), F32)],
        in_specs=[pl.BlockSpec(blk, lambda r, i: (i, qc(r))),
                  pl.BlockSpec(blk, lambda r, i: (jnp.maximum(i - 1, 0), 2 * r)),
                  pl.BlockSpec(blk, lambda r, i: (i, 2 * r)),
                  pl.BlockSpec(blk, lambda r, i: (jnp.maximum(i - 1, 0), 2 * r + 1)),
                  pl.BlockSpec(blk, lambda r, i: (i, 2 * r + 1))],
        out_specs=(pl.BlockSpec(blk, lambda r, i: (i, r)), pl.BlockSpec(cblk, lambda r, i: (i, r))),
        out_shape=(jax.ShapeDtypeStruct((sub, d * hd), F32), jax.ShapeDtypeStruct((sub, d * LANE), F32)),
        compiler_params=_params(2))(qv, kvv, kvv, kvv, kvv)
    return _from_dilated(f"o_natural{g}", o, d), lse.reshape(t_len, LANE)


def _att_merge(os, ls):
    t_len, hd = os[0].shape
    tm = _tile(t_len, (256, 128))
    e = HEAD_DIM

    def body(o0, o1, o2, l0, l1, l2, of_ref, ob_ref, l_ref):
        a0, a1, a2 = l0[...], l1[...], l2[...]
        m = jnp.maximum(jnp.maximum(a0, a1), a2)
        e0, e1, e2 = jnp.exp(a0 - m), jnp.exp(a1 - m), jnp.exp(a2 - m)
        den = e0 + e1 + e2
        w0, w1, w2 = e0 / den, e1 / den, e2 / den
        l_ref[...] = m + jnp.log(den)
        for h in range(ATT_HEADS):
            hs = slice(h * e, (h + 1) * e)
            c = slice(h, h + 1)
            o = w0[:, c] * o0[:, hs] + w1[:, c] * o1[:, hs] + w2[:, c] * o2[:, hs]
            of_ref[:, hs] = o
            ob_ref[:, hs] = o.astype(ob_ref.dtype)

    row = pl.BlockSpec((tm, hd), lambda i: (i, 0))
    crow = pl.BlockSpec((tm, LANE), lambda i: (i, 0))
    return pl.pallas_call(
        body, name="att_merge", grid=(t_len // tm,), in_specs=[row] * 3 + [crow] * 3, out_specs=(row, row, crow),
        out_shape=(jax.ShapeDtypeStruct((t_len, hd), F32), jax.ShapeDtypeStruct((t_len, hd), BF16),
                   jax.ShapeDtypeStruct((t_len, LANE), F32)),
        compiler_params=_params(1))(*os, *ls)


def _att_delta(do, o):
    t_len, hd = o.shape
    tm = _tile(t_len, (256, 128))
    e = HEAD_DIM

    def body(do_ref, o_ref, d_ref):
        d_ref[...] = jnp.zeros_like(d_ref)
        for h in range(ATT_HEADS):
            hs = slice(h * e, (h + 1) * e)
            d_ref[:, h:h + 1] = jnp.sum(do_ref[:, hs].astype(F32) * o_ref[:, hs], axis=1, keepdims=True)

    row = pl.BlockSpec((tm, hd), lambda i: (i, 0))
    return pl.pallas_call(
        body, name="att_delta", grid=(t_len // tm,), in_specs=[row, row],
        out_specs=pl.BlockSpec((tm, LANE), lambda i: (i, 0)),
        out_shape=jax.ShapeDtypeStruct((t_len, LANE), F32), compiler_params=_params(1))(do, o)


def _att_bwd(views, delta, lse, do, g):
    d = DILATIONS[g]
    qv, kvv = views
    hd = ATT_HEADS * HEAD_DIM
    sub = kvv.shape[0]
    t_len = sub * d
    nb = sub // ATT_BLOCK
    b = ATT_BLOCK
    e = HEAD_DIM
    scale = e ** -0.5
    slopes = _alibi_slopes()
    qc = (lambda r: 3 * r + g) if d == 1 else (lambda r: r)
    dlv = delta.reshape(sub, d * LANE)
    lv = lse.reshape(sub, d * LANE)
    dov = do if d == 1 else _to_dilated(f"do_dilated{g}", do, d)

    def body(qj_ref, qn_ref, kp_ref, kc_ref, vp_ref, vc_ref, doj_ref, don_ref, dj_ref, dn_ref, lj_ref, ln_ref,
             dq_ref, dk_ref, dv_ref, s_scr, dp_scr, p_scr, ds_scr):
        j = pl.program_id(1)
        valid_c, valid_p0, dist_c, dist_p = _att_masks(d)
        valid = (valid_c, valid_p0 & (j > 0), valid_p0 & (j + 1 < nb))
        dist = (dist_c, dist_p, dist_p)
        for h in range(ATT_HEADS):
            hs = slice(h * e, (h + 1) * e)
            qj, qn = qj_ref[:, hs], qn_ref[:, hs]
            kc, kp = kc_ref[:, hs], kp_ref[:, hs]
            vc, vp = vc_ref[:, hs], vp_ref[:, hs]
            doj, don = doj_ref[:, hs], don_ref[:, hs]
            s_scr[h, 0] = _dot(qj, kc, NT)
            s_scr[h, 1] = _dot(qj, kp, NT)
            s_scr[h, 2] = _dot(qn, kc, NT)
            dp_scr[h, 0] = _dot(doj, vc, NT)
            dp_scr[h, 1] = _dot(doj, vp, NT)
            dp_scr[h, 2] = _dot(don, vc, NT)
        for h in range(ATT_HEADS):
            c = slice(h, h + 1)
            lse_t = (lj_ref[:, c], lj_ref[:, c], ln_ref[:, c])
            dlt_t = (dj_ref[:, c], dj_ref[:, c], dn_ref[:, c])
            for t in range(3):
                s = s_scr[h, t] * scale - slopes[h] * dist[t]
                p = jnp.where(valid[t], jnp.exp(jnp.where(valid[t], s - lse_t[t], NEG)), 0.0)
                p_scr[h, t] = p.astype(BF16)
                ds_scr[h, t] = (p * (dp_scr[h, t] - dlt_t[t])).astype(BF16)
        for h in range(ATT_HEADS):
            hs = slice(h * e, (h + 1) * e)
            dq = _dot(ds_scr[h, 0], kc_ref[:, hs], NN) + _dot(ds_scr[h, 1], kp_ref[:, hs], NN)
            dk = _dot(ds_scr[h, 0], qj_ref[:, hs], TN) + _dot(ds_scr[h, 2], qn_ref[:, hs], TN)
            dv = _dot(p_scr[h, 0], doj_ref[:, hs], TN) + _dot(p_scr[h, 2], don_ref[:, hs], TN)
            dq_ref[:, hs] = (dq * scale).astype(dq_ref.dtype)
            dk_ref[:, hs] = (dk * scale).astype(dk_ref.dtype)
            dv_ref[:, hs] = dv.astype(dv_ref.dtype)

    blk = (b, hd)
    cblk = (b, LANE)
    nxt = lambda i: jnp.minimum(i + 1, nb - 1)
    prv = lambda i: jnp.maximum(i - 1, 0)
    tiles = (ATT_HEADS, 3, b, b)
    dq, dk, dv = pl.pallas_call(
        body, name=f"att_bwd{g}", grid=(d, nb),
        scratch_shapes=[pltpu.VMEM(tiles, F32), pltpu.VMEM(tiles, F32), pltpu.VMEM(tiles, BF16), pltpu.VMEM(tiles, BF16)],
        in_specs=[pl.BlockSpec(blk, lambda r, i: (i, qc(r))),
                  pl.BlockSpec(blk, lambda r, i: (nxt(i), qc(r))),
                  pl.BlockSpec(blk, lambda r, i: (prv(i), 2 * r)),
                  pl.BlockSpec(blk, lambda r, i: (i, 2 * r)),
                  pl.BlockSpec(blk, lambda r, i: (prv(i), 2 * r + 1)),
                  pl.BlockSpec(blk, lambda r, i: (i, 2 * r + 1)),
                  pl.BlockSpec(blk, lambda r, i: (i, r)),
                  pl.BlockSpec(blk, lambda r, i: (nxt(i), r)),
                  pl.BlockSpec(cblk, lambda r, i: (i, r)),
                  pl.BlockSpec(cblk, lambda r, i: (nxt(i), r)),
                  pl.BlockSpec(cblk, lambda r, i: (i, r)),
                  pl.BlockSpec(cblk, lambda r, i: (nxt(i), r))],
        out_specs=(pl.BlockSpec(blk, lambda r, i: (i, r)),) * 3,
        out_shape=(jax.ShapeDtypeStruct((sub, d * hd), BF16),) * 3,
        compiler_params=_params(2))(qv, qv, kvv, kvv, kvv, kvv, dov, dov, dlv, dlv, lv, lv)
    return tuple(_from_dilated(f"{n}_natural{g}", t, d) for n, t in (("dq", dq), ("dk", dk), ("dv", dv)))


def _kv_grad_sum(dks, dvs):
    t_len, hd = dks[0].shape
    tm = _tile(t_len, (256, 128))

    def body(k0, k1, k2, v0, v1, v2, o_ref):
        o_ref[:, :hd] = (k0[...].astype(F32) + k1[...].astype(F32) + k2[...].astype(F32)).astype(o_ref.dtype)
        o_ref[:, hd:] = (v0[...].astype(F32) + v1[...].astype(F32) + v2[...].astype(F32)).astype(o_ref.dtype)

    row = pl.BlockSpec((tm, hd), lambda i: (i, 0))
    return pl.pallas_call(
        body, name="kv_grad_sum", grid=(t_len // tm,), in_specs=[row] * 6,
        out_specs=pl.BlockSpec((tm, 2 * hd), lambda i: (i, 0)),
        out_shape=jax.ShapeDtypeStruct((t_len, 2 * hd), BF16), compiler_params=_params(1))(*dks, *dvs)


HALO = 16
INV_SQRT2 = 1.0 / math.sqrt(2.0)
INV_SQRT2PI = 1.0 / math.sqrt(2.0 * math.pi)


def _conv_taps(g, halo, cw, cb):
    row = lax.broadcasted_iota(jnp.int32, g.shape, 0)
    h1 = halo[HALO - 1:HALO, :]
    h2 = halo[HALO - 2:HALO - 1, :]
    g1 = jnp.where(row == 0, h1, pltpu.roll(g, 1, 0))
    g2 = jnp.where(row == 0, h2, jnp.where(row == 1, h1, pltpu.roll(g, 2, 0)))
    gc = cw[0:1, :] * g2 + cw[1:2, :] * g1 + cw[2:3, :] * g + cb
    return gc, g1, g2


def _glu_specs(t_len, f, tm, tc):
    nj = f // tc
    hb = tm // HALO
    u = pl.BlockSpec((tm, tc), lambda j, i: (i, j))
    g = pl.BlockSpec((tm, tc), lambda j, i: (i, nj + j))
    gh = pl.BlockSpec((HALO, tc), lambda j, i: (jnp.maximum(i * hb - 1, 0), nj + j))
    cw = pl.BlockSpec((8, tc), lambda j, i: (0, j))
    cb = pl.BlockSpec((1, tc), lambda j, i: (0, j))
    return u, g, gh, cw, cb


def _glu_fwd(name, up, cw, cb):
    t_len = up.shape[0]
    f = up.shape[1] // 2
    tm = _tile(t_len, (512, 256, 128))
    tc = _tile(f, (1408, 1024, 512, 256, 128))
    u_s, g_s, gh_s, cw_s, cb_s = _glu_specs(t_len, f, tm, tc)

    def body(u_ref, g_ref, gh_ref, cw_ref, cb_ref, o_ref):
        first = pl.program_id(1) == 0
        halo = jnp.where(first, 0.0, gh_ref[...].astype(F32))
        gc, _, _ = _conv_taps(g_ref[...].astype(F32), halo, cw_ref[...], cb_ref[...])
        gel = 0.5 * gc * (1.0 + lax.erf(gc * INV_SQRT2))
        o_ref[...] = (gel * u_ref[...].astype(F32)).astype(o_ref.dtype)

    return pl.pallas_call(
        body, name=name, grid=(f // tc, t_len // tm), in_specs=[u_s, g_s, gh_s, cw_s, cb_s],
        out_specs=pl.BlockSpec((tm, tc), lambda j, i: (i, j)),
        out_shape=jax.ShapeDtypeStruct((t_len, f), BF16), compiler_params=_params(2))(up, up, up, cw, cb)


def _glu_bwd_a(name, dact, up, cw, cb):
    t_len = up.shape[0]
    f = up.shape[1] // 2
    tm = _tile(t_len, (256, 128))
    tc = _tile(f, (1408, 1024, 512, 256, 128))
    u_s, g_s, gh_s, cw_s, cb_s = _glu_specs(t_len, f, tm, tc)

    def body(da_ref, u_ref, g_ref, gh_ref, cw_ref, cb_ref, du_ref, dgc_ref, w0_ref, w1_ref, w2_ref, b_ref):
        first = pl.program_id(1) == 0
        halo = jnp.where(first, 0.0, gh_ref[...].astype(F32))
        g = g_ref[...].astype(F32)
        gc, g1, g2 = _conv_taps(g, halo, cw_ref[...], cb_ref[...])
        phi = 0.5 * (1.0 + lax.erf(gc * INV_SQRT2))
        dgel = phi + gc * jnp.exp(-0.5 * gc * gc) * INV_SQRT2PI
        da = da_ref[...].astype(F32)
        du_ref[...] = (da * gc * phi).astype(du_ref.dtype)
        dgc = da * u_ref[...].astype(F32) * dgel
        dgc_ref[...] = dgc.astype(dgc_ref.dtype)
        parts = (jnp.sum(dgc * g2, axis=0, keepdims=True), jnp.sum(dgc * g1, axis=0, keepdims=True),
                 jnp.sum(dgc * g, axis=0, keepdims=True), jnp.sum(dgc, axis=0, keepdims=True))
        refs = (w0_ref, w1_ref, w2_ref, b_ref)

        @pl.when(first)
        def _():
            for r, p in zip(refs, parts):
                r[...] = p

        @pl.when(jnp.logical_not(first))
        def _():
            for r, p in zip(refs, parts):
                r[...] += p

    tile = pl.BlockSpec((tm, tc), lambda j, i: (i, j))
    vec = pl.BlockSpec((1, tc), lambda j, i: (0, j))
    vshape = jax.ShapeDtypeStruct((1, f), F32)
    return pl.pallas_call(
        body, name=name, grid=(f // tc, t_len // tm), in_specs=[tile, u_s, g_s, gh_s, cw_s, cb_s],
        out_specs=(tile, tile, vec, vec, vec, vec),
        out_shape=(jax.ShapeDtypeStruct((t_len, 2 * f), BF16), jax.ShapeDtypeStruct((t_len, f), BF16),
                   vshape, vshape, vshape, vshape),
        compiler_params=_params(2))(dact, up, up, up, cw, cb)


def _glu_bwd_b(name, dup, dgc, cw):
    t_len, f = dgc.shape
    tm = _tile(t_len, (256, 128, 64))
    hb = tm // HALO
    n_i = t_len // tm
    last_hb = t_len // HALO - 1

    def body(dup_ref, d_ref, dh_ref, cw_ref, o_ref):
        last = pl.program_id(0) == n_i - 1
        halo = jnp.where(last, 0.0, dh_ref[...].astype(F32))
        dd = d_ref[...].astype(F32)
        row = lax.broadcasted_iota(jnp.int32, dd.shape, 0)
        h0 = halo[0:1, :]
        h1 = halo[1:2, :]
        d1 = jnp.where(row == tm - 1, h0, pltpu.roll(dd, tm - 1, 0))
        d2 = jnp.where(row == tm - 1, h1, jnp.where(row == tm - 2, h0, pltpu.roll(dd, tm - 2, 0)))
        cwv = cw_ref[...]
        dg = cwv[2:3, :] * dd + cwv[1:2, :] * d1 + cwv[0:1, :] * d2
        o_ref[...] = dg.astype(o_ref.dtype)

    row_s = pl.BlockSpec((tm, f), lambda i: (i, 0))
    return pl.pallas_call(
        body, name=name, grid=(n_i,),
        in_specs=[ANY, row_s, pl.BlockSpec((HALO, f), lambda i: (jnp.minimum((i + 1) * hb, last_hb), 0)),
                  pl.BlockSpec((8, f), lambda i: (0, 0))],
        out_specs=pl.BlockSpec((tm, f), lambda i: (i, 1)), input_output_aliases={0: 0},
        out_shape=jax.ShapeDtypeStruct((t_len, 2 * f), BF16), compiler_params=_params(1))(dup, dgc, dgc, cw)


def _adamw(name, w, g, m, v):
    rows, cols = w.shape
    gcols = g.shape[1]
    n_out = 3 if gcols == cols else 4
    tr = _tile(rows, (256, 128, 64, 32, 16, 8))
    c1 = 1.0 / (1.0 - ADAM_B1 ** ADAM_STEP)
    c2 = 1.0 / (1.0 - ADAM_B2 ** ADAM_STEP)

    def body(w_ref, g_ref, m_ref, v_ref, d_ref, nm_ref, nv_ref, *g_out):
        gv = g_ref[...][:, :cols]
        nm = ADAM_B1 * m_ref[...] + (1.0 - ADAM_B1) * gv
        nv = ADAM_B2 * v_ref[...] + (1.0 - ADAM_B2) * (gv * gv)
        nm_ref[...] = nm
        nv_ref[...] = nv
        d_ref[...] = -ADAM_LR * ((nm * c1) / (jnp.sqrt(nv * c2) + ADAM_EPS) + ADAM_WD * w_ref[...])
        for ref in g_out:
            ref[...] = gv

    blk = pl.BlockSpec((tr, cols), lambda i: (i, 0))
    gblk = pl.BlockSpec((tr, gcols), lambda i: (i, 0))
    shp = jax.ShapeDtypeStruct((rows, cols), F32)
    return pl.pallas_call(body, name=name, grid=(rows // tr,), in_specs=[blk, gblk, blk, blk], out_specs=(blk,) * n_out,
                          out_shape=(shp,) * n_out, compiler_params=_params(1))(w, g, m, v)


def _adamw_layers(name, w, gs, m, v):
    _, rows, cols = w.shape
    tr = _tile(rows, (128, 64, 32, 16, 8))
    nr = rows // tr
    c1 = 1.0 / (1.0 - ADAM_B1 ** ADAM_STEP)
    c2 = 1.0 / (1.0 - ADAM_B2 ** ADAM_STEP)

    def body(w_ref, g0_ref, g1_ref, m_ref, v_ref, d_ref, nm_ref, nv_ref, g_ref):
        gv = jnp.where(pl.program_id(0) == 0, g0_ref[...], g1_ref[...])
        nm = ADAM_B1 * m_ref[...] + (1.0 - ADAM_B1) * gv
        nv = ADAM_B2 * v_ref[...] + (1.0 - ADAM_B2) * (gv * gv)
        nm_ref[...] = nm
        nv_ref[...] = nv
        d_ref[...] = -ADAM_LR * ((nm * c1) / (jnp.sqrt(nv * c2) + ADAM_EPS) + ADAM_WD * w_ref[...])
        g_ref[...] = gv

    blk = pl.BlockSpec((None, tr, cols), lambda l, i: (l, i, 0))
    g0_blk = pl.BlockSpec((tr, cols), lambda l, i: (jnp.where(l == 0, i, nr - 1), 0))
    g1_blk = pl.BlockSpec((tr, cols), lambda l, i: (jnp.where(l == 0, 0, i), 0))
    shp = jax.ShapeDtypeStruct(w.shape, F32)
    return pl.pallas_call(body, name=name, grid=(2, nr), in_specs=[blk, g0_blk, g1_blk, blk, blk], out_specs=(blk,) * 4,
                          out_shape=(shp,) * 4, compiler_params=_params(2))(w, gs[0], gs[1], m, v)


class _NoComm:
    def __init__(self):
        self.grads = {}

    def prefetch(self, group, ws, carry):
        return ws, carry

    def need(self, group, ws, after):
        return ws

    def reduce(self, group, grads, carry):
        self.grads.update(grads)
        return carry

    def tick(self, carry):
        return carry


def _local_step(x, target, ws, norms, small, hooks):
    lay = _layout()

    w_main, w_a = _unpack_gin(ws["gin"])
    hn0 = _rms_fwd("rms_attn0", x, norms["attn0"])
    proj = _mm_plain("gla_proj", hn0, w_main, NN, F32)
    a = _mm_plain("gla_proj_a", hn0, w_a, NN, BF16)
    ga, cum = _gla_gate_fwd(a, small["w_a2p"], small["b_a2"])
    ws, cum = hooks.prefetch("B0", ws, cum)
    o_gla, states = _gla_fwd(proj, cum)
    gated = _gla_out_fwd(o_gla, proj, small["head_norm"])
    ws = hooks.need("B0", ws, gated)
    ws, gated = hooks.prefetch("B1", ws, gated)
    h1 = _mm_act_wr("gla_out", gated, ws["gout"], lay["gout"], add=x)
    ws = hooks.need("B1", ws, h1)

    def ffn_fwd(l, h, own=None, prefetch=None):
        nonlocal ws
        hn = _rms_fwd(f"rms_ffn{l}", h, norms[f"ffn{l}"])
        if own is not None:
            ws, hn = hooks.prefetch(own, ws, hn)
        up = _mm_act_wc(f"ffn_up{l}", hn, ws[f"up{l}"], lay[f"up{l}"], BF16)
        act = _glu_fwd(f"glu_fwd{l}", up, small["conv_w"][l], small["conv_b"][l])
        if own is not None:
            ws = hooks.need(own, ws, act)
        if prefetch is not None:
            ws, act = hooks.prefetch(prefetch, ws, act)
        return hn, up, act, _mm_act_wr(f"ffn_down{l}", act, ws[f"down{l}"], lay[f"down{l}"], add=h)

    hnf0, up0, act0, h2 = ffn_fwd(0, h1, own="B2", prefetch="C1")

    ws = hooks.need("C1", ws, h2)
    kvn = _rms_fwd("rms_kv", h2, norms["kv"])
    kv = _mm_act_wc("kv_proj", kvn, ws["wkv"], lay["wkv"], BF16)
    hn1 = _rms_fwd("rms_attn1", h2, norms["attn1"])
    q_all = _mm_act_wc("q_proj", hn1, ws["wq"], lay["wq"], BF16)
    views = [_att_views(q_all, kv, g) for g in range(3)]
    branch = [_att_fwd(views[g], g) for g in range(3)]
    ws, lse2 = hooks.prefetch("C2", ws, branch[-1][1])
    o_att, o_att_b, lse = _att_merge([br[0] for br in branch], [br[1] for br in branch[:-1]] + [lse2])
    h3 = _mm_act_wr("att_out", o_att_b, ws["dout"], lay["dout"], add=h2)
    ws = hooks.need("C2", ws, h3)
    hnf1, up1, act1, h4 = ffn_fwd(1, h3)

    dh4, d_final, loss = _loss_head(h4, norms["final"], target)

    sm = {"final": d_final}

    def ffn_bwd(l, dh, h, hn, up, act):
        big = {}
        dact = _mm_dact_wrT(f"ffn_down_dx{l}", dh, ws[f"down{l}"], lay[f"down{l}"])
        big[f"down{l}"] = _mm_grad_wr(f"ffn_down_dw{l}", act, dh, lay[f"down{l}"])
        du, dgc, w0, w1, w2, db = _glu_bwd_a(f"glu_bwd_a{l}", dact, up, small["conv_w"][l], small["conv_b"][l])
        sm[f"conv_w{l}"] = (w0, w1, w2)
        sm[f"conv_b{l}"] = db
        dup = hooks.tick(_glu_bwd_b(f"glu_bwd_b{l}", du, dgc, small["conv_w"][l]))
        dhn = _mm_dact_wcT(f"ffn_up_dx{l}", dup, ws[f"up{l}"], lay[f"up{l}"])
        big[f"up{l}"] = _mm_grad_wc(f"ffn_up_dw{l}", hn, dup, lay[f"up{l}"])
        dh_in, (sm[f"ffn{l}"],) = _rms_bwd(f"rms_ffn_bwd{l}", [dhn], h, [norms[f"ffn{l}"]], dh)
        return hooks.reduce(f"ffn{l}", big, dh_in)

    dh3 = ffn_bwd(1, dh4, h3, hnf1, up1, act1)

    big = {}
    do_att = _mm_dact_wrT("att_out_dx", dh3, ws["dout"], lay["dout"])
    big["dout"] = _mm_grad_wr("att_out_dw", o_att_b, dh3, lay["dout"])
    delta = _att_delta(do_att, o_att)
    bw = [_att_bwd(views[g], delta, lse, do_att, g) for g in range(3)]
    dq_all = jnp.concatenate([t[0] for t in bw], axis=1)
    dhn1 = _mm_dact_wcT("q_proj_dx", dq_all, ws["wq"], lay["wq"])
    big["wq"] = _mm_grad_wc("q_proj_dw", hn1, dq_all, lay["wq"])
    dkv = hooks.tick(_kv_grad_sum([t[1] for t in bw], [t[2] for t in bw]))
    dkvn = _mm_dact_wcT("kv_proj_dx", dkv, ws["wkv"], lay["wkv"])
    big["wkv"] = _mm_grad_wc("kv_proj_dw", kvn, dkv, lay["wkv"])
    dh2, (sm["attn1"], sm["kv"]) = _rms_bwd("rms_h2_bwd", [dhn1, dkvn], h2, [norms["attn1"], norms["kv"]], dh3)
    dh2 = hooks.reduce("att", big, dh2)

    dh1 = ffn_bwd(0, dh2, h1, hnf0, up0, act0)

    big = {}
    dgated = _mm_dact_wrT("gla_out_dx", dh1, ws["gout"], lay["gout"])
    big["gout"] = _mm_grad_wr("gla_out_dw", gated, dh1, lay["gout"])
    do_gla, dr, sm["head_norm"] = _gla_out_bwd(dgated, o_gla, proj, small["head_norm"])
    dq, dk, dv, dcum = _gla_bwd(proj, cum, states, hooks.tick(do_gla))
    da, sm["w_a2p"], sm["b_a2"] = _gla_gate_bwd(dcum, ga, a, small["w_a2p"])
    dproj = jnp.concatenate([dq, dk, dv, dr], axis=1)
    dhn0 = _mm_plain("gla_proj_dx", dproj, w_main, NT, F32)
    dhn0 = _mm_plain("gla_proj_a_dx", da, w_a, NT, F32, add=dhn0)
    gin_main = _mm_plain("gla_proj_dw", hn0, dproj, TN, BF16)
    gin_a = _mm_plain("gla_proj_a_dw", hn0, da, TN, BF16)
    big["gin"] = _pack_gin_grad(gin_main, gin_a)
    grad_x, (sm["attn0"],) = _rms_bwd("rms_attn0_bwd", [dhn0], x, [norms["attn0"]], dh1)
    return loss, grad_x, sm, big


def _pack_weights(chip, names, gla_w_in, gla_w_out, w_kv, dsa_w_q, dsa_w_out, ffn_w_up, ffn_w_down):
    gin = gla_w_in[0]
    gin = jnp.pad(gin, ((0, 0), (0, _roundup(gin.shape[1], LANE) - gin.shape[1])))
    shards = {"gin": gin, "gout": gla_w_out[0], "up0": ffn_w_up[0], "up1": ffn_w_up[1], "down0": ffn_w_down[0],
              "down1": ffn_w_down[1], "wq": dsa_w_q[0], "wkv": w_kv, "dout": dsa_w_out[0]}
    out = {}
    for name in names:
        w = shards[name]
        buf = jnp.zeros((N_CHIPS,) + w.shape, BF16)
        out[name] = lax.dynamic_update_slice(buf, w.astype(BF16)[None], (chip, 0, 0))
    return out


def _unpack_gin(w_gin):
    w = _layout()["gin"][1]
    d = w_gin.shape[1]
    wp = w_gin.shape[2]
    n_main = 2 * GLA_KEY_DIM + 2 * GLA_VAL_DIM
    tm = _tile(d, (256, 128, 64, 32, 16))

    def body(s_ref, main_ref, a_ref):
        full = jnp.concatenate([s_ref[s][:, :w] for s in range(N_CHIPS)], axis=1)
        main_ref[...] = full[:, :n_main]
        a_ref[...] = jnp.concatenate([full[:, n_main:], jnp.zeros((tm, A_PAD - GATE_RANK), full.dtype)], axis=1)

    return pl.pallas_call(
        body, name="unpack_gin", grid=(d // tm,), in_specs=[pl.BlockSpec((N_CHIPS, tm, wp), lambda i: (0, i, 0))],
        out_specs=(pl.BlockSpec((tm, n_main), lambda i: (i, 0)), pl.BlockSpec((tm, A_PAD), lambda i: (i, 0))),
        out_shape=(jax.ShapeDtypeStruct((d, n_main), w_gin.dtype), jax.ShapeDtypeStruct((d, A_PAD), w_gin.dtype)),
        compiler_params=_params(1))(w_gin)


def _pack_gin_grad(gin_main, gin_a):
    w = _layout()["gin"][1]
    wp = _roundup(w, LANE)
    d, n_main = gin_main.shape
    tm = _tile(d, (256, 128, 64, 32, 16))

    def body(main_ref, a_ref, o_ref):
        full = jnp.concatenate([main_ref[...], a_ref[:, :GATE_RANK]], axis=1)
        fill = jnp.zeros((tm, wp - w), full.dtype)
        for s in range(N_CHIPS):
            o_ref[s] = jnp.concatenate([full[:, s * w:(s + 1) * w], fill], axis=1)

    return pl.pallas_call(
        body, name="pack_gin_grad", grid=(d // tm,),
        in_specs=[pl.BlockSpec((tm, n_main), lambda i: (i, 0)), pl.BlockSpec((tm, A_PAD), lambda i: (i, 0))],
        out_specs=pl.BlockSpec((N_CHIPS, tm, wp), lambda i: (0, i, 0)),
        out_shape=jax.ShapeDtypeStruct((N_CHIPS, d, wp), gin_main.dtype), compiler_params=_params(1))(gin_main, gin_a)


def _small_params(attn_norm, ffn_norm, kv_norm, final_norm, conv_b, w_a2, b_a2, head_norm, conv_w):
    norms = {"attn0": attn_norm[0:1], "attn1": attn_norm[1:2], "ffn0": ffn_norm[0:1], "ffn1": ffn_norm[1:2],
             "kv": kv_norm[None, :], "final": final_norm[None, :]}
    small = {"w_a2p": jnp.pad(w_a2, ((0, A_PAD - GATE_RANK), (0, 0))), "b_a2": b_a2[None, :],
             "head_norm": head_norm[None, :], "conv_w": jnp.pad(conv_w, ((0, 0), (0, 8 - conv_w.shape[1]), (0, 0))),
             "conv_b": conv_b[:, None, :]}
    return norms, small


ANY = pl.BlockSpec(memory_space=pl.ANY)


def _place():
    return lax.axis_index("x"), lax.axis_index("y"), lax.axis_index("c")


def _other_chips(x, y):
    return [(1 - x, y), (x, 1 - y), (1 - x, 1 - y)]


def _rcopy(src, dst, ssem, rsem, dev):
    return pltpu.make_async_remote_copy(src_ref=src, dst_ref=dst, send_sem=ssem, recv_sem=rsem, device_id=dev,
                                        device_id_type=MESH)


def _pack_shard(name, w, layer, chip_arr, after):
    rows, cols = w.shape[-2:]
    tr = _tile(rows, (512, 352, 256, 128, 64, 32, 16))

    def body(p_ref, w_ref, after_ref, o_ref):
        o_ref[...] = w_ref[...].astype(o_ref.dtype)

    if w.ndim == 3:
        w_spec = pl.BlockSpec((None, tr, cols), lambda i, p: (layer, i, 0))
    else:
        w_spec = pl.BlockSpec((tr, cols), lambda i, p: (i, 0))
    return pl.pallas_call(
        body, name=name,
        grid_spec=pltpu.PrefetchScalarGridSpec(
            num_scalar_prefetch=1, grid=(rows // tr,), in_specs=[w_spec, ANY],
            out_specs=pl.BlockSpec((None, tr, cols), lambda i, p: (p[0], i, 0))),
        out_shape=jax.ShapeDtypeStruct((N_CHIPS, rows, cols), BF16), compiler_params=_params(1))(chip_arr, w, after)


def _swap_halves(name, arrs):
    n = len(arrs)

    def body(*refs):
        ins, outs = refs[:n], refs[n:2 * n]
        send, recv = refs[2 * n:]
        x, y, c = _place()
        cps = []
        for a in range(n):
            h = ins[a].shape[1] // 2
            cp = _rcopy(ins[a].at[:, pl.ds((1 - c) * h, h)], outs[a], send.at[a], recv.at[a], (x, y, 1 - c))
            cp.start()
            cps.append(cp)
        for cp in cps:
            cp.wait()

    return pl.pallas_call(
        body, name=name, in_specs=[ANY] * n, out_specs=[ANY] * n,
        out_shape=[jax.ShapeDtypeStruct((a.shape[0], a.shape[1] // 2, a.shape[2]), a.dtype) for a in arrs],
        scratch_shapes=[pltpu.SemaphoreType.DMA((n,)), pltpu.SemaphoreType.DMA((n,))])(*arrs)


SEM = pl.BlockSpec(memory_space=pltpu.SEMAPHORE)
EFFECT = pltpu.SideEffectType.DATAFLOW_SIDE_EFFECTING


def _shapes(arrs):
    return [jax.ShapeDtypeStruct(a.shape, a.dtype) for a in arrs]


def _gather_start(name, thru, arrs):
    n, nt = len(arrs), len(thru)

    def body(*refs):
        ins = refs[nt:nt + n]
        send, recv = refs[nt + n], refs[nt + n + 1]
        outs = refs[2 * nt + n + 2:]
        x, y, c = _place()
        me = 2 * x + y
        for a in range(n):
            h = ins[a].shape[1] // 2
            mine = pl.ds(c * h, h)
            for j, (px, py) in enumerate(_other_chips(x, y)):
                _rcopy(ins[a].at[me, mine], outs[a].at[me, mine], send.at[3 * a + j], recv.at[3 * a + j], (px, py, c)).start()

    res = pl.pallas_call(
        body, name=name, in_specs=[ANY] * (nt + n), out_specs=[SEM, SEM] + [ANY] * (nt + n),
        out_shape=[pltpu.SemaphoreType.DMA((3 * n,)), pltpu.SemaphoreType.DMA((3 * n,))] + _shapes(thru) + _shapes(arrs),
        input_output_aliases={i: 2 + i for i in range(nt + n)},
        compiler_params=pltpu.CompilerParams(has_side_effects=EFFECT))(*thru, *arrs)
    return res[0], res[1], res[2:2 + nt], res[2 + nt:]


def _gather_wait(name, send, recv, arrs, after):
    n = len(arrs)

    def body(*refs):
        ins = refs[:n]
        send_ref, recv_ref = refs[n], refs[n + 1]
        x, y, c = _place()
        me = 2 * x + y
        for a in range(n):
            h = ins[a].shape[1] // 2
            mine = pl.ds(c * h, h)
            for j, (px, py) in enumerate(_other_chips(x, y)):
                sent = ins[a].at[me, mine]
                landed = ins[a].at[2 * px + py, mine]
                cp = _rcopy(sent, landed, send_ref.at[3 * a + j], recv_ref.at[3 * a + j], (px, py, c))
                cp.wait_send()
                cp.wait_recv()

    after = list(after) if isinstance(after, (list, tuple)) else [after]
    return pl.pallas_call(
        body, name=name, in_specs=[ANY] * n + [SEM, SEM] + [ANY] * len(after), out_specs=[ANY] * n,
        out_shape=_shapes(arrs), input_output_aliases={a: a for a in range(n)},
        compiler_params=pltpu.CompilerParams(has_side_effects=EFFECT))(*arrs, send, recv, *after)


def _forward_halves(name, arrs):
    n = len(arrs)

    def body(*refs):
        ins, outs = refs[:n], refs[n:2 * n]
        send, recv = refs[2 * n:]
        x, y, c = _place()
        sib = (x, y, 1 - c)
        chips = _other_chips(x, y)
        cps = []
        for a in range(n):
            h = ins[a].shape[1] // 2
            mine = pl.ds(c * h, h)
            for j, (px, py) in enumerate(chips):
                cp = _rcopy(ins[a].at[2 * px + py, mine], outs[a].at[2 * px + py, mine], send.at[3 * a + j],
                            recv.at[3 * a + j], sib)
                cp.start()
                cps.append(cp)
        for a in range(n):
            h = ins[a].shape[1] // 2
            theirs = pl.ds((1 - c) * h, h)
            for j, (px, py) in enumerate(chips):
                got = outs[a].at[2 * px + py, theirs]
                _rcopy(got, got, send.at[3 * a + j], recv.at[3 * a + j], sib).wait_recv()
        for cp in cps:
            cp.wait_send()

    return pl.pallas_call(
        body, name=name, in_specs=[ANY] * n, out_specs=[ANY] * n, out_shape=_shapes(arrs),
        input_output_aliases={a: a for a in range(n)},
        scratch_shapes=[pltpu.SemaphoreType.DMA((3 * n,)), pltpu.SemaphoreType.DMA((3 * n,))])(*arrs)


def _forward_start(name, thru, arrs):
    n, nt = len(arrs), len(thru)

    def body(*refs):
        ins = refs[nt:nt + n]
        send, recv = refs[nt + n], refs[nt + n + 1]
        outs = refs[2 * nt + n + 2:]
        x, y, c = _place()
        for a in range(n):
            h = ins[a].shape[1] // 2
            mine = pl.ds(c * h, h)
            for j, (px, py) in enumerate(_other_chips(x, y)):
                _rcopy(ins[a].at[2 * px + py, mine], outs[a].at[2 * px + py, mine], send.at[3 * a + j], recv.at[3 * a + j],
                       (x, y, 1 - c)).start()

    res = pl.pallas_call(
        body, name=name, in_specs=[ANY] * (nt + n), out_specs=[SEM, SEM] + [ANY] * (nt + n),
        out_shape=[pltpu.SemaphoreType.DMA((3 * n,)), pltpu.SemaphoreType.DMA((3 * n,))] + _shapes(thru) + _shapes(arrs),
        input_output_aliases={i: 2 + i for i in range(nt + n)},
        compiler_params=pltpu.CompilerParams(has_side_effects=EFFECT))(*thru, *arrs)
    return res[0], res[1], res[2:2 + nt], res[2 + nt:]


def _forward_wait(name, send, recv, arrs, after):
    n = len(arrs)

    def body(*refs):
        ins = refs[:n]
        send_ref, recv_ref = refs[n], refs[n + 1]
        x, y, c = _place()
        for a in range(n):
            h = ins[a].shape[1] // 2
            for j, (px, py) in enumerate(_other_chips(x, y)):
                sent = ins[a].at[2 * px + py, pl.ds(c * h, h)]
                got = ins[a].at[2 * px + py, pl.ds((1 - c) * h, h)]
                cp = _rcopy(sent, got, send_ref.at[3 * a + j], recv_ref.at[3 * a + j], (x, y, 1 - c))
                cp.wait_send()
                cp.wait_recv()

    return pl.pallas_call(
        body, name=name, in_specs=[ANY] * n + [SEM, SEM, ANY], out_specs=[ANY] * n, out_shape=_shapes(arrs),
        input_output_aliases={a: a for a in range(n)},
        compiler_params=pltpu.CompilerParams(has_side_effects=EFFECT))(*arrs, send, recv, after)


def _scatter_start(name, thru, arrs):
    n, nt = len(arrs), len(thru)
    landing = [lax.empty(a.shape, a.dtype) for a in arrs]

    def body(*refs):
        ins = refs[nt:nt + n]
        send, recv = refs[nt + 2 * n], refs[nt + 2 * n + 1]
        outs = refs[2 * nt + 3 * n + 2:]
        x, y, c = _place()
        me = 2 * x + y
        for a in range(n):
            for j, (px, py) in enumerate(_other_chips(x, y)):
                _rcopy(ins[a].at[2 * px + py], outs[a].at[me], send.at[3 * a + j], recv.at[3 * a + j], (px, py, c)).start()

    res = pl.pallas_call(
        body, name=name, in_specs=[ANY] * (nt + 2 * n), out_specs=[SEM, SEM] + [ANY] * (nt + 2 * n),
        out_shape=[pltpu.SemaphoreType.DMA((3 * n,)), pltpu.SemaphoreType.DMA((3 * n,))] + _shapes(thru) + _shapes(arrs)
        + _shapes(landing),
        input_output_aliases={i: 2 + i for i in range(nt + 2 * n)},
        compiler_params=pltpu.CompilerParams(has_side_effects=EFFECT))(*thru, *arrs, *landing)
    return res[0], res[1], res[2:2 + nt], res[2 + nt:2 + nt + n], res[2 + nt + n:]


def _scatter_wait(name, send, recv, arrs, landing, after):
    n = len(arrs)

    def body(*refs):
        ins, land = refs[:n], refs[n:2 * n]
        send_ref, recv_ref = refs[2 * n], refs[2 * n + 1]
        x, y, c = _place()
        for a in range(n):
            for j, (px, py) in enumerate(_other_chips(x, y)):
                cp = _rcopy(ins[a].at[2 * px + py], land[a].at[2 * px + py], send_ref.at[3 * a + j], recv_ref.at[3 * a + j],
                            (px, py, c))
                cp.wait_send()
                cp.wait_recv()

    res = pl.pallas_call(
        body, name=name, in_specs=[ANY] * (2 * n) + [SEM, SEM, ANY], out_specs=[ANY] * (2 * n),
        out_shape=_shapes(arrs) + _shapes(landing), input_output_aliases={i: i for i in range(2 * n)},
        compiler_params=pltpu.CompilerParams(has_side_effects=EFFECT))(*arrs, *landing, send, recv, after)
    return res[:n], res[n:]


def _swap_start(name, thru, arrs):
    n, nt = len(arrs), len(thru)
    landing = [lax.empty((a.shape[0], a.shape[1] // 2, a.shape[2]), a.dtype) for a in arrs]

    def body(*refs):
        ins = refs[nt:nt + n]
        send, recv = refs[nt + 2 * n], refs[nt + 2 * n + 1]
        outs = refs[2 * nt + 3 * n + 2:]
        x, y, c = _place()
        for a in range(n):
            h = ins[a].shape[1] // 2
            _rcopy(ins[a].at[:, pl.ds((1 - c) * h, h)], outs[a], send.at[a], recv.at[a], (x, y, 1 - c)).start()

    res = pl.pallas_call(
        body, name=name, in_specs=[ANY] * (nt + 2 * n), out_specs=[SEM, SEM] + [ANY] * (nt + 2 * n),
        out_shape=[pltpu.SemaphoreType.DMA((n,)), pltpu.SemaphoreType.DMA((n,))] + _shapes(thru) + _shapes(arrs)
        + _shapes(landing),
        input_output_aliases={i: 2 + i for i in range(nt + 2 * n)},
        compiler_params=pltpu.CompilerParams(has_side_effects=EFFECT))(*thru, *arrs, *landing)
    return res[0], res[1], res[2:2 + nt], res[2 + nt:2 + nt + n], res[2 + nt + n:]


def _swap_wait(name, send, recv, arrs, landing, after):
    n = len(arrs)

    def body(*refs):
        ins, land = refs[:n], refs[n:2 * n]
        send_ref, recv_ref = refs[2 * n], refs[2 * n + 1]
        x, y, c = _place()
        for a in range(n):
            h = ins[a].shape[1] // 2
            cp = _rcopy(ins[a].at[:, pl.ds((1 - c) * h, h)], land[a], send_ref.at[a], recv_ref.at[a], (x, y, 1 - c))
            cp.wait_send()
            cp.wait_recv()

    res = pl.pallas_call(
        body, name=name, in_specs=[ANY] * (2 * n) + [SEM, SEM, ANY], out_specs=[ANY] * (2 * n),
        out_shape=_shapes(arrs) + _shapes(landing), input_output_aliases={i: i for i in range(2 * n)},
        compiler_params=pltpu.CompilerParams(has_side_effects=EFFECT))(*arrs, *landing, send, recv, after)
    return res[:n], res[n:]


def _join_start(name, arrs):
    n = len(arrs)

    def body(*refs):
        ins = refs[:n]
        send, recv = refs[n], refs[n + 1]
        outs = refs[n + 2:]
        x, y, c = _place()
        for a in range(n):
            h = ins[a].shape[0] // 2
            mine = pl.ds(c * h, h)
            _rcopy(ins[a].at[mine], outs[a].at[mine], send.at[a], recv.at[a], (x, y, 1 - c)).start()

    res = pl.pallas_call(
        body, name=name, in_specs=[ANY] * n, out_specs=[SEM, SEM] + [ANY] * n,
        out_shape=[pltpu.SemaphoreType.DMA((n,)), pltpu.SemaphoreType.DMA((n,))] + _shapes(arrs),
        input_output_aliases={i: 2 + i for i in range(n)},
        compiler_params=pltpu.CompilerParams(has_side_effects=EFFECT))(*arrs)
    return res[0], res[1], res[2:]


def _join_wait(name, send, recv, arrs, after):
    n = len(arrs)

    def body(*refs):
        ins = refs[:n]
        send_ref, recv_ref = refs[n], refs[n + 1]
        x, y, c = _place()
        for a in range(n):
            h = ins[a].shape[0] // 2
            cp = _rcopy(ins[a].at[pl.ds(c * h, h)], ins[a].at[pl.ds((1 - c) * h, h)], send_ref.at[a], recv_ref.at[a],
                        (x, y, 1 - c))
            cp.wait_send()
            cp.wait_recv()

    return pl.pallas_call(
        body, name=name, in_specs=[ANY] * n + [SEM, SEM, ANY], out_specs=[ANY] * n, out_shape=_shapes(arrs),
        input_output_aliases={a: a for a in range(n)},
        compiler_params=pltpu.CompilerParams(has_side_effects=EFFECT))(*arrs, send, recv, after)


def _join_halves(name, arrs):
    n = len(arrs)

    def body(*refs):
        ins, outs = refs[:n], refs[n:2 * n]
        send, recv = refs[2 * n:]
        x, y, c = _place()
        cps = []
        for a in range(n):
            h = ins[a].shape[0] // 2
            mine = pl.ds(c * h, h)
            cp = _rcopy(ins[a].at[mine], outs[a].at[mine], send.at[a], recv.at[a], (x, y, 1 - c))
            cp.start()
            cps.append(cp)
        for a in range(n):
            h = ins[a].shape[0] // 2
            got = outs[a].at[pl.ds((1 - c) * h, h)]
            _rcopy(got, got, send.at[a], recv.at[a], (x, y, 1 - c)).wait_recv()
        for cp in cps:
            cp.wait_send()

    return pl.pallas_call(
        body, name=name, in_specs=[ANY] * n, out_specs=[ANY] * n,
        out_shape=[jax.ShapeDtypeStruct(a.shape, a.dtype) for a in arrs],
        input_output_aliases={a: a for a in range(n)},
        scratch_shapes=[pltpu.SemaphoreType.DMA((n,)), pltpu.SemaphoreType.DMA((n,))])(*arrs)


def _allgather8(name, xs, reduce):
    m_per, n = xs.shape

    def body(x_ref, out_ref, *rest):
        if reduce:
            sum_ref, send, recv, lsem = rest
        else:
            send, recv, lsem = rest
        x, y, c = _place()
        me, sib = (x, y, c), (x, y, 1 - c)
        chips = _other_chips(x, y)

        def rows(px, py, pc):
            return out_ref.at[pl.ds((4 * px + 2 * py + pc) * m_per, m_per), :]

        def copy(k, block, to, src=None):
            return _rcopy(rows(*block) if src is None else src, rows(*block), send.at[k], recv.at[k], to)

        mine = pltpu.make_async_copy(x_ref, rows(*me), lsem)
        mine.start()
        first = [copy(0, me, sib, src=x_ref)]
        first += [copy(1 + j, me, (*chip, c), src=x_ref) for j, chip in enumerate(chips)]
        for cp in first:
            cp.start()
        passed = [copy(4 + j, (*chip, c), sib) for j, chip in enumerate(chips)]
        for j, chip in enumerate(chips):
            copy(1 + j, (*chip, c), me).wait_recv()
            passed[j].start()
        copy(0, sib, me).wait_recv()
        for j, chip in enumerate(chips):
            copy(4 + j, (*chip, 1 - c), me).wait_recv()
        for cp in first + passed:
            cp.wait_send()
        mine.wait()
        if reduce:
            acc = out_ref[pl.ds(0, m_per), :]
            for dev in range(1, N_DEV):
                acc = acc + out_ref[pl.ds(dev * m_per, m_per), :]
            sum_ref[...] = acc

    vm = pl.BlockSpec(memory_space=pltpu.VMEM)
    out_shape = [jax.ShapeDtypeStruct((N_DEV * m_per, n), xs.dtype)]
    if reduce:
        out_shape.append(jax.ShapeDtypeStruct((m_per, n), xs.dtype))
    return pl.pallas_call(
        body, name=name, in_specs=[vm], out_specs=[vm] * len(out_shape), out_shape=out_shape,
        scratch_shapes=[pltpu.SemaphoreType.DMA((7,)), pltpu.SemaphoreType.DMA((7,)), pltpu.SemaphoreType.DMA],
        compiler_params=pltpu.CompilerParams(vmem_limit_bytes=VMEM_LIMIT))(xs)


def _add_my_half(name, a, rb, c_arr):
    s, h, cols = rb.shape
    tr = _tile(h, (512, 352, 256, 128, 64, 32, 16))
    nt = h // tr

    def body(c_ref, a_ref, b_ref, o_ref):
        o_ref[...] = (a_ref[...].astype(F32) + b_ref[...].astype(F32)).astype(o_ref.dtype)

    return pl.pallas_call(
        body, name=name,
        grid_spec=pltpu.PrefetchScalarGridSpec(
            num_scalar_prefetch=1, grid=(s, nt),
            in_specs=[pl.BlockSpec((None, tr, cols), lambda k, i, c: (k, c[0] * nt + i, 0)),
                      pl.BlockSpec((None, tr, cols), lambda k, i, c: (k, i, 0))],
            out_specs=pl.BlockSpec((None, tr, cols), lambda k, i, c: (k, i, 0))),
        out_shape=jax.ShapeDtypeStruct(rb.shape, BF16), compiler_params=_params(2))(c_arr, a, rb)


def _sum_chips(name, own, q, place):
    s, h, cols = q.shape
    tr = _tile(h, (512, 352, 256, 128, 64, 32, 16))
    nt = h // tr

    def body(p_ref, own_ref, q1_ref, q2_ref, q3_ref, o_ref):
        acc = own_ref[...].astype(F32) + q1_ref[...].astype(F32)
        o_ref[...] = acc + q2_ref[...].astype(F32) + q3_ref[...].astype(F32)

    def slab(t):
        return pl.BlockSpec((None, tr, cols), lambda i, p: ((p[0] + t) % s, i, 0))

    return pl.pallas_call(
        body, name=name,
        grid_spec=pltpu.PrefetchScalarGridSpec(
            num_scalar_prefetch=1, grid=(nt,), in_specs=[slab(0), slab(1), slab(2), slab(3)],
            out_specs=pl.BlockSpec((tr, cols), lambda i, p: (p[1] * nt + i, 0))),
        out_shape=jax.ShapeDtypeStruct((2 * h, cols), F32), compiler_params=_params(1))(place, own, q, q, q)


def _pack_rows(parts):
    rows = []
    for p in parts:
        flat = p.reshape(-1).astype(F32)
        n = _roundup(flat.shape[0], 8 * LANE)
        rows.append(jnp.pad(flat, (0, n - flat.shape[0])).reshape(-1, LANE))
    return jnp.concatenate(rows, axis=0)


def _unpack_rows(buf, shapes):
    out, r = [], 0
    for shp in shapes:
        size = math.prod(shp)
        nr = _roundup(size, 8 * LANE) // LANE
        out.append(buf[r:r + nr].reshape(-1)[:size].reshape(shp))
        r += nr
    return out


def kernel(x, attn_norm, gla_w_in, gla_w_a2, gla_b_a2, gla_head_norm, gla_w_out, kv_norm, w_kv, dsa_w_q, dsa_w_out, ffn_norm, ffn_w_up, ffn_conv_w, ffn_conv_b, ffn_w_down, final_norm, loss_target, m_attn_norm, m_gla_w_in, m_gla_w_a2, m_gla_b_a2, m_gla_head_norm, m_gla_w_out, m_kv_norm, m_w_kv, m_dsa_w_q, m_dsa_w_out, m_ffn_norm, m_ffn_w_up, m_ffn_conv_w, m_ffn_conv_b, m_ffn_w_down, m_final_norm, v_attn_norm, v_gla_w_in, v_gla_w_a2, v_gla_b_a2, v_gla_head_norm, v_gla_w_out, v_kv_norm, v_w_kv, v_dsa_w_q, v_dsa_w_out, v_ffn_norm, v_ffn_w_up, v_ffn_conv_w, v_ffn_conv_b, v_ffn_w_down, v_final_norm):
    lay = _layout()
    d, f = D_MODEL, D_FF
    cx, cy, cc = _place()
    chip = 2 * cx + cy
    c_arr = jnp.reshape(cc, (1,)).astype(jnp.int32)
    place = jnp.stack([chip, cc]).astype(jnp.int32)

    groups = {"A": ("gin", "small"), "B0": ("gout",), "B1": ("up0",), "B2": ("down0",), "C1": ("wkv", "wq", "dout"),
              "C2": ("up1", "down1")}
    big_shards = (gla_w_in, gla_w_out, w_kv, dsa_w_q, dsa_w_out, ffn_w_up, ffn_w_down)
    ws = _pack_weights(chip, groups["A"][:1], *big_shards)
    sharded_small = [gla_w_a2[0], gla_b_a2[0], gla_head_norm[0], ffn_conv_w]
    packed = _pack_rows(sharded_small)
    packed = jnp.pad(packed, ((0, _roundup(packed.shape[0], 16) - packed.shape[0]), (0, 0)))
    ws["small"] = lax.dynamic_update_slice(jnp.zeros((N_CHIPS,) + packed.shape, F32), packed[None], (chip, 0, 0))
    send, recv, _, arrs = _gather_start("gather_a_start", [], [ws[k] for k in groups["A"]])
    chip_arr = place[:1]
    sources = {"up0": (ffn_w_up, 0), "up1": (ffn_w_up, 1), "down0": (ffn_w_down, 0), "down1": (ffn_w_down, 1),
               "wq": (dsa_w_q, 0), "wkv": (w_kv, 0), "dout": (dsa_w_out, 0), "gout": (gla_w_out, 0)}
    later = ("B0", "B1", "B2", "C1", "C2")
    for k in sum((groups[grp] for grp in later), ()):
        ws[k] = _pack_shard(f"pack_{k}", *sources[k], chip_arr, arrs[1])
    moments = [t.reshape(-1, t.shape[-1]) for t in (m_gla_w_in, v_gla_w_in)]
    arrs = _gather_wait("gather_a_wait", send, recv, arrs, [ws["dout"]] + moments)
    ws.update(zip(groups["A"], _forward_halves("forward_a", arrs)))
    in_flight = {}
    thru = [ws[k] for k in groups["A"]]
    for grp in later:
        send, recv, thru, arrs = _gather_start(f"gather_{grp.lower()}_start", thru, [ws[k] for k in groups[grp]])
        ws.update(zip(groups[grp], arrs))
        in_flight[grp] = (send, recv)
    ws.update(zip(groups["A"], thru))
    pending = []

    class _Comm:
        def prefetch(self, grp, ws, carry):
            send, recv = in_flight[grp]
            arrs = _gather_wait(f"gather_{grp.lower()}_wait", send, recv, [ws[k] for k in groups[grp]], carry)
            send, recv, thru, arrs = _forward_start(f"forward_{grp.lower()}_start", [carry], arrs)
            in_flight[grp] = (send, recv)
            return {**ws, **dict(zip(groups[grp], arrs))}, thru[0]

        def need(self, grp, ws, after):
            send, recv = in_flight[grp]
            arrs = _forward_wait(f"forward_{grp.lower()}_wait", send, recv, [ws[k] for k in groups[grp]], after)
            return {**ws, **dict(zip(groups[grp], arrs))}

        swapping = None

        def reduce(self, grp, grads, carry):
            names = list(grads)
            send, recv, thru, parts, theirs = _swap_start(f"swap_{grp}_start", [carry], [grads[k] for k in names])
            self.swapping = (grp, names, send, recv, parts, theirs)
            return thru[0]

        def tick(self, carry):
            if self.swapping is None:
                return carry
            grp, names, send, recv, parts, theirs = self.swapping
            self.swapping = None
            parts, theirs = _swap_wait(f"swap_{grp}_wait", send, recv, parts, theirs, carry)
            return self.scatter(grp, names, parts, theirs, carry)

        def scatter(self, grp, names, parts, theirs, carry):
            sums = [_add_my_half(f"add_half_{k}", a, b, c_arr) for k, a, b in zip(names, parts, theirs)]
            send, recv, thru, sums, landing = _scatter_start(f"scatter_{grp}_start", [carry], sums)
            pending.append((grp, names, send, recv, sums, landing))
            return thru[0]

        def reduce_now(self, grp, grads, carry):
            names = list(grads)
            parts = [grads[k] for k in names]
            return self.scatter(grp, names, parts, _swap_halves(f"swap_{grp}", parts), carry)

    shards = [_unpack_rows(ws["small"][s], [p.shape for p in sharded_small]) for s in range(N_CHIPS)]
    w_a2, b_a2, head_norm, conv_w = [jnp.concatenate([shards[s][k] for s in range(N_CHIPS)], axis=-1) for k in range(4)]
    norms, small = _small_params(attn_norm, ffn_norm, kv_norm, final_norm, ffn_conv_b, w_a2, b_a2, head_norm, conv_w)

    comm = _Comm()
    loss_blk, grad_x, sm, last_big = _local_step(x[0], loss_target[0], ws, norms, small, comm)

    small_parts = [loss_blk, jnp.concatenate([sm["attn0"], sm["attn1"]]), jnp.concatenate([sm["ffn0"], sm["ffn1"]]),
                   sm["kv"], sm["final"], jnp.concatenate([sm["conv_b0"], sm["conv_b1"]]),
                   sm["w_a2p"][:GATE_RANK], sm["b_a2"], sm["head_norm"],
                   jnp.stack([jnp.concatenate(sm["conv_w0"]), jnp.concatenate(sm["conv_w1"])])]
    small_shapes = [(8, LANE), (2, d), (2, d), (d,), (d,), (2, f), (GATE_RANK, GLA_KEY_DIM), (GLA_KEY_DIM,),
                    (GLA_VAL_DIM // GLA_HEADS,), (2, 3, f)]
    _, reduced = _allgather8("reduce_small", _pack_rows(small_parts), True)
    reduced = comm.reduce_now("gla", last_big, reduced)

    loss_r, g_attn, g_ffn, g_kv, g_final, g_cb, g_a2, g_ba2, g_hn, g_cw = _unpack_rows(reduced, small_shapes)
    loss = loss_r[0, 0]

    def mine(g, axis):
        w = g.shape[axis] // N_CHIPS
        return lax.dynamic_slice_in_dim(g, chip * w, w, axis)

    grads = {
        "attn_norm": g_attn, "gla_w_a2": mine(g_a2, 1)[None], "gla_b_a2": mine(g_ba2, 0)[None],
        "gla_head_norm": mine(g_hn, 0)[None], "kv_norm": g_kv, "ffn_norm": g_ffn, "ffn_conv_w": mine(g_cw, 2),
        "ffn_conv_b": g_cb, "final_norm": g_final,
    }
    weights = {"attn_norm": (attn_norm, m_attn_norm, v_attn_norm), "gla_w_in": (gla_w_in, m_gla_w_in, v_gla_w_in),
               "gla_w_a2": (gla_w_a2, m_gla_w_a2, v_gla_w_a2), "gla_b_a2": (gla_b_a2, m_gla_b_a2, v_gla_b_a2),
               "gla_head_norm": (gla_head_norm, m_gla_head_norm, v_gla_head_norm),
               "gla_w_out": (gla_w_out, m_gla_w_out, v_gla_w_out), "kv_norm": (kv_norm, m_kv_norm, v_kv_norm),
               "w_kv": (w_kv, m_w_kv, v_w_kv), "dsa_w_q": (dsa_w_q, m_dsa_w_q, v_dsa_w_q),
               "dsa_w_out": (dsa_w_out, m_dsa_w_out, v_dsa_w_out), "ffn_norm": (ffn_norm, m_ffn_norm, v_ffn_norm),
               "ffn_w_up": (ffn_w_up, m_ffn_w_up, v_ffn_w_up), "ffn_conv_w": (ffn_conv_w, m_ffn_conv_w, v_ffn_conv_w),
               "ffn_conv_b": (ffn_conv_b, m_ffn_conv_b, v_ffn_conv_b),
               "ffn_w_down": (ffn_w_down, m_ffn_w_down, v_ffn_w_down), "final_norm": (final_norm, m_final_norm, v_final_norm)}
    order = list(weights)
    big_names = ("gla_w_in", "gla_w_out", "w_kv", "dsa_w_q", "dsa_w_out", "ffn_w_up", "ffn_w_down")
    delta, new_m, new_v = {}, {}, {}

    def adam_big(k, g):
        w, m, v = weights[k]
        cols = w.shape[-1]
        res = _adamw(f"adamw_{k}", w.reshape(-1, cols), g.reshape(-1, g.shape[-1]), m.reshape(-1, cols), v.reshape(-1, cols))
        delta[k], new_m[k], new_v[k] = [r.reshape(w.shape) for r in res[:3]]
        grads[k] = res[3].reshape(w.shape) if len(res) == 4 else g
        return res[0]

    full = {}
    after = reduced
    joining = []
    for grp, names, send, recv, sums, landing in pending[:-1]:
        sums, landing = _scatter_wait(f"scatter_{grp}_wait", send, recv, sums, landing, after)
        halves = [_sum_chips(f"sum_chips_{k}", s, q, place) for k, s, q in zip(names, sums, landing)]
        send, recv, halves = _join_start(f"join_{grp}_start", halves)
        joining.append((grp, names, send, recv, halves))
        after = halves[0]
    for grp, names, send, recv, halves in joining:
        joined = _join_wait(f"join_{grp}_wait", send, recv, halves, after)
        full.update(zip(names, joined))
        after = joined[0]
    after = adam_big("w_kv", full["wkv"])
    after = adam_big("dsa_w_q", full["wq"][None])
    after = adam_big("dsa_w_out", full["dout"][None])
    for k, g0, g1 in (("ffn_w_up", "up0", "up1"), ("ffn_w_down", "down0", "down1")):
        delta[k], new_m[k], new_v[k], grads[k] = _adamw_layers(f"adamw_{k}", weights[k][0], (full[g0], full[g1]),
                                                               weights[k][1], weights[k][2])
        after = delta[k]
    grp, names, send, recv, sums, landing = pending[-1]
    sums, landing = _scatter_wait(f"scatter_{grp}_wait", send, recv, sums, landing, after)
    halves = [_sum_chips(f"sum_chips_{k}", s, q, place) for k, s, q in zip(names, sums, landing)]
    full.update(zip(names, _join_halves(f"join_{grp}", halves)))
    adam_big("gla_w_in", full["gin"])
    adam_big("gla_w_out", full["gout"][None])
    small_names = [k for k in order if k not in big_names]
    packed = [_pack_rows([src[k] for k in small_names])
              for src in ({k: weights[k][0] for k in small_names}, grads, {k: weights[k][1] for k in small_names},
                          {k: weights[k][2] for k in small_names})]
    res = _adamw("adamw_small", *packed)
    shapes = [weights[k][0].shape for k in small_names]
    for dst, buf in zip((delta, new_m, new_v), res):
        for k, val in zip(small_names, _unpack_rows(buf, shapes)):
            dst[k] = val
    return (loss, grad_x[None], *[grads[k] for k in order], *[delta[k] for k in order], *[new_m[k] for k in order],
            *[new_v[k] for k in order])
```

```python
import math

import jax
import jax.numpy as jnp
from jax import lax
from jax.experimental import pallas as pl
from jax.experimental.pallas import tpu as pltpu

F32 = jnp.float32
BF16 = jnp.bfloat16

D_MODEL = 2048
SEQ = 4096
GLA_HEADS = 4
GLA_KEY_DIM = D_MODEL // 2
GLA_VAL_DIM = D_MODEL
GATE_RANK = 16
GATE_NORMALIZER = 16.0
GLA_CHUNK = 64
ATT_HEADS = 16
HEAD_DIM = 128
WINDOWS = (128, 512, 2048)
DILATIONS = (1, 4, 16)
ATT_BLOCK = 128
D_FF = 5632
EPS = 1e-6
ADAM_LR = 0.001
ADAM_B1 = 0.9
ADAM_B2 = 0.999
ADAM_EPS = 1e-08
ADAM_WD = 0.01
ADAM_STEP = 10

N_CHIPS = 4
N_DEV = 8
LANE = 128
A_PAD = 128
VMEM_LIMIT = 56 * 1024 * 1024
MAX_K_TILE = 2816
NEG = -1e30
MESH = pl.DeviceIdType.MESH

NN = (((1,), (0,)), ((), ()))
NT = (((1,), (1,)), ((), ()))
TN = (((0,), (0,)), ((), ()))


def _tile(n, cands):
    for c in cands:
        if c <= n and n % c == 0:
            return c
    return n


def _roundup(n, m):
    return -(-n // m) * m


def _params(n_axes):
    return pltpu.CompilerParams(dimension_semantics=("arbitrary",) * n_axes, vmem_limit_bytes=VMEM_LIMIT)


def _dot(a, b, dims):
    return lax.dot_general(a, b, dims, preferred_element_type=F32)


def _sigmoid(x):
    return 1.0 / (1.0 + jnp.exp(-x))


COL_SHARDED = ("gin", "up0", "up1", "wq", "wkv")
ROW_SHARDED = ("gout", "down0", "down1", "dout")


def _layout():
    f = D_FF
    hd = ATT_HEADS * HEAD_DIM
    gin = 2 * GLA_KEY_DIM + 2 * GLA_VAL_DIM + GATE_RANK
    up_w = 2 * f // N_CHIPS
    q_w = 3 * hd // N_CHIPS
    kv_w = 2 * hd // N_CHIPS
    dn_r = f // N_CHIPS
    go_r = GLA_VAL_DIM // N_CHIPS
    do_r = hd // N_CHIPS
    big = (1408, 1024, 512, 256, 128)
    return {
        "gin": (0, gin // N_CHIPS, LANE),
        "up0": (0, up_w, _tile(up_w, big)), "up1": (0, up_w, _tile(up_w, big)),
        "wq": (0, q_w, _tile(q_w, (1536, 768, 512, 384, 256, 128))), "wkv": (0, kv_w, _tile(kv_w, (1024, 512, 256, 128))),
        "down0": (0, dn_r, _tile(dn_r, big)), "down1": (0, dn_r, _tile(dn_r, big)),
        "gout": (0, go_r, _tile(go_r, (512, 256, 128))), "dout": (0, do_r, _tile(do_r, (512, 256, 128))),
    }


def _matmul(name, a, b, dims, grid, a_spec, b_spec, o_spec, out_shape, acc_shape, add=None, add_spec=None):
    nk = grid[2]
    has_add = add is not None

    def body(*refs):
        a_ref, b_ref = refs[0], refs[1]
        pos = 2
        add_ref = None
        if has_add:
            add_ref = refs[pos]
            pos += 1
        o_ref = refs[pos]
        prod = _dot(a_ref[...].astype(BF16), b_ref[...].astype(BF16), dims)

        def finish(val):
            if has_add:
                val = val + add_ref[...].astype(F32)
            o_ref[...] = val.astype(o_ref.dtype)

        if nk == 1:
            finish(prod)
        else:
            acc_ref = refs[pos + 1]
            k = pl.program_id(2)

            @pl.when(k == 0)
            def _():
                acc_ref[...] = prod

            @pl.when(k > 0)
            def _():
                acc_ref[...] += prod

            @pl.when(k == nk - 1)
            def _():
                finish(acc_ref[...])

    in_specs = [a_spec, b_spec]
    args = [a, b]
    if has_add:
        in_specs.append(add_spec)
        args.append(add)
    scratch = [] if nk == 1 else [pltpu.VMEM(acc_shape, F32)]
    return pl.pallas_call(body, name=name, grid=grid, in_specs=in_specs, out_specs=o_spec, out_shape=out_shape,
                          scratch_shapes=scratch, compiler_params=_params(3))(*args)


def _mm_act_wc(name, a, wc, seg, out_dtype):
    off, w, tn = seg
    t_len, d = a.shape
    tm = _tile(t_len, (1024, 512, 256, 128))
    nps = w // tn
    ob = off // tn
    grid = (t_len // tm, N_CHIPS * nps, 1)
    return _matmul(
        name, a, wc, NN, grid,
        pl.BlockSpec((tm, d), lambda i, j, k: (i, 0)),
        pl.BlockSpec((None, d, tn), lambda i, j, k: (j // nps, 0, ob + j % nps)),
        pl.BlockSpec((tm, tn), lambda i, j, k: (i, j)),
        jax.ShapeDtypeStruct((t_len, N_CHIPS * w), out_dtype), (tm, tn))


def _mm_dact_wcT(name, dy, wc, seg, add=None):
    off, w, tk = seg
    if off == 0 and w <= MAX_K_TILE:
        tk = w
    t_len = dy.shape[0]
    d = wc.shape[1]
    tm = _tile(t_len, (1024, 512, 256, 128))
    tn = _tile(d, (1024, 512, 256, 128))
    kps = w // tk
    ob = off // tk
    grid = (t_len // tm, d // tn, N_CHIPS * kps)
    return _matmul(
        name, dy, wc, NT, grid,
        pl.BlockSpec((tm, tk), lambda i, j, k: (i, k)),
        pl.BlockSpec((None, tn, tk), lambda i, j, k: (k // kps, j, ob + k % kps)),
        pl.BlockSpec((tm, tn), lambda i, j, k: (i, j)),
        jax.ShapeDtypeStruct((t_len, d), F32), (tm, tn),
        add=add, add_spec=None if add is None else pl.BlockSpec((tm, tn), lambda i, j, k: (i, j)))


def _mm_grad_wc(name, a, dy, seg):
    _, w, tn = seg
    t_len, d = a.shape
    tm = _tile(d, (1024, 512, 256, 128))
    tk = _tile(t_len, (2048, 1024, 512, 256, 128))
    nps = w // tn
    grid = (d // tm, N_CHIPS * nps, t_len // tk)
    return _matmul(
        name, a, dy, TN, grid,
        pl.BlockSpec((tk, tm), lambda i, j, k: (k, i)),
        pl.BlockSpec((tk, tn), lambda i, j, k: (k, j)),
        pl.BlockSpec((None, tm, tn), lambda i, j, k: (j // nps, i, j % nps)),
        jax.ShapeDtypeStruct((N_CHIPS, d, w), BF16), (tm, tn))


def _is_plain(wr, seg):
    return seg[0] == 0 and wr.shape[1] == seg[1] and (N_CHIPS * seg[1]) % 1024 == 0


def _mm_act_wr(name, a, wr, seg, add):
    off, r, tk = seg
    t_len = a.shape[0]
    d = wr.shape[2]
    if seg[0] == 0 and wr.shape[1] == r:
        return _mm_plain(name, a, wr.reshape(N_CHIPS * r, d), NN, F32, add=add)
    tm = _tile(t_len, (1024, 512, 256, 128))
    tn = _tile(d, (1024, 512, 256, 128))
    kps = r // tk
    ob = off // tk
    grid = (t_len // tm, d // tn, N_CHIPS * kps)
    return _matmul(
        name, a, wr, NN, grid,
        pl.BlockSpec((tm, tk), lambda i, j, k: (i, k)),
        pl.BlockSpec((None, tk, tn), lambda i, j, k: (k // kps, ob + k % kps, j)),
        pl.BlockSpec((tm, tn), lambda i, j, k: (i, j)),
        jax.ShapeDtypeStruct((t_len, d), F32), (tm, tn),
        add=add, add_spec=pl.BlockSpec((tm, tn), lambda i, j, k: (i, j)))


def _mm_dact_wrT(name, dh, wr, seg):
    off, r, tn = seg
    t_len, d = dh.shape
    if _is_plain(wr, seg):
        return _mm_plain(name, dh, wr.reshape(N_CHIPS * r, d), NT, BF16)
    tm = _tile(t_len, (1024, 512, 256, 128))
    nps = r // tn
    ob = off // tn
    grid = (t_len // tm, N_CHIPS * nps, 1)
    return _matmul(
        name, dh, wr, NT, grid,
        pl.BlockSpec((tm, d), lambda i, j, k: (i, 0)),
        pl.BlockSpec((None, tn, d), lambda i, j, k: (j // nps, ob + j % nps, 0)),
        pl.BlockSpec((tm, tn), lambda i, j, k: (i, j)),
        jax.ShapeDtypeStruct((t_len, N_CHIPS * r), BF16), (tm, tn))


def _mm_grad_wr(name, a, dh, seg):
    _, r, tm = seg
    t_len, d = dh.shape
    if (N_CHIPS * r) % 1024 == 0:
        return _mm_plain(name, a, dh, TN, BF16).reshape(N_CHIPS, r, d)
    tn = _tile(d, (1024, 512, 256, 128))
    tk = _tile(t_len, (2048, 1024, 512, 256, 128))
    mps = r // tm
    grid = (N_CHIPS * mps, d // tn, t_len // tk)
    return _matmul(
        name, a, dh, TN, grid,
        pl.BlockSpec((tk, tm), lambda i, j, k: (k, i)),
        pl.BlockSpec((tk, tn), lambda i, j, k: (k, j)),
        pl.BlockSpec((None, tm, tn), lambda i, j, k: (i // mps, i % mps, j)),
        jax.ShapeDtypeStruct((N_CHIPS, r, d), BF16), (tm, tn))


def _mm_plain(name, a, b, dims, out_dtype, add=None):
    if dims == NN:
        m, kd = a.shape
        n = b.shape[1]
    elif dims == NT:
        m, kd = a.shape
        n = b.shape[0]
    else:
        kd, m = a.shape
        n = b.shape[1]
    tm = _tile(m, (1024, 512, 256, 128))
    tn = _tile(n, (1024, 768, 512, 256, 128))
    tk = _tile(kd, (MAX_K_TILE, 2048, 1408, 1024, 512, 256, 128))
    grid = (m // tm, n // tn, kd // tk)
    if dims == NN:
        a_spec = pl.BlockSpec((tm, tk), lambda i, j, k: (i, k))
        b_spec = pl.BlockSpec((tk, tn), lambda i, j, k: (k, j))
    elif dims == NT:
        a_spec = pl.BlockSpec((tm, tk), lambda i, j, k: (i, k))
        b_spec = pl.BlockSpec((tn, tk), lambda i, j, k: (j, k))
    else:
        a_spec = pl.BlockSpec((tk, tm), lambda i, j, k: (k, i))
        b_spec = pl.BlockSpec((tk, tn), lambda i, j, k: (k, j))
    o_spec = pl.BlockSpec((tm, tn), lambda i, j, k: (i, j))
    return _matmul(name, a, b, dims, grid, a_spec, b_spec, o_spec, jax.ShapeDtypeStruct((m, n), out_dtype), (tm, tn),
                   add=add, add_spec=None if add is None else o_spec)


def _rms_fwd(name, x, g):
    t_len, d = x.shape
    tm = _tile(t_len, (512, 256, 128))

    def body(x_ref, g_ref, o_ref):
        xv = x_ref[...]
        r = lax.rsqrt(jnp.mean(xv * xv, axis=-1, keepdims=True) + EPS)
        o_ref[...] = (xv * r * g_ref[...]).astype(o_ref.dtype)

    return pl.pallas_call(
        body, name=name, grid=(t_len // tm,),
        in_specs=[pl.BlockSpec((tm, d), lambda i: (i, 0)), pl.BlockSpec((1, d), lambda i: (0, 0))],
        out_specs=pl.BlockSpec((tm, d), lambda i: (i, 0)),
        out_shape=jax.ShapeDtypeStruct((t_len, d), BF16), compiler_params=_params(1))(x, g)


def _rms_bwd(name, dys, x, gs, dres):
    t_len, d = x.shape
    n = len(dys)
    tm = _tile(t_len, (256, 128))

    def body(*refs):
        dy_refs, x_ref, g_refs = refs[:n], refs[n], refs[n + 1:2 * n + 1]
        dres_ref, dx_ref, dg_refs = refs[2 * n + 1], refs[2 * n + 2], refs[2 * n + 3:]
        xv = x_ref[...]
        r = lax.rsqrt(jnp.mean(xv * xv, axis=-1, keepdims=True) + EPS)
        xhat = xv * r
        dyv = [ref[...].astype(F32) for ref in dy_refs]
        dxn = dyv[0] * g_refs[0][...]
        for k in range(1, n):
            dxn = dxn + dyv[k] * g_refs[k][...]
        dx = r * (dxn - xhat * jnp.mean(dxn * xhat, axis=-1, keepdims=True))
        dx_ref[...] = dres_ref[...] + dx
        parts = [jnp.sum(v * xhat, axis=0, keepdims=True) for v in dyv]

        @pl.when(pl.program_id(0) == 0)
        def _():
            for ref, p in zip(dg_refs, parts):
                ref[...] = p

        @pl.when(pl.program_id(0) > 0)
        def _():
            for ref, p in zip(dg_refs, parts):
                ref[...] += p

    row = pl.BlockSpec((tm, d), lambda i: (i, 0))
    vec = pl.BlockSpec((1, d), lambda i: (0, 0))
    res = pl.pallas_call(
        body, name=name, grid=(t_len // tm,), in_specs=[row] * (n + 1) + [vec] * n + [row], out_specs=(row,) + (vec,) * n,
        out_shape=(jax.ShapeDtypeStruct((t_len, d), F32),) + (jax.ShapeDtypeStruct((1, d), F32),) * n,
        compiler_params=_params(1))(*dys, x, *gs, dres)
    return res[0], res[1:]


def _loss_head(h, g, target):
    t_len, d = h.shape
    tm = _tile(t_len, (256, 128))

    def body(h_ref, g_ref, t_ref, dh_ref, dg_ref, loss_ref):
        xv = h_ref[...]
        gv = g_ref[...]
        r = lax.rsqrt(jnp.mean(xv * xv, axis=-1, keepdims=True) + EPS)
        xhat = xv * r
        err = xhat * gv - t_ref[...]
        dyv = err * (1.0 / d)
        dxn = dyv * gv
        dh_ref[...] = r * (dxn - xhat * jnp.mean(dxn * xhat, axis=-1, keepdims=True))
        part = jnp.sum(dyv * xhat, axis=0, keepdims=True)
        lpart = jnp.zeros((8, LANE), F32) + (0.5 / d) * jnp.sum(err * err)

        @pl.when(pl.program_id(0) == 0)
        def _():
            dg_ref[...] = part
            loss_ref[...] = lpart

        @pl.when(pl.program_id(0) > 0)
        def _():
            dg_ref[...] += part
            loss_ref[...] += lpart

    row = pl.BlockSpec((tm, d), lambda i: (i, 0))
    vec = pl.BlockSpec((1, d), lambda i: (0, 0))
    return pl.pallas_call(
        body, name="loss_head", grid=(t_len // tm,), in_specs=[row, vec, row],
        out_specs=(row, vec, pl.BlockSpec((8, LANE), lambda i: (0, 0))),
        out_shape=(jax.ShapeDtypeStruct((t_len, d), F32), jax.ShapeDtypeStruct((1, d), F32),
                   jax.ShapeDtypeStruct((8, LANE), F32)),
        compiler_params=_params(1))(h, g, target)


def _chunk_row(shape):
    return lax.broadcasted_iota(jnp.int32, shape, 0) % GLA_CHUNK


def _gla_gate_fwd(a, w_a2p, b_a2):
    t_len = a.shape[0]
    kd = w_a2p.shape[1]
    tm = _tile(t_len, (256, 128, 64))

    def body(a_ref, w_ref, b_ref, ga_ref, cum_ref):
        ga = _dot(a_ref[...], w_ref[...].astype(BF16), NN) + b_ref[...]
        ga_ref[...] = ga
        la = (jnp.minimum(ga, 0.0) - jnp.log(1.0 + jnp.exp(-jnp.abs(ga)))) * (1.0 / GATE_NORMALIZER)
        row = _chunk_row(la.shape)
        s = 1
        while s < GLA_CHUNK:
            la = la + jnp.where(row >= s, pltpu.roll(la, s, 0), 0.0)
            s *= 2
        cum_ref[...] = la

    return pl.pallas_call(
        body, name="gla_gate_fwd", grid=(t_len // tm,),
        in_specs=[pl.BlockSpec((tm, A_PAD), lambda i: (i, 0)), pl.BlockSpec((A_PAD, kd), lambda i: (0, 0)),
                  pl.BlockSpec((1, kd), lambda i: (0, 0))],
        out_specs=(pl.BlockSpec((tm, kd), lambda i: (i, 0)), pl.BlockSpec((tm, kd), lambda i: (i, 0))),
        out_shape=(jax.ShapeDtypeStruct((t_len, kd), F32), jax.ShapeDtypeStruct((t_len, kd), F32)),
        compiler_params=_params(1))(a, w_a2p, b_a2)


def _gla_gate_bwd(dcum, ga, a, w_a2p):
    t_len, kd = dcum.shape
    tm = _tile(t_len, (256, 128, 64))

    def body(dc_ref, ga_ref, a_ref, w_ref, da_ref, dw_ref, db_ref):
        x = dc_ref[...]
        row = _chunk_row(x.shape)
        s = 1
        while s < GLA_CHUNK:
            x = x + jnp.where(row < GLA_CHUNK - s, pltpu.roll(x, tm - s, 0), 0.0)
            s *= 2
        dga = x * (1.0 / GATE_NORMALIZER) * _sigmoid(-ga_ref[...])
        dgab = dga.astype(BF16)
        da_ref[...] = _dot(dgab, w_ref[...].astype(BF16), NT).astype(da_ref.dtype)
        dw = _dot(a_ref[...], dgab, TN)
        db = jnp.sum(dga, axis=0, keepdims=True)

        @pl.when(pl.program_id(0) == 0)
        def _():
            dw_ref[...] = dw
            db_ref[...] = db

        @pl.when(pl.program_id(0) > 0)
        def _():
            dw_ref[...] += dw
            db_ref[...] += db

    wide = pl.BlockSpec((tm, kd), lambda i: (i, 0))
    return pl.pallas_call(
        body, name="gla_gate_bwd", grid=(t_len // tm,),
        in_specs=[wide, wide, pl.BlockSpec((tm, A_PAD), lambda i: (i, 0)), pl.BlockSpec((A_PAD, kd), lambda i: (0, 0))],
        out_specs=(pl.BlockSpec((tm, A_PAD), lambda i: (i, 0)), pl.BlockSpec((A_PAD, kd), lambda i: (0, 0)),
                   pl.BlockSpec((1, kd), lambda i: (0, 0))),
        out_shape=(jax.ShapeDtypeStruct((t_len, A_PAD), BF16), jax.ShapeDtypeStruct((A_PAD, kd), F32),
                   jax.ShapeDtypeStruct((1, kd), F32)),
        compiler_params=_params(1))(dcum, ga, a, w_a2p)


GLA_STEP_CHUNKS = 4


def _gla_dims():
    dk = GLA_KEY_DIM // GLA_HEADS
    dv = GLA_VAL_DIM // GLA_HEADS
    return dk, dv


def _gla_fwd(proj, cum):
    t_len = proj.shape[0]
    dk, dv = _gla_dims()
    nc = t_len // GLA_CHUNK
    c = GLA_CHUNK
    scale = dk ** -0.5
    v0 = 2 * GLA_KEY_DIM // dv

    per = _tile(nc, (GLA_STEP_CHUNKS, 2, 1))
    rows = per * c

    def body(q_ref, k_ref, v_ref, cum_ref, o_ref, st_ref, s_scr):
        @pl.when(pl.program_id(1) == 0)
        def _():
            s_scr[...] = jnp.zeros_like(s_scr)

        tri = lax.broadcasted_iota(jnp.int32, (c, c), 0) >= lax.broadcasted_iota(jnp.int32, (c, c), 1)
        for i in range(per):
            rs = slice(i * c, (i + 1) * c)
            cm = cum_ref[rs, :]
            last = cm[c - 1:c, :]
            q = q_ref[rs, :].astype(F32) * scale
            k = k_ref[rs, :].astype(F32)
            v = v_ref[rs, :].astype(BF16)
            qd = (q * jnp.exp(cm)).astype(BF16)
            ki = (k * jnp.exp(-cm)).astype(BF16)
            ke = (k * jnp.exp(last - cm)).astype(BF16)
            sc = jnp.where(tri, _dot(qd, ki, NT), 0.0)
            st = s_scr[...]
            st_ref[i] = st
            o_ref[rs, :] = _dot(sc.astype(BF16), v, NN) + _dot(qd, st.astype(BF16), NT)
            s_scr[...] = st * jnp.exp(last) + _dot(v, ke, TN)

    return pl.pallas_call(
        body, name="gla_fwd", grid=(GLA_HEADS, nc // per),
        in_specs=[pl.BlockSpec((rows, dk), lambda h, n: (n, h)),
                  pl.BlockSpec((rows, dk), lambda h, n: (n, GLA_HEADS + h)),
                  pl.BlockSpec((rows, dv), lambda h, n: (n, v0 + h)),
                  pl.BlockSpec((rows, dk), lambda h, n: (n, h))],
        out_specs=(pl.BlockSpec((rows, dv), lambda h, n: (n, h)),
                   pl.BlockSpec((None, per, dv, dk), lambda h, n: (h, n, 0, 0))),
        out_shape=(jax.ShapeDtypeStruct((t_len, GLA_VAL_DIM), F32),
                   jax.ShapeDtypeStruct((GLA_HEADS, nc, dv, dk), F32)),
        scratch_shapes=[pltpu.VMEM((dv, dk), F32)], compiler_params=_params(2))(proj, proj, proj, cum)


def _gla_bwd(proj, cum, states, do):
    t_len = proj.shape[0]
    dk, dv = _gla_dims()
    nc = t_len // GLA_CHUNK
    c = GLA_CHUNK
    scale = dk ** -0.5
    v0 = 2 * GLA_KEY_DIM // dv

    per = _tile(nc, (GLA_STEP_CHUNKS, 2, 1))
    rows = per * c

    def body(q_ref, k_ref, v_ref, cum_ref, st_ref, do_ref, dq_ref, dk_ref, dv_ref, dc_ref, ds_scr):
        @pl.when(pl.program_id(1) == 0)
        def _():
            ds_scr[...] = jnp.zeros_like(ds_scr)

        tri = lax.broadcasted_iota(jnp.int32, (c, c), 0) >= lax.broadcasted_iota(jnp.int32, (c, c), 1)
        row = lax.broadcasted_iota(jnp.int32, (c, dk), 0)
        for i in reversed(range(per)):
            rs = slice(i * c, (i + 1) * c)
            cm = cum_ref[rs, :]
            last = cm[c - 1:c, :]
            e_c = jnp.exp(cm)
            e_nc = jnp.exp(-cm)
            e_lc = jnp.exp(last - cm)
            e_l = jnp.exp(last)
            q = q_ref[rs, :].astype(F32) * scale
            k = k_ref[rs, :].astype(F32)
            v = v_ref[rs, :].astype(BF16)
            dov = do_ref[rs, :]
            qd32 = q * e_c
            ki32 = k * e_nc
            ke32 = k * e_lc
            qd = qd32.astype(BF16)
            ki = ki32.astype(BF16)
            ke = ke32.astype(BF16)
            st = st_ref[i]
            dst = ds_scr[...]
            dstb = dst.astype(BF16)
            am = jnp.where(tri, _dot(dov, v, NT), 0.0).astype(BF16)
            pm = jnp.where(tri, _dot(qd, ki, NT), 0.0).astype(BF16)
            dqd = _dot(am, ki, NN) + _dot(dov, st.astype(BF16), NN)
            dki = _dot(am, qd, TN)
            dvv = _dot(pm, dov, TN) + _dot(ke, dstb, NT)
            dke = _dot(v, dstb, NN)
            d_el = jnp.sum(dst * st, axis=0, keepdims=True)
            ds_scr[...] = dst * e_l + _dot(dov, qd, TN)
            dq_ref[rs, :] = (dqd * scale * e_c).astype(dq_ref.dtype)
            dk_ref[rs, :] = (dki * e_nc + dke * e_lc).astype(dk_ref.dtype)
            dv_ref[rs, :] = dvv.astype(dv_ref.dtype)
            dkeke = dke * ke32
            dcum = dqd * qd32 - dki * ki32 - dkeke
            dlast = jnp.sum(dkeke, axis=0, keepdims=True) + d_el * e_l
            dc_ref[rs, :] = jnp.where(row == c - 1, dcum + dlast, dcum)

    rev = nc // per - 1
    return pl.pallas_call(
        body, name="gla_bwd", grid=(GLA_HEADS, nc // per),
        in_specs=[pl.BlockSpec((rows, dk), lambda h, n: (rev - n, h)),
                  pl.BlockSpec((rows, dk), lambda h, n: (rev - n, GLA_HEADS + h)),
                  pl.BlockSpec((rows, dv), lambda h, n: (rev - n, v0 + h)),
                  pl.BlockSpec((rows, dk), lambda h, n: (rev - n, h)),
                  pl.BlockSpec((None, per, dv, dk), lambda h, n: (h, rev - n, 0, 0)),
                  pl.BlockSpec((rows, dv), lambda h, n: (rev - n, h))],
        out_specs=(pl.BlockSpec((rows, dk), lambda h, n: (rev - n, h)),
                   pl.BlockSpec((rows, dk), lambda h, n: (rev - n, h)),
                   pl.BlockSpec((rows, dv), lambda h, n: (rev - n, h)),
                   pl.BlockSpec((rows, dk), lambda h, n: (rev - n, h))),
        out_shape=(jax.ShapeDtypeStruct((t_len, GLA_KEY_DIM), BF16), jax.ShapeDtypeStruct((t_len, GLA_KEY_DIM), BF16),
                   jax.ShapeDtypeStruct((t_len, GLA_VAL_DIM), BF16), jax.ShapeDtypeStruct((t_len, GLA_KEY_DIM), F32)),
        scratch_shapes=[pltpu.VMEM((dv, dk), F32)], compiler_params=_params(2))(proj, proj, proj, cum, states, do)


def _gla_out_fwd(o, proj, gn):
    t_len = o.shape[0]
    _, dv = _gla_dims()
    tm = _tile(t_len, (512, 256, 128))
    r0 = (2 * GLA_KEY_DIM + GLA_VAL_DIM) // dv

    def body(o_ref, r_ref, g_ref, y_ref):
        ov = o_ref[...]
        rs = lax.rsqrt(jnp.mean(ov * ov, axis=-1, keepdims=True) + EPS)
        rv = r_ref[...].astype(F32)
        y_ref[...] = (ov * rs * g_ref[...] * (rv * _sigmoid(rv))).astype(y_ref.dtype)

    return pl.pallas_call(
        body, name="gla_out_fwd", grid=(t_len // tm, GLA_HEADS),
        in_specs=[pl.BlockSpec((tm, dv), lambda i, h: (i, h)), pl.BlockSpec((tm, dv), lambda i, h: (i, r0 + h)),
                  pl.BlockSpec((1, dv), lambda i, h: (0, 0))],
        out_specs=pl.BlockSpec((tm, dv), lambda i, h: (i, h)),
        out_shape=jax.ShapeDtypeStruct((t_len, GLA_VAL_DIM), BF16), compiler_params=_params(2))(o, proj, gn)


def _gla_out_bwd(dy, o, proj, gn):
    t_len = o.shape[0]
    _, dv = _gla_dims()
    tm = _tile(t_len, (512, 256, 128))
    r0 = (2 * GLA_KEY_DIM + GLA_VAL_DIM) // dv

    def body(dy_ref, o_ref, r_ref, g_ref, do_ref, dr_ref, dg_ref):
        ov = o_ref[...]
        gv = g_ref[...]
        rs = lax.rsqrt(jnp.mean(ov * ov, axis=-1, keepdims=True) + EPS)
        xhat = ov * rs
        rv = r_ref[...].astype(F32)
        sg = _sigmoid(rv)
        gate = rv * sg
        dyv = dy_ref[...].astype(F32)
        dn = dyv * gate
        dr_ref[...] = (dyv * xhat * gv * (sg * (1.0 + rv * (1.0 - sg)))).astype(dr_ref.dtype)
        dxn = dn * gv
        do_ref[...] = (rs * (dxn - xhat * jnp.mean(dxn * xhat, axis=-1, keepdims=True))).astype(do_ref.dtype)
        part = jnp.sum(dn * xhat, axis=0, keepdims=True)
        first = (pl.program_id(0) == 0) & (pl.program_id(1) == 0)

        @pl.when(first)
        def _():
            dg_ref[...] = part

        @pl.when(jnp.logical_not(first))
        def _():
            dg_ref[...] += part

    blk = pl.BlockSpec((tm, dv), lambda i, h: (i, h))
    return pl.pallas_call(
        body, name="gla_out_bwd", grid=(t_len // tm, GLA_HEADS),
        in_specs=[blk, blk, pl.BlockSpec((tm, dv), lambda i, h: (i, r0 + h)), pl.BlockSpec((1, dv), lambda i, h: (0, 0))],
        out_specs=(blk, blk, pl.BlockSpec((1, dv), lambda i, h: (0, 0))),
        out_shape=(jax.ShapeDtypeStruct((t_len, GLA_VAL_DIM), BF16), jax.ShapeDtypeStruct((t_len, GLA_VAL_DIM), BF16),
                   jax.ShapeDtypeStruct((1, dv), F32)),
        compiler_params=_params(2))(dy, o, proj, gn)


def _alibi_slopes():
    n = ATT_HEADS
    start = 2.0 ** (-8.0 / n)
    return [start ** (i + 1) for i in range(n)]


def _att_masks(d):
    b = ATT_BLOCK
    qa = lax.broadcasted_iota(jnp.int32, (b, b), 0)
    kb = lax.broadcasted_iota(jnp.int32, (b, b), 1)
    dist_c = qa - kb
    dist_p = qa - kb + b
    return dist_c >= 0, dist_p <= b, (dist_c * d).astype(F32), (dist_p * d).astype(F32)


def _to_dilated(name, x, d, c0=0, w=None):
    part = x if w is None else x[:, c0:c0 + w]
    return part.reshape(x.shape[0] // d, -1)


def _from_dilated(name, y, d):
    return y.reshape(y.shape[0] * d, y.shape[1] // d)


def _att_views(q_all, kv, g):
    d = DILATIONS[g]
    hd = ATT_HEADS * HEAD_DIM
    if d == 1:
        return q_all, kv
    return _to_dilated(f"q_dilated{g}", q_all, d, g * hd, hd), _to_dilated(f"kv_dilated{g}", kv, d)


def _att_fwd(views, g):
    d = DILATIONS[g]
    assert WINDOWS[g] // d == ATT_BLOCK
    qv, kvv = views
    hd = ATT_HEADS * HEAD_DIM
    sub = kvv.shape[0]
    t_len = sub * d
    nb = sub // ATT_BLOCK
    b = ATT_BLOCK
    e = HEAD_DIM
    scale = e ** -0.5
    slopes = _alibi_slopes()
    qc = (lambda r: 3 * r + g) if d == 1 else (lambda r: r)

    def body(q_ref, kp_ref, kc_ref, vp_ref, vc_ref, o_ref, l_ref, s_scr, p_scr, li_scr):
        ib = pl.program_id(1)
        valid_c, valid_p0, dist_c, dist_p = _att_masks(d)
        valid_p = valid_p0 & (ib > 0)
        for h in range(ATT_HEADS):
            hs = slice(h * e, (h + 1) * e)
            qh = q_ref[:, hs]
            s_scr[h, 0] = _dot(qh, kc_ref[:, hs], NT)
            s_scr[h, 1] = _dot(qh, kp_ref[:, hs], NT)
        l_ref[...] = jnp.zeros_like(l_ref)
        for h in range(ATT_HEADS):
            s_c = jnp.where(valid_c, s_scr[h, 0] * scale - slopes[h] * dist_c, NEG)
            s_p = jnp.where(valid_p, s_scr[h, 1] * scale - slopes[h] * dist_p, NEG)
            m = jnp.maximum(jnp.max(s_c, axis=1, keepdims=True), jnp.max(s_p, axis=1, keepdims=True))
            p_c = jnp.where(valid_c, jnp.exp(s_c - m), 0.0)
            p_p = jnp.where(valid_p, jnp.exp(s_p - m), 0.0)
            l = jnp.sum(p_c, axis=1, keepdims=True) + jnp.sum(p_p, axis=1, keepdims=True)
            p_scr[h, 0] = p_c.astype(BF16)
            p_scr[h, 1] = p_p.astype(BF16)
            li_scr[:, h:h + 1] = 1.0 / l
            l_ref[:, h:h + 1] = m + jnp.log(l)
        for h in range(ATT_HEADS):
            hs = slice(h * e, (h + 1) * e)
            acc = _dot(p_scr[h, 0], vc_ref[:, hs], NN) + _dot(p_scr[h, 1], vp_ref[:, hs], NN)
            o_ref[:, hs] = acc * li_scr[:, h:h + 1]

    blk = (b, hd)
    cblk = (b, LANE)
    o, lse = pl.pallas_call(
        body, name=f"att_fwd{g}", grid=(d, nb),
        scratch_shapes=[pltpu.VMEM((ATT_HEADS, 2, b, b), F32), pltpu.VMEM((ATT_HEADS, 2, b, b), BF16),
                        pltpu.VMEM((b, LANE), F32)],
        in_specs=[pl.BlockSpec(blk, lambda r, i: (i, qc(r))),
                  pl.BlockSpec(blk, lambda r, i: (jnp.maximum(i - 1, 0), 2 * r)),
                  pl.BlockSpec(blk, lambda r, i: (i, 2 * r)),
                  pl.BlockSpec(blk, lambda r, i: (jnp.maximum(i - 1, 0), 2 * r + 1)),
                  pl.BlockSpec(blk, lambda r, i: (i, 2 * r + 1))],
        out_specs=(pl.BlockSpec(blk, lambda r, i: (i, r)), pl.BlockSpec(cblk, lambda r, i: (i, r))),
        out_shape=(jax.ShapeDtypeStruct((sub, d * hd), F32), jax.ShapeDtypeStruct((sub, d * LANE), F32)),
        compiler_params=_params(2))(qv, kvv, kvv, kvv, kvv)
    return _from_dilated(f"o_natural{g}", o, d), lse.reshape(t_len, LANE)


def _att_merge(os, ls):
    t_len, hd = os[0].shape
    tm = _tile(t_len, (256, 128))
    e = HEAD_DIM

    def body(o0, o1, o2, l0, l1, l2, of_ref, ob_ref, l_ref):
        a0, a1, a2 = l0[...], l1[...], l2[...]
        m = jnp.maximum(jnp.maximum(a0, a1), a2)
        e0, e1, e2 = jnp.exp(a0 - m), jnp.exp(a1 - m), jnp.exp(a2 - m)
        den = e0 + e1 + e2
        w0, w1, w2 = e0 / den, e1 / den, e2 / den
        l_ref[...] = m + jnp.log(den)
        for h in range(ATT_HEADS):
            hs = slice(h * e, (h + 1) * e)
            c = slice(h, h + 1)
            o = w0[:, c] * o0[:, hs] + w1[:, c] * o1[:, hs] + w2[:, c] * o2[:, hs]
            of_ref[:, hs] = o
            ob_ref[:, hs] = o.astype(ob_ref.dtype)

    row = pl.BlockSpec((tm, hd), lambda i: (i, 0))
    crow = pl.BlockSpec((tm, LANE), lambda i: (i, 0))
    return pl.pallas_call(
        body, name="att_merge", grid=(t_len // tm,), in_specs=[row] * 3 + [crow] * 3, out_specs=(row, row, crow),
        out_shape=(jax.ShapeDtypeStruct((t_len, hd), F32), jax.ShapeDtypeStruct((t_len, hd), BF16),
                   jax.ShapeDtypeStruct((t_len, LANE), F32)),
        compiler_params=_params(1))(*os, *ls)


def _att_delta(do, o):
    t_len, hd = o.shape
    tm = _tile(t_len, (256, 128))
    e = HEAD_DIM

    def body(do_ref, o_ref, d_ref):
        d_ref[...] = jnp.zeros_like(d_ref)
        for h in range(ATT_HEADS):
            hs = slice(h * e, (h + 1) * e)
            d_ref[:, h:h + 1] = jnp.sum(do_ref[:, hs].astype(F32) * o_ref[:, hs], axis=1, keepdims=True)

    row = pl.BlockSpec((tm, hd), lambda i: (i, 0))
    return pl.pallas_call(
        body, name="att_delta", grid=(t_len // tm,), in_specs=[row, row],
        out_specs=pl.BlockSpec((tm, LANE), lambda i: (i, 0)),
        out_shape=jax.ShapeDtypeStruct((t_len, LANE), F32), compiler_params=_params(1))(do, o)


def _att_bwd(views, delta, lse, do, g):
    d = DILATIONS[g]
    qv, kvv = views
    hd = ATT_HEADS * HEAD_DIM
    sub = kvv.shape[0]
    t_len = sub * d
    nb = sub // ATT_BLOCK
    b = ATT_BLOCK
    e = HEAD_DIM
    scale = e ** -0.5
    slopes = _alibi_slopes()
    qc = (lambda r: 3 * r + g) if d == 1 else (lambda r: r)
    dlv = delta.reshape(sub, d * LANE)
    lv = lse.reshape(sub, d * LANE)
    dov = do if d == 1 else _to_dilated(f"do_dilated{g}", do, d)

    def body(qj_ref, qn_ref, kp_ref, kc_ref, vp_ref, vc_ref, doj_ref, don_ref, dj_ref, dn_ref, lj_ref, ln_ref,
             dq_ref, dk_ref, dv_ref, s_scr, dp_scr, p_scr, ds_scr):
        j = pl.program_id(1)
        valid_c, valid_p0, dist_c, dist_p = _att_masks(d)
        valid = (valid_c, valid_p0 & (j > 0), valid_p0 & (j + 1 < nb))
        dist = (dist_c, dist_p, dist_p)
        for h in range(ATT_HEADS):
            hs = slice(h * e, (h + 1) * e)
            qj, qn = qj_ref[:, hs], qn_ref[:, hs]
            kc, kp = kc_ref[:, hs], kp_ref[:, hs]
            vc, vp = vc_ref[:, hs], vp_ref[:, hs]
            doj, don = doj_ref[:, hs], don_ref[:, hs]
            s_scr[h, 0] = _dot(qj, kc, NT)
            s_scr[h, 1] = _dot(qj, kp, NT)
            s_scr[h, 2] = _dot(qn, kc, NT)
            dp_scr[h, 0] = _dot(doj, vc, NT)
            dp_scr[h, 1] = _dot(doj, vp, NT)
            dp_scr[h, 2] = _dot(don, vc, NT)
        for h in range(ATT_HEADS):
            c = slice(h, h + 1)
            lse_t = (lj_ref[:, c], lj_ref[:, c], ln_ref[:, c])
            dlt_t = (dj_ref[:, c], dj_ref[:, c], dn_ref[:, c])
            for t in range(3):
                s = s_scr[h, t] * scale - slopes[h] * dist[t]
                p = jnp.where(valid[t], jnp.exp(jnp.where(valid[t], s - lse_t[t], NEG)), 0.0)
                p_scr[h, t] = p.astype(BF16)
                ds_scr[h, t] = (p * (dp_scr[h, t] - dlt_t[t])).astype(BF16)
        for h in range(ATT_HEADS):
            hs = slice(h * e, (h + 1) * e)
            dq = _dot(ds_scr[h, 0], kc_ref[:, hs], NN) + _dot(ds_scr[h, 1], kp_ref[:, hs], NN)
            dk = _dot(ds_scr[h, 0], qj_ref[:, hs], TN) + _dot(ds_scr[h, 2], qn_ref[:, hs], TN)
            dv = _dot(p_scr[h, 0], doj_ref[:, hs], TN) + _dot(p_scr[h, 2], don_ref[:, hs], TN)
            dq_ref[:, hs] = (dq * scale).astype(dq_ref.dtype)
            dk_ref[:, hs] = (dk * scale).astype(dk_ref.dtype)
            dv_ref[:, hs] = dv.astype(dv_ref.dtype)

    blk = (b, hd)
    cblk = (b, LANE)
    nxt = lambda i: jnp.minimum(i + 1, nb - 1)
    prv = lambda i: jnp.maximum(i - 1, 0)
    tiles = (ATT_HEADS, 3, b, b)
    dq, dk, dv = pl.pallas_call(
        body, name=f"att_bwd{g}", grid=(d, nb),
        scratch_shapes=[pltpu.VMEM(tiles, F32), pltpu.VMEM(tiles, F32), pltpu.VMEM(tiles, BF16), pltpu.VMEM(tiles, BF16)],
        in_specs=[pl.BlockSpec(blk, lambda r, i: (i, qc(r))),
                  pl.BlockSpec(blk, lambda r, i: (nxt(i), qc(r))),
                  pl.BlockSpec(blk, lambda r, i: (prv(i), 2 * r)),
                  pl.BlockSpec(blk, lambda r, i: (i, 2 * r)),
                  pl.BlockSpec(blk, lambda r, i: (prv(i), 2 * r + 1)),
                  pl.BlockSpec(blk, lambda r, i: (i, 2 * r + 1)),
                  pl.BlockSpec(blk, lambda r, i: (i, r)),
                  pl.BlockSpec(blk, lambda r, i: (nxt(i), r)),
                  pl.BlockSpec(cblk, lambda r, i: (i, r)),
                  pl.BlockSpec(cblk, lambda r, i: (nxt(i), r)),
                  pl.BlockSpec(cblk, lambda r, i: (i, r)),
                  pl.BlockSpec(cblk, lambda r, i: (nxt(i), r))],
        out_specs=(pl.BlockSpec(blk, lambda r, i: (i, r)),) * 3,
        out_shape=(jax.ShapeDtypeStruct((sub, d * hd), BF16),) * 3,
        compiler_params=_params(2))(qv, qv, kvv, kvv, kvv, kvv, dov, dov, dlv, dlv, lv, lv)
    return tuple(_from_dilated(f"{n}_natural{g}", t, d) for n, t in (("dq", dq), ("dk", dk), ("dv", dv)))


def _kv_grad_sum(dks, dvs):
    t_len, hd = dks[0].shape
    tm = _tile(t_len, (256, 128))

    def body(k0, k1, k2, v0, v1, v2, o_ref):
        o_ref[:, :hd] = (k0[...].astype(F32) + k1[...].astype(F32) + k2[...].astype(F32)).astype(o_ref.dtype)
        o_ref[:, hd:] = (v0[...].astype(F32) + v1[...].astype(F32) + v2[...].astype(F32)).astype(o_ref.dtype)

    row = pl.BlockSpec((tm, hd), lambda i: (i, 0))
    return pl.pallas_call(
        body, name="kv_grad_sum", grid=(t_len // tm,), in_specs=[row] * 6,
        out_specs=pl.BlockSpec((tm, 2 * hd), lambda i: (i, 0)),
        out_shape=jax.ShapeDtypeStruct((t_len, 2 * hd), BF16), compiler_params=_params(1))(*dks, *dvs)


HALO = 16
INV_SQRT2 = 1.0 / math.sqrt(2.0)
INV_SQRT2PI = 1.0 / math.sqrt(2.0 * math.pi)


def _conv_taps(g, halo, cw, cb):
    row = lax.broadcasted_iota(jnp.int32, g.shape, 0)
    h1 = halo[HALO - 1:HALO, :]
    h2 = halo[HALO - 2:HALO - 1, :]
    g1 = jnp.where(row == 0, h1, pltpu.roll(g, 1, 0))
    g2 = jnp.where(row == 0, h2, jnp.where(row == 1, h1, pltpu.roll(g, 2, 0)))
    gc = cw[0:1, :] * g2 + cw[1:2, :] * g1 + cw[2:3, :] * g + cb
    return gc, g1, g2


def _glu_specs(t_len, f, tm, tc):
    nj = f // tc
    hb = tm // HALO
    u = pl.BlockSpec((tm, tc), lambda j, i: (i, j))
    g = pl.BlockSpec((tm, tc), lambda j, i: (i, nj + j))
    gh = pl.BlockSpec((HALO, tc), lambda j, i: (jnp.maximum(i * hb - 1, 0), nj + j))
    cw = pl.BlockSpec((8, tc), lambda j, i: (0, j))
    cb = pl.BlockSpec((1, tc), lambda j, i: (0, j))
    return u, g, gh, cw, cb


def _glu_fwd(name, up, cw, cb):
    t_len = up.shape[0]
    f = up.shape[1] // 2
    tm = _tile(t_len, (512, 256, 128))
    tc = _tile(f, (1408, 1024, 512, 256, 128))
    u_s, g_s, gh_s, cw_s, cb_s = _glu_specs(t_len, f, tm, tc)

    def body(u_ref, g_ref, gh_ref, cw_ref, cb_ref, o_ref):
        first = pl.program_id(1) == 0
        halo = jnp.where(first, 0.0, gh_ref[...].astype(F32))
        gc, _, _ = _conv_taps(g_ref[...].astype(F32), halo, cw_ref[...], cb_ref[...])
        gel = 0.5 * gc * (1.0 + lax.erf(gc * INV_SQRT2))
        o_ref[...] = (gel * u_ref[...].astype(F32)).astype(o_ref.dtype)

    return pl.pallas_call(
        body, name=name, grid=(f // tc, t_len // tm), in_specs=[u_s, g_s, gh_s, cw_s, cb_s],
        out_specs=pl.BlockSpec((tm, tc), lambda j, i: (i, j)),
        out_shape=jax.ShapeDtypeStruct((t_len, f), BF16), compiler_params=_params(2))(up, up, up, cw, cb)


def _glu_bwd_a(name, dact, up, cw, cb):
    t_len = up.shape[0]
    f = up.shape[1] // 2
    tm = _tile(t_len, (256, 128))
    tc = _tile(f, (1408, 1024, 512, 256, 128))
    u_s, g_s, gh_s, cw_s, cb_s = _glu_specs(t_len, f, tm, tc)

    def body(da_ref, u_ref, g_ref, gh_ref, cw_ref, cb_ref, du_ref, dgc_ref, w0_ref, w1_ref, w2_ref, b_ref):
        first = pl.program_id(1) == 0
        halo = jnp.where(first, 0.0, gh_ref[...].astype(F32))
        g = g_ref[...].astype(F32)
        gc, g1, g2 = _conv_taps(g, halo, cw_ref[...], cb_ref[...])
        phi = 0.5 * (1.0 + lax.erf(gc * INV_SQRT2))
        dgel = phi + gc * jnp.exp(-0.5 * gc * gc) * INV_SQRT2PI
        da = da_ref[...].astype(F32)
        du_ref[...] = (da * gc * phi).astype(du_ref.dtype)
        dgc = da * u_ref[...].astype(F32) * dgel
        dgc_ref[...] = dgc.astype(dgc_ref.dtype)
        parts = (jnp.sum(dgc * g2, axis=0, keepdims=True), jnp.sum(dgc * g1, axis=0, keepdims=True),
                 jnp.sum(dgc * g, axis=0, keepdims=True), jnp.sum(dgc, axis=0, keepdims=True))
        refs = (w0_ref, w1_ref, w2_ref, b_ref)

        @pl.when(first)
        def _():
            for r, p in zip(refs, parts):
                r[...] = p

        @pl.when(jnp.logical_not(first))
        def _():
            for r, p in zip(refs, parts):
                r[...] += p

    tile = pl.BlockSpec((tm, tc), lambda j, i: (i, j))
    vec = pl.BlockSpec((1, tc), lambda j, i: (0, j))
    vshape = jax.ShapeDtypeStruct((1, f), F32)
    return pl.pallas_call(
        body, name=name, grid=(f // tc, t_len // tm), in_specs=[tile, u_s, g_s, gh_s, cw_s, cb_s],
        out_specs=(tile, tile, vec, vec, vec, vec),
        out_shape=(jax.ShapeDtypeStruct((t_len, 2 * f), BF16), jax.ShapeDtypeStruct((t_len, f), BF16),
                   vshape, vshape, vshape, vshape),
        compiler_params=_params(2))(dact, up, up, up, cw, cb)


def _glu_bwd_b(name, dup, dgc, cw):
    t_len, f = dgc.shape
    tm = _tile(t_len, (256, 128, 64))
    hb = tm // HALO
    n_i = t_len // tm
    last_hb = t_len // HALO - 1

    def body(dup_ref, d_ref, dh_ref, cw_ref, o_ref):
        last = pl.program_id(0) == n_i - 1
        halo = jnp.where(last, 0.0, dh_ref[...].astype(F32))
        dd = d_ref[...].astype(F32)
        row = lax.broadcasted_iota(jnp.int32, dd.shape, 0)
        h0 = halo[0:1, :]
        h1 = halo[1:2, :]
        d1 = jnp.where(row == tm - 1, h0, pltpu.roll(dd, tm - 1, 0))
        d2 = jnp.where(row == tm - 1, h1, jnp.where(row == tm - 2, h0, pltpu.roll(dd, tm - 2, 0)))
        cwv = cw_ref[...]
        dg = cwv[2:3, :] * dd + cwv[1:2, :] * d1 + cwv[0:1, :] * d2
        o_ref[...] = dg.astype(o_ref.dtype)

    row_s = pl.BlockSpec((tm, f), lambda i: (i, 0))
    return pl.pallas_call(
        body, name=name, grid=(n_i,),
        in_specs=[ANY, row_s, pl.BlockSpec((HALO, f), lambda i: (jnp.minimum((i + 1) * hb, last_hb), 0)),
                  pl.BlockSpec((8, f), lambda i: (0, 0))],
        out_specs=pl.BlockSpec((tm, f), lambda i: (i, 1)), input_output_aliases={0: 0},
        out_shape=jax.ShapeDtypeStruct((t_len, 2 * f), BF16), compiler_params=_params(1))(dup, dgc, dgc, cw)


def _adamw(name, w, g, m, v):
    rows, cols = w.shape
    gcols = g.shape[1]
    n_out = 3 if gcols == cols else 4
    tr = _tile(rows, (256, 128, 64, 32, 16, 8))
    c1 = 1.0 / (1.0 - ADAM_B1 ** ADAM_STEP)
    c2 = 1.0 / (1.0 - ADAM_B2 ** ADAM_STEP)

    def body(w_ref, g_ref, m_ref, v_ref, d_ref, nm_ref, nv_ref, *g_out):
        gv = g_ref[...][:, :cols]
        nm = ADAM_B1 * m_ref[...] + (1.0 - ADAM_B1) * gv
        nv = ADAM_B2 * v_ref[...] + (1.0 - ADAM_B2) * (gv * gv)
        nm_ref[...] = nm
        nv_ref[...] = nv
        d_ref[...] = -ADAM_LR * ((nm * c1) / (jnp.sqrt(nv * c2) + ADAM_EPS) + ADAM_WD * w_ref[...])
        for ref in g_out:
            ref[...] = gv

    blk = pl.BlockSpec((tr, cols), lambda i: (i, 0))
    gblk = pl.BlockSpec((tr, gcols), lambda i: (i, 0))
    shp = jax.ShapeDtypeStruct((rows, cols), F32)
    return pl.pallas_call(body, name=name, grid=(rows // tr,), in_specs=[blk, gblk, blk, blk], out_specs=(blk,) * n_out,
                          out_shape=(shp,) * n_out, compiler_params=_params(1))(w, g, m, v)


def _adamw_layers(name, w, gs, m, v):
    _, rows, cols = w.shape
    tr = _tile(rows, (128, 64, 32, 16, 8))
    nr = rows // tr
    c1 = 1.0 / (1.0 - ADAM_B1 ** ADAM_STEP)
    c2 = 1.0 / (1.0 - ADAM_B2 ** ADAM_STEP)

    def body(w_ref, g0_ref, g1_ref, m_ref, v_ref, d_ref, nm_ref, nv_ref, g_ref):
        gv = jnp.where(pl.program_id(0) == 0, g0_ref[...], g1_ref[...])
        nm = ADAM_B1 * m_ref[...] + (1.0 - ADAM_B1) * gv
        nv = ADAM_B2 * v_ref[...] + (1.0 - ADAM_B2) * (gv * gv)
        nm_ref[...] = nm
        nv_ref[...] = nv
        d_ref[...] = -ADAM_LR * ((nm * c1) / (jnp.sqrt(nv * c2) + ADAM_EPS) + ADAM_WD * w_ref[...])
        g_ref[...] = gv

    blk = pl.BlockSpec((None, tr, cols), lambda l, i: (l, i, 0))
    g0_blk = pl.BlockSpec((tr, cols), lambda l, i: (jnp.where(l == 0, i, nr - 1), 0))
    g1_blk = pl.BlockSpec((tr, cols), lambda l, i: (jnp.where(l == 0, 0, i), 0))
    shp = jax.ShapeDtypeStruct(w.shape, F32)
    return pl.pallas_call(body, name=name, grid=(2, nr), in_specs=[blk, g0_blk, g1_blk, blk, blk], out_specs=(blk,) * 4,
                          out_shape=(shp,) * 4, compiler_params=_params(2))(w, gs[0], gs[1], m, v)


class _NoComm:
    def __init__(self):
        self.grads = {}

    def prefetch(self, group, ws, carry):
        return ws, carry

    def need(self, group, ws, after):
        return ws

    def reduce(self, group, grads, carry):
        self.grads.update(grads)
        return carry

    def tick(self, carry):
        return carry


def _local_step(x, target, ws, norms, small, hooks):
    lay = _layout()

    w_main, w_a = _unpack_gin(ws["gin"])
    hn0 = _rms_fwd("rms_attn0", x, norms["attn0"])
    proj = _mm_plain("gla_proj", hn0, w_main, NN, F32)
    a = _mm_plain("gla_proj_a", hn0, w_a, NN, BF16)
    ga, cum = _gla_gate_fwd(a, small["w_a2p"], small["b_a2"])
    ws, cum = hooks.prefetch("B0", ws, cum)
    o_gla, states = _gla_fwd(proj, cum)
    gated = _gla_out_fwd(o_gla, proj, small["head_norm"])
    ws = hooks.need("B0", ws, gated)
    ws, gated = hooks.prefetch("B1", ws, gated)
    h1 = _mm_act_wr("gla_out", gated, ws["gout"], lay["gout"], add=x)
    ws = hooks.need("B1", ws, h1)

    def ffn_fwd(l, h, own=None, prefetch=None):
        nonlocal ws
        hn = _rms_fwd(f"rms_ffn{l}", h, norms[f"ffn{l}"])
        if own is not None:
            ws, hn = hooks.prefetch(own, ws, hn)
        up = _mm_act_wc(f"ffn_up{l}", hn, ws[f"up{l}"], lay[f"up{l}"], BF16)
        act = _glu_fwd(f"glu_fwd{l}", up, small["conv_w"][l], small["conv_b"][l])
        if own is not None:
            ws = hooks.need(own, ws, act)
        if prefetch is not None:
            ws, act = hooks.prefetch(prefetch, ws, act)
        return hn, up, act, _mm_act_wr(f"ffn_down{l}", act, ws[f"down{l}"], lay[f"down{l}"], add=h)

    hnf0, up0, act0, h2 = ffn_fwd(0, h1, own="B2", prefetch="C1")

    ws = hooks.need("C1", ws, h2)
    kvn = _rms_fwd("rms_kv", h2, norms["kv"])
    kv = _mm_act_wc("kv_proj", kvn, ws["wkv"], lay["wkv"], BF16)
    hn1 = _rms_fwd("rms_attn1", h2, norms["attn1"])
    q_all = _mm_act_wc("q_proj", hn1, ws["wq"], lay["wq"], BF16)
    views = [_att_views(q_all, kv, g) for g in range(3)]
    branch = [_att_fwd(views[g], g) for g in range(3)]
    ws, lse2 = hooks.prefetch("C2", ws, branch[-1][1])
    o_att, o_att_b, lse = _att_merge([br[0] for br in branch], [br[1] for br in branch[:-1]] + [lse2])
    h3 = _mm_act_wr("att_out", o_att_b, ws["dout"], lay["dout"], add=h2)
    ws = hooks.need("C2", ws, h3)
    hnf1, up1, act1, h4 = ffn_fwd(1, h3)

    dh4, d_final, loss = _loss_head(h4, norms["final"], target)

    sm = {"final": d_final}

    def ffn_bwd(l, dh, h, hn, up, act):
        big = {}
        dact = _mm_dact_wrT(f"ffn_down_dx{l}", dh, ws[f"down{l}"], lay[f"down{l}"])
        big[f"down{l}"] = _mm_grad_wr(f"ffn_down_dw{l}", act, dh, lay[f"down{l}"])
        du, dgc, w0, w1, w2, db = _glu_bwd_a(f"glu_bwd_a{l}", dact, up, small["conv_w"][l], small["conv_b"][l])
        sm[f"conv_w{l}"] = (w0, w1, w2)
        sm[f"conv_b{l}"] = db
        dup = hooks.tick(_glu_bwd_b(f"glu_bwd_b{l}", du, dgc, small["conv_w"][l]))
        dhn = _mm_dact_wcT(f"ffn_up_dx{l}", dup, ws[f"up{l}"], lay[f"up{l}"])
        big[f"up{l}"] = _mm_grad_wc(f"ffn_up_dw{l}", hn, dup, lay[f"up{l}"])
        dh_in, (sm[f"ffn{l}"],) = _rms_bwd(f"rms_ffn_bwd{l}", [dhn], h, [norms[f"ffn{l}"]], dh)
        return hooks.reduce(f"ffn{l}", big, dh_in)

    dh3 = ffn_bwd(1, dh4, h3, hnf1, up1, act1)

    big = {}
    do_att = _mm_dact_wrT("att_out_dx", dh3, ws["dout"], lay["dout"])
    big["dout"] = _mm_grad_wr("att_out_dw", o_att_b, dh3, lay["dout"])
    delta = _att_delta(do_att, o_att)
    bw = [_att_bwd(views[g], delta, lse, do_att, g) for g in range(3)]
    dq_all = jnp.concatenate([t[0] for t in bw], axis=1)
    dhn1 = _mm_dact_wcT("q_proj_dx", dq_all, ws["wq"], lay["wq"])
    big["wq"] = _mm_grad_wc("q_proj_dw", hn1, dq_all, lay["wq"])
    dkv = hooks.tick(_kv_grad_sum([t[1] for t in bw], [t[2] for t in bw]))
    dkvn = _mm_dact_wcT("kv_proj_dx", dkv, ws["wkv"], lay["wkv"])
    big["wkv"] = _mm_grad_wc("kv_proj_dw", kvn, dkv, lay["wkv"])
    dh2, (sm["attn1"], sm["kv"]) = _rms_bwd("rms_h2_bwd", [dhn1, dkvn], h2, [norms["attn1"], norms["kv"]], dh3)
    dh2 = hooks.reduce("att", big, dh2)

    dh1 = ffn_bwd(0, dh2, h1, hnf0, up0, act0)

    big = {}
    dgated = _mm_dact_wrT("gla_out_dx", dh1, ws["gout"], lay["gout"])
    big["gout"] = _mm_grad_wr("gla_out_dw", gated, dh1, lay["gout"])
    do_gla, dr, sm["head_norm"] = _gla_out_bwd(dgated, o_gla, proj, small["head_norm"])
    dq, dk, dv, dcum = _gla_bwd(proj, cum, states, hooks.tick(do_gla))
    da, sm["w_a2p"], sm["b_a2"] = _gla_gate_bwd(dcum, ga, a, small["w_a2p"])
    dproj = jnp.concatenate([dq, dk, dv, dr], axis=1)
    dhn0 = _mm_plain("gla_proj_dx", dproj, w_main, NT, F32)
    dhn0 = _mm_plain("gla_proj_a_dx", da, w_a, NT, F32, add=dhn0)
    gin_main = _mm_plain("gla_proj_dw", hn0, dproj, TN, BF16)
    gin_a = _mm_plain("gla_proj_a_dw", hn0, da, TN, BF16)
    big["gin"] = _pack_gin_grad(gin_main, gin_a)
    grad_x, (sm["attn0"],) = _rms_bwd("rms_attn0_bwd", [dhn0], x, [norms["attn0"]], dh1)
    return loss, grad_x, sm, big


def _pack_weights(chip, names, gla_w_in, gla_w_out, w_kv, dsa_w_q, dsa_w_out, ffn_w_up, ffn_w_down):
    gin = gla_w_in[0]
    gin = jnp.pad(gin, ((0, 0), (0, _roundup(gin.shape[1], LANE) - gin.shape[1])))
    shards = {"gin": gin, "gout": gla_w_out[0], "up0": ffn_w_up[0], "up1": ffn_w_up[1], "down0": ffn_w_down[0],
              "down1": ffn_w_down[1], "wq": dsa_w_q[0], "wkv": w_kv, "dout": dsa_w_out[0]}
    out = {}
    for name in names:
        w = shards[name]
        buf = jnp.zeros((N_CHIPS,) + w.shape, BF16)
        out[name] = lax.dynamic_update_slice(buf, w.astype(BF16)[None], (chip, 0, 0))
    return out


def _unpack_gin(w_gin):
    w = _layout()["gin"][1]
    d = w_gin.shape[1]
    wp = w_gin.shape[2]
    n_main = 2 * GLA_KEY_DIM + 2 * GLA_VAL_DIM
    tm = _tile(d, (256, 128, 64, 32, 16))

    def body(s_ref, main_ref, a_ref):
        full = jnp.concatenate([s_ref[s][:, :w] for s in range(N_CHIPS)], axis=1)
        main_ref[...] = full[:, :n_main]
        a_ref[...] = jnp.concatenate([full[:, n_main:], jnp.zeros((tm, A_PAD - GATE_RANK), full.dtype)], axis=1)

    return pl.pallas_call(
        body, name="unpack_gin", grid=(d // tm,), in_specs=[pl.BlockSpec((N_CHIPS, tm, wp), lambda i: (0, i, 0))],
        out_specs=(pl.BlockSpec((tm, n_main), lambda i: (i, 0)), pl.BlockSpec((tm, A_PAD), lambda i: (i, 0))),
        out_shape=(jax.ShapeDtypeStruct((d, n_main), w_gin.dtype), jax.ShapeDtypeStruct((d, A_PAD), w_gin.dtype)),
        compiler_params=_params(1))(w_gin)


def _pack_gin_grad(gin_main, gin_a):
    w = _layout()["gin"][1]
    wp = _roundup(w, LANE)
    d, n_main = gin_main.shape
    tm = _tile(d, (256, 128, 64, 32, 16))

    def body(main_ref, a_ref, o_ref):
        full = jnp.concatenate([main_ref[...], a_ref[:, :GATE_RANK]], axis=1)
        fill = jnp.zeros((tm, wp - w), full.dtype)
        for s in range(N_CHIPS):
            o_ref[s] = jnp.concatenate([full[:, s * w:(s + 1) * w], fill], axis=1)

    return pl.pallas_call(
        body, name="pack_gin_grad", grid=(d // tm,),
        in_specs=[pl.BlockSpec((tm, n_main), lambda i: (i, 0)), pl.BlockSpec((tm, A_PAD), lambda i: (i, 0))],
        out_specs=pl.BlockSpec((N_CHIPS, tm, wp), lambda i: (0, i, 0)),
        out_shape=jax.ShapeDtypeStruct((N_CHIPS, d, wp), gin_main.dtype), compiler_params=_params(1))(gin_main, gin_a)


def _small_params(attn_norm, ffn_norm, kv_norm, final_norm, conv_b, w_a2, b_a2, head_norm, conv_w):
    norms = {"attn0": attn_norm[0:1], "attn1": attn_norm[1:2], "ffn0": ffn_norm[0:1], "ffn1": ffn_norm[1:2],
             "kv": kv_norm[None, :], "final": final_norm[None, :]}
    small = {"w_a2p": jnp.pad(w_a2, ((0, A_PAD - GATE_RANK), (0, 0))), "b_a2": b_a2[None, :],
             "head_norm": head_norm[None, :], "conv_w": jnp.pad(conv_w, ((0, 0), (0, 8 - conv_w.shape[1]), (0, 0))),
             "conv_b": conv_b[:, None, :]}
    return norms, small


ANY = pl.BlockSpec(memory_space=pl.ANY)


def _place():
    return lax.axis_index("x"), lax.axis_index("y"), lax.axis_index("c")


def _other_chips(x, y):
    return [(1 - x, y), (x, 1 - y), (1 - x, 1 - y)]


def _rcopy(src, dst, ssem, rsem, dev):
    return pltpu.make_async_remote_copy(src_ref=src, dst_ref=dst, send_sem=ssem, recv_sem=rsem, device_id=dev,
                                        device_id_type=MESH)


def _pack_shard(name, w, layer, chip_arr, after):
    rows, cols = w.shape[-2:]
    tr = _tile(rows, (512, 352, 256, 128, 64, 32, 16))

    def body(p_ref, w_ref, after_ref, o_ref):
        o_ref[...] = w_ref[...].astype(o_ref.dtype)

    if w.ndim == 3:
        w_spec = pl.BlockSpec((None, tr, cols), lambda i, p: (layer, i, 0))
    else:
        w_spec = pl.BlockSpec((tr, cols), lambda i, p: (i, 0))
    return pl.pallas_call(
        body, name=name,
        grid_spec=pltpu.PrefetchScalarGridSpec(
            num_scalar_prefetch=1, grid=(rows // tr,), in_specs=[w_spec, ANY],
            out_specs=pl.BlockSpec((None, tr, cols), lambda i, p: (p[0], i, 0))),
        out_shape=jax.ShapeDtypeStruct((N_CHIPS, rows, cols), BF16), compiler_params=_params(1))(chip_arr, w, after)


def _swap_halves(name, arrs):
    n = len(arrs)

    def body(*refs):
        ins, outs = refs[:n], refs[n:2 * n]
        send, recv = refs[2 * n:]
        x, y, c = _place()
        cps = []
        for a in range(n):
            h = ins[a].shape[1] // 2
            cp = _rcopy(ins[a].at[:, pl.ds((1 - c) * h, h)], outs[a], send.at[a], recv.at[a], (x, y, 1 - c))
            cp.start()
            cps.append(cp)
        for cp in cps:
            cp.wait()

    return pl.pallas_call(
        body, name=name, in_specs=[ANY] * n, out_specs=[ANY] * n,
        out_shape=[jax.ShapeDtypeStruct((a.shape[0], a.shape[1] // 2, a.shape[2]), a.dtype) for a in arrs],
        scratch_shapes=[pltpu.SemaphoreType.DMA((n,)), pltpu.SemaphoreType.DMA((n,))])(*arrs)


SEM = pl.BlockSpec(memory_space=pltpu.SEMAPHORE)
EFFECT = pltpu.SideEffectType.DATAFLOW_SIDE_EFFECTING


def _shapes(arrs):
    return [jax.ShapeDtypeStruct(a.shape, a.dtype) for a in arrs]


def _gather_start(name, thru, arrs):
    n, nt = len(arrs), len(thru)

    def body(*refs):
        ins = refs[nt:nt + n]
        send, recv = refs[nt + n], refs[nt + n + 1]
        outs = refs[2 * nt + n + 2:]
        x, y, c = _place()
        me = 2 * x + y
        for a in range(n):
            h = ins[a].shape[1] // 2
            mine = pl.ds(c * h, h)
            for j, (px, py) in enumerate(_other_chips(x, y)):
                _rcopy(ins[a].at[me, mine], outs[a].at[me, mine], send.at[3 * a + j], recv.at[3 * a + j], (px, py, c)).start()

    res = pl.pallas_call(
        body, name=name, in_specs=[ANY] * (nt + n), out_specs=[SEM, SEM] + [ANY] * (nt + n),
        out_shape=[pltpu.SemaphoreType.DMA((3 * n,)), pltpu.SemaphoreType.DMA((3 * n,))] + _shapes(thru) + _shapes(arrs),
        input_output_aliases={i: 2 + i for i in range(nt + n)},
        compiler_params=pltpu.CompilerParams(has_side_effects=EFFECT))(*thru, *arrs)
    return res[0], res[1], res[2:2 + nt], res[2 + nt:]


def _gather_wait(name, send, recv, arrs, after):
    n = len(arrs)

    def body(*refs):
        ins = refs[:n]
        send_ref, recv_ref = refs[n], refs[n + 1]
        x, y, c = _place()
        me = 2 * x + y
        for a in range(n):
            h = ins[a].shape[1] // 2
            mine = pl.ds(c * h, h)
            for j, (px, py) in enumerate(_other_chips(x, y)):
                sent = ins[a].at[me, mine]
                landed = ins[a].at[2 * px + py, mine]
                cp = _rcopy(sent, landed, send_ref.at[3 * a + j], recv_ref.at[3 * a + j], (px, py, c))
                cp.wait_send()
                cp.wait_recv()

    after = list(after) if isinstance(after, (list, tuple)) else [after]
    return pl.pallas_call(
        body, name=name, in_specs=[ANY] * n + [SEM, SEM] + [ANY] * len(after), out_specs=[ANY] * n,
        out_shape=_shapes(arrs), input_output_aliases={a: a for a in range(n)},
        compiler_params=pltpu.CompilerParams(has_side_effects=EFFECT))(*arrs, send, recv, *after)


def _forward_halves(name, arrs):
    n = len(arrs)

    def body(*refs):
        ins, outs = refs[:n], refs[n:2 * n]
        send, recv = refs[2 * n:]
        x, y, c = _place()
        sib = (x, y, 1 - c)
        chips = _other_chips(x, y)
        cps = []
        for a in range(n):
            h = ins[a].shape[1] // 2
            mine = pl.ds(c * h, h)
            for j, (px, py) in enumerate(chips):
                cp = _rcopy(ins[a].at[2 * px + py, mine], outs[a].at[2 * px + py, mine], send.at[3 * a + j],
                            recv.at[3 * a + j], sib)
                cp.start()
                cps.append(cp)
        for a in range(n):
            h = ins[a].shape[1] // 2
            theirs = pl.ds((1 - c) * h, h)
            for j, (px, py) in enumerate(chips):
                got = outs[a].at[2 * px + py, theirs]
                _rcopy(got, got, send.at[3 * a + j], recv.at[3 * a + j], sib).wait_recv()
        for cp in cps:
            cp.wait_send()

    return pl.pallas_call(
        body, name=name, in_specs=[ANY] * n, out_specs=[ANY] * n, out_shape=_shapes(arrs),
        input_output_aliases={a: a for a in range(n)},
        scratch_shapes=[pltpu.SemaphoreType.DMA((3 * n,)), pltpu.SemaphoreType.DMA((3 * n,))])(*arrs)


def _forward_start(name, thru, arrs):
    n, nt = len(arrs), len(thru)

    def body(*refs):
        ins = refs[nt:nt + n]
        send, recv = refs[nt + n], refs[nt + n + 1]
        outs = refs[2 * nt + n + 2:]
        x, y, c = _place()
        for a in range(n):
            h = ins[a].shape[1] // 2
            mine = pl.ds(c * h, h)
            for j, (px, py) in enumerate(_other_chips(x, y)):
                _rcopy(ins[a].at[2 * px + py, mine], outs[a].at[2 * px + py, mine], send.at[3 * a + j], recv.at[3 * a + j],
                       (x, y, 1 - c)).start()

    res = pl.pallas_call(
        body, name=name, in_specs=[ANY] * (nt + n), out_specs=[SEM, SEM] + [ANY] * (nt + n),
        out_shape=[pltpu.SemaphoreType.DMA((3 * n,)), pltpu.SemaphoreType.DMA((3 * n,))] + _shapes(thru) + _shapes(arrs),
        input_output_aliases={i: 2 + i for i in range(nt + n)},
        compiler_params=pltpu.CompilerParams(has_side_effects=EFFECT))(*thru, *arrs)
    return res[0], res[1], res[2:2 + nt], res[2 + nt:]


def _forward_wait(name, send, recv, arrs, after):
    n = len(arrs)

    def body(*refs):
        ins = refs[:n]
        send_ref, recv_ref = refs[n], refs[n + 1]
        x, y, c = _place()
        for a in range(n):
            h = ins[a].shape[1] // 2
            for j, (px, py) in enumerate(_other_chips(x, y)):
                sent = ins[a].at[2 * px + py, pl.ds(c * h, h)]
                got = ins[a].at[2 * px + py, pl.ds((1 - c) * h, h)]
                cp = _rcopy(sent, got, send_ref.at[3 * a + j], recv_ref.at[3 * a + j], (x, y, 1 - c))
                cp.wait_send()
                cp.wait_recv()

    return pl.pallas_call(
        body, name=name, in_specs=[ANY] * n + [SEM, SEM, ANY], out_specs=[ANY] * n, out_shape=_shapes(arrs),
        input_output_aliases={a: a for a in range(n)},
        compiler_params=pltpu.CompilerParams(has_side_effects=EFFECT))(*arrs, send, recv, after)


def _scatter_start(name, thru, arrs):
    n, nt = len(arrs), len(thru)
    landing = [lax.empty(a.shape, a.dtype) for a in arrs]

    def body(*refs):
        ins = refs[nt:nt + n]
        send, recv = refs[nt + 2 * n], refs[nt + 2 * n + 1]
        outs = refs[2 * nt + 3 * n + 2:]
        x, y, c = _place()
        me = 2 * x + y
        for a in range(n):
            for j, (px, py) in enumerate(_other_chips(x, y)):
                _rcopy(ins[a].at[2 * px + py], outs[a].at[me], send.at[3 * a + j], recv.at[3 * a + j], (px, py, c)).start()

    res = pl.pallas_call(
        body, name=name, in_specs=[ANY] * (nt + 2 * n), out_specs=[SEM, SEM] + [ANY] * (nt + 2 * n),
        out_shape=[pltpu.SemaphoreType.DMA((3 * n,)), pltpu.SemaphoreType.DMA((3 * n,))] + _shapes(thru) + _shapes(arrs)
        + _shapes(landing),
        input_output_aliases={i: 2 + i for i in range(nt + 2 * n)},
        compiler_params=pltpu.CompilerParams(has_side_effects=EFFECT))(*thru, *arrs, *landing)
    return res[0], res[1], res[2:2 + nt], res[2 + nt:2 + nt + n], res[2 + nt + n:]


def _scatter_wait(name, send, recv, arrs, landing, after):
    n = len(arrs)

    def body(*refs):
        ins, land = refs[:n], refs[n:2 * n]
        send_ref, recv_ref = refs[2 * n], refs[2 * n + 1]
        x, y, c = _place()
        for a in range(n):
            for j, (px, py) in enumerate(_other_chips(x, y)):
                cp = _rcopy(ins[a].at[2 * px + py], land[a].at[2 * px + py], send_ref.at[3 * a + j], recv_ref.at[3 * a + j],
                            (px, py, c))
                cp.wait_send()
                cp.wait_recv()

    res = pl.pallas_call(
        body, name=name, in_specs=[ANY] * (2 * n) + [SEM, SEM, ANY], out_specs=[ANY] * (2 * n),
        out_shape=_shapes(arrs) + _shapes(landing), input_output_aliases={i: i for i in range(2 * n)},
        compiler_params=pltpu.CompilerParams(has_side_effects=EFFECT))(*arrs, *landing, send, recv, after)
    return res[:n], res[n:]


def _swap_start(name, thru, arrs):
    n, nt = len(arrs), len(thru)
    landing = [lax.empty((a.shape[0], a.shape[1] // 2, a.shape[2]), a.dtype) for a in arrs]

    def body(*refs):
        ins = refs[nt:nt + n]
        send, recv = refs[nt + 2 * n], refs[nt + 2 * n + 1]
        outs = refs[2 * nt + 3 * n + 2:]
        x, y, c = _place()
        for a in range(n):
            h = ins[a].shape[1] // 2
            _rcopy(ins[a].at[:, pl.ds((1 - c) * h, h)], outs[a], send.at[a], recv.at[a], (x, y, 1 - c)).start()

    res = pl.pallas_call(
        body, name=name, in_specs=[ANY] * (nt + 2 * n), out_specs=[SEM, SEM] + [ANY] * (nt + 2 * n),
        out_shape=[pltpu.SemaphoreType.DMA((n,)), pltpu.SemaphoreType.DMA((n,))] + _shapes(thru) + _shapes(arrs)
        + _shapes(landing),
        input_output_aliases={i: 2 + i for i in range(nt + 2 * n)},
        compiler_params=pltpu.CompilerParams(has_side_effects=EFFECT))(*thru, *arrs, *landing)
    return res[0], res[1], res[2:2 + nt], res[2 + nt:2 + nt + n], res[2 + nt + n:]


def _swap_wait(name, send, recv, arrs, landing, after):
    n = len(arrs)

    def body(*refs):
        ins, land = refs[:n], refs[n:2 * n]
        send_ref, recv_ref = refs[2 * n], refs[2 * n + 1]
        x, y, c = _place()
        for a in range(n):
            h = ins[a].shape[1] // 2
            cp = _rcopy(ins[a].at[:, pl.ds((1 - c) * h, h)], land[a], send_ref.at[a], recv_ref.at[a], (x, y, 1 - c))
            cp.wait_send()
            cp.wait_recv()

    res = pl.pallas_call(
        body, name=name, in_specs=[ANY] * (2 * n) + [SEM, SEM, ANY], out_specs=[ANY] * (2 * n),
        out_shape=_shapes(arrs) + _shapes(landing), input_output_aliases={i: i for i in range(2 * n)},
        compiler_params=pltpu.CompilerParams(has_side_effects=EFFECT))(*arrs, *landing, send, recv, after)
    return res[:n], res[n:]


def _join_start(name, arrs):
    n = len(arrs)

    def body(*refs):
        ins = refs[:n]
        send, recv = refs[n], refs[n + 1]
        outs = refs[n + 2:]
        x, y, c = _place()
        for a in range(n):
            h = ins[a].shape[0] // 2
            mine = pl.ds(c * h, h)
            _rcopy(ins[a].at[mine], outs[a].at[mine], send.at[a], recv.at[a], (x, y, 1 - c)).start()

    res = pl.pallas_call(
        body, name=name, in_specs=[ANY] * n, out_specs=[SEM, SEM] + [ANY] * n,
        out_shape=[pltpu.SemaphoreType.DMA((n,)), pltpu.SemaphoreType.DMA((n,))] + _shapes(arrs),
        input_output_aliases={i: 2 + i for i in range(n)},
        compiler_params=pltpu.CompilerParams(has_side_effects=EFFECT))(*arrs)
    return res[0], res[1], res[2:]


def _join_wait(name, send, recv, arrs, after):
    n = len(arrs)

    def body(*refs):
        ins = refs[:n]
        send_ref, recv_ref = refs[n], refs[n + 1]
        x, y, c = _place()
        for a in range(n):
            h = ins[a].shape[0] // 2
            cp = _rcopy(ins[a].at[pl.ds(c * h, h)], ins[a].at[pl.ds((1 - c) * h, h)], send_ref.at[a], recv_ref.at[a],
                        (x, y, 1 - c))
            cp.wait_send()
            cp.wait_recv()

    return pl.pallas_call(
        body, name=name, in_specs=[ANY] * n + [SEM, SEM, ANY], out_specs=[ANY] * n, out_shape=_shapes(arrs),
        input_output_aliases={a: a for a in range(n)},
        compiler_params=pltpu.CompilerParams(has_side_effects=EFFECT))(*arrs, send, recv, after)


def _join_halves(name, arrs):
    n = len(arrs)

    def body(*refs):
        ins, outs = refs[:n], refs[n:2 * n]
        send, recv = refs[2 * n:]
        x, y, c = _place()
        cps = []
        for a in range(n):
            h = ins[a].shape[0] // 2
            mine = pl.ds(c * h, h)
            cp = _rcopy(ins[a].at[mine], outs[a].at[mine], send.at[a], recv.at[a], (x, y, 1 - c))
            cp.start()
            cps.append(cp)
        for a in range(n):
            h = ins[a].shape[0] // 2
            got = outs[a].at[pl.ds((1 - c) * h, h)]
            _rcopy(got, got, send.at[a], recv.at[a], (x, y, 1 - c)).wait_recv()
        for cp in cps:
            cp.wait_send()

    return pl.pallas_call(
        body, name=name, in_specs=[ANY] * n, out_specs=[ANY] * n,
        out_shape=[jax.ShapeDtypeStruct(a.shape, a.dtype) for a in arrs],
        input_output_aliases={a: a for a in range(n)},
        scratch_shapes=[pltpu.SemaphoreType.DMA((n,)), pltpu.SemaphoreType.DMA((n,))])(*arrs)


def _allgather8(name, xs, reduce):
    m_per, n = xs.shape

    def body(x_ref, out_ref, *rest):
        if reduce:
            sum_ref, send, recv, lsem = rest
        else:
            send, recv, lsem = rest
        x, y, c = _place()
        me, sib = (x, y, c), (x, y, 1 - c)
        chips = _other_chips(x, y)

        def rows(px, py, pc):
            return out_ref.at[pl.ds((4 * px + 2 * py + pc) * m_per, m_per), :]

        def copy(k, block, to, src=None):
            return _rcopy(rows(*block) if src is None else src, rows(*block), send.at[k], recv.at[k], to)

        mine = pltpu.make_async_copy(x_ref, rows(*me), lsem)
        mine.start()
        first = [copy(0, me, sib, src=x_ref)]
        first += [copy(1 + j, me, (*chip, c), src=x_ref) for j, chip in enumerate(chips)]
        for cp in first:
            cp.start()
        passed = [copy(4 + j, (*chip, c), sib) for j, chip in enumerate(chips)]
        for j, chip in enumerate(chips):
            copy(1 + j, (*chip, c), me).wait_recv()
            passed[j].start()
        copy(0, sib, me).wait_recv()
        for j, chip in enumerate(chips):
            copy(4 + j, (*chip, 1 - c), me).wait_recv()
        for cp in first + passed:
            cp.wait_send()
        mine.wait()
        if reduce:
            acc = out_ref[pl.ds(0, m_per), :]
            for dev in range(1, N_DEV):
                acc = acc + out_ref[pl.ds(dev * m_per, m_per), :]
            sum_ref[...] = acc

    vm = pl.BlockSpec(memory_space=pltpu.VMEM)
    out_shape = [jax.ShapeDtypeStruct((N_DEV * m_per, n), xs.dtype)]
    if reduce:
        out_shape.append(jax.ShapeDtypeStruct((m_per, n), xs.dtype))
    return pl.pallas_call(
        body, name=name, in_specs=[vm], out_specs=[vm] * len(out_shape), out_shape=out_shape,
        scratch_shapes=[pltpu.SemaphoreType.DMA((7,)), pltpu.SemaphoreType.DMA((7,)), pltpu.SemaphoreType.DMA],
        compiler_params=pltpu.CompilerParams(vmem_limit_bytes=VMEM_LIMIT))(xs)


def _add_my_half(name, a, rb, c_arr):
    s, h, cols = rb.shape
    tr = _tile(h, (512, 352, 256, 128, 64, 32, 16))
    nt = h // tr

    def body(c_ref, a_ref, b_ref, o_ref):
        o_ref[...] = (a_ref[...].astype(F32) + b_ref[...].astype(F32)).astype(o_ref.dtype)

    return pl.pallas_call(
        body, name=name,
        grid_spec=pltpu.PrefetchScalarGridSpec(
            num_scalar_prefetch=1, grid=(s, nt),
            in_specs=[pl.BlockSpec((None, tr, cols), lambda k, i, c: (k, c[0] * nt + i, 0)),
                      pl.BlockSpec((None, tr, cols), lambda k, i, c: (k, i, 0))],
            out_specs=pl.BlockSpec((None, tr, cols), lambda k, i, c: (k, i, 0))),
        out_shape=jax.ShapeDtypeStruct(rb.shape, BF16), compiler_params=_params(2))(c_arr, a, rb)


def _sum_chips(name, own, q, place):
    s, h, cols = q.shape
    tr = _tile(h, (512, 352, 256, 128, 64, 32, 16))
    nt = h // tr

    def body(p_ref, own_ref, q1_ref, q2_ref, q3_ref, o_ref):
        acc = own_ref[...].astype(F32) + q1_ref[...].astype(F32)
        o_ref[...] = acc + q2_ref[...].astype(F32) + q3_ref[...].astype(F32)

    def slab(t):
        return pl.BlockSpec((None, tr, cols), lambda i, p: ((p[0] + t) % s, i, 0))

    return pl.pallas_call(
        body, name=name,
        grid_spec=pltpu.PrefetchScalarGridSpec(
            num_scalar_prefetch=1, grid=(nt,), in_specs=[slab(0), slab(1), slab(2), slab(3)],
            out_specs=pl.BlockSpec((tr, cols), lambda i, p: (p[1] * nt + i, 0))),
        out_shape=jax.ShapeDtypeStruct((2 * h, cols), F32), compiler_params=_params(1))(place, own, q, q, q)


def _pack_rows(parts):
    rows = []
    for p in parts:
        flat = p.reshape(-1).astype(F32)
        n = _roundup(flat.shape[0], 8 * LANE)
        rows.append(jnp.pad(flat, (0, n - flat.shape[0])).reshape(-1, LANE))
    return jnp.concatenate(rows, axis=0)


def _unpack_rows(buf, shapes):
    out, r = [], 0
    for shp in shapes:
        size = math.prod(shp)
        nr = _roundup(size, 8 * LANE) // LANE
        out.append(buf[r:r + nr].reshape(-1)[:size].reshape(shp))
        r += nr
    return out


def kernel(x, attn_norm, gla_w_in, gla_w_a2, gla_b_a2, gla_head_norm, gla_w_out, kv_norm, w_kv, dsa_w_q, dsa_w_out, ffn_norm, ffn_w_up, ffn_conv_w, ffn_conv_b, ffn_w_down, final_norm, loss_target, m_attn_norm, m_gla_w_in, m_gla_w_a2, m_gla_b_a2, m_gla_head_norm, m_gla_w_out, m_kv_norm, m_w_kv, m_dsa_w_q, m_dsa_w_out, m_ffn_norm, m_ffn_w_up, m_ffn_conv_w, m_ffn_conv_b, m_ffn_w_down, m_final_norm, v_attn_norm, v_gla_w_in, v_gla_w_a2, v_gla_b_a2, v_gla_head_norm, v_gla_w_out, v_kv_norm, v_w_kv, v_dsa_w_q, v_dsa_w_out, v_ffn_norm, v_ffn_w_up, v_ffn_conv_w, v_ffn_conv_b, v_ffn_w_down, v_final_norm):
    lay = _layout()
    d, f = D_MODEL, D_FF
    cx, cy, cc = _place()
    chip = 2 * cx + cy
    c_arr = jnp.reshape(cc, (1,)).astype(jnp.int32)
    place = jnp.stack([chip, cc]).astype(jnp.int32)

    groups = {"A": ("gin", "small"), "B0": ("gout",), "B1": ("up0",), "B2": ("down0",), "C1": ("wkv", "wq", "dout"),
              "C2": ("up1", "down1")}
    big_shards = (gla_w_in, gla_w_out, w_kv, dsa_w_q, dsa_w_out, ffn_w_up, ffn_w_down)
    ws = _pack_weights(chip, groups["A"][:1], *big_shards)
    sharded_small = [gla_w_a2[0], gla_b_a2[0], gla_head_norm[0], ffn_conv_w]
    packed = _pack_rows(sharded_small)
    packed = jnp.pad(packed, ((0, _roundup(packed.shape[0], 16) - packed.shape[0]), (0, 0)))
    ws["small"] = lax.dynamic_update_slice(jnp.zeros((N_CHIPS,) + packed.shape, F32), packed[None], (chip, 0, 0))
    send, recv, _, arrs = _gather_start("gather_a_start", [], [ws[k] for k in groups["A"]])
    chip_arr = place[:1]
    sources = {"up0": (ffn_w_up, 0), "up1": (ffn_w_up, 1), "down0": (ffn_w_down, 0), "down1": (ffn_w_down, 1),
               "wq": (dsa_w_q, 0), "wkv": (w_kv, 0), "dout": (dsa_w_out, 0), "gout": (gla_w_out, 0)}
    later = ("B0", "B1", "B2", "C1", "C2")
    for k in sum((groups[grp] for grp in later), ()):
        ws[k] = _pack_shard(f"pack_{k}", *sources[k], chip_arr, arrs[1])
    moments = [t.reshape(-1, t.shape[-1]) for t in (m_gla_w_in, v_gla_w_in)]
    arrs = _gather_wait("gather_a_wait", send, recv, arrs, [ws["dout"]] + moments)
    ws.update(zip(groups["A"], _forward_halves("forward_a", arrs)))
    in_flight = {}
    thru = [ws[k] for k in groups["A"]]
    for grp in later:
        send, recv, thru, arrs = _gather_start(f"gather_{grp.lower()}_start", thru, [ws[k] for k in groups[grp]])
        ws.update(zip(groups[grp], arrs))
        in_flight[grp] = (send, recv)
    ws.update(zip(groups["A"], thru))
    pending = []

    class _Comm:
        def prefetch(self, grp, ws, carry):
            send, recv = in_flight[grp]
            arrs = _gather_wait(f"gather_{grp.lower()}_wait", send, recv, [ws[k] for k in groups[grp]], carry)
            send, recv, thru, arrs = _forward_start(f"forward_{grp.lower()}_start", [carry], arrs)
            in_flight[grp] = (send, recv)
            return {**ws, **dict(zip(groups[grp], arrs))}, thru[0]

        def need(self, grp, ws, after):
            send, recv = in_flight[grp]
            arrs = _forward_wait(f"forward_{grp.lower()}_wait", send, recv, [ws[k] for k in groups[grp]], after)
            return {**ws, **dict(zip(groups[grp], arrs))}

        swapping = None

        def reduce(self, grp, grads, carry):
            names = list(grads)
            send, recv, thru, parts, theirs = _swap_start(f"swap_{grp}_start", [carry], [grads[k] for k in names])
            self.swapping = (grp, names, send, recv, parts, theirs)
            return thru[0]

        def tick(self, carry):
            if self.swapping is None:
                return carry
            grp, names, send, recv, parts, theirs = self.swapping
            self.swapping = None
            parts, theirs = _swap_wait(f"swap_{grp}_wait", send, recv, parts, theirs, carry)
            return self.scatter(grp, names, parts, theirs, carry)

        def scatter(self, grp, names, parts, theirs, carry):
            sums = [_add_my_half(f"add_half_{k}", a, b, c_arr) for k, a, b in zip(names, parts, theirs)]
            send, recv, thru, sums, landing = _scatter_start(f"scatter_{grp}_start", [carry], sums)
            pending.append((grp, names, send, recv, sums, landing))
            return thru[0]

        def reduce_now(self, grp, grads, carry):
            names = list(grads)
            parts = [grads[k] for k in names]
            return self.scatter(grp, names, parts, _swap_halves(f"swap_{grp}", parts), carry)

    shards = [_unpack_rows(ws["small"][s], [p.shape for p in sharded_small]) for s in range(N_CHIPS)]
    w_a2, b_a2, head_norm, conv_w = [jnp.concatenate([shards[s][k] for s in range(N_CHIPS)], axis=-1) for k in range(4)]
    norms, small = _small_params(attn_norm, ffn_norm, kv_norm, final_norm, ffn_conv_b, w_a2, b_a2, head_norm, conv_w)

    comm = _Comm()
    loss_blk, grad_x, sm, last_big = _local_step(x[0], loss_target[0], ws, norms, small, comm)

    small_parts = [loss_blk, jnp.concatenate([sm["attn0"], sm["attn1"]]), jnp.concatenate([sm["ffn0"], sm["ffn1"]]),
                   sm["kv"], sm["final"], jnp.concatenate([sm["conv_b0"], sm["conv_b1"]]),
                   sm["w_a2p"][:GATE_RANK], sm["b_a2"], sm["head_norm"],
                   jnp.stack([jnp.concatenate(sm["conv_w0"]), jnp.concatenate(sm["conv_w1"])])]
    small_shapes = [(8, LANE), (2, d), (2, d), (d,), (d,), (2, f), (GATE_RANK, GLA_KEY_DIM), (GLA_KEY_DIM,),
                    (GLA_VAL_DIM // GLA_HEADS,), (2, 3, f)]
    _, reduced = _allgather8("reduce_small", _pack_rows(small_parts), True)
    reduced = comm.reduce_now("gla", last_big, reduced)

    loss_r, g_attn, g_ffn, g_kv, g_final, g_cb, g_a2, g_ba2, g_hn, g_cw = _unpack_rows(reduced, small_shapes)
    loss = loss_r[0, 0]

    def mine(g, axis):
        w = g.shape[axis] // N_CHIPS
        return lax.dynamic_slice_in_dim(g, chip * w, w, axis)

    grads = {
        "attn_norm": g_attn, "gla_w_a2": mine(g_a2, 1)[None], "gla_b_a2": mine(g_ba2, 0)[None],
        "gla_head_norm": mine(g_hn, 0)[None], "kv_norm": g_kv, "ffn_norm": g_ffn, "ffn_conv_w": mine(g_cw, 2),
        "ffn_conv_b": g_cb, "final_norm": g_final,
    }
    weights = {"attn_norm": (attn_norm, m_attn_norm, v_attn_norm), "gla_w_in": (gla_w_in, m_gla_w_in, v_gla_w_in),
               "gla_w_a2": (gla_w_a2, m_gla_w_a2, v_gla_w_a2), "gla_b_a2": (gla_b_a2, m_gla_b_a2, v_gla_b_a2),
               "gla_head_norm": (gla_head_norm, m_gla_head_norm, v_gla_head_norm),
               "gla_w_out": (gla_w_out, m_gla_w_out, v_gla_w_out), "kv_norm": (kv_norm, m_kv_norm, v_kv_norm),
               "w_kv": (w_kv, m_w_kv, v_w_kv), "dsa_w_q": (dsa_w_q, m_dsa_w_q, v_dsa_w_q),
               "dsa_w_out": (dsa_w_out, m_dsa_w_out, v_dsa_w_out), "ffn_norm": (ffn_norm, m_ffn_norm, v_ffn_norm),
               "ffn_w_up": (ffn_w_up, m_ffn_w_up, v_ffn_w_up), "ffn_conv_w": (ffn_conv_w, m_ffn_conv_w, v_ffn_conv_w),
               "ffn_conv_b": (ffn_conv_b, m_ffn_conv_b, v_ffn_conv_b),
               "ffn_w_down": (ffn_w_down, m_ffn_w_down, v_ffn_w_down), "final_norm": (final_norm, m_final_norm, v_final_norm)}
    order = list(weights)
    big_names = ("gla_w_in", "gla_w_out", "w_kv", "dsa_w_q", "dsa_w_out", "ffn_w_up", "ffn_w_down")
    delta, new_m, new_v = {}, {}, {}

    def adam_big(k, g):
        w, m, v = weights[k]
        cols = w.shape[-1]
        res = _adamw(f"adamw_{k}", w.reshape(-1, cols), g.reshape(-1, g.shape[-1]), m.reshape(-1, cols), v.reshape(-1, cols))
        delta[k], new_m[k], new_v[k] = [r.reshape(w.shape) for r in res[:3]]
        grads[k] = res[3].reshape(w.shape) if len(res) == 4 else g
        return res[0]

    full = {}
    after = reduced
    joining = []
    for grp, names, send, recv, sums, landing in pending[:-1]:
        sums, landing = _scatter_wait(f"scatter_{grp}_wait", send, recv, sums, landing, after)
        halves = [_sum_chips(f"sum_chips_{k}", s, q, place) for k, s, q in zip(names, sums, landing)]
        send, recv, halves = _join_start(f"join_{grp}_start", halves)
        joining.append((grp, names, send, recv, halves))
        after = halves[0]
    for grp, names, send, recv, halves in joining:
        joined = _join_wait(f"join_{grp}_wait", send, recv, halves, after)
        full.update(zip(names, joined))
        after = joined[0]
    after = adam_big("w_kv", full["wkv"])
    after = adam_big("dsa_w_q", full["wq"][None])
    after = adam_big("dsa_w_out", full["dout"][None])
    for k, g0, g1 in (("ffn_w_up", "up0", "up1"), ("ffn_w_down", "down0", "down1")):
        delta[k], new_m[k], new_v[k], grads[k] = _adamw_layers(f"adamw_{k}", weights[k][0], (full[g0], full[g1]),
                                                               weights[k][1], weights[k][2])
        after = delta[k]
    grp, names, send, recv, sums, landing = pending[-1]
    sums, landing = _scatter_wait(f"scatter_{grp}_wait", send, recv, sums, landing, after)
    halves = [_sum_chips(f"sum_chips_{k}", s, q, place) for k, s, q in zip(names, sums, landing)]
    full.update(zip(names, _join_halves(f"join_{grp}", halves)))
    adam_big("gla_w_in", full["gin"])
    adam_big("gla_w_out", full["gout"][None])
    small_names = [k for k in order if k not in big_names]
    packed = [_pack_rows([src[k] for k in small_names])
              for src in ({k: weights[k][0] for k in small_names}, grads, {k: weights[k][1] for k in small_names},
                          {k: weights[k][2] for k in small_names})]
    res = _adamw("adamw_small", *packed)
    shapes = [weights[k][0].shape for k in small_names]
    for dst, buf in zip((delta, new_m, new_v), res):
        for k, val in zip(small_names, _unpack_rows(buf, shapes)):
            dst[k] = val
    return (loss, grad_x[None], *[grads[k] for k in order], *[delta[k] for k in order], *[new_m[k] for k in order],
            *[new_v[k] for k in order])
```

```python
import math

import jax
import jax.numpy as jnp
from jax import lax
from jax.experimental import pallas as pl
from jax.experimental.pallas import tpu as pltpu

F32 = jnp.float32
BF16 = jnp.bfloat16

D_MODEL = 2048
SEQ = 4096
GLA_HEADS = 4
GLA_KEY_DIM = D_MODEL // 2
GLA_VAL_DIM = D_MODEL
GATE_RANK = 16
GATE_NORMALIZER = 16.0
GLA_CHUNK = 64
ATT_HEADS = 16
HEAD_DIM = 128
WINDOWS = (128, 512, 2048)
DILATIONS = (1, 4, 16)
ATT_BLOCK = 128
D_FF = 5632
EPS = 1e-6
ADAM_LR = 0.001
ADAM_B1 = 0.9
ADAM_B2 = 0.999
ADAM_EPS = 1e-08
ADAM_WD = 0.01
ADAM_STEP = 10

N_CHIPS = 4
N_DEV = 8
LANE = 128
A_PAD = 128
VMEM_LIMIT = 56 * 1024 * 1024
MAX_K_TILE = 2816
NEG = -1e30
MESH = pl.DeviceIdType.MESH

NN = (((1,), (0,)), ((), ()))
NT = (((1,), (1,)), ((), ()))
TN = (((0,), (0,)), ((), ()))


def _tile(n, cands):
    for c in cands:
        if c <= n and n % c == 0:
            return c
    return n


def _roundup(n, m):
    return -(-n // m) * m


def _params(n_axes):
    return pltpu.CompilerParams(dimension_semantics=("arbitrary",) * n_axes, vmem_limit_bytes=VMEM_LIMIT)


def _dot(a, b, dims):
    return lax.dot_general(a, b, dims, preferred_element_type=F32)


def _sigmoid(x):
    return 1.0 / (1.0 + jnp.exp(-x))


COL_SHARDED = ("gin", "up0", "up1", "wq", "wkv")
ROW_SHARDED = ("gout", "down0", "down1", "dout")


def _layout():
    f = D_FF
    hd = ATT_HEADS * HEAD_DIM
    gin = 2 * GLA_KEY_DIM + 2 * GLA_VAL_DIM + GATE_RANK
    up_w = 2 * f // N_CHIPS
    q_w = 3 * hd // N_CHIPS
    kv_w = 2 * hd // N_CHIPS
    dn_r = f // N_CHIPS
    go_r = GLA_VAL_DIM // N_CHIPS
    do_r = hd // N_CHIPS
    big = (1408, 1024, 512, 256, 128)
    return {
        "gin": (0, gin // N_CHIPS, LANE),
        "up0": (0, up_w, _tile(up_w, big)), "up1": (0, up_w, _tile(up_w, big)),
        "wq": (0, q_w, _tile(q_w, (1536, 768, 512, 384, 256, 128))), "wkv": (0, kv_w, _tile(kv_w, (1024, 512, 256, 128))),
        "down0": (0, dn_r, _tile(dn_r, big)), "down1": (0, dn_r, _tile(dn_r, big)),
        "gout": (0, go_r, _tile(go_r, (512, 256, 128))), "dout": (0, do_r, _tile(do_r, (512, 256, 128))),
    }


def _matmul(name, a, b, dims, grid, a_spec, b_spec, o_spec, out_shape, acc_shape, add=None, add_spec=None):
    nk = grid[2]
    has_add = add is not None

    def body(*refs):
        a_ref, b_ref = refs[0], refs[1]
        pos = 2
        add_ref = None
        if has_add:
            add_ref = refs[pos]
            pos += 1
        o_ref = refs[pos]
        prod = _dot(a_ref[...].astype(BF16), b_ref[...].astype(BF16), dims)

        def finish(val):
            if has_add:
                val = val + add_ref[...].astype(F32)
            o_ref[...] = val.astype(o_ref.dtype)

        if nk == 1:
            finish(prod)
        else:
            acc_ref = refs[pos + 1]
            k = pl.program_id(2)

            @pl.when(k == 0)
            def _():
                acc_ref[...] = prod

            @pl.when(k > 0)
            def _():
                acc_ref[...] += prod

            @pl.when(k == nk - 1)
            def _():
                finish(acc_ref[...])

    in_specs = [a_spec, b_spec]
    args = [a, b]
    if has_add:
        in_specs.append(add_spec)
        args.append(add)
    scratch = [] if nk == 1 else [pltpu.VMEM(acc_shape, F32)]
    return pl.pallas_call(body, name=name, grid=grid, in_specs=in_specs, out_specs=o_spec, out_shape=out_shape,
                          scratch_shapes=scratch, compiler_params=_params(3))(*args)


def _mm_act_wc(name, a, wc, seg, out_dtype):
    off, w, tn = seg
    if off == 0 and w <= MAX_K_TILE:
        tn = w
    t_len, d = a.shape
    tm = _tile(t_len, (1024, 512, 256, 128))
    nps = w // tn
    ob = off // tn
    grid = (t_len // tm, N_CHIPS * nps, 1)
    return _matmul(
        name, a, wc, NN, grid,
        pl.BlockSpec((tm, d), lambda i, j, k: (i, 0)),
        pl.BlockSpec((None, d, tn), lambda i, j, k: (j // nps, 0, ob + j % nps)),
        pl.BlockSpec((tm, tn), lambda i, j, k: (i, j)),
        jax.ShapeDtypeStruct((t_len, N_CHIPS * w), out_dtype), (tm, tn))


def _mm_dact_wcT(name, dy, wc, seg, add=None):
    off, w, tk = seg
    if off == 0 and w <= MAX_K_TILE:
        tk = w
    t_len = dy.shape[0]
    d = wc.shape[1]
    tm = _tile(t_len, (1024, 512, 256, 128))
    tn = _tile(d, (1024, 512, 256, 128))
    kps = w // tk
    ob = off // tk
    grid = (t_len // tm, d // tn, N_CHIPS * kps)
    return _matmul(
        name, dy, wc, NT, grid,
        pl.BlockSpec((tm, tk), lambda i, j, k: (i, k)),
        pl.BlockSpec((None, tn, tk), lambda i, j, k: (k // kps, j, ob + k % kps)),
        pl.BlockSpec((tm, tn), lambda i, j, k: (i, j)),
        jax.ShapeDtypeStruct((t_len, d), F32), (tm, tn),
        add=add, add_spec=None if add is None else pl.BlockSpec((tm, tn), lambda i, j, k: (i, j)))


def _mm_grad_wc(name, a, dy, seg):
    _, w, tn = seg
    t_len, d = a.shape
    tm = _tile(d, (1024, 512, 256, 128))
    tk = _tile(t_len, (2048, 1024, 512, 256, 128))
    nps = w // tn
    grid = (d // tm, N_CHIPS * nps, t_len // tk)
    return _matmul(
        name, a, dy, TN, grid,
        pl.BlockSpec((tk, tm), lambda i, j, k: (k, i)),
        pl.BlockSpec((tk, tn), lambda i, j, k: (k, j)),
        pl.BlockSpec((None, tm, tn), lambda i, j, k: (j // nps, i, j % nps)),
        jax.ShapeDtypeStruct((N_CHIPS, d, w), BF16), (tm, tn))


def _is_plain(wr, seg):
    return seg[0] == 0 and wr.shape[1] == seg[1] and (N_CHIPS * seg[1]) % 1024 == 0


def _mm_act_wr(name, a, wr, seg, add):
    off, r, tk = seg
    t_len = a.shape[0]
    d = wr.shape[2]
    if seg[0] == 0 and wr.shape[1] == r:
        return _mm_plain(name, a, wr.reshape(N_CHIPS * r, d), NN, F32, add=add)
    tm = _tile(t_len, (1024, 512, 256, 128))
    tn = _tile(d, (1024, 512, 256, 128))
    kps = r // tk
    ob = off // tk
    grid = (t_len // tm, d // tn, N_CHIPS * kps)
    return _matmul(
        name, a, wr, NN, grid,
        pl.BlockSpec((tm, tk), lambda i, j, k: (i, k)),
        pl.BlockSpec((None, tk, tn), lambda i, j, k: (k // kps, ob + k % kps, j)),
        pl.BlockSpec((tm, tn), lambda i, j, k: (i, j)),
        jax.ShapeDtypeStruct((t_len, d), F32), (tm, tn),
        add=add, add_spec=pl.BlockSpec((tm, tn), lambda i, j, k: (i, j)))


def _mm_dact_wrT(name, dh, wr, seg):
    off, r, tn = seg
    t_len, d = dh.shape
    if _is_plain(wr, seg):
        return _mm_plain(name, dh, wr.reshape(N_CHIPS * r, d), NT, BF16)
    tm = _tile(t_len, (1024, 512, 256, 128))
    nps = r // tn
    ob = off // tn
    grid = (t_len // tm, N_CHIPS * nps, 1)
    return _matmul(
        name, dh, wr, NT, grid,
        pl.BlockSpec((tm, d), lambda i, j, k: (i, 0)),
        pl.BlockSpec((None, tn, d), lambda i, j, k: (j // nps, ob + j % nps, 0)),
        pl.BlockSpec((tm, tn), lambda i, j, k: (i, j)),
        jax.ShapeDtypeStruct((t_len, N_CHIPS * r), BF16), (tm, tn))


def _mm_grad_wr(name, a, dh, seg):
    _, r, tm = seg
    t_len, d = dh.shape
    if (N_CHIPS * r) % 1024 == 0:
        return _mm_plain(name, a, dh, TN, BF16).reshape(N_CHIPS, r, d)
    tn = _tile(d, (1024, 512, 256, 128))
    tk = _tile(t_len, (2048, 1024, 512, 256, 128))
    mps = r // tm
    grid = (N_CHIPS * mps, d // tn, t_len // tk)
    return _matmul(
        name, a, dh, TN, grid,
        pl.BlockSpec((tk, tm), lambda i, j, k: (k, i)),
        pl.BlockSpec((tk, tn), lambda i, j, k: (k, j)),
        pl.BlockSpec((None, tm, tn), lambda i, j, k: (i // mps, i % mps, j)),
        jax.ShapeDtypeStruct((N_CHIPS, r, d), BF16), (tm, tn))


def _mm_plain(name, a, b, dims, out_dtype, add=None):
    if dims == NN:
        m, kd = a.shape
        n = b.shape[1]
    elif dims == NT:
        m, kd = a.shape
        n = b.shape[0]
    else:
        kd, m = a.shape
        n = b.shape[1]
    tm = _tile(m, (1024, 512, 256, 128))
    tn = _tile(n, (1024, 768, 512, 256, 128))
    tk = _tile(kd, (MAX_K_TILE, 2048, 1408, 1024, 512, 256, 128))
    grid = (m // tm, n // tn, kd // tk)
    if dims == NN:
        a_spec = pl.BlockSpec((tm, tk), lambda i, j, k: (i, k))
        b_spec = pl.BlockSpec((tk, tn), lambda i, j, k: (k, j))
    elif dims == NT:
        a_spec = pl.BlockSpec((tm, tk), lambda i, j, k: (i, k))
        b_spec = pl.BlockSpec((tn, tk), lambda i, j, k: (j, k))
    else:
        a_spec = pl.BlockSpec((tk, tm), lambda i, j, k: (k, i))
        b_spec = pl.BlockSpec((tk, tn), lambda i, j, k: (k, j))
    o_spec = pl.BlockSpec((tm, tn), lambda i, j, k: (i, j))
    return _matmul(name, a, b, dims, grid, a_spec, b_spec, o_spec, jax.ShapeDtypeStruct((m, n), out_dtype), (tm, tn),
                   add=add, add_spec=None if add is None else o_spec)


def _rms_fwd(name, x, g):
    t_len, d = x.shape
    tm = _tile(t_len, (512, 256, 128))

    def body(x_ref, g_ref, o_ref):
        xv = x_ref[...]
        r = lax.rsqrt(jnp.mean(xv * xv, axis=-1, keepdims=True) + EPS)
        o_ref[...] = (xv * r * g_ref[...]).astype(o_ref.dtype)

    return pl.pallas_call(
        body, name=name, grid=(t_len // tm,),
        in_specs=[pl.BlockSpec((tm, d), lambda i: (i, 0)), pl.BlockSpec((1, d), lambda i: (0, 0))],
        out_specs=pl.BlockSpec((tm, d), lambda i: (i, 0)),
        out_shape=jax.ShapeDtypeStruct((t_len, d), BF16), compiler_params=_params(1))(x, g)


def _rms_bwd(name, dys, x, gs, dres):
    t_len, d = x.shape
    n = len(dys)
    tm = _tile(t_len, (256, 128))

    def body(*refs):
        dy_refs, x_ref, g_refs = refs[:n], refs[n], refs[n + 1:2 * n + 1]
        dres_ref, dx_ref, dg_refs = refs[2 * n + 1], refs[2 * n + 2], refs[2 * n + 3:]
        xv = x_ref[...]
        r = lax.rsqrt(jnp.mean(xv * xv, axis=-1, keepdims=True) + EPS)
        xhat = xv * r
        dyv = [ref[...].astype(F32) for ref in dy_refs]
        dxn = dyv[0] * g_refs[0][...]
        for k in range(1, n):
            dxn = dxn + dyv[k] * g_refs[k][...]
        dx = r * (dxn - xhat * jnp.mean(dxn * xhat, axis=-1, keepdims=True))
        dx_ref[...] = dres_ref[...] + dx
        parts = [jnp.sum(v * xhat, axis=0, keepdims=True) for v in dyv]

        @pl.when(pl.program_id(0) == 0)
        def _():
            for ref, p in zip(dg_refs, parts):
                ref[...] = p

        @pl.when(pl.program_id(0) > 0)
        def _():
            for ref, p in zip(dg_refs, parts):
                ref[...] += p

    row = pl.BlockSpec((tm, d), lambda i: (i, 0))
    vec = pl.BlockSpec((1, d), lambda i: (0, 0))
    res = pl.pallas_call(
        body, name=name, grid=(t_len // tm,), in_specs=[row] * (n + 1) + [vec] * n + [row], out_specs=(row,) + (vec,) * n,
        out_shape=(jax.ShapeDtypeStruct((t_len, d), F32),) + (jax.ShapeDtypeStruct((1, d), F32),) * n,
        compiler_params=_params(1))(*dys, x, *gs, dres)
    return res[0], res[1:]


def _loss_head(h, g, target):
    t_len, d = h.shape
    tm = _tile(t_len, (256, 128))

    def body(h_ref, g_ref, t_ref, dh_ref, dg_ref, loss_ref):
        xv = h_ref[...]
        gv = g_ref[...]
        r = lax.rsqrt(jnp.mean(xv * xv, axis=-1, keepdims=True) + EPS)
        xhat = xv * r
        err = xhat * gv - t_ref[...]
        dyv = err * (1.0 / d)
        dxn = dyv * gv
        dh_ref[...] = r * (dxn - xhat * jnp.mean(dxn * xhat, axis=-1, keepdims=True))
        part = jnp.sum(dyv * xhat, axis=0, keepdims=True)
        lpart = jnp.zeros((8, LANE), F32) + (0.5 / d) * jnp.sum(err * err)

        @pl.when(pl.program_id(0) == 0)
        def _():
            dg_ref[...] = part
            loss_ref[...] = lpart

        @pl.when(pl.program_id(0) > 0)
        def _():
            dg_ref[...] += part
            loss_ref[...] += lpart

    row = pl.BlockSpec((tm, d), lambda i: (i, 0))
    vec = pl.BlockSpec((1, d), lambda i: (0, 0))
    return pl.pallas_call(
        body, name="loss_head", grid=(t_len // tm,), in_specs=[row, vec, row],
        out_specs=(row, vec, pl.BlockSpec((8, LANE), lambda i: (0, 0))),
        out_shape=(jax.ShapeDtypeStruct((t_len, d), F32), jax.ShapeDtypeStruct((1, d), F32),
                   jax.ShapeDtypeStruct((8, LANE), F32)),
        compiler_params=_params(1))(h, g, target)


def _chunk_row(shape):
    return lax.broadcasted_iota(jnp.int32, shape, 0) % GLA_CHUNK


def _gla_gate_fwd(a, w_a2p, b_a2):
    t_len = a.shape[0]
    kd = w_a2p.shape[1]
    tm = _tile(t_len, (256, 128, 64))

    def body(a_ref, w_ref, b_ref, ga_ref, cum_ref):
        ga = _dot(a_ref[...], w_ref[...].astype(BF16), NN) + b_ref[...]
        ga_ref[...] = ga
        la = (jnp.minimum(ga, 0.0) - jnp.log(1.0 + jnp.exp(-jnp.abs(ga)))) * (1.0 / GATE_NORMALIZER)
        row = _chunk_row(la.shape)
        s = 1
        while s < GLA_CHUNK:
            la = la + jnp.where(row >= s, pltpu.roll(la, s, 0), 0.0)
            s *= 2
        cum_ref[...] = la

    return pl.pallas_call(
        body, name="gla_gate_fwd", grid=(t_len // tm,),
        in_specs=[pl.BlockSpec((tm, A_PAD), lambda i: (i, 0)), pl.BlockSpec((A_PAD, kd), lambda i: (0, 0)),
                  pl.BlockSpec((1, kd), lambda i: (0, 0))],
        out_specs=(pl.BlockSpec((tm, kd), lambda i: (i, 0)), pl.BlockSpec((tm, kd), lambda i: (i, 0))),
        out_shape=(jax.ShapeDtypeStruct((t_len, kd), F32), jax.ShapeDtypeStruct((t_len, kd), F32)),
        compiler_params=_params(1))(a, w_a2p, b_a2)


def _gla_gate_bwd(dcum, ga, a, w_a2p):
    t_len, kd = dcum.shape
    tm = _tile(t_len, (256, 128, 64))

    def body(dc_ref, ga_ref, a_ref, w_ref, da_ref, dw_ref, db_ref):
        x = dc_ref[...]
        row = _chunk_row(x.shape)
        s = 1
        while s < GLA_CHUNK:
            x = x + jnp.where(row < GLA_CHUNK - s, pltpu.roll(x, tm - s, 0), 0.0)
            s *= 2
        dga = x * (1.0 / GATE_NORMALIZER) * _sigmoid(-ga_ref[...])
        dgab = dga.astype(BF16)
        da_ref[...] = _dot(dgab, w_ref[...].astype(BF16), NT).astype(da_ref.dtype)
        dw = _dot(a_ref[...], dgab, TN)
        db = jnp.sum(dga, axis=0, keepdims=True)

        @pl.when(pl.program_id(0) == 0)
        def _():
            dw_ref[...] = dw
            db_ref[...] = db

        @pl.when(pl.program_id(0) > 0)
        def _():
            dw_ref[...] += dw
            db_ref[...] += db

    wide = pl.BlockSpec((tm, kd), lambda i: (i, 0))
    return pl.pallas_call(
        body, name="gla_gate_bwd", grid=(t_len // tm,),
        in_specs=[wide, wide, pl.BlockSpec((tm, A_PAD), lambda i: (i, 0)), pl.BlockSpec((A_PAD, kd), lambda i: (0, 0))],
        out_specs=(pl.BlockSpec((tm, A_PAD), lambda i: (i, 0)), pl.BlockSpec((A_PAD, kd), lambda i: (0, 0)),
                   pl.BlockSpec((1, kd), lambda i: (0, 0))),
        out_shape=(jax.ShapeDtypeStruct((t_len, A_PAD), BF16), jax.ShapeDtypeStruct((A_PAD, kd), F32),
                   jax.ShapeDtypeStruct((1, kd), F32)),
        compiler_params=_params(1))(dcum, ga, a, w_a2p)


GLA_STEP_CHUNKS = 4


def _gla_dims():
    dk = GLA_KEY_DIM // GLA_HEADS
    dv = GLA_VAL_DIM // GLA_HEADS
    return dk, dv


def _gla_fwd(proj, cum):
    t_len = proj.shape[0]
    dk, dv = _gla_dims()
    nc = t_len // GLA_CHUNK
    c = GLA_CHUNK
    scale = dk ** -0.5
    v0 = 2 * GLA_KEY_DIM // dv

    per = _tile(nc, (GLA_STEP_CHUNKS, 2, 1))
    rows = per * c

    def body(q_ref, k_ref, v_ref, cum_ref, o_ref, st_ref, s_scr):
        @pl.when(pl.program_id(1) == 0)
        def _():
            s_scr[...] = jnp.zeros_like(s_scr)

        tri = lax.broadcasted_iota(jnp.int32, (c, c), 0) >= lax.broadcasted_iota(jnp.int32, (c, c), 1)
        for i in range(per):
            rs = slice(i * c, (i + 1) * c)
            cm = cum_ref[rs, :]
            last = cm[c - 1:c, :]
            q = q_ref[rs, :].astype(F32) * scale
            k = k_ref[rs, :].astype(F32)
            v = v_ref[rs, :].astype(BF16)
            qd = (q * jnp.exp(cm)).astype(BF16)
            ki = (k * jnp.exp(-cm)).astype(BF16)
            ke = (k * jnp.exp(last - cm)).astype(BF16)
            sc = jnp.where(tri, _dot(qd, ki, NT), 0.0)
            st = s_scr[...]
            st_ref[i] = st
            o_ref[rs, :] = _dot(sc.astype(BF16), v, NN) + _dot(qd, st.astype(BF16), NT)
            s_scr[...] = st * jnp.exp(last) + _dot(v, ke, TN)

    return pl.pallas_call(
        body, name="gla_fwd", grid=(GLA_HEADS, nc // per),
        in_specs=[pl.BlockSpec((rows, dk), lambda h, n: (n, h)),
                  pl.BlockSpec((rows, dk), lambda h, n: (n, GLA_HEADS + h)),
                  pl.BlockSpec((rows, dv), lambda h, n: (n, v0 + h)),
                  pl.BlockSpec((rows, dk), lambda h, n: (n, h))],
        out_specs=(pl.BlockSpec((rows, dv), lambda h, n: (n, h)),
                   pl.BlockSpec((None, per, dv, dk), lambda h, n: (h, n, 0, 0))),
        out_shape=(jax.ShapeDtypeStruct((t_len, GLA_VAL_DIM), F32),
                   jax.ShapeDtypeStruct((GLA_HEADS, nc, dv, dk), F32)),
        scratch_shapes=[pltpu.VMEM((dv, dk), F32)], compiler_params=_params(2))(proj, proj, proj, cum)


def _gla_bwd(proj, cum, states, do):
    t_len = proj.shape[0]
    dk, dv = _gla_dims()
    nc = t_len // GLA_CHUNK
    c = GLA_CHUNK
    scale = dk ** -0.5
    v0 = 2 * GLA_KEY_DIM // dv

    per = _tile(nc, (GLA_STEP_CHUNKS, 2, 1))
    rows = per * c

    def body(q_ref, k_ref, v_ref, cum_ref, st_ref, do_ref, dq_ref, dk_ref, dv_ref, dc_ref, ds_scr):
        @pl.when(pl.program_id(1) == 0)
        def _():
            ds_scr[...] = jnp.zeros_like(ds_scr)

        tri = lax.broadcasted_iota(jnp.int32, (c, c), 0) >= lax.broadcasted_iota(jnp.int32, (c, c), 1)
        row = lax.broadcasted_iota(jnp.int32, (c, dk), 0)
        for i in reversed(range(per)):
            rs = slice(i * c, (i + 1) * c)
            cm = cum_ref[rs, :]
            last = cm[c - 1:c, :]
            e_c = jnp.exp(cm)
            e_nc = jnp.exp(-cm)
            e_lc = jnp.exp(last - cm)
            e_l = jnp.exp(last)
            q = q_ref[rs, :].astype(F32) * scale
            k = k_ref[rs, :].astype(F32)
            v = v_ref[rs, :].astype(BF16)
            dov = do_ref[rs, :]
            qd32 = q * e_c
            ki32 = k * e_nc
            ke32 = k * e_lc
            qd = qd32.astype(BF16)
            ki = ki32.astype(BF16)
            ke = ke32.astype(BF16)
            st = st_ref[i]
            dst = ds_scr[...]
            dstb = dst.astype(BF16)
            am = jnp.where(tri, _dot(dov, v, NT), 0.0).astype(BF16)
            pm = jnp.where(tri, _dot(qd, ki, NT), 0.0).astype(BF16)
            dqd = _dot(am, ki, NN) + _dot(dov, st.astype(BF16), NN)
            dki = _dot(am, qd, TN)
            dvv = _dot(pm, dov, TN) + _dot(ke, dstb, NT)
            dke = _dot(v, dstb, NN)
            d_el = jnp.sum(dst * st, axis=0, keepdims=True)
            ds_scr[...] = dst * e_l + _dot(dov, qd, TN)
            dq_ref[rs, :] = (dqd * scale * e_c).astype(dq_ref.dtype)
            dk_ref[rs, :] = (dki * e_nc + dke * e_lc).astype(dk_ref.dtype)
            dv_ref[rs, :] = dvv.astype(dv_ref.dtype)
            dkeke = dke * ke32
            dcum = dqd * qd32 - dki * ki32 - dkeke
            dlast = jnp.sum(dkeke, axis=0, keepdims=True) + d_el * e_l
            dc_ref[rs, :] = jnp.where(row == c - 1, dcum + dlast, dcum)

    rev = nc // per - 1
    return pl.pallas_call(
        body, name="gla_bwd", grid=(GLA_HEADS, nc // per),
        in_specs=[pl.BlockSpec((rows, dk), lambda h, n: (rev - n, h)),
                  pl.BlockSpec((rows, dk), lambda h, n: (rev - n, GLA_HEADS + h)),
                  pl.BlockSpec((rows, dv), lambda h, n: (rev - n, v0 + h)),
                  pl.BlockSpec((rows, dk), lambda h, n: (rev - n, h)),
                  pl.BlockSpec((None, per, dv, dk), lambda h, n: (h, rev - n, 0, 0)),
                  pl.BlockSpec((rows, dv), lambda h, n: (rev - n, h))],
        out_specs=(pl.BlockSpec((rows, dk), lambda h, n: (rev - n, h)),
                   pl.BlockSpec((rows, dk), lambda h, n: (rev - n, h)),
                   pl.BlockSpec((rows, dv), lambda h, n: (rev - n, h)),
                   pl.BlockSpec((rows, dk), lambda h, n: (rev - n, h))),
        out_shape=(jax.ShapeDtypeStruct((t_len, GLA_KEY_DIM), BF16), jax.ShapeDtypeStruct((t_len, GLA_KEY_DIM), BF16),
                   jax.ShapeDtypeStruct((t_len, GLA_VAL_DIM), BF16), jax.ShapeDtypeStruct((t_len, GLA_KEY_DIM), F32)),
        scratch_shapes=[pltpu.VMEM((dv, dk), F32)], compiler_params=_params(2))(proj, proj, proj, cum, states, do)


def _gla_out_fwd(o, proj, gn):
    t_len = o.shape[0]
    _, dv = _gla_dims()
    tm = _tile(t_len, (512, 256, 128))
    r0 = (2 * GLA_KEY_DIM + GLA_VAL_DIM) // dv

    def body(o_ref, r_ref, g_ref, y_ref):
        ov = o_ref[...]
        rs = lax.rsqrt(jnp.mean(ov * ov, axis=-1, keepdims=True) + EPS)
        rv = r_ref[...].astype(F32)
        y_ref[...] = (ov * rs * g_ref[...] * (rv * _sigmoid(rv))).astype(y_ref.dtype)

    return pl.pallas_call(
        body, name="gla_out_fwd", grid=(t_len // tm, GLA_HEADS),
        in_specs=[pl.BlockSpec((tm, dv), lambda i, h: (i, h)), pl.BlockSpec((tm, dv), lambda i, h: (i, r0 + h)),
                  pl.BlockSpec((1, dv), lambda i, h: (0, 0))],
        out_specs=pl.BlockSpec((tm, dv), lambda i, h: (i, h)),
        out_shape=jax.ShapeDtypeStruct((t_len, GLA_VAL_DIM), BF16), compiler_params=_params(2))(o, proj, gn)


def _gla_out_bwd(dy, o, proj, gn):
    t_len = o.shape[0]
    _, dv = _gla_dims()
    tm = _tile(t_len, (512, 256, 128))
    r0 = (2 * GLA_KEY_DIM + GLA_VAL_DIM) // dv

    def body(dy_ref, o_ref, r_ref, g_ref, do_ref, dr_ref, dg_ref):
        ov = o_ref[...]
        gv = g_ref[...]
        rs = lax.rsqrt(jnp.mean(ov * ov, axis=-1, keepdims=True) + EPS)
        xhat = ov * rs
        rv = r_ref[...].astype(F32)
        sg = _sigmoid(rv)
        gate = rv * sg
        dyv = dy_ref[...].astype(F32)
        dn = dyv * gate
        dr_ref[...] = (dyv * xhat * gv * (sg * (1.0 + rv * (1.0 - sg)))).astype(dr_ref.dtype)
        dxn = dn * gv
        do_ref[...] = (rs * (dxn - xhat * jnp.mean(dxn * xhat, axis=-1, keepdims=True))).astype(do_ref.dtype)
        part = jnp.sum(dn * xhat, axis=0, keepdims=True)
        first = (pl.program_id(0) == 0) & (pl.program_id(1) == 0)

        @pl.when(first)
        def _():
            dg_ref[...] = part

        @pl.when(jnp.logical_not(first))
        def _():
            dg_ref[...] += part

    blk = pl.BlockSpec((tm, dv), lambda i, h: (i, h))
    return pl.pallas_call(
        body, name="gla_out_bwd", grid=(t_len // tm, GLA_HEADS),
        in_specs=[blk, blk, pl.BlockSpec((tm, dv), lambda i, h: (i, r0 + h)), pl.BlockSpec((1, dv), lambda i, h: (0, 0))],
        out_specs=(blk, blk, pl.BlockSpec((1, dv), lambda i, h: (0, 0))),
        out_shape=(jax.ShapeDtypeStruct((t_len, GLA_VAL_DIM), BF16), jax.ShapeDtypeStruct((t_len, GLA_VAL_DIM), BF16),
                   jax.ShapeDtypeStruct((1, dv), F32)),
        compiler_params=_params(2))(dy, o, proj, gn)


def _alibi_slopes():
    n = ATT_HEADS
    start = 2.0 ** (-8.0 / n)
    return [start ** (i + 1) for i in range(n)]


def _att_masks(d):
    b = ATT_BLOCK
    qa = lax.broadcasted_iota(jnp.int32, (b, b), 0)
    kb = lax.broadcasted_iota(jnp.int32, (b, b), 1)
    dist_c = qa - kb
    dist_p = qa - kb + b
    return dist_c >= 0, dist_p <= b, (dist_c * d).astype(F32), (dist_p * d).astype(F32)


def _to_dilated(name, x, d, c0=0, w=None):
    part = x if w is None else x[:, c0:c0 + w]
    return part.reshape(x.shape[0] // d, -1)


def _from_dilated(name, y, d):
    return y.reshape(y.shape[0] * d, y.shape[1] // d)


def _att_views(q_all, kv, g):
    d = DILATIONS[g]
    hd = ATT_HEADS * HEAD_DIM
    if d == 1:
        return q_all, kv
    return _to_dilated(f"q_dilated{g}", q_all, d, g * hd, hd), _to_dilated(f"kv_dilated{g}", kv, d)


def _att_fwd(views, g):
    d = DILATIONS[g]
    assert WINDOWS[g] // d == ATT_BLOCK
    qv, kvv = views
    hd = ATT_HEADS * HEAD_DIM
    sub = kvv.shape[0]
    t_len = sub * d
    nb = sub // ATT_BLOCK
    b = ATT_BLOCK
    e = HEAD_DIM
    scale = e ** -0.5
    slopes = _alibi_slopes()
    qc = (lambda r: 3 * r + g) if d == 1 else (lambda r: r)

    def body(q_ref, kp_ref, kc_ref, vp_ref, vc_ref, o_ref, l_ref, s_scr, p_scr, li_scr):
        ib = pl.program_id(1)
        valid_c, valid_p0, dist_c, dist_p = _att_masks(d)
        valid_p = valid_p0 & (ib > 0)
        for h in range(ATT_HEADS):
            hs = slice(h * e, (h + 1) * e)
            qh = q_ref[:, hs]
            s_scr[h, 0] = _dot(qh, kc_ref[:, hs], NT)
            s_scr[h, 1] = _dot(qh, kp_ref[:, hs], NT)
        l_ref[...] = jnp.zeros_like(l_ref)
        for h in range(ATT_HEADS):
            s_c = jnp.where(valid_c, s_scr[h, 0] * scale - slopes[h] * dist_c, NEG)
            s_p = jnp.where(valid_p, s_scr[h, 1] * scale - slopes[h] * dist_p, NEG)
            m = jnp.maximum(jnp.max(s_c, axis=1, keepdims=True), jnp.max(s_p, axis=1, keepdims=True))
            p_c = jnp.where(valid_c, jnp.exp(s_c - m), 0.0)
            p_p = jnp.where(valid_p, jnp.exp(s_p - m), 0.0)
            l = jnp.sum(p_c, axis=1, keepdims=True) + jnp.sum(p_p, axis=1, keepdims=True)
            p_scr[h, 0] = p_c.astype(BF16)
            p_scr[h, 1] = p_p.astype(BF16)
            li_scr[:, h:h + 1] = 1.0 / l
            l_ref[:, h:h + 1] = m + jnp.log(l)
        for h in range(ATT_HEADS):
            hs = slice(h * e, (h + 1) * e)
            acc = _dot(p_scr[h, 0], vc_ref[:, hs], NN) + _dot(p_scr[h, 1], vp_ref[:, hs], NN)
            o_ref[:, hs] = acc * li_scr[:, h:h + 1]

    blk = (b, hd)
    cblk = (b, LANE)
    o, lse = pl.pallas_call(
        body, name=f"att_fwd{g}", grid=(d, nb),
        scratch_shapes=[pltpu.VMEM((ATT_HEADS, 2, b, b), F32), pltpu.VMEM((ATT_HEADS, 2, b, b), BF16),
                        pltpu.VMEM((b, LANE), F32)],
        in_specs=[pl.BlockSpec(blk, lambda r, i: (i, qc(r))),
                  pl.BlockSpec(blk, lambda r, i: (jnp.maximum(i - 1, 0), 2 * r)),
                  pl.BlockSpec(blk, lambda r, i: (i, 2 * r)),
                  pl.BlockSpec(blk, lambda r, i: (jnp.maximum(i - 1, 0), 2 * r + 1)),
                  pl.BlockSpec(blk, lambda r, i: (i, 2 * r + 1))],
        out_specs=(pl.BlockSpec(blk, lambda r, i: (i, r)), pl.BlockSpec(cblk, lambda r, i: (i, r))),
        out_shape=(jax.ShapeDtypeStruct((sub, d * hd), F32), jax.ShapeDtypeStruct((sub, d * LANE), F32)),
        compiler_params=_params(2))(qv, kvv, kvv, kvv, kvv)
    return _from_dilated(f"o_natural{g}", o, d), lse.reshape(t_len, LANE)


def _att_merge(os, ls):
    t_len, hd = os[0].shape
    tm = _tile(t_len, (256, 128))
    e = HEAD_DIM

    def body(o0, o1, o2, l0, l1, l2, of_ref, ob_ref, l_ref):
        a0, a1, a2 = l0[...], l1[...], l2[...]
        m = jnp.maximum(jnp.maximum(a0, a1), a2)
        e0, e1, e2 = jnp.exp(a0 - m), jnp.exp(a1 - m), jnp.exp(a2 - m)
        den = e0 + e1 + e2
        w0, w1, w2 = e0 / den, e1 / den, e2 / den
        l_ref[...] = m + jnp.log(den)
        for h in range(ATT_HEADS):
            hs = slice(h * e, (h + 1) * e)
            c = slice(h, h + 1)
            o = w0[:, c] * o0[:, hs] + w1[:, c] * o1[:, hs] + w2[:, c] * o2[:, hs]
            of_ref[:, hs] = o
            ob_ref[:, hs] = o.astype(ob_ref.dtype)

    row = pl.BlockSpec((tm, hd), lambda i: (i, 0))
    crow = pl.BlockSpec((tm, LANE), lambda i: (i, 0))
    return pl.pallas_call(
        body, name="att_merge", grid=(t_len // tm,), in_specs=[row] * 3 + [crow] * 3, out_specs=(row, row, crow),
        out_shape=(jax.ShapeDtypeStruct((t_len, hd), F32), jax.ShapeDtypeStruct((t_len, hd), BF16),
                   jax.ShapeDtypeStruct((t_len, LANE), F32)),
        compiler_params=_params(1))(*os, *ls)


def _att_delta(do, o):
    t_len, hd = o.shape
    tm = _tile(t_len, (256, 128))
    e = HEAD_DIM

    def body(do_ref, o_ref, d_ref):
        d_ref[...] = jnp.zeros_like(d_ref)
        for h in range(ATT_HEADS):
            hs = slice(h * e, (h + 1) * e)
            d_ref[:, h:h + 1] = jnp.sum(do_ref[:, hs].astype(F32) * o_ref[:, hs], axis=1, keepdims=True)

    row = pl.BlockSpec((tm, hd), lambda i: (i, 0))
    return pl.pallas_call(
        body, name="att_delta", grid=(t_len // tm,), in_specs=[row, row],
        out_specs=pl.BlockSpec((tm, LANE), lambda i: (i, 0)),
        out_shape=jax.ShapeDtypeStruct((t_len, LANE), F32), compiler_params=_params(1))(do, o)


def _att_bwd(views, delta, lse, do, g):
    d = DILATIONS[g]
    qv, kvv = views
    hd = ATT_HEADS * HEAD_DIM
    sub = kvv.shape[0]
    t_len = sub * d
    nb = sub // ATT_BLOCK
    b = ATT_BLOCK
    e = HEAD_DIM
    scale = e ** -0.5
    slopes = _alibi_slopes()
    qc = (lambda r: 3 * r + g) if d == 1 else (lambda r: r)
    dlv = delta.reshape(sub, d * LANE)
    lv = lse.reshape(sub, d * LANE)
    dov = do if d == 1 else _to_dilated(f"do_dilated{g}", do, d)

    def body(qj_ref, qn_ref, kp_ref, kc_ref, vp_ref, vc_ref, doj_ref, don_ref, dj_ref, dn_ref, lj_ref, ln_ref,
             dq_ref, dk_ref, dv_ref, s_scr, dp_scr, p_scr, ds_scr):
        j = pl.program_id(1)
        valid_c, valid_p0, dist_c, dist_p = _att_masks(d)
        valid = (valid_c, valid_p0 & (j > 0), valid_p0 & (j + 1 < nb))
        dist = (dist_c, dist_p, dist_p)
        for h in range(ATT_HEADS):
            hs = slice(h * e, (h + 1) * e)
            qj, qn = qj_ref[:, hs], qn_ref[:, hs]
            kc, kp = kc_ref[:, hs], kp_ref[:, hs]
            vc, vp = vc_ref[:, hs], vp_ref[:, hs]
            doj, don = doj_ref[:, hs], don_ref[:, hs]
            s_scr[h, 0] = _dot(qj, kc, NT)
            s_scr[h, 1] = _dot(qj, kp, NT)
            s_scr[h, 2] = _dot(qn, kc, NT)
            dp_scr[h, 0] = _dot(doj, vc, NT)
            dp_scr[h, 1] = _dot(doj, vp, NT)
            dp_scr[h, 2] = _dot(don, vc, NT)
        for h in range(ATT_HEADS):
            c = slice(h, h + 1)
            lse_t = (lj_ref[:, c], lj_ref[:, c], ln_ref[:, c])
            dlt_t = (dj_ref[:, c], dj_ref[:, c], dn_ref[:, c])
            for t in range(3):
                s = s_scr[h, t] * scale - slopes[h] * dist[t]
                p = jnp.where(valid[t], jnp.exp(jnp.where(valid[t], s - lse_t[t], NEG)), 0.0)
                p_scr[h, t] = p.astype(BF16)
                ds_scr[h, t] = (p * (dp_scr[h, t] - dlt_t[t])).astype(BF16)
        for h in range(ATT_HEADS):
            hs = slice(h * e, (h + 1) * e)
            dq = _dot(ds_scr[h, 0], kc_ref[:, hs], NN) + _dot(ds_scr[h, 1], kp_ref[:, hs], NN)
            dk = _dot(ds_scr[h, 0], qj_ref[:, hs], TN) + _dot(ds_scr[h, 2], qn_ref[:, hs], TN)
            dv = _dot(p_scr[h, 0], doj_ref[:, hs], TN) + _dot(p_scr[h, 2], don_ref[:, hs], TN)
            dq_ref[:, hs] = (dq * scale).astype(dq_ref.dtype)
            dk_ref[:, hs] = (dk * scale).astype(dk_ref.dtype)
            dv_ref[:, hs] = dv.astype(dv_ref.dtype)

    blk = (b, hd)
    cblk = (b, LANE)
    nxt = lambda i: jnp.minimum(i + 1, nb - 1)
    prv = lambda i: jnp.maximum(i - 1, 0)
    tiles = (ATT_HEADS, 3, b, b)
    dq, dk, dv = pl.pallas_call(
        body, name=f"att_bwd{g}", grid=(d, nb),
        scratch_shapes=[pltpu.VMEM(tiles, F32), pltpu.VMEM(tiles, F32), pltpu.VMEM(tiles, BF16), pltpu.VMEM(tiles, BF16)],
        in_specs=[pl.BlockSpec(blk, lambda r, i: (i, qc(r))),
                  pl.BlockSpec(blk, lambda r, i: (nxt(i), qc(r))),
                  pl.BlockSpec(blk, lambda r, i: (prv(i), 2 * r)),
                  pl.BlockSpec(blk, lambda r, i: (i, 2 * r)),
                  pl.BlockSpec(blk, lambda r, i: (prv(i), 2 * r + 1)),
                  pl.BlockSpec(blk, lambda r, i: (i, 2 * r + 1)),
                  pl.BlockSpec(blk, lambda r, i: (i, r)),
                  pl.BlockSpec(blk, lambda r, i: (nxt(i), r)),
                  pl.BlockSpec(cblk, lambda r, i: (i, r)),
                  pl.BlockSpec(cblk, lambda r, i: (nxt(i), r)),
                  pl.BlockSpec(cblk, lambda r, i: (i, r)),
                  pl.BlockSpec(cblk, lambda r, i: (nxt(i), r))],
        out_specs=(pl.BlockSpec(blk, lambda r, i: (i, r)),) * 3,
        out_shape=(jax.ShapeDtypeStruct((sub, d * hd), BF16),) * 3,
        compiler_params=_params(2))(qv, qv, kvv, kvv, kvv, kvv, dov, dov, dlv, dlv, lv, lv)
    return tuple(_from_dilated(f"{n}_natural{g}", t, d) for n, t in (("dq", dq), ("dk", dk), ("dv", dv)))


def _kv_grad_sum(dks, dvs):
    t_len, hd = dks[0].shape
    tm = _tile(t_len, (256, 128))

    def body(k0, k1, k2, v0, v1, v2, o_ref):
        o_ref[:, :hd] = (k0[...].astype(F32) + k1[...].astype(F32) + k2[...].astype(F32)).astype(o_ref.dtype)
        o_ref[:, hd:] = (v0[...].astype(F32) + v1[...].astype(F32) + v2[...].astype(F32)).astype(o_ref.dtype)

    row = pl.BlockSpec((tm, hd), lambda i: (i, 0))
    return pl.pallas_call(
        body, name="kv_grad_sum", grid=(t_len // tm,), in_specs=[row] * 6,
        out_specs=pl.BlockSpec((tm, 2 * hd), lambda i: (i, 0)),
        out_shape=jax.ShapeDtypeStruct((t_len, 2 * hd), BF16), compiler_params=_params(1))(*dks, *dvs)


HALO = 16
INV_SQRT2 = 1.0 / math.sqrt(2.0)
INV_SQRT2PI = 1.0 / math.sqrt(2.0 * math.pi)


def _conv_taps(g, halo, cw, cb):
    row = lax.broadcasted_iota(jnp.int32, g.shape, 0)
    h1 = halo[HALO - 1:HALO, :]
    h2 = halo[HALO - 2:HALO - 1, :]
    g1 = jnp.where(row == 0, h1, pltpu.roll(g, 1, 0))
    g2 = jnp.where(row == 0, h2, jnp.where(row == 1, h1, pltpu.roll(g, 2, 0)))
    gc = cw[0:1, :] * g2 + cw[1:2, :] * g1 + cw[2:3, :] * g + cb
    return gc, g1, g2


def _glu_specs(t_len, f, tm, tc):
    nj = f // tc
    hb = tm // HALO
    u = pl.BlockSpec((tm, tc), lambda j, i: (i, j))
    g = pl.BlockSpec((tm, tc), lambda j, i: (i, nj + j))
    gh = pl.BlockSpec((HALO, tc), lambda j, i: (jnp.maximum(i * hb - 1, 0), nj + j))
    cw = pl.BlockSpec((8, tc), lambda j, i: (0, j))
    cb = pl.BlockSpec((1, tc), lambda j, i: (0, j))
    return u, g, gh, cw, cb


def _glu_fwd(name, up, cw, cb):
    t_len = up.shape[0]
    f = up.shape[1] // 2
    tm = _tile(t_len, (512, 256, 128))
    tc = _tile(f, (1408, 1024, 512, 256, 128))
    u_s, g_s, gh_s, cw_s, cb_s = _glu_specs(t_len, f, tm, tc)

    def body(u_ref, g_ref, gh_ref, cw_ref, cb_ref, o_ref):
        first = pl.program_id(1) == 0
        halo = jnp.where(first, 0.0, gh_ref[...].astype(F32))
        gc, _, _ = _conv_taps(g_ref[...].astype(F32), halo, cw_ref[...], cb_ref[...])
        gel = 0.5 * gc * (1.0 + lax.erf(gc * INV_SQRT2))
        o_ref[...] = (gel * u_ref[...].astype(F32)).astype(o_ref.dtype)

    return pl.pallas_call(
        body, name=name, grid=(f // tc, t_len // tm), in_specs=[u_s, g_s, gh_s, cw_s, cb_s],
        out_specs=pl.BlockSpec((tm, tc), lambda j, i: (i, j)),
        out_shape=jax.ShapeDtypeStruct((t_len, f), BF16), compiler_params=_params(2))(up, up, up, cw, cb)


def _glu_bwd_a(name, dact, up, cw, cb):
    t_len = up.shape[0]
    f = up.shape[1] // 2
    tm = _tile(t_len, (256, 128))
    tc = _tile(f, (1408, 1024, 512, 256, 128))
    u_s, g_s, gh_s, cw_s, cb_s = _glu_specs(t_len, f, tm, tc)

    def body(da_ref, u_ref, g_ref, gh_ref, cw_ref, cb_ref, du_ref, dgc_ref, w0_ref, w1_ref, w2_ref, b_ref):
        first = pl.program_id(1) == 0
        halo = jnp.where(first, 0.0, gh_ref[...].astype(F32))
        g = g_ref[...].astype(F32)
        gc, g1, g2 = _conv_taps(g, halo, cw_ref[...], cb_ref[...])
        phi = 0.5 * (1.0 + lax.erf(gc * INV_SQRT2))
        dgel = phi + gc * jnp.exp(-0.5 * gc * gc) * INV_SQRT2PI
        da = da_ref[...].astype(F32)
        du_ref[...] = (da * gc * phi).astype(du_ref.dtype)
        dgc = da * u_ref[...].astype(F32) * dgel
        dgc_ref[...] = dgc.astype(dgc_ref.dtype)
        parts = (jnp.sum(dgc * g2, axis=0, keepdims=True), jnp.sum(dgc * g1, axis=0, keepdims=True),
                 jnp.sum(dgc * g, axis=0, keepdims=True), jnp.sum(dgc, axis=0, keepdims=True))
        refs = (w0_ref, w1_ref, w2_ref, b_ref)

        @pl.when(first)
        def _():
            for r, p in zip(refs, parts):
                r[...] = p

        @pl.when(jnp.logical_not(first))
        def _():
            for r, p in zip(refs, parts):
                r[...] += p

    tile = pl.BlockSpec((tm, tc), lambda j, i: (i, j))
    vec = pl.BlockSpec((1, tc), lambda j, i: (0, j))
    vshape = jax.ShapeDtypeStruct((1, f), F32)
    return pl.pallas_call(
        body, name=name, grid=(f // tc, t_len // tm), in_specs=[tile, u_s, g_s, gh_s, cw_s, cb_s],
        out_specs=(tile, tile, vec, vec, vec, vec),
        out_shape=(jax.ShapeDtypeStruct((t_len, 2 * f), BF16), jax.ShapeDtypeStruct((t_len, f), BF16),
                   vshape, vshape, vshape, vshape),
        compiler_params=_params(2))(dact, up, up, up, cw, cb)


def _glu_bwd_b(name, dup, dgc, cw):
    t_len, f = dgc.shape
    tm = _tile(t_len, (256, 128, 64))
    hb = tm // HALO
    n_i = t_len // tm
    last_hb = t_len // HALO - 1

    def body(dup_ref, d_ref, dh_ref, cw_ref, o_ref):
        last = pl.program_id(0) == n_i - 1
        halo = jnp.where(last, 0.0, dh_ref[...].astype(F32))
        dd = d_ref[...].astype(F32)
        row = lax.broadcasted_iota(jnp.int32, dd.shape, 0)
        h0 = halo[0:1, :]
        h1 = halo[1:2, :]
        d1 = jnp.where(row == tm - 1, h0, pltpu.roll(dd, tm - 1, 0))
        d2 = jnp.where(row == tm - 1, h1, jnp.where(row == tm - 2, h0, pltpu.roll(dd, tm - 2, 0)))
        cwv = cw_ref[...]
        dg = cwv[2:3, :] * dd + cwv[1:2, :] * d1 + cwv[0:1, :] * d2
        o_ref[...] = dg.astype(o_ref.dtype)

    row_s = pl.BlockSpec((tm, f), lambda i: (i, 0))
    return pl.pallas_call(
        body, name=name, grid=(n_i,),
        in_specs=[ANY, row_s, pl.BlockSpec((HALO, f), lambda i: (jnp.minimum((i + 1) * hb, last_hb), 0)),
                  pl.BlockSpec((8, f), lambda i: (0, 0))],
        out_specs=pl.BlockSpec((tm, f), lambda i: (i, 1)), input_output_aliases={0: 0},
        out_shape=jax.ShapeDtypeStruct((t_len, 2 * f), BF16), compiler_params=_params(1))(dup, dgc, dgc, cw)


def _adamw(name, w, g, m, v):
    rows, cols = w.shape
    gcols = g.shape[1]
    n_out = 3 if gcols == cols else 4
    tr = _tile(rows, (256, 128, 64, 32, 16, 8))
    c1 = 1.0 / (1.0 - ADAM_B1 ** ADAM_STEP)
    c2 = 1.0 / (1.0 - ADAM_B2 ** ADAM_STEP)

    def body(w_ref, g_ref, m_ref, v_ref, d_ref, nm_ref, nv_ref, *g_out):
        gv = g_ref[...][:, :cols]
        nm = ADAM_B1 * m_ref[...] + (1.0 - ADAM_B1) * gv
        nv = ADAM_B2 * v_ref[...] + (1.0 - ADAM_B2) * (gv * gv)
        nm_ref[...] = nm
        nv_ref[...] = nv
        d_ref[...] = -ADAM_LR * ((nm * c1) / (jnp.sqrt(nv * c2) + ADAM_EPS) + ADAM_WD * w_ref[...])
        for ref in g_out:
            ref[...] = gv

    blk = pl.BlockSpec((tr, cols), lambda i: (i, 0))
    gblk = pl.BlockSpec((tr, gcols), lambda i: (i, 0))
    shp = jax.ShapeDtypeStruct((rows, cols), F32)
    return pl.pallas_call(body, name=name, grid=(rows // tr,), in_specs=[blk, gblk, blk, blk], out_specs=(blk,) * n_out,
                          out_shape=(shp,) * n_out, compiler_params=_params(1))(w, g, m, v)


def _adamw_layers(name, w, gs, m, v):
    _, rows, cols = w.shape
    tr = _tile(rows, (128, 64, 32, 16, 8))
    nr = rows // tr
    c1 = 1.0 / (1.0 - ADAM_B1 ** ADAM_STEP)
    c2 = 1.0 / (1.0 - ADAM_B2 ** ADAM_STEP)

    def body(w_ref, g0_ref, g1_ref, m_ref, v_ref, d_ref, nm_ref, nv_ref, g_ref):
        gv = jnp.where(pl.program_id(0) == 0, g0_ref[...], g1_ref[...])
        nm = ADAM_B1 * m_ref[...] + (1.0 - ADAM_B1) * gv
        nv = ADAM_B2 * v_ref[...] + (1.0 - ADAM_B2) * (gv * gv)
        nm_ref[...] = nm
        nv_ref[...] = nv
        d_ref[...] = -ADAM_LR * ((nm * c1) / (jnp.sqrt(nv * c2) + ADAM_EPS) + ADAM_WD * w_ref[...])
        g_ref[...] = gv

    blk = pl.BlockSpec((None, tr, cols), lambda l, i: (l, i, 0))
    g0_blk = pl.BlockSpec((tr, cols), lambda l, i: (jnp.where(l == 0, i, nr - 1), 0))
    g1_blk = pl.BlockSpec((tr, cols), lambda l, i: (jnp.where(l == 0, 0, i), 0))
    shp = jax.ShapeDtypeStruct(w.shape, F32)
    return pl.pallas_call(body, name=name, grid=(2, nr), in_specs=[blk, g0_blk, g1_blk, blk, blk], out_specs=(blk,) * 4,
                          out_shape=(shp,) * 4, compiler_params=_params(2))(w, gs[0], gs[1], m, v)


class _NoComm:
    def __init__(self):
        self.grads = {}

    def prefetch(self, group, ws, carry):
        return ws, carry

    def need(self, group, ws, after):
        return ws

    def reduce(self, group, grads, carry):
        self.grads.update(grads)
        return carry

    def tick(self, carry):
        return carry


def _local_step(x, target, ws, norms, small, hooks):
    lay = _layout()

    w_main, w_a = _unpack_gin(ws["gin"])
    hn0 = _rms_fwd("rms_attn0", x, norms["attn0"])
    proj = _mm_plain("gla_proj", hn0, w_main, NN, F32)
    a = _mm_plain("gla_proj_a", hn0, w_a, NN, BF16)
    ga, cum = _gla_gate_fwd(a, small["w_a2p"], small["b_a2"])
    ws, cum = hooks.prefetch("B0", ws, cum)
    o_gla, states = _gla_fwd(proj, cum)
    gated = _gla_out_fwd(o_gla, proj, small["head_norm"])
    ws = hooks.need("B0", ws, gated)
    ws, gated = hooks.prefetch("B1", ws, gated)
    h1 = _mm_act_wr("gla_out", gated, ws["gout"], lay["gout"], add=x)
    ws = hooks.need("B1", ws, h1)

    def ffn_fwd(l, h, own=None, prefetch=None):
        nonlocal ws
        hn = _rms_fwd(f"rms_ffn{l}", h, norms[f"ffn{l}"])
        if own is not None:
            ws, hn = hooks.prefetch(own, ws, hn)
        up = _mm_act_wc(f"ffn_up{l}", hn, ws[f"up{l}"], lay[f"up{l}"], BF16)
        act = _glu_fwd(f"glu_fwd{l}", up, small["conv_w"][l], small["conv_b"][l])
        if own is not None:
            ws = hooks.need(own, ws, act)
        if prefetch is not None:
            ws, act = hooks.prefetch(prefetch, ws, act)
        return hn, up, act, _mm_act_wr(f"ffn_down{l}", act, ws[f"down{l}"], lay[f"down{l}"], add=h)

    hnf0, up0, act0, h2 = ffn_fwd(0, h1, own="B2", prefetch="C1")

    ws = hooks.need("C1", ws, h2)
    kvn = _rms_fwd("rms_kv", h2, norms["kv"])
    kv = _mm_act_wc("kv_proj", kvn, ws["wkv"], lay["wkv"], BF16)
    hn1 = _rms_fwd("rms_attn1", h2, norms["attn1"])
    q_all = _mm_act_wc("q_proj", hn1, ws["wq"], lay["wq"], BF16)
    views = [_att_views(q_all, kv, g) for g in range(3)]
    branch = [_att_fwd(views[g], g) for g in range(3)]
    ws, lse2 = hooks.prefetch("C2", ws, branch[-1][1])
    o_att, o_att_b, lse = _att_merge([br[0] for br in branch], [br[1] for br in branch[:-1]] + [lse2])
    h3 = _mm_act_wr("att_out", o_att_b, ws["dout"], lay["dout"], add=h2)
    ws = hooks.need("C2", ws, h3)
    hnf1, up1, act1, h4 = ffn_fwd(1, h3)

    dh4, d_final, loss = _loss_head(h4, norms["final"], target)

    sm = {"final": d_final}

    def ffn_bwd(l, dh, h, hn, up, act):
        big = {}
        dact = _mm_dact_wrT(f"ffn_down_dx{l}", dh, ws[f"down{l}"], lay[f"down{l}"])
        big[f"down{l}"] = _mm_grad_wr(f"ffn_down_dw{l}", act, dh, lay[f"down{l}"])
        du, dgc, w0, w1, w2, db = _glu_bwd_a(f"glu_bwd_a{l}", dact, up, small["conv_w"][l], small["conv_b"][l])
        sm[f"conv_w{l}"] = (w0, w1, w2)
        sm[f"conv_b{l}"] = db
        dup = hooks.tick(_glu_bwd_b(f"glu_bwd_b{l}", du, dgc, small["conv_w"][l]))
        dhn = _mm_dact_wcT(f"ffn_up_dx{l}", dup, ws[f"up{l}"], lay[f"up{l}"])
        big[f"up{l}"] = _mm_grad_wc(f"ffn_up_dw{l}", hn, dup, lay[f"up{l}"])
        dh_in, (sm[f"ffn{l}"],) = _rms_bwd(f"rms_ffn_bwd{l}", [dhn], h, [norms[f"ffn{l}"]], dh)
        return hooks.reduce(f"ffn{l}", big, dh_in)

    dh3 = ffn_bwd(1, dh4, h3, hnf1, up1, act1)

    big = {}
    do_att = _mm_dact_wrT("att_out_dx", dh3, ws["dout"], lay["dout"])
    big["dout"] = _mm_grad_wr("att_out_dw", o_att_b, dh3, lay["dout"])
    delta = _att_delta(do_att, o_att)
    bw = [_att_bwd(views[g], delta, lse, do_att, g) for g in range(3)]
    dq_all = jnp.concatenate([t[0] for t in bw], axis=1)
    dhn1 = _mm_dact_wcT("q_proj_dx", dq_all, ws["wq"], lay["wq"])
    big["wq"] = _mm_grad_wc("q_proj_dw", hn1, dq_all, lay["wq"])
    dkv = hooks.tick(_kv_grad_sum([t[1] for t in bw], [t[2] for t in bw]))
    dkvn = _mm_dact_wcT("kv_proj_dx", dkv, ws["wkv"], lay["wkv"])
    big["wkv"] = _mm_grad_wc("kv_proj_dw", kvn, dkv, lay["wkv"])
    dh2, (sm["attn1"], sm["kv"]) = _rms_bwd("rms_h2_bwd", [dhn1, dkvn], h2, [norms["attn1"], norms["kv"]], dh3)
    dh2 = hooks.reduce("att", big, dh2)

    dh1 = ffn_bwd(0, dh2, h1, hnf0, up0, act0)

    big = {}
    dgated = _mm_dact_wrT("gla_out_dx", dh1, ws["gout"], lay["gout"])
    big["gout"] = _mm_grad_wr("gla_out_dw", gated, dh1, lay["gout"])
    do_gla, dr, sm["head_norm"] = _gla_out_bwd(dgated, o_gla, proj, small["head_norm"])
    dq, dk, dv, dcum = _gla_bwd(proj, cum, states, hooks.tick(do_gla))
    da, sm["w_a2p"], sm["b_a2"] = _gla_gate_bwd(dcum, ga, a, small["w_a2p"])
    dproj = jnp.concatenate([dq, dk, dv, dr], axis=1)
    dhn0 = _mm_plain("gla_proj_dx", dproj, w_main, NT, F32)
    dhn0 = _mm_plain("gla_proj_a_dx", da, w_a, NT, F32, add=dhn0)
    gin_main = _mm_plain("gla_proj_dw", hn0, dproj, TN, BF16)
    gin_a = _mm_plain("gla_proj_a_dw", hn0, da, TN, BF16)
    big["gin"] = _pack_gin_grad(gin_main, gin_a)
    grad_x, (sm["attn0"],) = _rms_bwd("rms_attn0_bwd", [dhn0], x, [norms["attn0"]], dh1)
    return loss, grad_x, sm, big


def _pack_weights(chip, names, gla_w_in, gla_w_out, w_kv, dsa_w_q, dsa_w_out, ffn_w_up, ffn_w_down):
    gin = gla_w_in[0]
    gin = jnp.pad(gin, ((0, 0), (0, _roundup(gin.shape[1], LANE) - gin.shape[1])))
    shards = {"gin": gin, "gout": gla_w_out[0], "up0": ffn_w_up[0], "up1": ffn_w_up[1], "down0": ffn_w_down[0],
              "down1": ffn_w_down[1], "wq": dsa_w_q[0], "wkv": w_kv, "dout": dsa_w_out[0]}
    out = {}
    for name in names:
        w = shards[name]
        buf = jnp.zeros((N_CHIPS,) + w.shape, BF16)
        out[name] = lax.dynamic_update_slice(buf, w.astype(BF16)[None], (chip, 0, 0))
    return out


def _unpack_gin(w_gin):
    w = _layout()["gin"][1]
    d = w_gin.shape[1]
    wp = w_gin.shape[2]
    n_main = 2 * GLA_KEY_DIM + 2 * GLA_VAL_DIM
    tm = _tile(d, (256, 128, 64, 32, 16))

    def body(s_ref, main_ref, a_ref):
        full = jnp.concatenate([s_ref[s][:, :w] for s in range(N_CHIPS)], axis=1)
        main_ref[...] = full[:, :n_main]
        a_ref[...] = jnp.concatenate([full[:, n_main:], jnp.zeros((tm, A_PAD - GATE_RANK), full.dtype)], axis=1)

    return pl.pallas_call(
        body, name="unpack_gin", grid=(d // tm,), in_specs=[pl.BlockSpec((N_CHIPS, tm, wp), lambda i: (0, i, 0))],
        out_specs=(pl.BlockSpec((tm, n_main), lambda i: (i, 0)), pl.BlockSpec((tm, A_PAD), lambda i: (i, 0))),
        out_shape=(jax.ShapeDtypeStruct((d, n_main), w_gin.dtype), jax.ShapeDtypeStruct((d, A_PAD), w_gin.dtype)),
        compiler_params=_params(1))(w_gin)


def _pack_gin_grad(gin_main, gin_a):
    w = _layout()["gin"][1]
    wp = _roundup(w, LANE)
    d, n_main = gin_main.shape
    tm = _tile(d, (256, 128, 64, 32, 16))

    def body(main_ref, a_ref, o_ref):
        full = jnp.concatenate([main_ref[...], a_ref[:, :GATE_RANK]], axis=1)
        fill = jnp.zeros((tm, wp - w), full.dtype)
        for s in range(N_CHIPS):
            o_ref[s] = jnp.concatenate([full[:, s * w:(s + 1) * w], fill], axis=1)

    return pl.pallas_call(
        body, name="pack_gin_grad", grid=(d // tm,),
        in_specs=[pl.BlockSpec((tm, n_main), lambda i: (i, 0)), pl.BlockSpec((tm, A_PAD), lambda i: (i, 0))],
        out_specs=pl.BlockSpec((N_CHIPS, tm, wp), lambda i: (0, i, 0)),
        out_shape=jax.ShapeDtypeStruct((N_CHIPS, d, wp), gin_main.dtype), compiler_params=_params(1))(gin_main, gin_a)


def _small_params(attn_norm, ffn_norm, kv_norm, final_norm, conv_b, w_a2, b_a2, head_norm, conv_w):
    norms = {"attn0": attn_norm[0:1], "attn1": attn_norm[1:2], "ffn0": ffn_norm[0:1], "ffn1": ffn_norm[1:2],
             "kv": kv_norm[None, :], "final": final_norm[None, :]}
    small = {"w_a2p": jnp.pad(w_a2, ((0, A_PAD - GATE_RANK), (0, 0))), "b_a2": b_a2[None, :],
             "head_norm": head_norm[None, :], "conv_w": jnp.pad(conv_w, ((0, 0), (0, 8 - conv_w.shape[1]), (0, 0))),
             "conv_b": conv_b[:, None, :]}
    return norms, small


ANY = pl.BlockSpec(memory_space=pl.ANY)


def _place():
    return lax.axis_index("x"), lax.axis_index("y"), lax.axis_index("c")


def _other_chips(x, y):
    return [(1 - x, y), (x, 1 - y), (1 - x, 1 - y)]


def _rcopy(src, dst, ssem, rsem, dev):
    return pltpu.make_async_remote_copy(src_ref=src, dst_ref=dst, send_sem=ssem, recv_sem=rsem, device_id=dev,
                                        device_id_type=MESH)


def _pack_shard(name, w, layer, chip_arr, after):
    rows, cols = w.shape[-2:]
    tr = _tile(rows, (512, 352, 256, 128, 64, 32, 16))

    def body(p_ref, w_ref, after_ref, o_ref):
        o_ref[...] = w_ref[...].astype(o_ref.dtype)

    if w.ndim == 3:
        w_spec = pl.BlockSpec((None, tr, cols), lambda i, p: (layer, i, 0))
    else:
        w_spec = pl.BlockSpec((tr, cols), lambda i, p: (i, 0))
    return pl.pallas_call(
        body, name=name,
        grid_spec=pltpu.PrefetchScalarGridSpec(
            num_scalar_prefetch=1, grid=(rows // tr,), in_specs=[w_spec, ANY],
            out_specs=pl.BlockSpec((None, tr, cols), lambda i, p: (p[0], i, 0))),
        out_shape=jax.ShapeDtypeStruct((N_CHIPS, rows, cols), BF16), compiler_params=_params(1))(chip_arr, w, after)


def _swap_halves(name, arrs):
    n = len(arrs)

    def body(*refs):
        ins, outs = refs[:n], refs[n:2 * n]
        send, recv = refs[2 * n:]
        x, y, c = _place()
        cps = []
        for a in range(n):
            h = ins[a].shape[1] // 2
            cp = _rcopy(ins[a].at[:, pl.ds((1 - c) * h, h)], outs[a], send.at[a], recv.at[a], (x, y, 1 - c))
            cp.start()
            cps.append(cp)
        for cp in cps:
            cp.wait()

    return pl.pallas_call(
        body, name=name, in_specs=[ANY] * n, out_specs=[ANY] * n,
        out_shape=[jax.ShapeDtypeStruct((a.shape[0], a.shape[1] // 2, a.shape[2]), a.dtype) for a in arrs],
        scratch_shapes=[pltpu.SemaphoreType.DMA((n,)), pltpu.SemaphoreType.DMA((n,))])(*arrs)


SEM = pl.BlockSpec(memory_space=pltpu.SEMAPHORE)
EFFECT = pltpu.SideEffectType.DATAFLOW_SIDE_EFFECTING


def _shapes(arrs):
    return [jax.ShapeDtypeStruct(a.shape, a.dtype) for a in arrs]


def _gather_start(name, thru, arrs):
    n, nt = len(arrs), len(thru)

    def body(*refs):
        ins = refs[nt:nt + n]
        send, recv = refs[nt + n], refs[nt + n + 1]
        outs = refs[2 * nt + n + 2:]
        x, y, c = _place()
        me = 2 * x + y
        for a in range(n):
            h = ins[a].shape[1] // 2
            mine = pl.ds(c * h, h)
            for j, (px, py) in enumerate(_other_chips(x, y)):
                _rcopy(ins[a].at[me, mine], outs[a].at[me, mine], send.at[3 * a + j], recv.at[3 * a + j], (px, py, c)).start()

    res = pl.pallas_call(
        body, name=name, in_specs=[ANY] * (nt + n), out_specs=[SEM, SEM] + [ANY] * (nt + n),
        out_shape=[pltpu.SemaphoreType.DMA((3 * n,)), pltpu.SemaphoreType.DMA((3 * n,))] + _shapes(thru) + _shapes(arrs),
        input_output_aliases={i: 2 + i for i in range(nt + n)},
        compiler_params=pltpu.CompilerParams(has_side_effects=EFFECT))(*thru, *arrs)
    return res[0], res[1], res[2:2 + nt], res[2 + nt:]


def _gather_wait(name, send, recv, arrs, after):
    n = len(arrs)

    def body(*refs):
        ins = refs[:n]
        send_ref, recv_ref = refs[n], refs[n + 1]
        x, y, c = _place()
        me = 2 * x + y
        for a in range(n):
            h = ins[a].shape[1] // 2
            mine = pl.ds(c * h, h)
            for j, (px, py) in enumerate(_other_chips(x, y)):
                sent = ins[a].at[me, mine]
                landed = ins[a].at[2 * px + py, mine]
                cp = _rcopy(sent, landed, send_ref.at[3 * a + j], recv_ref.at[3 * a + j], (px, py, c))
                cp.wait_send()
                cp.wait_recv()

    after = list(after) if isinstance(after, (list, tuple)) else [after]
    return pl.pallas_call(
        body, name=name, in_specs=[ANY] * n + [SEM, SEM] + [ANY] * len(after), out_specs=[ANY] * n,
        out_shape=_shapes(arrs), input_output_aliases={a: a for a in range(n)},
        compiler_params=pltpu.CompilerParams(has_side_effects=EFFECT))(*arrs, send, recv, *after)


def _forward_halves(name, arrs):
    n = len(arrs)

    def body(*refs):
        ins, outs = refs[:n], refs[n:2 * n]
        send, recv = refs[2 * n:]
        x, y, c = _place()
        sib = (x, y, 1 - c)
        chips = _other_chips(x, y)
        cps = []
        for a in range(n):
            h = ins[a].shape[1] // 2
            mine = pl.ds(c * h, h)
            for j, (px, py) in enumerate(chips):
                cp = _rcopy(ins[a].at[2 * px + py, mine], outs[a].at[2 * px + py, mine], send.at[3 * a + j],
                            recv.at[3 * a + j], sib)
                cp.start()
                cps.append(cp)
        for a in range(n):
            h = ins[a].shape[1] // 2
            theirs = pl.ds((1 - c) * h, h)
            for j, (px, py) in enumerate(chips):
                got = outs[a].at[2 * px + py, theirs]
                _rcopy(got, got, send.at[3 * a + j], recv.at[3 * a + j], sib).wait_recv()
        for cp in cps:
            cp.wait_send()

    return pl.pallas_call(
        body, name=name, in_specs=[ANY] * n, out_specs=[ANY] * n, out_shape=_shapes(arrs),
        input_output_aliases={a: a for a in range(n)},
        scratch_shapes=[pltpu.SemaphoreType.DMA((3 * n,)), pltpu.SemaphoreType.DMA((3 * n,))])(*arrs)


def _forward_start(name, thru, arrs):
    n, nt = len(arrs), len(thru)

    def body(*refs):
        ins = refs[nt:nt + n]
        send, recv = refs[nt + n], refs[nt + n + 1]
        outs = refs[2 * nt + n + 2:]
        x, y, c = _place()
        for a in range(n):
            h = ins[a].shape[1] // 2
            mine = pl.ds(c * h, h)
            for j, (px, py) in enumerate(_other_chips(x, y)):
                _rcopy(ins[a].at[2 * px + py, mine], outs[a].at[2 * px + py, mine], send.at[3 * a + j], recv.at[3 * a + j],
                       (x, y, 1 - c)).start()

    res = pl.pallas_call(
        body, name=name, in_specs=[ANY] * (nt + n), out_specs=[SEM, SEM] + [ANY] * (nt + n),
        out_shape=[pltpu.SemaphoreType.DMA((3 * n,)), pltpu.SemaphoreType.DMA((3 * n,))] + _shapes(thru) + _shapes(arrs),
        input_output_aliases={i: 2 + i for i in range(nt + n)},
        compiler_params=pltpu.CompilerParams(has_side_effects=EFFECT))(*thru, *arrs)
    return res[0], res[1], res[2:2 + nt], res[2 + nt:]


def _forward_wait(name, send, recv, arrs, after):
    n = len(arrs)

    def body(*refs):
        ins = refs[:n]
        send_ref, recv_ref = refs[n], refs[n + 1]
        x, y, c = _place()
        for a in range(n):
            h = ins[a].shape[1] // 2
            for j, (px, py) in enumerate(_other_chips(x, y)):
                sent = ins[a].at[2 * px + py, pl.ds(c * h, h)]
                got = ins[a].at[2 * px + py, pl.ds((1 - c) * h, h)]
                cp = _rcopy(sent, got, send_ref.at[3 * a + j], recv_ref.at[3 * a + j], (x, y, 1 - c))
                cp.wait_send()
                cp.wait_recv()

    return pl.pallas_call(
        body, name=name, in_specs=[ANY] * n + [SEM, SEM, ANY], out_specs=[ANY] * n, out_shape=_shapes(arrs),
        input_output_aliases={a: a for a in range(n)},
        compiler_params=pltpu.CompilerParams(has_side_effects=EFFECT))(*arrs, send, recv, after)


def _scatter_start(name, thru, arrs):
    n, nt = len(arrs), len(thru)
    landing = [lax.empty(a.shape, a.dtype) for a in arrs]

    def body(*refs):
        ins = refs[nt:nt + n]
        send, recv = refs[nt + 2 * n], refs[nt + 2 * n + 1]
        outs = refs[2 * nt + 3 * n + 2:]
        x, y, c = _place()
        me = 2 * x + y
        for a in range(n):
            for j, (px, py) in enumerate(_other_chips(x, y)):
                _rcopy(ins[a].at[2 * px + py], outs[a].at[me], send.at[3 * a + j], recv.at[3 * a + j], (px, py, c)).start()

    res = pl.pallas_call(
        body, name=name, in_specs=[ANY] * (nt + 2 * n), out_specs=[SEM, SEM] + [ANY] * (nt + 2 * n),
        out_shape=[pltpu.SemaphoreType.DMA((3 * n,)), pltpu.SemaphoreType.DMA((3 * n,))] + _shapes(thru) + _shapes(arrs)
        + _shapes(landing),
        input_output_aliases={i: 2 + i for i in range(nt + 2 * n)},
        compiler_params=pltpu.CompilerParams(has_side_effects=EFFECT))(*thru, *arrs, *landing)
    return res[0], res[1], res[2:2 + nt], res[2 + nt:2 + nt + n], res[2 + nt + n:]


def _scatter_wait(name, send, recv, arrs, landing, after):
    n = len(arrs)

    def body(*refs):
        ins, land = refs[:n], refs[n:2 * n]
        send_ref, recv_ref = refs[2 * n], refs[2 * n + 1]
        x, y, c = _place()
        for a in range(n):
            for j, (px, py) in enumerate(_other_chips(x, y)):
                cp = _rcopy(ins[a].at[2 * px + py], land[a].at[2 * px + py], send_ref.at[3 * a + j], recv_ref.at[3 * a + j],
                            (px, py, c))
                cp.wait_send()
                cp.wait_recv()

    res = pl.pallas_call(
        body, name=name, in_specs=[ANY] * (2 * n) + [SEM, SEM, ANY], out_specs=[ANY] * (2 * n),
        out_shape=_shapes(arrs) + _shapes(landing), input_output_aliases={i: i for i in range(2 * n)},
        compiler_params=pltpu.CompilerParams(has_side_effects=EFFECT))(*arrs, *landing, send, recv, after)
    return res[:n], res[n:]


def _swap_start(name, thru, arrs):
    n, nt = len(arrs), len(thru)
    landing = [lax.empty((a.shape[0], a.shape[1] // 2, a.shape[2]), a.dtype) for a in arrs]

    def body(*refs):
        ins = refs[nt:nt + n]
        send, recv = refs[nt + 2 * n], refs[nt + 2 * n + 1]
        outs = refs[2 * nt + 3 * n + 2:]
        x, y, c = _place()
        for a in range(n):
            h = ins[a].shape[1] // 2
            _rcopy(ins[a].at[:, pl.ds((1 - c) * h, h)], outs[a], send.at[a], recv.at[a], (x, y, 1 - c)).start()

    res = pl.pallas_call(
        body, name=name, in_specs=[ANY] * (nt + 2 * n), out_specs=[SEM, SEM] + [ANY] * (nt + 2 * n),
        out_shape=[pltpu.SemaphoreType.DMA((n,)), pltpu.SemaphoreType.DMA((n,))] + _shapes(thru) + _shapes(arrs)
        + _shapes(landing),
        input_output_aliases={i: 2 + i for i in range(nt + 2 * n)},
        compiler_params=pltpu.CompilerParams(has_side_effects=EFFECT))(*thru, *arrs, *landing)
    return res[0], res[1], res[2:2 + nt], res[2 + nt:2 + nt + n], res[2 + nt + n:]


def _swap_wait(name, send, recv, arrs, landing, after):
    n = len(arrs)

    def body(*refs):
        ins, land = refs[:n], refs[n:2 * n]
        send_ref, recv_ref = refs[2 * n], refs[2 * n + 1]
        x, y, c = _place()
        for a in range(n):
            h = ins[a].shape[1] // 2
            cp = _rcopy(ins[a].at[:, pl.ds((1 - c) * h, h)], land[a], send_ref.at[a], recv_ref.at[a], (x, y, 1 - c))
            cp.wait_send()
            cp.wait_recv()

    res = pl.pallas_call(
        body, name=name, in_specs=[ANY] * (2 * n) + [SEM, SEM, ANY], out_specs=[ANY] * (2 * n),
        out_shape=_shapes(arrs) + _shapes(landing), input_output_aliases={i: i for i in range(2 * n)},
        compiler_params=pltpu.CompilerParams(has_side_effects=EFFECT))(*arrs, *landing, send, recv, after)
    return res[:n], res[n:]


def _join_start(name, arrs):
    n = len(arrs)

    def body(*refs):
        ins = refs[:n]
        send, recv = refs[n], refs[n + 1]
        outs = refs[n + 2:]
        x, y, c = _place()
        for a in range(n):
            h = ins[a].shape[0] // 2
            mine = pl.ds(c * h, h)
            _rcopy(ins[a].at[mine], outs[a].at[mine], send.at[a], recv.at[a], (x, y, 1 - c)).start()

    res = pl.pallas_call(
        body, name=name, in_specs=[ANY] * n, out_specs=[SEM, SEM] + [ANY] * n,
        out_shape=[pltpu.SemaphoreType.DMA((n,)), pltpu.SemaphoreType.DMA((n,))] + _shapes(arrs),
        input_output_aliases={i: 2 + i for i in range(n)},
        compiler_params=pltpu.CompilerParams(has_side_effects=EFFECT))(*arrs)
    return res[0], res[1], res[2:]


def _join_wait(name, send, recv, arrs, after):
    n = len(arrs)

    def body(*refs):
        ins = refs[:n]
        send_ref, recv_ref = refs[n], refs[n + 1]
        x, y, c = _place()
        for a in range(n):
            h = ins[a].shape[0] // 2
            cp = _rcopy(ins[a].at[pl.ds(c * h, h)], ins[a].at[pl.ds((1 - c) * h, h)], send_ref.at[a], recv_ref.at[a],
                        (x, y, 1 - c))
            cp.wait_send()
            cp.wait_recv()

    return pl.pallas_call(
        body, name=name, in_specs=[ANY] * n + [SEM, SEM, ANY], out_specs=[ANY] * n, out_shape=_shapes(arrs),
        input_output_aliases={a: a for a in range(n)},
        compiler_params=pltpu.CompilerParams(has_side_effects=EFFECT))(*arrs, send, recv, after)


def _join_halves(name, arrs):
    n = len(arrs)

    def body(*refs):
        ins, outs = refs[:n], refs[n:2 * n]
        send, recv = refs[2 * n:]
        x, y, c = _place()
        cps = []
        for a in range(n):
            h = ins[a].shape[0] // 2
            mine = pl.ds(c * h, h)
            cp = _rcopy(ins[a].at[mine], outs[a].at[mine], send.at[a], recv.at[a], (x, y, 1 - c))
            cp.start()
            cps.append(cp)
        for a in range(n):
            h = ins[a].shape[0] // 2
            got = outs[a].at[pl.ds((1 - c) * h, h)]
            _rcopy(got, got, send.at[a], recv.at[a], (x, y, 1 - c)).wait_recv()
        for cp in cps:
            cp.wait_send()

    return pl.pallas_call(
        body, name=name, in_specs=[ANY] * n, out_specs=[ANY] * n,
        out_shape=[jax.ShapeDtypeStruct(a.shape, a.dtype) for a in arrs],
        input_output_aliases={a: a for a in range(n)},
        scratch_shapes=[pltpu.SemaphoreType.DMA((n,)), pltpu.SemaphoreType.DMA((n,))])(*arrs)


def _allgather8(name, xs, reduce):
    m_per, n = xs.shape

    def body(x_ref, out_ref, *rest):
        if reduce:
            sum_ref, send, recv, lsem = rest
        else:
            send, recv, lsem = rest
        x, y, c = _place()
        me, sib = (x, y, c), (x, y, 1 - c)
        chips = _other_chips(x, y)

        def rows(px, py, pc):
            return out_ref.at[pl.ds((4 * px + 2 * py + pc) * m_per, m_per), :]

        def copy(k, block, to, src=None):
            return _rcopy(rows(*block) if src is None else src, rows(*block), send.at[k], recv.at[k], to)

        mine = pltpu.make_async_copy(x_ref, rows(*me), lsem)
        mine.start()
        first = [copy(0, me, sib, src=x_ref)]
        first += [copy(1 + j, me, (*chip, c), src=x_ref) for j, chip in enumerate(chips)]
        for cp in first:
            cp.start()
        passed = [copy(4 + j, (*chip, c), sib) for j, chip in enumerate(chips)]
        for j, chip in enumerate(chips):
            copy(1 + j, (*chip, c), me).wait_recv()
            passed[j].start()
        copy(0, sib, me).wait_recv()
        for j, chip in enumerate(chips):
            copy(4 + j, (*chip, 1 - c), me).wait_recv()
        for cp in first + passed:
            cp.wait_send()
        mine.wait()
        if reduce:
            acc = out_ref[pl.ds(0, m_per), :]
            for dev in range(1, N_DEV):
                acc = acc + out_ref[pl.ds(dev * m_per, m_per), :]
            sum_ref[...] = acc

    vm = pl.BlockSpec(memory_space=pltpu.VMEM)
    out_shape = [jax.ShapeDtypeStruct((N_DEV * m_per, n), xs.dtype)]
    if reduce:
        out_shape.append(jax.ShapeDtypeStruct((m_per, n), xs.dtype))
    return pl.pallas_call(
        body, name=name, in_specs=[vm], out_specs=[vm] * len(out_shape), out_shape=out_shape,
        scratch_shapes=[pltpu.SemaphoreType.DMA((7,)), pltpu.SemaphoreType.DMA((7,)), pltpu.SemaphoreType.DMA],
        compiler_params=pltpu.CompilerParams(vmem_limit_bytes=VMEM_LIMIT))(xs)


def _add_my_half(name, a, rb, c_arr):
    s, h, cols = rb.shape
    tr = _tile(h, (512, 352, 256, 128, 64, 32, 16))
    nt = h // tr

    def body(c_ref, a_ref, b_ref, o_ref):
        o_ref[...] = (a_ref[...].astype(F32) + b_ref[...].astype(F32)).astype(o_ref.dtype)

    return pl.pallas_call(
        body, name=name,
        grid_spec=pltpu.PrefetchScalarGridSpec(
            num_scalar_prefetch=1, grid=(s, nt),
            in_specs=[pl.BlockSpec((None, tr, cols), lambda k, i, c: (k, c[0] * nt + i, 0)),
                      pl.BlockSpec((None, tr, cols), lambda k, i, c: (k, i, 0))],
            out_specs=pl.BlockSpec((None, tr, cols), lambda k, i, c: (k, i, 0))),
        out_shape=jax.ShapeDtypeStruct(rb.shape, BF16), compiler_params=_params(2))(c_arr, a, rb)


def _sum_chips(name, own, q, place):
    s, h, cols = q.shape
    tr = _tile(h, (512, 352, 256, 128, 64, 32, 16))
    nt = h // tr

    def body(p_ref, own_ref, q1_ref, q2_ref, q3_ref, o_ref):
        acc = own_ref[...].astype(F32) + q1_ref[...].astype(F32)
        o_ref[...] = acc + q2_ref[...].astype(F32) + q3_ref[...].astype(F32)

    def slab(t):
        return pl.BlockSpec((None, tr, cols), lambda i, p: ((p[0] + t) % s, i, 0))

    return pl.pallas_call(
        body, name=name,
        grid_spec=pltpu.PrefetchScalarGridSpec(
            num_scalar_prefetch=1, grid=(nt,), in_specs=[slab(0), slab(1), slab(2), slab(3)],
            out_specs=pl.BlockSpec((tr, cols), lambda i, p: (p[1] * nt + i, 0))),
        out_shape=jax.ShapeDtypeStruct((2 * h, cols), F32), compiler_params=_params(1))(place, own, q, q, q)


def _pack_rows(parts):
    rows = []
    for p in parts:
        flat = p.reshape(-1).astype(F32)
        n = _roundup(flat.shape[0], 8 * LANE)
        rows.append(jnp.pad(flat, (0, n - flat.shape[0])).reshape(-1, LANE))
    return jnp.concatenate(rows, axis=0)


def _unpack_rows(buf, shapes):
    out, r = [], 0
    for shp in shapes:
        size = math.prod(shp)
        nr = _roundup(size, 8 * LANE) // LANE
        out.append(buf[r:r + nr].reshape(-1)[:size].reshape(shp))
        r += nr
    return out


def kernel(x, attn_norm, gla_w_in, gla_w_a2, gla_b_a2, gla_head_norm, gla_w_out, kv_norm, w_kv, dsa_w_q, dsa_w_out, ffn_norm, ffn_w_up, ffn_conv_w, ffn_conv_b, ffn_w_down, final_norm, loss_target, m_attn_norm, m_gla_w_in, m_gla_w_a2, m_gla_b_a2, m_gla_head_norm, m_gla_w_out, m_kv_norm, m_w_kv, m_dsa_w_q, m_dsa_w_out, m_ffn_norm, m_ffn_w_up, m_ffn_conv_w, m_ffn_conv_b, m_ffn_w_down, m_final_norm, v_attn_norm, v_gla_w_in, v_gla_w_a2, v_gla_b_a2, v_gla_head_norm, v_gla_w_out, v_kv_norm, v_w_kv, v_dsa_w_q, v_dsa_w_out, v_ffn_norm, v_ffn_w_up, v_ffn_conv_w, v_ffn_conv_b, v_ffn_w_down, v_final_norm):
    lay = _layout()
    d, f = D_MODEL, D_FF
    cx, cy, cc = _place()
    chip = 2 * cx + cy
    c_arr = jnp.reshape(cc, (1,)).astype(jnp.int32)
    place = jnp.stack([chip, cc]).astype(jnp.int32)

    groups = {"A": ("gin", "small"), "B0": ("gout",), "B1": ("up0",), "B2": ("down0",), "C1": ("wkv", "wq", "dout"),
              "C2": ("up1", "down1")}
    big_shards = (gla_w_in, gla_w_out, w_kv, dsa_w_q, dsa_w_out, ffn_w_up, ffn_w_down)
    ws = _pack_weights(chip, groups["A"][:1], *big_shards)
    sharded_small = [gla_w_a2[0], gla_b_a2[0], gla_head_norm[0], ffn_conv_w]
    packed = _pack_rows(sharded_small)
    packed = jnp.pad(packed, ((0, _roundup(packed.shape[0], 16) - packed.shape[0]), (0, 0)))
    ws["small"] = lax.dynamic_update_slice(jnp.zeros((N_CHIPS,) + packed.shape, F32), packed[None], (chip, 0, 0))
    send, recv, _, arrs = _gather_start("gather_a_start", [], [ws[k] for k in groups["A"]])
    chip_arr = place[:1]
    sources = {"up0": (ffn_w_up, 0), "up1": (ffn_w_up, 1), "down0": (ffn_w_down, 0), "down1": (ffn_w_down, 1),
               "wq": (dsa_w_q, 0), "wkv": (w_kv, 0), "dout": (dsa_w_out, 0), "gout": (gla_w_out, 0)}
    later = ("B0", "B1", "B2", "C1", "C2")
    for k in sum((groups[grp] for grp in later), ()):
        ws[k] = _pack_shard(f"pack_{k}", *sources[k], chip_arr, arrs[1])
    moments = [t.reshape(-1, t.shape[-1]) for t in (m_gla_w_in, v_gla_w_in)]
    arrs = _gather_wait("gather_a_wait", send, recv, arrs, [ws["dout"]] + moments)
    ws.update(zip(groups["A"], _forward_halves("forward_a", arrs)))
    in_flight = {}
    thru = [ws[k] for k in groups["A"]]
    for grp in later:
        send, recv, thru, arrs = _gather_start(f"gather_{grp.lower()}_start", thru, [ws[k] for k in groups[grp]])
        ws.update(zip(groups[grp], arrs))
        in_flight[grp] = (send, recv)
    ws.update(zip(groups["A"], thru))
    pending = []

    class _Comm:
        def prefetch(self, grp, ws, carry):
            send, recv = in_flight[grp]
            arrs = _gather_wait(f"gather_{grp.lower()}_wait", send, recv, [ws[k] for k in groups[grp]], carry)
            send, recv, thru, arrs = _forward_start(f"forward_{grp.lower()}_start", [carry], arrs)
            in_flight[grp] = (send, recv)
            return {**ws, **dict(zip(groups[grp], arrs))}, thru[0]

        def need(self, grp, ws, after):
            send, recv = in_flight[grp]
            arrs = _forward_wait(f"forward_{grp.lower()}_wait", send, recv, [ws[k] for k in groups[grp]], after)
            return {**ws, **dict(zip(groups[grp], arrs))}

        swapping = None

        def reduce(self, grp, grads, carry):
            names = list(grads)
            send, recv, thru, parts, theirs = _swap_start(f"swap_{grp}_start", [carry], [grads[k] for k in names])
            self.swapping = (grp, names, send, recv, parts, theirs)
            return thru[0]

        def tick(self, carry):
            if self.swapping is None:
                return carry
            grp, names, send, recv, parts, theirs = self.swapping
            self.swapping = None
            parts, theirs = _swap_wait(f"swap_{grp}_wait", send, recv, parts, theirs, carry)
            return self.scatter(grp, names, parts, theirs, carry)

        def scatter(self, grp, names, parts, theirs, carry):
            sums = [_add_my_half(f"add_half_{k}", a, b, c_arr) for k, a, b in zip(names, parts, theirs)]
            send, recv, thru, sums, landing = _scatter_start(f"scatter_{grp}_start", [carry], sums)
            pending.append((grp, names, send, recv, sums, landing))
            return thru[0]

        def reduce_now(self, grp, grads, carry):
            names = list(grads)
            parts = [grads[k] for k in names]
            return self.scatter(grp, names, parts, _swap_halves(f"swap_{grp}", parts), carry)

    shards = [_unpack_rows(ws["small"][s], [p.shape for p in sharded_small]) for s in range(N_CHIPS)]
    w_a2, b_a2, head_norm, conv_w = [jnp.concatenate([shards[s][k] for s in range(N_CHIPS)], axis=-1) for k in range(4)]
    norms, small = _small_params(attn_norm, ffn_norm, kv_norm, final_norm, ffn_conv_b, w_a2, b_a2, head_norm, conv_w)

    comm = _Comm()
    loss_blk, grad_x, sm, last_big = _local_step(x[0], loss_target[0], ws, norms, small, comm)

    small_parts = [loss_blk, jnp.concatenate([sm["attn0"], sm["attn1"]]), jnp.concatenate([sm["ffn0"], sm["ffn1"]]),
                   sm["kv"], sm["final"], jnp.concatenate([sm["conv_b0"], sm["conv_b1"]]),
                   sm["w_a2p"][:GATE_RANK], sm["b_a2"], sm["head_norm"],
                   jnp.stack([jnp.concatenate(sm["conv_w0"]), jnp.concatenate(sm["conv_w1"])])]
    small_shapes = [(8, LANE), (2, d), (2, d), (d,), (d,), (2, f), (GATE_RANK, GLA_KEY_DIM), (GLA_KEY_DIM,),
                    (GLA_VAL_DIM // GLA_HEADS,), (2, 3, f)]
    _, reduced = _allgather8("reduce_small", _pack_rows(small_parts), True)
    reduced = comm.reduce_now("gla", last_big, reduced)

    loss_r, g_attn, g_ffn, g_kv, g_final, g_cb, g_a2, g_ba2, g_hn, g_cw = _unpack_rows(reduced, small_shapes)
    loss = loss_r[0, 0]

    def mine(g, axis):
        w = g.shape[axis] // N_CHIPS
        return lax.dynamic_slice_in_dim(g, chip * w, w, axis)

    grads = {
        "attn_norm": g_attn, "gla_w_a2": mine(g_a2, 1)[None], "gla_b_a2": mine(g_ba2, 0)[None],
        "gla_head_norm": mine(g_hn, 0)[None], "kv_norm": g_kv, "ffn_norm": g_ffn, "ffn_conv_w": mine(g_cw, 2),
        "ffn_conv_b": g_cb, "final_norm": g_final,
    }
    weights = {"attn_norm": (attn_norm, m_attn_norm, v_attn_norm), "gla_w_in": (gla_w_in, m_gla_w_in, v_gla_w_in),
               "gla_w_a2": (gla_w_a2, m_gla_w_a2, v_gla_w_a2), "gla_b_a2": (gla_b_a2, m_gla_b_a2, v_gla_b_a2),
               "gla_head_norm": (gla_head_norm, m_gla_head_norm, v_gla_head_norm),
               "gla_w_out": (gla_w_out, m_gla_w_out, v_gla_w_out), "kv_norm": (kv_norm, m_kv_norm, v_kv_norm),
               "w_kv": (w_kv, m_w_kv, v_w_kv), "dsa_w_q": (dsa_w_q, m_dsa_w_q, v_dsa_w_q),
               "dsa_w_out": (dsa_w_out, m_dsa_w_out, v_dsa_w_out), "ffn_norm": (ffn_norm, m_ffn_norm, v_ffn_norm),
               "ffn_w_up": (ffn_w_up, m_ffn_w_up, v_ffn_w_up), "ffn_conv_w": (ffn_conv_w, m_ffn_conv_w, v_ffn_conv_w),
               "ffn_conv_b": (ffn_conv_b, m_ffn_conv_b, v_ffn_conv_b),
               "ffn_w_down": (ffn_w_down, m_ffn_w_down, v_ffn_w_down), "final_norm": (final_norm, m_final_norm, v_final_norm)}
    order = list(weights)
    big_names = ("gla_w_in", "gla_w_out", "w_kv", "dsa_w_q", "dsa_w_out", "ffn_w_up", "ffn_w_down")
    delta, new_m, new_v = {}, {}, {}

    def adam_big(k, g):
        w, m, v = weights[k]
        cols = w.shape[-1]
        res = _adamw(f"adamw_{k}", w.reshape(-1, cols), g.reshape(-1, g.shape[-1]), m.reshape(-1, cols), v.reshape(-1, cols))
        delta[k], new_m[k], new_v[k] = [r.reshape(w.shape) for r in res[:3]]
        grads[k] = res[3].reshape(w.shape) if len(res) == 4 else g
        return res[0]

    full = {}
    after = reduced
    joining = []
    for grp, names, send, recv, sums, landing in pending[:-1]:
        sums, landing = _scatter_wait(f"scatter_{grp}_wait", send, recv, sums, landing, after)
        halves = [_sum_chips(f"sum_chips_{k}", s, q, place) for k, s, q in zip(names, sums, landing)]
        send, recv, halves = _join_start(f"join_{grp}_start", halves)
        joining.append((grp, names, send, recv, halves))
        after = halves[0]
    for grp, names, send, recv, halves in joining:
        joined = _join_wait(f"join_{grp}_wait", send, recv, halves, after)
        full.update(zip(names, joined))
        after = joined[0]
    after = adam_big("w_kv", full["wkv"])
    after = adam_big("dsa_w_q", full["wq"][None])
    after = adam_big("dsa_w_out", full["dout"][None])
    for k, g0, g1 in (("ffn_w_up", "up0", "up1"), ("ffn_w_down", "down0", "down1")):
        delta[k], new_m[k], new_v[k], grads[k] = _adamw_layers(f"adamw_{k}", weights[k][0], (full[g0], full[g1]),
                                                               weights[k][1], weights[k][2])
        after = delta[k]
    grp, names, send, recv, sums, landing = pending[-1]
    sums, landing = _scatter_wait(f"scatter_{grp}_wait", send, recv, sums, landing, after)
    halves = [_sum_chips(f"sum_chips_{k}", s, q, place) for k, s, q in zip(names, sums, landing)]
    full.update(zip(names, _join_halves(f"join_{grp}", halves)))
    adam_big("gla_w_in", full["gin"])
    adam_big("gla_w_out", full["gout"][None])
    small_names = [k for k in order if k not in big_names]
    packed = [_pack_rows([src[k] for k in small_names])
              for src in ({k: weights[k][0] for k in small_names}, grads, {k: weights[k][1] for k in small_names},
                          {k: weights[k][2] for k in small_names})]
    res = _adamw("adamw_small", *packed)
    shapes = [weights[k][0].shape for k in small_names]
    for dst, buf in zip((delta, new_m, new_v), res):
        for k, val in zip(small_names, _unpack_rows(buf, shapes)):
            dst[k] = val
    return (loss, grad_x[None], *[grads[k] for k in order], *[delta[k] for k in order], *[new_m[k] for k in order],
            *[new_v[k] for k in order])
```

```python
import math

import jax
import jax.numpy as jnp
from jax import lax
from jax.experimental import pallas as pl
from jax.experimental.pallas import tpu as pltpu

F32 = jnp.float32
BF16 = jnp.bfloat16

D_MODEL = 2048
SEQ = 4096
GLA_HEADS = 4
GLA_KEY_DIM = D_MODEL // 2
GLA_VAL_DIM = D_MODEL
GATE_RANK = 16
GATE_NORMALIZER = 16.0
GLA_CHUNK = 64
ATT_HEADS = 16
HEAD_DIM = 128
WINDOWS = (128, 512, 2048)
DILATIONS = (1, 4, 16)
ATT_BLOCK = 128
D_FF = 5632
EPS = 1e-6
ADAM_LR = 0.001
ADAM_B1 = 0.9
ADAM_B2 = 0.999
ADAM_EPS = 1e-08
ADAM_WD = 0.01
ADAM_STEP = 10

N_CHIPS = 4
N_DEV = 8
LANE = 128
A_PAD = 128
VMEM_LIMIT = 56 * 1024 * 1024
MAX_K_TILE = 2816
NEG = -1e30
MESH = pl.DeviceIdType.MESH

NN = (((1,), (0,)), ((), ()))
NT = (((1,), (1,)), ((), ()))
TN = (((0,), (0,)), ((), ()))


def _tile(n, cands):
    for c in cands:
        if c <= n and n % c == 0:
            return c
    return n


def _roundup(n, m):
    return -(-n // m) * m


def _params(n_axes):
    return pltpu.CompilerParams(dimension_semantics=("arbitrary",) * n_axes, vmem_limit_bytes=VMEM_LIMIT)


def _dot(a, b, dims):
    return lax.dot_general(a, b, dims, preferred_element_type=F32)


def _sigmoid(x):
    return 1.0 / (1.0 + jnp.exp(-x))


COL_SHARDED = ("gin", "up0", "up1", "wq", "wkv")
ROW_SHARDED = ("gout", "down0", "down1", "dout")


def _layout():
    f = D_FF
    hd = ATT_HEADS * HEAD_DIM
    gin = 2 * GLA_KEY_DIM + 2 * GLA_VAL_DIM + GATE_RANK
    up_w = 2 * f // N_CHIPS
    q_w = 3 * hd // N_CHIPS
    kv_w = 2 * hd // N_CHIPS
    dn_r = f // N_CHIPS
    go_r = GLA_VAL_DIM // N_CHIPS
    do_r = hd // N_CHIPS
    big = (1408, 1024, 512, 256, 128)
    return {
        "gin": (0, gin // N_CHIPS, LANE),
        "up0": (0, up_w, _tile(up_w, big)), "up1": (0, up_w, _tile(up_w, big)),
        "wq": (0, q_w, _tile(q_w, (1536, 768, 512, 384, 256, 128))), "wkv": (0, kv_w, _tile(kv_w, (1024, 512, 256, 128))),
        "down0": (0, dn_r, _tile(dn_r, big)), "down1": (0, dn_r, _tile(dn_r, big)),
        "gout": (0, go_r, _tile(go_r, (512, 256, 128))), "dout": (0, do_r, _tile(do_r, (512, 256, 128))),
    }


def _matmul(name, a, b, dims, grid, a_spec, b_spec, o_spec, out_shape, acc_shape, add=None, add_spec=None):
    nk = grid[2]
    has_add = add is not None

    def body(*refs):
        a_ref, b_ref = refs[0], refs[1]
        pos = 2
        add_ref = None
        if has_add:
            add_ref = refs[pos]
            pos += 1
        o_ref = refs[pos]
        prod = _dot(a_ref[...].astype(BF16), b_ref[...].astype(BF16), dims)

        def finish(val):
            if has_add:
                val = val + add_ref[...].astype(F32)
            o_ref[...] = val.astype(o_ref.dtype)

        if nk == 1:
            finish(prod)
        else:
            acc_ref = refs[pos + 1]
            k = pl.program_id(2)

            @pl.when(k == 0)
            def _():
                acc_ref[...] = prod

            @pl.when(k > 0)
            def _():
                acc_ref[...] += prod

            @pl.when(k == nk - 1)
            def _():
                finish(acc_ref[...])

    in_specs = [a_spec, b_spec]
    args = [a, b]
    if has_add:
        in_specs.append(add_spec)
        args.append(add)
    scratch = [] if nk == 1 else [pltpu.VMEM(acc_shape, F32)]
    return pl.pallas_call(body, name=name, grid=grid, in_specs=in_specs, out_specs=o_spec, out_shape=out_shape,
                          scratch_shapes=scratch, compiler_params=_params(3))(*args)


def _mm_act_wc(name, a, wc, seg, out_dtype):
    off, w, tn = seg
    if off == 0 and w <= MAX_K_TILE:
        tn = w
    t_len, d = a.shape
    tm = _tile(t_len, (1024, 512, 256, 128))
    nps = w // tn
    ob = off // tn
    grid = (t_len // tm, N_CHIPS * nps, 1)
    return _matmul(
        name, a, wc, NN, grid,
        pl.BlockSpec((tm, d), lambda i, j, k: (i, 0)),
        pl.BlockSpec((None, d, tn), lambda i, j, k: (j // nps, 0, ob + j % nps)),
        pl.BlockSpec((tm, tn), lambda i, j, k: (i, j)),
        jax.ShapeDtypeStruct((t_len, N_CHIPS * w), out_dtype), (tm, tn))


def _mm_dact_wcT(name, dy, wc, seg, add=None):
    off, w, tk = seg
    if off == 0 and w <= MAX_K_TILE:
        tk = w
    t_len = dy.shape[0]
    d = wc.shape[1]
    tm = _tile(t_len, (1024, 512, 256, 128))
    tn = _tile(d, (1024, 512, 256, 128))
    kps = w // tk
    ob = off // tk
    grid = (t_len // tm, d // tn, N_CHIPS * kps)
    return _matmul(
        name, dy, wc, NT, grid,
        pl.BlockSpec((tm, tk), lambda i, j, k: (i, k)),
        pl.BlockSpec((None, tn, tk), lambda i, j, k: (k // kps, j, ob + k % kps)),
        pl.BlockSpec((tm, tn), lambda i, j, k: (i, j)),
        jax.ShapeDtypeStruct((t_len, d), F32), (tm, tn),
        add=add, add_spec=None if add is None else pl.BlockSpec((tm, tn), lambda i, j, k: (i, j)))


def _mm_grad_wc(name, a, dy, seg):
    _, w, tn = seg
    t_len, d = a.shape
    tm = _tile(d, (1024, 512, 256, 128))
    tk = _tile(t_len, (2048, 1024, 512, 256, 128))
    nps = w // tn
    grid = (d // tm, N_CHIPS * nps, t_len // tk)
    return _matmul(
        name, a, dy, TN, grid,
        pl.BlockSpec((tk, tm), lambda i, j, k: (k, i)),
        pl.BlockSpec((tk, tn), lambda i, j, k: (k, j)),
        pl.BlockSpec((None, tm, tn), lambda i, j, k: (j // nps, i, j % nps)),
        jax.ShapeDtypeStruct((N_CHIPS, d, w), BF16), (tm, tn))


def _is_plain(wr, seg):
    return seg[0] == 0 and wr.shape[1] == seg[1] and (N_CHIPS * seg[1]) % 1024 == 0


def _mm_act_wr(name, a, wr, seg, add):
    off, r, tk = seg
    t_len = a.shape[0]
    d = wr.shape[2]
    if seg[0] == 0 and wr.shape[1] == r:
        return _mm_plain(name, a, wr.reshape(N_CHIPS * r, d), NN, F32, add=add)
    tm = _tile(t_len, (1024, 512, 256, 128))
    tn = _tile(d, (1024, 512, 256, 128))
    kps = r // tk
    ob = off // tk
    grid = (t_len // tm, d // tn, N_CHIPS * kps)
    return _matmul(
        name, a, wr, NN, grid,
        pl.BlockSpec((tm, tk), lambda i, j, k: (i, k)),
        pl.BlockSpec((None, tk, tn), lambda i, j, k: (k // kps, ob + k % kps, j)),
        pl.BlockSpec((tm, tn), lambda i, j, k: (i, j)),
        jax.ShapeDtypeStruct((t_len, d), F32), (tm, tn),
        add=add, add_spec=pl.BlockSpec((tm, tn), lambda i, j, k: (i, j)))


def _mm_dact_wrT(name, dh, wr, seg):
    off, r, tn = seg
    t_len, d = dh.shape
    if _is_plain(wr, seg):
        return _mm_plain(name, dh, wr.reshape(N_CHIPS * r, d), NT, BF16)
    tm = _tile(t_len, (1024, 512, 256, 128))
    nps = r // tn
    ob = off // tn
    grid = (t_len // tm, N_CHIPS * nps, 1)
    return _matmul(
        name, dh, wr, NT, grid,
        pl.BlockSpec((tm, d), lambda i, j, k: (i, 0)),
        pl.BlockSpec((None, tn, d), lambda i, j, k: (j // nps, ob + j % nps, 0)),
        pl.BlockSpec((tm, tn), lambda i, j, k: (i, j)),
        jax.ShapeDtypeStruct((t_len, N_CHIPS * r), BF16), (tm, tn))


def _mm_grad_wr(name, a, dh, seg):
    _, r, tm = seg
    t_len, d = dh.shape
    if (N_CHIPS * r) % 1024 == 0:
        return _mm_plain(name, a, dh, TN, BF16).reshape(N_CHIPS, r, d)
    tn = _tile(d, (1024, 512, 256, 128))
    tk = _tile(t_len, (2048, 1024, 512, 256, 128))
    mps = r // tm
    grid = (N_CHIPS * mps, d // tn, t_len // tk)
    return _matmul(
        name, a, dh, TN, grid,
        pl.BlockSpec((tk, tm), lambda i, j, k: (k, i)),
        pl.BlockSpec((tk, tn), lambda i, j, k: (k, j)),
        pl.BlockSpec((None, tm, tn), lambda i, j, k: (i // mps, i % mps, j)),
        jax.ShapeDtypeStruct((N_CHIPS, r, d), BF16), (tm, tn))


def _mm_plain(name, a, b, dims, out_dtype, add=None):
    if dims == NN:
        m, kd = a.shape
        n = b.shape[1]
    elif dims == NT:
        m, kd = a.shape
        n = b.shape[0]
    else:
        kd, m = a.shape
        n = b.shape[1]
    tm = _tile(m, (1024, 512, 256, 128))
    tn = _tile(n, (1024, 768, 512, 256, 128))
    tk = _tile(kd, (MAX_K_TILE, 2048, 1408, 1024, 512, 256, 128))
    grid = (m // tm, n // tn, kd // tk)
    if dims == NN:
        a_spec = pl.BlockSpec((tm, tk), lambda i, j, k: (i, k))
        b_spec = pl.BlockSpec((tk, tn), lambda i, j, k: (k, j))
    elif dims == NT:
        a_spec = pl.BlockSpec((tm, tk), lambda i, j, k: (i, k))
        b_spec = pl.BlockSpec((tn, tk), lambda i, j, k: (j, k))
    else:
        a_spec = pl.BlockSpec((tk, tm), lambda i, j, k: (k, i))
        b_spec = pl.BlockSpec((tk, tn), lambda i, j, k: (k, j))
    o_spec = pl.BlockSpec((tm, tn), lambda i, j, k: (i, j))
    return _matmul(name, a, b, dims, grid, a_spec, b_spec, o_spec, jax.ShapeDtypeStruct((m, n), out_dtype), (tm, tn),
                   add=add, add_spec=None if add is None else o_spec)


def _rms_fwd(name, x, g):
    t_len, d = x.shape
    tm = _tile(t_len, (512, 256, 128))

    def body(x_ref, g_ref, o_ref):
        xv = x_ref[...]
        r = lax.rsqrt(jnp.mean(xv * xv, axis=-1, keepdims=True) + EPS)
        o_ref[...] = (xv * r * g_ref[...]).astype(o_ref.dtype)

    return pl.pallas_call(
        body, name=name, grid=(t_len // tm,),
        in_specs=[pl.BlockSpec((tm, d), lambda i: (i, 0)), pl.BlockSpec((1, d), lambda i: (0, 0))],
        out_specs=pl.BlockSpec((tm, d), lambda i: (i, 0)),
        out_shape=jax.ShapeDtypeStruct((t_len, d), BF16), compiler_params=_params(1))(x, g)


def _rms_bwd(name, dys, x, gs, dres):
    t_len, d = x.shape
    n = len(dys)
    tm = _tile(t_len, (256, 128))

    def body(*refs):
        dy_refs, x_ref, g_refs = refs[:n], refs[n], refs[n + 1:2 * n + 1]
        dres_ref, dx_ref, dg_refs = refs[2 * n + 1], refs[2 * n + 2], refs[2 * n + 3:]
        xv = x_ref[...]
        r = lax.rsqrt(jnp.mean(xv * xv, axis=-1, keepdims=True) + EPS)
        xhat = xv * r
        dyv = [ref[...].astype(F32) for ref in dy_refs]
        dxn = dyv[0] * g_refs[0][...]
        for k in range(1, n):
            dxn = dxn + dyv[k] * g_refs[k][...]
        dx = r * (dxn - xhat * jnp.mean(dxn * xhat, axis=-1, keepdims=True))
        dx_ref[...] = dres_ref[...] + dx
        parts = [jnp.sum(v * xhat, axis=0, keepdims=True) for v in dyv]

        @pl.when(pl.program_id(0) == 0)
        def _():
            for ref, p in zip(dg_refs, parts):
                ref[...] = p

        @pl.when(pl.program_id(0) > 0)
        def _():
            for ref, p in zip(dg_refs, parts):
                ref[...] += p

    row = pl.BlockSpec((tm, d), lambda i: (i, 0))
    vec = pl.BlockSpec((1, d), lambda i: (0, 0))
    res = pl.pallas_call(
        body, name=name, grid=(t_len // tm,), in_specs=[row] * (n + 1) + [vec] * n + [row], out_specs=(row,) + (vec,) * n,
        out_shape=(jax.ShapeDtypeStruct((t_len, d), F32),) + (jax.ShapeDtypeStruct((1, d), F32),) * n,
        compiler_params=_params(1))(*dys, x, *gs, dres)
    return res[0], res[1:]


def _loss_head(h, g, target):
    t_len, d = h.shape
    tm = _tile(t_len, (256, 128))

    def body(h_ref, g_ref, t_ref, dh_ref, dg_ref, loss_ref):
        xv = h_ref[...]
        gv = g_ref[...]
        r = lax.rsqrt(jnp.mean(xv * xv, axis=-1, keepdims=True) + EPS)
        xhat = xv * r
        err = xhat * gv - t_ref[...]
        dyv = err * (1.0 / d)
        dxn = dyv * gv
        dh_ref[...] = r * (dxn - xhat * jnp.mean(dxn * xhat, axis=-1, keepdims=True))
        part = jnp.sum(dyv * xhat, axis=0, keepdims=True)
        lpart = jnp.zeros((8, LANE), F32) + (0.5 / d) * jnp.sum(err * err)

        @pl.when(pl.program_id(0) == 0)
        def _():
            dg_ref[...] = part
            loss_ref[...] = lpart

        @pl.when(pl.program_id(0) > 0)
        def _():
            dg_ref[...] += part
            loss_ref[...] += lpart

    row = pl.BlockSpec((tm, d), lambda i: (i, 0))
    vec = pl.BlockSpec((1, d), lambda i: (0, 0))
    return pl.pallas_call(
        body, name="loss_head", grid=(t_len // tm,), in_specs=[row, vec, row],
        out_specs=(row, vec, pl.BlockSpec((8, LANE), lambda i: (0, 0))),
        out_shape=(jax.ShapeDtypeStruct((t_len, d), F32), jax.ShapeDtypeStruct((1, d), F32),
                   jax.ShapeDtypeStruct((8, LANE), F32)),
        compiler_params=_params(1))(h, g, target)


def _chunk_row(shape):
    return lax.broadcasted_iota(jnp.int32, shape, 0) % GLA_CHUNK


def _gla_gate_fwd(a, w_a2p, b_a2):
    t_len = a.shape[0]
    kd = w_a2p.shape[1]
    tm = _tile(t_len, (256, 128, 64))

    def body(a_ref, w_ref, b_ref, ga_ref, cum_ref):
        ga = _dot(a_ref[...], w_ref[...].astype(BF16), NN) + b_ref[...]
        ga_ref[...] = ga
        la = (jnp.minimum(ga, 0.0) - jnp.log(1.0 + jnp.exp(-jnp.abs(ga)))) * (1.0 / GATE_NORMALIZER)
        row = _chunk_row(la.shape)
        s = 1
        while s < GLA_CHUNK:
            la = la + jnp.where(row >= s, pltpu.roll(la, s, 0), 0.0)
            s *= 2
        cum_ref[...] = la

    return pl.pallas_call(
        body, name="gla_gate_fwd", grid=(t_len // tm,),
        in_specs=[pl.BlockSpec((tm, A_PAD), lambda i: (i, 0)), pl.BlockSpec((A_PAD, kd), lambda i: (0, 0)),
                  pl.BlockSpec((1, kd), lambda i: (0, 0))],
        out_specs=(pl.BlockSpec((tm, kd), lambda i: (i, 0)), pl.BlockSpec((tm, kd), lambda i: (i, 0))),
        out_shape=(jax.ShapeDtypeStruct((t_len, kd), F32), jax.ShapeDtypeStruct((t_len, kd), F32)),
        compiler_params=_params(1))(a, w_a2p, b_a2)


def _gla_gate_bwd(dcum, ga, a, w_a2p):
    t_len, kd = dcum.shape
    tm = _tile(t_len, (256, 128, 64))

    def body(dc_ref, ga_ref, a_ref, w_ref, da_ref, dw_ref, db_ref):
        x = dc_ref[...]
        row = _chunk_row(x.shape)
        s = 1
        while s < GLA_CHUNK:
            x = x + jnp.where(row < GLA_CHUNK - s, pltpu.roll(x, tm - s, 0), 0.0)
            s *= 2
        dga = x * (1.0 / GATE_NORMALIZER) * _sigmoid(-ga_ref[...])
        dgab = dga.astype(BF16)
        da_ref[...] = _dot(dgab, w_ref[...].astype(BF16), NT).astype(da_ref.dtype)
        dw = _dot(a_ref[...], dgab, TN)
        db = jnp.sum(dga, axis=0, keepdims=True)

        @pl.when(pl.program_id(0) == 0)
        def _():
            dw_ref[...] = dw
            db_ref[...] = db

        @pl.when(pl.program_id(0) > 0)
        def _():
            dw_ref[...] += dw
            db_ref[...] += db

    wide = pl.BlockSpec((tm, kd), lambda i: (i, 0))
    return pl.pallas_call(
        body, name="gla_gate_bwd", grid=(t_len // tm,),
        in_specs=[wide, wide, pl.BlockSpec((tm, A_PAD), lambda i: (i, 0)), pl.BlockSpec((A_PAD, kd), lambda i: (0, 0))],
        out_specs=(pl.BlockSpec((tm, A_PAD), lambda i: (i, 0)), pl.BlockSpec((A_PAD, kd), lambda i: (0, 0)),
                   pl.BlockSpec((1, kd), lambda i: (0, 0))),
        out_shape=(jax.ShapeDtypeStruct((t_len, A_PAD), BF16), jax.ShapeDtypeStruct((A_PAD, kd), F32),
                   jax.ShapeDtypeStruct((1, kd), F32)),
        compiler_params=_params(1))(dcum, ga, a, w_a2p)


GLA_STEP_CHUNKS = 8


def _gla_dims():
    dk = GLA_KEY_DIM // GLA_HEADS
    dv = GLA_VAL_DIM // GLA_HEADS
    return dk, dv


def _gla_fwd(proj, cum):
    t_len = proj.shape[0]
    dk, dv = _gla_dims()
    nc = t_len // GLA_CHUNK
    c = GLA_CHUNK
    scale = dk ** -0.5
    v0 = 2 * GLA_KEY_DIM // dv

    per = _tile(nc, (GLA_STEP_CHUNKS, 2, 1))
    rows = per * c

    def body(q_ref, k_ref, v_ref, cum_ref, o_ref, st_ref, s_scr):
        @pl.when(pl.program_id(1) == 0)
        def _():
            s_scr[...] = jnp.zeros_like(s_scr)

        tri = lax.broadcasted_iota(jnp.int32, (c, c), 0) >= lax.broadcasted_iota(jnp.int32, (c, c), 1)
        for i in range(per):
            rs = slice(i * c, (i + 1) * c)
            cm = cum_ref[rs, :]
            last = cm[c - 1:c, :]
            q = q_ref[rs, :].astype(F32) * scale
            k = k_ref[rs, :].astype(F32)
            v = v_ref[rs, :].astype(BF16)
            qd = (q * jnp.exp(cm)).astype(BF16)
            ki = (k * jnp.exp(-cm)).astype(BF16)
            ke = (k * jnp.exp(last - cm)).astype(BF16)
            sc = jnp.where(tri, _dot(qd, ki, NT), 0.0)
            st = s_scr[...]
            st_ref[i] = st
            o_ref[rs, :] = _dot(sc.astype(BF16), v, NN) + _dot(qd, st.astype(BF16), NT)
            s_scr[...] = st * jnp.exp(last) + _dot(v, ke, TN)

    return pl.pallas_call(
        body, name="gla_fwd", grid=(GLA_HEADS, nc // per),
        in_specs=[pl.BlockSpec((rows, dk), lambda h, n: (n, h)),
                  pl.BlockSpec((rows, dk), lambda h, n: (n, GLA_HEADS + h)),
                  pl.BlockSpec((rows, dv), lambda h, n: (n, v0 + h)),
                  pl.BlockSpec((rows, dk), lambda h, n: (n, h))],
        out_specs=(pl.BlockSpec((rows, dv), lambda h, n: (n, h)),
                   pl.BlockSpec((None, per, dv, dk), lambda h, n: (h, n, 0, 0))),
        out_shape=(jax.ShapeDtypeStruct((t_len, GLA_VAL_DIM), F32),
                   jax.ShapeDtypeStruct((GLA_HEADS, nc, dv, dk), F32)),
        scratch_shapes=[pltpu.VMEM((dv, dk), F32)], compiler_params=_params(2))(proj, proj, proj, cum)


def _gla_bwd(proj, cum, states, do):
    t_len = proj.shape[0]
    dk, dv = _gla_dims()
    nc = t_len // GLA_CHUNK
    c = GLA_CHUNK
    scale = dk ** -0.5
    v0 = 2 * GLA_KEY_DIM // dv

    per = _tile(nc, (GLA_STEP_CHUNKS, 2, 1))
    rows = per * c

    def body(q_ref, k_ref, v_ref, cum_ref, st_ref, do_ref, dq_ref, dk_ref, dv_ref, dc_ref, ds_scr):
        @pl.when(pl.program_id(1) == 0)
        def _():
            ds_scr[...] = jnp.zeros_like(ds_scr)

        tri = lax.broadcasted_iota(jnp.int32, (c, c), 0) >= lax.broadcasted_iota(jnp.int32, (c, c), 1)
        row = lax.broadcasted_iota(jnp.int32, (c, dk), 0)
        for i in reversed(range(per)):
            rs = slice(i * c, (i + 1) * c)
            cm = cum_ref[rs, :]
            last = cm[c - 1:c, :]
            e_c = jnp.exp(cm)
            e_nc = jnp.exp(-cm)
            e_lc = jnp.exp(last - cm)
            e_l = jnp.exp(last)
            q = q_ref[rs, :].astype(F32) * scale
            k = k_ref[rs, :].astype(F32)
            v = v_ref[rs, :].astype(BF16)
            dov = do_ref[rs, :]
            qd32 = q * e_c
            ki32 = k * e_nc
            ke32 = k * e_lc
            qd = qd32.astype(BF16)
            ki = ki32.astype(BF16)
            ke = ke32.astype(BF16)
            st = st_ref[i]
            dst = ds_scr[...]
            dstb = dst.astype(BF16)
            am = jnp.where(tri, _dot(dov, v, NT), 0.0).astype(BF16)
            pm = jnp.where(tri, _dot(qd, ki, NT), 0.0).astype(BF16)
            dqd = _dot(am, ki, NN) + _dot(dov, st.astype(BF16), NN)
            dki = _dot(am, qd, TN)
            dvv = _dot(pm, dov, TN) + _dot(ke, dstb, NT)
            dke = _dot(v, dstb, NN)
            d_el = jnp.sum(dst * st, axis=0, keepdims=True)
            ds_scr[...] = dst * e_l + _dot(dov, qd, TN)
            dq_ref[rs, :] = (dqd * scale * e_c).astype(dq_ref.dtype)
            dk_ref[rs, :] = (dki * e_nc + dke * e_lc).astype(dk_ref.dtype)
            dv_ref[rs, :] = dvv.astype(dv_ref.dtype)
            dkeke = dke * ke32
            dcum = dqd * qd32 - dki * ki32 - dkeke
            dlast = jnp.sum(dkeke, axis=0, keepdims=True) + d_el * e_l
            dc_ref[rs, :] = jnp.where(row == c - 1, dcum + dlast, dcum)

    rev = nc // per - 1
    return pl.pallas_call(
        body, name="gla_bwd", grid=(GLA_HEADS, nc // per),
        in_specs=[pl.BlockSpec((rows, dk), lambda h, n: (rev - n, h)),
                  pl.BlockSpec((rows, dk), lambda h, n: (rev - n, GLA_HEADS + h)),
                  pl.BlockSpec((rows, dv), lambda h, n: (rev - n, v0 + h)),
                  pl.BlockSpec((rows, dk), lambda h, n: (rev - n, h)),
                  pl.BlockSpec((None, per, dv, dk), lambda h, n: (h, rev - n, 0, 0)),
                  pl.BlockSpec((rows, dv), lambda h, n: (rev - n, h))],
        out_specs=(pl.BlockSpec((rows, dk), lambda h, n: (rev - n, h)),
                   pl.BlockSpec((rows, dk), lambda h, n: (rev - n, h)),
                   pl.BlockSpec((rows, dv), lambda h, n: (rev - n, h)),
                   pl.BlockSpec((rows, dk), lambda h, n: (rev - n, h))),
        out_shape=(jax.ShapeDtypeStruct((t_len, GLA_KEY_DIM), BF16), jax.ShapeDtypeStruct((t_len, GLA_KEY_DIM), BF16),
                   jax.ShapeDtypeStruct((t_len, GLA_VAL_DIM), BF16), jax.ShapeDtypeStruct((t_len, GLA_KEY_DIM), F32)),
        scratch_shapes=[pltpu.VMEM((dv, dk), F32)], compiler_params=_params(2))(proj, proj, proj, cum, states, do)


def _gla_out_fwd(o, proj, gn):
    t_len = o.shape[0]
    _, dv = _gla_dims()
    tm = _tile(t_len, (512, 256, 128))
    r0 = (2 * GLA_KEY_DIM + GLA_VAL_DIM) // dv

    def body(o_ref, r_ref, g_ref, y_ref):
        ov = o_ref[...]
        rs = lax.rsqrt(jnp.mean(ov * ov, axis=-1, keepdims=True) + EPS)
        rv = r_ref[...].astype(F32)
        y_ref[...] = (ov * rs * g_ref[...] * (rv * _sigmoid(rv))).astype(y_ref.dtype)

    return pl.pallas_call(
        body, name="gla_out_fwd", grid=(t_len // tm, GLA_HEADS),
        in_specs=[pl.BlockSpec((tm, dv), lambda i, h: (i, h)), pl.BlockSpec((tm, dv), lambda i, h: (i, r0 + h)),
                  pl.BlockSpec((1, dv), lambda i, h: (0, 0))],
        out_specs=pl.BlockSpec((tm, dv), lambda i, h: (i, h)),
        out_shape=jax.ShapeDtypeStruct((t_len, GLA_VAL_DIM), BF16), compiler_params=_params(2))(o, proj, gn)


def _gla_out_bwd(dy, o, proj, gn):
    t_len = o.shape[0]
    _, dv = _gla_dims()
    tm = _tile(t_len, (512, 256, 128))
    r0 = (2 * GLA_KEY_DIM + GLA_VAL_DIM) // dv

    def body(dy_ref, o_ref, r_ref, g_ref, do_ref, dr_ref, dg_ref):
        ov = o_ref[...]
        gv = g_ref[...]
        rs = lax.rsqrt(jnp.mean(ov * ov, axis=-1, keepdims=True) + EPS)
        xhat = ov * rs
        rv = r_ref[...].astype(F32)
        sg = _sigmoid(rv)
        gate = rv * sg
        dyv = dy_ref[...].astype(F32)
        dn = dyv * gate
        dr_ref[...] = (dyv * xhat * gv * (sg * (1.0 + rv * (1.0 - sg)))).astype(dr_ref.dtype)
        dxn = dn * gv
        do_ref[...] = (rs * (dxn - xhat * jnp.mean(dxn * xhat, axis=-1, keepdims=True))).astype(do_ref.dtype)
        part = jnp.sum(dn * xhat, axis=0, keepdims=True)
        first = (pl.program_id(0) == 0) & (pl.program_id(1) == 0)

        @pl.when(first)
        def _():
            dg_ref[...] = part

        @pl.when(jnp.logical_not(first))
        def _():
            dg_ref[...] += part

    blk = pl.BlockSpec((tm, dv), lambda i, h: (i, h))
    return pl.pallas_call(
        body, name="gla_out_bwd", grid=(t_len // tm, GLA_HEADS),
        in_specs=[blk, blk, pl.BlockSpec((tm, dv), lambda i, h: (i, r0 + h)), pl.BlockSpec((1, dv), lambda i, h: (0, 0))],
        out_specs=(blk, blk, pl.BlockSpec((1, dv), lambda i, h: (0, 0))),
        out_shape=(jax.ShapeDtypeStruct((t_len, GLA_VAL_DIM), BF16), jax.ShapeDtypeStruct((t_len, GLA_VAL_DIM), BF16),
                   jax.ShapeDtypeStruct((1, dv), F32)),
        compiler_params=_params(2))(dy, o, proj, gn)


def _alibi_slopes():
    n = ATT_HEADS
    start = 2.0 ** (-8.0 / n)
    return [start ** (i + 1) for i in range(n)]


def _att_masks(d):
    b = ATT_BLOCK
    qa = lax.broadcasted_iota(jnp.int32, (b, b), 0)
    kb = lax.broadcasted_iota(jnp.int32, (b, b), 1)
    dist_c = qa - kb
    dist_p = qa - kb + b
    return dist_c >= 0, dist_p <= b, (dist_c * d).astype(F32), (dist_p * d).astype(F32)


def _to_dilated(name, x, d, c0=0, w=None):
    part = x if w is None else x[:, c0:c0 + w]
    return part.reshape(x.shape[0] // d, -1)


def _from_dilated(name, y, d):
    return y.reshape(y.shape[0] * d, y.shape[1] // d)


def _att_views(q_all, kv, g):
    d = DILATIONS[g]
    hd = ATT_HEADS * HEAD_DIM
    if d == 1:
        return q_all, kv
    return _to_dilated(f"q_dilated{g}", q_all, d, g * hd, hd), _to_dilated(f"kv_dilated{g}", kv, d)


def _att_fwd(views, g):
    d = DILATIONS[g]
    assert WINDOWS[g] // d == ATT_BLOCK
    qv, kvv = views
    hd = ATT_HEADS * HEAD_DIM
    sub = kvv.shape[0]
    t_len = sub * d
    nb = sub // ATT_BLOCK
    b = ATT_BLOCK
    e = HEAD_DIM
    scale = e ** -0.5
    slopes = _alibi_slopes()
    qc = (lambda r: 3 * r + g) if d == 1 else (lambda r: r)

    def body(q_ref, kp_ref, kc_ref, vp_ref, vc_ref, o_ref, l_ref, s_scr, p_scr, li_scr):
        ib = pl.program_id(1)
        valid_c, valid_p0, dist_c, dist_p = _att_masks(d)
        valid_p = valid_p0 & (ib > 0)
        for h in range(ATT_HEADS):
            hs = slice(h * e, (h + 1) * e)
            qh = q_ref[:, hs]
            s_scr[h, 0] = _dot(qh, kc_ref[:, hs], NT)
            s_scr[h, 1] = _dot(qh, kp_ref[:, hs], NT)
        l_ref[...] = jnp.zeros_like(l_ref)
        for h in range(ATT_HEADS):
            s_c = jnp.where(valid_c, s_scr[h, 0] * scale - slopes[h] * dist_c, NEG)
            s_p = jnp.where(valid_p, s_scr[h, 1] * scale - slopes[h] * dist_p, NEG)
            m = jnp.maximum(jnp.max(s_c, axis=1, keepdims=True), jnp.max(s_p, axis=1, keepdims=True))
            p_c = jnp.where(valid_c, jnp.exp(s_c - m), 0.0)
            p_p = jnp.where(valid_p, jnp.exp(s_p - m), 0.0)
            l = jnp.sum(p_c, axis=1, keepdims=True) + jnp.sum(p_p, axis=1, keepdims=True)
            p_scr[h, 0] = p_c.astype(BF16)
            p_scr[h, 1] = p_p.astype(BF16)
            li_scr[:, h:h + 1] = 1.0 / l
            l_ref[:, h:h + 1] = m + jnp.log(l)
        for h in range(ATT_HEADS):
            hs = slice(h * e, (h + 1) * e)
            acc = _dot(p_scr[h, 0], vc_ref[:, hs], NN) + _dot(p_scr[h, 1], vp_ref[:, hs], NN)
            o_ref[:, hs] = acc * li_scr[:, h:h + 1]

    blk = (b, hd)
    cblk = (b, LANE)
    o, lse = pl.pallas_call(
        body, name=f"att_fwd{g}", grid=(d, nb),
        scratch_shapes=[pltpu.VMEM((ATT_HEADS, 2, b, b), F32), pltpu.VMEM((ATT_HEADS, 2, b, b), BF16),
                        pltpu.VMEM((b, LANE), F32)],
        in_specs=[pl.BlockSpec(blk, lambda r, i: (i, qc(r))),
                  pl.BlockSpec(blk, lambda r, i: (jnp.maximum(i - 1, 0), 2 * r)),
                  pl.BlockSpec(blk, lambda r, i: (i, 2 * r)),
                  pl.BlockSpec(blk, lambda r, i: (jnp.maximum(i - 1, 0), 2 * r + 1)),
                  pl.BlockSpec(blk, lambda r, i: (i, 2 * r + 1))],
        out_specs=(pl.BlockSpec(blk, lambda r, i: (i, r)), pl.BlockSpec(cblk, lambda r, i: (i, r))),
        out_shape=(jax.ShapeDtypeStruct((sub, d * hd), F32), jax.ShapeDtypeStruct((sub, d * LANE), F32)),
        compiler_params=_params(2))(qv, kvv, kvv, kvv, kvv)
    return _from_dilated(f"o_natural{g}", o, d), lse.reshape(t_len, LANE)


def _att_merge(os, ls):
    t_len, hd = os[0].shape
    tm = _tile(t_len, (256, 128))
    e = HEAD_DIM

    def body(o0, o1, o2, l0, l1, l2, of_ref, ob_ref, l_ref):
        a0, a1, a2 = l0[...], l1[...], l2[...]
        m = jnp.maximum(jnp.maximum(a0, a1), a2)
        e0, e1, e2 = jnp.exp(a0 - m), jnp.exp(a1 - m), jnp.exp(a2 - m)
        den = e0 + e1 + e2
        w0, w1, w2 = e0 / den, e1 / den, e2 / den
        l_ref[...] = m + jnp.log(den)
        for h in range(ATT_HEADS):
            hs = slice(h * e, (h + 1) * e)
            c = slice(h, h + 1)
            o = w0[:, c] * o0[:, hs] + w1[:, c] * o1[:, hs] + w2[:, c] * o2[:, hs]
            of_ref[:, hs] = o
            ob_ref[:, hs] = o.astype(ob_ref.dtype)

    row = pl.BlockSpec((tm, hd), lambda i: (i, 0))
    crow = pl.BlockSpec((tm, LANE), lambda i: (i, 0))
    return pl.pallas_call(
        body, name="att_merge", grid=(t_len // tm,), in_specs=[row] * 3 + [crow] * 3, out_specs=(row, row, crow),
        out_shape=(jax.ShapeDtypeStruct((t_len, hd), F32), jax.ShapeDtypeStruct((t_len, hd), BF16),
                   jax.ShapeDtypeStruct((t_len, LANE), F32)),
        compiler_params=_params(1))(*os, *ls)


def _att_delta(do, o):
    t_len, hd = o.shape
    tm = _tile(t_len, (256, 128))
    e = HEAD_DIM

    def body(do_ref, o_ref, d_ref):
        d_ref[...] = jnp.zeros_like(d_ref)
        for h in range(ATT_HEADS):
            hs = slice(h * e, (h + 1) * e)
            d_ref[:, h:h + 1] = jnp.sum(do_ref[:, hs].astype(F32) * o_ref[:, hs], axis=1, keepdims=True)

    row = pl.BlockSpec((tm, hd), lambda i: (i, 0))
    return pl.pallas_call(
        body, name="att_delta", grid=(t_len // tm,), in_specs=[row, row],
        out_specs=pl.BlockSpec((tm, LANE), lambda i: (i, 0)),
        out_shape=jax.ShapeDtypeStruct((t_len, LANE), F32), compiler_params=_params(1))(do, o)


def _att_bwd(views, delta, lse, do, g):
    d = DILATIONS[g]
    qv, kvv = views
    hd = ATT_HEADS * HEAD_DIM
    sub = kvv.shape[0]
    t_len = sub * d
    nb = sub // ATT_BLOCK
    b = ATT_BLOCK
    e = HEAD_DIM
    scale = e ** -0.5
    slopes = _alibi_slopes()
    qc = (lambda r: 3 * r + g) if d == 1 else (lambda r: r)
    dlv = delta.reshape(sub, d * LANE)
    lv = lse.reshape(sub, d * LANE)
    dov = do if d == 1 else _to_dilated(f"do_dilated{g}", do, d)

    def body(qj_ref, qn_ref, kp_ref, kc_ref, vp_ref, vc_ref, doj_ref, don_ref, dj_ref, dn_ref, lj_ref, ln_ref,
             dq_ref, dk_ref, dv_ref, s_scr, dp_scr, p_scr, ds_scr):
        j = pl.program_id(1)
        valid_c, valid_p0, dist_c, dist_p = _att_masks(d)
        valid = (valid_c, valid_p0 & (j > 0), valid_p0 & (j + 1 < nb))
        dist = (dist_c, dist_p, dist_p)
        for h in range(ATT_HEADS):
            hs = slice(h * e, (h + 1) * e)
            qj, qn = qj_ref[:, hs], qn_ref[:, hs]
            kc, kp = kc_ref[:, hs], kp_ref[:, hs]
            vc, vp = vc_ref[:, hs], vp_ref[:, hs]
            doj, don = doj_ref[:, hs], don_ref[:, hs]
            s_scr[h, 0] = _dot(qj, kc, NT)
            s_scr[h, 1] = _dot(qj, kp, NT)
            s_scr[h, 2] = _dot(qn, kc, NT)
            dp_scr[h, 0] = _dot(doj, vc, NT)
            dp_scr[h, 1] = _dot(doj, vp, NT)
            dp_scr[h, 2] = _dot(don, vc, NT)
        for h in range(ATT_HEADS):
            c = slice(h, h + 1)
            lse_t = (lj_ref[:, c], lj_ref[:, c], ln_ref[:, c])
            dlt_t = (dj_ref[:, c], dj_ref[:, c], dn_ref[:, c])
            for t in range(3):
                s = s_scr[h, t] * scale - slopes[h] * dist[t]
                p = jnp.where(valid[t], jnp.exp(jnp.where(valid[t], s - lse_t[t], NEG)), 0.0)
                p_scr[h, t] = p.astype(BF16)
                ds_scr[h, t] = (p * (dp_scr[h, t] - dlt_t[t])).astype(BF16)
        for h in range(ATT_HEADS):
            hs = slice(h * e, (h + 1) * e)
            dq = _dot(ds_scr[h, 0], kc_ref[:, hs], NN) + _dot(ds_scr[h, 1], kp_ref[:, hs], NN)
            dk = _dot(ds_scr[h, 0], qj_ref[:, hs], TN) + _dot(ds_scr[h, 2], qn_ref[:, hs], TN)
            dv = _dot(p_scr[h, 0], doj_ref[:, hs], TN) + _dot(p_scr[h, 2], don_ref[:, hs], TN)
            dq_ref[:, hs] = (dq * scale).astype(dq_ref.dtype)
            dk_ref[:, hs] = (dk * scale).astype(dk_ref.dtype)
            dv_ref[:, hs] = dv.astype(dv_ref.dtype)

    blk = (b, hd)
    cblk = (b, LANE)
    nxt = lambda i: jnp.minimum(i + 1, nb - 1)
    prv = lambda i: jnp.maximum(i - 1, 0)
    tiles = (ATT_HEADS, 3, b, b)
    dq, dk, dv = pl.pallas_call(
        body, name=f"att_bwd{g}", grid=(d, nb),
        scratch_shapes=[pltpu.VMEM(tiles, F32), pltpu.VMEM(tiles, F32), pltpu.VMEM(tiles, BF16), pltpu.VMEM(tiles, BF16)],
        in_specs=[pl.BlockSpec(blk, lambda r, i: (i, qc(r))),
                  pl.BlockSpec(blk, lambda r, i: (nxt(i), qc(r))),
                  pl.BlockSpec(blk, lambda r, i: (prv(i), 2 * r)),
                  pl.BlockSpec(blk, lambda r, i: (i, 2 * r)),
                  pl.BlockSpec(blk, lambda r, i: (prv(i), 2 * r + 1)),
                  pl.BlockSpec(blk, lambda r, i: (i, 2 * r + 1)),
                  pl.BlockSpec(blk, lambda r, i: (i, r)),
                  pl.BlockSpec(blk, lambda r, i: (nxt(i), r)),
                  pl.BlockSpec(cblk, lambda r, i: (i, r)),
                  pl.BlockSpec(cblk, lambda r, i: (nxt(i), r)),
                  pl.BlockSpec(cblk, lambda r, i: (i, r)),
                  pl.BlockSpec(cblk, lambda r, i: (nxt(i), r))],
        out_specs=(pl.BlockSpec(blk, lambda r, i: (i, r)),) * 3,
        out_shape=(jax.ShapeDtypeStruct((sub, d * hd), BF16),) * 3,
        compiler_params=_params(2))(qv, qv, kvv, kvv, kvv, kvv, dov, dov, dlv, dlv, lv, lv)
    return tuple(_from_dilated(f"{n}_natural{g}", t, d) for n, t in (("dq", dq), ("dk", dk), ("dv", dv)))


def _kv_grad_sum(dks, dvs):
    t_len, hd = dks[0].shape
    tm = _tile(t_len, (256, 128))

    def body(k0, k1, k2, v0, v1, v2, o_ref):
        o_ref[:, :hd] = (k0[...].astype(F32) + k1[...].astype(F32) + k2[...].astype(F32)).astype(o_ref.dtype)
        o_ref[:, hd:] = (v0[...].astype(F32) + v1[...].astype(F32) + v2[...].astype(F32)).astype(o_ref.dtype)

    row = pl.BlockSpec((tm, hd), lambda i: (i, 0))
    return pl.pallas_call(
        body, name="kv_grad_sum", grid=(t_len // tm,), in_specs=[row] * 6,
        out_specs=pl.BlockSpec((tm, 2 * hd), lambda i: (i, 0)),
        out_shape=jax.ShapeDtypeStruct((t_len, 2 * hd), BF16), compiler_params=_params(1))(*dks, *dvs)


HALO = 16
INV_SQRT2 = 1.0 / math.sqrt(2.0)
INV_SQRT2PI = 1.0 / math.sqrt(2.0 * math.pi)


def _conv_taps(g, halo, cw, cb):
    row = lax.broadcasted_iota(jnp.int32, g.shape, 0)
    h1 = halo[HALO - 1:HALO, :]
    h2 = halo[HALO - 2:HALO - 1, :]
    g1 = jnp.where(row == 0, h1, pltpu.roll(g, 1, 0))
    g2 = jnp.where(row == 0, h2, jnp.where(row == 1, h1, pltpu.roll(g, 2, 0)))
    gc = cw[0:1, :] * g2 + cw[1:2, :] * g1 + cw[2:3, :] * g + cb
    return gc, g1, g2


def _glu_specs(t_len, f, tm, tc):
    nj = f // tc
    hb = tm // HALO
    u = pl.BlockSpec((tm, tc), lambda j, i: (i, j))
    g = pl.BlockSpec((tm, tc), lambda j, i: (i, nj + j))
    gh = pl.BlockSpec((HALO, tc), lambda j, i: (jnp.maximum(i * hb - 1, 0), nj + j))
    cw = pl.BlockSpec((8, tc), lambda j, i: (0, j))
    cb = pl.BlockSpec((1, tc), lambda j, i: (0, j))
    return u, g, gh, cw, cb


def _glu_fwd(name, up, cw, cb):
    t_len = up.shape[0]
    f = up.shape[1] // 2
    tm = _tile(t_len, (512, 256, 128))
    tc = _tile(f, (1408, 1024, 512, 256, 128))
    u_s, g_s, gh_s, cw_s, cb_s = _glu_specs(t_len, f, tm, tc)

    def body(u_ref, g_ref, gh_ref, cw_ref, cb_ref, o_ref):
        first = pl.program_id(1) == 0
        halo = jnp.where(first, 0.0, gh_ref[...].astype(F32))
        gc, _, _ = _conv_taps(g_ref[...].astype(F32), halo, cw_ref[...], cb_ref[...])
        gel = 0.5 * gc * (1.0 + lax.erf(gc * INV_SQRT2))
        o_ref[...] = (gel * u_ref[...].astype(F32)).astype(o_ref.dtype)

    return pl.pallas_call(
        body, name=name, grid=(f // tc, t_len // tm), in_specs=[u_s, g_s, gh_s, cw_s, cb_s],
        out_specs=pl.BlockSpec((tm, tc), lambda j, i: (i, j)),
        out_shape=jax.ShapeDtypeStruct((t_len, f), BF16), compiler_params=_params(2))(up, up, up, cw, cb)


def _glu_bwd_a(name, dact, up, cw, cb):
    t_len = up.shape[0]
    f = up.shape[1] // 2
    tm = _tile(t_len, (256, 128))
    tc = _tile(f, (1408, 1024, 512, 256, 128))
    u_s, g_s, gh_s, cw_s, cb_s = _glu_specs(t_len, f, tm, tc)

    def body(da_ref, u_ref, g_ref, gh_ref, cw_ref, cb_ref, du_ref, dgc_ref, w0_ref, w1_ref, w2_ref, b_ref):
        first = pl.program_id(1) == 0
        halo = jnp.where(first, 0.0, gh_ref[...].astype(F32))
        g = g_ref[...].astype(F32)
        gc, g1, g2 = _conv_taps(g, halo, cw_ref[...], cb_ref[...])
        phi = 0.5 * (1.0 + lax.erf(gc * INV_SQRT2))
        dgel = phi + gc * jnp.exp(-0.5 * gc * gc) * INV_SQRT2PI
        da = da_ref[...].astype(F32)
        du_ref[...] = (da * gc * phi).astype(du_ref.dtype)
        dgc = da * u_ref[...].astype(F32) * dgel
        dgc_ref[...] = dgc.astype(dgc_ref.dtype)
        parts = (jnp.sum(dgc * g2, axis=0, keepdims=True), jnp.sum(dgc * g1, axis=0, keepdims=True),
                 jnp.sum(dgc * g, axis=0, keepdims=True), jnp.sum(dgc, axis=0, keepdims=True))
        refs = (w0_ref, w1_ref, w2_ref, b_ref)

        @pl.when(first)
        def _():
            for r, p in zip(refs, parts):
                r[...] = p

        @pl.when(jnp.logical_not(first))
        def _():
            for r, p in zip(refs, parts):
                r[...] += p

    tile = pl.BlockSpec((tm, tc), lambda j, i: (i, j))
    vec = pl.BlockSpec((1, tc), lambda j, i: (0, j))
    vshape = jax.ShapeDtypeStruct((1, f), F32)
    return pl.pallas_call(
        body, name=name, grid=(f // tc, t_len // tm), in_specs=[tile, u_s, g_s, gh_s, cw_s, cb_s],
        out_specs=(tile, tile, vec, vec, vec, vec),
        out_shape=(jax.ShapeDtypeStruct((t_len, 2 * f), BF16), jax.ShapeDtypeStruct((t_len, f), BF16),
                   vshape, vshape, vshape, vshape),
        compiler_params=_params(2))(dact, up, up, up, cw, cb)


def _glu_bwd_b(name, dup, dgc, cw):
    t_len, f = dgc.shape
    tm = _tile(t_len, (256, 128, 64))
    hb = tm // HALO
    n_i = t_len // tm
    last_hb = t_len // HALO - 1

    def body(dup_ref, d_ref, dh_ref, cw_ref, o_ref):
        last = pl.program_id(0) == n_i - 1
        halo = jnp.where(last, 0.0, dh_ref[...].astype(F32))
        dd = d_ref[...].astype(F32)
        row = lax.broadcasted_iota(jnp.int32, dd.shape, 0)
        h0 = halo[0:1, :]
        h1 = halo[1:2, :]
        d1 = jnp.where(row == tm - 1, h0, pltpu.roll(dd, tm - 1, 0))
        d2 = jnp.where(row == tm - 1, h1, jnp.where(row == tm - 2, h0, pltpu.roll(dd, tm - 2, 0)))
        cwv = cw_ref[...]
        dg = cwv[2:3, :] * dd + cwv[1:2, :] * d1 + cwv[0:1, :] * d2
        o_ref[...] = dg.astype(o_ref.dtype)

    row_s = pl.BlockSpec((tm, f), lambda i: (i, 0))
    return pl.pallas_call(
        body, name=name, grid=(n_i,),
        in_specs=[ANY, row_s, pl.BlockSpec((HALO, f), lambda i: (jnp.minimum((i + 1) * hb, last_hb), 0)),
                  pl.BlockSpec((8, f), lambda i: (0, 0))],
        out_specs=pl.BlockSpec((tm, f), lambda i: (i, 1)), input_output_aliases={0: 0},
        out_shape=jax.ShapeDtypeStruct((t_len, 2 * f), BF16), compiler_params=_params(1))(dup, dgc, dgc, cw)


def _adamw(name, w, g, m, v):
    rows, cols = w.shape
    gcols = g.shape[1]
    n_out = 3 if gcols == cols else 4
    tr = _tile(rows, (256, 128, 64, 32, 16, 8))
    c1 = 1.0 / (1.0 - ADAM_B1 ** ADAM_STEP)
    c2 = 1.0 / (1.0 - ADAM_B2 ** ADAM_STEP)

    def body(w_ref, g_ref, m_ref, v_ref, d_ref, nm_ref, nv_ref, *g_out):
        gv = g_ref[...][:, :cols]
        nm = ADAM_B1 * m_ref[...] + (1.0 - ADAM_B1) * gv
        nv = ADAM_B2 * v_ref[...] + (1.0 - ADAM_B2) * (gv * gv)
        nm_ref[...] = nm
        nv_ref[...] = nv
        d_ref[...] = -ADAM_LR * ((nm * c1) / (jnp.sqrt(nv * c2) + ADAM_EPS) + ADAM_WD * w_ref[...])
        for ref in g_out:
            ref[...] = gv

    blk = pl.BlockSpec((tr, cols), lambda i: (i, 0))
    gblk = pl.BlockSpec((tr, gcols), lambda i: (i, 0))
    shp = jax.ShapeDtypeStruct((rows, cols), F32)
    return pl.pallas_call(body, name=name, grid=(rows // tr,), in_specs=[blk, gblk, blk, blk], out_specs=(blk,) * n_out,
                          out_shape=(shp,) * n_out, compiler_params=_params(1))(w, g, m, v)


def _adamw_layers(name, w, gs, m, v):
    _, rows, cols = w.shape
    tr = _tile(rows, (128, 64, 32, 16, 8))
    nr = rows // tr
    c1 = 1.0 / (1.0 - ADAM_B1 ** ADAM_STEP)
    c2 = 1.0 / (1.0 - ADAM_B2 ** ADAM_STEP)

    def body(w_ref, g0_ref, g1_ref, m_ref, v_ref, d_ref, nm_ref, nv_ref, g_ref):
        gv = jnp.where(pl.program_id(0) == 0, g0_ref[...], g1_ref[...])
        nm = ADAM_B1 * m_ref[...] + (1.0 - ADAM_B1) * gv
        nv = ADAM_B2 * v_ref[...] + (1.0 - ADAM_B2) * (gv * gv)
        nm_ref[...] = nm
        nv_ref[...] = nv
        d_ref[...] = -ADAM_LR * ((nm * c1) / (jnp.sqrt(nv * c2) + ADAM_EPS) + ADAM_WD * w_ref[...])
        g_ref[...] = gv

    blk = pl.BlockSpec((None, tr, cols), lambda l, i: (l, i, 0))
    g0_blk = pl.BlockSpec((tr, cols), lambda l, i: (jnp.where(l == 0, i, nr - 1), 0))
    g1_blk = pl.BlockSpec((tr, cols), lambda l, i: (jnp.where(l == 0, 0, i), 0))
    shp = jax.ShapeDtypeStruct(w.shape, F32)
    return pl.pallas_call(body, name=name, grid=(2, nr), in_specs=[blk, g0_blk, g1_blk, blk, blk], out_specs=(blk,) * 4,
                          out_shape=(shp,) * 4, compiler_params=_params(2))(w, gs[0], gs[1], m, v)


class _NoComm:
    def __init__(self):
        self.grads = {}

    def prefetch(self, group, ws, carry):
        return ws, carry

    def need(self, group, ws, after):
        return ws

    def reduce(self, group, grads, carry):
        self.grads.update(grads)
        return carry

    def tick(self, carry):
        return carry


def _local_step(x, target, ws, norms, small, hooks):
    lay = _layout()

    w_main, w_a = _unpack_gin(ws["gin"])
    hn0 = _rms_fwd("rms_attn0", x, norms["attn0"])
    proj = _mm_plain("gla_proj", hn0, w_main, NN, F32)
    a = _mm_plain("gla_proj_a", hn0, w_a, NN, BF16)
    ga, cum = _gla_gate_fwd(a, small["w_a2p"], small["b_a2"])
    ws, cum = hooks.prefetch("B0", ws, cum)
    o_gla, states = _gla_fwd(proj, cum)
    gated = _gla_out_fwd(o_gla, proj, small["head_norm"])
    ws = hooks.need("B0", ws, gated)
    ws, gated = hooks.prefetch("B1", ws, gated)
    h1 = _mm_act_wr("gla_out", gated, ws["gout"], lay["gout"], add=x)
    ws = hooks.need("B1", ws, h1)

    def ffn_fwd(l, h, own=None, prefetch=None):
        nonlocal ws
        hn = _rms_fwd(f"rms_ffn{l}", h, norms[f"ffn{l}"])
        if own is not None:
            ws, hn = hooks.prefetch(own, ws, hn)
        up = _mm_act_wc(f"ffn_up{l}", hn, ws[f"up{l}"], lay[f"up{l}"], BF16)
        act = _glu_fwd(f"glu_fwd{l}", up, small["conv_w"][l], small["conv_b"][l])
        if own is not None:
            ws = hooks.need(own, ws, act)
        if prefetch is not None:
            ws, act = hooks.prefetch(prefetch, ws, act)
        return hn, up, act, _mm_act_wr(f"ffn_down{l}", act, ws[f"down{l}"], lay[f"down{l}"], add=h)

    hnf0, up0, act0, h2 = ffn_fwd(0, h1, own="B2", prefetch="C1")

    ws = hooks.need("C1", ws, h2)
    kvn = _rms_fwd("rms_kv", h2, norms["kv"])
    kv = _mm_act_wc("kv_proj", kvn, ws["wkv"], lay["wkv"], BF16)
    hn1 = _rms_fwd("rms_attn1", h2, norms["attn1"])
    q_all = _mm_act_wc("q_proj", hn1, ws["wq"], lay["wq"], BF16)
    views = [_att_views(q_all, kv, g) for g in range(3)]
    branch = [_att_fwd(views[g], g) for g in range(3)]
    ws, lse2 = hooks.prefetch("C2", ws, branch[-1][1])
    o_att, o_att_b, lse = _att_merge([br[0] for br in branch], [br[1] for br in branch[:-1]] + [lse2])
    h3 = _mm_act_wr("att_out", o_att_b, ws["dout"], lay["dout"], add=h2)
    ws = hooks.need("C2", ws, h3)
    hnf1, up1, act1, h4 = ffn_fwd(1, h3)

    dh4, d_final, loss = _loss_head(h4, norms["final"], target)

    sm = {"final": d_final}

    def ffn_bwd(l, dh, h, hn, up, act):
        big = {}
        dact = _mm_dact_wrT(f"ffn_down_dx{l}", dh, ws[f"down{l}"], lay[f"down{l}"])
        big[f"down{l}"] = _mm_grad_wr(f"ffn_down_dw{l}", act, dh, lay[f"down{l}"])
        du, dgc, w0, w1, w2, db = _glu_bwd_a(f"glu_bwd_a{l}", dact, up, small["conv_w"][l], small["conv_b"][l])
        sm[f"conv_w{l}"] = (w0, w1, w2)
        sm[f"conv_b{l}"] = db
        dup = hooks.tick(_glu_bwd_b(f"glu_bwd_b{l}", du, dgc, small["conv_w"][l]))
        dhn = _mm_dact_wcT(f"ffn_up_dx{l}", dup, ws[f"up{l}"], lay[f"up{l}"])
        big[f"up{l}"] = _mm_grad_wc(f"ffn_up_dw{l}", hn, dup, lay[f"up{l}"])
        dh_in, (sm[f"ffn{l}"],) = _rms_bwd(f"rms_ffn_bwd{l}", [dhn], h, [norms[f"ffn{l}"]], dh)
        return hooks.reduce(f"ffn{l}", big, dh_in)

    dh3 = ffn_bwd(1, dh4, h3, hnf1, up1, act1)

    big = {}
    do_att = _mm_dact_wrT("att_out_dx", dh3, ws["dout"], lay["dout"])
    big["dout"] = _mm_grad_wr("att_out_dw", o_att_b, dh3, lay["dout"])
    delta = _att_delta(do_att, o_att)
    bw = [_att_bwd(views[g], delta, lse, do_att, g) for g in range(3)]
    dq_all = jnp.concatenate([t[0] for t in bw], axis=1)
    dhn1 = _mm_dact_wcT("q_proj_dx", dq_all, ws["wq"], lay["wq"])
    big["wq"] = _mm_grad_wc("q_proj_dw", hn1, dq_all, lay["wq"])
    dkv = hooks.tick(_kv_grad_sum([t[1] for t in bw], [t[2] for t in bw]))
    dkvn = _mm_dact_wcT("kv_proj_dx", dkv, ws["wkv"], lay["wkv"])
    big["wkv"] = _mm_grad_wc("kv_proj_dw", kvn, dkv, lay["wkv"])
    dh2, (sm["attn1"], sm["kv"]) = _rms_bwd("rms_h2_bwd", [dhn1, dkvn], h2, [norms["attn1"], norms["kv"]], dh3)
    dh2 = hooks.reduce("att", big, dh2)

    dh1 = ffn_bwd(0, dh2, h1, hnf0, up0, act0)

    big = {}
    dgated = _mm_dact_wrT("gla_out_dx", dh1, ws["gout"], lay["gout"])
    big["gout"] = _mm_grad_wr("gla_out_dw", gated, dh1, lay["gout"])
    do_gla, dr, sm["head_norm"] = _gla_out_bwd(dgated, o_gla, proj, small["head_norm"])
    dq, dk, dv, dcum = _gla_bwd(proj, cum, states, hooks.tick(do_gla))
    da, sm["w_a2p"], sm["b_a2"] = _gla_gate_bwd(dcum, ga, a, small["w_a2p"])
    dproj = jnp.concatenate([dq, dk, dv, dr], axis=1)
    dhn0 = _mm_plain("gla_proj_dx", dproj, w_main, NT, F32)
    dhn0 = _mm_plain("gla_proj_a_dx", da, w_a, NT, F32, add=dhn0)
    gin_main = _mm_plain("gla_proj_dw", hn0, dproj, TN, BF16)
    gin_a = _mm_plain("gla_proj_a_dw", hn0, da, TN, BF16)
    big["gin"] = _pack_gin_grad(gin_main, gin_a)
    grad_x, (sm["attn0"],) = _rms_bwd("rms_attn0_bwd", [dhn0], x, [norms["attn0"]], dh1)
    return loss, grad_x, sm, big


def _pack_weights(chip, names, gla_w_in, gla_w_out, w_kv, dsa_w_q, dsa_w_out, ffn_w_up, ffn_w_down):
    gin = gla_w_in[0]
    gin = jnp.pad(gin, ((0, 0), (0, _roundup(gin.shape[1], LANE) - gin.shape[1])))
    shards = {"gin": gin, "gout": gla_w_out[0], "up0": ffn_w_up[0], "up1": ffn_w_up[1], "down0": ffn_w_down[0],
              "down1": ffn_w_down[1], "wq": dsa_w_q[0], "wkv": w_kv, "dout": dsa_w_out[0]}
    out = {}
    for name in names:
        w = shards[name]
        buf = jnp.zeros((N_CHIPS,) + w.shape, BF16)
        out[name] = lax.dynamic_update_slice(buf, w.astype(BF16)[None], (chip, 0, 0))
    return out


def _unpack_gin(w_gin):
    w = _layout()["gin"][1]
    d = w_gin.shape[1]
    wp = w_gin.shape[2]
    n_main = 2 * GLA_KEY_DIM + 2 * GLA_VAL_DIM
    tm = _tile(d, (256, 128, 64, 32, 16))

    def body(s_ref, main_ref, a_ref):
        full = jnp.concatenate([s_ref[s][:, :w] for s in range(N_CHIPS)], axis=1)
        main_ref[...] = full[:, :n_main]
        a_ref[...] = jnp.concatenate([full[:, n_main:], jnp.zeros((tm, A_PAD - GATE_RANK), full.dtype)], axis=1)

    return pl.pallas_call(
        body, name="unpack_gin", grid=(d // tm,), in_specs=[pl.BlockSpec((N_CHIPS, tm, wp), lambda i: (0, i, 0))],
        out_specs=(pl.BlockSpec((tm, n_main), lambda i: (i, 0)), pl.BlockSpec((tm, A_PAD), lambda i: (i, 0))),
        out_shape=(jax.ShapeDtypeStruct((d, n_main), w_gin.dtype), jax.ShapeDtypeStruct((d, A_PAD), w_gin.dtype)),
        compiler_params=_params(1))(w_gin)


def _pack_gin_grad(gin_main, gin_a):
    w = _layout()["gin"][1]
    wp = _roundup(w, LANE)
    d, n_main = gin_main.shape
    tm = _tile(d, (256, 128, 64, 32, 16))

    def body(main_ref, a_ref, o_ref):
        full = jnp.concatenate([main_ref[...], a_ref[:, :GATE_RANK]], axis=1)
        fill = jnp.zeros((tm, wp - w), full.dtype)
        for s in range(N_CHIPS):
            o_ref[s] = jnp.concatenate([full[:, s * w:(s + 1) * w], fill], axis=1)

    return pl.pallas_call(
        body, name="pack_gin_grad", grid=(d // tm,),
        in_specs=[pl.BlockSpec((tm, n_main), lambda i: (i, 0)), pl.BlockSpec((tm, A_PAD), lambda i: (i, 0))],
        out_specs=pl.BlockSpec((N_CHIPS, tm, wp), lambda i: (0, i, 0)),
        out_shape=jax.ShapeDtypeStruct((N_CHIPS, d, wp), gin_main.dtype), compiler_params=_params(1))(gin_main, gin_a)


def _small_params(attn_norm, ffn_norm, kv_norm, final_norm, conv_b, w_a2, b_a2, head_norm, conv_w):
    norms = {"attn0": attn_norm[0:1], "attn1": attn_norm[1:2], "ffn0": ffn_norm[0:1], "ffn1": ffn_norm[1:2],
             "kv": kv_norm[None, :], "final": final_norm[None, :]}
    small = {"w_a2p": jnp.pad(w_a2, ((0, A_PAD - GATE_RANK), (0, 0))), "b_a2": b_a2[None, :],
             "head_norm": head_norm[None, :], "conv_w": jnp.pad(conv_w, ((0, 0), (0, 8 - conv_w.shape[1]), (0, 0))),
             "conv_b": conv_b[:, None, :]}
    return norms, small


ANY = pl.BlockSpec(memory_space=pl.ANY)


def _place():
    return lax.axis_index("x"), lax.axis_index("y"), lax.axis_index("c")


def _other_chips(x, y):
    return [(1 - x, y), (x, 1 - y), (1 - x, 1 - y)]


def _rcopy(src, dst, ssem, rsem, dev):
    return pltpu.make_async_remote_copy(src_ref=src, dst_ref=dst, send_sem=ssem, recv_sem=rsem, device_id=dev,
                                        device_id_type=MESH)


def _pack_shard(name, w, layer, chip_arr, after):
    rows, cols = w.shape[-2:]
    tr = _tile(rows, (512, 352, 256, 128, 64, 32, 16))

    def body(p_ref, w_ref, after_ref, o_ref):
        o_ref[...] = w_ref[...].astype(o_ref.dtype)

    if w.ndim == 3:
        w_spec = pl.BlockSpec((None, tr, cols), lambda i, p: (layer, i, 0))
    else:
        w_spec = pl.BlockSpec((tr, cols), lambda i, p: (i, 0))
    return pl.pallas_call(
        body, name=name,
        grid_spec=pltpu.PrefetchScalarGridSpec(
            num_scalar_prefetch=1, grid=(rows // tr,), in_specs=[w_spec, ANY],
            out_specs=pl.BlockSpec((None, tr, cols), lambda i, p: (p[0], i, 0))),
        out_shape=jax.ShapeDtypeStruct((N_CHIPS, rows, cols), BF16), compiler_params=_params(1))(chip_arr, w, after)


def _swap_halves(name, arrs):
    n = len(arrs)

    def body(*refs):
        ins, outs = refs[:n], refs[n:2 * n]
        send, recv = refs[2 * n:]
        x, y, c = _place()
        cps = []
        for a in range(n):
            h = ins[a].shape[1] // 2
            cp = _rcopy(ins[a].at[:, pl.ds((1 - c) * h, h)], outs[a], send.at[a], recv.at[a], (x, y, 1 - c))
            cp.start()
            cps.append(cp)
        for cp in cps:
            cp.wait()

    return pl.pallas_call(
        body, name=name, in_specs=[ANY] * n, out_specs=[ANY] * n,
        out_shape=[jax.ShapeDtypeStruct((a.shape[0], a.shape[1] // 2, a.shape[2]), a.dtype) for a in arrs],
        scratch_shapes=[pltpu.SemaphoreType.DMA((n,)), pltpu.SemaphoreType.DMA((n,))])(*arrs)


SEM = pl.BlockSpec(memory_space=pltpu.SEMAPHORE)
EFFECT = pltpu.SideEffectType.DATAFLOW_SIDE_EFFECTING


def _shapes(arrs):
    return [jax.ShapeDtypeStruct(a.shape, a.dtype) for a in arrs]


def _gather_start(name, thru, arrs):
    n, nt = len(arrs), len(thru)

    def body(*refs):
        ins = refs[nt:nt + n]
        send, recv = refs[nt + n], refs[nt + n + 1]
        outs = refs[2 * nt + n + 2:]
        x, y, c = _place()
        me = 2 * x + y
        for a in range(n):
            h = ins[a].shape[1] // 2
            mine = pl.ds(c * h, h)
            for j, (px, py) in enumerate(_other_chips(x, y)):
                _rcopy(ins[a].at[me, mine], outs[a].at[me, mine], send.at[3 * a + j], recv.at[3 * a + j], (px, py, c)).start()

    res = pl.pallas_call(
        body, name=name, in_specs=[ANY] * (nt + n), out_specs=[SEM, SEM] + [ANY] * (nt + n),
        out_shape=[pltpu.SemaphoreType.DMA((3 * n,)), pltpu.SemaphoreType.DMA((3 * n,))] + _shapes(thru) + _shapes(arrs),
        input_output_aliases={i: 2 + i for i in range(nt + n)},
        compiler_params=pltpu.CompilerParams(has_side_effects=EFFECT))(*thru, *arrs)
    return res[0], res[1], res[2:2 + nt], res[2 + nt:]


def _gather_wait(name, send, recv, arrs, after):
    n = len(arrs)

    def body(*refs):
        ins = refs[:n]
        send_ref, recv_ref = refs[n], refs[n + 1]
        x, y, c = _place()
        me = 2 * x + y
        for a in range(n):
            h = ins[a].shape[1] // 2
            mine = pl.ds(c * h, h)
            for j, (px, py) in enumerate(_other_chips(x, y)):
                sent = ins[a].at[me, mine]
                landed = ins[a].at[2 * px + py, mine]
                cp = _rcopy(sent, landed, send_ref.at[3 * a + j], recv_ref.at[3 * a + j], (px, py, c))
                cp.wait_send()
                cp.wait_recv()

    after = list(after) if isinstance(after, (list, tuple)) else [after]
    return pl.pallas_call(
        body, name=name, in_specs=[ANY] * n + [SEM, SEM] + [ANY] * len(after), out_specs=[ANY] * n,
        out_shape=_shapes(arrs), input_output_aliases={a: a for a in range(n)},
        compiler_params=pltpu.CompilerParams(has_side_effects=EFFECT))(*arrs, send, recv, *after)


def _forward_halves(name, arrs):
    n = len(arrs)

    def body(*refs):
        ins, outs = refs[:n], refs[n:2 * n]
        send, recv = refs[2 * n:]
        x, y, c = _place()
        sib = (x, y, 1 - c)
        chips = _other_chips(x, y)
        cps = []
        for a in range(n):
            h = ins[a].shape[1] // 2
            mine = pl.ds(c * h, h)
            for j, (px, py) in enumerate(chips):
                cp = _rcopy(ins[a].at[2 * px + py, mine], outs[a].at[2 * px + py, mine], send.at[3 * a + j],
                            recv.at[3 * a + j], sib)
                cp.start()
                cps.append(cp)
        for a in range(n):
            h = ins[a].shape[1] // 2
            theirs = pl.ds((1 - c) * h, h)
            for j, (px, py) in enumerate(chips):
                got = outs[a].at[2 * px + py, theirs]
                _rcopy(got, got, send.at[3 * a + j], recv.at[3 * a + j], sib).wait_recv()
        for cp in cps:
            cp.wait_send()

    return pl.pallas_call(
        body, name=name, in_specs=[ANY] * n, out_specs=[ANY] * n, out_shape=_shapes(arrs),
        input_output_aliases={a: a for a in range(n)},
        scratch_shapes=[pltpu.SemaphoreType.DMA((3 * n,)), pltpu.SemaphoreType.DMA((3 * n,))])(*arrs)


def _forward_start(name, thru, arrs):
    n, nt = len(arrs), len(thru)

    def body(*refs):
        ins = refs[nt:nt + n]
        send, recv = refs[nt + n], refs[nt + n + 1]
        outs = refs[2 * nt + n + 2:]
        x, y, c = _place()
        for a in range(n):
            h = ins[a].shape[1] // 2
            mine = pl.ds(c * h, h)
            for j, (px, py) in enumerate(_other_chips(x, y)):
                _rcopy(ins[a].at[2 * px + py, mine], outs[a].at[2 * px + py, mine], send.at[3 * a + j], recv.at[3 * a + j],
                       (x, y, 1 - c)).start()

    res = pl.pallas_call(
        body, name=name, in_specs=[ANY] * (nt + n), out_specs=[SEM, SEM] + [ANY] * (nt + n),
        out_shape=[pltpu.SemaphoreType.DMA((3 * n,)), pltpu.SemaphoreType.DMA((3 * n,))] + _shapes(thru) + _shapes(arrs),
        input_output_aliases={i: 2 + i for i in range(nt + n)},
        compiler_params=pltpu.CompilerParams(has_side_effects=EFFECT))(*thru, *arrs)
    return res[0], res[1], res[2:2 + nt], res[2 + nt:]


def _forward_wait(name, send, recv, arrs, after):
    n = len(arrs)

    def body(*refs):
        ins = refs[:n]
        send_ref, recv_ref = refs[n], refs[n + 1]
        x, y, c = _place()
        for a in range(n):
            h = ins[a].shape[1] // 2
            for j, (px, py) in enumerate(_other_chips(x, y)):
                sent = ins[a].at[2 * px + py, pl.ds(c * h, h)]
                got = ins[a].at[2 * px + py, pl.ds((1 - c) * h, h)]
                cp = _rcopy(sent, got, send_ref.at[3 * a + j], recv_ref.at[3 * a + j], (x, y, 1 - c))
                cp.wait_send()
                cp.wait_recv()

    return pl.pallas_call(
        body, name=name, in_specs=[ANY] * n + [SEM, SEM, ANY], out_specs=[ANY] * n, out_shape=_shapes(arrs),
        input_output_aliases={a: a for a in range(n)},
        compiler_params=pltpu.CompilerParams(has_side_effects=EFFECT))(*arrs, send, recv, after)


def _scatter_start(name, thru, arrs):
    n, nt = len(arrs), len(thru)
    landing = [lax.empty(a.shape, a.dtype) for a in arrs]

    def body(*refs):
        ins = refs[nt:nt + n]
        send, recv = refs[nt + 2 * n], refs[nt + 2 * n + 1]
        outs = refs[2 * nt + 3 * n + 2:]
        x, y, c = _place()
        me = 2 * x + y
        for a in range(n):
            for j, (px, py) in enumerate(_other_chips(x, y)):
                _rcopy(ins[a].at[2 * px + py], outs[a].at[me], send.at[3 * a + j], recv.at[3 * a + j], (px, py, c)).start()

    res = pl.pallas_call(
        body, name=name, in_specs=[ANY] * (nt + 2 * n), out_specs=[SEM, SEM] + [ANY] * (nt + 2 * n),
        out_shape=[pltpu.SemaphoreType.DMA((3 * n,)), pltpu.SemaphoreType.DMA((3 * n,))] + _shapes(thru) + _shapes(arrs)
        + _shapes(landing),
        input_output_aliases={i: 2 + i for i in range(nt + 2 * n)},
        compiler_params=pltpu.CompilerParams(has_side_effects=EFFECT))(*thru, *arrs, *landing)
    return res[0], res[1], res[2:2 + nt], res[2 + nt:2 + nt + n], res[2 + nt + n:]


def _scatter_wait(name, send, recv, arrs, landing, after):
    n = len(arrs)

    def body(*refs):
        ins, land = refs[:n], refs[n:2 * n]
        send_ref, recv_ref = refs[2 * n], refs[2 * n + 1]
        x, y, c = _place()
        for a in range(n):
            for j, (px, py) in enumerate(_other_chips(x, y)):
                cp = _rcopy(ins[a].at[2 * px + py], land[a].at[2 * px + py], send_ref.at[3 * a + j], recv_ref.at[3 * a + j],
                            (px, py, c))
                cp.wait_send()
                cp.wait_recv()

    res = pl.pallas_call(
        body, name=name, in_specs=[ANY] * (2 * n) + [SEM, SEM, ANY], out_specs=[ANY] * (2 * n),
        out_shape=_shapes(arrs) + _shapes(landing), input_output_aliases={i: i for i in range(2 * n)},
        compiler_params=pltpu.CompilerParams(has_side_effects=EFFECT))(*arrs, *landing, send, recv, after)
    return res[:n], res[n:]


def _swap_start(name, thru, arrs):
    n, nt = len(arrs), len(thru)
    landing = [lax.empty((a.shape[0], a.shape[1] // 2, a.shape[2]), a.dtype) for a in arrs]

    def body(*refs):
        ins = refs[nt:nt + n]
        send, recv = refs[nt + 2 * n], refs[nt + 2 * n + 1]
        outs = refs[2 * nt + 3 * n + 2:]
        x, y, c = _place()
        for a in range(n):
            h = ins[a].shape[1] // 2
            _rcopy(ins[a].at[:, pl.ds((1 - c) * h, h)], outs[a], send.at[a], recv.at[a], (x, y, 1 - c)).start()

    res = pl.pallas_call(
        body, name=name, in_specs=[ANY] * (nt + 2 * n), out_specs=[SEM, SEM] + [ANY] * (nt + 2 * n),
        out_shape=[pltpu.SemaphoreType.DMA((n,)), pltpu.SemaphoreType.DMA((n,))] + _shapes(thru) + _shapes(arrs)
        + _shapes(landing),
        input_output_aliases={i: 2 + i for i in range(nt + 2 * n)},
        compiler_params=pltpu.CompilerParams(has_side_effects=EFFECT))(*thru, *arrs, *landing)
    return res[0], res[1], res[2:2 + nt], res[2 + nt:2 + nt + n], res[2 + nt + n:]


def _swap_wait(name, send, recv, arrs, landing, after):
    n = len(arrs)

    def body(*refs):
        ins, land = refs[:n], refs[n:2 * n]
        send_ref, recv_ref = refs[2 * n], refs[2 * n + 1]
        x, y, c = _place()
        for a in range(n):
            h = ins[a].shape[1] // 2
            cp = _rcopy(ins[a].at[:, pl.ds((1 - c) * h, h)], land[a], send_ref.at[a], recv_ref.at[a], (x, y, 1 - c))
            cp.wait_send()
            cp.wait_recv()

    res = pl.pallas_call(
        body, name=name, in_specs=[ANY] * (2 * n) + [SEM, SEM, ANY], out_specs=[ANY] * (2 * n),
        out_shape=_shapes(arrs) + _shapes(landing), input_output_aliases={i: i for i in range(2 * n)},
        compiler_params=pltpu.CompilerParams(has_side_effects=EFFECT))(*arrs, *landing, send, recv, after)
    return res[:n], res[n:]


def _join_start(name, arrs):
    n = len(arrs)

    def body(*refs):
        ins = refs[:n]
        send, recv = refs[n], refs[n + 1]
        outs = refs[n + 2:]
        x, y, c = _place()
        for a in range(n):
            h = ins[a].shape[0] // 2
            mine = pl.ds(c * h, h)
            _rcopy(ins[a].at[mine], outs[a].at[mine], send.at[a], recv.at[a], (x, y, 1 - c)).start()

    res = pl.pallas_call(
        body, name=name, in_specs=[ANY] * n, out_specs=[SEM, SEM] + [ANY] * n,
        out_shape=[pltpu.SemaphoreType.DMA((n,)), pltpu.SemaphoreType.DMA((n,))] + _shapes(arrs),
        input_output_aliases={i: 2 + i for i in range(n)},
        compiler_params=pltpu.CompilerParams(has_side_effects=EFFECT))(*arrs)
    return res[0], res[1], res[2:]


def _join_wait(name, send, recv, arrs, after):
    n = len(arrs)

    def body(*refs):
        ins = refs[:n]
        send_ref, recv_ref = refs[n], refs[n + 1]
        x, y, c = _place()
        for a in range(n):
            h = ins[a].shape[0] // 2
            cp = _rcopy(ins[a].at[pl.ds(c * h, h)], ins[a].at[pl.ds((1 - c) * h, h)], send_ref.at[a], recv_ref.at[a],
                        (x, y, 1 - c))
            cp.wait_send()
            cp.wait_recv()

    return pl.pallas_call(
        body, name=name, in_specs=[ANY] * n + [SEM, SEM, ANY], out_specs=[ANY] * n, out_shape=_shapes(arrs),
        input_output_aliases={a: a for a in range(n)},
        compiler_params=pltpu.CompilerParams(has_side_effects=EFFECT))(*arrs, send, recv, after)


def _join_halves(name, arrs):
    n = len(arrs)

    def body(*refs):
        ins, outs = refs[:n], refs[n:2 * n]
        send, recv = refs[2 * n:]
        x, y, c = _place()
        cps = []
        for a in range(n):
            h = ins[a].shape[0] // 2
            mine = pl.ds(c * h, h)
            cp = _rcopy(ins[a].at[mine], outs[a].at[mine], send.at[a], recv.at[a], (x, y, 1 - c))
            cp.start()
            cps.append(cp)
        for a in range(n):
            h = ins[a].shape[0] // 2
            got = outs[a].at[pl.ds((1 - c) * h, h)]
            _rcopy(got, got, send.at[a], recv.at[a], (x, y, 1 - c)).wait_recv()
        for cp in cps:
            cp.wait_send()

    return pl.pallas_call(
        body, name=name, in_specs=[ANY] * n, out_specs=[ANY] * n,
        out_shape=[jax.ShapeDtypeStruct(a.shape, a.dtype) for a in arrs],
        input_output_aliases={a: a for a in range(n)},
        scratch_shapes=[pltpu.SemaphoreType.DMA((n,)), pltpu.SemaphoreType.DMA((n,))])(*arrs)


def _allgather8(name, xs, reduce):
    m_per, n = xs.shape

    def body(x_ref, out_ref, *rest):
        if reduce:
            sum_ref, send, recv, lsem = rest
        else:
            send, recv, lsem = rest
        x, y, c = _place()
        me, sib = (x, y, c), (x, y, 1 - c)
        chips = _other_chips(x, y)

        def rows(px, py, pc):
            return out_ref.at[pl.ds((4 * px + 2 * py + pc) * m_per, m_per), :]

        def copy(k, block, to, src=None):
            return _rcopy(rows(*block) if src is None else src, rows(*block), send.at[k], recv.at[k], to)

        mine = pltpu.make_async_copy(x_ref, rows(*me), lsem)
        mine.start()
        first = [copy(0, me, sib, src=x_ref)]
        first += [copy(1 + j, me, (*chip, c), src=x_ref) for j, chip in enumerate(chips)]
        for cp in first:
            cp.start()
        passed = [copy(4 + j, (*chip, c), sib) for j, chip in enumerate(chips)]
        for j, chip in enumerate(chips):
            copy(1 + j, (*chip, c), me).wait_recv()
            passed[j].start()
        copy(0, sib, me).wait_recv()
        for j, chip in enumerate(chips):
            copy(4 + j, (*chip, 1 - c), me).wait_recv()
        for cp in first + passed:
            cp.wait_send()
        mine.wait()
        if reduce:
            acc = out_ref[pl.ds(0, m_per), :]
            for dev in range(1, N_DEV):
                acc = acc + out_ref[pl.ds(dev * m_per, m_per), :]
            sum_ref[...] = acc

    vm = pl.BlockSpec(memory_space=pltpu.VMEM)
    out_shape = [jax.ShapeDtypeStruct((N_DEV * m_per, n), xs.dtype)]
    if reduce:
        out_shape.append(jax.ShapeDtypeStruct((m_per, n), xs.dtype))
    return pl.pallas_call(
        body, name=name, in_specs=[vm], out_specs=[vm] * len(out_shape), out_shape=out_shape,
        scratch_shapes=[pltpu.SemaphoreType.DMA((7,)), pltpu.SemaphoreType.DMA((7,)), pltpu.SemaphoreType.DMA],
        compiler_params=pltpu.CompilerParams(vmem_limit_bytes=VMEM_LIMIT))(xs)


def _add_my_half(name, a, rb, c_arr):
    s, h, cols = rb.shape
    tr = _tile(h, (512, 352, 256, 128, 64, 32, 16))
    nt = h // tr

    def body(c_ref, a_ref, b_ref, o_ref):
        o_ref[...] = (a_ref[...].astype(F32) + b_ref[...].astype(F32)).astype(o_ref.dtype)

    return pl.pallas_call(
        body, name=name,
        grid_spec=pltpu.PrefetchScalarGridSpec(
            num_scalar_prefetch=1, grid=(s, nt),
            in_specs=[pl.BlockSpec((None, tr, cols), lambda k, i, c: (k, c[0] * nt + i, 0)),
                      pl.BlockSpec((None, tr, cols), lambda k, i, c: (k, i, 0))],
            out_specs=pl.BlockSpec((None, tr, cols), lambda k, i, c: (k, i, 0))),
        out_shape=jax.ShapeDtypeStruct(rb.shape, BF16), compiler_params=_params(2))(c_arr, a, rb)


def _sum_chips(name, own, q, place):
    s, h, cols = q.shape
    tr = _tile(h, (512, 352, 256, 128, 64, 32, 16))
    nt = h // tr

    def body(p_ref, own_ref, q1_ref, q2_ref, q3_ref, o_ref):
        acc = own_ref[...].astype(F32) + q1_ref[...].astype(F32)
        o_ref[...] = acc + q2_ref[...].astype(F32) + q3_ref[...].astype(F32)

    def slab(t):
        return pl.BlockSpec((None, tr, cols), lambda i, p: ((p[0] + t) % s, i, 0))

    return pl.pallas_call(
        body, name=name,
        grid_spec=pltpu.PrefetchScalarGridSpec(
            num_scalar_prefetch=1, grid=(nt,), in_specs=[slab(0), slab(1), slab(2), slab(3)],
            out_specs=pl.BlockSpec((tr, cols), lambda i, p: (p[1] * nt + i, 0))),
        out_shape=jax.ShapeDtypeStruct((2 * h, cols), F32), compiler_params=_params(1))(place, own, q, q, q)


def _pack_rows(parts):
    rows = []
    for p in parts:
        flat = p.reshape(-1).astype(F32)
        n = _roundup(flat.shape[0], 8 * LANE)
        rows.append(jnp.pad(flat, (0, n - flat.shape[0])).reshape(-1, LANE))
    return jnp.concatenate(rows, axis=0)


def _unpack_rows(buf, shapes):
    out, r = [], 0
    for shp in shapes:
        size = math.prod(shp)
        nr = _roundup(size, 8 * LANE) // LANE
        out.append(buf[r:r + nr].reshape(-1)[:size].reshape(shp))
        r += nr
    return out


def kernel(x, attn_norm, gla_w_in, gla_w_a2, gla_b_a2, gla_head_norm, gla_w_out, kv_norm, w_kv, dsa_w_q, dsa_w_out, ffn_norm, ffn_w_up, ffn_conv_w, ffn_conv_b, ffn_w_down, final_norm, loss_target, m_attn_norm, m_gla_w_in, m_gla_w_a2, m_gla_b_a2, m_gla_head_norm, m_gla_w_out, m_kv_norm, m_w_kv, m_dsa_w_q, m_dsa_w_out, m_ffn_norm, m_ffn_w_up, m_ffn_conv_w, m_ffn_conv_b, m_ffn_w_down, m_final_norm, v_attn_norm, v_gla_w_in, v_gla_w_a2, v_gla_b_a2, v_gla_head_norm, v_gla_w_out, v_kv_norm, v_w_kv, v_dsa_w_q, v_dsa_w_out, v_ffn_norm, v_ffn_w_up, v_ffn_conv_w, v_ffn_conv_b, v_ffn_w_down, v_final_norm):
    lay = _layout()
    d, f = D_MODEL, D_FF
    cx, cy, cc = _place()
    chip = 2 * cx + cy
    c_arr = jnp.reshape(cc, (1,)).astype(jnp.int32)
    place = jnp.stack([chip, cc]).astype(jnp.int32)

    groups = {"A": ("gin", "small"), "B0": ("gout",), "B1": ("up0",), "B2": ("down0",), "C1": ("wkv", "wq", "dout"),
              "C2": ("up1", "down1")}
    big_shards = (gla_w_in, gla_w_out, w_kv, dsa_w_q, dsa_w_out, ffn_w_up, ffn_w_down)
    ws = _pack_weights(chip, groups["A"][:1], *big_shards)
    sharded_small = [gla_w_a2[0], gla_b_a2[0], gla_head_norm[0], ffn_conv_w]
    packed = _pack_rows(sharded_small)
    packed = jnp.pad(packed, ((0, _roundup(packed.shape[0], 16) - packed.shape[0]), (0, 0)))
    ws["small"] = lax.dynamic_update_slice(jnp.zeros((N_CHIPS,) + packed.shape, F32), packed[None], (chip, 0, 0))
    send, recv, _, arrs = _gather_start("gather_a_start", [], [ws[k] for k in groups["A"]])
    chip_arr = place[:1]
    sources = {"up0": (ffn_w_up, 0), "up1": (ffn_w_up, 1), "down0": (ffn_w_down, 0), "down1": (ffn_w_down, 1),
               "wq": (dsa_w_q, 0), "wkv": (w_kv, 0), "dout": (dsa_w_out, 0), "gout": (gla_w_out, 0)}
    later = ("B0", "B1", "B2", "C1", "C2")
    for k in sum((groups[grp] for grp in later), ()):
        ws[k] = _pack_shard(f"pack_{k}", *sources[k], chip_arr, arrs[1])
    moments = [t.reshape(-1, t.shape[-1]) for t in (m_gla_w_in, v_gla_w_in)]
    arrs = _gather_wait("gather_a_wait", send, recv, arrs, [ws["dout"]] + moments)
    ws.update(zip(groups["A"], _forward_halves("forward_a", arrs)))
    in_flight = {}
    thru = [ws[k] for k in groups["A"]]
    for grp in later:
        send, recv, thru, arrs = _gather_start(f"gather_{grp.lower()}_start", thru, [ws[k] for k in groups[grp]])
        ws.update(zip(groups[grp], arrs))
        in_flight[grp] = (send, recv)
    ws.update(zip(groups["A"], thru))
    pending = []

    class _Comm:
        def prefetch(self, grp, ws, carry):
            send, recv = in_flight[grp]
            arrs = _gather_wait(f"gather_{grp.lower()}_wait", send, recv, [ws[k] for k in groups[grp]], carry)
            send, recv, thru, arrs = _forward_start(f"forward_{grp.lower()}_start", [carry], arrs)
            in_flight[grp] = (send, recv)
            return {**ws, **dict(zip(groups[grp], arrs))}, thru[0]

        def need(self, grp, ws, after):
            send, recv = in_flight[grp]
            arrs = _forward_wait(f"forward_{grp.lower()}_wait", send, recv, [ws[k] for k in groups[grp]], after)
            return {**ws, **dict(zip(groups[grp], arrs))}

        swapping = None

        def reduce(self, grp, grads, carry):
            names = list(grads)
            send, recv, thru, parts, theirs = _swap_start(f"swap_{grp}_start", [carry], [grads[k] for k in names])
            self.swapping = (grp, names, send, recv, parts, theirs)
            return thru[0]

        def tick(self, carry):
            if self.swapping is None:
                return carry
            grp, names, send, recv, parts, theirs = self.swapping
            self.swapping = None
            parts, theirs = _swap_wait(f"swap_{grp}_wait", send, recv, parts, theirs, carry)
            return self.scatter(grp, names, parts, theirs, carry)

        def scatter(self, grp, names, parts, theirs, carry):
            sums = [_add_my_half(f"add_half_{k}", a, b, c_arr) for k, a, b in zip(names, parts, theirs)]
            send, recv, thru, sums, landing = _scatter_start(f"scatter_{grp}_start", [carry], sums)
            pending.append((grp, names, send, recv, sums, landing))
            return thru[0]

        def reduce_now(self, grp, grads, carry):
            names = list(grads)
            parts = [grads[k] for k in names]
            return self.scatter(grp, names, parts, _swap_halves(f"swap_{grp}", parts), carry)

    shards = [_unpack_rows(ws["small"][s], [p.shape for p in sharded_small]) for s in range(N_CHIPS)]
    w_a2, b_a2, head_norm, conv_w = [jnp.concatenate([shards[s][k] for s in range(N_CHIPS)], axis=-1) for k in range(4)]
    norms, small = _small_params(attn_norm, ffn_norm, kv_norm, final_norm, ffn_conv_b, w_a2, b_a2, head_norm, conv_w)

    comm = _Comm()
    loss_blk, grad_x, sm, last_big = _local_step(x[0], loss_target[0], ws, norms, small, comm)

    small_parts = [loss_blk, jnp.concatenate([sm["attn0"], sm["attn1"]]), jnp.concatenate([sm["ffn0"], sm["ffn1"]]),
                   sm["kv"], sm["final"], jnp.concatenate([sm["conv_b0"], sm["conv_b1"]]),
                   sm["w_a2p"][:GATE_RANK], sm["b_a2"], sm["head_norm"],
                   jnp.stack([jnp.concatenate(sm["conv_w0"]), jnp.concatenate(sm["conv_w1"])])]
    small_shapes = [(8, LANE), (2, d), (2, d), (d,), (d,), (2, f), (GATE_RANK, GLA_KEY_DIM), (GLA_KEY_DIM,),
                    (GLA_VAL_DIM // GLA_HEADS,), (2, 3, f)]
    _, reduced = _allgather8("reduce_small", _pack_rows(small_parts), True)
    reduced = comm.reduce_now("gla", last_big, reduced)

    loss_r, g_attn, g_ffn, g_kv, g_final, g_cb, g_a2, g_ba2, g_hn, g_cw = _unpack_rows(reduced, small_shapes)
    loss = loss_r[0, 0]

    def mine(g, axis):
        w = g.shape[axis] // N_CHIPS
        return lax.dynamic_slice_in_dim(g, chip * w, w, axis)

    grads = {
        "attn_norm": g_attn, "gla_w_a2": mine(g_a2, 1)[None], "gla_b_a2": mine(g_ba2, 0)[None],
        "gla_head_norm": mine(g_hn, 0)[None], "kv_norm": g_kv, "ffn_norm": g_ffn, "ffn_conv_w": mine(g_cw, 2),
        "ffn_conv_b": g_cb, "final_norm": g_final,
    }
    weights = {"attn_norm": (attn_norm, m_attn_norm, v_attn_norm), "gla_w_in": (gla_w_in, m_gla_w_in, v_gla_w_in),
               "gla_w_a2": (gla_w_a2, m_gla_w_a2, v_gla_w_a2), "gla_b_a2": (gla_b_a2, m_gla_b_a2, v_gla_b_a2),
               "gla_head_norm": (gla_head_norm, m_gla_head_norm, v_gla_head_norm),
               "gla_w_out": (gla_w_out, m_gla_w_out, v_gla_w_out), "kv_norm": (kv_norm, m_kv_norm, v_kv_norm),
               "w_kv": (w_kv, m_w_kv, v_w_kv), "dsa_w_q": (dsa_w_q, m_dsa_w_q, v_dsa_w_q),
               "dsa_w_out": (dsa_w_out, m_dsa_w_out, v_dsa_w_out), "ffn_norm": (ffn_norm, m_ffn_norm, v_ffn_norm),
               "ffn_w_up": (ffn_w_up, m_ffn_w_up, v_ffn_w_up), "ffn_conv_w": (ffn_conv_w, m_ffn_conv_w, v_ffn_conv_w),
               "ffn_conv_b": (ffn_conv_b, m_ffn_conv_b, v_ffn_conv_b),
               "ffn_w_down": (ffn_w_down, m_ffn_w_down, v_ffn_w_down), "final_norm": (final_norm, m_final_norm, v_final_norm)}
    order = list(weights)
    big_names = ("gla_w_in", "gla_w_out", "w_kv", "dsa_w_q", "dsa_w_out", "ffn_w_up", "ffn_w_down")
    delta, new_m, new_v = {}, {}, {}

    def adam_big(k, g):
        w, m, v = weights[k]
        cols = w.shape[-1]
        res = _adamw(f"adamw_{k}", w.reshape(-1, cols), g.reshape(-1, g.shape[-1]), m.reshape(-1, cols), v.reshape(-1, cols))
        delta[k], new_m[k], new_v[k] = [r.reshape(w.shape) for r in res[:3]]
        grads[k] = res[3].reshape(w.shape) if len(res) == 4 else g
        return res[0]

    full = {}
    after = reduced
    joining = []
    for grp, names, send, recv, sums, landing in pending[:-1]:
        sums, landing = _scatter_wait(f"scatter_{grp}_wait", send, recv, sums, landing, after)
        halves = [_sum_chips(f"sum_chips_{k}", s, q, place) for k, s, q in zip(names, sums, landing)]
        send, recv, halves = _join_start(f"join_{grp}_start", halves)
        joining.append((grp, names, send, recv, halves))
        after = halves[0]
    for grp, names, send, recv, halves in joining:
        joined = _join_wait(f"join_{grp}_wait", send, recv, halves, after)
        full.update(zip(names, joined))
        after = joined[0]
    after = adam_big("w_kv", full["wkv"])
    after = adam_big("dsa_w_q", full["wq"][None])
    after = adam_big("dsa_w_out", full["dout"][None])
    for k, g0, g1 in (("ffn_w_up", "up0", "up1"), ("ffn_w_down", "down0", "down1")):
        delta[k], new_m[k], new_v[k], grads[k] = _adamw_layers(f"adamw_{k}", weights[k][0], (full[g0], full[g1]),
                                                               weights[k][1], weights[k][2])
        after = delta[k]
    grp, names, send, recv, sums, landing = pending[-1]
    sums, landing = _scatter_wait(f"scatter_{grp}_wait", send, recv, sums, landing, after)
    halves = [_sum_chips(f"sum_chips_{k}", s, q, place) for k, s, q in zip(names, sums, landing)]
    full.update(zip(names, _join_halves(f"join_{grp}", halves)))
    adam_big("gla_w_in", full["gin"])
    adam_big("gla_w_out", full["gout"][None])
    small_names = [k for k in order if k not in big_names]
    packed = [_pack_rows([src[k] for k in small_names])
              for src in ({k: weights[k][0] for k in small_names}, grads, {k: weights[k][1] for k in small_names},
                          {k: weights[k][2] for k in small_names})]
    res = _adamw("adamw_small", *packed)
    shapes = [weights[k][0].shape for k in small_names]
    for dst, buf in zip((delta, new_m, new_v), res):
        for k, val in zip(small_names, _unpack_rows(buf, shapes)):
            dst[k] = val
    return (loss, grad_x[None], *[grads[k] for k in order], *[delta[k] for k in order], *[new_m[k] for k in order],
            *[new_v[k] for k in order])
```

```python
import math

import jax
import jax.numpy as jnp
from jax import lax
from jax.experimental import pallas as pl
from jax.experimental.pallas import tpu as pltpu

F32 = jnp.float32
BF16 = jnp.bfloat16

D_MODEL = 2048
SEQ = 4096
GLA_HEADS = 4
GLA_KEY_DIM = D_MODEL // 2
GLA_VAL_DIM = D_MODEL
GATE_RANK = 16
GATE_NORMALIZER = 16.0
GLA_CHUNK = 64
ATT_HEADS = 16
HEAD_DIM = 128
WINDOWS = (128, 512, 2048)
DILATIONS = (1, 4, 16)
ATT_BLOCK = 128
D_FF = 5632
EPS = 1e-6
ADAM_LR = 0.001
ADAM_B1 = 0.9
ADAM_B2 = 0.999
ADAM_EPS = 1e-08
ADAM_WD = 0.01
ADAM_STEP = 10

N_CHIPS = 4
N_DEV = 8
LANE = 128
A_PAD = 128
VMEM_LIMIT = 56 * 1024 * 1024
MAX_K_TILE = 2816
NEG = -1e30
MESH = pl.DeviceIdType.MESH

NN = (((1,), (0,)), ((), ()))
NT = (((1,), (1,)), ((), ()))
TN = (((0,), (0,)), ((), ()))


def _tile(n, cands):
    for c in cands:
        if c <= n and n % c == 0:
            return c
    return n


def _roundup(n, m):
    return -(-n // m) * m


def _params(n_axes):
    return pltpu.CompilerParams(dimension_semantics=("arbitrary",) * n_axes, vmem_limit_bytes=VMEM_LIMIT)


def _dot(a, b, dims):
    return lax.dot_general(a, b, dims, preferred_element_type=F32)


def _sigmoid(x):
    return 1.0 / (1.0 + jnp.exp(-x))


COL_SHARDED = ("gin", "up0", "up1", "wq", "wkv")
ROW_SHARDED = ("gout", "down0", "down1", "dout")


def _layout():
    f = D_FF
    hd = ATT_HEADS * HEAD_DIM
    gin = 2 * GLA_KEY_DIM + 2 * GLA_VAL_DIM + GATE_RANK
    up_w = 2 * f // N_CHIPS
    q_w = 3 * hd // N_CHIPS
    kv_w = 2 * hd // N_CHIPS
    dn_r = f // N_CHIPS
    go_r = GLA_VAL_DIM // N_CHIPS
    do_r = hd // N_CHIPS
    big = (1408, 1024, 512, 256, 128)
    return {
        "gin": (0, gin // N_CHIPS, LANE),
        "up0": (0, up_w, _tile(up_w, big)), "up1": (0, up_w, _tile(up_w, big)),
        "wq": (0, q_w, _tile(q_w, (1536, 768, 512, 384, 256, 128))), "wkv": (0, kv_w, _tile(kv_w, (1024, 512, 256, 128))),
        "down0": (0, dn_r, _tile(dn_r, big)), "down1": (0, dn_r, _tile(dn_r, big)),
        "gout": (0, go_r, _tile(go_r, (512, 256, 128))), "dout": (0, do_r, _tile(do_r, (512, 256, 128))),
    }


def _matmul(name, a, b, dims, grid, a_spec, b_spec, o_spec, out_shape, acc_shape, add=None, add_spec=None):
    nk = grid[2]
    has_add = add is not None

    def body(*refs):
        a_ref, b_ref = refs[0], refs[1]
        pos = 2
        add_ref = None
        if has_add:
            add_ref = refs[pos]
            pos += 1
        o_ref = refs[pos]
        prod = _dot(a_ref[...].astype(BF16), b_ref[...].astype(BF16), dims)

        def finish(val):
            if has_add:
                val = val + add_ref[...].astype(F32)
            o_ref[...] = val.astype(o_ref.dtype)

        if nk == 1:
            finish(prod)
        else:
            acc_ref = refs[pos + 1]
            k = pl.program_id(2)

            @pl.when(k == 0)
            def _():
                acc_ref[...] = prod

            @pl.when(k > 0)
            def _():
                acc_ref[...] += prod

            @pl.when(k == nk - 1)
            def _():
                finish(acc_ref[...])

    in_specs = [a_spec, b_spec]
    args = [a, b]
    if has_add:
        in_specs.append(add_spec)
        args.append(add)
    scratch = [] if nk == 1 else [pltpu.VMEM(acc_shape, F32)]
    return pl.pallas_call(body, name=name, grid=grid, in_specs=in_specs, out_specs=o_spec, out_shape=out_shape,
                          scratch_shapes=scratch, compiler_params=_params(3))(*args)


def _mm_act_wc(name, a, wc, seg, out_dtype):
    off, w, tn = seg
    if off == 0 and w <= MAX_K_TILE:
        tn = w
    t_len, d = a.shape
    tm = _tile(t_len, (1024, 512, 256, 128))
    nps = w // tn
    ob = off // tn
    grid = (t_len // tm, N_CHIPS * nps, 1)
    return _matmul(
        name, a, wc, NN, grid,
        pl.BlockSpec((tm, d), lambda i, j, k: (i, 0)),
        pl.BlockSpec((None, d, tn), lambda i, j, k: (j // nps, 0, ob + j % nps)),
        pl.BlockSpec((tm, tn), lambda i, j, k: (i, j)),
        jax.ShapeDtypeStruct((t_len, N_CHIPS * w), out_dtype), (tm, tn))


def _mm_dact_wcT(name, dy, wc, seg, add=None):
    off, w, tk = seg
    if off == 0 and w <= MAX_K_TILE:
        tk = w
    t_len = dy.shape[0]
    d = wc.shape[1]
    tm = _tile(t_len, (1024, 512, 256, 128))
    tn = _tile(d, (1024, 512, 256, 128))
    kps = w // tk
    ob = off // tk
    grid = (t_len // tm, d // tn, N_CHIPS * kps)
    return _matmul(
        name, dy, wc, NT, grid,
        pl.BlockSpec((tm, tk), lambda i, j, k: (i, k)),
        pl.BlockSpec((None, tn, tk), lambda i, j, k: (k // kps, j, ob + k % kps)),
        pl.BlockSpec((tm, tn), lambda i, j, k: (i, j)),
        jax.ShapeDtypeStruct((t_len, d), F32), (tm, tn),
        add=add, add_spec=None if add is None else pl.BlockSpec((tm, tn), lambda i, j, k: (i, j)))


def _mm_grad_wc(name, a, dy, seg):
    _, w, tn = seg
    t_len, d = a.shape
    tm = _tile(d, (1024, 512, 256, 128))
    tk = _tile(t_len, (2048, 1024, 512, 256, 128))
    nps = w // tn
    grid = (d // tm, N_CHIPS * nps, t_len // tk)
    return _matmul(
        name, a, dy, TN, grid,
        pl.BlockSpec((tk, tm), lambda i, j, k: (k, i)),
        pl.BlockSpec((tk, tn), lambda i, j, k: (k, j)),
        pl.BlockSpec((None, tm, tn), lambda i, j, k: (j // nps, i, j % nps)),
        jax.ShapeDtypeStruct((N_CHIPS, d, w), BF16), (tm, tn))


def _is_plain(wr, seg):
    return seg[0] == 0 and wr.shape[1] == seg[1] and (N_CHIPS * seg[1]) % 1024 == 0


def _mm_act_wr(name, a, wr, seg, add):
    off, r, tk = seg
    t_len = a.shape[0]
    d = wr.shape[2]
    if seg[0] == 0 and wr.shape[1] == r:
        return _mm_plain(name, a, wr.reshape(N_CHIPS * r, d), NN, F32, add=add)
    tm = _tile(t_len, (1024, 512, 256, 128))
    tn = _tile(d, (1024, 512, 256, 128))
    kps = r // tk
    ob = off // tk
    grid = (t_len // tm, d // tn, N_CHIPS * kps)
    return _matmul(
        name, a, wr, NN, grid,
        pl.BlockSpec((tm, tk), lambda i, j, k: (i, k)),
        pl.BlockSpec((None, tk, tn), lambda i, j, k: (k // kps, ob + k % kps, j)),
        pl.BlockSpec((tm, tn), lambda i, j, k: (i, j)),
        jax.ShapeDtypeStruct((t_len, d), F32), (tm, tn),
        add=add, add_spec=pl.BlockSpec((tm, tn), lambda i, j, k: (i, j)))


def _mm_dact_wrT(name, dh, wr, seg):
    off, r, tn = seg
    t_len, d = dh.shape
    if _is_plain(wr, seg):
        return _mm_plain(name, dh, wr.reshape(N_CHIPS * r, d), NT, BF16)
    tm = _tile(t_len, (1024, 512, 256, 128))
    nps = r // tn
    ob = off // tn
    grid = (t_len // tm, N_CHIPS * nps, 1)
    return _matmul(
        name, dh, wr, NT, grid,
        pl.BlockSpec((tm, d), lambda i, j, k: (i, 0)),
        pl.BlockSpec((None, tn, d), lambda i, j, k: (j // nps, ob + j % nps, 0)),
        pl.BlockSpec((tm, tn), lambda i, j, k: (i, j)),
        jax.ShapeDtypeStruct((t_len, N_CHIPS * r), BF16), (tm, tn))


def _mm_grad_wr(name, a, dh, seg):
    _, r, tm = seg
    t_len, d = dh.shape
    if (N_CHIPS * r) % 1024 == 0:
        return _mm_plain(name, a, dh, TN, BF16).reshape(N_CHIPS, r, d)
    tn = _tile(d, (1024, 512, 256, 128))
    tk = _tile(t_len, (2048, 1024, 512, 256, 128))
    mps = r // tm
    grid = (N_CHIPS * mps, d // tn, t_len // tk)
    return _matmul(
        name, a, dh, TN, grid,
        pl.BlockSpec((tk, tm), lambda i, j, k: (k, i)),
        pl.BlockSpec((tk, tn), lambda i, j, k: (k, j)),
        pl.BlockSpec((None, tm, tn), lambda i, j, k: (i // mps, i % mps, j)),
        jax.ShapeDtypeStruct((N_CHIPS, r, d), BF16), (tm, tn))


def _mm_plain(name, a, b, dims, out_dtype, add=None):
    if dims == NN:
        m, kd = a.shape
        n = b.shape[1]
    elif dims == NT:
        m, kd = a.shape
        n = b.shape[0]
    else:
        kd, m = a.shape
        n = b.shape[1]
    tm = _tile(m, (1024, 512, 256, 128))
    tn = _tile(n, (1024, 768, 512, 256, 128))
    tk = _tile(kd, (MAX_K_TILE, 2048, 1408, 1024, 512, 256, 128))
    grid = (m // tm, n // tn, kd // tk)
    if dims == NN:
        a_spec = pl.BlockSpec((tm, tk), lambda i, j, k: (i, k))
        b_spec = pl.BlockSpec((tk, tn), lambda i, j, k: (k, j))
    elif dims == NT:
        a_spec = pl.BlockSpec((tm, tk), lambda i, j, k: (i, k))
        b_spec = pl.BlockSpec((tn, tk), lambda i, j, k: (j, k))
    else:
        a_spec = pl.BlockSpec((tk, tm), lambda i, j, k: (k, i))
        b_spec = pl.BlockSpec((tk, tn), lambda i, j, k: (k, j))
    o_spec = pl.BlockSpec((tm, tn), lambda i, j, k: (i, j))
    return _matmul(name, a, b, dims, grid, a_spec, b_spec, o_spec, jax.ShapeDtypeStruct((m, n), out_dtype), (tm, tn),
                   add=add, add_spec=None if add is None else o_spec)


def _rms_fwd(name, x, g):
    t_len, d = x.shape
    tm = _tile(t_len, (512, 256, 128))

    def body(x_ref, g_ref, o_ref):
        xv = x_ref[...]
        r = lax.rsqrt(jnp.mean(xv * xv, axis=-1, keepdims=True) + EPS)
        o_ref[...] = (xv * r * g_ref[...]).astype(o_ref.dtype)

    return pl.pallas_call(
        body, name=name, grid=(t_len // tm,),
        in_specs=[pl.BlockSpec((tm, d), lambda i: (i, 0)), pl.BlockSpec((1, d), lambda i: (0, 0))],
        out_specs=pl.BlockSpec((tm, d), lambda i: (i, 0)),
        out_shape=jax.ShapeDtypeStruct((t_len, d), BF16), compiler_params=_params(1))(x, g)


def _rms_bwd(name, dys, x, gs, dres):
    t_len, d = x.shape
    n = len(dys)
    tm = _tile(t_len, (256, 128))

    def body(*refs):
        dy_refs, x_ref, g_refs = refs[:n], refs[n], refs[n + 1:2 * n + 1]
        dres_ref, dx_ref, dg_refs = refs[2 * n + 1], refs[2 * n + 2], refs[2 * n + 3:]
        xv = x_ref[...]
        r = lax.rsqrt(jnp.mean(xv * xv, axis=-1, keepdims=True) + EPS)
        xhat = xv * r
        dyv = [ref[...].astype(F32) for ref in dy_refs]
        dxn = dyv[0] * g_refs[0][...]
        for k in range(1, n):
            dxn = dxn + dyv[k] * g_refs[k][...]
        dx = r * (dxn - xhat * jnp.mean(dxn * xhat, axis=-1, keepdims=True))
        dx_ref[...] = dres_ref[...] + dx
        parts = [jnp.sum(v * xhat, axis=0, keepdims=True) for v in dyv]

        @pl.when(pl.program_id(0) == 0)
        def _():
            for ref, p in zip(dg_refs, parts):
                ref[...] = p

        @pl.when(pl.program_id(0) > 0)
        def _():
            for ref, p in zip(dg_refs, parts):
                ref[...] += p

    row = pl.BlockSpec((tm, d), lambda i: (i, 0))
    vec = pl.BlockSpec((1, d), lambda i: (0, 0))
    res = pl.pallas_call(
        body, name=name, grid=(t_len // tm,), in_specs=[row] * (n + 1) + [vec] * n + [row], out_specs=(row,) + (vec,) * n,
        out_shape=(jax.ShapeDtypeStruct((t_len, d), F32),) + (jax.ShapeDtypeStruct((1, d), F32),) * n,
        compiler_params=_params(1))(*dys, x, *gs, dres)
    return res[0], res[1:]


def _loss_head(h, g, target):
    t_len, d = h.shape
    tm = _tile(t_len, (256, 128))

    def body(h_ref, g_ref, t_ref, dh_ref, dg_ref, loss_ref):
        xv = h_ref[...]
        gv = g_ref[...]
        r = lax.rsqrt(jnp.mean(xv * xv, axis=-1, keepdims=True) + EPS)
        xhat = xv * r
        err = xhat * gv - t_ref[...]
        dyv = err * (1.0 / d)
        dxn = dyv * gv
        dh_ref[...] = r * (dxn - xhat * jnp.mean(dxn * xhat, axis=-1, keepdims=True))
        part = jnp.sum(dyv * xhat, axis=0, keepdims=True)
        lpart = jnp.zeros((8, LANE), F32) + (0.5 / d) * jnp.sum(err * err)

        @pl.when(pl.program_id(0) == 0)
        def _():
            dg_ref[...] = part
            loss_ref[...] = lpart

        @pl.when(pl.program_id(0) > 0)
        def _():
            dg_ref[...] += part
            loss_ref[...] += lpart

    row = pl.BlockSpec((tm, d), lambda i: (i, 0))
    vec = pl.BlockSpec((1, d), lambda i: (0, 0))
    return pl.pallas_call(
        body, name="loss_head", grid=(t_len // tm,), in_specs=[row, vec, row],
        out_specs=(row, vec, pl.BlockSpec((8, LANE), lambda i: (0, 0))),
        out_shape=(jax.ShapeDtypeStruct((t_len, d), F32), jax.ShapeDtypeStruct((1, d), F32),
                   jax.ShapeDtypeStruct((8, LANE), F32)),
        compiler_params=_params(1))(h, g, target)


def _chunk_row(shape):
    return lax.broadcasted_iota(jnp.int32, shape, 0) % GLA_CHUNK


def _gla_gate_fwd(a, w_a2p, b_a2):
    t_len = a.shape[0]
    kd = w_a2p.shape[1]
    tm = _tile(t_len, (256, 128, 64))

    def body(a_ref, w_ref, b_ref, ga_ref, cum_ref):
        ga = _dot(a_ref[...], w_ref[...].astype(BF16), NN) + b_ref[...]
        ga_ref[...] = ga
        la = (jnp.minimum(ga, 0.0) - jnp.log(1.0 + jnp.exp(-jnp.abs(ga)))) * (1.0 / GATE_NORMALIZER)
        row = _chunk_row(la.shape)
        s = 1
        while s < GLA_CHUNK:
            la = la + jnp.where(row >= s, pltpu.roll(la, s, 0), 0.0)
            s *= 2
        cum_ref[...] = la

    return pl.pallas_call(
        body, name="gla_gate_fwd", grid=(t_len // tm,),
        in_specs=[pl.BlockSpec((tm, A_PAD), lambda i: (i, 0)), pl.BlockSpec((A_PAD, kd), lambda i: (0, 0)),
                  pl.BlockSpec((1, kd), lambda i: (0, 0))],
        out_specs=(pl.BlockSpec((tm, kd), lambda i: (i, 0)), pl.BlockSpec((tm, kd), lambda i: (i, 0))),
        out_shape=(jax.ShapeDtypeStruct((t_len, kd), F32), jax.ShapeDtypeStruct((t_len, kd), F32)),
        compiler_params=_params(1))(a, w_a2p, b_a2)


def _gla_gate_bwd(dcum, ga, a, w_a2p):
    t_len, kd = dcum.shape
    tm = _tile(t_len, (256, 128, 64))

    def body(dc_ref, ga_ref, a_ref, w_ref, da_ref, dw_ref, db_ref):
        x = dc_ref[...]
        row = _chunk_row(x.shape)
        s = 1
        while s < GLA_CHUNK:
            x = x + jnp.where(row < GLA_CHUNK - s, pltpu.roll(x, tm - s, 0), 0.0)
            s *= 2
        dga = x * (1.0 / GATE_NORMALIZER) * _sigmoid(-ga_ref[...])
        dgab = dga.astype(BF16)
        da_ref[...] = _dot(dgab, w_ref[...].astype(BF16), NT).astype(da_ref.dtype)
        dw = _dot(a_ref[...], dgab, TN)
        db = jnp.sum(dga, axis=0, keepdims=True)

        @pl.when(pl.program_id(0) == 0)
        def _():
            dw_ref[...] = dw
            db_ref[...] = db

        @pl.when(pl.program_id(0) > 0)
        def _():
            dw_ref[...] += dw
            db_ref[...] += db

    wide = pl.BlockSpec((tm, kd), lambda i: (i, 0))
    return pl.pallas_call(
        body, name="gla_gate_bwd", grid=(t_len // tm,),
        in_specs=[wide, wide, pl.BlockSpec((tm, A_PAD), lambda i: (i, 0)), pl.BlockSpec((A_PAD, kd), lambda i: (0, 0))],
        out_specs=(pl.BlockSpec((tm, A_PAD), lambda i: (i, 0)), pl.BlockSpec((A_PAD, kd), lambda i: (0, 0)),
                   pl.BlockSpec((1, kd), lambda i: (0, 0))),
        out_shape=(jax.ShapeDtypeStruct((t_len, A_PAD), BF16), jax.ShapeDtypeStruct((A_PAD, kd), F32),
                   jax.ShapeDtypeStruct((1, kd), F32)),
        compiler_params=_params(1))(dcum, ga, a, w_a2p)


GLA_STEP_CHUNKS = 8


def _gla_dims():
    dk = GLA_KEY_DIM // GLA_HEADS
    dv = GLA_VAL_DIM // GLA_HEADS
    return dk, dv


def _gla_fwd(proj, cum):
    t_len = proj.shape[0]
    dk, dv = _gla_dims()
    nc = t_len // GLA_CHUNK
    c = GLA_CHUNK
    scale = dk ** -0.5
    v0 = 2 * GLA_KEY_DIM // dv

    per = _tile(nc, (GLA_STEP_CHUNKS, 2, 1))
    rows = per * c

    def body(q_ref, k_ref, v_ref, cum_ref, o_ref, st_ref, s_scr):
        @pl.when(pl.program_id(1) == 0)
        def _():
            s_scr[...] = jnp.zeros_like(s_scr)

        tri = lax.broadcasted_iota(jnp.int32, (c, c), 0) >= lax.broadcasted_iota(jnp.int32, (c, c), 1)
        for i in range(per):
            rs = slice(i * c, (i + 1) * c)
            cm = cum_ref[rs, :]
            last = cm[c - 1:c, :]
            q = q_ref[rs, :].astype(F32) * scale
            k = k_ref[rs, :].astype(F32)
            v = v_ref[rs, :].astype(BF16)
            qd = (q * jnp.exp(cm)).astype(BF16)
            ki = (k * jnp.exp(-cm)).astype(BF16)
            ke = (k * jnp.exp(last - cm)).astype(BF16)
            sc = jnp.where(tri, _dot(qd, ki, NT), 0.0)
            st = s_scr[...]
            st_ref[i] = st
            o_ref[rs, :] = _dot(sc.astype(BF16), v, NN) + _dot(qd, st.astype(BF16), NT)
            s_scr[...] = st * jnp.exp(last) + _dot(v, ke, TN)

    return pl.pallas_call(
        body, name="gla_fwd", grid=(GLA_HEADS, nc // per),
        in_specs=[pl.BlockSpec((rows, dk), lambda h, n: (n, h)),
                  pl.BlockSpec((rows, dk), lambda h, n: (n, GLA_HEADS + h)),
                  pl.BlockSpec((rows, dv), lambda h, n: (n, v0 + h)),
                  pl.BlockSpec((rows, dk), lambda h, n: (n, h))],
        out_specs=(pl.BlockSpec((rows, dv), lambda h, n: (n, h)),
                   pl.BlockSpec((None, per, dv, dk), lambda h, n: (h, n, 0, 0))),
        out_shape=(jax.ShapeDtypeStruct((t_len, GLA_VAL_DIM), F32),
                   jax.ShapeDtypeStruct((GLA_HEADS, nc, dv, dk), F32)),
        scratch_shapes=[pltpu.VMEM((dv, dk), F32)], compiler_params=_params(2))(proj, proj, proj, cum)


def _gla_bwd(proj, cum, states, do):
    t_len = proj.shape[0]
    dk, dv = _gla_dims()
    nc = t_len // GLA_CHUNK
    c = GLA_CHUNK
    scale = dk ** -0.5
    v0 = 2 * GLA_KEY_DIM // dv

    per = _tile(nc, (GLA_STEP_CHUNKS, 2, 1))
    rows = per * c

    def body(q_ref, k_ref, v_ref, cum_ref, st_ref, do_ref, dq_ref, dk_ref, dv_ref, dc_ref, ds_scr):
        @pl.when(pl.program_id(1) == 0)
        def _():
            ds_scr[...] = jnp.zeros_like(ds_scr)

        tri = lax.broadcasted_iota(jnp.int32, (c, c), 0) >= lax.broadcasted_iota(jnp.int32, (c, c), 1)
        row = lax.broadcasted_iota(jnp.int32, (c, dk), 0)
        for i in reversed(range(per)):
            rs = slice(i * c, (i + 1) * c)
            cm = cum_ref[rs, :]
            last = cm[c - 1:c, :]
            e_c = jnp.exp(cm)
            e_nc = jnp.exp(-cm)
            e_lc = jnp.exp(last - cm)
            e_l = jnp.exp(last)
            q = q_ref[rs, :].astype(F32) * scale
            k = k_ref[rs, :].astype(F32)
            v = v_ref[rs, :].astype(BF16)
            dov = do_ref[rs, :]
            qd32 = q * e_c
            ki32 = k * e_nc
            ke32 = k * e_lc
            qd = qd32.astype(BF16)
            ki = ki32.astype(BF16)
            ke = ke32.astype(BF16)
            st = st_ref[i]
            dst = ds_scr[...]
            dstb = dst.astype(BF16)
            am = jnp.where(tri, _dot(dov, v, NT), 0.0).astype(BF16)
            pm = jnp.where(tri, _dot(qd, ki, NT), 0.0).astype(BF16)
            dqd = _dot(am, ki, NN) + _dot(dov, st.astype(BF16), NN)
            dki = _dot(am, qd, TN)
            dvv = _dot(pm, dov, TN) + _dot(ke, dstb, NT)
            dke = _dot(v, dstb, NN)
            d_el = jnp.sum(dst * st, axis=0, keepdims=True)
            ds_scr[...] = dst * e_l + _dot(dov, qd, TN)
            dq_ref[rs, :] = (dqd * scale * e_c).astype(dq_ref.dtype)
            dk_ref[rs, :] = (dki * e_nc + dke * e_lc).astype(dk_ref.dtype)
            dv_ref[rs, :] = dvv.astype(dv_ref.dtype)
            dkeke = dke * ke32
            dcum = dqd * qd32 - dki * ki32 - dkeke
            dlast = jnp.sum(dkeke, axis=0, keepdims=True) + d_el * e_l
            dc_ref[rs, :] = jnp.where(row == c - 1, dcum + dlast, dcum)

    rev = nc // per - 1
    return pl.pallas_call(
        body, name="gla_bwd", grid=(GLA_HEADS, nc // per),
        in_specs=[pl.BlockSpec((rows, dk), lambda h, n: (rev - n, h)),
                  pl.BlockSpec((rows, dk), lambda h, n: (rev - n, GLA_HEADS + h)),
                  pl.BlockSpec((rows, dv), lambda h, n: (rev - n, v0 + h)),
                  pl.BlockSpec((rows, dk), lambda h, n: (rev - n, h)),
                  pl.BlockSpec((None, per, dv, dk), lambda h, n: (h, rev - n, 0, 0)),
                  pl.BlockSpec((rows, dv), lambda h, n: (rev - n, h))],
        out_specs=(pl.BlockSpec((rows, dk), lambda h, n: (rev - n, h)),
                   pl.BlockSpec((rows, dk), lambda h, n: (rev - n, h)),
                   pl.BlockSpec((rows, dv), lambda h, n: (rev - n, h)),
                   pl.BlockSpec((rows, dk), lambda h, n: (rev - n, h))),
        out_shape=(jax.ShapeDtypeStruct((t_len, GLA_KEY_DIM), BF16), jax.ShapeDtypeStruct((t_len, GLA_KEY_DIM), BF16),
                   jax.ShapeDtypeStruct((t_len, GLA_VAL_DIM), BF16), jax.ShapeDtypeStruct((t_len, GLA_KEY_DIM), F32)),
        scratch_shapes=[pltpu.VMEM((dv, dk), F32)], compiler_params=_params(2))(proj, proj, proj, cum, states, do)


def _gla_out_fwd(o, proj, gn):
    t_len = o.shape[0]
    _, dv = _gla_dims()
    tm = _tile(t_len, (512, 256, 128))
    r0 = (2 * GLA_KEY_DIM + GLA_VAL_DIM) // dv

    def body(o_ref, r_ref, g_ref, y_ref):
        ov = o_ref[...]
        rs = lax.rsqrt(jnp.mean(ov * ov, axis=-1, keepdims=True) + EPS)
        rv = r_ref[...].astype(F32)
        y_ref[...] = (ov * rs * g_ref[...] * (rv * _sigmoid(rv))).astype(y_ref.dtype)

    return pl.pallas_call(
        body, name="gla_out_fwd", grid=(t_len // tm, GLA_HEADS),
        in_specs=[pl.BlockSpec((tm, dv), lambda i, h: (i, h)), pl.BlockSpec((tm, dv), lambda i, h: (i, r0 + h)),
                  pl.BlockSpec((1, dv), lambda i, h: (0, 0))],
        out_specs=pl.BlockSpec((tm, dv), lambda i, h: (i, h)),
        out_shape=jax.ShapeDtypeStruct((t_len, GLA_VAL_DIM), BF16), compiler_params=_params(2))(o, proj, gn)


def _gla_out_bwd(dy, o, proj, gn):
    t_len = o.shape[0]
    _, dv = _gla_dims()
    tm = _tile(t_len, (512, 256, 128))
    r0 = (2 * GLA_KEY_DIM + GLA_VAL_DIM) // dv

    def body(dy_ref, o_ref, r_ref, g_ref, do_ref, dr_ref, dg_ref):
        ov = o_ref[...]
        gv = g_ref[...]
        rs = lax.rsqrt(jnp.mean(ov * ov, axis=-1, keepdims=True) + EPS)
        xhat = ov * rs
        rv = r_ref[...].astype(F32)
        sg = _sigmoid(rv)
        gate = rv * sg
        dyv = dy_ref[...].astype(F32)
        dn = dyv * gate
        dr_ref[...] = (dyv * xhat * gv * (sg * (1.0 + rv * (1.0 - sg)))).astype(dr_ref.dtype)
        dxn = dn * gv
        do_ref[...] = (rs * (dxn - xhat * jnp.mean(dxn * xhat, axis=-1, keepdims=True))).astype(do_ref.dtype)
        part = jnp.sum(dn * xhat, axis=0, keepdims=True)
        first = (pl.program_id(0) == 0) & (pl.program_id(1) == 0)

        @pl.when(first)
        def _():
            dg_ref[...] = part

        @pl.when(jnp.logical_not(first))
        def _():
            dg_ref[...] += part

    blk = pl.BlockSpec((tm, dv), lambda i, h: (i, h))
    return pl.pallas_call(
        body, name="gla_out_bwd", grid=(t_len // tm, GLA_HEADS),
        in_specs=[blk, blk, pl.BlockSpec((tm, dv), lambda i, h: (i, r0 + h)), pl.BlockSpec((1, dv), lambda i, h: (0, 0))],
        out_specs=(blk, blk, pl.BlockSpec((1, dv), lambda i, h: (0, 0))),
        out_shape=(jax.ShapeDtypeStruct((t_len, GLA_VAL_DIM), BF16), jax.ShapeDtypeStruct((t_len, GLA_VAL_DIM), BF16),
                   jax.ShapeDtypeStruct((1, dv), F32)),
        compiler_params=_params(2))(dy, o, proj, gn)


def _alibi_slopes():
    n = ATT_HEADS
    start = 2.0 ** (-8.0 / n)
    return [start ** (i + 1) for i in range(n)]


def _att_masks(d):
    b = ATT_BLOCK
    qa = lax.broadcasted_iota(jnp.int32, (b, b), 0)
    kb = lax.broadcasted_iota(jnp.int32, (b, b), 1)
    dist_c = qa - kb
    dist_p = qa - kb + b
    return dist_c >= 0, dist_p <= b, (dist_c * d).astype(F32), (dist_p * d).astype(F32)


def _to_dilated(name, x, d, c0=0, w=None):
    part = x if w is None else x[:, c0:c0 + w]
    return part.reshape(x.shape[0] // d, -1)


def _from_dilated(name, y, d):
    return y.reshape(y.shape[0] * d, y.shape[1] // d)


def _att_views(q_all, kv, g):
    d = DILATIONS[g]
    hd = ATT_HEADS * HEAD_DIM
    if d == 1:
        return q_all, kv
    return _to_dilated(f"q_dilated{g}", q_all, d, g * hd, hd), _to_dilated(f"kv_dilated{g}", kv, d)


def _att_fwd(views, g):
    d = DILATIONS[g]
    assert WINDOWS[g] // d == ATT_BLOCK
    qv, kvv = views
    hd = ATT_HEADS * HEAD_DIM
    sub = kvv.shape[0]
    t_len = sub * d
    nb = sub // ATT_BLOCK
    b = ATT_BLOCK
    e = HEAD_DIM
    scale = e ** -0.5
    slopes = _alibi_slopes()
    qc = (lambda r: 3 * r + g) if d == 1 else (lambda r: r)

    def body(q_ref, kp_ref, kc_ref, vp_ref, vc_ref, o_ref, l_ref, s_scr, p_scr, li_scr):
        ib = pl.program_id(1)
        valid_c, valid_p0, dist_c, dist_p = _att_masks(d)
        valid_p = valid_p0 & (ib > 0)
        for h in range(ATT_HEADS):
            hs = slice(h * e, (h + 1) * e)
            qh = q_ref[:, hs]
            s_scr[h, 0] = _dot(qh, kc_ref[:, hs], NT)
            s_scr[h, 1] = _dot(qh, kp_ref[:, hs], NT)
        l_ref[...] = jnp.zeros_like(l_ref)
        for h in range(ATT_HEADS):
            s_c = jnp.where(valid_c, s_scr[h, 0] * scale - slopes[h] * dist_c, NEG)
            s_p = jnp.where(valid_p, s_scr[h, 1] * scale - slopes[h] * dist_p, NEG)
            m = jnp.maximum(jnp.max(s_c, axis=1, keepdims=True), jnp.max(s_p, axis=1, keepdims=True))
            p_c = jnp.where(valid_c, jnp.exp(s_c - m), 0.0)
            p_p = jnp.where(valid_p, jnp.exp(s_p - m), 0.0)
            l = jnp.sum(p_c, axis=1, keepdims=True) + jnp.sum(p_p, axis=1, keepdims=True)
            p_scr[h, 0] = p_c.astype(BF16)
            p_scr[h, 1] = p_p.astype(BF16)
            li_scr[:, h:h + 1] = 1.0 / l
            l_ref[:, h:h + 1] = m + jnp.log(l)
        for h in range(ATT_HEADS):
            hs = slice(h * e, (h + 1) * e)
            acc = _dot(p_scr[h, 0], vc_ref[:, hs], NN) + _dot(p_scr[h, 1], vp_ref[:, hs], NN)
            o_ref[:, hs] = acc * li_scr[:, h:h + 1]

    blk = (b, hd)
    cblk = (b, LANE)
    o, lse = pl.pallas_call(
        body, name=f"att_fwd{g}", grid=(d, nb),
        scratch_shapes=[pltpu.VMEM((ATT_HEADS, 2, b, b), F32), pltpu.VMEM((ATT_HEADS, 2, b, b), BF16),
                        pltpu.VMEM((b, LANE), F32)],
        in_specs=[pl.BlockSpec(blk, lambda r, i: (i, qc(r))),
                  pl.BlockSpec(blk, lambda r, i: (jnp.maximum(i - 1, 0), 2 * r)),
                  pl.BlockSpec(blk, lambda r, i: (i, 2 * r)),
                  pl.BlockSpec(blk, lambda r, i: (jnp.maximum(i - 1, 0), 2 * r + 1)),
                  pl.BlockSpec(blk, lambda r, i: (i, 2 * r + 1))],
        out_specs=(pl.BlockSpec(blk, lambda r, i: (i, r)), pl.BlockSpec(cblk, lambda r, i: (i, r))),
        out_shape=(jax.ShapeDtypeStruct((sub, d * hd), F32), jax.ShapeDtypeStruct((sub, d * LANE), F32)),
        compiler_params=_params(2))(qv, kvv, kvv, kvv, kvv)
    return _from_dilated(f"o_natural{g}", o, d), lse.reshape(t_len, LANE)


def _att_merge(os, ls):
    t_len, hd = os[0].shape
    tm = _tile(t_len, (256, 128))
    e = HEAD_DIM

    def body(o0, o1, o2, l0, l1, l2, of_ref, ob_ref, l_ref):
        a0, a1, a2 = l0[...], l1[...], l2[...]
        m = jnp.maximum(jnp.maximum(a0, a1), a2)
        e0, e1, e2 = jnp.exp(a0 - m), jnp.exp(a1 - m), jnp.exp(a2 - m)
        den = e0 + e1 + e2
        w0, w1, w2 = e0 / den, e1 / den, e2 / den
        l_ref[...] = m + jnp.log(den)
        for h in range(ATT_HEADS):
            hs = slice(h * e, (h + 1) * e)
            c = slice(h, h + 1)
            o = w0[:, c] * o0[:, hs] + w1[:, c] * o1[:, hs] + w2[:, c] * o2[:, hs]
            of_ref[:, hs] = o
            ob_ref[:, hs] = o.astype(ob_ref.dtype)

    row = pl.BlockSpec((tm, hd), lambda i: (i, 0))
    crow = pl.BlockSpec((tm, LANE), lambda i: (i, 0))
    return pl.pallas_call(
        body, name="att_merge", grid=(t_len // tm,), in_specs=[row] * 3 + [crow] * 3, out_specs=(row, row, crow),
        out_shape=(jax.ShapeDtypeStruct((t_len, hd), F32), jax.ShapeDtypeStruct((t_len, hd), BF16),
                   jax.ShapeDtypeStruct((t_len, LANE), F32)),
        compiler_params=_params(1))(*os, *ls)


def _att_delta(do, o):
    t_len, hd = o.shape
    tm = _tile(t_len, (256, 128))
    e = HEAD_DIM

    def body(do_ref, o_ref, d_ref):
        d_ref[...] = jnp.zeros_like(d_ref)
        for h in range(ATT_HEADS):
            hs = slice(h * e, (h + 1) * e)
            d_ref[:, h:h + 1] = jnp.sum(do_ref[:, hs].astype(F32) * o_ref[:, hs], axis=1, keepdims=True)

    row = pl.BlockSpec((tm, hd), lambda i: (i, 0))
    return pl.pallas_call(
        body, name="att_delta", grid=(t_len // tm,), in_specs=[row, row],
        out_specs=pl.BlockSpec((tm, LANE), lambda i: (i, 0)),
        out_shape=jax.ShapeDtypeStruct((t_len, LANE), F32), compiler_params=_params(1))(do, o)


def _att_bwd(views, delta, lse, do, g):
    d = DILATIONS[g]
    qv, kvv = views
    hd = ATT_HEADS * HEAD_DIM
    sub = kvv.shape[0]
    t_len = sub * d
    nb = sub // ATT_BLOCK
    b = ATT_BLOCK
    e = HEAD_DIM
    scale = e ** -0.5
    slopes = _alibi_slopes()
    qc = (lambda r: 3 * r + g) if d == 1 else (lambda r: r)
    dlv = delta.reshape(sub, d * LANE)
    lv = lse.reshape(sub, d * LANE)
    dov = do if d == 1 else _to_dilated(f"do_dilated{g}", do, d)

    def body(qj_ref, qn_ref, kp_ref, kc_ref, vp_ref, vc_ref, doj_ref, don_ref, dj_ref, dn_ref, lj_ref, ln_ref,
             dq_ref, dk_ref, dv_ref, s_scr, dp_scr, p_scr, ds_scr):
        j = pl.program_id(1)
        valid_c, valid_p0, dist_c, dist_p = _att_masks(d)
        valid = (valid_c, valid_p0 & (j > 0), valid_p0 & (j + 1 < nb))
        dist = (dist_c, dist_p, dist_p)
        for h in range(ATT_HEADS):
            hs = slice(h * e, (h + 1) * e)
            qj, qn = qj_ref[:, hs], qn_ref[:, hs]
            kc, kp = kc_ref[:, hs], kp_ref[:, hs]
            vc, vp = vc_ref[:, hs], vp_ref[:, hs]
            doj, don = doj_ref[:, hs], don_ref[:, hs]
            s_scr[h, 0] = _dot(qj, kc, NT)
            s_scr[h, 1] = _dot(qj, kp, NT)
            s_scr[h, 2] = _dot(qn, kc, NT)
            dp_scr[h, 0] = _dot(doj, vc, NT)
            dp_scr[h, 1] = _dot(doj, vp, NT)
            dp_scr[h, 2] = _dot(don, vc, NT)
        for h in range(ATT_HEADS):
            c = slice(h, h + 1)
            lse_t = (lj_ref[:, c], lj_ref[:, c], ln_ref[:, c])
            dlt_t = (dj_ref[:, c], dj_ref[:, c], dn_ref[:, c])
            for t in range(3):
                s = s_scr[h, t] * scale - slopes[h] * dist[t]
                p = jnp.where(valid[t], jnp.exp(jnp.where(valid[t], s - lse_t[t], NEG)), 0.0)
                p_scr[h, t] = p.astype(BF16)
                ds_scr[h, t] = (p * (dp_scr[h, t] - dlt_t[t])).astype(BF16)
        for h in range(ATT_HEADS):
            hs = slice(h * e, (h + 1) * e)
            dq = _dot(ds_scr[h, 0], kc_ref[:, hs], NN) + _dot(ds_scr[h, 1], kp_ref[:, hs], NN)
            dk = _dot(ds_scr[h, 0], qj_ref[:, hs], TN) + _dot(ds_scr[h, 2], qn_ref[:, hs], TN)
            dv = _dot(p_scr[h, 0], doj_ref[:, hs], TN) + _dot(p_scr[h, 2], don_ref[:, hs], TN)
            dq_ref[:, hs] = (dq * scale).astype(dq_ref.dtype)
            dk_ref[:, hs] = (dk * scale).astype(dk_ref.dtype)
            dv_ref[:, hs] = dv.astype(dv_ref.dtype)

    blk = (b, hd)
    cblk = (b, LANE)
    nxt = lambda i: jnp.minimum(i + 1, nb - 1)
    prv = lambda i: jnp.maximum(i - 1, 0)
    tiles = (ATT_HEADS, 3, b, b)
    dq, dk, dv = pl.pallas_call(
        body, name=f"att_bwd{g}", grid=(d, nb),
        scratch_shapes=[pltpu.VMEM(tiles, F32), pltpu.VMEM(tiles, F32), pltpu.VMEM(tiles, BF16), pltpu.VMEM(tiles, BF16)],
        in_specs=[pl.BlockSpec(blk, lambda r, i: (i, qc(r))),
                  pl.BlockSpec(blk, lambda r, i: (nxt(i), qc(r))),
                  pl.BlockSpec(blk, lambda r, i: (prv(i), 2 * r)),
                  pl.BlockSpec(blk, lambda r, i: (i, 2 * r)),
                  pl.BlockSpec(blk, lambda r, i: (prv(i), 2 * r + 1)),
                  pl.BlockSpec(blk, lambda r, i: (i, 2 * r + 1)),
                  pl.BlockSpec(blk, lambda r, i: (i, r)),
                  pl.BlockSpec(blk, lambda r, i: (nxt(i), r)),
                  pl.BlockSpec(cblk, lambda r, i: (i, r)),
                  pl.BlockSpec(cblk, lambda r, i: (nxt(i), r)),
                  pl.BlockSpec(cblk, lambda r, i: (i, r)),
                  pl.BlockSpec(cblk, lambda r, i: (nxt(i), r))],
        out_specs=(pl.BlockSpec(blk, lambda r, i: (i, r)),) * 3,
        out_shape=(jax.ShapeDtypeStruct((sub, d * hd), BF16),) * 3,
        compiler_params=_params(2))(qv, qv, kvv, kvv, kvv, kvv, dov, dov, dlv, dlv, lv, lv)
    return tuple(_from_dilated(f"{n}_natural{g}", t, d) for n, t in (("dq", dq), ("dk", dk), ("dv", dv)))


def _kv_grad_sum(dks, dvs):
    t_len, hd = dks[0].shape
    tm = _tile(t_len, (256, 128))

    def body(k0, k1, k2, v0, v1, v2, o_ref):
        o_ref[:, :hd] = (k0[...].astype(F32) + k1[...].astype(F32) + k2[...].astype(F32)).astype(o_ref.dtype)
        o_ref[:, hd:] = (v0[...].astype(F32) + v1[...].astype(F32) + v2[...].astype(F32)).astype(o_ref.dtype)

    row = pl.BlockSpec((tm, hd), lambda i: (i, 0))
    return pl.pallas_call(
        body, name="kv_grad_sum", grid=(t_len // tm,), in_specs=[row] * 6,
        out_specs=pl.BlockSpec((tm, 2 * hd), lambda i: (i, 0)),
        out_shape=jax.ShapeDtypeStruct((t_len, 2 * hd), BF16), compiler_params=_params(1))(*dks, *dvs)


HALO = 16
INV_SQRT2 = 1.0 / math.sqrt(2.0)
INV_SQRT2PI = 1.0 / math.sqrt(2.0 * math.pi)


def _conv_taps(g, halo, cw, cb):
    row = lax.broadcasted_iota(jnp.int32, g.shape, 0)
    h1 = halo[HALO - 1:HALO, :]
    h2 = halo[HALO - 2:HALO - 1, :]
    g1 = jnp.where(row == 0, h1, pltpu.roll(g, 1, 0))
    g2 = jnp.where(row == 0, h2, jnp.where(row == 1, h1, pltpu.roll(g, 2, 0)))
    gc = cw[0:1, :] * g2 + cw[1:2, :] * g1 + cw[2:3, :] * g + cb
    return gc, g1, g2


def _glu_specs(t_len, f, tm, tc):
    nj = f // tc
    hb = tm // HALO
    u = pl.BlockSpec((tm, tc), lambda j, i: (i, j))
    g = pl.BlockSpec((tm, tc), lambda j, i: (i, nj + j))
    gh = pl.BlockSpec((HALO, tc), lambda j, i: (jnp.maximum(i * hb - 1, 0), nj + j))
    cw = pl.BlockSpec((8, tc), lambda j, i: (0, j))
    cb = pl.BlockSpec((1, tc), lambda j, i: (0, j))
    return u, g, gh, cw, cb


def _glu_fwd(name, up, cw, cb):
    t_len = up.shape[0]
    f = up.shape[1] // 2
    tm = _tile(t_len, (512, 256, 128))
    tc = _tile(f, (1408, 1024, 512, 256, 128))
    u_s, g_s, gh_s, cw_s, cb_s = _glu_specs(t_len, f, tm, tc)

    def body(u_ref, g_ref, gh_ref, cw_ref, cb_ref, o_ref):
        first = pl.program_id(1) == 0
        halo = jnp.where(first, 0.0, gh_ref[...].astype(F32))
        gc, _, _ = _conv_taps(g_ref[...].astype(F32), halo, cw_ref[...], cb_ref[...])
        gel = 0.5 * gc * (1.0 + lax.erf(gc * INV_SQRT2))
        o_ref[...] = (gel * u_ref[...].astype(F32)).astype(o_ref.dtype)

    return pl.pallas_call(
        body, name=name, grid=(f // tc, t_len // tm), in_specs=[u_s, g_s, gh_s, cw_s, cb_s],
        out_specs=pl.BlockSpec((tm, tc), lambda j, i: (i, j)),
        out_shape=jax.ShapeDtypeStruct((t_len, f), BF16), compiler_params=_params(2))(up, up, up, cw, cb)


def _glu_bwd_a(name, dact, up, cw, cb):
    t_len = up.shape[0]
    f = up.shape[1] // 2
    tm = _tile(t_len, (256, 128))
    tc = _tile(f, (1408, 1024, 512, 256, 128))
    u_s, g_s, gh_s, cw_s, cb_s = _glu_specs(t_len, f, tm, tc)

    def body(da_ref, u_ref, g_ref, gh_ref, cw_ref, cb_ref, du_ref, dgc_ref, w0_ref, w1_ref, w2_ref, b_ref):
        first = pl.program_id(1) == 0
        halo = jnp.where(first, 0.0, gh_ref[...].astype(F32))
        g = g_ref[...].astype(F32)
        gc, g1, g2 = _conv_taps(g, halo, cw_ref[...], cb_ref[...])
        phi = 0.5 * (1.0 + lax.erf(gc * INV_SQRT2))
        dgel = phi + gc * jnp.exp(-0.5 * gc * gc) * INV_SQRT2PI
        da = da_ref[...].astype(F32)
        du_ref[...] = (da * gc * phi).astype(du_ref.dtype)
        dgc = da * u_ref[...].astype(F32) * dgel
        dgc_ref[...] = dgc.astype(dgc_ref.dtype)
        parts = (jnp.sum(dgc * g2, axis=0, keepdims=True), jnp.sum(dgc * g1, axis=0, keepdims=True),
                 jnp.sum(dgc * g, axis=0, keepdims=True), jnp.sum(dgc, axis=0, keepdims=True))
        refs = (w0_ref, w1_ref, w2_ref, b_ref)

        @pl.when(first)
        def _():
            for r, p in zip(refs, parts):
                r[...] = p

        @pl.when(jnp.logical_not(first))
        def _():
            for r, p in zip(refs, parts):
                r[...] += p

    tile = pl.BlockSpec((tm, tc), lambda j, i: (i, j))
    vec = pl.BlockSpec((1, tc), lambda j, i: (0, j))
    vshape = jax.ShapeDtypeStruct((1, f), F32)
    return pl.pallas_call(
        body, name=name, grid=(f // tc, t_len // tm), in_specs=[tile, u_s, g_s, gh_s, cw_s, cb_s],
        out_specs=(tile, tile, vec, vec, vec, vec),
        out_shape=(jax.ShapeDtypeStruct((t_len, 2 * f), BF16), jax.ShapeDtypeStruct((t_len, f), BF16),
                   vshape, vshape, vshape, vshape),
        compiler_params=_params(2))(dact, up, up, up, cw, cb)


def _glu_bwd_b(name, dup, dgc, cw):
    t_len, f = dgc.shape
    tm = _tile(t_len, (256, 128, 64))
    hb = tm // HALO
    n_i = t_len // tm
    last_hb = t_len // HALO - 1

    def body(dup_ref, d_ref, dh_ref, cw_ref, o_ref):
        last = pl.program_id(0) == n_i - 1
        halo = jnp.where(last, 0.0, dh_ref[...].astype(F32))
        dd = d_ref[...].astype(F32)
        row = lax.broadcasted_iota(jnp.int32, dd.shape, 0)
        h0 = halo[0:1, :]
        h1 = halo[1:2, :]
        d1 = jnp.where(row == tm - 1, h0, pltpu.roll(dd, tm - 1, 0))
        d2 = jnp.where(row == tm - 1, h1, jnp.where(row == tm - 2, h0, pltpu.roll(dd, tm - 2, 0)))
        cwv = cw_ref[...]
        dg = cwv[2:3, :] * dd + cwv[1:2, :] * d1 + cwv[0:1, :] * d2
        o_ref[...] = dg.astype(o_ref.dtype)

    row_s = pl.BlockSpec((tm, f), lambda i: (i, 0))
    return pl.pallas_call(
        body, name=name, grid=(n_i,),
        in_specs=[ANY, row_s, pl.BlockSpec((HALO, f), lambda i: (jnp.minimum((i + 1) * hb, last_hb), 0)),
                  pl.BlockSpec((8, f), lambda i: (0, 0))],
        out_specs=pl.BlockSpec((tm, f), lambda i: (i, 1)), input_output_aliases={0: 0},
        out_shape=jax.ShapeDtypeStruct((t_len, 2 * f), BF16), compiler_params=_params(1))(dup, dgc, dgc, cw)


def _adamw(name, w, g, m, v):
    rows, cols = w.shape
    gcols = g.shape[1]
    n_out = 3 if gcols == cols else 4
    tr = _tile(rows, (256, 128, 64, 32, 16, 8))
    c1 = 1.0 / (1.0 - ADAM_B1 ** ADAM_STEP)
    c2 = 1.0 / (1.0 - ADAM_B2 ** ADAM_STEP)

    def body(w_ref, g_ref, m_ref, v_ref, d_ref, nm_ref, nv_ref, *g_out):
        gv = g_ref[...][:, :cols]
        nm = ADAM_B1 * m_ref[...] + (1.0 - ADAM_B1) * gv
        nv = ADAM_B2 * v_ref[...] + (1.0 - ADAM_B2) * (gv * gv)
        nm_ref[...] = nm
        nv_ref[...] = nv
        d_ref[...] = -ADAM_LR * ((nm * c1) / (jnp.sqrt(nv * c2) + ADAM_EPS) + ADAM_WD * w_ref[...])
        for ref in g_out:
            ref[...] = gv

    blk = pl.BlockSpec((tr, cols), lambda i: (i, 0))
    gblk = pl.BlockSpec((tr, gcols), lambda i: (i, 0))
    shp = jax.ShapeDtypeStruct((rows, cols), F32)
    return pl.pallas_call(body, name=name, grid=(rows // tr,), in_specs=[blk, gblk, blk, blk], out_specs=(blk,) * n_out,
                          out_shape=(shp,) * n_out, compiler_params=_params(1))(w, g, m, v)


def _adamw_layers(name, w, gs, m, v):
    _, rows, cols = w.shape
    tr = _tile(rows, (128, 64, 32, 16, 8))
    nr = rows // tr
    c1 = 1.0 / (1.0 - ADAM_B1 ** ADAM_STEP)
    c2 = 1.0 / (1.0 - ADAM_B2 ** ADAM_STEP)

    def body(w_ref, g0_ref, g1_ref, m_ref, v_ref, d_ref, nm_ref, nv_ref, g_ref):
        gv = jnp.where(pl.program_id(0) == 0, g0_ref[...], g1_ref[...])
        nm = ADAM_B1 * m_ref[...] + (1.0 - ADAM_B1) * gv
        nv = ADAM_B2 * v_ref[...] + (1.0 - ADAM_B2) * (gv * gv)
        nm_ref[...] = nm
        nv_ref[...] = nv
        d_ref[...] = -ADAM_LR * ((nm * c1) / (jnp.sqrt(nv * c2) + ADAM_EPS) + ADAM_WD * w_ref[...])
        g_ref[...] = gv

    blk = pl.BlockSpec((None, tr, cols), lambda l, i: (l, i, 0))
    g0_blk = pl.BlockSpec((tr, cols), lambda l, i: (jnp.where(l == 0, i, nr - 1), 0))
    g1_blk = pl.BlockSpec((tr, cols), lambda l, i: (jnp.where(l == 0, 0, i), 0))
    shp = jax.ShapeDtypeStruct(w.shape, F32)
    return pl.pallas_call(body, name=name, grid=(2, nr), in_specs=[blk, g0_blk, g1_blk, blk, blk], out_specs=(blk,) * 4,
                          out_shape=(shp,) * 4, compiler_params=_params(2))(w, gs[0], gs[1], m, v)


class _NoComm:
    def __init__(self):
        self.grads = {}

    def prefetch(self, group, ws, carry):
        return ws, carry

    def need(self, group, ws, after):
        return ws

    def reduce(self, group, grads, carry):
        self.grads.update(grads)
        return carry

    def tick(self, carry):
        return carry


def _local_step(x, target, ws, norms, small, hooks):
    lay = _layout()

    w_main, w_a = _unpack_gin(ws["gin"])
    hn0 = _rms_fwd("rms_attn0", x, norms["attn0"])
    proj = _mm_plain("gla_proj", hn0, w_main, NN, F32)
    a = _mm_plain("gla_proj_a", hn0, w_a, NN, BF16)
    ga, cum = _gla_gate_fwd(a, small["w_a2p"], small["b_a2"])
    ws, cum = hooks.prefetch("B0", ws, cum)
    o_gla, states = _gla_fwd(proj, cum)
    gated = _gla_out_fwd(o_gla, proj, small["head_norm"])
    ws = hooks.need("B0", ws, gated)
    ws, gated = hooks.prefetch("B1", ws, gated)
    h1 = _mm_act_wr("gla_out", gated, ws["gout"], lay["gout"], add=x)
    ws = hooks.need("B1", ws, h1)

    def ffn_fwd(l, h, own=None, prefetch=None):
        nonlocal ws
        hn = _rms_fwd(f"rms_ffn{l}", h, norms[f"ffn{l}"])
        if own is not None:
            ws, hn = hooks.prefetch(own, ws, hn)
        up = _mm_act_wc(f"ffn_up{l}", hn, ws[f"up{l}"], lay[f"up{l}"], BF16)
        act = _glu_fwd(f"glu_fwd{l}", up, small["conv_w"][l], small["conv_b"][l])
        if own is not None:
            ws = hooks.need(own, ws, act)
        if prefetch is not None:
            ws, act = hooks.prefetch(prefetch, ws, act)
        return hn, up, act, _mm_act_wr(f"ffn_down{l}", act, ws[f"down{l}"], lay[f"down{l}"], add=h)

    hnf0, up0, act0, h2 = ffn_fwd(0, h1, own="B2", prefetch="C1")

    ws = hooks.need("C1", ws, h2)
    def rms_pair(xin, g1, g2):
        t_len, d = xin.shape
        tm = _tile(t_len, (512, 256, 128))

        def body(x_ref, g1_ref, g2_ref, o1_ref, o2_ref):
            xv = x_ref[...]
            xn = xv * lax.rsqrt(jnp.mean(xv * xv, axis=-1, keepdims=True) + EPS)
            o1_ref[...] = (xn * g1_ref[...]).astype(o1_ref.dtype)
            o2_ref[...] = (xn * g2_ref[...]).astype(o2_ref.dtype)

        row = pl.BlockSpec((tm, d), lambda i: (i, 0))
        vec = pl.BlockSpec((1, d), lambda i: (0, 0))
        shp = jax.ShapeDtypeStruct((t_len, d), BF16)
        return pl.pallas_call(body, name="rms_h2", grid=(t_len // tm,), in_specs=[row, vec, vec], out_specs=(row, row),
                              out_shape=(shp, shp), compiler_params=_params(1))(xin, g1, g2)

    kvn, hn1 = rms_pair(h2, norms["kv"], norms["attn1"])
    kv = _mm_act_wc("kv_proj", kvn, ws["wkv"], lay["wkv"], BF16)
    q_all = _mm_act_wc("q_proj", hn1, ws["wq"], lay["wq"], BF16)
    views = [_att_views(q_all, kv, g) for g in range(3)]
    branch = [_att_fwd(views[g], g) for g in range(3)]
    ws, lse2 = hooks.prefetch("C2", ws, branch[-1][1])
    o_att, o_att_b, lse = _att_merge([br[0] for br in branch], [br[1] for br in branch[:-1]] + [lse2])
    h3 = _mm_act_wr("att_out", o_att_b, ws["dout"], lay["dout"], add=h2)
    ws = hooks.need("C2", ws, h3)
    hnf1, up1, act1, h4 = ffn_fwd(1, h3)

    dh4, d_final, loss = _loss_head(h4, norms["final"], target)

    sm = {"final": d_final}

    def ffn_bwd(l, dh, h, hn, up, act):
        big = {}
        dact = _mm_dact_wrT(f"ffn_down_dx{l}", dh, ws[f"down{l}"], lay[f"down{l}"])
        big[f"down{l}"] = _mm_grad_wr(f"ffn_down_dw{l}", act, dh, lay[f"down{l}"])
        du, dgc, w0, w1, w2, db = _glu_bwd_a(f"glu_bwd_a{l}", dact, up, small["conv_w"][l], small["conv_b"][l])
        sm[f"conv_w{l}"] = (w0, w1, w2)
        sm[f"conv_b{l}"] = db
        dup = hooks.tick(_glu_bwd_b(f"glu_bwd_b{l}", du, dgc, small["conv_w"][l]))
        dhn = _mm_dact_wcT(f"ffn_up_dx{l}", dup, ws[f"up{l}"], lay[f"up{l}"])
        big[f"up{l}"] = _mm_grad_wc(f"ffn_up_dw{l}", hn, dup, lay[f"up{l}"])
        dh_in, (sm[f"ffn{l}"],) = _rms_bwd(f"rms_ffn_bwd{l}", [dhn], h, [norms[f"ffn{l}"]], dh)
        return hooks.reduce(f"ffn{l}", big, dh_in)

    dh3 = ffn_bwd(1, dh4, h3, hnf1, up1, act1)

    big = {}
    do_att = _mm_dact_wrT("att_out_dx", dh3, ws["dout"], lay["dout"])
    big["dout"] = _mm_grad_wr("att_out_dw", o_att_b, dh3, lay["dout"])
    delta = _att_delta(do_att, o_att)
    bw = [_att_bwd(views[g], delta, lse, do_att, g) for g in range(3)]
    dq_all = jnp.concatenate([t[0] for t in bw], axis=1)
    dhn1 = _mm_dact_wcT("q_proj_dx", dq_all, ws["wq"], lay["wq"])
    big["wq"] = _mm_grad_wc("q_proj_dw", hn1, dq_all, lay["wq"])
    dkv = hooks.tick(_kv_grad_sum([t[1] for t in bw], [t[2] for t in bw]))
    dkvn = _mm_dact_wcT("kv_proj_dx", dkv, ws["wkv"], lay["wkv"])
    big["wkv"] = _mm_grad_wc("kv_proj_dw", kvn, dkv, lay["wkv"])
    dh2, (sm["attn1"], sm["kv"]) = _rms_bwd("rms_h2_bwd", [dhn1, dkvn], h2, [norms["attn1"], norms["kv"]], dh3)
    dh2 = hooks.reduce("att", big, dh2)

    dh1 = ffn_bwd(0, dh2, h1, hnf0, up0, act0)

    big = {}
    dgated = _mm_dact_wrT("gla_out_dx", dh1, ws["gout"], lay["gout"])
    big["gout"] = _mm_grad_wr("gla_out_dw", gated, dh1, lay["gout"])
    do_gla, dr, sm["head_norm"] = _gla_out_bwd(dgated, o_gla, proj, small["head_norm"])
    dq, dk, dv, dcum = _gla_bwd(proj, cum, states, hooks.tick(do_gla))
    da, sm["w_a2p"], sm["b_a2"] = _gla_gate_bwd(dcum, ga, a, small["w_a2p"])
    dproj = jnp.concatenate([dq, dk, dv, dr], axis=1)
    dhn0 = _mm_plain("gla_proj_dx", dproj, w_main, NT, F32)
    dhn0 = _mm_plain("gla_proj_a_dx", da, w_a, NT, F32, add=dhn0)
    gin_main = _mm_plain("gla_proj_dw", hn0, dproj, TN, BF16)
    gin_a = _mm_plain("gla_proj_a_dw", hn0, da, TN, BF16)
    big["gin"] = _pack_gin_grad(gin_main, gin_a)
    grad_x, (sm["attn0"],) = _rms_bwd("rms_attn0_bwd", [dhn0], x, [norms["attn0"]], dh1)
    return loss, grad_x, sm, big


def _pack_weights(chip, names, gla_w_in, gla_w_out, w_kv, dsa_w_q, dsa_w_out, ffn_w_up, ffn_w_down):
    gin = gla_w_in[0]
    gin = jnp.pad(gin, ((0, 0), (0, _roundup(gin.shape[1], LANE) - gin.shape[1])))
    shards = {"gin": gin, "gout": gla_w_out[0], "up0": ffn_w_up[0], "up1": ffn_w_up[1], "down0": ffn_w_down[0],
              "down1": ffn_w_down[1], "wq": dsa_w_q[0], "wkv": w_kv, "dout": dsa_w_out[0]}
    out = {}
    for name in names:
        w = shards[name]
        buf = jnp.zeros((N_CHIPS,) + w.shape, BF16)
        out[name] = lax.dynamic_update_slice(buf, w.astype(BF16)[None], (chip, 0, 0))
    return out


def _unpack_gin(w_gin):
    w = _layout()["gin"][1]
    d = w_gin.shape[1]
    wp = w_gin.shape[2]
    n_main = 2 * GLA_KEY_DIM + 2 * GLA_VAL_DIM
    tm = _tile(d, (256, 128, 64, 32, 16))

    def body(s_ref, main_ref, a_ref):
        full = jnp.concatenate([s_ref[s][:, :w] for s in range(N_CHIPS)], axis=1)
        main_ref[...] = full[:, :n_main]
        a_ref[...] = jnp.concatenate([full[:, n_main:], jnp.zeros((tm, A_PAD - GATE_RANK), full.dtype)], axis=1)

    return pl.pallas_call(
        body, name="unpack_gin", grid=(d // tm,), in_specs=[pl.BlockSpec((N_CHIPS, tm, wp), lambda i: (0, i, 0))],
        out_specs=(pl.BlockSpec((tm, n_main), lambda i: (i, 0)), pl.BlockSpec((tm, A_PAD), lambda i: (i, 0))),
        out_shape=(jax.ShapeDtypeStruct((d, n_main), w_gin.dtype), jax.ShapeDtypeStruct((d, A_PAD), w_gin.dtype)),
        compiler_params=_params(1))(w_gin)


def _pack_gin_grad(gin_main, gin_a):
    w = _layout()["gin"][1]
    wp = _roundup(w, LANE)
    d, n_main = gin_main.shape
    tm = _tile(d, (256, 128, 64, 32, 16))

    def body(main_ref, a_ref, o_ref):
        full = jnp.concatenate([main_ref[...], a_ref[:, :GATE_RANK]], axis=1)
        fill = jnp.zeros((tm, wp - w), full.dtype)
        for s in range(N_CHIPS):
            o_ref[s] = jnp.concatenate([full[:, s * w:(s + 1) * w], fill], axis=1)

    return pl.pallas_call(
        body, name="pack_gin_grad", grid=(d // tm,),
        in_specs=[pl.BlockSpec((tm, n_main), lambda i: (i, 0)), pl.BlockSpec((tm, A_PAD), lambda i: (i, 0))],
        out_specs=pl.BlockSpec((N_CHIPS, tm, wp), lambda i: (0, i, 0)),
        out_shape=jax.ShapeDtypeStruct((N_CHIPS, d, wp), gin_main.dtype), compiler_params=_params(1))(gin_main, gin_a)


def _small_params(attn_norm, ffn_norm, kv_norm, final_norm, conv_b, w_a2, b_a2, head_norm, conv_w):
    norms = {"attn0": attn_norm[0:1], "attn1": attn_norm[1:2], "ffn0": ffn_norm[0:1], "ffn1": ffn_norm[1:2],
             "kv": kv_norm[None, :], "final": final_norm[None, :]}
    small = {"w_a2p": jnp.pad(w_a2, ((0, A_PAD - GATE_RANK), (0, 0))), "b_a2": b_a2[None, :],
             "head_norm": head_norm[None, :], "conv_w": jnp.pad(conv_w, ((0, 0), (0, 8 - conv_w.shape[1]), (0, 0))),
             "conv_b": conv_b[:, None, :]}
    return norms, small


ANY = pl.BlockSpec(memory_space=pl.ANY)


def _place():
    return lax.axis_index("x"), lax.axis_index("y"), lax.axis_index("c")


def _other_chips(x, y):
    return [(1 - x, y), (x, 1 - y), (1 - x, 1 - y)]


def _rcopy(src, dst, ssem, rsem, dev):
    return pltpu.make_async_remote_copy(src_ref=src, dst_ref=dst, send_sem=ssem, recv_sem=rsem, device_id=dev,
                                        device_id_type=MESH)


def _pack_shard(name, w, layer, chip_arr, after):
    rows, cols = w.shape[-2:]
    tr = _tile(rows, (512, 352, 256, 128, 64, 32, 16))

    def body(p_ref, w_ref, after_ref, o_ref):
        o_ref[...] = w_ref[...].astype(o_ref.dtype)

    if w.ndim == 3:
        w_spec = pl.BlockSpec((None, tr, cols), lambda i, p: (layer, i, 0))
    else:
        w_spec = pl.BlockSpec((tr, cols), lambda i, p: (i, 0))
    return pl.pallas_call(
        body, name=name,
        grid_spec=pltpu.PrefetchScalarGridSpec(
            num_scalar_prefetch=1, grid=(rows // tr,), in_specs=[w_spec, ANY],
            out_specs=pl.BlockSpec((None, tr, cols), lambda i, p: (p[0], i, 0))),
        out_shape=jax.ShapeDtypeStruct((N_CHIPS, rows, cols), BF16), compiler_params=_params(1))(chip_arr, w, after)


def _swap_halves(name, arrs):
    n = len(arrs)

    def body(*refs):
        ins, outs = refs[:n], refs[n:2 * n]
        send, recv = refs[2 * n:]
        x, y, c = _place()
        cps = []
        for a in range(n):
            h = ins[a].shape[1] // 2
            cp = _rcopy(ins[a].at[:, pl.ds((1 - c) * h, h)], outs[a], send.at[a], recv.at[a], (x, y, 1 - c))
            cp.start()
            cps.append(cp)
        for cp in cps:
            cp.wait()

    return pl.pallas_call(
        body, name=name, in_specs=[ANY] * n, out_specs=[ANY] * n,
        out_shape=[jax.ShapeDtypeStruct((a.shape[0], a.shape[1] // 2, a.shape[2]), a.dtype) for a in arrs],
        scratch_shapes=[pltpu.SemaphoreType.DMA((n,)), pltpu.SemaphoreType.DMA((n,))])(*arrs)


SEM = pl.BlockSpec(memory_space=pltpu.SEMAPHORE)
EFFECT = pltpu.SideEffectType.DATAFLOW_SIDE_EFFECTING


def _shapes(arrs):
    return [jax.ShapeDtypeStruct(a.shape, a.dtype) for a in arrs]


def _gather_start(name, thru, arrs):
    n, nt = len(arrs), len(thru)

    def body(*refs):
        ins = refs[nt:nt + n]
        send, recv = refs[nt + n], refs[nt + n + 1]
        outs = refs[2 * nt + n + 2:]
        x, y, c = _place()
        me = 2 * x + y
        for a in range(n):
            h = ins[a].shape[1] // 2
            mine = pl.ds(c * h, h)
            for j, (px, py) in enumerate(_other_chips(x, y)):
                _rcopy(ins[a].at[me, mine], outs[a].at[me, mine], send.at[3 * a + j], recv.at[3 * a + j], (px, py, c)).start()

    res = pl.pallas_call(
        body, name=name, in_specs=[ANY] * (nt + n), out_specs=[SEM, SEM] + [ANY] * (nt + n),
        out_shape=[pltpu.SemaphoreType.DMA((3 * n,)), pltpu.SemaphoreType.DMA((3 * n,))] + _shapes(thru) + _shapes(arrs),
        input_output_aliases={i: 2 + i for i in range(nt + n)},
        compiler_params=pltpu.CompilerParams(has_side_effects=EFFECT))(*thru, *arrs)
    return res[0], res[1], res[2:2 + nt], res[2 + nt:]


def _gather_wait(name, send, recv, arrs, after):
    n = len(arrs)

    def body(*refs):
        ins = refs[:n]
        send_ref, recv_ref = refs[n], refs[n + 1]
        x, y, c = _place()
        me = 2 * x + y
        for a in range(n):
            h = ins[a].shape[1] // 2
            mine = pl.ds(c * h, h)
            for j, (px, py) in enumerate(_other_chips(x, y)):
                sent = ins[a].at[me, mine]
                landed = ins[a].at[2 * px + py, mine]
                cp = _rcopy(sent, landed, send_ref.at[3 * a + j], recv_ref.at[3 * a + j], (px, py, c))
                cp.wait_send()
                cp.wait_recv()

    after = list(after) if isinstance(after, (list, tuple)) else [after]
    return pl.pallas_call(
        body, name=name, in_specs=[ANY] * n + [SEM, SEM] + [ANY] * len(after), out_specs=[ANY] * n,
        out_shape=_shapes(arrs), input_output_aliases={a: a for a in range(n)},
        compiler_params=pltpu.CompilerParams(has_side_effects=EFFECT))(*arrs, send, recv, *after)


def _forward_halves(name, arrs):
    n = len(arrs)

    def body(*refs):
        ins, outs = refs[:n], refs[n:2 * n]
        send, recv = refs[2 * n:]
        x, y, c = _place()
        sib = (x, y, 1 - c)
        chips = _other_chips(x, y)
        cps = []
        for a in range(n):
            h = ins[a].shape[1] // 2
            mine = pl.ds(c * h, h)
            for j, (px, py) in enumerate(chips):
                cp = _rcopy(ins[a].at[2 * px + py, mine], outs[a].at[2 * px + py, mine], send.at[3 * a + j],
                            recv.at[3 * a + j], sib)
                cp.start()
                cps.append(cp)
        for a in range(n):
            h = ins[a].shape[1] // 2
            theirs = pl.ds((1 - c) * h, h)
            for j, (px, py) in enumerate(chips):
                got = outs[a].at[2 * px + py, theirs]
                _rcopy(got, got, send.at[3 * a + j], recv.at[3 * a + j], sib).wait_recv()
        for cp in cps:
            cp.wait_send()

    return pl.pallas_call(
        body, name=name, in_specs=[ANY] * n, out_specs=[ANY] * n, out_shape=_shapes(arrs),
        input_output_aliases={a: a for a in range(n)},
        scratch_shapes=[pltpu.SemaphoreType.DMA((3 * n,)), pltpu.SemaphoreType.DMA((3 * n,))])(*arrs)


def _forward_start(name, thru, arrs):
    n, nt = len(arrs), len(thru)

    def body(*refs):
        ins = refs[nt:nt + n]
        send, recv = refs[nt + n], refs[nt + n + 1]
        outs = refs[2 * nt + n + 2:]
        x, y, c = _place()
        for a in range(n):
            h = ins[a].shape[1] // 2
            mine = pl.ds(c * h, h)
            for j, (px, py) in enumerate(_other_chips(x, y)):
                _rcopy(ins[a].at[2 * px + py, mine], outs[a].at[2 * px + py, mine], send.at[3 * a + j], recv.at[3 * a + j],
                       (x, y, 1 - c)).start()

    res = pl.pallas_call(
        body, name=name, in_specs=[ANY] * (nt + n), out_specs=[SEM, SEM] + [ANY] * (nt + n),
        out_shape=[pltpu.SemaphoreType.DMA((3 * n,)), pltpu.SemaphoreType.DMA((3 * n,))] + _shapes(thru) + _shapes(arrs),
        input_output_aliases={i: 2 + i for i in range(nt + n)},
        compiler_params=pltpu.CompilerParams(has_side_effects=EFFECT))(*thru, *arrs)
    return res[0], res[1], res[2:2 + nt], res[2 + nt:]


def _forward_wait(name, send, recv, arrs, after):
    n = len(arrs)

    def body(*refs):
        ins = refs[:n]
        send_ref, recv_ref = refs[n], refs[n + 1]
        x, y, c = _place()
        for a in range(n):
            h = ins[a].shape[1] // 2
            for j, (px, py) in enumerate(_other_chips(x, y)):
                sent = ins[a].at[2 * px + py, pl.ds(c * h, h)]
                got = ins[a].at[2 * px + py, pl.ds((1 - c) * h, h)]
                cp = _rcopy(sent, got, send_ref.at[3 * a + j], recv_ref.at[3 * a + j], (x, y, 1 - c))
                cp.wait_send()
                cp.wait_recv()

    return pl.pallas_call(
        body, name=name, in_specs=[ANY] * n + [SEM, SEM, ANY], out_specs=[ANY] * n, out_shape=_shapes(arrs),
        input_output_aliases={a: a for a in range(n)},
        compiler_params=pltpu.CompilerParams(has_side_effects=EFFECT))(*arrs, send, recv, after)


def _scatter_start(name, thru, arrs):
    n, nt = len(arrs), len(thru)
    landing = [lax.empty(a.shape, a.dtype) for a in arrs]

    def body(*refs):
        ins = refs[nt:nt + n]
        send, recv = refs[nt + 2 * n], refs[nt + 2 * n + 1]
        outs = refs[2 * nt + 3 * n + 2:]
        x, y, c = _place()
        me = 2 * x + y
        for a in range(n):
            for j, (px, py) in enumerate(_other_chips(x, y)):
                _rcopy(ins[a].at[2 * px + py], outs[a].at[me], send.at[3 * a + j], recv.at[3 * a + j], (px, py, c)).start()

    res = pl.pallas_call(
        body, name=name, in_specs=[ANY] * (nt + 2 * n), out_specs=[SEM, SEM] + [ANY] * (nt + 2 * n),
        out_shape=[pltpu.SemaphoreType.DMA((3 * n,)), pltpu.SemaphoreType.DMA((3 * n,))] + _shapes(thru) + _shapes(arrs)
        + _shapes(landing),
        input_output_aliases={i: 2 + i for i in range(nt + 2 * n)},
        compiler_params=pltpu.CompilerParams(has_side_effects=EFFECT))(*thru, *arrs, *landing)
    return res[0], res[1], res[2:2 + nt], res[2 + nt:2 + nt + n], res[2 + nt + n:]


def _scatter_wait(name, send, recv, arrs, landing, after):
    n = len(arrs)

    def body(*refs):
        ins, land = refs[:n], refs[n:2 * n]
        send_ref, recv_ref = refs[2 * n], refs[2 * n + 1]
        x, y, c = _place()
        for a in range(n):
            for j, (px, py) in enumerate(_other_chips(x, y)):
                cp = _rcopy(ins[a].at[2 * px + py], land[a].at[2 * px + py], send_ref.at[3 * a + j], recv_ref.at[3 * a + j],
                            (px, py, c))
                cp.wait_send()
                cp.wait_recv()

    res = pl.pallas_call(
        body, name=name, in_specs=[ANY] * (2 * n) + [SEM, SEM, ANY], out_specs=[ANY] * (2 * n),
        out_shape=_shapes(arrs) + _shapes(landing), input_output_aliases={i: i for i in range(2 * n)},
        compiler_params=pltpu.CompilerParams(has_side_effects=EFFECT))(*arrs, *landing, send, recv, after)
    return res[:n], res[n:]


def _swap_start(name, thru, arrs):
    n, nt = len(arrs), len(thru)
    landing = [lax.empty((a.shape[0], a.shape[1] // 2, a.shape[2]), a.dtype) for a in arrs]

    def body(*refs):
        ins = refs[nt:nt + n]
        send, recv = refs[nt + 2 * n], refs[nt + 2 * n + 1]
        outs = refs[2 * nt + 3 * n + 2:]
        x, y, c = _place()
        for a in range(n):
            h = ins[a].shape[1] // 2
            _rcopy(ins[a].at[:, pl.ds((1 - c) * h, h)], outs[a], send.at[a], recv.at[a], (x, y, 1 - c)).start()

    res = pl.pallas_call(
        body, name=name, in_specs=[ANY] * (nt + 2 * n), out_specs=[SEM, SEM] + [ANY] * (nt + 2 * n),
        out_shape=[pltpu.SemaphoreType.DMA((n,)), pltpu.SemaphoreType.DMA((n,))] + _shapes(thru) + _shapes(arrs)
        + _shapes(landing),
        input_output_aliases={i: 2 + i for i in range(nt + 2 * n)},
        compiler_params=pltpu.CompilerParams(has_side_effects=EFFECT))(*thru, *arrs, *landing)
    return res[0], res[1], res[2:2 + nt], res[2 + nt:2 + nt + n], res[2 + nt + n:]


def _swap_wait(name, send, recv, arrs, landing, after):
    n = len(arrs)

    def body(*refs):
        ins, land = refs[:n], refs[n:2 * n]
        send_ref, recv_ref = refs[2 * n], refs[2 * n + 1]
        x, y, c = _place()
        for a in range(n):
            h = ins[a].shape[1] // 2
            cp = _rcopy(ins[a].at[:, pl.ds((1 - c) * h, h)], land[a], send_ref.at[a], recv_ref.at[a], (x, y, 1 - c))
            cp.wait_send()
            cp.wait_recv()

    res = pl.pallas_call(
        body, name=name, in_specs=[ANY] * (2 * n) + [SEM, SEM, ANY], out_specs=[ANY] * (2 * n),
        out_shape=_shapes(arrs) + _shapes(landing), input_output_aliases={i: i for i in range(2 * n)},
        compiler_params=pltpu.CompilerParams(has_side_effects=EFFECT))(*arrs, *landing, send, recv, after)
    return res[:n], res[n:]


def _join_start(name, arrs):
    n = len(arrs)

    def body(*refs):
        ins = refs[:n]
        send, recv = refs[n], refs[n + 1]
        outs = refs[n + 2:]
        x, y, c = _place()
        for a in range(n):
            h = ins[a].shape[0] // 2
            mine = pl.ds(c * h, h)
            _rcopy(ins[a].at[mine], outs[a].at[mine], send.at[a], recv.at[a], (x, y, 1 - c)).start()

    res = pl.pallas_call(
        body, name=name, in_specs=[ANY] * n, out_specs=[SEM, SEM] + [ANY] * n,
        out_shape=[pltpu.SemaphoreType.DMA((n,)), pltpu.SemaphoreType.DMA((n,))] + _shapes(arrs),
        input_output_aliases={i: 2 + i for i in range(n)},
        compiler_params=pltpu.CompilerParams(has_side_effects=EFFECT))(*arrs)
    return res[0], res[1], res[2:]


def _join_wait(name, send, recv, arrs, after):
    n = len(arrs)

    def body(*refs):
        ins = refs[:n]
        send_ref, recv_ref = refs[n], refs[n + 1]
        x, y, c = _place()
        for a in range(n):
            h = ins[a].shape[0] // 2
            cp = _rcopy(ins[a].at[pl.ds(c * h, h)], ins[a].at[pl.ds((1 - c) * h, h)], send_ref.at[a], recv_ref.at[a],
                        (x, y, 1 - c))
            cp.wait_send()
            cp.wait_recv()

    return pl.pallas_call(
        body, name=name, in_specs=[ANY] * n + [SEM, SEM, ANY], out_specs=[ANY] * n, out_shape=_shapes(arrs),
        input_output_aliases={a: a for a in range(n)},
        compiler_params=pltpu.CompilerParams(has_side_effects=EFFECT))(*arrs, send, recv, after)


def _join_halves(name, arrs):
    n = len(arrs)

    def body(*refs):
        ins, outs = refs[:n], refs[n:2 * n]
        send, recv = refs[2 * n:]
        x, y, c = _place()
        cps = []
        for a in range(n):
            h = ins[a].shape[0] // 2
            mine = pl.ds(c * h, h)
            cp = _rcopy(ins[a].at[mine], outs[a].at[mine], send.at[a], recv.at[a], (x, y, 1 - c))
            cp.start()
            cps.append(cp)
        for a in range(n):
            h = ins[a].shape[0] // 2
            got = outs[a].at[pl.ds((1 - c) * h, h)]
            _rcopy(got, got, send.at[a], recv.at[a], (x, y, 1 - c)).wait_recv()
        for cp in cps:
            cp.wait_send()

    return pl.pallas_call(
        body, name=name, in_specs=[ANY] * n, out_specs=[ANY] * n,
        out_shape=[jax.ShapeDtypeStruct(a.shape, a.dtype) for a in arrs],
        input_output_aliases={a: a for a in range(n)},
        scratch_shapes=[pltpu.SemaphoreType.DMA((n,)), pltpu.SemaphoreType.DMA((n,))])(*arrs)


def _allgather8(name, xs, reduce):
    m_per, n = xs.shape

    def body(x_ref, out_ref, *rest):
        if reduce:
            sum_ref, send, recv, lsem = rest
        else:
            send, recv, lsem = rest
        x, y, c = _place()
        me, sib = (x, y, c), (x, y, 1 - c)
        chips = _other_chips(x, y)

        def rows(px, py, pc):
            return out_ref.at[pl.ds((4 * px + 2 * py + pc) * m_per, m_per), :]

        def copy(k, block, to, src=None):
            return _rcopy(rows(*block) if src is None else src, rows(*block), send.at[k], recv.at[k], to)

        mine = pltpu.make_async_copy(x_ref, rows(*me), lsem)
        mine.start()
        first = [copy(0, me, sib, src=x_ref)]
        first += [copy(1 + j, me, (*chip, c), src=x_ref) for j, chip in enumerate(chips)]
        for cp in first:
            cp.start()
        passed = [copy(4 + j, (*chip, c), sib) for j, chip in enumerate(chips)]
        for j, chip in enumerate(chips):
            copy(1 + j, (*chip, c), me).wait_recv()
            passed[j].start()
        copy(0, sib, me).wait_recv()
        for j, chip in enumerate(chips):
            copy(4 + j, (*chip, 1 - c), me).wait_recv()
        for cp in first + passed:
            cp.wait_send()
        mine.wait()
        if reduce:
            acc = out_ref[pl.ds(0, m_per), :]
            for dev in range(1, N_DEV):
                acc = acc + out_ref[pl.ds(dev * m_per, m_per), :]
            sum_ref[...] = acc

    vm = pl.BlockSpec(memory_space=pltpu.VMEM)
    out_shape = [jax.ShapeDtypeStruct((N_DEV * m_per, n), xs.dtype)]
    if reduce:
        out_shape.append(jax.ShapeDtypeStruct((m_per, n), xs.dtype))
    return pl.pallas_call(
        body, name=name, in_specs=[vm], out_specs=[vm] * len(out_shape), out_shape=out_shape,
        scratch_shapes=[pltpu.SemaphoreType.DMA((7,)), pltpu.SemaphoreType.DMA((7,)), pltpu.SemaphoreType.DMA],
        compiler_params=pltpu.CompilerParams(vmem_limit_bytes=VMEM_LIMIT))(xs)


def _add_my_half(name, a, rb, c_arr):
    s, h, cols = rb.shape
    tr = _tile(h, (512, 352, 256, 128, 64, 32, 16))
    nt = h // tr

    def body(c_ref, a_ref, b_ref, o_ref):
        o_ref[...] = (a_ref[...].astype(F32) + b_ref[...].astype(F32)).astype(o_ref.dtype)

    return pl.pallas_call(
        body, name=name,
        grid_spec=pltpu.PrefetchScalarGridSpec(
            num_scalar_prefetch=1, grid=(s, nt),
            in_specs=[pl.BlockSpec((None, tr, cols), lambda k, i, c: (k, c[0] * nt + i, 0)),
                      pl.BlockSpec((None, tr, cols), lambda k, i, c: (k, i, 0))],
            out_specs=pl.BlockSpec((None, tr, cols), lambda k, i, c: (k, i, 0))),
        out_shape=jax.ShapeDtypeStruct(rb.shape, BF16), compiler_params=_params(2))(c_arr, a, rb)


def _sum_chips(name, own, q, place):
    s, h, cols = q.shape
    tr = _tile(h, (512, 352, 256, 128, 64, 32, 16))
    nt = h // tr

    def body(p_ref, own_ref, q1_ref, q2_ref, q3_ref, o_ref):
        acc = own_ref[...].astype(F32) + q1_ref[...].astype(F32)
        o_ref[...] = acc + q2_ref[...].astype(F32) + q3_ref[...].astype(F32)

    def slab(t):
        return pl.BlockSpec((None, tr, cols), lambda i, p: ((p[0] + t) % s, i, 0))

    return pl.pallas_call(
        body, name=name,
        grid_spec=pltpu.PrefetchScalarGridSpec(
            num_scalar_prefetch=1, grid=(nt,), in_specs=[slab(0), slab(1), slab(2), slab(3)],
            out_specs=pl.BlockSpec((tr, cols), lambda i, p: (p[1] * nt + i, 0))),
        out_shape=jax.ShapeDtypeStruct((2 * h, cols), F32), compiler_params=_params(1))(place, own, q, q, q)


def _pack_rows(parts):
    rows = []
    for p in parts:
        flat = p.reshape(-1).astype(F32)
        n = _roundup(flat.shape[0], 8 * LANE)
        rows.append(jnp.pad(flat, (0, n - flat.shape[0])).reshape(-1, LANE))
    return jnp.concatenate(rows, axis=0)


def _unpack_rows(buf, shapes):
    out, r = [], 0
    for shp in shapes:
        size = math.prod(shp)
        nr = _roundup(size, 8 * LANE) // LANE
        out.append(buf[r:r + nr].reshape(-1)[:size].reshape(shp))
        r += nr
    return out


def kernel(x, attn_norm, gla_w_in, gla_w_a2, gla_b_a2, gla_head_norm, gla_w_out, kv_norm, w_kv, dsa_w_q, dsa_w_out, ffn_norm, ffn_w_up, ffn_conv_w, ffn_conv_b, ffn_w_down, final_norm, loss_target, m_attn_norm, m_gla_w_in, m_gla_w_a2, m_gla_b_a2, m_gla_head_norm, m_gla_w_out, m_kv_norm, m_w_kv, m_dsa_w_q, m_dsa_w_out, m_ffn_norm, m_ffn_w_up, m_ffn_conv_w, m_ffn_conv_b, m_ffn_w_down, m_final_norm, v_attn_norm, v_gla_w_in, v_gla_w_a2, v_gla_b_a2, v_gla_head_norm, v_gla_w_out, v_kv_norm, v_w_kv, v_dsa_w_q, v_dsa_w_out, v_ffn_norm, v_ffn_w_up, v_ffn_conv_w, v_ffn_conv_b, v_ffn_w_down, v_final_norm):
    lay = _layout()
    d, f = D_MODEL, D_FF
    cx, cy, cc = _place()
    chip = 2 * cx + cy
    c_arr = jnp.reshape(cc, (1,)).astype(jnp.int32)
    place = jnp.stack([chip, cc]).astype(jnp.int32)

    groups = {"A": ("gin", "small"), "B0": ("gout",), "B1": ("up0",), "B2": ("down0",), "C1": ("wkv", "wq", "dout"),
              "C2": ("up1", "down1")}
    big_shards = (gla_w_in, gla_w_out, w_kv, dsa_w_q, dsa_w_out, ffn_w_up, ffn_w_down)
    ws = _pack_weights(chip, groups["A"][:1], *big_shards)
    sharded_small = [gla_w_a2[0], gla_b_a2[0], gla_head_norm[0], ffn_conv_w]
    packed = _pack_rows(sharded_small)
    packed = jnp.pad(packed, ((0, _roundup(packed.shape[0], 16) - packed.shape[0]), (0, 0)))
    ws["small"] = lax.dynamic_update_slice(jnp.zeros((N_CHIPS,) + packed.shape, F32), packed[None], (chip, 0, 0))
    send, recv, _, arrs = _gather_start("gather_a_start", [], [ws[k] for k in groups["A"]])
    chip_arr = place[:1]
    sources = {"up0": (ffn_w_up, 0), "up1": (ffn_w_up, 1), "down0": (ffn_w_down, 0), "down1": (ffn_w_down, 1),
               "wq": (dsa_w_q, 0), "wkv": (w_kv, 0), "dout": (dsa_w_out, 0), "gout": (gla_w_out, 0)}
    later = ("B0", "B1", "B2", "C1", "C2")
    for k in sum((groups[grp] for grp in later), ()):
        ws[k] = _pack_shard(f"pack_{k}", *sources[k], chip_arr, arrs[1])
    moments = [t.reshape(-1, t.shape[-1]) for t in (m_gla_w_in, v_gla_w_in)]
    arrs = _gather_wait("gather_a_wait", send, recv, arrs, [ws["dout"]] + moments)
    ws.update(zip(groups["A"], _forward_halves("forward_a", arrs)))
    in_flight = {}
    thru = [ws[k] for k in groups["A"]]
    for grp in later:
        send, recv, thru, arrs = _gather_start(f"gather_{grp.lower()}_start", thru, [ws[k] for k in groups[grp]])
        ws.update(zip(groups[grp], arrs))
        in_flight[grp] = (send, recv)
    ws.update(zip(groups["A"], thru))
    pending = []

    class _Comm:
        def prefetch(self, grp, ws, carry):
            send, recv = in_flight[grp]
            arrs = _gather_wait(f"gather_{grp.lower()}_wait", send, recv, [ws[k] for k in groups[grp]], carry)
            send, recv, thru, arrs = _forward_start(f"forward_{grp.lower()}_start", [carry], arrs)
            in_flight[grp] = (send, recv)
            return {**ws, **dict(zip(groups[grp], arrs))}, thru[0]

        def need(self, grp, ws, after):
            send, recv = in_flight[grp]
            arrs = _forward_wait(f"forward_{grp.lower()}_wait", send, recv, [ws[k] for k in groups[grp]], after)
            return {**ws, **dict(zip(groups[grp], arrs))}

        swapping = None

        def reduce(self, grp, grads, carry):
            names = list(grads)
            send, recv, thru, parts, theirs = _swap_start(f"swap_{grp}_start", [carry], [grads[k] for k in names])
            self.swapping = (grp, names, send, recv, parts, theirs)
            return thru[0]

        def tick(self, carry):
            if self.swapping is None:
                return carry
            grp, names, send, recv, parts, theirs = self.swapping
            self.swapping = None
            parts, theirs = _swap_wait(f"swap_{grp}_wait", send, recv, parts, theirs, carry)
            return self.scatter(grp, names, parts, theirs, carry)

        def scatter(self, grp, names, parts, theirs, carry):
            sums = [_add_my_half(f"add_half_{k}", a, b, c_arr) for k, a, b in zip(names, parts, theirs)]
            send, recv, thru, sums, landing = _scatter_start(f"scatter_{grp}_start", [carry], sums)
            pending.append((grp, names, send, recv, sums, landing))
            return thru[0]

        def reduce_now(self, grp, grads, carry):
            names = list(grads)
            parts = [grads[k] for k in names]
            return self.scatter(grp, names, parts, _swap_halves(f"swap_{grp}", parts), carry)

    shards = [_unpack_rows(ws["small"][s], [p.shape for p in sharded_small]) for s in range(N_CHIPS)]
    w_a2, b_a2, head_norm, conv_w = [jnp.concatenate([shards[s][k] for s in range(N_CHIPS)], axis=-1) for k in range(4)]
    norms, small = _small_params(attn_norm, ffn_norm, kv_norm, final_norm, ffn_conv_b, w_a2, b_a2, head_norm, conv_w)

    comm = _Comm()
    loss_blk, grad_x, sm, last_big = _local_step(x[0], loss_target[0], ws, norms, small, comm)

    small_parts = [loss_blk, jnp.concatenate([sm["attn0"], sm["attn1"]]), jnp.concatenate([sm["ffn0"], sm["ffn1"]]),
                   sm["kv"], sm["final"], jnp.concatenate([sm["conv_b0"], sm["conv_b1"]]),
                   sm["w_a2p"][:GATE_RANK], sm["b_a2"], sm["head_norm"],
                   jnp.stack([jnp.concatenate(sm["conv_w0"]), jnp.concatenate(sm["conv_w1"])])]
    small_shapes = [(8, LANE), (2, d), (2, d), (d,), (d,), (2, f), (GATE_RANK, GLA_KEY_DIM), (GLA_KEY_DIM,),
                    (GLA_VAL_DIM // GLA_HEADS,), (2, 3, f)]
    _, reduced = _allgather8("reduce_small", _pack_rows(small_parts), True)
    reduced = comm.reduce_now("gla", last_big, reduced)

    loss_r, g_attn, g_ffn, g_kv, g_final, g_cb, g_a2, g_ba2, g_hn, g_cw = _unpack_rows(reduced, small_shapes)
    loss = loss_r[0, 0]

    def mine(g, axis):
        w = g.shape[axis] // N_CHIPS
        return lax.dynamic_slice_in_dim(g, chip * w, w, axis)

    grads = {
        "attn_norm": g_attn, "gla_w_a2": mine(g_a2, 1)[None], "gla_b_a2": mine(g_ba2, 0)[None],
        "gla_head_norm": mine(g_hn, 0)[None], "kv_norm": g_kv, "ffn_norm": g_ffn, "ffn_conv_w": mine(g_cw, 2),
        "ffn_conv_b": g_cb, "final_norm": g_final,
    }
    weights = {"attn_norm": (attn_norm, m_attn_norm, v_attn_norm), "gla_w_in": (gla_w_in, m_gla_w_in, v_gla_w_in),
               "gla_w_a2": (gla_w_a2, m_gla_w_a2, v_gla_w_a2), "gla_b_a2": (gla_b_a2, m_gla_b_a2, v_gla_b_a2),
               "gla_head_norm": (gla_head_norm, m_gla_head_norm, v_gla_head_norm),
               "gla_w_out": (gla_w_out, m_gla_w_out, v_gla_w_out), "kv_norm": (kv_norm, m_kv_norm, v_kv_norm),
               "w_kv": (w_kv, m_w_kv, v_w_kv), "dsa_w_q": (dsa_w_q, m_dsa_w_q, v_dsa_w_q),
               "dsa_w_out": (dsa_w_out, m_dsa_w_out, v_dsa_w_out), "ffn_norm": (ffn_norm, m_ffn_norm, v_ffn_norm),
               "ffn_w_up": (ffn_w_up, m_ffn_w_up, v_ffn_w_up), "ffn_conv_w": (ffn_conv_w, m_ffn_conv_w, v_ffn_conv_w),
               "ffn_conv_b": (ffn_conv_b, m_ffn_conv_b, v_ffn_conv_b),
               "ffn_w_down": (ffn_w_down, m_ffn_w_down, v_ffn_w_down), "final_norm": (final_norm, m_final_norm, v_final_norm)}
    order = list(weights)
    big_names = ("gla_w_in", "gla_w_out", "w_kv", "dsa_w_q", "dsa_w_out", "ffn_w_up", "ffn_w_down")
    delta, new_m, new_v = {}, {}, {}

    def adam_big(k, g):
        w, m, v = weights[k]
        cols = w.shape[-1]
        res = _adamw(f"adamw_{k}", w.reshape(-1, cols), g.reshape(-1, g.shape[-1]), m.reshape(-1, cols), v.reshape(-1, cols))
        delta[k], new_m[k], new_v[k] = [r.reshape(w.shape) for r in res[:3]]
        grads[k] = res[3].reshape(w.shape) if len(res) == 4 else g
        return res[0]

    full = {}
    after = reduced
    joining = []
    for grp, names, send, recv, sums, landing in pending[:-1]:
        sums, landing = _scatter_wait(f"scatter_{grp}_wait", send, recv, sums, landing, after)
        halves = [_sum_chips(f"sum_chips_{k}", s, q, place) for k, s, q in zip(names, sums, landing)]
        send, recv, halves = _join_start(f"join_{grp}_start", halves)
        joining.append((grp, names, send, recv, halves))
        after = halves[0]
    for grp, names, send, recv, halves in joining:
        joined = _join_wait(f"join_{grp}_wait", send, recv, halves, after)
        full.update(zip(names, joined))
        after = joined[0]
    after = adam_big("w_kv", full["wkv"])
    after = adam_big("dsa_w_q", full["wq"][None])
    after = adam_big("dsa_w_out", full["dout"][None])
    for k, g0, g1 in (("ffn_w_up", "up0", "up1"), ("ffn_w_down", "down0", "down1")):
        delta[k], new_m[k], new_v[k], grads[k] = _adamw_layers(f"adamw_{k}", weights[k][0], (full[g0], full[g1]),
                                                               weights[k][1], weights[k][2])
        after = delta[k]
    grp, names, send, recv, sums, landing = pending[-1]
    sums, landing = _scatter_wait(f"scatter_{grp}_wait", send, recv, sums, landing, after)
    halves = [_sum_chips(f"sum_chips_{k}", s, q, place) for k, s, q in zip(names, sums, landing)]
    full.update(zip(names, _join_halves(f"join_{grp}", halves)))
    adam_big("gla_w_in", full["gin"])
    adam_big("gla_w_out", full["gout"][None])
    small_names = [k for k in order if k not in big_names]
    packed = [_pack_rows([src[k] for k in small_names])
              for src in ({k: weights[k][0] for k in small_names}, grads, {k: weights[k][1] for k in small_names},
                          {k: weights[k][2] for k in small_names})]
    res = _adamw("adamw_small", *packed)
    shapes = [weights[k][0].shape for k in small_names]
    for dst, buf in zip((delta, new_m, new_v), res):
        for k, val in zip(small_names, _unpack_rows(buf, shapes)):
            dst[k] = val
    return (loss, grad_x[None], *[grads[k] for k in order], *[delta[k] for k in order], *[new_m[k] for k in order],
            *[new_v[k] for k in order])
```
